```python
import math
import numpy as np
import jax
import jax.numpy as jnp
from jax import lax

D_MODEL = 1024
BATCH = 16
SEQ = 2048
DEPTH = 2

EXPAND = 2
MIX_WIDTH = EXPAND * D_MODEL
HG_WIDTH = MIX_WIDTH // 2
HG_HEAD_DIM = 128
HG_HEADS = HG_WIDTH // HG_HEAD_DIM
HG_CHUNK = 64
ATT_WIDTH = MIX_WIDTH - HG_WIDTH
ATT_HEAD_DIM = 64
ATT_HEADS = ATT_WIDTH // ATT_HEAD_DIM
ATT_KV_HEADS = max(1, ATT_HEADS // 8)
ATT_GROUP = ATT_HEADS // ATT_KV_HEADS
KV_WIDTH = ATT_KV_HEADS * ATT_HEAD_DIM
WINDOW = 128
ATT_BLOCK = 128
ATT_SCALE = 1.0 / math.sqrt(ATT_HEAD_DIM)
ROPE_THETA = 10000.0
NORM_EPS = 1e-6
NEG_INF = -1e30
LB_FLOOR = 1e-20

SPLIT_SIZES = (HG_WIDTH, HG_WIDTH, HG_WIDTH, HG_WIDTH, ATT_WIDTH, KV_WIDTH, KV_WIDTH, ATT_WIDTH)
IN_WIDTH = int(sum(SPLIT_SIZES))
SPLIT_POINTS = tuple(int(v) for v in np.cumsum(SPLIT_SIZES)[:-1])

kernel_name = "hymba_hgrn2_swa_sink_hybrid"


def rms_norm(x, g):
    xf = x.astype(jnp.float32)
    y = xf * lax.rsqrt(jnp.mean(xf * xf, axis=-1, keepdims=True) + NORM_EPS)
    return (y * g.astype(jnp.float32)).astype(x.dtype)


def rotary(x, pos):
    half = x.shape[-1] // 2
    inv_freq = ROPE_THETA ** (-jnp.arange(half, dtype=jnp.float32) / half)
    ang = pos.astype(jnp.float32)[:, None] * inv_freq[None, :]
    cos = jnp.cos(ang)[None, :, None, :]
    sin = jnp.sin(ang)[None, :, None, :]
    xf = x.astype(jnp.float32)
    x1, x2 = xf[..., :half], xf[..., half:]
    return jnp.concatenate([x1 * cos - x2 * sin, x2 * cos + x1 * sin], axis=-1).astype(x.dtype)


def hgrn2(q, f_logit, i, lb):
    B, S, _ = q.shape
    C, H, D = HG_CHUNK, HG_HEADS, HG_HEAD_DIM
    nC = S // C
    lb = lb.astype(jnp.float32)
    qf = jax.nn.silu(q.astype(jnp.float32))
    logf = jnp.logaddexp(jnp.log(jnp.maximum(lb, LB_FLOOR)),
                         jnp.log1p(-lb) + jax.nn.log_sigmoid(f_logit.astype(jnp.float32)))
    k = -jnp.expm1(logf)

    def chunks(t):
        return t.reshape(B, nC, C, H, D).transpose(1, 0, 3, 2, 4)

    qc, kc, vc, gc = chunks(qf), chunks(k), chunks(i.astype(jnp.float32)), chunks(logf)
    bc = jnp.cumsum(gc, axis=3)
    causal = jnp.tril(jnp.ones((C, C), dtype=bool))[None, None, :, :, None]

    def step(state, xs):
        qch, kch, vch, b = xs
        b_last = b[:, :, C - 1:C, :]
        diff = b[:, :, :, None, :] - b[:, :, None, :, :]
        decay = jnp.where(causal, jnp.exp(jnp.where(causal, diff, 0.0)), 0.0)
        scores = jnp.einsum('bhtd,bhsd,bhtsd->bhts', qch, kch, decay)
        o = jnp.einsum('bhts,bhsv->bhtv', scores, vch)
        o = o + jnp.einsum('bhtd,bhdv->bhtv', qch * jnp.exp(b), state)
        state = jnp.exp(b_last[:, :, 0, :])[..., None] * state + \
            jnp.einsum('bhsd,bhsv->bhdv', kch * jnp.exp(b_last - b), vch)
        return state, o

    s0 = jnp.zeros((B, H, D, D), dtype=jnp.float32)
    _, o = lax.scan(step, s0, (qc, kc, vc, bc))
    return o.transpose(1, 0, 3, 2, 4).reshape(B, S, H, D)


def sliding_window_attention(q, k, v, sinks):
    B, S = q.shape[0], q.shape[1]
    L, KV, G, D = ATT_BLOCK, ATT_KV_HEADS, ATT_GROUP, ATT_HEAD_DIM
    nB = S // L
    qb = q.reshape(B, nB, L, KV, G, D)
    kb = k.reshape(B, nB, L, KV, D)
    vb = v.reshape(B, nB, L, KV, D)
    k_prev = jnp.concatenate([jnp.zeros_like(kb[:, :1]), kb[:, :-1]], axis=1)
    v_prev = jnp.concatenate([jnp.zeros_like(vb[:, :1]), vb[:, :-1]], axis=1)
    kw = jnp.concatenate([k_prev, kb], axis=2)
    vw = jnp.concatenate([v_prev, vb], axis=2)
    s = jnp.einsum('bnqkgd,bnskd->bnkgqs', qb, kw).astype(jnp.float32) * ATT_SCALE
    qpos = jnp.arange(L)[:, None] + L
    kpos = jnp.arange(2 * L)[None, :]
    diff = qpos - kpos
    band = (diff >= 0) & (diff < WINDOW)
    key_exists = (jnp.arange(nB)[:, None] * L - L + jnp.arange(2 * L)[None, :]) >= 0
    mask = band[None, :, :] & key_exists[:, None, :]
    s = jnp.where(mask[None, :, None, None, :, :], s, NEG_INF)
    sink = jnp.broadcast_to(sinks.astype(jnp.float32).reshape(KV, G)[None, None, :, :, None, None],
                            s.shape[:-1] + (1,))
    p = jax.nn.softmax(jnp.concatenate([s, sink], axis=-1), axis=-1)[..., :-1]
    o = jnp.einsum('bnkgqs,bnskd->bnqkgd', p.astype(v.dtype), vw)
    return o.reshape(B, S, KV * G * D)


def hybrid_layer(x, w_in, w_out, g_pre, g_post, lb, g_head, sinks):
    B, S, _ = x.shape
    h = rms_norm(x, g_pre)
    proj = jnp.einsum('bsd,de->bse', h, w_in)
    q_h, f_h, i_h, z_h, q_a, k_a, v_a, z_a = jnp.split(proj, SPLIT_POINTS, axis=-1)

    o_h = hgrn2(q_h, f_h, i_h, lb)
    o_h = rms_norm(o_h, g_head).reshape(B, S, HG_WIDTH).astype(z_h.dtype) * jax.nn.silu(z_h)

    pos = jnp.arange(S)
    q_a = rotary(q_a.reshape(B, S, ATT_HEADS, ATT_HEAD_DIM), pos)
    k_a = rotary(k_a.reshape(B, S, ATT_KV_HEADS, ATT_HEAD_DIM), pos)
    v_a = v_a.reshape(B, S, ATT_KV_HEADS, ATT_HEAD_DIM)
    o_a = sliding_window_attention(q_a, k_a, v_a, sinks).astype(z_a.dtype) * jax.nn.silu(z_a)

    y = jnp.einsum('bse,ed->bsd', jnp.concatenate([o_h, o_a], axis=-1), w_out)
    return x + rms_norm(y, g_post)


def _fwd_setup_inputs(seed: int = 0) -> dict:
    key = jax.random.key(seed)
    ks = jax.random.split(key, 9)
    x = jax.random.normal(ks[0], (BATCH, SEQ, D_MODEL), dtype=jnp.float32)
    w_in = jax.random.normal(ks[1], (DEPTH, D_MODEL, IN_WIDTH), dtype=jnp.float32) * D_MODEL ** -0.5
    w_out = jax.random.normal(ks[2], (DEPTH, MIX_WIDTH, D_MODEL), dtype=jnp.float32) * MIX_WIDTH ** -0.5
    g_pre = 1.0 + 0.05 * jax.random.normal(ks[3], (DEPTH, D_MODEL), dtype=jnp.float32)
    g_post = 1.0 + 0.05 * jax.random.normal(ks[4], (DEPTH, D_MODEL), dtype=jnp.float32)
    lb_param = 0.1 * jax.random.normal(ks[5], (DEPTH, HG_WIDTH), dtype=jnp.float32)
    g_head = 1.0 + 0.05 * jax.random.normal(ks[6], (DEPTH, HG_HEAD_DIM), dtype=jnp.float32)
    sinks = jax.random.normal(ks[7], (DEPTH, ATT_HEADS), dtype=jnp.float32)
    return {"x": x, "w_in": w_in, "w_out": w_out, "g_pre": g_pre, "g_post": g_post,
            "lb_param": lb_param, "g_head": g_head, "sinks": sinks}


def _fwd_reference(x, w_in, w_out, g_pre, g_post, lb_param, g_head, sinks):
    p = jax.nn.softmax(lb_param.astype(jnp.float32), axis=0)
    lower_bounds = jnp.cumsum(p, axis=0) - p[0:1]
    for l in range(DEPTH):
        x = hybrid_layer(x, w_in[l], w_out[l], g_pre[l], g_post[l], lower_bounds[l], g_head[l], sinks[l])
    return x


import jax as _jax
import jax.numpy as _jnp

TWIN_FORMAT = 'train_step'
FWD_PARAMS = ['x', 'w_in', 'w_out', 'g_pre', 'g_post', 'lb_param', 'g_head', 'sinks']
TWIN_WEIGHTS = ['w_in', 'w_out', 'g_pre', 'g_post', 'lb_param', 'g_head', 'sinks']
TWIN_DIFF_INPUT = 'x'
TWIN_INPUTS = ['x', 'w_in', 'w_out', 'g_pre', 'g_post', 'lb_param', 'g_head', 'sinks', 'loss_target', 'm_w_in', 'm_w_out', 'm_g_pre', 'm_g_post', 'm_lb_param', 'm_g_head', 'm_sinks', 'v_w_in', 'v_w_out', 'v_g_pre', 'v_g_post', 'v_lb_param', 'v_g_head', 'v_sinks']
TWIN_OUTPUTS = ['loss', 'grad_x', 'grad_w_in', 'grad_w_out', 'grad_g_pre', 'grad_g_post', 'grad_lb_param', 'grad_g_head', 'grad_sinks', 'delta_w_in', 'delta_w_out', 'delta_g_pre', 'delta_g_post', 'delta_lb_param', 'delta_g_head', 'delta_sinks', 'new_m_w_in', 'new_m_w_out', 'new_m_g_pre', 'new_m_g_post', 'new_m_lb_param', 'new_m_g_head', 'new_m_sinks', 'new_v_w_in', 'new_v_w_out', 'new_v_g_pre', 'new_v_g_post', 'new_v_lb_param', 'new_v_g_head', 'new_v_sinks']
TWIN_LEAF_KINDS = {'loss': 'loss', 'grad_x': 'grad_x', 'grad_w_in': 'grad_w', 'grad_w_out': 'grad_w', 'grad_g_pre': 'grad_w', 'grad_g_post': 'grad_w', 'grad_lb_param': 'grad_w', 'grad_g_head': 'grad_w', 'grad_sinks': 'grad_w', 'delta_w_in': 'delta_w', 'delta_w_out': 'delta_w', 'delta_g_pre': 'delta_w', 'delta_g_post': 'delta_w', 'delta_lb_param': 'delta_w', 'delta_g_head': 'delta_w', 'delta_sinks': 'delta_w', 'new_m_w_in': 'new_m', 'new_m_w_out': 'new_m', 'new_m_g_pre': 'new_m', 'new_m_g_post': 'new_m', 'new_m_lb_param': 'new_m', 'new_m_g_head': 'new_m', 'new_m_sinks': 'new_m', 'new_v_w_in': 'new_v', 'new_v_w_out': 'new_v', 'new_v_g_pre': 'new_v', 'new_v_g_post': 'new_v', 'new_v_lb_param': 'new_v', 'new_v_g_head': 'new_v', 'new_v_sinks': 'new_v'}


def _forward(args):
    return _fwd_reference(*[args[k] for k in FWD_PARAMS])


def _output_shape():
    out = _jax.eval_shape(lambda: _forward(_fwd_setup_inputs(0)))
    return out.shape, out.dtype

N_MICROBATCH = 1
ADAM_LR = 0.001
ADAM_B1 = 0.9
ADAM_B2 = 0.999
ADAM_EPS = 1e-08
ADAM_WD = 0.01
ADAM_STEP = 10
PER_EXAMPLE_BATCH_AXIS = {'x': 0, 'loss_target': 0}
SHARED_INPUTS = []
_WEIGHT_DTYPES = {'w_in': _jnp.float32, 'w_out': _jnp.float32, 'g_pre': _jnp.float32, 'g_post': _jnp.float32, 'lb_param': _jnp.float32, 'g_head': _jnp.float32, 'sinks': _jnp.float32}
MOMENT_SCALE = {'w_in': 2.344485e-01, 'w_out': 4.377171e-01, 'g_pre': 6.409858e-01, 'g_post': 3.200629e+01, 'lb_param': 2.700329e-02, 'g_head': 1.266625e+00, 'sinks': 6.325545e-02}


def _to_microbatches(a, axis):
    t = _jnp.moveaxis(a, axis, 0)
    t = t.reshape((N_MICROBATCH, t.shape[0] // N_MICROBATCH) + t.shape[1:])
    return _jnp.moveaxis(t, 1, axis + 1)


def setup_inputs(seed: int = 0) -> dict:
    inp = _fwd_setup_inputs(seed)
    key = _jax.random.fold_in(_jax.random.key(seed), 7919)
    shape, _ = _output_shape()
    out = dict(inp)
    out["loss_target"] = _jax.random.normal(_jax.random.fold_in(key, 0), shape, _jnp.float32)
    for i, name in enumerate(TWIN_WEIGHTS):
        w = inp[name].astype(_jnp.float32)
        if MOMENT_SCALE is None:
            s = _jnp.sqrt(_jnp.mean(_jnp.square(w)) + 1e-30)
        else:
            s = MOMENT_SCALE[name]
        km, kv = _jax.random.split(_jax.random.fold_in(key, i + 1))
        out[name] = w
        out["m_" + name] = s * _jax.random.normal(km, w.shape, _jnp.float32)
        out["v_" + name] = (s * s) * _jax.random.uniform(kv, w.shape, _jnp.float32, 0.5, 1.5)
    if N_MICROBATCH > 1:
        for name, axis in PER_EXAMPLE_BATCH_AXIS.items():
            out[name] = _to_microbatches(out[name], axis)
    return {'x': out['x'], 'w_in': out['w_in'], 'w_out': out['w_out'], 'g_pre': out['g_pre'], 'g_post': out['g_post'], 'lb_param': out['lb_param'], 'g_head': out['g_head'], 'sinks': out['sinks'], 'loss_target': out['loss_target'], 'm_w_in': out['m_w_in'], 'm_w_out': out['m_w_out'], 'm_g_pre': out['m_g_pre'], 'm_g_post': out['m_g_post'], 'm_lb_param': out['m_lb_param'], 'm_g_head': out['m_g_head'], 'm_sinks': out['m_sinks'], 'v_w_in': out['v_w_in'], 'v_w_out': out['v_w_out'], 'v_g_pre': out['v_g_pre'], 'v_g_post': out['v_g_post'], 'v_lb_param': out['v_lb_param'], 'v_g_head': out['v_g_head'], 'v_sinks': out['v_sinks']}


def _loss(weights, diff, rest, loss_target):
    with _jax.named_scope("forward"):
        args = {**rest, TWIN_DIFF_INPUT: diff, **{k: w.astype(_WEIGHT_DTYPES[k]) for k, w in weights.items()}}
        y = _forward(args)
    with _jax.named_scope("loss_head"):
        err = _jnp.square(y.astype(_jnp.float32) - loss_target)
        return 0.5 * _jnp.sum(_jnp.mean(err, axis=-1)) if err.ndim else 0.5 * err


def _adamw(w, g, m, v):
    m = ADAM_B1 * m + (1.0 - ADAM_B1) * g
    v = ADAM_B2 * v + (1.0 - ADAM_B2) * _jnp.square(g)
    m_hat = m / (1.0 - ADAM_B1 ** ADAM_STEP)
    v_hat = v / (1.0 - ADAM_B2 ** ADAM_STEP)
    delta = -ADAM_LR * (m_hat / (_jnp.sqrt(v_hat) + ADAM_EPS) + ADAM_WD * w)
    return delta, m, v


def reference(x, w_in, w_out, g_pre, g_post, lb_param, g_head, sinks, loss_target, m_w_in, m_w_out, m_g_pre, m_g_post, m_lb_param, m_g_head, m_sinks, v_w_in, v_w_out, v_g_pre, v_g_post, v_lb_param, v_g_head, v_sinks):
    given = dict(x=x, w_in=w_in, w_out=w_out, g_pre=g_pre, g_post=g_post, lb_param=lb_param, g_head=g_head, sinks=sinks, loss_target=loss_target, m_w_in=m_w_in, m_w_out=m_w_out, m_g_pre=m_g_pre, m_g_post=m_g_post, m_lb_param=m_lb_param, m_g_head=m_g_head, m_sinks=m_sinks, v_w_in=v_w_in, v_w_out=v_w_out, v_g_pre=v_g_pre, v_g_post=v_g_post, v_lb_param=v_lb_param, v_g_head=v_g_head, v_sinks=v_sinks)
    weights = {n: given[n] for n in TWIN_WEIGHTS}
    shared = {n: given[n] for n in SHARED_INPUTS}
    per_example = {n: given[n] for n in ['x']}
    grad_fn = _jax.value_and_grad(_loss, argnums=(0, 1))

    def one_microbatch(ex, loss_target):
        ex = dict(ex)
        diff = ex.pop(TWIN_DIFF_INPUT)
        return grad_fn(weights, diff, {**shared, **ex}, loss_target)

    if N_MICROBATCH == 1:
        loss, (grad_w, grad_x) = one_microbatch(per_example, given["loss_target"])
    else:
        def body(carry, xs):
            loss_sum, grad_sum = carry
            l_k, (gw_k, gx_k) = one_microbatch(xs[0], xs[1])
            with _jax.named_scope("update"):
                return (loss_sum + l_k, _jax.tree.map(_jnp.add, grad_sum, gw_k)), gx_k

        init = (_jnp.zeros((), _jnp.float32), _jax.tree.map(_jnp.zeros_like, weights))
        (loss, grad_w), grad_x = _jax.lax.scan(body, init, (per_example, given["loss_target"]))
    with _jax.named_scope("update"):
        delta_w, new_m, new_v = {}, {}, {}
        for n in TWIN_WEIGHTS:
            delta_w[n], new_m[n], new_v[n] = _adamw(weights[n], grad_w[n], given["m_" + n], given["v_" + n])
    return (loss, grad_x, *[grad_w[n] for n in TWIN_WEIGHTS], *[delta_w[n] for n in TWIN_WEIGHTS],
            *[new_m[n] for n in TWIN_WEIGHTS], *[new_v[n] for n in TWIN_WEIGHTS])
```

```python
import functools
import math

import numpy as np
import jax
import jax.numpy as jnp
from jax import lax
from jax.experimental import pallas as pl
from jax.experimental.pallas import tpu as pltpu

F32 = jnp.float32
BF16 = jnp.bfloat16

D_MODEL = 1024
DEPTH = 2
HG_HEADS = 8
HG_DIM = 128
HG_WIDTH = HG_HEADS * HG_DIM
CHUNK = 64
ATT_HEADS = 16
ATT_DIM = 64
ATT_WIDTH = ATT_HEADS * ATT_DIM
KV_WIDTH = 128
ATT_BLOCK = 128
ATT_SCALE = 1.0 / math.sqrt(ATT_DIM)
ROPE_THETA = 10000.0
NORM_EPS = 1e-6
NEG_INF = -1e30
LB_FLOOR = 1e-20
N_H = 4 * HG_WIDTH
N_A = 2 * ATT_WIDTH + 2 * KV_WIDTH
IN_WIDTH = N_H + N_A
MIX_WIDTH = HG_WIDTH + ATT_WIDTH

ADAM_LR = 0.001
ADAM_B1 = 0.9
ADAM_B2 = 0.999
ADAM_EPS = 1e-08
ADAM_WD = 0.01
ADAM_STEP = 10

N_DEV = 8
MESH = pl.DeviceIdType.MESH
VMEM_LIMIT = 56 * 1024 * 1024

NN = ((1,), (0,))
NT = ((1,), (1,))
TN = ((0,), (0,))


def _dot(a, b, dims):
    return lax.dot_general(a.astype(BF16), b.astype(BF16), (dims, ((), ())), preferred_element_type=F32)


def _params(sem=None, **kw):
    return pltpu.CompilerParams(dimension_semantics=sem, vmem_limit_bytes=VMEM_LIMIT, **kw)


def _sigmoid(x):
    return 1.0 / (1.0 + jnp.exp(-x))


def _silu(x):
    return x * _sigmoid(x)


def _silu_grad(x):
    s = _sigmoid(x)
    return s * (1.0 + x * (1.0 - s))


def _pick(n, prefs):
    for p in prefs:
        if n % p == 0:
            return p
    return n


def _inproj_first(x2, g, w, name):
    T, D = x2.shape
    N = w.shape[0]
    tm = _pick(T, (512, 256, 128))
    tn = _pick(N, (512, 256, 128))

    def body(x_ref, g_ref, w_ref, o_ref, h_ref):
        @pl.when(pl.program_id(1) == 0)
        def _():
            x = x_ref[...]
            r = lax.rsqrt(jnp.mean(x * x, axis=-1, keepdims=True) + NORM_EPS)
            h_ref[...] = ((x * r) * g_ref[...]).astype(BF16)

        o_ref[...] = lax.dot_general(h_ref[...], w_ref[...], (NT, ((), ())), preferred_element_type=F32)

    return pl.pallas_call(
        body, name=name,
        grid=(T // tm, N // tn),
        in_specs=[pl.BlockSpec((tm, D), lambda i, j: (i, 0)),
                  pl.BlockSpec((1, D), lambda i, j: (0, 0)),
                  pl.BlockSpec((tn, D), lambda i, j: (j, 0))],
        out_specs=[pl.BlockSpec((tm, tn), lambda i, j: (i, j)),
                   pl.BlockSpec((tm, D), lambda i, j: (i, 0))],
        out_shape=[jax.ShapeDtypeStruct((T, N), F32), jax.ShapeDtypeStruct((T, D), BF16)],
        compiler_params=_params(("parallel", "arbitrary")),
    )(x2, g, w)


def _mm_nt(a, b, name, out_dtype=F32):
    M, K = a.shape
    N = b.shape[0]
    tm = _pick(M, (512, 256, 128))
    tn = _pick(N, (768, 512, 256, 128))

    def body(a_ref, b_ref, o_ref):
        o_ref[...] = lax.dot_general(a_ref[...], b_ref[...], (NT, ((), ())),
                                     preferred_element_type=F32).astype(out_dtype)

    return pl.pallas_call(
        body, name=name,
        grid=(M // tm, N // tn),
        in_specs=[pl.BlockSpec((tm, K), lambda i, j: (i, 0)),
                  pl.BlockSpec((tn, K), lambda i, j: (j, 0))],
        out_specs=pl.BlockSpec((tm, tn), lambda i, j: (i, j)),
        out_shape=jax.ShapeDtypeStruct((M, N), out_dtype),
        compiler_params=_params(("parallel", "parallel")),
    )(a, b)


def _mm_tn(a, b, name, out_dtype=BF16):
    T, n = a.shape
    m = b.shape[1]
    tn = _pick(n, (512, 256, 128))
    tk = _pick(T, (512, 256, 128))
    nk = T // tk

    def body(a_ref, b_ref, o_ref, acc_ref):
        k = pl.program_id(1)

        @pl.when(k == 0)
        def _():
            acc_ref[...] = jnp.zeros_like(acc_ref)

        acc_ref[...] += lax.dot_general(a_ref[...], b_ref[...], (TN, ((), ())), preferred_element_type=F32)

        @pl.when(k == nk - 1)
        def _():
            o_ref[...] = acc_ref[...].astype(out_dtype)

    return pl.pallas_call(
        body, name=name,
        grid=(n // tn, nk),
        in_specs=[pl.BlockSpec((tk, tn), lambda i, k: (k, i)),
                  pl.BlockSpec((tk, m), lambda i, k: (k, 0))],
        out_specs=pl.BlockSpec((tn, m), lambda i, k: (i, 0)),
        out_shape=jax.ShapeDtypeStruct((n, m), out_dtype),
        scratch_shapes=[pltpu.VMEM((tn, m), F32)],
        compiler_params=_params(("parallel", "arbitrary")),
    )(a, b)


_LEVELS = (8, 16, 32)


def _cum_matrices():
    t = np.arange(CHUNK)[:, None]
    r = np.arange(CHUNK)[None, :]
    mats = []
    for L in (8, 16, 32, 64):
        mats.append(((r // L == t // L) & (r <= t)).astype(np.float32))
    for L in (8, 16, 32, 64):
        mats.append(((r // L == t // L) & (r > t)).astype(np.float32))
    ts = np.concatenate(mats, axis=0)
    return jnp.asarray(ts, BF16), jnp.asarray(ts.T.copy(), BF16)


def _split3(x):
    hi = x.astype(BF16)
    r1 = x - hi.astype(F32)
    mid = r1.astype(BF16)
    lo = (r1 - mid.astype(F32)).astype(BF16)
    return hi, mid, lo


def _cum3(ts, x):
    hi, mid, lo = _split3(x)
    d = lambda p: lax.dot_general(ts, p, (NN, ((), ())), preferred_element_type=F32)
    return d(hi) + d(mid) + d(lo)


def _lb_terms(lbp, layer):
    mx = jnp.max(lbp, axis=0, keepdims=True)
    e = jnp.exp(lbp - mx)
    p = e / jnp.sum(e, axis=0, keepdims=True)
    cum = p[0:1]
    for j in range(1, layer + 1):
        cum = cum + p[j:j + 1]
    lb = cum - p[0:1]
    lbf = jnp.maximum(lb, LB_FLOOR)
    return dict(lb=lb, a=jnp.log(lbf), c=jnp.log(1.0 - lb), one_m=1.0 - lb, kcorr=lb - lbf,
                dlb1=jnp.where(lb > LB_FLOOR, 1.0 / lbf, 0.0), dlb2=1.0 / (1.0 - lb))


def _gate_fwd(x, lt):
    ls = jnp.minimum(x, 0.0) - jnp.log(1.0 + jnp.exp(-jnp.abs(x)))
    u1 = lt["a"]
    u2 = lt["c"] + ls
    mx = jnp.maximum(u1, u2)
    logf = mx + jnp.log(1.0 + jnp.exp(-jnp.abs(u1 - u2)))
    k = lt["one_m"] * (1.0 / (1.0 + jnp.exp(x))) + lt["kcorr"]
    return logf, k, u1, u2


def _level_masks():
    t = lax.broadcasted_iota(jnp.int32, (CHUNK, 1), 0)
    tt = lax.broadcasted_iota(jnp.int32, (CHUNK, CHUNK), 0)
    ss = lax.broadcasted_iota(jnp.int32, (CHUNK, CHUNK), 1)
    ups = {L: ((t // L) % 2) == 1 for L in _LEVELS}
    same = {L: (tt // (2 * L)) == (ss // (2 * L)) for L in _LEVELS}
    return ups, same


def _hg_chunk_fwd(qf, k, v, g, st, ts):
    cs = _cum3(ts, g)
    c = [cs[CHUNK * i:CHUNK * (i + 1)] for i in range(8)]
    ups, same = _level_masks()
    q3, k3, v3, c3 = (a.reshape(8, 8, HG_DIM) for a in (qf, k, v, c[0]))
    row = lax.broadcasted_iota(jnp.int32, (8, 8, HG_DIM), 1)
    o3 = jnp.zeros((8, 8, HG_DIM), F32)
    for j in range(8):
        e = jnp.exp(jnp.minimum(c3 - c3[:, j:j + 1, :], 0.0))
        p = jnp.where(row >= j, q3 * k3[:, j:j + 1, :] * e, 0.0)
        a = jnp.sum(p, axis=-1, keepdims=True)
        o3 = o3 + a * v3[:, j:j + 1, :]
    o = o3.reshape(CHUNK, HG_DIM)
    amat = jnp.zeros((CHUNK, CHUNK), F32)
    for li, L in enumerate(_LEVELS):
        ql = jnp.where(ups[L], qf * jnp.exp(c[li]), 0.0)
        kl = jnp.where(ups[L], 0.0, k * jnp.exp(c[4 + li]))
        al = _dot(ql, kl, NT)
        if L < 32:
            al = jnp.where(same[L], al, 0.0)
        amat = amat + al
    o = o + _dot(amat, v, NN)
    o = o + _dot(qf * jnp.exp(c[3]), st, NT)
    kst = k * jnp.exp(c[7])
    st_new = st * jnp.exp(c[3][CHUNK - 1:CHUNK, :]) + _dot(v, kst, TN)
    return o, st_new


def _hg_chunk_bwd(qf, k, v, g, st, do, dst, ts, tst):
    cs = _cum3(ts, g)
    c = [cs[CHUNK * i:CHUNK * (i + 1)] for i in range(8)]
    ups, same = _level_masks()
    dc = [None] * 8
    da = _dot(do, v, NT)
    dq = jnp.zeros((CHUNK, HG_DIM), F32)
    dk = jnp.zeros((CHUNK, HG_DIM), F32)
    amat = jnp.zeros((CHUNK, CHUNK), F32)
    for li, L in enumerate(_LEVELS):
        eq = jnp.exp(c[li])
        ek = jnp.exp(c[4 + li])
        ql = jnp.where(ups[L], qf * eq, 0.0)
        kl = jnp.where(ups[L], 0.0, k * ek)
        al = _dot(ql, kl, NT)
        dal = da
        if L < 32:
            al = jnp.where(same[L], al, 0.0)
            dal = jnp.where(same[L], da, 0.0)
        amat = amat + al
        dql = _dot(dal, kl, NN)
        dkl = _dot(dal, ql, TN)
        dq = dq + jnp.where(ups[L], dql * eq, 0.0)
        dk = dk + jnp.where(ups[L], 0.0, dkl * ek)
        dc[li] = dql * ql
        dc[4 + li] = dkl * kl
    dv = _dot(amat, do, TN)
    q3, k3, v3, c3, do3 = (a.reshape(8, 8, HG_DIM) for a in (qf, k, v, c[0], do))
    row = lax.broadcasted_iota(jnp.int32, (8, 8, HG_DIM), 1)
    dq3 = jnp.zeros((8, 8, HG_DIM), F32)
    dk3 = jnp.zeros((8, 8, HG_DIM), F32)
    dv3 = jnp.zeros((8, 8, HG_DIM), F32)
    dc3 = jnp.zeros((8, 8, HG_DIM), F32)
    for j in range(8):
        keep = row >= j
        kj = k3[:, j:j + 1, :]
        vj = v3[:, j:j + 1, :]
        e = jnp.where(keep, jnp.exp(jnp.minimum(c3 - c3[:, j:j + 1, :], 0.0)), 0.0)
        p = q3 * kj * e
        a = jnp.sum(p, axis=-1, keepdims=True)
        daj = jnp.sum(do3 * vj, axis=-1, keepdims=True)
        ge = daj * e
        dq3 = dq3 + ge * kj
        h = daj * p
        dc3 = dc3 + h
        onj = row == j
        dk3 = dk3 + jnp.where(onj, jnp.sum(ge * q3, axis=1, keepdims=True), 0.0)
        dc3 = dc3 - jnp.where(onj, jnp.sum(h, axis=1, keepdims=True), 0.0)
        dv3 = dv3 + jnp.where(onj, jnp.sum(a * do3, axis=1, keepdims=True), 0.0)
    dq = dq + dq3.reshape(CHUNK, HG_DIM)
    dk = dk + dk3.reshape(CHUNK, HG_DIM)
    dv = dv + dv3.reshape(CHUNK, HG_DIM)
    dc[0] = dc[0] + dc3.reshape(CHUNK, HG_DIM)
    e64 = jnp.exp(c[3])
    er64 = jnp.exp(c[7])
    qb = qf * e64
    kst = k * er64
    dlast = jnp.exp(c[3][CHUNK - 1:CHUNK, :])
    dqb = _dot(do, st, NN)
    dq = dq + dqb * e64
    dc64 = dqb * qb
    dkst = _dot(v, dst, NN)
    dk = dk + dkst * er64
    dc[7] = dkst * kst
    dv = dv + _dot(kst, dst, NT)
    dst_in = dst * dlast + _dot(do, qb, TN)
    dtot = jnp.sum(dst * st, axis=0, keepdims=True) * dlast
    trow = lax.broadcasted_iota(jnp.int32, (CHUNK, 1), 0)
    dc[3] = dc64 + jnp.where(trow == CHUNK - 1, dtot, 0.0)
    dg = _cum3(tst, jnp.concatenate(dc, axis=0))
    return dq, dk, dv, dg, dst_in


def _hgrn_fwd(proj_h, u_rows, lb_param, g_head, layer, name):
    B, S, _ = proj_h.shape
    sb = _pick(S, (512, 256, 128, 64))
    nc = sb // CHUNK
    ts, _ = _cum_matrices()

    def body(q_ref, f_ref, i_ref, z_ref, lbp_ref, gh_ref, ts_ref, o_ref, u_ref, sts_ref, st, qf_s, k_s, g_s):
        @pl.when(pl.program_id(2) == 0)
        def _():
            st[...] = jnp.zeros_like(st)

        lt = _lb_terms(lbp_ref[...], layer)
        logf, k, _, _ = _gate_fwd(f_ref[...], lt)
        g_s[...] = logf
        k_s[...] = k
        qf_s[...] = _silu(q_ref[...])
        tsv = ts_ref[...]

        def chunk(ci, carry):
            r0 = pl.multiple_of(ci * CHUNK, CHUNK)
            rows = pl.ds(r0, CHUNK)
            cur = st[...]
            sts_ref[ci] = cur
            o, st_new = _hg_chunk_fwd(qf_s[rows, :], k_s[rows, :], i_ref[rows, :], g_s[rows, :], cur, tsv)
            o_ref[rows, :] = o
            st[...] = st_new
            return carry

        lax.fori_loop(0, nc, chunk, 0)
        o = o_ref[...]
        r = lax.rsqrt(jnp.mean(o * o, axis=-1, keepdims=True) + NORM_EPS)
        u_ref[...] = (((o * r) * gh_ref[...]) * _silu(z_ref[...])).astype(BF16)

    col = lambda base: pl.BlockSpec((None, sb, HG_DIM), lambda h, b, s: (b, s, base + h))
    return pl.pallas_call(
        body, name=name,
        grid=(HG_HEADS, B, S // sb),
        in_specs=[col(0), col(HG_HEADS), col(2 * HG_HEADS), col(3 * HG_HEADS),
                  pl.BlockSpec((DEPTH, HG_DIM), lambda h, b, s: (0, h)),
                  pl.BlockSpec((1, HG_DIM), lambda h, b, s: (0, 0)),
                  pl.BlockSpec((8 * CHUNK, CHUNK), lambda h, b, s: (0, 0))],
        out_specs=[col(0), col(0),
                   pl.BlockSpec((None, None, nc, HG_DIM, HG_DIM), lambda h, b, s: (b, h, s, 0, 0))],
        out_shape=[jax.ShapeDtypeStruct((B, S, HG_WIDTH), F32),
                   jax.ShapeDtypeStruct((B, S, u_rows), BF16),
                   jax.ShapeDtypeStruct((B, HG_HEADS, S // CHUNK, HG_DIM, HG_DIM), F32)],
        scratch_shapes=[pltpu.VMEM((HG_DIM, HG_DIM), F32)] + [pltpu.VMEM((sb, HG_DIM), F32)] * 3,
        compiler_params=_params(("parallel", "parallel", "arbitrary")),
    )(proj_h, proj_h, proj_h, proj_h, lb_param, g_head, ts)


def _hgrn_bwd(proj_h, o_h, du, states, lb_param, g_head, layer, name):
    B, S, _ = proj_h.shape
    sb = _pick(S, (512, 256, 128, 64))
    nc = sb // CHUNK
    ns = S // sb
    ts, tst = _cum_matrices()

    def body(q_ref, f_ref, i_ref, z_ref, o_ref, du_ref, sts_ref, lbp_ref, gh_ref, ts_ref, tst_ref,
             dq_ref, df_ref, di_ref, dz_ref, dlb_ref, dgh_ref,
             dst, qf_s, k_s, g_s, do_s, dq_s, dk_s, dg_s):
        h_id, b_id, s_id = pl.program_id(0), pl.program_id(1), pl.program_id(2)

        @pl.when(s_id == 0)
        def _():
            dst[...] = jnp.zeros_like(dst)

        @pl.when((b_id == 0) & (s_id == 0))
        def _():
            dlb_ref[...] = jnp.zeros_like(dlb_ref)

        @pl.when((h_id == 0) & (b_id == 0) & (s_id == 0))
        def _():
            dgh_ref[...] = jnp.zeros_like(dgh_ref)

        lt = _lb_terms(lbp_ref[...], layer)
        x = f_ref[...]
        logf, k, u1, u2 = _gate_fwd(x, lt)
        g_s[...] = logf
        k_s[...] = k
        q = q_ref[...]
        qf_s[...] = _silu(q)
        o = o_ref[...]
        z = z_ref[...]
        gh = gh_ref[...]
        dub = du_ref[...]
        r = lax.rsqrt(jnp.mean(o * o, axis=-1, keepdims=True) + NORM_EPS)
        n = o * r
        sg = _silu(z)
        dz_ref[...] = (dub * (n * gh) * _silu_grad(z)).astype(BF16)
        dgh_ref[...] += jnp.sum(dub * sg * n, axis=0, keepdims=True)
        dn = dub * sg * gh
        do_s[...] = r * (dn - n * jnp.mean(dn * n, axis=-1, keepdims=True))
        tsv = ts_ref[...]
        tstv = tst_ref[...]

        def chunk(cc, carry):
            ci = nc - 1 - cc
            r0 = pl.multiple_of(ci * CHUNK, CHUNK)
            rows = pl.ds(r0, CHUNK)
            dq, dk, dv, dg, dst_in = _hg_chunk_bwd(qf_s[rows, :], k_s[rows, :], i_ref[rows, :], g_s[rows, :],
                                                   sts_ref[ci], do_s[rows, :], dst[...], tsv, tstv)
            dq_s[rows, :] = dq
            dk_s[rows, :] = dk
            dg_s[rows, :] = dg
            di_ref[rows, :] = dv.astype(BF16)
            dst[...] = dst_in
            return carry

        lax.fori_loop(0, nc, chunk, 0)
        dq_ref[...] = (dq_s[...] * _silu_grad(q)).astype(BF16)
        f = jnp.exp(logf)
        dlogf = dg_s[...] - f * dk_s[...]
        w1 = jnp.exp(u1 - logf)
        w2 = jnp.exp(u2 - logf)
        df_ref[...] = (dlogf * w2 * (1.0 / (1.0 + jnp.exp(x)))).astype(BF16)
        dlb_ref[...] += jnp.sum(dlogf * (w1 * lt["dlb1"] - w2 * lt["dlb2"]), axis=0, keepdims=True)

    col = lambda base: pl.BlockSpec((None, sb, HG_DIM), lambda h, b, s: (b, ns - 1 - s, base + h))
    out_col = pl.BlockSpec((None, sb, HG_DIM), lambda h, b, s: (b, ns - 1 - s, h))
    dt = jax.ShapeDtypeStruct((B, S, HG_WIDTH), BF16)
    return pl.pallas_call(
        body, name=name,
        grid=(HG_HEADS, B, ns),
        in_specs=[col(0), col(HG_HEADS), col(2 * HG_HEADS), col(3 * HG_HEADS), col(0), col(0),
                  pl.BlockSpec((None, None, nc, HG_DIM, HG_DIM), lambda h, b, s: (b, h, ns - 1 - s, 0, 0)),
                  pl.BlockSpec((DEPTH, HG_DIM), lambda h, b, s: (0, h)),
                  pl.BlockSpec((1, HG_DIM), lambda h, b, s: (0, 0)),
                  pl.BlockSpec((8 * CHUNK, CHUNK), lambda h, b, s: (0, 0)),
                  pl.BlockSpec((CHUNK, 8 * CHUNK), lambda h, b, s: (0, 0))],
        out_specs=[out_col, out_col, out_col, out_col,
                   pl.BlockSpec((1, HG_DIM), lambda h, b, s: (0, h)),
                   pl.BlockSpec((1, HG_DIM), lambda h, b, s: (0, 0))],
        out_shape=[dt, dt, dt, dt, jax.ShapeDtypeStruct((1, HG_WIDTH), F32), jax.ShapeDtypeStruct((1, HG_DIM), F32)],
        scratch_shapes=[pltpu.VMEM((HG_DIM, HG_DIM), F32)] + [pltpu.VMEM((sb, HG_DIM), F32)] * 7,
        compiler_params=_params(("arbitrary", "arbitrary", "arbitrary")),
    )(proj_h, proj_h, proj_h, proj_h, o_h, du, states, lb_param, g_head, ts, tst)


def _rope_tables(S):
    half = ATT_DIM // 2
    inv_freq = ROPE_THETA ** (-jnp.arange(half, dtype=F32) / half)
    ang = jnp.arange(S).astype(F32)[:, None] * inv_freq[None, :]
    cos = jnp.cos(ang)
    sin = jnp.sin(ang)
    cos = jnp.concatenate([cos, cos, cos, cos], axis=1)
    sin = jnp.concatenate([-sin, sin, -sin, sin], axis=1)
    return cos, sin


def _attn_common():
    lane = lax.broadcasted_iota(jnp.int32, (1, 2 * ATT_DIM), 1)
    first_half = (lane % ATT_DIM) < (ATT_DIM // 2)
    left = lane < ATT_DIM

    def swap(x):
        return jnp.where(first_half, pltpu.roll(x, 128 - ATT_DIM // 2, 1), pltpu.roll(x, ATT_DIM // 2, 1))

    def rope(x, cos, sin):
        return x * cos + swap(x) * sin

    def rope_bwd(dy, cos, sin):
        return dy * cos + swap(dy * sin)

    def dup(x):
        xs = pltpu.roll(x, ATT_DIM, 1)
        return [jnp.where(left, x, xs), jnp.where(left, xs, x)]

    return left, rope, rope_bwd, dup


def _attn_mask(i):
    r = lax.broadcasted_iota(jnp.int32, (ATT_BLOCK, 2 * ATT_BLOCK), 0)
    c = lax.broadcasted_iota(jnp.int32, (ATT_BLOCK, 2 * ATT_BLOCK), 1)
    return (c > r) & (c <= r + ATT_BLOCK) & ((c >= ATT_BLOCK) | (i > 0))


def _attn_probs(q128, kh, sink, mask):
    s = _dot(q128, kh, NT) * ATT_SCALE
    s = jnp.where(mask, s, NEG_INF)
    m = jnp.maximum(jnp.max(s, axis=-1, keepdims=True), sink)
    p = jnp.exp(s - m)
    es = jnp.exp(sink - m)
    inv = 1.0 / (jnp.sum(p, axis=-1, keepdims=True) + es)
    return p * inv, es * inv


_Z0 = (2 * ATT_WIDTH + 2 * KV_WIDTH - ATT_WIDTH) // 256


def _attn_fwd(proj_a, u, sinks_l, cos, sin, name):
    B, S, _ = proj_a.shape
    nb = S // ATT_BLOCK

    def body(q_ref, kvc_ref, kvp_ref, z0, z1, z2, z3, cos_ref, sin_ref, cosp_ref, sinp_ref, sinks_ref, u_in, u_ref):
        del u_in
        i = pl.program_id(1)
        left, rope, _, dup = _attn_common()
        cos_c, sin_c = cos_ref[...], sin_ref[...]
        kvc = kvc_ref[...]
        kvp = kvp_ref[...]
        kw = jnp.concatenate([rope(kvp[:, :KV_WIDTH], cosp_ref[...], sinp_ref[...]),
                              rope(kvc[:, :KV_WIDTH], cos_c, sin_c)], axis=0)
        vw = jnp.concatenate([kvp[:, KV_WIDTH:], kvc[:, KV_WIDTH:]], axis=0)
        kd, vd = dup(kw), dup(vw)
        mask = _attn_mask(i)
        zs = (z0, z1, z2, z3)
        for pair in range(ATT_HEADS // 2):
            kvh = pair // 4
            cols = slice(128 * pair, 128 * (pair + 1))
            q128 = rope(q_ref[:, cols], cos_c, sin_c)
            out = jnp.zeros((ATT_BLOCK, 128), F32)
            for hh in range(2):
                lm = left if hh == 0 else jnp.logical_not(left)
                p, _ = _attn_probs(q128, jnp.where(lm, kd[kvh], 0.0), sinks_ref[2 * pair + hh], mask)
                out = out + _dot(p, jnp.where(lm, vd[kvh], 0.0), NN)
            z = zs[pair // 2][:, 128 * (pair % 2):128 * (pair % 2 + 1)]
            u_ref[:, cols] = (out * _silu(z)).astype(BF16)

    rowblk = lambda w, cb: pl.BlockSpec((None, ATT_BLOCK, w), lambda b, i: (b, i, cb))
    tab = pl.BlockSpec((ATT_BLOCK, 128), lambda b, i: (i, 0))
    tabp = pl.BlockSpec((ATT_BLOCK, 128), lambda b, i: (jnp.maximum(i - 1, 0), 0))
    return pl.pallas_call(
        body, name=name,
        grid=(B, nb),
        in_specs=[rowblk(ATT_WIDTH, 0), rowblk(256, 4),
                  pl.BlockSpec((None, ATT_BLOCK, 256), lambda b, i: (b, jnp.maximum(i - 1, 0), 4)),
                  rowblk(256, _Z0), rowblk(256, _Z0 + 1), rowblk(256, _Z0 + 2), rowblk(256, _Z0 + 3),
                  tab, tab, tabp, tabp,
                  pl.BlockSpec(memory_space=pltpu.SMEM),
                  pl.BlockSpec(memory_space=pl.ANY)],
        out_specs=pl.BlockSpec((None, ATT_BLOCK, ATT_WIDTH), lambda b, i: (b, i, 1)),
        out_shape=jax.ShapeDtypeStruct(u.shape, BF16),
        input_output_aliases={12: 0},
        compiler_params=_params(("parallel", "parallel")),
    )(proj_a, proj_a, proj_a, proj_a, proj_a, proj_a, proj_a, cos, sin, cos, sin, sinks_l, u)


def _attn_bwd(proj_a, du, sinks_l, cos, sin, name):
    B, S, _ = proj_a.shape
    nb = S // ATT_BLOCK

    def body(q_ref, kvc_ref, kvp_ref, z0, z1, z2, z3, du_ref, cos_ref, sin_ref, cosp_ref, sinp_ref, sinks_ref,
             dq_ref, dkv_ref, dz_ref, dsk_ref, carry, sk_acc):
        b_id, i = pl.program_id(0), pl.program_id(1)

        @pl.when((b_id == 0) & (i == 0))
        def _():
            sk_acc[...] = jnp.zeros_like(sk_acc)

        @pl.when(i == 0)
        def _():
            carry[...] = jnp.zeros_like(carry)

        @pl.when(i < nb)
        def _():
            left, rope, rope_bwd, dup = _attn_common()
            cos_c, sin_c = cos_ref[...], sin_ref[...]
            cos_p, sin_p = cosp_ref[...], sinp_ref[...]
            kvc = kvc_ref[...]
            kvp = kvp_ref[...]
            kw = jnp.concatenate([rope(kvp[:, :KV_WIDTH], cos_p, sin_p), rope(kvc[:, :KV_WIDTH], cos_c, sin_c)], axis=0)
            vw = jnp.concatenate([kvp[:, KV_WIDTH:], kvc[:, KV_WIDTH:]], axis=0)
            kd, vd = dup(kw), dup(vw)
            mask = _attn_mask(i)
            zs = (z0, z1, z2, z3)
            lane = lax.broadcasted_iota(jnp.int32, (1, 128), 1)
            dkd = [jnp.zeros((2 * ATT_BLOCK, 128), F32) for _ in range(2)]
            dvd = [jnp.zeros((2 * ATT_BLOCK, 128), F32) for _ in range(2)]
            sk = jnp.zeros((ATT_BLOCK, 128), F32)
            for pair in range(ATT_HEADS // 2):
                kvh = pair // 4
                cols = slice(128 * pair, 128 * (pair + 1))
                q128 = rope(q_ref[:, cols], cos_c, sin_c)
                lms = (left, jnp.logical_not(left))
                probs = []
                out = jnp.zeros((ATT_BLOCK, 128), F32)
                for hh in range(2):
                    p, ps = _attn_probs(q128, jnp.where(lms[hh], kd[kvh], 0.0), sinks_ref[2 * pair + hh], mask)
                    probs.append((p, ps))
                    out = out + _dot(p, jnp.where(lms[hh], vd[kvh], 0.0), NN)
                z = zs[pair // 2][:, 128 * (pair % 2):128 * (pair % 2 + 1)]
                du128 = du_ref[:, cols]
                dz_ref[:, cols] = (du128 * out * _silu_grad(z)).astype(BF16)
                do128 = du128 * _silu(z)
                dq128 = jnp.zeros((ATT_BLOCK, 128), F32)
                for hh in range(2):
                    p, ps = probs[hh]
                    kh = jnp.where(lms[hh], kd[kvh], 0.0)
                    vh = jnp.where(lms[hh], vd[kvh], 0.0)
                    dp = _dot(do128, vh, NT)
                    delta = jnp.sum(p * dp, axis=-1, keepdims=True)
                    ds = p * (dp - delta) * ATT_SCALE
                    sk = sk + jnp.where(lane == 2 * pair + hh, -ps * delta, 0.0)
                    dq128 = dq128 + _dot(ds, kh, NN)
                    dkd[kvh] = dkd[kvh] + jnp.where(lms[hh], _dot(ds, q128, TN), 0.0)
                    dvd[kvh] = dvd[kvh] + jnp.where(lms[hh], _dot(p, do128, TN), 0.0)
                dq_ref[:, cols] = rope_bwd(dq128, cos_c, sin_c).astype(BF16)
            sk_acc[...] += sk
            fold = lambda pr: jnp.where(left, pr[0] + pltpu.roll(pr[0], ATT_DIM, 1), pr[1] + pltpu.roll(pr[1], ATT_DIM, 1))
            dkw = fold(dkd)
            dvw = fold(dvd)
            prev = jnp.concatenate([rope_bwd(dkw[:ATT_BLOCK], cos_p, sin_p), dvw[:ATT_BLOCK]], axis=1)
            cur = jnp.concatenate([rope_bwd(dkw[ATT_BLOCK:], cos_c, sin_c), dvw[ATT_BLOCK:]], axis=1)
            dkv_ref[...] = (carry[...] + prev).astype(BF16)
            carry[...] = cur

        @pl.when(i == nb)
        def _():
            dkv_ref[...] = carry[...].astype(BF16)

        @pl.when((b_id == B - 1) & (i == nb))
        def _():
            dsk_ref[...] = jnp.sum(sk_acc[...], axis=0, keepdims=True)

    cl = lambda i: jnp.minimum(i, nb - 1)
    pv = lambda i: jnp.maximum(jnp.minimum(i, nb - 1) - 1, 0)
    rowblk = lambda w, cb: pl.BlockSpec((None, ATT_BLOCK, w), lambda b, i: (b, cl(i), cb))
    tab = pl.BlockSpec((ATT_BLOCK, 128), lambda b, i: (cl(i), 0))
    tabp = pl.BlockSpec((ATT_BLOCK, 128), lambda b, i: (pv(i), 0))
    return pl.pallas_call(
        body, name=name,
        grid=(B, nb + 1),
        in_specs=[rowblk(ATT_WIDTH, 0), rowblk(256, 4),
                  pl.BlockSpec((None, ATT_BLOCK, 256), lambda b, i: (b, pv(i), 4)),
                  rowblk(256, _Z0), rowblk(256, _Z0 + 1), rowblk(256, _Z0 + 2), rowblk(256, _Z0 + 3),
                  rowblk(ATT_WIDTH, 1),
                  tab, tab, tabp, tabp,
                  pl.BlockSpec(memory_space=pltpu.SMEM)],
        out_specs=[rowblk(ATT_WIDTH, 0),
                   pl.BlockSpec((None, ATT_BLOCK, 256), lambda b, i: (b, jnp.maximum(i - 1, 0), 0)),
                   rowblk(ATT_WIDTH, 0),
                   pl.BlockSpec((1, 128), lambda b, i: (0, 0))],
        out_shape=[jax.ShapeDtypeStruct((B, S, ATT_WIDTH), BF16), jax.ShapeDtypeStruct((B, S, 256), BF16),
                   jax.ShapeDtypeStruct((B, S, ATT_WIDTH), BF16), jax.ShapeDtypeStruct((1, 128), F32)],
        scratch_shapes=[pltpu.VMEM((ATT_BLOCK, 256), F32), pltpu.VMEM((ATT_BLOCK, 128), F32)],
        compiler_params=_params(("arbitrary", "arbitrary")),
    )(proj_a, proj_a, proj_a, proj_a, proj_a, proj_a, proj_a, du, cos, sin, cos, sin, sinks_l)


def _outproj_fwd(u2, w_out, x2, g_post, target2, name):
    T, D = x2.shape
    tm = _pick(T, (512, 256, 128))
    last = target2 is not None

    def body(u_ref, w_ref, x_ref, g_ref, *rest):
        y = lax.dot_general(u_ref[...], w_ref[...], (NN, ((), ())), preferred_element_type=F32)
        r = lax.rsqrt(jnp.mean(y * y, axis=-1, keepdims=True) + NORM_EPS)
        xn = x_ref[...] + (y * r) * g_ref[...]
        if last:
            t_ref, y_ref, dx_ref, loss_ref = rest
            err = xn - t_ref[...]
            dx_ref[...] = err * (1.0 / D)
            sq = err * err
            acc = sq[:, 0:128]
            for kk in range(1, D // 128):
                acc = acc + sq[:, 128 * kk:128 * (kk + 1)]
            part = jnp.sum(acc.reshape(tm // 8, 8, 128), axis=0) * (0.5 / D)

            @pl.when(pl.program_id(0) == 0)
            def _():
                loss_ref[...] = jnp.zeros_like(loss_ref)

            loss_ref[...] += part
        else:
            y_ref, xn_ref = rest
            xn_ref[...] = xn
        y_ref[...] = y

    row = pl.BlockSpec((tm, D), lambda i: (i, 0))
    in_specs = [pl.BlockSpec((tm, MIX_WIDTH), lambda i: (i, 0)),
                pl.BlockSpec((MIX_WIDTH, D), lambda i: (0, 0)), row,
                pl.BlockSpec((1, D), lambda i: (0, 0))]
    args = [u2, w_out, x2, g_post]
    out_specs = [row, row]
    out_shape = [jax.ShapeDtypeStruct((T, D), F32), jax.ShapeDtypeStruct((T, D), F32)]
    if last:
        in_specs.append(row)
        args.append(target2)
        out_specs.append(pl.BlockSpec((8, 128), lambda i: (0, 0)))
        out_shape.append(jax.ShapeDtypeStruct((8, 128), F32))
    return pl.pallas_call(
        body, name=name, grid=(T // tm,), in_specs=in_specs, out_specs=out_specs, out_shape=out_shape,
        compiler_params=_params(("arbitrary",)),
    )(*args)


def _postnorm_bwd(dxn2, y2, g_post, name):
    T, D = y2.shape
    tm = _pick(T, (512, 256, 128))
    nt = T // tm

    def body(dx_ref, y_ref, g_ref, dy_ref, dg_ref, acc):
        i = pl.program_id(0)

        @pl.when(i == 0)
        def _():
            acc[...] = jnp.zeros_like(acc)

        y = y_ref[...]
        dxn = dx_ref[...]
        r = lax.rsqrt(jnp.mean(y * y, axis=-1, keepdims=True) + NORM_EPS)
        n = y * r
        dn = dxn * g_ref[...]
        dy_ref[...] = (r * (dn - n * jnp.mean(dn * n, axis=-1, keepdims=True))).astype(BF16)
        acc[...] += jnp.sum((dxn * n).reshape(tm // 8, 8, D), axis=0)

        @pl.when(i == nt - 1)
        def _():
            dg_ref[...] = jnp.sum(acc[...], axis=0, keepdims=True)

    row = pl.BlockSpec((tm, D), lambda i: (i, 0))
    vec = pl.BlockSpec((1, D), lambda i: (0, 0))
    return pl.pallas_call(
        body, name=name, grid=(nt,), in_specs=[row, row, vec], out_specs=[row, vec],
        out_shape=[jax.ShapeDtypeStruct((T, D), BF16), jax.ShapeDtypeStruct((1, D), F32)],
        scratch_shapes=[pltpu.VMEM((8, D), F32)],
        compiler_params=_params(("arbitrary",)),
    )(dxn2, y2, g_post)


def _inproj_bwd(dproj2, w_t, x2, dxn2, g_pre, name):
    T, D = x2.shape
    K = dproj2.shape[1]
    tm = _pick(T, (256, 128))
    nt = T // tm

    def body(dp_ref, w_ref, x_ref, dxn_ref, g_ref, dx_ref, dg_ref, acc):
        i = pl.program_id(0)

        @pl.when(i == 0)
        def _():
            acc[...] = jnp.zeros_like(acc)

        dh = lax.dot_general(dp_ref[...], w_ref[...], (NN, ((), ())), preferred_element_type=F32)
        x = x_ref[...]
        r = lax.rsqrt(jnp.mean(x * x, axis=-1, keepdims=True) + NORM_EPS)
        n = x * r
        dn = dh * g_ref[...]
        dx_ref[...] = dxn_ref[...] + r * (dn - n * jnp.mean(dn * n, axis=-1, keepdims=True))
        acc[...] += jnp.sum((dh * n).reshape(tm // 8, 8, D), axis=0)

        @pl.when(i == nt - 1)
        def _():
            dg_ref[...] = jnp.sum(acc[...], axis=0, keepdims=True)

    row = pl.BlockSpec((tm, D), lambda i: (i, 0))
    vec = pl.BlockSpec((1, D), lambda i: (0, 0))
    return pl.pallas_call(
        body, name=name, grid=(nt,),
        in_specs=[pl.BlockSpec((tm, K), lambda i: (i, 0)),
                  pl.BlockSpec((K, D), lambda i: (0, 0), pipeline_mode=pl.Buffered(1)), row, row, vec],
        out_specs=[row, vec],
        out_shape=[jax.ShapeDtypeStruct((T, D), F32), jax.ShapeDtypeStruct((1, D), F32)],
        scratch_shapes=[pltpu.VMEM((8, D), F32)],
        compiler_params=_params(("arbitrary",)),
    )(dproj2, w_t, x2, dxn2, g_pre)


def _local_step(x, target, w_in_t, w_out, g_pre, g_post, lb_param, g_head, sinks):
    B, S, D = x.shape
    T = B * S
    cos, sin = _rope_tables(S)
    saved = []
    xs = x
    loss_part = None
    dxn = None
    for l in range(DEPTH):
        x2 = xs.reshape(T, D)
        proj_h, h = _inproj_first(x2, g_pre[l:l + 1], w_in_t[l, :N_H], f"inproj_h{l}")
        proj_a = _mm_nt(h, w_in_t[l, N_H:], f"inproj_a{l}")
        proj_h = proj_h.reshape(B, S, N_H)
        proj_a = proj_a.reshape(B, S, N_A)
        o_h, u, states = _hgrn_fwd(proj_h, MIX_WIDTH, lb_param, g_head[l:l + 1], l, f"hgrn_fwd{l}")
        u = _attn_fwd(proj_a, u, sinks[l], cos, sin, f"attn_fwd{l}")
        u2 = u.reshape(T, MIX_WIDTH)
        if l < DEPTH - 1:
            y, xn = _outproj_fwd(u2, w_out[l], x2, g_post[l:l + 1], None, f"outproj{l}")
            xn = xn.reshape(B, S, D)
        else:
            y, dxn, loss_part = _outproj_fwd(u2, w_out[l], x2, g_post[l:l + 1], target.reshape(T, D), f"outproj{l}")
            xn = None
        saved.append((x2, h, proj_h, proj_a, o_h, u2, states, y))
        xs = xn

    dw_in, dw_out, dg_pre, dg_post, dlb, dg_head, dsinks = [], [], [], [], [], [], []
    for l in reversed(range(DEPTH)):
        x2, h, proj_h, proj_a, o_h, u2, states, y = saved[l]
        dy, dgp = _postnorm_bwd(dxn, y, g_post[l:l + 1], f"postnorm_bwd{l}")
        dw_out.append(_mm_tn(u2, dy, f"wgrad_out{l}"))
        du = _mm_nt(dy, w_out[l], f"dgrad_out{l}").reshape(B, S, MIX_WIDTH)
        dqh, dfh, dih, dzh, dlb_l, dgh = _hgrn_bwd(proj_h, o_h, du, states, lb_param, g_head[l:l + 1], l, f"hgrn_bwd{l}")
        dqa, dkv, dza, dsk = _attn_bwd(proj_a, du, sinks[l], cos, sin, f"attn_bwd{l}")
        dproj = jnp.concatenate([dqh, dfh, dih, dzh, dqa, dkv, dza], axis=-1).reshape(T, IN_WIDTH)
        dw_in.append(_mm_tn(dproj, h, f"wgrad_in{l}"))
        dxn, dgpre = _inproj_bwd(dproj, w_in_t[l], x2, dxn, g_pre[l:l + 1], f"inproj_bwd{l}")
        dg_pre.append(dgpre)
        dg_post.append(dgp)
        dlb.append(dlb_l)
        dg_head.append(dgh)
        dsinks.append(dsk)
    rev = lambda lst: jnp.concatenate(lst[::-1], axis=0)
    return (loss_part, dxn.reshape(B, S, D), jnp.stack(dw_in[::-1]), jnp.stack(dw_out[::-1]),
            rev(dg_pre), rev(dg_post), rev(dlb), rev(dg_head), rev(dsinks))


def _me_and_peers():
    x, y, c = lax.axis_index("x"), lax.axis_index("y"), lax.axis_index("c")
    me = 4 * x + 2 * y + c
    peers = []
    for k in range(1, N_DEV):
        px = 1 - x if k & 4 else x
        py = 1 - y if k & 2 else y
        pc = 1 - c if k & 1 else c
        peers.append(((px, py, pc), 4 * px + 2 * py + pc))
    return me, peers


def _gather_weights(a_loc, b_loc):
    L, ra, D = a_loc.shape
    rb = b_loc.shape[1]

    def body(a_ref, b_ref, ao_ref, bo_ref, send_sems, recv_sems, loc_sems):
        me, peers = _me_and_peers()
        parts = [(a_ref, ao_ref, ra), (b_ref, bo_ref, rb)]

        def rows(dst, l, r, dev):
            return dst.at[l, pl.ds(pl.multiple_of(dev * r, 16), r), :]

        local = []
        for l in range(L):
            for t, (src, dst, r) in enumerate(parts):
                cp = pltpu.make_async_copy(src.at[l], rows(dst, l, r, me), loc_sems.at[2 * l + t])
                cp.start()
                local.append(cp)
        sends = []
        for k, (pid, _) in enumerate(peers):
            for l in range(L):
                for t, (src, dst, r) in enumerate(parts):
                    cp = pltpu.make_async_remote_copy(
                        src_ref=src.at[l], dst_ref=rows(dst, l, r, me),
                        send_sem=send_sems.at[k, 2 * l + t], recv_sem=recv_sems.at[k, 2 * l + t],
                        device_id=pid, device_id_type=MESH)
                    cp.start()
                    sends.append(cp)
        for k, (pid, pnum) in enumerate(peers):
            for l in range(L):
                for t, (src, dst, r) in enumerate(parts):
                    pltpu.make_async_remote_copy(
                        src_ref=src.at[l], dst_ref=rows(dst, l, r, pnum),
                        send_sem=send_sems.at[k, 2 * l + t], recv_sem=recv_sems.at[k, 2 * l + t],
                        device_id=pid, device_id_type=MESH).wait_recv()
        for cp in sends:
            cp.wait_send()
        for cp in local:
            cp.wait()

    hbm = pl.BlockSpec(memory_space=pl.ANY)
    return pl.pallas_call(
        body, name="gather_weights",
        in_specs=[hbm, hbm], out_specs=[hbm, hbm],
        out_shape=[jax.ShapeDtypeStruct((L, N_DEV * ra, D), a_loc.dtype),
                   jax.ShapeDtypeStruct((L, N_DEV * rb, D), b_loc.dtype)],
        scratch_shapes=[pltpu.SemaphoreType.DMA((N_DEV - 1, 2 * L)), pltpu.SemaphoreType.DMA((N_DEV - 1, 2 * L)),
                        pltpu.SemaphoreType.DMA((2 * L,))],
        compiler_params=pltpu.CompilerParams(has_side_effects=True),
    )(a_loc, b_loc)


def _scatter_grads(ga, gb):
    L, ra8, D = ga.shape
    ra = ra8 // N_DEV
    rb = gb.shape[1] // N_DEV

    def body(a_ref, b_ref, ao_ref, bo_ref, send_sems, recv_sems, loc_sems):
        me, peers = _me_and_peers()
        parts = [(a_ref, ao_ref, ra), (b_ref, bo_ref, rb)]

        def rows(src, l, r, dev):
            return src.at[l, pl.ds(pl.multiple_of(dev * r, 16), r), :]

        local = []
        for l in range(L):
            for t, (src, dst, r) in enumerate(parts):
                cp = pltpu.make_async_copy(rows(src, l, r, me), dst.at[me, l], loc_sems.at[2 * l + t])
                cp.start()
                local.append(cp)
        sends = []
        for k, (pid, pnum) in enumerate(peers):
            for l in range(L):
                for t, (src, dst, r) in enumerate(parts):
                    cp = pltpu.make_async_remote_copy(
                        src_ref=rows(src, l, r, pnum), dst_ref=dst.at[me, l],
                        send_sem=send_sems.at[k, 2 * l + t], recv_sem=recv_sems.at[k, 2 * l + t],
                        device_id=pid, device_id_type=MESH)
                    cp.start()
                    sends.append(cp)
        for k, (pid, pnum) in enumerate(peers):
            for l in range(L):
                for t, (src, dst, r) in enumerate(parts):
                    pltpu.make_async_remote_copy(
                        src_ref=rows(src, l, r, pnum), dst_ref=dst.at[pnum, l],
                        send_sem=send_sems.at[k, 2 * l + t], recv_sem=recv_sems.at[k, 2 * l + t],
                        device_id=pid, device_id_type=MESH).wait_recv()
        for cp in sends:
            cp.wait_send()
        for cp in local:
            cp.wait()

    hbm = pl.BlockSpec(memory_space=pl.ANY)
    return pl.pallas_call(
        body, name="scatter_grads",
        in_specs=[hbm, hbm], out_specs=[hbm, hbm],
        out_shape=[jax.ShapeDtypeStruct((N_DEV, L, ra, D), ga.dtype),
                   jax.ShapeDtypeStruct((N_DEV, L, rb, D), gb.dtype)],
        scratch_shapes=[pltpu.SemaphoreType.DMA((N_DEV - 1, 2 * L)), pltpu.SemaphoreType.DMA((N_DEV - 1, 2 * L)),
                        pltpu.SemaphoreType.DMA((2 * L,))],
        compiler_params=pltpu.CompilerParams(has_side_effects=True),
    )(ga, gb)


def _sum_slots(r, name):
    _, R, D = r.shape
    tr = _pick(R, (400, 256, 200, 128, 64, 16))

    def body(r_ref, o_ref):
        acc = r_ref[0].astype(F32)
        for d in range(1, N_DEV):
            acc = acc + r_ref[d].astype(F32)
        o_ref[...] = acc

    return pl.pallas_call(
        body, name=name, grid=(R // tr,),
        in_specs=[pl.BlockSpec((N_DEV, tr, D), lambda i: (0, i, 0))],
        out_specs=pl.BlockSpec((tr, D), lambda i: (i, 0)),
        out_shape=jax.ShapeDtypeStruct((R, D), F32),
        compiler_params=_params(("parallel",)),
    )(r)


def _allreduce_small(vec):
    R, C = vec.shape

    def body(v_ref, o_ref, buf, send_sems, recv_sems):
        me, peers = _me_and_peers()
        buf[me] = v_ref[...]
        sends = []
        for k, (pid, _) in enumerate(peers):
            cp = pltpu.make_async_remote_copy(src_ref=v_ref, dst_ref=buf.at[me], send_sem=send_sems.at[k],
                                              recv_sem=recv_sems.at[k], device_id=pid, device_id_type=MESH)
            cp.start()
            sends.append(cp)
        for k, (pid, pnum) in enumerate(peers):
            pltpu.make_async_remote_copy(src_ref=v_ref, dst_ref=buf.at[pnum], send_sem=send_sems.at[k],
                                         recv_sem=recv_sems.at[k], device_id=pid, device_id_type=MESH).wait_recv()
        for cp in sends:
            cp.wait_send()
        acc = buf[0]
        for d in range(1, N_DEV):
            acc = acc + buf[d]
        o_ref[...] = acc

    vm = pl.BlockSpec(memory_space=pltpu.VMEM)
    return pl.pallas_call(
        body, name="allreduce_small",
        in_specs=[vm], out_specs=vm,
        out_shape=jax.ShapeDtypeStruct((R, C), F32),
        scratch_shapes=[pltpu.VMEM((N_DEV, R, C), F32), pltpu.SemaphoreType.DMA((N_DEV - 1,)),
                        pltpu.SemaphoreType.DMA((N_DEV - 1,))],
        compiler_params=pltpu.CompilerParams(has_side_effects=True),
    )(vec)


def _adamw(w, g, m, v, name):
    R, C = w.shape
    tr = _pick(R, (256, 128, 64, 32, 16, 8)) if R >= 8 else R
    c1 = 1.0 - ADAM_B1 ** ADAM_STEP
    c2 = 1.0 - ADAM_B2 ** ADAM_STEP

    def body(w_ref, g_ref, m_ref, v_ref, d_ref, mo_ref, vo_ref):
        gg = g_ref[...]
        mn = ADAM_B1 * m_ref[...] + (1.0 - ADAM_B1) * gg
        vn = ADAM_B2 * v_ref[...] + (1.0 - ADAM_B2) * (gg * gg)
        d_ref[...] = -ADAM_LR * ((mn / c1) / (jnp.sqrt(vn / c2) + ADAM_EPS) + ADAM_WD * w_ref[...])
        mo_ref[...] = mn
        vo_ref[...] = vn

    blk = pl.BlockSpec((tr, C), lambda i: (i, 0))
    sh = jax.ShapeDtypeStruct((R, C), F32)
    return pl.pallas_call(
        body, name=name, grid=(R // tr,), in_specs=[blk] * 4, out_specs=[blk] * 3, out_shape=[sh] * 3,
        compiler_params=_params(("parallel",)),
    )(w, g, m, v)


def _lb_param_grad(lb_param, dlb):
    L, C = lb_param.shape

    def body(p_ref, d_ref, o_ref):
        lbp = p_ref[...]
        d = d_ref[...]
        mx = jnp.max(lbp, axis=0, keepdims=True)
        e = jnp.exp(lbp - mx)
        p = e / jnp.sum(e, axis=0, keepdims=True)
        tot = jnp.sum(d, axis=0, keepdims=True)
        dps = []
        rest = tot
        for j in range(L):
            dps.append(rest - tot if j == 0 else rest)
            rest = rest - d[j:j + 1]
        dp = jnp.concatenate(dps, axis=0)
        o_ref[...] = p * (dp - jnp.sum(p * dp, axis=0, keepdims=True))

    vm = pl.BlockSpec(memory_space=pltpu.VMEM)
    return pl.pallas_call(body, name="lb_param_grad", in_specs=[vm, vm], out_specs=vm,
                          out_shape=jax.ShapeDtypeStruct((L, C), F32))(lb_param, dlb)


def _pack_small(loss_part, dg_pre, dg_post, dlb, dg_head, dsinks):
    pad8 = lambda a: jnp.pad(a.reshape(-1, 128), ((0, 8 - DEPTH), (0, 0)))
    rows = [dg_pre.reshape(-1, 128), dg_post.reshape(-1, 128), dlb.reshape(-1, 128), pad8(dg_head), pad8(dsinks),
            loss_part]
    return jnp.concatenate(rows, axis=0)


def _unpack_small(vec):
    n = DEPTH * D_MODEL // 128
    o = 0
    dg_pre = vec[o:o + n].reshape(DEPTH, D_MODEL); o += n
    dg_post = vec[o:o + n].reshape(DEPTH, D_MODEL); o += n
    dlb = vec[o:o + n].reshape(DEPTH, HG_WIDTH); o += n
    dg_head = vec[o:o + DEPTH]; o += 8
    dsinks = vec[o:o + DEPTH, :ATT_HEADS]; o += 8
    loss = jnp.sum(vec[o:o + 8])
    return loss, dg_pre, dg_post, dlb, dg_head, dsinks


def kernel(x, w_in, w_out, g_pre, g_post, lb_param, g_head, sinks, loss_target, m_w_in, m_w_out, m_g_pre, m_g_post, m_lb_param, m_g_head, m_sinks, v_w_in, v_w_out, v_g_pre, v_g_post, v_lb_param, v_g_head, v_sinks):
    L, D, nloc = w_in.shape
    w_in_t_loc = jnp.swapaxes(w_in, 1, 2).astype(BF16)
    w_in_t, w_out_full = _gather_weights(w_in_t_loc, w_out.astype(BF16))

    (loss_part, dx, dw_in_t, dw_out, dg_pre, dg_post, dlb, dg_head, dsinks) = _local_step(
        x, loss_target, w_in_t, w_out_full, g_pre, g_post, lb_param, g_head, sinks)

    ra, rb = _scatter_grads(dw_in_t, dw_out)
    gw_in_t = _sum_slots(ra.reshape(N_DEV, L * nloc, D), "sum_w_in").reshape(L, nloc, D)
    gw_out = _sum_slots(rb.reshape(N_DEV, L * w_out.shape[1], D), "sum_w_out").reshape(w_out.shape)
    gw_in = jnp.swapaxes(gw_in_t, 1, 2)

    small = _allreduce_small(_pack_small(loss_part, dg_pre, dg_post, dlb, dg_head, dsinks))
    loss, gg_pre, gg_post, gdlb, gg_head, gsinks = _unpack_small(small)
    glb = _lb_param_grad(lb_param, gdlb)

    grads = [gw_in, gw_out, gg_pre, gg_post, glb, gg_head, gsinks]
    ws = [w_in, w_out, g_pre, g_post, lb_param, g_head, sinks]
    ms = [m_w_in, m_w_out, m_g_pre, m_g_post, m_lb_param, m_g_head, m_sinks]
    vs = [v_w_in, v_w_out, v_g_pre, v_g_post, v_lb_param, v_g_head, v_sinks]
    names = ["w_in", "w_out", "g_pre", "g_post", "lb_param", "g_head", "sinks"]
    deltas, new_m, new_v = [], [], []
    for w, g, m, v, nm in zip(ws, grads, ms, vs, names):
        sh = w.shape
        two = lambda a: a.reshape(-1, sh[-1])
        d, mn, vn = _adamw(two(w), two(g), two(m), two(v), "adamw_" + nm)
        deltas.append(d.reshape(sh))
        new_m.append(mn.reshape(sh))
        new_v.append(vn.reshape(sh))
    return (loss, dx, *grads, *deltas, *new_m, *new_v)
```

```python
import functools
import math

import numpy as np
import jax
import jax.numpy as jnp
from jax import lax
from jax.experimental import pallas as pl
from jax.experimental.pallas import tpu as pltpu

F32 = jnp.float32
BF16 = jnp.bfloat16

D_MODEL = 1024
DEPTH = 2
HG_HEADS = 8
HG_DIM = 128
HG_WIDTH = HG_HEADS * HG_DIM
CHUNK = 64
ATT_HEADS = 16
ATT_DIM = 64
ATT_WIDTH = ATT_HEADS * ATT_DIM
KV_WIDTH = 128
ATT_BLOCK = 128
ATT_SCALE = 1.0 / math.sqrt(ATT_DIM)
ROPE_THETA = 10000.0
NORM_EPS = 1e-6
NEG_INF = -1e30
LB_FLOOR = 1e-20
N_H = 4 * HG_WIDTH
N_A = 2 * ATT_WIDTH + 2 * KV_WIDTH
IN_WIDTH = N_H + N_A
MIX_WIDTH = HG_WIDTH + ATT_WIDTH

ADAM_LR = 0.001
ADAM_B1 = 0.9
ADAM_B2 = 0.999
ADAM_EPS = 1e-08
ADAM_WD = 0.01
ADAM_STEP = 10

N_DEV = 8
MESH = pl.DeviceIdType.MESH
VMEM_LIMIT = 56 * 1024 * 1024

NN = ((1,), (0,))
NT = ((1,), (1,))
TN = ((0,), (0,))


def _dot(a, b, dims):
    return lax.dot_general(a.astype(BF16), b.astype(BF16), (dims, ((), ())), preferred_element_type=F32)


def _params(sem=None, **kw):
    return pltpu.CompilerParams(dimension_semantics=sem, vmem_limit_bytes=VMEM_LIMIT, **kw)


def _sigmoid(x):
    return 1.0 / (1.0 + jnp.exp(-x))


def _silu(x):
    return x * _sigmoid(x)


def _silu_grad(x):
    s = _sigmoid(x)
    return s * (1.0 + x * (1.0 - s))


def _pick(n, prefs):
    for p in prefs:
        if n % p == 0:
            return p
    return n


def _inproj(x2, g, w, name):
    T, D = x2.shape
    tm = _pick(T, (256, 128))
    nchunk = 1024

    def body(x_ref, g_ref, w_ref, oh_ref, oa_ref, h_ref):
        x = x_ref[...]
        r = lax.rsqrt(jnp.mean(x * x, axis=-1, keepdims=True) + NORM_EPS)
        h = ((x * r) * g_ref[...]).astype(BF16)
        h_ref[...] = h
        for j in range(0, N_H, nchunk):
            oh_ref[:, j:j + nchunk] = lax.dot_general(h, w_ref[j:j + nchunk, :], (NT, ((), ())),
                                                      preferred_element_type=F32)
        for j in range(0, N_A, N_A // 2):
            oa_ref[:, j:j + N_A // 2] = lax.dot_general(h, w_ref[N_H + j:N_H + j + N_A // 2, :], (NT, ((), ())),
                                                        preferred_element_type=F32)

    row = lambda w_: pl.BlockSpec((tm, w_), lambda i: (i, 0))
    return pl.pallas_call(
        body, name=name,
        grid=(T // tm,),
        in_specs=[row(D), pl.BlockSpec((1, D), lambda i: (0, 0)),
                  pl.BlockSpec((IN_WIDTH, D), lambda i: (0, 0), pipeline_mode=pl.Buffered(1))],
        out_specs=[row(N_H), row(N_A), row(D)],
        out_shape=[jax.ShapeDtypeStruct((T, N_H), F32), jax.ShapeDtypeStruct((T, N_A), F32),
                   jax.ShapeDtypeStruct((T, D), BF16)],
        compiler_params=_params(("parallel",)),
    )(x2, g, w)


def _mm_nt(a, b, name, out_dtype=F32):
    M, K = a.shape
    N = b.shape[0]
    tm = _pick(M, (512, 256, 128))

    def body(a_ref, b_ref, o_ref):
        o_ref[...] = lax.dot_general(a_ref[...], b_ref[...], (NT, ((), ())),
                                     preferred_element_type=F32).astype(out_dtype)

    return pl.pallas_call(
        body, name=name,
        grid=(M // tm,),
        in_specs=[pl.BlockSpec((tm, K), lambda i: (i, 0)),
                  pl.BlockSpec((N, K), lambda i: (0, 0), pipeline_mode=pl.Buffered(1))],
        out_specs=pl.BlockSpec((tm, N), lambda i: (i, 0)),
        out_shape=jax.ShapeDtypeStruct((M, N), out_dtype),
        compiler_params=_params(("parallel",)),
    )(a, b)


def _mm_tn(a, b, name, out_dtype=BF16):
    T, n = a.shape
    m = b.shape[1]
    tn = _pick(n, (256, 128))

    def body(a_ref, b_ref, o_ref):
        o_ref[...] = lax.dot_general(a_ref[...], b_ref[...], (TN, ((), ())),
                                     preferred_element_type=F32).astype(out_dtype)

    return pl.pallas_call(
        body, name=name,
        grid=(n // tn,),
        in_specs=[pl.BlockSpec((T, tn), lambda i: (0, i)),
                  pl.BlockSpec((T, m), lambda i: (0, 0), pipeline_mode=pl.Buffered(1))],
        out_specs=pl.BlockSpec((tn, m), lambda i: (i, 0)),
        out_shape=jax.ShapeDtypeStruct((n, m), out_dtype),
        compiler_params=_params(("parallel",)),
    )(a, b)


_LEVELS = (0, 1, 2, 4, 8, 16, 32)
_CUM_L = (2, 4, 8, 16, 32, 64)
_ALL_KINDS = tuple(("c", L) for L in _CUM_L) + tuple(("r", L) for L in _CUM_L)
_MXU_KINDS = (("c", 2), ("c", 4), ("c", CHUNK), ("r", 2), ("r", 4))
N_CUM = len(_ALL_KINDS) * CHUNK
N_CUM_F = len(_MXU_KINDS) * CHUNK


def _cum_matrices():
    t = np.arange(CHUNK)[:, None]
    r = np.arange(CHUNK)[None, :]

    def mat(kind):
        c, L = kind
        return ((r // L == t // L) & ((r <= t) if c == "c" else (r > t))).astype(np.float32)

    fwd = np.concatenate([mat(kd) for kd in _MXU_KINDS], axis=0)
    full = np.concatenate([mat(kd) for kd in _ALL_KINDS], axis=0)
    return jnp.asarray(fwd, BF16), jnp.asarray(full.T.copy(), BF16)


def _level_masks():
    t = np.arange(CHUNK)[:, None]
    s = np.arange(CHUNK)[None, :]
    ms = []
    for L in _LEVELS:
        if L == 0:
            ms.append(t == s)
        else:
            ms.append((t // (2 * L) == s // (2 * L)) & ((t // L) % 2 == 1) & ((s // L) % 2 == 0))
    return jnp.asarray(np.stack(ms).astype(np.float32))


def _split3(x):
    hi = x.astype(BF16)
    r1 = x - hi.astype(F32)
    mid = r1.astype(BF16)
    lo = (r1 - mid.astype(F32)).astype(BF16)
    return hi, mid, lo


def _cum3(ts, x):
    hi, mid, lo = _split3(x)
    d = lambda p: lax.dot_general(ts, p, (NN, ((), ())), preferred_element_type=F32)
    return d(hi) + d(mid) + d(lo)


def _lb_terms(lbp, layer):
    mx = jnp.max(lbp, axis=0, keepdims=True)
    e = jnp.exp(lbp - mx)
    p = e / jnp.sum(e, axis=0, keepdims=True)
    cum = p[0:1]
    for j in range(1, layer + 1):
        cum = cum + p[j:j + 1]
    lb = cum - p[0:1]
    lbf = jnp.maximum(lb, LB_FLOOR)
    return dict(lb=lb, a=jnp.log(lbf), c=jnp.log(1.0 - lb), one_m=1.0 - lb, kcorr=lb - lbf,
                dlb1=jnp.where(lb > LB_FLOOR, 1.0 / lbf, 0.0), dlb2=1.0 / (1.0 - lb))


def _gate_fwd(x, lt):
    ls = jnp.minimum(x, 0.0) - jnp.log(1.0 + jnp.exp(-jnp.abs(x)))
    u1 = lt["a"]
    u2 = lt["c"] + ls
    mx = jnp.maximum(u1, u2)
    logf = mx + jnp.log(1.0 + jnp.exp(-jnp.abs(u1 - u2)))
    k = lt["one_m"] * (1.0 / (1.0 + jnp.exp(x))) + lt["kcorr"]
    return logf, k, u1, u2


def _chunk_cums(ts, g):
    cs = _cum3(ts, g)
    out = {kind: cs[CHUNK * i:CHUNK * (i + 1)] for i, kind in enumerate(_MXU_KINDS)}
    b = out[("c", CHUNK)]
    last = [jnp.broadcast_to(b[8 * r + 7:8 * r + 8, :], (8, HG_DIM)) for r in range(CHUNK // 8)]
    zero = jnp.zeros((8, HG_DIM), F32)
    for L in (8, 16, 32):
        nb = L // 8
        before = [last[(r // nb) * nb - 1] if r >= nb else zero for r in range(CHUNK // 8)]
        end = [last[(r // nb) * nb + nb - 1] for r in range(CHUNK // 8)]
        out[("c", L)] = b - jnp.concatenate(before, axis=0)
        out[("r", L)] = jnp.concatenate(end, axis=0) - b
    out[("r", CHUNK)] = jnp.broadcast_to(b[CHUNK - 1:CHUNK, :], (CHUNK, HG_DIM)) - b
    return out


def _level_factors(cums, g, L):
    if L == 0:
        return None, None
    if L == 1:
        return jnp.exp(g), None
    return jnp.exp(cums[("c", L)]), jnp.exp(cums[("r", L)])


def _mul(a, e):
    return a if e is None else a * e


def _hg_intra_fwd(qf, k, v, g, ts, m_ref):
    cums = _chunk_cums(ts, g)
    amat = jnp.zeros((CHUNK, CHUNK), F32)
    for li, L in enumerate(_LEVELS):
        eq, ek = _level_factors(cums, g, L)
        amat = amat + _dot(_mul(qf, eq), _mul(k, ek), NT) * m_ref[li]
    b = cums[("c", CHUNK)]
    kv = _dot(v, k * jnp.exp(cums[("r", CHUNK)]), TN)
    return _dot(amat, v, NN), qf * jnp.exp(b), jnp.exp(b[CHUNK - 1:CHUNK, :]), kv


def _hg_intra_bwd(qf, k, v, g, do, ts, m_ref):
    cums = _chunk_cums(ts, g)
    dcs = {}
    da = _dot(do, v, NT)
    dq = jnp.zeros((CHUNK, HG_DIM), F32)
    dk = jnp.zeros((CHUNK, HG_DIM), F32)
    dg = jnp.zeros((CHUNK, HG_DIM), F32)
    amat = jnp.zeros((CHUNK, CHUNK), F32)
    for li, L in enumerate(_LEVELS):
        eq, ek = _level_factors(cums, g, L)
        ql = _mul(qf, eq)
        kl = _mul(k, ek)
        m = m_ref[li]
        amat = amat + _dot(ql, kl, NT) * m
        dal = da * m
        dql = _dot(dal, kl, NN)
        dkl = _dot(dal, ql, TN)
        dq = dq + _mul(dql, eq)
        dk = dk + _mul(dkl, ek)
        if L == 1:
            dg = dg + dql * ql
        elif L > 1:
            dcs[("c", L)] = dql * ql
            dcs[("r", L)] = dkl * kl
    b = cums[("c", CHUNK)]
    e64 = jnp.exp(b)
    er64 = jnp.exp(cums[("r", CHUNK)])
    qb = qf * e64
    return dict(dq=dq, dk=dk, dv=_dot(amat, do, TN), dg=dg, dcs=dcs, e64=e64, er64=er64, qb=qb, kst=k * er64,
                dec=jnp.exp(b[CHUNK - 1:CHUNK, :]), qd=_dot(do, qb, TN))


def _hg_state_bwd(w, v, do, st, dst, tst):
    dqb = _dot(do, st, NN)
    dkst = _dot(v, dst, NN)
    dq = w["dq"] + dqb * w["e64"]
    dk = w["dk"] + dkst * w["er64"]
    dv = w["dv"] + _dot(w["kst"], dst, NT)
    dtot = jnp.sum(dst * st, axis=0, keepdims=True) * w["dec"]
    trow = lax.broadcasted_iota(jnp.int32, (CHUNK, 1), 0)
    dcs = dict(w["dcs"])
    dcs[("c", CHUNK)] = dqb * w["qb"] + jnp.where(trow == CHUNK - 1, dtot, 0.0)
    dcs[("r", CHUNK)] = dkst * w["kst"]
    stack = jnp.concatenate([dcs[kind] for kind in _ALL_KINDS], axis=0)
    return dq, dk, dv, w["dg"] + _cum3(tst, stack)


def _hgrn_fwd(proj_h, u_rows, lb_param, g_head, layer, name):
    B, S, _ = proj_h.shape
    sb = _pick(S, (512, 256, 128, 64))
    nc = sb // CHUNK
    ts, _ = _cum_matrices()

    def body(q_ref, f_ref, i_ref, z_ref, lbp_ref, gh_ref, ts_ref, m_ref, o_ref, u_ref, sts_ref, st):
        @pl.when(pl.program_id(2) == 0)
        def _():
            st[...] = jnp.zeros_like(st)

        lt = _lb_terms(lbp_ref[...], layer)
        tsv = ts_ref[...]
        gh = gh_ref[...]
        parts = []
        for ci in range(nc):
            rows = slice(ci * CHUNK, (ci + 1) * CHUNK)
            logf, k, _, _ = _gate_fwd(f_ref[rows, :], lt)
            parts.append(_hg_intra_fwd(_silu(q_ref[rows, :]), k, i_ref[rows, :], logf, tsv, m_ref))
        cur = st[...]
        starts = []
        for ci in range(nc):
            sts_ref[ci] = cur
            starts.append(cur)
            cur = cur * parts[ci][2] + parts[ci][3]
        st[...] = cur
        for ci in range(nc):
            rows = slice(ci * CHUNK, (ci + 1) * CHUNK)
            o = parts[ci][0] + _dot(parts[ci][1], starts[ci], NT)
            o_ref[rows, :] = o
            r = lax.rsqrt(jnp.mean(o * o, axis=-1, keepdims=True) + NORM_EPS)
            u_ref[rows, :] = (((o * r) * gh) * _silu(z_ref[rows, :])).astype(BF16)

    col = lambda base: pl.BlockSpec((None, sb, HG_DIM), lambda h, b, s: (b, s, base + h))
    return pl.pallas_call(
        body, name=name,
        grid=(HG_HEADS, B, S // sb),
        in_specs=[col(0), col(HG_HEADS), col(2 * HG_HEADS), col(3 * HG_HEADS),
                  pl.BlockSpec((DEPTH, HG_DIM), lambda h, b, s: (0, h)),
                  pl.BlockSpec((1, HG_DIM), lambda h, b, s: (0, 0)),
                  pl.BlockSpec((N_CUM_F, CHUNK), lambda h, b, s: (0, 0)),
                  pl.BlockSpec((len(_LEVELS), CHUNK, CHUNK), lambda h, b, s: (0, 0, 0))],
        out_specs=[col(0), col(0),
                   pl.BlockSpec((None, None, nc, HG_DIM, HG_DIM), lambda h, b, s: (b, h, s, 0, 0))],
        out_shape=[jax.ShapeDtypeStruct((B, S, HG_WIDTH), F32),
                   jax.ShapeDtypeStruct((B, S, u_rows), BF16),
                   jax.ShapeDtypeStruct((B, HG_HEADS, S // CHUNK, HG_DIM, HG_DIM), F32)],
        scratch_shapes=[pltpu.VMEM((HG_DIM, HG_DIM), F32)],
        compiler_params=_params(("parallel", "parallel", "arbitrary")),
    )(proj_h, proj_h, proj_h, proj_h, lb_param, g_head, ts, _level_masks())


def _hgrn_bwd(proj_h, o_h, du, states, lb_param, g_head, layer, name):
    B, S, _ = proj_h.shape
    sb = _pick(S, (512, 256, 128, 64))
    nc = sb // CHUNK
    ns = S // sb
    ts, tst = _cum_matrices()

    def body(q_ref, f_ref, i_ref, z_ref, o_ref, du_ref, sts_ref, lbp_ref, gh_ref, ts_ref, tst_ref, m_ref,
             dq_ref, df_ref, di_ref, dz_ref, dlb_ref, dgh_ref, dst):
        h_id, b_id, s_id = pl.program_id(0), pl.program_id(1), pl.program_id(2)

        @pl.when(s_id == 0)
        def _():
            dst[...] = jnp.zeros_like(dst)

        @pl.when((b_id == 0) & (s_id == 0))
        def _():
            dlb_ref[...] = jnp.zeros_like(dlb_ref)

        @pl.when((h_id == 0) & (b_id == 0) & (s_id == 0))
        def _():
            dgh_ref[...] = jnp.zeros_like(dgh_ref)

        lt = _lb_terms(lbp_ref[...], layer)
        gh = gh_ref[...]
        tsv = ts_ref[...]
        tstv = tst_ref[...]
        work = []
        dgh = jnp.zeros((1, HG_DIM), F32)
        for ci in range(nc):
            rows = slice(ci * CHUNK, (ci + 1) * CHUNK)
            x = f_ref[rows, :]
            logf, k, u1, u2 = _gate_fwd(x, lt)
            q = q_ref[rows, :]
            o = o_ref[rows, :]
            z = z_ref[rows, :]
            dub = du_ref[rows, :]
            r = lax.rsqrt(jnp.mean(o * o, axis=-1, keepdims=True) + NORM_EPS)
            n = o * r
            sg = _silu(z)
            dz_ref[rows, :] = (dub * (n * gh) * _silu_grad(z)).astype(BF16)
            dgh = dgh + jnp.sum(dub * sg * n, axis=0, keepdims=True)
            dn = dub * sg * gh
            do = r * (dn - n * jnp.mean(dn * n, axis=-1, keepdims=True))
            v = i_ref[rows, :]
            w = _hg_intra_bwd(_silu(q), k, v, logf, do, tsv, m_ref)
            w.update(x=x, logf=logf, u1=u1, u2=u2, q=q, v=v, do=do)
            work.append(w)
        dgh_ref[...] += dgh
        cur = dst[...]
        ends = [None] * nc
        for ci in reversed(range(nc)):
            ends[ci] = cur
            cur = cur * work[ci]["dec"] + work[ci]["qd"]
        dst[...] = cur
        dlb = jnp.zeros((1, HG_DIM), F32)
        for ci in range(nc):
            rows = slice(ci * CHUNK, (ci + 1) * CHUNK)
            w = work[ci]
            dq, dk, dv, dg = _hg_state_bwd(w, w["v"], w["do"], sts_ref[ci], ends[ci], tstv)
            di_ref[rows, :] = dv.astype(BF16)
            dq_ref[rows, :] = (dq * _silu_grad(w["q"])).astype(BF16)
            logf = w["logf"]
            dlogf = dg - jnp.exp(logf) * dk
            w1 = jnp.exp(w["u1"] - logf)
            w2 = jnp.exp(w["u2"] - logf)
            df_ref[rows, :] = (dlogf * w2 * (1.0 / (1.0 + jnp.exp(w["x"])))).astype(BF16)
            dlb = dlb + jnp.sum(dlogf * (w1 * lt["dlb1"] - w2 * lt["dlb2"]), axis=0, keepdims=True)
        dlb_ref[...] += dlb

    col = lambda base: pl.BlockSpec((None, sb, HG_DIM), lambda h, b, s: (b, ns - 1 - s, base + h))
    out_col = pl.BlockSpec((None, sb, HG_DIM), lambda h, b, s: (b, ns - 1 - s, h))
    dt = jax.ShapeDtypeStruct((B, S, HG_WIDTH), BF16)
    return pl.pallas_call(
        body, name=name,
        grid=(HG_HEADS, B, ns),
        in_specs=[col(0), col(HG_HEADS), col(2 * HG_HEADS), col(3 * HG_HEADS), col(0), col(0),
                  pl.BlockSpec((None, None, nc, HG_DIM, HG_DIM), lambda h, b, s: (b, h, ns - 1 - s, 0, 0)),
                  pl.BlockSpec((DEPTH, HG_DIM), lambda h, b, s: (0, h)),
                  pl.BlockSpec((1, HG_DIM), lambda h, b, s: (0, 0)),
                  pl.BlockSpec((N_CUM_F, CHUNK), lambda h, b, s: (0, 0)),
                  pl.BlockSpec((CHUNK, N_CUM), lambda h, b, s: (0, 0)),
                  pl.BlockSpec((len(_LEVELS), CHUNK, CHUNK), lambda h, b, s: (0, 0, 0))],
        out_specs=[out_col, out_col, out_col, out_col,
                   pl.BlockSpec((1, HG_DIM), lambda h, b, s: (0, h)),
                   pl.BlockSpec((1, HG_DIM), lambda h, b, s: (0, 0))],
        out_shape=[dt, dt, dt, dt, jax.ShapeDtypeStruct((1, HG_WIDTH), F32), jax.ShapeDtypeStruct((1, HG_DIM), F32)],
        scratch_shapes=[pltpu.VMEM((HG_DIM, HG_DIM), F32)],
        compiler_params=_params(("arbitrary", "arbitrary", "arbitrary")),
    )(proj_h, proj_h, proj_h, proj_h, o_h, du, states, lb_param, g_head, ts, tst, _level_masks())


def _rope_tables(S):
    half = ATT_DIM // 2
    inv_freq = ROPE_THETA ** (-jnp.arange(half, dtype=F32) / half)
    ang = jnp.arange(S).astype(F32)[:, None] * inv_freq[None, :]
    cos = jnp.cos(ang)
    sin = jnp.sin(ang)
    cos = jnp.concatenate([cos, cos, cos, cos], axis=1)
    sin = jnp.concatenate([-sin, sin, -sin, sin], axis=1)
    return cos, sin


def _attn_common():
    lane = lax.broadcasted_iota(jnp.int32, (1, 2 * ATT_DIM), 1)
    first_half = (lane % ATT_DIM) < (ATT_DIM // 2)
    left = lane < ATT_DIM

    def swap(x):
        return jnp.where(first_half, pltpu.roll(x, 128 - ATT_DIM // 2, 1), pltpu.roll(x, ATT_DIM // 2, 1))

    def rope(x, cos, sin):
        return x * cos + swap(x) * sin

    def rope_bwd(dy, cos, sin):
        return dy * cos + swap(dy * sin)

    def dup(x):
        xs = pltpu.roll(x, ATT_DIM, 1)
        return [jnp.where(left, x, xs), jnp.where(left, xs, x)]

    return left, rope, rope_bwd, dup


def _attn_mask(i):
    r = lax.broadcasted_iota(jnp.int32, (ATT_BLOCK, 2 * ATT_BLOCK), 0)
    c = lax.broadcasted_iota(jnp.int32, (ATT_BLOCK, 2 * ATT_BLOCK), 1)
    return (c > r) & (c <= r + ATT_BLOCK) & ((c >= ATT_BLOCK) | (i > 0))


def _attn_probs(q128, kh, sink, mask):
    s = _dot(q128, kh, NT) * ATT_SCALE
    s = jnp.where(mask, s, NEG_INF)
    m = jnp.maximum(jnp.max(s, axis=-1, keepdims=True), sink)
    p = jnp.exp(s - m)
    es = jnp.exp(sink - m)
    inv = 1.0 / (jnp.sum(p, axis=-1, keepdims=True) + es)
    return p * inv, es * inv


_Z0 = (2 * ATT_WIDTH + 2 * KV_WIDTH - ATT_WIDTH) // 256


def _attn_fwd(proj_a, u, sinks_l, cos, sin, name):
    B, S, _ = proj_a.shape
    nb = S // ATT_BLOCK

    def body(q_ref, kvc_ref, kvp_ref, z0, z1, z2, z3, cos_ref, sin_ref, cosp_ref, sinp_ref, sinks_ref, u_in, u_ref):
        del u_in
        i = pl.program_id(1)
        left, rope, _, dup = _attn_common()
        cos_c, sin_c = cos_ref[...], sin_ref[...]
        kvc = kvc_ref[...]
        kvp = kvp_ref[...]
        kw = jnp.concatenate([rope(kvp[:, :KV_WIDTH], cosp_ref[...], sinp_ref[...]),
                              rope(kvc[:, :KV_WIDTH], cos_c, sin_c)], axis=0)
        vw = jnp.concatenate([kvp[:, KV_WIDTH:], kvc[:, KV_WIDTH:]], axis=0)
        kd, vd = dup(kw), dup(vw)
        mask = _attn_mask(i)
        zs = (z0, z1, z2, z3)
        for pair in range(ATT_HEADS // 2):
            kvh = pair // 4
            cols = slice(128 * pair, 128 * (pair + 1))
            q128 = rope(q_ref[:, cols], cos_c, sin_c)
            out = jnp.zeros((ATT_BLOCK, 128), F32)
            for hh in range(2):
                lm = left if hh == 0 else jnp.logical_not(left)
                p, _ = _attn_probs(q128, jnp.where(lm, kd[kvh], 0.0), sinks_ref[2 * pair + hh], mask)
                out = out + _dot(p, jnp.where(lm, vd[kvh], 0.0), NN)
            z = zs[pair // 2][:, 128 * (pair % 2):128 * (pair % 2 + 1)]
            u_ref[:, cols] = (out * _silu(z)).astype(BF16)

    rowblk = lambda w, cb: pl.BlockSpec((None, ATT_BLOCK, w), lambda b, i: (b, i, cb))
    tab = pl.BlockSpec((ATT_BLOCK, 128), lambda b, i: (i, 0))
    tabp = pl.BlockSpec((ATT_BLOCK, 128), lambda b, i: (jnp.maximum(i - 1, 0), 0))
    return pl.pallas_call(
        body, name=name,
        grid=(B, nb),
        in_specs=[rowblk(ATT_WIDTH, 0), rowblk(256, 4),
                  pl.BlockSpec((None, ATT_BLOCK, 256), lambda b, i: (b, jnp.maximum(i - 1, 0), 4)),
                  rowblk(256, _Z0), rowblk(256, _Z0 + 1), rowblk(256, _Z0 + 2), rowblk(256, _Z0 + 3),
                  tab, tab, tabp, tabp,
                  pl.BlockSpec(memory_space=pltpu.SMEM),
                  pl.BlockSpec(memory_space=pl.ANY)],
        out_specs=pl.BlockSpec((None, ATT_BLOCK, ATT_WIDTH), lambda b, i: (b, i, 1)),
        out_shape=jax.ShapeDtypeStruct(u.shape, BF16),
        input_output_aliases={12: 0},
        compiler_params=_params(("parallel", "parallel")),
    )(proj_a, proj_a, proj_a, proj_a, proj_a, proj_a, proj_a, cos, sin, cos, sin, sinks_l, u)


def _attn_bwd(proj_a, du, sinks_l, cos, sin, name):
    B, S, _ = proj_a.shape
    nb = S // ATT_BLOCK

    def body(q_ref, kvc_ref, kvp_ref, z0, z1, z2, z3, du_ref, cos_ref, sin_ref, cosp_ref, sinp_ref, sinks_ref,
             dq_ref, dkv_ref, dz_ref, dsk_ref, carry, sk_acc):
        b_id, i = pl.program_id(0), pl.program_id(1)

        @pl.when((b_id == 0) & (i == 0))
        def _():
            sk_acc[...] = jnp.zeros_like(sk_acc)

        @pl.when(i == 0)
        def _():
            carry[...] = jnp.zeros_like(carry)

        @pl.when(i < nb)
        def _():
            left, rope, rope_bwd, dup = _attn_common()
            cos_c, sin_c = cos_ref[...], sin_ref[...]
            cos_p, sin_p = cosp_ref[...], sinp_ref[...]
            kvc = kvc_ref[...]
            kvp = kvp_ref[...]
            kw = jnp.concatenate([rope(kvp[:, :KV_WIDTH], cos_p, sin_p), rope(kvc[:, :KV_WIDTH], cos_c, sin_c)], axis=0)
            vw = jnp.concatenate([kvp[:, KV_WIDTH:], kvc[:, KV_WIDTH:]], axis=0)
            kd, vd = dup(kw), dup(vw)
            mask = _attn_mask(i)
            zs = (z0, z1, z2, z3)
            lane = lax.broadcasted_iota(jnp.int32, (1, 128), 1)
            dkd = [jnp.zeros((2 * ATT_BLOCK, 128), F32) for _ in range(2)]
            dvd = [jnp.zeros((2 * ATT_BLOCK, 128), F32) for _ in range(2)]
            sk = jnp.zeros((ATT_BLOCK, 128), F32)
            for pair in range(ATT_HEADS // 2):
                kvh = pair // 4
                cols = slice(128 * pair, 128 * (pair + 1))
                q128 = rope(q_ref[:, cols], cos_c, sin_c)
                lms = (left, jnp.logical_not(left))
                probs = []
                out = jnp.zeros((ATT_BLOCK, 128), F32)
                for hh in range(2):
                    p, ps = _attn_probs(q128, jnp.where(lms[hh], kd[kvh], 0.0), sinks_ref[2 * pair + hh], mask)
                    probs.append((p, ps))
                    out = out + _dot(p, jnp.where(lms[hh], vd[kvh], 0.0), NN)
                z = zs[pair // 2][:, 128 * (pair % 2):128 * (pair % 2 + 1)]
                du128 = du_ref[:, cols]
                dz_ref[:, cols] = (du128 * out * _silu_grad(z)).astype(BF16)
                do128 = du128 * _silu(z)
                dq128 = jnp.zeros((ATT_BLOCK, 128), F32)
                for hh in range(2):
                    p, ps = probs[hh]
                    kh = jnp.where(lms[hh], kd[kvh], 0.0)
                    vh = jnp.where(lms[hh], vd[kvh], 0.0)
                    dp = _dot(do128, vh, NT)
                    delta = jnp.sum(p * dp, axis=-1, keepdims=True)
                    ds = p * (dp - delta) * ATT_SCALE
                    sk = sk + jnp.where(lane == 2 * pair + hh, -ps * delta, 0.0)
                    dq128 = dq128 + _dot(ds, kh, NN)
                    dkd[kvh] = dkd[kvh] + jnp.where(lms[hh], _dot(ds, q128, TN), 0.0)
                    dvd[kvh] = dvd[kvh] + jnp.where(lms[hh], _dot(p, do128, TN), 0.0)
                dq_ref[:, cols] = rope_bwd(dq128, cos_c, sin_c).astype(BF16)
            sk_acc[...] += sk
            fold = lambda pr: jnp.where(left, pr[0] + pltpu.roll(pr[0], ATT_DIM, 1), pr[1] + pltpu.roll(pr[1], ATT_DIM, 1))
            dkw = fold(dkd)
            dvw = fold(dvd)
            prev = jnp.concatenate([rope_bwd(dkw[:ATT_BLOCK], cos_p, sin_p), dvw[:ATT_BLOCK]], axis=1)
            cur = jnp.concatenate([rope_bwd(dkw[ATT_BLOCK:], cos_c, sin_c), dvw[ATT_BLOCK:]], axis=1)
            dkv_ref[...] = (carry[...] + prev).astype(BF16)
            carry[...] = cur

        @pl.when(i == nb)
        def _():
            dkv_ref[...] = carry[...].astype(BF16)

        @pl.when((b_id == B - 1) & (i == nb))
        def _():
            dsk_ref[...] = jnp.sum(sk_acc[...], axis=0, keepdims=True)

    cl = lambda i: jnp.minimum(i, nb - 1)
    pv = lambda i: jnp.maximum(jnp.minimum(i, nb - 1) - 1, 0)
    rowblk = lambda w, cb: pl.BlockSpec((None, ATT_BLOCK, w), lambda b, i: (b, cl(i), cb))
    tab = pl.BlockSpec((ATT_BLOCK, 128), lambda b, i: (cl(i), 0))
    tabp = pl.BlockSpec((ATT_BLOCK, 128), lambda b, i: (pv(i), 0))
    return pl.pallas_call(
        body, name=name,
        grid=(B, nb + 1),
        in_specs=[rowblk(ATT_WIDTH, 0), rowblk(256, 4),
                  pl.BlockSpec((None, ATT_BLOCK, 256), lambda b, i: (b, pv(i), 4)),
                  rowblk(256, _Z0), rowblk(256, _Z0 + 1), rowblk(256, _Z0 + 2), rowblk(256, _Z0 + 3),
                  rowblk(ATT_WIDTH, 1),
                  tab, tab, tabp, tabp,
                  pl.BlockSpec(memory_space=pltpu.SMEM)],
        out_specs=[rowblk(ATT_WIDTH, 0),
                   pl.BlockSpec((None, ATT_BLOCK, 256), lambda b, i: (b, jnp.maximum(i - 1, 0), 0)),
                   rowblk(ATT_WIDTH, 0),
                   pl.BlockSpec((1, 128), lambda b, i: (0, 0))],
        out_shape=[jax.ShapeDtypeStruct((B, S, ATT_WIDTH), BF16), jax.ShapeDtypeStruct((B, S, 256), BF16),
                   jax.ShapeDtypeStruct((B, S, ATT_WIDTH), BF16), jax.ShapeDtypeStruct((1, 128), F32)],
        scratch_shapes=[pltpu.VMEM((ATT_BLOCK, 256), F32), pltpu.VMEM((ATT_BLOCK, 128), F32)],
        compiler_params=_params(("arbitrary", "arbitrary")),
    )(proj_a, proj_a, proj_a, proj_a, proj_a, proj_a, proj_a, du, cos, sin, cos, sin, sinks_l)


def _outproj_fwd(u2, w_out, x2, g_post, target2, name):
    T, D = x2.shape
    tm = _pick(T, (512, 256, 128))
    last = target2 is not None

    def body(u_ref, w_ref, x_ref, g_ref, *rest):
        y = lax.dot_general(u_ref[...], w_ref[...], (NN, ((), ())), preferred_element_type=F32)
        r = lax.rsqrt(jnp.mean(y * y, axis=-1, keepdims=True) + NORM_EPS)
        xn = x_ref[...] + (y * r) * g_ref[...]
        if last:
            t_ref, y_ref, dx_ref, loss_ref = rest
            err = xn - t_ref[...]
            dx_ref[...] = err * (1.0 / D)
            sq = err * err
            acc = sq[:, 0:128]
            for kk in range(1, D // 128):
                acc = acc + sq[:, 128 * kk:128 * (kk + 1)]
            part = jnp.sum(acc.reshape(tm // 8, 8, 128), axis=0) * (0.5 / D)

            @pl.when(pl.program_id(0) == 0)
            def _():
                loss_ref[...] = jnp.zeros_like(loss_ref)

            loss_ref[...] += part
        else:
            y_ref, xn_ref = rest
            xn_ref[...] = xn
        y_ref[...] = y

    row = pl.BlockSpec((tm, D), lambda i: (i, 0))
    in_specs = [pl.BlockSpec((tm, MIX_WIDTH), lambda i: (i, 0)),
                pl.BlockSpec((MIX_WIDTH, D), lambda i: (0, 0)), row,
                pl.BlockSpec((1, D), lambda i: (0, 0))]
    args = [u2, w_out, x2, g_post]
    out_specs = [row, row]
    out_shape = [jax.ShapeDtypeStruct((T, D), F32), jax.ShapeDtypeStruct((T, D), F32)]
    if last:
        in_specs.append(row)
        args.append(target2)
        out_specs.append(pl.BlockSpec((8, 128), lambda i: (0, 0)))
        out_shape.append(jax.ShapeDtypeStruct((8, 128), F32))
    return pl.pallas_call(
        body, name=name, grid=(T // tm,), in_specs=in_specs, out_specs=out_specs, out_shape=out_shape,
        compiler_params=_params(("arbitrary",)),
    )(*args)


def _postnorm_bwd(dxn2, y2, g_post, name):
    T, D = y2.shape
    tm = _pick(T, (512, 256, 128))
    nt = T // tm

    def body(dx_ref, y_ref, g_ref, dy_ref, dg_ref, acc):
        i = pl.program_id(0)

        @pl.when(i == 0)
        def _():
            acc[...] = jnp.zeros_like(acc)

        y = y_ref[...]
        dxn = dx_ref[...]
        r = lax.rsqrt(jnp.mean(y * y, axis=-1, keepdims=True) + NORM_EPS)
        n = y * r
        dn = dxn * g_ref[...]
        dy_ref[...] = (r * (dn - n * jnp.mean(dn * n, axis=-1, keepdims=True))).astype(BF16)
        acc[...] += jnp.sum((dxn * n).reshape(tm // 8, 8, D), axis=0)

        @pl.when(i == nt - 1)
        def _():
            dg_ref[...] = jnp.sum(acc[...], axis=0, keepdims=True)

    row = pl.BlockSpec((tm, D), lambda i: (i, 0))
    vec = pl.BlockSpec((1, D), lambda i: (0, 0))
    return pl.pallas_call(
        body, name=name, grid=(nt,), in_specs=[row, row, vec], out_specs=[row, vec],
        out_shape=[jax.ShapeDtypeStruct((T, D), BF16), jax.ShapeDtypeStruct((1, D), F32)],
        scratch_shapes=[pltpu.VMEM((8, D), F32)],
        compiler_params=_params(("arbitrary",)),
    )(dxn2, y2, g_post)


def _inproj_bwd(dproj2, w_t, x2, dxn2, g_pre, name):
    T, D = x2.shape
    K = dproj2.shape[1]
    tm = _pick(T, (256, 128))
    nt = T // tm

    def body(dp_ref, w_ref, x_ref, dxn_ref, g_ref, dx_ref, dg_ref, acc):
        i = pl.program_id(0)

        @pl.when(i == 0)
        def _():
            acc[...] = jnp.zeros_like(acc)

        dh = lax.dot_general(dp_ref[...], w_ref[...], (NN, ((), ())), preferred_element_type=F32)
        x = x_ref[...]
        r = lax.rsqrt(jnp.mean(x * x, axis=-1, keepdims=True) + NORM_EPS)
        n = x * r
        dn = dh * g_ref[...]
        dx_ref[...] = dxn_ref[...] + r * (dn - n * jnp.mean(dn * n, axis=-1, keepdims=True))
        acc[...] += jnp.sum((dh * n).reshape(tm // 8, 8, D), axis=0)

        @pl.when(i == nt - 1)
        def _():
            dg_ref[...] = jnp.sum(acc[...], axis=0, keepdims=True)

    row = pl.BlockSpec((tm, D), lambda i: (i, 0))
    vec = pl.BlockSpec((1, D), lambda i: (0, 0))
    return pl.pallas_call(
        body, name=name, grid=(nt,),
        in_specs=[pl.BlockSpec((tm, K), lambda i: (i, 0)),
                  pl.BlockSpec((K, D), lambda i: (0, 0), pipeline_mode=pl.Buffered(1)), row, row, vec],
        out_specs=[row, vec],
        out_shape=[jax.ShapeDtypeStruct((T, D), F32), jax.ShapeDtypeStruct((1, D), F32)],
        scratch_shapes=[pltpu.VMEM((8, D), F32)],
        compiler_params=_params(("arbitrary",)),
    )(dproj2, w_t, x2, dxn2, g_pre)


def _local_step(x, target, w_in_t, w_out, g_pre, g_post, lb_param, g_head, sinks):
    B, S, D = x.shape
    T = B * S
    cos, sin = _rope_tables(S)
    saved = []
    xs = x
    loss_part = None
    dxn = None
    for l in range(DEPTH):
        x2 = xs.reshape(T, D)
        proj_h, proj_a, h = _inproj(x2, g_pre[l:l + 1], w_in_t[l], f"inproj{l}")
        proj_h = proj_h.reshape(B, S, N_H)
        proj_a = proj_a.reshape(B, S, N_A)
        o_h, u, states = _hgrn_fwd(proj_h, MIX_WIDTH, lb_param, g_head[l:l + 1], l, f"hgrn_fwd{l}")
        u = _attn_fwd(proj_a, u, sinks[l], cos, sin, f"attn_fwd{l}")
        u2 = u.reshape(T, MIX_WIDTH)
        if l < DEPTH - 1:
            y, xn = _outproj_fwd(u2, w_out[l], x2, g_post[l:l + 1], None, f"outproj{l}")
            xn = xn.reshape(B, S, D)
        else:
            y, dxn, loss_part = _outproj_fwd(u2, w_out[l], x2, g_post[l:l + 1], target.reshape(T, D), f"outproj{l}")
            xn = None
        saved.append((x2, h, proj_h, proj_a, o_h, u2, states, y))
        xs = xn

    dw_in, dw_out, dg_pre, dg_post, dlb, dg_head, dsinks = [], [], [], [], [], [], []
    for l in reversed(range(DEPTH)):
        x2, h, proj_h, proj_a, o_h, u2, states, y = saved[l]
        dy, dgp = _postnorm_bwd(dxn, y, g_post[l:l + 1], f"postnorm_bwd{l}")
        dw_out.append(_mm_tn(u2, dy, f"wgrad_out{l}"))
        du = _mm_nt(dy, w_out[l], f"dgrad_out{l}").reshape(B, S, MIX_WIDTH)
        dqh, dfh, dih, dzh, dlb_l, dgh = _hgrn_bwd(proj_h, o_h, du, states, lb_param, g_head[l:l + 1], l, f"hgrn_bwd{l}")
        dqa, dkv, dza, dsk = _attn_bwd(proj_a, du, sinks[l], cos, sin, f"attn_bwd{l}")
        dproj = jnp.concatenate([dqh, dfh, dih, dzh, dqa, dkv, dza], axis=-1).reshape(T, IN_WIDTH)
        dw_in.append(_mm_tn(dproj, h, f"wgrad_in{l}"))
        dxn, dgpre = _inproj_bwd(dproj, w_in_t[l], x2, dxn, g_pre[l:l + 1], f"inproj_bwd{l}")
        dg_pre.append(dgpre)
        dg_post.append(dgp)
        dlb.append(dlb_l)
        dg_head.append(dgh)
        dsinks.append(dsk)
    rev = lambda lst: jnp.concatenate(lst[::-1], axis=0)
    return (loss_part, dxn.reshape(B, S, D), jnp.stack(dw_in[::-1]), jnp.stack(dw_out[::-1]),
            rev(dg_pre), rev(dg_post), rev(dlb), rev(dg_head), rev(dsinks))


def _me_and_peers():
    x, y, c = lax.axis_index("x"), lax.axis_index("y"), lax.axis_index("c")
    me = 4 * x + 2 * y + c
    peers = []
    for k in range(1, N_DEV):
        px = 1 - x if k & 4 else x
        py = 1 - y if k & 2 else y
        pc = 1 - c if k & 1 else c
        peers.append(((px, py, pc), 4 * px + 2 * py + pc))
    return me, peers


def _gather_weights(a_loc, b_loc):
    L, ra, D = a_loc.shape
    rb = b_loc.shape[1]

    def body(a_ref, b_ref, ao_ref, bo_ref, send_sems, recv_sems, loc_sems):
        me, peers = _me_and_peers()
        parts = [(a_ref, ao_ref, ra), (b_ref, bo_ref, rb)]

        def rows(dst, l, r, dev):
            return dst.at[l, pl.ds(pl.multiple_of(dev * r, 16), r), :]

        local = []
        for l in range(L):
            for t, (src, dst, r) in enumerate(parts):
                cp = pltpu.make_async_copy(src.at[l], rows(dst, l, r, me), loc_sems.at[2 * l + t])
                cp.start()
                local.append(cp)
        sends = []
        for k, (pid, _) in enumerate(peers):
            for l in range(L):
                for t, (src, dst, r) in enumerate(parts):
                    cp = pltpu.make_async_remote_copy(
                        src_ref=src.at[l], dst_ref=rows(dst, l, r, me),
                        send_sem=send_sems.at[k, 2 * l + t], recv_sem=recv_sems.at[k, 2 * l + t],
                        device_id=pid, device_id_type=MESH)
                    cp.start()
                    sends.append(cp)
        for k, (pid, pnum) in enumerate(peers):
            for l in range(L):
                for t, (src, dst, r) in enumerate(parts):
                    pltpu.make_async_remote_copy(
                        src_ref=src.at[l], dst_ref=rows(dst, l, r, pnum),
                        send_sem=send_sems.at[k, 2 * l + t], recv_sem=recv_sems.at[k, 2 * l + t],
                        device_id=pid, device_id_type=MESH).wait_recv()
        for cp in sends:
            cp.wait_send()
        for cp in local:
            cp.wait()

    hbm = pl.BlockSpec(memory_space=pl.ANY)
    return pl.pallas_call(
        body, name="gather_weights",
        in_specs=[hbm, hbm], out_specs=[hbm, hbm],
        out_shape=[jax.ShapeDtypeStruct((L, N_DEV * ra, D), a_loc.dtype),
                   jax.ShapeDtypeStruct((L, N_DEV * rb, D), b_loc.dtype)],
        scratch_shapes=[pltpu.SemaphoreType.DMA((N_DEV - 1, 2 * L)), pltpu.SemaphoreType.DMA((N_DEV - 1, 2 * L)),
                        pltpu.SemaphoreType.DMA((2 * L,))],
        compiler_params=pltpu.CompilerParams(has_side_effects=True),
    )(a_loc, b_loc)


def _scatter_grads(ga, gb):
    L, ra8, D = ga.shape
    ra = ra8 // N_DEV
    rb = gb.shape[1] // N_DEV

    def body(a_ref, b_ref, ao_ref, bo_ref, send_sems, recv_sems, loc_sems):
        me, peers = _me_and_peers()
        parts = [(a_ref, ao_ref, ra), (b_ref, bo_ref, rb)]

        def rows(src, l, r, dev):
            return src.at[l, pl.ds(pl.multiple_of(dev * r, 16), r), :]

        local = []
        for l in range(L):
            for t, (src, dst, r) in enumerate(parts):
                cp = pltpu.make_async_copy(rows(src, l, r, me), dst.at[me, l], loc_sems.at[2 * l + t])
                cp.start()
                local.append(cp)
        sends = []
        for k, (pid, pnum) in enumerate(peers):
            for l in range(L):
                for t, (src, dst, r) in enumerate(parts):
                    cp = pltpu.make_async_remote_copy(
                        src_ref=rows(src, l, r, pnum), dst_ref=dst.at[me, l],
                        send_sem=send_sems.at[k, 2 * l + t], recv_sem=recv_sems.at[k, 2 * l + t],
                        device_id=pid, device_id_type=MESH)
                    cp.start()
                    sends.append(cp)
        for k, (pid, pnum) in enumerate(peers):
            for l in range(L):
                for t, (src, dst, r) in enumerate(parts):
                    pltpu.make_async_remote_copy(
                        src_ref=rows(src, l, r, pnum), dst_ref=dst.at[pnum, l],
                        send_sem=send_sems.at[k, 2 * l + t], recv_sem=recv_sems.at[k, 2 * l + t],
                        device_id=pid, device_id_type=MESH).wait_recv()
        for cp in sends:
            cp.wait_send()
        for cp in local:
            cp.wait()

    hbm = pl.BlockSpec(memory_space=pl.ANY)
    return pl.pallas_call(
        body, name="scatter_grads",
        in_specs=[hbm, hbm], out_specs=[hbm, hbm],
        out_shape=[jax.ShapeDtypeStruct((N_DEV, L, ra, D), ga.dtype),
                   jax.ShapeDtypeStruct((N_DEV, L, rb, D), gb.dtype)],
        scratch_shapes=[pltpu.SemaphoreType.DMA((N_DEV - 1, 2 * L)), pltpu.SemaphoreType.DMA((N_DEV - 1, 2 * L)),
                        pltpu.SemaphoreType.DMA((2 * L,))],
        compiler_params=pltpu.CompilerParams(has_side_effects=True),
    )(ga, gb)


def _sum_slots(r, name):
    _, R, D = r.shape
    tr = _pick(R, (400, 256, 200, 128, 64, 16))

    def body(r_ref, o_ref):
        acc = r_ref[0].astype(F32)
        for d in range(1, N_DEV):
            acc = acc + r_ref[d].astype(F32)
        o_ref[...] = acc

    return pl.pallas_call(
        body, name=name, grid=(R // tr,),
        in_specs=[pl.BlockSpec((N_DEV, tr, D), lambda i: (0, i, 0))],
        out_specs=pl.BlockSpec((tr, D), lambda i: (i, 0)),
        out_shape=jax.ShapeDtypeStruct((R, D), F32),
        compiler_params=_params(("parallel",)),
    )(r)


def _allreduce_small(vec):
    R, C = vec.shape

    def body(v_ref, o_ref, buf, send_sems, recv_sems):
        me, peers = _me_and_peers()
        buf[me] = v_ref[...]
        sends = []
        for k, (pid, _) in enumerate(peers):
            cp = pltpu.make_async_remote_copy(src_ref=v_ref, dst_ref=buf.at[me], send_sem=send_sems.at[k],
                                              recv_sem=recv_sems.at[k], device_id=pid, device_id_type=MESH)
            cp.start()
            sends.append(cp)
        for k, (pid, pnum) in enumerate(peers):
            pltpu.make_async_remote_copy(src_ref=v_ref, dst_ref=buf.at[pnum], send_sem=send_sems.at[k],
                                         recv_sem=recv_sems.at[k], device_id=pid, device_id_type=MESH).wait_recv()
        for cp in sends:
            cp.wait_send()
        acc = buf[0]
        for d in range(1, N_DEV):
            acc = acc + buf[d]
        o_ref[...] = acc

    vm = pl.BlockSpec(memory_space=pltpu.VMEM)
    return pl.pallas_call(
        body, name="allreduce_small",
        in_specs=[vm], out_specs=vm,
        out_shape=jax.ShapeDtypeStruct((R, C), F32),
        scratch_shapes=[pltpu.VMEM((N_DEV, R, C), F32), pltpu.SemaphoreType.DMA((N_DEV - 1,)),
                        pltpu.SemaphoreType.DMA((N_DEV - 1,))],
        compiler_params=pltpu.CompilerParams(has_side_effects=True),
    )(vec)


def _adamw(w, g, m, v, name):
    R, C = w.shape
    tr = _pick(R, (256, 128, 64, 32, 16, 8)) if R >= 8 else R
    c1 = 1.0 - ADAM_B1 ** ADAM_STEP
    c2 = 1.0 - ADAM_B2 ** ADAM_STEP

    def body(w_ref, g_ref, m_ref, v_ref, d_ref, mo_ref, vo_ref):
        gg = g_ref[...]
        mn = ADAM_B1 * m_ref[...] + (1.0 - ADAM_B1) * gg
        vn = ADAM_B2 * v_ref[...] + (1.0 - ADAM_B2) * (gg * gg)
        d_ref[...] = -ADAM_LR * ((mn / c1) / (jnp.sqrt(vn / c2) + ADAM_EPS) + ADAM_WD * w_ref[...])
        mo_ref[...] = mn
        vo_ref[...] = vn

    blk = pl.BlockSpec((tr, C), lambda i: (i, 0))
    sh = jax.ShapeDtypeStruct((R, C), F32)
    return pl.pallas_call(
        body, name=name, grid=(R // tr,), in_specs=[blk] * 4, out_specs=[blk] * 3, out_shape=[sh] * 3,
        compiler_params=_params(("parallel",)),
    )(w, g, m, v)


def _lb_param_grad(lb_param, dlb):
    L, C = lb_param.shape

    def body(p_ref, d_ref, o_ref):
        lbp = p_ref[...]
        d = d_ref[...]
        mx = jnp.max(lbp, axis=0, keepdims=True)
        e = jnp.exp(lbp - mx)
        p = e / jnp.sum(e, axis=0, keepdims=True)
        tot = jnp.sum(d, axis=0, keepdims=True)
        dps = []
        rest = tot
        for j in range(L):
            dps.append(rest - tot if j == 0 else rest)
            rest = rest - d[j:j + 1]
        dp = jnp.concatenate(dps, axis=0)
        o_ref[...] = p * (dp - jnp.sum(p * dp, axis=0, keepdims=True))

    vm = pl.BlockSpec(memory_space=pltpu.VMEM)
    return pl.pallas_call(body, name="lb_param_grad", in_specs=[vm, vm], out_specs=vm,
                          out_shape=jax.ShapeDtypeStruct((L, C), F32))(lb_param, dlb)


def _pack_small(loss_part, dg_pre, dg_post, dlb, dg_head, dsinks):
    pad8 = lambda a: jnp.pad(a.reshape(-1, 128), ((0, 8 - DEPTH), (0, 0)))
    rows = [dg_pre.reshape(-1, 128), dg_post.reshape(-1, 128), dlb.reshape(-1, 128), pad8(dg_head), pad8(dsinks),
            loss_part]
    return jnp.concatenate(rows, axis=0)


def _unpack_small(vec):
    n = DEPTH * D_MODEL // 128
    o = 0
    dg_pre = vec[o:o + n].reshape(DEPTH, D_MODEL); o += n
    dg_post = vec[o:o + n].reshape(DEPTH, D_MODEL); o += n
    dlb = vec[o:o + n].reshape(DEPTH, HG_WIDTH); o += n
    dg_head = vec[o:o + DEPTH]; o += 8
    dsinks = vec[o:o + DEPTH, :ATT_HEADS]; o += 8
    loss = jnp.sum(vec[o:o + 8])
    return loss, dg_pre, dg_post, dlb, dg_head, dsinks


def kernel(x, w_in, w_out, g_pre, g_post, lb_param, g_head, sinks, loss_target, m_w_in, m_w_out, m_g_pre, m_g_post, m_lb_param, m_g_head, m_sinks, v_w_in, v_w_out, v_g_pre, v_g_post, v_lb_param, v_g_head, v_sinks):
    L, D, nloc = w_in.shape
    w_in_t_loc = jnp.swapaxes(w_in, 1, 2).astype(BF16)
    w_in_t, w_out_full = _gather_weights(w_in_t_loc, w_out.astype(BF16))

    (loss_part, dx, dw_in_t, dw_out, dg_pre, dg_post, dlb, dg_head, dsinks) = _local_step(
        x, loss_target, w_in_t, w_out_full, g_pre, g_post, lb_param, g_head, sinks)

    ra, rb = _scatter_grads(dw_in_t, dw_out)
    gw_in_t = _sum_slots(ra.reshape(N_DEV, L * nloc, D), "sum_w_in").reshape(L, nloc, D)
    gw_out = _sum_slots(rb.reshape(N_DEV, L * w_out.shape[1], D), "sum_w_out").reshape(w_out.shape)
    gw_in = jnp.swapaxes(gw_in_t, 1, 2)

    small = _allreduce_small(_pack_small(loss_part, dg_pre, dg_post, dlb, dg_head, dsinks))
    loss, gg_pre, gg_post, gdlb, gg_head, gsinks = _unpack_small(small)
    glb = _lb_param_grad(lb_param, gdlb)

    grads = [gw_in, gw_out, gg_pre, gg_post, glb, gg_head, gsinks]
    ws = [w_in, w_out, g_pre, g_post, lb_param, g_head, sinks]
    ms = [m_w_in, m_w_out, m_g_pre, m_g_post, m_lb_param, m_g_head, m_sinks]
    vs = [v_w_in, v_w_out, v_g_pre, v_g_post, v_lb_param, v_g_head, v_sinks]
    names = ["w_in", "w_out", "g_pre", "g_post", "lb_param", "g_head", "sinks"]
    deltas, new_m, new_v = [], [], []
    for w, g, m, v, nm in zip(ws, grads, ms, vs, names):
        sh = w.shape
        two = lambda a: a.reshape(-1, sh[-1])
        d, mn, vn = _adamw(two(w), two(g), two(m), two(v), "adamw_" + nm)
        deltas.append(d.reshape(sh))
        new_m.append(mn.reshape(sh))
        new_v.append(vn.reshape(sh))
    return (loss, dx, *grads, *deltas, *new_m, *new_v)
```

```python
import functools
import math

import numpy as np
import jax
import jax.numpy as jnp
from jax import lax
from jax.experimental import pallas as pl
from jax.experimental.pallas import tpu as pltpu

F32 = jnp.float32
BF16 = jnp.bfloat16

D_MODEL = 1024
DEPTH = 2
HG_HEADS = 8
HG_DIM = 128
HG_WIDTH = HG_HEADS * HG_DIM
CHUNK = 64
ATT_HEADS = 16
ATT_DIM = 64
ATT_WIDTH = ATT_HEADS * ATT_DIM
KV_WIDTH = 128
ATT_BLOCK = 128
ATT_SCALE = 1.0 / math.sqrt(ATT_DIM)
ROPE_THETA = 10000.0
NORM_EPS = 1e-6
NEG_INF = -1e30
LB_FLOOR = 1e-20
N_H = 4 * HG_WIDTH
N_A = 2 * ATT_WIDTH + 2 * KV_WIDTH
IN_WIDTH = N_H + N_A
MIX_WIDTH = HG_WIDTH + ATT_WIDTH

ADAM_LR = 0.001
ADAM_B1 = 0.9
ADAM_B2 = 0.999
ADAM_EPS = 1e-08
ADAM_WD = 0.01
ADAM_STEP = 10

N_DEV = 8
MESH = pl.DeviceIdType.MESH
VMEM_LIMIT = 56 * 1024 * 1024

NN = ((1,), (0,))
NT = ((1,), (1,))
TN = ((0,), (0,))


def _dot(a, b, dims):
    return lax.dot_general(a.astype(BF16), b.astype(BF16), (dims, ((), ())), preferred_element_type=F32)


def _params(sem=None, **kw):
    return pltpu.CompilerParams(dimension_semantics=sem, vmem_limit_bytes=VMEM_LIMIT, **kw)


def _sigmoid(x):
    return 1.0 / (1.0 + jnp.exp(-x))


def _silu(x):
    return x * _sigmoid(x)


def _silu_grad(x):
    s = _sigmoid(x)
    return s * (1.0 + x * (1.0 - s))


def _pick(n, prefs):
    for p in prefs:
        if n % p == 0:
            return p
    return n


def _inproj(x2, g, w, name):
    T, D = x2.shape
    tm = _pick(T, (256, 128))
    nchunk = 1024

    def body(x_ref, g_ref, w_ref, oh_ref, oa_ref, h_ref):
        x = x_ref[...]
        r = lax.rsqrt(jnp.mean(x * x, axis=-1, keepdims=True) + NORM_EPS)
        h = ((x * r) * g_ref[...]).astype(BF16)
        h_ref[...] = h
        for j in range(0, N_H, nchunk):
            oh_ref[:, j:j + nchunk] = lax.dot_general(h, w_ref[j:j + nchunk, :], (NT, ((), ())),
                                                      preferred_element_type=F32)
        for j in range(0, N_A, N_A // 2):
            oa_ref[:, j:j + N_A // 2] = lax.dot_general(h, w_ref[N_H + j:N_H + j + N_A // 2, :], (NT, ((), ())),
                                                        preferred_element_type=F32)

    row = lambda w_: pl.BlockSpec((tm, w_), lambda i: (i, 0))
    return pl.pallas_call(
        body, name=name,
        grid=(T // tm,),
        in_specs=[row(D), pl.BlockSpec((1, D), lambda i: (0, 0)),
                  pl.BlockSpec((IN_WIDTH, D), lambda i: (0, 0), pipeline_mode=pl.Buffered(1))],
        out_specs=[row(N_H), row(N_A), row(D)],
        out_shape=[jax.ShapeDtypeStruct((T, N_H), F32), jax.ShapeDtypeStruct((T, N_A), F32),
                   jax.ShapeDtypeStruct((T, D), BF16)],
        compiler_params=_params(("parallel",)),
    )(x2, g, w)


def _mm_nt(a, b, name, out_dtype=F32):
    M, K = a.shape
    N = b.shape[0]
    tm = _pick(M, (512, 256, 128))

    def body(a_ref, b_ref, o_ref):
        o_ref[...] = lax.dot_general(a_ref[...], b_ref[...], (NT, ((), ())),
                                     preferred_element_type=F32).astype(out_dtype)

    return pl.pallas_call(
        body, name=name,
        grid=(M // tm,),
        in_specs=[pl.BlockSpec((tm, K), lambda i: (i, 0)),
                  pl.BlockSpec((N, K), lambda i: (0, 0), pipeline_mode=pl.Buffered(1))],
        out_specs=pl.BlockSpec((tm, N), lambda i: (i, 0)),
        out_shape=jax.ShapeDtypeStruct((M, N), out_dtype),
        compiler_params=_params(("parallel",)),
    )(a, b)


def _mm_tn(a, b, name, out_dtype=BF16):
    T, n = a.shape
    m = b.shape[1]
    tn = _pick(n, (256, 128))

    def body(a_ref, b_ref, o_ref):
        o_ref[...] = lax.dot_general(a_ref[...], b_ref[...], (TN, ((), ())),
                                     preferred_element_type=F32).astype(out_dtype)

    return pl.pallas_call(
        body, name=name,
        grid=(n // tn,),
        in_specs=[pl.BlockSpec((T, tn), lambda i: (0, i)),
                  pl.BlockSpec((T, m), lambda i: (0, 0), pipeline_mode=pl.Buffered(1))],
        out_specs=pl.BlockSpec((tn, m), lambda i: (i, 0)),
        out_shape=jax.ShapeDtypeStruct((n, m), out_dtype),
        compiler_params=_params(("parallel",)),
    )(a, b)


_LEVELS = (0, 1, 2, 4, 8, 16, 32)
_CUM_L = (2, 4, 8, 16, 32, 64)
_ALL_KINDS = tuple(("c", L) for L in _CUM_L) + tuple(("r", L) for L in _CUM_L)
_MXU_KINDS = (("c", 2), ("c", 4), ("c", CHUNK), ("r", 2), ("r", 4))
N_CUM = len(_ALL_KINDS) * CHUNK
N_CUM_F = len(_MXU_KINDS) * CHUNK


def _cum_matrices():
    t = np.arange(CHUNK)[:, None]
    r = np.arange(CHUNK)[None, :]

    def mat(kind):
        c, L = kind
        return ((r // L == t // L) & ((r <= t) if c == "c" else (r > t))).astype(np.float32)

    fwd = np.concatenate([mat(kd) for kd in _MXU_KINDS], axis=0)
    full = np.concatenate([mat(kd) for kd in _ALL_KINDS], axis=0)
    return jnp.asarray(fwd, BF16), jnp.asarray(full.T.copy(), BF16)


def _level_masks():
    t = np.arange(CHUNK)[:, None]
    s = np.arange(CHUNK)[None, :]
    ms = []
    for L in _LEVELS:
        if L == 0:
            ms.append(t == s)
        else:
            ms.append((t // (2 * L) == s // (2 * L)) & ((t // L) % 2 == 1) & ((s // L) % 2 == 0))
    return jnp.asarray(np.stack(ms).astype(np.float32))


def _split3(x):
    hi = x.astype(BF16)
    r1 = x - hi.astype(F32)
    mid = r1.astype(BF16)
    lo = (r1 - mid.astype(F32)).astype(BF16)
    return hi, mid, lo


def _cum3(ts, x):
    hi, mid, lo = _split3(x)
    d = lambda p: lax.dot_general(ts, p, (NN, ((), ())), preferred_element_type=F32)
    return d(hi) + d(mid) + d(lo)


def _lb_terms(lbp, layer):
    mx = jnp.max(lbp, axis=0, keepdims=True)
    e = jnp.exp(lbp - mx)
    p = e / jnp.sum(e, axis=0, keepdims=True)
    cum = p[0:1]
    for j in range(1, layer + 1):
        cum = cum + p[j:j + 1]
    lb = cum - p[0:1]
    lbf = jnp.maximum(lb, LB_FLOOR)
    return dict(lb=lb, a=jnp.log(lbf), c=jnp.log(1.0 - lb), one_m=1.0 - lb, kcorr=lb - lbf,
                dlb1=jnp.where(lb > LB_FLOOR, 1.0 / lbf, 0.0), dlb2=1.0 / (1.0 - lb))


def _gate_fwd(x, lt):
    ls = jnp.minimum(x, 0.0) - jnp.log(1.0 + jnp.exp(-jnp.abs(x)))
    u1 = lt["a"]
    u2 = lt["c"] + ls
    mx = jnp.maximum(u1, u2)
    logf = mx + jnp.log(1.0 + jnp.exp(-jnp.abs(u1 - u2)))
    k = lt["one_m"] * (1.0 / (1.0 + jnp.exp(x))) + lt["kcorr"]
    return logf, k, u1, u2


def _chunk_cums(ts, g):
    cs = _cum3(ts, g)
    out = {kind: cs[CHUNK * i:CHUNK * (i + 1)] for i, kind in enumerate(_MXU_KINDS)}
    b = out[("c", CHUNK)]
    last = [jnp.broadcast_to(b[8 * r + 7:8 * r + 8, :], (8, HG_DIM)) for r in range(CHUNK // 8)]
    zero = jnp.zeros((8, HG_DIM), F32)
    for L in (8, 16, 32):
        nb = L // 8
        before = [last[(r // nb) * nb - 1] if r >= nb else zero for r in range(CHUNK // 8)]
        end = [last[(r // nb) * nb + nb - 1] for r in range(CHUNK // 8)]
        out[("c", L)] = b - jnp.concatenate(before, axis=0)
        out[("r", L)] = jnp.concatenate(end, axis=0) - b
    out[("r", CHUNK)] = jnp.broadcast_to(b[CHUNK - 1:CHUNK, :], (CHUNK, HG_DIM)) - b
    return out


def _level_factors(cums, g, L):
    if L == 0:
        return None, None
    if L == 1:
        return jnp.exp(g), None
    return jnp.exp(cums[("c", L)]), jnp.exp(cums[("r", L)])


def _mul(a, e):
    return a if e is None else a * e


def _hg_intra_fwd(qf, k, v, g, ts, m_ref):
    cums = _chunk_cums(ts, g)
    amat = jnp.zeros((CHUNK, CHUNK), F32)
    for li, L in enumerate(_LEVELS):
        eq, ek = _level_factors(cums, g, L)
        amat = amat + _dot(_mul(qf, eq), _mul(k, ek), NT) * m_ref[li]
    b = cums[("c", CHUNK)]
    kv = _dot(v, k * jnp.exp(cums[("r", CHUNK)]), TN)
    return _dot(amat, v, NN), qf * jnp.exp(b), jnp.exp(b[CHUNK - 1:CHUNK, :]), kv


def _hg_intra_bwd(qf, k, v, g, do, ts, m_ref):
    cums = _chunk_cums(ts, g)
    dcs = {}
    da = _dot(do, v, NT)
    dq = jnp.zeros((CHUNK, HG_DIM), F32)
    dk = jnp.zeros((CHUNK, HG_DIM), F32)
    dg = jnp.zeros((CHUNK, HG_DIM), F32)
    amat = jnp.zeros((CHUNK, CHUNK), F32)
    for li, L in enumerate(_LEVELS):
        eq, ek = _level_factors(cums, g, L)
        ql = _mul(qf, eq)
        kl = _mul(k, ek)
        m = m_ref[li]
        amat = amat + _dot(ql, kl, NT) * m
        dal = da * m
        dql = _dot(dal, kl, NN)
        dkl = _dot(dal, ql, TN)
        dq = dq + _mul(dql, eq)
        dk = dk + _mul(dkl, ek)
        if L == 1:
            dg = dg + dql * ql
        elif L > 1:
            dcs[("c", L)] = dql * ql
            dcs[("r", L)] = dkl * kl
    b = cums[("c", CHUNK)]
    e64 = jnp.exp(b)
    er64 = jnp.exp(cums[("r", CHUNK)])
    qb = qf * e64
    return dict(dq=dq, dk=dk, dv=_dot(amat, do, TN), dg=dg, dcs=dcs, e64=e64, er64=er64, qb=qb, kst=k * er64,
                dec=jnp.exp(b[CHUNK - 1:CHUNK, :]), qd=_dot(do, qb, TN))


def _hg_state_bwd(w, v, do, st, dst, tst):
    dqb = _dot(do, st, NN)
    dkst = _dot(v, dst, NN)
    dq = w["dq"] + dqb * w["e64"]
    dk = w["dk"] + dkst * w["er64"]
    dv = w["dv"] + _dot(w["kst"], dst, NT)
    dtot = jnp.sum(dst * st, axis=0, keepdims=True) * w["dec"]
    trow = lax.broadcasted_iota(jnp.int32, (CHUNK, 1), 0)
    dcs = dict(w["dcs"])
    dcs[("c", CHUNK)] = dqb * w["qb"] + jnp.where(trow == CHUNK - 1, dtot, 0.0)
    dcs[("r", CHUNK)] = dkst * w["kst"]
    stack = jnp.concatenate([dcs[kind] for kind in _ALL_KINDS], axis=0)
    return dq, dk, dv, w["dg"] + _cum3(tst, stack)


def _hgrn_fwd(proj_h, u_rows, lb_param, g_head, layer, name, phase=None):
    B, S, _ = proj_h.shape
    sb = _pick(S, (512, 256, 128, 64))
    nc = sb // CHUNK
    ts, _ = _cum_matrices()

    def body(*refs):
        ins, outs, (st,), p_in, p_out, p_sems = _split_refs(refs, 8, 3, 1, phase)
        q_ref, f_ref, i_ref, z_ref, lbp_ref, gh_ref, ts_ref, m_ref = ins
        o_ref, u_ref, sts_ref = outs
        h_id, b_id, s_id = pl.program_id(0), pl.program_id(1), pl.program_id(2)
        _hosted_start(phase, p_in, p_out, p_sems, (h_id == 0) & (b_id == 0) & (s_id == 0))

        @pl.when(s_id == 0)
        def _():
            st[...] = jnp.zeros_like(st)

        lt = _lb_terms(lbp_ref[...], layer)
        tsv = ts_ref[...]
        gh = gh_ref[...]
        parts = []
        for ci in range(nc):
            rows = slice(ci * CHUNK, (ci + 1) * CHUNK)
            logf, k, _, _ = _gate_fwd(f_ref[rows, :], lt)
            parts.append(_hg_intra_fwd(_silu(q_ref[rows, :]), k, i_ref[rows, :], logf, tsv, m_ref))
        cur = st[...]
        starts = []
        for ci in range(nc):
            sts_ref[ci] = cur
            starts.append(cur)
            cur = cur * parts[ci][2] + parts[ci][3]
        st[...] = cur
        for ci in range(nc):
            rows = slice(ci * CHUNK, (ci + 1) * CHUNK)
            o = parts[ci][0] + _dot(parts[ci][1], starts[ci], NT)
            o_ref[rows, :] = o
            r = lax.rsqrt(jnp.mean(o * o, axis=-1, keepdims=True) + NORM_EPS)
            u_ref[rows, :] = (((o * r) * gh) * _silu(z_ref[rows, :])).astype(BF16)
        _hosted_finish(phase, p_in, p_out, p_sems, (h_id == HG_HEADS - 1) & (b_id == B - 1) & (s_id == S // sb - 1))

    col = lambda base: pl.BlockSpec((None, sb, HG_DIM), lambda h, b, s: (b, s, base + h))
    p_ispecs, p_ospecs, p_oshapes, p_alias, p_scratch, p_args = _host_phase(phase, 8, 3)
    res = pl.pallas_call(
        body, name=name,
        grid=(HG_HEADS, B, S // sb),
        in_specs=[col(0), col(HG_HEADS), col(2 * HG_HEADS), col(3 * HG_HEADS),
                  pl.BlockSpec((DEPTH, HG_DIM), lambda h, b, s: (0, h)),
                  pl.BlockSpec((1, HG_DIM), lambda h, b, s: (0, 0)),
                  pl.BlockSpec((N_CUM_F, CHUNK), lambda h, b, s: (0, 0)),
                  pl.BlockSpec((len(_LEVELS), CHUNK, CHUNK), lambda h, b, s: (0, 0, 0))] + p_ispecs,
        out_specs=[col(0), col(0),
                   pl.BlockSpec((None, None, nc, HG_DIM, HG_DIM), lambda h, b, s: (b, h, s, 0, 0))] + p_ospecs,
        out_shape=[jax.ShapeDtypeStruct((B, S, HG_WIDTH), F32),
                   jax.ShapeDtypeStruct((B, S, u_rows), BF16),
                   jax.ShapeDtypeStruct((B, HG_HEADS, S // CHUNK, HG_DIM, HG_DIM), F32)] + p_oshapes,
        input_output_aliases=p_alias,
        scratch_shapes=[pltpu.VMEM((HG_DIM, HG_DIM), F32)] + p_scratch,
        compiler_params=_params(("arbitrary", "arbitrary", "arbitrary")),
    )(proj_h, proj_h, proj_h, proj_h, lb_param, g_head, ts, _level_masks(), *p_args)
    return res[0], res[1], res[2], list(res[3:])


def _hgrn_bwd(proj_h, o_h, du, states, lb_param, g_head, layer, name, phase=None):
    B, S, _ = proj_h.shape
    sb = _pick(S, (512, 256, 128, 64))
    nc = sb // CHUNK
    ns = S // sb
    ts, tst = _cum_matrices()

    def body(*refs):
        ins, outs, (dst,), p_in, p_out, p_sems = _split_refs(refs, 12, 6, 1, phase)
        q_ref, f_ref, i_ref, z_ref, o_ref, du_ref, sts_ref, lbp_ref, gh_ref, ts_ref, tst_ref, m_ref = ins
        dq_ref, df_ref, di_ref, dz_ref, dlb_ref, dgh_ref = outs
        h_id, b_id, s_id = pl.program_id(0), pl.program_id(1), pl.program_id(2)
        _hosted_start(phase, p_in, p_out, p_sems, (h_id == 0) & (b_id == 0) & (s_id == 0))

        @pl.when(s_id == 0)
        def _():
            dst[...] = jnp.zeros_like(dst)

        @pl.when((b_id == 0) & (s_id == 0))
        def _():
            dlb_ref[...] = jnp.zeros_like(dlb_ref)

        @pl.when((h_id == 0) & (b_id == 0) & (s_id == 0))
        def _():
            dgh_ref[...] = jnp.zeros_like(dgh_ref)

        lt = _lb_terms(lbp_ref[...], layer)
        gh = gh_ref[...]
        tsv = ts_ref[...]
        tstv = tst_ref[...]
        work = []
        dgh = jnp.zeros((1, HG_DIM), F32)
        for ci in range(nc):
            rows = slice(ci * CHUNK, (ci + 1) * CHUNK)
            x = f_ref[rows, :]
            logf, k, u1, u2 = _gate_fwd(x, lt)
            q = q_ref[rows, :]
            o = o_ref[rows, :]
            z = z_ref[rows, :]
            dub = du_ref[rows, :]
            r = lax.rsqrt(jnp.mean(o * o, axis=-1, keepdims=True) + NORM_EPS)
            n = o * r
            sg = _silu(z)
            dz_ref[rows, :] = (dub * (n * gh) * _silu_grad(z)).astype(BF16)
            dgh = dgh + jnp.sum(dub * sg * n, axis=0, keepdims=True)
            dn = dub * sg * gh
            do = r * (dn - n * jnp.mean(dn * n, axis=-1, keepdims=True))
            v = i_ref[rows, :]
            w = _hg_intra_bwd(_silu(q), k, v, logf, do, tsv, m_ref)
            w.update(x=x, logf=logf, u1=u1, u2=u2, q=q, v=v, do=do)
            work.append(w)
        dgh_ref[...] += dgh
        cur = dst[...]
        ends = [None] * nc
        for ci in reversed(range(nc)):
            ends[ci] = cur
            cur = cur * work[ci]["dec"] + work[ci]["qd"]
        dst[...] = cur
        dlb = jnp.zeros((1, HG_DIM), F32)
        for ci in range(nc):
            rows = slice(ci * CHUNK, (ci + 1) * CHUNK)
            w = work[ci]
            dq, dk, dv, dg = _hg_state_bwd(w, w["v"], w["do"], sts_ref[ci], ends[ci], tstv)
            di_ref[rows, :] = dv.astype(BF16)
            dq_ref[rows, :] = (dq * _silu_grad(w["q"])).astype(BF16)
            logf = w["logf"]
            dlogf = dg - jnp.exp(logf) * dk
            w1 = jnp.exp(w["u1"] - logf)
            w2 = jnp.exp(w["u2"] - logf)
            df_ref[rows, :] = (dlogf * w2 * (1.0 / (1.0 + jnp.exp(w["x"])))).astype(BF16)
            dlb = dlb + jnp.sum(dlogf * (w1 * lt["dlb1"] - w2 * lt["dlb2"]), axis=0, keepdims=True)
        dlb_ref[...] += dlb
        _hosted_finish(phase, p_in, p_out, p_sems, (h_id == HG_HEADS - 1) & (b_id == B - 1) & (s_id == ns - 1))

    col = lambda base: pl.BlockSpec((None, sb, HG_DIM), lambda h, b, s: (b, ns - 1 - s, base + h))
    out_col = pl.BlockSpec((None, sb, HG_DIM), lambda h, b, s: (b, ns - 1 - s, h))
    dt = jax.ShapeDtypeStruct((B, S, HG_WIDTH), BF16)
    p_ispecs, p_ospecs, p_oshapes, p_alias, p_scratch, p_args = _host_phase(phase, 12, 6)
    res = pl.pallas_call(
        body, name=name,
        grid=(HG_HEADS, B, ns),
        in_specs=[col(0), col(HG_HEADS), col(2 * HG_HEADS), col(3 * HG_HEADS), col(0), col(0),
                  pl.BlockSpec((None, None, nc, HG_DIM, HG_DIM), lambda h, b, s: (b, h, ns - 1 - s, 0, 0)),
                  pl.BlockSpec((DEPTH, HG_DIM), lambda h, b, s: (0, h)),
                  pl.BlockSpec((1, HG_DIM), lambda h, b, s: (0, 0)),
                  pl.BlockSpec((N_CUM_F, CHUNK), lambda h, b, s: (0, 0)),
                  pl.BlockSpec((CHUNK, N_CUM), lambda h, b, s: (0, 0)),
                  pl.BlockSpec((len(_LEVELS), CHUNK, CHUNK), lambda h, b, s: (0, 0, 0))] + p_ispecs,
        out_specs=[out_col, out_col, out_col, out_col,
                   pl.BlockSpec((1, HG_DIM), lambda h, b, s: (0, h)),
                   pl.BlockSpec((1, HG_DIM), lambda h, b, s: (0, 0))] + p_ospecs,
        out_shape=[dt, dt, dt, dt, jax.ShapeDtypeStruct((1, HG_WIDTH), F32),
                   jax.ShapeDtypeStruct((1, HG_DIM), F32)] + p_oshapes,
        input_output_aliases=p_alias,
        scratch_shapes=[pltpu.VMEM((HG_DIM, HG_DIM), F32)] + p_scratch,
        compiler_params=_params(("arbitrary", "arbitrary", "arbitrary")),
    )(proj_h, proj_h, proj_h, proj_h, o_h, du, states, lb_param, g_head, ts, tst, _level_masks(), *p_args)
    return tuple(res[:6]) + (list(res[6:]),)


def _rope_tables(S):
    half = ATT_DIM // 2
    inv_freq = ROPE_THETA ** (-jnp.arange(half, dtype=F32) / half)
    ang = jnp.arange(S).astype(F32)[:, None] * inv_freq[None, :]
    cos = jnp.cos(ang)
    sin = jnp.sin(ang)
    cos = jnp.concatenate([cos, cos, cos, cos], axis=1)
    sin = jnp.concatenate([-sin, sin, -sin, sin], axis=1)
    return cos, sin


def _attn_common():
    lane = lax.broadcasted_iota(jnp.int32, (1, 2 * ATT_DIM), 1)
    first_half = (lane % ATT_DIM) < (ATT_DIM // 2)
    left = lane < ATT_DIM

    def swap(x):
        return jnp.where(first_half, pltpu.roll(x, 128 - ATT_DIM // 2, 1), pltpu.roll(x, ATT_DIM // 2, 1))

    def rope(x, cos, sin):
        return x * cos + swap(x) * sin

    def rope_bwd(dy, cos, sin):
        return dy * cos + swap(dy * sin)

    def dup(x):
        xs = pltpu.roll(x, ATT_DIM, 1)
        return [jnp.where(left, x, xs), jnp.where(left, xs, x)]

    return left, rope, rope_bwd, dup


def _attn_mask(i):
    r = lax.broadcasted_iota(jnp.int32, (ATT_BLOCK, 2 * ATT_BLOCK), 0)
    c = lax.broadcasted_iota(jnp.int32, (ATT_BLOCK, 2 * ATT_BLOCK), 1)
    return (c > r) & (c <= r + ATT_BLOCK) & ((c >= ATT_BLOCK) | (i > 0))


def _attn_probs(q128, kh, sink, mask):
    s = _dot(q128, kh, NT) * ATT_SCALE
    s = jnp.where(mask, s, NEG_INF)
    m = jnp.maximum(jnp.max(s, axis=-1, keepdims=True), sink)
    p = jnp.exp(s - m)
    es = jnp.exp(sink - m)
    inv = 1.0 / (jnp.sum(p, axis=-1, keepdims=True) + es)
    return p * inv, es * inv


_Z0 = (2 * ATT_WIDTH + 2 * KV_WIDTH - ATT_WIDTH) // 256


def _attn_fwd(proj_a, u, sinks_l, cos, sin, name, phase=None):
    B, S, _ = proj_a.shape
    nb = S // ATT_BLOCK

    def body(*refs):
        ins, (u_ref,), _, p_in, p_out, p_sems = _split_refs(refs, 13, 1, 0, phase)
        q_ref, kvc_ref, kvp_ref, z0, z1, z2, z3, cos_ref, sin_ref, cosp_ref, sinp_ref, sinks_ref, _ = ins
        i = pl.program_id(1)
        _hosted_start(phase, p_in, p_out, p_sems, (pl.program_id(0) == 0) & (i == 0))
        left, rope, _, dup = _attn_common()
        cos_c, sin_c = cos_ref[...], sin_ref[...]
        kvc = kvc_ref[...]
        kvp = kvp_ref[...]
        kw = jnp.concatenate([rope(kvp[:, :KV_WIDTH], cosp_ref[...], sinp_ref[...]),
                              rope(kvc[:, :KV_WIDTH], cos_c, sin_c)], axis=0)
        vw = jnp.concatenate([kvp[:, KV_WIDTH:], kvc[:, KV_WIDTH:]], axis=0)
        kd, vd = dup(kw), dup(vw)
        mask = _attn_mask(i)
        zs = (z0, z1, z2, z3)
        for pair in range(ATT_HEADS // 2):
            kvh = pair // 4
            cols = slice(128 * pair, 128 * (pair + 1))
            q128 = rope(q_ref[:, cols], cos_c, sin_c)
            out = jnp.zeros((ATT_BLOCK, 128), F32)
            for hh in range(2):
                lm = left if hh == 0 else jnp.logical_not(left)
                p, _ = _attn_probs(q128, jnp.where(lm, kd[kvh], 0.0), sinks_ref[2 * pair + hh], mask)
                out = out + _dot(p, jnp.where(lm, vd[kvh], 0.0), NN)
            z = zs[pair // 2][:, 128 * (pair % 2):128 * (pair % 2 + 1)]
            u_ref[:, cols] = (out * _silu(z)).astype(BF16)
        _hosted_finish(phase, p_in, p_out, p_sems, (pl.program_id(0) == B - 1) & (i == nb - 1))

    rowblk = lambda w, cb: pl.BlockSpec((None, ATT_BLOCK, w), lambda b, i: (b, i, cb))
    tab = pl.BlockSpec((ATT_BLOCK, 128), lambda b, i: (i, 0))
    tabp = pl.BlockSpec((ATT_BLOCK, 128), lambda b, i: (jnp.maximum(i - 1, 0), 0))
    p_ispecs, p_ospecs, p_oshapes, p_alias, p_scratch, p_args = _host_phase(phase, 13, 1)
    res = pl.pallas_call(
        body, name=name,
        grid=(B, nb),
        in_specs=[rowblk(ATT_WIDTH, 0), rowblk(256, 4),
                  pl.BlockSpec((None, ATT_BLOCK, 256), lambda b, i: (b, jnp.maximum(i - 1, 0), 4)),
                  rowblk(256, _Z0), rowblk(256, _Z0 + 1), rowblk(256, _Z0 + 2), rowblk(256, _Z0 + 3),
                  tab, tab, tabp, tabp,
                  pl.BlockSpec(memory_space=pltpu.SMEM),
                  pl.BlockSpec(memory_space=pl.ANY)] + p_ispecs,
        out_specs=[pl.BlockSpec((None, ATT_BLOCK, ATT_WIDTH), lambda b, i: (b, i, 1))] + p_ospecs,
        out_shape=[jax.ShapeDtypeStruct(u.shape, BF16)] + p_oshapes,
        input_output_aliases={12: 0, **p_alias},
        scratch_shapes=p_scratch,
        compiler_params=_params(("arbitrary", "arbitrary")),
    )(proj_a, proj_a, proj_a, proj_a, proj_a, proj_a, proj_a, cos, sin, cos, sin, sinks_l, u, *p_args)
    return res[0], list(res[1:])


def _attn_bwd(proj_a, du, sinks_l, cos, sin, name, phase=None):
    B, S, _ = proj_a.shape
    nb = S // ATT_BLOCK

    def body(*refs):
        ins, outs, (carry, sk_acc), p_in, p_out, p_sems = _split_refs(refs, 13, 4, 2, phase)
        q_ref, kvc_ref, kvp_ref, z0, z1, z2, z3, du_ref, cos_ref, sin_ref, cosp_ref, sinp_ref, sinks_ref = ins
        dq_ref, dkv_ref, dz_ref, dsk_ref = outs
        b_id, i = pl.program_id(0), pl.program_id(1)
        _hosted_start(phase, p_in, p_out, p_sems, (b_id == 0) & (i == 0))

        @pl.when((b_id == 0) & (i == 0))
        def _():
            sk_acc[...] = jnp.zeros_like(sk_acc)

        @pl.when(i == 0)
        def _():
            carry[...] = jnp.zeros_like(carry)

        @pl.when(i < nb)
        def _():
            left, rope, rope_bwd, dup = _attn_common()
            cos_c, sin_c = cos_ref[...], sin_ref[...]
            cos_p, sin_p = cosp_ref[...], sinp_ref[...]
            kvc = kvc_ref[...]
            kvp = kvp_ref[...]
            kw = jnp.concatenate([rope(kvp[:, :KV_WIDTH], cos_p, sin_p), rope(kvc[:, :KV_WIDTH], cos_c, sin_c)], axis=0)
            vw = jnp.concatenate([kvp[:, KV_WIDTH:], kvc[:, KV_WIDTH:]], axis=0)
            kd, vd = dup(kw), dup(vw)
            mask = _attn_mask(i)
            zs = (z0, z1, z2, z3)
            lane = lax.broadcasted_iota(jnp.int32, (1, 128), 1)
            dkd = [jnp.zeros((2 * ATT_BLOCK, 128), F32) for _ in range(2)]
            dvd = [jnp.zeros((2 * ATT_BLOCK, 128), F32) for _ in range(2)]
            sk = jnp.zeros((ATT_BLOCK, 128), F32)
            for pair in range(ATT_HEADS // 2):
                kvh = pair // 4
                cols = slice(128 * pair, 128 * (pair + 1))
                q128 = rope(q_ref[:, cols], cos_c, sin_c)
                lms = (left, jnp.logical_not(left))
                probs = []
                out = jnp.zeros((ATT_BLOCK, 128), F32)
                for hh in range(2):
                    p, ps = _attn_probs(q128, jnp.where(lms[hh], kd[kvh], 0.0), sinks_ref[2 * pair + hh], mask)
                    probs.append((p, ps))
                    out = out + _dot(p, jnp.where(lms[hh], vd[kvh], 0.0), NN)
                z = zs[pair // 2][:, 128 * (pair % 2):128 * (pair % 2 + 1)]
                du128 = du_ref[:, cols]
                dz_ref[:, cols] = (du128 * out * _silu_grad(z)).astype(BF16)
                do128 = du128 * _silu(z)
                dq128 = jnp.zeros((ATT_BLOCK, 128), F32)
                for hh in range(2):
                    p, ps = probs[hh]
                    kh = jnp.where(lms[hh], kd[kvh], 0.0)
                    vh = jnp.where(lms[hh], vd[kvh], 0.0)
                    dp = _dot(do128, vh, NT)
                    delta = jnp.sum(p * dp, axis=-1, keepdims=True)
                    ds = p * (dp - delta) * ATT_SCALE
                    sk = sk + jnp.where(lane == 2 * pair + hh, -ps * delta, 0.0)
                    dq128 = dq128 + _dot(ds, kh, NN)
                    dkd[kvh] = dkd[kvh] + jnp.where(lms[hh], _dot(ds, q128, TN), 0.0)
                    dvd[kvh] = dvd[kvh] + jnp.where(lms[hh], _dot(p, do128, TN), 0.0)
                dq_ref[:, cols] = rope_bwd(dq128, cos_c, sin_c).astype(BF16)
            sk_acc[...] += sk
            fold = lambda pr: jnp.where(left, pr[0] + pltpu.roll(pr[0], ATT_DIM, 1), pr[1] + pltpu.roll(pr[1], ATT_DIM, 1))
            dkw = fold(dkd)
            dvw = fold(dvd)
            prev = jnp.concatenate([rope_bwd(dkw[:ATT_BLOCK], cos_p, sin_p), dvw[:ATT_BLOCK]], axis=1)
            cur = jnp.concatenate([rope_bwd(dkw[ATT_BLOCK:], cos_c, sin_c), dvw[ATT_BLOCK:]], axis=1)
            dkv_ref[...] = (carry[...] + prev).astype(BF16)
            carry[...] = cur

        @pl.when(i == nb)
        def _():
            dkv_ref[...] = carry[...].astype(BF16)

        @pl.when((b_id == B - 1) & (i == nb))
        def _():
            dsk_ref[...] = jnp.sum(sk_acc[...], axis=0, keepdims=True)

        _hosted_finish(phase, p_in, p_out, p_sems, (b_id == B - 1) & (i == nb))

    cl = lambda i: jnp.minimum(i, nb - 1)
    pv = lambda i: jnp.maximum(jnp.minimum(i, nb - 1) - 1, 0)
    rowblk = lambda w, cb: pl.BlockSpec((None, ATT_BLOCK, w), lambda b, i: (b, cl(i), cb))
    tab = pl.BlockSpec((ATT_BLOCK, 128), lambda b, i: (cl(i), 0))
    tabp = pl.BlockSpec((ATT_BLOCK, 128), lambda b, i: (pv(i), 0))
    p_ispecs, p_ospecs, p_oshapes, p_alias, p_scratch, p_args = _host_phase(phase, 13, 4)
    res = pl.pallas_call(
        body, name=name,
        grid=(B, nb + 1),
        in_specs=[rowblk(ATT_WIDTH, 0), rowblk(256, 4),
                  pl.BlockSpec((None, ATT_BLOCK, 256), lambda b, i: (b, pv(i), 4)),
                  rowblk(256, _Z0), rowblk(256, _Z0 + 1), rowblk(256, _Z0 + 2), rowblk(256, _Z0 + 3),
                  rowblk(ATT_WIDTH, 1),
                  tab, tab, tabp, tabp,
                  pl.BlockSpec(memory_space=pltpu.SMEM)] + p_ispecs,
        out_specs=[rowblk(ATT_WIDTH, 0),
                   pl.BlockSpec((None, ATT_BLOCK, 256), lambda b, i: (b, jnp.maximum(i - 1, 0), 0)),
                   rowblk(ATT_WIDTH, 0),
                   pl.BlockSpec((1, 128), lambda b, i: (0, 0))] + p_ospecs,
        out_shape=[jax.ShapeDtypeStruct((B, S, ATT_WIDTH), BF16), jax.ShapeDtypeStruct((B, S, 256), BF16),
                   jax.ShapeDtypeStruct((B, S, ATT_WIDTH), BF16), jax.ShapeDtypeStruct((1, 128), F32)] + p_oshapes,
        input_output_aliases=p_alias,
        scratch_shapes=[pltpu.VMEM((ATT_BLOCK, 256), F32), pltpu.VMEM((ATT_BLOCK, 128), F32)] + p_scratch,
        compiler_params=_params(("arbitrary", "arbitrary")),
    )(proj_a, proj_a, proj_a, proj_a, proj_a, proj_a, proj_a, du, cos, sin, cos, sin, sinks_l, *p_args)
    return tuple(res[:4]) + (list(res[4:]),)


def _outproj_fwd(u2, w_out, x2, g_post, target2, name):
    T, D = x2.shape
    tm = _pick(T, (512, 256, 128))
    last = target2 is not None

    def body(u_ref, w_ref, x_ref, g_ref, *rest):
        y = lax.dot_general(u_ref[...], w_ref[...], (NN, ((), ())), preferred_element_type=F32)
        r = lax.rsqrt(jnp.mean(y * y, axis=-1, keepdims=True) + NORM_EPS)
        xn = x_ref[...] + (y * r) * g_ref[...]
        if last:
            t_ref, y_ref, dx_ref, loss_ref = rest
            err = xn - t_ref[...]
            dx_ref[...] = err * (1.0 / D)
            sq = err * err
            acc = sq[:, 0:128]
            for kk in range(1, D // 128):
                acc = acc + sq[:, 128 * kk:128 * (kk + 1)]
            part = jnp.sum(acc.reshape(tm // 8, 8, 128), axis=0) * (0.5 / D)

            @pl.when(pl.program_id(0) == 0)
            def _():
                loss_ref[...] = jnp.zeros_like(loss_ref)

            loss_ref[...] += part
        else:
            y_ref, xn_ref = rest
            xn_ref[...] = xn
        y_ref[...] = y

    row = pl.BlockSpec((tm, D), lambda i: (i, 0))
    in_specs = [pl.BlockSpec((tm, MIX_WIDTH), lambda i: (i, 0)),
                pl.BlockSpec((MIX_WIDTH, D), lambda i: (0, 0)), row,
                pl.BlockSpec((1, D), lambda i: (0, 0))]
    args = [u2, w_out, x2, g_post]
    out_specs = [row, row]
    out_shape = [jax.ShapeDtypeStruct((T, D), F32), jax.ShapeDtypeStruct((T, D), F32)]
    if last:
        in_specs.append(row)
        args.append(target2)
        out_specs.append(pl.BlockSpec((8, 128), lambda i: (0, 0)))
        out_shape.append(jax.ShapeDtypeStruct((8, 128), F32))
    return pl.pallas_call(
        body, name=name, grid=(T // tm,), in_specs=in_specs, out_specs=out_specs, out_shape=out_shape,
        compiler_params=_params(("arbitrary",)),
    )(*args)


def _postnorm_bwd(dxn2, y2, g_post, name):
    T, D = y2.shape
    tm = _pick(T, (512, 256, 128))
    nt = T // tm

    def body(dx_ref, y_ref, g_ref, dy_ref, dg_ref, acc):
        i = pl.program_id(0)

        @pl.when(i == 0)
        def _():
            acc[...] = jnp.zeros_like(acc)

        y = y_ref[...]
        dxn = dx_ref[...]
        r = lax.rsqrt(jnp.mean(y * y, axis=-1, keepdims=True) + NORM_EPS)
        n = y * r
        dn = dxn * g_ref[...]
        dy_ref[...] = (r * (dn - n * jnp.mean(dn * n, axis=-1, keepdims=True))).astype(BF16)
        acc[...] += jnp.sum((dxn * n).reshape(tm // 8, 8, D), axis=0)

        @pl.when(i == nt - 1)
        def _():
            dg_ref[...] = jnp.sum(acc[...], axis=0, keepdims=True)

    row = pl.BlockSpec((tm, D), lambda i: (i, 0))
    vec = pl.BlockSpec((1, D), lambda i: (0, 0))
    return pl.pallas_call(
        body, name=name, grid=(nt,), in_specs=[row, row, vec], out_specs=[row, vec],
        out_shape=[jax.ShapeDtypeStruct((T, D), BF16), jax.ShapeDtypeStruct((1, D), F32)],
        scratch_shapes=[pltpu.VMEM((8, D), F32)],
        compiler_params=_params(("arbitrary",)),
    )(dxn2, y2, g_post)


def _inproj_bwd(dproj2, w_t, x2, dxn2, g_pre, name):
    T, D = x2.shape
    K = dproj2.shape[1]
    tm = _pick(T, (256, 128))
    nt = T // tm

    def body(dp_ref, w_ref, x_ref, dxn_ref, g_ref, dx_ref, dg_ref, acc):
        i = pl.program_id(0)

        @pl.when(i == 0)
        def _():
            acc[...] = jnp.zeros_like(acc)

        dh = lax.dot_general(dp_ref[...], w_ref[...], (NN, ((), ())), preferred_element_type=F32)
        x = x_ref[...]
        r = lax.rsqrt(jnp.mean(x * x, axis=-1, keepdims=True) + NORM_EPS)
        n = x * r
        dn = dh * g_ref[...]
        dx_ref[...] = dxn_ref[...] + r * (dn - n * jnp.mean(dn * n, axis=-1, keepdims=True))
        acc[...] += jnp.sum((dh * n).reshape(tm // 8, 8, D), axis=0)

        @pl.when(i == nt - 1)
        def _():
            dg_ref[...] = jnp.sum(acc[...], axis=0, keepdims=True)

    row = pl.BlockSpec((tm, D), lambda i: (i, 0))
    vec = pl.BlockSpec((1, D), lambda i: (0, 0))
    return pl.pallas_call(
        body, name=name, grid=(nt,),
        in_specs=[pl.BlockSpec((tm, K), lambda i: (i, 0)),
                  pl.BlockSpec((K, D), lambda i: (0, 0), pipeline_mode=pl.Buffered(1)), row, row, vec],
        out_specs=[row, vec],
        out_shape=[jax.ShapeDtypeStruct((T, D), F32), jax.ShapeDtypeStruct((1, D), F32)],
        scratch_shapes=[pltpu.VMEM((8, D), F32)],
        compiler_params=_params(("arbitrary",)),
    )(dproj2, w_t, x2, dxn2, g_pre)


def _step(x, target, g_pre, g_post, lb_param, g_head, sinks, shards=None, full=None):
    B, S, D = x.shape
    T = B * S
    dist = shards is not None
    if dist:
        a_loc, b_loc = shards
        rs = [a_loc.shape[1], b_loc.shape[1]]
        w0 = _run_phase(_gather_ici_phase([a_loc[0], b_loc[0]]), "gather0_ici")
        w0 = _run_phase(_gather_d2d_phase(w0, rs), "gather0_d2d")
        weights = [tuple(w0), None]
    else:
        weights = [(full[0][l], full[1][l]) for l in range(DEPTH)]
    cos, sin = _rope_tables(S)
    saved = []
    xs = x
    loss_part = None
    dxn = None
    for l in range(DEPTH):
        x2 = xs.reshape(T, D)
        host = dist and l == 0
        proj_h, proj_a, h = _inproj(x2, g_pre[l:l + 1], weights[l][0], f"inproj{l}")
        proj_h = proj_h.reshape(B, S, N_H)
        proj_a = proj_a.reshape(B, S, N_A)
        o_h, u, states, got = _hgrn_fwd(proj_h, MIX_WIDTH, lb_param, g_head[l:l + 1], l, f"hgrn_fwd{l}",
                                        _gather_ici_phase([a_loc[1], b_loc[1]]) if host else None)
        u, got = _attn_fwd(proj_a, u, sinks[l], cos, sin, f"attn_fwd{l}",
                           _gather_d2d_phase(got, rs) if host else None)
        if host:
            weights[1] = tuple(got)
        u2 = u.reshape(T, MIX_WIDTH)
        if l < DEPTH - 1:
            y, xn = _outproj_fwd(u2, weights[l][1], x2, g_post[l:l + 1], None, f"outproj{l}")
            xn = xn.reshape(B, S, D)
        else:
            y, dxn, loss_part = _outproj_fwd(u2, weights[l][1], x2, g_post[l:l + 1], target.reshape(T, D),
                                             f"outproj{l}")
            xn = None
        saved.append((x2, h, proj_h, proj_a, o_h, u2, states, y))
        xs = xn

    dw_in, dw_out = [None] * DEPTH, [None] * DEPTH
    dg_pre, dg_post, dlb, dg_head, dsinks = [], [], [], [], []
    for l in reversed(range(DEPTH)):
        x2, h, proj_h, proj_a, o_h, u2, states, y = saved[l]
        host = dist and l == 0
        dy, dgp = _postnorm_bwd(dxn, y, g_post[l:l + 1], f"postnorm_bwd{l}")
        dw_out[l] = _mm_tn(u2, dy, f"wgrad_out{l}")
        du = _mm_nt(dy, weights[l][1], f"dgrad_out{l}").reshape(B, S, MIX_WIDTH)
        dqh, dfh, dih, dzh, dlb_l, dgh, got = _hgrn_bwd(
            proj_h, o_h, du, states, lb_param, g_head[l:l + 1], l, f"hgrn_bwd{l}",
            _reduce_d2d_phase([dw_in[1], dw_out[1]], rs) if host else None)
        if host:
            parts = [_pair_sum(got[0], got[1], "pair_sum_in1"), _pair_sum(got[2], got[3], "pair_sum_out1")]
        dqa, dkv, dza, dsk, got = _attn_bwd(proj_a, du, sinks[l], cos, sin, f"attn_bwd{l}",
                                            _reduce_ici_phase(parts) if host else None)
        if host:
            dw_in[1] = _chip_sum(parts[0], got[0], "chip_sum_in1")
            dw_out[1] = _chip_sum(parts[1], got[1], "chip_sum_out1")
        dproj = jnp.concatenate([dqh, dfh, dih, dzh, dqa, dkv, dza], axis=-1).reshape(T, IN_WIDTH)
        dw_in[l] = _mm_tn(dproj, h, f"wgrad_in{l}")
        dxn, dgpre = _inproj_bwd(dproj, weights[l][0], x2, dxn, g_pre[l:l + 1], f"inproj_bwd{l}")
        dg_pre.append(dgpre)
        dg_post.append(dgp)
        dlb.append(dlb_l)
        dg_head.append(dgh)
        dsinks.append(dsk)
    if dist:
        got = _run_phase(_reduce_d2d_phase([dw_in[0], dw_out[0]], rs), "reduce0_d2d")
        parts = [_pair_sum(got[0], got[1], "pair_sum_in0"), _pair_sum(got[2], got[3], "pair_sum_out0")]
        got = _run_phase(_reduce_ici_phase(parts), "reduce0_ici")
        dw_in[0] = _chip_sum(parts[0], got[0], "chip_sum_in0")
        dw_out[0] = _chip_sum(parts[1], got[1], "chip_sum_out0")
    rev = lambda lst: jnp.concatenate(lst[::-1], axis=0)
    return (loss_part, dxn.reshape(B, S, D), jnp.stack(dw_in), jnp.stack(dw_out),
            rev(dg_pre), rev(dg_post), rev(dlb), rev(dg_head), rev(dsinks))


def _me_and_peers():
    x, y, c = lax.axis_index("x"), lax.axis_index("y"), lax.axis_index("c")
    me = 4 * x + 2 * y + c
    peers = []
    for k in range(1, N_DEV):
        px = 1 - x if k & 4 else x
        py = 1 - y if k & 2 else y
        pc = 1 - c if k & 1 else c
        peers.append(((px, py, pc), 4 * px + 2 * py + pc))
    return me, peers


class _Phase:
    def __init__(self, arrays, out_shapes, aliases, n_local, n_send, build):
        self.arrays, self.out_shapes, self.aliases = list(arrays), list(out_shapes), dict(aliases)
        self.n_local, self.n_send, self.build = n_local, n_send, build

    def scratch(self):
        return [pltpu.SemaphoreType.DMA((self.n_send,)), pltpu.SemaphoreType.DMA((self.n_send,)),
                pltpu.SemaphoreType.DMA((max(self.n_local, 1),))]

    def _copies(self, in_refs, out_refs, sems, arrivals):
        send_sems, recv_sems, loc_sems = sems
        local, sends, recvs = self.build(in_refs, out_refs)
        assert len(local) == self.n_local and len(sends) == self.n_send == len(recvs)
        loc = [pltpu.make_async_copy(s, d, loc_sems.at[i]) for i, (s, d) in enumerate(local)]
        out = [pltpu.make_async_remote_copy(src_ref=s, dst_ref=d, send_sem=send_sems.at[i], recv_sem=recv_sems.at[i],
                                            device_id=dev, device_id_type=MESH) for i, (s, d, dev) in enumerate(sends)]
        inc = [pltpu.make_async_remote_copy(src_ref=s, dst_ref=r, send_sem=send_sems.at[i], recv_sem=recv_sems.at[i],
                                            device_id=dev, device_id_type=MESH)
               for i, ((s, _, dev), r) in enumerate(zip(sends, recvs))] if arrivals else []
        return loc, out, inc

    def start(self, in_refs, out_refs, sems):
        loc, out, _ = self._copies(in_refs, out_refs, sems, False)
        for cp in loc + out:
            cp.start()

    def finish(self, in_refs, out_refs, sems):
        loc, out, inc = self._copies(in_refs, out_refs, sems, True)
        for cp in inc:
            cp.wait_recv()
        for cp in out:
            cp.wait_send()
        for cp in loc:
            cp.wait()


_ANY = pl.BlockSpec(memory_space=pl.ANY)


def _host_phase(phase, n_in, n_out):
    if phase is None:
        return [], [], [], {}, [], []
    aliases = {n_in + i: n_out + o for i, o in phase.aliases.items()}
    return ([_ANY] * len(phase.arrays), [_ANY] * len(phase.out_shapes), phase.out_shapes, aliases, phase.scratch(),
            phase.arrays)


def _split_refs(refs, n_in, n_out, n_scr, phase):
    pi = len(phase.arrays) if phase else 0
    po = len(phase.out_shapes) if phase else 0
    a = n_in + pi
    b = a + n_out + po
    return (refs[:n_in], refs[a:a + n_out], refs[b:b + n_scr], refs[n_in:a], refs[a + n_out:b], refs[b + n_scr:])


def _hosted_start(phase, p_in, p_out, p_sems, first):
    if phase is not None:
        @pl.when(first)
        def _():
            phase.start(p_in, p_out, p_sems)


def _hosted_finish(phase, p_in, p_out, p_sems, last):
    if phase is not None:
        @pl.when(last)
        def _():
            phase.finish(p_in, p_out, p_sems)


def _run_phase(phase, name):
    n_in = len(phase.arrays)

    def body(*refs):
        phase.start(refs[:n_in], refs[n_in:n_in + len(phase.out_shapes)], refs[n_in + len(phase.out_shapes):])
        phase.finish(refs[:n_in], refs[n_in:n_in + len(phase.out_shapes)], refs[n_in + len(phase.out_shapes):])

    return pl.pallas_call(
        body, name=name, in_specs=[_ANY] * n_in, out_specs=[_ANY] * len(phase.out_shapes),
        out_shape=phase.out_shapes, input_output_aliases=phase.aliases, scratch_shapes=phase.scratch(),
        compiler_params=pltpu.CompilerParams(has_side_effects=True),
    )(*phase.arrays)


def _mesh_place():
    x, y, c = lax.axis_index("x"), lax.axis_index("y"), lax.axis_index("c")
    chips = [(x, y), (1 - x, y), (x, 1 - y), (1 - x, 1 - y)]
    num = lambda chip, core: 4 * chip[0] + 2 * chip[1] + core
    return c, chips, num


def _rows(ref, r, dev):
    return ref.at[pl.ds(pl.multiple_of(dev * r, 16), r), :]


def _gather_ici_phase(locs):
    rs = [a.shape[0] for a in locs]

    def build(ins, outs):
        c, chips, num = _mesh_place()
        me = num(chips[0], c)
        local = [(ins[i], _rows(outs[i], r, me)) for i, r in enumerate(rs)]
        targets = [((*chips[0], 1 - c), num(chips[0], 1 - c))] + [((*ch, c), num(ch, c)) for ch in chips[1:]]
        sends, recvs = [], []
        for dev, dnum in targets:
            for i, r in enumerate(rs):
                sends.append((ins[i], _rows(outs[i], r, me), dev))
                recvs.append(_rows(outs[i], r, dnum))
        return local, sends, recvs

    shapes = [jax.ShapeDtypeStruct((N_DEV * a.shape[0], a.shape[1]), a.dtype) for a in locs]
    return _Phase(locs, shapes, {}, len(locs), 4 * len(locs), build)


def _gather_d2d_phase(fulls, rs):
    def build(ins, outs):
        c, chips, num = _mesh_place()
        sib = (*chips[0], 1 - c)
        sends, recvs = [], []
        for ch in chips[1:]:
            for i, r in enumerate(rs):
                blk = _rows(outs[i], r, num(ch, c))
                sends.append((blk, blk, sib))
                recvs.append(_rows(outs[i], r, num(ch, 1 - c)))
        return [], sends, recvs

    shapes = [jax.ShapeDtypeStruct(a.shape, a.dtype) for a in fulls]
    return _Phase(fulls, shapes, {i: i for i in range(len(fulls))}, 0, 3 * len(fulls), build)


def _reduce_d2d_phase(grads, rs):
    def build(ins, outs):
        c, chips, num = _mesh_place()
        sib = (*chips[0], 1 - c)
        local, sends, recvs = [], [], []
        for j, ch in enumerate(chips):
            for i, r in enumerate(rs):
                local.append((_rows(ins[i], r, num(ch, c)), outs[2 * i].at[j]))
                sends.append((_rows(ins[i], r, num(ch, 1 - c)), outs[2 * i + 1].at[j], sib))
                recvs.append(outs[2 * i + 1].at[j])
        return local, sends, recvs

    shapes = []
    for g, r in zip(grads, rs):
        shapes += [jax.ShapeDtypeStruct((4, r, g.shape[1]), g.dtype)] * 2
    return _Phase(grads, shapes, {}, 4 * len(grads), 4 * len(grads), build)


def _reduce_ici_phase(parts):
    def build(ins, outs):
        c, chips, _ = _mesh_place()
        sends, recvs = [], []
        for t in range(1, 4):
            for i in range(len(parts)):
                sends.append((ins[i].at[t], outs[i].at[t - 1], (*chips[t], c)))
                recvs.append(outs[i].at[t - 1])
        return [], sends, recvs

    shapes = [jax.ShapeDtypeStruct((3,) + p.shape[1:], p.dtype) for p in parts]
    return _Phase(parts, shapes, {}, 0, 3 * len(parts), build)


def _pair_sum(q, r, name):
    n, R, D = q.shape
    tr = _pick(R, (400, 256, 200, 128, 64, 16))

    def body(q_ref, r_ref, o_ref):
        o_ref[...] = (q_ref[...].astype(F32) + r_ref[...].astype(F32)).astype(o_ref.dtype)

    blk = pl.BlockSpec((None, tr, D), lambda j, i: (j, i, 0))
    return pl.pallas_call(body, name=name, grid=(n, R // tr), in_specs=[blk, blk], out_specs=blk,
                          out_shape=jax.ShapeDtypeStruct(q.shape, q.dtype),
                          compiler_params=_params(("parallel", "parallel")))(q, r)


def _chip_sum(p, r, name):
    _, R, D = p.shape
    tr = _pick(R, (400, 256, 200, 128, 64, 16))

    def body(p_ref, r_ref, o_ref):
        acc = p_ref[...].astype(F32)
        for t in range(3):
            acc = acc + r_ref[t].astype(F32)
        o_ref[...] = acc

    return pl.pallas_call(
        body, name=name, grid=(R // tr,),
        in_specs=[pl.BlockSpec((None, tr, D), lambda i: (0, i, 0)), pl.BlockSpec((3, tr, D), lambda i: (0, i, 0))],
        out_specs=pl.BlockSpec((tr, D), lambda i: (i, 0)), out_shape=jax.ShapeDtypeStruct((R, D), F32),
        compiler_params=_params(("parallel",)))(p, r)


def _gather_weights(a_loc, b_loc):
    L, ra, D = a_loc.shape
    rb = b_loc.shape[1]

    def body(a_ref, b_ref, ao_ref, bo_ref, send_sems, recv_sems, loc_sems):
        me, peers = _me_and_peers()
        parts = [(a_ref, ao_ref, ra), (b_ref, bo_ref, rb)]

        def rows(dst, l, r, dev):
            return dst.at[l, pl.ds(pl.multiple_of(dev * r, 16), r), :]

        local = []
        for l in range(L):
            for t, (src, dst, r) in enumerate(parts):
                cp = pltpu.make_async_copy(src.at[l], rows(dst, l, r, me), loc_sems.at[2 * l + t])
                cp.start()
                local.append(cp)
        sends = []
        for k, (pid, _) in enumerate(peers):
            for l in range(L):
                for t, (src, dst, r) in enumerate(parts):
                    cp = pltpu.make_async_remote_copy(
                        src_ref=src.at[l], dst_ref=rows(dst, l, r, me),
                        send_sem=send_sems.at[k, 2 * l + t], recv_sem=recv_sems.at[k, 2 * l + t],
                        device_id=pid, device_id_type=MESH)
                    cp.start()
                    sends.append(cp)
        for k, (pid, pnum) in enumerate(peers):
            for l in range(L):
                for t, (src, dst, r) in enumerate(parts):
                    pltpu.make_async_remote_copy(
                        src_ref=src.at[l], dst_ref=rows(dst, l, r, pnum),
                        send_sem=send_sems.at[k, 2 * l + t], recv_sem=recv_sems.at[k, 2 * l + t],
                        device_id=pid, device_id_type=MESH).wait_recv()
        for cp in sends:
            cp.wait_send()
        for cp in local:
            cp.wait()

    hbm = pl.BlockSpec(memory_space=pl.ANY)
    return pl.pallas_call(
        body, name="gather_weights",
        in_specs=[hbm, hbm], out_specs=[hbm, hbm],
        out_shape=[jax.ShapeDtypeStruct((L, N_DEV * ra, D), a_loc.dtype),
                   jax.ShapeDtypeStruct((L, N_DEV * rb, D), b_loc.dtype)],
        scratch_shapes=[pltpu.SemaphoreType.DMA((N_DEV - 1, 2 * L)), pltpu.SemaphoreType.DMA((N_DEV - 1, 2 * L)),
                        pltpu.SemaphoreType.DMA((2 * L,))],
        compiler_params=pltpu.CompilerParams(has_side_effects=True),
    )(a_loc, b_loc)


def _scatter_grads(ga, gb):
    L, ra8, D = ga.shape
    ra = ra8 // N_DEV
    rb = gb.shape[1] // N_DEV

    def body(a_ref, b_ref, ao_ref, bo_ref, send_sems, recv_sems, loc_sems):
        me, peers = _me_and_peers()
        parts = [(a_ref, ao_ref, ra), (b_ref, bo_ref, rb)]

        def rows(src, l, r, dev):
            return src.at[l, pl.ds(pl.multiple_of(dev * r, 16), r), :]

        local = []
        for l in range(L):
            for t, (src, dst, r) in enumerate(parts):
                cp = pltpu.make_async_copy(rows(src, l, r, me), dst.at[me, l], loc_sems.at[2 * l + t])
                cp.start()
                local.append(cp)
        sends = []
        for k, (pid, pnum) in enumerate(peers):
            for l in range(L):
                for t, (src, dst, r) in enumerate(parts):
                    cp = pltpu.make_async_remote_copy(
                        src_ref=rows(src, l, r, pnum), dst_ref=dst.at[me, l],
                        send_sem=send_sems.at[k, 2 * l + t], recv_sem=recv_sems.at[k, 2 * l + t],
                        device_id=pid, device_id_type=MESH)
                    cp.start()
                    sends.append(cp)
        for k, (pid, pnum) in enumerate(peers):
            for l in range(L):
                for t, (src, dst, r) in enumerate(parts):
                    pltpu.make_async_remote_copy(
                        src_ref=rows(src, l, r, pnum), dst_ref=dst.at[pnum, l],
                        send_sem=send_sems.at[k, 2 * l + t], recv_sem=recv_sems.at[k, 2 * l + t],
                        device_id=pid, device_id_type=MESH).wait_recv()
        for cp in sends:
            cp.wait_send()
        for cp in local:
            cp.wait()

    hbm = pl.BlockSpec(memory_space=pl.ANY)
    return pl.pallas_call(
        body, name="scatter_grads",
        in_specs=[hbm, hbm], out_specs=[hbm, hbm],
        out_shape=[jax.ShapeDtypeStruct((N_DEV, L, ra, D), ga.dtype),
                   jax.ShapeDtypeStruct((N_DEV, L, rb, D), gb.dtype)],
        scratch_shapes=[pltpu.SemaphoreType.DMA((N_DEV - 1, 2 * L)), pltpu.SemaphoreType.DMA((N_DEV - 1, 2 * L)),
                        pltpu.SemaphoreType.DMA((2 * L,))],
        compiler_params=pltpu.CompilerParams(has_side_effects=True),
    )(ga, gb)


def _sum_slots(r, name):
    _, R, D = r.shape
    tr = _pick(R, (400, 256, 200, 128, 64, 16))

    def body(r_ref, o_ref):
        acc = r_ref[0].astype(F32)
        for d in range(1, N_DEV):
            acc = acc + r_ref[d].astype(F32)
        o_ref[...] = acc

    return pl.pallas_call(
        body, name=name, grid=(R // tr,),
        in_specs=[pl.BlockSpec((N_DEV, tr, D), lambda i: (0, i, 0))],
        out_specs=pl.BlockSpec((tr, D), lambda i: (i, 0)),
        out_shape=jax.ShapeDtypeStruct((R, D), F32),
        compiler_params=_params(("parallel",)),
    )(r)


def _allreduce_small(vec):
    R, C = vec.shape

    def body(v_ref, o_ref, buf, send_sems, recv_sems):
        me, peers = _me_and_peers()
        buf[me] = v_ref[...]
        sends = []
        for k, (pid, _) in enumerate(peers):
            cp = pltpu.make_async_remote_copy(src_ref=v_ref, dst_ref=buf.at[me], send_sem=send_sems.at[k],
                                              recv_sem=recv_sems.at[k], device_id=pid, device_id_type=MESH)
            cp.start()
            sends.append(cp)
        for k, (pid, pnum) in enumerate(peers):
            pltpu.make_async_remote_copy(src_ref=v_ref, dst_ref=buf.at[pnum], send_sem=send_sems.at[k],
                                         recv_sem=recv_sems.at[k], device_id=pid, device_id_type=MESH).wait_recv()
        for cp in sends:
            cp.wait_send()
        acc = buf[0]
        for d in range(1, N_DEV):
            acc = acc + buf[d]
        o_ref[...] = acc

    vm = pl.BlockSpec(memory_space=pltpu.VMEM)
    return pl.pallas_call(
        body, name="allreduce_small",
        in_specs=[vm], out_specs=vm,
        out_shape=jax.ShapeDtypeStruct((R, C), F32),
        scratch_shapes=[pltpu.VMEM((N_DEV, R, C), F32), pltpu.SemaphoreType.DMA((N_DEV - 1,)),
                        pltpu.SemaphoreType.DMA((N_DEV - 1,))],
        compiler_params=pltpu.CompilerParams(has_side_effects=True),
    )(vec)


def _adamw(w, g, m, v, name):
    R, C = w.shape
    tr = _pick(R, (256, 128, 64, 32, 16, 8)) if R >= 8 else R
    c1 = 1.0 - ADAM_B1 ** ADAM_STEP
    c2 = 1.0 - ADAM_B2 ** ADAM_STEP

    def body(w_ref, g_ref, m_ref, v_ref, d_ref, mo_ref, vo_ref):
        gg = g_ref[...]
        mn = ADAM_B1 * m_ref[...] + (1.0 - ADAM_B1) * gg
        vn = ADAM_B2 * v_ref[...] + (1.0 - ADAM_B2) * (gg * gg)
        d_ref[...] = -ADAM_LR * ((mn / c1) / (jnp.sqrt(vn / c2) + ADAM_EPS) + ADAM_WD * w_ref[...])
        mo_ref[...] = mn
        vo_ref[...] = vn

    blk = pl.BlockSpec((tr, C), lambda i: (i, 0))
    sh = jax.ShapeDtypeStruct((R, C), F32)
    return pl.pallas_call(
        body, name=name, grid=(R // tr,), in_specs=[blk] * 4, out_specs=[blk] * 3, out_shape=[sh] * 3,
        compiler_params=_params(("parallel",)),
    )(w, g, m, v)


def _lb_param_grad(lb_param, dlb):
    L, C = lb_param.shape

    def body(p_ref, d_ref, o_ref):
        lbp = p_ref[...]
        d = d_ref[...]
        mx = jnp.max(lbp, axis=0, keepdims=True)
        e = jnp.exp(lbp - mx)
        p = e / jnp.sum(e, axis=0, keepdims=True)
        tot = jnp.sum(d, axis=0, keepdims=True)
        dps = []
        rest = tot
        for j in range(L):
            dps.append(rest - tot if j == 0 else rest)
            rest = rest - d[j:j + 1]
        dp = jnp.concatenate(dps, axis=0)
        o_ref[...] = p * (dp - jnp.sum(p * dp, axis=0, keepdims=True))

    vm = pl.BlockSpec(memory_space=pltpu.VMEM)
    return pl.pallas_call(body, name="lb_param_grad", in_specs=[vm, vm], out_specs=vm,
                          out_shape=jax.ShapeDtypeStruct((L, C), F32))(lb_param, dlb)


def _pack_small(loss_part, dg_pre, dg_post, dlb, dg_head, dsinks):
    pad8 = lambda a: jnp.pad(a.reshape(-1, 128), ((0, 8 - DEPTH), (0, 0)))
    rows = [dg_pre.reshape(-1, 128), dg_post.reshape(-1, 128), dlb.reshape(-1, 128), pad8(dg_head), pad8(dsinks),
            loss_part]
    return jnp.concatenate(rows, axis=0)


def _unpack_small(vec):
    n = DEPTH * D_MODEL // 128
    o = 0
    dg_pre = vec[o:o + n].reshape(DEPTH, D_MODEL); o += n
    dg_post = vec[o:o + n].reshape(DEPTH, D_MODEL); o += n
    dlb = vec[o:o + n].reshape(DEPTH, HG_WIDTH); o += n
    dg_head = vec[o:o + DEPTH]; o += 8
    dsinks = vec[o:o + DEPTH, :ATT_HEADS]; o += 8
    loss = jnp.sum(vec[o:o + 8])
    return loss, dg_pre, dg_post, dlb, dg_head, dsinks


def kernel(x, w_in, w_out, g_pre, g_post, lb_param, g_head, sinks, loss_target, m_w_in, m_w_out, m_g_pre, m_g_post, m_lb_param, m_g_head, m_sinks, v_w_in, v_w_out, v_g_pre, v_g_post, v_lb_param, v_g_head, v_sinks):
    L, D, nloc = w_in.shape
    w_in_t_loc = jnp.swapaxes(w_in, 1, 2).astype(BF16)
    (loss_part, dx, gw_in_t, gw_out, dg_pre, dg_post, dlb, dg_head, dsinks) = _step(
        x, loss_target, g_pre, g_post, lb_param, g_head, sinks, shards=(w_in_t_loc, w_out.astype(BF16)))
    gw_in = jnp.swapaxes(gw_in_t, 1, 2)

    small = _allreduce_small(_pack_small(loss_part, dg_pre, dg_post, dlb, dg_head, dsinks))
    loss, gg_pre, gg_post, gdlb, gg_head, gsinks = _unpack_small(small)
    glb = _lb_param_grad(lb_param, gdlb)

    grads = [gw_in, gw_out, gg_pre, gg_post, glb, gg_head, gsinks]
    ws = [w_in, w_out, g_pre, g_post, lb_param, g_head, sinks]
    ms = [m_w_in, m_w_out, m_g_pre, m_g_post, m_lb_param, m_g_head, m_sinks]
    vs = [v_w_in, v_w_out, v_g_pre, v_g_post, v_lb_param, v_g_head, v_sinks]
    names = ["w_in", "w_out", "g_pre", "g_post", "lb_param", "g_head", "sinks"]
    deltas, new_m, new_v = [], [], []
    for w, g, m, v, nm in zip(ws, grads, ms, vs, names):
        sh = w.shape
        two = lambda a: a.reshape(-1, sh[-1])
        d, mn, vn = _adamw(two(w), two(g), two(m), two(v), "adamw_" + nm)
        deltas.append(d.reshape(sh))
        new_m.append(mn.reshape(sh))
        new_v.append(vn.reshape(sh))
    return (loss, dx, *grads, *deltas, *new_m, *new_v)
```

```python
import functools
import math

import numpy as np
import jax
import jax.numpy as jnp
from jax import lax
from jax.experimental import pallas as pl
from jax.experimental.pallas import tpu as pltpu

F32 = jnp.float32
BF16 = jnp.bfloat16

D_MODEL = 1024
DEPTH = 2
HG_HEADS = 8
HG_DIM = 128
HG_WIDTH = HG_HEADS * HG_DIM
CHUNK = 64
ATT_HEADS = 16
ATT_DIM = 64
ATT_WIDTH = ATT_HEADS * ATT_DIM
KV_WIDTH = 128
ATT_BLOCK = 128
ATT_SCALE = 1.0 / math.sqrt(ATT_DIM)
ROPE_THETA = 10000.0
NORM_EPS = 1e-6
NEG_INF = -1e30
LB_FLOOR = 1e-20
N_H = 4 * HG_WIDTH
N_A = 2 * ATT_WIDTH + 2 * KV_WIDTH
IN_WIDTH = N_H + N_A
MIX_WIDTH = HG_WIDTH + ATT_WIDTH

ADAM_LR = 0.001
ADAM_B1 = 0.9
ADAM_B2 = 0.999
ADAM_EPS = 1e-08
ADAM_WD = 0.01
ADAM_STEP = 10

N_DEV = 8
MESH = pl.DeviceIdType.MESH
VMEM_LIMIT = 56 * 1024 * 1024

NN = ((1,), (0,))
NT = ((1,), (1,))
TN = ((0,), (0,))


def _dot(a, b, dims):
    return lax.dot_general(a.astype(BF16), b.astype(BF16), (dims, ((), ())), preferred_element_type=F32)


def _params(sem=None, **kw):
    return pltpu.CompilerParams(dimension_semantics=sem, vmem_limit_bytes=VMEM_LIMIT, **kw)


def _sigmoid(x):
    return 1.0 / (1.0 + jnp.exp(-x))


def _silu(x):
    return x * _sigmoid(x)


def _silu_grad(x):
    s = _sigmoid(x)
    return s * (1.0 + x * (1.0 - s))


def _pick(n, prefs):
    for p in prefs:
        if n % p == 0:
            return p
    return n


def _inproj(x2, g, w, name):
    T, D = x2.shape
    tm = _pick(T, (256, 128))
    nchunk = 1024

    def body(x_ref, g_ref, w_ref, oh_ref, oa_ref, h_ref):
        x = x_ref[...]
        r = lax.rsqrt(jnp.mean(x * x, axis=-1, keepdims=True) + NORM_EPS)
        h = ((x * r) * g_ref[...]).astype(BF16)
        h_ref[...] = h
        for j in range(0, N_H, nchunk):
            oh_ref[:, j:j + nchunk] = lax.dot_general(h, w_ref[j:j + nchunk, :], (NT, ((), ())),
                                                      preferred_element_type=F32)
        for j in range(0, N_A, N_A // 2):
            oa_ref[:, j:j + N_A // 2] = lax.dot_general(h, w_ref[N_H + j:N_H + j + N_A // 2, :], (NT, ((), ())),
                                                        preferred_element_type=F32)

    row = lambda w_: pl.BlockSpec((tm, w_), lambda i: (i, 0))
    return pl.pallas_call(
        body, name=name,
        grid=(T // tm,),
        in_specs=[row(D), pl.BlockSpec((1, D), lambda i: (0, 0)),
                  pl.BlockSpec((IN_WIDTH, D), lambda i: (0, 0), pipeline_mode=pl.Buffered(1))],
        out_specs=[row(N_H), row(N_A), row(D)],
        out_shape=[jax.ShapeDtypeStruct((T, N_H), F32), jax.ShapeDtypeStruct((T, N_A), F32),
                   jax.ShapeDtypeStruct((T, D), BF16)],
        compiler_params=_params(("parallel",)),
    )(x2, g, w)


def _mm_nt(a, b, name, out_dtype=F32):
    M, K = a.shape
    N = b.shape[0]
    tm = _pick(M, (512, 256, 128))

    def body(a_ref, b_ref, o_ref):
        o_ref[...] = lax.dot_general(a_ref[...], b_ref[...], (NT, ((), ())),
                                     preferred_element_type=F32).astype(out_dtype)

    return pl.pallas_call(
        body, name=name,
        grid=(M // tm,),
        in_specs=[pl.BlockSpec((tm, K), lambda i: (i, 0)),
                  pl.BlockSpec((N, K), lambda i: (0, 0), pipeline_mode=pl.Buffered(1))],
        out_specs=pl.BlockSpec((tm, N), lambda i: (i, 0)),
        out_shape=jax.ShapeDtypeStruct((M, N), out_dtype),
        compiler_params=_params(("parallel",)),
    )(a, b)


def _mm_tn(pieces, b, name, out_dtype=BF16):
    T, m = b.shape
    tn = 256
    counts = [p.shape[1] // tn for p in pieces]
    starts = [sum(counts[:i]) for i in range(len(pieces))]
    n_p = len(pieces)

    def body(*refs):
        b_ref, o_ref = refs[n_p], refs[n_p + 1]
        i = pl.program_id(0)
        for p in range(n_p):
            @pl.when((i >= starts[p]) & (i < starts[p] + counts[p]))
            def _(p=p):
                o_ref[...] = lax.dot_general(refs[p][...], b_ref[...], (TN, ((), ())),
                                             preferred_element_type=F32).astype(out_dtype)

    piece_spec = lambda s, c: pl.BlockSpec((T, tn), lambda i: (0, jnp.clip(i - s, 0, c - 1)))
    return pl.pallas_call(
        body, name=name,
        grid=(sum(counts),),
        in_specs=[piece_spec(s, c) for s, c in zip(starts, counts)]
        + [pl.BlockSpec((T, m), lambda i: (0, 0), pipeline_mode=pl.Buffered(1))],
        out_specs=pl.BlockSpec((tn, m), lambda i: (i, 0)),
        out_shape=jax.ShapeDtypeStruct((sum(counts) * tn, m), out_dtype),
        compiler_params=_params(("arbitrary",)),
    )(*pieces, b)


_LEVELS = (0, 1, 2, 4, 8, 16, 32)
_CUM_L = (2, 4, 8, 16, 32, 64)
_ALL_KINDS = tuple(("c", L) for L in _CUM_L) + tuple(("r", L) for L in _CUM_L)
_MXU_KINDS = (("c", 2), ("c", 4), ("c", CHUNK), ("r", 2), ("r", 4))
N_CUM = len(_ALL_KINDS) * CHUNK
N_CUM_F = len(_MXU_KINDS) * CHUNK


def _cum_matrices():
    t = np.arange(CHUNK)[:, None]
    r = np.arange(CHUNK)[None, :]

    def mat(kind):
        c, L = kind
        return ((r // L == t // L) & ((r <= t) if c == "c" else (r > t))).astype(np.float32)

    fwd = np.concatenate([mat(kd) for kd in _MXU_KINDS], axis=0)
    full = np.concatenate([mat(kd) for kd in _ALL_KINDS], axis=0)
    return jnp.asarray(fwd, BF16), jnp.asarray(full.T.copy(), BF16)


def _level_masks():
    t = np.arange(CHUNK)[:, None]
    s = np.arange(CHUNK)[None, :]
    ms = []
    for L in _LEVELS:
        if L == 0:
            ms.append(t == s)
        else:
            ms.append((t // (2 * L) == s // (2 * L)) & ((t // L) % 2 == 1) & ((s // L) % 2 == 0))
    return jnp.asarray(np.stack(ms).astype(np.float32))


def _split3(x):
    hi = x.astype(BF16)
    r1 = x - hi.astype(F32)
    mid = r1.astype(BF16)
    lo = (r1 - mid.astype(F32)).astype(BF16)
    return hi, mid, lo


def _cum3(ts, x):
    hi, mid, lo = _split3(x)
    d = lambda p: lax.dot_general(ts, p, (NN, ((), ())), preferred_element_type=F32)
    return d(hi) + d(mid) + d(lo)


def _lb_terms(lbp, layer):
    mx = jnp.max(lbp, axis=0, keepdims=True)
    e = jnp.exp(lbp - mx)
    p = e / jnp.sum(e, axis=0, keepdims=True)
    cum = p[0:1]
    for j in range(1, layer + 1):
        cum = cum + p[j:j + 1]
    lb = cum - p[0:1]
    lbf = jnp.maximum(lb, LB_FLOOR)
    return dict(lb=lb, a=jnp.log(lbf), c=jnp.log(1.0 - lb), one_m=1.0 - lb, kcorr=lb - lbf,
                dlb1=jnp.where(lb > LB_FLOOR, 1.0 / lbf, 0.0), dlb2=1.0 / (1.0 - lb))


def _gate_fwd(x, lt):
    ls = jnp.minimum(x, 0.0) - jnp.log(1.0 + jnp.exp(-jnp.abs(x)))
    u1 = lt["a"]
    u2 = lt["c"] + ls
    mx = jnp.maximum(u1, u2)
    logf = mx + jnp.log(1.0 + jnp.exp(-jnp.abs(u1 - u2)))
    k = lt["one_m"] * (1.0 / (1.0 + jnp.exp(x))) + lt["kcorr"]
    return logf, k, u1, u2


def _chunk_cums(ts, g):
    cs = _cum3(ts, g)
    out = {kind: cs[CHUNK * i:CHUNK * (i + 1)] for i, kind in enumerate(_MXU_KINDS)}
    b = out[("c", CHUNK)]
    last = [jnp.broadcast_to(b[8 * r + 7:8 * r + 8, :], (8, HG_DIM)) for r in range(CHUNK // 8)]
    zero = jnp.zeros((8, HG_DIM), F32)
    for L in (8, 16, 32):
        nb = L // 8
        before = [last[(r // nb) * nb - 1] if r >= nb else zero for r in range(CHUNK // 8)]
        end = [last[(r // nb) * nb + nb - 1] for r in range(CHUNK // 8)]
        out[("c", L)] = b - jnp.concatenate(before, axis=0)
        out[("r", L)] = jnp.concatenate(end, axis=0) - b
    out[("r", CHUNK)] = jnp.broadcast_to(b[CHUNK - 1:CHUNK, :], (CHUNK, HG_DIM)) - b
    return out


def _level_factors(cums, g, L):
    if L == 0:
        return None, None
    if L == 1:
        return jnp.exp(g), None
    return jnp.exp(cums[("c", L)]), jnp.exp(cums[("r", L)])


def _mul(a, e):
    return a if e is None else a * e


def _hg_intra_fwd(qf, k, v, g, ts, m_ref):
    cums = _chunk_cums(ts, g)
    amat = jnp.zeros((CHUNK, CHUNK), F32)
    for li, L in enumerate(_LEVELS):
        eq, ek = _level_factors(cums, g, L)
        amat = amat + _dot(_mul(qf, eq), _mul(k, ek), NT) * m_ref[li]
    b = cums[("c", CHUNK)]
    kv = _dot(v, k * jnp.exp(cums[("r", CHUNK)]), TN)
    return _dot(amat, v, NN), qf * jnp.exp(b), jnp.exp(b[CHUNK - 1:CHUNK, :]), kv


def _hg_intra_bwd(qf, k, v, g, do, ts, m_ref):
    cums = _chunk_cums(ts, g)
    dcs = {}
    da = _dot(do, v, NT)
    dq = jnp.zeros((CHUNK, HG_DIM), F32)
    dk = jnp.zeros((CHUNK, HG_DIM), F32)
    dg = jnp.zeros((CHUNK, HG_DIM), F32)
    amat = jnp.zeros((CHUNK, CHUNK), F32)
    for li, L in enumerate(_LEVELS):
        eq, ek = _level_factors(cums, g, L)
        ql = _mul(qf, eq)
        kl = _mul(k, ek)
        m = m_ref[li]
        amat = amat + _dot(ql, kl, NT) * m
        dal = da * m
        dql = _dot(dal, kl, NN)
        dkl = _dot(dal, ql, TN)
        dq = dq + _mul(dql, eq)
        dk = dk + _mul(dkl, ek)
        if L == 1:
            dg = dg + dql * ql
        elif L > 1:
            dcs[("c", L)] = dql * ql
            dcs[("r", L)] = dkl * kl
    b = cums[("c", CHUNK)]
    e64 = jnp.exp(b)
    er64 = jnp.exp(cums[("r", CHUNK)])
    qb = qf * e64
    return dict(dq=dq, dk=dk, dv=_dot(amat, do, TN), dg=dg, dcs=dcs, e64=e64, er64=er64, qb=qb, kst=k * er64,
                dec=jnp.exp(b[CHUNK - 1:CHUNK, :]), qd=_dot(do, qb, TN))


def _hg_state_bwd(w, v, do, st, dst, tst):
    dqb = _dot(do, st, NN)
    dkst = _dot(v, dst, NN)
    dq = w["dq"] + dqb * w["e64"]
    dk = w["dk"] + dkst * w["er64"]
    dv = w["dv"] + _dot(w["kst"], dst, NT)
    dtot = jnp.sum(dst * st, axis=0, keepdims=True) * w["dec"]
    trow = lax.broadcasted_iota(jnp.int32, (CHUNK, 1), 0)
    dcs = dict(w["dcs"])
    dcs[("c", CHUNK)] = dqb * w["qb"] + jnp.where(trow == CHUNK - 1, dtot, 0.0)
    dcs[("r", CHUNK)] = dkst * w["kst"]
    stack = jnp.concatenate([dcs[kind] for kind in _ALL_KINDS], axis=0)
    return dq, dk, dv, w["dg"] + _cum3(tst, stack)


def _hgrn_fwd(proj_h, u_rows, lb_param, g_head, layer, name, phase=None):
    B, S, _ = proj_h.shape
    sb = _pick(S, (512, 256, 128, 64))
    nc = sb // CHUNK
    ts, _ = _cum_matrices()

    def body(*refs):
        ins, outs, (st,), p_in, p_out, p_sems = _split_refs(refs, 8, 3, 1, phase)
        q_ref, f_ref, i_ref, z_ref, lbp_ref, gh_ref, ts_ref, m_ref = ins
        o_ref, u_ref, sts_ref = outs
        h_id, b_id, s_id = pl.program_id(0), pl.program_id(1), pl.program_id(2)
        _hosted_start(phase, p_in, p_out, p_sems, (h_id == 0) & (b_id == 0) & (s_id == 0))

        @pl.when(s_id == 0)
        def _():
            st[...] = jnp.zeros_like(st)

        lt = _lb_terms(lbp_ref[...], layer)
        tsv = ts_ref[...]
        gh = gh_ref[...]
        parts = []
        for ci in range(nc):
            rows = slice(ci * CHUNK, (ci + 1) * CHUNK)
            logf, k, _, _ = _gate_fwd(f_ref[rows, :], lt)
            parts.append(_hg_intra_fwd(_silu(q_ref[rows, :]), k, i_ref[rows, :], logf, tsv, m_ref))
        cur = st[...]
        starts = []
        for ci in range(nc):
            sts_ref[ci] = cur
            starts.append(cur)
            cur = cur * parts[ci][2] + parts[ci][3]
        st[...] = cur
        for ci in range(nc):
            rows = slice(ci * CHUNK, (ci + 1) * CHUNK)
            o = parts[ci][0] + _dot(parts[ci][1], starts[ci], NT)
            o_ref[rows, :] = o
            r = lax.rsqrt(jnp.mean(o * o, axis=-1, keepdims=True) + NORM_EPS)
            u_ref[rows, :] = (((o * r) * gh) * _silu(z_ref[rows, :])).astype(BF16)
        _hosted_finish(phase, p_in, p_out, p_sems, (h_id == HG_HEADS - 1) & (b_id == B - 1) & (s_id == S // sb - 1))

    col = lambda base: pl.BlockSpec((None, sb, HG_DIM), lambda h, b, s: (b, s, base + h))
    p_ispecs, p_ospecs, p_oshapes, p_alias, p_scratch, p_args = _host_phase(phase, 8, 3)
    res = pl.pallas_call(
        body, name=name,
        grid=(HG_HEADS, B, S // sb),
        in_specs=[col(0), col(HG_HEADS), col(2 * HG_HEADS), col(3 * HG_HEADS),
                  pl.BlockSpec((DEPTH, HG_DIM), lambda h, b, s: (0, h)),
                  pl.BlockSpec((1, HG_DIM), lambda h, b, s: (0, 0)),
                  pl.BlockSpec((N_CUM_F, CHUNK), lambda h, b, s: (0, 0)),
                  pl.BlockSpec((len(_LEVELS), CHUNK, CHUNK), lambda h, b, s: (0, 0, 0))] + p_ispecs,
        out_specs=[col(0), col(0),
                   pl.BlockSpec((None, None, nc, HG_DIM, HG_DIM), lambda h, b, s: (b, h, s, 0, 0))] + p_ospecs,
        out_shape=[jax.ShapeDtypeStruct((B, S, HG_WIDTH), F32),
                   jax.ShapeDtypeStruct((B, S, u_rows), BF16),
                   jax.ShapeDtypeStruct((B, HG_HEADS, S // CHUNK, HG_DIM, HG_DIM), F32)] + p_oshapes,
        input_output_aliases=p_alias,
        scratch_shapes=[pltpu.VMEM((HG_DIM, HG_DIM), F32)] + p_scratch,
        compiler_params=_params(("arbitrary", "arbitrary", "arbitrary")),
    )(proj_h, proj_h, proj_h, proj_h, lb_param, g_head, ts, _level_masks(), *p_args)
    return res[0], res[1], res[2], list(res[3:])


def _hgrn_bwd(proj_h, o_h, du, states, lb_param, g_head, layer, name, phase=None):
    B, S, _ = proj_h.shape
    sb = _pick(S, (512, 256, 128, 64))
    nc = sb // CHUNK
    ns = S // sb
    ts, tst = _cum_matrices()

    def body(*refs):
        ins, outs, (dst,), p_in, p_out, p_sems = _split_refs(refs, 12, 6, 1, phase)
        q_ref, f_ref, i_ref, z_ref, o_ref, du_ref, sts_ref, lbp_ref, gh_ref, ts_ref, tst_ref, m_ref = ins
        dq_ref, df_ref, di_ref, dz_ref, dlb_ref, dgh_ref = outs
        h_id, b_id, s_id = pl.program_id(0), pl.program_id(1), pl.program_id(2)
        _hosted_start(phase, p_in, p_out, p_sems, (h_id == 0) & (b_id == 0) & (s_id == 0))

        @pl.when(s_id == 0)
        def _():
            dst[...] = jnp.zeros_like(dst)

        @pl.when((b_id == 0) & (s_id == 0))
        def _():
            dlb_ref[...] = jnp.zeros_like(dlb_ref)

        @pl.when((h_id == 0) & (b_id == 0) & (s_id == 0))
        def _():
            dgh_ref[...] = jnp.zeros_like(dgh_ref)

        lt = _lb_terms(lbp_ref[...], layer)
        gh = gh_ref[...]
        tsv = ts_ref[...]
        tstv = tst_ref[...]
        work = []
        dgh = jnp.zeros((1, HG_DIM), F32)
        for ci in range(nc):
            rows = slice(ci * CHUNK, (ci + 1) * CHUNK)
            x = f_ref[rows, :]
            logf, k, u1, u2 = _gate_fwd(x, lt)
            q = q_ref[rows, :]
            o = o_ref[rows, :]
            z = z_ref[rows, :]
            dub = du_ref[rows, :]
            r = lax.rsqrt(jnp.mean(o * o, axis=-1, keepdims=True) + NORM_EPS)
            n = o * r
            sg = _silu(z)
            dz_ref[rows, :] = (dub * (n * gh) * _silu_grad(z)).astype(BF16)
            dgh = dgh + jnp.sum(dub * sg * n, axis=0, keepdims=True)
            dn = dub * sg * gh
            do = r * (dn - n * jnp.mean(dn * n, axis=-1, keepdims=True))
            v = i_ref[rows, :]
            w = _hg_intra_bwd(_silu(q), k, v, logf, do, tsv, m_ref)
            w.update(x=x, logf=logf, u1=u1, u2=u2, q=q, v=v, do=do)
            work.append(w)
        dgh_ref[...] += dgh
        cur = dst[...]
        ends = [None] * nc
        for ci in reversed(range(nc)):
            ends[ci] = cur
            cur = cur * work[ci]["dec"] + work[ci]["qd"]
        dst[...] = cur
        dlb = jnp.zeros((1, HG_DIM), F32)
        for ci in range(nc):
            rows = slice(ci * CHUNK, (ci + 1) * CHUNK)
            w = work[ci]
            dq, dk, dv, dg = _hg_state_bwd(w, w["v"], w["do"], sts_ref[ci], ends[ci], tstv)
            di_ref[rows, :] = dv.astype(BF16)
            dq_ref[rows, :] = (dq * _silu_grad(w["q"])).astype(BF16)
            logf = w["logf"]
            dlogf = dg - jnp.exp(logf) * dk
            w1 = jnp.exp(w["u1"] - logf)
            w2 = jnp.exp(w["u2"] - logf)
            df_ref[rows, :] = (dlogf * w2 * (1.0 / (1.0 + jnp.exp(w["x"])))).astype(BF16)
            dlb = dlb + jnp.sum(dlogf * (w1 * lt["dlb1"] - w2 * lt["dlb2"]), axis=0, keepdims=True)
        dlb_ref[...] += dlb
        _hosted_finish(phase, p_in, p_out, p_sems, (h_id == HG_HEADS - 1) & (b_id == B - 1) & (s_id == ns - 1))

    col = lambda base: pl.BlockSpec((None, sb, HG_DIM), lambda h, b, s: (b, ns - 1 - s, base + h))
    out_col = pl.BlockSpec((None, sb, HG_DIM), lambda h, b, s: (b, ns - 1 - s, h))
    dt = jax.ShapeDtypeStruct((B, S, HG_WIDTH), BF16)
    p_ispecs, p_ospecs, p_oshapes, p_alias, p_scratch, p_args = _host_phase(phase, 12, 6)
    res = pl.pallas_call(
        body, name=name,
        grid=(HG_HEADS, B, ns),
        in_specs=[col(0), col(HG_HEADS), col(2 * HG_HEADS), col(3 * HG_HEADS), col(0), col(0),
                  pl.BlockSpec((None, None, nc, HG_DIM, HG_DIM), lambda h, b, s: (b, h, ns - 1 - s, 0, 0)),
                  pl.BlockSpec((DEPTH, HG_DIM), lambda h, b, s: (0, h)),
                  pl.BlockSpec((1, HG_DIM), lambda h, b, s: (0, 0)),
                  pl.BlockSpec((N_CUM_F, CHUNK), lambda h, b, s: (0, 0)),
                  pl.BlockSpec((CHUNK, N_CUM), lambda h, b, s: (0, 0)),
                  pl.BlockSpec((len(_LEVELS), CHUNK, CHUNK), lambda h, b, s: (0, 0, 0))] + p_ispecs,
        out_specs=[out_col, out_col, out_col, out_col,
                   pl.BlockSpec((1, HG_DIM), lambda h, b, s: (0, h)),
                   pl.BlockSpec((1, HG_DIM), lambda h, b, s: (0, 0))] + p_ospecs,
        out_shape=[dt, dt, dt, dt, jax.ShapeDtypeStruct((1, HG_WIDTH), F32),
                   jax.ShapeDtypeStruct((1, HG_DIM), F32)] + p_oshapes,
        input_output_aliases=p_alias,
        scratch_shapes=[pltpu.VMEM((HG_DIM, HG_DIM), F32)] + p_scratch,
        compiler_params=_params(("arbitrary", "arbitrary", "arbitrary")),
    )(proj_h, proj_h, proj_h, proj_h, o_h, du, states, lb_param, g_head, ts, tst, _level_masks(), *p_args)
    return tuple(res[:6]) + (list(res[6:]),)


def _rope_tables(S):
    half = ATT_DIM // 2
    inv_freq = ROPE_THETA ** (-jnp.arange(half, dtype=F32) / half)
    ang = jnp.arange(S).astype(F32)[:, None] * inv_freq[None, :]
    cos = jnp.cos(ang)
    sin = jnp.sin(ang)
    cos = jnp.concatenate([cos, cos, cos, cos], axis=1)
    sin = jnp.concatenate([-sin, sin, -sin, sin], axis=1)
    return cos, sin


def _attn_common():
    lane = lax.broadcasted_iota(jnp.int32, (1, 2 * ATT_DIM), 1)
    first_half = (lane % ATT_DIM) < (ATT_DIM // 2)
    left = lane < ATT_DIM

    def swap(x):
        return jnp.where(first_half, pltpu.roll(x, 128 - ATT_DIM // 2, 1), pltpu.roll(x, ATT_DIM // 2, 1))

    def rope(x, cos, sin):
        return x * cos + swap(x) * sin

    def rope_bwd(dy, cos, sin):
        return dy * cos + swap(dy * sin)

    def dup(x):
        xs = pltpu.roll(x, ATT_DIM, 1)
        return [jnp.where(left, x, xs), jnp.where(left, xs, x)]

    return left, rope, rope_bwd, dup


def _attn_mask(i):
    r = lax.broadcasted_iota(jnp.int32, (ATT_BLOCK, 2 * ATT_BLOCK), 0)
    c = lax.broadcasted_iota(jnp.int32, (ATT_BLOCK, 2 * ATT_BLOCK), 1)
    return (c > r) & (c <= r + ATT_BLOCK) & ((c >= ATT_BLOCK) | (i > 0))


def _attn_probs(q128, kh, sink, mask):
    s = _dot(q128, kh, NT) * ATT_SCALE
    s = jnp.where(mask, s, NEG_INF)
    m = jnp.maximum(jnp.max(s, axis=-1, keepdims=True), sink)
    p = jnp.exp(s - m)
    es = jnp.exp(sink - m)
    inv = 1.0 / (jnp.sum(p, axis=-1, keepdims=True) + es)
    return p * inv, es * inv


_Z0 = (2 * ATT_WIDTH + 2 * KV_WIDTH - ATT_WIDTH) // 256


def _attn_fwd(proj_a, u, sinks_l, cos, sin, name, phase=None):
    B, S, _ = proj_a.shape
    nb = S // ATT_BLOCK

    def body(*refs):
        ins, (u_ref,), _, p_in, p_out, p_sems = _split_refs(refs, 13, 1, 0, phase)
        q_ref, kvc_ref, kvp_ref, z0, z1, z2, z3, cos_ref, sin_ref, cosp_ref, sinp_ref, sinks_ref, _ = ins
        i = pl.program_id(1)
        _hosted_start(phase, p_in, p_out, p_sems, (pl.program_id(0) == 0) & (i == 0))
        left, rope, _, dup = _attn_common()
        cos_c, sin_c = cos_ref[...], sin_ref[...]
        kvc = kvc_ref[...]
        kvp = kvp_ref[...]
        kw = jnp.concatenate([rope(kvp[:, :KV_WIDTH], cosp_ref[...], sinp_ref[...]),
                              rope(kvc[:, :KV_WIDTH], cos_c, sin_c)], axis=0)
        vw = jnp.concatenate([kvp[:, KV_WIDTH:], kvc[:, KV_WIDTH:]], axis=0)
        kd, vd = dup(kw), dup(vw)
        mask = _attn_mask(i)
        zs = (z0, z1, z2, z3)
        for pair in range(ATT_HEADS // 2):
            kvh = pair // 4
            cols = slice(128 * pair, 128 * (pair + 1))
            q128 = rope(q_ref[:, cols], cos_c, sin_c)
            out = jnp.zeros((ATT_BLOCK, 128), F32)
            for hh in range(2):
                lm = left if hh == 0 else jnp.logical_not(left)
                p, _ = _attn_probs(q128, jnp.where(lm, kd[kvh], 0.0), sinks_ref[2 * pair + hh], mask)
                out = out + _dot(p, jnp.where(lm, vd[kvh], 0.0), NN)
            z = zs[pair // 2][:, 128 * (pair % 2):128 * (pair % 2 + 1)]
            u_ref[:, cols] = (out * _silu(z)).astype(BF16)
        _hosted_finish(phase, p_in, p_out, p_sems, (pl.program_id(0) == B - 1) & (i == nb - 1))

    rowblk = lambda w, cb: pl.BlockSpec((None, ATT_BLOCK, w), lambda b, i: (b, i, cb))
    tab = pl.BlockSpec((ATT_BLOCK, 128), lambda b, i: (i, 0))
    tabp = pl.BlockSpec((ATT_BLOCK, 128), lambda b, i: (jnp.maximum(i - 1, 0), 0))
    p_ispecs, p_ospecs, p_oshapes, p_alias, p_scratch, p_args = _host_phase(phase, 13, 1)
    res = pl.pallas_call(
        body, name=name,
        grid=(B, nb),
        in_specs=[rowblk(ATT_WIDTH, 0), rowblk(256, 4),
                  pl.BlockSpec((None, ATT_BLOCK, 256), lambda b, i: (b, jnp.maximum(i - 1, 0), 4)),
                  rowblk(256, _Z0), rowblk(256, _Z0 + 1), rowblk(256, _Z0 + 2), rowblk(256, _Z0 + 3),
                  tab, tab, tabp, tabp,
                  pl.BlockSpec(memory_space=pltpu.SMEM),
                  pl.BlockSpec(memory_space=pl.ANY)] + p_ispecs,
        out_specs=[pl.BlockSpec((None, ATT_BLOCK, ATT_WIDTH), lambda b, i: (b, i, 1))] + p_ospecs,
        out_shape=[jax.ShapeDtypeStruct(u.shape, BF16)] + p_oshapes,
        input_output_aliases={12: 0, **p_alias},
        scratch_shapes=p_scratch,
        compiler_params=_params(("arbitrary", "arbitrary")),
    )(proj_a, proj_a, proj_a, proj_a, proj_a, proj_a, proj_a, cos, sin, cos, sin, sinks_l, u, *p_args)
    return res[0], list(res[1:])


def _attn_bwd(proj_a, du, sinks_l, cos, sin, name, phase=None):
    B, S, _ = proj_a.shape
    nb = S // ATT_BLOCK

    def body(*refs):
        ins, outs, (carry, sk_acc), p_in, p_out, p_sems = _split_refs(refs, 13, 4, 2, phase)
        q_ref, kvc_ref, kvp_ref, z0, z1, z2, z3, du_ref, cos_ref, sin_ref, cosp_ref, sinp_ref, sinks_ref = ins
        dq_ref, dkv_ref, dz_ref, dsk_ref = outs
        b_id, i = pl.program_id(0), pl.program_id(1)
        _hosted_start(phase, p_in, p_out, p_sems, (b_id == 0) & (i == 0))

        @pl.when((b_id == 0) & (i == 0))
        def _():
            sk_acc[...] = jnp.zeros_like(sk_acc)

        @pl.when(i == 0)
        def _():
            carry[...] = jnp.zeros_like(carry)

        @pl.when(i < nb)
        def _():
            left, rope, rope_bwd, dup = _attn_common()
            cos_c, sin_c = cos_ref[...], sin_ref[...]
            cos_p, sin_p = cosp_ref[...], sinp_ref[...]
            kvc = kvc_ref[...]
            kvp = kvp_ref[...]
            kw = jnp.concatenate([rope(kvp[:, :KV_WIDTH], cos_p, sin_p), rope(kvc[:, :KV_WIDTH], cos_c, sin_c)], axis=0)
            vw = jnp.concatenate([kvp[:, KV_WIDTH:], kvc[:, KV_WIDTH:]], axis=0)
            kd, vd = dup(kw), dup(vw)
            mask = _attn_mask(i)
            zs = (z0, z1, z2, z3)
            lane = lax.broadcasted_iota(jnp.int32, (1, 128), 1)
            dkd = [jnp.zeros((2 * ATT_BLOCK, 128), F32) for _ in range(2)]
            dvd = [jnp.zeros((2 * ATT_BLOCK, 128), F32) for _ in range(2)]
            sk = jnp.zeros((ATT_BLOCK, 128), F32)
            for pair in range(ATT_HEADS // 2):
                kvh = pair // 4
                cols = slice(128 * pair, 128 * (pair + 1))
                q128 = rope(q_ref[:, cols], cos_c, sin_c)
                lms = (left, jnp.logical_not(left))
                probs = []
                out = jnp.zeros((ATT_BLOCK, 128), F32)
                for hh in range(2):
                    p, ps = _attn_probs(q128, jnp.where(lms[hh], kd[kvh], 0.0), sinks_ref[2 * pair + hh], mask)
                    probs.append((p, ps))
                    out = out + _dot(p, jnp.where(lms[hh], vd[kvh], 0.0), NN)
                z = zs[pair // 2][:, 128 * (pair % 2):128 * (pair % 2 + 1)]
                du128 = du_ref[:, cols]
                dz_ref[:, cols] = (du128 * out * _silu_grad(z)).astype(BF16)
                do128 = du128 * _silu(z)
                dq128 = jnp.zeros((ATT_BLOCK, 128), F32)
                for hh in range(2):
                    p, ps = probs[hh]
                    kh = jnp.where(lms[hh], kd[kvh], 0.0)
                    vh = jnp.where(lms[hh], vd[kvh], 0.0)
                    dp = _dot(do128, vh, NT)
                    delta = jnp.sum(p * dp, axis=-1, keepdims=True)
                    ds = p * (dp - delta) * ATT_SCALE
                    sk = sk + jnp.where(lane == 2 * pair + hh, -ps * delta, 0.0)
                    dq128 = dq128 + _dot(ds, kh, NN)
                    dkd[kvh] = dkd[kvh] + jnp.where(lms[hh], _dot(ds, q128, TN), 0.0)
                    dvd[kvh] = dvd[kvh] + jnp.where(lms[hh], _dot(p, do128, TN), 0.0)
                dq_ref[:, cols] = rope_bwd(dq128, cos_c, sin_c).astype(BF16)
            sk_acc[...] += sk
            fold = lambda pr: jnp.where(left, pr[0] + pltpu.roll(pr[0], ATT_DIM, 1), pr[1] + pltpu.roll(pr[1], ATT_DIM, 1))
            dkw = fold(dkd)
            dvw = fold(dvd)
            prev = jnp.concatenate([rope_bwd(dkw[:ATT_BLOCK], cos_p, sin_p), dvw[:ATT_BLOCK]], axis=1)
            cur = jnp.concatenate([rope_bwd(dkw[ATT_BLOCK:], cos_c, sin_c), dvw[ATT_BLOCK:]], axis=1)
            dkv_ref[...] = (carry[...] + prev).astype(BF16)
            carry[...] = cur

        @pl.when(i == nb)
        def _():
            dkv_ref[...] = carry[...].astype(BF16)

        @pl.when((b_id == B - 1) & (i == nb))
        def _():
            dsk_ref[...] = jnp.sum(sk_acc[...], axis=0, keepdims=True)

        _hosted_finish(phase, p_in, p_out, p_sems, (b_id == B - 1) & (i == nb))

    cl = lambda i: jnp.minimum(i, nb - 1)
    pv = lambda i: jnp.maximum(jnp.minimum(i, nb - 1) - 1, 0)
    rowblk = lambda w, cb: pl.BlockSpec((None, ATT_BLOCK, w), lambda b, i: (b, cl(i), cb))
    tab = pl.BlockSpec((ATT_BLOCK, 128), lambda b, i: (cl(i), 0))
    tabp = pl.BlockSpec((ATT_BLOCK, 128), lambda b, i: (pv(i), 0))
    p_ispecs, p_ospecs, p_oshapes, p_alias, p_scratch, p_args = _host_phase(phase, 13, 4)
    res = pl.pallas_call(
        body, name=name,
        grid=(B, nb + 1),
        in_specs=[rowblk(ATT_WIDTH, 0), rowblk(256, 4),
                  pl.BlockSpec((None, ATT_BLOCK, 256), lambda b, i: (b, pv(i), 4)),
                  rowblk(256, _Z0), rowblk(256, _Z0 + 1), rowblk(256, _Z0 + 2), rowblk(256, _Z0 + 3),
                  rowblk(ATT_WIDTH, 1),
                  tab, tab, tabp, tabp,
                  pl.BlockSpec(memory_space=pltpu.SMEM)] + p_ispecs,
        out_specs=[rowblk(ATT_WIDTH, 0),
                   pl.BlockSpec((None, ATT_BLOCK, 256), lambda b, i: (b, jnp.maximum(i - 1, 0), 0)),
                   rowblk(ATT_WIDTH, 0),
                   pl.BlockSpec((1, 128), lambda b, i: (0, 0))] + p_ospecs,
        out_shape=[jax.ShapeDtypeStruct((B, S, ATT_WIDTH), BF16), jax.ShapeDtypeStruct((B, S, 256), BF16),
                   jax.ShapeDtypeStruct((B, S, ATT_WIDTH), BF16), jax.ShapeDtypeStruct((1, 128), F32)] + p_oshapes,
        input_output_aliases=p_alias,
        scratch_shapes=[pltpu.VMEM((ATT_BLOCK, 256), F32), pltpu.VMEM((ATT_BLOCK, 128), F32)] + p_scratch,
        compiler_params=_params(("arbitrary", "arbitrary")),
    )(proj_a, proj_a, proj_a, proj_a, proj_a, proj_a, proj_a, du, cos, sin, cos, sin, sinks_l, *p_args)
    return tuple(res[:4]) + (list(res[4:]),)


def _outproj_fwd(u2, w_out, x2, g_post, target2, name):
    T, D = x2.shape
    tm = _pick(T, (512, 256, 128))
    last = target2 is not None

    def body(u_ref, w_ref, x_ref, g_ref, *rest):
        y = lax.dot_general(u_ref[...], w_ref[...], (NN, ((), ())), preferred_element_type=F32)
        r = lax.rsqrt(jnp.mean(y * y, axis=-1, keepdims=True) + NORM_EPS)
        xn = x_ref[...] + (y * r) * g_ref[...]
        if last:
            t_ref, y_ref, dx_ref, loss_ref = rest
            err = xn - t_ref[...]
            dx_ref[...] = err * (1.0 / D)
            sq = err * err
            acc = sq[:, 0:128]
            for kk in range(1, D // 128):
                acc = acc + sq[:, 128 * kk:128 * (kk + 1)]
            part = jnp.sum(acc.reshape(tm // 8, 8, 128), axis=0) * (0.5 / D)

            @pl.when(pl.program_id(0) == 0)
            def _():
                loss_ref[...] = jnp.zeros_like(loss_ref)

            loss_ref[...] += part
        else:
            y_ref, xn_ref = rest
            xn_ref[...] = xn
        y_ref[...] = y

    row = pl.BlockSpec((tm, D), lambda i: (i, 0))
    in_specs = [pl.BlockSpec((tm, MIX_WIDTH), lambda i: (i, 0)),
                pl.BlockSpec((MIX_WIDTH, D), lambda i: (0, 0)), row,
                pl.BlockSpec((1, D), lambda i: (0, 0))]
    args = [u2, w_out, x2, g_post]
    out_specs = [row, row]
    out_shape = [jax.ShapeDtypeStruct((T, D), F32), jax.ShapeDtypeStruct((T, D), F32)]
    if last:
        in_specs.append(row)
        args.append(target2)
        out_specs.append(pl.BlockSpec((8, 128), lambda i: (0, 0)))
        out_shape.append(jax.ShapeDtypeStruct((8, 128), F32))
    return pl.pallas_call(
        body, name=name, grid=(T // tm,), in_specs=in_specs, out_specs=out_specs, out_shape=out_shape,
        compiler_params=_params(("arbitrary",)),
    )(*args)


def _postnorm_bwd(dxn2, y2, g_post, name):
    T, D = y2.shape
    tm = _pick(T, (512, 256, 128))
    nt = T // tm

    def body(dx_ref, y_ref, g_ref, dy_ref, dg_ref, acc):
        i = pl.program_id(0)

        @pl.when(i == 0)
        def _():
            acc[...] = jnp.zeros_like(acc)

        y = y_ref[...]
        dxn = dx_ref[...]
        r = lax.rsqrt(jnp.mean(y * y, axis=-1, keepdims=True) + NORM_EPS)
        n = y * r
        dn = dxn * g_ref[...]
        dy_ref[...] = (r * (dn - n * jnp.mean(dn * n, axis=-1, keepdims=True))).astype(BF16)
        acc[...] += jnp.sum((dxn * n).reshape(tm // 8, 8, D), axis=0)

        @pl.when(i == nt - 1)
        def _():
            dg_ref[...] = jnp.sum(acc[...], axis=0, keepdims=True)

    row = pl.BlockSpec((tm, D), lambda i: (i, 0))
    vec = pl.BlockSpec((1, D), lambda i: (0, 0))
    return pl.pallas_call(
        body, name=name, grid=(nt,), in_specs=[row, row, vec], out_specs=[row, vec],
        out_shape=[jax.ShapeDtypeStruct((T, D), BF16), jax.ShapeDtypeStruct((1, D), F32)],
        scratch_shapes=[pltpu.VMEM((8, D), F32)],
        compiler_params=_params(("arbitrary",)),
    )(dxn2, y2, g_post)


def _inproj_bwd(pieces, w_t, x2, dxn2, g_pre, name):
    T, D = x2.shape
    widths = [p.shape[1] for p in pieces]
    offs = [sum(widths[:i]) for i in range(len(pieces))]
    n_p = len(pieces)
    tm = _pick(T, (256, 128))
    nt = T // tm

    def body(*refs):
        w_ref, x_ref, dxn_ref, g_ref, dx_ref, dg_ref, acc = refs[n_p:]
        i = pl.program_id(0)

        @pl.when(i == 0)
        def _():
            acc[...] = jnp.zeros_like(acc)

        dh = jnp.zeros((tm, D), F32)
        for p in range(n_p):
            dh = dh + lax.dot_general(refs[p][...], w_ref[offs[p]:offs[p] + widths[p], :], (NN, ((), ())),
                                      preferred_element_type=F32)
        x = x_ref[...]
        r = lax.rsqrt(jnp.mean(x * x, axis=-1, keepdims=True) + NORM_EPS)
        n = x * r
        dn = dh * g_ref[...]
        dx_ref[...] = dxn_ref[...] + r * (dn - n * jnp.mean(dn * n, axis=-1, keepdims=True))
        acc[...] += jnp.sum((dh * n).reshape(tm // 8, 8, D), axis=0)

        @pl.when(i == nt - 1)
        def _():
            dg_ref[...] = jnp.sum(acc[...], axis=0, keepdims=True)

    row = pl.BlockSpec((tm, D), lambda i: (i, 0))
    vec = pl.BlockSpec((1, D), lambda i: (0, 0))
    return pl.pallas_call(
        body, name=name, grid=(nt,),
        in_specs=[pl.BlockSpec((tm, w), lambda i: (i, 0)) for w in widths]
        + [pl.BlockSpec((sum(widths), D), lambda i: (0, 0), pipeline_mode=pl.Buffered(1)), row, row, vec],
        out_specs=[row, vec],
        out_shape=[jax.ShapeDtypeStruct((T, D), F32), jax.ShapeDtypeStruct((1, D), F32)],
        scratch_shapes=[pltpu.VMEM((8, D), F32)],
        compiler_params=_params(("arbitrary",)),
    )(*pieces, w_t, x2, dxn2, g_pre)


def _step(x, target, g_pre, g_post, lb_param, g_head, sinks, shards=None, full=None):
    B, S, D = x.shape
    T = B * S
    dist = shards is not None
    if dist:
        a_loc, b_loc = shards
        ra, rb = a_loc.shape[1], b_loc.shape[1]
        side = _own_side_blocks()
        placed = lambda loc, nm: _place_own(loc, side, "place_" + nm)
        gather = lambda phase, nm: _run_phase(phase, nm)
        w_in0 = gather(_gather_ici_phase([a_loc[0]], [placed(a_loc[0], "in0")]), "gather_in0_ici")
        w_in0 = gather(_gather_d2d_phase(w_in0, [ra]), "gather_in0_d2d")[0]
        late_locs = [a_loc[1], b_loc[1], b_loc[0]]
        late_rs = [ra, rb, rb]
        late_full = [placed(a_loc[1], "in1"), placed(b_loc[1], "out1"), placed(b_loc[0], "out0")]
        w_in, w_out = [w_in0, None], [None, None]
    else:
        w_in, w_out = list(full[0]), list(full[1])
    cos, sin = _rope_tables(S)
    saved = []
    xs = x
    loss_part = None
    dxn = None
    for l in range(DEPTH):
        x2 = xs.reshape(T, D)
        host = dist and l == 0
        proj_h, proj_a, h = _inproj(x2, g_pre[l:l + 1], w_in[l], f"inproj{l}")
        proj_h = proj_h.reshape(B, S, N_H)
        proj_a = proj_a.reshape(B, S, N_A)
        o_h, u, states, got = _hgrn_fwd(proj_h, MIX_WIDTH, lb_param, g_head[l:l + 1], l, f"hgrn_fwd{l}",
                                        _gather_ici_phase(late_locs, late_full) if host else None)
        u, got = _attn_fwd(proj_a, u, sinks[l], cos, sin, f"attn_fwd{l}",
                           _gather_d2d_phase(got, late_rs) if host else None)
        if host:
            w_in[1], w_out[1], w_out[0] = got
        u2 = u.reshape(T, MIX_WIDTH)
        if l < DEPTH - 1:
            y, xn = _outproj_fwd(u2, w_out[l], x2, g_post[l:l + 1], None, f"outproj{l}")
            xn = xn.reshape(B, S, D)
        else:
            y, dxn, loss_part = _outproj_fwd(u2, w_out[l], x2, g_post[l:l + 1], target.reshape(T, D), f"outproj{l}")
            xn = None
        saved.append((x2, h, proj_h, proj_a, o_h, u2, states, y))
        xs = xn

    dw_in, dw_out = [None] * DEPTH, [None] * DEPTH
    dg_pre, dg_post, dlb, dg_head, dsinks = [], [], [], [], []
    for l in reversed(range(DEPTH)):
        x2, h, proj_h, proj_a, o_h, u2, states, y = saved[l]
        host = dist and l == 0
        dy, dgp = _postnorm_bwd(dxn, y, g_post[l:l + 1], f"postnorm_bwd{l}")
        dw_out[l] = _mm_tn([u2], dy, f"wgrad_out{l}")
        du = _mm_nt(dy, w_out[l], f"dgrad_out{l}").reshape(B, S, MIX_WIDTH)
        if host:
            early = [dw_in[1], dw_out[1], dw_out[0]]
        dqh, dfh, dih, dzh, dlb_l, dgh, got = _hgrn_bwd(
            proj_h, o_h, du, states, lb_param, g_head[l:l + 1], l, f"hgrn_bwd{l}",
            _reduce_d2d_phase(early, late_rs) if host else None)
        if host:
            parts = [_pair_sum(g, r, side, f"pair_sum{i}") for i, (g, r) in enumerate(zip(early, got))]
        dqa, dkv, dza, dsk, got = _attn_bwd(proj_a, du, sinks[l], cos, sin, f"attn_bwd{l}",
                                            _reduce_ici_phase(parts) if host else None)
        if host:
            dw_in[1], dw_out[1], dw_out[0] = [_chip_sum(p, r, f"chip_sum{i}")
                                              for i, (p, r) in enumerate(zip(parts, got))]
        dproj = [p.reshape(T, p.shape[-1]) for p in (dqh, dfh, dih, dzh, dqa, dkv, dza)]
        dw_in[l] = _mm_tn(dproj, h, f"wgrad_in{l}")
        dxn, dgpre = _inproj_bwd(dproj, w_in[l], x2, dxn, g_pre[l:l + 1], f"inproj_bwd{l}")
        dg_pre.append(dgpre)
        dg_post.append(dgp)
        dlb.append(dlb_l)
        dg_head.append(dgh)
        dsinks.append(dsk)
    if dist:
        got = _run_phase(_reduce_d2d_phase([dw_in[0]], [ra]), "reduce_in0_d2d")
        part = _pair_sum(dw_in[0], got[0], side, "pair_sum_in0")
        got = _run_phase(_reduce_ici_phase([part]), "reduce_in0_ici")
        dw_in[0] = _chip_sum(part, got[0], "chip_sum_in0")
    rev = lambda lst: jnp.concatenate(lst[::-1], axis=0)
    return (loss_part, dxn.reshape(B, S, D), jnp.stack(dw_in), jnp.stack(dw_out),
            rev(dg_pre), rev(dg_post), rev(dlb), rev(dg_head), rev(dsinks))


def _me_and_peers():
    x, y, c = lax.axis_index("x"), lax.axis_index("y"), lax.axis_index("c")
    me = 4 * x + 2 * y + c
    peers = []
    for k in range(1, N_DEV):
        px = 1 - x if k & 4 else x
        py = 1 - y if k & 2 else y
        pc = 1 - c if k & 1 else c
        peers.append(((px, py, pc), 4 * px + 2 * py + pc))
    return me, peers


class _Phase:
    def __init__(self, arrays, out_shapes, aliases, n_send, build):
        self.arrays, self.out_shapes, self.aliases = list(arrays), list(out_shapes), dict(aliases)
        self.n_send, self.build = n_send, build

    def scratch(self):
        return [pltpu.SemaphoreType.DMA((self.n_send,)), pltpu.SemaphoreType.DMA((self.n_send,))]

    def _copies(self, in_refs, out_refs, sems, arrivals):
        send_sems, recv_sems = sems
        sends, recvs = self.build(in_refs, out_refs)
        assert len(sends) == self.n_send == len(recvs)
        out = [pltpu.make_async_remote_copy(src_ref=s, dst_ref=d, send_sem=send_sems.at[i], recv_sem=recv_sems.at[i],
                                            device_id=dev, device_id_type=MESH) for i, (s, d, dev) in enumerate(sends)]
        inc = [pltpu.make_async_remote_copy(src_ref=s, dst_ref=r, send_sem=send_sems.at[i], recv_sem=recv_sems.at[i],
                                            device_id=dev, device_id_type=MESH)
               for i, ((s, _, dev), r) in enumerate(zip(sends, recvs))] if arrivals else []
        return out, inc

    def start(self, in_refs, out_refs, sems):
        out, _ = self._copies(in_refs, out_refs, sems, False)
        for cp in out:
            cp.start()

    def finish(self, in_refs, out_refs, sems):
        out, inc = self._copies(in_refs, out_refs, sems, True)
        for cp in inc:
            cp.wait_recv()
        for cp in out:
            cp.wait_send()


_ANY = pl.BlockSpec(memory_space=pl.ANY)


def _host_phase(phase, n_in, n_out):
    if phase is None:
        return [], [], [], {}, [], []
    aliases = {n_in + i: n_out + o for i, o in phase.aliases.items()}
    return ([_ANY] * len(phase.arrays), [_ANY] * len(phase.out_shapes), phase.out_shapes, aliases, phase.scratch(),
            phase.arrays)


def _split_refs(refs, n_in, n_out, n_scr, phase):
    pi = len(phase.arrays) if phase else 0
    po = len(phase.out_shapes) if phase else 0
    a = n_in + pi
    b = a + n_out + po
    return (refs[:n_in], refs[a:a + n_out], refs[b:b + n_scr], refs[n_in:a], refs[a + n_out:b], refs[b + n_scr:])


def _hosted_start(phase, p_in, p_out, p_sems, first):
    if phase is not None:
        @pl.when(first)
        def _():
            phase.start(p_in, p_out, p_sems)


def _hosted_finish(phase, p_in, p_out, p_sems, last):
    if phase is not None:
        @pl.when(last)
        def _():
            phase.finish(p_in, p_out, p_sems)


def _run_phase(phase, name):
    n_in, n_out = len(phase.arrays), len(phase.out_shapes)

    def body(*refs):
        phase.start(refs[:n_in], refs[n_in:n_in + n_out], refs[n_in + n_out:])
        phase.finish(refs[:n_in], refs[n_in:n_in + n_out], refs[n_in + n_out:])

    return pl.pallas_call(
        body, name=name, in_specs=[_ANY] * n_in, out_specs=[_ANY] * n_out,
        out_shape=phase.out_shapes, input_output_aliases=phase.aliases, scratch_shapes=phase.scratch(),
        compiler_params=pltpu.CompilerParams(has_side_effects=True),
    )(*phase.arrays)


def _mesh_place():
    x, y, c = lax.axis_index("x"), lax.axis_index("y"), lax.axis_index("c")
    chips = [(x, y), (1 - x, y), (x, 1 - y), (1 - x, 1 - y)]
    num = lambda chip, core: 4 * chip[0] + 2 * chip[1] + core
    return c, chips, num


def _own_side_blocks():
    c, chips, num = _mesh_place()
    return jnp.stack([num(ch, c) for ch in chips]).astype(jnp.int32)


def _rows(ref, r, dev):
    return ref.at[pl.ds(pl.multiple_of(dev * r, 16), r), :]


def _place_own(loc, blocks, name):
    r, D = loc.shape
    tr = _pick(r, (400, 256, 200, 128, 64, 16))

    def body(idx_ref, l_ref, o_ref):
        del idx_ref
        o_ref[...] = l_ref[...]

    return pl.pallas_call(
        body, name=name,
        grid_spec=pltpu.PrefetchScalarGridSpec(
            num_scalar_prefetch=1, grid=(r // tr,),
            in_specs=[pl.BlockSpec((tr, D), lambda i, idx: (i, 0))],
            out_specs=pl.BlockSpec((tr, D), lambda i, idx: (idx[0] * (r // tr) + i, 0))),
        out_shape=jax.ShapeDtypeStruct((N_DEV * r, D), loc.dtype),
        compiler_params=_params(("arbitrary",)),
    )(blocks, loc)


def _gather_ici_phase(locs, fulls):
    rs = [a.shape[0] for a in locs]
    n = len(locs)

    def build(ins, outs):
        c, chips, num = _mesh_place()
        me = num(chips[0], c)
        targets = [((*chips[0], 1 - c), num(chips[0], 1 - c))] + [((*ch, c), num(ch, c)) for ch in chips[1:]]
        sends, recvs = [], []
        for dev, dnum in targets:
            for i, r in enumerate(rs):
                sends.append((ins[i], _rows(outs[i], r, me), dev))
                recvs.append(_rows(outs[i], r, dnum))
        return sends, recvs

    shapes = [jax.ShapeDtypeStruct(a.shape, a.dtype) for a in fulls]
    return _Phase(list(locs) + list(fulls), shapes, {n + i: i for i in range(n)}, 4 * n, build)


def _gather_d2d_phase(fulls, rs):
    def build(ins, outs):
        c, chips, num = _mesh_place()
        sib = (*chips[0], 1 - c)
        sends, recvs = [], []
        for ch in chips[1:]:
            for i, r in enumerate(rs):
                blk = _rows(outs[i], r, num(ch, c))
                sends.append((blk, blk, sib))
                recvs.append(_rows(outs[i], r, num(ch, 1 - c)))
        return sends, recvs

    shapes = [jax.ShapeDtypeStruct(a.shape, a.dtype) for a in fulls]
    return _Phase(fulls, shapes, {i: i for i in range(len(fulls))}, 3 * len(fulls), build)


def _reduce_d2d_phase(grads, rs):
    def build(ins, outs):
        c, chips, num = _mesh_place()
        sib = (*chips[0], 1 - c)
        sends, recvs = [], []
        for j, ch in enumerate(chips):
            for i, r in enumerate(rs):
                sends.append((_rows(ins[i], r, num(ch, 1 - c)), outs[i].at[j], sib))
                recvs.append(outs[i].at[j])
        return sends, recvs

    shapes = [jax.ShapeDtypeStruct((4, r, g.shape[1]), g.dtype) for g, r in zip(grads, rs)]
    return _Phase(grads, shapes, {}, 4 * len(grads), build)


def _reduce_ici_phase(parts):
    def build(ins, outs):
        c, chips, _ = _mesh_place()
        sends, recvs = [], []
        for t in range(1, 4):
            for i in range(len(parts)):
                sends.append((ins[i].at[t], outs[i].at[t - 1], (*chips[t], c)))
                recvs.append(outs[i].at[t - 1])
        return sends, recvs

    shapes = [jax.ShapeDtypeStruct((3,) + p.shape[1:], p.dtype) for p in parts]
    return _Phase(parts, shapes, {}, 3 * len(parts), build)


def _pair_sum(g, got, blocks, name):
    n, r, D = got.shape
    tr = _pick(r, (400, 256, 200, 128, 64, 16))

    def body(idx_ref, g_ref, r_ref, o_ref):
        del idx_ref
        o_ref[...] = (g_ref[...].astype(F32) + r_ref[...].astype(F32)).astype(o_ref.dtype)

    blk = pl.BlockSpec((None, tr, D), lambda j, i, idx: (j, i, 0))
    return pl.pallas_call(
        body, name=name,
        grid_spec=pltpu.PrefetchScalarGridSpec(
            num_scalar_prefetch=1, grid=(n, r // tr),
            in_specs=[pl.BlockSpec((tr, D), lambda j, i, idx: (idx[j] * (r // tr) + i, 0)), blk],
            out_specs=blk),
        out_shape=jax.ShapeDtypeStruct(got.shape, got.dtype),
        compiler_params=_params(("arbitrary", "arbitrary")),
    )(blocks, g, got)


def _chip_sum(p, r, name):
    _, R, D = p.shape
    tr = _pick(R, (400, 256, 200, 128, 64, 16))

    def body(p_ref, r_ref, o_ref):
        acc = p_ref[...].astype(F32)
        for t in range(3):
            acc = acc + r_ref[t].astype(F32)
        o_ref[...] = acc

    return pl.pallas_call(
        body, name=name, grid=(R // tr,),
        in_specs=[pl.BlockSpec((None, tr, D), lambda i: (0, i, 0)), pl.BlockSpec((3, tr, D), lambda i: (0, i, 0))],
        out_specs=pl.BlockSpec((tr, D), lambda i: (i, 0)), out_shape=jax.ShapeDtypeStruct((R, D), F32),
        compiler_params=_params(("parallel",)))(p, r)


def _allreduce_small(vec):
    R, C = vec.shape

    def body(v_ref, o_ref, buf, send_sems, recv_sems):
        me, peers = _me_and_peers()
        buf[me] = v_ref[...]
        sends = []
        for k, (pid, _) in enumerate(peers):
            cp = pltpu.make_async_remote_copy(src_ref=v_ref, dst_ref=buf.at[me], send_sem=send_sems.at[k],
                                              recv_sem=recv_sems.at[k], device_id=pid, device_id_type=MESH)
            cp.start()
            sends.append(cp)
        for k, (pid, pnum) in enumerate(peers):
            pltpu.make_async_remote_copy(src_ref=v_ref, dst_ref=buf.at[pnum], send_sem=send_sems.at[k],
                                         recv_sem=recv_sems.at[k], device_id=pid, device_id_type=MESH).wait_recv()
        for cp in sends:
            cp.wait_send()
        acc = buf[0]
        for d in range(1, N_DEV):
            acc = acc + buf[d]
        o_ref[...] = acc

    vm = pl.BlockSpec(memory_space=pltpu.VMEM)
    return pl.pallas_call(
        body, name="allreduce_small",
        in_specs=[vm], out_specs=vm,
        out_shape=jax.ShapeDtypeStruct((R, C), F32),
        scratch_shapes=[pltpu.VMEM((N_DEV, R, C), F32), pltpu.SemaphoreType.DMA((N_DEV - 1,)),
                        pltpu.SemaphoreType.DMA((N_DEV - 1,))],
        compiler_params=pltpu.CompilerParams(has_side_effects=True),
    )(vec)


def _adamw(w, g, m, v, name):
    R, C = w.shape
    tr = _pick(R, (256, 128, 64, 32, 16, 8)) if R >= 8 else R
    c1 = 1.0 - ADAM_B1 ** ADAM_STEP
    c2 = 1.0 - ADAM_B2 ** ADAM_STEP

    def body(w_ref, g_ref, m_ref, v_ref, d_ref, mo_ref, vo_ref):
        gg = g_ref[...]
        mn = ADAM_B1 * m_ref[...] + (1.0 - ADAM_B1) * gg
        vn = ADAM_B2 * v_ref[...] + (1.0 - ADAM_B2) * (gg * gg)
        d_ref[...] = -ADAM_LR * ((mn / c1) / (jnp.sqrt(vn / c2) + ADAM_EPS) + ADAM_WD * w_ref[...])
        mo_ref[...] = mn
        vo_ref[...] = vn

    blk = pl.BlockSpec((tr, C), lambda i: (i, 0))
    sh = jax.ShapeDtypeStruct((R, C), F32)
    return pl.pallas_call(
        body, name=name, grid=(R // tr,), in_specs=[blk] * 4, out_specs=[blk] * 3, out_shape=[sh] * 3,
        compiler_params=_params(("parallel",)),
    )(w, g, m, v)


def _lb_param_grad(lb_param, dlb):
    L, C = lb_param.shape

    def body(p_ref, d_ref, o_ref):
        lbp = p_ref[...]
        d = d_ref[...]
        mx = jnp.max(lbp, axis=0, keepdims=True)
        e = jnp.exp(lbp - mx)
        p = e / jnp.sum(e, axis=0, keepdims=True)
        tot = jnp.sum(d, axis=0, keepdims=True)
        dps = []
        rest = tot
        for j in range(L):
            dps.append(rest - tot if j == 0 else rest)
            rest = rest - d[j:j + 1]
        dp = jnp.concatenate(dps, axis=0)
        o_ref[...] = p * (dp - jnp.sum(p * dp, axis=0, keepdims=True))

    vm = pl.BlockSpec(memory_space=pltpu.VMEM)
    return pl.pallas_call(body, name="lb_param_grad", in_specs=[vm, vm], out_specs=vm,
                          out_shape=jax.ShapeDtypeStruct((L, C), F32))(lb_param, dlb)


def _pack_small(loss_part, dg_pre, dg_post, dlb, dg_head, dsinks):
    pad8 = lambda a: jnp.pad(a.reshape(-1, 128), ((0, 8 - DEPTH), (0, 0)))
    rows = [dg_pre.reshape(-1, 128), dg_post.reshape(-1, 128), dlb.reshape(-1, 128), pad8(dg_head), pad8(dsinks),
            loss_part]
    return jnp.concatenate(rows, axis=0)


def _unpack_small(vec):
    n = DEPTH * D_MODEL // 128
    o = 0
    dg_pre = vec[o:o + n].reshape(DEPTH, D_MODEL); o += n
    dg_post = vec[o:o + n].reshape(DEPTH, D_MODEL); o += n
    dlb = vec[o:o + n].reshape(DEPTH, HG_WIDTH); o += n
    dg_head = vec[o:o + DEPTH]; o += 8
    dsinks = vec[o:o + DEPTH, :ATT_HEADS]; o += 8
    loss = jnp.sum(vec[o:o + 8])
    return loss, dg_pre, dg_post, dlb, dg_head, dsinks


def kernel(x, w_in, w_out, g_pre, g_post, lb_param, g_head, sinks, loss_target, m_w_in, m_w_out, m_g_pre, m_g_post, m_lb_param, m_g_head, m_sinks, v_w_in, v_w_out, v_g_pre, v_g_post, v_lb_param, v_g_head, v_sinks):
    L, D, nloc = w_in.shape
    w_in_t_loc = jnp.swapaxes(w_in, 1, 2).astype(BF16)
    (loss_part, dx, gw_in_t, gw_out, dg_pre, dg_post, dlb, dg_head, dsinks) = _step(
        x, loss_target, g_pre, g_post, lb_param, g_head, sinks, shards=(w_in_t_loc, w_out.astype(BF16)))
    gw_in = jnp.swapaxes(gw_in_t, 1, 2)

    small = _allreduce_small(_pack_small(loss_part, dg_pre, dg_post, dlb, dg_head, dsinks))
    loss, gg_pre, gg_post, gdlb, gg_head, gsinks = _unpack_small(small)
    glb = _lb_param_grad(lb_param, gdlb)

    grads = [gw_in, gw_out, gg_pre, gg_post, glb, gg_head, gsinks]
    ws = [w_in, w_out, g_pre, g_post, lb_param, g_head, sinks]
    ms = [m_w_in, m_w_out, m_g_pre, m_g_post, m_lb_param, m_g_head, m_sinks]
    vs = [v_w_in, v_w_out, v_g_pre, v_g_post, v_lb_param, v_g_head, v_sinks]
    names = ["w_in", "w_out", "g_pre", "g_post", "lb_param", "g_head", "sinks"]
    deltas, new_m, new_v = [], [], []
    for w, g, m, v, nm in zip(ws, grads, ms, vs, names):
        sh = w.shape
        two = lambda a: a.reshape(-1, sh[-1])
        d, mn, vn = _adamw(two(w), two(g), two(m), two(v), "adamw_" + nm)
        deltas.append(d.reshape(sh))
        new_m.append(mn.reshape(sh))
        new_v.append(vn.reshape(sh))
    return (loss, dx, *grads, *deltas, *new_m, *new_v)
```

```python
import functools
import math

import numpy as np
import jax
import jax.numpy as jnp
from jax import lax
from jax.experimental import pallas as pl
from jax.experimental.pallas import tpu as pltpu

F32 = jnp.float32
BF16 = jnp.bfloat16

D_MODEL = 1024
DEPTH = 2
HG_HEADS = 8
HG_DIM = 128
HG_WIDTH = HG_HEADS * HG_DIM
CHUNK = 64
ATT_HEADS = 16
ATT_DIM = 64
ATT_WIDTH = ATT_HEADS * ATT_DIM
KV_WIDTH = 128
ATT_BLOCK = 128
ATT_SCALE = 1.0 / math.sqrt(ATT_DIM)
ROPE_THETA = 10000.0
NORM_EPS = 1e-6
NEG_INF = -1e30
LB_FLOOR = 1e-20
N_H = 4 * HG_WIDTH
N_A = 2 * ATT_WIDTH + 2 * KV_WIDTH
IN_WIDTH = N_H + N_A
MIX_WIDTH = HG_WIDTH + ATT_WIDTH

ADAM_LR = 0.001
ADAM_B1 = 0.9
ADAM_B2 = 0.999
ADAM_EPS = 1e-08
ADAM_WD = 0.01
ADAM_STEP = 10

N_DEV = 8
MESH = pl.DeviceIdType.MESH
VMEM_LIMIT = 56 * 1024 * 1024

NN = ((1,), (0,))
NT = ((1,), (1,))
TN = ((0,), (0,))


def _dot(a, b, dims):
    return lax.dot_general(a.astype(BF16), b.astype(BF16), (dims, ((), ())), preferred_element_type=F32)


def _params(sem=None, **kw):
    return pltpu.CompilerParams(dimension_semantics=sem, vmem_limit_bytes=VMEM_LIMIT, **kw)


def _sigmoid(x):
    return 1.0 / (1.0 + jnp.exp(-x))


def _silu(x):
    return x * _sigmoid(x)


def _silu_grad(x):
    s = _sigmoid(x)
    return s * (1.0 + x * (1.0 - s))


def _pick(n, prefs):
    for p in prefs:
        if n % p == 0:
            return p
    return n


def _inproj(x2, g, w, name):
    T, D = x2.shape
    tm = _pick(T, (256, 128))
    nchunk = 1024

    def body(x_ref, g_ref, w_ref, oh_ref, oa_ref, h_ref):
        x = x_ref[...]
        r = lax.rsqrt(jnp.mean(x * x, axis=-1, keepdims=True) + NORM_EPS)
        h = ((x * r) * g_ref[...]).astype(BF16)
        h_ref[...] = h
        for j in range(0, N_H, nchunk):
            oh_ref[:, j:j + nchunk] = lax.dot_general(h, w_ref[j:j + nchunk, :], (NT, ((), ())),
                                                      preferred_element_type=F32)
        for j in range(0, N_A, N_A // 2):
            oa_ref[:, j:j + N_A // 2] = lax.dot_general(h, w_ref[N_H + j:N_H + j + N_A // 2, :], (NT, ((), ())),
                                                        preferred_element_type=F32)

    row = lambda w_: pl.BlockSpec((tm, w_), lambda i: (i, 0))
    return pl.pallas_call(
        body, name=name,
        grid=(T // tm,),
        in_specs=[row(D), pl.BlockSpec((1, D), lambda i: (0, 0)),
                  pl.BlockSpec((IN_WIDTH, D), lambda i: (0, 0), pipeline_mode=pl.Buffered(1))],
        out_specs=[row(N_H), row(N_A), row(D)],
        out_shape=[jax.ShapeDtypeStruct((T, N_H), F32), jax.ShapeDtypeStruct((T, N_A), F32),
                   jax.ShapeDtypeStruct((T, D), BF16)],
        compiler_params=_params(("parallel",)),
    )(x2, g, w)


def _mm_nt(a, b, name, out_dtype=F32):
    M, K = a.shape
    N = b.shape[0]
    tm = _pick(M, (512, 256, 128))

    def body(a_ref, b_ref, o_ref):
        o_ref[...] = lax.dot_general(a_ref[...], b_ref[...], (NT, ((), ())),
                                     preferred_element_type=F32).astype(out_dtype)

    return pl.pallas_call(
        body, name=name,
        grid=(M // tm,),
        in_specs=[pl.BlockSpec((tm, K), lambda i: (i, 0)),
                  pl.BlockSpec((N, K), lambda i: (0, 0), pipeline_mode=pl.Buffered(1))],
        out_specs=pl.BlockSpec((tm, N), lambda i: (i, 0)),
        out_shape=jax.ShapeDtypeStruct((M, N), out_dtype),
        compiler_params=_params(("parallel",)),
    )(a, b)


def _mm_tn(pieces, b, name, out_dtype=BF16):
    T, m = b.shape
    tn = 256
    counts = [p.shape[1] // tn for p in pieces]
    starts = [sum(counts[:i]) for i in range(len(pieces))]
    n_p = len(pieces)

    def body(*refs):
        b_ref, o_ref = refs[n_p], refs[n_p + 1]
        i = pl.program_id(0)
        for p in range(n_p):
            @pl.when((i >= starts[p]) & (i < starts[p] + counts[p]))
            def _(p=p):
                o_ref[...] = lax.dot_general(refs[p][...], b_ref[...], (TN, ((), ())),
                                             preferred_element_type=F32).astype(out_dtype)

    piece_spec = lambda s, c: pl.BlockSpec((T, tn), lambda i: (0, jnp.clip(i - s, 0, c - 1)))
    return pl.pallas_call(
        body, name=name,
        grid=(sum(counts),),
        in_specs=[piece_spec(s, c) for s, c in zip(starts, counts)]
        + [pl.BlockSpec((T, m), lambda i: (0, 0), pipeline_mode=pl.Buffered(1))],
        out_specs=pl.BlockSpec((tn, m), lambda i: (i, 0)),
        out_shape=jax.ShapeDtypeStruct((sum(counts) * tn, m), out_dtype),
        compiler_params=_params(("arbitrary",)),
    )(*pieces, b)


_LEVELS = (0, 1, 2, 4, 8, 16, 32)
_CUM_L = (2, 4, 8, 16, 32, 64)
_ALL_KINDS = tuple(("c", L) for L in _CUM_L) + tuple(("r", L) for L in _CUM_L)
_MXU_KINDS = (("c", 2), ("c", 4), ("c", CHUNK), ("r", 2), ("r", 4))
N_CUM = len(_ALL_KINDS) * CHUNK
N_CUM_F = len(_MXU_KINDS) * CHUNK


def _cum_matrices():
    t = np.arange(CHUNK)[:, None]
    r = np.arange(CHUNK)[None, :]

    def mat(kind):
        c, L = kind
        return ((r // L == t // L) & ((r <= t) if c == "c" else (r > t))).astype(np.float32)

    fwd = np.concatenate([mat(kd) for kd in _MXU_KINDS], axis=0)
    full = np.concatenate([mat(kd) for kd in _ALL_KINDS], axis=0)
    return jnp.asarray(fwd, BF16), jnp.asarray(full.T.copy(), BF16)


def _level_masks():
    t = np.arange(CHUNK)[:, None]
    s = np.arange(CHUNK)[None, :]
    ms = []
    for L in _LEVELS:
        if L == 0:
            ms.append(t == s)
        else:
            ms.append((t // (2 * L) == s // (2 * L)) & ((t // L) % 2 == 1) & ((s // L) % 2 == 0))
    return jnp.asarray(np.stack(ms).astype(np.float32))


def _split3(x):
    hi = x.astype(BF16)
    r1 = x - hi.astype(F32)
    mid = r1.astype(BF16)
    lo = (r1 - mid.astype(F32)).astype(BF16)
    return hi, mid, lo


def _cum3(ts, x):
    hi, mid, lo = _split3(x)
    d = lambda p: lax.dot_general(ts, p, (NN, ((), ())), preferred_element_type=F32)
    return d(hi) + d(mid) + d(lo)


def _lb_terms(lbp, layer):
    mx = jnp.max(lbp, axis=0, keepdims=True)
    e = jnp.exp(lbp - mx)
    p = e / jnp.sum(e, axis=0, keepdims=True)
    cum = p[0:1]
    for j in range(1, layer + 1):
        cum = cum + p[j:j + 1]
    lb = cum - p[0:1]
    lbf = jnp.maximum(lb, LB_FLOOR)
    return dict(lb=lb, a=jnp.log(lbf), c=jnp.log(1.0 - lb), one_m=1.0 - lb, kcorr=lb - lbf,
                dlb1=jnp.where(lb > LB_FLOOR, 1.0 / lbf, 0.0), dlb2=1.0 / (1.0 - lb))


def _gate_fwd(x, lt):
    ls = jnp.minimum(x, 0.0) - jnp.log(1.0 + jnp.exp(-jnp.abs(x)))
    u1 = lt["a"]
    u2 = lt["c"] + ls
    mx = jnp.maximum(u1, u2)
    logf = mx + jnp.log(1.0 + jnp.exp(-jnp.abs(u1 - u2)))
    k = lt["one_m"] * (1.0 / (1.0 + jnp.exp(x))) + lt["kcorr"]
    return logf, k, u1, u2


def _chunk_cums(ts, g):
    cs = _cum3(ts, g)
    out = {kind: cs[CHUNK * i:CHUNK * (i + 1)] for i, kind in enumerate(_MXU_KINDS)}
    b = out[("c", CHUNK)]
    last = [jnp.broadcast_to(b[8 * r + 7:8 * r + 8, :], (8, HG_DIM)) for r in range(CHUNK // 8)]
    zero = jnp.zeros((8, HG_DIM), F32)
    for L in (8, 16, 32):
        nb = L // 8
        before = [last[(r // nb) * nb - 1] if r >= nb else zero for r in range(CHUNK // 8)]
        end = [last[(r // nb) * nb + nb - 1] for r in range(CHUNK // 8)]
        out[("c", L)] = b - jnp.concatenate(before, axis=0)
        out[("r", L)] = jnp.concatenate(end, axis=0) - b
    out[("r", CHUNK)] = jnp.broadcast_to(b[CHUNK - 1:CHUNK, :], (CHUNK, HG_DIM)) - b
    return out


def _level_factors(cums, g, L):
    if L == 0:
        return None, None
    if L == 1:
        return jnp.exp(g), None
    return jnp.exp(cums[("c", L)]), jnp.exp(cums[("r", L)])


def _mul(a, e):
    return a if e is None else a * e


def _hg_intra_fwd(qf, k, v, g, ts, m_ref):
    cums = _chunk_cums(ts, g)
    amat = jnp.zeros((CHUNK, CHUNK), F32)
    for li, L in enumerate(_LEVELS):
        eq, ek = _level_factors(cums, g, L)
        amat = amat + _dot(_mul(qf, eq), _mul(k, ek), NT) * m_ref[li]
    b = cums[("c", CHUNK)]
    kv = _dot(v, k * jnp.exp(cums[("r", CHUNK)]), TN)
    return _dot(amat, v, NN), qf * jnp.exp(b), jnp.exp(b[CHUNK - 1:CHUNK, :]), kv


def _hg_intra_bwd(qf, k, v, g, do, ts, m_ref):
    cums = _chunk_cums(ts, g)
    dcs = {}
    da = _dot(do, v, NT)
    dq = jnp.zeros((CHUNK, HG_DIM), F32)
    dk = jnp.zeros((CHUNK, HG_DIM), F32)
    dg = jnp.zeros((CHUNK, HG_DIM), F32)
    amat = jnp.zeros((CHUNK, CHUNK), F32)
    for li, L in enumerate(_LEVELS):
        eq, ek = _level_factors(cums, g, L)
        ql = _mul(qf, eq)
        kl = _mul(k, ek)
        m = m_ref[li]
        amat = amat + _dot(ql, kl, NT) * m
        dal = da * m
        dql = _dot(dal, kl, NN)
        dkl = _dot(dal, ql, TN)
        dq = dq + _mul(dql, eq)
        dk = dk + _mul(dkl, ek)
        if L == 1:
            dg = dg + dql * ql
        elif L > 1:
            dcs[("c", L)] = dql * ql
            dcs[("r", L)] = dkl * kl
    b = cums[("c", CHUNK)]
    e64 = jnp.exp(b)
    er64 = jnp.exp(cums[("r", CHUNK)])
    qb = qf * e64
    return dict(dq=dq, dk=dk, dv=_dot(amat, do, TN), dg=dg, dcs=dcs, e64=e64, er64=er64, qb=qb, kst=k * er64,
                dec=jnp.exp(b[CHUNK - 1:CHUNK, :]), qd=_dot(do, qb, TN))


def _hg_state_bwd(w, v, do, st, dst, tst):
    dqb = _dot(do, st, NN)
    dkst = _dot(v, dst, NN)
    dq = w["dq"] + dqb * w["e64"]
    dk = w["dk"] + dkst * w["er64"]
    dv = w["dv"] + _dot(w["kst"], dst, NT)
    dtot = jnp.sum(dst * st, axis=0, keepdims=True) * w["dec"]
    trow = lax.broadcasted_iota(jnp.int32, (CHUNK, 1), 0)
    dcs = dict(w["dcs"])
    dcs[("c", CHUNK)] = dqb * w["qb"] + jnp.where(trow == CHUNK - 1, dtot, 0.0)
    dcs[("r", CHUNK)] = dkst * w["kst"]
    stack = jnp.concatenate([dcs[kind] for kind in _ALL_KINDS], axis=0)
    return dq, dk, dv, w["dg"] + _cum3(tst, stack)


def _hgrn_fwd(proj_h, u_rows, lb_param, g_head, layer, name, phase=None):
    B, S, _ = proj_h.shape
    sb = _pick(S, (512, 256, 128, 64))
    nc = sb // CHUNK
    ts, _ = _cum_matrices()

    def body(*refs):
        ins, outs, (st,), p_in, p_out, p_sems = _split_refs(refs, 8, 3, 1, phase)
        q_ref, f_ref, i_ref, z_ref, lbp_ref, gh_ref, ts_ref, m_ref = ins
        o_ref, u_ref, sts_ref = outs
        h_id, b_id, s_id = pl.program_id(0), pl.program_id(1), pl.program_id(2)
        _hosted_start(phase, p_in, p_out, p_sems, (h_id == 0) & (b_id == 0) & (s_id == 0))

        @pl.when(s_id == 0)
        def _():
            st[...] = jnp.zeros_like(st)

        lt = _lb_terms(lbp_ref[...], layer)
        tsv = ts_ref[...]
        gh = gh_ref[...]
        parts = []
        for ci in range(nc):
            rows = slice(ci * CHUNK, (ci + 1) * CHUNK)
            logf, k, _, _ = _gate_fwd(f_ref[rows, :], lt)
            parts.append(_hg_intra_fwd(_silu(q_ref[rows, :]), k, i_ref[rows, :], logf, tsv, m_ref))
        cur = st[...]
        starts = []
        for ci in range(nc):
            sts_ref[ci] = cur
            starts.append(cur)
            cur = cur * parts[ci][2] + parts[ci][3]
        st[...] = cur
        for ci in range(nc):
            rows = slice(ci * CHUNK, (ci + 1) * CHUNK)
            o = parts[ci][0] + _dot(parts[ci][1], starts[ci], NT)
            o_ref[rows, :] = o
            r = lax.rsqrt(jnp.mean(o * o, axis=-1, keepdims=True) + NORM_EPS)
            u_ref[rows, :] = (((o * r) * gh) * _silu(z_ref[rows, :])).astype(BF16)
        _hosted_finish(phase, p_in, p_out, p_sems, (h_id == HG_HEADS - 1) & (b_id == B - 1) & (s_id == S // sb - 1))

    col = lambda base: pl.BlockSpec((None, sb, HG_DIM), lambda h, b, s: (b, s, base + h))
    p_ispecs, p_ospecs, p_oshapes, p_alias, p_scratch, p_args = _host_phase(phase, 8, 3)
    res = pl.pallas_call(
        body, name=name,
        grid=(HG_HEADS, B, S // sb),
        in_specs=[col(0), col(HG_HEADS), col(2 * HG_HEADS), col(3 * HG_HEADS),
                  pl.BlockSpec((DEPTH, HG_DIM), lambda h, b, s: (0, h)),
                  pl.BlockSpec((1, HG_DIM), lambda h, b, s: (0, 0)),
                  pl.BlockSpec((N_CUM_F, CHUNK), lambda h, b, s: (0, 0)),
                  pl.BlockSpec((len(_LEVELS), CHUNK, CHUNK), lambda h, b, s: (0, 0, 0))] + p_ispecs,
        out_specs=[col(0), col(0),
                   pl.BlockSpec((None, None, nc, HG_DIM, HG_DIM), lambda h, b, s: (b, h, s, 0, 0))] + p_ospecs,
        out_shape=[jax.ShapeDtypeStruct((B, S, HG_WIDTH), F32),
                   jax.ShapeDtypeStruct((B, S, u_rows), BF16),
                   jax.ShapeDtypeStruct((B, HG_HEADS, S // CHUNK, HG_DIM, HG_DIM), F32)] + p_oshapes,
        input_output_aliases=p_alias,
        scratch_shapes=[pltpu.VMEM((HG_DIM, HG_DIM), F32)] + p_scratch,
        compiler_params=_params(("arbitrary", "arbitrary", "arbitrary")),
    )(proj_h, proj_h, proj_h, proj_h, lb_param, g_head, ts, _level_masks(), *p_args)
    return res[0], res[1], res[2], list(res[3:])


def _hgrn_bwd(proj_h, o_h, du, states, lb_param, g_head, layer, name, phase=None):
    B, S, _ = proj_h.shape
    sb = _pick(S, (512, 256, 128, 64))
    nc = sb // CHUNK
    ns = S // sb
    ts, tst = _cum_matrices()

    def body(*refs):
        ins, outs, (dst,), p_in, p_out, p_sems = _split_refs(refs, 12, 6, 1, phase)
        q_ref, f_ref, i_ref, z_ref, o_ref, du_ref, sts_ref, lbp_ref, gh_ref, ts_ref, tst_ref, m_ref = ins
        dq_ref, df_ref, di_ref, dz_ref, dlb_ref, dgh_ref = outs
        h_id, b_id, s_id = pl.program_id(0), pl.program_id(1), pl.program_id(2)
        _hosted_start(phase, p_in, p_out, p_sems, (h_id == 0) & (b_id == 0) & (s_id == 0))

        @pl.when(s_id == 0)
        def _():
            dst[...] = jnp.zeros_like(dst)

        @pl.when((b_id == 0) & (s_id == 0))
        def _():
            dlb_ref[...] = jnp.zeros_like(dlb_ref)

        @pl.when((h_id == 0) & (b_id == 0) & (s_id == 0))
        def _():
            dgh_ref[...] = jnp.zeros_like(dgh_ref)

        lt = _lb_terms(lbp_ref[...], layer)
        gh = gh_ref[...]
        tsv = ts_ref[...]
        tstv = tst_ref[...]
        work = []
        dgh = jnp.zeros((1, HG_DIM), F32)
        for ci in range(nc):
            rows = slice(ci * CHUNK, (ci + 1) * CHUNK)
            x = f_ref[rows, :]
            logf, k, u1, u2 = _gate_fwd(x, lt)
            q = q_ref[rows, :]
            o = o_ref[rows, :]
            z = z_ref[rows, :]
            dub = du_ref[rows, :]
            r = lax.rsqrt(jnp.mean(o * o, axis=-1, keepdims=True) + NORM_EPS)
            n = o * r
            sg = _silu(z)
            dz_ref[rows, :] = (dub * (n * gh) * _silu_grad(z)).astype(BF16)
            dgh = dgh + jnp.sum(dub * sg * n, axis=0, keepdims=True)
            dn = dub * sg * gh
            do = r * (dn - n * jnp.mean(dn * n, axis=-1, keepdims=True))
            v = i_ref[rows, :]
            w = _hg_intra_bwd(_silu(q), k, v, logf, do, tsv, m_ref)
            w.update(x=x, logf=logf, u1=u1, u2=u2, q=q, v=v, do=do)
            work.append(w)
        dgh_ref[...] += dgh
        cur = dst[...]
        ends = [None] * nc
        for ci in reversed(range(nc)):
            ends[ci] = cur
            cur = cur * work[ci]["dec"] + work[ci]["qd"]
        dst[...] = cur
        dlb = jnp.zeros((1, HG_DIM), F32)
        for ci in range(nc):
            rows = slice(ci * CHUNK, (ci + 1) * CHUNK)
            w = work[ci]
            dq, dk, dv, dg = _hg_state_bwd(w, w["v"], w["do"], sts_ref[ci], ends[ci], tstv)
            di_ref[rows, :] = dv.astype(BF16)
            dq_ref[rows, :] = (dq * _silu_grad(w["q"])).astype(BF16)
            logf = w["logf"]
            dlogf = dg - jnp.exp(logf) * dk
            w1 = jnp.exp(w["u1"] - logf)
            w2 = jnp.exp(w["u2"] - logf)
            df_ref[rows, :] = (dlogf * w2 * (1.0 / (1.0 + jnp.exp(w["x"])))).astype(BF16)
            dlb = dlb + jnp.sum(dlogf * (w1 * lt["dlb1"] - w2 * lt["dlb2"]), axis=0, keepdims=True)
        dlb_ref[...] += dlb
        _hosted_finish(phase, p_in, p_out, p_sems, (h_id == HG_HEADS - 1) & (b_id == B - 1) & (s_id == ns - 1))

    col = lambda base: pl.BlockSpec((None, sb, HG_DIM), lambda h, b, s: (b, ns - 1 - s, base + h))
    out_col = pl.BlockSpec((None, sb, HG_DIM), lambda h, b, s: (b, ns - 1 - s, h))
    dt = jax.ShapeDtypeStruct((B, S, HG_WIDTH), BF16)
    p_ispecs, p_ospecs, p_oshapes, p_alias, p_scratch, p_args = _host_phase(phase, 12, 6)
    res = pl.pallas_call(
        body, name=name,
        grid=(HG_HEADS, B, ns),
        in_specs=[col(0), col(HG_HEADS), col(2 * HG_HEADS), col(3 * HG_HEADS), col(0), col(0),
                  pl.BlockSpec((None, None, nc, HG_DIM, HG_DIM), lambda h, b, s: (b, h, ns - 1 - s, 0, 0)),
                  pl.BlockSpec((DEPTH, HG_DIM), lambda h, b, s: (0, h)),
                  pl.BlockSpec((1, HG_DIM), lambda h, b, s: (0, 0)),
                  pl.BlockSpec((N_CUM_F, CHUNK), lambda h, b, s: (0, 0)),
                  pl.BlockSpec((CHUNK, N_CUM), lambda h, b, s: (0, 0)),
                  pl.BlockSpec((len(_LEVELS), CHUNK, CHUNK), lambda h, b, s: (0, 0, 0))] + p_ispecs,
        out_specs=[out_col, out_col, out_col, out_col,
                   pl.BlockSpec((1, HG_DIM), lambda h, b, s: (0, h)),
                   pl.BlockSpec((1, HG_DIM), lambda h, b, s: (0, 0))] + p_ospecs,
        out_shape=[dt, dt, dt, dt, jax.ShapeDtypeStruct((1, HG_WIDTH), F32),
                   jax.ShapeDtypeStruct((1, HG_DIM), F32)] + p_oshapes,
        input_output_aliases=p_alias,
        scratch_shapes=[pltpu.VMEM((HG_DIM, HG_DIM), F32)] + p_scratch,
        compiler_params=_params(("arbitrary", "arbitrary", "arbitrary")),
    )(proj_h, proj_h, proj_h, proj_h, o_h, du, states, lb_param, g_head, ts, tst, _level_masks(), *p_args)
    return tuple(res[:6]) + (list(res[6:]),)


def _rope_tables(S):
    half = ATT_DIM // 2
    inv_freq = ROPE_THETA ** (-jnp.arange(half, dtype=F32) / half)
    ang = jnp.arange(S).astype(F32)[:, None] * inv_freq[None, :]
    cos = jnp.cos(ang)
    sin = jnp.sin(ang)
    cos = jnp.concatenate([cos, cos, cos, cos], axis=1)
    sin = jnp.concatenate([-sin, sin, -sin, sin], axis=1)
    return cos, sin


def _attn_common():
    lane = lax.broadcasted_iota(jnp.int32, (1, 2 * ATT_DIM), 1)
    first_half = (lane % ATT_DIM) < (ATT_DIM // 2)
    left = lane < ATT_DIM

    def swap(x):
        return jnp.where(first_half, pltpu.roll(x, 128 - ATT_DIM // 2, 1), pltpu.roll(x, ATT_DIM // 2, 1))

    def rope(x, cos, sin):
        return x * cos + swap(x) * sin

    def rope_bwd(dy, cos, sin):
        return dy * cos + swap(dy * sin)

    def dup(x):
        xs = pltpu.roll(x, ATT_DIM, 1)
        return [jnp.where(left, x, xs), jnp.where(left, xs, x)]

    return left, rope, rope_bwd, dup


GROUP = ATT_HEADS // 2
GROUP_ROWS = GROUP * ATT_BLOCK


def _attn_bias(i):
    r = lax.broadcasted_iota(jnp.int32, (ATT_BLOCK, 2 * ATT_BLOCK), 0)
    c = lax.broadcasted_iota(jnp.int32, (ATT_BLOCK, 2 * ATT_BLOCK), 1)
    ok = (c > r) & (c <= r + ATT_BLOCK) & ((c >= ATT_BLOCK) | (i > 0))
    return jnp.where(ok, 0.0, NEG_INF)


def _stack_heads(pairs, left):
    rows = []
    for x in pairs:
        rows += [jnp.where(left, x, 0.0), jnp.where(left, 0.0, x)]
    return jnp.concatenate(rows, axis=0)


def _unstack_heads(y, left, pp):
    r0 = 2 * pp * ATT_BLOCK
    return jnp.where(left, y[r0:r0 + ATT_BLOCK], y[r0 + ATT_BLOCK:r0 + 2 * ATT_BLOCK])


def _row_sums(x):
    return _dot(x, jnp.ones((x.shape[1], 128), BF16), NN)


def _attn_probs(qs, kd, sink, bias):
    s = _dot(qs, kd, NT).reshape(GROUP, ATT_BLOCK, 2 * ATT_BLOCK) * ATT_SCALE + bias[None]
    s = s.reshape(GROUP_ROWS, 2 * ATT_BLOCK)
    m = jnp.max(jnp.maximum(jnp.maximum(s[:, :128], s[:, 128:]), sink), axis=-1, keepdims=True)
    p = jnp.exp(s - m)
    es = jnp.exp(sink - m)
    inv = 1.0 / (_row_sums(p) + es)
    return p * jnp.concatenate([inv, inv], axis=1), es * inv


def _sink_rows(sinks_l):
    return jnp.broadcast_to(jnp.repeat(sinks_l, ATT_BLOCK)[:, None], (ATT_HEADS * ATT_BLOCK, 128))


_Z0 = (2 * ATT_WIDTH + 2 * KV_WIDTH - ATT_WIDTH) // 256


def _attn_fwd(proj_a, u, sinks_l, cos, sin, name, phase=None):
    B, S, _ = proj_a.shape
    nb = S // ATT_BLOCK

    def body(*refs):
        ins, (u_ref,), _, p_in, p_out, p_sems = _split_refs(refs, 13, 1, 0, phase)
        q_ref, kvc_ref, kvp_ref, z0, z1, z2, z3, cos_ref, sin_ref, cosp_ref, sinp_ref, sinks_ref, _ = ins
        i = pl.program_id(1)
        _hosted_start(phase, p_in, p_out, p_sems, (pl.program_id(0) == 0) & (i == 0))
        left, rope, _, dup = _attn_common()
        cos_c, sin_c = cos_ref[...], sin_ref[...]
        kvc = kvc_ref[...]
        kvp = kvp_ref[...]
        kw = jnp.concatenate([rope(kvp[:, :KV_WIDTH], cosp_ref[...], sinp_ref[...]),
                              rope(kvc[:, :KV_WIDTH], cos_c, sin_c)], axis=0)
        vw = jnp.concatenate([kvp[:, KV_WIDTH:], kvc[:, KV_WIDTH:]], axis=0)
        kd, vd = dup(kw), dup(vw)
        bias = _attn_bias(i)
        zs = (z0, z1, z2, z3)
        for kvh in range(2):
            pairs = range(4 * kvh, 4 * kvh + 4)
            qs = _stack_heads([rope(q_ref[:, 128 * pr:128 * (pr + 1)], cos_c, sin_c) for pr in pairs], left)
            p, _ = _attn_probs(qs, kd[kvh], sinks_ref[kvh * GROUP_ROWS:(kvh + 1) * GROUP_ROWS, :], bias)
            o = _dot(p, vd[kvh], NN)
            for pp, pr in enumerate(pairs):
                z = zs[pr // 2][:, 128 * (pr % 2):128 * (pr % 2 + 1)]
                u_ref[:, 128 * pr:128 * (pr + 1)] = (_unstack_heads(o, left, pp) * _silu(z)).astype(BF16)
        _hosted_finish(phase, p_in, p_out, p_sems, (pl.program_id(0) == B - 1) & (i == nb - 1))

    rowblk = lambda w, cb: pl.BlockSpec((None, ATT_BLOCK, w), lambda b, i: (b, i, cb))
    tab = pl.BlockSpec((ATT_BLOCK, 128), lambda b, i: (i, 0))
    tabp = pl.BlockSpec((ATT_BLOCK, 128), lambda b, i: (jnp.maximum(i - 1, 0), 0))
    p_ispecs, p_ospecs, p_oshapes, p_alias, p_scratch, p_args = _host_phase(phase, 13, 1)
    res = pl.pallas_call(
        body, name=name,
        grid=(B, nb),
        in_specs=[rowblk(ATT_WIDTH, 0), rowblk(256, 4),
                  pl.BlockSpec((None, ATT_BLOCK, 256), lambda b, i: (b, jnp.maximum(i - 1, 0), 4)),
                  rowblk(256, _Z0), rowblk(256, _Z0 + 1), rowblk(256, _Z0 + 2), rowblk(256, _Z0 + 3),
                  tab, tab, tabp, tabp,
                  pl.BlockSpec((ATT_HEADS * ATT_BLOCK, 128), lambda b, i: (0, 0)),
                  pl.BlockSpec(memory_space=pl.ANY)] + p_ispecs,
        out_specs=[pl.BlockSpec((None, ATT_BLOCK, ATT_WIDTH), lambda b, i: (b, i, 1))] + p_ospecs,
        out_shape=[jax.ShapeDtypeStruct(u.shape, BF16)] + p_oshapes,
        input_output_aliases={12: 0, **p_alias},
        scratch_shapes=p_scratch,
        compiler_params=_params(("arbitrary", "arbitrary")),
    )(proj_a, proj_a, proj_a, proj_a, proj_a, proj_a, proj_a, cos, sin, cos, sin, sinks_l, u, *p_args)
    return res[0], list(res[1:])


def _attn_bwd(proj_a, du, sinks_l, cos, sin, name, phase=None):
    B, S, _ = proj_a.shape
    nb = S // ATT_BLOCK

    def body(*refs):
        ins, outs, (carry, sk_acc), p_in, p_out, p_sems = _split_refs(refs, 13, 4, 2, phase)
        q_ref, kvc_ref, kvp_ref, z0, z1, z2, z3, du_ref, cos_ref, sin_ref, cosp_ref, sinp_ref, sinks_ref = ins
        dq_ref, dkv_ref, dz_ref, dsk_ref = outs
        b_id, i = pl.program_id(0), pl.program_id(1)
        _hosted_start(phase, p_in, p_out, p_sems, (b_id == 0) & (i == 0))

        @pl.when((b_id == 0) & (i == 0))
        def _():
            sk_acc[...] = jnp.zeros_like(sk_acc)

        @pl.when(i == 0)
        def _():
            carry[...] = jnp.zeros_like(carry)

        @pl.when(i < nb)
        def _():
            left, rope, rope_bwd, dup = _attn_common()
            cos_c, sin_c = cos_ref[...], sin_ref[...]
            cos_p, sin_p = cosp_ref[...], sinp_ref[...]
            kvc = kvc_ref[...]
            kvp = kvp_ref[...]
            kw = jnp.concatenate([rope(kvp[:, :KV_WIDTH], cos_p, sin_p), rope(kvc[:, :KV_WIDTH], cos_c, sin_c)], axis=0)
            vw = jnp.concatenate([kvp[:, KV_WIDTH:], kvc[:, KV_WIDTH:]], axis=0)
            kd, vd = dup(kw), dup(vw)
            bias = _attn_bias(i)
            zs = (z0, z1, z2, z3)
            dkd, dvd = [], []
            for kvh in range(2):
                pairs = range(4 * kvh, 4 * kvh + 4)
                qs = _stack_heads([rope(q_ref[:, 128 * pr:128 * (pr + 1)], cos_c, sin_c) for pr in pairs], left)
                p, ps = _attn_probs(qs, kd[kvh], sinks_ref[kvh * GROUP_ROWS:(kvh + 1) * GROUP_ROWS, :], bias)
                o = _dot(p, vd[kvh], NN)
                dos = []
                for pp, pr in enumerate(pairs):
                    cols = slice(128 * pr, 128 * (pr + 1))
                    z = zs[pr // 2][:, 128 * (pr % 2):128 * (pr % 2 + 1)]
                    du128 = du_ref[:, cols]
                    dz_ref[:, cols] = (du128 * _unstack_heads(o, left, pp) * _silu_grad(z)).astype(BF16)
                    dos.append(du128 * _silu(z))
                dos = _stack_heads(dos, left)
                dp = _dot(dos, vd[kvh], NT)
                delta = _row_sums(p * dp)
                ds = p * (dp - jnp.concatenate([delta, delta], axis=1)) * ATT_SCALE
                sk_acc[kvh] += -ps * delta
                dqs = _dot(ds, kd[kvh], NN)
                for pp, pr in enumerate(pairs):
                    dq_ref[:, 128 * pr:128 * (pr + 1)] = rope_bwd(_unstack_heads(dqs, left, pp), cos_c, sin_c).astype(BF16)
                dkd.append(_dot(ds, qs, TN))
                dvd.append(_dot(p, dos, TN))
            fold = lambda pr: jnp.where(left, pr[0] + pltpu.roll(pr[0], ATT_DIM, 1), pr[1] + pltpu.roll(pr[1], ATT_DIM, 1))
            dkw = fold(dkd)
            dvw = fold(dvd)
            prev = jnp.concatenate([rope_bwd(dkw[:ATT_BLOCK], cos_p, sin_p), dvw[:ATT_BLOCK]], axis=1)
            cur = jnp.concatenate([rope_bwd(dkw[ATT_BLOCK:], cos_c, sin_c), dvw[ATT_BLOCK:]], axis=1)
            dkv_ref[...] = (carry[...] + prev).astype(BF16)
            carry[...] = cur

        @pl.when(i == nb)
        def _():
            dkv_ref[...] = carry[...].astype(BF16)

        @pl.when((b_id == B - 1) & (i == nb))
        def _():
            lane = lax.broadcasted_iota(jnp.int32, (1, 128), 1)
            tot = jnp.zeros((1, 128), F32)
            for hd in range(ATT_HEADS):
                rows = sk_acc[hd // GROUP, (hd % GROUP) * ATT_BLOCK:(hd % GROUP + 1) * ATT_BLOCK, :]
                tot = tot + jnp.where(lane == hd, jnp.sum(rows, axis=0, keepdims=True), 0.0)
            dsk_ref[...] = tot

        _hosted_finish(phase, p_in, p_out, p_sems, (b_id == B - 1) & (i == nb))

    cl = lambda i: jnp.minimum(i, nb - 1)
    pv = lambda i: jnp.maximum(jnp.minimum(i, nb - 1) - 1, 0)
    rowblk = lambda w, cb: pl.BlockSpec((None, ATT_BLOCK, w), lambda b, i: (b, cl(i), cb))
    tab = pl.BlockSpec((ATT_BLOCK, 128), lambda b, i: (cl(i), 0))
    tabp = pl.BlockSpec((ATT_BLOCK, 128), lambda b, i: (pv(i), 0))
    p_ispecs, p_ospecs, p_oshapes, p_alias, p_scratch, p_args = _host_phase(phase, 13, 4)
    res = pl.pallas_call(
        body, name=name,
        grid=(B, nb + 1),
        in_specs=[rowblk(ATT_WIDTH, 0), rowblk(256, 4),
                  pl.BlockSpec((None, ATT_BLOCK, 256), lambda b, i: (b, pv(i), 4)),
                  rowblk(256, _Z0), rowblk(256, _Z0 + 1), rowblk(256, _Z0 + 2), rowblk(256, _Z0 + 3),
                  rowblk(ATT_WIDTH, 1),
                  tab, tab, tabp, tabp,
                  pl.BlockSpec((ATT_HEADS * ATT_BLOCK, 128), lambda b, i: (0, 0))] + p_ispecs,
        out_specs=[rowblk(ATT_WIDTH, 0),
                   pl.BlockSpec((None, ATT_BLOCK, 256), lambda b, i: (b, jnp.maximum(i - 1, 0), 0)),
                   rowblk(ATT_WIDTH, 0),
                   pl.BlockSpec((1, 128), lambda b, i: (0, 0))] + p_ospecs,
        out_shape=[jax.ShapeDtypeStruct((B, S, ATT_WIDTH), BF16), jax.ShapeDtypeStruct((B, S, 256), BF16),
                   jax.ShapeDtypeStruct((B, S, ATT_WIDTH), BF16), jax.ShapeDtypeStruct((1, 128), F32)] + p_oshapes,
        input_output_aliases=p_alias,
        scratch_shapes=[pltpu.VMEM((ATT_BLOCK, 256), F32), pltpu.VMEM((2, GROUP_ROWS, 128), F32)] + p_scratch,
        compiler_params=_params(("arbitrary", "arbitrary")),
    )(proj_a, proj_a, proj_a, proj_a, proj_a, proj_a, proj_a, du, cos, sin, cos, sin, sinks_l, *p_args)
    return tuple(res[:4]) + (list(res[4:]),)


def _outproj_fwd(u2, w_out, x2, g_post, target2, name):
    T, D = x2.shape
    tm = _pick(T, (512, 256, 128))
    last = target2 is not None

    def body(u_ref, w_ref, x_ref, g_ref, *rest):
        y = lax.dot_general(u_ref[...], w_ref[...], (NN, ((), ())), preferred_element_type=F32)
        r = lax.rsqrt(jnp.mean(y * y, axis=-1, keepdims=True) + NORM_EPS)
        xn = x_ref[...] + (y * r) * g_ref[...]
        if last:
            t_ref, y_ref, dx_ref, loss_ref = rest
            err = xn - t_ref[...]
            dx_ref[...] = err * (1.0 / D)
            sq = err * err
            acc = sq[:, 0:128]
            for kk in range(1, D // 128):
                acc = acc + sq[:, 128 * kk:128 * (kk + 1)]
            part = jnp.sum(acc.reshape(tm // 8, 8, 128), axis=0) * (0.5 / D)

            @pl.when(pl.program_id(0) == 0)
            def _():
                loss_ref[...] = jnp.zeros_like(loss_ref)

            loss_ref[...] += part
        else:
            y_ref, xn_ref = rest
            xn_ref[...] = xn
        y_ref[...] = y

    row = pl.BlockSpec((tm, D), lambda i: (i, 0))
    in_specs = [pl.BlockSpec((tm, MIX_WIDTH), lambda i: (i, 0)),
                pl.BlockSpec((MIX_WIDTH, D), lambda i: (0, 0)), row,
                pl.BlockSpec((1, D), lambda i: (0, 0))]
    args = [u2, w_out, x2, g_post]
    out_specs = [row, row]
    out_shape = [jax.ShapeDtypeStruct((T, D), F32), jax.ShapeDtypeStruct((T, D), F32)]
    if last:
        in_specs.append(row)
        args.append(target2)
        out_specs.append(pl.BlockSpec((8, 128), lambda i: (0, 0)))
        out_shape.append(jax.ShapeDtypeStruct((8, 128), F32))
    return pl.pallas_call(
        body, name=name, grid=(T // tm,), in_specs=in_specs, out_specs=out_specs, out_shape=out_shape,
        compiler_params=_params(("arbitrary",)),
    )(*args)


def _postnorm_bwd(dxn2, y2, g_post, name):
    T, D = y2.shape
    tm = _pick(T, (512, 256, 128))
    nt = T // tm

    def body(dx_ref, y_ref, g_ref, dy_ref, dg_ref, acc):
        i = pl.program_id(0)

        @pl.when(i == 0)
        def _():
            acc[...] = jnp.zeros_like(acc)

        y = y_ref[...]
        dxn = dx_ref[...]
        r = lax.rsqrt(jnp.mean(y * y, axis=-1, keepdims=True) + NORM_EPS)
        n = y * r
        dn = dxn * g_ref[...]
        dy_ref[...] = (r * (dn - n * jnp.mean(dn * n, axis=-1, keepdims=True))).astype(BF16)
        acc[...] += jnp.sum((dxn * n).reshape(tm // 8, 8, D), axis=0)

        @pl.when(i == nt - 1)
        def _():
            dg_ref[...] = jnp.sum(acc[...], axis=0, keepdims=True)

    row = pl.BlockSpec((tm, D), lambda i: (i, 0))
    vec = pl.BlockSpec((1, D), lambda i: (0, 0))
    return pl.pallas_call(
        body, name=name, grid=(nt,), in_specs=[row, row, vec], out_specs=[row, vec],
        out_shape=[jax.ShapeDtypeStruct((T, D), BF16), jax.ShapeDtypeStruct((1, D), F32)],
        scratch_shapes=[pltpu.VMEM((8, D), F32)],
        compiler_params=_params(("arbitrary",)),
    )(dxn2, y2, g_post)


def _inproj_bwd(pieces, w_t, x2, dxn2, g_pre, name):
    T, D = x2.shape
    widths = [p.shape[1] for p in pieces]
    offs = [sum(widths[:i]) for i in range(len(pieces))]
    n_p = len(pieces)
    tm = _pick(T, (256, 128))
    nt = T // tm

    def body(*refs):
        w_ref, x_ref, dxn_ref, g_ref, dx_ref, dg_ref, acc = refs[n_p:]
        i = pl.program_id(0)

        @pl.when(i == 0)
        def _():
            acc[...] = jnp.zeros_like(acc)

        dh = jnp.zeros((tm, D), F32)
        for p in range(n_p):
            dh = dh + lax.dot_general(refs[p][...], w_ref[offs[p]:offs[p] + widths[p], :], (NN, ((), ())),
                                      preferred_element_type=F32)
        x = x_ref[...]
        r = lax.rsqrt(jnp.mean(x * x, axis=-1, keepdims=True) + NORM_EPS)
        n = x * r
        dn = dh * g_ref[...]
        dx_ref[...] = dxn_ref[...] + r * (dn - n * jnp.mean(dn * n, axis=-1, keepdims=True))
        acc[...] += jnp.sum((dh * n).reshape(tm // 8, 8, D), axis=0)

        @pl.when(i == nt - 1)
        def _():
            dg_ref[...] = jnp.sum(acc[...], axis=0, keepdims=True)

    row = pl.BlockSpec((tm, D), lambda i: (i, 0))
    vec = pl.BlockSpec((1, D), lambda i: (0, 0))
    return pl.pallas_call(
        body, name=name, grid=(nt,),
        in_specs=[pl.BlockSpec((tm, w), lambda i: (i, 0)) for w in widths]
        + [pl.BlockSpec((sum(widths), D), lambda i: (0, 0), pipeline_mode=pl.Buffered(1)), row, row, vec],
        out_specs=[row, vec],
        out_shape=[jax.ShapeDtypeStruct((T, D), F32), jax.ShapeDtypeStruct((1, D), F32)],
        scratch_shapes=[pltpu.VMEM((8, D), F32)],
        compiler_params=_params(("arbitrary",)),
    )(*pieces, w_t, x2, dxn2, g_pre)


def _step(x, target, g_pre, g_post, lb_param, g_head, sinks, shards=None, full=None):
    B, S, D = x.shape
    T = B * S
    dist = shards is not None
    if dist:
        a_loc, b_loc = shards
        ra, rb = a_loc.shape[1], b_loc.shape[1]
        side = _own_side_blocks()
        placed = lambda loc, nm: _place_own(loc, side, "place_" + nm)
        gather = lambda phase, nm: _run_phase(phase, nm)
        w_in0 = gather(_gather_ici_phase([a_loc[0]], [placed(a_loc[0], "in0")]), "gather_in0_ici")
        w_in0 = gather(_gather_d2d_phase(w_in0, [ra]), "gather_in0_d2d")[0]
        late_locs = [a_loc[1], b_loc[1], b_loc[0]]
        late_rs = [ra, rb, rb]
        late_full = [placed(a_loc[1], "in1"), placed(b_loc[1], "out1"), placed(b_loc[0], "out0")]
        w_in, w_out = [w_in0, None], [None, None]
    else:
        w_in, w_out = list(full[0]), list(full[1])
    cos, sin = _rope_tables(S)
    saved = []
    xs = x
    loss_part = None
    dxn = None
    for l in range(DEPTH):
        x2 = xs.reshape(T, D)
        host = dist and l == 0
        proj_h, proj_a, h = _inproj(x2, g_pre[l:l + 1], w_in[l], f"inproj{l}")
        proj_h = proj_h.reshape(B, S, N_H)
        proj_a = proj_a.reshape(B, S, N_A)
        o_h, u, states, got = _hgrn_fwd(proj_h, MIX_WIDTH, lb_param, g_head[l:l + 1], l, f"hgrn_fwd{l}",
                                        _gather_ici_phase(late_locs, late_full) if host else None)
        u, got = _attn_fwd(proj_a, u, _sink_rows(sinks[l]), cos, sin, f"attn_fwd{l}",
                           _gather_d2d_phase(got, late_rs) if host else None)
        if host:
            w_in[1], w_out[1], w_out[0] = got
        u2 = u.reshape(T, MIX_WIDTH)
        if l < DEPTH - 1:
            y, xn = _outproj_fwd(u2, w_out[l], x2, g_post[l:l + 1], None, f"outproj{l}")
            xn = xn.reshape(B, S, D)
        else:
            y, dxn, loss_part = _outproj_fwd(u2, w_out[l], x2, g_post[l:l + 1], target.reshape(T, D), f"outproj{l}")
            xn = None
        saved.append((x2, h, proj_h, proj_a, o_h, u2, states, y))
        xs = xn

    dw_in, dw_out = [None] * DEPTH, [None] * DEPTH
    dg_pre, dg_post, dlb, dg_head, dsinks = [], [], [], [], []
    for l in reversed(range(DEPTH)):
        x2, h, proj_h, proj_a, o_h, u2, states, y = saved[l]
        host = dist and l == 0
        dy, dgp = _postnorm_bwd(dxn, y, g_post[l:l + 1], f"postnorm_bwd{l}")
        dw_out[l] = _mm_tn([u2], dy, f"wgrad_out{l}")
        du = _mm_nt(dy, w_out[l], f"dgrad_out{l}").reshape(B, S, MIX_WIDTH)
        if host:
            early = [dw_in[1], dw_out[1], dw_out[0]]
        dqh, dfh, dih, dzh, dlb_l, dgh, got = _hgrn_bwd(
            proj_h, o_h, du, states, lb_param, g_head[l:l + 1], l, f"hgrn_bwd{l}",
            _reduce_d2d_phase(early, late_rs) if host else None)
        if host:
            parts = [_pair_sum(g, r, side, f"pair_sum{i}") for i, (g, r) in enumerate(zip(early, got))]
        dqa, dkv, dza, dsk, got = _attn_bwd(proj_a, du, _sink_rows(sinks[l]), cos, sin, f"attn_bwd{l}",
                                            _reduce_ici_phase(parts) if host else None)
        if host:
            dw_in[1], dw_out[1], dw_out[0] = [_chip_sum(p, r, f"chip_sum{i}")
                                              for i, (p, r) in enumerate(zip(parts, got))]
        dproj = [p.reshape(T, p.shape[-1]) for p in (dqh, dfh, dih, dzh, dqa, dkv, dza)]
        dw_in[l] = _mm_tn(dproj, h, f"wgrad_in{l}")
        dxn, dgpre = _inproj_bwd(dproj, w_in[l], x2, dxn, g_pre[l:l + 1], f"inproj_bwd{l}")
        dg_pre.append(dgpre)
        dg_post.append(dgp)
        dlb.append(dlb_l)
        dg_head.append(dgh)
        dsinks.append(dsk)
    if dist:
        got = _run_phase(_reduce_d2d_phase([dw_in[0]], [ra]), "reduce_in0_d2d")
        part = _pair_sum(dw_in[0], got[0], side, "pair_sum_in0")
        got = _run_phase(_reduce_ici_phase([part]), "reduce_in0_ici")
        dw_in[0] = _chip_sum(part, got[0], "chip_sum_in0")
    rev = lambda lst: jnp.concatenate(lst[::-1], axis=0)
    return (loss_part, dxn.reshape(B, S, D), jnp.stack(dw_in), jnp.stack(dw_out),
            rev(dg_pre), rev(dg_post), rev(dlb), rev(dg_head), rev(dsinks))


def _me_and_peers():
    x, y, c = lax.axis_index("x"), lax.axis_index("y"), lax.axis_index("c")
    me = 4 * x + 2 * y + c
    peers = []
    for k in range(1, N_DEV):
        px = 1 - x if k & 4 else x
        py = 1 - y if k & 2 else y
        pc = 1 - c if k & 1 else c
        peers.append(((px, py, pc), 4 * px + 2 * py + pc))
    return me, peers


class _Phase:
    def __init__(self, arrays, out_shapes, aliases, n_send, build):
        self.arrays, self.out_shapes, self.aliases = list(arrays), list(out_shapes), dict(aliases)
        self.n_send, self.build = n_send, build

    def scratch(self):
        return [pltpu.SemaphoreType.DMA((self.n_send,)), pltpu.SemaphoreType.DMA((self.n_send,))]

    def _copies(self, in_refs, out_refs, sems, arrivals):
        send_sems, recv_sems = sems
        sends, recvs = self.build(in_refs, out_refs)
        assert len(sends) == self.n_send == len(recvs)
        out = [pltpu.make_async_remote_copy(src_ref=s, dst_ref=d, send_sem=send_sems.at[i], recv_sem=recv_sems.at[i],
                                            device_id=dev, device_id_type=MESH) for i, (s, d, dev) in enumerate(sends)]
        inc = [pltpu.make_async_remote_copy(src_ref=s, dst_ref=r, send_sem=send_sems.at[i], recv_sem=recv_sems.at[i],
                                            device_id=dev, device_id_type=MESH)
               for i, ((s, _, dev), r) in enumerate(zip(sends, recvs))] if arrivals else []
        return out, inc

    def start(self, in_refs, out_refs, sems):
        out, _ = self._copies(in_refs, out_refs, sems, False)
        for cp in out:
            cp.start()

    def finish(self, in_refs, out_refs, sems):
        out, inc = self._copies(in_refs, out_refs, sems, True)
        for cp in inc:
            cp.wait_recv()
        for cp in out:
            cp.wait_send()


_ANY = pl.BlockSpec(memory_space=pl.ANY)


def _host_phase(phase, n_in, n_out):
    if phase is None:
        return [], [], [], {}, [], []
    aliases = {n_in + i: n_out + o for i, o in phase.aliases.items()}
    return ([_ANY] * len(phase.arrays), [_ANY] * len(phase.out_shapes), phase.out_shapes, aliases, phase.scratch(),
            phase.arrays)


def _split_refs(refs, n_in, n_out, n_scr, phase):
    pi = len(phase.arrays) if phase else 0
    po = len(phase.out_shapes) if phase else 0
    a = n_in + pi
    b = a + n_out + po
    return (refs[:n_in], refs[a:a + n_out], refs[b:b + n_scr], refs[n_in:a], refs[a + n_out:b], refs[b + n_scr:])


def _hosted_start(phase, p_in, p_out, p_sems, first):
    if phase is not None:
        @pl.when(first)
        def _():
            phase.start(p_in, p_out, p_sems)


def _hosted_finish(phase, p_in, p_out, p_sems, last):
    if phase is not None:
        @pl.when(last)
        def _():
            phase.finish(p_in, p_out, p_sems)


def _run_phase(phase, name):
    n_in, n_out = len(phase.arrays), len(phase.out_shapes)

    def body(*refs):
        phase.start(refs[:n_in], refs[n_in:n_in + n_out], refs[n_in + n_out:])
        phase.finish(refs[:n_in], refs[n_in:n_in + n_out], refs[n_in + n_out:])

    return pl.pallas_call(
        body, name=name, in_specs=[_ANY] * n_in, out_specs=[_ANY] * n_out,
        out_shape=phase.out_shapes, input_output_aliases=phase.aliases, scratch_shapes=phase.scratch(),
        compiler_params=pltpu.CompilerParams(has_side_effects=True),
    )(*phase.arrays)


def _mesh_place():
    x, y, c = lax.axis_index("x"), lax.axis_index("y"), lax.axis_index("c")
    chips = [(x, y), (1 - x, y), (x, 1 - y), (1 - x, 1 - y)]
    num = lambda chip, core: 4 * chip[0] + 2 * chip[1] + core
    return c, chips, num


def _own_side_blocks():
    c, chips, num = _mesh_place()
    return jnp.stack([num(ch, c) for ch in chips]).astype(jnp.int32)


def _rows(ref, r, dev):
    return ref.at[pl.ds(pl.multiple_of(dev * r, 16), r), :]


def _place_own(loc, blocks, name):
    r, D = loc.shape
    tr = _pick(r, (400, 256, 200, 128, 64, 16))

    def body(idx_ref, l_ref, o_ref):
        del idx_ref
        o_ref[...] = l_ref[...]

    return pl.pallas_call(
        body, name=name,
        grid_spec=pltpu.PrefetchScalarGridSpec(
            num_scalar_prefetch=1, grid=(r // tr,),
            in_specs=[pl.BlockSpec((tr, D), lambda i, idx: (i, 0))],
            out_specs=pl.BlockSpec((tr, D), lambda i, idx: (idx[0] * (r // tr) + i, 0))),
        out_shape=jax.ShapeDtypeStruct((N_DEV * r, D), loc.dtype),
        compiler_params=_params(("arbitrary",)),
    )(blocks, loc)


def _gather_ici_phase(locs, fulls):
    rs = [a.shape[0] for a in locs]
    n = len(locs)

    def build(ins, outs):
        c, chips, num = _mesh_place()
        me = num(chips[0], c)
        targets = [((*chips[0], 1 - c), num(chips[0], 1 - c))] + [((*ch, c), num(ch, c)) for ch in chips[1:]]
        sends, recvs = [], []
        for dev, dnum in targets:
            for i, r in enumerate(rs):
                sends.append((ins[i], _rows(outs[i], r, me), dev))
                recvs.append(_rows(outs[i], r, dnum))
        return sends, recvs

    shapes = [jax.ShapeDtypeStruct(a.shape, a.dtype) for a in fulls]
    return _Phase(list(locs) + list(fulls), shapes, {n + i: i for i in range(n)}, 4 * n, build)


def _gather_d2d_phase(fulls, rs):
    def build(ins, outs):
        c, chips, num = _mesh_place()
        sib = (*chips[0], 1 - c)
        sends, recvs = [], []
        for ch in chips[1:]:
            for i, r in enumerate(rs):
                blk = _rows(outs[i], r, num(ch, c))
                sends.append((blk, blk, sib))
                recvs.append(_rows(outs[i], r, num(ch, 1 - c)))
        return sends, recvs

    shapes = [jax.ShapeDtypeStruct(a.shape, a.dtype) for a in fulls]
    return _Phase(fulls, shapes, {i: i for i in range(len(fulls))}, 3 * len(fulls), build)


def _reduce_d2d_phase(grads, rs):
    def build(ins, outs):
        c, chips, num = _mesh_place()
        sib = (*chips[0], 1 - c)
        sends, recvs = [], []
        for j, ch in enumerate(chips):
            for i, r in enumerate(rs):
                sends.append((_rows(ins[i], r, num(ch, 1 - c)), outs[i].at[j], sib))
                recvs.append(outs[i].at[j])
        return sends, recvs

    shapes = [jax.ShapeDtypeStruct((4, r, g.shape[1]), g.dtype) for g, r in zip(grads, rs)]
    return _Phase(grads, shapes, {}, 4 * len(grads), build)


def _reduce_ici_phase(parts):
    def build(ins, outs):
        c, chips, _ = _mesh_place()
        sends, recvs = [], []
        for t in range(1, 4):
            for i in range(len(parts)):
                sends.append((ins[i].at[t], outs[i].at[t - 1], (*chips[t], c)))
                recvs.append(outs[i].at[t - 1])
        return sends, recvs

    shapes = [jax.ShapeDtypeStruct((3,) + p.shape[1:], p.dtype) for p in parts]
    return _Phase(parts, shapes, {}, 3 * len(parts), build)


def _pair_sum(g, got, blocks, name):
    n, r, D = got.shape
    tr = _pick(r, (400, 256, 200, 128, 64, 16))

    def body(idx_ref, g_ref, r_ref, o_ref):
        del idx_ref
        o_ref[...] = (g_ref[...].astype(F32) + r_ref[...].astype(F32)).astype(o_ref.dtype)

    blk = pl.BlockSpec((None, tr, D), lambda j, i, idx: (j, i, 0))
    return pl.pallas_call(
        body, name=name,
        grid_spec=pltpu.PrefetchScalarGridSpec(
            num_scalar_prefetch=1, grid=(n, r // tr),
            in_specs=[pl.BlockSpec((tr, D), lambda j, i, idx: (idx[j] * (r // tr) + i, 0)), blk],
            out_specs=blk),
        out_shape=jax.ShapeDtypeStruct(got.shape, got.dtype),
        compiler_params=_params(("arbitrary", "arbitrary")),
    )(blocks, g, got)


def _chip_sum(p, r, name):
    _, R, D = p.shape
    tr = _pick(R, (400, 256, 200, 128, 64, 16))

    def body(p_ref, r_ref, o_ref):
        acc = p_ref[...].astype(F32)
        for t in range(3):
            acc = acc + r_ref[t].astype(F32)
        o_ref[...] = acc

    return pl.pallas_call(
        body, name=name, grid=(R // tr,),
        in_specs=[pl.BlockSpec((None, tr, D), lambda i: (0, i, 0)), pl.BlockSpec((3, tr, D), lambda i: (0, i, 0))],
        out_specs=pl.BlockSpec((tr, D), lambda i: (i, 0)), out_shape=jax.ShapeDtypeStruct((R, D), F32),
        compiler_params=_params(("parallel",)))(p, r)


def _allreduce_small(vec):
    R, C = vec.shape

    def body(v_ref, o_ref, buf, send_sems, recv_sems):
        me, peers = _me_and_peers()
        buf[me] = v_ref[...]
        sends = []
        for k, (pid, _) in enumerate(peers):
            cp = pltpu.make_async_remote_copy(src_ref=v_ref, dst_ref=buf.at[me], send_sem=send_sems.at[k],
                                              recv_sem=recv_sems.at[k], device_id=pid, device_id_type=MESH)
            cp.start()
            sends.append(cp)
        for k, (pid, pnum) in enumerate(peers):
            pltpu.make_async_remote_copy(src_ref=v_ref, dst_ref=buf.at[pnum], send_sem=send_sems.at[k],
                                         recv_sem=recv_sems.at[k], device_id=pid, device_id_type=MESH).wait_recv()
        for cp in sends:
            cp.wait_send()
        acc = buf[0]
        for d in range(1, N_DEV):
            acc = acc + buf[d]
        o_ref[...] = acc

    vm = pl.BlockSpec(memory_space=pltpu.VMEM)
    return pl.pallas_call(
        body, name="allreduce_small",
        in_specs=[vm], out_specs=vm,
        out_shape=jax.ShapeDtypeStruct((R, C), F32),
        scratch_shapes=[pltpu.VMEM((N_DEV, R, C), F32), pltpu.SemaphoreType.DMA((N_DEV - 1,)),
                        pltpu.SemaphoreType.DMA((N_DEV - 1,))],
        compiler_params=pltpu.CompilerParams(has_side_effects=True),
    )(vec)


def _adamw(w, g, m, v, name):
    R, C = w.shape
    tr = _pick(R, (256, 128, 64, 32, 16, 8)) if R >= 8 else R
    c1 = 1.0 - ADAM_B1 ** ADAM_STEP
    c2 = 1.0 - ADAM_B2 ** ADAM_STEP

    def body(w_ref, g_ref, m_ref, v_ref, d_ref, mo_ref, vo_ref):
        gg = g_ref[...]
        mn = ADAM_B1 * m_ref[...] + (1.0 - ADAM_B1) * gg
        vn = ADAM_B2 * v_ref[...] + (1.0 - ADAM_B2) * (gg * gg)
        d_ref[...] = -ADAM_LR * ((mn / c1) / (jnp.sqrt(vn / c2) + ADAM_EPS) + ADAM_WD * w_ref[...])
        mo_ref[...] = mn
        vo_ref[...] = vn

    blk = pl.BlockSpec((tr, C), lambda i: (i, 0))
    sh = jax.ShapeDtypeStruct((R, C), F32)
    return pl.pallas_call(
        body, name=name, grid=(R // tr,), in_specs=[blk] * 4, out_specs=[blk] * 3, out_shape=[sh] * 3,
        compiler_params=_params(("parallel",)),
    )(w, g, m, v)


def _lb_param_grad(lb_param, dlb):
    L, C = lb_param.shape

    def body(p_ref, d_ref, o_ref):
        lbp = p_ref[...]
        d = d_ref[...]
        mx = jnp.max(lbp, axis=0, keepdims=True)
        e = jnp.exp(lbp - mx)
        p = e / jnp.sum(e, axis=0, keepdims=True)
        tot = jnp.sum(d, axis=0, keepdims=True)
        dps = []
        rest = tot
        for j in range(L):
            dps.append(rest - tot if j == 0 else rest)
            rest = rest - d[j:j + 1]
        dp = jnp.concatenate(dps, axis=0)
        o_ref[...] = p * (dp - jnp.sum(p * dp, axis=0, keepdims=True))

    vm = pl.BlockSpec(memory_space=pltpu.VMEM)
    return pl.pallas_call(body, name="lb_param_grad", in_specs=[vm, vm], out_specs=vm,
                          out_shape=jax.ShapeDtypeStruct((L, C), F32))(lb_param, dlb)


def _pack_small(loss_part, dg_pre, dg_post, dlb, dg_head, dsinks):
    pad8 = lambda a: jnp.pad(a.reshape(-1, 128), ((0, 8 - DEPTH), (0, 0)))
    rows = [dg_pre.reshape(-1, 128), dg_post.reshape(-1, 128), dlb.reshape(-1, 128), pad8(dg_head), pad8(dsinks),
            loss_part]
    return jnp.concatenate(rows, axis=0)


def _unpack_small(vec):
    n = DEPTH * D_MODEL // 128
    o = 0
    dg_pre = vec[o:o + n].reshape(DEPTH, D_MODEL); o += n
    dg_post = vec[o:o + n].reshape(DEPTH, D_MODEL); o += n
    dlb = vec[o:o + n].reshape(DEPTH, HG_WIDTH); o += n
    dg_head = vec[o:o + DEPTH]; o += 8
    dsinks = vec[o:o + DEPTH, :ATT_HEADS]; o += 8
    loss = jnp.sum(vec[o:o + 8])
    return loss, dg_pre, dg_post, dlb, dg_head, dsinks


def kernel(x, w_in, w_out, g_pre, g_post, lb_param, g_head, sinks, loss_target, m_w_in, m_w_out, m_g_pre, m_g_post, m_lb_param, m_g_head, m_sinks, v_w_in, v_w_out, v_g_pre, v_g_post, v_lb_param, v_g_head, v_sinks):
    L, D, nloc = w_in.shape
    w_in_t_loc = jnp.swapaxes(w_in, 1, 2).astype(BF16)
    (loss_part, dx, gw_in_t, gw_out, dg_pre, dg_post, dlb, dg_head, dsinks) = _step(
        x, loss_target, g_pre, g_post, lb_param, g_head, sinks, shards=(w_in_t_loc, w_out.astype(BF16)))
    gw_in = jnp.swapaxes(gw_in_t, 1, 2)

    small = _allreduce_small(_pack_small(loss_part, dg_pre, dg_post, dlb, dg_head, dsinks))
    loss, gg_pre, gg_post, gdlb, gg_head, gsinks = _unpack_small(small)
    glb = _lb_param_grad(lb_param, gdlb)

    grads = [gw_in, gw_out, gg_pre, gg_post, glb, gg_head, gsinks]
    ws = [w_in, w_out, g_pre, g_post, lb_param, g_head, sinks]
    ms = [m_w_in, m_w_out, m_g_pre, m_g_post, m_lb_param, m_g_head, m_sinks]
    vs = [v_w_in, v_w_out, v_g_pre, v_g_post, v_lb_param, v_g_head, v_sinks]
    names = ["w_in", "w_out", "g_pre", "g_post", "lb_param", "g_head", "sinks"]
    deltas, new_m, new_v = [], [], []
    for w, g, m, v, nm in zip(ws, grads, ms, vs, names):
        sh = w.shape
        two = lambda a: a.reshape(-1, sh[-1])
        d, mn, vn = _adamw(two(w), two(g), two(m), two(v), "adamw_" + nm)
        deltas.append(d.reshape(sh))
        new_m.append(mn.reshape(sh))
        new_v.append(vn.reshape(sh))
    return (loss, dx, *grads, *deltas, *new_m, *new_v)
```

```python
import functools
import math

import numpy as np
import jax
import jax.numpy as jnp
from jax import lax
from jax.experimental import pallas as pl
from jax.experimental.pallas import tpu as pltpu

F32 = jnp.float32
BF16 = jnp.bfloat16

D_MODEL = 1024
DEPTH = 2
HG_HEADS = 8
HG_DIM = 128
HG_WIDTH = HG_HEADS * HG_DIM
CHUNK = 64
ATT_HEADS = 16
ATT_DIM = 64
ATT_WIDTH = ATT_HEADS * ATT_DIM
KV_WIDTH = 128
ATT_BLOCK = 128
ATT_SCALE = 1.0 / math.sqrt(ATT_DIM)
ROPE_THETA = 10000.0
NORM_EPS = 1e-6
NEG_INF = -1e30
LB_FLOOR = 1e-20
N_H = 4 * HG_WIDTH
N_A = 2 * ATT_WIDTH + 2 * KV_WIDTH
IN_WIDTH = N_H + N_A
MIX_WIDTH = HG_WIDTH + ATT_WIDTH

ADAM_LR = 0.001
ADAM_B1 = 0.9
ADAM_B2 = 0.999
ADAM_EPS = 1e-08
ADAM_WD = 0.01
ADAM_STEP = 10

N_DEV = 8
MESH = pl.DeviceIdType.MESH
VMEM_LIMIT = 56 * 1024 * 1024

NN = ((1,), (0,))
NT = ((1,), (1,))
TN = ((0,), (0,))


def _dot(a, b, dims):
    return lax.dot_general(a.astype(BF16), b.astype(BF16), (dims, ((), ())), preferred_element_type=F32)


def _params(sem=None, **kw):
    return pltpu.CompilerParams(dimension_semantics=sem, vmem_limit_bytes=VMEM_LIMIT, **kw)


def _sigmoids(x):
    e = jnp.exp(-jnp.abs(x))
    r = 1.0 / (1.0 + e)
    er = e * r
    pos = x >= 0.0
    return jnp.where(pos, r, er), jnp.where(pos, er, r)


def _silu(x):
    return x * _sigmoids(x)[0]


def _silu_and_grad(x):
    s, ns = _sigmoids(x)
    return x * s, s * (1.0 + x * ns)


def _pick(n, prefs):
    for p in prefs:
        if n % p == 0:
            return p
    return n


def _inproj(x2, g, w, name):
    T, D = x2.shape
    tm = _pick(T, (256, 128))
    nchunk = 1024

    def body(x_ref, g_ref, w_ref, oh_ref, oa_ref, h_ref):
        x = x_ref[...]
        r = lax.rsqrt(jnp.mean(x * x, axis=-1, keepdims=True) + NORM_EPS)
        h = ((x * r) * g_ref[...]).astype(BF16)
        h_ref[...] = h
        for j in range(0, N_H, nchunk):
            oh_ref[:, j:j + nchunk] = lax.dot_general(h, w_ref[j:j + nchunk, :], (NT, ((), ())),
                                                      preferred_element_type=F32)
        for j in range(0, N_A, N_A // 2):
            oa_ref[:, j:j + N_A // 2] = lax.dot_general(h, w_ref[N_H + j:N_H + j + N_A // 2, :], (NT, ((), ())),
                                                        preferred_element_type=F32)

    row = lambda w_: pl.BlockSpec((tm, w_), lambda i: (i, 0))
    return pl.pallas_call(
        body, name=name,
        grid=(T // tm,),
        in_specs=[row(D), pl.BlockSpec((1, D), lambda i: (0, 0)),
                  pl.BlockSpec((IN_WIDTH, D), lambda i: (0, 0), pipeline_mode=pl.Buffered(1))],
        out_specs=[row(N_H), row(N_A), row(D)],
        out_shape=[jax.ShapeDtypeStruct((T, N_H), F32), jax.ShapeDtypeStruct((T, N_A), F32),
                   jax.ShapeDtypeStruct((T, D), BF16)],
        compiler_params=_params(("parallel",)),
    )(x2, g, w)


def _mm_nt(a, b, name, out_dtype=F32):
    M, K = a.shape
    N = b.shape[0]
    tm = _pick(M, (512, 256, 128))

    def body(a_ref, b_ref, o_ref):
        o_ref[...] = lax.dot_general(a_ref[...], b_ref[...], (NT, ((), ())),
                                     preferred_element_type=F32).astype(out_dtype)

    return pl.pallas_call(
        body, name=name,
        grid=(M // tm,),
        in_specs=[pl.BlockSpec((tm, K), lambda i: (i, 0)),
                  pl.BlockSpec((N, K), lambda i: (0, 0), pipeline_mode=pl.Buffered(1))],
        out_specs=pl.BlockSpec((tm, N), lambda i: (i, 0)),
        out_shape=jax.ShapeDtypeStruct((M, N), out_dtype),
        compiler_params=_params(("parallel",)),
    )(a, b)


def _mm_tn(pieces, b, name, out_dtype=BF16):
    T, m = b.shape
    tn = 256
    counts = [p.shape[1] // tn for p in pieces]
    starts = [sum(counts[:i]) for i in range(len(pieces))]
    n_p = len(pieces)

    def body(*refs):
        b_ref, o_ref = refs[n_p], refs[n_p + 1]
        i = pl.program_id(0)
        for p in range(n_p):
            @pl.when((i >= starts[p]) & (i < starts[p] + counts[p]))
            def _(p=p):
                o_ref[...] = lax.dot_general(refs[p][...], b_ref[...], (TN, ((), ())),
                                             preferred_element_type=F32).astype(out_dtype)

    piece_spec = lambda s, c: pl.BlockSpec((T, tn), lambda i: (0, jnp.clip(i - s, 0, c - 1)))
    return pl.pallas_call(
        body, name=name,
        grid=(sum(counts),),
        in_specs=[piece_spec(s, c) for s, c in zip(starts, counts)]
        + [pl.BlockSpec((T, m), lambda i: (0, 0), pipeline_mode=pl.Buffered(1))],
        out_specs=pl.BlockSpec((tn, m), lambda i: (i, 0)),
        out_shape=jax.ShapeDtypeStruct((sum(counts) * tn, m), out_dtype),
        compiler_params=_params(("arbitrary",)),
    )(*pieces, b)


_LEVELS = (0, 1, 2, 4, 8, 16, 32)
_CUM_L = (2, 4, 8, 16, 32, 64)
_ALL_KINDS = tuple(("c", L) for L in _CUM_L) + tuple(("r", L) for L in _CUM_L)
_MXU_KINDS = (("c", 2), ("c", 4), ("c", CHUNK), ("r", 2), ("r", 4))
N_CUM = len(_ALL_KINDS) * CHUNK
N_CUM_F = len(_MXU_KINDS) * CHUNK


def _cum_matrices():
    t = np.arange(CHUNK)[:, None]
    r = np.arange(CHUNK)[None, :]

    def mat(kind):
        c, L = kind
        return ((r // L == t // L) & ((r <= t) if c == "c" else (r > t))).astype(np.float32)

    fwd = np.concatenate([mat(kd) for kd in _MXU_KINDS], axis=0)
    full = np.concatenate([mat(kd) for kd in _ALL_KINDS], axis=0)
    return jnp.asarray(fwd, BF16), jnp.asarray(full.T.copy(), BF16)


def _level_masks():
    t = np.arange(CHUNK)[:, None]
    s = np.arange(CHUNK)[None, :]
    ms = []
    for L in _LEVELS:
        if L == 0:
            ms.append(t == s)
        else:
            ms.append((t // (2 * L) == s // (2 * L)) & ((t // L) % 2 == 1) & ((s // L) % 2 == 0))
    return jnp.asarray(np.stack(ms).astype(np.float32))


def _split3(x):
    hi = x.astype(BF16)
    r1 = x - hi.astype(F32)
    mid = r1.astype(BF16)
    lo = (r1 - mid.astype(F32)).astype(BF16)
    return hi, mid, lo


def _cum3(ts, x, terms=3):
    d = lambda p: lax.dot_general(ts, p, (NN, ((), ())), preferred_element_type=F32)
    return sum(d(p) for p in _split3(x)[:terms])


def _lb_terms(lbp, layer):
    mx = jnp.max(lbp, axis=0, keepdims=True)
    e = jnp.exp(lbp - mx)
    p = e / jnp.sum(e, axis=0, keepdims=True)
    cum = p[0:1]
    for j in range(1, layer + 1):
        cum = cum + p[j:j + 1]
    lb = cum - p[0:1]
    lbf = jnp.maximum(lb, LB_FLOOR)
    return dict(lbf=lbf, one_m=1.0 - lb, kcorr=lb - lbf, ind=jnp.where(lb > LB_FLOOR, 1.0, 0.0))


def _gate(x, lt):
    sig, nsig = _sigmoids(x)
    f = lt["lbf"] + lt["one_m"] * sig
    return jnp.log(f), lt["one_m"] * nsig + lt["kcorr"], f, sig, nsig


def _ck(x, ci):
    return x[ci * CHUNK:(ci + 1) * CHUNK]


def _block_cums(ts, g, nc):
    cs = [_cum3(ts, _ck(g, ci)) for ci in range(nc)]
    out = {kind: jnp.concatenate([c[CHUNK * i:CHUNK * (i + 1)] for c in cs], axis=0)
           for i, kind in enumerate(_MXU_KINDS)}
    b = out[("c", CHUNK)]
    ng = CHUNK // 8
    last = b.reshape(nc, ng, 8, HG_DIM)[:, :, 7:8, :]
    zero = jnp.zeros((nc, 1, 1, HG_DIM), F32)

    def spread(groups):
        return jnp.broadcast_to(jnp.concatenate(groups, axis=1), (nc, ng, 8, HG_DIM)).reshape(nc * CHUNK, HG_DIM)

    for L in (8, 16, 32):
        nb = L // 8
        first = lambda r: (r // nb) * nb
        out[("c", L)] = b - spread([last[:, first(r) - 1:first(r)] if r >= nb else zero for r in range(ng)])
        out[("r", L)] = spread([last[:, first(r) + nb - 1:first(r) + nb] for r in range(ng)]) - b
    out[("r", CHUNK)] = spread([last[:, ng - 1:ng]] * ng) - b
    return out


def _level_factors(cums, g, L):
    if L == 0:
        return None, None
    if L == 1:
        return jnp.exp(g), None
    return jnp.exp(cums[("c", L)]), jnp.exp(cums[("r", L)])


def _mul(a, e):
    return a if e is None else a * e


def _hg_block_fwd(qf, k, v, g, ts, m_ref, nc):
    cums = _block_cums(ts, g, nc)
    amat = [jnp.zeros((CHUNK, CHUNK), F32)] * nc
    for li, L in enumerate(_LEVELS):
        eq, ek = _level_factors(cums, g, L)
        ql, kl, m = _mul(qf, eq), _mul(k, ek), m_ref[li]
        amat = [amat[ci] + _dot(_ck(ql, ci), _ck(kl, ci), NT) * m for ci in range(nc)]
    b = cums[("c", CHUNK)]
    kst = k * jnp.exp(cums[("r", CHUNK)])
    o = [_dot(amat[ci], _ck(v, ci), NN) for ci in range(nc)]
    kv = [_dot(_ck(v, ci), _ck(kst, ci), TN) for ci in range(nc)]
    dec = [jnp.exp(b[(ci + 1) * CHUNK - 1:(ci + 1) * CHUNK, :]) for ci in range(nc)]
    return o, dec, kv, qf * jnp.exp(b)


def _hg_block_bwd(qf, k, v, g, do, ts, m_ref, nc):
    cums = _block_cums(ts, g, nc)
    dcs = {}
    da = [_dot(_ck(do, ci), _ck(v, ci), NT) for ci in range(nc)]
    dq = jnp.zeros_like(qf)
    dk = jnp.zeros_like(qf)
    dg = jnp.zeros_like(qf)
    amat = [jnp.zeros((CHUNK, CHUNK), F32)] * nc
    for li, L in enumerate(_LEVELS):
        eq, ek = _level_factors(cums, g, L)
        ql, kl, m = _mul(qf, eq), _mul(k, ek), m_ref[li]
        amat = [amat[ci] + _dot(_ck(ql, ci), _ck(kl, ci), NT) * m for ci in range(nc)]
        dal = [da[ci] * m for ci in range(nc)]
        dql = jnp.concatenate([_dot(dal[ci], _ck(kl, ci), NN) for ci in range(nc)], axis=0)
        dkl = jnp.concatenate([_dot(dal[ci], _ck(ql, ci), TN) for ci in range(nc)], axis=0)
        dq = dq + _mul(dql, eq)
        dk = dk + _mul(dkl, ek)
        if L == 1:
            dg = dg + dql * ql
        elif L > 1:
            dcs[("c", L)] = dql * ql
            dcs[("r", L)] = dkl * kl
    b = cums[("c", CHUNK)]
    e64 = jnp.exp(b)
    er64 = jnp.exp(cums[("r", CHUNK)])
    qb = qf * e64
    return dict(dq=dq, dk=dk, dg=dg, dcs=dcs, e64=e64, er64=er64, qb=qb, kst=k * er64,
                dv=[_dot(amat[ci], _ck(do, ci), TN) for ci in range(nc)],
                dec=[jnp.exp(b[(ci + 1) * CHUNK - 1:(ci + 1) * CHUNK, :]) for ci in range(nc)],
                qd=[_dot(_ck(do, ci), _ck(qb, ci), TN) for ci in range(nc)])


def _hg_state_bwd(w, v, do, starts, ends, tst, nc):
    dqb = jnp.concatenate([_dot(_ck(do, ci), starts[ci], NN) for ci in range(nc)], axis=0)
    dkst = jnp.concatenate([_dot(_ck(v, ci), ends[ci], NN) for ci in range(nc)], axis=0)
    dq = w["dq"] + dqb * w["e64"]
    dk = w["dk"] + dkst * w["er64"]
    dv = jnp.concatenate([w["dv"][ci] + _dot(_ck(w["kst"], ci), ends[ci], NT) for ci in range(nc)], axis=0)
    trow = lax.broadcasted_iota(jnp.int32, (CHUNK, 1), 0)
    dtot = jnp.concatenate(
        [jnp.where(trow == CHUNK - 1, jnp.sum(ends[ci] * starts[ci], axis=0, keepdims=True) * w["dec"][ci], 0.0)
         for ci in range(nc)], axis=0)
    dcs = dict(w["dcs"])
    dcs[("c", CHUNK)] = dqb * w["qb"] + dtot
    dcs[("r", CHUNK)] = dkst * w["kst"]
    dgs = [_cum3(tst, jnp.concatenate([_ck(dcs[kind], ci) for kind in _ALL_KINDS], axis=0), terms=2)
           for ci in range(nc)]
    return dq, dk, dv, w["dg"] + jnp.concatenate(dgs, axis=0)


def _hgrn_fwd(proj_h, u_rows, lb_param, g_head, layer, name, phase=None):
    B, S, _ = proj_h.shape
    sb = _pick(S, (512, 256, 128, 64))
    nc = sb // CHUNK
    ts, _ = _cum_matrices()

    def body(*refs):
        ins, outs, (st,), p_in, p_out, p_sems = _split_refs(refs, 8, 3, 1, phase)
        q_ref, f_ref, i_ref, z_ref, lbp_ref, gh_ref, ts_ref, m_ref = ins
        o_ref, u_ref, sts_ref = outs
        h_id, b_id, s_id = pl.program_id(0), pl.program_id(1), pl.program_id(2)
        _hosted_start(phase, p_in, p_out, p_sems, (h_id == 0) & (b_id == 0) & (s_id == 0))

        @pl.when(s_id == 0)
        def _():
            st[...] = jnp.zeros_like(st)

        lt = _lb_terms(lbp_ref[...], layer)
        tsv = ts_ref[...]
        gh = gh_ref[...]
        logf, k = _gate(f_ref[...], lt)[:2]
        o_part, dec, kv, qb = _hg_block_fwd(_silu(q_ref[...]), k, i_ref[...], logf, tsv, m_ref, nc)
        cur = st[...]
        starts = []
        for ci in range(nc):
            sts_ref[ci] = cur
            starts.append(cur)
            cur = cur * dec[ci] + kv[ci]
        st[...] = cur
        o = jnp.concatenate([o_part[ci] + _dot(_ck(qb, ci), starts[ci], NT) for ci in range(nc)], axis=0)
        o_ref[...] = o
        r = lax.rsqrt(jnp.mean(o * o, axis=-1, keepdims=True) + NORM_EPS)
        u_ref[...] = (((o * r) * gh) * _silu(z_ref[...])).astype(BF16)
        _hosted_finish(phase, p_in, p_out, p_sems, (h_id == HG_HEADS - 1) & (b_id == B - 1) & (s_id == S // sb - 1))

    col = lambda base: pl.BlockSpec((None, sb, HG_DIM), lambda h, b, s: (b, s, base + h))
    p_ispecs, p_ospecs, p_oshapes, p_alias, p_scratch, p_args = _host_phase(phase, 8, 3)
    res = pl.pallas_call(
        body, name=name,
        grid=(HG_HEADS, B, S // sb),
        in_specs=[col(0), col(HG_HEADS), col(2 * HG_HEADS), col(3 * HG_HEADS),
                  pl.BlockSpec((DEPTH, HG_DIM), lambda h, b, s: (0, h)),
                  pl.BlockSpec((1, HG_DIM), lambda h, b, s: (0, 0)),
                  pl.BlockSpec((N_CUM_F, CHUNK), lambda h, b, s: (0, 0)),
                  pl.BlockSpec((len(_LEVELS), CHUNK, CHUNK), lambda h, b, s: (0, 0, 0))] + p_ispecs,
        out_specs=[col(0), col(0),
                   pl.BlockSpec((None, None, nc, HG_DIM, HG_DIM), lambda h, b, s: (b, h, s, 0, 0))] + p_ospecs,
        out_shape=[jax.ShapeDtypeStruct((B, S, HG_WIDTH), F32),
                   jax.ShapeDtypeStruct((B, S, u_rows), BF16),
                   jax.ShapeDtypeStruct((B, HG_HEADS, S // CHUNK, HG_DIM, HG_DIM), F32)] + p_oshapes,
        input_output_aliases=p_alias,
        scratch_shapes=[pltpu.VMEM((HG_DIM, HG_DIM), F32)] + p_scratch,
        compiler_params=_params(("arbitrary", "arbitrary", "arbitrary")),
    )(proj_h, proj_h, proj_h, proj_h, lb_param, g_head, ts, _level_masks(), *p_args)
    return res[0], res[1], res[2], list(res[3:])


def _hgrn_bwd(proj_h, o_h, du, states, lb_param, g_head, layer, name, phase=None):
    B, S, _ = proj_h.shape
    sb = _pick(S, (512, 256, 128, 64))
    nc = sb // CHUNK
    ns = S // sb
    ts, tst = _cum_matrices()

    def body(*refs):
        ins, outs, (dst,), p_in, p_out, p_sems = _split_refs(refs, 12, 6, 1, phase)
        q_ref, f_ref, i_ref, z_ref, o_ref, du_ref, sts_ref, lbp_ref, gh_ref, ts_ref, tst_ref, m_ref = ins
        dq_ref, df_ref, di_ref, dz_ref, dlb_ref, dgh_ref = outs
        h_id, b_id, s_id = pl.program_id(0), pl.program_id(1), pl.program_id(2)
        _hosted_start(phase, p_in, p_out, p_sems, (h_id == 0) & (b_id == 0) & (s_id == 0))

        @pl.when(s_id == 0)
        def _():
            dst[...] = jnp.zeros_like(dst)

        @pl.when((b_id == 0) & (s_id == 0))
        def _():
            dlb_ref[...] = jnp.zeros_like(dlb_ref)

        @pl.when((h_id == 0) & (b_id == 0) & (s_id == 0))
        def _():
            dgh_ref[...] = jnp.zeros_like(dgh_ref)

        lt = _lb_terms(lbp_ref[...], layer)
        gh = gh_ref[...]
        tsv = ts_ref[...]
        tstv = tst_ref[...]
        logf, k, f, sig, nsig = _gate(f_ref[...], lt)
        o = o_ref[...]
        dub = du_ref[...]
        r = lax.rsqrt(jnp.mean(o * o, axis=-1, keepdims=True) + NORM_EPS)
        n = o * r
        sg, sg_grad = _silu_and_grad(z_ref[...])
        dz_ref[...] = (dub * (n * gh) * sg_grad).astype(BF16)
        dgh_ref[...] += jnp.sum(dub * sg * n, axis=0, keepdims=True)
        dn = dub * sg * gh
        do = r * (dn - n * jnp.mean(dn * n, axis=-1, keepdims=True))
        v = i_ref[...]
        qf, qf_grad = _silu_and_grad(q_ref[...])
        w = _hg_block_bwd(qf, k, v, logf, do, tsv, m_ref, nc)
        cur = dst[...]
        ends = [None] * nc
        for ci in reversed(range(nc)):
            ends[ci] = cur
            cur = cur * w["dec"][ci] + w["qd"][ci]
        dst[...] = cur
        dq, dk, dv, dg = _hg_state_bwd(w, v, do, [sts_ref[ci] for ci in range(nc)], ends, tstv, nc)
        di_ref[...] = dv.astype(BF16)
        dq_ref[...] = (dq * qf_grad).astype(BF16)
        scaled = (dg - f * dk) / f
        df_ref[...] = (scaled * lt["one_m"] * sig * nsig).astype(BF16)
        dlb_ref[...] += jnp.sum(scaled * (lt["ind"] - sig), axis=0, keepdims=True)
        _hosted_finish(phase, p_in, p_out, p_sems, (h_id == HG_HEADS - 1) & (b_id == B - 1) & (s_id == ns - 1))

    col = lambda base: pl.BlockSpec((None, sb, HG_DIM), lambda h, b, s: (b, ns - 1 - s, base + h))
    out_col = pl.BlockSpec((None, sb, HG_DIM), lambda h, b, s: (b, ns - 1 - s, h))
    dt = jax.ShapeDtypeStruct((B, S, HG_WIDTH), BF16)
    p_ispecs, p_ospecs, p_oshapes, p_alias, p_scratch, p_args = _host_phase(phase, 12, 6)
    res = pl.pallas_call(
        body, name=name,
        grid=(HG_HEADS, B, ns),
        in_specs=[col(0), col(HG_HEADS), col(2 * HG_HEADS), col(3 * HG_HEADS), col(0), col(0),
                  pl.BlockSpec((None, None, nc, HG_DIM, HG_DIM), lambda h, b, s: (b, h, ns - 1 - s, 0, 0)),
                  pl.BlockSpec((DEPTH, HG_DIM), lambda h, b, s: (0, h)),
                  pl.BlockSpec((1, HG_DIM), lambda h, b, s: (0, 0)),
                  pl.BlockSpec((N_CUM_F, CHUNK), lambda h, b, s: (0, 0)),
                  pl.BlockSpec((CHUNK, N_CUM), lambda h, b, s: (0, 0)),
                  pl.BlockSpec((len(_LEVELS), CHUNK, CHUNK), lambda h, b, s: (0, 0, 0))] + p_ispecs,
        out_specs=[out_col, out_col, out_col, out_col,
                   pl.BlockSpec((1, HG_DIM), lambda h, b, s: (0, h)),
                   pl.BlockSpec((1, HG_DIM), lambda h, b, s: (0, 0))] + p_ospecs,
        out_shape=[dt, dt, dt, dt, jax.ShapeDtypeStruct((1, HG_WIDTH), F32),
                   jax.ShapeDtypeStruct((1, HG_DIM), F32)] + p_oshapes,
        input_output_aliases=p_alias,
        scratch_shapes=[pltpu.VMEM((HG_DIM, HG_DIM), F32)] + p_scratch,
        compiler_params=_params(("arbitrary", "arbitrary", "arbitrary")),
    )(proj_h, proj_h, proj_h, proj_h, o_h, du, states, lb_param, g_head, ts, tst, _level_masks(), *p_args)
    return tuple(res[:6]) + (list(res[6:]),)


def _rope_tables(S):
    half = ATT_DIM // 2
    inv_freq = ROPE_THETA ** (-jnp.arange(half, dtype=F32) / half)
    ang = jnp.arange(S).astype(F32)[:, None] * inv_freq[None, :]
    cos = jnp.cos(ang)
    sin = jnp.sin(ang)
    cos = jnp.concatenate([cos, cos, cos, cos], axis=1)
    sin = jnp.concatenate([-sin, sin, -sin, sin], axis=1)
    return cos, sin


def _attn_common():
    lane = lax.broadcasted_iota(jnp.int32, (1, 2 * ATT_DIM), 1)
    first_half = (lane % ATT_DIM) < (ATT_DIM // 2)
    left = lane < ATT_DIM

    def swap(x):
        return jnp.where(first_half, pltpu.roll(x, 128 - ATT_DIM // 2, 1), pltpu.roll(x, ATT_DIM // 2, 1))

    def rope(x, cos, sin):
        return x * cos + swap(x) * sin

    def rope_bwd(dy, cos, sin):
        return dy * cos + swap(dy * sin)

    def dup(x):
        xs = pltpu.roll(x, ATT_DIM, 1)
        return [jnp.where(left, x, xs), jnp.where(left, xs, x)]

    return left, rope, rope_bwd, dup


GROUP = ATT_HEADS // 2
GROUP_ROWS = GROUP * ATT_BLOCK


def _attn_bias(i):
    r = lax.broadcasted_iota(jnp.int32, (ATT_BLOCK, 2 * ATT_BLOCK), 0)
    c = lax.broadcasted_iota(jnp.int32, (ATT_BLOCK, 2 * ATT_BLOCK), 1)
    ok = (c > r) & (c <= r + ATT_BLOCK) & ((c >= ATT_BLOCK) | (i > 0))
    return jnp.where(ok, 0.0, NEG_INF)


def _stack_heads(pairs, left):
    rows = []
    for x in pairs:
        rows += [jnp.where(left, x, 0.0), jnp.where(left, 0.0, x)]
    return jnp.concatenate(rows, axis=0)


def _unstack_heads(y, left, pp):
    r0 = 2 * pp * ATT_BLOCK
    return jnp.where(left, y[r0:r0 + ATT_BLOCK], y[r0 + ATT_BLOCK:r0 + 2 * ATT_BLOCK])


def _row_sums(x):
    return _dot(x, jnp.ones((x.shape[1], 128), BF16), NN)


def _attn_probs(qs, kd, sink, bias):
    s = _dot(qs, kd, NT).reshape(GROUP, ATT_BLOCK, 2 * ATT_BLOCK) * ATT_SCALE + bias[None]
    s = s.reshape(GROUP_ROWS, 2 * ATT_BLOCK)
    m = jnp.max(jnp.maximum(jnp.maximum(s[:, :128], s[:, 128:]), sink), axis=-1, keepdims=True)
    p = jnp.exp(s - m)
    es = jnp.exp(sink - m)
    inv = 1.0 / (_row_sums(p) + es)
    return p * jnp.concatenate([inv, inv], axis=1), es * inv


def _sink_rows(sinks_l):
    return jnp.broadcast_to(jnp.repeat(sinks_l, ATT_BLOCK)[:, None], (ATT_HEADS * ATT_BLOCK, 128))


_Z0 = (2 * ATT_WIDTH + 2 * KV_WIDTH - ATT_WIDTH) // 256


def _attn_fwd(proj_a, u, sinks_l, cos, sin, name, phase=None):
    B, S, _ = proj_a.shape
    nb = S // ATT_BLOCK

    def body(*refs):
        ins, (u_ref,), _, p_in, p_out, p_sems = _split_refs(refs, 13, 1, 0, phase)
        q_ref, kvc_ref, kvp_ref, z0, z1, z2, z3, cos_ref, sin_ref, cosp_ref, sinp_ref, sinks_ref, _ = ins
        i = pl.program_id(1)
        _hosted_start(phase, p_in, p_out, p_sems, (pl.program_id(0) == 0) & (i == 0))
        left, rope, _, dup = _attn_common()
        cos_c, sin_c = cos_ref[...], sin_ref[...]
        kvc = kvc_ref[...]
        kvp = kvp_ref[...]
        kw = jnp.concatenate([rope(kvp[:, :KV_WIDTH], cosp_ref[...], sinp_ref[...]),
                              rope(kvc[:, :KV_WIDTH], cos_c, sin_c)], axis=0)
        vw = jnp.concatenate([kvp[:, KV_WIDTH:], kvc[:, KV_WIDTH:]], axis=0)
        kd, vd = dup(kw), dup(vw)
        bias = _attn_bias(i)
        zs = (z0, z1, z2, z3)
        for kvh in range(2):
            pairs = range(4 * kvh, 4 * kvh + 4)
            qs = _stack_heads([rope(q_ref[:, 128 * pr:128 * (pr + 1)], cos_c, sin_c) for pr in pairs], left)
            p, _ = _attn_probs(qs, kd[kvh], sinks_ref[kvh * GROUP_ROWS:(kvh + 1) * GROUP_ROWS, :], bias)
            o = _dot(p, vd[kvh], NN)
            for pp, pr in enumerate(pairs):
                z = zs[pr // 2][:, 128 * (pr % 2):128 * (pr % 2 + 1)]
                u_ref[:, 128 * pr:128 * (pr + 1)] = (_unstack_heads(o, left, pp) * _silu(z)).astype(BF16)
        _hosted_finish(phase, p_in, p_out, p_sems, (pl.program_id(0) == B - 1) & (i == nb - 1))

    rowblk = lambda w, cb: pl.BlockSpec((None, ATT_BLOCK, w), lambda b, i: (b, i, cb))
    tab = pl.BlockSpec((ATT_BLOCK, 128), lambda b, i: (i, 0))
    tabp = pl.BlockSpec((ATT_BLOCK, 128), lambda b, i: (jnp.maximum(i - 1, 0), 0))
    p_ispecs, p_ospecs, p_oshapes, p_alias, p_scratch, p_args = _host_phase(phase, 13, 1)
    res = pl.pallas_call(
        body, name=name,
        grid=(B, nb),
        in_specs=[rowblk(ATT_WIDTH, 0), rowblk(256, 4),
                  pl.BlockSpec((None, ATT_BLOCK, 256), lambda b, i: (b, jnp.maximum(i - 1, 0), 4)),
                  rowblk(256, _Z0), rowblk(256, _Z0 + 1), rowblk(256, _Z0 + 2), rowblk(256, _Z0 + 3),
                  tab, tab, tabp, tabp,
                  pl.BlockSpec((ATT_HEADS * ATT_BLOCK, 128), lambda b, i: (0, 0)),
                  pl.BlockSpec(memory_space=pl.ANY)] + p_ispecs,
        out_specs=[pl.BlockSpec((None, ATT_BLOCK, ATT_WIDTH), lambda b, i: (b, i, 1))] + p_ospecs,
        out_shape=[jax.ShapeDtypeStruct(u.shape, BF16)] + p_oshapes,
        input_output_aliases={12: 0, **p_alias},
        scratch_shapes=p_scratch,
        compiler_params=_params(("arbitrary", "arbitrary")),
    )(proj_a, proj_a, proj_a, proj_a, proj_a, proj_a, proj_a, cos, sin, cos, sin, sinks_l, u, *p_args)
    return res[0], list(res[1:])


def _attn_bwd(proj_a, du, sinks_l, cos, sin, name, phase=None):
    B, S, _ = proj_a.shape
    nb = S // ATT_BLOCK

    def body(*refs):
        ins, outs, (carry, sk_acc), p_in, p_out, p_sems = _split_refs(refs, 13, 4, 2, phase)
        q_ref, kvc_ref, kvp_ref, z0, z1, z2, z3, du_ref, cos_ref, sin_ref, cosp_ref, sinp_ref, sinks_ref = ins
        dq_ref, dkv_ref, dz_ref, dsk_ref = outs
        b_id, i = pl.program_id(0), pl.program_id(1)
        _hosted_start(phase, p_in, p_out, p_sems, (b_id == 0) & (i == 0))

        @pl.when((b_id == 0) & (i == 0))
        def _():
            sk_acc[...] = jnp.zeros_like(sk_acc)

        @pl.when(i == 0)
        def _():
            carry[...] = jnp.zeros_like(carry)

        @pl.when(i < nb)
        def _():
            left, rope, rope_bwd, dup = _attn_common()
            cos_c, sin_c = cos_ref[...], sin_ref[...]
            cos_p, sin_p = cosp_ref[...], sinp_ref[...]
            kvc = kvc_ref[...]
            kvp = kvp_ref[...]
            kw = jnp.concatenate([rope(kvp[:, :KV_WIDTH], cos_p, sin_p), rope(kvc[:, :KV_WIDTH], cos_c, sin_c)], axis=0)
            vw = jnp.concatenate([kvp[:, KV_WIDTH:], kvc[:, KV_WIDTH:]], axis=0)
            kd, vd = dup(kw), dup(vw)
            bias = _attn_bias(i)
            zs = (z0, z1, z2, z3)
            dkd, dvd = [], []
            for kvh in range(2):
                pairs = range(4 * kvh, 4 * kvh + 4)
                qs = _stack_heads([rope(q_ref[:, 128 * pr:128 * (pr + 1)], cos_c, sin_c) for pr in pairs], left)
                p, ps = _attn_probs(qs, kd[kvh], sinks_ref[kvh * GROUP_ROWS:(kvh + 1) * GROUP_ROWS, :], bias)
                o = _dot(p, vd[kvh], NN)
                dos = []
                for pp, pr in enumerate(pairs):
                    cols = slice(128 * pr, 128 * (pr + 1))
                    sg, sg_grad = _silu_and_grad(zs[pr // 2][:, 128 * (pr % 2):128 * (pr % 2 + 1)])
                    du128 = du_ref[:, cols]
                    dz_ref[:, cols] = (du128 * _unstack_heads(o, left, pp) * sg_grad).astype(BF16)
                    dos.append(du128 * sg)
                dos = _stack_heads(dos, left)
                dp = _dot(dos, vd[kvh], NT)
                delta = _row_sums(p * dp)
                ds = p * (dp - jnp.concatenate([delta, delta], axis=1)) * ATT_SCALE
                sk_acc[kvh] += -ps * delta
                dqs = _dot(ds, kd[kvh], NN)
                for pp, pr in enumerate(pairs):
                    dq_ref[:, 128 * pr:128 * (pr + 1)] = rope_bwd(_unstack_heads(dqs, left, pp), cos_c, sin_c).astype(BF16)
                dkd.append(_dot(ds, qs, TN))
                dvd.append(_dot(p, dos, TN))
            fold = lambda pr: jnp.where(left, pr[0] + pltpu.roll(pr[0], ATT_DIM, 1), pr[1] + pltpu.roll(pr[1], ATT_DIM, 1))
            dkw = fold(dkd)
            dvw = fold(dvd)
            prev = jnp.concatenate([rope_bwd(dkw[:ATT_BLOCK], cos_p, sin_p), dvw[:ATT_BLOCK]], axis=1)
            cur = jnp.concatenate([rope_bwd(dkw[ATT_BLOCK:], cos_c, sin_c), dvw[ATT_BLOCK:]], axis=1)
            dkv_ref[...] = (carry[...] + prev).astype(BF16)
            carry[...] = cur

        @pl.when(i == nb)
        def _():
            dkv_ref[...] = carry[...].astype(BF16)

        @pl.when((b_id == B - 1) & (i == nb))
        def _():
            lane = lax.broadcasted_iota(jnp.int32, (1, 128), 1)
            tot = jnp.zeros((1, 128), F32)
            for hd in range(ATT_HEADS):
                rows = sk_acc[hd // GROUP, (hd % GROUP) * ATT_BLOCK:(hd % GROUP + 1) * ATT_BLOCK, :]
                tot = tot + jnp.where(lane == hd, jnp.sum(rows, axis=0, keepdims=True), 0.0)
            dsk_ref[...] = tot

        _hosted_finish(phase, p_in, p_out, p_sems, (b_id == B - 1) & (i == nb))

    cl = lambda i: jnp.minimum(i, nb - 1)
    pv = lambda i: jnp.maximum(jnp.minimum(i, nb - 1) - 1, 0)
    rowblk = lambda w, cb: pl.BlockSpec((None, ATT_BLOCK, w), lambda b, i: (b, cl(i), cb))
    tab = pl.BlockSpec((ATT_BLOCK, 128), lambda b, i: (cl(i), 0))
    tabp = pl.BlockSpec((ATT_BLOCK, 128), lambda b, i: (pv(i), 0))
    p_ispecs, p_ospecs, p_oshapes, p_alias, p_scratch, p_args = _host_phase(phase, 13, 4)
    res = pl.pallas_call(
        body, name=name,
        grid=(B, nb + 1),
        in_specs=[rowblk(ATT_WIDTH, 0), rowblk(256, 4),
                  pl.BlockSpec((None, ATT_BLOCK, 256), lambda b, i: (b, pv(i), 4)),
                  rowblk(256, _Z0), rowblk(256, _Z0 + 1), rowblk(256, _Z0 + 2), rowblk(256, _Z0 + 3),
                  rowblk(ATT_WIDTH, 1),
                  tab, tab, tabp, tabp,
                  pl.BlockSpec((ATT_HEADS * ATT_BLOCK, 128), lambda b, i: (0, 0))] + p_ispecs,
        out_specs=[rowblk(ATT_WIDTH, 0),
                   pl.BlockSpec((None, ATT_BLOCK, 256), lambda b, i: (b, jnp.maximum(i - 1, 0), 0)),
                   rowblk(ATT_WIDTH, 0),
                   pl.BlockSpec((1, 128), lambda b, i: (0, 0))] + p_ospecs,
        out_shape=[jax.ShapeDtypeStruct((B, S, ATT_WIDTH), BF16), jax.ShapeDtypeStruct((B, S, 256), BF16),
                   jax.ShapeDtypeStruct((B, S, ATT_WIDTH), BF16), jax.ShapeDtypeStruct((1, 128), F32)] + p_oshapes,
        input_output_aliases=p_alias,
        scratch_shapes=[pltpu.VMEM((ATT_BLOCK, 256), F32), pltpu.VMEM((2, GROUP_ROWS, 128), F32)] + p_scratch,
        compiler_params=_params(("arbitrary", "arbitrary")),
    )(proj_a, proj_a, proj_a, proj_a, proj_a, proj_a, proj_a, du, cos, sin, cos, sin, sinks_l, *p_args)
    return tuple(res[:4]) + (list(res[4:]),)


def _outproj_fwd(u2, w_out, x2, g_post, target2, name):
    T, D = x2.shape
    tm = _pick(T, (512, 256, 128))
    last = target2 is not None

    def body(u_ref, w_ref, x_ref, g_ref, *rest):
        y = lax.dot_general(u_ref[...], w_ref[...], (NN, ((), ())), preferred_element_type=F32)
        r = lax.rsqrt(jnp.mean(y * y, axis=-1, keepdims=True) + NORM_EPS)
        xn = x_ref[...] + (y * r) * g_ref[...]
        if last:
            t_ref, y_ref, dx_ref, loss_ref = rest
            err = xn - t_ref[...]
            dx_ref[...] = err * (1.0 / D)
            sq = err * err
            acc = sq[:, 0:128]
            for kk in range(1, D // 128):
                acc = acc + sq[:, 128 * kk:128 * (kk + 1)]
            part = jnp.sum(acc.reshape(tm // 8, 8, 128), axis=0) * (0.5 / D)

            @pl.when(pl.program_id(0) == 0)
            def _():
                loss_ref[...] = jnp.zeros_like(loss_ref)

            loss_ref[...] += part
        else:
            y_ref, xn_ref = rest
            xn_ref[...] = xn
        y_ref[...] = y

    row = pl.BlockSpec((tm, D), lambda i: (i, 0))
    in_specs = [pl.BlockSpec((tm, MIX_WIDTH), lambda i: (i, 0)),
                pl.BlockSpec((MIX_WIDTH, D), lambda i: (0, 0)), row,
                pl.BlockSpec((1, D), lambda i: (0, 0))]
    args = [u2, w_out, x2, g_post]
    out_specs = [row, row]
    out_shape = [jax.ShapeDtypeStruct((T, D), F32), jax.ShapeDtypeStruct((T, D), F32)]
    if last:
        in_specs.append(row)
        args.append(target2)
        out_specs.append(pl.BlockSpec((8, 128), lambda i: (0, 0)))
        out_shape.append(jax.ShapeDtypeStruct((8, 128), F32))
    return pl.pallas_call(
        body, name=name, grid=(T // tm,), in_specs=in_specs, out_specs=out_specs, out_shape=out_shape,
        compiler_params=_params(("arbitrary",)),
    )(*args)


def _postnorm_bwd(dxn2, y2, g_post, name):
    T, D = y2.shape
    tm = _pick(T, (512, 256, 128))
    nt = T // tm

    def body(dx_ref, y_ref, g_ref, dy_ref, dg_ref, acc):
        i = pl.program_id(0)

        @pl.when(i == 0)
        def _():
            acc[...] = jnp.zeros_like(acc)

        y = y_ref[...]
        dxn = dx_ref[...]
        r = lax.rsqrt(jnp.mean(y * y, axis=-1, keepdims=True) + NORM_EPS)
        n = y * r
        dn = dxn * g_ref[...]
        dy_ref[...] = (r * (dn - n * jnp.mean(dn * n, axis=-1, keepdims=True))).astype(BF16)
        acc[...] += jnp.sum((dxn * n).reshape(tm // 8, 8, D), axis=0)

        @pl.when(i == nt - 1)
        def _():
            dg_ref[...] = jnp.sum(acc[...], axis=0, keepdims=True)

    row = pl.BlockSpec((tm, D), lambda i: (i, 0))
    vec = pl.BlockSpec((1, D), lambda i: (0, 0))
    return pl.pallas_call(
        body, name=name, grid=(nt,), in_specs=[row, row, vec], out_specs=[row, vec],
        out_shape=[jax.ShapeDtypeStruct((T, D), BF16), jax.ShapeDtypeStruct((1, D), F32)],
        scratch_shapes=[pltpu.VMEM((8, D), F32)],
        compiler_params=_params(("arbitrary",)),
    )(dxn2, y2, g_post)


def _inproj_bwd(pieces, w_t, x2, dxn2, g_pre, name):
    T, D = x2.shape
    widths = [p.shape[1] for p in pieces]
    offs = [sum(widths[:i]) for i in range(len(pieces))]
    n_p = len(pieces)
    tm = _pick(T, (256, 128))
    nt = T // tm

    def body(*refs):
        w_ref, x_ref, dxn_ref, g_ref, dx_ref, dg_ref, acc = refs[n_p:]
        i = pl.program_id(0)

        @pl.when(i == 0)
        def _():
            acc[...] = jnp.zeros_like(acc)

        dh = jnp.zeros((tm, D), F32)
        for p in range(n_p):
            dh = dh + lax.dot_general(refs[p][...], w_ref[offs[p]:offs[p] + widths[p], :], (NN, ((), ())),
                                      preferred_element_type=F32)
        x = x_ref[...]
        r = lax.rsqrt(jnp.mean(x * x, axis=-1, keepdims=True) + NORM_EPS)
        n = x * r
        dn = dh * g_ref[...]
        dx_ref[...] = dxn_ref[...] + r * (dn - n * jnp.mean(dn * n, axis=-1, keepdims=True))
        acc[...] += jnp.sum((dh * n).reshape(tm // 8, 8, D), axis=0)

        @pl.when(i == nt - 1)
        def _():
            dg_ref[...] = jnp.sum(acc[...], axis=0, keepdims=True)

    row = pl.BlockSpec((tm, D), lambda i: (i, 0))
    vec = pl.BlockSpec((1, D), lambda i: (0, 0))
    return pl.pallas_call(
        body, name=name, grid=(nt,),
        in_specs=[pl.BlockSpec((tm, w), lambda i: (i, 0)) for w in widths]
        + [pl.BlockSpec((sum(widths), D), lambda i: (0, 0), pipeline_mode=pl.Buffered(1)), row, row, vec],
        out_specs=[row, vec],
        out_shape=[jax.ShapeDtypeStruct((T, D), F32), jax.ShapeDtypeStruct((1, D), F32)],
        scratch_shapes=[pltpu.VMEM((8, D), F32)],
        compiler_params=_params(("arbitrary",)),
    )(*pieces, w_t, x2, dxn2, g_pre)


def _step(x, target, g_pre, g_post, lb_param, g_head, sinks, shards=None, full=None):
    B, S, D = x.shape
    T = B * S
    dist = shards is not None
    if dist:
        a_loc, b_loc = shards
        ra, rb = a_loc.shape[1], b_loc.shape[1]
        side = _own_side_blocks()
        placed = lambda loc, nm: _place_own(loc, side, "place_" + nm)
        gather = lambda phase, nm: _run_phase(phase, nm)
        w_in0 = gather(_gather_ici_phase([a_loc[0]], [placed(a_loc[0], "in0")]), "gather_in0_ici")
        w_in0 = gather(_gather_d2d_phase(w_in0, [ra]), "gather_in0_d2d")[0]
        late_locs = [a_loc[1], b_loc[1], b_loc[0]]
        late_rs = [ra, rb, rb]
        late_full = [placed(a_loc[1], "in1"), placed(b_loc[1], "out1"), placed(b_loc[0], "out0")]
        w_in, w_out = [w_in0, None], [None, None]
    else:
        w_in, w_out = list(full[0]), list(full[1])
    cos, sin = _rope_tables(S)
    saved = []
    xs = x
    loss_part = None
    dxn = None
    for l in range(DEPTH):
        x2 = xs.reshape(T, D)
        host = dist and l == 0
        proj_h, proj_a, h = _inproj(x2, g_pre[l:l + 1], w_in[l], f"inproj{l}")
        proj_h = proj_h.reshape(B, S, N_H)
        proj_a = proj_a.reshape(B, S, N_A)
        o_h, u, states, got = _hgrn_fwd(proj_h, MIX_WIDTH, lb_param, g_head[l:l + 1], l, f"hgrn_fwd{l}",
                                        _gather_ici_phase(late_locs, late_full) if host else None)
        u, got = _attn_fwd(proj_a, u, _sink_rows(sinks[l]), cos, sin, f"attn_fwd{l}",
                           _gather_d2d_phase(got, late_rs) if host else None)
        if host:
            w_in[1], w_out[1], w_out[0] = got
        u2 = u.reshape(T, MIX_WIDTH)
        if l < DEPTH - 1:
            y, xn = _outproj_fwd(u2, w_out[l], x2, g_post[l:l + 1], None, f"outproj{l}")
            xn = xn.reshape(B, S, D)
        else:
            y, dxn, loss_part = _outproj_fwd(u2, w_out[l], x2, g_post[l:l + 1], target.reshape(T, D), f"outproj{l}")
            xn = None
        saved.append((x2, h, proj_h, proj_a, o_h, u2, states, y))
        xs = xn

    dw_in, dw_out = [None] * DEPTH, [None] * DEPTH
    dg_pre, dg_post, dlb, dg_head, dsinks = [], [], [], [], []
    for l in reversed(range(DEPTH)):
        x2, h, proj_h, proj_a, o_h, u2, states, y = saved[l]
        host = dist and l == 0
        dy, dgp = _postnorm_bwd(dxn, y, g_post[l:l + 1], f"postnorm_bwd{l}")
        dw_out[l] = _mm_tn([u2], dy, f"wgrad_out{l}")
        du = _mm_nt(dy, w_out[l], f"dgrad_out{l}").reshape(B, S, MIX_WIDTH)
        if host:
            early = [dw_in[1], dw_out[1], dw_out[0]]
        dqh, dfh, dih, dzh, dlb_l, dgh, got = _hgrn_bwd(
            proj_h, o_h, du, states, lb_param, g_head[l:l + 1], l, f"hgrn_bwd{l}",
            _reduce_d2d_phase(early, late_rs) if host else None)
        if host:
            parts = [_pair_sum(g, r, side, f"pair_sum{i}") for i, (g, r) in enumerate(zip(early, got))]
        dqa, dkv, dza, dsk, got = _attn_bwd(proj_a, du, _sink_rows(sinks[l]), cos, sin, f"attn_bwd{l}",
                                            _reduce_ici_phase(parts) if host else None)
        if host:
            dw_in[1], dw_out[1], dw_out[0] = [_chip_sum(p, r, f"chip_sum{i}")
                                              for i, (p, r) in enumerate(zip(parts, got))]
        dproj = [p.reshape(T, p.shape[-1]) for p in (dqh, dfh, dih, dzh, dqa, dkv, dza)]
        dw_in[l] = _mm_tn(dproj, h, f"wgrad_in{l}")
        dxn, dgpre = _inproj_bwd(dproj, w_in[l], x2, dxn, g_pre[l:l + 1], f"inproj_bwd{l}")
        dg_pre.append(dgpre)
        dg_post.append(dgp)
        dlb.append(dlb_l)
        dg_head.append(dgh)
        dsinks.append(dsk)
    if dist:
        got = _run_phase(_reduce_d2d_phase([dw_in[0]], [ra]), "reduce_in0_d2d")
        part = _pair_sum(dw_in[0], got[0], side, "pair_sum_in0")
        got = _run_phase(_reduce_ici_phase([part]), "reduce_in0_ici")
        dw_in[0] = _chip_sum(part, got[0], "chip_sum_in0")
    rev = lambda lst: jnp.concatenate(lst[::-1], axis=0)
    return (loss_part, dxn.reshape(B, S, D), jnp.stack(dw_in), jnp.stack(dw_out),
            rev(dg_pre), rev(dg_post), rev(dlb), rev(dg_head), rev(dsinks))


def _me_and_peers():
    x, y, c = lax.axis_index("x"), lax.axis_index("y"), lax.axis_index("c")
    me = 4 * x + 2 * y + c
    peers = []
    for k in range(1, N_DEV):
        px = 1 - x if k & 4 else x
        py = 1 - y if k & 2 else y
        pc = 1 - c if k & 1 else c
        peers.append(((px, py, pc), 4 * px + 2 * py + pc))
    return me, peers


class _Phase:
    def __init__(self, arrays, out_shapes, aliases, n_send, build):
        self.arrays, self.out_shapes, self.aliases = list(arrays), list(out_shapes), dict(aliases)
        self.n_send, self.build = n_send, build

    def scratch(self):
        return [pltpu.SemaphoreType.DMA((self.n_send,)), pltpu.SemaphoreType.DMA((self.n_send,))]

    def _copies(self, in_refs, out_refs, sems, arrivals):
        send_sems, recv_sems = sems
        sends, recvs = self.build(in_refs, out_refs)
        assert len(sends) == self.n_send == len(recvs)
        out = [pltpu.make_async_remote_copy(src_ref=s, dst_ref=d, send_sem=send_sems.at[i], recv_sem=recv_sems.at[i],
                                            device_id=dev, device_id_type=MESH) for i, (s, d, dev) in enumerate(sends)]
        inc = [pltpu.make_async_remote_copy(src_ref=s, dst_ref=r, send_sem=send_sems.at[i], recv_sem=recv_sems.at[i],
                                            device_id=dev, device_id_type=MESH)
               for i, ((s, _, dev), r) in enumerate(zip(sends, recvs))] if arrivals else []
        return out, inc

    def start(self, in_refs, out_refs, sems):
        out, _ = self._copies(in_refs, out_refs, sems, False)
        for cp in out:
            cp.start()

    def finish(self, in_refs, out_refs, sems):
        out, inc = self._copies(in_refs, out_refs, sems, True)
        for cp in inc:
            cp.wait_recv()
        for cp in out:
            cp.wait_send()


_ANY = pl.BlockSpec(memory_space=pl.ANY)


def _host_phase(phase, n_in, n_out):
    if phase is None:
        return [], [], [], {}, [], []
    aliases = {n_in + i: n_out + o for i, o in phase.aliases.items()}
    return ([_ANY] * len(phase.arrays), [_ANY] * len(phase.out_shapes), phase.out_shapes, aliases, phase.scratch(),
            phase.arrays)


def _split_refs(refs, n_in, n_out, n_scr, phase):
    pi = len(phase.arrays) if phase else 0
    po = len(phase.out_shapes) if phase else 0
    a = n_in + pi
    b = a + n_out + po
    return (refs[:n_in], refs[a:a + n_out], refs[b:b + n_scr], refs[n_in:a], refs[a + n_out:b], refs[b + n_scr:])


def _hosted_start(phase, p_in, p_out, p_sems, first):
    if phase is not None:
        @pl.when(first)
        def _():
            phase.start(p_in, p_out, p_sems)


def _hosted_finish(phase, p_in, p_out, p_sems, last):
    if phase is not None:
        @pl.when(last)
        def _():
            phase.finish(p_in, p_out, p_sems)


def _run_phase(phase, name):
    n_in, n_out = len(phase.arrays), len(phase.out_shapes)

    def body(*refs):
        phase.start(refs[:n_in], refs[n_in:n_in + n_out], refs[n_in + n_out:])
        phase.finish(refs[:n_in], refs[n_in:n_in + n_out], refs[n_in + n_out:])

    return pl.pallas_call(
        body, name=name, in_specs=[_ANY] * n_in, out_specs=[_ANY] * n_out,
        out_shape=phase.out_shapes, input_output_aliases=phase.aliases, scratch_shapes=phase.scratch(),
        compiler_params=pltpu.CompilerParams(has_side_effects=True),
    )(*phase.arrays)


def _mesh_place():
    x, y, c = lax.axis_index("x"), lax.axis_index("y"), lax.axis_index("c")
    chips = [(x, y), (1 - x, y), (x, 1 - y), (1 - x, 1 - y)]
    num = lambda chip, core: 4 * chip[0] + 2 * chip[1] + core
    return c, chips, num


def _own_side_blocks():
    c, chips, num = _mesh_place()
    return jnp.stack([num(ch, c) for ch in chips]).astype(jnp.int32)


def _rows(ref, r, dev):
    return ref.at[pl.ds(pl.multiple_of(dev * r, 16), r), :]


def _place_own(loc, blocks, name):
    r, D = loc.shape
    tr = _pick(r, (400, 256, 200, 128, 64, 16))

    def body(idx_ref, l_ref, o_ref):
        del idx_ref
        o_ref[...] = l_ref[...]

    return pl.pallas_call(
        body, name=name,
        grid_spec=pltpu.PrefetchScalarGridSpec(
            num_scalar_prefetch=1, grid=(r // tr,),
            in_specs=[pl.BlockSpec((tr, D), lambda i, idx: (i, 0))],
            out_specs=pl.BlockSpec((tr, D), lambda i, idx: (idx[0] * (r // tr) + i, 0))),
        out_shape=jax.ShapeDtypeStruct((N_DEV * r, D), loc.dtype),
        compiler_params=_params(("arbitrary",)),
    )(blocks, loc)


def _gather_ici_phase(locs, fulls):
    rs = [a.shape[0] for a in locs]
    n = len(locs)

    def build(ins, outs):
        c, chips, num = _mesh_place()
        me = num(chips[0], c)
        targets = [((*chips[0], 1 - c), num(chips[0], 1 - c))] + [((*ch, c), num(ch, c)) for ch in chips[1:]]
        sends, recvs = [], []
        for dev, dnum in targets:
            for i, r in enumerate(rs):
                sends.append((ins[i], _rows(outs[i], r, me), dev))
                recvs.append(_rows(outs[i], r, dnum))
        return sends, recvs

    shapes = [jax.ShapeDtypeStruct(a.shape, a.dtype) for a in fulls]
    return _Phase(list(locs) + list(fulls), shapes, {n + i: i for i in range(n)}, 4 * n, build)


def _gather_d2d_phase(fulls, rs):
    def build(ins, outs):
        c, chips, num = _mesh_place()
        sib = (*chips[0], 1 - c)
        sends, recvs = [], []
        for ch in chips[1:]:
            for i, r in enumerate(rs):
                blk = _rows(outs[i], r, num(ch, c))
                sends.append((blk, blk, sib))
                recvs.append(_rows(outs[i], r, num(ch, 1 - c)))
        return sends, recvs

    shapes = [jax.ShapeDtypeStruct(a.shape, a.dtype) for a in fulls]
    return _Phase(fulls, shapes, {i: i for i in range(len(fulls))}, 3 * len(fulls), build)


def _reduce_d2d_phase(grads, rs):
    def build(ins, outs):
        c, chips, num = _mesh_place()
        sib = (*chips[0], 1 - c)
        sends, recvs = [], []
        for j, ch in enumerate(chips):
            for i, r in enumerate(rs):
                sends.append((_rows(ins[i], r, num(ch, 1 - c)), outs[i].at[j], sib))
                recvs.append(outs[i].at[j])
        return sends, recvs

    shapes = [jax.ShapeDtypeStruct((4, r, g.shape[1]), g.dtype) for g, r in zip(grads, rs)]
    return _Phase(grads, shapes, {}, 4 * len(grads), build)


def _reduce_ici_phase(parts):
    def build(ins, outs):
        c, chips, _ = _mesh_place()
        sends, recvs = [], []
        for t in range(1, 4):
            for i in range(len(parts)):
                sends.append((ins[i].at[t], outs[i].at[t - 1], (*chips[t], c)))
                recvs.append(outs[i].at[t - 1])
        return sends, recvs

    shapes = [jax.ShapeDtypeStruct((3,) + p.shape[1:], p.dtype) for p in parts]
    return _Phase(parts, shapes, {}, 3 * len(parts), build)


def _pair_sum(g, got, blocks, name):
    n, r, D = got.shape
    tr = _pick(r, (400, 256, 200, 128, 64, 16))

    def body(idx_ref, g_ref, r_ref, o_ref):
        del idx_ref
        o_ref[...] = (g_ref[...].astype(F32) + r_ref[...].astype(F32)).astype(o_ref.dtype)

    blk = pl.BlockSpec((None, tr, D), lambda j, i, idx: (j, i, 0))
    return pl.pallas_call(
        body, name=name,
        grid_spec=pltpu.PrefetchScalarGridSpec(
            num_scalar_prefetch=1, grid=(n, r // tr),
            in_specs=[pl.BlockSpec((tr, D), lambda j, i, idx: (idx[j] * (r // tr) + i, 0)), blk],
            out_specs=blk),
        out_shape=jax.ShapeDtypeStruct(got.shape, got.dtype),
        compiler_params=_params(("arbitrary", "arbitrary")),
    )(blocks, g, got)


def _chip_sum(p, r, name):
    _, R, D = p.shape
    tr = _pick(R, (400, 256, 200, 128, 64, 16))

    def body(p_ref, r_ref, o_ref):
        acc = p_ref[...].astype(F32)
        for t in range(3):
            acc = acc + r_ref[t].astype(F32)
        o_ref[...] = acc

    return pl.pallas_call(
        body, name=name, grid=(R // tr,),
        in_specs=[pl.BlockSpec((None, tr, D), lambda i: (0, i, 0)), pl.BlockSpec((3, tr, D), lambda i: (0, i, 0))],
        out_specs=pl.BlockSpec((tr, D), lambda i: (i, 0)), out_shape=jax.ShapeDtypeStruct((R, D), F32),
        compiler_params=_params(("parallel",)))(p, r)


def _allreduce_small(vec):
    R, C = vec.shape

    def body(v_ref, o_ref, buf, send_sems, recv_sems):
        me, peers = _me_and_peers()
        buf[me] = v_ref[...]
        sends = []
        for k, (pid, _) in enumerate(peers):
            cp = pltpu.make_async_remote_copy(src_ref=v_ref, dst_ref=buf.at[me], send_sem=send_sems.at[k],
                                              recv_sem=recv_sems.at[k], device_id=pid, device_id_type=MESH)
            cp.start()
            sends.append(cp)
        for k, (pid, pnum) in enumerate(peers):
            pltpu.make_async_remote_copy(src_ref=v_ref, dst_ref=buf.at[pnum], send_sem=send_sems.at[k],
                                         recv_sem=recv_sems.at[k], device_id=pid, device_id_type=MESH).wait_recv()
        for cp in sends:
            cp.wait_send()
        acc = buf[0]
        for d in range(1, N_DEV):
            acc = acc + buf[d]
        o_ref[...] = acc

    vm = pl.BlockSpec(memory_space=pltpu.VMEM)
    return pl.pallas_call(
        body, name="allreduce_small",
        in_specs=[vm], out_specs=vm,
        out_shape=jax.ShapeDtypeStruct((R, C), F32),
        scratch_shapes=[pltpu.VMEM((N_DEV, R, C), F32), pltpu.SemaphoreType.DMA((N_DEV - 1,)),
                        pltpu.SemaphoreType.DMA((N_DEV - 1,))],
        compiler_params=pltpu.CompilerParams(has_side_effects=True),
    )(vec)


def _adamw(w, g, m, v, name):
    R, C = w.shape
    tr = _pick(R, (256, 128, 64, 32, 16, 8)) if R >= 8 else R
    c1 = 1.0 - ADAM_B1 ** ADAM_STEP
    c2 = 1.0 - ADAM_B2 ** ADAM_STEP

    def body(w_ref, g_ref, m_ref, v_ref, d_ref, mo_ref, vo_ref):
        gg = g_ref[...]
        mn = ADAM_B1 * m_ref[...] + (1.0 - ADAM_B1) * gg
        vn = ADAM_B2 * v_ref[...] + (1.0 - ADAM_B2) * (gg * gg)
        d_ref[...] = -ADAM_LR * ((mn / c1) / (jnp.sqrt(vn / c2) + ADAM_EPS) + ADAM_WD * w_ref[...])
        mo_ref[...] = mn
        vo_ref[...] = vn

    blk = pl.BlockSpec((tr, C), lambda i: (i, 0))
    sh = jax.ShapeDtypeStruct((R, C), F32)
    return pl.pallas_call(
        body, name=name, grid=(R // tr,), in_specs=[blk] * 4, out_specs=[blk] * 3, out_shape=[sh] * 3,
        compiler_params=_params(("parallel",)),
    )(w, g, m, v)


def _lb_param_grad(lb_param, dlb):
    L, C = lb_param.shape

    def body(p_ref, d_ref, o_ref):
        lbp = p_ref[...]
        d = d_ref[...]
        mx = jnp.max(lbp, axis=0, keepdims=True)
        e = jnp.exp(lbp - mx)
        p = e / jnp.sum(e, axis=0, keepdims=True)
        tot = jnp.sum(d, axis=0, keepdims=True)
        dps = []
        rest = tot
        for j in range(L):
            dps.append(rest - tot if j == 0 else rest)
            rest = rest - d[j:j + 1]
        dp = jnp.concatenate(dps, axis=0)
        o_ref[...] = p * (dp - jnp.sum(p * dp, axis=0, keepdims=True))

    vm = pl.BlockSpec(memory_space=pltpu.VMEM)
    return pl.pallas_call(body, name="lb_param_grad", in_specs=[vm, vm], out_specs=vm,
                          out_shape=jax.ShapeDtypeStruct((L, C), F32))(lb_param, dlb)


def _pack_small(loss_part, dg_pre, dg_post, dlb, dg_head, dsinks):
    pad8 = lambda a: jnp.pad(a.reshape(-1, 128), ((0, 8 - DEPTH), (0, 0)))
    rows = [dg_pre.reshape(-1, 128), dg_post.reshape(-1, 128), dlb.reshape(-1, 128), pad8(dg_head), pad8(dsinks),
            loss_part]
    return jnp.concatenate(rows, axis=0)


def _unpack_small(vec):
    n = DEPTH * D_MODEL // 128
    o = 0
    dg_pre = vec[o:o + n].reshape(DEPTH, D_MODEL); o += n
    dg_post = vec[o:o + n].reshape(DEPTH, D_MODEL); o += n
    dlb = vec[o:o + n].reshape(DEPTH, HG_WIDTH); o += n
    dg_head = vec[o:o + DEPTH]; o += 8
    dsinks = vec[o:o + DEPTH, :ATT_HEADS]; o += 8
    loss = jnp.sum(vec[o:o + 8])
    return loss, dg_pre, dg_post, dlb, dg_head, dsinks


def kernel(x, w_in, w_out, g_pre, g_post, lb_param, g_head, sinks, loss_target, m_w_in, m_w_out, m_g_pre, m_g_post, m_lb_param, m_g_head, m_sinks, v_w_in, v_w_out, v_g_pre, v_g_post, v_lb_param, v_g_head, v_sinks):
    L, D, nloc = w_in.shape
    w_in_t_loc = jnp.swapaxes(w_in, 1, 2).astype(BF16)
    (loss_part, dx, gw_in_t, gw_out, dg_pre, dg_post, dlb, dg_head, dsinks) = _step(
        x, loss_target, g_pre, g_post, lb_param, g_head, sinks, shards=(w_in_t_loc, w_out.astype(BF16)))
    gw_in = jnp.swapaxes(gw_in_t, 1, 2)

    small = _allreduce_small(_pack_small(loss_part, dg_pre, dg_post, dlb, dg_head, dsinks))
    loss, gg_pre, gg_post, gdlb, gg_head, gsinks = _unpack_small(small)
    glb = _lb_param_grad(lb_param, gdlb)

    grads = [gw_in, gw_out, gg_pre, gg_post, glb, gg_head, gsinks]
    ws = [w_in, w_out, g_pre, g_post, lb_param, g_head, sinks]
    ms = [m_w_in, m_w_out, m_g_pre, m_g_post, m_lb_param, m_g_head, m_sinks]
    vs = [v_w_in, v_w_out, v_g_pre, v_g_post, v_lb_param, v_g_head, v_sinks]
    names = ["w_in", "w_out", "g_pre", "g_post", "lb_param", "g_head", "sinks"]
    deltas, new_m, new_v = [], [], []
    for w, g, m, v, nm in zip(ws, grads, ms, vs, names):
        sh = w.shape
        two = lambda a: a.reshape(-1, sh[-1])
        d, mn, vn = _adamw(two(w), two(g), two(m), two(v), "adamw_" + nm)
        deltas.append(d.reshape(sh))
        new_m.append(mn.reshape(sh))
        new_v.append(vn.reshape(sh))
    return (loss, dx, *grads, *deltas, *new_m, *new_v)
```

```python
import functools
import math

import numpy as np
import jax
import jax.numpy as jnp
from jax import lax
from jax.experimental import pallas as pl
from jax.experimental.pallas import tpu as pltpu

F32 = jnp.float32
BF16 = jnp.bfloat16

D_MODEL = 1024
DEPTH = 2
HG_HEADS = 8
HG_DIM = 128
HG_WIDTH = HG_HEADS * HG_DIM
CHUNK = 64
ATT_HEADS = 16
ATT_DIM = 64
ATT_WIDTH = ATT_HEADS * ATT_DIM
KV_WIDTH = 128
ATT_BLOCK = 128
ATT_SCALE = 1.0 / math.sqrt(ATT_DIM)
ROPE_THETA = 10000.0
NORM_EPS = 1e-6
NEG_INF = -1e30
LB_FLOOR = 1e-20
N_H = 4 * HG_WIDTH
N_A = 2 * ATT_WIDTH + 2 * KV_WIDTH
IN_WIDTH = N_H + N_A
MIX_WIDTH = HG_WIDTH + ATT_WIDTH

ADAM_LR = 0.001
ADAM_B1 = 0.9
ADAM_B2 = 0.999
ADAM_EPS = 1e-08
ADAM_WD = 0.01
ADAM_STEP = 10

N_DEV = 8
MESH = pl.DeviceIdType.MESH
VMEM_LIMIT = 56 * 1024 * 1024

NN = ((1,), (0,))
NT = ((1,), (1,))
TN = ((0,), (0,))


def _dot(a, b, dims):
    return lax.dot_general(a.astype(BF16), b.astype(BF16), (dims, ((), ())), preferred_element_type=F32)


def _params(sem=None, **kw):
    return pltpu.CompilerParams(dimension_semantics=sem, vmem_limit_bytes=VMEM_LIMIT, **kw)


def _sigmoids(x):
    e = jnp.exp(-jnp.abs(x))
    r = 1.0 / (1.0 + e)
    er = e * r
    pos = x >= 0.0
    return jnp.where(pos, r, er), jnp.where(pos, er, r)


def _silu(x):
    return x * _sigmoids(x)[0]


def _silu_and_grad(x):
    s, ns = _sigmoids(x)
    return x * s, s * (1.0 + x * ns)


def _pick(n, prefs):
    for p in prefs:
        if n % p == 0:
            return p
    return n


def _inproj(x2, g, w, name):
    T, D = x2.shape
    tm = _pick(T, (256, 128))
    nchunk = 1024

    def body(x_ref, g_ref, w_ref, oh_ref, oa_ref, h_ref):
        x = x_ref[...]
        r = lax.rsqrt(jnp.mean(x * x, axis=-1, keepdims=True) + NORM_EPS)
        h = ((x * r) * g_ref[...]).astype(BF16)
        h_ref[...] = h
        for j in range(0, N_H, nchunk):
            oh_ref[:, j:j + nchunk] = lax.dot_general(h, w_ref[j:j + nchunk, :], (NT, ((), ())),
                                                      preferred_element_type=F32)
        for j in range(0, N_A, N_A // 2):
            oa_ref[:, j:j + N_A // 2] = lax.dot_general(h, w_ref[N_H + j:N_H + j + N_A // 2, :], (NT, ((), ())),
                                                        preferred_element_type=F32)

    row = lambda w_: pl.BlockSpec((tm, w_), lambda i: (i, 0))
    return pl.pallas_call(
        body, name=name,
        grid=(T // tm,),
        in_specs=[row(D), pl.BlockSpec((1, D), lambda i: (0, 0)),
                  pl.BlockSpec((IN_WIDTH, D), lambda i: (0, 0), pipeline_mode=pl.Buffered(1))],
        out_specs=[row(N_H), row(N_A), row(D)],
        out_shape=[jax.ShapeDtypeStruct((T, N_H), F32), jax.ShapeDtypeStruct((T, N_A), F32),
                   jax.ShapeDtypeStruct((T, D), BF16)],
        compiler_params=_params(("parallel",)),
    )(x2, g, w)


def _mm_nt(a, b, name, out_dtype=F32):
    M, K = a.shape
    N = b.shape[0]
    tm = _pick(M, (512, 256, 128))

    def body(a_ref, b_ref, o_ref):
        o_ref[...] = lax.dot_general(a_ref[...], b_ref[...], (NT, ((), ())),
                                     preferred_element_type=F32).astype(out_dtype)

    return pl.pallas_call(
        body, name=name,
        grid=(M // tm,),
        in_specs=[pl.BlockSpec((tm, K), lambda i: (i, 0)),
                  pl.BlockSpec((N, K), lambda i: (0, 0), pipeline_mode=pl.Buffered(1))],
        out_specs=pl.BlockSpec((tm, N), lambda i: (i, 0)),
        out_shape=jax.ShapeDtypeStruct((M, N), out_dtype),
        compiler_params=_params(("parallel",)),
    )(a, b)


def _mm_tn(pieces, b, name, out_dtype=BF16):
    T, m = b.shape
    tn = 256
    counts = [p.shape[1] // tn for p in pieces]
    starts = [sum(counts[:i]) for i in range(len(pieces))]
    n_p = len(pieces)

    def body(*refs):
        b_ref, o_ref = refs[n_p], refs[n_p + 1]
        i = pl.program_id(0)
        for p in range(n_p):
            @pl.when((i >= starts[p]) & (i < starts[p] + counts[p]))
            def _(p=p):
                o_ref[...] = lax.dot_general(refs[p][...], b_ref[...], (TN, ((), ())),
                                             preferred_element_type=F32).astype(out_dtype)

    piece_spec = lambda s, c: pl.BlockSpec((T, tn), lambda i: (0, jnp.clip(i - s, 0, c - 1)))
    return pl.pallas_call(
        body, name=name,
        grid=(sum(counts),),
        in_specs=[piece_spec(s, c) for s, c in zip(starts, counts)]
        + [pl.BlockSpec((T, m), lambda i: (0, 0), pipeline_mode=pl.Buffered(1))],
        out_specs=pl.BlockSpec((tn, m), lambda i: (i, 0)),
        out_shape=jax.ShapeDtypeStruct((sum(counts) * tn, m), out_dtype),
        compiler_params=_params(("arbitrary",)),
    )(*pieces, b)


_LEVELS = (0, 1, 2, 4, 8, 16, 32)
_CUM_L = (2, 4, 8, 16, 32, 64)
_ALL_KINDS = tuple(("c", L) for L in _CUM_L) + tuple(("r", L) for L in _CUM_L)
_MXU_KINDS = (("c", 2), ("c", 4), ("c", CHUNK), ("r", 2), ("r", 4))
N_CUM = len(_ALL_KINDS) * CHUNK
N_CUM_F = len(_MXU_KINDS) * CHUNK


def _cum_matrices():
    t = np.arange(CHUNK)[:, None]
    r = np.arange(CHUNK)[None, :]

    def mat(kind):
        c, L = kind
        return ((r // L == t // L) & ((r <= t) if c == "c" else (r > t))).astype(np.float32)

    fwd = np.concatenate([mat(kd) for kd in _MXU_KINDS], axis=0)
    full = np.concatenate([mat(kd) for kd in _ALL_KINDS], axis=0)
    return jnp.asarray(fwd, BF16), jnp.asarray(full.T.copy(), BF16)


def _level_masks():
    t = np.arange(CHUNK)[:, None]
    s = np.arange(CHUNK)[None, :]
    ms = []
    for L in _LEVELS:
        if L == 0:
            ms.append(t == s)
        else:
            ms.append((t // (2 * L) == s // (2 * L)) & ((t // L) % 2 == 1) & ((s // L) % 2 == 0))
    return jnp.asarray(np.stack(ms).astype(np.float32))


def _split3(x):
    hi = x.astype(BF16)
    r1 = x - hi.astype(F32)
    mid = r1.astype(BF16)
    lo = (r1 - mid.astype(F32)).astype(BF16)
    return hi, mid, lo


def _cum3(ts, x, terms=3):
    d = lambda p: lax.dot_general(ts, p, (NN, ((), ())), preferred_element_type=F32)
    return sum(d(p) for p in _split3(x)[:terms])


def _lb_terms(lbp, layer):
    mx = jnp.max(lbp, axis=0, keepdims=True)
    e = jnp.exp(lbp - mx)
    p = e / jnp.sum(e, axis=0, keepdims=True)
    cum = p[0:1]
    for j in range(1, layer + 1):
        cum = cum + p[j:j + 1]
    lb = cum - p[0:1]
    lbf = jnp.maximum(lb, LB_FLOOR)
    return dict(lbf=lbf, one_m=1.0 - lb, kcorr=lb - lbf, ind=jnp.where(lb > LB_FLOOR, 1.0, 0.0))


def _gate(x, lt):
    sig, nsig = _sigmoids(x)
    f = lt["lbf"] + lt["one_m"] * sig
    return jnp.log(f), lt["one_m"] * nsig + lt["kcorr"], f, sig, nsig


def _ck(x, ci):
    return x[ci * CHUNK:(ci + 1) * CHUNK]


def _block_cums(ts, g, nc):
    cs = [_cum3(ts, _ck(g, ci)) for ci in range(nc)]
    out = {kind: jnp.concatenate([c[CHUNK * i:CHUNK * (i + 1)] for c in cs], axis=0)
           for i, kind in enumerate(_MXU_KINDS)}
    b = out[("c", CHUNK)]
    ng = CHUNK // 8
    last = b.reshape(nc, ng, 8, HG_DIM)[:, :, 7:8, :]
    zero = jnp.zeros((nc, 1, 1, HG_DIM), F32)

    def spread(groups):
        return jnp.broadcast_to(jnp.concatenate(groups, axis=1), (nc, ng, 8, HG_DIM)).reshape(nc * CHUNK, HG_DIM)

    def get(kind):
        if kind in out:
            return out[kind]
        c, L = kind
        nb = L // 8
        first = lambda r: (r // nb) * nb
        if c == "c":
            return b - spread([last[:, first(r) - 1:first(r)] if r >= nb else zero for r in range(ng)])
        return spread([last[:, first(r) + nb - 1:first(r) + nb] for r in range(ng)]) - b

    return get


def _level_factors(cums, g, L):
    if L == 0:
        return None, None
    if L == 1:
        return jnp.exp(g), None
    return jnp.exp(cums(("c", L))), jnp.exp(cums(("r", L)))


def _mul(a, e):
    return a if e is None else a * e


def _hg_block_fwd(qf, k, v, g, ts, m_ref, nc):
    cums = _block_cums(ts, g, nc)
    amat = [jnp.zeros((CHUNK, CHUNK), F32)] * nc
    for li, L in enumerate(_LEVELS):
        eq, ek = _level_factors(cums, g, L)
        ql, kl, m = _mul(qf, eq), _mul(k, ek), m_ref[li]
        amat = [amat[ci] + _dot(_ck(ql, ci), _ck(kl, ci), NT) * m for ci in range(nc)]
    b = cums(("c", CHUNK))
    kst = k * jnp.exp(cums(("r", CHUNK)))
    o = [_dot(amat[ci], _ck(v, ci), NN) for ci in range(nc)]
    kv = [_dot(_ck(v, ci), _ck(kst, ci), TN) for ci in range(nc)]
    dec = [jnp.exp(b[(ci + 1) * CHUNK - 1:(ci + 1) * CHUNK, :]) for ci in range(nc)]
    return o, dec, kv, qf * jnp.exp(b)


def _hg_block_bwd(qf, k, v, g, do, ts, m_ref, nc):
    cums = _block_cums(ts, g, nc)
    dcs = {}
    da = [_dot(_ck(do, ci), _ck(v, ci), NT) for ci in range(nc)]
    dq = jnp.zeros_like(qf)
    dk = jnp.zeros_like(qf)
    dg = jnp.zeros_like(qf)
    amat = [jnp.zeros((CHUNK, CHUNK), F32)] * nc
    for li, L in enumerate(_LEVELS):
        eq, ek = _level_factors(cums, g, L)
        ql, kl, m = _mul(qf, eq), _mul(k, ek), m_ref[li]
        qlb, klb = ql.astype(BF16), kl.astype(BF16)
        amat = [amat[ci] + _dot(_ck(qlb, ci), _ck(klb, ci), NT) * m for ci in range(nc)]
        dal = [(da[ci] * m).astype(BF16) for ci in range(nc)]
        dql = jnp.concatenate([_dot(dal[ci], _ck(klb, ci), NN) for ci in range(nc)], axis=0)
        dkl = jnp.concatenate([_dot(dal[ci], _ck(qlb, ci), TN) for ci in range(nc)], axis=0)
        dq = dq + _mul(dql, eq)
        dk = dk + _mul(dkl, ek)
        if L == 1:
            dg = dg + dql * ql
        elif L > 1:
            dcs[("c", L)] = (dql * ql).astype(BF16)
            dcs[("r", L)] = (dkl * kl).astype(BF16)
    b = cums(("c", CHUNK))
    e64 = jnp.exp(b)
    er64 = jnp.exp(cums(("r", CHUNK)))
    qb = qf * e64
    return dict(dq=dq, dk=dk, dg=dg, dcs=dcs, e64=e64, er64=er64, qb=qb, kst=k * er64,
                dv=[_dot(amat[ci], _ck(do, ci), TN) for ci in range(nc)],
                dec=[jnp.exp(b[(ci + 1) * CHUNK - 1:(ci + 1) * CHUNK, :]) for ci in range(nc)],
                qd=[_dot(_ck(do, ci), _ck(qb, ci), TN) for ci in range(nc)])


def _hg_state_bwd(w, v, do, starts, ends, tst, nc):
    dqb = jnp.concatenate([_dot(_ck(do, ci), starts[ci], NN) for ci in range(nc)], axis=0)
    dkst = jnp.concatenate([_dot(_ck(v, ci), ends[ci], NN) for ci in range(nc)], axis=0)
    dq = w["dq"] + dqb * w["e64"]
    dk = w["dk"] + dkst * w["er64"]
    dv = jnp.concatenate([w["dv"][ci] + _dot(_ck(w["kst"], ci), ends[ci], NT) for ci in range(nc)], axis=0)
    trow = lax.broadcasted_iota(jnp.int32, (CHUNK, 1), 0)
    dtot = jnp.concatenate(
        [jnp.where(trow == CHUNK - 1, jnp.sum(ends[ci] * starts[ci], axis=0, keepdims=True) * w["dec"][ci], 0.0)
         for ci in range(nc)], axis=0)
    dcs = dict(w["dcs"])
    dcs[("c", CHUNK)] = (dqb * w["qb"] + dtot).astype(BF16)
    dcs[("r", CHUNK)] = (dkst * w["kst"]).astype(BF16)
    dgs = [_dot(tst, jnp.concatenate([_ck(dcs[kind], ci) for kind in _ALL_KINDS], axis=0), NN) for ci in range(nc)]
    return dq, dk, dv, w["dg"] + jnp.concatenate(dgs, axis=0)


def _hgrn_fwd(proj_h, u_rows, lb_param, g_head, layer, name, phase=None):
    B, S, _ = proj_h.shape
    sb = _pick(S, (512, 256, 128, 64))
    nc = sb // CHUNK
    ts, _ = _cum_matrices()

    def body(*refs):
        ins, outs, (st,), p_in, p_out, p_sems = _split_refs(refs, 8, 3, 1, phase)
        q_ref, f_ref, i_ref, z_ref, lbp_ref, gh_ref, ts_ref, m_ref = ins
        o_ref, u_ref, sts_ref = outs
        h_id, b_id, s_id = pl.program_id(0), pl.program_id(1), pl.program_id(2)
        _hosted_start(phase, p_in, p_out, p_sems, (h_id == 0) & (b_id == 0) & (s_id == 0))

        @pl.when(s_id == 0)
        def _():
            st[...] = jnp.zeros_like(st)

        lt = _lb_terms(lbp_ref[...], layer)
        tsv = ts_ref[...]
        gh = gh_ref[...]
        logf, k = _gate(f_ref[...], lt)[:2]
        o_part, dec, kv, qb = _hg_block_fwd(_silu(q_ref[...]), k, i_ref[...], logf, tsv, m_ref, nc)
        cur = st[...]
        starts = []
        for ci in range(nc):
            sts_ref[ci] = cur
            starts.append(cur)
            cur = cur * dec[ci] + kv[ci]
        st[...] = cur
        o = jnp.concatenate([o_part[ci] + _dot(_ck(qb, ci), starts[ci], NT) for ci in range(nc)], axis=0)
        o_ref[...] = o
        r = lax.rsqrt(jnp.mean(o * o, axis=-1, keepdims=True) + NORM_EPS)
        u_ref[...] = (((o * r) * gh) * _silu(z_ref[...])).astype(BF16)
        _hosted_finish(phase, p_in, p_out, p_sems, (h_id == HG_HEADS - 1) & (b_id == B - 1) & (s_id == S // sb - 1))

    col = lambda base: pl.BlockSpec((None, sb, HG_DIM), lambda h, b, s: (b, s, base + h))
    p_ispecs, p_ospecs, p_oshapes, p_alias, p_scratch, p_args = _host_phase(phase, 8, 3)
    res = pl.pallas_call(
        body, name=name,
        grid=(HG_HEADS, B, S // sb),
        in_specs=[col(0), col(HG_HEADS), col(2 * HG_HEADS), col(3 * HG_HEADS),
                  pl.BlockSpec((DEPTH, HG_DIM), lambda h, b, s: (0, h)),
                  pl.BlockSpec((1, HG_DIM), lambda h, b, s: (0, 0)),
                  pl.BlockSpec((N_CUM_F, CHUNK), lambda h, b, s: (0, 0)),
                  pl.BlockSpec((len(_LEVELS), CHUNK, CHUNK), lambda h, b, s: (0, 0, 0))] + p_ispecs,
        out_specs=[col(0), col(0),
                   pl.BlockSpec((None, None, nc, HG_DIM, HG_DIM), lambda h, b, s: (b, h, s, 0, 0))] + p_ospecs,
        out_shape=[jax.ShapeDtypeStruct((B, S, HG_WIDTH), F32),
                   jax.ShapeDtypeStruct((B, S, u_rows), BF16),
                   jax.ShapeDtypeStruct((B, HG_HEADS, S // CHUNK, HG_DIM, HG_DIM), F32)] + p_oshapes,
        input_output_aliases=p_alias,
        scratch_shapes=[pltpu.VMEM((HG_DIM, HG_DIM), F32)] + p_scratch,
        compiler_params=_params(("arbitrary", "arbitrary", "arbitrary")),
    )(proj_h, proj_h, proj_h, proj_h, lb_param, g_head, ts, _level_masks(), *p_args)
    return res[0], res[1], res[2], list(res[3:])


def _hgrn_bwd(proj_h, o_h, du, states, lb_param, g_head, layer, name, phase=None):
    B, S, _ = proj_h.shape
    sb = _pick(S, (512, 256, 128, 64))
    nc = sb // CHUNK
    ns = S // sb
    ts, tst = _cum_matrices()

    def body(*refs):
        ins, outs, (dst,), p_in, p_out, p_sems = _split_refs(refs, 12, 6, 1, phase)
        q_ref, f_ref, i_ref, z_ref, o_ref, du_ref, sts_ref, lbp_ref, gh_ref, ts_ref, tst_ref, m_ref = ins
        dq_ref, df_ref, di_ref, dz_ref, dlb_ref, dgh_ref = outs
        h_id, b_id, s_id = pl.program_id(0), pl.program_id(1), pl.program_id(2)
        _hosted_start(phase, p_in, p_out, p_sems, (h_id == 0) & (b_id == 0) & (s_id == 0))

        @pl.when(s_id == 0)
        def _():
            dst[...] = jnp.zeros_like(dst)

        @pl.when((b_id == 0) & (s_id == 0))
        def _():
            dlb_ref[...] = jnp.zeros_like(dlb_ref)

        @pl.when((h_id == 0) & (b_id == 0) & (s_id == 0))
        def _():
            dgh_ref[...] = jnp.zeros_like(dgh_ref)

        lt = _lb_terms(lbp_ref[...], layer)
        gh = gh_ref[...]
        tsv = ts_ref[...]
        tstv = tst_ref[...]
        logf, k, f, sig, nsig = _gate(f_ref[...], lt)
        o = o_ref[...]
        dub = du_ref[...]
        r = lax.rsqrt(jnp.mean(o * o, axis=-1, keepdims=True) + NORM_EPS)
        n = o * r
        sg, sg_grad = _silu_and_grad(z_ref[...])
        dz_ref[...] = (dub * (n * gh) * sg_grad).astype(BF16)
        dgh_ref[...] += jnp.sum(dub * sg * n, axis=0, keepdims=True)
        dn = dub * sg * gh
        do = r * (dn - n * jnp.mean(dn * n, axis=-1, keepdims=True))
        v = i_ref[...]
        qf, qf_grad = _silu_and_grad(q_ref[...])
        w = _hg_block_bwd(qf, k, v, logf, do, tsv, m_ref, nc)
        cur = dst[...]
        ends = [None] * nc
        for ci in reversed(range(nc)):
            ends[ci] = cur
            cur = cur * w["dec"][ci] + w["qd"][ci]
        dst[...] = cur
        dq, dk, dv, dg = _hg_state_bwd(w, v, do, [sts_ref[ci] for ci in range(nc)], ends, tstv, nc)
        di_ref[...] = dv.astype(BF16)
        dq_ref[...] = (dq * qf_grad).astype(BF16)
        scaled = (dg - f * dk) / f
        df_ref[...] = (scaled * lt["one_m"] * sig * nsig).astype(BF16)
        dlb_ref[...] += jnp.sum(scaled * (lt["ind"] - sig), axis=0, keepdims=True)
        _hosted_finish(phase, p_in, p_out, p_sems, (h_id == HG_HEADS - 1) & (b_id == B - 1) & (s_id == ns - 1))

    col = lambda base: pl.BlockSpec((None, sb, HG_DIM), lambda h, b, s: (b, ns - 1 - s, base + h))
    out_col = pl.BlockSpec((None, sb, HG_DIM), lambda h, b, s: (b, ns - 1 - s, h))
    dt = jax.ShapeDtypeStruct((B, S, HG_WIDTH), BF16)
    p_ispecs, p_ospecs, p_oshapes, p_alias, p_scratch, p_args = _host_phase(phase, 12, 6)
    res = pl.pallas_call(
        body, name=name,
        grid=(HG_HEADS, B, ns),
        in_specs=[col(0), col(HG_HEADS), col(2 * HG_HEADS), col(3 * HG_HEADS), col(0), col(0),
                  pl.BlockSpec((None, None, nc, HG_DIM, HG_DIM), lambda h, b, s: (b, h, ns - 1 - s, 0, 0)),
                  pl.BlockSpec((DEPTH, HG_DIM), lambda h, b, s: (0, h)),
                  pl.BlockSpec((1, HG_DIM), lambda h, b, s: (0, 0)),
                  pl.BlockSpec((N_CUM_F, CHUNK), lambda h, b, s: (0, 0)),
                  pl.BlockSpec((CHUNK, N_CUM), lambda h, b, s: (0, 0)),
                  pl.BlockSpec((len(_LEVELS), CHUNK, CHUNK), lambda h, b, s: (0, 0, 0))] + p_ispecs,
        out_specs=[out_col, out_col, out_col, out_col,
                   pl.BlockSpec((1, HG_DIM), lambda h, b, s: (0, h)),
                   pl.BlockSpec((1, HG_DIM), lambda h, b, s: (0, 0))] + p_ospecs,
        out_shape=[dt, dt, dt, dt, jax.ShapeDtypeStruct((1, HG_WIDTH), F32),
                   jax.ShapeDtypeStruct((1, HG_DIM), F32)] + p_oshapes,
        input_output_aliases=p_alias,
        scratch_shapes=[pltpu.VMEM((HG_DIM, HG_DIM), F32)] + p_scratch,
        compiler_params=_params(("arbitrary", "arbitrary", "arbitrary")),
    )(proj_h, proj_h, proj_h, proj_h, o_h, du, states, lb_param, g_head, ts, tst, _level_masks(), *p_args)
    return tuple(res[:6]) + (list(res[6:]),)


def _rope_tables(S):
    half = ATT_DIM // 2
    inv_freq = ROPE_THETA ** (-jnp.arange(half, dtype=F32) / half)
    ang = jnp.arange(S).astype(F32)[:, None] * inv_freq[None, :]
    cos = jnp.cos(ang)
    sin = jnp.sin(ang)
    cos = jnp.concatenate([cos, cos, cos, cos], axis=1)
    sin = jnp.concatenate([-sin, sin, -sin, sin], axis=1)
    return cos, sin


def _attn_common():
    lane = lax.broadcasted_iota(jnp.int32, (1, 2 * ATT_DIM), 1)
    first_half = (lane % ATT_DIM) < (ATT_DIM // 2)
    left = lane < ATT_DIM

    def swap(x):
        return jnp.where(first_half, pltpu.roll(x, 128 - ATT_DIM // 2, 1), pltpu.roll(x, ATT_DIM // 2, 1))

    def rope(x, cos, sin):
        return x * cos + swap(x) * sin

    def rope_bwd(dy, cos, sin):
        return dy * cos + swap(dy * sin)

    def dup(x):
        xs = pltpu.roll(x, ATT_DIM, 1)
        return [jnp.where(left, x, xs), jnp.where(left, xs, x)]

    return left, rope, rope_bwd, dup


GROUP = ATT_HEADS // 2
GROUP_ROWS = GROUP * ATT_BLOCK


def _attn_bias(i):
    r = lax.broadcasted_iota(jnp.int32, (ATT_BLOCK, 2 * ATT_BLOCK), 0)
    c = lax.broadcasted_iota(jnp.int32, (ATT_BLOCK, 2 * ATT_BLOCK), 1)
    ok = (c > r) & (c <= r + ATT_BLOCK) & ((c >= ATT_BLOCK) | (i > 0))
    return jnp.where(ok, 0.0, NEG_INF)


def _stack_heads(pairs, left):
    rows = []
    for x in pairs:
        rows += [jnp.where(left, x, 0.0), jnp.where(left, 0.0, x)]
    return jnp.concatenate(rows, axis=0)


def _unstack_heads(y, left, pp):
    r0 = 2 * pp * ATT_BLOCK
    return jnp.where(left, y[r0:r0 + ATT_BLOCK], y[r0 + ATT_BLOCK:r0 + 2 * ATT_BLOCK])


def _row_sums(x):
    return _dot(x, jnp.ones((x.shape[1], 128), BF16), NN)


def _attn_probs(qs, kd, sink, bias):
    s = _dot(qs, kd, NT).reshape(GROUP, ATT_BLOCK, 2 * ATT_BLOCK) * ATT_SCALE + bias[None]
    s = s.reshape(GROUP_ROWS, 2 * ATT_BLOCK)
    m = jnp.max(jnp.maximum(jnp.maximum(s[:, :128], s[:, 128:]), sink), axis=-1, keepdims=True)
    p = jnp.exp(s - m)
    es = jnp.exp(sink - m)
    inv = 1.0 / (_row_sums(p) + es)
    return p * jnp.concatenate([inv, inv], axis=1), es * inv


def _sink_rows(sinks_l):
    return jnp.broadcast_to(jnp.repeat(sinks_l, ATT_BLOCK)[:, None], (ATT_HEADS * ATT_BLOCK, 128))


_Z0 = (2 * ATT_WIDTH + 2 * KV_WIDTH - ATT_WIDTH) // 256


def _attn_fwd(proj_a, u, sinks_l, cos, sin, name, phase=None):
    B, S, _ = proj_a.shape
    nb = S // ATT_BLOCK

    def body(*refs):
        ins, (u_ref,), _, p_in, p_out, p_sems = _split_refs(refs, 13, 1, 0, phase)
        q_ref, kvc_ref, kvp_ref, z0, z1, z2, z3, cos_ref, sin_ref, cosp_ref, sinp_ref, sinks_ref, _ = ins
        i = pl.program_id(1)
        _hosted_start(phase, p_in, p_out, p_sems, (pl.program_id(0) == 0) & (i == 0))
        left, rope, _, dup = _attn_common()
        cos_c, sin_c = cos_ref[...], sin_ref[...]
        kvc = kvc_ref[...]
        kvp = kvp_ref[...]
        kw = jnp.concatenate([rope(kvp[:, :KV_WIDTH], cosp_ref[...], sinp_ref[...]),
                              rope(kvc[:, :KV_WIDTH], cos_c, sin_c)], axis=0)
        vw = jnp.concatenate([kvp[:, KV_WIDTH:], kvc[:, KV_WIDTH:]], axis=0)
        kd, vd = dup(kw), dup(vw)
        bias = _attn_bias(i)
        zs = (z0, z1, z2, z3)
        for kvh in range(2):
            pairs = range(4 * kvh, 4 * kvh + 4)
            qs = _stack_heads([rope(q_ref[:, 128 * pr:128 * (pr + 1)], cos_c, sin_c) for pr in pairs], left)
            p, _ = _attn_probs(qs, kd[kvh], sinks_ref[kvh * GROUP_ROWS:(kvh + 1) * GROUP_ROWS, :], bias)
            o = _dot(p, vd[kvh], NN)
            for pp, pr in enumerate(pairs):
                z = zs[pr // 2][:, 128 * (pr % 2):128 * (pr % 2 + 1)]
                u_ref[:, 128 * pr:128 * (pr + 1)] = (_unstack_heads(o, left, pp) * _silu(z)).astype(BF16)
        _hosted_finish(phase, p_in, p_out, p_sems, (pl.program_id(0) == B - 1) & (i == nb - 1))

    rowblk = lambda w, cb: pl.BlockSpec((None, ATT_BLOCK, w), lambda b, i: (b, i, cb))
    tab = pl.BlockSpec((ATT_BLOCK, 128), lambda b, i: (i, 0))
    tabp = pl.BlockSpec((ATT_BLOCK, 128), lambda b, i: (jnp.maximum(i - 1, 0), 0))
    p_ispecs, p_ospecs, p_oshapes, p_alias, p_scratch, p_args = _host_phase(phase, 13, 1)
    res = pl.pallas_call(
        body, name=name,
        grid=(B, nb),
        in_specs=[rowblk(ATT_WIDTH, 0), rowblk(256, 4),
                  pl.BlockSpec((None, ATT_BLOCK, 256), lambda b, i: (b, jnp.maximum(i - 1, 0), 4)),
                  rowblk(256, _Z0), rowblk(256, _Z0 + 1), rowblk(256, _Z0 + 2), rowblk(256, _Z0 + 3),
                  tab, tab, tabp, tabp,
                  pl.BlockSpec((ATT_HEADS * ATT_BLOCK, 128), lambda b, i: (0, 0)),
                  pl.BlockSpec(memory_space=pl.ANY)] + p_ispecs,
        out_specs=[pl.BlockSpec((None, ATT_BLOCK, ATT_WIDTH), lambda b, i: (b, i, 1))] + p_ospecs,
        out_shape=[jax.ShapeDtypeStruct(u.shape, BF16)] + p_oshapes,
        input_output_aliases={12: 0, **p_alias},
        scratch_shapes=p_scratch,
        compiler_params=_params(("arbitrary", "arbitrary")),
    )(proj_a, proj_a, proj_a, proj_a, proj_a, proj_a, proj_a, cos, sin, cos, sin, sinks_l, u, *p_args)
    return res[0], list(res[1:])


def _attn_bwd(proj_a, du, sinks_l, cos, sin, name, phase=None):
    B, S, _ = proj_a.shape
    nb = S // ATT_BLOCK

    def body(*refs):
        ins, outs, (carry, sk_acc), p_in, p_out, p_sems = _split_refs(refs, 13, 4, 2, phase)
        q_ref, kvc_ref, kvp_ref, z0, z1, z2, z3, du_ref, cos_ref, sin_ref, cosp_ref, sinp_ref, sinks_ref = ins
        dq_ref, dkv_ref, dz_ref, dsk_ref = outs
        b_id, i = pl.program_id(0), pl.program_id(1)
        _hosted_start(phase, p_in, p_out, p_sems, (b_id == 0) & (i == 0))

        @pl.when((b_id == 0) & (i == 0))
        def _():
            sk_acc[...] = jnp.zeros_like(sk_acc)

        @pl.when(i == 0)
        def _():
            carry[...] = jnp.zeros_like(carry)

        @pl.when(i < nb)
        def _():
            left, rope, rope_bwd, dup = _attn_common()
            cos_c, sin_c = cos_ref[...], sin_ref[...]
            cos_p, sin_p = cosp_ref[...], sinp_ref[...]
            kvc = kvc_ref[...]
            kvp = kvp_ref[...]
            kw = jnp.concatenate([rope(kvp[:, :KV_WIDTH], cos_p, sin_p), rope(kvc[:, :KV_WIDTH], cos_c, sin_c)], axis=0)
            vw = jnp.concatenate([kvp[:, KV_WIDTH:], kvc[:, KV_WIDTH:]], axis=0)
            kd, vd = dup(kw), dup(vw)
            bias = _attn_bias(i)
            zs = (z0, z1, z2, z3)
            dkd, dvd = [], []
            for kvh in range(2):
                pairs = range(4 * kvh, 4 * kvh + 4)
                qs = _stack_heads([rope(q_ref[:, 128 * pr:128 * (pr + 1)], cos_c, sin_c) for pr in pairs], left)
                p, ps = _attn_probs(qs, kd[kvh], sinks_ref[kvh * GROUP_ROWS:(kvh + 1) * GROUP_ROWS, :], bias)
                o = _dot(p, vd[kvh], NN)
                dos = []
                for pp, pr in enumerate(pairs):
                    cols = slice(128 * pr, 128 * (pr + 1))
                    sg, sg_grad = _silu_and_grad(zs[pr // 2][:, 128 * (pr % 2):128 * (pr % 2 + 1)])
                    du128 = du_ref[:, cols]
                    dz_ref[:, cols] = (du128 * _unstack_heads(o, left, pp) * sg_grad).astype(BF16)
                    dos.append(du128 * sg)
                dos = _stack_heads(dos, left)
                dp = _dot(dos, vd[kvh], NT)
                delta = _row_sums(p * dp)
                ds = p * (dp - jnp.concatenate([delta, delta], axis=1)) * ATT_SCALE
                sk_acc[kvh] += -ps * delta
                dqs = _dot(ds, kd[kvh], NN)
                for pp, pr in enumerate(pairs):
                    dq_ref[:, 128 * pr:128 * (pr + 1)] = rope_bwd(_unstack_heads(dqs, left, pp), cos_c, sin_c).astype(BF16)
                dkd.append(_dot(ds, qs, TN))
                dvd.append(_dot(p, dos, TN))
            fold = lambda pr: jnp.where(left, pr[0] + pltpu.roll(pr[0], ATT_DIM, 1), pr[1] + pltpu.roll(pr[1], ATT_DIM, 1))
            dkw = fold(dkd)
            dvw = fold(dvd)
            prev = jnp.concatenate([rope_bwd(dkw[:ATT_BLOCK], cos_p, sin_p), dvw[:ATT_BLOCK]], axis=1)
            cur = jnp.concatenate([rope_bwd(dkw[ATT_BLOCK:], cos_c, sin_c), dvw[ATT_BLOCK:]], axis=1)
            dkv_ref[...] = (carry[...] + prev).astype(BF16)
            carry[...] = cur

        @pl.when(i == nb)
        def _():
            dkv_ref[...] = carry[...].astype(BF16)

        @pl.when((b_id == B - 1) & (i == nb))
        def _():
            lane = lax.broadcasted_iota(jnp.int32, (1, 128), 1)
            tot = jnp.zeros((1, 128), F32)
            for hd in range(ATT_HEADS):
                rows = sk_acc[hd // GROUP, (hd % GROUP) * ATT_BLOCK:(hd % GROUP + 1) * ATT_BLOCK, :]
                tot = tot + jnp.where(lane == hd, jnp.sum(rows, axis=0, keepdims=True), 0.0)
            dsk_ref[...] = tot

        _hosted_finish(phase, p_in, p_out, p_sems, (b_id == B - 1) & (i == nb))

    cl = lambda i: jnp.minimum(i, nb - 1)
    pv = lambda i: jnp.maximum(jnp.minimum(i, nb - 1) - 1, 0)
    rowblk = lambda w, cb: pl.BlockSpec((None, ATT_BLOCK, w), lambda b, i: (b, cl(i), cb))
    tab = pl.BlockSpec((ATT_BLOCK, 128), lambda b, i: (cl(i), 0))
    tabp = pl.BlockSpec((ATT_BLOCK, 128), lambda b, i: (pv(i), 0))
    p_ispecs, p_ospecs, p_oshapes, p_alias, p_scratch, p_args = _host_phase(phase, 13, 4)
    res = pl.pallas_call(
        body, name=name,
        grid=(B, nb + 1),
        in_specs=[rowblk(ATT_WIDTH, 0), rowblk(256, 4),
                  pl.BlockSpec((None, ATT_BLOCK, 256), lambda b, i: (b, pv(i), 4)),
                  rowblk(256, _Z0), rowblk(256, _Z0 + 1), rowblk(256, _Z0 + 2), rowblk(256, _Z0 + 3),
                  rowblk(ATT_WIDTH, 1),
                  tab, tab, tabp, tabp,
                  pl.BlockSpec((ATT_HEADS * ATT_BLOCK, 128), lambda b, i: (0, 0))] + p_ispecs,
        out_specs=[rowblk(ATT_WIDTH, 0),
                   pl.BlockSpec((None, ATT_BLOCK, 256), lambda b, i: (b, jnp.maximum(i - 1, 0), 0)),
                   rowblk(ATT_WIDTH, 0),
                   pl.BlockSpec((1, 128), lambda b, i: (0, 0))] + p_ospecs,
        out_shape=[jax.ShapeDtypeStruct((B, S, ATT_WIDTH), BF16), jax.ShapeDtypeStruct((B, S, 256), BF16),
                   jax.ShapeDtypeStruct((B, S, ATT_WIDTH), BF16), jax.ShapeDtypeStruct((1, 128), F32)] + p_oshapes,
        input_output_aliases=p_alias,
        scratch_shapes=[pltpu.VMEM((ATT_BLOCK, 256), F32), pltpu.VMEM((2, GROUP_ROWS, 128), F32)] + p_scratch,
        compiler_params=_params(("arbitrary", "arbitrary")),
    )(proj_a, proj_a, proj_a, proj_a, proj_a, proj_a, proj_a, du, cos, sin, cos, sin, sinks_l, *p_args)
    return tuple(res[:4]) + (list(res[4:]),)


def _outproj_fwd(u2, w_out, x2, g_post, target2, name):
    T, D = x2.shape
    tm = _pick(T, (512, 256, 128))
    last = target2 is not None

    def body(u_ref, w_ref, x_ref, g_ref, *rest):
        y = lax.dot_general(u_ref[...], w_ref[...], (NN, ((), ())), preferred_element_type=F32)
        r = lax.rsqrt(jnp.mean(y * y, axis=-1, keepdims=True) + NORM_EPS)
        xn = x_ref[...] + (y * r) * g_ref[...]
        if last:
            t_ref, y_ref, dx_ref, loss_ref = rest
            err = xn - t_ref[...]
            dx_ref[...] = err * (1.0 / D)
            sq = err * err
            acc = sq[:, 0:128]
            for kk in range(1, D // 128):
                acc = acc + sq[:, 128 * kk:128 * (kk + 1)]
            part = jnp.sum(acc.reshape(tm // 8, 8, 128), axis=0) * (0.5 / D)

            @pl.when(pl.program_id(0) == 0)
            def _():
                loss_ref[...] = jnp.zeros_like(loss_ref)

            loss_ref[...] += part
        else:
            y_ref, xn_ref = rest
            xn_ref[...] = xn
        y_ref[...] = y

    row = pl.BlockSpec((tm, D), lambda i: (i, 0))
    in_specs = [pl.BlockSpec((tm, MIX_WIDTH), lambda i: (i, 0)),
                pl.BlockSpec((MIX_WIDTH, D), lambda i: (0, 0)), row,
                pl.BlockSpec((1, D), lambda i: (0, 0))]
    args = [u2, w_out, x2, g_post]
    out_specs = [row, row]
    out_shape = [jax.ShapeDtypeStruct((T, D), F32), jax.ShapeDtypeStruct((T, D), F32)]
    if last:
        in_specs.append(row)
        args.append(target2)
        out_specs.append(pl.BlockSpec((8, 128), lambda i: (0, 0)))
        out_shape.append(jax.ShapeDtypeStruct((8, 128), F32))
    return pl.pallas_call(
        body, name=name, grid=(T // tm,), in_specs=in_specs, out_specs=out_specs, out_shape=out_shape,
        compiler_params=_params(("arbitrary",)),
    )(*args)


def _postnorm_bwd(dxn2, y2, g_post, name):
    T, D = y2.shape
    tm = _pick(T, (512, 256, 128))
    nt = T // tm

    def body(dx_ref, y_ref, g_ref, dy_ref, dg_ref, acc):
        i = pl.program_id(0)

        @pl.when(i == 0)
        def _():
            acc[...] = jnp.zeros_like(acc)

        y = y_ref[...]
        dxn = dx_ref[...]
        r = lax.rsqrt(jnp.mean(y * y, axis=-1, keepdims=True) + NORM_EPS)
        n = y * r
        dn = dxn * g_ref[...]
        dy_ref[...] = (r * (dn - n * jnp.mean(dn * n, axis=-1, keepdims=True))).astype(BF16)
        acc[...] += jnp.sum((dxn * n).reshape(tm // 8, 8, D), axis=0)

        @pl.when(i == nt - 1)
        def _():
            dg_ref[...] = jnp.sum(acc[...], axis=0, keepdims=True)

    row = pl.BlockSpec((tm, D), lambda i: (i, 0))
    vec = pl.BlockSpec((1, D), lambda i: (0, 0))
    return pl.pallas_call(
        body, name=name, grid=(nt,), in_specs=[row, row, vec], out_specs=[row, vec],
        out_shape=[jax.ShapeDtypeStruct((T, D), BF16), jax.ShapeDtypeStruct((1, D), F32)],
        scratch_shapes=[pltpu.VMEM((8, D), F32)],
        compiler_params=_params(("arbitrary",)),
    )(dxn2, y2, g_post)


def _inproj_bwd(pieces, w_t, x2, dxn2, g_pre, name, phase=None):
    T, D = x2.shape
    widths = [p.shape[1] for p in pieces]
    offs = [sum(widths[:i]) for i in range(len(pieces))]
    n_p = len(pieces)
    tm = _pick(T, (256, 128))
    nt = T // tm

    def body(*refs):
        ins, (dx_ref, dg_ref), (acc,), p_in, p_out, p_sems = _split_refs(refs, n_p + 4, 2, 1, phase)
        w_ref, x_ref, dxn_ref, g_ref = ins[n_p:]
        i = pl.program_id(0)
        _hosted_start(phase, p_in, p_out, p_sems, i == 0)

        @pl.when(i == 0)
        def _():
            acc[...] = jnp.zeros_like(acc)

        dh = jnp.zeros((tm, D), F32)
        for p in range(n_p):
            dh = dh + lax.dot_general(ins[p][...], w_ref[offs[p]:offs[p] + widths[p], :], (NN, ((), ())),
                                      preferred_element_type=F32)
        x = x_ref[...]
        r = lax.rsqrt(jnp.mean(x * x, axis=-1, keepdims=True) + NORM_EPS)
        n = x * r
        dn = dh * g_ref[...]
        dx_ref[...] = dxn_ref[...] + r * (dn - n * jnp.mean(dn * n, axis=-1, keepdims=True))
        acc[...] += jnp.sum((dh * n).reshape(tm // 8, 8, D), axis=0)

        @pl.when(i == nt - 1)
        def _():
            dg_ref[...] = jnp.sum(acc[...], axis=0, keepdims=True)

        _hosted_finish(phase, p_in, p_out, p_sems, i == nt - 1)

    row = pl.BlockSpec((tm, D), lambda i: (i, 0))
    vec = pl.BlockSpec((1, D), lambda i: (0, 0))
    p_ispecs, p_ospecs, p_oshapes, p_alias, p_scratch, p_args = _host_phase(phase, n_p + 4, 2)
    res = pl.pallas_call(
        body, name=name, grid=(nt,),
        in_specs=[pl.BlockSpec((tm, w), lambda i: (i, 0)) for w in widths]
        + [pl.BlockSpec((sum(widths), D), lambda i: (0, 0), pipeline_mode=pl.Buffered(1)), row, row, vec] + p_ispecs,
        out_specs=[row, vec] + p_ospecs,
        out_shape=[jax.ShapeDtypeStruct((T, D), F32), jax.ShapeDtypeStruct((1, D), F32)] + p_oshapes,
        input_output_aliases=p_alias,
        scratch_shapes=[pltpu.VMEM((8, D), F32)] + p_scratch,
        compiler_params=_params(("arbitrary",)),
    )(*pieces, w_t, x2, dxn2, g_pre, *p_args)
    return res[0], res[1], list(res[2:])


def _step(x, target, g_pre, g_post, lb_param, g_head, sinks, shards=None, full=None):
    B, S, D = x.shape
    T = B * S
    dist = shards is not None
    if dist:
        a_loc, b_loc = shards
        ra, rb = a_loc.shape[1], b_loc.shape[1]
        side = _own_side_blocks()
        placed = lambda loc, nm: _place_own(loc, side, "place_" + nm)
        gather = lambda phase, nm: _run_phase(phase, nm)
        w_in0 = gather(_gather_ici_phase([a_loc[0]], [placed(a_loc[0], "in0")]), "gather_in0_ici")
        w_in0 = gather(_gather_d2d_phase(w_in0, [ra]), "gather_in0_d2d")[0]
        late_locs = [a_loc[1], b_loc[1], b_loc[0]]
        late_rs = [ra, rb, rb]
        late_full = [placed(a_loc[1], "in1"), placed(b_loc[1], "out1"), placed(b_loc[0], "out0")]
        w_in, w_out = [w_in0, None], [None, None]
    else:
        w_in, w_out = list(full[0]), list(full[1])
    cos, sin = _rope_tables(S)
    saved = []
    xs = x
    loss_part = None
    dxn = None
    for l in range(DEPTH):
        x2 = xs.reshape(T, D)
        host = dist and l == 0
        proj_h, proj_a, h = _inproj(x2, g_pre[l:l + 1], w_in[l], f"inproj{l}")
        proj_h = proj_h.reshape(B, S, N_H)
        proj_a = proj_a.reshape(B, S, N_A)
        o_h, u, states, got = _hgrn_fwd(proj_h, MIX_WIDTH, lb_param, g_head[l:l + 1], l, f"hgrn_fwd{l}",
                                        _gather_ici_phase(late_locs, late_full) if host else None)
        u, got = _attn_fwd(proj_a, u, _sink_rows(sinks[l]), cos, sin, f"attn_fwd{l}",
                           _gather_d2d_phase(got, late_rs) if host else None)
        if host:
            w_in[1], w_out[1], w_out[0] = got
        u2 = u.reshape(T, MIX_WIDTH)
        if l < DEPTH - 1:
            y, xn = _outproj_fwd(u2, w_out[l], x2, g_post[l:l + 1], None, f"outproj{l}")
            xn = xn.reshape(B, S, D)
        else:
            y, dxn, loss_part = _outproj_fwd(u2, w_out[l], x2, g_post[l:l + 1], target.reshape(T, D), f"outproj{l}")
            xn = None
        saved.append((x2, h, proj_h, proj_a, o_h, u2, states, y))
        xs = xn

    dw_in, dw_out = [None] * DEPTH, [None] * DEPTH
    dg_pre, dg_post, dlb, dg_head, dsinks = [], [], [], [], []
    for l in reversed(range(DEPTH)):
        x2, h, proj_h, proj_a, o_h, u2, states, y = saved[l]
        host = dist and l == 0
        dy, dgp = _postnorm_bwd(dxn, y, g_post[l:l + 1], f"postnorm_bwd{l}")
        dw_out[l] = _mm_tn([u2], dy, f"wgrad_out{l}")
        du = _mm_nt(dy, w_out[l], f"dgrad_out{l}").reshape(B, S, MIX_WIDTH)
        if host:
            early = [dw_in[1], dw_out[1], dw_out[0]]
        dqh, dfh, dih, dzh, dlb_l, dgh, got = _hgrn_bwd(
            proj_h, o_h, du, states, lb_param, g_head[l:l + 1], l, f"hgrn_bwd{l}",
            _reduce_d2d_phase(early, late_rs) if host else None)
        if host:
            parts = [_pair_sum(g, r, side, f"pair_sum{i}") for i, (g, r) in enumerate(zip(early, got))]
        dqa, dkv, dza, dsk, got = _attn_bwd(proj_a, du, _sink_rows(sinks[l]), cos, sin, f"attn_bwd{l}",
                                            _reduce_ici_phase(parts) if host else None)
        if host:
            dw_in[1], dw_out[1], dw_out[0] = [_chip_sum(p, r, f"chip_sum{i}")
                                              for i, (p, r) in enumerate(zip(parts, got))]
        dproj = [p.reshape(T, p.shape[-1]) for p in (dqh, dfh, dih, dzh, dqa, dkv, dza)]
        dw_in[l] = _mm_tn(dproj, h, f"wgrad_in{l}")
        if host:
            got = _run_phase(_reduce_d2d_phase([dw_in[0]], [ra]), "reduce_in0_d2d")
            part = _pair_sum(dw_in[0], got[0], side, "pair_sum_in0")
        dxn, dgpre, got = _inproj_bwd(dproj, w_in[l], x2, dxn, g_pre[l:l + 1], f"inproj_bwd{l}",
                                      _reduce_ici_phase([part]) if host else None)
        if host:
            dw_in[0] = _chip_sum(part, got[0], "chip_sum_in0")
        dg_pre.append(dgpre)
        dg_post.append(dgp)
        dlb.append(dlb_l)
        dg_head.append(dgh)
        dsinks.append(dsk)
    rev = lambda lst: jnp.concatenate(lst[::-1], axis=0)
    return (loss_part, dxn.reshape(B, S, D), jnp.stack(dw_in), jnp.stack(dw_out),
            rev(dg_pre), rev(dg_post), rev(dlb), rev(dg_head), rev(dsinks))


def _me_and_peers():
    x, y, c = lax.axis_index("x"), lax.axis_index("y"), lax.axis_index("c")
    me = 4 * x + 2 * y + c
    peers = []
    for k in range(1, N_DEV):
        px = 1 - x if k & 4 else x
        py = 1 - y if k & 2 else y
        pc = 1 - c if k & 1 else c
        peers.append(((px, py, pc), 4 * px + 2 * py + pc))
    return me, peers


class _Phase:
    def __init__(self, arrays, out_shapes, aliases, n_send, build):
        self.arrays, self.out_shapes, self.aliases = list(arrays), list(out_shapes), dict(aliases)
        self.n_send, self.build = n_send, build

    def scratch(self):
        return [pltpu.SemaphoreType.DMA((self.n_send,)), pltpu.SemaphoreType.DMA((self.n_send,))]

    def _copies(self, in_refs, out_refs, sems, arrivals):
        send_sems, recv_sems = sems
        sends, recvs = self.build(in_refs, out_refs)
        assert len(sends) == self.n_send == len(recvs)
        out = [pltpu.make_async_remote_copy(src_ref=s, dst_ref=d, send_sem=send_sems.at[i], recv_sem=recv_sems.at[i],
                                            device_id=dev, device_id_type=MESH) for i, (s, d, dev) in enumerate(sends)]
        inc = [pltpu.make_async_remote_copy(src_ref=s, dst_ref=r, send_sem=send_sems.at[i], recv_sem=recv_sems.at[i],
                                            device_id=dev, device_id_type=MESH)
               for i, ((s, _, dev), r) in enumerate(zip(sends, recvs))] if arrivals else []
        return out, inc

    def start(self, in_refs, out_refs, sems):
        out, _ = self._copies(in_refs, out_refs, sems, False)
        for cp in out:
            cp.start()

    def finish(self, in_refs, out_refs, sems):
        out, inc = self._copies(in_refs, out_refs, sems, True)
        for cp in inc:
            cp.wait_recv()
        for cp in out:
            cp.wait_send()


_ANY = pl.BlockSpec(memory_space=pl.ANY)


def _host_phase(phase, n_in, n_out):
    if phase is None:
        return [], [], [], {}, [], []
    aliases = {n_in + i: n_out + o for i, o in phase.aliases.items()}
    return ([_ANY] * len(phase.arrays), [_ANY] * len(phase.out_shapes), phase.out_shapes, aliases, phase.scratch(),
            phase.arrays)


def _split_refs(refs, n_in, n_out, n_scr, phase):
    pi = len(phase.arrays) if phase else 0
    po = len(phase.out_shapes) if phase else 0
    a = n_in + pi
    b = a + n_out + po
    return (refs[:n_in], refs[a:a + n_out], refs[b:b + n_scr], refs[n_in:a], refs[a + n_out:b], refs[b + n_scr:])


def _hosted_start(phase, p_in, p_out, p_sems, first):
    if phase is not None:
        @pl.when(first)
        def _():
            phase.start(p_in, p_out, p_sems)


def _hosted_finish(phase, p_in, p_out, p_sems, last):
    if phase is not None:
        @pl.when(last)
        def _():
            phase.finish(p_in, p_out, p_sems)


def _run_phase(phase, name):
    n_in, n_out = len(phase.arrays), len(phase.out_shapes)

    def body(*refs):
        phase.start(refs[:n_in], refs[n_in:n_in + n_out], refs[n_in + n_out:])
        phase.finish(refs[:n_in], refs[n_in:n_in + n_out], refs[n_in + n_out:])

    return pl.pallas_call(
        body, name=name, in_specs=[_ANY] * n_in, out_specs=[_ANY] * n_out,
        out_shape=phase.out_shapes, input_output_aliases=phase.aliases, scratch_shapes=phase.scratch(),
        compiler_params=pltpu.CompilerParams(has_side_effects=True),
    )(*phase.arrays)


def _mesh_place():
    x, y, c = lax.axis_index("x"), lax.axis_index("y"), lax.axis_index("c")
    chips = [(x, y), (1 - x, y), (x, 1 - y), (1 - x, 1 - y)]
    num = lambda chip, core: 4 * chip[0] + 2 * chip[1] + core
    return c, chips, num


def _own_side_blocks():
    c, chips, num = _mesh_place()
    return jnp.stack([num(ch, c) for ch in chips]).astype(jnp.int32)


def _rows(ref, r, dev):
    return ref.at[pl.ds(pl.multiple_of(dev * r, 16), r), :]


def _place_own(loc, blocks, name):
    r, D = loc.shape
    tr = _pick(r, (400, 256, 200, 128, 64, 16))

    def body(idx_ref, l_ref, o_ref):
        del idx_ref
        o_ref[...] = l_ref[...]

    return pl.pallas_call(
        body, name=name,
        grid_spec=pltpu.PrefetchScalarGridSpec(
            num_scalar_prefetch=1, grid=(r // tr,),
            in_specs=[pl.BlockSpec((tr, D), lambda i, idx: (i, 0))],
            out_specs=pl.BlockSpec((tr, D), lambda i, idx: (idx[0] * (r // tr) + i, 0))),
        out_shape=jax.ShapeDtypeStruct((N_DEV * r, D), loc.dtype),
        compiler_params=_params(("arbitrary",)),
    )(blocks, loc)


def _gather_ici_phase(locs, fulls):
    rs = [a.shape[0] for a in locs]
    n = len(locs)

    def build(ins, outs):
        c, chips, num = _mesh_place()
        me = num(chips[0], c)
        targets = [((*chips[0], 1 - c), num(chips[0], 1 - c))] + [((*ch, c), num(ch, c)) for ch in chips[1:]]
        sends, recvs = [], []
        for dev, dnum in targets:
            for i, r in enumerate(rs):
                sends.append((ins[i], _rows(outs[i], r, me), dev))
                recvs.append(_rows(outs[i], r, dnum))
        return sends, recvs

    shapes = [jax.ShapeDtypeStruct(a.shape, a.dtype) for a in fulls]
    return _Phase(list(locs) + list(fulls), shapes, {n + i: i for i in range(n)}, 4 * n, build)


def _gather_d2d_phase(fulls, rs):
    def build(ins, outs):
        c, chips, num = _mesh_place()
        sib = (*chips[0], 1 - c)
        sends, recvs = [], []
        for ch in chips[1:]:
            for i, r in enumerate(rs):
                blk = _rows(outs[i], r, num(ch, c))
                sends.append((blk, blk, sib))
                recvs.append(_rows(outs[i], r, num(ch, 1 - c)))
        return sends, recvs

    shapes = [jax.ShapeDtypeStruct(a.shape, a.dtype) for a in fulls]
    return _Phase(fulls, shapes, {i: i for i in range(len(fulls))}, 3 * len(fulls), build)


def _reduce_d2d_phase(grads, rs):
    def build(ins, outs):
        c, chips, num = _mesh_place()
        sib = (*chips[0], 1 - c)
        sends, recvs = [], []
        for j, ch in enumerate(chips):
            for i, r in enumerate(rs):
                sends.append((_rows(ins[i], r, num(ch, 1 - c)), outs[i].at[j], sib))
                recvs.append(outs[i].at[j])
        return sends, recvs

    shapes = [jax.ShapeDtypeStruct((4, r, g.shape[1]), g.dtype) for g, r in zip(grads, rs)]
    return _Phase(grads, shapes, {}, 4 * len(grads), build)


def _reduce_ici_phase(parts):
    def build(ins, outs):
        c, chips, _ = _mesh_place()
        sends, recvs = [], []
        for t in range(1, 4):
            for i in range(len(parts)):
                sends.append((ins[i].at[t], outs[i].at[t - 1], (*chips[t], c)))
                recvs.append(outs[i].at[t - 1])
        return sends, recvs

    shapes = [jax.ShapeDtypeStruct((3,) + p.shape[1:], p.dtype) for p in parts]
    return _Phase(parts, shapes, {}, 3 * len(parts), build)


def _pair_sum(g, got, blocks, name):
    n, r, D = got.shape
    tr = _pick(r, (400, 256, 200, 128, 64, 16))

    def body(idx_ref, g_ref, r_ref, o_ref):
        del idx_ref
        o_ref[...] = (g_ref[...].astype(F32) + r_ref[...].astype(F32)).astype(o_ref.dtype)

    blk = pl.BlockSpec((None, tr, D), lambda j, i, idx: (j, i, 0))
    return pl.pallas_call(
        body, name=name,
        grid_spec=pltpu.PrefetchScalarGridSpec(
            num_scalar_prefetch=1, grid=(n, r // tr),
            in_specs=[pl.BlockSpec((tr, D), lambda j, i, idx: (idx[j] * (r // tr) + i, 0)), blk],
            out_specs=blk),
        out_shape=jax.ShapeDtypeStruct(got.shape, got.dtype),
        compiler_params=_params(("arbitrary", "arbitrary")),
    )(blocks, g, got)


def _chip_sum(p, r, name):
    _, R, D = p.shape
    tr = _pick(R, (400, 256, 200, 128, 64, 16))

    def body(p_ref, r_ref, o_ref):
        acc = p_ref[...].astype(F32)
        for t in range(3):
            acc = acc + r_ref[t].astype(F32)
        o_ref[...] = acc

    return pl.pallas_call(
        body, name=name, grid=(R // tr,),
        in_specs=[pl.BlockSpec((None, tr, D), lambda i: (0, i, 0)), pl.BlockSpec((3, tr, D), lambda i: (0, i, 0))],
        out_specs=pl.BlockSpec((tr, D), lambda i: (i, 0)), out_shape=jax.ShapeDtypeStruct((R, D), F32),
        compiler_params=_params(("parallel",)))(p, r)


def _allreduce_small(vec):
    R, C = vec.shape

    def body(v_ref, o_ref, buf, send_sems, recv_sems):
        me, peers = _me_and_peers()
        buf[me] = v_ref[...]
        sends = []
        for k, (pid, _) in enumerate(peers):
            cp = pltpu.make_async_remote_copy(src_ref=v_ref, dst_ref=buf.at[me], send_sem=send_sems.at[k],
                                              recv_sem=recv_sems.at[k], device_id=pid, device_id_type=MESH)
            cp.start()
            sends.append(cp)
        for k, (pid, pnum) in enumerate(peers):
            pltpu.make_async_remote_copy(src_ref=v_ref, dst_ref=buf.at[pnum], send_sem=send_sems.at[k],
                                         recv_sem=recv_sems.at[k], device_id=pid, device_id_type=MESH).wait_recv()
        for cp in sends:
            cp.wait_send()
        acc = buf[0]
        for d in range(1, N_DEV):
            acc = acc + buf[d]
        o_ref[...] = acc

    vm = pl.BlockSpec(memory_space=pltpu.VMEM)
    return pl.pallas_call(
        body, name="allreduce_small",
        in_specs=[vm], out_specs=vm,
        out_shape=jax.ShapeDtypeStruct((R, C), F32),
        scratch_shapes=[pltpu.VMEM((N_DEV, R, C), F32), pltpu.SemaphoreType.DMA((N_DEV - 1,)),
                        pltpu.SemaphoreType.DMA((N_DEV - 1,))],
        compiler_params=pltpu.CompilerParams(has_side_effects=True),
    )(vec)


def _adamw(w, g, m, v, name):
    R, C = w.shape
    tr = _pick(R, (256, 128, 64, 32, 16, 8)) if R >= 8 else R
    c1 = 1.0 - ADAM_B1 ** ADAM_STEP
    c2 = 1.0 - ADAM_B2 ** ADAM_STEP

    def body(w_ref, g_ref, m_ref, v_ref, d_ref, mo_ref, vo_ref):
        gg = g_ref[...]
        mn = ADAM_B1 * m_ref[...] + (1.0 - ADAM_B1) * gg
        vn = ADAM_B2 * v_ref[...] + (1.0 - ADAM_B2) * (gg * gg)
        d_ref[...] = -ADAM_LR * ((mn / c1) / (jnp.sqrt(vn / c2) + ADAM_EPS) + ADAM_WD * w_ref[...])
        mo_ref[...] = mn
        vo_ref[...] = vn

    blk = pl.BlockSpec((tr, C), lambda i: (i, 0))
    sh = jax.ShapeDtypeStruct((R, C), F32)
    return pl.pallas_call(
        body, name=name, grid=(R // tr,), in_specs=[blk] * 4, out_specs=[blk] * 3, out_shape=[sh] * 3,
        compiler_params=_params(("parallel",)),
    )(w, g, m, v)


def _lb_param_grad(lb_param, dlb):
    L, C = lb_param.shape

    def body(p_ref, d_ref, o_ref):
        lbp = p_ref[...]
        d = d_ref[...]
        mx = jnp.max(lbp, axis=0, keepdims=True)
        e = jnp.exp(lbp - mx)
        p = e / jnp.sum(e, axis=0, keepdims=True)
        tot = jnp.sum(d, axis=0, keepdims=True)
        dps = []
        rest = tot
        for j in range(L):
            dps.append(rest - tot if j == 0 else rest)
            rest = rest - d[j:j + 1]
        dp = jnp.concatenate(dps, axis=0)
        o_ref[...] = p * (dp - jnp.sum(p * dp, axis=0, keepdims=True))

    vm = pl.BlockSpec(memory_space=pltpu.VMEM)
    return pl.pallas_call(body, name="lb_param_grad", in_specs=[vm, vm], out_specs=vm,
                          out_shape=jax.ShapeDtypeStruct((L, C), F32))(lb_param, dlb)


def _pack_small(loss_part, dg_pre, dg_post, dlb, dg_head, dsinks):
    pad8 = lambda a: jnp.pad(a.reshape(-1, 128), ((0, 8 - DEPTH), (0, 0)))
    rows = [dg_pre.reshape(-1, 128), dg_post.reshape(-1, 128), dlb.reshape(-1, 128), pad8(dg_head), pad8(dsinks),
            loss_part]
    return jnp.concatenate(rows, axis=0)


def _unpack_small(vec):
    n = DEPTH * D_MODEL // 128
    o = 0
    dg_pre = vec[o:o + n].reshape(DEPTH, D_MODEL); o += n
    dg_post = vec[o:o + n].reshape(DEPTH, D_MODEL); o += n
    dlb = vec[o:o + n].reshape(DEPTH, HG_WIDTH); o += n
    dg_head = vec[o:o + DEPTH]; o += 8
    dsinks = vec[o:o + DEPTH, :ATT_HEADS]; o += 8
    loss = jnp.sum(vec[o:o + 8])
    return loss, dg_pre, dg_post, dlb, dg_head, dsinks


def kernel(x, w_in, w_out, g_pre, g_post, lb_param, g_head, sinks, loss_target, m_w_in, m_w_out, m_g_pre, m_g_post, m_lb_param, m_g_head, m_sinks, v_w_in, v_w_out, v_g_pre, v_g_post, v_lb_param, v_g_head, v_sinks):
    L, D, nloc = w_in.shape
    w_in_t_loc = jnp.swapaxes(w_in, 1, 2).astype(BF16)
    (loss_part, dx, gw_in_t, gw_out, dg_pre, dg_post, dlb, dg_head, dsinks) = _step(
        x, loss_target, g_pre, g_post, lb_param, g_head, sinks, shards=(w_in_t_loc, w_out.astype(BF16)))
    gw_in = jnp.swapaxes(gw_in_t, 1, 2)

    small = _allreduce_small(_pack_small(loss_part, dg_pre, dg_post, dlb, dg_head, dsinks))
    loss, gg_pre, gg_post, gdlb, gg_head, gsinks = _unpack_small(small)
    glb = _lb_param_grad(lb_param, gdlb)

    grads = [gw_in, gw_out, gg_pre, gg_post, glb, gg_head, gsinks]
    ws = [w_in, w_out, g_pre, g_post, lb_param, g_head, sinks]
    ms = [m_w_in, m_w_out, m_g_pre, m_g_post, m_lb_param, m_g_head, m_sinks]
    vs = [v_w_in, v_w_out, v_g_pre, v_g_post, v_lb_param, v_g_head, v_sinks]
    names = ["w_in", "w_out", "g_pre", "g_post", "lb_param", "g_head", "sinks"]
    deltas, new_m, new_v = [], [], []
    for w, g, m, v, nm in zip(ws, grads, ms, vs, names):
        sh = w.shape
        two = lambda a: a.reshape(-1, sh[-1])
        d, mn, vn = _adamw(two(w), two(g), two(m), two(v), "adamw_" + nm)
        deltas.append(d.reshape(sh))
        new_m.append(mn.reshape(sh))
        new_v.append(vn.reshape(sh))
    return (loss, dx, *grads, *deltas, *new_m, *new_v)
```

```python
import functools
import math

import numpy as np
import jax
import jax.numpy as jnp
from jax import lax
from jax.experimental import pallas as pl
from jax.experimental.pallas import tpu as pltpu

F32 = jnp.float32
BF16 = jnp.bfloat16

D_MODEL = 1024
DEPTH = 2
HG_HEADS = 8
HG_DIM = 128
HG_WIDTH = HG_HEADS * HG_DIM
CHUNK = 64
ATT_HEADS = 16
ATT_DIM = 64
ATT_WIDTH = ATT_HEADS * ATT_DIM
KV_WIDTH = 128
ATT_BLOCK = 128
ATT_SCALE = 1.0 / math.sqrt(ATT_DIM)
ROPE_THETA = 10000.0
NORM_EPS = 1e-6
NEG_INF = -1e30
LB_FLOOR = 1e-20
N_H = 4 * HG_WIDTH
N_A = 2 * ATT_WIDTH + 2 * KV_WIDTH
IN_WIDTH = N_H + N_A
MIX_WIDTH = HG_WIDTH + ATT_WIDTH

ADAM_LR = 0.001
ADAM_B1 = 0.9
ADAM_B2 = 0.999
ADAM_EPS = 1e-08
ADAM_WD = 0.01
ADAM_STEP = 10

N_DEV = 8
MESH = pl.DeviceIdType.MESH
VMEM_LIMIT = 56 * 1024 * 1024

NN = ((1,), (0,))
NT = ((1,), (1,))
TN = ((0,), (0,))


def _dot(a, b, dims):
    return lax.dot_general(a.astype(BF16), b.astype(BF16), (dims, ((), ())), preferred_element_type=F32)


def _params(sem=None, **kw):
    return pltpu.CompilerParams(dimension_semantics=sem, vmem_limit_bytes=VMEM_LIMIT, **kw)


def _sigmoids(x):
    e = jnp.exp(-jnp.abs(x))
    r = 1.0 / (1.0 + e)
    er = e * r
    pos = x >= 0.0
    return jnp.where(pos, r, er), jnp.where(pos, er, r)


def _silu(x):
    return x * _sigmoids(x)[0]


def _silu_and_grad(x):
    s, ns = _sigmoids(x)
    return x * s, s * (1.0 + x * ns)


def _pick(n, prefs):
    for p in prefs:
        if n % p == 0:
            return p
    return n


def _inproj(x2, g, w, name):
    T, D = x2.shape
    tm = _pick(T, (256, 128))
    nchunk = 1024

    def body(x_ref, g_ref, w_ref, oh_ref, oa_ref, h_ref):
        x = x_ref[...]
        r = lax.rsqrt(jnp.mean(x * x, axis=-1, keepdims=True) + NORM_EPS)
        h = ((x * r) * g_ref[...]).astype(BF16)
        h_ref[...] = h
        for j in range(0, N_H, nchunk):
            oh_ref[:, j:j + nchunk] = lax.dot_general(h, w_ref[j:j + nchunk, :], (NT, ((), ())),
                                                      preferred_element_type=F32)
        for j in range(0, N_A, N_A // 2):
            oa_ref[:, j:j + N_A // 2] = lax.dot_general(h, w_ref[N_H + j:N_H + j + N_A // 2, :], (NT, ((), ())),
                                                        preferred_element_type=F32)

    row = lambda w_: pl.BlockSpec((tm, w_), lambda i: (i, 0))
    return pl.pallas_call(
        body, name=name,
        grid=(T // tm,),
        in_specs=[row(D), pl.BlockSpec((1, D), lambda i: (0, 0)),
                  pl.BlockSpec((IN_WIDTH, D), lambda i: (0, 0), pipeline_mode=pl.Buffered(1))],
        out_specs=[row(N_H), row(N_A), row(D)],
        out_shape=[jax.ShapeDtypeStruct((T, N_H), F32), jax.ShapeDtypeStruct((T, N_A), F32),
                   jax.ShapeDtypeStruct((T, D), BF16)],
        compiler_params=_params(("parallel",)),
    )(x2, g, w)


def _mm_tn(pieces, b, name, out_dtype=BF16):
    T, m = b.shape
    tn = 256
    counts = [p.shape[1] // tn for p in pieces]
    starts = [sum(counts[:i]) for i in range(len(pieces))]
    n_p = len(pieces)

    def body(*refs):
        b_ref, o_ref = refs[n_p], refs[n_p + 1]
        i = pl.program_id(0)
        for p in range(n_p):
            @pl.when((i >= starts[p]) & (i < starts[p] + counts[p]))
            def _(p=p):
                o_ref[...] = lax.dot_general(refs[p][...], b_ref[...], (TN, ((), ())),
                                             preferred_element_type=F32).astype(out_dtype)

    piece_spec = lambda s, c: pl.BlockSpec((T, tn), lambda i: (0, jnp.clip(i - s, 0, c - 1)))
    return pl.pallas_call(
        body, name=name,
        grid=(sum(counts),),
        in_specs=[piece_spec(s, c) for s, c in zip(starts, counts)]
        + [pl.BlockSpec((T, m), lambda i: (0, 0), pipeline_mode=pl.Buffered(1))],
        out_specs=pl.BlockSpec((tn, m), lambda i: (i, 0)),
        out_shape=jax.ShapeDtypeStruct((sum(counts) * tn, m), out_dtype),
        compiler_params=_params(("arbitrary",)),
    )(*pieces, b)


_LEVELS = (0, 1, 2, 4, 8, 16, 32)
_CUM_L = (2, 4, 8, 16, 32, 64)
_ALL_KINDS = tuple(("c", L) for L in _CUM_L) + tuple(("r", L) for L in _CUM_L)
_MXU_KINDS = (("c", 2), ("c", 4), ("c", CHUNK), ("r", 2), ("r", 4))
N_CUM = len(_ALL_KINDS) * CHUNK
N_CUM_F = len(_MXU_KINDS) * CHUNK


def _cum_matrices():
    t = np.arange(CHUNK)[:, None]
    r = np.arange(CHUNK)[None, :]

    def mat(kind):
        c, L = kind
        return ((r // L == t // L) & ((r <= t) if c == "c" else (r > t))).astype(np.float32)

    fwd = np.concatenate([mat(kd) for kd in _MXU_KINDS], axis=0)
    full = np.concatenate([mat(kd) for kd in _ALL_KINDS], axis=0)
    return jnp.asarray(fwd, BF16), jnp.asarray(full.T.copy(), BF16)


def _level_masks():
    t = np.arange(CHUNK)[:, None]
    s = np.arange(CHUNK)[None, :]
    ms = []
    for L in _LEVELS:
        if L == 0:
            ms.append(t == s)
        else:
            ms.append((t // (2 * L) == s // (2 * L)) & ((t // L) % 2 == 1) & ((s // L) % 2 == 0))
    return jnp.asarray(np.stack(ms).astype(np.float32))


def _split3(x):
    hi = x.astype(BF16)
    r1 = x - hi.astype(F32)
    mid = r1.astype(BF16)
    lo = (r1 - mid.astype(F32)).astype(BF16)
    return hi, mid, lo


def _cum3(ts, x, terms=3):
    d = lambda p: lax.dot_general(ts, p, (NN, ((), ())), preferred_element_type=F32)
    return sum(d(p) for p in _split3(x)[:terms])


def _lb_terms(lbp, layer):
    mx = jnp.max(lbp, axis=0, keepdims=True)
    e = jnp.exp(lbp - mx)
    p = e / jnp.sum(e, axis=0, keepdims=True)
    cum = p[0:1]
    for j in range(1, layer + 1):
        cum = cum + p[j:j + 1]
    lb = cum - p[0:1]
    lbf = jnp.maximum(lb, LB_FLOOR)
    return dict(lbf=lbf, one_m=1.0 - lb, kcorr=lb - lbf, ind=jnp.where(lb > LB_FLOOR, 1.0, 0.0))


def _gate(x, lt):
    sig, nsig = _sigmoids(x)
    f = lt["lbf"] + lt["one_m"] * sig
    return jnp.log(f), lt["one_m"] * nsig + lt["kcorr"], f, sig, nsig


def _ck(x, ci):
    return x[ci * CHUNK:(ci + 1) * CHUNK]


def _block_cums(ts, g, nc):
    cs = [_cum3(ts, _ck(g, ci)) for ci in range(nc)]
    out = {kind: jnp.concatenate([c[CHUNK * i:CHUNK * (i + 1)] for c in cs], axis=0)
           for i, kind in enumerate(_MXU_KINDS)}
    b = out[("c", CHUNK)]
    ng = CHUNK // 8
    last = b.reshape(nc, ng, 8, HG_DIM)[:, :, 7:8, :]
    zero = jnp.zeros((nc, 1, 1, HG_DIM), F32)

    def spread(groups):
        return jnp.broadcast_to(jnp.concatenate(groups, axis=1), (nc, ng, 8, HG_DIM)).reshape(nc * CHUNK, HG_DIM)

    def get(kind):
        if kind in out:
            return out[kind]
        c, L = kind
        nb = L // 8
        first = lambda r: (r // nb) * nb
        if c == "c":
            return b - spread([last[:, first(r) - 1:first(r)] if r >= nb else zero for r in range(ng)])
        return spread([last[:, first(r) + nb - 1:first(r) + nb] for r in range(ng)]) - b

    return get


def _level_factors(cums, g, L):
    if L == 0:
        return None, None
    if L == 1:
        return jnp.exp(g), None
    return jnp.exp(cums(("c", L))), jnp.exp(cums(("r", L)))


def _mul(a, e):
    return a if e is None else a * e


def _hg_block_fwd(qf, k, v, g, ts, m_ref, nc):
    cums = _block_cums(ts, g, nc)
    amat = [jnp.zeros((CHUNK, CHUNK), F32)] * nc
    for li, L in enumerate(_LEVELS):
        eq, ek = _level_factors(cums, g, L)
        ql, kl, m = _mul(qf, eq), _mul(k, ek), m_ref[li]
        amat = [amat[ci] + _dot(_ck(ql, ci), _ck(kl, ci), NT) * m for ci in range(nc)]
    b = cums(("c", CHUNK))
    kst = k * jnp.exp(cums(("r", CHUNK)))
    o = [_dot(amat[ci], _ck(v, ci), NN) for ci in range(nc)]
    kv = [_dot(_ck(v, ci), _ck(kst, ci), TN) for ci in range(nc)]
    dec = [jnp.exp(b[(ci + 1) * CHUNK - 1:(ci + 1) * CHUNK, :]) for ci in range(nc)]
    return o, dec, kv, qf * jnp.exp(b)


def _hg_block_bwd(qf, k, v, g, do, ts, m_ref, nc):
    cums = _block_cums(ts, g, nc)
    dcs = {}
    da = [_dot(_ck(do, ci), _ck(v, ci), NT) for ci in range(nc)]
    dq = jnp.zeros_like(qf)
    dk = jnp.zeros_like(qf)
    dg = jnp.zeros_like(qf)
    amat = [jnp.zeros((CHUNK, CHUNK), F32)] * nc
    for li, L in enumerate(_LEVELS):
        eq, ek = _level_factors(cums, g, L)
        ql, kl, m = _mul(qf, eq), _mul(k, ek), m_ref[li]
        qlb, klb = ql.astype(BF16), kl.astype(BF16)
        amat = [amat[ci] + _dot(_ck(qlb, ci), _ck(klb, ci), NT) * m for ci in range(nc)]
        dal = [(da[ci] * m).astype(BF16) for ci in range(nc)]
        dql = jnp.concatenate([_dot(dal[ci], _ck(klb, ci), NN) for ci in range(nc)], axis=0)
        dkl = jnp.concatenate([_dot(dal[ci], _ck(qlb, ci), TN) for ci in range(nc)], axis=0)
        dq = dq + _mul(dql, eq)
        dk = dk + _mul(dkl, ek)
        if L == 1:
            dg = dg + dql * ql
        elif L > 1:
            dcs[("c", L)] = (dql * ql).astype(BF16)
            dcs[("r", L)] = (dkl * kl).astype(BF16)
    b = cums(("c", CHUNK))
    e64 = jnp.exp(b)
    er64 = jnp.exp(cums(("r", CHUNK)))
    qb = qf * e64
    return dict(dq=dq, dk=dk, dg=dg, dcs=dcs, e64=e64, er64=er64, qb=qb, kst=k * er64,
                dv=[_dot(amat[ci], _ck(do, ci), TN) for ci in range(nc)],
                dec=[jnp.exp(b[(ci + 1) * CHUNK - 1:(ci + 1) * CHUNK, :]) for ci in range(nc)],
                qd=[_dot(_ck(do, ci), _ck(qb, ci), TN) for ci in range(nc)])


def _hg_state_bwd(w, v, do, starts, ends, tst, nc):
    dqb = jnp.concatenate([_dot(_ck(do, ci), starts[ci], NN) for ci in range(nc)], axis=0)
    dkst = jnp.concatenate([_dot(_ck(v, ci), ends[ci], NN) for ci in range(nc)], axis=0)
    dq = w["dq"] + dqb * w["e64"]
    dk = w["dk"] + dkst * w["er64"]
    dv = jnp.concatenate([w["dv"][ci] + _dot(_ck(w["kst"], ci), ends[ci], NT) for ci in range(nc)], axis=0)
    trow = lax.broadcasted_iota(jnp.int32, (CHUNK, 1), 0)
    dtot = jnp.concatenate(
        [jnp.where(trow == CHUNK - 1, jnp.sum(ends[ci] * starts[ci], axis=0, keepdims=True) * w["dec"][ci], 0.0)
         for ci in range(nc)], axis=0)
    dcs = dict(w["dcs"])
    dcs[("c", CHUNK)] = (dqb * w["qb"] + dtot).astype(BF16)
    dcs[("r", CHUNK)] = (dkst * w["kst"]).astype(BF16)
    dgs = [_dot(tst, jnp.concatenate([_ck(dcs[kind], ci) for kind in _ALL_KINDS], axis=0), NN) for ci in range(nc)]
    return dq, dk, dv, w["dg"] + jnp.concatenate(dgs, axis=0)


def _hgrn_fwd(proj_h, u_rows, lb_param, g_head, layer, name, phase=None):
    B, S, _ = proj_h.shape
    sb = _pick(S, (512, 256, 128, 64))
    nc = sb // CHUNK
    ts, _ = _cum_matrices()

    def body(*refs):
        ins, outs, (st,), p_in, p_out, p_sems = _split_refs(refs, 8, 3, 1, phase)
        q_ref, f_ref, i_ref, z_ref, lbp_ref, gh_ref, ts_ref, m_ref = ins
        o_ref, u_ref, sts_ref = outs
        h_id, b_id, s_id = pl.program_id(0), pl.program_id(1), pl.program_id(2)
        _hosted_start(phase, p_in, p_out, p_sems, (h_id == 0) & (b_id == 0) & (s_id == 0))

        @pl.when(s_id == 0)
        def _():
            st[...] = jnp.zeros_like(st)

        lt = _lb_terms(lbp_ref[...], layer)
        tsv = ts_ref[...]
        gh = gh_ref[...]
        logf, k = _gate(f_ref[...], lt)[:2]
        o_part, dec, kv, qb = _hg_block_fwd(_silu(q_ref[...]), k, i_ref[...], logf, tsv, m_ref, nc)
        cur = st[...]
        starts = []
        for ci in range(nc):
            sts_ref[ci] = cur
            starts.append(cur)
            cur = cur * dec[ci] + kv[ci]
        st[...] = cur
        o = jnp.concatenate([o_part[ci] + _dot(_ck(qb, ci), starts[ci], NT) for ci in range(nc)], axis=0)
        o_ref[...] = o
        r = lax.rsqrt(jnp.mean(o * o, axis=-1, keepdims=True) + NORM_EPS)
        u_ref[...] = (((o * r) * gh) * _silu(z_ref[...])).astype(BF16)
        _hosted_finish(phase, p_in, p_out, p_sems, (h_id == HG_HEADS - 1) & (b_id == B - 1) & (s_id == S // sb - 1))

    col = lambda base: pl.BlockSpec((None, sb, HG_DIM), lambda h, b, s: (b, s, base + h))
    p_ispecs, p_ospecs, p_oshapes, p_alias, p_scratch, p_args = _host_phase(phase, 8, 3)
    res = pl.pallas_call(
        body, name=name,
        grid=(HG_HEADS, B, S // sb),
        in_specs=[col(0), col(HG_HEADS), col(2 * HG_HEADS), col(3 * HG_HEADS),
                  pl.BlockSpec((DEPTH, HG_DIM), lambda h, b, s: (0, h)),
                  pl.BlockSpec((1, HG_DIM), lambda h, b, s: (0, 0)),
                  pl.BlockSpec((N_CUM_F, CHUNK), lambda h, b, s: (0, 0)),
                  pl.BlockSpec((len(_LEVELS), CHUNK, CHUNK), lambda h, b, s: (0, 0, 0))] + p_ispecs,
        out_specs=[col(0), col(0),
                   pl.BlockSpec((None, None, nc, HG_DIM, HG_DIM), lambda h, b, s: (b, h, s, 0, 0))] + p_ospecs,
        out_shape=[jax.ShapeDtypeStruct((B, S, HG_WIDTH), F32),
                   jax.ShapeDtypeStruct((B, S, u_rows), BF16),
                   jax.ShapeDtypeStruct((B, HG_HEADS, S // CHUNK, HG_DIM, HG_DIM), F32)] + p_oshapes,
        input_output_aliases=p_alias,
        scratch_shapes=[pltpu.VMEM((HG_DIM, HG_DIM), F32)] + p_scratch,
        compiler_params=_params(("arbitrary", "arbitrary", "arbitrary")),
    )(proj_h, proj_h, proj_h, proj_h, lb_param, g_head, ts, _level_masks(), *p_args)
    return res[0], res[1], res[2], list(res[3:])


def _hgrn_bwd(proj_h, o_h, du, states, lb_param, g_head, layer, name, phase=None):
    B, S, _ = proj_h.shape
    sb = _pick(S, (512, 256, 128, 64))
    nc = sb // CHUNK
    ns = S // sb
    ts, tst = _cum_matrices()

    def body(*refs):
        ins, outs, (dst,), p_in, p_out, p_sems = _split_refs(refs, 12, 6, 1, phase)
        q_ref, f_ref, i_ref, z_ref, o_ref, du_ref, sts_ref, lbp_ref, gh_ref, ts_ref, tst_ref, m_ref = ins
        dq_ref, df_ref, di_ref, dz_ref, dlb_ref, dgh_ref = outs
        h_id, b_id, s_id = pl.program_id(0), pl.program_id(1), pl.program_id(2)
        _hosted_start(phase, p_in, p_out, p_sems, (h_id == 0) & (b_id == 0) & (s_id == 0))

        @pl.when(s_id == 0)
        def _():
            dst[...] = jnp.zeros_like(dst)

        @pl.when((b_id == 0) & (s_id == 0))
        def _():
            dlb_ref[...] = jnp.zeros_like(dlb_ref)

        @pl.when((h_id == 0) & (b_id == 0) & (s_id == 0))
        def _():
            dgh_ref[...] = jnp.zeros_like(dgh_ref)

        lt = _lb_terms(lbp_ref[...], layer)
        gh = gh_ref[...]
        tsv = ts_ref[...]
        tstv = tst_ref[...]
        logf, k, f, sig, nsig = _gate(f_ref[...], lt)
        o = o_ref[...]
        dub = du_ref[...]
        r = lax.rsqrt(jnp.mean(o * o, axis=-1, keepdims=True) + NORM_EPS)
        n = o * r
        sg, sg_grad = _silu_and_grad(z_ref[...])
        dz_ref[...] = (dub * (n * gh) * sg_grad).astype(BF16)
        dgh_ref[...] += jnp.sum(dub * sg * n, axis=0, keepdims=True)
        dn = dub * sg * gh
        do = r * (dn - n * jnp.mean(dn * n, axis=-1, keepdims=True))
        v = i_ref[...]
        qf, qf_grad = _silu_and_grad(q_ref[...])
        w = _hg_block_bwd(qf, k, v, logf, do, tsv, m_ref, nc)
        cur = dst[...]
        ends = [None] * nc
        for ci in reversed(range(nc)):
            ends[ci] = cur
            cur = cur * w["dec"][ci] + w["qd"][ci]
        dst[...] = cur
        dq, dk, dv, dg = _hg_state_bwd(w, v, do, [sts_ref[ci] for ci in range(nc)], ends, tstv, nc)
        di_ref[...] = dv.astype(BF16)
        dq_ref[...] = (dq * qf_grad).astype(BF16)
        scaled = (dg - f * dk) / f
        df_ref[...] = (scaled * lt["one_m"] * sig * nsig).astype(BF16)
        dlb_ref[...] += jnp.sum(scaled * (lt["ind"] - sig), axis=0, keepdims=True)
        _hosted_finish(phase, p_in, p_out, p_sems, (h_id == HG_HEADS - 1) & (b_id == B - 1) & (s_id == ns - 1))

    col = lambda base: pl.BlockSpec((None, sb, HG_DIM), lambda h, b, s: (b, ns - 1 - s, base + h))
    out_col = pl.BlockSpec((None, sb, HG_DIM), lambda h, b, s: (b, ns - 1 - s, h))
    dt = jax.ShapeDtypeStruct((B, S, HG_WIDTH), BF16)
    p_ispecs, p_ospecs, p_oshapes, p_alias, p_scratch, p_args = _host_phase(phase, 12, 6)
    res = pl.pallas_call(
        body, name=name,
        grid=(HG_HEADS, B, ns),
        in_specs=[col(0), col(HG_HEADS), col(2 * HG_HEADS), col(3 * HG_HEADS), col(0), col(0),
                  pl.BlockSpec((None, None, nc, HG_DIM, HG_DIM), lambda h, b, s: (b, h, ns - 1 - s, 0, 0)),
                  pl.BlockSpec((DEPTH, HG_DIM), lambda h, b, s: (0, h)),
                  pl.BlockSpec((1, HG_DIM), lambda h, b, s: (0, 0)),
                  pl.BlockSpec((N_CUM_F, CHUNK), lambda h, b, s: (0, 0)),
                  pl.BlockSpec((CHUNK, N_CUM), lambda h, b, s: (0, 0)),
                  pl.BlockSpec((len(_LEVELS), CHUNK, CHUNK), lambda h, b, s: (0, 0, 0))] + p_ispecs,
        out_specs=[out_col, out_col, out_col, out_col,
                   pl.BlockSpec((1, HG_DIM), lambda h, b, s: (0, h)),
                   pl.BlockSpec((1, HG_DIM), lambda h, b, s: (0, 0))] + p_ospecs,
        out_shape=[dt, dt, dt, dt, jax.ShapeDtypeStruct((1, HG_WIDTH), F32),
                   jax.ShapeDtypeStruct((1, HG_DIM), F32)] + p_oshapes,
        input_output_aliases=p_alias,
        scratch_shapes=[pltpu.VMEM((HG_DIM, HG_DIM), F32)] + p_scratch,
        compiler_params=_params(("arbitrary", "arbitrary", "arbitrary")),
    )(proj_h, proj_h, proj_h, proj_h, o_h, du, states, lb_param, g_head, ts, tst, _level_masks(), *p_args)
    return tuple(res[:6]) + (list(res[6:]),)


def _rope_tables(S):
    half = ATT_DIM // 2
    inv_freq = ROPE_THETA ** (-jnp.arange(half, dtype=F32) / half)
    ang = jnp.arange(S).astype(F32)[:, None] * inv_freq[None, :]
    cos = jnp.cos(ang)
    sin = jnp.sin(ang)
    cos = jnp.concatenate([cos, cos, cos, cos], axis=1)
    sin = jnp.concatenate([-sin, sin, -sin, sin], axis=1)
    return cos, sin


def _attn_common():
    lane = lax.broadcasted_iota(jnp.int32, (1, 2 * ATT_DIM), 1)
    first_half = (lane % ATT_DIM) < (ATT_DIM // 2)
    left = lane < ATT_DIM

    def swap(x):
        return jnp.where(first_half, pltpu.roll(x, 128 - ATT_DIM // 2, 1), pltpu.roll(x, ATT_DIM // 2, 1))

    def rope(x, cos, sin):
        return x * cos + swap(x) * sin

    def rope_bwd(dy, cos, sin):
        return dy * cos + swap(dy * sin)

    def dup(x):
        xs = pltpu.roll(x, ATT_DIM, 1)
        return [jnp.where(left, x, xs), jnp.where(left, xs, x)]

    return left, rope, rope_bwd, dup


GROUP = ATT_HEADS // 2
GROUP_ROWS = GROUP * ATT_BLOCK


def _attn_bias(i):
    r = lax.broadcasted_iota(jnp.int32, (ATT_BLOCK, 2 * ATT_BLOCK), 0)
    c = lax.broadcasted_iota(jnp.int32, (ATT_BLOCK, 2 * ATT_BLOCK), 1)
    ok = (c > r) & (c <= r + ATT_BLOCK) & ((c >= ATT_BLOCK) | (i > 0))
    return jnp.where(ok, 0.0, NEG_INF)


def _stack_heads(pairs, left):
    rows = []
    for x in pairs:
        rows += [jnp.where(left, x, 0.0), jnp.where(left, 0.0, x)]
    return jnp.concatenate(rows, axis=0)


def _unstack_heads(y, left, pp):
    r0 = 2 * pp * ATT_BLOCK
    return jnp.where(left, y[r0:r0 + ATT_BLOCK], y[r0 + ATT_BLOCK:r0 + 2 * ATT_BLOCK])


def _row_sums(x):
    return _dot(x, jnp.ones((x.shape[1], 128), BF16), NN)


def _attn_probs(qs, kd, vd, sink, bias):
    n = range(len(qs))
    s = [(_dot(qs[j], kd[j], NT).reshape(GROUP, ATT_BLOCK, 2 * ATT_BLOCK) * ATT_SCALE + bias[None])
         .reshape(GROUP_ROWS, 2 * ATT_BLOCK) for j in n]
    m = [jnp.max(jnp.maximum(jnp.maximum(s[j][:, :128], s[j][:, 128:]), sink[j]), axis=-1, keepdims=True) for j in n]
    pu = [jnp.exp(s[j] - m[j]) for j in n]
    es = [jnp.exp(sink[j] - m[j]) for j in n]
    ones = jnp.ones((2 * ATT_BLOCK, 128), BF16)
    ov = [_dot(pu[j], jnp.concatenate([vd[j].astype(BF16), ones], axis=1), NN) for j in n]
    inv = [1.0 / (ov[j][:, 128:] + es[j]) for j in n]
    return ([pu[j] * jnp.concatenate([inv[j], inv[j]], axis=1) for j in n], [es[j] * inv[j] for j in n],
            [ov[j][:, :128] * inv[j] for j in n])


def _sink_rows(sinks_l):
    return jnp.broadcast_to(jnp.repeat(sinks_l, ATT_BLOCK)[:, None], (ATT_HEADS * ATT_BLOCK, 128))


_Z0 = (2 * ATT_WIDTH + 2 * KV_WIDTH - ATT_WIDTH) // 256


def _attn_fwd(proj_a, u, sinks_l, cos, sin, name, phase=None):
    B, S, _ = proj_a.shape
    nb = S // ATT_BLOCK

    def body(*refs):
        ins, (u_ref,), _, p_in, p_out, p_sems = _split_refs(refs, 13, 1, 0, phase)
        q_ref, kvc_ref, kvp_ref, z0, z1, z2, z3, cos_ref, sin_ref, cosp_ref, sinp_ref, sinks_ref, _ = ins
        i = pl.program_id(1)
        _hosted_start(phase, p_in, p_out, p_sems, (pl.program_id(0) == 0) & (i == 0))
        left, rope, _, dup = _attn_common()
        cos_c, sin_c = cos_ref[...], sin_ref[...]
        kvc = kvc_ref[...]
        kvp = kvp_ref[...]
        kw = jnp.concatenate([rope(kvp[:, :KV_WIDTH], cosp_ref[...], sinp_ref[...]),
                              rope(kvc[:, :KV_WIDTH], cos_c, sin_c)], axis=0)
        vw = jnp.concatenate([kvp[:, KV_WIDTH:], kvc[:, KV_WIDTH:]], axis=0)
        kd, vd = dup(kw), dup(vw)
        bias = _attn_bias(i)
        zs = (z0, z1, z2, z3)
        pairs = [range(4 * kvh, 4 * kvh + 4) for kvh in range(2)]
        qs = [_stack_heads([rope(q_ref[:, 128 * pr:128 * (pr + 1)], cos_c, sin_c) for pr in pairs[kvh]], left)
              for kvh in range(2)]
        sink = [sinks_ref[kvh * GROUP_ROWS:(kvh + 1) * GROUP_ROWS, :] for kvh in range(2)]
        o = _attn_probs(qs, kd, vd, sink, bias)[2]
        for kvh in range(2):
            for pp, pr in enumerate(pairs[kvh]):
                z = zs[pr // 2][:, 128 * (pr % 2):128 * (pr % 2 + 1)]
                u_ref[:, 128 * pr:128 * (pr + 1)] = (_unstack_heads(o[kvh], left, pp) * _silu(z)).astype(BF16)
        _hosted_finish(phase, p_in, p_out, p_sems, (pl.program_id(0) == B - 1) & (i == nb - 1))

    rowblk = lambda w, cb: pl.BlockSpec((None, ATT_BLOCK, w), lambda b, i: (b, i, cb))
    tab = pl.BlockSpec((ATT_BLOCK, 128), lambda b, i: (i, 0))
    tabp = pl.BlockSpec((ATT_BLOCK, 128), lambda b, i: (jnp.maximum(i - 1, 0), 0))
    p_ispecs, p_ospecs, p_oshapes, p_alias, p_scratch, p_args = _host_phase(phase, 13, 1)
    res = pl.pallas_call(
        body, name=name,
        grid=(B, nb),
        in_specs=[rowblk(ATT_WIDTH, 0), rowblk(256, 4),
                  pl.BlockSpec((None, ATT_BLOCK, 256), lambda b, i: (b, jnp.maximum(i - 1, 0), 4)),
                  rowblk(256, _Z0), rowblk(256, _Z0 + 1), rowblk(256, _Z0 + 2), rowblk(256, _Z0 + 3),
                  tab, tab, tabp, tabp,
                  pl.BlockSpec((ATT_HEADS * ATT_BLOCK, 128), lambda b, i: (0, 0)),
                  pl.BlockSpec(memory_space=pl.ANY)] + p_ispecs,
        out_specs=[pl.BlockSpec((None, ATT_BLOCK, ATT_WIDTH), lambda b, i: (b, i, 1))] + p_ospecs,
        out_shape=[jax.ShapeDtypeStruct(u.shape, BF16)] + p_oshapes,
        input_output_aliases={12: 0, **p_alias},
        scratch_shapes=p_scratch,
        compiler_params=_params(("arbitrary", "arbitrary")),
    )(proj_a, proj_a, proj_a, proj_a, proj_a, proj_a, proj_a, cos, sin, cos, sin, sinks_l, u, *p_args)
    return res[0], list(res[1:])


def _attn_bwd(proj_a, du, sinks_l, cos, sin, name, phase=None):
    B, S, _ = proj_a.shape
    nb = S // ATT_BLOCK

    def body(*refs):
        ins, outs, (carry, sk_acc), p_in, p_out, p_sems = _split_refs(refs, 13, 4, 2, phase)
        q_ref, kvc_ref, kvp_ref, z0, z1, z2, z3, du_ref, cos_ref, sin_ref, cosp_ref, sinp_ref, sinks_ref = ins
        dq_ref, dkv_ref, dz_ref, dsk_ref = outs
        b_id, i = pl.program_id(0), pl.program_id(1)
        _hosted_start(phase, p_in, p_out, p_sems, (b_id == 0) & (i == 0))

        @pl.when((b_id == 0) & (i == 0))
        def _():
            sk_acc[...] = jnp.zeros_like(sk_acc)

        @pl.when(i == 0)
        def _():
            carry[...] = jnp.zeros_like(carry)

        @pl.when(i < nb)
        def _():
            left, rope, rope_bwd, dup = _attn_common()
            cos_c, sin_c = cos_ref[...], sin_ref[...]
            cos_p, sin_p = cosp_ref[...], sinp_ref[...]
            kvc = kvc_ref[...]
            kvp = kvp_ref[...]
            kw = jnp.concatenate([rope(kvp[:, :KV_WIDTH], cos_p, sin_p), rope(kvc[:, :KV_WIDTH], cos_c, sin_c)], axis=0)
            vw = jnp.concatenate([kvp[:, KV_WIDTH:], kvc[:, KV_WIDTH:]], axis=0)
            kd, vd = dup(kw), dup(vw)
            bias = _attn_bias(i)
            zs = (z0, z1, z2, z3)
            pairs = [range(4 * kvh, 4 * kvh + 4) for kvh in range(2)]
            kvs = range(2)
            qs = [_stack_heads([rope(q_ref[:, 128 * pr:128 * (pr + 1)], cos_c, sin_c) for pr in pairs[kvh]], left)
                  for kvh in kvs]
            sink = [sinks_ref[kvh * GROUP_ROWS:(kvh + 1) * GROUP_ROWS, :] for kvh in kvs]
            p, ps, o = _attn_probs(qs, kd, vd, sink, bias)
            dos = []
            for kvh in kvs:
                parts = []
                for pp, pr in enumerate(pairs[kvh]):
                    cols = slice(128 * pr, 128 * (pr + 1))
                    sg, sg_grad = _silu_and_grad(zs[pr // 2][:, 128 * (pr % 2):128 * (pr % 2 + 1)])
                    du128 = du_ref[:, cols]
                    dz_ref[:, cols] = (du128 * _unstack_heads(o[kvh], left, pp) * sg_grad).astype(BF16)
                    parts.append(du128 * sg)
                dos.append(_stack_heads(parts, left))
            dp = [_dot(dos[kvh], vd[kvh], NT) for kvh in kvs]
            delta = [_row_sums(p[kvh] * dp[kvh]) for kvh in kvs]
            ds = [p[kvh] * (dp[kvh] - jnp.concatenate([delta[kvh], delta[kvh]], axis=1)) * ATT_SCALE for kvh in kvs]
            dqs = [_dot(ds[kvh], kd[kvh], NN) for kvh in kvs]
            dkd = [_dot(ds[kvh], qs[kvh], TN) for kvh in kvs]
            dvd = [_dot(p[kvh], dos[kvh], TN) for kvh in kvs]
            for kvh in kvs:
                sk_acc[kvh] += -ps[kvh] * delta[kvh]
                for pp, pr in enumerate(pairs[kvh]):
                    dq_ref[:, 128 * pr:128 * (pr + 1)] = rope_bwd(_unstack_heads(dqs[kvh], left, pp),
                                                                  cos_c, sin_c).astype(BF16)
            fold = lambda pr: jnp.where(left, pr[0] + pltpu.roll(pr[0], ATT_DIM, 1), pr[1] + pltpu.roll(pr[1], ATT_DIM, 1))
            dkw = fold(dkd)
            dvw = fold(dvd)
            prev = jnp.concatenate([rope_bwd(dkw[:ATT_BLOCK], cos_p, sin_p), dvw[:ATT_BLOCK]], axis=1)
            cur = jnp.concatenate([rope_bwd(dkw[ATT_BLOCK:], cos_c, sin_c), dvw[ATT_BLOCK:]], axis=1)
            dkv_ref[...] = (carry[...] + prev).astype(BF16)
            carry[...] = cur

        @pl.when(i == nb)
        def _():
            dkv_ref[...] = carry[...].astype(BF16)

        @pl.when((b_id == B - 1) & (i == nb))
        def _():
            lane = lax.broadcasted_iota(jnp.int32, (1, 128), 1)
            tot = jnp.zeros((1, 128), F32)
            for hd in range(ATT_HEADS):
                rows = sk_acc[hd // GROUP, (hd % GROUP) * ATT_BLOCK:(hd % GROUP + 1) * ATT_BLOCK, :]
                tot = tot + jnp.where(lane == hd, jnp.sum(rows, axis=0, keepdims=True), 0.0)
            dsk_ref[...] = tot

        _hosted_finish(phase, p_in, p_out, p_sems, (b_id == B - 1) & (i == nb))

    cl = lambda i: jnp.minimum(i, nb - 1)
    pv = lambda i: jnp.maximum(jnp.minimum(i, nb - 1) - 1, 0)
    rowblk = lambda w, cb: pl.BlockSpec((None, ATT_BLOCK, w), lambda b, i: (b, cl(i), cb))
    tab = pl.BlockSpec((ATT_BLOCK, 128), lambda b, i: (cl(i), 0))
    tabp = pl.BlockSpec((ATT_BLOCK, 128), lambda b, i: (pv(i), 0))
    p_ispecs, p_ospecs, p_oshapes, p_alias, p_scratch, p_args = _host_phase(phase, 13, 4)
    res = pl.pallas_call(
        body, name=name,
        grid=(B, nb + 1),
        in_specs=[rowblk(ATT_WIDTH, 0), rowblk(256, 4),
                  pl.BlockSpec((None, ATT_BLOCK, 256), lambda b, i: (b, pv(i), 4)),
                  rowblk(256, _Z0), rowblk(256, _Z0 + 1), rowblk(256, _Z0 + 2), rowblk(256, _Z0 + 3),
                  rowblk(ATT_WIDTH, 1),
                  tab, tab, tabp, tabp,
                  pl.BlockSpec((ATT_HEADS * ATT_BLOCK, 128), lambda b, i: (0, 0))] + p_ispecs,
        out_specs=[rowblk(ATT_WIDTH, 0),
                   pl.BlockSpec((None, ATT_BLOCK, 256), lambda b, i: (b, jnp.maximum(i - 1, 0), 0)),
                   rowblk(ATT_WIDTH, 0),
                   pl.BlockSpec((1, 128), lambda b, i: (0, 0))] + p_ospecs,
        out_shape=[jax.ShapeDtypeStruct((B, S, ATT_WIDTH), BF16), jax.ShapeDtypeStruct((B, S, 256), BF16),
                   jax.ShapeDtypeStruct((B, S, ATT_WIDTH), BF16), jax.ShapeDtypeStruct((1, 128), F32)] + p_oshapes,
        input_output_aliases=p_alias,
        scratch_shapes=[pltpu.VMEM((ATT_BLOCK, 256), F32), pltpu.VMEM((2, GROUP_ROWS, 128), F32)] + p_scratch,
        compiler_params=_params(("arbitrary", "arbitrary")),
    )(proj_a, proj_a, proj_a, proj_a, proj_a, proj_a, proj_a, du, cos, sin, cos, sin, sinks_l, *p_args)
    return tuple(res[:4]) + (list(res[4:]),)


def _outproj_fwd(u2, w_out, x2, g_post, target2, name):
    T, D = x2.shape
    tm = _pick(T, (512, 256, 128))
    last = target2 is not None

    def body(u_ref, w_ref, x_ref, g_ref, *rest):
        y = lax.dot_general(u_ref[...], w_ref[...], (NN, ((), ())), preferred_element_type=F32)
        r = lax.rsqrt(jnp.mean(y * y, axis=-1, keepdims=True) + NORM_EPS)
        xn = x_ref[...] + (y * r) * g_ref[...]
        if last:
            t_ref, y_ref, dx_ref, loss_ref = rest
            err = xn - t_ref[...]
            dx_ref[...] = err * (1.0 / D)
            sq = err * err
            acc = sq[:, 0:128]
            for kk in range(1, D // 128):
                acc = acc + sq[:, 128 * kk:128 * (kk + 1)]
            part = jnp.sum(acc.reshape(tm // 8, 8, 128), axis=0) * (0.5 / D)

            @pl.when(pl.program_id(0) == 0)
            def _():
                loss_ref[...] = jnp.zeros_like(loss_ref)

            loss_ref[...] += part
        else:
            y_ref, xn_ref = rest
            xn_ref[...] = xn
        y_ref[...] = y

    row = pl.BlockSpec((tm, D), lambda i: (i, 0))
    in_specs = [pl.BlockSpec((tm, MIX_WIDTH), lambda i: (i, 0)),
                pl.BlockSpec((MIX_WIDTH, D), lambda i: (0, 0)), row,
                pl.BlockSpec((1, D), lambda i: (0, 0))]
    args = [u2, w_out, x2, g_post]
    out_specs = [row, row]
    out_shape = [jax.ShapeDtypeStruct((T, D), F32), jax.ShapeDtypeStruct((T, D), F32)]
    if last:
        in_specs.append(row)
        args.append(target2)
        out_specs.append(pl.BlockSpec((8, 128), lambda i: (0, 0)))
        out_shape.append(jax.ShapeDtypeStruct((8, 128), F32))
    return pl.pallas_call(
        body, name=name, grid=(T // tm,), in_specs=in_specs, out_specs=out_specs, out_shape=out_shape,
        compiler_params=_params(("arbitrary",)),
    )(*args)


def _outproj_bwd(dxn2, y2, g_post, w_out, name):
    T, D = y2.shape
    N = w_out.shape[0]
    tm = _pick(T, (512, 256, 128))
    nt = T // tm

    def body(dx_ref, y_ref, g_ref, w_ref, dy_ref, dg_ref, du_ref, acc):
        i = pl.program_id(0)

        @pl.when(i == 0)
        def _():
            acc[...] = jnp.zeros_like(acc)

        y = y_ref[...]
        dxn = dx_ref[...]
        r = lax.rsqrt(jnp.mean(y * y, axis=-1, keepdims=True) + NORM_EPS)
        n = y * r
        dn = dxn * g_ref[...]
        dy = (r * (dn - n * jnp.mean(dn * n, axis=-1, keepdims=True))).astype(BF16)
        dy_ref[...] = dy
        du_ref[...] = lax.dot_general(dy, w_ref[...], (NT, ((), ())), preferred_element_type=F32)
        acc[...] += jnp.sum((dxn * n).reshape(tm // 8, 8, D), axis=0)

        @pl.when(i == nt - 1)
        def _():
            dg_ref[...] = jnp.sum(acc[...], axis=0, keepdims=True)

    row = pl.BlockSpec((tm, D), lambda i: (i, 0))
    vec = pl.BlockSpec((1, D), lambda i: (0, 0))
    return pl.pallas_call(
        body, name=name, grid=(nt,),
        in_specs=[row, row, vec, pl.BlockSpec((N, D), lambda i: (0, 0), pipeline_mode=pl.Buffered(1))],
        out_specs=[row, vec, pl.BlockSpec((tm, N), lambda i: (i, 0))],
        out_shape=[jax.ShapeDtypeStruct((T, D), BF16), jax.ShapeDtypeStruct((1, D), F32),
                   jax.ShapeDtypeStruct((T, N), F32)],
        scratch_shapes=[pltpu.VMEM((8, D), F32)],
        compiler_params=_params(("arbitrary",)),
    )(dxn2, y2, g_post, w_out)


def _inproj_bwd(pieces, w_t, x2, dxn2, g_pre, name, phase=None):
    T, D = x2.shape
    widths = [p.shape[1] for p in pieces]
    offs = [sum(widths[:i]) for i in range(len(pieces))]
    n_p = len(pieces)
    tm = _pick(T, (256, 128))
    nt = T // tm

    def body(*refs):
        ins, (dx_ref, dg_ref), (acc,), p_in, p_out, p_sems = _split_refs(refs, n_p + 4, 2, 1, phase)
        w_ref, x_ref, dxn_ref, g_ref = ins[n_p:]
        i = pl.program_id(0)
        _hosted_start(phase, p_in, p_out, p_sems, i == 0)

        @pl.when(i == 0)
        def _():
            acc[...] = jnp.zeros_like(acc)

        dh = jnp.zeros((tm, D), F32)
        for p in range(n_p):
            dh = dh + lax.dot_general(ins[p][...], w_ref[offs[p]:offs[p] + widths[p], :], (NN, ((), ())),
                                      preferred_element_type=F32)
        x = x_ref[...]
        r = lax.rsqrt(jnp.mean(x * x, axis=-1, keepdims=True) + NORM_EPS)
        n = x * r
        dn = dh * g_ref[...]
        dx_ref[...] = dxn_ref[...] + r * (dn - n * jnp.mean(dn * n, axis=-1, keepdims=True))
        acc[...] += jnp.sum((dh * n).reshape(tm // 8, 8, D), axis=0)

        @pl.when(i == nt - 1)
        def _():
            dg_ref[...] = jnp.sum(acc[...], axis=0, keepdims=True)

        _hosted_finish(phase, p_in, p_out, p_sems, i == nt - 1)

    row = pl.BlockSpec((tm, D), lambda i: (i, 0))
    vec = pl.BlockSpec((1, D), lambda i: (0, 0))
    p_ispecs, p_ospecs, p_oshapes, p_alias, p_scratch, p_args = _host_phase(phase, n_p + 4, 2)
    res = pl.pallas_call(
        body, name=name, grid=(nt,),
        in_specs=[pl.BlockSpec((tm, w), lambda i: (i, 0)) for w in widths]
        + [pl.BlockSpec((sum(widths), D), lambda i: (0, 0), pipeline_mode=pl.Buffered(1)), row, row, vec] + p_ispecs,
        out_specs=[row, vec] + p_ospecs,
        out_shape=[jax.ShapeDtypeStruct((T, D), F32), jax.ShapeDtypeStruct((1, D), F32)] + p_oshapes,
        input_output_aliases=p_alias,
        scratch_shapes=[pltpu.VMEM((8, D), F32)] + p_scratch,
        compiler_params=_params(("arbitrary",)),
    )(*pieces, w_t, x2, dxn2, g_pre, *p_args)
    return res[0], res[1], list(res[2:])


def _step(x, target, g_pre, g_post, lb_param, g_head, sinks, shards=None, full=None):
    B, S, D = x.shape
    T = B * S
    dist = shards is not None
    if dist:
        a_loc, b_loc = shards
        ra, rb = a_loc.shape[1], b_loc.shape[1]
        side = _own_side_blocks()
        placed = lambda loc, nm: _place_own(loc, side, "place_" + nm)
        gather = lambda phase, nm: _run_phase(phase, nm)
        w_in0 = gather(_gather_ici_phase([a_loc[0]], [placed(a_loc[0], "in0")]), "gather_in0_ici")
        w_in0 = gather(_gather_d2d_phase(w_in0, [ra]), "gather_in0_d2d")[0]
        late_locs = [a_loc[1], b_loc[1], b_loc[0]]
        late_rs = [ra, rb, rb]
        late_full = [placed(a_loc[1], "in1"), placed(b_loc[1], "out1"), placed(b_loc[0], "out0")]
        w_in, w_out = [w_in0, None], [None, None]
    else:
        w_in, w_out = list(full[0]), list(full[1])
    cos, sin = _rope_tables(S)
    saved = []
    xs = x
    loss_part = None
    dxn = None
    for l in range(DEPTH):
        x2 = xs.reshape(T, D)
        host = dist and l == 0
        proj_h, proj_a, h = _inproj(x2, g_pre[l:l + 1], w_in[l], f"inproj{l}")
        proj_h = proj_h.reshape(B, S, N_H)
        proj_a = proj_a.reshape(B, S, N_A)
        o_h, u, states, got = _hgrn_fwd(proj_h, MIX_WIDTH, lb_param, g_head[l:l + 1], l, f"hgrn_fwd{l}",
                                        _gather_ici_phase(late_locs, late_full) if host else None)
        u, got = _attn_fwd(proj_a, u, _sink_rows(sinks[l]), cos, sin, f"attn_fwd{l}",
                           _gather_d2d_phase(got, late_rs) if host else None)
        if host:
            w_in[1], w_out[1], w_out[0] = got
        u2 = u.reshape(T, MIX_WIDTH)
        if l < DEPTH - 1:
            y, xn = _outproj_fwd(u2, w_out[l], x2, g_post[l:l + 1], None, f"outproj{l}")
            xn = xn.reshape(B, S, D)
        else:
            y, dxn, loss_part = _outproj_fwd(u2, w_out[l], x2, g_post[l:l + 1], target.reshape(T, D), f"outproj{l}")
            xn = None
        saved.append((x2, h, proj_h, proj_a, o_h, u2, states, y))
        xs = xn

    dw_in, dw_out = [None] * DEPTH, [None] * DEPTH
    dg_pre, dg_post, dlb, dg_head, dsinks = [], [], [], [], []
    for l in reversed(range(DEPTH)):
        x2, h, proj_h, proj_a, o_h, u2, states, y = saved[l]
        host = dist and l == 0
        dy, dgp, du = _outproj_bwd(dxn, y, g_post[l:l + 1], w_out[l], f"outproj_bwd{l}")
        du = du.reshape(B, S, MIX_WIDTH)
        dw_out[l] = _mm_tn([u2], dy, f"wgrad_out{l}")
        if host:
            early = [dw_in[1], dw_out[1], dw_out[0]]
        dqh, dfh, dih, dzh, dlb_l, dgh, got = _hgrn_bwd(
            proj_h, o_h, du, states, lb_param, g_head[l:l + 1], l, f"hgrn_bwd{l}",
            _reduce_d2d_phase(early, late_rs) if host else None)
        if host:
            parts = [_pair_sum(g, r, side, f"pair_sum{i}") for i, (g, r) in enumerate(zip(early, got))]
        dqa, dkv, dza, dsk, got = _attn_bwd(proj_a, du, _sink_rows(sinks[l]), cos, sin, f"attn_bwd{l}",
                                            _reduce_ici_phase(parts) if host else None)
        if host:
            dw_in[1], dw_out[1], dw_out[0] = [_chip_sum(p, r, f"chip_sum{i}")
                                              for i, (p, r) in enumerate(zip(parts, got))]
        dproj = [p.reshape(T, p.shape[-1]) for p in (dqh, dfh, dih, dzh, dqa, dkv, dza)]
        dw_in[l] = _mm_tn(dproj, h, f"wgrad_in{l}")
        if host:
            got = _run_phase(_reduce_d2d_phase([dw_in[0]], [ra]), "reduce_in0_d2d")
            part = _pair_sum(dw_in[0], got[0], side, "pair_sum_in0")
        dxn, dgpre, got = _inproj_bwd(dproj, w_in[l], x2, dxn, g_pre[l:l + 1], f"inproj_bwd{l}",
                                      _reduce_ici_phase([part]) if host else None)
        if host:
            dw_in[0] = _chip_sum(part, got[0], "chip_sum_in0")
        dg_pre.append(dgpre)
        dg_post.append(dgp)
        dlb.append(dlb_l)
        dg_head.append(dgh)
        dsinks.append(dsk)
    rev = lambda lst: jnp.concatenate(lst[::-1], axis=0)
    return (loss_part, dxn.reshape(B, S, D), jnp.stack(dw_in), jnp.stack(dw_out),
            rev(dg_pre), rev(dg_post), rev(dlb), rev(dg_head), rev(dsinks))


def _me_and_peers():
    x, y, c = lax.axis_index("x"), lax.axis_index("y"), lax.axis_index("c")
    me = 4 * x + 2 * y + c
    peers = []
    for k in range(1, N_DEV):
        px = 1 - x if k & 4 else x
        py = 1 - y if k & 2 else y
        pc = 1 - c if k & 1 else c
        peers.append(((px, py, pc), 4 * px + 2 * py + pc))
    return me, peers


class _Phase:
    def __init__(self, arrays, out_shapes, aliases, n_send, build):
        self.arrays, self.out_shapes, self.aliases = list(arrays), list(out_shapes), dict(aliases)
        self.n_send, self.build = n_send, build

    def scratch(self):
        return [pltpu.SemaphoreType.DMA((self.n_send,)), pltpu.SemaphoreType.DMA((self.n_send,))]

    def _copies(self, in_refs, out_refs, sems, arrivals):
        send_sems, recv_sems = sems
        sends, recvs = self.build(in_refs, out_refs)
        assert len(sends) == self.n_send == len(recvs)
        out = [pltpu.make_async_remote_copy(src_ref=s, dst_ref=d, send_sem=send_sems.at[i], recv_sem=recv_sems.at[i],
                                            device_id=dev, device_id_type=MESH) for i, (s, d, dev) in enumerate(sends)]
        inc = [pltpu.make_async_remote_copy(src_ref=s, dst_ref=r, send_sem=send_sems.at[i], recv_sem=recv_sems.at[i],
                                            device_id=dev, device_id_type=MESH)
               for i, ((s, _, dev), r) in enumerate(zip(sends, recvs))] if arrivals else []
        return out, inc

    def start(self, in_refs, out_refs, sems):
        out, _ = self._copies(in_refs, out_refs, sems, False)
        for cp in out:
            cp.start()

    def finish(self, in_refs, out_refs, sems):
        out, inc = self._copies(in_refs, out_refs, sems, True)
        for cp in inc:
            cp.wait_recv()
        for cp in out:
            cp.wait_send()


_ANY = pl.BlockSpec(memory_space=pl.ANY)


def _host_phase(phase, n_in, n_out):
    if phase is None:
        return [], [], [], {}, [], []
    aliases = {n_in + i: n_out + o for i, o in phase.aliases.items()}
    return ([_ANY] * len(phase.arrays), [_ANY] * len(phase.out_shapes), phase.out_shapes, aliases, phase.scratch(),
            phase.arrays)


def _split_refs(refs, n_in, n_out, n_scr, phase):
    pi = len(phase.arrays) if phase else 0
    po = len(phase.out_shapes) if phase else 0
    a = n_in + pi
    b = a + n_out + po
    return (refs[:n_in], refs[a:a + n_out], refs[b:b + n_scr], refs[n_in:a], refs[a + n_out:b], refs[b + n_scr:])


def _hosted_start(phase, p_in, p_out, p_sems, first):
    if phase is not None:
        @pl.when(first)
        def _():
            phase.start(p_in, p_out, p_sems)


def _hosted_finish(phase, p_in, p_out, p_sems, last):
    if phase is not None:
        @pl.when(last)
        def _():
            phase.finish(p_in, p_out, p_sems)


def _run_phase(phase, name):
    n_in, n_out = len(phase.arrays), len(phase.out_shapes)

    def body(*refs):
        phase.start(refs[:n_in], refs[n_in:n_in + n_out], refs[n_in + n_out:])
        phase.finish(refs[:n_in], refs[n_in:n_in + n_out], refs[n_in + n_out:])

    return pl.pallas_call(
        body, name=name, in_specs=[_ANY] * n_in, out_specs=[_ANY] * n_out,
        out_shape=phase.out_shapes, input_output_aliases=phase.aliases, scratch_shapes=phase.scratch(),
        compiler_params=pltpu.CompilerParams(has_side_effects=True),
    )(*phase.arrays)


def _mesh_place():
    x, y, c = lax.axis_index("x"), lax.axis_index("y"), lax.axis_index("c")
    chips = [(x, y), (1 - x, y), (x, 1 - y), (1 - x, 1 - y)]
    num = lambda chip, core: 4 * chip[0] + 2 * chip[1] + core
    return c, chips, num


def _own_side_blocks():
    c, chips, num = _mesh_place()
    return jnp.stack([num(ch, c) for ch in chips]).astype(jnp.int32)


def _rows(ref, r, dev):
    return ref.at[pl.ds(pl.multiple_of(dev * r, 16), r), :]


def _place_own(loc, blocks, name):
    r, D = loc.shape
    tr = _pick(r, (400, 256, 200, 128, 64, 16))

    def body(idx_ref, l_ref, o_ref):
        del idx_ref
        o_ref[...] = l_ref[...]

    return pl.pallas_call(
        body, name=name,
        grid_spec=pltpu.PrefetchScalarGridSpec(
            num_scalar_prefetch=1, grid=(r // tr,),
            in_specs=[pl.BlockSpec((tr, D), lambda i, idx: (i, 0))],
            out_specs=pl.BlockSpec((tr, D), lambda i, idx: (idx[0] * (r // tr) + i, 0))),
        out_shape=jax.ShapeDtypeStruct((N_DEV * r, D), loc.dtype),
        compiler_params=_params(("arbitrary",)),
    )(blocks, loc)


def _gather_ici_phase(locs, fulls):
    rs = [a.shape[0] for a in locs]
    n = len(locs)

    def build(ins, outs):
        c, chips, num = _mesh_place()
        me = num(chips[0], c)
        targets = [((*chips[0], 1 - c), num(chips[0], 1 - c))] + [((*ch, c), num(ch, c)) for ch in chips[1:]]
        sends, recvs = [], []
        for dev, dnum in targets:
            for i, r in enumerate(rs):
                sends.append((ins[i], _rows(outs[i], r, me), dev))
                recvs.append(_rows(outs[i], r, dnum))
        return sends, recvs

    shapes = [jax.ShapeDtypeStruct(a.shape, a.dtype) for a in fulls]
    return _Phase(list(locs) + list(fulls), shapes, {n + i: i for i in range(n)}, 4 * n, build)


def _gather_d2d_phase(fulls, rs):
    def build(ins, outs):
        c, chips, num = _mesh_place()
        sib = (*chips[0], 1 - c)
        sends, recvs = [], []
        for ch in chips[1:]:
            for i, r in enumerate(rs):
                blk = _rows(outs[i], r, num(ch, c))
                sends.append((blk, blk, sib))
                recvs.append(_rows(outs[i], r, num(ch, 1 - c)))
        return sends, recvs

    shapes = [jax.ShapeDtypeStruct(a.shape, a.dtype) for a in fulls]
    return _Phase(fulls, shapes, {i: i for i in range(len(fulls))}, 3 * len(fulls), build)


def _reduce_d2d_phase(grads, rs):
    def build(ins, outs):
        c, chips, num = _mesh_place()
        sib = (*chips[0], 1 - c)
        sends, recvs = [], []
        for j, ch in enumerate(chips):
            for i, r in enumerate(rs):
                sends.append((_rows(ins[i], r, num(ch, 1 - c)), outs[i].at[j], sib))
                recvs.append(outs[i].at[j])
        return sends, recvs

    shapes = [jax.ShapeDtypeStruct((4, r, g.shape[1]), g.dtype) for g, r in zip(grads, rs)]
    return _Phase(grads, shapes, {}, 4 * len(grads), build)


def _reduce_ici_phase(parts):
    def build(ins, outs):
        c, chips, _ = _mesh_place()
        sends, recvs = [], []
        for t in range(1, 4):
            for i in range(len(parts)):
                sends.append((ins[i].at[t], outs[i].at[t - 1], (*chips[t], c)))
                recvs.append(outs[i].at[t - 1])
        return sends, recvs

    shapes = [jax.ShapeDtypeStruct((3,) + p.shape[1:], p.dtype) for p in parts]
    return _Phase(parts, shapes, {}, 3 * len(parts), build)


def _pair_sum(g, got, blocks, name):
    n, r, D = got.shape
    tr = _pick(r, (400, 256, 200, 128, 64, 16))

    def body(idx_ref, g_ref, r_ref, o_ref):
        del idx_ref
        o_ref[...] = (g_ref[...].astype(F32) + r_ref[...].astype(F32)).astype(o_ref.dtype)

    blk = pl.BlockSpec((None, tr, D), lambda j, i, idx: (j, i, 0))
    return pl.pallas_call(
        body, name=name,
        grid_spec=pltpu.PrefetchScalarGridSpec(
            num_scalar_prefetch=1, grid=(n, r // tr),
            in_specs=[pl.BlockSpec((tr, D), lambda j, i, idx: (idx[j] * (r // tr) + i, 0)), blk],
            out_specs=blk),
        out_shape=jax.ShapeDtypeStruct(got.shape, got.dtype),
        compiler_params=_params(("arbitrary", "arbitrary")),
    )(blocks, g, got)


def _chip_sum(p, r, name):
    _, R, D = p.shape
    tr = _pick(R, (400, 256, 200, 128, 64, 16))

    def body(p_ref, r_ref, o_ref):
        acc = p_ref[...].astype(F32)
        for t in range(3):
            acc = acc + r_ref[t].astype(F32)
        o_ref[...] = acc

    return pl.pallas_call(
        body, name=name, grid=(R // tr,),
        in_specs=[pl.BlockSpec((None, tr, D), lambda i: (0, i, 0)), pl.BlockSpec((3, tr, D), lambda i: (0, i, 0))],
        out_specs=pl.BlockSpec((tr, D), lambda i: (i, 0)), out_shape=jax.ShapeDtypeStruct((R, D), F32),
        compiler_params=_params(("parallel",)))(p, r)


def _allreduce_small(vec):
    R, C = vec.shape

    def body(v_ref, o_ref, buf, send_sems, recv_sems):
        me, peers = _me_and_peers()
        buf[me] = v_ref[...]
        sends = []
        for k, (pid, _) in enumerate(peers):
            cp = pltpu.make_async_remote_copy(src_ref=v_ref, dst_ref=buf.at[me], send_sem=send_sems.at[k],
                                              recv_sem=recv_sems.at[k], device_id=pid, device_id_type=MESH)
            cp.start()
            sends.append(cp)
        for k, (pid, pnum) in enumerate(peers):
            pltpu.make_async_remote_copy(src_ref=v_ref, dst_ref=buf.at[pnum], send_sem=send_sems.at[k],
                                         recv_sem=recv_sems.at[k], device_id=pid, device_id_type=MESH).wait_recv()
        for cp in sends:
            cp.wait_send()
        acc = buf[0]
        for d in range(1, N_DEV):
            acc = acc + buf[d]
        o_ref[...] = acc

    vm = pl.BlockSpec(memory_space=pltpu.VMEM)
    return pl.pallas_call(
        body, name="allreduce_small",
        in_specs=[vm], out_specs=vm,
        out_shape=jax.ShapeDtypeStruct((R, C), F32),
        scratch_shapes=[pltpu.VMEM((N_DEV, R, C), F32), pltpu.SemaphoreType.DMA((N_DEV - 1,)),
                        pltpu.SemaphoreType.DMA((N_DEV - 1,))],
        compiler_params=pltpu.CompilerParams(has_side_effects=True),
    )(vec)


def _adamw(w, g, m, v, name):
    R, C = w.shape
    tr = _pick(R, (256, 128, 64, 32, 16, 8)) if R >= 8 else R
    c1 = 1.0 - ADAM_B1 ** ADAM_STEP
    c2 = 1.0 - ADAM_B2 ** ADAM_STEP

    def body(w_ref, g_ref, m_ref, v_ref, d_ref, mo_ref, vo_ref):
        gg = g_ref[...]
        mn = ADAM_B1 * m_ref[...] + (1.0 - ADAM_B1) * gg
        vn = ADAM_B2 * v_ref[...] + (1.0 - ADAM_B2) * (gg * gg)
        d_ref[...] = -ADAM_LR * ((mn / c1) / (jnp.sqrt(vn / c2) + ADAM_EPS) + ADAM_WD * w_ref[...])
        mo_ref[...] = mn
        vo_ref[...] = vn

    blk = pl.BlockSpec((tr, C), lambda i: (i, 0))
    sh = jax.ShapeDtypeStruct((R, C), F32)
    return pl.pallas_call(
        body, name=name, grid=(R // tr,), in_specs=[blk] * 4, out_specs=[blk] * 3, out_shape=[sh] * 3,
        compiler_params=_params(("parallel",)),
    )(w, g, m, v)


def _lb_param_grad(lb_param, dlb):
    L, C = lb_param.shape

    def body(p_ref, d_ref, o_ref):
        lbp = p_ref[...]
        d = d_ref[...]
        mx = jnp.max(lbp, axis=0, keepdims=True)
        e = jnp.exp(lbp - mx)
        p = e / jnp.sum(e, axis=0, keepdims=True)
        tot = jnp.sum(d, axis=0, keepdims=True)
        dps = []
        rest = tot
        for j in range(L):
            dps.append(rest - tot if j == 0 else rest)
            rest = rest - d[j:j + 1]
        dp = jnp.concatenate(dps, axis=0)
        o_ref[...] = p * (dp - jnp.sum(p * dp, axis=0, keepdims=True))

    vm = pl.BlockSpec(memory_space=pltpu.VMEM)
    return pl.pallas_call(body, name="lb_param_grad", in_specs=[vm, vm], out_specs=vm,
                          out_shape=jax.ShapeDtypeStruct((L, C), F32))(lb_param, dlb)


def _pack_small(loss_part, dg_pre, dg_post, dlb, dg_head, dsinks):
    pad8 = lambda a: jnp.pad(a.reshape(-1, 128), ((0, 8 - DEPTH), (0, 0)))
    rows = [dg_pre.reshape(-1, 128), dg_post.reshape(-1, 128), dlb.reshape(-1, 128), pad8(dg_head), pad8(dsinks),
            loss_part]
    return jnp.concatenate(rows, axis=0)


def _unpack_small(vec):
    n = DEPTH * D_MODEL // 128
    o = 0
    dg_pre = vec[o:o + n].reshape(DEPTH, D_MODEL); o += n
    dg_post = vec[o:o + n].reshape(DEPTH, D_MODEL); o += n
    dlb = vec[o:o + n].reshape(DEPTH, HG_WIDTH); o += n
    dg_head = vec[o:o + DEPTH]; o += 8
    dsinks = vec[o:o + DEPTH, :ATT_HEADS]; o += 8
    loss = jnp.sum(vec[o:o + 8])
    return loss, dg_pre, dg_post, dlb, dg_head, dsinks


def kernel(x, w_in, w_out, g_pre, g_post, lb_param, g_head, sinks, loss_target, m_w_in, m_w_out, m_g_pre, m_g_post, m_lb_param, m_g_head, m_sinks, v_w_in, v_w_out, v_g_pre, v_g_post, v_lb_param, v_g_head, v_sinks):
    L, D, nloc = w_in.shape
    w_in_t_loc = jnp.swapaxes(w_in, 1, 2).astype(BF16)
    (loss_part, dx, gw_in_t, gw_out, dg_pre, dg_post, dlb, dg_head, dsinks) = _step(
        x, loss_target, g_pre, g_post, lb_param, g_head, sinks, shards=(w_in_t_loc, w_out.astype(BF16)))
    gw_in = jnp.swapaxes(gw_in_t, 1, 2)

    small = _allreduce_small(_pack_small(loss_part, dg_pre, dg_post, dlb, dg_head, dsinks))
    loss, gg_pre, gg_post, gdlb, gg_head, gsinks = _unpack_small(small)
    glb = _lb_param_grad(lb_param, gdlb)

    grads = [gw_in, gw_out, gg_pre, gg_post, glb, gg_head, gsinks]
    ws = [w_in, w_out, g_pre, g_post, lb_param, g_head, sinks]
    ms = [m_w_in, m_w_out, m_g_pre, m_g_post, m_lb_param, m_g_head, m_sinks]
    vs = [v_w_in, v_w_out, v_g_pre, v_g_post, v_lb_param, v_g_head, v_sinks]
    names = ["w_in", "w_out", "g_pre", "g_post", "lb_param", "g_head", "sinks"]
    deltas, new_m, new_v = [], [], []
    for w, g, m, v, nm in zip(ws, grads, ms, vs, names):
        sh = w.shape
        two = lambda a: a.reshape(-1, sh[-1])
        d, mn, vn = _adamw(two(w), two(g), two(m), two(v), "adamw_" + nm)
        deltas.append(d.reshape(sh))
        new_m.append(mn.reshape(sh))
        new_v.append(vn.reshape(sh))
    return (loss, dx, *grads, *deltas, *new_m, *new_v)
```

```python
import functools
import math

import numpy as np
import jax
import jax.numpy as jnp
from jax import lax
from jax.experimental import pallas as pl
from jax.experimental.pallas import tpu as pltpu

F32 = jnp.float32
BF16 = jnp.bfloat16

D_MODEL = 1024
DEPTH = 2
HG_HEADS = 8
HG_DIM = 128
HG_WIDTH = HG_HEADS * HG_DIM
CHUNK = 64
ATT_HEADS = 16
ATT_DIM = 64
ATT_WIDTH = ATT_HEADS * ATT_DIM
KV_WIDTH = 128
ATT_BLOCK = 128
ATT_SCALE = 1.0 / math.sqrt(ATT_DIM)
ROPE_THETA = 10000.0
NORM_EPS = 1e-6
NEG_INF = -1e30
LB_FLOOR = 1e-20
N_H = 4 * HG_WIDTH
N_A = 2 * ATT_WIDTH + 2 * KV_WIDTH
IN_WIDTH = N_H + N_A
MIX_WIDTH = HG_WIDTH + ATT_WIDTH

ADAM_LR = 0.001
ADAM_B1 = 0.9
ADAM_B2 = 0.999
ADAM_EPS = 1e-08
ADAM_WD = 0.01
ADAM_STEP = 10

N_DEV = 8
MESH = pl.DeviceIdType.MESH
VMEM_LIMIT = 56 * 1024 * 1024

NN = ((1,), (0,))
NT = ((1,), (1,))
TN = ((0,), (0,))


def _dot(a, b, dims):
    return lax.dot_general(a.astype(BF16), b.astype(BF16), (dims, ((), ())), preferred_element_type=F32)


def _params(sem=None, **kw):
    return pltpu.CompilerParams(dimension_semantics=sem, vmem_limit_bytes=VMEM_LIMIT, **kw)


def _sigmoids(x):
    e = jnp.exp(-jnp.abs(x))
    r = 1.0 / (1.0 + e)
    er = e * r
    pos = x >= 0.0
    return jnp.where(pos, r, er), jnp.where(pos, er, r)


def _silu(x):
    return x * _sigmoids(x)[0]


def _silu_and_grad(x):
    s, ns = _sigmoids(x)
    return x * s, s * (1.0 + x * ns)


def _pick(n, prefs):
    for p in prefs:
        if n % p == 0:
            return p
    return n


def _inproj(x2, g, w, name):
    T, D = x2.shape
    tm = _pick(T, (256, 128))
    nchunk = 1024

    def body(x_ref, g_ref, w_ref, oh_ref, oa_ref, h_ref):
        x = x_ref[...]
        r = lax.rsqrt(jnp.mean(x * x, axis=-1, keepdims=True) + NORM_EPS)
        h = ((x * r) * g_ref[...]).astype(BF16)
        h_ref[...] = h
        for j in range(0, N_H, nchunk):
            oh_ref[:, j:j + nchunk] = lax.dot_general(h, w_ref[j:j + nchunk, :], (NT, ((), ())),
                                                      preferred_element_type=F32)
        for j in range(0, N_A, N_A // 2):
            oa_ref[:, j:j + N_A // 2] = lax.dot_general(h, w_ref[N_H + j:N_H + j + N_A // 2, :], (NT, ((), ())),
                                                        preferred_element_type=F32)

    row = lambda w_: pl.BlockSpec((tm, w_), lambda i: (i, 0))
    return pl.pallas_call(
        body, name=name,
        grid=(T // tm,),
        in_specs=[row(D), pl.BlockSpec((1, D), lambda i: (0, 0)),
                  pl.BlockSpec((IN_WIDTH, D), lambda i: (0, 0), pipeline_mode=pl.Buffered(1))],
        out_specs=[row(N_H), row(N_A), row(D)],
        out_shape=[jax.ShapeDtypeStruct((T, N_H), F32), jax.ShapeDtypeStruct((T, N_A), F32),
                   jax.ShapeDtypeStruct((T, D), BF16)],
        compiler_params=_params(("parallel",)),
    )(x2, g, w)


def _mm_tn(pieces, b, name, out_dtype=BF16):
    T, m = b.shape
    tn = 256
    counts = [p.shape[1] // tn for p in pieces]
    starts = [sum(counts[:i]) for i in range(len(pieces))]
    n_p = len(pieces)

    def body(*refs):
        b_ref, o_ref = refs[n_p], refs[n_p + 1]
        i = pl.program_id(0)
        for p in range(n_p):
            @pl.when((i >= starts[p]) & (i < starts[p] + counts[p]))
            def _(p=p):
                o_ref[...] = lax.dot_general(refs[p][...], b_ref[...], (TN, ((), ())),
                                             preferred_element_type=F32).astype(out_dtype)

    piece_spec = lambda s, c: pl.BlockSpec((T, tn), lambda i: (0, jnp.clip(i - s, 0, c - 1)))
    return pl.pallas_call(
        body, name=name,
        grid=(sum(counts),),
        in_specs=[piece_spec(s, c) for s, c in zip(starts, counts)]
        + [pl.BlockSpec((T, m), lambda i: (0, 0), pipeline_mode=pl.Buffered(1))],
        out_specs=pl.BlockSpec((tn, m), lambda i: (i, 0)),
        out_shape=jax.ShapeDtypeStruct((sum(counts) * tn, m), out_dtype),
        compiler_params=_params(("arbitrary",)),
    )(*pieces, b)


_LEVELS = (0, 1, 2, 4, 8, 16, 32)
_CUM_L = (2, 4, 8, 16, 32, 64)
_ALL_KINDS = tuple(("c", L) for L in _CUM_L) + tuple(("r", L) for L in _CUM_L)
_MXU_KINDS = (("c", 2), ("c", 4), ("c", CHUNK), ("r", 2), ("r", 4))
N_CUM = len(_ALL_KINDS) * CHUNK
N_CUM_F = len(_MXU_KINDS) * CHUNK


def _cum_matrices():
    t = np.arange(CHUNK)[:, None]
    r = np.arange(CHUNK)[None, :]

    def mat(kind):
        c, L = kind
        return ((r // L == t // L) & ((r <= t) if c == "c" else (r > t))).astype(np.float32)

    fwd = np.concatenate([mat(kd) for kd in _MXU_KINDS], axis=0)
    full = np.concatenate([mat(kd) for kd in _ALL_KINDS], axis=0)
    return jnp.asarray(fwd, BF16), jnp.asarray(full.T.copy(), BF16)


def _level_masks():
    t = np.arange(CHUNK)[:, None]
    s = np.arange(CHUNK)[None, :]
    ms = []
    for L in _LEVELS:
        if L == 0:
            ms.append(t == s)
        else:
            ms.append((t // (2 * L) == s // (2 * L)) & ((t // L) % 2 == 1) & ((s // L) % 2 == 0))
    return jnp.asarray(np.stack(ms).astype(np.float32))


def _split3(x):
    hi = x.astype(BF16)
    r1 = x - hi.astype(F32)
    mid = r1.astype(BF16)
    lo = (r1 - mid.astype(F32)).astype(BF16)
    return hi, mid, lo


def _cum3(ts, x, terms=3):
    d = lambda p: lax.dot_general(ts, p, (NN, ((), ())), preferred_element_type=F32)
    return sum(d(p) for p in _split3(x)[:terms])


def _lb_terms(lbp, layer):
    mx = jnp.max(lbp, axis=0, keepdims=True)
    e = jnp.exp(lbp - mx)
    p = e / jnp.sum(e, axis=0, keepdims=True)
    cum = p[0:1]
    for j in range(1, layer + 1):
        cum = cum + p[j:j + 1]
    lb = cum - p[0:1]
    lbf = jnp.maximum(lb, LB_FLOOR)
    return dict(lbf=lbf, one_m=1.0 - lb, kcorr=lb - lbf, ind=jnp.where(lb > LB_FLOOR, 1.0, 0.0))


def _gate(x, lt):
    sig, nsig = _sigmoids(x)
    f = lt["lbf"] + lt["one_m"] * sig
    return jnp.log(f), lt["one_m"] * nsig + lt["kcorr"], f, sig, nsig


def _ck(x, ci):
    return x[ci * CHUNK:(ci + 1) * CHUNK]


def _block_cums(ts, g, nc):
    cs = [_cum3(ts, _ck(g, ci)) for ci in range(nc)]
    out = {kind: jnp.concatenate([c[CHUNK * i:CHUNK * (i + 1)] for c in cs], axis=0)
           for i, kind in enumerate(_MXU_KINDS)}
    b = out[("c", CHUNK)]
    ng = CHUNK // 8
    last = b.reshape(nc, ng, 8, HG_DIM)[:, :, 7:8, :]
    zero = jnp.zeros((nc, 1, 1, HG_DIM), F32)

    def spread(groups):
        return jnp.broadcast_to(jnp.concatenate(groups, axis=1), (nc, ng, 8, HG_DIM)).reshape(nc * CHUNK, HG_DIM)

    def get(kind):
        if kind in out:
            return out[kind]
        c, L = kind
        nb = L // 8
        first = lambda r: (r // nb) * nb
        if c == "c":
            return b - spread([last[:, first(r) - 1:first(r)] if r >= nb else zero for r in range(ng)])
        return spread([last[:, first(r) + nb - 1:first(r) + nb] for r in range(ng)]) - b

    return get


def _level_factors(cums, g, L):
    if L == 0:
        return None, None
    if L == 1:
        return jnp.exp(g), None
    return jnp.exp(cums(("c", L))), jnp.exp(cums(("r", L)))


def _mul(a, e):
    return a if e is None else a * e


def _hg_block_fwd(qf, k, v, g, ts, m_ref, nc):
    cums = _block_cums(ts, g, nc)
    amat = [jnp.zeros((CHUNK, CHUNK), F32)] * nc
    for li, L in enumerate(_LEVELS):
        eq, ek = _level_factors(cums, g, L)
        ql, kl, m = _mul(qf, eq), _mul(k, ek), m_ref[li]
        amat = [amat[ci] + _dot(_ck(ql, ci), _ck(kl, ci), NT) * m for ci in range(nc)]
    b = cums(("c", CHUNK))
    kst = k * jnp.exp(cums(("r", CHUNK)))
    o = [_dot(amat[ci], _ck(v, ci), NN) for ci in range(nc)]
    kv = [_dot(_ck(v, ci), _ck(kst, ci), TN) for ci in range(nc)]
    dec = [jnp.exp(b[(ci + 1) * CHUNK - 1:(ci + 1) * CHUNK, :]) for ci in range(nc)]
    return o, dec, kv, qf * jnp.exp(b)


def _hg_block_bwd(qf, k, v, g, do, ts, m_ref, nc):
    cums = _block_cums(ts, g, nc)
    dcs = {}
    da = [_dot(_ck(do, ci), _ck(v, ci), NT) for ci in range(nc)]
    dq = jnp.zeros_like(qf)
    dk = jnp.zeros_like(qf)
    dg = jnp.zeros_like(qf)
    amat = [jnp.zeros((CHUNK, CHUNK), F32)] * nc
    for li, L in enumerate(_LEVELS):
        eq, ek = _level_factors(cums, g, L)
        ql, kl, m = _mul(qf, eq), _mul(k, ek), m_ref[li]
        qlb, klb = ql.astype(BF16), kl.astype(BF16)
        amat = [amat[ci] + _dot(_ck(qlb, ci), _ck(klb, ci), NT) * m for ci in range(nc)]
        dal = [(da[ci] * m).astype(BF16) for ci in range(nc)]
        dql = jnp.concatenate([_dot(dal[ci], _ck(klb, ci), NN) for ci in range(nc)], axis=0)
        dkl = jnp.concatenate([_dot(dal[ci], _ck(qlb, ci), TN) for ci in range(nc)], axis=0)
        dq = dq + _mul(dql, eq)
        dk = dk + _mul(dkl, ek)
        if L == 1:
            dg = dg + dql * ql
        elif L > 1:
            dcs[("c", L)] = (dql * ql).astype(BF16)
            dcs[("r", L)] = (dkl * kl).astype(BF16)
    b = cums(("c", CHUNK))
    e64 = jnp.exp(b)
    er64 = jnp.exp(cums(("r", CHUNK)))
    qb = qf * e64
    return dict(dq=dq, dk=dk, dg=dg, dcs=dcs, e64=e64, er64=er64, qb=qb, kst=k * er64,
                dv=[_dot(amat[ci], _ck(do, ci), TN) for ci in range(nc)],
                dec=[jnp.exp(b[(ci + 1) * CHUNK - 1:(ci + 1) * CHUNK, :]) for ci in range(nc)],
                qd=[_dot(_ck(do, ci), _ck(qb, ci), TN) for ci in range(nc)])


def _hg_state_bwd(w, v, do, starts, ends, tst, nc):
    dqb = jnp.concatenate([_dot(_ck(do, ci), starts[ci], NN) for ci in range(nc)], axis=0)
    dkst = jnp.concatenate([_dot(_ck(v, ci), ends[ci], NN) for ci in range(nc)], axis=0)
    dq = w["dq"] + dqb * w["e64"]
    dk = w["dk"] + dkst * w["er64"]
    dv = jnp.concatenate([w["dv"][ci] + _dot(_ck(w["kst"], ci), ends[ci], NT) for ci in range(nc)], axis=0)
    trow = lax.broadcasted_iota(jnp.int32, (CHUNK, 1), 0)
    dtot = jnp.concatenate(
        [jnp.where(trow == CHUNK - 1, jnp.sum(ends[ci] * starts[ci], axis=0, keepdims=True) * w["dec"][ci], 0.0)
         for ci in range(nc)], axis=0)
    dcs = dict(w["dcs"])
    dcs[("c", CHUNK)] = (dqb * w["qb"] + dtot).astype(BF16)
    dcs[("r", CHUNK)] = (dkst * w["kst"]).astype(BF16)
    dgs = [_dot(tst, jnp.concatenate([_ck(dcs[kind], ci) for kind in _ALL_KINDS], axis=0), NN) for ci in range(nc)]
    return dq, dk, dv, w["dg"] + jnp.concatenate(dgs, axis=0)


def _hgrn_fwd(proj_h, u_rows, lb_param, g_head, layer, name, phase=None):
    B, S, _ = proj_h.shape
    sb = _pick(S, (512, 256, 128, 64))
    nc = sb // CHUNK
    ts, _ = _cum_matrices()

    def body(*refs):
        ins, outs, (st,), p_in, p_out, p_sems = _split_refs(refs, 8, 3, 1, phase)
        q_ref, f_ref, i_ref, z_ref, lbp_ref, gh_ref, ts_ref, m_ref = ins
        o_ref, u_ref, sts_ref = outs
        h_id, b_id, s_id = pl.program_id(0), pl.program_id(1), pl.program_id(2)
        _hosted_start(phase, p_in, p_out, p_sems, (h_id == 0) & (b_id == 0) & (s_id == 0))

        @pl.when(s_id == 0)
        def _():
            st[...] = jnp.zeros_like(st)

        lt = _lb_terms(lbp_ref[...], layer)
        tsv = ts_ref[...]
        gh = gh_ref[...]
        logf, k = _gate(f_ref[...], lt)[:2]
        o_part, dec, kv, qb = _hg_block_fwd(_silu(q_ref[...]), k, i_ref[...], logf, tsv, m_ref, nc)
        cur = st[...]
        starts = []
        for ci in range(nc):
            sts_ref[ci] = cur
            starts.append(cur)
            cur = cur * dec[ci] + kv[ci]
        st[...] = cur
        o = jnp.concatenate([o_part[ci] + _dot(_ck(qb, ci), starts[ci], NT) for ci in range(nc)], axis=0)
        o_ref[...] = o
        r = lax.rsqrt(jnp.mean(o * o, axis=-1, keepdims=True) + NORM_EPS)
        u_ref[...] = (((o * r) * gh) * _silu(z_ref[...])).astype(BF16)
        _hosted_finish(phase, p_in, p_out, p_sems, (h_id == HG_HEADS - 1) & (b_id == B - 1) & (s_id == S // sb - 1))

    col = lambda base: pl.BlockSpec((None, sb, HG_DIM), lambda h, b, s: (b, s, base + h))
    p_ispecs, p_ospecs, p_oshapes, p_alias, p_scratch, p_args = _host_phase(phase, 8, 3)
    res = pl.pallas_call(
        body, name=name,
        grid=(HG_HEADS, B, S // sb),
        in_specs=[col(0), col(HG_HEADS), col(2 * HG_HEADS), col(3 * HG_HEADS),
                  pl.BlockSpec((DEPTH, HG_DIM), lambda h, b, s: (0, h)),
                  pl.BlockSpec((1, HG_DIM), lambda h, b, s: (0, 0)),
                  pl.BlockSpec((N_CUM_F, CHUNK), lambda h, b, s: (0, 0)),
                  pl.BlockSpec((len(_LEVELS), CHUNK, CHUNK), lambda h, b, s: (0, 0, 0))] + p_ispecs,
        out_specs=[col(0), col(0),
                   pl.BlockSpec((None, None, nc, HG_DIM, HG_DIM), lambda h, b, s: (b, h, s, 0, 0))] + p_ospecs,
        out_shape=[jax.ShapeDtypeStruct((B, S, HG_WIDTH), F32),
                   jax.ShapeDtypeStruct((B, S, u_rows), BF16),
                   jax.ShapeDtypeStruct((B, HG_HEADS, S // CHUNK, HG_DIM, HG_DIM), F32)] + p_oshapes,
        input_output_aliases=p_alias,
        scratch_shapes=[pltpu.VMEM((HG_DIM, HG_DIM), F32)] + p_scratch,
        compiler_params=_params(("arbitrary", "arbitrary", "arbitrary")),
    )(proj_h, proj_h, proj_h, proj_h, lb_param, g_head, ts, _level_masks(), *p_args)
    return res[0], res[1], res[2], list(res[3:])


def _hgrn_bwd(proj_h, o_h, du, states, lb_param, g_head, layer, name, phase=None):
    B, S, _ = proj_h.shape
    sb = _pick(S, (512, 256, 128, 64))
    nc = sb // CHUNK
    ns = S // sb
    ts, tst = _cum_matrices()

    def body(*refs):
        ins, outs, (dst,), p_in, p_out, p_sems = _split_refs(refs, 12, 6, 1, phase)
        q_ref, f_ref, i_ref, z_ref, o_ref, du_ref, sts_ref, lbp_ref, gh_ref, ts_ref, tst_ref, m_ref = ins
        dq_ref, df_ref, di_ref, dz_ref, dlb_ref, dgh_ref = outs
        h_id, b_id, s_id = pl.program_id(0), pl.program_id(1), pl.program_id(2)
        _hosted_start(phase, p_in, p_out, p_sems, (h_id == 0) & (b_id == 0) & (s_id == 0))

        @pl.when(s_id == 0)
        def _():
            dst[...] = jnp.zeros_like(dst)

        @pl.when((b_id == 0) & (s_id == 0))
        def _():
            dlb_ref[...] = jnp.zeros_like(dlb_ref)

        @pl.when((h_id == 0) & (b_id == 0) & (s_id == 0))
        def _():
            dgh_ref[...] = jnp.zeros_like(dgh_ref)

        lt = _lb_terms(lbp_ref[...], layer)
        gh = gh_ref[...]
        tsv = ts_ref[...]
        tstv = tst_ref[...]
        logf, k, f, sig, nsig = _gate(f_ref[...], lt)
        o = o_ref[...]
        dub = du_ref[...]
        r = lax.rsqrt(jnp.mean(o * o, axis=-1, keepdims=True) + NORM_EPS)
        n = o * r
        sg, sg_grad = _silu_and_grad(z_ref[...])
        dz_ref[...] = (dub * (n * gh) * sg_grad).astype(BF16)
        dgh_ref[...] += jnp.sum(dub * sg * n, axis=0, keepdims=True)
        dn = dub * sg * gh
        do = r * (dn - n * jnp.mean(dn * n, axis=-1, keepdims=True))
        v = i_ref[...]
        qf, qf_grad = _silu_and_grad(q_ref[...])
        w = _hg_block_bwd(qf, k, v, logf, do, tsv, m_ref, nc)
        cur = dst[...]
        ends = [None] * nc
        for ci in reversed(range(nc)):
            ends[ci] = cur
            cur = cur * w["dec"][ci] + w["qd"][ci]
        dst[...] = cur
        dq, dk, dv, dg = _hg_state_bwd(w, v, do, [sts_ref[ci] for ci in range(nc)], ends, tstv, nc)
        di_ref[...] = dv.astype(BF16)
        dq_ref[...] = (dq * qf_grad).astype(BF16)
        scaled = (dg - f * dk) / f
        df_ref[...] = (scaled * lt["one_m"] * sig * nsig).astype(BF16)
        dlb_ref[...] += jnp.sum(scaled * (lt["ind"] - sig), axis=0, keepdims=True)
        _hosted_finish(phase, p_in, p_out, p_sems, (h_id == HG_HEADS - 1) & (b_id == B - 1) & (s_id == ns - 1))

    col = lambda base: pl.BlockSpec((None, sb, HG_DIM), lambda h, b, s: (b, ns - 1 - s, base + h))
    out_col = pl.BlockSpec((None, sb, HG_DIM), lambda h, b, s: (b, ns - 1 - s, h))
    dt = jax.ShapeDtypeStruct((B, S, HG_WIDTH), BF16)
    p_ispecs, p_ospecs, p_oshapes, p_alias, p_scratch, p_args = _host_phase(phase, 12, 6)
    res = pl.pallas_call(
        body, name=name,
        grid=(HG_HEADS, B, ns),
        in_specs=[col(0), col(HG_HEADS), col(2 * HG_HEADS), col(3 * HG_HEADS), col(0), col(0),
                  pl.BlockSpec((None, None, nc, HG_DIM, HG_DIM), lambda h, b, s: (b, h, ns - 1 - s, 0, 0)),
                  pl.BlockSpec((DEPTH, HG_DIM), lambda h, b, s: (0, h)),
                  pl.BlockSpec((1, HG_DIM), lambda h, b, s: (0, 0)),
                  pl.BlockSpec((N_CUM_F, CHUNK), lambda h, b, s: (0, 0)),
                  pl.BlockSpec((CHUNK, N_CUM), lambda h, b, s: (0, 0)),
                  pl.BlockSpec((len(_LEVELS), CHUNK, CHUNK), lambda h, b, s: (0, 0, 0))] + p_ispecs,
        out_specs=[out_col, out_col, out_col, out_col,
                   pl.BlockSpec((1, HG_DIM), lambda h, b, s: (0, h)),
                   pl.BlockSpec((1, HG_DIM), lambda h, b, s: (0, 0))] + p_ospecs,
        out_shape=[dt, dt, dt, dt, jax.ShapeDtypeStruct((1, HG_WIDTH), F32),
                   jax.ShapeDtypeStruct((1, HG_DIM), F32)] + p_oshapes,
        input_output_aliases=p_alias,
        scratch_shapes=[pltpu.VMEM((HG_DIM, HG_DIM), F32)] + p_scratch,
        compiler_params=_params(("arbitrary", "arbitrary", "arbitrary")),
    )(proj_h, proj_h, proj_h, proj_h, o_h, du, states, lb_param, g_head, ts, tst, _level_masks(), *p_args)
    return tuple(res[:6]) + (list(res[6:]),)


def _rope_tables(S):
    half = ATT_DIM // 2
    inv_freq = ROPE_THETA ** (-jnp.arange(half, dtype=F32) / half)
    ang = jnp.arange(S).astype(F32)[:, None] * inv_freq[None, :]
    cos = jnp.cos(ang)
    sin = jnp.sin(ang)
    cos = jnp.concatenate([cos, cos, cos, cos], axis=1)
    sin = jnp.concatenate([-sin, sin, -sin, sin], axis=1)
    return cos, sin


def _attn_common():
    lane = lax.broadcasted_iota(jnp.int32, (1, 2 * ATT_DIM), 1)
    first_half = (lane % ATT_DIM) < (ATT_DIM // 2)
    left = lane < ATT_DIM

    def swap(x):
        return jnp.where(first_half, pltpu.roll(x, 128 - ATT_DIM // 2, 1), pltpu.roll(x, ATT_DIM // 2, 1))

    def rope(x, cos, sin):
        return x * cos + swap(x) * sin

    def rope_bwd(dy, cos, sin):
        return dy * cos + swap(dy * sin)

    def dup(x):
        xs = pltpu.roll(x, ATT_DIM, 1)
        return [jnp.where(left, x, xs), jnp.where(left, xs, x)]

    return left, rope, rope_bwd, dup


GROUP = ATT_HEADS // 2
GROUP_ROWS = GROUP * ATT_BLOCK


def _attn_bias(i):
    r = lax.broadcasted_iota(jnp.int32, (ATT_BLOCK, 2 * ATT_BLOCK), 0)
    c = lax.broadcasted_iota(jnp.int32, (ATT_BLOCK, 2 * ATT_BLOCK), 1)
    ok = (c > r) & (c <= r + ATT_BLOCK) & ((c >= ATT_BLOCK) | (i > 0))
    return jnp.where(ok, 0.0, NEG_INF)


def _stack_heads(pairs, left):
    rows = []
    for x in pairs:
        rows += [jnp.where(left, x, 0.0), jnp.where(left, 0.0, x)]
    return jnp.concatenate(rows, axis=0)


def _unstack_heads(y, left, pp):
    r0 = 2 * pp * ATT_BLOCK
    return jnp.where(left, y[r0:r0 + ATT_BLOCK], y[r0 + ATT_BLOCK:r0 + 2 * ATT_BLOCK])


def _row_sums(x):
    return _dot(x, jnp.ones((x.shape[1], 128), BF16), NN)


def _attn_probs(qs, kd, vd, sink, bias):
    n = range(len(qs))
    s = [(_dot(qs[j], kd[j], NT).reshape(GROUP, ATT_BLOCK, 2 * ATT_BLOCK) * ATT_SCALE + bias[None])
         .reshape(GROUP_ROWS, 2 * ATT_BLOCK) for j in n]
    m = [jnp.max(jnp.maximum(jnp.maximum(s[j][:, :128], s[j][:, 128:]), sink[j]), axis=-1, keepdims=True) for j in n]
    pu = [jnp.exp(s[j] - m[j]) for j in n]
    es = [jnp.exp(sink[j] - m[j]) for j in n]
    ones = jnp.ones((2 * ATT_BLOCK, 128), BF16)
    ov = [_dot(pu[j], jnp.concatenate([vd[j].astype(BF16), ones], axis=1), NN) for j in n]
    inv = [1.0 / (ov[j][:, 128:] + es[j]) for j in n]
    return ([pu[j] * jnp.concatenate([inv[j], inv[j]], axis=1) for j in n], [es[j] * inv[j] for j in n],
            [ov[j][:, :128] * inv[j] for j in n])


def _sink_rows(sinks_l):
    return jnp.broadcast_to(jnp.repeat(sinks_l, ATT_BLOCK)[:, None], (ATT_HEADS * ATT_BLOCK, 128))


_Z0 = (2 * ATT_WIDTH + 2 * KV_WIDTH - ATT_WIDTH) // 256


def _attn_fwd(proj_a, u, sinks_l, cos, sin, name, phase=None):
    B, S, _ = proj_a.shape
    nb = S // ATT_BLOCK

    def body(*refs):
        ins, (u_ref,), _, p_in, p_out, p_sems = _split_refs(refs, 13, 1, 0, phase)
        q_ref, kvc_ref, kvp_ref, z0, z1, z2, z3, cos_ref, sin_ref, cosp_ref, sinp_ref, sinks_ref, _ = ins
        i = pl.program_id(1)
        _hosted_start(phase, p_in, p_out, p_sems, (pl.program_id(0) == 0) & (i == 0))
        left, rope, _, dup = _attn_common()
        cos_c, sin_c = cos_ref[...], sin_ref[...]
        kvc = kvc_ref[...]
        kvp = kvp_ref[...]
        kw = jnp.concatenate([rope(kvp[:, :KV_WIDTH], cosp_ref[...], sinp_ref[...]),
                              rope(kvc[:, :KV_WIDTH], cos_c, sin_c)], axis=0)
        vw = jnp.concatenate([kvp[:, KV_WIDTH:], kvc[:, KV_WIDTH:]], axis=0)
        kd, vd = dup(kw), dup(vw)
        bias = _attn_bias(i)
        zs = (z0, z1, z2, z3)
        pairs = [range(4 * kvh, 4 * kvh + 4) for kvh in range(2)]
        qs = [_stack_heads([rope(q_ref[:, 128 * pr:128 * (pr + 1)], cos_c, sin_c) for pr in pairs[kvh]], left)
              for kvh in range(2)]
        sink = [sinks_ref[kvh * GROUP_ROWS:(kvh + 1) * GROUP_ROWS, :] for kvh in range(2)]
        o = _attn_probs(qs, kd, vd, sink, bias)[2]
        for kvh in range(2):
            for pp, pr in enumerate(pairs[kvh]):
                z = zs[pr // 2][:, 128 * (pr % 2):128 * (pr % 2 + 1)]
                u_ref[:, 128 * pr:128 * (pr + 1)] = (_unstack_heads(o[kvh], left, pp) * _silu(z)).astype(BF16)
        _hosted_finish(phase, p_in, p_out, p_sems, (pl.program_id(0) == B - 1) & (i == nb - 1))

    rowblk = lambda w, cb: pl.BlockSpec((None, ATT_BLOCK, w), lambda b, i: (b, i, cb))
    tab = pl.BlockSpec((ATT_BLOCK, 128), lambda b, i: (i, 0))
    tabp = pl.BlockSpec((ATT_BLOCK, 128), lambda b, i: (jnp.maximum(i - 1, 0), 0))
    p_ispecs, p_ospecs, p_oshapes, p_alias, p_scratch, p_args = _host_phase(phase, 13, 1)
    res = pl.pallas_call(
        body, name=name,
        grid=(B, nb),
        in_specs=[rowblk(ATT_WIDTH, 0), rowblk(256, 4),
                  pl.BlockSpec((None, ATT_BLOCK, 256), lambda b, i: (b, jnp.maximum(i - 1, 0), 4)),
                  rowblk(256, _Z0), rowblk(256, _Z0 + 1), rowblk(256, _Z0 + 2), rowblk(256, _Z0 + 3),
                  tab, tab, tabp, tabp,
                  pl.BlockSpec((ATT_HEADS * ATT_BLOCK, 128), lambda b, i: (0, 0)),
                  pl.BlockSpec(memory_space=pl.ANY)] + p_ispecs,
        out_specs=[pl.BlockSpec((None, ATT_BLOCK, ATT_WIDTH), lambda b, i: (b, i, 1))] + p_ospecs,
        out_shape=[jax.ShapeDtypeStruct(u.shape, BF16)] + p_oshapes,
        input_output_aliases={12: 0, **p_alias},
        scratch_shapes=p_scratch,
        compiler_params=_params(("arbitrary", "arbitrary")),
    )(proj_a, proj_a, proj_a, proj_a, proj_a, proj_a, proj_a, cos, sin, cos, sin, sinks_l, u, *p_args)
    return res[0], list(res[1:])


def _attn_bwd(proj_a, du, sinks_l, cos, sin, name, phase=None):
    B, S, _ = proj_a.shape
    nb = S // ATT_BLOCK

    def body(*refs):
        ins, outs, (carry, sk_acc), p_in, p_out, p_sems = _split_refs(refs, 13, 4, 2, phase)
        q_ref, kvc_ref, kvp_ref, z0, z1, z2, z3, du_ref, cos_ref, sin_ref, cosp_ref, sinp_ref, sinks_ref = ins
        dq_ref, dkv_ref, dz_ref, dsk_ref = outs
        b_id, i = pl.program_id(0), pl.program_id(1)
        _hosted_start(phase, p_in, p_out, p_sems, (b_id == 0) & (i == 0))

        @pl.when((b_id == 0) & (i == 0))
        def _():
            sk_acc[...] = jnp.zeros_like(sk_acc)

        @pl.when(i == 0)
        def _():
            carry[...] = jnp.zeros_like(carry)

        @pl.when(i < nb)
        def _():
            left, rope, rope_bwd, dup = _attn_common()
            cos_c, sin_c = cos_ref[...], sin_ref[...]
            cos_p, sin_p = cosp_ref[...], sinp_ref[...]
            kvc = kvc_ref[...]
            kvp = kvp_ref[...]
            kw = jnp.concatenate([rope(kvp[:, :KV_WIDTH], cos_p, sin_p), rope(kvc[:, :KV_WIDTH], cos_c, sin_c)], axis=0)
            vw = jnp.concatenate([kvp[:, KV_WIDTH:], kvc[:, KV_WIDTH:]], axis=0)
            kd, vd = dup(kw), dup(vw)
            bias = _attn_bias(i)
            zs = (z0, z1, z2, z3)
            pairs = [range(4 * kvh, 4 * kvh + 4) for kvh in range(2)]
            kvs = range(2)
            qs = [_stack_heads([rope(q_ref[:, 128 * pr:128 * (pr + 1)], cos_c, sin_c) for pr in pairs[kvh]], left)
                  for kvh in kvs]
            sink = [sinks_ref[kvh * GROUP_ROWS:(kvh + 1) * GROUP_ROWS, :] for kvh in kvs]
            p, ps, o = _attn_probs(qs, kd, vd, sink, bias)
            dos = []
            for kvh in kvs:
                parts = []
                for pp, pr in enumerate(pairs[kvh]):
                    cols = slice(128 * pr, 128 * (pr + 1))
                    sg, sg_grad = _silu_and_grad(zs[pr // 2][:, 128 * (pr % 2):128 * (pr % 2 + 1)])
                    du128 = du_ref[:, cols]
                    dz_ref[:, cols] = (du128 * _unstack_heads(o[kvh], left, pp) * sg_grad).astype(BF16)
                    parts.append(du128 * sg)
                dos.append(_stack_heads(parts, left))
            dp = [_dot(dos[kvh], vd[kvh], NT) for kvh in kvs]
            delta = [_row_sums(p[kvh] * dp[kvh]) for kvh in kvs]
            ds = [p[kvh] * (dp[kvh] - jnp.concatenate([delta[kvh], delta[kvh]], axis=1)) * ATT_SCALE for kvh in kvs]
            dqs = [_dot(ds[kvh], kd[kvh], NN) for kvh in kvs]
            dkd = [_dot(ds[kvh], qs[kvh], TN) for kvh in kvs]
            dvd = [_dot(p[kvh], dos[kvh], TN) for kvh in kvs]
            for kvh in kvs:
                sk_acc[kvh] += -ps[kvh] * delta[kvh]
                for pp, pr in enumerate(pairs[kvh]):
                    dq_ref[:, 128 * pr:128 * (pr + 1)] = rope_bwd(_unstack_heads(dqs[kvh], left, pp),
                                                                  cos_c, sin_c).astype(BF16)
            fold = lambda pr: jnp.where(left, pr[0] + pltpu.roll(pr[0], ATT_DIM, 1), pr[1] + pltpu.roll(pr[1], ATT_DIM, 1))
            dkw = fold(dkd)
            dvw = fold(dvd)
            prev = jnp.concatenate([rope_bwd(dkw[:ATT_BLOCK], cos_p, sin_p), dvw[:ATT_BLOCK]], axis=1)
            cur = jnp.concatenate([rope_bwd(dkw[ATT_BLOCK:], cos_c, sin_c), dvw[ATT_BLOCK:]], axis=1)
            dkv_ref[...] = (carry[...] + prev).astype(BF16)
            carry[...] = cur

        @pl.when(i == nb)
        def _():
            dkv_ref[...] = carry[...].astype(BF16)

        @pl.when((b_id == B - 1) & (i == nb))
        def _():
            lane = lax.broadcasted_iota(jnp.int32, (1, 128), 1)
            tot = jnp.zeros((1, 128), F32)
            for hd in range(ATT_HEADS):
                rows = sk_acc[hd // GROUP, (hd % GROUP) * ATT_BLOCK:(hd % GROUP + 1) * ATT_BLOCK, :]
                tot = tot + jnp.where(lane == hd, jnp.sum(rows, axis=0, keepdims=True), 0.0)
            dsk_ref[...] = tot

        _hosted_finish(phase, p_in, p_out, p_sems, (b_id == B - 1) & (i == nb))

    cl = lambda i: jnp.minimum(i, nb - 1)
    pv = lambda i: jnp.maximum(jnp.minimum(i, nb - 1) - 1, 0)
    rowblk = lambda w, cb: pl.BlockSpec((None, ATT_BLOCK, w), lambda b, i: (b, cl(i), cb))
    tab = pl.BlockSpec((ATT_BLOCK, 128), lambda b, i: (cl(i), 0))
    tabp = pl.BlockSpec((ATT_BLOCK, 128), lambda b, i: (pv(i), 0))
    p_ispecs, p_ospecs, p_oshapes, p_alias, p_scratch, p_args = _host_phase(phase, 13, 4)
    res = pl.pallas_call(
        body, name=name,
        grid=(B, nb + 1),
        in_specs=[rowblk(ATT_WIDTH, 0), rowblk(256, 4),
                  pl.BlockSpec((None, ATT_BLOCK, 256), lambda b, i: (b, pv(i), 4)),
                  rowblk(256, _Z0), rowblk(256, _Z0 + 1), rowblk(256, _Z0 + 2), rowblk(256, _Z0 + 3),
                  rowblk(ATT_WIDTH, 1),
                  tab, tab, tabp, tabp,
                  pl.BlockSpec((ATT_HEADS * ATT_BLOCK, 128), lambda b, i: (0, 0))] + p_ispecs,
        out_specs=[rowblk(ATT_WIDTH, 0),
                   pl.BlockSpec((None, ATT_BLOCK, 256), lambda b, i: (b, jnp.maximum(i - 1, 0), 0)),
                   rowblk(ATT_WIDTH, 0),
                   pl.BlockSpec((1, 128), lambda b, i: (0, 0))] + p_ospecs,
        out_shape=[jax.ShapeDtypeStruct((B, S, ATT_WIDTH), BF16), jax.ShapeDtypeStruct((B, S, 256), BF16),
                   jax.ShapeDtypeStruct((B, S, ATT_WIDTH), BF16), jax.ShapeDtypeStruct((1, 128), F32)] + p_oshapes,
        input_output_aliases=p_alias,
        scratch_shapes=[pltpu.VMEM((ATT_BLOCK, 256), F32), pltpu.VMEM((2, GROUP_ROWS, 128), F32)] + p_scratch,
        compiler_params=_params(("arbitrary", "arbitrary")),
    )(proj_a, proj_a, proj_a, proj_a, proj_a, proj_a, proj_a, du, cos, sin, cos, sin, sinks_l, *p_args)
    return tuple(res[:4]) + (list(res[4:]),)


def _outproj_fwd(u2, w_out, x2, g_post, target2, name):
    T, D = x2.shape
    tm = _pick(T, (512, 256, 128))
    last = target2 is not None

    def body(u_ref, w_ref, x_ref, g_ref, *rest):
        y = lax.dot_general(u_ref[...], w_ref[...], (NN, ((), ())), preferred_element_type=F32)
        r = lax.rsqrt(jnp.mean(y * y, axis=-1, keepdims=True) + NORM_EPS)
        xn = x_ref[...] + (y * r) * g_ref[...]
        if last:
            t_ref, y_ref, dx_ref, loss_ref = rest
            err = xn - t_ref[...]
            dx_ref[...] = err * (1.0 / D)
            sq = err * err
            acc = sq[:, 0:128]
            for kk in range(1, D // 128):
                acc = acc + sq[:, 128 * kk:128 * (kk + 1)]
            part = jnp.sum(acc.reshape(tm // 8, 8, 128), axis=0) * (0.5 / D)

            @pl.when(pl.program_id(0) == 0)
            def _():
                loss_ref[...] = jnp.zeros_like(loss_ref)

            loss_ref[...] += part
        else:
            y_ref, xn_ref = rest
            xn_ref[...] = xn
        y_ref[...] = y

    row = pl.BlockSpec((tm, D), lambda i: (i, 0))
    in_specs = [pl.BlockSpec((tm, MIX_WIDTH), lambda i: (i, 0)),
                pl.BlockSpec((MIX_WIDTH, D), lambda i: (0, 0)), row,
                pl.BlockSpec((1, D), lambda i: (0, 0))]
    args = [u2, w_out, x2, g_post]
    out_specs = [row, row]
    out_shape = [jax.ShapeDtypeStruct((T, D), F32), jax.ShapeDtypeStruct((T, D), F32)]
    if last:
        in_specs.append(row)
        args.append(target2)
        out_specs.append(pl.BlockSpec((8, 128), lambda i: (0, 0)))
        out_shape.append(jax.ShapeDtypeStruct((8, 128), F32))
    return pl.pallas_call(
        body, name=name, grid=(T // tm,), in_specs=in_specs, out_specs=out_specs, out_shape=out_shape,
        compiler_params=_params(("arbitrary",)),
    )(*args)


def _outproj_bwd(dxn2, y2, g_post, w_out, name):
    T, D = y2.shape
    N = w_out.shape[0]
    tm = _pick(T, (512, 256, 128))
    nt = T // tm

    def body(dx_ref, y_ref, g_ref, w_ref, dy_ref, dg_ref, du_ref, acc):
        i = pl.program_id(0)

        @pl.when(i == 0)
        def _():
            acc[...] = jnp.zeros_like(acc)

        y = y_ref[...]
        dxn = dx_ref[...]
        r = lax.rsqrt(jnp.mean(y * y, axis=-1, keepdims=True) + NORM_EPS)
        n = y * r
        dn = dxn * g_ref[...]
        dy = (r * (dn - n * jnp.mean(dn * n, axis=-1, keepdims=True))).astype(BF16)
        dy_ref[...] = dy
        du_ref[...] = lax.dot_general(dy, w_ref[...], (NT, ((), ())), preferred_element_type=F32)
        acc[...] += jnp.sum((dxn * n).reshape(tm // 8, 8, D), axis=0)

        @pl.when(i == nt - 1)
        def _():
            dg_ref[...] = jnp.sum(acc[...], axis=0, keepdims=True)

    row = pl.BlockSpec((tm, D), lambda i: (i, 0))
    vec = pl.BlockSpec((1, D), lambda i: (0, 0))
    return pl.pallas_call(
        body, name=name, grid=(nt,),
        in_specs=[row, row, vec, pl.BlockSpec((N, D), lambda i: (0, 0), pipeline_mode=pl.Buffered(1))],
        out_specs=[row, vec, pl.BlockSpec((tm, N), lambda i: (i, 0))],
        out_shape=[jax.ShapeDtypeStruct((T, D), BF16), jax.ShapeDtypeStruct((1, D), F32),
                   jax.ShapeDtypeStruct((T, N), F32)],
        scratch_shapes=[pltpu.VMEM((8, D), F32)],
        compiler_params=_params(("arbitrary",)),
    )(dxn2, y2, g_post, w_out)


def _inproj_bwd(pieces, w_t, x2, dxn2, g_pre, name, phase=None):
    T, D = x2.shape
    widths = [p.shape[1] for p in pieces]
    offs = [sum(widths[:i]) for i in range(len(pieces))]
    n_p = len(pieces)
    tm = _pick(T, (256, 128))
    nt = T // tm

    def body(*refs):
        ins, (dx_ref, dg_ref), (acc,), p_in, p_out, p_sems = _split_refs(refs, n_p + 4, 2, 1, phase)
        w_ref, x_ref, dxn_ref, g_ref = ins[n_p:]
        i = pl.program_id(0)
        _hosted_start(phase, p_in, p_out, p_sems, i == 0)

        @pl.when(i == 0)
        def _():
            acc[...] = jnp.zeros_like(acc)

        dh = jnp.zeros((tm, D), F32)
        for p in range(n_p):
            dh = dh + lax.dot_general(ins[p][...], w_ref[offs[p]:offs[p] + widths[p], :], (NN, ((), ())),
                                      preferred_element_type=F32)
        x = x_ref[...]
        r = lax.rsqrt(jnp.mean(x * x, axis=-1, keepdims=True) + NORM_EPS)
        n = x * r
        dn = dh * g_ref[...]
        dx_ref[...] = dxn_ref[...] + r * (dn - n * jnp.mean(dn * n, axis=-1, keepdims=True))
        acc[...] += jnp.sum((dh * n).reshape(tm // 8, 8, D), axis=0)

        @pl.when(i == nt - 1)
        def _():
            dg_ref[...] = jnp.sum(acc[...], axis=0, keepdims=True)

        _hosted_finish(phase, p_in, p_out, p_sems, i == nt - 1)

    row = pl.BlockSpec((tm, D), lambda i: (i, 0))
    vec = pl.BlockSpec((1, D), lambda i: (0, 0))
    p_ispecs, p_ospecs, p_oshapes, p_alias, p_scratch, p_args = _host_phase(phase, n_p + 4, 2)
    res = pl.pallas_call(
        body, name=name, grid=(nt,),
        in_specs=[pl.BlockSpec((tm, w), lambda i: (i, 0)) for w in widths]
        + [pl.BlockSpec((sum(widths), D), lambda i: (0, 0), pipeline_mode=pl.Buffered(1)), row, row, vec] + p_ispecs,
        out_specs=[row, vec] + p_ospecs,
        out_shape=[jax.ShapeDtypeStruct((T, D), F32), jax.ShapeDtypeStruct((1, D), F32)] + p_oshapes,
        input_output_aliases=p_alias,
        scratch_shapes=[pltpu.VMEM((8, D), F32)] + p_scratch,
        compiler_params=_params(("arbitrary",)),
    )(*pieces, w_t, x2, dxn2, g_pre, *p_args)
    return res[0], res[1], list(res[2:])


def _step(x, target, g_pre, g_post, lb_param, g_head, sinks, shards=None, full=None):
    B, S, D = x.shape
    T = B * S
    dist = shards is not None
    if dist:
        a_loc, b_loc = shards
        ra, rb = a_loc.shape[1], b_loc.shape[1]
        side = _own_side_blocks()
        placed = lambda loc, nm: _place_own(loc, side, "place_" + nm)
        gather = lambda phase, nm: _run_phase(phase, nm)
        w_in0 = gather(_gather_ici_phase([a_loc[0]], [placed(a_loc[0], "in0")]), "gather_in0_ici")
        w_in0 = gather(_gather_d2d_phase(w_in0, [ra]), "gather_in0_d2d")[0]
        late_locs = [a_loc[1], b_loc[1], b_loc[0]]
        late_rs = [ra, rb, rb]
        late_full = [placed(a_loc[1], "in1"), placed(b_loc[1], "out1"), placed(b_loc[0], "out0")]
        w_in, w_out = [w_in0, None], [None, None]
    else:
        w_in, w_out = list(full[0]), list(full[1])
    cos, sin = _rope_tables(S)
    saved = []
    xs = x
    loss_part = None
    dxn = None
    for l in range(DEPTH):
        x2 = xs.reshape(T, D)
        host = dist and l == 0
        proj_h, proj_a, h = _inproj(x2, g_pre[l:l + 1], w_in[l], f"inproj{l}")
        proj_h = proj_h.reshape(B, S, N_H)
        proj_a = proj_a.reshape(B, S, N_A)
        o_h, u, states, got = _hgrn_fwd(proj_h, MIX_WIDTH, lb_param, g_head[l:l + 1], l, f"hgrn_fwd{l}",
                                        _gather_ici_phase(late_locs, late_full) if host else None)
        u, got = _attn_fwd(proj_a, u, _sink_rows(sinks[l]), cos, sin, f"attn_fwd{l}",
                           _gather_d2d_phase(got, late_rs) if host else None)
        if host:
            w_in[1], w_out[1], w_out[0] = got
        u2 = u.reshape(T, MIX_WIDTH)
        if l < DEPTH - 1:
            y, xn = _outproj_fwd(u2, w_out[l], x2, g_post[l:l + 1], None, f"outproj{l}")
            xn = xn.reshape(B, S, D)
        else:
            y, dxn, loss_part = _outproj_fwd(u2, w_out[l], x2, g_post[l:l + 1], target.reshape(T, D), f"outproj{l}")
            xn = None
        saved.append((x2, h, proj_h, proj_a, o_h, u2, states, y))
        xs = xn

    dw_in, dw_out = [None] * DEPTH, [None] * DEPTH
    dg_pre, dg_post, dlb, dg_head, dsinks = [], [], [], [], []
    for l in reversed(range(DEPTH)):
        x2, h, proj_h, proj_a, o_h, u2, states, y = saved[l]
        host = dist and l == 0
        dy, dgp, du = _outproj_bwd(dxn, y, g_post[l:l + 1], w_out[l], f"outproj_bwd{l}")
        du = du.reshape(B, S, MIX_WIDTH)
        dw_out[l] = _mm_tn([u2], dy, f"wgrad_out{l}")
        if host:
            early = [dw_in[1], dw_out[1], dw_out[0]]
        dqh, dfh, dih, dzh, dlb_l, dgh, got = _hgrn_bwd(
            proj_h, o_h, du, states, lb_param, g_head[l:l + 1], l, f"hgrn_bwd{l}",
            _reduce_d2d_phase(early, late_rs) if host else None)
        if host:
            parts = [_pair_sum(g, r, side, f"pair_sum{i}") for i, (g, r) in enumerate(zip(early, got))]
        dqa, dkv, dza, dsk, got = _attn_bwd(proj_a, du, _sink_rows(sinks[l]), cos, sin, f"attn_bwd{l}",
                                            _reduce_ici_phase(parts) if host else None)
        if host:
            dw_in[1], dw_out[1], dw_out[0] = [_chip_sum(p, r, f"chip_sum{i}")
                                              for i, (p, r) in enumerate(zip(parts, got))]
        dproj = [p.reshape(T, p.shape[-1]) for p in (dqh, dfh, dih, dzh, dqa, dkv, dza)]
        dw_in[l] = _mm_tn(dproj, h, f"wgrad_in{l}")
        if host:
            got = _run_phase(_reduce_d2d_phase([dw_in[0]], [ra]), "reduce_in0_d2d")
            part = _pair_sum(dw_in[0], got[0], side, "pair_sum_in0")
        dxn, dgpre, got = _inproj_bwd(dproj, w_in[l], x2, dxn, g_pre[l:l + 1], f"inproj_bwd{l}",
                                      _reduce_ici_phase([part]) if host else None)
        if host:
            dw_in[0] = _chip_sum(part, got[0], "chip_sum_in0")
        dg_pre.append(dgpre)
        dg_post.append(dgp)
        dlb.append(dlb_l)
        dg_head.append(dgh)
        dsinks.append(dsk)
    rev = lambda lst: jnp.concatenate(lst[::-1], axis=0)
    return (loss_part, dxn.reshape(B, S, D), jnp.stack(dw_in), jnp.stack(dw_out),
            rev(dg_pre), rev(dg_post), rev(dlb), rev(dg_head), rev(dsinks))


def _me_and_peers():
    x, y, c = lax.axis_index("x"), lax.axis_index("y"), lax.axis_index("c")
    me = 4 * x + 2 * y + c
    peers = []
    for k in range(1, N_DEV):
        px = 1 - x if k & 4 else x
        py = 1 - y if k & 2 else y
        pc = 1 - c if k & 1 else c
        peers.append(((px, py, pc), 4 * px + 2 * py + pc))
    return me, peers


class _Phase:
    def __init__(self, arrays, out_shapes, aliases, n_send, build):
        self.arrays, self.out_shapes, self.aliases = list(arrays), list(out_shapes), dict(aliases)
        self.n_send, self.build = n_send, build

    def scratch(self):
        return [pltpu.SemaphoreType.DMA((self.n_send,)), pltpu.SemaphoreType.DMA((self.n_send,))]

    def _copies(self, in_refs, out_refs, sems, arrivals):
        send_sems, recv_sems = sems
        sends, recvs = self.build(in_refs, out_refs)
        assert len(sends) == self.n_send == len(recvs)
        out = [pltpu.make_async_remote_copy(src_ref=s, dst_ref=d, send_sem=send_sems.at[i], recv_sem=recv_sems.at[i],
                                            device_id=dev, device_id_type=MESH) for i, (s, d, dev) in enumerate(sends)]
        inc = [pltpu.make_async_remote_copy(src_ref=s, dst_ref=r, send_sem=send_sems.at[i], recv_sem=recv_sems.at[i],
                                            device_id=dev, device_id_type=MESH)
               for i, ((s, _, dev), r) in enumerate(zip(sends, recvs))] if arrivals else []
        return out, inc

    def start(self, in_refs, out_refs, sems):
        out, _ = self._copies(in_refs, out_refs, sems, False)
        for cp in out:
            cp.start()

    def finish(self, in_refs, out_refs, sems):
        out, inc = self._copies(in_refs, out_refs, sems, True)
        for cp in inc:
            cp.wait_recv()
        for cp in out:
            cp.wait_send()


_ANY = pl.BlockSpec(memory_space=pl.ANY)


def _host_phase(phase, n_in, n_out):
    if phase is None:
        return [], [], [], {}, [], []
    aliases = {n_in + i: n_out + o for i, o in phase.aliases.items()}
    return ([_ANY] * len(phase.arrays), [_ANY] * len(phase.out_shapes), phase.out_shapes, aliases, phase.scratch(),
            phase.arrays)


def _split_refs(refs, n_in, n_out, n_scr, phase):
    pi = len(phase.arrays) if phase else 0
    po = len(phase.out_shapes) if phase else 0
    a = n_in + pi
    b = a + n_out + po
    return (refs[:n_in], refs[a:a + n_out], refs[b:b + n_scr], refs[n_in:a], refs[a + n_out:b], refs[b + n_scr:])


def _hosted_start(phase, p_in, p_out, p_sems, first):
    if phase is not None:
        @pl.when(first)
        def _():
            phase.start(p_in, p_out, p_sems)


def _hosted_finish(phase, p_in, p_out, p_sems, last):
    if phase is not None:
        @pl.when(last)
        def _():
            phase.finish(p_in, p_out, p_sems)


def _run_phase(phase, name):
    n_in, n_out = len(phase.arrays), len(phase.out_shapes)

    def body(*refs):
        phase.start(refs[:n_in], refs[n_in:n_in + n_out], refs[n_in + n_out:])
        phase.finish(refs[:n_in], refs[n_in:n_in + n_out], refs[n_in + n_out:])

    return pl.pallas_call(
        body, name=name, in_specs=[_ANY] * n_in, out_specs=[_ANY] * n_out,
        out_shape=phase.out_shapes, input_output_aliases=phase.aliases, scratch_shapes=phase.scratch(),
        compiler_params=pltpu.CompilerParams(has_side_effects=True),
    )(*phase.arrays)


def _mesh_place():
    x, y, c = lax.axis_index("x"), lax.axis_index("y"), lax.axis_index("c")
    chips = [(x, y), (1 - x, y), (x, 1 - y), (1 - x, 1 - y)]
    num = lambda chip, core: 4 * chip[0] + 2 * chip[1] + core
    return c, chips, num


def _own_side_blocks():
    c, chips, num = _mesh_place()
    return jnp.stack([num(ch, c) for ch in chips]).astype(jnp.int32)


def _rows(ref, r, dev):
    return ref.at[pl.ds(pl.multiple_of(dev * r, 16), r), :]


def _place_own(loc, blocks, name):
    r, D = loc.shape
    tr = _pick(r, (400, 256, 200, 128, 64, 16))

    def body(idx_ref, l_ref, o_ref):
        del idx_ref
        o_ref[...] = l_ref[...]

    return pl.pallas_call(
        body, name=name,
        grid_spec=pltpu.PrefetchScalarGridSpec(
            num_scalar_prefetch=1, grid=(r // tr,),
            in_specs=[pl.BlockSpec((tr, D), lambda i, idx: (i, 0))],
            out_specs=pl.BlockSpec((tr, D), lambda i, idx: (idx[0] * (r // tr) + i, 0))),
        out_shape=jax.ShapeDtypeStruct((N_DEV * r, D), loc.dtype),
        compiler_params=_params(("arbitrary",)),
    )(blocks, loc)


def _gather_ici_phase(locs, fulls):
    rs = [a.shape[0] for a in locs]
    n = len(locs)

    def build(ins, outs):
        c, chips, num = _mesh_place()
        me = num(chips[0], c)
        targets = [((*chips[0], 1 - c), num(chips[0], 1 - c))] + [((*ch, c), num(ch, c)) for ch in chips[1:]]
        sends, recvs = [], []
        for dev, dnum in targets:
            for i, r in enumerate(rs):
                sends.append((ins[i], _rows(outs[i], r, me), dev))
                recvs.append(_rows(outs[i], r, dnum))
        return sends, recvs

    shapes = [jax.ShapeDtypeStruct(a.shape, a.dtype) for a in fulls]
    return _Phase(list(locs) + list(fulls), shapes, {n + i: i for i in range(n)}, 4 * n, build)


def _gather_d2d_phase(fulls, rs):
    def build(ins, outs):
        c, chips, num = _mesh_place()
        sib = (*chips[0], 1 - c)
        sends, recvs = [], []
        for ch in chips[1:]:
            for i, r in enumerate(rs):
                blk = _rows(outs[i], r, num(ch, c))
                sends.append((blk, blk, sib))
                recvs.append(_rows(outs[i], r, num(ch, 1 - c)))
        return sends, recvs

    shapes = [jax.ShapeDtypeStruct(a.shape, a.dtype) for a in fulls]
    return _Phase(fulls, shapes, {i: i for i in range(len(fulls))}, 3 * len(fulls), build)


def _reduce_d2d_phase(grads, rs):
    def build(ins, outs):
        c, chips, num = _mesh_place()
        sib = (*chips[0], 1 - c)
        sends, recvs = [], []
        for j, ch in enumerate(chips):
            for i, r in enumerate(rs):
                sends.append((_rows(ins[i], r, num(ch, 1 - c)), outs[i].at[j], sib))
                recvs.append(outs[i].at[j])
        return sends, recvs

    shapes = [jax.ShapeDtypeStruct((4, r, g.shape[1]), g.dtype) for g, r in zip(grads, rs)]
    return _Phase(grads, shapes, {}, 4 * len(grads), build)


def _reduce_ici_phase(parts):
    def build(ins, outs):
        c, chips, _ = _mesh_place()
        sends, recvs = [], []
        for t in range(1, 4):
            for i in range(len(parts)):
                sends.append((ins[i].at[t], outs[i].at[t - 1], (*chips[t], c)))
                recvs.append(outs[i].at[t - 1])
        return sends, recvs

    shapes = [jax.ShapeDtypeStruct((3,) + p.shape[1:], p.dtype) for p in parts]
    return _Phase(parts, shapes, {}, 3 * len(parts), build)


def _pair_sum(g, got, blocks, name):
    n, r, D = got.shape
    tr = _pick(r, (400, 256, 200, 128, 64, 16))

    def body(idx_ref, g_ref, r_ref, o_ref):
        del idx_ref
        o_ref[...] = (g_ref[...].astype(F32) + r_ref[...].astype(F32)).astype(o_ref.dtype)

    blk = pl.BlockSpec((None, tr, D), lambda j, i, idx: (j, i, 0))
    return pl.pallas_call(
        body, name=name,
        grid_spec=pltpu.PrefetchScalarGridSpec(
            num_scalar_prefetch=1, grid=(n, r // tr),
            in_specs=[pl.BlockSpec((tr, D), lambda j, i, idx: (idx[j] * (r // tr) + i, 0)), blk],
            out_specs=blk),
        out_shape=jax.ShapeDtypeStruct(got.shape, got.dtype),
        compiler_params=_params(("arbitrary", "arbitrary")),
    )(blocks, g, got)


def _chip_sum(p, r, name):
    _, R, D = p.shape
    tr = _pick(R, (400, 256, 200, 128, 64, 16))

    def body(p_ref, r_ref, o_ref):
        acc = p_ref[...].astype(F32)
        for t in range(3):
            acc = acc + r_ref[t].astype(F32)
        o_ref[...] = acc

    return pl.pallas_call(
        body, name=name, grid=(R // tr,),
        in_specs=[pl.BlockSpec((None, tr, D), lambda i: (0, i, 0)), pl.BlockSpec((3, tr, D), lambda i: (0, i, 0))],
        out_specs=pl.BlockSpec((tr, D), lambda i: (i, 0)), out_shape=jax.ShapeDtypeStruct((R, D), F32),
        compiler_params=_params(("parallel",)))(p, r)


def _allreduce_small(vec):
    R, C = vec.shape

    def body(v_ref, o_ref, buf, send_sems, recv_sems):
        me, peers = _me_and_peers()
        buf[me] = v_ref[...]
        sends = []
        for k, (pid, _) in enumerate(peers):
            cp = pltpu.make_async_remote_copy(src_ref=v_ref, dst_ref=buf.at[me], send_sem=send_sems.at[k],
                                              recv_sem=recv_sems.at[k], device_id=pid, device_id_type=MESH)
            cp.start()
            sends.append(cp)
        for k, (pid, pnum) in enumerate(peers):
            pltpu.make_async_remote_copy(src_ref=v_ref, dst_ref=buf.at[pnum], send_sem=send_sems.at[k],
                                         recv_sem=recv_sems.at[k], device_id=pid, device_id_type=MESH).wait_recv()
        for cp in sends:
            cp.wait_send()
        acc = buf[0]
        for d in range(1, N_DEV):
            acc = acc + buf[d]
        o_ref[...] = acc

    vm = pl.BlockSpec(memory_space=pltpu.VMEM)
    return pl.pallas_call(
        body, name="allreduce_small",
        in_specs=[vm], out_specs=vm,
        out_shape=jax.ShapeDtypeStruct((R, C), F32),
        scratch_shapes=[pltpu.VMEM((N_DEV, R, C), F32), pltpu.SemaphoreType.DMA((N_DEV - 1,)),
                        pltpu.SemaphoreType.DMA((N_DEV - 1,))],
        compiler_params=pltpu.CompilerParams(has_side_effects=True),
    )(vec)


def _adamw(w, g, m, v, name):
    R, C = w.shape
    tr = _pick(R, (256, 128, 64, 32, 16, 8)) if R >= 8 else R
    c1 = 1.0 - ADAM_B1 ** ADAM_STEP
    c2 = 1.0 - ADAM_B2 ** ADAM_STEP

    def body(w_ref, g_ref, m_ref, v_ref, d_ref, mo_ref, vo_ref):
        gg = g_ref[...]
        mn = ADAM_B1 * m_ref[...] + (1.0 - ADAM_B1) * gg
        vn = ADAM_B2 * v_ref[...] + (1.0 - ADAM_B2) * (gg * gg)
        d_ref[...] = -ADAM_LR * ((mn / c1) / (jnp.sqrt(vn / c2) + ADAM_EPS) + ADAM_WD * w_ref[...])
        mo_ref[...] = mn
        vo_ref[...] = vn

    blk = pl.BlockSpec((tr, C), lambda i: (i, 0))
    sh = jax.ShapeDtypeStruct((R, C), F32)
    return pl.pallas_call(
        body, name=name, grid=(R // tr,), in_specs=[blk] * 4, out_specs=[blk] * 3, out_shape=[sh] * 3,
        compiler_params=_params(("parallel",)),
    )(w, g, m, v)


def _lb_param_grad(lb_param, dlb):
    L, C = lb_param.shape

    def body(p_ref, d_ref, o_ref):
        lbp = p_ref[...]
        d = d_ref[...]
        mx = jnp.max(lbp, axis=0, keepdims=True)
        e = jnp.exp(lbp - mx)
        p = e / jnp.sum(e, axis=0, keepdims=True)
        tot = jnp.sum(d, axis=0, keepdims=True)
        dps = []
        rest = tot
        for j in range(L):
            dps.append(rest - tot if j == 0 else rest)
            rest = rest - d[j:j + 1]
        dp = jnp.concatenate(dps, axis=0)
        o_ref[...] = p * (dp - jnp.sum(p * dp, axis=0, keepdims=True))

    vm = pl.BlockSpec(memory_space=pltpu.VMEM)
    return pl.pallas_call(body, name="lb_param_grad", in_specs=[vm, vm], out_specs=vm,
                          out_shape=jax.ShapeDtypeStruct((L, C), F32))(lb_param, dlb)


def _pack_small(loss_part, dg_pre, dg_post, dlb, dg_head, dsinks):
    pad8 = lambda a: jnp.pad(a.reshape(-1, 128), ((0, 8 - DEPTH), (0, 0)))
    rows = [dg_pre.reshape(-1, 128), dg_post.reshape(-1, 128), dlb.reshape(-1, 128), pad8(dg_head), pad8(dsinks),
            loss_part]
    return jnp.concatenate(rows, axis=0)


def _unpack_small(vec):
    n = DEPTH * D_MODEL // 128
    o = 0
    dg_pre = vec[o:o + n].reshape(DEPTH, D_MODEL); o += n
    dg_post = vec[o:o + n].reshape(DEPTH, D_MODEL); o += n
    dlb = vec[o:o + n].reshape(DEPTH, HG_WIDTH); o += n
    dg_head = vec[o:o + DEPTH]; o += 8
    dsinks = vec[o:o + DEPTH, :ATT_HEADS]; o += 8
    loss = jnp.sum(vec[o:o + 8])
    return loss, dg_pre, dg_post, dlb, dg_head, dsinks


def kernel(x, w_in, w_out, g_pre, g_post, lb_param, g_head, sinks, loss_target, m_w_in, m_w_out, m_g_pre, m_g_post, m_lb_param, m_g_head, m_sinks, v_w_in, v_w_out, v_g_pre, v_g_post, v_lb_param, v_g_head, v_sinks):
    tr = lambda a: jnp.swapaxes(a, 1, 2)
    w_in_t = tr(w_in)
    (loss_part, dx, gw_in_t, gw_out, dg_pre, dg_post, dlb, dg_head, dsinks) = _step(
        x, loss_target, g_pre, g_post, lb_param, g_head, sinks, shards=(w_in_t.astype(BF16), w_out.astype(BF16)))

    small = _allreduce_small(_pack_small(loss_part, dg_pre, dg_post, dlb, dg_head, dsinks))
    loss, gg_pre, gg_post, gdlb, gg_head, gsinks = _unpack_small(small)
    glb = _lb_param_grad(lb_param, gdlb)

    grads = [gw_in_t, gw_out, gg_pre, gg_post, glb, gg_head, gsinks]
    ws = [w_in_t, w_out, g_pre, g_post, lb_param, g_head, sinks]
    ms = [tr(m_w_in), m_w_out, m_g_pre, m_g_post, m_lb_param, m_g_head, m_sinks]
    vs = [tr(v_w_in), v_w_out, v_g_pre, v_g_post, v_lb_param, v_g_head, v_sinks]
    names = ["w_in", "w_out", "g_pre", "g_post", "lb_param", "g_head", "sinks"]
    deltas, new_m, new_v = [], [], []
    for w, g, m, v, nm in zip(ws, grads, ms, vs, names):
        sh = w.shape
        two = lambda a: a.reshape(-1, sh[-1])
        d, mn, vn = _adamw(two(w), two(g), two(m), two(v), "adamw_" + nm)
        deltas.append(d.reshape(sh))
        new_m.append(mn.reshape(sh))
        new_v.append(vn.reshape(sh))
    grads[0], deltas[0], new_m[0], new_v[0] = tr(grads[0]), tr(deltas[0]), tr(new_m[0]), tr(new_v[0])
    return (loss, dx, *grads, *deltas, *new_m, *new_v)
```

```python
import functools
import math

import numpy as np
import jax
import jax.numpy as jnp
from jax import lax
from jax.experimental import pallas as pl
from jax.experimental.pallas import tpu as pltpu

F32 = jnp.float32
BF16 = jnp.bfloat16

D_MODEL = 1024
DEPTH = 2
HG_HEADS = 8
HG_DIM = 128
HG_WIDTH = HG_HEADS * HG_DIM
CHUNK = 64
ATT_HEADS = 16
ATT_DIM = 64
ATT_WIDTH = ATT_HEADS * ATT_DIM
KV_WIDTH = 128
ATT_BLOCK = 128
ATT_SCALE = 1.0 / math.sqrt(ATT_DIM)
ROPE_THETA = 10000.0
NORM_EPS = 1e-6
NEG_INF = -1e30
LB_FLOOR = 1e-20
N_H = 4 * HG_WIDTH
N_A = 2 * ATT_WIDTH + 2 * KV_WIDTH
IN_WIDTH = N_H + N_A
MIX_WIDTH = HG_WIDTH + ATT_WIDTH

ADAM_LR = 0.001
ADAM_B1 = 0.9
ADAM_B2 = 0.999
ADAM_EPS = 1e-08
ADAM_WD = 0.01
ADAM_STEP = 10

N_DEV = 8
MESH = pl.DeviceIdType.MESH
VMEM_LIMIT = 56 * 1024 * 1024

NN = ((1,), (0,))
NT = ((1,), (1,))
TN = ((0,), (0,))


def _dot(a, b, dims):
    return lax.dot_general(a.astype(BF16), b.astype(BF16), (dims, ((), ())), preferred_element_type=F32)


def _params(sem=None, **kw):
    return pltpu.CompilerParams(dimension_semantics=sem, vmem_limit_bytes=VMEM_LIMIT, **kw)


def _sigmoids(x):
    e = jnp.exp(-jnp.abs(x))
    r = 1.0 / (1.0 + e)
    er = e * r
    pos = x >= 0.0
    return jnp.where(pos, r, er), jnp.where(pos, er, r)


def _silu(x):
    return x * _sigmoids(x)[0]


def _silu_and_grad(x):
    s, ns = _sigmoids(x)
    return x * s, s * (1.0 + x * ns)


def _pick(n, prefs):
    for p in prefs:
        if n % p == 0:
            return p
    return n


def _inproj(x2, g, w, name):
    T, D = x2.shape
    tm = _pick(T, (256, 128))
    nchunk = 1024

    def body(x_ref, g_ref, w_ref, oh_ref, oa_ref, h_ref):
        x = x_ref[...]
        r = lax.rsqrt(jnp.mean(x * x, axis=-1, keepdims=True) + NORM_EPS)
        h = ((x * r) * g_ref[...]).astype(BF16)
        h_ref[...] = h
        for j in range(0, N_H, nchunk):
            oh_ref[:, j:j + nchunk] = lax.dot_general(h, w_ref[j:j + nchunk, :], (NT, ((), ())),
                                                      preferred_element_type=F32)
        for j in range(0, N_A, N_A // 2):
            oa_ref[:, j:j + N_A // 2] = lax.dot_general(h, w_ref[N_H + j:N_H + j + N_A // 2, :], (NT, ((), ())),
                                                        preferred_element_type=F32)

    row = lambda w_: pl.BlockSpec((tm, w_), lambda i: (i, 0))
    return pl.pallas_call(
        body, name=name,
        grid=(T // tm,),
        in_specs=[row(D), pl.BlockSpec((1, D), lambda i: (0, 0)),
                  pl.BlockSpec((IN_WIDTH, D), lambda i: (0, 0), pipeline_mode=pl.Buffered(1))],
        out_specs=[row(N_H), row(N_A), row(D)],
        out_shape=[jax.ShapeDtypeStruct((T, N_H), F32), jax.ShapeDtypeStruct((T, N_A), F32),
                   jax.ShapeDtypeStruct((T, D), BF16)],
        compiler_params=_params(("parallel",)),
    )(x2, g, w)


def _mm_tn(pieces, b, name, out_dtype=BF16):
    T, m = b.shape
    tn = 256
    counts = [p.shape[1] // tn for p in pieces]
    starts = [sum(counts[:i]) for i in range(len(pieces))]
    n_p = len(pieces)

    def body(*refs):
        b_ref, o_ref = refs[n_p], refs[n_p + 1]
        i = pl.program_id(0)
        for p in range(n_p):
            @pl.when((i >= starts[p]) & (i < starts[p] + counts[p]))
            def _(p=p):
                o_ref[...] = lax.dot_general(refs[p][...], b_ref[...], (TN, ((), ())),
                                             preferred_element_type=F32).astype(out_dtype)

    piece_spec = lambda s, c: pl.BlockSpec((T, tn), lambda i: (0, jnp.clip(i - s, 0, c - 1)))
    return pl.pallas_call(
        body, name=name,
        grid=(sum(counts),),
        in_specs=[piece_spec(s, c) for s, c in zip(starts, counts)]
        + [pl.BlockSpec((T, m), lambda i: (0, 0), pipeline_mode=pl.Buffered(1))],
        out_specs=pl.BlockSpec((tn, m), lambda i: (i, 0)),
        out_shape=jax.ShapeDtypeStruct((sum(counts) * tn, m), out_dtype),
        compiler_params=_params(("arbitrary",)),
    )(*pieces, b)


_LEVELS = (0, 1, 2, 4, 8, 16, 32)
_CUM_L = (2, 4, 8, 16, 32, 64)
_ALL_KINDS = tuple(("c", L) for L in _CUM_L) + tuple(("r", L) for L in _CUM_L)
_MXU_KINDS = (("c", 2), ("c", 4), ("c", CHUNK), ("r", 2), ("r", 4))
N_CUM = len(_ALL_KINDS) * CHUNK
N_CUM_F = len(_MXU_KINDS) * CHUNK


def _cum_matrices():
    t = np.arange(CHUNK)[:, None]
    r = np.arange(CHUNK)[None, :]

    def mat(kind):
        c, L = kind
        return ((r // L == t // L) & ((r <= t) if c == "c" else (r > t))).astype(np.float32)

    fwd = np.concatenate([mat(kd) for kd in _MXU_KINDS], axis=0)
    full = np.concatenate([mat(kd) for kd in _ALL_KINDS], axis=0)
    return jnp.asarray(fwd, BF16), jnp.asarray(full.T.copy(), BF16)


def _level_masks():
    t = np.arange(CHUNK)[:, None]
    s = np.arange(CHUNK)[None, :]
    ms = []
    for L in _LEVELS:
        if L == 0:
            ms.append(t == s)
        else:
            ms.append((t // (2 * L) == s // (2 * L)) & ((t // L) % 2 == 1) & ((s // L) % 2 == 0))
    return jnp.asarray(np.stack(ms).astype(np.float32))


def _split3(x):
    hi = x.astype(BF16)
    r1 = x - hi.astype(F32)
    mid = r1.astype(BF16)
    lo = (r1 - mid.astype(F32)).astype(BF16)
    return hi, mid, lo


def _cum3(ts, x, terms=3):
    d = lambda p: lax.dot_general(ts, p, (NN, ((), ())), preferred_element_type=F32)
    return sum(d(p) for p in _split3(x)[:terms])


def _lb_terms(lbp, layer):
    mx = jnp.max(lbp, axis=0, keepdims=True)
    e = jnp.exp(lbp - mx)
    p = e / jnp.sum(e, axis=0, keepdims=True)
    cum = p[0:1]
    for j in range(1, layer + 1):
        cum = cum + p[j:j + 1]
    lb = cum - p[0:1]
    lbf = jnp.maximum(lb, LB_FLOOR)
    return dict(lbf=lbf, one_m=1.0 - lb, kcorr=lb - lbf, ind=jnp.where(lb > LB_FLOOR, 1.0, 0.0))


def _gate(x, lt):
    sig, nsig = _sigmoids(x)
    f = lt["lbf"] + lt["one_m"] * sig
    return jnp.log(f), lt["one_m"] * nsig + lt["kcorr"], f, sig, nsig


def _ck(x, ci):
    return x[ci * CHUNK:(ci + 1) * CHUNK]


def _block_cums(ts, g, nc):
    cs = [_cum3(ts, _ck(g, ci)) for ci in range(nc)]
    out = {kind: jnp.concatenate([c[CHUNK * i:CHUNK * (i + 1)] for c in cs], axis=0)
           for i, kind in enumerate(_MXU_KINDS)}
    b = out[("c", CHUNK)]
    ng = CHUNK // 8
    last = b.reshape(nc, ng, 8, HG_DIM)[:, :, 7:8, :]
    zero = jnp.zeros((nc, 1, 1, HG_DIM), F32)

    def spread(groups):
        return jnp.broadcast_to(jnp.concatenate(groups, axis=1), (nc, ng, 8, HG_DIM)).reshape(nc * CHUNK, HG_DIM)

    def get(kind):
        if kind in out:
            return out[kind]
        c, L = kind
        nb = L // 8
        first = lambda r: (r // nb) * nb
        if c == "c":
            return b - spread([last[:, first(r) - 1:first(r)] if r >= nb else zero for r in range(ng)])
        return spread([last[:, first(r) + nb - 1:first(r) + nb] for r in range(ng)]) - b

    return get


def _level_factors(cums, g, L):
    if L == 0:
        return None, None
    if L == 1:
        return jnp.exp(g), None
    return jnp.exp(cums(("c", L))), jnp.exp(cums(("r", L)))


def _mul(a, e):
    return a if e is None else a * e


def _hg_block_fwd(qf, k, v, g, ts, m_ref, nc):
    cums = _block_cums(ts, g, nc)
    amat = [jnp.zeros((CHUNK, CHUNK), F32)] * nc
    for li, L in enumerate(_LEVELS):
        eq, ek = _level_factors(cums, g, L)
        ql, kl, m = _mul(qf, eq), _mul(k, ek), m_ref[li]
        amat = [amat[ci] + _dot(_ck(ql, ci), _ck(kl, ci), NT) * m for ci in range(nc)]
    b = cums(("c", CHUNK))
    kst = k * jnp.exp(cums(("r", CHUNK)))
    o = [_dot(amat[ci], _ck(v, ci), NN) for ci in range(nc)]
    kv = [_dot(_ck(v, ci), _ck(kst, ci), TN) for ci in range(nc)]
    dec = [jnp.exp(b[(ci + 1) * CHUNK - 1:(ci + 1) * CHUNK, :]) for ci in range(nc)]
    return o, dec, kv, qf * jnp.exp(b)


def _hg_block_bwd(qf, k, v, g, do, ts, m_ref, nc):
    cums = _block_cums(ts, g, nc)
    dcs = {}
    da = [_dot(_ck(do, ci), _ck(v, ci), NT) for ci in range(nc)]
    dq = jnp.zeros_like(qf)
    dk = jnp.zeros_like(qf)
    dg = jnp.zeros_like(qf)
    amat = [jnp.zeros((CHUNK, CHUNK), F32)] * nc
    for li, L in enumerate(_LEVELS):
        eq, ek = _level_factors(cums, g, L)
        ql, kl, m = _mul(qf, eq), _mul(k, ek), m_ref[li]
        qlb, klb = ql.astype(BF16), kl.astype(BF16)
        amat = [amat[ci] + _dot(_ck(qlb, ci), _ck(klb, ci), NT) * m for ci in range(nc)]
        dal = [(da[ci] * m).astype(BF16) for ci in range(nc)]
        dql = jnp.concatenate([_dot(dal[ci], _ck(klb, ci), NN) for ci in range(nc)], axis=0)
        dkl = jnp.concatenate([_dot(dal[ci], _ck(qlb, ci), TN) for ci in range(nc)], axis=0)
        dq = dq + _mul(dql, eq)
        dk = dk + _mul(dkl, ek)
        if L == 1:
            dg = dg + dql * ql
        elif L > 1:
            dcs[("c", L)] = (dql * ql).astype(BF16)
            dcs[("r", L)] = (dkl * kl).astype(BF16)
    b = cums(("c", CHUNK))
    e64 = jnp.exp(b)
    er64 = jnp.exp(cums(("r", CHUNK)))
    qb = qf * e64
    return dict(dq=dq, dk=dk, dg=dg, dcs=dcs, e64=e64, er64=er64, qb=qb, kst=k * er64,
                dv=[_dot(amat[ci], _ck(do, ci), TN) for ci in range(nc)],
                dec=[jnp.exp(b[(ci + 1) * CHUNK - 1:(ci + 1) * CHUNK, :]) for ci in range(nc)],
                qd=[_dot(_ck(do, ci), _ck(qb, ci), TN) for ci in range(nc)])


def _hg_state_bwd(w, v, do, starts, ends, tst, nc):
    dqb = jnp.concatenate([_dot(_ck(do, ci), starts[ci], NN) for ci in range(nc)], axis=0)
    dkst = jnp.concatenate([_dot(_ck(v, ci), ends[ci], NN) for ci in range(nc)], axis=0)
    dq = w["dq"] + dqb * w["e64"]
    dk = w["dk"] + dkst * w["er64"]
    dv = jnp.concatenate([w["dv"][ci] + _dot(_ck(w["kst"], ci), ends[ci], NT) for ci in range(nc)], axis=0)
    trow = lax.broadcasted_iota(jnp.int32, (CHUNK, 1), 0)
    dtot = jnp.concatenate(
        [jnp.where(trow == CHUNK - 1, jnp.sum(ends[ci] * starts[ci], axis=0, keepdims=True) * w["dec"][ci], 0.0)
         for ci in range(nc)], axis=0)
    dcs = dict(w["dcs"])
    dcs[("c", CHUNK)] = (dqb * w["qb"] + dtot).astype(BF16)
    dcs[("r", CHUNK)] = (dkst * w["kst"]).astype(BF16)
    dgs = [_dot(tst, jnp.concatenate([_ck(dcs[kind], ci) for kind in _ALL_KINDS], axis=0), NN) for ci in range(nc)]
    return dq, dk, dv, w["dg"] + jnp.concatenate(dgs, axis=0)


def _hgrn_fwd(proj_h, u_rows, lb_param, g_head, layer, name, phase=None):
    B, S, _ = proj_h.shape
    sb = _pick(S, (512, 256, 128, 64))
    nc = sb // CHUNK
    ts, _ = _cum_matrices()

    def body(*refs):
        ins, outs, (st,), p_in, p_out, p_sems = _split_refs(refs, 8, 3, 1, phase)
        q_ref, f_ref, i_ref, z_ref, lbp_ref, gh_ref, ts_ref, m_ref = ins
        o_ref, u_ref, sts_ref = outs
        h_id, b_id, s_id = pl.program_id(0), pl.program_id(1), pl.program_id(2)
        _hosted_start(phase, p_in, p_out, p_sems, (h_id == 0) & (b_id == 0) & (s_id == 0))

        @pl.when(s_id == 0)
        def _():
            st[...] = jnp.zeros_like(st)

        lt = _lb_terms(lbp_ref[...], layer)
        tsv = ts_ref[...]
        gh = gh_ref[...]
        logf, k = _gate(f_ref[...], lt)[:2]
        o_part, dec, kv, qb = _hg_block_fwd(_silu(q_ref[...]), k, i_ref[...], logf, tsv, m_ref, nc)
        cur = st[...]
        starts = []
        for ci in range(nc):
            sts_ref[ci] = cur
            starts.append(cur)
            cur = cur * dec[ci] + kv[ci]
        st[...] = cur
        o = jnp.concatenate([o_part[ci] + _dot(_ck(qb, ci), starts[ci], NT) for ci in range(nc)], axis=0)
        o_ref[...] = o
        r = lax.rsqrt(jnp.mean(o * o, axis=-1, keepdims=True) + NORM_EPS)
        u_ref[...] = (((o * r) * gh) * _silu(z_ref[...])).astype(BF16)
        _hosted_finish(phase, p_in, p_out, p_sems, (h_id == HG_HEADS - 1) & (b_id == B - 1) & (s_id == S // sb - 1))

    col = lambda base: pl.BlockSpec((None, sb, HG_DIM), lambda h, b, s: (b, s, base + h))
    p_ispecs, p_ospecs, p_oshapes, p_alias, p_scratch, p_args = _host_phase(phase, 8, 3)
    res = pl.pallas_call(
        body, name=name,
        grid=(HG_HEADS, B, S // sb),
        in_specs=[col(0), col(HG_HEADS), col(2 * HG_HEADS), col(3 * HG_HEADS),
                  pl.BlockSpec((DEPTH, HG_DIM), lambda h, b, s: (0, h)),
                  pl.BlockSpec((1, HG_DIM), lambda h, b, s: (0, 0)),
                  pl.BlockSpec((N_CUM_F, CHUNK), lambda h, b, s: (0, 0)),
                  pl.BlockSpec((len(_LEVELS), CHUNK, CHUNK), lambda h, b, s: (0, 0, 0))] + p_ispecs,
        out_specs=[col(0), col(0),
                   pl.BlockSpec((None, None, nc, HG_DIM, HG_DIM), lambda h, b, s: (b, h, s, 0, 0))] + p_ospecs,
        out_shape=[jax.ShapeDtypeStruct((B, S, HG_WIDTH), F32),
                   jax.ShapeDtypeStruct((B, S, u_rows), BF16),
                   jax.ShapeDtypeStruct((B, HG_HEADS, S // CHUNK, HG_DIM, HG_DIM), F32)] + p_oshapes,
        input_output_aliases=p_alias,
        scratch_shapes=[pltpu.VMEM((HG_DIM, HG_DIM), F32)] + p_scratch,
        compiler_params=_params(("arbitrary", "arbitrary", "arbitrary")),
    )(proj_h, proj_h, proj_h, proj_h, lb_param, g_head, ts, _level_masks(), *p_args)
    return res[0], res[1], res[2], list(res[3:])


def _hgrn_bwd(proj_h, o_h, du, states, lb_param, g_head, layer, name, phase=None):
    B, S, _ = proj_h.shape
    sb = _pick(S, (512, 256, 128, 64))
    nc = sb // CHUNK
    ns = S // sb
    ts, tst = _cum_matrices()

    def body(*refs):
        ins, outs, (dst,), p_in, p_out, p_sems = _split_refs(refs, 12, 6, 1, phase)
        q_ref, f_ref, i_ref, z_ref, o_ref, du_ref, sts_ref, lbp_ref, gh_ref, ts_ref, tst_ref, m_ref = ins
        dq_ref, df_ref, di_ref, dz_ref, dlb_ref, dgh_ref = outs
        h_id, b_id, s_id = pl.program_id(0), pl.program_id(1), pl.program_id(2)
        _hosted_start(phase, p_in, p_out, p_sems, (h_id == 0) & (b_id == 0) & (s_id == 0))

        @pl.when(s_id == 0)
        def _():
            dst[...] = jnp.zeros_like(dst)

        @pl.when((b_id == 0) & (s_id == 0))
        def _():
            dlb_ref[...] = jnp.zeros_like(dlb_ref)

        @pl.when((h_id == 0) & (b_id == 0) & (s_id == 0))
        def _():
            dgh_ref[...] = jnp.zeros_like(dgh_ref)

        lt = _lb_terms(lbp_ref[...], layer)
        gh = gh_ref[...]
        tsv = ts_ref[...]
        tstv = tst_ref[...]
        logf, k, f, sig, nsig = _gate(f_ref[...], lt)
        o = o_ref[...]
        dub = du_ref[...]
        r = lax.rsqrt(jnp.mean(o * o, axis=-1, keepdims=True) + NORM_EPS)
        n = o * r
        sg, sg_grad = _silu_and_grad(z_ref[...])
        dz_ref[...] = (dub * (n * gh) * sg_grad).astype(BF16)
        dgh_ref[...] += jnp.sum(dub * sg * n, axis=0, keepdims=True)
        dn = dub * sg * gh
        do = r * (dn - n * jnp.mean(dn * n, axis=-1, keepdims=True))
        v = i_ref[...]
        qf, qf_grad = _silu_and_grad(q_ref[...])
        w = _hg_block_bwd(qf, k, v, logf, do, tsv, m_ref, nc)
        cur = dst[...]
        ends = [None] * nc
        for ci in reversed(range(nc)):
            ends[ci] = cur
            cur = cur * w["dec"][ci] + w["qd"][ci]
        dst[...] = cur
        dq, dk, dv, dg = _hg_state_bwd(w, v, do, [sts_ref[ci] for ci in range(nc)], ends, tstv, nc)
        di_ref[...] = dv.astype(BF16)
        dq_ref[...] = (dq * qf_grad).astype(BF16)
        scaled = (dg - f * dk) / f
        df_ref[...] = (scaled * lt["one_m"] * sig * nsig).astype(BF16)
        dlb_ref[...] += jnp.sum(scaled * (lt["ind"] - sig), axis=0, keepdims=True)
        _hosted_finish(phase, p_in, p_out, p_sems, (h_id == HG_HEADS - 1) & (b_id == B - 1) & (s_id == ns - 1))

    col = lambda base: pl.BlockSpec((None, sb, HG_DIM), lambda h, b, s: (b, ns - 1 - s, base + h))
    out_col = pl.BlockSpec((None, sb, HG_DIM), lambda h, b, s: (b, ns - 1 - s, h))
    dt = jax.ShapeDtypeStruct((B, S, HG_WIDTH), BF16)
    p_ispecs, p_ospecs, p_oshapes, p_alias, p_scratch, p_args = _host_phase(phase, 12, 6)
    res = pl.pallas_call(
        body, name=name,
        grid=(HG_HEADS, B, ns),
        in_specs=[col(0), col(HG_HEADS), col(2 * HG_HEADS), col(3 * HG_HEADS), col(0), col(0),
                  pl.BlockSpec((None, None, nc, HG_DIM, HG_DIM), lambda h, b, s: (b, h, ns - 1 - s, 0, 0)),
                  pl.BlockSpec((DEPTH, HG_DIM), lambda h, b, s: (0, h)),
                  pl.BlockSpec((1, HG_DIM), lambda h, b, s: (0, 0)),
                  pl.BlockSpec((N_CUM_F, CHUNK), lambda h, b, s: (0, 0)),
                  pl.BlockSpec((CHUNK, N_CUM), lambda h, b, s: (0, 0)),
                  pl.BlockSpec((len(_LEVELS), CHUNK, CHUNK), lambda h, b, s: (0, 0, 0))] + p_ispecs,
        out_specs=[out_col, out_col, out_col, out_col,
                   pl.BlockSpec((1, HG_DIM), lambda h, b, s: (0, h)),
                   pl.BlockSpec((1, HG_DIM), lambda h, b, s: (0, 0))] + p_ospecs,
        out_shape=[dt, dt, dt, dt, jax.ShapeDtypeStruct((1, HG_WIDTH), F32),
                   jax.ShapeDtypeStruct((1, HG_DIM), F32)] + p_oshapes,
        input_output_aliases=p_alias,
        scratch_shapes=[pltpu.VMEM((HG_DIM, HG_DIM), F32)] + p_scratch,
        compiler_params=_params(("arbitrary", "arbitrary", "arbitrary")),
    )(proj_h, proj_h, proj_h, proj_h, o_h, du, states, lb_param, g_head, ts, tst, _level_masks(), *p_args)
    return tuple(res[:6]) + (list(res[6:]),)


def _rope_tables(S):
    half = ATT_DIM // 2
    inv_freq = ROPE_THETA ** (-jnp.arange(half, dtype=F32) / half)
    ang = jnp.arange(S).astype(F32)[:, None] * inv_freq[None, :]
    cos = jnp.cos(ang)
    sin = jnp.sin(ang)
    cos = jnp.concatenate([cos, cos, cos, cos], axis=1)
    sin = jnp.concatenate([-sin, sin, -sin, sin], axis=1)
    return cos, sin


def _attn_common():
    lane = lax.broadcasted_iota(jnp.int32, (1, 2 * ATT_DIM), 1)
    first_half = (lane % ATT_DIM) < (ATT_DIM // 2)
    left = lane < ATT_DIM

    def swap(x):
        return jnp.where(first_half, pltpu.roll(x, 128 - ATT_DIM // 2, 1), pltpu.roll(x, ATT_DIM // 2, 1))

    def rope(x, cos, sin):
        return x * cos + swap(x) * sin

    def rope_bwd(dy, cos, sin):
        return dy * cos + swap(dy * sin)

    def dup(x):
        xs = pltpu.roll(x, ATT_DIM, 1)
        return [jnp.where(left, x, xs), jnp.where(left, xs, x)]

    return left, rope, rope_bwd, dup


GROUP = ATT_HEADS // 2
GROUP_ROWS = GROUP * ATT_BLOCK


def _attn_bias(i):
    r = lax.broadcasted_iota(jnp.int32, (ATT_BLOCK, 2 * ATT_BLOCK), 0)
    c = lax.broadcasted_iota(jnp.int32, (ATT_BLOCK, 2 * ATT_BLOCK), 1)
    ok = (c > r) & (c <= r + ATT_BLOCK) & ((c >= ATT_BLOCK) | (i > 0))
    return jnp.where(ok, 0.0, NEG_INF)


def _stack_heads(pairs, left):
    rows = []
    for x in pairs:
        rows += [jnp.where(left, x, 0.0), jnp.where(left, 0.0, x)]
    return jnp.concatenate(rows, axis=0)


def _unstack_heads(y, left, pp):
    r0 = 2 * pp * ATT_BLOCK
    return jnp.where(left, y[r0:r0 + ATT_BLOCK], y[r0 + ATT_BLOCK:r0 + 2 * ATT_BLOCK])


def _row_sums(x):
    return _dot(x, jnp.ones((x.shape[1], 128), BF16), NN)


def _attn_probs(qs, kd, vd, sink, bias):
    n = range(len(qs))
    rows = qs[0].shape[0]
    s = [(_dot(qs[j], kd[j], NT).reshape(rows // ATT_BLOCK, ATT_BLOCK, 2 * ATT_BLOCK) * ATT_SCALE + bias[None])
         .reshape(rows, 2 * ATT_BLOCK) for j in n]
    m = [jnp.max(jnp.maximum(jnp.maximum(s[j][:, :128], s[j][:, 128:]), sink[j]), axis=-1, keepdims=True) for j in n]
    pu = [jnp.exp(s[j] - m[j]) for j in n]
    es = [jnp.exp(sink[j] - m[j]) for j in n]
    ones = jnp.ones((2 * ATT_BLOCK, 128), BF16)
    ov = [_dot(pu[j], jnp.concatenate([vd[j].astype(BF16), ones], axis=1), NN) for j in n]
    inv = [1.0 / (ov[j][:, 128:] + es[j]) for j in n]
    return ([pu[j] * jnp.concatenate([inv[j], inv[j]], axis=1) for j in n], [es[j] * inv[j] for j in n],
            [ov[j][:, :128] * inv[j] for j in n])


def _sink_rows(sinks_l):
    return jnp.broadcast_to(jnp.repeat(sinks_l, ATT_BLOCK)[:, None], (ATT_HEADS * ATT_BLOCK, 128))


_Z0 = (2 * ATT_WIDTH + 2 * KV_WIDTH - ATT_WIDTH) // 256


def _attn_fwd(proj_a, u, sinks_l, cos, sin, name, phase=None):
    B, S, _ = proj_a.shape
    nb = S // ATT_BLOCK

    def body(*refs):
        ins, (u_ref,), _, p_in, p_out, p_sems = _split_refs(refs, 13, 1, 0, phase)
        q_ref, kvc_ref, kvp_ref, z0, z1, z2, z3, cos_ref, sin_ref, cosp_ref, sinp_ref, sinks_ref, _ = ins
        i = pl.program_id(1)
        _hosted_start(phase, p_in, p_out, p_sems, (pl.program_id(0) == 0) & (i == 0))
        left, rope, _, dup = _attn_common()
        cos_c, sin_c = cos_ref[...], sin_ref[...]
        kvc = kvc_ref[...]
        kvp = kvp_ref[...]
        kw = jnp.concatenate([rope(kvp[:, :KV_WIDTH], cosp_ref[...], sinp_ref[...]),
                              rope(kvc[:, :KV_WIDTH], cos_c, sin_c)], axis=0)
        vw = jnp.concatenate([kvp[:, KV_WIDTH:], kvc[:, KV_WIDTH:]], axis=0)
        kd, vd = dup(kw), dup(vw)
        bias = _attn_bias(i)
        zs = (z0, z1, z2, z3)
        pairs = [range(4 * kvh, 4 * kvh + 4) for kvh in range(2)]
        qs = [_stack_heads([rope(q_ref[:, 128 * pr:128 * (pr + 1)], cos_c, sin_c) for pr in pairs[kvh]], left)
              for kvh in range(2)]
        sink = [sinks_ref[kvh * GROUP_ROWS:(kvh + 1) * GROUP_ROWS, :] for kvh in range(2)]
        o = _attn_probs(qs, kd, vd, sink, bias)[2]
        for kvh in range(2):
            for pp, pr in enumerate(pairs[kvh]):
                z = zs[pr // 2][:, 128 * (pr % 2):128 * (pr % 2 + 1)]
                u_ref[:, 128 * pr:128 * (pr + 1)] = (_unstack_heads(o[kvh], left, pp) * _silu(z)).astype(BF16)
        _hosted_finish(phase, p_in, p_out, p_sems, (pl.program_id(0) == B - 1) & (i == nb - 1))

    rowblk = lambda w, cb: pl.BlockSpec((None, ATT_BLOCK, w), lambda b, i: (b, i, cb))
    tab = pl.BlockSpec((ATT_BLOCK, 128), lambda b, i: (i, 0))
    tabp = pl.BlockSpec((ATT_BLOCK, 128), lambda b, i: (jnp.maximum(i - 1, 0), 0))
    p_ispecs, p_ospecs, p_oshapes, p_alias, p_scratch, p_args = _host_phase(phase, 13, 1)
    res = pl.pallas_call(
        body, name=name,
        grid=(B, nb),
        in_specs=[rowblk(ATT_WIDTH, 0), rowblk(256, 4),
                  pl.BlockSpec((None, ATT_BLOCK, 256), lambda b, i: (b, jnp.maximum(i - 1, 0), 4)),
                  rowblk(256, _Z0), rowblk(256, _Z0 + 1), rowblk(256, _Z0 + 2), rowblk(256, _Z0 + 3),
                  tab, tab, tabp, tabp,
                  pl.BlockSpec((ATT_HEADS * ATT_BLOCK, 128), lambda b, i: (0, 0)),
                  pl.BlockSpec(memory_space=pl.ANY)] + p_ispecs,
        out_specs=[pl.BlockSpec((None, ATT_BLOCK, ATT_WIDTH), lambda b, i: (b, i, 1))] + p_ospecs,
        out_shape=[jax.ShapeDtypeStruct(u.shape, BF16)] + p_oshapes,
        input_output_aliases={12: 0, **p_alias},
        scratch_shapes=p_scratch,
        compiler_params=_params(("arbitrary", "arbitrary")),
    )(proj_a, proj_a, proj_a, proj_a, proj_a, proj_a, proj_a, cos, sin, cos, sin, sinks_l, u, *p_args)
    return res[0], list(res[1:])


def _attn_bwd(proj_a, du, sinks_l, cos, sin, name, phase=None):
    B, S, _ = proj_a.shape
    nb = S // ATT_BLOCK

    def body(*refs):
        ins, outs, (carry, sk_acc), p_in, p_out, p_sems = _split_refs(refs, 13, 4, 2, phase)
        q_ref, kvc_ref, kvp_ref, z0, z1, z2, z3, du_ref, cos_ref, sin_ref, cosp_ref, sinp_ref, sinks_ref = ins
        dq_ref, dkv_ref, dz_ref, dsk_ref = outs
        b_id, i = pl.program_id(0), pl.program_id(1)
        _hosted_start(phase, p_in, p_out, p_sems, (b_id == 0) & (i == 0))

        @pl.when((b_id == 0) & (i == 0))
        def _():
            sk_acc[...] = jnp.zeros_like(sk_acc)

        @pl.when(i == 0)
        def _():
            carry[...] = jnp.zeros_like(carry)

        @pl.when(i < nb)
        def _():
            left, rope, rope_bwd, dup = _attn_common()
            cos_c, sin_c = cos_ref[...], sin_ref[...]
            cos_p, sin_p = cosp_ref[...], sinp_ref[...]
            kvc = kvc_ref[...]
            kvp = kvp_ref[...]
            kw = jnp.concatenate([rope(kvp[:, :KV_WIDTH], cos_p, sin_p), rope(kvc[:, :KV_WIDTH], cos_c, sin_c)], axis=0)
            vw = jnp.concatenate([kvp[:, KV_WIDTH:], kvc[:, KV_WIDTH:]], axis=0)
            kd, vd = dup(kw), dup(vw)
            bias = _attn_bias(i)
            zs = (z0, z1, z2, z3)
            units = [(kvh, hf) for kvh in range(2) for hf in range(2)]
            half = GROUP_ROWS // 2
            pairs = [range(4 * kvh + 2 * hf, 4 * kvh + 2 * hf + 2) for kvh, hf in units]
            un = range(len(units))
            qs = [_stack_heads([rope(q_ref[:, 128 * pr:128 * (pr + 1)], cos_c, sin_c) for pr in pairs[j]], left)
                  for j in un]
            sink = [sinks_ref[kvh * GROUP_ROWS + hf * half:kvh * GROUP_ROWS + (hf + 1) * half, :] for kvh, hf in units]
            ku = [kd[kvh] for kvh, _ in units]
            vu = [vd[kvh] for kvh, _ in units]
            def first(j):
                p, ps, o = (r[0] for r in _attn_probs([qs[j]], [ku[j]], [vu[j]], [sink[j]], bias))
                parts = []
                for pp, pr in enumerate(pairs[j]):
                    cols = slice(128 * pr, 128 * (pr + 1))
                    sg, sg_grad = _silu_and_grad(zs[pr // 2][:, 128 * (pr % 2):128 * (pr % 2 + 1)])
                    du128 = du_ref[:, cols]
                    dz_ref[:, cols] = (du128 * _unstack_heads(o, left, pp) * sg_grad).astype(BF16)
                    parts.append(du128 * sg)
                dos = _stack_heads(parts, left)
                dp = _dot(dos, vu[j], NT)
                delta = _row_sums(p * dp)
                ds = (p * (dp - jnp.concatenate([delta, delta], axis=1)) * ATT_SCALE).astype(BF16)
                kvh, hf = units[j]
                sk_acc[kvh, hf * half:(hf + 1) * half, :] += -ps * delta
                return ds, p.astype(BF16), dos.astype(BF16)

            def second(j, ds, p, dos):
                dqs = _dot(ds, ku[j], NN)
                for pp, pr in enumerate(pairs[j]):
                    dq_ref[:, 128 * pr:128 * (pr + 1)] = rope_bwd(_unstack_heads(dqs, left, pp),
                                                                  cos_c, sin_c).astype(BF16)
                return _dot(ds, qs[j], TN), _dot(p, dos, TN)

            got, dku, dvu = {}, [None] * len(units), [None] * len(units)
            for j in range(len(units) + 1):
                if j < len(units):
                    got[j] = first(j)
                if j >= 1:
                    dku[j - 1], dvu[j - 1] = second(j - 1, *got.pop(j - 1))
            dkd = [dku[0] + dku[1], dku[2] + dku[3]]
            dvd = [dvu[0] + dvu[1], dvu[2] + dvu[3]]
            fold = lambda pr: jnp.where(left, pr[0] + pltpu.roll(pr[0], ATT_DIM, 1), pr[1] + pltpu.roll(pr[1], ATT_DIM, 1))
            dkw = fold(dkd)
            dvw = fold(dvd)
            prev = jnp.concatenate([rope_bwd(dkw[:ATT_BLOCK], cos_p, sin_p), dvw[:ATT_BLOCK]], axis=1)
            cur = jnp.concatenate([rope_bwd(dkw[ATT_BLOCK:], cos_c, sin_c), dvw[ATT_BLOCK:]], axis=1)
            dkv_ref[...] = (carry[...] + prev).astype(BF16)
            carry[...] = cur

        @pl.when(i == nb)
        def _():
            dkv_ref[...] = carry[...].astype(BF16)

        @pl.when((b_id == B - 1) & (i == nb))
        def _():
            lane = lax.broadcasted_iota(jnp.int32, (1, 128), 1)
            tot = jnp.zeros((1, 128), F32)
            for hd in range(ATT_HEADS):
                rows = sk_acc[hd // GROUP, (hd % GROUP) * ATT_BLOCK:(hd % GROUP + 1) * ATT_BLOCK, :]
                tot = tot + jnp.where(lane == hd, jnp.sum(rows, axis=0, keepdims=True), 0.0)
            dsk_ref[...] = tot

        _hosted_finish(phase, p_in, p_out, p_sems, (b_id == B - 1) & (i == nb))

    cl = lambda i: jnp.minimum(i, nb - 1)
    pv = lambda i: jnp.maximum(jnp.minimum(i, nb - 1) - 1, 0)
    rowblk = lambda w, cb: pl.BlockSpec((None, ATT_BLOCK, w), lambda b, i: (b, cl(i), cb))
    tab = pl.BlockSpec((ATT_BLOCK, 128), lambda b, i: (cl(i), 0))
    tabp = pl.BlockSpec((ATT_BLOCK, 128), lambda b, i: (pv(i), 0))
    p_ispecs, p_ospecs, p_oshapes, p_alias, p_scratch, p_args = _host_phase(phase, 13, 4)
    res = pl.pallas_call(
        body, name=name,
        grid=(B, nb + 1),
        in_specs=[rowblk(ATT_WIDTH, 0), rowblk(256, 4),
                  pl.BlockSpec((None, ATT_BLOCK, 256), lambda b, i: (b, pv(i), 4)),
                  rowblk(256, _Z0), rowblk(256, _Z0 + 1), rowblk(256, _Z0 + 2), rowblk(256, _Z0 + 3),
                  rowblk(ATT_WIDTH, 1),
                  tab, tab, tabp, tabp,
                  pl.BlockSpec((ATT_HEADS * ATT_BLOCK, 128), lambda b, i: (0, 0))] + p_ispecs,
        out_specs=[rowblk(ATT_WIDTH, 0),
                   pl.BlockSpec((None, ATT_BLOCK, 256), lambda b, i: (b, jnp.maximum(i - 1, 0), 0)),
                   rowblk(ATT_WIDTH, 0),
                   pl.BlockSpec((1, 128), lambda b, i: (0, 0))] + p_ospecs,
        out_shape=[jax.ShapeDtypeStruct((B, S, ATT_WIDTH), BF16), jax.ShapeDtypeStruct((B, S, 256), BF16),
                   jax.ShapeDtypeStruct((B, S, ATT_WIDTH), BF16), jax.ShapeDtypeStruct((1, 128), F32)] + p_oshapes,
        input_output_aliases=p_alias,
        scratch_shapes=[pltpu.VMEM((ATT_BLOCK, 256), F32), pltpu.VMEM((2, GROUP_ROWS, 128), F32)] + p_scratch,
        compiler_params=_params(("arbitrary", "arbitrary")),
    )(proj_a, proj_a, proj_a, proj_a, proj_a, proj_a, proj_a, du, cos, sin, cos, sin, sinks_l, *p_args)
    return tuple(res[:4]) + (list(res[4:]),)


def _outproj_fwd(u2, w_out, x2, g_post, target2, name):
    T, D = x2.shape
    tm = _pick(T, (512, 256, 128))
    last = target2 is not None

    def body(u_ref, w_ref, x_ref, g_ref, *rest):
        y = lax.dot_general(u_ref[...], w_ref[...], (NN, ((), ())), preferred_element_type=F32)
        r = lax.rsqrt(jnp.mean(y * y, axis=-1, keepdims=True) + NORM_EPS)
        xn = x_ref[...] + (y * r) * g_ref[...]
        if last:
            t_ref, y_ref, dx_ref, loss_ref = rest
            err = xn - t_ref[...]
            dx_ref[...] = err * (1.0 / D)
            sq = err * err
            acc = sq[:, 0:128]
            for kk in range(1, D // 128):
                acc = acc + sq[:, 128 * kk:128 * (kk + 1)]
            part = jnp.sum(acc.reshape(tm // 8, 8, 128), axis=0) * (0.5 / D)

            @pl.when(pl.program_id(0) == 0)
            def _():
                loss_ref[...] = jnp.zeros_like(loss_ref)

            loss_ref[...] += part
        else:
            y_ref, xn_ref = rest
            xn_ref[...] = xn
        y_ref[...] = y

    row = pl.BlockSpec((tm, D), lambda i: (i, 0))
    in_specs = [pl.BlockSpec((tm, MIX_WIDTH), lambda i: (i, 0)),
                pl.BlockSpec((MIX_WIDTH, D), lambda i: (0, 0)), row,
                pl.BlockSpec((1, D), lambda i: (0, 0))]
    args = [u2, w_out, x2, g_post]
    out_specs = [row, row]
    out_shape = [jax.ShapeDtypeStruct((T, D), F32), jax.ShapeDtypeStruct((T, D), F32)]
    if last:
        in_specs.append(row)
        args.append(target2)
        out_specs.append(pl.BlockSpec((8, 128), lambda i: (0, 0)))
        out_shape.append(jax.ShapeDtypeStruct((8, 128), F32))
    return pl.pallas_call(
        body, name=name, grid=(T // tm,), in_specs=in_specs, out_specs=out_specs, out_shape=out_shape,
        compiler_params=_params(("arbitrary",)),
    )(*args)


def _outproj_bwd(dxn2, y2, g_post, w_out, name):
    T, D = y2.shape
    N = w_out.shape[0]
    tm = _pick(T, (512, 256, 128))
    nt = T // tm

    def body(dx_ref, y_ref, g_ref, w_ref, dy_ref, dg_ref, du_ref, acc):
        i = pl.program_id(0)

        @pl.when(i == 0)
        def _():
            acc[...] = jnp.zeros_like(acc)

        y = y_ref[...]
        dxn = dx_ref[...]
        r = lax.rsqrt(jnp.mean(y * y, axis=-1, keepdims=True) + NORM_EPS)
        n = y * r
        dn = dxn * g_ref[...]
        dy = (r * (dn - n * jnp.mean(dn * n, axis=-1, keepdims=True))).astype(BF16)
        dy_ref[...] = dy
        du_ref[...] = lax.dot_general(dy, w_ref[...], (NT, ((), ())), preferred_element_type=F32)
        acc[...] += jnp.sum((dxn * n).reshape(tm // 8, 8, D), axis=0)

        @pl.when(i == nt - 1)
        def _():
            dg_ref[...] = jnp.sum(acc[...], axis=0, keepdims=True)

    row = pl.BlockSpec((tm, D), lambda i: (i, 0))
    vec = pl.BlockSpec((1, D), lambda i: (0, 0))
    return pl.pallas_call(
        body, name=name, grid=(nt,),
        in_specs=[row, row, vec, pl.BlockSpec((N, D), lambda i: (0, 0), pipeline_mode=pl.Buffered(1))],
        out_specs=[row, vec, pl.BlockSpec((tm, N), lambda i: (i, 0))],
        out_shape=[jax.ShapeDtypeStruct((T, D), BF16), jax.ShapeDtypeStruct((1, D), F32),
                   jax.ShapeDtypeStruct((T, N), F32)],
        scratch_shapes=[pltpu.VMEM((8, D), F32)],
        compiler_params=_params(("arbitrary",)),
    )(dxn2, y2, g_post, w_out)


def _inproj_bwd(pieces, w_t, x2, dxn2, g_pre, name, phase=None):
    T, D = x2.shape
    widths = [p.shape[1] for p in pieces]
    offs = [sum(widths[:i]) for i in range(len(pieces))]
    n_p = len(pieces)
    tm = _pick(T, (256, 128))
    nt = T // tm

    def body(*refs):
        ins, (dx_ref, dg_ref), (acc,), p_in, p_out, p_sems = _split_refs(refs, n_p + 4, 2, 1, phase)
        w_ref, x_ref, dxn_ref, g_ref = ins[n_p:]
        i = pl.program_id(0)
        _hosted_start(phase, p_in, p_out, p_sems, i == 0)

        @pl.when(i == 0)
        def _():
            acc[...] = jnp.zeros_like(acc)

        dh = jnp.zeros((tm, D), F32)
        for p in range(n_p):
            dh = dh + lax.dot_general(ins[p][...], w_ref[offs[p]:offs[p] + widths[p], :], (NN, ((), ())),
                                      preferred_element_type=F32)
        x = x_ref[...]
        r = lax.rsqrt(jnp.mean(x * x, axis=-1, keepdims=True) + NORM_EPS)
        n = x * r
        dn = dh * g_ref[...]
        dx_ref[...] = dxn_ref[...] + r * (dn - n * jnp.mean(dn * n, axis=-1, keepdims=True))
        acc[...] += jnp.sum((dh * n).reshape(tm // 8, 8, D), axis=0)

        @pl.when(i == nt - 1)
        def _():
            dg_ref[...] = jnp.sum(acc[...], axis=0, keepdims=True)

        _hosted_finish(phase, p_in, p_out, p_sems, i == nt - 1)

    row = pl.BlockSpec((tm, D), lambda i: (i, 0))
    vec = pl.BlockSpec((1, D), lambda i: (0, 0))
    p_ispecs, p_ospecs, p_oshapes, p_alias, p_scratch, p_args = _host_phase(phase, n_p + 4, 2)
    res = pl.pallas_call(
        body, name=name, grid=(nt,),
        in_specs=[pl.BlockSpec((tm, w), lambda i: (i, 0)) for w in widths]
        + [pl.BlockSpec((sum(widths), D), lambda i: (0, 0), pipeline_mode=pl.Buffered(1)), row, row, vec] + p_ispecs,
        out_specs=[row, vec] + p_ospecs,
        out_shape=[jax.ShapeDtypeStruct((T, D), F32), jax.ShapeDtypeStruct((1, D), F32)] + p_oshapes,
        input_output_aliases=p_alias,
        scratch_shapes=[pltpu.VMEM((8, D), F32)] + p_scratch,
        compiler_params=_params(("arbitrary",)),
    )(*pieces, w_t, x2, dxn2, g_pre, *p_args)
    return res[0], res[1], list(res[2:])


def _step(x, target, g_pre, g_post, lb_param, g_head, sinks, shards=None, full=None):
    B, S, D = x.shape
    T = B * S
    dist = shards is not None
    if dist:
        a_loc, b_loc = shards
        ra, rb = a_loc.shape[1], b_loc.shape[1]
        side = _own_side_blocks()
        placed = lambda loc, nm: _place_own(loc, side, "place_" + nm)
        gather = lambda phase, nm: _run_phase(phase, nm)
        w_in0 = gather(_gather_ici_phase([a_loc[0]], [placed(a_loc[0], "in0")]), "gather_in0_ici")
        w_in0 = gather(_gather_d2d_phase(w_in0, [ra]), "gather_in0_d2d")[0]
        late_locs = [a_loc[1], b_loc[1], b_loc[0]]
        late_rs = [ra, rb, rb]
        late_full = [placed(a_loc[1], "in1"), placed(b_loc[1], "out1"), placed(b_loc[0], "out0")]
        w_in, w_out = [w_in0, None], [None, None]
    else:
        w_in, w_out = list(full[0]), list(full[1])
    cos, sin = _rope_tables(S)
    saved = []
    xs = x
    loss_part = None
    dxn = None
    for l in range(DEPTH):
        x2 = xs.reshape(T, D)
        host = dist and l == 0
        proj_h, proj_a, h = _inproj(x2, g_pre[l:l + 1], w_in[l], f"inproj{l}")
        proj_h = proj_h.reshape(B, S, N_H)
        proj_a = proj_a.reshape(B, S, N_A)
        o_h, u, states, got = _hgrn_fwd(proj_h, MIX_WIDTH, lb_param, g_head[l:l + 1], l, f"hgrn_fwd{l}",
                                        _gather_ici_phase(late_locs, late_full) if host else None)
        u, got = _attn_fwd(proj_a, u, _sink_rows(sinks[l]), cos, sin, f"attn_fwd{l}",
                           _gather_d2d_phase(got, late_rs) if host else None)
        if host:
            w_in[1], w_out[1], w_out[0] = got
        u2 = u.reshape(T, MIX_WIDTH)
        if l < DEPTH - 1:
            y, xn = _outproj_fwd(u2, w_out[l], x2, g_post[l:l + 1], None, f"outproj{l}")
            xn = xn.reshape(B, S, D)
        else:
            y, dxn, loss_part = _outproj_fwd(u2, w_out[l], x2, g_post[l:l + 1], target.reshape(T, D), f"outproj{l}")
            xn = None
        saved.append((x2, h, proj_h, proj_a, o_h, u2, states, y))
        xs = xn

    dw_in, dw_out = [None] * DEPTH, [None] * DEPTH
    dg_pre, dg_post, dlb, dg_head, dsinks = [], [], [], [], []
    for l in reversed(range(DEPTH)):
        x2, h, proj_h, proj_a, o_h, u2, states, y = saved[l]
        host = dist and l == 0
        dy, dgp, du = _outproj_bwd(dxn, y, g_post[l:l + 1], w_out[l], f"outproj_bwd{l}")
        du = du.reshape(B, S, MIX_WIDTH)
        dw_out[l] = _mm_tn([u2], dy, f"wgrad_out{l}")
        if host:
            early = [dw_in[1], dw_out[1], dw_out[0]]
        dqh, dfh, dih, dzh, dlb_l, dgh, got = _hgrn_bwd(
            proj_h, o_h, du, states, lb_param, g_head[l:l + 1], l, f"hgrn_bwd{l}",
            _reduce_d2d_phase(early, late_rs) if host else None)
        if host:
            parts = [_pair_sum(g, r, side, f"pair_sum{i}") for i, (g, r) in enumerate(zip(early, got))]
        dqa, dkv, dza, dsk, got = _attn_bwd(proj_a, du, _sink_rows(sinks[l]), cos, sin, f"attn_bwd{l}",
                                            _reduce_ici_phase(parts) if host else None)
        if host:
            dw_in[1], dw_out[1], dw_out[0] = [_chip_sum(p, r, f"chip_sum{i}")
                                              for i, (p, r) in enumerate(zip(parts, got))]
        dproj = [p.reshape(T, p.shape[-1]) for p in (dqh, dfh, dih, dzh, dqa, dkv, dza)]
        dw_in[l] = _mm_tn(dproj, h, f"wgrad_in{l}")
        if host:
            got = _run_phase(_reduce_d2d_phase([dw_in[0]], [ra]), "reduce_in0_d2d")
            part = _pair_sum(dw_in[0], got[0], side, "pair_sum_in0")
        dxn, dgpre, got = _inproj_bwd(dproj, w_in[l], x2, dxn, g_pre[l:l + 1], f"inproj_bwd{l}",
                                      _reduce_ici_phase([part]) if host else None)
        if host:
            dw_in[0] = _chip_sum(part, got[0], "chip_sum_in0")
        dg_pre.append(dgpre)
        dg_post.append(dgp)
        dlb.append(dlb_l)
        dg_head.append(dgh)
        dsinks.append(dsk)
    rev = lambda lst: jnp.concatenate(lst[::-1], axis=0)
    return (loss_part, dxn.reshape(B, S, D), jnp.stack(dw_in), jnp.stack(dw_out),
            rev(dg_pre), rev(dg_post), rev(dlb), rev(dg_head), rev(dsinks))


def _me_and_peers():
    x, y, c = lax.axis_index("x"), lax.axis_index("y"), lax.axis_index("c")
    me = 4 * x + 2 * y + c
    peers = []
    for k in range(1, N_DEV):
        px = 1 - x if k & 4 else x
        py = 1 - y if k & 2 else y
        pc = 1 - c if k & 1 else c
        peers.append(((px, py, pc), 4 * px + 2 * py + pc))
    return me, peers


class _Phase:
    def __init__(self, arrays, out_shapes, aliases, n_send, build):
        self.arrays, self.out_shapes, self.aliases = list(arrays), list(out_shapes), dict(aliases)
        self.n_send, self.build = n_send, build

    def scratch(self):
        return [pltpu.SemaphoreType.DMA((self.n_send,)), pltpu.SemaphoreType.DMA((self.n_send,))]

    def _copies(self, in_refs, out_refs, sems, arrivals):
        send_sems, recv_sems = sems
        sends, recvs = self.build(in_refs, out_refs)
        assert len(sends) == self.n_send == len(recvs)
        out = [pltpu.make_async_remote_copy(src_ref=s, dst_ref=d, send_sem=send_sems.at[i], recv_sem=recv_sems.at[i],
                                            device_id=dev, device_id_type=MESH) for i, (s, d, dev) in enumerate(sends)]
        inc = [pltpu.make_async_remote_copy(src_ref=s, dst_ref=r, send_sem=send_sems.at[i], recv_sem=recv_sems.at[i],
                                            device_id=dev, device_id_type=MESH)
               for i, ((s, _, dev), r) in enumerate(zip(sends, recvs))] if arrivals else []
        return out, inc

    def start(self, in_refs, out_refs, sems):
        out, _ = self._copies(in_refs, out_refs, sems, False)
        for cp in out:
            cp.start()

    def finish(self, in_refs, out_refs, sems):
        out, inc = self._copies(in_refs, out_refs, sems, True)
        for cp in inc:
            cp.wait_recv()
        for cp in out:
            cp.wait_send()


_ANY = pl.BlockSpec(memory_space=pl.ANY)


def _host_phase(phase, n_in, n_out):
    if phase is None:
        return [], [], [], {}, [], []
    aliases = {n_in + i: n_out + o for i, o in phase.aliases.items()}
    return ([_ANY] * len(phase.arrays), [_ANY] * len(phase.out_shapes), phase.out_shapes, aliases, phase.scratch(),
            phase.arrays)


def _split_refs(refs, n_in, n_out, n_scr, phase):
    pi = len(phase.arrays) if phase else 0
    po = len(phase.out_shapes) if phase else 0
    a = n_in + pi
    b = a + n_out + po
    return (refs[:n_in], refs[a:a + n_out], refs[b:b + n_scr], refs[n_in:a], refs[a + n_out:b], refs[b + n_scr:])


def _hosted_start(phase, p_in, p_out, p_sems, first):
    if phase is not None:
        @pl.when(first)
        def _():
            phase.start(p_in, p_out, p_sems)


def _hosted_finish(phase, p_in, p_out, p_sems, last):
    if phase is not None:
        @pl.when(last)
        def _():
            phase.finish(p_in, p_out, p_sems)


def _run_phase(phase, name):
    n_in, n_out = len(phase.arrays), len(phase.out_shapes)

    def body(*refs):
        phase.start(refs[:n_in], refs[n_in:n_in + n_out], refs[n_in + n_out:])
        phase.finish(refs[:n_in], refs[n_in:n_in + n_out], refs[n_in + n_out:])

    return pl.pallas_call(
        body, name=name, in_specs=[_ANY] * n_in, out_specs=[_ANY] * n_out,
        out_shape=phase.out_shapes, input_output_aliases=phase.aliases, scratch_shapes=phase.scratch(),
        compiler_params=pltpu.CompilerParams(has_side_effects=True),
    )(*phase.arrays)


def _mesh_place():
    x, y, c = lax.axis_index("x"), lax.axis_index("y"), lax.axis_index("c")
    chips = [(x, y), (1 - x, y), (x, 1 - y), (1 - x, 1 - y)]
    num = lambda chip, core: 4 * chip[0] + 2 * chip[1] + core
    return c, chips, num


def _own_side_blocks():
    c, chips, num = _mesh_place()
    return jnp.stack([num(ch, c) for ch in chips]).astype(jnp.int32)


def _rows(ref, r, dev):
    return ref.at[pl.ds(pl.multiple_of(dev * r, 16), r), :]


def _place_own(loc, blocks, name):
    r, D = loc.shape
    tr = _pick(r, (400, 256, 200, 128, 64, 16))

    def body(idx_ref, l_ref, o_ref):
        del idx_ref
        o_ref[...] = l_ref[...]

    return pl.pallas_call(
        body, name=name,
        grid_spec=pltpu.PrefetchScalarGridSpec(
            num_scalar_prefetch=1, grid=(r // tr,),
            in_specs=[pl.BlockSpec((tr, D), lambda i, idx: (i, 0))],
            out_specs=pl.BlockSpec((tr, D), lambda i, idx: (idx[0] * (r // tr) + i, 0))),
        out_shape=jax.ShapeDtypeStruct((N_DEV * r, D), loc.dtype),
        compiler_params=_params(("arbitrary",)),
    )(blocks, loc)


def _gather_ici_phase(locs, fulls):
    rs = [a.shape[0] for a in locs]
    n = len(locs)

    def build(ins, outs):
        c, chips, num = _mesh_place()
        me = num(chips[0], c)
        targets = [((*chips[0], 1 - c), num(chips[0], 1 - c))] + [((*ch, c), num(ch, c)) for ch in chips[1:]]
        sends, recvs = [], []
        for dev, dnum in targets:
            for i, r in enumerate(rs):
                sends.append((ins[i], _rows(outs[i], r, me), dev))
                recvs.append(_rows(outs[i], r, dnum))
        return sends, recvs

    shapes = [jax.ShapeDtypeStruct(a.shape, a.dtype) for a in fulls]
    return _Phase(list(locs) + list(fulls), shapes, {n + i: i for i in range(n)}, 4 * n, build)


def _gather_d2d_phase(fulls, rs):
    def build(ins, outs):
        c, chips, num = _mesh_place()
        sib = (*chips[0], 1 - c)
        sends, recvs = [], []
        for ch in chips[1:]:
            for i, r in enumerate(rs):
                blk = _rows(outs[i], r, num(ch, c))
                sends.append((blk, blk, sib))
                recvs.append(_rows(outs[i], r, num(ch, 1 - c)))
        return sends, recvs

    shapes = [jax.ShapeDtypeStruct(a.shape, a.dtype) for a in fulls]
    return _Phase(fulls, shapes, {i: i for i in range(len(fulls))}, 3 * len(fulls), build)


def _reduce_d2d_phase(grads, rs):
    def build(ins, outs):
        c, chips, num = _mesh_place()
        sib = (*chips[0], 1 - c)
        sends, recvs = [], []
        for j, ch in enumerate(chips):
            for i, r in enumerate(rs):
                sends.append((_rows(ins[i], r, num(ch, 1 - c)), outs[i].at[j], sib))
                recvs.append(outs[i].at[j])
        return sends, recvs

    shapes = [jax.ShapeDtypeStruct((4, r, g.shape[1]), g.dtype) for g, r in zip(grads, rs)]
    return _Phase(grads, shapes, {}, 4 * len(grads), build)


def _reduce_ici_phase(parts):
    def build(ins, outs):
        c, chips, _ = _mesh_place()
        sends, recvs = [], []
        for t in range(1, 4):
            for i in range(len(parts)):
                sends.append((ins[i].at[t], outs[i].at[t - 1], (*chips[t], c)))
                recvs.append(outs[i].at[t - 1])
        return sends, recvs

    shapes = [jax.ShapeDtypeStruct((3,) + p.shape[1:], p.dtype) for p in parts]
    return _Phase(parts, shapes, {}, 3 * len(parts), build)


def _pair_sum(g, got, blocks, name):
    n, r, D = got.shape
    tr = _pick(r, (400, 256, 200, 128, 64, 16))

    def body(idx_ref, g_ref, r_ref, o_ref):
        del idx_ref
        o_ref[...] = (g_ref[...].astype(F32) + r_ref[...].astype(F32)).astype(o_ref.dtype)

    blk = pl.BlockSpec((None, tr, D), lambda j, i, idx: (j, i, 0))
    return pl.pallas_call(
        body, name=name,
        grid_spec=pltpu.PrefetchScalarGridSpec(
            num_scalar_prefetch=1, grid=(n, r // tr),
            in_specs=[pl.BlockSpec((tr, D), lambda j, i, idx: (idx[j] * (r // tr) + i, 0)), blk],
            out_specs=blk),
        out_shape=jax.ShapeDtypeStruct(got.shape, got.dtype),
        compiler_params=_params(("arbitrary", "arbitrary")),
    )(blocks, g, got)


def _chip_sum(p, r, name):
    _, R, D = p.shape
    tr = _pick(R, (400, 256, 200, 128, 64, 16))

    def body(p_ref, r_ref, o_ref):
        acc = p_ref[...].astype(F32)
        for t in range(3):
            acc = acc + r_ref[t].astype(F32)
        o_ref[...] = acc

    return pl.pallas_call(
        body, name=name, grid=(R // tr,),
        in_specs=[pl.BlockSpec((None, tr, D), lambda i: (0, i, 0)), pl.BlockSpec((3, tr, D), lambda i: (0, i, 0))],
        out_specs=pl.BlockSpec((tr, D), lambda i: (i, 0)), out_shape=jax.ShapeDtypeStruct((R, D), F32),
        compiler_params=_params(("parallel",)))(p, r)


def _allreduce_small(vec):
    R, C = vec.shape

    def body(v_ref, o_ref, buf, send_sems, recv_sems):
        me, peers = _me_and_peers()
        buf[me] = v_ref[...]
        sends = []
        for k, (pid, _) in enumerate(peers):
            cp = pltpu.make_async_remote_copy(src_ref=v_ref, dst_ref=buf.at[me], send_sem=send_sems.at[k],
                                              recv_sem=recv_sems.at[k], device_id=pid, device_id_type=MESH)
            cp.start()
            sends.append(cp)
        for k, (pid, pnum) in enumerate(peers):
            pltpu.make_async_remote_copy(src_ref=v_ref, dst_ref=buf.at[pnum], send_sem=send_sems.at[k],
                                         recv_sem=recv_sems.at[k], device_id=pid, device_id_type=MESH).wait_recv()
        for cp in sends:
            cp.wait_send()
        acc = buf[0]
        for d in range(1, N_DEV):
            acc = acc + buf[d]
        o_ref[...] = acc

    vm = pl.BlockSpec(memory_space=pltpu.VMEM)
    return pl.pallas_call(
        body, name="allreduce_small",
        in_specs=[vm], out_specs=vm,
        out_shape=jax.ShapeDtypeStruct((R, C), F32),
        scratch_shapes=[pltpu.VMEM((N_DEV, R, C), F32), pltpu.SemaphoreType.DMA((N_DEV - 1,)),
                        pltpu.SemaphoreType.DMA((N_DEV - 1,))],
        compiler_params=pltpu.CompilerParams(has_side_effects=True),
    )(vec)


def _adamw(w, g, m, v, name):
    R, C = w.shape
    tr = _pick(R, (512, 400, 256, 128, 64, 32, 16, 8)) if R >= 8 else R
    c1 = 1.0 - ADAM_B1 ** ADAM_STEP
    c2 = 1.0 - ADAM_B2 ** ADAM_STEP

    def body(w_ref, g_ref, m_ref, v_ref, d_ref, mo_ref, vo_ref):
        gg = g_ref[...]
        mn = ADAM_B1 * m_ref[...] + (1.0 - ADAM_B1) * gg
        vn = ADAM_B2 * v_ref[...] + (1.0 - ADAM_B2) * (gg * gg)
        d_ref[...] = -ADAM_LR * ((mn / c1) / (jnp.sqrt(vn / c2) + ADAM_EPS) + ADAM_WD * w_ref[...])
        mo_ref[...] = mn
        vo_ref[...] = vn

    blk = pl.BlockSpec((tr, C), lambda i: (i, 0))
    sh = jax.ShapeDtypeStruct((R, C), F32)
    return pl.pallas_call(
        body, name=name, grid=(R // tr,), in_specs=[blk] * 4, out_specs=[blk] * 3, out_shape=[sh] * 3,
        compiler_params=_params(("parallel",)),
    )(w, g, m, v)


def _lb_param_grad(lb_param, dlb):
    L, C = lb_param.shape

    def body(p_ref, d_ref, o_ref):
        lbp = p_ref[...]
        d = d_ref[...]
        mx = jnp.max(lbp, axis=0, keepdims=True)
        e = jnp.exp(lbp - mx)
        p = e / jnp.sum(e, axis=0, keepdims=True)
        tot = jnp.sum(d, axis=0, keepdims=True)
        dps = []
        rest = tot
        for j in range(L):
            dps.append(rest - tot if j == 0 else rest)
            rest = rest - d[j:j + 1]
        dp = jnp.concatenate(dps, axis=0)
        o_ref[...] = p * (dp - jnp.sum(p * dp, axis=0, keepdims=True))

    vm = pl.BlockSpec(memory_space=pltpu.VMEM)
    return pl.pallas_call(body, name="lb_param_grad", in_specs=[vm, vm], out_specs=vm,
                          out_shape=jax.ShapeDtypeStruct((L, C), F32))(lb_param, dlb)


def _pack_small(loss_part, dg_pre, dg_post, dlb, dg_head, dsinks):
    pad8 = lambda a: jnp.pad(a.reshape(-1, 128), ((0, 8 - DEPTH), (0, 0)))
    rows = [dg_pre.reshape(-1, 128), dg_post.reshape(-1, 128), dlb.reshape(-1, 128), pad8(dg_head), pad8(dsinks),
            loss_part]
    return jnp.concatenate(rows, axis=0)


def _unpack_small(vec):
    n = DEPTH * D_MODEL // 128
    o = 0
    dg_pre = vec[o:o + n].reshape(DEPTH, D_MODEL); o += n
    dg_post = vec[o:o + n].reshape(DEPTH, D_MODEL); o += n
    dlb = vec[o:o + n].reshape(DEPTH, HG_WIDTH); o += n
    dg_head = vec[o:o + DEPTH]; o += 8
    dsinks = vec[o:o + DEPTH, :ATT_HEADS]; o += 8
    loss = jnp.sum(vec[o:o + 8])
    return loss, dg_pre, dg_post, dlb, dg_head, dsinks


def kernel(x, w_in, w_out, g_pre, g_post, lb_param, g_head, sinks, loss_target, m_w_in, m_w_out, m_g_pre, m_g_post, m_lb_param, m_g_head, m_sinks, v_w_in, v_w_out, v_g_pre, v_g_post, v_lb_param, v_g_head, v_sinks):
    tr = lambda a: jnp.swapaxes(a, 1, 2)
    w_in_t = tr(w_in)
    (loss_part, dx, gw_in_t, gw_out, dg_pre, dg_post, dlb, dg_head, dsinks) = _step(
        x, loss_target, g_pre, g_post, lb_param, g_head, sinks, shards=(w_in_t.astype(BF16), w_out.astype(BF16)))

    small = _allreduce_small(_pack_small(loss_part, dg_pre, dg_post, dlb, dg_head, dsinks))
    loss, gg_pre, gg_post, gdlb, gg_head, gsinks = _unpack_small(small)
    glb = _lb_param_grad(lb_param, gdlb)

    grads = [gw_in_t, gw_out, gg_pre, gg_post, glb, gg_head, gsinks]
    ws = [w_in_t, w_out, g_pre, g_post, lb_param, g_head, sinks]
    ms = [tr(m_w_in), m_w_out, m_g_pre, m_g_post, m_lb_param, m_g_head, m_sinks]
    vs = [tr(v_w_in), v_w_out, v_g_pre, v_g_post, v_lb_param, v_g_head, v_sinks]
    names = ["w_in", "w_out", "g_pre", "g_post", "lb_param", "g_head", "sinks"]
    deltas, new_m, new_v = [], [], []
    for w, g, m, v, nm in zip(ws, grads, ms, vs, names):
        sh = w.shape
        two = lambda a: a.reshape(-1, sh[-1])
        d, mn, vn = _adamw(two(w), two(g), two(m), two(v), "adamw_" + nm)
        deltas.append(d.reshape(sh))
        new_m.append(mn.reshape(sh))
        new_v.append(vn.reshape(sh))
    grads[0], deltas[0], new_m[0], new_v[0] = tr(grads[0]), tr(deltas[0]), tr(new_m[0]), tr(new_v[0])
    return (loss, dx, *grads, *deltas, *new_m, *new_v)
```

```python
import functools
import math

import numpy as np
import jax
import jax.numpy as jnp
from jax import lax
from jax.experimental import pallas as pl
from jax.experimental.pallas import tpu as pltpu

F32 = jnp.float32
BF16 = jnp.bfloat16

D_MODEL = 1024
DEPTH = 2
HG_HEADS = 8
HG_DIM = 128
HG_WIDTH = HG_HEADS * HG_DIM
CHUNK = 64
ATT_HEADS = 16
ATT_DIM = 64
ATT_WIDTH = ATT_HEADS * ATT_DIM
KV_WIDTH = 128
ATT_BLOCK = 128
ATT_SCALE = 1.0 / math.sqrt(ATT_DIM)
ROPE_THETA = 10000.0
NORM_EPS = 1e-6
NEG_INF = -1e30
LB_FLOOR = 1e-20
N_H = 4 * HG_WIDTH
N_A = 2 * ATT_WIDTH + 2 * KV_WIDTH
IN_WIDTH = N_H + N_A
MIX_WIDTH = HG_WIDTH + ATT_WIDTH

ADAM_LR = 0.001
ADAM_B1 = 0.9
ADAM_B2 = 0.999
ADAM_EPS = 1e-08
ADAM_WD = 0.01
ADAM_STEP = 10

N_DEV = 8
MESH = pl.DeviceIdType.MESH
VMEM_LIMIT = 56 * 1024 * 1024

NN = ((1,), (0,))
NT = ((1,), (1,))
TN = ((0,), (0,))


def _dot(a, b, dims):
    return lax.dot_general(a.astype(BF16), b.astype(BF16), (dims, ((), ())), preferred_element_type=F32)


def _params(sem=None, **kw):
    return pltpu.CompilerParams(dimension_semantics=sem, vmem_limit_bytes=VMEM_LIMIT, **kw)


def _sigmoids(x):
    e = jnp.exp(-jnp.abs(x))
    r = 1.0 / (1.0 + e)
    er = e * r
    pos = x >= 0.0
    return jnp.where(pos, r, er), jnp.where(pos, er, r)


def _silu(x):
    return x * _sigmoids(x)[0]


def _silu_and_grad(x):
    s, ns = _sigmoids(x)
    return x * s, s * (1.0 + x * ns)


def _pick(n, prefs):
    for p in prefs:
        if n % p == 0:
            return p
    return n


def _inproj(x2, g, w, name):
    T, D = x2.shape
    tm = _pick(T, (256, 128))
    nchunk = 1024

    def body(x_ref, g_ref, w_ref, oh_ref, oa_ref, h_ref):
        x = x_ref[...]
        r = lax.rsqrt(jnp.mean(x * x, axis=-1, keepdims=True) + NORM_EPS)
        h = ((x * r) * g_ref[...]).astype(BF16)
        h_ref[...] = h
        for j in range(0, N_H, nchunk):
            oh_ref[:, j:j + nchunk] = lax.dot_general(h, w_ref[j:j + nchunk, :], (NT, ((), ())),
                                                      preferred_element_type=F32)
        for j in range(0, N_A, N_A // 2):
            oa_ref[:, j:j + N_A // 2] = lax.dot_general(h, w_ref[N_H + j:N_H + j + N_A // 2, :], (NT, ((), ())),
                                                        preferred_element_type=F32)

    row = lambda w_: pl.BlockSpec((tm, w_), lambda i: (i, 0))
    return pl.pallas_call(
        body, name=name,
        grid=(T // tm,),
        in_specs=[row(D), pl.BlockSpec((1, D), lambda i: (0, 0)),
                  pl.BlockSpec((IN_WIDTH, D), lambda i: (0, 0), pipeline_mode=pl.Buffered(1))],
        out_specs=[row(N_H), row(N_A), row(D)],
        out_shape=[jax.ShapeDtypeStruct((T, N_H), F32), jax.ShapeDtypeStruct((T, N_A), F32),
                   jax.ShapeDtypeStruct((T, D), BF16)],
        compiler_params=_params(("parallel",)),
    )(x2, g, w)


def _mm_tn(pieces, b, name, out_dtype=BF16):
    T, m = b.shape
    tn = 256
    counts = [p.shape[1] // tn for p in pieces]
    starts = [sum(counts[:i]) for i in range(len(pieces))]
    n_p = len(pieces)

    def body(*refs):
        b_ref, o_ref = refs[n_p], refs[n_p + 1]
        i = pl.program_id(0)
        for p in range(n_p):
            @pl.when((i >= starts[p]) & (i < starts[p] + counts[p]))
            def _(p=p):
                o_ref[...] = lax.dot_general(refs[p][...], b_ref[...], (TN, ((), ())),
                                             preferred_element_type=F32).astype(out_dtype)

    piece_spec = lambda s, c: pl.BlockSpec((T, tn), lambda i: (0, jnp.clip(i - s, 0, c - 1)))
    return pl.pallas_call(
        body, name=name,
        grid=(sum(counts),),
        in_specs=[piece_spec(s, c) for s, c in zip(starts, counts)]
        + [pl.BlockSpec((T, m), lambda i: (0, 0), pipeline_mode=pl.Buffered(1))],
        out_specs=pl.BlockSpec((tn, m), lambda i: (i, 0)),
        out_shape=jax.ShapeDtypeStruct((sum(counts) * tn, m), out_dtype),
        compiler_params=_params(("arbitrary",)),
    )(*pieces, b)


_LEVELS = (0, 1, 2, 4, 8, 16, 32)
_CUM_L = (2, 4, 8, 16, 32, 64)
_ALL_KINDS = tuple(("c", L) for L in _CUM_L) + tuple(("r", L) for L in _CUM_L)
_MXU_KINDS = (("c", 2), ("c", 4), ("c", CHUNK), ("r", 2), ("r", 4))
N_CUM = len(_ALL_KINDS) * CHUNK
N_CUM_F = len(_MXU_KINDS) * CHUNK


def _cum_matrices():
    t = np.arange(CHUNK)[:, None]
    r = np.arange(CHUNK)[None, :]

    def mat(kind):
        c, L = kind
        return ((r // L == t // L) & ((r <= t) if c == "c" else (r > t))).astype(np.float32)

    fwd = np.concatenate([mat(kd) for kd in _MXU_KINDS], axis=0)
    full = np.concatenate([mat(kd) for kd in _ALL_KINDS], axis=0)
    return jnp.asarray(fwd, BF16), jnp.asarray(full.T.copy(), BF16)


def _level_masks():
    t = np.arange(CHUNK)[:, None]
    s = np.arange(CHUNK)[None, :]
    ms = []
    for L in _LEVELS:
        if L == 0:
            ms.append(t == s)
        else:
            ms.append((t // (2 * L) == s // (2 * L)) & ((t // L) % 2 == 1) & ((s // L) % 2 == 0))
    return jnp.asarray(np.stack(ms).astype(np.float32))


def _split3(x):
    hi = x.astype(BF16)
    r1 = x - hi.astype(F32)
    mid = r1.astype(BF16)
    lo = (r1 - mid.astype(F32)).astype(BF16)
    return hi, mid, lo


def _cum3(ts, x, terms=3):
    d = lambda p: lax.dot_general(ts, p, (NN, ((), ())), preferred_element_type=F32)
    return sum(d(p) for p in _split3(x)[:terms])


def _lb_terms(lbp, layer):
    mx = jnp.max(lbp, axis=0, keepdims=True)
    e = jnp.exp(lbp - mx)
    p = e / jnp.sum(e, axis=0, keepdims=True)
    cum = p[0:1]
    for j in range(1, layer + 1):
        cum = cum + p[j:j + 1]
    lb = cum - p[0:1]
    lbf = jnp.maximum(lb, LB_FLOOR)
    return dict(lbf=lbf, one_m=1.0 - lb, kcorr=lb - lbf, ind=jnp.where(lb > LB_FLOOR, 1.0, 0.0))


def _gate(x, lt):
    sig, nsig = _sigmoids(x)
    f = lt["lbf"] + lt["one_m"] * sig
    return jnp.log(f), lt["one_m"] * nsig + lt["kcorr"], f, sig, nsig


def _ck(x, ci):
    return x[ci * CHUNK:(ci + 1) * CHUNK]


def _block_cums(ts, g, nc):
    cs = [_cum3(ts, _ck(g, ci)) for ci in range(nc)]
    out = {kind: jnp.concatenate([c[CHUNK * i:CHUNK * (i + 1)] for c in cs], axis=0)
           for i, kind in enumerate(_MXU_KINDS)}
    b = out[("c", CHUNK)]
    ng = CHUNK // 8
    last = b.reshape(nc, ng, 8, HG_DIM)[:, :, 7:8, :]
    zero = jnp.zeros((nc, 1, 1, HG_DIM), F32)

    def spread(groups):
        return jnp.broadcast_to(jnp.concatenate(groups, axis=1), (nc, ng, 8, HG_DIM)).reshape(nc * CHUNK, HG_DIM)

    def get(kind):
        if kind in out:
            return out[kind]
        c, L = kind
        nb = L // 8
        first = lambda r: (r // nb) * nb
        if c == "c":
            return b - spread([last[:, first(r) - 1:first(r)] if r >= nb else zero for r in range(ng)])
        return spread([last[:, first(r) + nb - 1:first(r) + nb] for r in range(ng)]) - b

    return get


def _level_factors(cums, g, L):
    if L == 0:
        return None, None
    if L == 1:
        return jnp.exp(g), None
    return jnp.exp(cums(("c", L))), jnp.exp(cums(("r", L)))


def _mul(a, e):
    return a if e is None else a * e


def _hg_block_fwd(qf, k, v, g, ts, m_ref, nc):
    cums = _block_cums(ts, g, nc)
    amat = [jnp.zeros((CHUNK, CHUNK), F32)] * nc
    for li, L in enumerate(_LEVELS):
        eq, ek = _level_factors(cums, g, L)
        ql, kl, m = _mul(qf, eq), _mul(k, ek), m_ref[li]
        amat = [amat[ci] + _dot(_ck(ql, ci), _ck(kl, ci), NT) * m for ci in range(nc)]
    b = cums(("c", CHUNK))
    kst = k * jnp.exp(cums(("r", CHUNK)))
    o = [_dot(amat[ci], _ck(v, ci), NN) for ci in range(nc)]
    kv = [_dot(_ck(v, ci), _ck(kst, ci), TN) for ci in range(nc)]
    dec = [jnp.exp(b[(ci + 1) * CHUNK - 1:(ci + 1) * CHUNK, :]) for ci in range(nc)]
    return o, dec, kv, qf * jnp.exp(b)


def _hg_block_bwd(qf, k, v, g, do, ts, m_ref, nc):
    cums = _block_cums(ts, g, nc)
    dcs = {}
    da = [_dot(_ck(do, ci), _ck(v, ci), NT) for ci in range(nc)]
    dq = jnp.zeros_like(qf)
    dk = jnp.zeros_like(qf)
    dg = jnp.zeros_like(qf)
    amat = [jnp.zeros((CHUNK, CHUNK), F32)] * nc
    for li, L in enumerate(_LEVELS):
        eq, ek = _level_factors(cums, g, L)
        ql, kl, m = _mul(qf, eq), _mul(k, ek), m_ref[li]
        qlb, klb = ql.astype(BF16), kl.astype(BF16)
        amat = [amat[ci] + _dot(_ck(qlb, ci), _ck(klb, ci), NT) * m for ci in range(nc)]
        dal = [(da[ci] * m).astype(BF16) for ci in range(nc)]
        dql = jnp.concatenate([_dot(dal[ci], _ck(klb, ci), NN) for ci in range(nc)], axis=0)
        dkl = jnp.concatenate([_dot(dal[ci], _ck(qlb, ci), TN) for ci in range(nc)], axis=0)
        dq = dq + _mul(dql, eq)
        dk = dk + _mul(dkl, ek)
        if L == 1:
            dg = dg + dql * ql
        elif L > 1:
            dcs[("c", L)] = (dql * ql).astype(BF16)
            dcs[("r", L)] = (dkl * kl).astype(BF16)
    b = cums(("c", CHUNK))
    e64 = jnp.exp(b)
    er64 = jnp.exp(cums(("r", CHUNK)))
    qb = qf * e64
    return dict(dq=dq, dk=dk, dg=dg, dcs=dcs, e64=e64, er64=er64, qb=qb, kst=k * er64,
                dv=[_dot(amat[ci], _ck(do, ci), TN) for ci in range(nc)],
                dec=[jnp.exp(b[(ci + 1) * CHUNK - 1:(ci + 1) * CHUNK, :]) for ci in range(nc)],
                qd=[_dot(_ck(do, ci), _ck(qb, ci), TN) for ci in range(nc)])


def _hg_state_bwd(w, v, do, starts, ends, tst, nc):
    dqb = jnp.concatenate([_dot(_ck(do, ci), starts[ci], NN) for ci in range(nc)], axis=0)
    dkst = jnp.concatenate([_dot(_ck(v, ci), ends[ci], NN) for ci in range(nc)], axis=0)
    dq = w["dq"] + dqb * w["e64"]
    dk = w["dk"] + dkst * w["er64"]
    dv = jnp.concatenate([w["dv"][ci] + _dot(_ck(w["kst"], ci), ends[ci], NT) for ci in range(nc)], axis=0)
    trow = lax.broadcasted_iota(jnp.int32, (CHUNK, 1), 0)
    dtot = jnp.concatenate(
        [jnp.where(trow == CHUNK - 1, jnp.sum(ends[ci] * starts[ci], axis=0, keepdims=True) * w["dec"][ci], 0.0)
         for ci in range(nc)], axis=0)
    dcs = dict(w["dcs"])
    dcs[("c", CHUNK)] = (dqb * w["qb"] + dtot).astype(BF16)
    dcs[("r", CHUNK)] = (dkst * w["kst"]).astype(BF16)
    dgs = [_dot(tst, jnp.concatenate([_ck(dcs[kind], ci) for kind in _ALL_KINDS], axis=0), NN) for ci in range(nc)]
    return dq, dk, dv, w["dg"] + jnp.concatenate(dgs, axis=0)


def _hgrn_fwd(proj_h, u_rows, lb_param, g_head, layer, name, phase=None):
    B, S, _ = proj_h.shape
    sb = _pick(S, (512, 256, 128, 64))
    nc = sb // CHUNK
    ts, _ = _cum_matrices()

    def body(*refs):
        ins, outs, (st,), p_in, p_out, p_sems = _split_refs(refs, 8, 3, 1, phase)
        q_ref, f_ref, i_ref, z_ref, lbp_ref, gh_ref, ts_ref, m_ref = ins
        o_ref, u_ref, sts_ref = outs
        h_id, b_id, s_id = pl.program_id(0), pl.program_id(1), pl.program_id(2)
        _hosted_start(phase, p_in, p_out, p_sems, (h_id == 0) & (b_id == 0) & (s_id == 0))

        @pl.when(s_id == 0)
        def _():
            st[...] = jnp.zeros_like(st)

        lt = _lb_terms(lbp_ref[...], layer)
        tsv = ts_ref[...]
        gh = gh_ref[...]
        logf, k = _gate(f_ref[...], lt)[:2]
        o_part, dec, kv, qb = _hg_block_fwd(_silu(q_ref[...]), k, i_ref[...], logf, tsv, m_ref, nc)
        cur = st[...]
        starts = []
        for ci in range(nc):
            sts_ref[ci] = cur
            starts.append(cur)
            cur = cur * dec[ci] + kv[ci]
        st[...] = cur
        o = jnp.concatenate([o_part[ci] + _dot(_ck(qb, ci), starts[ci], NT) for ci in range(nc)], axis=0)
        o_ref[...] = o
        r = lax.rsqrt(jnp.mean(o * o, axis=-1, keepdims=True) + NORM_EPS)
        u_ref[...] = (((o * r) * gh) * _silu(z_ref[...])).astype(BF16)
        _hosted_finish(phase, p_in, p_out, p_sems, (h_id == HG_HEADS - 1) & (b_id == B - 1) & (s_id == S // sb - 1))

    col = lambda base: pl.BlockSpec((None, sb, HG_DIM), lambda h, b, s: (b, s, base + h))
    p_ispecs, p_ospecs, p_oshapes, p_alias, p_scratch, p_args = _host_phase(phase, 8, 3)
    res = pl.pallas_call(
        body, name=name,
        grid=(HG_HEADS, B, S // sb),
        in_specs=[col(0), col(HG_HEADS), col(2 * HG_HEADS), col(3 * HG_HEADS),
                  pl.BlockSpec((DEPTH, HG_DIM), lambda h, b, s: (0, h)),
                  pl.BlockSpec((1, HG_DIM), lambda h, b, s: (0, 0)),
                  pl.BlockSpec((N_CUM_F, CHUNK), lambda h, b, s: (0, 0)),
                  pl.BlockSpec((len(_LEVELS), CHUNK, CHUNK), lambda h, b, s: (0, 0, 0))] + p_ispecs,
        out_specs=[col(0), col(0),
                   pl.BlockSpec((None, None, nc, HG_DIM, HG_DIM), lambda h, b, s: (b, h, s, 0, 0))] + p_ospecs,
        out_shape=[jax.ShapeDtypeStruct((B, S, HG_WIDTH), F32),
                   jax.ShapeDtypeStruct((B, S, u_rows), BF16),
                   jax.ShapeDtypeStruct((B, HG_HEADS, S // CHUNK, HG_DIM, HG_DIM), F32)] + p_oshapes,
        input_output_aliases=p_alias,
        scratch_shapes=[pltpu.VMEM((HG_DIM, HG_DIM), F32)] + p_scratch,
        compiler_params=_params(("arbitrary", "arbitrary", "arbitrary")),
    )(proj_h, proj_h, proj_h, proj_h, lb_param, g_head, ts, _level_masks(), *p_args)
    return res[0], res[1], res[2], list(res[3:])


def _hgrn_bwd(proj_h, o_h, du, states, lb_param, g_head, layer, name, phase=None):
    B, S, _ = proj_h.shape
    sb = _pick(S, (512, 256, 128, 64))
    nc = sb // CHUNK
    ns = S // sb
    ts, tst = _cum_matrices()

    def body(*refs):
        ins, outs, (dst,), p_in, p_out, p_sems = _split_refs(refs, 12, 6, 1, phase)
        q_ref, f_ref, i_ref, z_ref, o_ref, du_ref, sts_ref, lbp_ref, gh_ref, ts_ref, tst_ref, m_ref = ins
        dq_ref, df_ref, di_ref, dz_ref, dlb_ref, dgh_ref = outs
        h_id, b_id, s_id = pl.program_id(0), pl.program_id(1), pl.program_id(2)
        _hosted_start(phase, p_in, p_out, p_sems, (h_id == 0) & (b_id == 0) & (s_id == 0))

        @pl.when(s_id == 0)
        def _():
            dst[...] = jnp.zeros_like(dst)

        @pl.when((b_id == 0) & (s_id == 0))
        def _():
            dlb_ref[...] = jnp.zeros_like(dlb_ref)

        @pl.when((h_id == 0) & (b_id == 0) & (s_id == 0))
        def _():
            dgh_ref[...] = jnp.zeros_like(dgh_ref)

        lt = _lb_terms(lbp_ref[...], layer)
        gh = gh_ref[...]
        tsv = ts_ref[...]
        tstv = tst_ref[...]
        logf, k, f, sig, nsig = _gate(f_ref[...], lt)
        o = o_ref[...]
        dub = du_ref[...]
        r = lax.rsqrt(jnp.mean(o * o, axis=-1, keepdims=True) + NORM_EPS)
        n = o * r
        sg, sg_grad = _silu_and_grad(z_ref[...])
        dz_ref[...] = (dub * (n * gh) * sg_grad).astype(BF16)
        dgh_ref[...] += jnp.sum(dub * sg * n, axis=0, keepdims=True)
        dn = dub * sg * gh
        do = r * (dn - n * jnp.mean(dn * n, axis=-1, keepdims=True))
        v = i_ref[...]
        qf, qf_grad = _silu_and_grad(q_ref[...])
        w = _hg_block_bwd(qf, k, v, logf, do, tsv, m_ref, nc)
        cur = dst[...]
        ends = [None] * nc
        for ci in reversed(range(nc)):
            ends[ci] = cur
            cur = cur * w["dec"][ci] + w["qd"][ci]
        dst[...] = cur
        dq, dk, dv, dg = _hg_state_bwd(w, v, do, [sts_ref[ci] for ci in range(nc)], ends, tstv, nc)
        di_ref[...] = dv.astype(BF16)
        dq_ref[...] = (dq * qf_grad).astype(BF16)
        scaled = (dg - f * dk) / f
        df_ref[...] = (scaled * lt["one_m"] * sig * nsig).astype(BF16)
        dlb_ref[...] += jnp.sum(scaled * (lt["ind"] - sig), axis=0, keepdims=True)
        _hosted_finish(phase, p_in, p_out, p_sems, (h_id == HG_HEADS - 1) & (b_id == B - 1) & (s_id == ns - 1))

    col = lambda base: pl.BlockSpec((None, sb, HG_DIM), lambda h, b, s: (b, ns - 1 - s, base + h))
    out_col = pl.BlockSpec((None, sb, HG_DIM), lambda h, b, s: (b, ns - 1 - s, h))
    dt = jax.ShapeDtypeStruct((B, S, HG_WIDTH), BF16)
    p_ispecs, p_ospecs, p_oshapes, p_alias, p_scratch, p_args = _host_phase(phase, 12, 6)
    res = pl.pallas_call(
        body, name=name,
        grid=(HG_HEADS, B, ns),
        in_specs=[col(0), col(HG_HEADS), col(2 * HG_HEADS), col(3 * HG_HEADS), col(0), col(0),
                  pl.BlockSpec((None, None, nc, HG_DIM, HG_DIM), lambda h, b, s: (b, h, ns - 1 - s, 0, 0)),
                  pl.BlockSpec((DEPTH, HG_DIM), lambda h, b, s: (0, h)),
                  pl.BlockSpec((1, HG_DIM), lambda h, b, s: (0, 0)),
                  pl.BlockSpec((N_CUM_F, CHUNK), lambda h, b, s: (0, 0)),
                  pl.BlockSpec((CHUNK, N_CUM), lambda h, b, s: (0, 0)),
                  pl.BlockSpec((len(_LEVELS), CHUNK, CHUNK), lambda h, b, s: (0, 0, 0))] + p_ispecs,
        out_specs=[out_col, out_col, out_col, out_col,
                   pl.BlockSpec((1, HG_DIM), lambda h, b, s: (0, h)),
                   pl.BlockSpec((1, HG_DIM), lambda h, b, s: (0, 0))] + p_ospecs,
        out_shape=[dt, dt, dt, dt, jax.ShapeDtypeStruct((1, HG_WIDTH), F32),
                   jax.ShapeDtypeStruct((1, HG_DIM), F32)] + p_oshapes,
        input_output_aliases=p_alias,
        scratch_shapes=[pltpu.VMEM((HG_DIM, HG_DIM), F32)] + p_scratch,
        compiler_params=_params(("arbitrary", "arbitrary", "arbitrary")),
    )(proj_h, proj_h, proj_h, proj_h, o_h, du, states, lb_param, g_head, ts, tst, _level_masks(), *p_args)
    return tuple(res[:6]) + (list(res[6:]),)


def _rope_tables(S):
    half = ATT_DIM // 2
    inv_freq = ROPE_THETA ** (-jnp.arange(half, dtype=F32) / half)
    ang = jnp.arange(S).astype(F32)[:, None] * inv_freq[None, :]
    cos = jnp.cos(ang)
    sin = jnp.sin(ang)
    cos = jnp.concatenate([cos, cos, cos, cos], axis=1)
    sin = jnp.concatenate([-sin, sin, -sin, sin], axis=1)
    return cos, sin


def _attn_common():
    lane = lax.broadcasted_iota(jnp.int32, (1, 2 * ATT_DIM), 1)
    first_half = (lane % ATT_DIM) < (ATT_DIM // 2)
    left = lane < ATT_DIM

    def swap(x):
        return jnp.where(first_half, pltpu.roll(x, 128 - ATT_DIM // 2, 1), pltpu.roll(x, ATT_DIM // 2, 1))

    def rope(x, cos, sin):
        return x * cos + swap(x) * sin

    def rope_bwd(dy, cos, sin):
        return dy * cos + swap(dy * sin)

    def dup(x):
        xs = pltpu.roll(x, ATT_DIM, 1)
        return [jnp.where(left, x, xs), jnp.where(left, xs, x)]

    return left, rope, rope_bwd, dup


GROUP = ATT_HEADS // 2
GROUP_ROWS = GROUP * ATT_BLOCK


def _attn_bias(i):
    r = lax.broadcasted_iota(jnp.int32, (ATT_BLOCK, 2 * ATT_BLOCK), 0)
    c = lax.broadcasted_iota(jnp.int32, (ATT_BLOCK, 2 * ATT_BLOCK), 1)
    ok = (c > r) & (c <= r + ATT_BLOCK) & ((c >= ATT_BLOCK) | (i > 0))
    return jnp.where(ok, 0.0, NEG_INF)


def _stack_heads(pairs, left):
    rows = []
    for x in pairs:
        rows += [jnp.where(left, x, 0.0), jnp.where(left, 0.0, x)]
    return jnp.concatenate(rows, axis=0)


def _unstack_heads(y, left, pp):
    r0 = 2 * pp * ATT_BLOCK
    return jnp.where(left, y[r0:r0 + ATT_BLOCK], y[r0 + ATT_BLOCK:r0 + 2 * ATT_BLOCK])


def _row_sums(x):
    return _dot(x, jnp.ones((x.shape[1], 128), BF16), NN)


def _attn_probs(qs, kd, vd, sink, bias):
    n = range(len(qs))
    rows = qs[0].shape[0]
    s = [(_dot(qs[j], kd[j], NT).reshape(rows // ATT_BLOCK, ATT_BLOCK, 2 * ATT_BLOCK) * ATT_SCALE + bias[None])
         .reshape(rows, 2 * ATT_BLOCK) for j in n]
    m = [jnp.max(jnp.maximum(jnp.maximum(s[j][:, :128], s[j][:, 128:]), sink[j]), axis=-1, keepdims=True) for j in n]
    pu = [jnp.exp(s[j] - m[j]) for j in n]
    es = [jnp.exp(sink[j] - m[j]) for j in n]
    ones = jnp.ones((2 * ATT_BLOCK, 128), BF16)
    ov = [_dot(pu[j], jnp.concatenate([vd[j].astype(BF16), ones], axis=1), NN) for j in n]
    inv = [1.0 / (ov[j][:, 128:] + es[j]) for j in n]
    return ([pu[j] * jnp.concatenate([inv[j], inv[j]], axis=1) for j in n], [es[j] * inv[j] for j in n],
            [ov[j][:, :128] * inv[j] for j in n])


def _sink_rows(sinks_l):
    return jnp.broadcast_to(jnp.repeat(sinks_l, ATT_BLOCK)[:, None], (ATT_HEADS * ATT_BLOCK, 128))


_Z0 = (2 * ATT_WIDTH + 2 * KV_WIDTH - ATT_WIDTH) // 256


def _attn_fwd(proj_a, u, sinks_l, cos, sin, name, phase=None):
    B, S, _ = proj_a.shape
    nb = S // ATT_BLOCK

    def body(*refs):
        ins, (u_ref,), _, p_in, p_out, p_sems = _split_refs(refs, 13, 1, 0, phase)
        q_ref, kvc_ref, kvp_ref, z0, z1, z2, z3, cos_ref, sin_ref, cosp_ref, sinp_ref, sinks_ref, _ = ins
        i = pl.program_id(1)
        _hosted_start(phase, p_in, p_out, p_sems, (pl.program_id(0) == 0) & (i == 0))
        left, rope, _, dup = _attn_common()
        cos_c, sin_c = cos_ref[...], sin_ref[...]
        kvc = kvc_ref[...]
        kvp = kvp_ref[...]
        kw = jnp.concatenate([rope(kvp[:, :KV_WIDTH], cosp_ref[...], sinp_ref[...]),
                              rope(kvc[:, :KV_WIDTH], cos_c, sin_c)], axis=0)
        vw = jnp.concatenate([kvp[:, KV_WIDTH:], kvc[:, KV_WIDTH:]], axis=0)
        kd, vd = dup(kw), dup(vw)
        bias = _attn_bias(i)
        zs = (z0, z1, z2, z3)
        pairs = [range(4 * kvh, 4 * kvh + 4) for kvh in range(2)]
        qs = [_stack_heads([rope(q_ref[:, 128 * pr:128 * (pr + 1)], cos_c, sin_c) for pr in pairs[kvh]], left)
              for kvh in range(2)]
        sink = [sinks_ref[kvh * GROUP_ROWS:(kvh + 1) * GROUP_ROWS, :] for kvh in range(2)]
        o = _attn_probs(qs, kd, vd, sink, bias)[2]
        for kvh in range(2):
            for pp, pr in enumerate(pairs[kvh]):
                z = zs[pr // 2][:, 128 * (pr % 2):128 * (pr % 2 + 1)]
                u_ref[:, 128 * pr:128 * (pr + 1)] = (_unstack_heads(o[kvh], left, pp) * _silu(z)).astype(BF16)
        _hosted_finish(phase, p_in, p_out, p_sems, (pl.program_id(0) == B - 1) & (i == nb - 1))

    rowblk = lambda w, cb: pl.BlockSpec((None, ATT_BLOCK, w), lambda b, i: (b, i, cb))
    tab = pl.BlockSpec((ATT_BLOCK, 128), lambda b, i: (i, 0))
    tabp = pl.BlockSpec((ATT_BLOCK, 128), lambda b, i: (jnp.maximum(i - 1, 0), 0))
    p_ispecs, p_ospecs, p_oshapes, p_alias, p_scratch, p_args = _host_phase(phase, 13, 1)
    res = pl.pallas_call(
        body, name=name,
        grid=(B, nb),
        in_specs=[rowblk(ATT_WIDTH, 0), rowblk(256, 4),
                  pl.BlockSpec((None, ATT_BLOCK, 256), lambda b, i: (b, jnp.maximum(i - 1, 0), 4)),
                  rowblk(256, _Z0), rowblk(256, _Z0 + 1), rowblk(256, _Z0 + 2), rowblk(256, _Z0 + 3),
                  tab, tab, tabp, tabp,
                  pl.BlockSpec((ATT_HEADS * ATT_BLOCK, 128), lambda b, i: (0, 0)),
                  pl.BlockSpec(memory_space=pl.ANY)] + p_ispecs,
        out_specs=[pl.BlockSpec((None, ATT_BLOCK, ATT_WIDTH), lambda b, i: (b, i, 1))] + p_ospecs,
        out_shape=[jax.ShapeDtypeStruct(u.shape, BF16)] + p_oshapes,
        input_output_aliases={12: 0, **p_alias},
        scratch_shapes=p_scratch,
        compiler_params=_params(("arbitrary", "arbitrary")),
    )(proj_a, proj_a, proj_a, proj_a, proj_a, proj_a, proj_a, cos, sin, cos, sin, sinks_l, u, *p_args)
    return res[0], list(res[1:])


def _attn_bwd(proj_a, du, sinks_l, cos, sin, name, phase=None):
    B, S, _ = proj_a.shape
    nb = S // ATT_BLOCK

    def body(*refs):
        ins, outs, (carry, sk_acc), p_in, p_out, p_sems = _split_refs(refs, 13, 4, 2, phase)
        q_ref, kvc_ref, kvp_ref, z0, z1, z2, z3, du_ref, cos_ref, sin_ref, cosp_ref, sinp_ref, sinks_ref = ins
        dq_ref, dkv_ref, dz_ref, dsk_ref = outs
        b_id, i = pl.program_id(0), pl.program_id(1)
        _hosted_start(phase, p_in, p_out, p_sems, (b_id == 0) & (i == 0))

        @pl.when((b_id == 0) & (i == 0))
        def _():
            sk_acc[...] = jnp.zeros_like(sk_acc)

        @pl.when(i == 0)
        def _():
            carry[...] = jnp.zeros_like(carry)

        @pl.when(i < nb)
        def _():
            left, rope, rope_bwd, dup = _attn_common()
            cos_c, sin_c = cos_ref[...], sin_ref[...]
            cos_p, sin_p = cosp_ref[...], sinp_ref[...]
            kvc = kvc_ref[...]
            kvp = kvp_ref[...]
            kw = jnp.concatenate([rope(kvp[:, :KV_WIDTH], cos_p, sin_p), rope(kvc[:, :KV_WIDTH], cos_c, sin_c)], axis=0)
            vw = jnp.concatenate([kvp[:, KV_WIDTH:], kvc[:, KV_WIDTH:]], axis=0)
            kd, vd = dup(kw), dup(vw)
            bias = _attn_bias(i)
            zs = (z0, z1, z2, z3)
            units = [(kvh, hf) for kvh in range(2) for hf in range(2)]
            half = GROUP_ROWS // 2
            pairs = [range(4 * kvh + 2 * hf, 4 * kvh + 2 * hf + 2) for kvh, hf in units]
            un = range(len(units))
            qs = [_stack_heads([rope(q_ref[:, 128 * pr:128 * (pr + 1)], cos_c, sin_c) for pr in pairs[j]], left)
                  for j in un]
            sink = [sinks_ref[kvh * GROUP_ROWS + hf * half:kvh * GROUP_ROWS + (hf + 1) * half, :] for kvh, hf in units]
            ku = [kd[kvh] for kvh, _ in units]
            vu = [vd[kvh] for kvh, _ in units]
            def first(j):
                p, ps, o = (r[0] for r in _attn_probs([qs[j]], [ku[j]], [vu[j]], [sink[j]], bias))
                parts = []
                for pp, pr in enumerate(pairs[j]):
                    cols = slice(128 * pr, 128 * (pr + 1))
                    sg, sg_grad = _silu_and_grad(zs[pr // 2][:, 128 * (pr % 2):128 * (pr % 2 + 1)])
                    du128 = du_ref[:, cols]
                    dz_ref[:, cols] = (du128 * _unstack_heads(o, left, pp) * sg_grad).astype(BF16)
                    parts.append(du128 * sg)
                dos = _stack_heads(parts, left)
                dp = _dot(dos, vu[j], NT)
                delta = _row_sums(p * dp)
                ds = (p * (dp - jnp.concatenate([delta, delta], axis=1)) * ATT_SCALE).astype(BF16)
                kvh, hf = units[j]
                sk_acc[kvh, hf * half:(hf + 1) * half, :] += -ps * delta
                return ds, p.astype(BF16), dos.astype(BF16)

            def second(j, ds, p, dos):
                dqs = _dot(ds, ku[j], NN)
                for pp, pr in enumerate(pairs[j]):
                    dq_ref[:, 128 * pr:128 * (pr + 1)] = rope_bwd(_unstack_heads(dqs, left, pp),
                                                                  cos_c, sin_c).astype(BF16)
                return _dot(ds, qs[j], TN), _dot(p, dos, TN)

            got, dku, dvu = {}, [None] * len(units), [None] * len(units)
            for j in range(len(units) + 1):
                if j < len(units):
                    got[j] = first(j)
                if j >= 1:
                    dku[j - 1], dvu[j - 1] = second(j - 1, *got.pop(j - 1))
            dkd = [dku[0] + dku[1], dku[2] + dku[3]]
            dvd = [dvu[0] + dvu[1], dvu[2] + dvu[3]]
            fold = lambda pr: jnp.where(left, pr[0] + pltpu.roll(pr[0], ATT_DIM, 1), pr[1] + pltpu.roll(pr[1], ATT_DIM, 1))
            dkw = fold(dkd)
            dvw = fold(dvd)
            prev = jnp.concatenate([rope_bwd(dkw[:ATT_BLOCK], cos_p, sin_p), dvw[:ATT_BLOCK]], axis=1)
            cur = jnp.concatenate([rope_bwd(dkw[ATT_BLOCK:], cos_c, sin_c), dvw[ATT_BLOCK:]], axis=1)
            dkv_ref[...] = (carry[...] + prev).astype(BF16)
            carry[...] = cur

        @pl.when(i == nb)
        def _():
            dkv_ref[...] = carry[...].astype(BF16)

        @pl.when((b_id == B - 1) & (i == nb))
        def _():
            lane = lax.broadcasted_iota(jnp.int32, (1, 128), 1)
            tot = jnp.zeros((1, 128), F32)
            for hd in range(ATT_HEADS):
                rows = sk_acc[hd // GROUP, (hd % GROUP) * ATT_BLOCK:(hd % GROUP + 1) * ATT_BLOCK, :]
                tot = tot + jnp.where(lane == hd, jnp.sum(rows, axis=0, keepdims=True), 0.0)
            dsk_ref[...] = tot

        _hosted_finish(phase, p_in, p_out, p_sems, (b_id == B - 1) & (i == nb))

    cl = lambda i: jnp.minimum(i, nb - 1)
    pv = lambda i: jnp.maximum(jnp.minimum(i, nb - 1) - 1, 0)
    rowblk = lambda w, cb: pl.BlockSpec((None, ATT_BLOCK, w), lambda b, i: (b, cl(i), cb))
    tab = pl.BlockSpec((ATT_BLOCK, 128), lambda b, i: (cl(i), 0))
    tabp = pl.BlockSpec((ATT_BLOCK, 128), lambda b, i: (pv(i), 0))
    p_ispecs, p_ospecs, p_oshapes, p_alias, p_scratch, p_args = _host_phase(phase, 13, 4)
    res = pl.pallas_call(
        body, name=name,
        grid=(B, nb + 1),
        in_specs=[rowblk(ATT_WIDTH, 0), rowblk(256, 4),
                  pl.BlockSpec((None, ATT_BLOCK, 256), lambda b, i: (b, pv(i), 4)),
                  rowblk(256, _Z0), rowblk(256, _Z0 + 1), rowblk(256, _Z0 + 2), rowblk(256, _Z0 + 3),
                  rowblk(ATT_WIDTH, 1),
                  tab, tab, tabp, tabp,
                  pl.BlockSpec((ATT_HEADS * ATT_BLOCK, 128), lambda b, i: (0, 0))] + p_ispecs,
        out_specs=[rowblk(ATT_WIDTH, 0),
                   pl.BlockSpec((None, ATT_BLOCK, 256), lambda b, i: (b, jnp.maximum(i - 1, 0), 0)),
                   rowblk(ATT_WIDTH, 0),
                   pl.BlockSpec((1, 128), lambda b, i: (0, 0))] + p_ospecs,
        out_shape=[jax.ShapeDtypeStruct((B, S, ATT_WIDTH), BF16), jax.ShapeDtypeStruct((B, S, 256), BF16),
                   jax.ShapeDtypeStruct((B, S, ATT_WIDTH), BF16), jax.ShapeDtypeStruct((1, 128), F32)] + p_oshapes,
        input_output_aliases=p_alias,
        scratch_shapes=[pltpu.VMEM((ATT_BLOCK, 256), F32), pltpu.VMEM((2, GROUP_ROWS, 128), F32)] + p_scratch,
        compiler_params=_params(("arbitrary", "arbitrary")),
    )(proj_a, proj_a, proj_a, proj_a, proj_a, proj_a, proj_a, du, cos, sin, cos, sin, sinks_l, *p_args)
    return tuple(res[:4]) + (list(res[4:]),)


def _outproj_fwd(u2, w_out, x2, g_post, target2, name):
    T, D = x2.shape
    tm = _pick(T, (512, 256, 128))
    last = target2 is not None

    def body(u_ref, w_ref, x_ref, g_ref, *rest):
        y = lax.dot_general(u_ref[...], w_ref[...], (NN, ((), ())), preferred_element_type=F32)
        r = lax.rsqrt(jnp.mean(y * y, axis=-1, keepdims=True) + NORM_EPS)
        xn = x_ref[...] + (y * r) * g_ref[...]
        if last:
            t_ref, y_ref, dx_ref, loss_ref = rest
            err = xn - t_ref[...]
            dx_ref[...] = err * (1.0 / D)
            sq = err * err
            acc = sq[:, 0:128]
            for kk in range(1, D // 128):
                acc = acc + sq[:, 128 * kk:128 * (kk + 1)]
            part = jnp.sum(acc.reshape(tm // 8, 8, 128), axis=0) * (0.5 / D)

            @pl.when(pl.program_id(0) == 0)
            def _():
                loss_ref[...] = jnp.zeros_like(loss_ref)

            loss_ref[...] += part
        else:
            y_ref, xn_ref = rest
            xn_ref[...] = xn
        y_ref[...] = y

    row = pl.BlockSpec((tm, D), lambda i: (i, 0))
    in_specs = [pl.BlockSpec((tm, MIX_WIDTH), lambda i: (i, 0)),
                pl.BlockSpec((MIX_WIDTH, D), lambda i: (0, 0)), row,
                pl.BlockSpec((1, D), lambda i: (0, 0))]
    args = [u2, w_out, x2, g_post]
    out_specs = [row, row]
    out_shape = [jax.ShapeDtypeStruct((T, D), F32), jax.ShapeDtypeStruct((T, D), F32)]
    if last:
        in_specs.append(row)
        args.append(target2)
        out_specs.append(pl.BlockSpec((8, 128), lambda i: (0, 0)))
        out_shape.append(jax.ShapeDtypeStruct((8, 128), F32))
    return pl.pallas_call(
        body, name=name, grid=(T // tm,), in_specs=in_specs, out_specs=out_specs, out_shape=out_shape,
        compiler_params=_params(("arbitrary",)),
    )(*args)


def _outproj_bwd(dxn2, y2, g_post, w_out, name):
    T, D = y2.shape
    N = w_out.shape[0]
    tm = _pick(T, (512, 256, 128))
    nt = T // tm

    def body(dx_ref, y_ref, g_ref, w_ref, dy_ref, dg_ref, du_ref, acc):
        i = pl.program_id(0)

        @pl.when(i == 0)
        def _():
            acc[...] = jnp.zeros_like(acc)

        y = y_ref[...]
        dxn = dx_ref[...]
        r = lax.rsqrt(jnp.mean(y * y, axis=-1, keepdims=True) + NORM_EPS)
        n = y * r
        dn = dxn * g_ref[...]
        dy = (r * (dn - n * jnp.mean(dn * n, axis=-1, keepdims=True))).astype(BF16)
        dy_ref[...] = dy
        du_ref[...] = lax.dot_general(dy, w_ref[...], (NT, ((), ())), preferred_element_type=F32)
        acc[...] += jnp.sum((dxn * n).reshape(tm // 8, 8, D), axis=0)

        @pl.when(i == nt - 1)
        def _():
            dg_ref[...] = jnp.sum(acc[...], axis=0, keepdims=True)

    row = pl.BlockSpec((tm, D), lambda i: (i, 0))
    vec = pl.BlockSpec((1, D), lambda i: (0, 0))
    return pl.pallas_call(
        body, name=name, grid=(nt,),
        in_specs=[row, row, vec, pl.BlockSpec((N, D), lambda i: (0, 0), pipeline_mode=pl.Buffered(1))],
        out_specs=[row, vec, pl.BlockSpec((tm, N), lambda i: (i, 0))],
        out_shape=[jax.ShapeDtypeStruct((T, D), BF16), jax.ShapeDtypeStruct((1, D), F32),
                   jax.ShapeDtypeStruct((T, N), F32)],
        scratch_shapes=[pltpu.VMEM((8, D), F32)],
        compiler_params=_params(("arbitrary",)),
    )(dxn2, y2, g_post, w_out)


def _inproj_bwd(pieces, w_t, x2, dxn2, g_pre, name, phase=None):
    T, D = x2.shape
    widths = [p.shape[1] for p in pieces]
    offs = [sum(widths[:i]) for i in range(len(pieces))]
    n_p = len(pieces)
    tm = _pick(T, (256, 128))
    nt = T // tm

    def body(*refs):
        ins, (dx_ref, dg_ref), (acc,), p_in, p_out, p_sems = _split_refs(refs, n_p + 4, 2, 1, phase)
        w_ref, x_ref, dxn_ref, g_ref = ins[n_p:]
        i = pl.program_id(0)
        _hosted_start(phase, p_in, p_out, p_sems, i == 0)

        @pl.when(i == 0)
        def _():
            acc[...] = jnp.zeros_like(acc)

        dh = jnp.zeros((tm, D), F32)
        for p in range(n_p):
            dh = dh + lax.dot_general(ins[p][...], w_ref[offs[p]:offs[p] + widths[p], :], (NN, ((), ())),
                                      preferred_element_type=F32)
        x = x_ref[...]
        r = lax.rsqrt(jnp.mean(x * x, axis=-1, keepdims=True) + NORM_EPS)
        n = x * r
        dn = dh * g_ref[...]
        dx_ref[...] = dxn_ref[...] + r * (dn - n * jnp.mean(dn * n, axis=-1, keepdims=True))
        acc[...] += jnp.sum((dh * n).reshape(tm // 8, 8, D), axis=0)

        @pl.when(i == nt - 1)
        def _():
            dg_ref[...] = jnp.sum(acc[...], axis=0, keepdims=True)

        _hosted_finish(phase, p_in, p_out, p_sems, i == nt - 1)

    row = pl.BlockSpec((tm, D), lambda i: (i, 0))
    vec = pl.BlockSpec((1, D), lambda i: (0, 0))
    p_ispecs, p_ospecs, p_oshapes, p_alias, p_scratch, p_args = _host_phase(phase, n_p + 4, 2)
    res = pl.pallas_call(
        body, name=name, grid=(nt,),
        in_specs=[pl.BlockSpec((tm, w), lambda i: (i, 0)) for w in widths]
        + [pl.BlockSpec((sum(widths), D), lambda i: (0, 0), pipeline_mode=pl.Buffered(1)), row, row, vec] + p_ispecs,
        out_specs=[row, vec] + p_ospecs,
        out_shape=[jax.ShapeDtypeStruct((T, D), F32), jax.ShapeDtypeStruct((1, D), F32)] + p_oshapes,
        input_output_aliases=p_alias,
        scratch_shapes=[pltpu.VMEM((8, D), F32)] + p_scratch,
        compiler_params=_params(("arbitrary",)),
    )(*pieces, w_t, x2, dxn2, g_pre, *p_args)
    return res[0], res[1], list(res[2:])


def _step(x, target, g_pre, g_post, lb_param, g_head, sinks, shards=None, full=None):
    B, S, D = x.shape
    T = B * S
    dist = shards is not None
    if dist:
        a_loc, b_loc = shards
        ra, rb = a_loc.shape[1], b_loc.shape[1]
        side = _own_side_blocks()
        placed = lambda loc, nm: _place_own(loc, side, "place_" + nm)
        gather = lambda phase, nm: _run_phase(phase, nm)
        w_in0 = gather(_gather_ici_phase([a_loc[0]], [placed(a_loc[0], "in0")]), "gather_in0_ici")
        w_in0 = gather(_gather_d2d_phase(w_in0, [ra]), "gather_in0_d2d")[0]
        late_locs = [a_loc[1], b_loc[1], b_loc[0]]
        late_rs = [ra, rb, rb]
        late_full = [placed(a_loc[1], "in1"), placed(b_loc[1], "out1"), placed(b_loc[0], "out0")]
        w_in, w_out = [w_in0, None], [None, None]
    else:
        w_in, w_out = list(full[0]), list(full[1])
    cos, sin = _rope_tables(S)
    saved = []
    xs = x
    loss_part = None
    dxn = None
    for l in range(DEPTH):
        x2 = xs.reshape(T, D)
        host = dist and l == 0
        proj_h, proj_a, h = _inproj(x2, g_pre[l:l + 1], w_in[l], f"inproj{l}")
        proj_h = proj_h.reshape(B, S, N_H)
        proj_a = proj_a.reshape(B, S, N_A)
        o_h, u, states, got = _hgrn_fwd(proj_h, MIX_WIDTH, lb_param, g_head[l:l + 1], l, f"hgrn_fwd{l}",
                                        _gather_ici_phase(late_locs, late_full) if host else None)
        u, got = _attn_fwd(proj_a, u, _sink_rows(sinks[l]), cos, sin, f"attn_fwd{l}",
                           _gather_d2d_phase(got, late_rs) if host else None)
        if host:
            w_in[1], w_out[1], w_out[0] = got
        u2 = u.reshape(T, MIX_WIDTH)
        if l < DEPTH - 1:
            y, xn = _outproj_fwd(u2, w_out[l], x2, g_post[l:l + 1], None, f"outproj{l}")
            xn = xn.reshape(B, S, D)
        else:
            y, dxn, loss_part = _outproj_fwd(u2, w_out[l], x2, g_post[l:l + 1], target.reshape(T, D), f"outproj{l}")
            xn = None
        saved.append((x2, h, proj_h, proj_a, o_h, u2, states, y))
        xs = xn

    dw_in, dw_out = [None] * DEPTH, [None] * DEPTH
    dg_pre, dg_post, dlb, dg_head, dsinks = [], [], [], [], []
    for l in reversed(range(DEPTH)):
        x2, h, proj_h, proj_a, o_h, u2, states, y = saved[l]
        host = dist and l == 0
        dy, dgp, du = _outproj_bwd(dxn, y, g_post[l:l + 1], w_out[l], f"outproj_bwd{l}")
        du = du.reshape(B, S, MIX_WIDTH)
        dw_out[l] = _mm_tn([u2], dy, f"wgrad_out{l}")
        if host:
            early = [dw_in[1], dw_out[1], dw_out[0]]
        dqh, dfh, dih, dzh, dlb_l, dgh, got = _hgrn_bwd(
            proj_h, o_h, du, states, lb_param, g_head[l:l + 1], l, f"hgrn_bwd{l}",
            _reduce_d2d_phase(early, late_rs) if host else None)
        if host:
            parts = [_pair_sum(g, r, side, f"pair_sum{i}") for i, (g, r) in enumerate(zip(early, got))]
        dqa, dkv, dza, dsk, got = _attn_bwd(proj_a, du, _sink_rows(sinks[l]), cos, sin, f"attn_bwd{l}",
                                            _reduce_ici_phase(parts) if host else None)
        if host:
            sum_in = _chip_sum(parts[0], got[0], "chip_sum_in1", 1)
            sum_out = _chip_sum(parts[2], got[2], "chip_sum_out0", 0,
                                _chip_sum(parts[1], got[1], "chip_sum_out1", 1))
        dproj = [p.reshape(T, p.shape[-1]) for p in (dqh, dfh, dih, dzh, dqa, dkv, dza)]
        dw_in[l] = _mm_tn(dproj, h, f"wgrad_in{l}")
        if host:
            got = _run_phase(_reduce_d2d_phase([dw_in[0]], [ra]), "reduce_in0_d2d")
            part = _pair_sum(dw_in[0], got[0], side, "pair_sum_in0")
        dxn, dgpre, got = _inproj_bwd(dproj, w_in[l], x2, dxn, g_pre[l:l + 1], f"inproj_bwd{l}",
                                      _reduce_ici_phase([part]) if host else None)
        if host:
            sum_in = _chip_sum(part, got[0], "chip_sum_in0", 0, sum_in)
        dg_pre.append(dgpre)
        dg_post.append(dgp)
        dlb.append(dlb_l)
        dg_head.append(dgh)
        dsinks.append(dsk)
    rev = lambda lst: jnp.concatenate(lst[::-1], axis=0)
    if not dist:
        sum_in, sum_out = jnp.stack(dw_in), jnp.stack(dw_out)
    return (loss_part, dxn.reshape(B, S, D), sum_in, sum_out,
            rev(dg_pre), rev(dg_post), rev(dlb), rev(dg_head), rev(dsinks))


def _me_and_peers():
    x, y, c = lax.axis_index("x"), lax.axis_index("y"), lax.axis_index("c")
    me = 4 * x + 2 * y + c
    peers = []
    for k in range(1, N_DEV):
        px = 1 - x if k & 4 else x
        py = 1 - y if k & 2 else y
        pc = 1 - c if k & 1 else c
        peers.append(((px, py, pc), 4 * px + 2 * py + pc))
    return me, peers


class _Phase:
    def __init__(self, arrays, out_shapes, aliases, n_send, build):
        self.arrays, self.out_shapes, self.aliases = list(arrays), list(out_shapes), dict(aliases)
        self.n_send, self.build = n_send, build

    def scratch(self):
        return [pltpu.SemaphoreType.DMA((self.n_send,)), pltpu.SemaphoreType.DMA((self.n_send,))]

    def _copies(self, in_refs, out_refs, sems, arrivals):
        send_sems, recv_sems = sems
        sends, recvs = self.build(in_refs, out_refs)
        assert len(sends) == self.n_send == len(recvs)
        out = [pltpu.make_async_remote_copy(src_ref=s, dst_ref=d, send_sem=send_sems.at[i], recv_sem=recv_sems.at[i],
                                            device_id=dev, device_id_type=MESH) for i, (s, d, dev) in enumerate(sends)]
        inc = [pltpu.make_async_remote_copy(src_ref=s, dst_ref=r, send_sem=send_sems.at[i], recv_sem=recv_sems.at[i],
                                            device_id=dev, device_id_type=MESH)
               for i, ((s, _, dev), r) in enumerate(zip(sends, recvs))] if arrivals else []
        return out, inc

    def start(self, in_refs, out_refs, sems):
        out, _ = self._copies(in_refs, out_refs, sems, False)
        for cp in out:
            cp.start()

    def finish(self, in_refs, out_refs, sems):
        out, inc = self._copies(in_refs, out_refs, sems, True)
        for cp in inc:
            cp.wait_recv()
        for cp in out:
            cp.wait_send()


_ANY = pl.BlockSpec(memory_space=pl.ANY)


def _host_phase(phase, n_in, n_out):
    if phase is None:
        return [], [], [], {}, [], []
    aliases = {n_in + i: n_out + o for i, o in phase.aliases.items()}
    return ([_ANY] * len(phase.arrays), [_ANY] * len(phase.out_shapes), phase.out_shapes, aliases, phase.scratch(),
            phase.arrays)


def _split_refs(refs, n_in, n_out, n_scr, phase):
    pi = len(phase.arrays) if phase else 0
    po = len(phase.out_shapes) if phase else 0
    a = n_in + pi
    b = a + n_out + po
    return (refs[:n_in], refs[a:a + n_out], refs[b:b + n_scr], refs[n_in:a], refs[a + n_out:b], refs[b + n_scr:])


def _hosted_start(phase, p_in, p_out, p_sems, first):
    if phase is not None:
        @pl.when(first)
        def _():
            phase.start(p_in, p_out, p_sems)


def _hosted_finish(phase, p_in, p_out, p_sems, last):
    if phase is not None:
        @pl.when(last)
        def _():
            phase.finish(p_in, p_out, p_sems)


def _run_phase(phase, name):
    n_in, n_out = len(phase.arrays), len(phase.out_shapes)

    def body(*refs):
        phase.start(refs[:n_in], refs[n_in:n_in + n_out], refs[n_in + n_out:])
        phase.finish(refs[:n_in], refs[n_in:n_in + n_out], refs[n_in + n_out:])

    return pl.pallas_call(
        body, name=name, in_specs=[_ANY] * n_in, out_specs=[_ANY] * n_out,
        out_shape=phase.out_shapes, input_output_aliases=phase.aliases, scratch_shapes=phase.scratch(),
        compiler_params=pltpu.CompilerParams(has_side_effects=True),
    )(*phase.arrays)


def _mesh_place():
    x, y, c = lax.axis_index("x"), lax.axis_index("y"), lax.axis_index("c")
    chips = [(x, y), (1 - x, y), (x, 1 - y), (1 - x, 1 - y)]
    num = lambda chip, core: 4 * chip[0] + 2 * chip[1] + core
    return c, chips, num


def _own_side_blocks():
    c, chips, num = _mesh_place()
    return jnp.stack([num(ch, c) for ch in chips]).astype(jnp.int32)


def _rows(ref, r, dev):
    return ref.at[pl.ds(pl.multiple_of(dev * r, 16), r), :]


def _place_own(loc, blocks, name):
    r, D = loc.shape
    tr = _pick(r, (400, 256, 200, 128, 64, 16))

    def body(idx_ref, l_ref, o_ref):
        del idx_ref
        o_ref[...] = l_ref[...]

    return pl.pallas_call(
        body, name=name,
        grid_spec=pltpu.PrefetchScalarGridSpec(
            num_scalar_prefetch=1, grid=(r // tr,),
            in_specs=[pl.BlockSpec((tr, D), lambda i, idx: (i, 0))],
            out_specs=pl.BlockSpec((tr, D), lambda i, idx: (idx[0] * (r // tr) + i, 0))),
        out_shape=jax.ShapeDtypeStruct((N_DEV * r, D), loc.dtype),
        compiler_params=_params(("arbitrary",)),
    )(blocks, loc)


def _gather_ici_phase(locs, fulls):
    rs = [a.shape[0] for a in locs]
    n = len(locs)

    def build(ins, outs):
        c, chips, num = _mesh_place()
        me = num(chips[0], c)
        targets = [((*chips[0], 1 - c), num(chips[0], 1 - c))] + [((*ch, c), num(ch, c)) for ch in chips[1:]]
        sends, recvs = [], []
        for dev, dnum in targets:
            for i, r in enumerate(rs):
                sends.append((ins[i], _rows(outs[i], r, me), dev))
                recvs.append(_rows(outs[i], r, dnum))
        return sends, recvs

    shapes = [jax.ShapeDtypeStruct(a.shape, a.dtype) for a in fulls]
    return _Phase(list(locs) + list(fulls), shapes, {n + i: i for i in range(n)}, 4 * n, build)


def _gather_d2d_phase(fulls, rs):
    def build(ins, outs):
        c, chips, num = _mesh_place()
        sib = (*chips[0], 1 - c)
        sends, recvs = [], []
        for ch in chips[1:]:
            for i, r in enumerate(rs):
                blk = _rows(outs[i], r, num(ch, c))
                sends.append((blk, blk, sib))
                recvs.append(_rows(outs[i], r, num(ch, 1 - c)))
        return sends, recvs

    shapes = [jax.ShapeDtypeStruct(a.shape, a.dtype) for a in fulls]
    return _Phase(fulls, shapes, {i: i for i in range(len(fulls))}, 3 * len(fulls), build)


def _reduce_d2d_phase(grads, rs):
    def build(ins, outs):
        c, chips, num = _mesh_place()
        sib = (*chips[0], 1 - c)
        sends, recvs = [], []
        for j, ch in enumerate(chips):
            for i, r in enumerate(rs):
                sends.append((_rows(ins[i], r, num(ch, 1 - c)), outs[i].at[j], sib))
                recvs.append(outs[i].at[j])
        return sends, recvs

    shapes = [jax.ShapeDtypeStruct((4, r, g.shape[1]), g.dtype) for g, r in zip(grads, rs)]
    return _Phase(grads, shapes, {}, 4 * len(grads), build)


def _reduce_ici_phase(parts):
    def build(ins, outs):
        c, chips, _ = _mesh_place()
        sends, recvs = [], []
        for t in range(1, 4):
            for i in range(len(parts)):
                sends.append((ins[i].at[t], outs[i].at[t - 1], (*chips[t], c)))
                recvs.append(outs[i].at[t - 1])
        return sends, recvs

    shapes = [jax.ShapeDtypeStruct((3,) + p.shape[1:], p.dtype) for p in parts]
    return _Phase(parts, shapes, {}, 3 * len(parts), build)


def _pair_sum(g, got, blocks, name):
    n, r, D = got.shape
    tr = _pick(r, (400, 256, 200, 128, 64, 16))

    def body(idx_ref, g_ref, r_ref, o_ref):
        del idx_ref
        o_ref[...] = (g_ref[...].astype(F32) + r_ref[...].astype(F32)).astype(o_ref.dtype)

    blk = pl.BlockSpec((None, tr, D), lambda j, i, idx: (j, i, 0))
    return pl.pallas_call(
        body, name=name,
        grid_spec=pltpu.PrefetchScalarGridSpec(
            num_scalar_prefetch=1, grid=(n, r // tr),
            in_specs=[pl.BlockSpec((tr, D), lambda j, i, idx: (idx[j] * (r // tr) + i, 0)), blk],
            out_specs=blk),
        out_shape=jax.ShapeDtypeStruct(got.shape, got.dtype),
        compiler_params=_params(("arbitrary", "arbitrary")),
    )(blocks, g, got)


def _chip_sum(p, r, name, layer, into=None):
    _, R, D = p.shape
    tr = _pick(R, (400, 256, 200, 128, 64, 16))

    def body(p_ref, r_ref, *rest):
        acc = p_ref[...].astype(F32)
        for t in range(3):
            acc = acc + r_ref[t].astype(F32)
        rest[-1][...] = acc

    args = [p, r] + ([] if into is None else [into])
    return pl.pallas_call(
        body, name=name, grid=(R // tr,),
        in_specs=[pl.BlockSpec((None, tr, D), lambda i: (0, i, 0)), pl.BlockSpec((3, tr, D), lambda i: (0, i, 0))]
        + ([] if into is None else [_ANY]),
        out_specs=pl.BlockSpec((None, tr, D), lambda i: (layer, i, 0)),
        out_shape=jax.ShapeDtypeStruct((DEPTH, R, D), F32),
        input_output_aliases={} if into is None else {2: 0},
        compiler_params=_params(("parallel",)))(*args)


def _allreduce_small(vec):
    R, C = vec.shape

    def body(v_ref, o_ref, buf, send_sems, recv_sems):
        me, peers = _me_and_peers()
        buf[me] = v_ref[...]
        sends = []
        for k, (pid, _) in enumerate(peers):
            cp = pltpu.make_async_remote_copy(src_ref=v_ref, dst_ref=buf.at[me], send_sem=send_sems.at[k],
                                              recv_sem=recv_sems.at[k], device_id=pid, device_id_type=MESH)
            cp.start()
            sends.append(cp)
        for k, (pid, pnum) in enumerate(peers):
            pltpu.make_async_remote_copy(src_ref=v_ref, dst_ref=buf.at[pnum], send_sem=send_sems.at[k],
                                         recv_sem=recv_sems.at[k], device_id=pid, device_id_type=MESH).wait_recv()
        for cp in sends:
            cp.wait_send()
        acc = buf[0]
        for d in range(1, N_DEV):
            acc = acc + buf[d]
        o_ref[...] = acc

    vm = pl.BlockSpec(memory_space=pltpu.VMEM)
    return pl.pallas_call(
        body, name="allreduce_small",
        in_specs=[vm], out_specs=vm,
        out_shape=jax.ShapeDtypeStruct((R, C), F32),
        scratch_shapes=[pltpu.VMEM((N_DEV, R, C), F32), pltpu.SemaphoreType.DMA((N_DEV - 1,)),
                        pltpu.SemaphoreType.DMA((N_DEV - 1,))],
        compiler_params=pltpu.CompilerParams(has_side_effects=True),
    )(vec)


def _adamw(w, g, m, v, name):
    R, C = w.shape
    tr = _pick(R, (512, 400, 256, 128, 64, 32, 16, 8)) if R >= 8 else R
    c1 = 1.0 - ADAM_B1 ** ADAM_STEP
    c2 = 1.0 - ADAM_B2 ** ADAM_STEP

    def body(w_ref, g_ref, m_ref, v_ref, d_ref, mo_ref, vo_ref):
        gg = g_ref[...]
        mn = ADAM_B1 * m_ref[...] + (1.0 - ADAM_B1) * gg
        vn = ADAM_B2 * v_ref[...] + (1.0 - ADAM_B2) * (gg * gg)
        d_ref[...] = -ADAM_LR * ((mn / c1) / (jnp.sqrt(vn / c2) + ADAM_EPS) + ADAM_WD * w_ref[...])
        mo_ref[...] = mn
        vo_ref[...] = vn

    blk = pl.BlockSpec((tr, C), lambda i: (i, 0))
    sh = jax.ShapeDtypeStruct((R, C), F32)
    return pl.pallas_call(
        body, name=name, grid=(R // tr,), in_specs=[blk] * 4, out_specs=[blk] * 3, out_shape=[sh] * 3,
        compiler_params=_params(("parallel",)),
    )(w, g, m, v)


def _lb_param_grad(lb_param, dlb):
    L, C = lb_param.shape

    def body(p_ref, d_ref, o_ref):
        lbp = p_ref[...]
        d = d_ref[...]
        mx = jnp.max(lbp, axis=0, keepdims=True)
        e = jnp.exp(lbp - mx)
        p = e / jnp.sum(e, axis=0, keepdims=True)
        tot = jnp.sum(d, axis=0, keepdims=True)
        dps = []
        rest = tot
        for j in range(L):
            dps.append(rest - tot if j == 0 else rest)
            rest = rest - d[j:j + 1]
        dp = jnp.concatenate(dps, axis=0)
        o_ref[...] = p * (dp - jnp.sum(p * dp, axis=0, keepdims=True))

    vm = pl.BlockSpec(memory_space=pltpu.VMEM)
    return pl.pallas_call(body, name="lb_param_grad", in_specs=[vm, vm], out_specs=vm,
                          out_shape=jax.ShapeDtypeStruct((L, C), F32))(lb_param, dlb)


def _pack_small(loss_part, dg_pre, dg_post, dlb, dg_head, dsinks):
    pad8 = lambda a: jnp.pad(a.reshape(-1, 128), ((0, 8 - DEPTH), (0, 0)))
    rows = [dg_pre.reshape(-1, 128), dg_post.reshape(-1, 128), dlb.reshape(-1, 128), pad8(dg_head), pad8(dsinks),
            loss_part]
    return jnp.concatenate(rows, axis=0)


def _unpack_small(vec):
    n = DEPTH * D_MODEL // 128
    o = 0
    dg_pre = vec[o:o + n].reshape(DEPTH, D_MODEL); o += n
    dg_post = vec[o:o + n].reshape(DEPTH, D_MODEL); o += n
    dlb = vec[o:o + n].reshape(DEPTH, HG_WIDTH); o += n
    dg_head = vec[o:o + DEPTH]; o += 8
    dsinks = vec[o:o + DEPTH, :ATT_HEADS]; o += 8
    loss = jnp.sum(vec[o:o + 8])
    return loss, dg_pre, dg_post, dlb, dg_head, dsinks


def kernel(x, w_in, w_out, g_pre, g_post, lb_param, g_head, sinks, loss_target, m_w_in, m_w_out, m_g_pre, m_g_post, m_lb_param, m_g_head, m_sinks, v_w_in, v_w_out, v_g_pre, v_g_post, v_lb_param, v_g_head, v_sinks):
    tr = lambda a: jnp.swapaxes(a, 1, 2)
    w_in_t = tr(w_in)
    (loss_part, dx, gw_in_t, gw_out, dg_pre, dg_post, dlb, dg_head, dsinks) = _step(
        x, loss_target, g_pre, g_post, lb_param, g_head, sinks, shards=(w_in_t.astype(BF16), w_out.astype(BF16)))

    small = _allreduce_small(_pack_small(loss_part, dg_pre, dg_post, dlb, dg_head, dsinks))
    loss, gg_pre, gg_post, gdlb, gg_head, gsinks = _unpack_small(small)
    glb = _lb_param_grad(lb_param, gdlb)

    grads = [gw_in_t, gw_out, gg_pre, gg_post, glb, gg_head, gsinks]
    ws = [w_in_t, w_out, g_pre, g_post, lb_param, g_head, sinks]
    ms = [tr(m_w_in), m_w_out, m_g_pre, m_g_post, m_lb_param, m_g_head, m_sinks]
    vs = [tr(v_w_in), v_w_out, v_g_pre, v_g_post, v_lb_param, v_g_head, v_sinks]
    names = ["w_in", "w_out", "g_pre", "g_post", "lb_param", "g_head", "sinks"]
    deltas, new_m, new_v = [], [], []
    for w, g, m, v, nm in zip(ws, grads, ms, vs, names):
        sh = w.shape
        two = lambda a: a.reshape(-1, sh[-1])
        d, mn, vn = _adamw(two(w), two(g), two(m), two(v), "adamw_" + nm)
        deltas.append(d.reshape(sh))
        new_m.append(mn.reshape(sh))
        new_v.append(vn.reshape(sh))
    grads[0], deltas[0], new_m[0], new_v[0] = tr(grads[0]), tr(deltas[0]), tr(new_m[0]), tr(new_v[0])
    return (loss, dx, *grads, *deltas, *new_m, *new_v)
```

```python
import functools
import math

import numpy as np
import jax
import jax.numpy as jnp
from jax import lax
from jax.experimental import pallas as pl
from jax.experimental.pallas import tpu as pltpu

F32 = jnp.float32
BF16 = jnp.bfloat16

D_MODEL = 1024
DEPTH = 2
HG_HEADS = 8
HG_DIM = 128
HG_WIDTH = HG_HEADS * HG_DIM
CHUNK = 64
ATT_HEADS = 16
ATT_DIM = 64
ATT_WIDTH = ATT_HEADS * ATT_DIM
KV_WIDTH = 128
ATT_BLOCK = 128
ATT_SCALE = 1.0 / math.sqrt(ATT_DIM)
ROPE_THETA = 10000.0
NORM_EPS = 1e-6
NEG_INF = -1e30
LB_FLOOR = 1e-20
N_H = 4 * HG_WIDTH
N_A = 2 * ATT_WIDTH + 2 * KV_WIDTH
IN_WIDTH = N_H + N_A
MIX_WIDTH = HG_WIDTH + ATT_WIDTH

ADAM_LR = 0.001
ADAM_B1 = 0.9
ADAM_B2 = 0.999
ADAM_EPS = 1e-08
ADAM_WD = 0.01
ADAM_STEP = 10

N_DEV = 8
MESH = pl.DeviceIdType.MESH
VMEM_LIMIT = 56 * 1024 * 1024

NN = ((1,), (0,))
NT = ((1,), (1,))
TN = ((0,), (0,))


def _dot(a, b, dims):
    return lax.dot_general(a.astype(BF16), b.astype(BF16), (dims, ((), ())), preferred_element_type=F32)


def _params(sem=None, **kw):
    return pltpu.CompilerParams(dimension_semantics=sem, vmem_limit_bytes=VMEM_LIMIT, **kw)


def _sigmoids(x):
    e = jnp.exp(-jnp.abs(x))
    r = 1.0 / (1.0 + e)
    er = e * r
    pos = x >= 0.0
    return jnp.where(pos, r, er), jnp.where(pos, er, r)


def _silu(x):
    return x * _sigmoids(x)[0]


def _silu_and_grad(x):
    s, ns = _sigmoids(x)
    return x * s, s * (1.0 + x * ns)


def _pick(n, prefs):
    for p in prefs:
        if n % p == 0:
            return p
    return n


def _inproj(x2, g, w, name):
    T, D = x2.shape
    tm = _pick(T, (256, 128))
    nchunk = 1024

    def body(x_ref, g_ref, w_ref, oh_ref, oa_ref, h_ref):
        x = x_ref[...]
        r = lax.rsqrt(jnp.mean(x * x, axis=-1, keepdims=True) + NORM_EPS)
        h = ((x * r) * g_ref[...]).astype(BF16)
        h_ref[...] = h
        for j in range(0, N_H, nchunk):
            oh_ref[:, j:j + nchunk] = lax.dot_general(h, w_ref[j:j + nchunk, :], (NT, ((), ())),
                                                      preferred_element_type=F32)
        for j in range(0, N_A, N_A // 2):
            oa_ref[:, j:j + N_A // 2] = lax.dot_general(h, w_ref[N_H + j:N_H + j + N_A // 2, :], (NT, ((), ())),
                                                        preferred_element_type=F32)

    row = lambda w_: pl.BlockSpec((tm, w_), lambda i: (i, 0))
    return pl.pallas_call(
        body, name=name,
        grid=(T // tm,),
        in_specs=[row(D), pl.BlockSpec((1, D), lambda i: (0, 0)),
                  pl.BlockSpec((IN_WIDTH, D), lambda i: (0, 0), pipeline_mode=pl.Buffered(1))],
        out_specs=[row(N_H), row(N_A), row(D)],
        out_shape=[jax.ShapeDtypeStruct((T, N_H), F32), jax.ShapeDtypeStruct((T, N_A), F32),
                   jax.ShapeDtypeStruct((T, D), BF16)],
        compiler_params=_params(("parallel",)),
    )(x2, g, w)


def _mm_tn(pieces, b, name, out_dtype=BF16):
    T, m = b.shape
    tn = 256
    counts = [p.shape[1] // tn for p in pieces]
    starts = [sum(counts[:i]) for i in range(len(pieces))]
    n_p = len(pieces)

    def body(*refs):
        b_ref, o_ref = refs[n_p], refs[n_p + 1]
        i = pl.program_id(0)
        for p in range(n_p):
            @pl.when((i >= starts[p]) & (i < starts[p] + counts[p]))
            def _(p=p):
                o_ref[...] = lax.dot_general(refs[p][...], b_ref[...], (TN, ((), ())),
                                             preferred_element_type=F32).astype(out_dtype)

    piece_spec = lambda s, c: pl.BlockSpec((T, tn), lambda i: (0, jnp.clip(i - s, 0, c - 1)))
    return pl.pallas_call(
        body, name=name,
        grid=(sum(counts),),
        in_specs=[piece_spec(s, c) for s, c in zip(starts, counts)]
        + [pl.BlockSpec((T, m), lambda i: (0, 0), pipeline_mode=pl.Buffered(1))],
        out_specs=pl.BlockSpec((tn, m), lambda i: (i, 0)),
        out_shape=jax.ShapeDtypeStruct((sum(counts) * tn, m), out_dtype),
        compiler_params=_params(("arbitrary",)),
    )(*pieces, b)


_LEVELS = (0, 1, 2, 4, 8, 16, 32)
_CUM_L = (2, 4, 8, 16, 32, 64)
_ALL_KINDS = tuple(("c", L) for L in _CUM_L) + tuple(("r", L) for L in _CUM_L)
_MXU_KINDS = (("c", 2), ("c", 4), ("c", CHUNK), ("r", 2), ("r", 4))
N_CUM = len(_ALL_KINDS) * CHUNK
N_CUM_F = len(_MXU_KINDS) * CHUNK


def _cum_matrices():
    t = np.arange(CHUNK)[:, None]
    r = np.arange(CHUNK)[None, :]

    def mat(kind):
        c, L = kind
        return ((r // L == t // L) & ((r <= t) if c == "c" else (r > t))).astype(np.float32)

    fwd = np.concatenate([mat(kd) for kd in _MXU_KINDS], axis=0)
    full = np.concatenate([mat(kd) for kd in _ALL_KINDS], axis=0)
    return jnp.asarray(fwd, BF16), jnp.asarray(full.T.copy(), BF16)


def _level_masks():
    t = np.arange(CHUNK)[:, None]
    s = np.arange(CHUNK)[None, :]
    ms = []
    for L in _LEVELS:
        if L == 0:
            ms.append(t == s)
        else:
            ms.append((t // (2 * L) == s // (2 * L)) & ((t // L) % 2 == 1) & ((s // L) % 2 == 0))
    return jnp.asarray(np.stack(ms).astype(np.float32))


def _split3(x):
    hi = x.astype(BF16)
    r1 = x - hi.astype(F32)
    mid = r1.astype(BF16)
    lo = (r1 - mid.astype(F32)).astype(BF16)
    return hi, mid, lo


def _cum3(ts, x, terms=3):
    d = lambda p: lax.dot_general(ts, p, (NN, ((), ())), preferred_element_type=F32)
    return sum(d(p) for p in _split3(x)[:terms])


def _lb_terms(lbp, layer):
    mx = jnp.max(lbp, axis=0, keepdims=True)
    e = jnp.exp(lbp - mx)
    p = e / jnp.sum(e, axis=0, keepdims=True)
    cum = p[0:1]
    for j in range(1, layer + 1):
        cum = cum + p[j:j + 1]
    lb = cum - p[0:1]
    lbf = jnp.maximum(lb, LB_FLOOR)
    return dict(lbf=lbf, one_m=1.0 - lb, kcorr=lb - lbf, ind=jnp.where(lb > LB_FLOOR, 1.0, 0.0))


def _gate(x, lt):
    sig, nsig = _sigmoids(x)
    f = lt["lbf"] + lt["one_m"] * sig
    return jnp.log(f), lt["one_m"] * nsig + lt["kcorr"], f, sig, nsig


def _ck(x, ci):
    return x[ci * CHUNK:(ci + 1) * CHUNK]


def _block_cums(ts, g, nc):
    cs = [_cum3(ts, _ck(g, ci)) for ci in range(nc)]
    out = {kind: jnp.concatenate([c[CHUNK * i:CHUNK * (i + 1)] for c in cs], axis=0)
           for i, kind in enumerate(_MXU_KINDS)}
    b = out[("c", CHUNK)]
    ng = CHUNK // 8
    last = b.reshape(nc, ng, 8, HG_DIM)[:, :, 7:8, :]
    zero = jnp.zeros((nc, 1, 1, HG_DIM), F32)

    def spread(groups):
        return jnp.broadcast_to(jnp.concatenate(groups, axis=1), (nc, ng, 8, HG_DIM)).reshape(nc * CHUNK, HG_DIM)

    def get(kind):
        if kind in out:
            return out[kind]
        c, L = kind
        nb = L // 8
        first = lambda r: (r // nb) * nb
        if c == "c":
            return b - spread([last[:, first(r) - 1:first(r)] if r >= nb else zero for r in range(ng)])
        return spread([last[:, first(r) + nb - 1:first(r) + nb] for r in range(ng)]) - b

    return get


def _level_factors(cums, g, L):
    if L == 0:
        return None, None
    if L == 1:
        return jnp.exp(g), None
    return jnp.exp(cums(("c", L))), jnp.exp(cums(("r", L)))


def _mul(a, e):
    return a if e is None else a * e


def _hg_block_fwd(qf, k, v, g, ts, m_ref, nc):
    cums = _block_cums(ts, g, nc)
    amat = [jnp.zeros((CHUNK, CHUNK), F32)] * nc
    for li, L in enumerate(_LEVELS):
        eq, ek = _level_factors(cums, g, L)
        ql, kl, m = _mul(qf, eq), _mul(k, ek), m_ref[li]
        amat = [amat[ci] + _dot(_ck(ql, ci), _ck(kl, ci), NT) * m for ci in range(nc)]
    b = cums(("c", CHUNK))
    kst = k * jnp.exp(cums(("r", CHUNK)))
    o = [_dot(amat[ci], _ck(v, ci), NN) for ci in range(nc)]
    kv = [_dot(_ck(v, ci), _ck(kst, ci), TN) for ci in range(nc)]
    dec = [jnp.exp(b[(ci + 1) * CHUNK - 1:(ci + 1) * CHUNK, :]) for ci in range(nc)]
    return o, dec, kv, qf * jnp.exp(b)


def _hg_block_bwd(qf, k, v, g, do, ts, m_ref, nc):
    cums = _block_cums(ts, g, nc)
    dcs = {}
    da = [_dot(_ck(do, ci), _ck(v, ci), NT) for ci in range(nc)]
    dq = jnp.zeros_like(qf)
    dk = jnp.zeros_like(qf)
    dg = jnp.zeros_like(qf)
    amat = [jnp.zeros((CHUNK, CHUNK), F32)] * nc
    for li, L in enumerate(_LEVELS):
        eq, ek = _level_factors(cums, g, L)
        ql, kl, m = _mul(qf, eq), _mul(k, ek), m_ref[li]
        qlb, klb = ql.astype(BF16), kl.astype(BF16)
        amat = [amat[ci] + _dot(_ck(qlb, ci), _ck(klb, ci), NT) * m for ci in range(nc)]
        dal = [(da[ci] * m).astype(BF16) for ci in range(nc)]
        dql = jnp.concatenate([_dot(dal[ci], _ck(klb, ci), NN) for ci in range(nc)], axis=0)
        dkl = jnp.concatenate([_dot(dal[ci], _ck(qlb, ci), TN) for ci in range(nc)], axis=0)
        dq = dq + _mul(dql, eq)
        dk = dk + _mul(dkl, ek)
        if L == 1:
            dg = dg + dql * ql
        elif L > 1:
            dcs[("c", L)] = (dql * ql).astype(BF16)
            dcs[("r", L)] = (dkl * kl).astype(BF16)
    b = cums(("c", CHUNK))
    e64 = jnp.exp(b)
    er64 = jnp.exp(cums(("r", CHUNK)))
    qb = qf * e64
    return dict(dq=dq, dk=dk, dg=dg, dcs=dcs, e64=e64, er64=er64, qb=qb, kst=k * er64,
                dv=[_dot(amat[ci], _ck(do, ci), TN) for ci in range(nc)],
                dec=[jnp.exp(b[(ci + 1) * CHUNK - 1:(ci + 1) * CHUNK, :]) for ci in range(nc)],
                qd=[_dot(_ck(do, ci), _ck(qb, ci), TN) for ci in range(nc)])


def _hg_state_bwd(w, v, do, starts, ends, tst, nc):
    dqb = jnp.concatenate([_dot(_ck(do, ci), starts[ci], NN) for ci in range(nc)], axis=0)
    dkst = jnp.concatenate([_dot(_ck(v, ci), ends[ci], NN) for ci in range(nc)], axis=0)
    dq = w["dq"] + dqb * w["e64"]
    dk = w["dk"] + dkst * w["er64"]
    dv = jnp.concatenate([w["dv"][ci] + _dot(_ck(w["kst"], ci), ends[ci], NT) for ci in range(nc)], axis=0)
    trow = lax.broadcasted_iota(jnp.int32, (CHUNK, 1), 0)
    dtot = jnp.concatenate(
        [jnp.where(trow == CHUNK - 1, jnp.sum(ends[ci] * starts[ci], axis=0, keepdims=True) * w["dec"][ci], 0.0)
         for ci in range(nc)], axis=0)
    dcs = dict(w["dcs"])
    dcs[("c", CHUNK)] = (dqb * w["qb"] + dtot).astype(BF16)
    dcs[("r", CHUNK)] = (dkst * w["kst"]).astype(BF16)
    dgs = [_dot(tst, jnp.concatenate([_ck(dcs[kind], ci) for kind in _ALL_KINDS], axis=0), NN) for ci in range(nc)]
    return dq, dk, dv, w["dg"] + jnp.concatenate(dgs, axis=0)


def _hgrn_fwd(proj_h, u_rows, lb_param, g_head, layer, name, phase=None):
    B, S, _ = proj_h.shape
    sb = _pick(S, (1024, 512, 256, 128, 64))
    nc = sb // CHUNK
    ts, _ = _cum_matrices()

    def body(*refs):
        ins, outs, (st,), p_in, p_out, p_sems = _split_refs(refs, 8, 3, 1, phase)
        q_ref, f_ref, i_ref, z_ref, lbp_ref, gh_ref, ts_ref, m_ref = ins
        o_ref, u_ref, sts_ref = outs
        h_id, b_id, s_id = pl.program_id(0), pl.program_id(1), pl.program_id(2)
        _hosted_start(phase, p_in, p_out, p_sems, (h_id == 0) & (b_id == 0) & (s_id == 0))

        @pl.when(s_id == 0)
        def _():
            st[...] = jnp.zeros_like(st)

        lt = _lb_terms(lbp_ref[...], layer)
        tsv = ts_ref[...]
        gh = gh_ref[...]
        logf, k = _gate(f_ref[...], lt)[:2]
        o_part, dec, kv, qb = _hg_block_fwd(_silu(q_ref[...]), k, i_ref[...], logf, tsv, m_ref, nc)
        cur = st[...]
        starts = []
        for ci in range(nc):
            sts_ref[ci] = cur
            starts.append(cur)
            cur = cur * dec[ci] + kv[ci]
        st[...] = cur
        o = jnp.concatenate([o_part[ci] + _dot(_ck(qb, ci), starts[ci], NT) for ci in range(nc)], axis=0)
        o_ref[...] = o
        r = lax.rsqrt(jnp.mean(o * o, axis=-1, keepdims=True) + NORM_EPS)
        u_ref[...] = (((o * r) * gh) * _silu(z_ref[...])).astype(BF16)
        _hosted_finish(phase, p_in, p_out, p_sems, (h_id == HG_HEADS - 1) & (b_id == B - 1) & (s_id == S // sb - 1))

    col = lambda base: pl.BlockSpec((None, sb, HG_DIM), lambda h, b, s: (b, s, base + h))
    p_ispecs, p_ospecs, p_oshapes, p_alias, p_scratch, p_args = _host_phase(phase, 8, 3)
    res = pl.pallas_call(
        body, name=name,
        grid=(HG_HEADS, B, S // sb),
        in_specs=[col(0), col(HG_HEADS), col(2 * HG_HEADS), col(3 * HG_HEADS),
                  pl.BlockSpec((DEPTH, HG_DIM), lambda h, b, s: (0, h)),
                  pl.BlockSpec((1, HG_DIM), lambda h, b, s: (0, 0)),
                  pl.BlockSpec((N_CUM_F, CHUNK), lambda h, b, s: (0, 0)),
                  pl.BlockSpec((len(_LEVELS), CHUNK, CHUNK), lambda h, b, s: (0, 0, 0))] + p_ispecs,
        out_specs=[col(0), col(0),
                   pl.BlockSpec((None, None, nc, HG_DIM, HG_DIM), lambda h, b, s: (b, h, s, 0, 0))] + p_ospecs,
        out_shape=[jax.ShapeDtypeStruct((B, S, HG_WIDTH), F32),
                   jax.ShapeDtypeStruct((B, S, u_rows), BF16),
                   jax.ShapeDtypeStruct((B, HG_HEADS, S // CHUNK, HG_DIM, HG_DIM), F32)] + p_oshapes,
        input_output_aliases=p_alias,
        scratch_shapes=[pltpu.VMEM((HG_DIM, HG_DIM), F32)] + p_scratch,
        compiler_params=_params(("arbitrary", "arbitrary", "arbitrary")),
    )(proj_h, proj_h, proj_h, proj_h, lb_param, g_head, ts, _level_masks(), *p_args)
    return res[0], res[1], res[2], list(res[3:])


def _hgrn_bwd(proj_h, o_h, du, states, lb_param, g_head, layer, name, phase=None):
    B, S, _ = proj_h.shape
    sb = _pick(S, (512, 256, 128, 64))
    nc = sb // CHUNK
    ns = S // sb
    ts, tst = _cum_matrices()

    def body(*refs):
        ins, outs, (dst,), p_in, p_out, p_sems = _split_refs(refs, 12, 6, 1, phase)
        q_ref, f_ref, i_ref, z_ref, o_ref, du_ref, sts_ref, lbp_ref, gh_ref, ts_ref, tst_ref, m_ref = ins
        dq_ref, df_ref, di_ref, dz_ref, dlb_ref, dgh_ref = outs
        h_id, b_id, s_id = pl.program_id(0), pl.program_id(1), pl.program_id(2)
        _hosted_start(phase, p_in, p_out, p_sems, (h_id == 0) & (b_id == 0) & (s_id == 0))

        @pl.when(s_id == 0)
        def _():
            dst[...] = jnp.zeros_like(dst)

        @pl.when((b_id == 0) & (s_id == 0))
        def _():
            dlb_ref[...] = jnp.zeros_like(dlb_ref)

        @pl.when((h_id == 0) & (b_id == 0) & (s_id == 0))
        def _():
            dgh_ref[...] = jnp.zeros_like(dgh_ref)

        lt = _lb_terms(lbp_ref[...], layer)
        gh = gh_ref[...]
        tsv = ts_ref[...]
        tstv = tst_ref[...]
        logf, k, f, sig, nsig = _gate(f_ref[...], lt)
        o = o_ref[...]
        dub = du_ref[...]
        r = lax.rsqrt(jnp.mean(o * o, axis=-1, keepdims=True) + NORM_EPS)
        n = o * r
        sg, sg_grad = _silu_and_grad(z_ref[...])
        dz_ref[...] = (dub * (n * gh) * sg_grad).astype(BF16)
        dgh_ref[...] += jnp.sum(dub * sg * n, axis=0, keepdims=True)
        dn = dub * sg * gh
        do = r * (dn - n * jnp.mean(dn * n, axis=-1, keepdims=True))
        v = i_ref[...]
        qf, qf_grad = _silu_and_grad(q_ref[...])
        w = _hg_block_bwd(qf, k, v, logf, do, tsv, m_ref, nc)
        cur = dst[...]
        ends = [None] * nc
        for ci in reversed(range(nc)):
            ends[ci] = cur
            cur = cur * w["dec"][ci] + w["qd"][ci]
        dst[...] = cur
        dq, dk, dv, dg = _hg_state_bwd(w, v, do, [sts_ref[ci] for ci in range(nc)], ends, tstv, nc)
        di_ref[...] = dv.astype(BF16)
        dq_ref[...] = (dq * qf_grad).astype(BF16)
        scaled = (dg - f * dk) / f
        df_ref[...] = (scaled * lt["one_m"] * sig * nsig).astype(BF16)
        dlb_ref[...] += jnp.sum(scaled * (lt["ind"] - sig), axis=0, keepdims=True)
        _hosted_finish(phase, p_in, p_out, p_sems, (h_id == HG_HEADS - 1) & (b_id == B - 1) & (s_id == ns - 1))

    col = lambda base: pl.BlockSpec((None, sb, HG_DIM), lambda h, b, s: (b, ns - 1 - s, base + h))
    out_col = pl.BlockSpec((None, sb, HG_DIM), lambda h, b, s: (b, ns - 1 - s, h))
    dt = jax.ShapeDtypeStruct((B, S, HG_WIDTH), BF16)
    p_ispecs, p_ospecs, p_oshapes, p_alias, p_scratch, p_args = _host_phase(phase, 12, 6)
    res = pl.pallas_call(
        body, name=name,
        grid=(HG_HEADS, B, ns),
        in_specs=[col(0), col(HG_HEADS), col(2 * HG_HEADS), col(3 * HG_HEADS), col(0), col(0),
                  pl.BlockSpec((None, None, nc, HG_DIM, HG_DIM), lambda h, b, s: (b, h, ns - 1 - s, 0, 0)),
                  pl.BlockSpec((DEPTH, HG_DIM), lambda h, b, s: (0, h)),
                  pl.BlockSpec((1, HG_DIM), lambda h, b, s: (0, 0)),
                  pl.BlockSpec((N_CUM_F, CHUNK), lambda h, b, s: (0, 0)),
                  pl.BlockSpec((CHUNK, N_CUM), lambda h, b, s: (0, 0)),
                  pl.BlockSpec((len(_LEVELS), CHUNK, CHUNK), lambda h, b, s: (0, 0, 0))] + p_ispecs,
        out_specs=[out_col, out_col, out_col, out_col,
                   pl.BlockSpec((1, HG_DIM), lambda h, b, s: (0, h)),
                   pl.BlockSpec((1, HG_DIM), lambda h, b, s: (0, 0))] + p_ospecs,
        out_shape=[dt, dt, dt, dt, jax.ShapeDtypeStruct((1, HG_WIDTH), F32),
                   jax.ShapeDtypeStruct((1, HG_DIM), F32)] + p_oshapes,
        input_output_aliases=p_alias,
        scratch_shapes=[pltpu.VMEM((HG_DIM, HG_DIM), F32)] + p_scratch,
        compiler_params=_params(("arbitrary", "arbitrary", "arbitrary")),
    )(proj_h, proj_h, proj_h, proj_h, o_h, du, states, lb_param, g_head, ts, tst, _level_masks(), *p_args)
    return tuple(res[:6]) + (list(res[6:]),)


def _rope_tables(S):
    half = ATT_DIM // 2
    inv_freq = ROPE_THETA ** (-jnp.arange(half, dtype=F32) / half)
    ang = jnp.arange(S).astype(F32)[:, None] * inv_freq[None, :]
    cos = jnp.cos(ang)
    sin = jnp.sin(ang)
    cos = jnp.concatenate([cos, cos, cos, cos], axis=1)
    sin = jnp.concatenate([-sin, sin, -sin, sin], axis=1)
    return cos, sin


def _attn_common():
    lane = lax.broadcasted_iota(jnp.int32, (1, 2 * ATT_DIM), 1)
    first_half = (lane % ATT_DIM) < (ATT_DIM // 2)
    left = lane < ATT_DIM

    def swap(x):
        return jnp.where(first_half, pltpu.roll(x, 128 - ATT_DIM // 2, 1), pltpu.roll(x, ATT_DIM // 2, 1))

    def rope(x, cos, sin):
        return x * cos + swap(x) * sin

    def rope_bwd(dy, cos, sin):
        return dy * cos + swap(dy * sin)

    def dup(x):
        xs = pltpu.roll(x, ATT_DIM, 1)
        return [jnp.where(left, x, xs), jnp.where(left, xs, x)]

    return left, rope, rope_bwd, dup


GROUP = ATT_HEADS // 2
GROUP_ROWS = GROUP * ATT_BLOCK


def _attn_bias(i):
    r = lax.broadcasted_iota(jnp.int32, (ATT_BLOCK, 2 * ATT_BLOCK), 0)
    c = lax.broadcasted_iota(jnp.int32, (ATT_BLOCK, 2 * ATT_BLOCK), 1)
    ok = (c > r) & (c <= r + ATT_BLOCK) & ((c >= ATT_BLOCK) | (i > 0))
    return jnp.where(ok, 0.0, NEG_INF)


def _stack_heads(pairs, left):
    rows = []
    for x in pairs:
        rows += [jnp.where(left, x, 0.0), jnp.where(left, 0.0, x)]
    return jnp.concatenate(rows, axis=0)


def _unstack_heads(y, left, pp):
    r0 = 2 * pp * ATT_BLOCK
    return jnp.where(left, y[r0:r0 + ATT_BLOCK], y[r0 + ATT_BLOCK:r0 + 2 * ATT_BLOCK])


def _row_sums(x):
    return _dot(x, jnp.ones((x.shape[1], 128), BF16), NN)


def _attn_probs(qs, kd, vd, sink, bias):
    n = range(len(qs))
    rows = qs[0].shape[0]
    s = [(_dot(qs[j], kd[j], NT).reshape(rows // ATT_BLOCK, ATT_BLOCK, 2 * ATT_BLOCK) * ATT_SCALE + bias[None])
         .reshape(rows, 2 * ATT_BLOCK) for j in n]
    m = [jnp.max(jnp.maximum(jnp.maximum(s[j][:, :128], s[j][:, 128:]), sink[j]), axis=-1, keepdims=True) for j in n]
    pu = [jnp.exp(s[j] - m[j]) for j in n]
    es = [jnp.exp(sink[j] - m[j]) for j in n]
    ones = jnp.ones((2 * ATT_BLOCK, 128), BF16)
    ov = [_dot(pu[j], jnp.concatenate([vd[j].astype(BF16), ones], axis=1), NN) for j in n]
    inv = [1.0 / (ov[j][:, 128:] + es[j]) for j in n]
    return ([pu[j] * jnp.concatenate([inv[j], inv[j]], axis=1) for j in n], [es[j] * inv[j] for j in n],
            [ov[j][:, :128] * inv[j] for j in n])


def _sink_rows(sinks_l):
    return jnp.broadcast_to(jnp.repeat(sinks_l, ATT_BLOCK)[:, None], (ATT_HEADS * ATT_BLOCK, 128))


_Z0 = (2 * ATT_WIDTH + 2 * KV_WIDTH - ATT_WIDTH) // 256


def _attn_fwd(proj_a, u, sinks_l, cos, sin, name, phase=None):
    B, S, _ = proj_a.shape
    nb = S // ATT_BLOCK

    def body(*refs):
        ins, (u_ref,), _, p_in, p_out, p_sems = _split_refs(refs, 13, 1, 0, phase)
        q_ref, kvc_ref, kvp_ref, z0, z1, z2, z3, cos_ref, sin_ref, cosp_ref, sinp_ref, sinks_ref, _ = ins
        i = pl.program_id(1)
        _hosted_start(phase, p_in, p_out, p_sems, (pl.program_id(0) == 0) & (i == 0))
        left, rope, _, dup = _attn_common()
        cos_c, sin_c = cos_ref[...], sin_ref[...]
        kvc = kvc_ref[...]
        kvp = kvp_ref[...]
        kw = jnp.concatenate([rope(kvp[:, :KV_WIDTH], cosp_ref[...], sinp_ref[...]),
                              rope(kvc[:, :KV_WIDTH], cos_c, sin_c)], axis=0)
        vw = jnp.concatenate([kvp[:, KV_WIDTH:], kvc[:, KV_WIDTH:]], axis=0)
        kd, vd = dup(kw), dup(vw)
        bias = _attn_bias(i)
        zs = (z0, z1, z2, z3)
        pairs = [range(4 * kvh, 4 * kvh + 4) for kvh in range(2)]
        qs = [_stack_heads([rope(q_ref[:, 128 * pr:128 * (pr + 1)], cos_c, sin_c) for pr in pairs[kvh]], left)
              for kvh in range(2)]
        sink = [sinks_ref[kvh * GROUP_ROWS:(kvh + 1) * GROUP_ROWS, :] for kvh in range(2)]
        o = _attn_probs(qs, kd, vd, sink, bias)[2]
        for kvh in range(2):
            for pp, pr in enumerate(pairs[kvh]):
                z = zs[pr // 2][:, 128 * (pr % 2):128 * (pr % 2 + 1)]
                u_ref[:, 128 * pr:128 * (pr + 1)] = (_unstack_heads(o[kvh], left, pp) * _silu(z)).astype(BF16)
        _hosted_finish(phase, p_in, p_out, p_sems, (pl.program_id(0) == B - 1) & (i == nb - 1))

    rowblk = lambda w, cb: pl.BlockSpec((None, ATT_BLOCK, w), lambda b, i: (b, i, cb))
    tab = pl.BlockSpec((ATT_BLOCK, 128), lambda b, i: (i, 0))
    tabp = pl.BlockSpec((ATT_BLOCK, 128), lambda b, i: (jnp.maximum(i - 1, 0), 0))
    p_ispecs, p_ospecs, p_oshapes, p_alias, p_scratch, p_args = _host_phase(phase, 13, 1)
    res = pl.pallas_call(
        body, name=name,
        grid=(B, nb),
        in_specs=[rowblk(ATT_WIDTH, 0), rowblk(256, 4),
                  pl.BlockSpec((None, ATT_BLOCK, 256), lambda b, i: (b, jnp.maximum(i - 1, 0), 4)),
                  rowblk(256, _Z0), rowblk(256, _Z0 + 1), rowblk(256, _Z0 + 2), rowblk(256, _Z0 + 3),
                  tab, tab, tabp, tabp,
                  pl.BlockSpec((ATT_HEADS * ATT_BLOCK, 128), lambda b, i: (0, 0)),
                  pl.BlockSpec(memory_space=pl.ANY)] + p_ispecs,
        out_specs=[pl.BlockSpec((None, ATT_BLOCK, ATT_WIDTH), lambda b, i: (b, i, 1))] + p_ospecs,
        out_shape=[jax.ShapeDtypeStruct(u.shape, BF16)] + p_oshapes,
        input_output_aliases={12: 0, **p_alias},
        scratch_shapes=p_scratch,
        compiler_params=_params(("arbitrary", "arbitrary")),
    )(proj_a, proj_a, proj_a, proj_a, proj_a, proj_a, proj_a, cos, sin, cos, sin, sinks_l, u, *p_args)
    return res[0], list(res[1:])


def _attn_bwd(proj_a, du, sinks_l, cos, sin, name, phase=None):
    B, S, _ = proj_a.shape
    nb = S // ATT_BLOCK

    def body(*refs):
        ins, outs, (carry, sk_acc), p_in, p_out, p_sems = _split_refs(refs, 13, 4, 2, phase)
        q_ref, kvc_ref, kvp_ref, z0, z1, z2, z3, du_ref, cos_ref, sin_ref, cosp_ref, sinp_ref, sinks_ref = ins
        dq_ref, dkv_ref, dz_ref, dsk_ref = outs
        b_id, i = pl.program_id(0), pl.program_id(1)
        _hosted_start(phase, p_in, p_out, p_sems, (b_id == 0) & (i == 0))

        @pl.when((b_id == 0) & (i == 0))
        def _():
            sk_acc[...] = jnp.zeros_like(sk_acc)

        @pl.when(i == 0)
        def _():
            carry[...] = jnp.zeros_like(carry)

        @pl.when(i < nb)
        def _():
            left, rope, rope_bwd, dup = _attn_common()
            cos_c, sin_c = cos_ref[...], sin_ref[...]
            cos_p, sin_p = cosp_ref[...], sinp_ref[...]
            kvc = kvc_ref[...]
            kvp = kvp_ref[...]
            kw = jnp.concatenate([rope(kvp[:, :KV_WIDTH], cos_p, sin_p), rope(kvc[:, :KV_WIDTH], cos_c, sin_c)], axis=0)
            vw = jnp.concatenate([kvp[:, KV_WIDTH:], kvc[:, KV_WIDTH:]], axis=0)
            kd, vd = dup(kw), dup(vw)
            bias = _attn_bias(i)
            zs = (z0, z1, z2, z3)
            units = [(kvh, hf) for kvh in range(2) for hf in range(2)]
            half = GROUP_ROWS // 2
            pairs = [range(4 * kvh + 2 * hf, 4 * kvh + 2 * hf + 2) for kvh, hf in units]
            un = range(len(units))
            qs = [_stack_heads([rope(q_ref[:, 128 * pr:128 * (pr + 1)], cos_c, sin_c) for pr in pairs[j]], left)
                  for j in un]
            sink = [sinks_ref[kvh * GROUP_ROWS + hf * half:kvh * GROUP_ROWS + (hf + 1) * half, :] for kvh, hf in units]
            ku = [kd[kvh] for kvh, _ in units]
            vu = [vd[kvh] for kvh, _ in units]
            def first(j):
                p, ps, o = (r[0] for r in _attn_probs([qs[j]], [ku[j]], [vu[j]], [sink[j]], bias))
                parts = []
                for pp, pr in enumerate(pairs[j]):
                    cols = slice(128 * pr, 128 * (pr + 1))
                    sg, sg_grad = _silu_and_grad(zs[pr // 2][:, 128 * (pr % 2):128 * (pr % 2 + 1)])
                    du128 = du_ref[:, cols]
                    dz_ref[:, cols] = (du128 * _unstack_heads(o, left, pp) * sg_grad).astype(BF16)
                    parts.append(du128 * sg)
                dos = _stack_heads(parts, left)
                dp = _dot(dos, vu[j], NT)
                delta = _row_sums(p * dp)
                ds = (p * (dp - jnp.concatenate([delta, delta], axis=1)) * ATT_SCALE).astype(BF16)
                kvh, hf = units[j]
                sk_acc[kvh, hf * half:(hf + 1) * half, :] += -ps * delta
                return ds, p.astype(BF16), dos.astype(BF16)

            def second(j, ds, p, dos):
                dqs = _dot(ds, ku[j], NN)
                for pp, pr in enumerate(pairs[j]):
                    dq_ref[:, 128 * pr:128 * (pr + 1)] = rope_bwd(_unstack_heads(dqs, left, pp),
                                                                  cos_c, sin_c).astype(BF16)
                return _dot(ds, qs[j], TN), _dot(p, dos, TN)

            got, dku, dvu = {}, [None] * len(units), [None] * len(units)
            for j in range(len(units) + 1):
                if j < len(units):
                    got[j] = first(j)
                if j >= 1:
                    dku[j - 1], dvu[j - 1] = second(j - 1, *got.pop(j - 1))
            dkd = [dku[0] + dku[1], dku[2] + dku[3]]
            dvd = [dvu[0] + dvu[1], dvu[2] + dvu[3]]
            fold = lambda pr: jnp.where(left, pr[0] + pltpu.roll(pr[0], ATT_DIM, 1), pr[1] + pltpu.roll(pr[1], ATT_DIM, 1))
            dkw = fold(dkd)
            dvw = fold(dvd)
            prev = jnp.concatenate([rope_bwd(dkw[:ATT_BLOCK], cos_p, sin_p), dvw[:ATT_BLOCK]], axis=1)
            cur = jnp.concatenate([rope_bwd(dkw[ATT_BLOCK:], cos_c, sin_c), dvw[ATT_BLOCK:]], axis=1)
            dkv_ref[...] = (carry[...] + prev).astype(BF16)
            carry[...] = cur

        @pl.when(i == nb)
        def _():
            dkv_ref[...] = carry[...].astype(BF16)

        @pl.when((b_id == B - 1) & (i == nb))
        def _():
            lane = lax.broadcasted_iota(jnp.int32, (1, 128), 1)
            tot = jnp.zeros((1, 128), F32)
            for hd in range(ATT_HEADS):
                rows = sk_acc[hd // GROUP, (hd % GROUP) * ATT_BLOCK:(hd % GROUP + 1) * ATT_BLOCK, :]
                tot = tot + jnp.where(lane == hd, jnp.sum(rows, axis=0, keepdims=True), 0.0)
            dsk_ref[...] = tot

        _hosted_finish(phase, p_in, p_out, p_sems, (b_id == B - 1) & (i == nb))

    cl = lambda i: jnp.minimum(i, nb - 1)
    pv = lambda i: jnp.maximum(jnp.minimum(i, nb - 1) - 1, 0)
    rowblk = lambda w, cb: pl.BlockSpec((None, ATT_BLOCK, w), lambda b, i: (b, cl(i), cb))
    tab = pl.BlockSpec((ATT_BLOCK, 128), lambda b, i: (cl(i), 0))
    tabp = pl.BlockSpec((ATT_BLOCK, 128), lambda b, i: (pv(i), 0))
    p_ispecs, p_ospecs, p_oshapes, p_alias, p_scratch, p_args = _host_phase(phase, 13, 4)
    res = pl.pallas_call(
        body, name=name,
        grid=(B, nb + 1),
        in_specs=[rowblk(ATT_WIDTH, 0), rowblk(256, 4),
                  pl.BlockSpec((None, ATT_BLOCK, 256), lambda b, i: (b, pv(i), 4)),
                  rowblk(256, _Z0), rowblk(256, _Z0 + 1), rowblk(256, _Z0 + 2), rowblk(256, _Z0 + 3),
                  rowblk(ATT_WIDTH, 1),
                  tab, tab, tabp, tabp,
                  pl.BlockSpec((ATT_HEADS * ATT_BLOCK, 128), lambda b, i: (0, 0))] + p_ispecs,
        out_specs=[rowblk(ATT_WIDTH, 0),
                   pl.BlockSpec((None, ATT_BLOCK, 256), lambda b, i: (b, jnp.maximum(i - 1, 0), 0)),
                   rowblk(ATT_WIDTH, 0),
                   pl.BlockSpec((1, 128), lambda b, i: (0, 0))] + p_ospecs,
        out_shape=[jax.ShapeDtypeStruct((B, S, ATT_WIDTH), BF16), jax.ShapeDtypeStruct((B, S, 256), BF16),
                   jax.ShapeDtypeStruct((B, S, ATT_WIDTH), BF16), jax.ShapeDtypeStruct((1, 128), F32)] + p_oshapes,
        input_output_aliases=p_alias,
        scratch_shapes=[pltpu.VMEM((ATT_BLOCK, 256), F32), pltpu.VMEM((2, GROUP_ROWS, 128), F32)] + p_scratch,
        compiler_params=_params(("arbitrary", "arbitrary")),
    )(proj_a, proj_a, proj_a, proj_a, proj_a, proj_a, proj_a, du, cos, sin, cos, sin, sinks_l, *p_args)
    return tuple(res[:4]) + (list(res[4:]),)


def _outproj_fwd(u2, w_out, x2, g_post, target2, name):
    T, D = x2.shape
    tm = _pick(T, (512, 256, 128))
    last = target2 is not None

    def body(u_ref, w_ref, x_ref, g_ref, *rest):
        y = lax.dot_general(u_ref[...], w_ref[...], (NN, ((), ())), preferred_element_type=F32)
        r = lax.rsqrt(jnp.mean(y * y, axis=-1, keepdims=True) + NORM_EPS)
        xn = x_ref[...] + (y * r) * g_ref[...]
        if last:
            t_ref, y_ref, dx_ref, loss_ref = rest
            err = xn - t_ref[...]
            dx_ref[...] = err * (1.0 / D)
            sq = err * err
            acc = sq[:, 0:128]
            for kk in range(1, D // 128):
                acc = acc + sq[:, 128 * kk:128 * (kk + 1)]
            part = jnp.sum(acc.reshape(tm // 8, 8, 128), axis=0) * (0.5 / D)

            @pl.when(pl.program_id(0) == 0)
            def _():
                loss_ref[...] = jnp.zeros_like(loss_ref)

            loss_ref[...] += part
        else:
            y_ref, xn_ref = rest
            xn_ref[...] = xn
        y_ref[...] = y

    row = pl.BlockSpec((tm, D), lambda i: (i, 0))
    in_specs = [pl.BlockSpec((tm, MIX_WIDTH), lambda i: (i, 0)),
                pl.BlockSpec((MIX_WIDTH, D), lambda i: (0, 0)), row,
                pl.BlockSpec((1, D), lambda i: (0, 0))]
    args = [u2, w_out, x2, g_post]
    out_specs = [row, row]
    out_shape = [jax.ShapeDtypeStruct((T, D), F32), jax.ShapeDtypeStruct((T, D), F32)]
    if last:
        in_specs.append(row)
        args.append(target2)
        out_specs.append(pl.BlockSpec((8, 128), lambda i: (0, 0)))
        out_shape.append(jax.ShapeDtypeStruct((8, 128), F32))
    return pl.pallas_call(
        body, name=name, grid=(T // tm,), in_specs=in_specs, out_specs=out_specs, out_shape=out_shape,
        compiler_params=_params(("arbitrary",)),
    )(*args)


def _outproj_bwd(dxn2, y2, g_post, w_out, name):
    T, D = y2.shape
    N = w_out.shape[0]
    tm = _pick(T, (512, 256, 128))
    nt = T // tm

    def body(dx_ref, y_ref, g_ref, w_ref, dy_ref, dg_ref, du_ref, acc):
        i = pl.program_id(0)

        @pl.when(i == 0)
        def _():
            acc[...] = jnp.zeros_like(acc)

        y = y_ref[...]
        dxn = dx_ref[...]
        r = lax.rsqrt(jnp.mean(y * y, axis=-1, keepdims=True) + NORM_EPS)
        n = y * r
        dn = dxn * g_ref[...]
        dy = (r * (dn - n * jnp.mean(dn * n, axis=-1, keepdims=True))).astype(BF16)
        dy_ref[...] = dy
        du_ref[...] = lax.dot_general(dy, w_ref[...], (NT, ((), ())), preferred_element_type=F32)
        acc[...] += jnp.sum((dxn * n).reshape(tm // 8, 8, D), axis=0)

        @pl.when(i == nt - 1)
        def _():
            dg_ref[...] = jnp.sum(acc[...], axis=0, keepdims=True)

    row = pl.BlockSpec((tm, D), lambda i: (i, 0))
    vec = pl.BlockSpec((1, D), lambda i: (0, 0))
    return pl.pallas_call(
        body, name=name, grid=(nt,),
        in_specs=[row, row, vec, pl.BlockSpec((N, D), lambda i: (0, 0), pipeline_mode=pl.Buffered(1))],
        out_specs=[row, vec, pl.BlockSpec((tm, N), lambda i: (i, 0))],
        out_shape=[jax.ShapeDtypeStruct((T, D), BF16), jax.ShapeDtypeStruct((1, D), F32),
                   jax.ShapeDtypeStruct((T, N), F32)],
        scratch_shapes=[pltpu.VMEM((8, D), F32)],
        compiler_params=_params(("arbitrary",)),
    )(dxn2, y2, g_post, w_out)


def _inproj_bwd(pieces, w_t, x2, dxn2, g_pre, name, phase=None):
    T, D = x2.shape
    widths = [p.shape[1] for p in pieces]
    offs = [sum(widths[:i]) for i in range(len(pieces))]
    n_p = len(pieces)
    tm = _pick(T, (256, 128))
    nt = T // tm

    def body(*refs):
        ins, (dx_ref, dg_ref), (acc,), p_in, p_out, p_sems = _split_refs(refs, n_p + 4, 2, 1, phase)
        w_ref, x_ref, dxn_ref, g_ref = ins[n_p:]
        i = pl.program_id(0)
        _hosted_start(phase, p_in, p_out, p_sems, i == 0)

        @pl.when(i == 0)
        def _():
            acc[...] = jnp.zeros_like(acc)

        dh = jnp.zeros((tm, D), F32)
        for p in range(n_p):
            dh = dh + lax.dot_general(ins[p][...], w_ref[offs[p]:offs[p] + widths[p], :], (NN, ((), ())),
                                      preferred_element_type=F32)
        x = x_ref[...]
        r = lax.rsqrt(jnp.mean(x * x, axis=-1, keepdims=True) + NORM_EPS)
        n = x * r
        dn = dh * g_ref[...]
        dx_ref[...] = dxn_ref[...] + r * (dn - n * jnp.mean(dn * n, axis=-1, keepdims=True))
        acc[...] += jnp.sum((dh * n).reshape(tm // 8, 8, D), axis=0)

        @pl.when(i == nt - 1)
        def _():
            dg_ref[...] = jnp.sum(acc[...], axis=0, keepdims=True)

        _hosted_finish(phase, p_in, p_out, p_sems, i == nt - 1)

    row = pl.BlockSpec((tm, D), lambda i: (i, 0))
    vec = pl.BlockSpec((1, D), lambda i: (0, 0))
    p_ispecs, p_ospecs, p_oshapes, p_alias, p_scratch, p_args = _host_phase(phase, n_p + 4, 2)
    res = pl.pallas_call(
        body, name=name, grid=(nt,),
        in_specs=[pl.BlockSpec((tm, w), lambda i: (i, 0)) for w in widths]
        + [pl.BlockSpec((sum(widths), D), lambda i: (0, 0), pipeline_mode=pl.Buffered(1)), row, row, vec] + p_ispecs,
        out_specs=[row, vec] + p_ospecs,
        out_shape=[jax.ShapeDtypeStruct((T, D), F32), jax.ShapeDtypeStruct((1, D), F32)] + p_oshapes,
        input_output_aliases=p_alias,
        scratch_shapes=[pltpu.VMEM((8, D), F32)] + p_scratch,
        compiler_params=_params(("arbitrary",)),
    )(*pieces, w_t, x2, dxn2, g_pre, *p_args)
    return res[0], res[1], list(res[2:])


def _step(x, target, g_pre, g_post, lb_param, g_head, sinks, shards=None, full=None):
    B, S, D = x.shape
    T = B * S
    dist = shards is not None
    first, last = 0, DEPTH - 1
    if dist:
        a_loc, b_loc = shards
        ra, rb = a_loc.shape[1], b_loc.shape[1]
        side = _own_side_blocks()
        placed = lambda loc, nm: _place_own(loc, side, "place_" + nm)
        w_in0 = _run_phase(_gather_ici_phase([a_loc[0]], [placed(a_loc[0], "in0")]), "gather_in0_ici")
        w_in0 = _run_phase(_gather_d2d_phase(w_in0, [ra]), "gather_in0_d2d")[0]
        w_in, w_out = [w_in0, None], [None, None]
    else:
        w_in, w_out = list(full[0]), list(full[1])
    cos, sin = _rope_tables(S)
    saved = []
    xs = x
    loss_part = None
    dxn = None
    for l in range(DEPTH):
        x2 = xs.reshape(T, D)
        proj_h, proj_a, h = _inproj(x2, g_pre[l:l + 1], w_in[l], f"inproj{l}")
        proj_h = proj_h.reshape(B, S, N_H)
        proj_a = proj_a.reshape(B, S, N_A)
        phase = None
        if dist and l == first:
            phase = _gather_ici_phase([a_loc[1], b_loc[0]], [placed(a_loc[1], "in1"), placed(b_loc[0], "out0")])
        if dist and l == last:
            phase = _gather_d2d_phase([w_out1_part], [rb])
        o_h, u, states, got = _hgrn_fwd(proj_h, MIX_WIDTH, lb_param, g_head[l:l + 1], l, f"hgrn_fwd{l}", phase)
        phase = None
        if dist and l == first:
            phase = _merge_phases(_gather_d2d_phase(got, [ra, rb]),
                                  _gather_ici_phase([b_loc[1]], [placed(b_loc[1], "out1")]))
        if dist and l == last:
            w_out[1] = got[0]
        u, got = _attn_fwd(proj_a, u, _sink_rows(sinks[l]), cos, sin, f"attn_fwd{l}", phase)
        if dist and l == first:
            w_in[1], w_out[0], w_out1_part = got
        u2 = u.reshape(T, MIX_WIDTH)
        if l < last:
            y, xn = _outproj_fwd(u2, w_out[l], x2, g_post[l:l + 1], None, f"outproj{l}")
            xn = xn.reshape(B, S, D)
        else:
            y, dxn, loss_part = _outproj_fwd(u2, w_out[l], x2, g_post[l:l + 1], target.reshape(T, D), f"outproj{l}")
            xn = None
        saved.append((x2, h, proj_h, proj_a, o_h, u2, states, y))
        xs = xn

    dw_in, dw_out = [None] * DEPTH, [None] * DEPTH
    dg_pre, dg_post, dlb, dg_head, dsinks = [], [], [], [], []
    for l in reversed(range(DEPTH)):
        x2, h, proj_h, proj_a, o_h, u2, states, y = saved[l]
        dy, dgp, du = _outproj_bwd(dxn, y, g_post[l:l + 1], w_out[l], f"outproj_bwd{l}")
        du = du.reshape(B, S, MIX_WIDTH)
        dw_out[l] = _mm_tn([u2], dy, f"wgrad_out{l}")
        phase = None
        if dist:
            phase = _reduce_d2d_phase([dw_out[l]], [rb])
            if l == first:
                phase = _merge_phases(_reduce_ici_phase([part_in1]), phase)
        dqh, dfh, dih, dzh, dlb_l, dgh, got = _hgrn_bwd(
            proj_h, o_h, du, states, lb_param, g_head[l:l + 1], l, f"hgrn_bwd{l}", phase)
        if dist:
            if l == first:
                sum_in = _chip_sum(part_in1, got[0], "chip_sum_in1", 1)
            part_out = _pair_sum(dw_out[l], got[-1], side, f"pair_sum_out{l}")
        dqa, dkv, dza, dsk, got = _attn_bwd(proj_a, du, _sink_rows(sinks[l]), cos, sin, f"attn_bwd{l}",
                                            _reduce_ici_phase([part_out]) if dist else None)
        if dist:
            sum_out = _chip_sum(part_out, got[0], f"chip_sum_out{l}", l, None if l == last else sum_out)
        dproj = [p.reshape(T, p.shape[-1]) for p in (dqh, dfh, dih, dzh, dqa, dkv, dza)]
        dw_in[l] = _mm_tn(dproj, h, f"wgrad_in{l}")
        phase = None
        if dist and l == last:
            phase = _reduce_d2d_phase([dw_in[l]], [ra])
        if dist and l == first:
            got = _run_phase(_reduce_d2d_phase([dw_in[l]], [ra]), "reduce_in0_d2d")
            part_in0 = _pair_sum(dw_in[l], got[0], side, "pair_sum_in0")
            phase = _reduce_ici_phase([part_in0])
        dxn, dgpre, got = _inproj_bwd(dproj, w_in[l], x2, dxn, g_pre[l:l + 1], f"inproj_bwd{l}", phase)
        if dist and l == last:
            part_in1 = _pair_sum(dw_in[l], got[0], side, "pair_sum_in1")
        if dist and l == first:
            sum_in = _chip_sum(part_in0, got[0], "chip_sum_in0", 0, sum_in)
        dg_pre.append(dgpre)
        dg_post.append(dgp)
        dlb.append(dlb_l)
        dg_head.append(dgh)
        dsinks.append(dsk)
    rev = lambda lst: jnp.concatenate(lst[::-1], axis=0)
    if not dist:
        sum_in, sum_out = jnp.stack(dw_in), jnp.stack(dw_out)
    return (loss_part, dxn.reshape(B, S, D), sum_in, sum_out,
            rev(dg_pre), rev(dg_post), rev(dlb), rev(dg_head), rev(dsinks))


def _me_and_peers():
    x, y, c = lax.axis_index("x"), lax.axis_index("y"), lax.axis_index("c")
    me = 4 * x + 2 * y + c
    peers = []
    for k in range(1, N_DEV):
        px = 1 - x if k & 4 else x
        py = 1 - y if k & 2 else y
        pc = 1 - c if k & 1 else c
        peers.append(((px, py, pc), 4 * px + 2 * py + pc))
    return me, peers


class _Phase:
    def __init__(self, arrays, out_shapes, aliases, n_send, build):
        self.arrays, self.out_shapes, self.aliases = list(arrays), list(out_shapes), dict(aliases)
        self.n_send, self.build = n_send, build

    def scratch(self):
        return [pltpu.SemaphoreType.DMA((self.n_send,)), pltpu.SemaphoreType.DMA((self.n_send,))]

    def _copies(self, in_refs, out_refs, sems, arrivals):
        send_sems, recv_sems = sems
        sends, recvs = self.build(in_refs, out_refs)
        assert len(sends) == self.n_send == len(recvs)
        out = [pltpu.make_async_remote_copy(src_ref=s, dst_ref=d, send_sem=send_sems.at[i], recv_sem=recv_sems.at[i],
                                            device_id=dev, device_id_type=MESH) for i, (s, d, dev) in enumerate(sends)]
        inc = [pltpu.make_async_remote_copy(src_ref=s, dst_ref=r, send_sem=send_sems.at[i], recv_sem=recv_sems.at[i],
                                            device_id=dev, device_id_type=MESH)
               for i, ((s, _, dev), r) in enumerate(zip(sends, recvs))] if arrivals else []
        return out, inc

    def start(self, in_refs, out_refs, sems):
        out, _ = self._copies(in_refs, out_refs, sems, False)
        for cp in out:
            cp.start()

    def finish(self, in_refs, out_refs, sems):
        out, inc = self._copies(in_refs, out_refs, sems, True)
        for cp in inc:
            cp.wait_recv()
        for cp in out:
            cp.wait_send()


_ANY = pl.BlockSpec(memory_space=pl.ANY)


def _host_phase(phase, n_in, n_out):
    if phase is None:
        return [], [], [], {}, [], []
    aliases = {n_in + i: n_out + o for i, o in phase.aliases.items()}
    return ([_ANY] * len(phase.arrays), [_ANY] * len(phase.out_shapes), phase.out_shapes, aliases, phase.scratch(),
            phase.arrays)


def _split_refs(refs, n_in, n_out, n_scr, phase):
    pi = len(phase.arrays) if phase else 0
    po = len(phase.out_shapes) if phase else 0
    a = n_in + pi
    b = a + n_out + po
    return (refs[:n_in], refs[a:a + n_out], refs[b:b + n_scr], refs[n_in:a], refs[a + n_out:b], refs[b + n_scr:])


def _hosted_start(phase, p_in, p_out, p_sems, first):
    if phase is not None:
        @pl.when(first)
        def _():
            phase.start(p_in, p_out, p_sems)


def _hosted_finish(phase, p_in, p_out, p_sems, last):
    if phase is not None:
        @pl.when(last)
        def _():
            phase.finish(p_in, p_out, p_sems)


def _run_phase(phase, name):
    n_in, n_out = len(phase.arrays), len(phase.out_shapes)

    def body(*refs):
        phase.start(refs[:n_in], refs[n_in:n_in + n_out], refs[n_in + n_out:])
        phase.finish(refs[:n_in], refs[n_in:n_in + n_out], refs[n_in + n_out:])

    return pl.pallas_call(
        body, name=name, in_specs=[_ANY] * n_in, out_specs=[_ANY] * n_out,
        out_shape=phase.out_shapes, input_output_aliases=phase.aliases, scratch_shapes=phase.scratch(),
        compiler_params=pltpu.CompilerParams(has_side_effects=True),
    )(*phase.arrays)


def _merge_phases(a, b):
    n_in, n_out = len(a.arrays), len(a.out_shapes)
    aliases = dict(a.aliases)
    aliases.update({n_in + i: n_out + o for i, o in b.aliases.items()})

    def build(ins, outs):
        sa, ra = a.build(ins[:n_in], outs[:n_out])
        sb, rb = b.build(ins[n_in:], outs[n_out:])
        return sa + sb, ra + rb

    return _Phase(a.arrays + b.arrays, a.out_shapes + b.out_shapes, aliases, a.n_send + b.n_send, build)


def _mesh_place():
    x, y, c = lax.axis_index("x"), lax.axis_index("y"), lax.axis_index("c")
    chips = [(x, y), (1 - x, y), (x, 1 - y), (1 - x, 1 - y)]
    num = lambda chip, core: 4 * chip[0] + 2 * chip[1] + core
    return c, chips, num


def _own_side_blocks():
    c, chips, num = _mesh_place()
    return jnp.stack([num(ch, c) for ch in chips]).astype(jnp.int32)


def _rows(ref, r, dev):
    return ref.at[pl.ds(pl.multiple_of(dev * r, 16), r), :]


def _place_own(loc, blocks, name):
    r, D = loc.shape
    tr = _pick(r, (400, 256, 200, 128, 64, 16))

    def body(idx_ref, l_ref, o_ref):
        del idx_ref
        o_ref[...] = l_ref[...]

    return pl.pallas_call(
        body, name=name,
        grid_spec=pltpu.PrefetchScalarGridSpec(
            num_scalar_prefetch=1, grid=(r // tr,),
            in_specs=[pl.BlockSpec((tr, D), lambda i, idx: (i, 0))],
            out_specs=pl.BlockSpec((tr, D), lambda i, idx: (idx[0] * (r // tr) + i, 0))),
        out_shape=jax.ShapeDtypeStruct((N_DEV * r, D), loc.dtype),
        compiler_params=_params(("arbitrary",)),
    )(blocks, loc)


def _gather_ici_phase(locs, fulls):
    rs = [a.shape[0] for a in locs]
    n = len(locs)

    def build(ins, outs):
        c, chips, num = _mesh_place()
        me = num(chips[0], c)
        targets = [((*chips[0], 1 - c), num(chips[0], 1 - c))] + [((*ch, c), num(ch, c)) for ch in chips[1:]]
        sends, recvs = [], []
        for dev, dnum in targets:
            for i, r in enumerate(rs):
                sends.append((ins[i], _rows(outs[i], r, me), dev))
                recvs.append(_rows(outs[i], r, dnum))
        return sends, recvs

    shapes = [jax.ShapeDtypeStruct(a.shape, a.dtype) for a in fulls]
    return _Phase(list(locs) + list(fulls), shapes, {n + i: i for i in range(n)}, 4 * n, build)


def _gather_d2d_phase(fulls, rs):
    def build(ins, outs):
        c, chips, num = _mesh_place()
        sib = (*chips[0], 1 - c)
        sends, recvs = [], []
        for ch in chips[1:]:
            for i, r in enumerate(rs):
                blk = _rows(outs[i], r, num(ch, c))
                sends.append((blk, blk, sib))
                recvs.append(_rows(outs[i], r, num(ch, 1 - c)))
        return sends, recvs

    shapes = [jax.ShapeDtypeStruct(a.shape, a.dtype) for a in fulls]
    return _Phase(fulls, shapes, {i: i for i in range(len(fulls))}, 3 * len(fulls), build)


def _reduce_d2d_phase(grads, rs):
    def build(ins, outs):
        c, chips, num = _mesh_place()
        sib = (*chips[0], 1 - c)
        sends, recvs = [], []
        for j, ch in enumerate(chips):
            for i, r in enumerate(rs):
                sends.append((_rows(ins[i], r, num(ch, 1 - c)), outs[i].at[j], sib))
                recvs.append(outs[i].at[j])
        return sends, recvs

    shapes = [jax.ShapeDtypeStruct((4, r, g.shape[1]), g.dtype) for g, r in zip(grads, rs)]
    return _Phase(grads, shapes, {}, 4 * len(grads), build)


def _reduce_ici_phase(parts):
    def build(ins, outs):
        c, chips, _ = _mesh_place()
        sends, recvs = [], []
        for t in range(1, 4):
            for i in range(len(parts)):
                sends.append((ins[i].at[t], outs[i].at[t - 1], (*chips[t], c)))
                recvs.append(outs[i].at[t - 1])
        return sends, recvs

    shapes = [jax.ShapeDtypeStruct((3,) + p.shape[1:], p.dtype) for p in parts]
    return _Phase(parts, shapes, {}, 3 * len(parts), build)


def _pair_sum(g, got, blocks, name):
    n, r, D = got.shape
    tr = _pick(r, (400, 256, 200, 128, 64, 16))

    def body(idx_ref, g_ref, r_ref, o_ref):
        del idx_ref
        o_ref[...] = (g_ref[...].astype(F32) + r_ref[...].astype(F32)).astype(o_ref.dtype)

    blk = pl.BlockSpec((None, tr, D), lambda j, i, idx: (j, i, 0))
    return pl.pallas_call(
        body, name=name,
        grid_spec=pltpu.PrefetchScalarGridSpec(
            num_scalar_prefetch=1, grid=(n, r // tr),
            in_specs=[pl.BlockSpec((tr, D), lambda j, i, idx: (idx[j] * (r // tr) + i, 0)), blk],
            out_specs=blk),
        out_shape=jax.ShapeDtypeStruct(got.shape, got.dtype),
        compiler_params=_params(("arbitrary", "arbitrary")),
    )(blocks, g, got)


def _chip_sum(p, r, name, layer, into=None):
    _, R, D = p.shape
    tr = _pick(R, (400, 256, 200, 128, 64, 16))

    def body(p_ref, r_ref, *rest):
        acc = p_ref[...].astype(F32)
        for t in range(3):
            acc = acc + r_ref[t].astype(F32)
        rest[-1][...] = acc

    args = [p, r] + ([] if into is None else [into])
    return pl.pallas_call(
        body, name=name, grid=(R // tr,),
        in_specs=[pl.BlockSpec((None, tr, D), lambda i: (0, i, 0)), pl.BlockSpec((3, tr, D), lambda i: (0, i, 0))]
        + ([] if into is None else [_ANY]),
        out_specs=pl.BlockSpec((None, tr, D), lambda i: (layer, i, 0)),
        out_shape=jax.ShapeDtypeStruct((DEPTH, R, D), F32),
        input_output_aliases={} if into is None else {2: 0},
        compiler_params=_params(("parallel",)))(*args)


def _allreduce_small(vec):
    R, C = vec.shape

    def body(v_ref, o_ref, buf, send_sems, recv_sems):
        me, peers = _me_and_peers()
        buf[me] = v_ref[...]
        sends = []
        for k, (pid, _) in enumerate(peers):
            cp = pltpu.make_async_remote_copy(src_ref=v_ref, dst_ref=buf.at[me], send_sem=send_sems.at[k],
                                              recv_sem=recv_sems.at[k], device_id=pid, device_id_type=MESH)
            cp.start()
            sends.append(cp)
        for k, (pid, pnum) in enumerate(peers):
            pltpu.make_async_remote_copy(src_ref=v_ref, dst_ref=buf.at[pnum], send_sem=send_sems.at[k],
                                         recv_sem=recv_sems.at[k], device_id=pid, device_id_type=MESH).wait_recv()
        for cp in sends:
            cp.wait_send()
        acc = buf[0]
        for d in range(1, N_DEV):
            acc = acc + buf[d]
        o_ref[...] = acc

    vm = pl.BlockSpec(memory_space=pltpu.VMEM)
    return pl.pallas_call(
        body, name="allreduce_small",
        in_specs=[vm], out_specs=vm,
        out_shape=jax.ShapeDtypeStruct((R, C), F32),
        scratch_shapes=[pltpu.VMEM((N_DEV, R, C), F32), pltpu.SemaphoreType.DMA((N_DEV - 1,)),
                        pltpu.SemaphoreType.DMA((N_DEV - 1,))],
        compiler_params=pltpu.CompilerParams(has_side_effects=True),
    )(vec)


def _adamw(w, g, m, v, name):
    R, C = w.shape
    tr = _pick(R, (512, 400, 256, 128, 64, 32, 16, 8)) if R >= 8 else R
    c1 = 1.0 - ADAM_B1 ** ADAM_STEP
    c2 = 1.0 - ADAM_B2 ** ADAM_STEP

    def body(w_ref, g_ref, m_ref, v_ref, d_ref, mo_ref, vo_ref):
        gg = g_ref[...]
        mn = ADAM_B1 * m_ref[...] + (1.0 - ADAM_B1) * gg
        vn = ADAM_B2 * v_ref[...] + (1.0 - ADAM_B2) * (gg * gg)
        d_ref[...] = -ADAM_LR * ((mn / c1) / (jnp.sqrt(vn / c2) + ADAM_EPS) + ADAM_WD * w_ref[...])
        mo_ref[...] = mn
        vo_ref[...] = vn

    blk = pl.BlockSpec((tr, C), lambda i: (i, 0))
    sh = jax.ShapeDtypeStruct((R, C), F32)
    return pl.pallas_call(
        body, name=name, grid=(R // tr,), in_specs=[blk] * 4, out_specs=[blk] * 3, out_shape=[sh] * 3,
        compiler_params=_params(("parallel",)),
    )(w, g, m, v)


def _lb_param_grad(lb_param, dlb):
    L, C = lb_param.shape

    def body(p_ref, d_ref, o_ref):
        lbp = p_ref[...]
        d = d_ref[...]
        mx = jnp.max(lbp, axis=0, keepdims=True)
        e = jnp.exp(lbp - mx)
        p = e / jnp.sum(e, axis=0, keepdims=True)
        tot = jnp.sum(d, axis=0, keepdims=True)
        dps = []
        rest = tot
        for j in range(L):
            dps.append(rest - tot if j == 0 else rest)
            rest = rest - d[j:j + 1]
        dp = jnp.concatenate(dps, axis=0)
        o_ref[...] = p * (dp - jnp.sum(p * dp, axis=0, keepdims=True))

    vm = pl.BlockSpec(memory_space=pltpu.VMEM)
    return pl.pallas_call(body, name="lb_param_grad", in_specs=[vm, vm], out_specs=vm,
                          out_shape=jax.ShapeDtypeStruct((L, C), F32))(lb_param, dlb)


def _pack_small(loss_part, dg_pre, dg_post, dlb, dg_head, dsinks):
    pad8 = lambda a: jnp.pad(a.reshape(-1, 128), ((0, 8 - DEPTH), (0, 0)))
    rows = [dg_pre.reshape(-1, 128), dg_post.reshape(-1, 128), dlb.reshape(-1, 128), pad8(dg_head), pad8(dsinks),
            loss_part]
    return jnp.concatenate(rows, axis=0)


def _unpack_small(vec):
    n = DEPTH * D_MODEL // 128
    o = 0
    dg_pre = vec[o:o + n].reshape(DEPTH, D_MODEL); o += n
    dg_post = vec[o:o + n].reshape(DEPTH, D_MODEL); o += n
    dlb = vec[o:o + n].reshape(DEPTH, HG_WIDTH); o += n
    dg_head = vec[o:o + DEPTH]; o += 8
    dsinks = vec[o:o + DEPTH, :ATT_HEADS]; o += 8
    loss = jnp.sum(vec[o:o + 8])
    return loss, dg_pre, dg_post, dlb, dg_head, dsinks


def kernel(x, w_in, w_out, g_pre, g_post, lb_param, g_head, sinks, loss_target, m_w_in, m_w_out, m_g_pre, m_g_post, m_lb_param, m_g_head, m_sinks, v_w_in, v_w_out, v_g_pre, v_g_post, v_lb_param, v_g_head, v_sinks):
    tr = lambda a: jnp.swapaxes(a, 1, 2)
    w_in_t = tr(w_in)
    (loss_part, dx, gw_in_t, gw_out, dg_pre, dg_post, dlb, dg_head, dsinks) = _step(
        x, loss_target, g_pre, g_post, lb_param, g_head, sinks, shards=(w_in_t.astype(BF16), w_out.astype(BF16)))

    small = _allreduce_small(_pack_small(loss_part, dg_pre, dg_post, dlb, dg_head, dsinks))
    loss, gg_pre, gg_post, gdlb, gg_head, gsinks = _unpack_small(small)
    glb = _lb_param_grad(lb_param, gdlb)

    grads = [gw_in_t, gw_out, gg_pre, gg_post, glb, gg_head, gsinks]
    ws = [w_in_t, w_out, g_pre, g_post, lb_param, g_head, sinks]
    ms = [tr(m_w_in), m_w_out, m_g_pre, m_g_post, m_lb_param, m_g_head, m_sinks]
    vs = [tr(v_w_in), v_w_out, v_g_pre, v_g_post, v_lb_param, v_g_head, v_sinks]
    names = ["w_in", "w_out", "g_pre", "g_post", "lb_param", "g_head", "sinks"]
    deltas, new_m, new_v = [], [], []
    for w, g, m, v, nm in zip(ws, grads, ms, vs, names):
        sh = w.shape
        two = lambda a: a.reshape(-1, sh[-1])
        d, mn, vn = _adamw(two(w), two(g), two(m), two(v), "adamw_" + nm)
        deltas.append(d.reshape(sh))
        new_m.append(mn.reshape(sh))
        new_v.append(vn.reshape(sh))
    grads[0], deltas[0], new_m[0], new_v[0] = tr(grads[0]), tr(deltas[0]), tr(new_m[0]), tr(new_v[0])
    return (loss, dx, *grads, *deltas, *new_m, *new_v)
```

```python
import math

import numpy as np
import jax
import jax.numpy as jnp
from jax import lax
from jax.experimental import pallas as pl
from jax.experimental.pallas import tpu as pltpu

F32 = jnp.float32
BF16 = jnp.bfloat16

D_MODEL = 1024
DEPTH = 2
HG_HEADS = 8
HG_DIM = 128
HG_WIDTH = HG_HEADS * HG_DIM
CHUNK = 64
ATT_HEADS = 16
ATT_DIM = 64
ATT_WIDTH = ATT_HEADS * ATT_DIM
KV_WIDTH = 128
ATT_BLOCK = 128
ATT_SCALE = 1.0 / math.sqrt(ATT_DIM)
ROPE_THETA = 10000.0
NORM_EPS = 1e-6
NEG_INF = -1e30
LB_FLOOR = 1e-20
N_H = 4 * HG_WIDTH
N_A = 2 * ATT_WIDTH + 2 * KV_WIDTH
IN_WIDTH = N_H + N_A
MIX_WIDTH = HG_WIDTH + ATT_WIDTH

ADAM_LR = 0.001
ADAM_B1 = 0.9
ADAM_B2 = 0.999
ADAM_EPS = 1e-08
ADAM_WD = 0.01
ADAM_STEP = 10

N_DEV = 8
MESH = pl.DeviceIdType.MESH
VMEM_LIMIT = 56 * 1024 * 1024

NN = ((1,), (0,))
NT = ((1,), (1,))
TN = ((0,), (0,))


def _dot(a, b, dims):
    return lax.dot_general(a.astype(BF16), b.astype(BF16), (dims, ((), ())), preferred_element_type=F32)


def _params(sem=None, **kw):
    return pltpu.CompilerParams(dimension_semantics=sem, vmem_limit_bytes=VMEM_LIMIT, **kw)


def _sigmoids(x):
    e = jnp.exp(-jnp.abs(x))
    r = 1.0 / (1.0 + e)
    er = e * r
    pos = x >= 0.0
    return jnp.where(pos, r, er), jnp.where(pos, er, r)


def _silu(x):
    return x * _sigmoids(x)[0]


def _silu_and_grad(x):
    s, ns = _sigmoids(x)
    return x * s, s * (1.0 + x * ns)


def _pick(n, prefs):
    for p in prefs:
        if n % p == 0:
            return p
    return n


def _inproj(x2, g, w, name):
    T, D = x2.shape
    tm = _pick(T, (512, 256, 128))
    nchunk = 1024

    def body(x_ref, g_ref, w_ref, oh_ref, oa_ref, h_ref):
        x = x_ref[...]
        r = lax.rsqrt(jnp.mean(x * x, axis=-1, keepdims=True) + NORM_EPS)
        h = ((x * r) * g_ref[...]).astype(BF16)
        h_ref[...] = h
        for j in range(0, N_H, nchunk):
            oh_ref[:, j:j + nchunk] = lax.dot_general(h, w_ref[j:j + nchunk, :], (NT, ((), ())),
                                                      preferred_element_type=F32)
        for j in range(0, N_A, N_A // 2):
            oa_ref[:, j:j + N_A // 2] = lax.dot_general(h, w_ref[N_H + j:N_H + j + N_A // 2, :], (NT, ((), ())),
                                                        preferred_element_type=F32)

    row = lambda w_: pl.BlockSpec((tm, w_), lambda i: (i, 0))
    return pl.pallas_call(
        body, name=name,
        grid=(T // tm,),
        in_specs=[row(D), pl.BlockSpec((1, D), lambda i: (0, 0)),
                  pl.BlockSpec((IN_WIDTH, D), lambda i: (0, 0), pipeline_mode=pl.Buffered(1))],
        out_specs=[row(N_H), row(N_A), row(D)],
        out_shape=[jax.ShapeDtypeStruct((T, N_H), F32), jax.ShapeDtypeStruct((T, N_A), F32),
                   jax.ShapeDtypeStruct((T, D), BF16)],
        compiler_params=_params(("parallel",)),
    )(x2, g, w)


def _mm_tn(pieces, b, name, out_dtype=BF16):
    T, m = b.shape
    tn = 256
    counts = [p.shape[1] // tn for p in pieces]
    starts = [sum(counts[:i]) for i in range(len(pieces))]
    n_p = len(pieces)

    def body(*refs):
        b_ref, o_ref = refs[n_p], refs[n_p + 1]
        i = pl.program_id(0)
        for p in range(n_p):
            @pl.when((i >= starts[p]) & (i < starts[p] + counts[p]))
            def _(p=p):
                o_ref[...] = lax.dot_general(refs[p][...], b_ref[...], (TN, ((), ())),
                                             preferred_element_type=F32).astype(out_dtype)

    piece_spec = lambda s, c: pl.BlockSpec((T, tn), lambda i: (0, jnp.clip(i - s, 0, c - 1)))
    return pl.pallas_call(
        body, name=name,
        grid=(sum(counts),),
        in_specs=[piece_spec(s, c) for s, c in zip(starts, counts)]
        + [pl.BlockSpec((T, m), lambda i: (0, 0), pipeline_mode=pl.Buffered(1))],
        out_specs=pl.BlockSpec((tn, m), lambda i: (i, 0)),
        out_shape=jax.ShapeDtypeStruct((sum(counts) * tn, m), out_dtype),
        compiler_params=_params(("arbitrary",)),
    )(*pieces, b)


_LEVELS = (0, 1, 2, 4, 8, 16, 32)
_CUM_L = (2, 4, 8, 16, 32, 64)
_ALL_KINDS = tuple(("c", L) for L in _CUM_L) + tuple(("r", L) for L in _CUM_L)
_MXU_KINDS = (("c", 2), ("c", 4), ("c", CHUNK), ("r", 2), ("r", 4))
N_CUM = len(_ALL_KINDS) * CHUNK
N_CUM_F = len(_MXU_KINDS) * CHUNK


def _cum_matrices():
    t = np.arange(CHUNK)[:, None]
    r = np.arange(CHUNK)[None, :]

    def mat(kind):
        c, L = kind
        return ((r // L == t // L) & ((r <= t) if c == "c" else (r > t))).astype(np.float32)

    fwd = np.concatenate([mat(kd) for kd in _MXU_KINDS], axis=0)
    full = np.concatenate([mat(kd) for kd in _ALL_KINDS], axis=0)
    return jnp.asarray(fwd, BF16), jnp.asarray(full.T.copy(), BF16)


def _level_masks():
    t = np.arange(CHUNK)[:, None]
    s = np.arange(CHUNK)[None, :]
    ms = []
    for L in _LEVELS:
        if L == 0:
            ms.append(t == s)
        else:
            ms.append((t // (2 * L) == s // (2 * L)) & ((t // L) % 2 == 1) & ((s // L) % 2 == 0))
    return jnp.asarray(np.stack(ms).astype(np.float32))


def _split3(x):
    hi = x.astype(BF16)
    r1 = x - hi.astype(F32)
    mid = r1.astype(BF16)
    lo = (r1 - mid.astype(F32)).astype(BF16)
    return hi, mid, lo


def _cum3(ts, x, terms=3):
    d = lambda p: lax.dot_general(ts, p, (NN, ((), ())), preferred_element_type=F32)
    return sum(d(p) for p in _split3(x)[:terms])


def _lb_terms(lbp, layer):
    mx = jnp.max(lbp, axis=0, keepdims=True)
    e = jnp.exp(lbp - mx)
    p = e / jnp.sum(e, axis=0, keepdims=True)
    cum = p[0:1]
    for j in range(1, layer + 1):
        cum = cum + p[j:j + 1]
    lb = cum - p[0:1]
    lbf = jnp.maximum(lb, LB_FLOOR)
    return dict(lbf=lbf, one_m=1.0 - lb, kcorr=lb - lbf, ind=jnp.where(lb > LB_FLOOR, 1.0, 0.0))


def _gate(x, lt):
    sig, nsig = _sigmoids(x)
    f = lt["lbf"] + lt["one_m"] * sig
    return jnp.log(f), lt["one_m"] * nsig + lt["kcorr"], f, sig, nsig


def _ck(x, ci):
    return x[ci * CHUNK:(ci + 1) * CHUNK]


def _block_cums(ts, g, nc):
    cs = [_cum3(ts, _ck(g, ci)) for ci in range(nc)]
    out = {kind: jnp.concatenate([c[CHUNK * i:CHUNK * (i + 1)] for c in cs], axis=0)
           for i, kind in enumerate(_MXU_KINDS)}
    b = out[("c", CHUNK)]
    ng = CHUNK // 8
    last = b.reshape(nc, ng, 8, HG_DIM)[:, :, 7:8, :]
    zero = jnp.zeros((nc, 1, 1, HG_DIM), F32)

    def spread(groups):
        return jnp.broadcast_to(jnp.concatenate(groups, axis=1), (nc, ng, 8, HG_DIM)).reshape(nc * CHUNK, HG_DIM)

    def get(kind):
        if kind in out:
            return out[kind]
        c, L = kind
        nb = L // 8
        first = lambda r: (r // nb) * nb
        if c == "c":
            return b - spread([last[:, first(r) - 1:first(r)] if r >= nb else zero for r in range(ng)])
        return spread([last[:, first(r) + nb - 1:first(r) + nb] for r in range(ng)]) - b

    return get


def _level_factors(cums, g, L):
    if L == 0:
        return None, None
    if L == 1:
        return jnp.exp(g), None
    return jnp.exp(cums(("c", L))), jnp.exp(cums(("r", L)))


def _mul(a, e):
    return a if e is None else a * e


def _hg_block_fwd(qf, k, v, g, ts, m_ref, nc):
    cums = _block_cums(ts, g, nc)
    amat = [jnp.zeros((CHUNK, CHUNK), F32)] * nc
    for li, L in enumerate(_LEVELS):
        eq, ek = _level_factors(cums, g, L)
        ql, kl, m = _mul(qf, eq), _mul(k, ek), m_ref[li]
        amat = [amat[ci] + _dot(_ck(ql, ci), _ck(kl, ci), NT) * m for ci in range(nc)]
    b = cums(("c", CHUNK))
    kst = k * jnp.exp(cums(("r", CHUNK)))
    o = [_dot(amat[ci], _ck(v, ci), NN) for ci in range(nc)]
    kv = [_dot(_ck(v, ci), _ck(kst, ci), TN) for ci in range(nc)]
    dec = [jnp.exp(b[(ci + 1) * CHUNK - 1:(ci + 1) * CHUNK, :]) for ci in range(nc)]
    return o, dec, kv, qf * jnp.exp(b), amat


def _hg_block_bwd(qf, k, v, g, do, amat, ts, m_ref, nc):
    cums = _block_cums(ts, g, nc)
    dcs = {}
    da = [_dot(_ck(do, ci), _ck(v, ci), NT) for ci in range(nc)]
    dq = jnp.zeros_like(qf)
    dk = jnp.zeros_like(qf)
    dg = jnp.zeros_like(qf)
    for li, L in enumerate(_LEVELS):
        eq, ek = _level_factors(cums, g, L)
        ql, kl, m = _mul(qf, eq), _mul(k, ek), m_ref[li]
        qlb, klb = ql.astype(BF16), kl.astype(BF16)
        dal = [(da[ci] * m).astype(BF16) for ci in range(nc)]
        dql = jnp.concatenate([_dot(dal[ci], _ck(klb, ci), NN) for ci in range(nc)], axis=0)
        dkl = jnp.concatenate([_dot(dal[ci], _ck(qlb, ci), TN) for ci in range(nc)], axis=0)
        dq = dq + _mul(dql, eq)
        dk = dk + _mul(dkl, ek)
        if L == 1:
            dg = dg + dql * ql
        elif L > 1:
            dcs[("c", L)] = (dql * ql).astype(BF16)
            dcs[("r", L)] = (dkl * kl).astype(BF16)
    b = cums(("c", CHUNK))
    e64 = jnp.exp(b)
    er64 = jnp.exp(cums(("r", CHUNK)))
    qb = qf * e64
    return dict(dq=dq, dk=dk, dg=dg, dcs=dcs, e64=e64, er64=er64, qb=qb, kst=k * er64,
                dv=[_dot(amat[ci], _ck(do, ci), TN) for ci in range(nc)],
                dec=[jnp.exp(b[(ci + 1) * CHUNK - 1:(ci + 1) * CHUNK, :]) for ci in range(nc)],
                qd=[_dot(_ck(do, ci), _ck(qb, ci), TN) for ci in range(nc)])


def _hg_state_bwd(w, v, do, starts, ends, tst, nc):
    dqb = jnp.concatenate([_dot(_ck(do, ci), starts[ci], NN) for ci in range(nc)], axis=0)
    dkst = jnp.concatenate([_dot(_ck(v, ci), ends[ci], NN) for ci in range(nc)], axis=0)
    dq = w["dq"] + dqb * w["e64"]
    dk = w["dk"] + dkst * w["er64"]
    dv = jnp.concatenate([w["dv"][ci] + _dot(_ck(w["kst"], ci), ends[ci], NT) for ci in range(nc)], axis=0)
    trow = lax.broadcasted_iota(jnp.int32, (CHUNK, 1), 0)
    dtot = jnp.concatenate(
        [jnp.where(trow == CHUNK - 1, jnp.sum(ends[ci] * starts[ci], axis=0, keepdims=True) * w["dec"][ci], 0.0)
         for ci in range(nc)], axis=0)
    dcs = dict(w["dcs"])
    dcs[("c", CHUNK)] = (dqb * w["qb"] + dtot).astype(BF16)
    dcs[("r", CHUNK)] = (dkst * w["kst"]).astype(BF16)
    dgs = [_dot(tst, jnp.concatenate([_ck(dcs[kind], ci) for kind in _ALL_KINDS], axis=0), NN) for ci in range(nc)]
    return dq, dk, dv, w["dg"] + jnp.concatenate(dgs, axis=0)


def _hgrn_fwd(proj_h, u_rows, lb_param, g_head, layer, name, phase=None):
    B, S, _ = proj_h.shape
    sb = _pick(S, (1024, 512, 256, 128, 64))
    nc = sb // CHUNK
    ts, _ = _cum_matrices()

    def body(*refs):
        ins, outs, (st,), p_in, p_out, p_sems = _split_refs(refs, 8, 4, 1, phase)
        q_ref, f_ref, i_ref, z_ref, lbp_ref, gh_ref, ts_ref, m_ref = ins
        o_ref, u_ref, sts_ref, am_ref = outs
        h_id, b_id, s_id = pl.program_id(0), pl.program_id(1), pl.program_id(2)
        _hosted_start(phase, p_in, p_out, p_sems, (h_id == 0) & (b_id == 0) & (s_id == 0))

        @pl.when(s_id == 0)
        def _():
            st[...] = jnp.zeros_like(st)

        lt = _lb_terms(lbp_ref[...], layer)
        tsv = ts_ref[...]
        gh = gh_ref[...]
        logf, k = _gate(f_ref[...], lt)[:2]
        o_part, dec, kv, qb, amat = _hg_block_fwd(_silu(q_ref[...]), k, i_ref[...], logf, tsv, m_ref, nc)
        for ci in range(nc):
            am_ref[ci] = amat[ci].astype(BF16)
        cur = st[...]
        starts = []
        for ci in range(nc):
            sts_ref[ci] = cur
            starts.append(cur)
            cur = cur * dec[ci] + kv[ci]
        st[...] = cur
        o = jnp.concatenate([o_part[ci] + _dot(_ck(qb, ci), starts[ci], NT) for ci in range(nc)], axis=0)
        o_ref[...] = o
        r = lax.rsqrt(jnp.mean(o * o, axis=-1, keepdims=True) + NORM_EPS)
        u_ref[...] = (((o * r) * gh) * _silu(z_ref[...])).astype(BF16)
        _hosted_finish(phase, p_in, p_out, p_sems, (h_id == HG_HEADS - 1) & (b_id == B - 1) & (s_id == S // sb - 1))

    col = lambda base: pl.BlockSpec((None, sb, HG_DIM), lambda h, b, s: (b, s, base + h))
    p_ispecs, p_ospecs, p_oshapes, p_alias, p_scratch, p_args = _host_phase(phase, 8, 4)
    res = pl.pallas_call(
        body, name=name,
        grid=(HG_HEADS, B, S // sb),
        in_specs=[col(0), col(HG_HEADS), col(2 * HG_HEADS), col(3 * HG_HEADS),
                  pl.BlockSpec((DEPTH, HG_DIM), lambda h, b, s: (0, h)),
                  pl.BlockSpec((1, HG_DIM), lambda h, b, s: (0, 0)),
                  pl.BlockSpec((N_CUM_F, CHUNK), lambda h, b, s: (0, 0)),
                  pl.BlockSpec((len(_LEVELS), CHUNK, CHUNK), lambda h, b, s: (0, 0, 0))] + p_ispecs,
        out_specs=[col(0), col(0),
                   pl.BlockSpec((None, None, nc, HG_DIM, HG_DIM), lambda h, b, s: (b, h, s, 0, 0)),
                   pl.BlockSpec((None, None, nc, CHUNK, CHUNK), lambda h, b, s: (b, h, s, 0, 0))] + p_ospecs,
        out_shape=[jax.ShapeDtypeStruct((B, S, HG_WIDTH), F32),
                   jax.ShapeDtypeStruct((B, S, u_rows), BF16),
                   jax.ShapeDtypeStruct((B, HG_HEADS, S // CHUNK, HG_DIM, HG_DIM), F32),
                   jax.ShapeDtypeStruct((B, HG_HEADS, S // CHUNK, CHUNK, CHUNK), BF16)] + p_oshapes,
        input_output_aliases=p_alias,
        scratch_shapes=[pltpu.VMEM((HG_DIM, HG_DIM), F32)] + p_scratch,
        compiler_params=_params(("arbitrary", "arbitrary", "arbitrary")),
    )(proj_h, proj_h, proj_h, proj_h, lb_param, g_head, ts, _level_masks(), *p_args)
    return res[0], res[1], (res[2], res[3]), list(res[4:])


def _hgrn_bwd(proj_h, o_h, du, kept, lb_param, g_head, layer, name, phase=None):
    B, S, _ = proj_h.shape
    sb = _pick(S, (512, 256, 128, 64))
    nc = sb // CHUNK
    ns = S // sb
    ts, tst = _cum_matrices()

    def body(*refs):
        ins, outs, (dst,), p_in, p_out, p_sems = _split_refs(refs, 13, 6, 1, phase)
        q_ref, f_ref, i_ref, z_ref, o_ref, du_ref, sts_ref, am_ref, lbp_ref, gh_ref, ts_ref, tst_ref, m_ref = ins
        dq_ref, df_ref, di_ref, dz_ref, dlb_ref, dgh_ref = outs
        h_id, b_id, s_id = pl.program_id(0), pl.program_id(1), pl.program_id(2)
        _hosted_start(phase, p_in, p_out, p_sems, (h_id == 0) & (b_id == 0) & (s_id == 0))

        @pl.when(s_id == 0)
        def _():
            dst[...] = jnp.zeros_like(dst)

        @pl.when((b_id == 0) & (s_id == 0))
        def _():
            dlb_ref[...] = jnp.zeros_like(dlb_ref)

        @pl.when((h_id == 0) & (b_id == 0) & (s_id == 0))
        def _():
            dgh_ref[...] = jnp.zeros_like(dgh_ref)

        lt = _lb_terms(lbp_ref[...], layer)
        gh = gh_ref[...]
        tsv = ts_ref[...]
        tstv = tst_ref[...]
        logf, k, f, sig, nsig = _gate(f_ref[...], lt)
        o = o_ref[...]
        dub = du_ref[...]
        r = lax.rsqrt(jnp.mean(o * o, axis=-1, keepdims=True) + NORM_EPS)
        n = o * r
        sg, sg_grad = _silu_and_grad(z_ref[...])
        dz_ref[...] = (dub * (n * gh) * sg_grad).astype(BF16)
        dgh_ref[...] += jnp.sum(dub * sg * n, axis=0, keepdims=True)
        dn = dub * sg * gh
        do = r * (dn - n * jnp.mean(dn * n, axis=-1, keepdims=True))
        v = i_ref[...]
        qf, qf_grad = _silu_and_grad(q_ref[...])
        w = _hg_block_bwd(qf, k, v, logf, do, [am_ref[ci] for ci in range(nc)], tsv, m_ref, nc)
        cur = dst[...]
        ends = [None] * nc
        for ci in reversed(range(nc)):
            ends[ci] = cur
            cur = cur * w["dec"][ci] + w["qd"][ci]
        dst[...] = cur
        dq, dk, dv, dg = _hg_state_bwd(w, v, do, [sts_ref[ci] for ci in range(nc)], ends, tstv, nc)
        di_ref[...] = dv.astype(BF16)
        dq_ref[...] = (dq * qf_grad).astype(BF16)
        scaled = (dg - f * dk) / f
        df_ref[...] = (scaled * lt["one_m"] * sig * nsig).astype(BF16)
        dlb_ref[...] += jnp.sum(scaled * (lt["ind"] - sig), axis=0, keepdims=True)
        _hosted_finish(phase, p_in, p_out, p_sems, (h_id == HG_HEADS - 1) & (b_id == B - 1) & (s_id == ns - 1))

    col = lambda base: pl.BlockSpec((None, sb, HG_DIM), lambda h, b, s: (b, ns - 1 - s, base + h))
    out_col = pl.BlockSpec((None, sb, HG_DIM), lambda h, b, s: (b, ns - 1 - s, h))
    dt = jax.ShapeDtypeStruct((B, S, HG_WIDTH), BF16)
    p_ispecs, p_ospecs, p_oshapes, p_alias, p_scratch, p_args = _host_phase(phase, 13, 6)
    res = pl.pallas_call(
        body, name=name,
        grid=(HG_HEADS, B, ns),
        in_specs=[col(0), col(HG_HEADS), col(2 * HG_HEADS), col(3 * HG_HEADS), col(0), col(0),
                  pl.BlockSpec((None, None, nc, HG_DIM, HG_DIM), lambda h, b, s: (b, h, ns - 1 - s, 0, 0)),
                  pl.BlockSpec((None, None, nc, CHUNK, CHUNK), lambda h, b, s: (b, h, ns - 1 - s, 0, 0)),
                  pl.BlockSpec((DEPTH, HG_DIM), lambda h, b, s: (0, h)),
                  pl.BlockSpec((1, HG_DIM), lambda h, b, s: (0, 0)),
                  pl.BlockSpec((N_CUM_F, CHUNK), lambda h, b, s: (0, 0)),
                  pl.BlockSpec((CHUNK, N_CUM), lambda h, b, s: (0, 0)),
                  pl.BlockSpec((len(_LEVELS), CHUNK, CHUNK), lambda h, b, s: (0, 0, 0))] + p_ispecs,
        out_specs=[out_col, out_col, out_col, out_col,
                   pl.BlockSpec((1, HG_DIM), lambda h, b, s: (0, h)),
                   pl.BlockSpec((1, HG_DIM), lambda h, b, s: (0, 0))] + p_ospecs,
        out_shape=[dt, dt, dt, dt, jax.ShapeDtypeStruct((1, HG_WIDTH), F32),
                   jax.ShapeDtypeStruct((1, HG_DIM), F32)] + p_oshapes,
        input_output_aliases=p_alias,
        scratch_shapes=[pltpu.VMEM((HG_DIM, HG_DIM), F32)] + p_scratch,
        compiler_params=_params(("arbitrary", "arbitrary", "arbitrary")),
    )(proj_h, proj_h, proj_h, proj_h, o_h, du, kept[0], kept[1], lb_param, g_head, ts, tst, _level_masks(), *p_args)
    return tuple(res[:6]) + (list(res[6:]),)


def _rope_tables(S):
    half = ATT_DIM // 2
    inv_freq = ROPE_THETA ** (-jnp.arange(half, dtype=F32) / half)
    ang = jnp.arange(S).astype(F32)[:, None] * inv_freq[None, :]
    cos = jnp.cos(ang)
    sin = jnp.sin(ang)
    cos = jnp.concatenate([cos, cos, cos, cos], axis=1)
    sin = jnp.concatenate([-sin, sin, -sin, sin], axis=1)
    return cos, sin


def _attn_common():
    lane = lax.broadcasted_iota(jnp.int32, (1, 2 * ATT_DIM), 1)
    first_half = (lane % ATT_DIM) < (ATT_DIM // 2)
    left = lane < ATT_DIM

    def swap(x):
        return jnp.where(first_half, pltpu.roll(x, 128 - ATT_DIM // 2, 1), pltpu.roll(x, ATT_DIM // 2, 1))

    def rope(x, cos, sin):
        return x * cos + swap(x) * sin

    def rope_bwd(dy, cos, sin):
        return dy * cos + swap(dy * sin)

    def dup(x):
        xs = pltpu.roll(x, ATT_DIM, 1)
        return [jnp.where(left, x, xs), jnp.where(left, xs, x)]

    return left, rope, rope_bwd, dup


GROUP = ATT_HEADS // 2
GROUP_ROWS = GROUP * ATT_BLOCK


def _attn_bias(i):
    r = lax.broadcasted_iota(jnp.int32, (ATT_BLOCK, 2 * ATT_BLOCK), 0)
    c = lax.broadcasted_iota(jnp.int32, (ATT_BLOCK, 2 * ATT_BLOCK), 1)
    ok = (c > r) & (c <= r + ATT_BLOCK) & ((c >= ATT_BLOCK) | (i > 0))
    return jnp.where(ok, 0.0, NEG_INF)


def _stack_heads(pairs, left):
    rows = []
    for x in pairs:
        rows += [jnp.where(left, x, 0.0), jnp.where(left, 0.0, x)]
    return jnp.concatenate(rows, axis=0)


def _unstack_heads(y, left, pp):
    r0 = 2 * pp * ATT_BLOCK
    return jnp.where(left, y[r0:r0 + ATT_BLOCK], y[r0 + ATT_BLOCK:r0 + 2 * ATT_BLOCK])


def _row_sums(x):
    return _dot(x, jnp.ones((x.shape[1], 128), BF16), NN)


def _attn_probs(qs, kd, vd, sink, bias):
    n = range(len(qs))
    rows = qs[0].shape[0]
    s = [(_dot(qs[j], kd[j], NT).reshape(rows // ATT_BLOCK, ATT_BLOCK, 2 * ATT_BLOCK) * ATT_SCALE + bias[None])
         .reshape(rows, 2 * ATT_BLOCK) for j in n]
    m = [jnp.max(jnp.maximum(jnp.maximum(s[j][:, :128], s[j][:, 128:]), sink[j]), axis=-1, keepdims=True) for j in n]
    pu = [jnp.exp(s[j] - m[j]) for j in n]
    es = [jnp.exp(sink[j] - m[j]) for j in n]
    ones = jnp.ones((2 * ATT_BLOCK, 128), BF16)
    ov = [_dot(pu[j], jnp.concatenate([vd[j].astype(BF16), ones], axis=1), NN) for j in n]
    inv = [1.0 / (ov[j][:, 128:] + es[j]) for j in n]
    return ([pu[j] * jnp.concatenate([inv[j], inv[j]], axis=1) for j in n], [es[j] * inv[j] for j in n],
            [ov[j][:, :128] * inv[j] for j in n])


def _sink_rows(sinks_l):
    return jnp.broadcast_to(jnp.repeat(sinks_l, ATT_BLOCK)[:, None], (ATT_HEADS * ATT_BLOCK, 128))


_Z0 = (2 * ATT_WIDTH + 2 * KV_WIDTH - ATT_WIDTH) // 256


def _attn_fwd(proj_a, u, sinks_l, cos, sin, name, phase=None):
    B, S, _ = proj_a.shape
    nb = S // ATT_BLOCK

    def body(*refs):
        ins, (u_ref,), _, p_in, p_out, p_sems = _split_refs(refs, 13, 1, 0, phase)
        q_ref, kvc_ref, kvp_ref, z0, z1, z2, z3, cos_ref, sin_ref, cosp_ref, sinp_ref, sinks_ref, _ = ins
        i = pl.program_id(1)
        _hosted_start(phase, p_in, p_out, p_sems, (pl.program_id(0) == 0) & (i == 0))
        left, rope, _, dup = _attn_common()
        cos_c, sin_c = cos_ref[...], sin_ref[...]
        kvc = kvc_ref[...]
        kvp = kvp_ref[...]
        kw = jnp.concatenate([rope(kvp[:, :KV_WIDTH], cosp_ref[...], sinp_ref[...]),
                              rope(kvc[:, :KV_WIDTH], cos_c, sin_c)], axis=0)
        vw = jnp.concatenate([kvp[:, KV_WIDTH:], kvc[:, KV_WIDTH:]], axis=0)
        kd, vd = dup(kw), dup(vw)
        bias = _attn_bias(i)
        zs = (z0, z1, z2, z3)
        pairs = [range(4 * kvh, 4 * kvh + 4) for kvh in range(2)]
        qs = [_stack_heads([rope(q_ref[:, 128 * pr:128 * (pr + 1)], cos_c, sin_c) for pr in pairs[kvh]], left)
              for kvh in range(2)]
        sink = [sinks_ref[kvh * GROUP_ROWS:(kvh + 1) * GROUP_ROWS, :] for kvh in range(2)]
        o = _attn_probs(qs, kd, vd, sink, bias)[2]
        for kvh in range(2):
            for pp, pr in enumerate(pairs[kvh]):
                z = zs[pr // 2][:, 128 * (pr % 2):128 * (pr % 2 + 1)]
                u_ref[:, 128 * pr:128 * (pr + 1)] = (_unstack_heads(o[kvh], left, pp) * _silu(z)).astype(BF16)
        _hosted_finish(phase, p_in, p_out, p_sems, (pl.program_id(0) == B - 1) & (i == nb - 1))

    rowblk = lambda w, cb: pl.BlockSpec((None, ATT_BLOCK, w), lambda b, i: (b, i, cb))
    tab = pl.BlockSpec((ATT_BLOCK, 128), lambda b, i: (i, 0))
    tabp = pl.BlockSpec((ATT_BLOCK, 128), lambda b, i: (jnp.maximum(i - 1, 0), 0))
    p_ispecs, p_ospecs, p_oshapes, p_alias, p_scratch, p_args = _host_phase(phase, 13, 1)
    res = pl.pallas_call(
        body, name=name,
        grid=(B, nb),
        in_specs=[rowblk(ATT_WIDTH, 0), rowblk(256, 4),
                  pl.BlockSpec((None, ATT_BLOCK, 256), lambda b, i: (b, jnp.maximum(i - 1, 0), 4)),
                  rowblk(256, _Z0), rowblk(256, _Z0 + 1), rowblk(256, _Z0 + 2), rowblk(256, _Z0 + 3),
                  tab, tab, tabp, tabp,
                  pl.BlockSpec((ATT_HEADS * ATT_BLOCK, 128), lambda b, i: (0, 0)),
                  pl.BlockSpec(memory_space=pl.ANY)] + p_ispecs,
        out_specs=[pl.BlockSpec((None, ATT_BLOCK, ATT_WIDTH), lambda b, i: (b, i, 1))] + p_ospecs,
        out_shape=[jax.ShapeDtypeStruct(u.shape, BF16)] + p_oshapes,
        input_output_aliases={12: 0, **p_alias},
        scratch_shapes=p_scratch,
        compiler_params=_params(("arbitrary", "arbitrary")),
    )(proj_a, proj_a, proj_a, proj_a, proj_a, proj_a, proj_a, cos, sin, cos, sin, sinks_l, u, *p_args)
    return res[0], list(res[1:])


def _attn_bwd(proj_a, du, sinks_l, cos, sin, name, phase=None):
    B, S, _ = proj_a.shape
    nb = S // ATT_BLOCK

    def body(*refs):
        ins, outs, (carry, sk_acc), p_in, p_out, p_sems = _split_refs(refs, 13, 4, 2, phase)
        q_ref, kvc_ref, kvp_ref, z0, z1, z2, z3, du_ref, cos_ref, sin_ref, cosp_ref, sinp_ref, sinks_ref = ins
        dq_ref, dkv_ref, dz_ref, dsk_ref = outs
        b_id, i = pl.program_id(0), pl.program_id(1)
        _hosted_start(phase, p_in, p_out, p_sems, (b_id == 0) & (i == 0))

        @pl.when((b_id == 0) & (i == 0))
        def _():
            sk_acc[...] = jnp.zeros_like(sk_acc)

        @pl.when(i == 0)
        def _():
            carry[...] = jnp.zeros_like(carry)

        @pl.when(i < nb)
        def _():
            left, rope, rope_bwd, dup = _attn_common()
            cos_c, sin_c = cos_ref[...], sin_ref[...]
            cos_p, sin_p = cosp_ref[...], sinp_ref[...]
            kvc = kvc_ref[...]
            kvp = kvp_ref[...]
            kw = jnp.concatenate([rope(kvp[:, :KV_WIDTH], cos_p, sin_p), rope(kvc[:, :KV_WIDTH], cos_c, sin_c)], axis=0)
            vw = jnp.concatenate([kvp[:, KV_WIDTH:], kvc[:, KV_WIDTH:]], axis=0)
            kd, vd = dup(kw), dup(vw)
            bias = _attn_bias(i)
            zs = (z0, z1, z2, z3)
            units = [(kvh, hf) for kvh in range(2) for hf in range(2)]
            half = GROUP_ROWS // 2
            pairs = [range(4 * kvh + 2 * hf, 4 * kvh + 2 * hf + 2) for kvh, hf in units]
            un = range(len(units))
            qs = [_stack_heads([rope(q_ref[:, 128 * pr:128 * (pr + 1)], cos_c, sin_c) for pr in pairs[j]], left)
                  for j in un]
            sink = [sinks_ref[kvh * GROUP_ROWS + hf * half:kvh * GROUP_ROWS + (hf + 1) * half, :] for kvh, hf in units]
            ku = [kd[kvh] for kvh, _ in units]
            vu = [vd[kvh] for kvh, _ in units]
            def first(j):
                p, ps, o = (r[0] for r in _attn_probs([qs[j]], [ku[j]], [vu[j]], [sink[j]], bias))
                parts = []
                for pp, pr in enumerate(pairs[j]):
                    cols = slice(128 * pr, 128 * (pr + 1))
                    sg, sg_grad = _silu_and_grad(zs[pr // 2][:, 128 * (pr % 2):128 * (pr % 2 + 1)])
                    du128 = du_ref[:, cols]
                    dz_ref[:, cols] = (du128 * _unstack_heads(o, left, pp) * sg_grad).astype(BF16)
                    parts.append(du128 * sg)
                dos = _stack_heads(parts, left)
                dp = _dot(dos, vu[j], NT)
                delta = _row_sums(p * dp)
                ds = (p * (dp - jnp.concatenate([delta, delta], axis=1)) * ATT_SCALE).astype(BF16)
                kvh, hf = units[j]
                sk_acc[kvh, hf * half:(hf + 1) * half, :] += -ps * delta
                return ds, p.astype(BF16), dos.astype(BF16)

            def second(j, ds, p, dos):
                dqs = _dot(ds, ku[j], NN)
                for pp, pr in enumerate(pairs[j]):
                    dq_ref[:, 128 * pr:128 * (pr + 1)] = rope_bwd(_unstack_heads(dqs, left, pp),
                                                                  cos_c, sin_c).astype(BF16)
                return _dot(ds, qs[j], TN), _dot(p, dos, TN)

            got, dku, dvu = {}, [None] * len(units), [None] * len(units)
            for j in range(len(units) + 1):
                if j < len(units):
                    got[j] = first(j)
                if j >= 1:
                    dku[j - 1], dvu[j - 1] = second(j - 1, *got.pop(j - 1))
            dkd = [dku[0] + dku[1], dku[2] + dku[3]]
            dvd = [dvu[0] + dvu[1], dvu[2] + dvu[3]]
            fold = lambda pr: jnp.where(left, pr[0] + pltpu.roll(pr[0], ATT_DIM, 1), pr[1] + pltpu.roll(pr[1], ATT_DIM, 1))
            dkw = fold(dkd)
            dvw = fold(dvd)
            prev = jnp.concatenate([rope_bwd(dkw[:ATT_BLOCK], cos_p, sin_p), dvw[:ATT_BLOCK]], axis=1)
            cur = jnp.concatenate([rope_bwd(dkw[ATT_BLOCK:], cos_c, sin_c), dvw[ATT_BLOCK:]], axis=1)
            dkv_ref[...] = (carry[...] + prev).astype(BF16)
            carry[...] = cur

        @pl.when(i == nb)
        def _():
            dkv_ref[...] = carry[...].astype(BF16)

        @pl.when((b_id == B - 1) & (i == nb))
        def _():
            lane = lax.broadcasted_iota(jnp.int32, (1, 128), 1)
            tot = jnp.zeros((1, 128), F32)
            for hd in range(ATT_HEADS):
                rows = sk_acc[hd // GROUP, (hd % GROUP) * ATT_BLOCK:(hd % GROUP + 1) * ATT_BLOCK, :]
                tot = tot + jnp.where(lane == hd, jnp.sum(rows, axis=0, keepdims=True), 0.0)
            dsk_ref[...] = tot

        _hosted_finish(phase, p_in, p_out, p_sems, (b_id == B - 1) & (i == nb))

    cl = lambda i: jnp.minimum(i, nb - 1)
    pv = lambda i: jnp.maximum(jnp.minimum(i, nb - 1) - 1, 0)
    rowblk = lambda w, cb: pl.BlockSpec((None, ATT_BLOCK, w), lambda b, i: (b, cl(i), cb))
    tab = pl.BlockSpec((ATT_BLOCK, 128), lambda b, i: (cl(i), 0))
    tabp = pl.BlockSpec((ATT_BLOCK, 128), lambda b, i: (pv(i), 0))
    p_ispecs, p_ospecs, p_oshapes, p_alias, p_scratch, p_args = _host_phase(phase, 13, 4)
    res = pl.pallas_call(
        body, name=name,
        grid=(B, nb + 1),
        in_specs=[rowblk(ATT_WIDTH, 0), rowblk(256, 4),
                  pl.BlockSpec((None, ATT_BLOCK, 256), lambda b, i: (b, pv(i), 4)),
                  rowblk(256, _Z0), rowblk(256, _Z0 + 1), rowblk(256, _Z0 + 2), rowblk(256, _Z0 + 3),
                  rowblk(ATT_WIDTH, 1),
                  tab, tab, tabp, tabp,
                  pl.BlockSpec((ATT_HEADS * ATT_BLOCK, 128), lambda b, i: (0, 0))] + p_ispecs,
        out_specs=[rowblk(ATT_WIDTH, 0),
                   pl.BlockSpec((None, ATT_BLOCK, 256), lambda b, i: (b, jnp.maximum(i - 1, 0), 0)),
                   rowblk(ATT_WIDTH, 0),
                   pl.BlockSpec((1, 128), lambda b, i: (0, 0))] + p_ospecs,
        out_shape=[jax.ShapeDtypeStruct((B, S, ATT_WIDTH), BF16), jax.ShapeDtypeStruct((B, S, 256), BF16),
                   jax.ShapeDtypeStruct((B, S, ATT_WIDTH), BF16), jax.ShapeDtypeStruct((1, 128), F32)] + p_oshapes,
        input_output_aliases=p_alias,
        scratch_shapes=[pltpu.VMEM((ATT_BLOCK, 256), F32), pltpu.VMEM((2, GROUP_ROWS, 128), F32)] + p_scratch,
        compiler_params=_params(("arbitrary", "arbitrary")),
    )(proj_a, proj_a, proj_a, proj_a, proj_a, proj_a, proj_a, du, cos, sin, cos, sin, sinks_l, *p_args)
    return tuple(res[:4]) + (list(res[4:]),)


def _outproj_fwd(u2, w_out, x2, g_post, target2, name):
    T, D = x2.shape
    tm = _pick(T, (512, 256, 128))
    last = target2 is not None

    def body(u_ref, w_ref, x_ref, g_ref, *rest):
        y = lax.dot_general(u_ref[...], w_ref[...], (NN, ((), ())), preferred_element_type=F32)
        r = lax.rsqrt(jnp.mean(y * y, axis=-1, keepdims=True) + NORM_EPS)
        xn = x_ref[...] + (y * r) * g_ref[...]
        if last:
            t_ref, y_ref, dx_ref, loss_ref = rest
            err = xn - t_ref[...]
            dx_ref[...] = err * (1.0 / D)
            sq = err * err
            acc = sq[:, 0:128]
            for kk in range(1, D // 128):
                acc = acc + sq[:, 128 * kk:128 * (kk + 1)]
            part = jnp.sum(acc.reshape(tm // 8, 8, 128), axis=0) * (0.5 / D)

            @pl.when(pl.program_id(0) == 0)
            def _():
                loss_ref[...] = jnp.zeros_like(loss_ref)

            loss_ref[...] += part
        else:
            y_ref, xn_ref = rest
            xn_ref[...] = xn
        y_ref[...] = y

    row = pl.BlockSpec((tm, D), lambda i: (i, 0))
    in_specs = [pl.BlockSpec((tm, MIX_WIDTH), lambda i: (i, 0)),
                pl.BlockSpec((MIX_WIDTH, D), lambda i: (0, 0)), row,
                pl.BlockSpec((1, D), lambda i: (0, 0))]
    args = [u2, w_out, x2, g_post]
    out_specs = [row, row]
    out_shape = [jax.ShapeDtypeStruct((T, D), F32), jax.ShapeDtypeStruct((T, D), F32)]
    if last:
        in_specs.append(row)
        args.append(target2)
        out_specs.append(pl.BlockSpec((8, 128), lambda i: (0, 0)))
        out_shape.append(jax.ShapeDtypeStruct((8, 128), F32))
    return pl.pallas_call(
        body, name=name, grid=(T // tm,), in_specs=in_specs, out_specs=out_specs, out_shape=out_shape,
        compiler_params=_params(("arbitrary",)),
    )(*args)


def _outproj_bwd(dxn2, y2, g_post, w_out, name):
    T, D = y2.shape
    N = w_out.shape[0]
    tm = _pick(T, (512, 256, 128))
    nt = T // tm

    def body(dx_ref, y_ref, g_ref, w_ref, dy_ref, dg_ref, du_ref, acc):
        i = pl.program_id(0)

        @pl.when(i == 0)
        def _():
            acc[...] = jnp.zeros_like(acc)

        y = y_ref[...]
        dxn = dx_ref[...]
        r = lax.rsqrt(jnp.mean(y * y, axis=-1, keepdims=True) + NORM_EPS)
        n = y * r
        dn = dxn * g_ref[...]
        dy = (r * (dn - n * jnp.mean(dn * n, axis=-1, keepdims=True))).astype(BF16)
        dy_ref[...] = dy
        du_ref[...] = lax.dot_general(dy, w_ref[...], (NT, ((), ())), preferred_element_type=F32)
        acc[...] += jnp.sum((dxn * n).reshape(tm // 8, 8, D), axis=0)

        @pl.when(i == nt - 1)
        def _():
            dg_ref[...] = jnp.sum(acc[...], axis=0, keepdims=True)

    row = pl.BlockSpec((tm, D), lambda i: (i, 0))
    vec = pl.BlockSpec((1, D), lambda i: (0, 0))
    return pl.pallas_call(
        body, name=name, grid=(nt,),
        in_specs=[row, row, vec, pl.BlockSpec((N, D), lambda i: (0, 0), pipeline_mode=pl.Buffered(1))],
        out_specs=[row, vec, pl.BlockSpec((tm, N), lambda i: (i, 0))],
        out_shape=[jax.ShapeDtypeStruct((T, D), BF16), jax.ShapeDtypeStruct((1, D), F32),
                   jax.ShapeDtypeStruct((T, N), F32)],
        scratch_shapes=[pltpu.VMEM((8, D), F32)],
        compiler_params=_params(("arbitrary",)),
    )(dxn2, y2, g_post, w_out)


def _inproj_bwd(pieces, w_t, x2, dxn2, g_pre, name, phase=None):
    T, D = x2.shape
    widths = [p.shape[1] for p in pieces]
    offs = [sum(widths[:i]) for i in range(len(pieces))]
    n_p = len(pieces)
    tm = _pick(T, (256, 128))
    nt = T // tm

    def body(*refs):
        ins, (dx_ref, dg_ref), (acc,), p_in, p_out, p_sems = _split_refs(refs, n_p + 4, 2, 1, phase)
        w_ref, x_ref, dxn_ref, g_ref = ins[n_p:]
        i = pl.program_id(0)
        _hosted_start(phase, p_in, p_out, p_sems, i == 0)

        @pl.when(i == 0)
        def _():
            acc[...] = jnp.zeros_like(acc)

        dh = jnp.zeros((tm, D), F32)
        for p in range(n_p):
            dh = dh + lax.dot_general(ins[p][...], w_ref[offs[p]:offs[p] + widths[p], :], (NN, ((), ())),
                                      preferred_element_type=F32)
        x = x_ref[...]
        r = lax.rsqrt(jnp.mean(x * x, axis=-1, keepdims=True) + NORM_EPS)
        n = x * r
        dn = dh * g_ref[...]
        dx_ref[...] = dxn_ref[...] + r * (dn - n * jnp.mean(dn * n, axis=-1, keepdims=True))
        acc[...] += jnp.sum((dh * n).reshape(tm // 8, 8, D), axis=0)

        @pl.when(i == nt - 1)
        def _():
            dg_ref[...] = jnp.sum(acc[...], axis=0, keepdims=True)

        _hosted_finish(phase, p_in, p_out, p_sems, i == nt - 1)

    row = pl.BlockSpec((tm, D), lambda i: (i, 0))
    vec = pl.BlockSpec((1, D), lambda i: (0, 0))
    p_ispecs, p_ospecs, p_oshapes, p_alias, p_scratch, p_args = _host_phase(phase, n_p + 4, 2)
    res = pl.pallas_call(
        body, name=name, grid=(nt,),
        in_specs=[pl.BlockSpec((tm, w), lambda i: (i, 0)) for w in widths]
        + [pl.BlockSpec((sum(widths), D), lambda i: (0, 0), pipeline_mode=pl.Buffered(1)), row, row, vec] + p_ispecs,
        out_specs=[row, vec] + p_ospecs,
        out_shape=[jax.ShapeDtypeStruct((T, D), F32), jax.ShapeDtypeStruct((1, D), F32)] + p_oshapes,
        input_output_aliases=p_alias,
        scratch_shapes=[pltpu.VMEM((8, D), F32)] + p_scratch,
        compiler_params=_params(("arbitrary",)),
    )(*pieces, w_t, x2, dxn2, g_pre, *p_args)
    return res[0], res[1], list(res[2:])


def _step(x, target, g_pre, g_post, lb_param, g_head, sinks, shards=None, full=None):
    B, S, D = x.shape
    T = B * S
    dist = shards is not None
    first, last = 0, DEPTH - 1
    if dist:
        a_loc, b_loc = shards
        ra, rb = a_loc.shape[1], b_loc.shape[1]
        side = _own_side_blocks()
        placed = lambda loc, nm: _place_own(loc, side, "place_" + nm)
        w_in0 = _run_phase(_gather_ici_phase([a_loc[0]], [placed(a_loc[0], "in0")]), "gather_in0_ici")
        w_in0 = _run_phase(_gather_d2d_phase(w_in0, [ra]), "gather_in0_d2d")[0]
        w_in, w_out = [w_in0, None], [None, None]
    else:
        w_in, w_out = list(full[0]), list(full[1])
    cos, sin = _rope_tables(S)
    saved = []
    xs = x
    loss_part = None
    dxn = None
    for l in range(DEPTH):
        x2 = xs.reshape(T, D)
        proj_h, proj_a, h = _inproj(x2, g_pre[l:l + 1], w_in[l], f"inproj{l}")
        proj_h = proj_h.reshape(B, S, N_H)
        proj_a = proj_a.reshape(B, S, N_A)
        phase = None
        if dist and l == first:
            phase = _gather_ici_phase([a_loc[1], b_loc[0]], [placed(a_loc[1], "in1"), placed(b_loc[0], "out0")])
        if dist and l == last:
            phase = _gather_d2d_phase([w_out1_part], [rb])
        o_h, u, states, got = _hgrn_fwd(proj_h, MIX_WIDTH, lb_param, g_head[l:l + 1], l, f"hgrn_fwd{l}", phase)
        phase = None
        if dist and l == first:
            phase = _merge_phases(_gather_d2d_phase(got, [ra, rb]),
                                  _gather_ici_phase([b_loc[1]], [placed(b_loc[1], "out1")]))
        if dist and l == last:
            w_out[1] = got[0]
        u, got = _attn_fwd(proj_a, u, _sink_rows(sinks[l]), cos, sin, f"attn_fwd{l}", phase)
        if dist and l == first:
            w_in[1], w_out[0], w_out1_part = got
        u2 = u.reshape(T, MIX_WIDTH)
        if l < last:
            y, xn = _outproj_fwd(u2, w_out[l], x2, g_post[l:l + 1], None, f"outproj{l}")
            xn = xn.reshape(B, S, D)
        else:
            y, dxn, loss_part = _outproj_fwd(u2, w_out[l], x2, g_post[l:l + 1], target.reshape(T, D), f"outproj{l}")
            xn = None
        saved.append((x2, h, proj_h, proj_a, o_h, u2, states, y))
        xs = xn

    dw_in, dw_out = [None] * DEPTH, [None] * DEPTH
    dg_pre, dg_post, dlb, dg_head, dsinks = [], [], [], [], []
    for l in reversed(range(DEPTH)):
        x2, h, proj_h, proj_a, o_h, u2, states, y = saved[l]
        dy, dgp, du = _outproj_bwd(dxn, y, g_post[l:l + 1], w_out[l], f"outproj_bwd{l}")
        du = du.reshape(B, S, MIX_WIDTH)
        dw_out[l] = _mm_tn([u2], dy, f"wgrad_out{l}")
        phase = None
        if dist:
            phase = _reduce_d2d_phase([dw_out[l]], [rb])
            if l == first:
                phase = _merge_phases(_reduce_ici_phase([part_in1]), phase)
        dqh, dfh, dih, dzh, dlb_l, dgh, got = _hgrn_bwd(
            proj_h, o_h, du, states, lb_param, g_head[l:l + 1], l, f"hgrn_bwd{l}", phase)
        if dist:
            if l == first:
                sum_in = _chip_sum(part_in1, got[0], "chip_sum_in1", 1)
            part_out = _pair_sum(dw_out[l], got[-1], side, f"pair_sum_out{l}")
        dqa, dkv, dza, dsk, got = _attn_bwd(proj_a, du, _sink_rows(sinks[l]), cos, sin, f"attn_bwd{l}",
                                            _reduce_ici_phase([part_out]) if dist else None)
        if dist:
            sum_out = _chip_sum(part_out, got[0], f"chip_sum_out{l}", l, None if l == last else sum_out)
        dproj = [p.reshape(T, p.shape[-1]) for p in (dqh, dfh, dih, dzh, dqa, dkv, dza)]
        dw_in[l] = _mm_tn(dproj, h, f"wgrad_in{l}")
        phase = None
        if dist and l == last:
            phase = _reduce_d2d_phase([dw_in[l]], [ra])
        if dist and l == first:
            got = _run_phase(_reduce_d2d_phase([dw_in[l]], [ra]), "reduce_in0_d2d")
            part_in0 = _pair_sum(dw_in[l], got[0], side, "pair_sum_in0")
            phase = _reduce_ici_phase([part_in0])
        dxn, dgpre, got = _inproj_bwd(dproj, w_in[l], x2, dxn, g_pre[l:l + 1], f"inproj_bwd{l}", phase)
        if dist and l == last:
            part_in1 = _pair_sum(dw_in[l], got[0], side, "pair_sum_in1")
        if dist and l == first:
            sum_in = _chip_sum(part_in0, got[0], "chip_sum_in0", 0, sum_in)
        dg_pre.append(dgpre)
        dg_post.append(dgp)
        dlb.append(dlb_l)
        dg_head.append(dgh)
        dsinks.append(dsk)
    rev = lambda lst: jnp.concatenate(lst[::-1], axis=0)
    if not dist:
        sum_in, sum_out = jnp.stack(dw_in), jnp.stack(dw_out)
    return (loss_part, dxn.reshape(B, S, D), sum_in, sum_out,
            rev(dg_pre), rev(dg_post), rev(dlb), rev(dg_head), rev(dsinks))


def _me_and_peers():
    x, y, c = lax.axis_index("x"), lax.axis_index("y"), lax.axis_index("c")
    me = 4 * x + 2 * y + c
    peers = []
    for k in range(1, N_DEV):
        px = 1 - x if k & 4 else x
        py = 1 - y if k & 2 else y
        pc = 1 - c if k & 1 else c
        peers.append(((px, py, pc), 4 * px + 2 * py + pc))
    return me, peers


class _Phase:
    def __init__(self, arrays, out_shapes, aliases, n_send, build):
        self.arrays, self.out_shapes, self.aliases = list(arrays), list(out_shapes), dict(aliases)
        self.n_send, self.build = n_send, build

    def scratch(self):
        return [pltpu.SemaphoreType.DMA((self.n_send,)), pltpu.SemaphoreType.DMA((self.n_send,))]

    def _copies(self, in_refs, out_refs, sems, arrivals):
        send_sems, recv_sems = sems
        sends, recvs = self.build(in_refs, out_refs)
        assert len(sends) == self.n_send == len(recvs)
        out = [pltpu.make_async_remote_copy(src_ref=s, dst_ref=d, send_sem=send_sems.at[i], recv_sem=recv_sems.at[i],
                                            device_id=dev, device_id_type=MESH) for i, (s, d, dev) in enumerate(sends)]
        inc = [pltpu.make_async_remote_copy(src_ref=s, dst_ref=r, send_sem=send_sems.at[i], recv_sem=recv_sems.at[i],
                                            device_id=dev, device_id_type=MESH)
               for i, ((s, _, dev), r) in enumerate(zip(sends, recvs))] if arrivals else []
        return out, inc

    def start(self, in_refs, out_refs, sems):
        out, _ = self._copies(in_refs, out_refs, sems, False)
        for cp in out:
            cp.start()

    def finish(self, in_refs, out_refs, sems):
        out, inc = self._copies(in_refs, out_refs, sems, True)
        for cp in inc:
            cp.wait_recv()
        for cp in out:
            cp.wait_send()


_ANY = pl.BlockSpec(memory_space=pl.ANY)


def _host_phase(phase, n_in, n_out):
    if phase is None:
        return [], [], [], {}, [], []
    aliases = {n_in + i: n_out + o for i, o in phase.aliases.items()}
    return ([_ANY] * len(phase.arrays), [_ANY] * len(phase.out_shapes), phase.out_shapes, aliases, phase.scratch(),
            phase.arrays)


def _split_refs(refs, n_in, n_out, n_scr, phase):
    pi = len(phase.arrays) if phase else 0
    po = len(phase.out_shapes) if phase else 0
    a = n_in + pi
    b = a + n_out + po
    return (refs[:n_in], refs[a:a + n_out], refs[b:b + n_scr], refs[n_in:a], refs[a + n_out:b], refs[b + n_scr:])


def _hosted_start(phase, p_in, p_out, p_sems, first):
    if phase is not None:
        @pl.when(first)
        def _():
            phase.start(p_in, p_out, p_sems)


def _hosted_finish(phase, p_in, p_out, p_sems, last):
    if phase is not None:
        @pl.when(last)
        def _():
            phase.finish(p_in, p_out, p_sems)


def _run_phase(phase, name):
    n_in, n_out = len(phase.arrays), len(phase.out_shapes)

    def body(*refs):
        phase.start(refs[:n_in], refs[n_in:n_in + n_out], refs[n_in + n_out:])
        phase.finish(refs[:n_in], refs[n_in:n_in + n_out], refs[n_in + n_out:])

    return pl.pallas_call(
        body, name=name, in_specs=[_ANY] * n_in, out_specs=[_ANY] * n_out,
        out_shape=phase.out_shapes, input_output_aliases=phase.aliases, scratch_shapes=phase.scratch(),
        compiler_params=pltpu.CompilerParams(has_side_effects=True),
    )(*phase.arrays)


def _merge_phases(a, b):
    n_in, n_out = len(a.arrays), len(a.out_shapes)
    aliases = dict(a.aliases)
    aliases.update({n_in + i: n_out + o for i, o in b.aliases.items()})

    def build(ins, outs):
        sa, ra = a.build(ins[:n_in], outs[:n_out])
        sb, rb = b.build(ins[n_in:], outs[n_out:])
        return sa + sb, ra + rb

    return _Phase(a.arrays + b.arrays, a.out_shapes + b.out_shapes, aliases, a.n_send + b.n_send, build)


def _mesh_place():
    x, y, c = lax.axis_index("x"), lax.axis_index("y"), lax.axis_index("c")
    chips = [(x, y), (1 - x, y), (x, 1 - y), (1 - x, 1 - y)]
    num = lambda chip, core: 4 * chip[0] + 2 * chip[1] + core
    return c, chips, num


def _own_side_blocks():
    c, chips, num = _mesh_place()
    return jnp.stack([num(ch, c) for ch in chips]).astype(jnp.int32)


def _rows(ref, r, dev):
    return ref.at[pl.ds(pl.multiple_of(dev * r, 16), r), :]


def _place_own(loc, blocks, name):
    r, D = loc.shape
    tr = _pick(r, (400, 256, 200, 128, 64, 16))

    def body(idx_ref, l_ref, o_ref):
        del idx_ref
        o_ref[...] = l_ref[...]

    return pl.pallas_call(
        body, name=name,
        grid_spec=pltpu.PrefetchScalarGridSpec(
            num_scalar_prefetch=1, grid=(r // tr,),
            in_specs=[pl.BlockSpec((tr, D), lambda i, idx: (i, 0))],
            out_specs=pl.BlockSpec((tr, D), lambda i, idx: (idx[0] * (r // tr) + i, 0))),
        out_shape=jax.ShapeDtypeStruct((N_DEV * r, D), loc.dtype),
        compiler_params=_params(("arbitrary",)),
    )(blocks, loc)


def _gather_ici_phase(locs, fulls):
    rs = [a.shape[0] for a in locs]
    n = len(locs)

    def build(ins, outs):
        c, chips, num = _mesh_place()
        me = num(chips[0], c)
        targets = [((*chips[0], 1 - c), num(chips[0], 1 - c))] + [((*ch, c), num(ch, c)) for ch in chips[1:]]
        sends, recvs = [], []
        for dev, dnum in targets:
            for i, r in enumerate(rs):
                sends.append((ins[i], _rows(outs[i], r, me), dev))
                recvs.append(_rows(outs[i], r, dnum))
        return sends, recvs

    shapes = [jax.ShapeDtypeStruct(a.shape, a.dtype) for a in fulls]
    return _Phase(list(locs) + list(fulls), shapes, {n + i: i for i in range(n)}, 4 * n, build)


def _gather_d2d_phase(fulls, rs):
    def build(ins, outs):
        c, chips, num = _mesh_place()
        sib = (*chips[0], 1 - c)
        sends, recvs = [], []
        for ch in chips[1:]:
            for i, r in enumerate(rs):
                blk = _rows(outs[i], r, num(ch, c))
                sends.append((blk, blk, sib))
                recvs.append(_rows(outs[i], r, num(ch, 1 - c)))
        return sends, recvs

    shapes = [jax.ShapeDtypeStruct(a.shape, a.dtype) for a in fulls]
    return _Phase(fulls, shapes, {i: i for i in range(len(fulls))}, 3 * len(fulls), build)


def _reduce_d2d_phase(grads, rs):
    def build(ins, outs):
        c, chips, num = _mesh_place()
        sib = (*chips[0], 1 - c)
        sends, recvs = [], []
        for j, ch in enumerate(chips):
            for i, r in enumerate(rs):
                sends.append((_rows(ins[i], r, num(ch, 1 - c)), outs[i].at[j], sib))
                recvs.append(outs[i].at[j])
        return sends, recvs

    shapes = [jax.ShapeDtypeStruct((4, r, g.shape[1]), g.dtype) for g, r in zip(grads, rs)]
    return _Phase(grads, shapes, {}, 4 * len(grads), build)


def _reduce_ici_phase(parts):
    def build(ins, outs):
        c, chips, _ = _mesh_place()
        sends, recvs = [], []
        for t in range(1, 4):
            for i in range(len(parts)):
                sends.append((ins[i].at[t], outs[i].at[t - 1], (*chips[t], c)))
                recvs.append(outs[i].at[t - 1])
        return sends, recvs

    shapes = [jax.ShapeDtypeStruct((3,) + p.shape[1:], p.dtype) for p in parts]
    return _Phase(parts, shapes, {}, 3 * len(parts), build)


def _pair_sum(g, got, blocks, name):
    n, r, D = got.shape
    tr = _pick(r, (400, 256, 200, 128, 64, 16))

    def body(idx_ref, g_ref, r_ref, o_ref):
        del idx_ref
        o_ref[...] = (g_ref[...].astype(F32) + r_ref[...].astype(F32)).astype(o_ref.dtype)

    blk = pl.BlockSpec((None, tr, D), lambda j, i, idx: (j, i, 0))
    return pl.pallas_call(
        body, name=name,
        grid_spec=pltpu.PrefetchScalarGridSpec(
            num_scalar_prefetch=1, grid=(n, r // tr),
            in_specs=[pl.BlockSpec((tr, D), lambda j, i, idx: (idx[j] * (r // tr) + i, 0)), blk],
            out_specs=blk),
        out_shape=jax.ShapeDtypeStruct(got.shape, got.dtype),
        compiler_params=_params(("arbitrary", "arbitrary")),
    )(blocks, g, got)


def _chip_sum(p, r, name, layer, into=None):
    _, R, D = p.shape
    tr = _pick(R, (400, 256, 200, 128, 64, 16))

    def body(p_ref, r_ref, *rest):
        acc = p_ref[...].astype(F32)
        for t in range(3):
            acc = acc + r_ref[t].astype(F32)
        rest[-1][...] = acc

    args = [p, r] + ([] if into is None else [into])
    return pl.pallas_call(
        body, name=name, grid=(R // tr,),
        in_specs=[pl.BlockSpec((None, tr, D), lambda i: (0, i, 0)), pl.BlockSpec((3, tr, D), lambda i: (0, i, 0))]
        + ([] if into is None else [_ANY]),
        out_specs=pl.BlockSpec((None, tr, D), lambda i: (layer, i, 0)),
        out_shape=jax.ShapeDtypeStruct((DEPTH, R, D), F32),
        input_output_aliases={} if into is None else {2: 0},
        compiler_params=_params(("parallel",)))(*args)


def _allreduce_small(vec):
    R, C = vec.shape

    def body(v_ref, o_ref, buf, send_sems, recv_sems):
        me, peers = _me_and_peers()
        buf[me] = v_ref[...]
        sends = []
        for k, (pid, _) in enumerate(peers):
            cp = pltpu.make_async_remote_copy(src_ref=v_ref, dst_ref=buf.at[me], send_sem=send_sems.at[k],
                                              recv_sem=recv_sems.at[k], device_id=pid, device_id_type=MESH)
            cp.start()
            sends.append(cp)
        for k, (pid, pnum) in enumerate(peers):
            pltpu.make_async_remote_copy(src_ref=v_ref, dst_ref=buf.at[pnum], send_sem=send_sems.at[k],
                                         recv_sem=recv_sems.at[k], device_id=pid, device_id_type=MESH).wait_recv()
        for cp in sends:
            cp.wait_send()
        acc = buf[0]
        for d in range(1, N_DEV):
            acc = acc + buf[d]
        o_ref[...] = acc

    vm = pl.BlockSpec(memory_space=pltpu.VMEM)
    return pl.pallas_call(
        body, name="allreduce_small",
        in_specs=[vm], out_specs=vm,
        out_shape=jax.ShapeDtypeStruct((R, C), F32),
        scratch_shapes=[pltpu.VMEM((N_DEV, R, C), F32), pltpu.SemaphoreType.DMA((N_DEV - 1,)),
                        pltpu.SemaphoreType.DMA((N_DEV - 1,))],
        compiler_params=pltpu.CompilerParams(has_side_effects=True),
    )(vec)


def _adamw(w, g, m, v, name):
    R, C = w.shape
    tr = _pick(R, (512, 400, 256, 128, 64, 32, 16, 8)) if R >= 8 else R
    c1 = 1.0 - ADAM_B1 ** ADAM_STEP
    c2 = 1.0 - ADAM_B2 ** ADAM_STEP

    def body(w_ref, g_ref, m_ref, v_ref, d_ref, mo_ref, vo_ref):
        gg = g_ref[...]
        mn = ADAM_B1 * m_ref[...] + (1.0 - ADAM_B1) * gg
        vn = ADAM_B2 * v_ref[...] + (1.0 - ADAM_B2) * (gg * gg)
        d_ref[...] = -ADAM_LR * ((mn / c1) / (jnp.sqrt(vn / c2) + ADAM_EPS) + ADAM_WD * w_ref[...])
        mo_ref[...] = mn
        vo_ref[...] = vn

    blk = pl.BlockSpec((tr, C), lambda i: (i, 0))
    sh = jax.ShapeDtypeStruct((R, C), F32)
    return pl.pallas_call(
        body, name=name, grid=(R // tr,), in_specs=[blk] * 4, out_specs=[blk] * 3, out_shape=[sh] * 3,
        compiler_params=_params(("parallel",)),
    )(w, g, m, v)


def _lb_param_grad(lb_param, dlb):
    L, C = lb_param.shape

    def body(p_ref, d_ref, o_ref):
        lbp = p_ref[...]
        d = d_ref[...]
        mx = jnp.max(lbp, axis=0, keepdims=True)
        e = jnp.exp(lbp - mx)
        p = e / jnp.sum(e, axis=0, keepdims=True)
        tot = jnp.sum(d, axis=0, keepdims=True)
        dps = []
        rest = tot
        for j in range(L):
            dps.append(rest - tot if j == 0 else rest)
            rest = rest - d[j:j + 1]
        dp = jnp.concatenate(dps, axis=0)
        o_ref[...] = p * (dp - jnp.sum(p * dp, axis=0, keepdims=True))

    vm = pl.BlockSpec(memory_space=pltpu.VMEM)
    return pl.pallas_call(body, name="lb_param_grad", in_specs=[vm, vm], out_specs=vm,
                          out_shape=jax.ShapeDtypeStruct((L, C), F32))(lb_param, dlb)


def _pack_small(loss_part, dg_pre, dg_post, dlb, dg_head, dsinks):
    pad8 = lambda a: jnp.pad(a.reshape(-1, 128), ((0, 8 - DEPTH), (0, 0)))
    rows = [dg_pre.reshape(-1, 128), dg_post.reshape(-1, 128), dlb.reshape(-1, 128), pad8(dg_head), pad8(dsinks),
            loss_part]
    return jnp.concatenate(rows, axis=0)


def _unpack_small(vec):
    n = DEPTH * D_MODEL // 128
    o = 0
    dg_pre = vec[o:o + n].reshape(DEPTH, D_MODEL); o += n
    dg_post = vec[o:o + n].reshape(DEPTH, D_MODEL); o += n
    dlb = vec[o:o + n].reshape(DEPTH, HG_WIDTH); o += n
    dg_head = vec[o:o + DEPTH]; o += 8
    dsinks = vec[o:o + DEPTH, :ATT_HEADS]; o += 8
    loss = jnp.sum(vec[o:o + 8])
    return loss, dg_pre, dg_post, dlb, dg_head, dsinks


def kernel(x, w_in, w_out, g_pre, g_post, lb_param, g_head, sinks, loss_target, m_w_in, m_w_out, m_g_pre, m_g_post, m_lb_param, m_g_head, m_sinks, v_w_in, v_w_out, v_g_pre, v_g_post, v_lb_param, v_g_head, v_sinks):
    tr = lambda a: jnp.swapaxes(a, 1, 2)
    w_in_t = tr(w_in)
    (loss_part, dx, gw_in_t, gw_out, dg_pre, dg_post, dlb, dg_head, dsinks) = _step(
        x, loss_target, g_pre, g_post, lb_param, g_head, sinks, shards=(w_in_t.astype(BF16), w_out.astype(BF16)))

    small = _allreduce_small(_pack_small(loss_part, dg_pre, dg_post, dlb, dg_head, dsinks))
    loss, gg_pre, gg_post, gdlb, gg_head, gsinks = _unpack_small(small)
    glb = _lb_param_grad(lb_param, gdlb)

    grads = [gw_in_t, gw_out, gg_pre, gg_post, glb, gg_head, gsinks]
    ws = [w_in_t, w_out, g_pre, g_post, lb_param, g_head, sinks]
    ms = [tr(m_w_in), m_w_out, m_g_pre, m_g_post, m_lb_param, m_g_head, m_sinks]
    vs = [tr(v_w_in), v_w_out, v_g_pre, v_g_post, v_lb_param, v_g_head, v_sinks]
    names = ["w_in", "w_out", "g_pre", "g_post", "lb_param", "g_head", "sinks"]
    deltas, new_m, new_v = [], [], []
    for w, g, m, v, nm in zip(ws, grads, ms, vs, names):
        sh = w.shape
        two = lambda a: a.reshape(-1, sh[-1])
        d, mn, vn = _adamw(two(w), two(g), two(m), two(v), "adamw_" + nm)
        deltas.append(d.reshape(sh))
        new_m.append(mn.reshape(sh))
        new_v.append(vn.reshape(sh))
    grads[0], deltas[0], new_m[0], new_v[0] = tr(grads[0]), tr(deltas[0]), tr(new_m[0]), tr(new_v[0])
    return (loss, dx, *grads, *deltas, *new_m, *new_v)
```

```python
import math

import numpy as np
import jax
import jax.numpy as jnp
from jax import lax
from jax.experimental import pallas as pl
from jax.experimental.pallas import tpu as pltpu

F32 = jnp.float32
BF16 = jnp.bfloat16

D_MODEL = 1024
DEPTH = 2
HG_HEADS = 8
HG_DIM = 128
HG_WIDTH = HG_HEADS * HG_DIM
CHUNK = 64
ATT_HEADS = 16
ATT_DIM = 64
ATT_WIDTH = ATT_HEADS * ATT_DIM
KV_WIDTH = 128
ATT_BLOCK = 128
ATT_SCALE = 1.0 / math.sqrt(ATT_DIM)
ROPE_THETA = 10000.0
NORM_EPS = 1e-6
NEG_INF = -1e30
LB_FLOOR = 1e-20
N_H = 4 * HG_WIDTH
N_A = 2 * ATT_WIDTH + 2 * KV_WIDTH
IN_WIDTH = N_H + N_A
MIX_WIDTH = HG_WIDTH + ATT_WIDTH

ADAM_LR = 0.001
ADAM_B1 = 0.9
ADAM_B2 = 0.999
ADAM_EPS = 1e-08
ADAM_WD = 0.01
ADAM_STEP = 10

N_DEV = 8
MESH = pl.DeviceIdType.MESH
VMEM_LIMIT = 56 * 1024 * 1024

NN = ((1,), (0,))
NT = ((1,), (1,))
TN = ((0,), (0,))


def _dot(a, b, dims):
    return lax.dot_general(a.astype(BF16), b.astype(BF16), (dims, ((), ())), preferred_element_type=F32)


def _params(sem=None, **kw):
    return pltpu.CompilerParams(dimension_semantics=sem, vmem_limit_bytes=VMEM_LIMIT, **kw)


def _sigmoids(x):
    e = jnp.exp(-jnp.abs(x))
    r = 1.0 / (1.0 + e)
    er = e * r
    pos = x >= 0.0
    return jnp.where(pos, r, er), jnp.where(pos, er, r)


def _silu(x):
    return x * _sigmoids(x)[0]


def _silu_and_grad(x):
    s, ns = _sigmoids(x)
    return x * s, s * (1.0 + x * ns)


def _pick(n, prefs):
    for p in prefs:
        if n % p == 0:
            return p
    return n


def _inproj(x2, g, w, name):
    T, D = x2.shape
    tm = _pick(T, (512, 256, 128))
    nchunk = 1024

    def body(x_ref, g_ref, w_ref, oh_ref, oa_ref, h_ref):
        x = x_ref[...]
        r = lax.rsqrt(jnp.mean(x * x, axis=-1, keepdims=True) + NORM_EPS)
        h = ((x * r) * g_ref[...]).astype(BF16)
        h_ref[...] = h
        for j in range(0, N_H, nchunk):
            oh_ref[:, j:j + nchunk] = lax.dot_general(h, w_ref[j:j + nchunk, :], (NT, ((), ())),
                                                      preferred_element_type=F32)
        for j in range(0, N_A, N_A // 2):
            oa_ref[:, j:j + N_A // 2] = lax.dot_general(h, w_ref[N_H + j:N_H + j + N_A // 2, :], (NT, ((), ())),
                                                        preferred_element_type=F32)

    row = lambda w_: pl.BlockSpec((tm, w_), lambda i: (i, 0))
    return pl.pallas_call(
        body, name=name,
        grid=(T // tm,),
        in_specs=[row(D), pl.BlockSpec((1, D), lambda i: (0, 0)),
                  pl.BlockSpec((IN_WIDTH, D), lambda i: (0, 0), pipeline_mode=pl.Buffered(1))],
        out_specs=[row(N_H), row(N_A), row(D)],
        out_shape=[jax.ShapeDtypeStruct((T, N_H), F32), jax.ShapeDtypeStruct((T, N_A), F32),
                   jax.ShapeDtypeStruct((T, D), BF16)],
        compiler_params=_params(("parallel",)),
    )(x2, g, w)


def _mm_tn(pieces, b, name, out_dtype=BF16):
    T, m = b.shape
    tn = 256
    counts = [p.shape[1] // tn for p in pieces]
    starts = [sum(counts[:i]) for i in range(len(pieces))]
    n_p = len(pieces)

    def body(*refs):
        b_ref, o_ref = refs[n_p], refs[n_p + 1]
        i = pl.program_id(0)
        for p in range(n_p):
            @pl.when((i >= starts[p]) & (i < starts[p] + counts[p]))
            def _(p=p):
                o_ref[...] = lax.dot_general(refs[p][...], b_ref[...], (TN, ((), ())),
                                             preferred_element_type=F32).astype(out_dtype)

    piece_spec = lambda s, c: pl.BlockSpec((T, tn), lambda i: (0, jnp.clip(i - s, 0, c - 1)))
    return pl.pallas_call(
        body, name=name,
        grid=(sum(counts),),
        in_specs=[piece_spec(s, c) for s, c in zip(starts, counts)]
        + [pl.BlockSpec((T, m), lambda i: (0, 0), pipeline_mode=pl.Buffered(1))],
        out_specs=pl.BlockSpec((tn, m), lambda i: (i, 0)),
        out_shape=jax.ShapeDtypeStruct((sum(counts) * tn, m), out_dtype),
        compiler_params=_params(("arbitrary",)),
    )(*pieces, b)


_LEVELS = (0, 1, 2, 4, 8, 16, 32)
_CUM_L = (2, 4, 8, 16, 32, 64)
_ALL_KINDS = tuple(("c", L) for L in _CUM_L) + tuple(("r", L) for L in _CUM_L)
_MXU_KINDS = (("c", 2), ("c", 4), ("c", CHUNK), ("r", 2), ("r", 4))
N_CUM = len(_ALL_KINDS) * CHUNK
N_CUM_F = len(_MXU_KINDS) * CHUNK


def _cum_matrices():
    t = np.arange(CHUNK)[:, None]
    r = np.arange(CHUNK)[None, :]

    def mat(kind):
        c, L = kind
        return ((r // L == t // L) & ((r <= t) if c == "c" else (r > t))).astype(np.float32)

    fwd = np.concatenate([mat(kd) for kd in _MXU_KINDS], axis=0)
    full = np.concatenate([mat(kd) for kd in _ALL_KINDS], axis=0)
    return jnp.asarray(fwd, BF16), jnp.asarray(full.T.copy(), BF16)


def _level_masks():
    t = np.arange(CHUNK)[:, None]
    s = np.arange(CHUNK)[None, :]
    ms = []
    for L in _LEVELS:
        if L == 0:
            ms.append(t == s)
        else:
            ms.append((t // (2 * L) == s // (2 * L)) & ((t // L) % 2 == 1) & ((s // L) % 2 == 0))
    return jnp.asarray(np.stack(ms).astype(np.float32))


def _split3(x):
    hi = x.astype(BF16)
    r1 = x - hi.astype(F32)
    mid = r1.astype(BF16)
    lo = (r1 - mid.astype(F32)).astype(BF16)
    return hi, mid, lo


def _cum3(ts, x, terms=3):
    d = lambda p: lax.dot_general(ts, p, (NN, ((), ())), preferred_element_type=F32)
    return sum(d(p) for p in _split3(x)[:terms])


def _lb_terms(lbp, layer):
    mx = jnp.max(lbp, axis=0, keepdims=True)
    e = jnp.exp(lbp - mx)
    p = e / jnp.sum(e, axis=0, keepdims=True)
    cum = p[0:1]
    for j in range(1, layer + 1):
        cum = cum + p[j:j + 1]
    lb = cum - p[0:1]
    lbf = jnp.maximum(lb, LB_FLOOR)
    return dict(lbf=lbf, one_m=1.0 - lb, kcorr=lb - lbf, ind=jnp.where(lb > LB_FLOOR, 1.0, 0.0))


def _gate(x, lt):
    sig, nsig = _sigmoids(x)
    f = lt["lbf"] + lt["one_m"] * sig
    return jnp.log(f), lt["one_m"] * nsig + lt["kcorr"], f, sig, nsig


def _ck(x, ci):
    return x[ci * CHUNK:(ci + 1) * CHUNK]


def _block_cums(ts, g, nc):
    cs = [_cum3(ts, _ck(g, ci)) for ci in range(nc)]
    out = {kind: jnp.concatenate([c[CHUNK * i:CHUNK * (i + 1)] for c in cs], axis=0)
           for i, kind in enumerate(_MXU_KINDS)}
    b = out[("c", CHUNK)]
    ng = CHUNK // 8
    last = b.reshape(nc, ng, 8, HG_DIM)[:, :, 7:8, :]
    zero = jnp.zeros((nc, 1, 1, HG_DIM), F32)

    def spread(groups):
        return jnp.broadcast_to(jnp.concatenate(groups, axis=1), (nc, ng, 8, HG_DIM)).reshape(nc * CHUNK, HG_DIM)

    def get(kind):
        if kind in out:
            return out[kind]
        c, L = kind
        nb = L // 8
        first = lambda r: (r // nb) * nb
        if c == "c":
            return b - spread([last[:, first(r) - 1:first(r)] if r >= nb else zero for r in range(ng)])
        return spread([last[:, first(r) + nb - 1:first(r) + nb] for r in range(ng)]) - b

    return get


def _level_factors(cums, g, L):
    if L == 0:
        return None, None
    if L == 1:
        return jnp.exp(g), None
    return jnp.exp(cums(("c", L))), jnp.exp(cums(("r", L)))


def _mul(a, e):
    return a if e is None else a * e


def _hg_block_fwd(qf, k, v, g, ts, m_ref, nc):
    cums = _block_cums(ts, g, nc)
    amat = [jnp.zeros((CHUNK, CHUNK), F32)] * nc
    for li, L in enumerate(_LEVELS):
        eq, ek = _level_factors(cums, g, L)
        ql, kl, m = _mul(qf, eq), _mul(k, ek), m_ref[li]
        amat = [amat[ci] + _dot(_ck(ql, ci), _ck(kl, ci), NT) * m for ci in range(nc)]
    b = cums(("c", CHUNK))
    kst = k * jnp.exp(cums(("r", CHUNK)))
    o = [_dot(amat[ci], _ck(v, ci), NN) for ci in range(nc)]
    kv = [_dot(_ck(v, ci), _ck(kst, ci), TN) for ci in range(nc)]
    dec = [jnp.exp(b[(ci + 1) * CHUNK - 1:(ci + 1) * CHUNK, :]) for ci in range(nc)]
    return o, dec, kv, qf * jnp.exp(b), amat


def _hg_block_bwd(qf, k, v, g, do, amat, ts, m_ref, nc):
    cums = _block_cums(ts, g, nc)
    dcs = {}
    da = [_dot(_ck(do, ci), _ck(v, ci), NT) for ci in range(nc)]
    dq = jnp.zeros_like(qf)
    dk = jnp.zeros_like(qf)
    dg = jnp.zeros_like(qf)
    for li, L in enumerate(_LEVELS):
        eq, ek = _level_factors(cums, g, L)
        ql, kl, m = _mul(qf, eq), _mul(k, ek), m_ref[li]
        qlb, klb = ql.astype(BF16), kl.astype(BF16)
        dal = [(da[ci] * m).astype(BF16) for ci in range(nc)]
        dql = jnp.concatenate([_dot(dal[ci], _ck(klb, ci), NN) for ci in range(nc)], axis=0)
        dkl = jnp.concatenate([_dot(dal[ci], _ck(qlb, ci), TN) for ci in range(nc)], axis=0)
        dq = dq + _mul(dql, eq)
        dk = dk + _mul(dkl, ek)
        if L == 1:
            dg = dg + dql * ql
        elif L > 1:
            dcs[("c", L)] = (dql * ql).astype(BF16)
            dcs[("r", L)] = (dkl * kl).astype(BF16)
    b = cums(("c", CHUNK))
    e64 = jnp.exp(b)
    er64 = jnp.exp(cums(("r", CHUNK)))
    qb = qf * e64
    return dict(dq=dq, dk=dk, dg=dg, dcs=dcs, e64=e64, er64=er64, qb=qb, kst=k * er64,
                dv=[_dot(amat[ci], _ck(do, ci), TN) for ci in range(nc)],
                dec=[jnp.exp(b[(ci + 1) * CHUNK - 1:(ci + 1) * CHUNK, :]) for ci in range(nc)],
                qd=[_dot(_ck(do, ci), _ck(qb, ci), TN) for ci in range(nc)])


def _hg_state_bwd(w, v, do, starts, ends, tst, nc):
    dqb = jnp.concatenate([_dot(_ck(do, ci), starts[ci], NN) for ci in range(nc)], axis=0)
    dkst = jnp.concatenate([_dot(_ck(v, ci), ends[ci], NN) for ci in range(nc)], axis=0)
    dq = w["dq"] + dqb * w["e64"]
    dk = w["dk"] + dkst * w["er64"]
    dv = jnp.concatenate([w["dv"][ci] + _dot(_ck(w["kst"], ci), ends[ci], NT) for ci in range(nc)], axis=0)
    trow = lax.broadcasted_iota(jnp.int32, (CHUNK, 1), 0)
    dtot = jnp.concatenate(
        [jnp.where(trow == CHUNK - 1, jnp.sum(ends[ci] * starts[ci], axis=0, keepdims=True) * w["dec"][ci], 0.0)
         for ci in range(nc)], axis=0)
    dcs = dict(w["dcs"])
    dcs[("c", CHUNK)] = (dqb * w["qb"] + dtot).astype(BF16)
    dcs[("r", CHUNK)] = (dkst * w["kst"]).astype(BF16)
    dgs = [_dot(tst, jnp.concatenate([_ck(dcs[kind], ci) for kind in _ALL_KINDS], axis=0), NN) for ci in range(nc)]
    return dq, dk, dv, w["dg"] + jnp.concatenate(dgs, axis=0)


def _hgrn_fwd(proj_h, u_rows, lb_param, g_head, layer, name, phase=None):
    B, S, _ = proj_h.shape
    sb = _pick(S, (1024, 512, 256, 128, 64))
    nc = sb // CHUNK
    ts, _ = _cum_matrices()

    def body(*refs):
        ins, outs, (st,), p_in, p_out, p_sems = _split_refs(refs, 8, 4, 1, phase)
        q_ref, f_ref, i_ref, z_ref, lbp_ref, gh_ref, ts_ref, m_ref = ins
        o_ref, u_ref, sts_ref, am_ref = outs
        h_id, b_id, s_id = pl.program_id(0), pl.program_id(1), pl.program_id(2)
        _hosted_start(phase, p_in, p_out, p_sems, (h_id == 0) & (b_id == 0) & (s_id == 0))

        @pl.when(s_id == 0)
        def _():
            st[...] = jnp.zeros_like(st)

        lt = _lb_terms(lbp_ref[...], layer)
        tsv = ts_ref[...]
        gh = gh_ref[...]
        logf, k = _gate(f_ref[...], lt)[:2]
        o_part, dec, kv, qb, amat = _hg_block_fwd(_silu(q_ref[...]), k, i_ref[...], logf, tsv, m_ref, nc)
        for ci in range(nc):
            am_ref[ci] = amat[ci].astype(BF16)
        cur = st[...]
        starts = []
        for ci in range(nc):
            sts_ref[ci] = cur
            starts.append(cur)
            cur = cur * dec[ci] + kv[ci]
        st[...] = cur
        o = jnp.concatenate([o_part[ci] + _dot(_ck(qb, ci), starts[ci], NT) for ci in range(nc)], axis=0)
        o_ref[...] = o
        r = lax.rsqrt(jnp.mean(o * o, axis=-1, keepdims=True) + NORM_EPS)
        u_ref[...] = (((o * r) * gh) * _silu(z_ref[...])).astype(BF16)
        _hosted_finish(phase, p_in, p_out, p_sems, (h_id == HG_HEADS - 1) & (b_id == B - 1) & (s_id == S // sb - 1))

    col = lambda base: pl.BlockSpec((None, sb, HG_DIM), lambda h, b, s: (b, s, base + h))
    p_ispecs, p_ospecs, p_oshapes, p_alias, p_scratch, p_args = _host_phase(phase, 8, 4)
    res = pl.pallas_call(
        body, name=name,
        grid=(HG_HEADS, B, S // sb),
        in_specs=[col(0), col(HG_HEADS), col(2 * HG_HEADS), col(3 * HG_HEADS),
                  pl.BlockSpec((DEPTH, HG_DIM), lambda h, b, s: (0, h)),
                  pl.BlockSpec((1, HG_DIM), lambda h, b, s: (0, 0)),
                  pl.BlockSpec((N_CUM_F, CHUNK), lambda h, b, s: (0, 0)),
                  pl.BlockSpec((len(_LEVELS), CHUNK, CHUNK), lambda h, b, s: (0, 0, 0))] + p_ispecs,
        out_specs=[col(0), col(0),
                   pl.BlockSpec((None, None, nc, HG_DIM, HG_DIM), lambda h, b, s: (b, h, s, 0, 0)),
                   pl.BlockSpec((None, None, nc, CHUNK, CHUNK), lambda h, b, s: (b, h, s, 0, 0))] + p_ospecs,
        out_shape=[jax.ShapeDtypeStruct((B, S, HG_WIDTH), F32),
                   jax.ShapeDtypeStruct((B, S, u_rows), BF16),
                   jax.ShapeDtypeStruct((B, HG_HEADS, S // CHUNK, HG_DIM, HG_DIM), F32),
                   jax.ShapeDtypeStruct((B, HG_HEADS, S // CHUNK, CHUNK, CHUNK), BF16)] + p_oshapes,
        input_output_aliases=p_alias,
        scratch_shapes=[pltpu.VMEM((HG_DIM, HG_DIM), F32)] + p_scratch,
        compiler_params=_params(("arbitrary", "arbitrary", "arbitrary")),
    )(proj_h, proj_h, proj_h, proj_h, lb_param, g_head, ts, _level_masks(), *p_args)
    return res[0], res[1], (res[2], res[3]), list(res[4:])


def _hgrn_bwd(proj_h, o_h, du, kept, lb_param, g_head, layer, name, phase=None):
    B, S, _ = proj_h.shape
    sb = _pick(S, (512, 256, 128, 64))
    nc = sb // CHUNK
    ns = S // sb
    ts, tst = _cum_matrices()

    def body(*refs):
        ins, outs, (dst,), p_in, p_out, p_sems = _split_refs(refs, 13, 6, 1, phase)
        q_ref, f_ref, i_ref, z_ref, o_ref, du_ref, sts_ref, am_ref, lbp_ref, gh_ref, ts_ref, tst_ref, m_ref = ins
        dq_ref, df_ref, di_ref, dz_ref, dlb_ref, dgh_ref = outs
        h_id, b_id, s_id = pl.program_id(0), pl.program_id(1), pl.program_id(2)
        _hosted_start(phase, p_in, p_out, p_sems, (h_id == 0) & (b_id == 0) & (s_id == 0))

        @pl.when(s_id == 0)
        def _():
            dst[...] = jnp.zeros_like(dst)

        @pl.when((b_id == 0) & (s_id == 0))
        def _():
            dlb_ref[...] = jnp.zeros_like(dlb_ref)

        @pl.when((h_id == 0) & (b_id == 0) & (s_id == 0))
        def _():
            dgh_ref[...] = jnp.zeros_like(dgh_ref)

        lt = _lb_terms(lbp_ref[...], layer)
        gh = gh_ref[...]
        tsv = ts_ref[...]
        tstv = tst_ref[...]
        logf, k, f, sig, nsig = _gate(f_ref[...], lt)
        o = o_ref[...]
        dub = du_ref[...]
        r = lax.rsqrt(jnp.mean(o * o, axis=-1, keepdims=True) + NORM_EPS)
        n = o * r
        sg, sg_grad = _silu_and_grad(z_ref[...])
        dz_ref[...] = (dub * (n * gh) * sg_grad).astype(BF16)
        dgh_ref[...] += jnp.sum(dub * sg * n, axis=0, keepdims=True)
        dn = dub * sg * gh
        do = r * (dn - n * jnp.mean(dn * n, axis=-1, keepdims=True))
        v = i_ref[...]
        qf, qf_grad = _silu_and_grad(q_ref[...])
        w = _hg_block_bwd(qf, k, v, logf, do, [am_ref[ci] for ci in range(nc)], tsv, m_ref, nc)
        cur = dst[...]
        ends = [None] * nc
        for ci in reversed(range(nc)):
            ends[ci] = cur
            cur = cur * w["dec"][ci] + w["qd"][ci]
        dst[...] = cur
        dq, dk, dv, dg = _hg_state_bwd(w, v, do, [sts_ref[ci] for ci in range(nc)], ends, tstv, nc)
        di_ref[...] = dv.astype(BF16)
        dq_ref[...] = (dq * qf_grad).astype(BF16)
        scaled = (dg - f * dk) / f
        df_ref[...] = (scaled * lt["one_m"] * sig * nsig).astype(BF16)
        dlb_ref[...] += jnp.sum(scaled * (lt["ind"] - sig), axis=0, keepdims=True)
        _hosted_finish(phase, p_in, p_out, p_sems, (h_id == HG_HEADS - 1) & (b_id == B - 1) & (s_id == ns - 1))

    col = lambda base: pl.BlockSpec((None, sb, HG_DIM), lambda h, b, s: (b, ns - 1 - s, base + h))
    out_col = pl.BlockSpec((None, sb, HG_DIM), lambda h, b, s: (b, ns - 1 - s, h))
    dt = jax.ShapeDtypeStruct((B, S, HG_WIDTH), BF16)
    p_ispecs, p_ospecs, p_oshapes, p_alias, p_scratch, p_args = _host_phase(phase, 13, 6)
    res = pl.pallas_call(
        body, name=name,
        grid=(HG_HEADS, B, ns),
        in_specs=[col(0), col(HG_HEADS), col(2 * HG_HEADS), col(3 * HG_HEADS), col(0), col(0),
                  pl.BlockSpec((None, None, nc, HG_DIM, HG_DIM), lambda h, b, s: (b, h, ns - 1 - s, 0, 0)),
                  pl.BlockSpec((None, None, nc, CHUNK, CHUNK), lambda h, b, s: (b, h, ns - 1 - s, 0, 0)),
                  pl.BlockSpec((DEPTH, HG_DIM), lambda h, b, s: (0, h)),
                  pl.BlockSpec((1, HG_DIM), lambda h, b, s: (0, 0)),
                  pl.BlockSpec((N_CUM_F, CHUNK), lambda h, b, s: (0, 0)),
                  pl.BlockSpec((CHUNK, N_CUM), lambda h, b, s: (0, 0)),
                  pl.BlockSpec((len(_LEVELS), CHUNK, CHUNK), lambda h, b, s: (0, 0, 0))] + p_ispecs,
        out_specs=[out_col, out_col, out_col, out_col,
                   pl.BlockSpec((1, HG_DIM), lambda h, b, s: (0, h)),
                   pl.BlockSpec((1, HG_DIM), lambda h, b, s: (0, 0))] + p_ospecs,
        out_shape=[dt, dt, dt, dt, jax.ShapeDtypeStruct((1, HG_WIDTH), F32),
                   jax.ShapeDtypeStruct((1, HG_DIM), F32)] + p_oshapes,
        input_output_aliases=p_alias,
        scratch_shapes=[pltpu.VMEM((HG_DIM, HG_DIM), F32)] + p_scratch,
        compiler_params=_params(("arbitrary", "arbitrary", "arbitrary")),
    )(proj_h, proj_h, proj_h, proj_h, o_h, du, kept[0], kept[1], lb_param, g_head, ts, tst, _level_masks(), *p_args)
    return tuple(res[:6]) + (list(res[6:]),)


def _rope_tables(S):
    half = ATT_DIM // 2
    inv_freq = ROPE_THETA ** (-jnp.arange(half, dtype=F32) / half)
    ang = jnp.arange(S).astype(F32)[:, None] * inv_freq[None, :]
    cos = jnp.cos(ang)
    sin = jnp.sin(ang)
    cos = jnp.concatenate([cos, cos, cos, cos], axis=1)
    sin = jnp.concatenate([-sin, sin, -sin, sin], axis=1)
    return cos, sin


def _attn_common():
    lane = lax.broadcasted_iota(jnp.int32, (1, 2 * ATT_DIM), 1)
    first_half = (lane % ATT_DIM) < (ATT_DIM // 2)
    left = lane < ATT_DIM

    def swap(x):
        return jnp.where(first_half, pltpu.roll(x, 128 - ATT_DIM // 2, 1), pltpu.roll(x, ATT_DIM // 2, 1))

    def rope(x, cos, sin):
        return x * cos + swap(x) * sin

    def rope_bwd(dy, cos, sin):
        return dy * cos + swap(dy * sin)

    def dup(x):
        xs = pltpu.roll(x, ATT_DIM, 1)
        return [jnp.where(left, x, xs), jnp.where(left, xs, x)]

    return left, rope, rope_bwd, dup


GROUP = ATT_HEADS // 2
GROUP_ROWS = GROUP * ATT_BLOCK


def _attn_bias(i):
    r = lax.broadcasted_iota(jnp.int32, (ATT_BLOCK, 2 * ATT_BLOCK), 0)
    c = lax.broadcasted_iota(jnp.int32, (ATT_BLOCK, 2 * ATT_BLOCK), 1)
    ok = (c > r) & (c <= r + ATT_BLOCK) & ((c >= ATT_BLOCK) | (i > 0))
    return jnp.where(ok, 0.0, NEG_INF)


def _stack_heads(pairs, left):
    rows = []
    for x in pairs:
        rows += [jnp.where(left, x, 0.0), jnp.where(left, 0.0, x)]
    return jnp.concatenate(rows, axis=0)


def _unstack_heads(y, left, pp):
    r0 = 2 * pp * ATT_BLOCK
    return jnp.where(left, y[r0:r0 + ATT_BLOCK], y[r0 + ATT_BLOCK:r0 + 2 * ATT_BLOCK])


def _row_sums(x):
    return _dot(x, jnp.ones((x.shape[1], 128), BF16), NN)


def _attn_probs(qs, kd, vd, sink, bias):
    n = range(len(qs))
    rows = qs[0].shape[0]
    s = [(_dot(qs[j], kd[j], NT).reshape(rows // ATT_BLOCK, ATT_BLOCK, 2 * ATT_BLOCK) * ATT_SCALE + bias[None])
         .reshape(rows, 2 * ATT_BLOCK) for j in n]
    m = [jnp.max(jnp.maximum(jnp.maximum(s[j][:, :128], s[j][:, 128:]), sink[j]), axis=-1, keepdims=True) for j in n]
    pu = [jnp.exp(s[j] - m[j]) for j in n]
    es = [jnp.exp(sink[j] - m[j]) for j in n]
    ones = jnp.ones((2 * ATT_BLOCK, 128), BF16)
    ov = [_dot(pu[j], jnp.concatenate([vd[j].astype(BF16), ones], axis=1), NN) for j in n]
    inv = [1.0 / (ov[j][:, 128:] + es[j]) for j in n]
    return ([pu[j] * jnp.concatenate([inv[j], inv[j]], axis=1) for j in n], [es[j] * inv[j] for j in n],
            [ov[j][:, :128] * inv[j] for j in n])


def _sink_rows(sinks_l):
    return jnp.broadcast_to(jnp.repeat(sinks_l, ATT_BLOCK)[:, None], (ATT_HEADS * ATT_BLOCK, 128))


_Z0 = (2 * ATT_WIDTH + 2 * KV_WIDTH - ATT_WIDTH) // 256


def _attn_fwd(proj_a, u, sinks_l, cos, sin, name, phase=None):
    B, S, _ = proj_a.shape
    nb = S // ATT_BLOCK

    def body(*refs):
        ins, (u_ref, p_ref, o_ref, ps_ref), _, p_in, p_out, p_sems = _split_refs(refs, 13, 4, 0, phase)
        q_ref, kvc_ref, kvp_ref, z0, z1, z2, z3, cos_ref, sin_ref, cosp_ref, sinp_ref, sinks_ref, _ = ins
        i = pl.program_id(1)
        _hosted_start(phase, p_in, p_out, p_sems, (pl.program_id(0) == 0) & (i == 0))
        left, rope, _, dup = _attn_common()
        cos_c, sin_c = cos_ref[...], sin_ref[...]
        kvc = kvc_ref[...]
        kvp = kvp_ref[...]
        kw = jnp.concatenate([rope(kvp[:, :KV_WIDTH], cosp_ref[...], sinp_ref[...]),
                              rope(kvc[:, :KV_WIDTH], cos_c, sin_c)], axis=0)
        vw = jnp.concatenate([kvp[:, KV_WIDTH:], kvc[:, KV_WIDTH:]], axis=0)
        kd, vd = dup(kw), dup(vw)
        bias = _attn_bias(i)
        zs = (z0, z1, z2, z3)
        pairs = [range(4 * kvh, 4 * kvh + 4) for kvh in range(2)]
        qs = [_stack_heads([rope(q_ref[:, 128 * pr:128 * (pr + 1)], cos_c, sin_c) for pr in pairs[kvh]], left)
              for kvh in range(2)]
        sink = [sinks_ref[kvh * GROUP_ROWS:(kvh + 1) * GROUP_ROWS, :] for kvh in range(2)]
        p, ps, o = _attn_probs(qs, kd, vd, sink, bias)
        eye = (lax.broadcasted_iota(jnp.int32, (ATT_BLOCK, 128), 0)
               == lax.broadcasted_iota(jnp.int32, (ATT_BLOCK, 128), 1))
        for kvh in range(2):
            p_ref[kvh] = p[kvh].astype(BF16)
            for g in range(GROUP):
                blk = ps[kvh][g * ATT_BLOCK:(g + 1) * ATT_BLOCK, :]
                ps_ref[kvh * GROUP + g:kvh * GROUP + g + 1, :] = jnp.sum(jnp.where(eye, blk, 0.0), axis=0, keepdims=True)
            for pp, pr in enumerate(pairs[kvh]):
                z = zs[pr // 2][:, 128 * (pr % 2):128 * (pr % 2 + 1)]
                o128 = _unstack_heads(o[kvh], left, pp)
                o_ref[:, 128 * pr:128 * (pr + 1)] = o128.astype(BF16)
                u_ref[:, 128 * pr:128 * (pr + 1)] = (o128 * _silu(z)).astype(BF16)
        _hosted_finish(phase, p_in, p_out, p_sems, (pl.program_id(0) == B - 1) & (i == nb - 1))

    rowblk = lambda w, cb: pl.BlockSpec((None, ATT_BLOCK, w), lambda b, i: (b, i, cb))
    tab = pl.BlockSpec((ATT_BLOCK, 128), lambda b, i: (i, 0))
    tabp = pl.BlockSpec((ATT_BLOCK, 128), lambda b, i: (jnp.maximum(i - 1, 0), 0))
    p_ispecs, p_ospecs, p_oshapes, p_alias, p_scratch, p_args = _host_phase(phase, 13, 4)
    res = pl.pallas_call(
        body, name=name,
        grid=(B, nb),
        in_specs=[rowblk(ATT_WIDTH, 0), rowblk(256, 4),
                  pl.BlockSpec((None, ATT_BLOCK, 256), lambda b, i: (b, jnp.maximum(i - 1, 0), 4)),
                  rowblk(256, _Z0), rowblk(256, _Z0 + 1), rowblk(256, _Z0 + 2), rowblk(256, _Z0 + 3),
                  tab, tab, tabp, tabp,
                  pl.BlockSpec((ATT_HEADS * ATT_BLOCK, 128), lambda b, i: (0, 0)),
                  pl.BlockSpec(memory_space=pl.ANY)] + p_ispecs,
        out_specs=[pl.BlockSpec((None, ATT_BLOCK, ATT_WIDTH), lambda b, i: (b, i, 1)),
                   pl.BlockSpec((None, None, 2, GROUP_ROWS, 2 * ATT_BLOCK), lambda b, i: (b, i, 0, 0, 0)),
                   pl.BlockSpec((None, ATT_BLOCK, ATT_WIDTH), lambda b, i: (b, i, 0)),
                   pl.BlockSpec((None, None, ATT_HEADS, 128), lambda b, i: (b, i, 0, 0))] + p_ospecs,
        out_shape=[jax.ShapeDtypeStruct(u.shape, BF16),
                   jax.ShapeDtypeStruct((B, nb, 2, GROUP_ROWS, 2 * ATT_BLOCK), BF16),
                   jax.ShapeDtypeStruct((B, S, ATT_WIDTH), BF16),
                   jax.ShapeDtypeStruct((B, nb, ATT_HEADS, 128), F32)] + p_oshapes,
        input_output_aliases={12: 0, **p_alias},
        scratch_shapes=p_scratch,
        compiler_params=_params(("arbitrary", "arbitrary")),
    )(proj_a, proj_a, proj_a, proj_a, proj_a, proj_a, proj_a, cos, sin, cos, sin, sinks_l, u, *p_args)
    return res[0], tuple(res[1:4]), list(res[4:])


def _attn_bwd(proj_a, du, kept, cos, sin, name, phase=None):
    B, S, _ = proj_a.shape
    nb = S // ATT_BLOCK
    p_kept, o_kept, ps_kept = kept

    def body(*refs):
        ins, outs, (carry, sk_acc), p_in, p_out, p_sems = _split_refs(refs, 15, 4, 2, phase)
        (q_ref, kvc_ref, kvp_ref, z0, z1, z2, z3, du_ref, cos_ref, sin_ref, cosp_ref, sinp_ref,
         p_ref, o_ref, ps_ref) = ins
        dq_ref, dkv_ref, dz_ref, dsk_ref = outs
        b_id, i = pl.program_id(0), pl.program_id(1)
        _hosted_start(phase, p_in, p_out, p_sems, (b_id == 0) & (i == 0))

        @pl.when((b_id == 0) & (i == 0))
        def _():
            sk_acc[...] = jnp.zeros_like(sk_acc)

        @pl.when(i == 0)
        def _():
            carry[...] = jnp.zeros_like(carry)

        @pl.when(i < nb)
        def _():
            left, rope, rope_bwd, dup = _attn_common()
            cos_c, sin_c = cos_ref[...], sin_ref[...]
            cos_p, sin_p = cosp_ref[...], sinp_ref[...]
            kvc = kvc_ref[...]
            kvp = kvp_ref[...]
            kw = jnp.concatenate([rope(kvp[:, :KV_WIDTH], cos_p, sin_p), rope(kvc[:, :KV_WIDTH], cos_c, sin_c)], axis=0)
            vw = jnp.concatenate([kvp[:, KV_WIDTH:], kvc[:, KV_WIDTH:]], axis=0)
            kd, vd = dup(kw), dup(vw)
            zs = (z0, z1, z2, z3)
            units = [(kvh, hf) for kvh in range(2) for hf in range(2)]
            half = GROUP_ROWS // 2
            pairs = [range(4 * kvh + 2 * hf, 4 * kvh + 2 * hf + 2) for kvh, hf in units]
            ku = [kd[kvh] for kvh, _ in units]
            vu = [vd[kvh] for kvh, _ in units]
            ps_all = ps_ref[...]
            head_row = lax.broadcasted_iota(jnp.int32, (ATT_HEADS, 128), 0)
            eye = (lax.broadcasted_iota(jnp.int32, (ATT_BLOCK, 128), 0)
                   == lax.broadcasted_iota(jnp.int32, (ATT_BLOCK, 128), 1))

            def first(j):
                kvh, hf = units[j]
                p = p_ref[kvh, hf * half:(hf + 1) * half, :]
                parts = []
                for pr in pairs[j]:
                    cols = slice(128 * pr, 128 * (pr + 1))
                    sg, sg_grad = _silu_and_grad(zs[pr // 2][:, 128 * (pr % 2):128 * (pr % 2 + 1)])
                    du128 = du_ref[:, cols]
                    dz_ref[:, cols] = (du128 * o_ref[:, cols].astype(F32) * sg_grad).astype(BF16)
                    parts.append(du128 * sg)
                dos = _stack_heads(parts, left)
                dp = _dot(dos, vu[j], NT)
                delta = _row_sums(p.astype(F32) * dp)
                ds = (p.astype(F32) * (dp - jnp.concatenate([delta, delta], axis=1)) * ATT_SCALE).astype(BF16)
                sk = jnp.zeros((ATT_HEADS, 128), F32)
                for hh in range(4):
                    hd = kvh * GROUP + 4 * hf + hh
                    drow = jnp.sum(jnp.where(eye, delta[hh * ATT_BLOCK:(hh + 1) * ATT_BLOCK, :], 0.0), axis=0,
                                   keepdims=True)
                    sk = sk - jnp.where(head_row == hd, ps_all * drow, 0.0)
                sk_acc[...] += sk
                qs = _stack_heads([rope(q_ref[:, 128 * pr:128 * (pr + 1)], cos_c, sin_c) for pr in pairs[j]], left)
                return ds, p, dos.astype(BF16), qs.astype(BF16)

            def second(j, ds, p, dos, qs):
                dqs = _dot(ds, ku[j], NN)
                for pp, pr in enumerate(pairs[j]):
                    dq_ref[:, 128 * pr:128 * (pr + 1)] = rope_bwd(_unstack_heads(dqs, left, pp),
                                                                  cos_c, sin_c).astype(BF16)
                return _dot(ds, qs, TN), _dot(p, dos, TN)

            got, dku, dvu = {}, [None] * len(units), [None] * len(units)
            for j in range(len(units) + 1):
                if j < len(units):
                    got[j] = first(j)
                if j >= 1:
                    dku[j - 1], dvu[j - 1] = second(j - 1, *got.pop(j - 1))
            dkd = [dku[0] + dku[1], dku[2] + dku[3]]
            dvd = [dvu[0] + dvu[1], dvu[2] + dvu[3]]
            fold = lambda pr: jnp.where(left, pr[0] + pltpu.roll(pr[0], ATT_DIM, 1), pr[1] + pltpu.roll(pr[1], ATT_DIM, 1))
            dkw = fold(dkd)
            dvw = fold(dvd)
            prev = jnp.concatenate([rope_bwd(dkw[:ATT_BLOCK], cos_p, sin_p), dvw[:ATT_BLOCK]], axis=1)
            cur = jnp.concatenate([rope_bwd(dkw[ATT_BLOCK:], cos_c, sin_c), dvw[ATT_BLOCK:]], axis=1)
            dkv_ref[...] = (carry[...] + prev).astype(BF16)
            carry[...] = cur

        @pl.when(i == nb)
        def _():
            dkv_ref[...] = carry[...].astype(BF16)

        @pl.when((b_id == B - 1) & (i == nb))
        def _():
            diag = (lax.broadcasted_iota(jnp.int32, (ATT_HEADS, 128), 0)
                    == lax.broadcasted_iota(jnp.int32, (ATT_HEADS, 128), 1))
            tot = jnp.sum(sk_acc[...], axis=1, keepdims=True)
            dsk_ref[...] = jnp.sum(jnp.where(diag, tot, 0.0), axis=0, keepdims=True)

        _hosted_finish(phase, p_in, p_out, p_sems, (b_id == B - 1) & (i == nb))

    cl = lambda i: jnp.minimum(i, nb - 1)
    pv = lambda i: jnp.maximum(jnp.minimum(i, nb - 1) - 1, 0)
    rowblk = lambda w, cb: pl.BlockSpec((None, ATT_BLOCK, w), lambda b, i: (b, cl(i), cb))
    tab = pl.BlockSpec((ATT_BLOCK, 128), lambda b, i: (cl(i), 0))
    tabp = pl.BlockSpec((ATT_BLOCK, 128), lambda b, i: (pv(i), 0))
    p_ispecs, p_ospecs, p_oshapes, p_alias, p_scratch, p_args = _host_phase(phase, 15, 4)
    res = pl.pallas_call(
        body, name=name,
        grid=(B, nb + 1),
        in_specs=[rowblk(ATT_WIDTH, 0), rowblk(256, 4),
                  pl.BlockSpec((None, ATT_BLOCK, 256), lambda b, i: (b, pv(i), 4)),
                  rowblk(256, _Z0), rowblk(256, _Z0 + 1), rowblk(256, _Z0 + 2), rowblk(256, _Z0 + 3),
                  rowblk(ATT_WIDTH, 1),
                  tab, tab, tabp, tabp,
                  pl.BlockSpec((None, None, 2, GROUP_ROWS, 2 * ATT_BLOCK), lambda b, i: (b, cl(i), 0, 0, 0)),
                  rowblk(ATT_WIDTH, 0),
                  pl.BlockSpec((None, None, ATT_HEADS, 128), lambda b, i: (b, cl(i), 0, 0))] + p_ispecs,
        out_specs=[rowblk(ATT_WIDTH, 0),
                   pl.BlockSpec((None, ATT_BLOCK, 256), lambda b, i: (b, jnp.maximum(i - 1, 0), 0)),
                   rowblk(ATT_WIDTH, 0),
                   pl.BlockSpec((1, 128), lambda b, i: (0, 0))] + p_ospecs,
        out_shape=[jax.ShapeDtypeStruct((B, S, ATT_WIDTH), BF16), jax.ShapeDtypeStruct((B, S, 256), BF16),
                   jax.ShapeDtypeStruct((B, S, ATT_WIDTH), BF16), jax.ShapeDtypeStruct((1, 128), F32)] + p_oshapes,
        input_output_aliases=p_alias,
        scratch_shapes=[pltpu.VMEM((ATT_BLOCK, 256), F32), pltpu.VMEM((ATT_HEADS, 128), F32)] + p_scratch,
        compiler_params=_params(("arbitrary", "arbitrary")),
    )(proj_a, proj_a, proj_a, proj_a, proj_a, proj_a, proj_a, du, cos, sin, cos, sin, p_kept, o_kept, ps_kept, *p_args)
    return tuple(res[:4]) + (list(res[4:]),)


def _outproj_fwd(u2, w_out, x2, g_post, target2, name):
    T, D = x2.shape
    tm = _pick(T, (512, 256, 128))
    last = target2 is not None

    def body(u_ref, w_ref, x_ref, g_ref, *rest):
        y = lax.dot_general(u_ref[...], w_ref[...], (NN, ((), ())), preferred_element_type=F32)
        r = lax.rsqrt(jnp.mean(y * y, axis=-1, keepdims=True) + NORM_EPS)
        xn = x_ref[...] + (y * r) * g_ref[...]
        if last:
            t_ref, y_ref, dx_ref, loss_ref = rest
            err = xn - t_ref[...]
            dx_ref[...] = err * (1.0 / D)
            sq = err * err
            acc = sq[:, 0:128]
            for kk in range(1, D // 128):
                acc = acc + sq[:, 128 * kk:128 * (kk + 1)]
            part = jnp.sum(acc.reshape(tm // 8, 8, 128), axis=0) * (0.5 / D)

            @pl.when(pl.program_id(0) == 0)
            def _():
                loss_ref[...] = jnp.zeros_like(loss_ref)

            loss_ref[...] += part
        else:
            y_ref, xn_ref = rest
            xn_ref[...] = xn
        y_ref[...] = y

    row = pl.BlockSpec((tm, D), lambda i: (i, 0))
    in_specs = [pl.BlockSpec((tm, MIX_WIDTH), lambda i: (i, 0)),
                pl.BlockSpec((MIX_WIDTH, D), lambda i: (0, 0)), row,
                pl.BlockSpec((1, D), lambda i: (0, 0))]
    args = [u2, w_out, x2, g_post]
    out_specs = [row, row]
    out_shape = [jax.ShapeDtypeStruct((T, D), F32), jax.ShapeDtypeStruct((T, D), F32)]
    if last:
        in_specs.append(row)
        args.append(target2)
        out_specs.append(pl.BlockSpec((8, 128), lambda i: (0, 0)))
        out_shape.append(jax.ShapeDtypeStruct((8, 128), F32))
    return pl.pallas_call(
        body, name=name, grid=(T // tm,), in_specs=in_specs, out_specs=out_specs, out_shape=out_shape,
        compiler_params=_params(("arbitrary",)),
    )(*args)


def _outproj_bwd(dxn2, y2, g_post, w_out, name):
    T, D = y2.shape
    N = w_out.shape[0]
    tm = _pick(T, (512, 256, 128))
    nt = T // tm

    def body(dx_ref, y_ref, g_ref, w_ref, dy_ref, dg_ref, du_ref, acc):
        i = pl.program_id(0)

        @pl.when(i == 0)
        def _():
            acc[...] = jnp.zeros_like(acc)

        y = y_ref[...]
        dxn = dx_ref[...]
        r = lax.rsqrt(jnp.mean(y * y, axis=-1, keepdims=True) + NORM_EPS)
        n = y * r
        dn = dxn * g_ref[...]
        dy = (r * (dn - n * jnp.mean(dn * n, axis=-1, keepdims=True))).astype(BF16)
        dy_ref[...] = dy
        du_ref[...] = lax.dot_general(dy, w_ref[...], (NT, ((), ())), preferred_element_type=F32)
        acc[...] += jnp.sum((dxn * n).reshape(tm // 8, 8, D), axis=0)

        @pl.when(i == nt - 1)
        def _():
            dg_ref[...] = jnp.sum(acc[...], axis=0, keepdims=True)

    row = pl.BlockSpec((tm, D), lambda i: (i, 0))
    vec = pl.BlockSpec((1, D), lambda i: (0, 0))
    return pl.pallas_call(
        body, name=name, grid=(nt,),
        in_specs=[row, row, vec, pl.BlockSpec((N, D), lambda i: (0, 0), pipeline_mode=pl.Buffered(1))],
        out_specs=[row, vec, pl.BlockSpec((tm, N), lambda i: (i, 0))],
        out_shape=[jax.ShapeDtypeStruct((T, D), BF16), jax.ShapeDtypeStruct((1, D), F32),
                   jax.ShapeDtypeStruct((T, N), F32)],
        scratch_shapes=[pltpu.VMEM((8, D), F32)],
        compiler_params=_params(("arbitrary",)),
    )(dxn2, y2, g_post, w_out)


def _inproj_bwd(pieces, w_t, x2, dxn2, g_pre, name, phase=None):
    T, D = x2.shape
    widths = [p.shape[1] for p in pieces]
    offs = [sum(widths[:i]) for i in range(len(pieces))]
    n_p = len(pieces)
    tm = _pick(T, (256, 128))
    nt = T // tm

    def body(*refs):
        ins, (dx_ref, dg_ref), (acc,), p_in, p_out, p_sems = _split_refs(refs, n_p + 4, 2, 1, phase)
        w_ref, x_ref, dxn_ref, g_ref = ins[n_p:]
        i = pl.program_id(0)
        _hosted_start(phase, p_in, p_out, p_sems, i == 0)

        @pl.when(i == 0)
        def _():
            acc[...] = jnp.zeros_like(acc)

        dh = jnp.zeros((tm, D), F32)
        for p in range(n_p):
            dh = dh + lax.dot_general(ins[p][...], w_ref[offs[p]:offs[p] + widths[p], :], (NN, ((), ())),
                                      preferred_element_type=F32)
        x = x_ref[...]
        r = lax.rsqrt(jnp.mean(x * x, axis=-1, keepdims=True) + NORM_EPS)
        n = x * r
        dn = dh * g_ref[...]
        dx_ref[...] = dxn_ref[...] + r * (dn - n * jnp.mean(dn * n, axis=-1, keepdims=True))
        acc[...] += jnp.sum((dh * n).reshape(tm // 8, 8, D), axis=0)

        @pl.when(i == nt - 1)
        def _():
            dg_ref[...] = jnp.sum(acc[...], axis=0, keepdims=True)

        _hosted_finish(phase, p_in, p_out, p_sems, i == nt - 1)

    row = pl.BlockSpec((tm, D), lambda i: (i, 0))
    vec = pl.BlockSpec((1, D), lambda i: (0, 0))
    p_ispecs, p_ospecs, p_oshapes, p_alias, p_scratch, p_args = _host_phase(phase, n_p + 4, 2)
    res = pl.pallas_call(
        body, name=name, grid=(nt,),
        in_specs=[pl.BlockSpec((tm, w), lambda i: (i, 0)) for w in widths]
        + [pl.BlockSpec((sum(widths), D), lambda i: (0, 0), pipeline_mode=pl.Buffered(1)), row, row, vec] + p_ispecs,
        out_specs=[row, vec] + p_ospecs,
        out_shape=[jax.ShapeDtypeStruct((T, D), F32), jax.ShapeDtypeStruct((1, D), F32)] + p_oshapes,
        input_output_aliases=p_alias,
        scratch_shapes=[pltpu.VMEM((8, D), F32)] + p_scratch,
        compiler_params=_params(("arbitrary",)),
    )(*pieces, w_t, x2, dxn2, g_pre, *p_args)
    return res[0], res[1], list(res[2:])


def _step(x, target, g_pre, g_post, lb_param, g_head, sinks, shards=None, full=None):
    B, S, D = x.shape
    T = B * S
    dist = shards is not None
    first, last = 0, DEPTH - 1
    if dist:
        a_loc, b_loc = shards
        ra, rb = a_loc.shape[1], b_loc.shape[1]
        side = _own_side_blocks()
        placed = lambda loc, nm: _place_own(loc, side, "place_" + nm)
        w_in0 = _run_phase(_gather_ici_phase([a_loc[0]], [placed(a_loc[0], "in0")]), "gather_in0_ici")
        w_in0 = _run_phase(_gather_d2d_phase(w_in0, [ra]), "gather_in0_d2d")[0]
        w_in, w_out = [w_in0, None], [None, None]
    else:
        w_in, w_out = list(full[0]), list(full[1])
    cos, sin = _rope_tables(S)
    saved = []
    xs = x
    loss_part = None
    dxn = None
    for l in range(DEPTH):
        x2 = xs.reshape(T, D)
        proj_h, proj_a, h = _inproj(x2, g_pre[l:l + 1], w_in[l], f"inproj{l}")
        proj_h = proj_h.reshape(B, S, N_H)
        proj_a = proj_a.reshape(B, S, N_A)
        phase = None
        if dist and l == first:
            phase = _gather_ici_phase([a_loc[1], b_loc[0]], [placed(a_loc[1], "in1"), placed(b_loc[0], "out0")])
        if dist and l == last:
            phase = _gather_d2d_phase([w_out1_part], [rb])
        o_h, u, states, got = _hgrn_fwd(proj_h, MIX_WIDTH, lb_param, g_head[l:l + 1], l, f"hgrn_fwd{l}", phase)
        phase = None
        if dist and l == first:
            phase = _merge_phases(_gather_d2d_phase(got, [ra, rb]),
                                  _gather_ici_phase([b_loc[1]], [placed(b_loc[1], "out1")]))
        if dist and l == last:
            w_out[1] = got[0]
        u, kept_a, got = _attn_fwd(proj_a, u, _sink_rows(sinks[l]), cos, sin, f"attn_fwd{l}", phase)
        if dist and l == first:
            w_in[1], w_out[0], w_out1_part = got
        u2 = u.reshape(T, MIX_WIDTH)
        if l < last:
            y, xn = _outproj_fwd(u2, w_out[l], x2, g_post[l:l + 1], None, f"outproj{l}")
            xn = xn.reshape(B, S, D)
        else:
            y, dxn, loss_part = _outproj_fwd(u2, w_out[l], x2, g_post[l:l + 1], target.reshape(T, D), f"outproj{l}")
            xn = None
        saved.append((x2, h, proj_h, proj_a, o_h, u2, states, kept_a, y))
        xs = xn

    dw_in, dw_out = [None] * DEPTH, [None] * DEPTH
    dg_pre, dg_post, dlb, dg_head, dsinks = [], [], [], [], []
    for l in reversed(range(DEPTH)):
        x2, h, proj_h, proj_a, o_h, u2, states, kept_a, y = saved[l]
        dy, dgp, du = _outproj_bwd(dxn, y, g_post[l:l + 1], w_out[l], f"outproj_bwd{l}")
        du = du.reshape(B, S, MIX_WIDTH)
        dw_out[l] = _mm_tn([u2], dy, f"wgrad_out{l}")
        phase = None
        if dist:
            phase = _reduce_d2d_phase([dw_out[l]], [rb])
            if l == first:
                phase = _merge_phases(_reduce_ici_phase([part_in1]), phase)
        dqh, dfh, dih, dzh, dlb_l, dgh, got = _hgrn_bwd(
            proj_h, o_h, du, states, lb_param, g_head[l:l + 1], l, f"hgrn_bwd{l}", phase)
        if dist:
            if l == first:
                sum_in = _chip_sum(part_in1, got[0], "chip_sum_in1", 1)
            part_out = _pair_sum(dw_out[l], got[-1], side, f"pair_sum_out{l}")
        dqa, dkv, dza, dsk, got = _attn_bwd(proj_a, du, kept_a, cos, sin, f"attn_bwd{l}",
                                            _reduce_ici_phase([part_out]) if dist else None)
        if dist:
            sum_out = _chip_sum(part_out, got[0], f"chip_sum_out{l}", l, None if l == last else sum_out)
        dproj = [p.reshape(T, p.shape[-1]) for p in (dqh, dfh, dih, dzh, dqa, dkv, dza)]
        dw_in[l] = _mm_tn(dproj, h, f"wgrad_in{l}")
        phase = None
        if dist and l == last:
            phase = _reduce_d2d_phase([dw_in[l]], [ra])
        if dist and l == first:
            got = _run_phase(_reduce_d2d_phase([dw_in[l]], [ra]), "reduce_in0_d2d")
            part_in0 = _pair_sum(dw_in[l], got[0], side, "pair_sum_in0")
            phase = _reduce_ici_phase([part_in0])
        dxn, dgpre, got = _inproj_bwd(dproj, w_in[l], x2, dxn, g_pre[l:l + 1], f"inproj_bwd{l}", phase)
        if dist and l == last:
            part_in1 = _pair_sum(dw_in[l], got[0], side, "pair_sum_in1")
        if dist and l == first:
            sum_in = _chip_sum(part_in0, got[0], "chip_sum_in0", 0, sum_in)
        dg_pre.append(dgpre)
        dg_post.append(dgp)
        dlb.append(dlb_l)
        dg_head.append(dgh)
        dsinks.append(dsk)
    rev = lambda lst: jnp.concatenate(lst[::-1], axis=0)
    if not dist:
        sum_in, sum_out = jnp.stack(dw_in), jnp.stack(dw_out)
    return (loss_part, dxn.reshape(B, S, D), sum_in, sum_out,
            rev(dg_pre), rev(dg_post), rev(dlb), rev(dg_head), rev(dsinks))


def _me_and_peers():
    x, y, c = lax.axis_index("x"), lax.axis_index("y"), lax.axis_index("c")
    me = 4 * x + 2 * y + c
    peers = []
    for k in range(1, N_DEV):
        px = 1 - x if k & 4 else x
        py = 1 - y if k & 2 else y
        pc = 1 - c if k & 1 else c
        peers.append(((px, py, pc), 4 * px + 2 * py + pc))
    return me, peers


class _Phase:
    def __init__(self, arrays, out_shapes, aliases, n_send, build):
        self.arrays, self.out_shapes, self.aliases = list(arrays), list(out_shapes), dict(aliases)
        self.n_send, self.build = n_send, build

    def scratch(self):
        return [pltpu.SemaphoreType.DMA((self.n_send,)), pltpu.SemaphoreType.DMA((self.n_send,))]

    def _copies(self, in_refs, out_refs, sems, arrivals):
        send_sems, recv_sems = sems
        sends, recvs = self.build(in_refs, out_refs)
        assert len(sends) == self.n_send == len(recvs)
        out = [pltpu.make_async_remote_copy(src_ref=s, dst_ref=d, send_sem=send_sems.at[i], recv_sem=recv_sems.at[i],
                                            device_id=dev, device_id_type=MESH) for i, (s, d, dev) in enumerate(sends)]
        inc = [pltpu.make_async_remote_copy(src_ref=s, dst_ref=r, send_sem=send_sems.at[i], recv_sem=recv_sems.at[i],
                                            device_id=dev, device_id_type=MESH)
               for i, ((s, _, dev), r) in enumerate(zip(sends, recvs))] if arrivals else []
        return out, inc

    def start(self, in_refs, out_refs, sems):
        out, _ = self._copies(in_refs, out_refs, sems, False)
        for cp in out:
            cp.start()

    def finish(self, in_refs, out_refs, sems):
        out, inc = self._copies(in_refs, out_refs, sems, True)
        for cp in inc:
            cp.wait_recv()
        for cp in out:
            cp.wait_send()


_ANY = pl.BlockSpec(memory_space=pl.ANY)


def _host_phase(phase, n_in, n_out):
    if phase is None:
        return [], [], [], {}, [], []
    aliases = {n_in + i: n_out + o for i, o in phase.aliases.items()}
    return ([_ANY] * len(phase.arrays), [_ANY] * len(phase.out_shapes), phase.out_shapes, aliases, phase.scratch(),
            phase.arrays)


def _split_refs(refs, n_in, n_out, n_scr, phase):
    pi = len(phase.arrays) if phase else 0
    po = len(phase.out_shapes) if phase else 0
    a = n_in + pi
    b = a + n_out + po
    return (refs[:n_in], refs[a:a + n_out], refs[b:b + n_scr], refs[n_in:a], refs[a + n_out:b], refs[b + n_scr:])


def _hosted_start(phase, p_in, p_out, p_sems, first):
    if phase is not None:
        @pl.when(first)
        def _():
            phase.start(p_in, p_out, p_sems)


def _hosted_finish(phase, p_in, p_out, p_sems, last):
    if phase is not None:
        @pl.when(last)
        def _():
            phase.finish(p_in, p_out, p_sems)


def _run_phase(phase, name):
    n_in, n_out = len(phase.arrays), len(phase.out_shapes)

    def body(*refs):
        phase.start(refs[:n_in], refs[n_in:n_in + n_out], refs[n_in + n_out:])
        phase.finish(refs[:n_in], refs[n_in:n_in + n_out], refs[n_in + n_out:])

    return pl.pallas_call(
        body, name=name, in_specs=[_ANY] * n_in, out_specs=[_ANY] * n_out,
        out_shape=phase.out_shapes, input_output_aliases=phase.aliases, scratch_shapes=phase.scratch(),
        compiler_params=pltpu.CompilerParams(has_side_effects=True),
    )(*phase.arrays)


def _merge_phases(a, b):
    n_in, n_out = len(a.arrays), len(a.out_shapes)
    aliases = dict(a.aliases)
    aliases.update({n_in + i: n_out + o for i, o in b.aliases.items()})

    def build(ins, outs):
        sa, ra = a.build(ins[:n_in], outs[:n_out])
        sb, rb = b.build(ins[n_in:], outs[n_out:])
        return sa + sb, ra + rb

    return _Phase(a.arrays + b.arrays, a.out_shapes + b.out_shapes, aliases, a.n_send + b.n_send, build)


def _mesh_place():
    x, y, c = lax.axis_index("x"), lax.axis_index("y"), lax.axis_index("c")
    chips = [(x, y), (1 - x, y), (x, 1 - y), (1 - x, 1 - y)]
    num = lambda chip, core: 4 * chip[0] + 2 * chip[1] + core
    return c, chips, num


def _own_side_blocks():
    c, chips, num = _mesh_place()
    return jnp.stack([num(ch, c) for ch in chips]).astype(jnp.int32)


def _rows(ref, r, dev):
    return ref.at[pl.ds(pl.multiple_of(dev * r, 16), r), :]


def _place_own(loc, blocks, name):
    r, D = loc.shape
    tr = _pick(r, (400, 256, 200, 128, 64, 16))

    def body(idx_ref, l_ref, o_ref):
        del idx_ref
        o_ref[...] = l_ref[...]

    return pl.pallas_call(
        body, name=name,
        grid_spec=pltpu.PrefetchScalarGridSpec(
            num_scalar_prefetch=1, grid=(r // tr,),
            in_specs=[pl.BlockSpec((tr, D), lambda i, idx: (i, 0))],
            out_specs=pl.BlockSpec((tr, D), lambda i, idx: (idx[0] * (r // tr) + i, 0))),
        out_shape=jax.ShapeDtypeStruct((N_DEV * r, D), loc.dtype),
        compiler_params=_params(("arbitrary",)),
    )(blocks, loc)


def _gather_ici_phase(locs, fulls):
    rs = [a.shape[0] for a in locs]
    n = len(locs)

    def build(ins, outs):
        c, chips, num = _mesh_place()
        me = num(chips[0], c)
        targets = [((*chips[0], 1 - c), num(chips[0], 1 - c))] + [((*ch, c), num(ch, c)) for ch in chips[1:]]
        sends, recvs = [], []
        for dev, dnum in targets:
            for i, r in enumerate(rs):
                sends.append((ins[i], _rows(outs[i], r, me), dev))
                recvs.append(_rows(outs[i], r, dnum))
        return sends, recvs

    shapes = [jax.ShapeDtypeStruct(a.shape, a.dtype) for a in fulls]
    return _Phase(list(locs) + list(fulls), shapes, {n + i: i for i in range(n)}, 4 * n, build)


def _gather_d2d_phase(fulls, rs):
    def build(ins, outs):
        c, chips, num = _mesh_place()
        sib = (*chips[0], 1 - c)
        sends, recvs = [], []
        for ch in chips[1:]:
            for i, r in enumerate(rs):
                blk = _rows(outs[i], r, num(ch, c))
                sends.append((blk, blk, sib))
                recvs.append(_rows(outs[i], r, num(ch, 1 - c)))
        return sends, recvs

    shapes = [jax.ShapeDtypeStruct(a.shape, a.dtype) for a in fulls]
    return _Phase(fulls, shapes, {i: i for i in range(len(fulls))}, 3 * len(fulls), build)


def _reduce_d2d_phase(grads, rs):
    def build(ins, outs):
        c, chips, num = _mesh_place()
        sib = (*chips[0], 1 - c)
        sends, recvs = [], []
        for j, ch in enumerate(chips):
            for i, r in enumerate(rs):
                sends.append((_rows(ins[i], r, num(ch, 1 - c)), outs[i].at[j], sib))
                recvs.append(outs[i].at[j])
        return sends, recvs

    shapes = [jax.ShapeDtypeStruct((4, r, g.shape[1]), g.dtype) for g, r in zip(grads, rs)]
    return _Phase(grads, shapes, {}, 4 * len(grads), build)


def _reduce_ici_phase(parts):
    def build(ins, outs):
        c, chips, _ = _mesh_place()
        sends, recvs = [], []
        for t in range(1, 4):
            for i in range(len(parts)):
                sends.append((ins[i].at[t], outs[i].at[t - 1], (*chips[t], c)))
                recvs.append(outs[i].at[t - 1])
        return sends, recvs

    shapes = [jax.ShapeDtypeStruct((3,) + p.shape[1:], p.dtype) for p in parts]
    return _Phase(parts, shapes, {}, 3 * len(parts), build)


def _pair_sum(g, got, blocks, name):
    n, r, D = got.shape
    tr = _pick(r, (400, 256, 200, 128, 64, 16))

    def body(idx_ref, g_ref, r_ref, o_ref):
        del idx_ref
        o_ref[...] = (g_ref[...].astype(F32) + r_ref[...].astype(F32)).astype(o_ref.dtype)

    blk = pl.BlockSpec((None, tr, D), lambda j, i, idx: (j, i, 0))
    return pl.pallas_call(
        body, name=name,
        grid_spec=pltpu.PrefetchScalarGridSpec(
            num_scalar_prefetch=1, grid=(n, r // tr),
            in_specs=[pl.BlockSpec((tr, D), lambda j, i, idx: (idx[j] * (r // tr) + i, 0)), blk],
            out_specs=blk),
        out_shape=jax.ShapeDtypeStruct(got.shape, got.dtype),
        compiler_params=_params(("arbitrary", "arbitrary")),
    )(blocks, g, got)


def _chip_sum(p, r, name, layer, into=None):
    _, R, D = p.shape
    tr = _pick(R, (400, 256, 200, 128, 64, 16))

    def body(p_ref, r_ref, *rest):
        acc = p_ref[...].astype(F32)
        for t in range(3):
            acc = acc + r_ref[t].astype(F32)
        rest[-1][...] = acc

    args = [p, r] + ([] if into is None else [into])
    return pl.pallas_call(
        body, name=name, grid=(R // tr,),
        in_specs=[pl.BlockSpec((None, tr, D), lambda i: (0, i, 0)), pl.BlockSpec((3, tr, D), lambda i: (0, i, 0))]
        + ([] if into is None else [_ANY]),
        out_specs=pl.BlockSpec((None, tr, D), lambda i: (layer, i, 0)),
        out_shape=jax.ShapeDtypeStruct((DEPTH, R, D), F32),
        input_output_aliases={} if into is None else {2: 0},
        compiler_params=_params(("parallel",)))(*args)


def _allreduce_small(vec):
    R, C = vec.shape

    def body(v_ref, o_ref, buf, send_sems, recv_sems):
        me, peers = _me_and_peers()
        buf[me] = v_ref[...]
        sends = []
        for k, (pid, _) in enumerate(peers):
            cp = pltpu.make_async_remote_copy(src_ref=v_ref, dst_ref=buf.at[me], send_sem=send_sems.at[k],
                                              recv_sem=recv_sems.at[k], device_id=pid, device_id_type=MESH)
            cp.start()
            sends.append(cp)
        for k, (pid, pnum) in enumerate(peers):
            pltpu.make_async_remote_copy(src_ref=v_ref, dst_ref=buf.at[pnum], send_sem=send_sems.at[k],
                                         recv_sem=recv_sems.at[k], device_id=pid, device_id_type=MESH).wait_recv()
        for cp in sends:
            cp.wait_send()
        acc = buf[0]
        for d in range(1, N_DEV):
            acc = acc + buf[d]
        o_ref[...] = acc

    vm = pl.BlockSpec(memory_space=pltpu.VMEM)
    return pl.pallas_call(
        body, name="allreduce_small",
        in_specs=[vm], out_specs=vm,
        out_shape=jax.ShapeDtypeStruct((R, C), F32),
        scratch_shapes=[pltpu.VMEM((N_DEV, R, C), F32), pltpu.SemaphoreType.DMA((N_DEV - 1,)),
                        pltpu.SemaphoreType.DMA((N_DEV - 1,))],
        compiler_params=pltpu.CompilerParams(has_side_effects=True),
    )(vec)


def _adamw(w, g, m, v, name):
    R, C = w.shape
    tr = _pick(R, (512, 400, 256, 128, 64, 32, 16, 8)) if R >= 8 else R
    c1 = 1.0 - ADAM_B1 ** ADAM_STEP
    c2 = 1.0 - ADAM_B2 ** ADAM_STEP

    def body(w_ref, g_ref, m_ref, v_ref, d_ref, mo_ref, vo_ref):
        gg = g_ref[...]
        mn = ADAM_B1 * m_ref[...] + (1.0 - ADAM_B1) * gg
        vn = ADAM_B2 * v_ref[...] + (1.0 - ADAM_B2) * (gg * gg)
        d_ref[...] = -ADAM_LR * ((mn / c1) / (jnp.sqrt(vn / c2) + ADAM_EPS) + ADAM_WD * w_ref[...])
        mo_ref[...] = mn
        vo_ref[...] = vn

    blk = pl.BlockSpec((tr, C), lambda i: (i, 0))
    sh = jax.ShapeDtypeStruct((R, C), F32)
    return pl.pallas_call(
        body, name=name, grid=(R // tr,), in_specs=[blk] * 4, out_specs=[blk] * 3, out_shape=[sh] * 3,
        compiler_params=_params(("parallel",)),
    )(w, g, m, v)


def _lb_param_grad(lb_param, dlb):
    L, C = lb_param.shape

    def body(p_ref, d_ref, o_ref):
        lbp = p_ref[...]
        d = d_ref[...]
        mx = jnp.max(lbp, axis=0, keepdims=True)
        e = jnp.exp(lbp - mx)
        p = e / jnp.sum(e, axis=0, keepdims=True)
        tot = jnp.sum(d, axis=0, keepdims=True)
        dps = []
        rest = tot
        for j in range(L):
            dps.append(rest - tot if j == 0 else rest)
            rest = rest - d[j:j + 1]
        dp = jnp.concatenate(dps, axis=0)
        o_ref[...] = p * (dp - jnp.sum(p * dp, axis=0, keepdims=True))

    vm = pl.BlockSpec(memory_space=pltpu.VMEM)
    return pl.pallas_call(body, name="lb_param_grad", in_specs=[vm, vm], out_specs=vm,
                          out_shape=jax.ShapeDtypeStruct((L, C), F32))(lb_param, dlb)


def _pack_small(loss_part, dg_pre, dg_post, dlb, dg_head, dsinks):
    pad8 = lambda a: jnp.pad(a.reshape(-1, 128), ((0, 8 - DEPTH), (0, 0)))
    rows = [dg_pre.reshape(-1, 128), dg_post.reshape(-1, 128), dlb.reshape(-1, 128), pad8(dg_head), pad8(dsinks),
            loss_part]
    return jnp.concatenate(rows, axis=0)


def _unpack_small(vec):
    n = DEPTH * D_MODEL // 128
    o = 0
    dg_pre = vec[o:o + n].reshape(DEPTH, D_MODEL); o += n
    dg_post = vec[o:o + n].reshape(DEPTH, D_MODEL); o += n
    dlb = vec[o:o + n].reshape(DEPTH, HG_WIDTH); o += n
    dg_head = vec[o:o + DEPTH]; o += 8
    dsinks = vec[o:o + DEPTH, :ATT_HEADS]; o += 8
    loss = jnp.sum(vec[o:o + 8])
    return loss, dg_pre, dg_post, dlb, dg_head, dsinks


def kernel(x, w_in, w_out, g_pre, g_post, lb_param, g_head, sinks, loss_target, m_w_in, m_w_out, m_g_pre, m_g_post, m_lb_param, m_g_head, m_sinks, v_w_in, v_w_out, v_g_pre, v_g_post, v_lb_param, v_g_head, v_sinks):
    tr = lambda a: jnp.swapaxes(a, 1, 2)
    w_in_t = tr(w_in)
    (loss_part, dx, gw_in_t, gw_out, dg_pre, dg_post, dlb, dg_head, dsinks) = _step(
        x, loss_target, g_pre, g_post, lb_param, g_head, sinks, shards=(w_in_t.astype(BF16), w_out.astype(BF16)))

    small = _allreduce_small(_pack_small(loss_part, dg_pre, dg_post, dlb, dg_head, dsinks))
    loss, gg_pre, gg_post, gdlb, gg_head, gsinks = _unpack_small(small)
    glb = _lb_param_grad(lb_param, gdlb)

    grads = [gw_in_t, gw_out, gg_pre, gg_post, glb, gg_head, gsinks]
    ws = [w_in_t, w_out, g_pre, g_post, lb_param, g_head, sinks]
    ms = [tr(m_w_in), m_w_out, m_g_pre, m_g_post, m_lb_param, m_g_head, m_sinks]
    vs = [tr(v_w_in), v_w_out, v_g_pre, v_g_post, v_lb_param, v_g_head, v_sinks]
    names = ["w_in", "w_out", "g_pre", "g_post", "lb_param", "g_head", "sinks"]
    deltas, new_m, new_v = [], [], []
    for w, g, m, v, nm in zip(ws, grads, ms, vs, names):
        sh = w.shape
        two = lambda a: a.reshape(-1, sh[-1])
        d, mn, vn = _adamw(two(w), two(g), two(m), two(v), "adamw_" + nm)
        deltas.append(d.reshape(sh))
        new_m.append(mn.reshape(sh))
        new_v.append(vn.reshape(sh))
    grads[0], deltas[0], new_m[0], new_v[0] = tr(grads[0]), tr(deltas[0]), tr(new_m[0]), tr(new_v[0])
    return (loss, dx, *grads, *deltas, *new_m, *new_v)
```

```python
import math

import numpy as np
import jax
import jax.numpy as jnp
from jax import lax
from jax.experimental import pallas as pl
from jax.experimental.pallas import tpu as pltpu

F32 = jnp.float32
BF16 = jnp.bfloat16

D_MODEL = 1024
DEPTH = 2
HG_HEADS = 8
HG_DIM = 128
HG_WIDTH = HG_HEADS * HG_DIM
CHUNK = 64
ATT_HEADS = 16
ATT_DIM = 64
ATT_WIDTH = ATT_HEADS * ATT_DIM
KV_WIDTH = 128
ATT_BLOCK = 128
ATT_SCALE = 1.0 / math.sqrt(ATT_DIM)
ROPE_THETA = 10000.0
NORM_EPS = 1e-6
NEG_INF = -1e30
LB_FLOOR = 1e-20
N_H = 4 * HG_WIDTH
N_A = 2 * ATT_WIDTH + 2 * KV_WIDTH
IN_WIDTH = N_H + N_A
MIX_WIDTH = HG_WIDTH + ATT_WIDTH

ADAM_LR = 0.001
ADAM_B1 = 0.9
ADAM_B2 = 0.999
ADAM_EPS = 1e-08
ADAM_WD = 0.01
ADAM_STEP = 10

N_DEV = 8
MESH = pl.DeviceIdType.MESH
VMEM_LIMIT = 56 * 1024 * 1024

NN = ((1,), (0,))
NT = ((1,), (1,))
TN = ((0,), (0,))


def _dot(a, b, dims):
    return lax.dot_general(a.astype(BF16), b.astype(BF16), (dims, ((), ())), preferred_element_type=F32)


def _params(sem=None, **kw):
    return pltpu.CompilerParams(dimension_semantics=sem, vmem_limit_bytes=VMEM_LIMIT, **kw)


def _sigmoids(x):
    e = jnp.exp(-jnp.abs(x))
    r = 1.0 / (1.0 + e)
    er = e * r
    pos = x >= 0.0
    return jnp.where(pos, r, er), jnp.where(pos, er, r)


def _silu(x):
    return x * _sigmoids(x)[0]


def _silu_and_grad(x):
    s, ns = _sigmoids(x)
    return x * s, s * (1.0 + x * ns)


def _pick(n, prefs):
    for p in prefs:
        if n % p == 0:
            return p
    return n


def _inproj(x2, g, w, name):
    T, D = x2.shape
    tm = _pick(T, (512, 256, 128))
    nchunk = 1024

    def body(x_ref, g_ref, w_ref, oh_ref, oa_ref, h_ref):
        x = x_ref[...]
        r = lax.rsqrt(jnp.mean(x * x, axis=-1, keepdims=True) + NORM_EPS)
        h = ((x * r) * g_ref[...]).astype(BF16)
        h_ref[...] = h
        for j in range(0, N_H, nchunk):
            oh_ref[:, j:j + nchunk] = lax.dot_general(h, w_ref[j:j + nchunk, :], (NT, ((), ())),
                                                      preferred_element_type=F32)
        for j in range(0, N_A, N_A // 2):
            oa_ref[:, j:j + N_A // 2] = lax.dot_general(h, w_ref[N_H + j:N_H + j + N_A // 2, :], (NT, ((), ())),
                                                        preferred_element_type=F32)

    row = lambda w_: pl.BlockSpec((tm, w_), lambda i: (i, 0))
    return pl.pallas_call(
        body, name=name,
        grid=(T // tm,),
        in_specs=[row(D), pl.BlockSpec((1, D), lambda i: (0, 0)),
                  pl.BlockSpec((IN_WIDTH, D), lambda i: (0, 0), pipeline_mode=pl.Buffered(1))],
        out_specs=[row(N_H), row(N_A), row(D)],
        out_shape=[jax.ShapeDtypeStruct((T, N_H), F32), jax.ShapeDtypeStruct((T, N_A), F32),
                   jax.ShapeDtypeStruct((T, D), BF16)],
        compiler_params=_params(("parallel",)),
    )(x2, g, w)


def _mm_tn(pieces, b, name, out_dtype=BF16):
    T, m = b.shape
    tn = 256
    counts = [p.shape[1] // tn for p in pieces]
    starts = [sum(counts[:i]) for i in range(len(pieces))]
    n_p = len(pieces)

    def body(*refs):
        b_ref, o_ref = refs[n_p], refs[n_p + 1]
        i = pl.program_id(0)
        for p in range(n_p):
            @pl.when((i >= starts[p]) & (i < starts[p] + counts[p]))
            def _(p=p):
                o_ref[...] = lax.dot_general(refs[p][...], b_ref[...], (TN, ((), ())),
                                             preferred_element_type=F32).astype(out_dtype)

    piece_spec = lambda s, c: pl.BlockSpec((T, tn), lambda i: (0, jnp.clip(i - s, 0, c - 1)))
    return pl.pallas_call(
        body, name=name,
        grid=(sum(counts),),
        in_specs=[piece_spec(s, c) for s, c in zip(starts, counts)]
        + [pl.BlockSpec((T, m), lambda i: (0, 0), pipeline_mode=pl.Buffered(1))],
        out_specs=pl.BlockSpec((tn, m), lambda i: (i, 0)),
        out_shape=jax.ShapeDtypeStruct((sum(counts) * tn, m), out_dtype),
        compiler_params=_params(("arbitrary",)),
    )(*pieces, b)


_LEVELS = (0, 1, 2, 4, 8, 16, 32)
_CUM_L = (2, 4, 8, 16, 32, 64)
_ALL_KINDS = tuple(("c", L) for L in _CUM_L) + tuple(("r", L) for L in _CUM_L)
_MXU_KINDS = (("c", 2), ("c", 4), ("c", CHUNK), ("r", 2), ("r", 4))
N_CUM = len(_ALL_KINDS) * CHUNK
N_CUM_F = len(_MXU_KINDS) * CHUNK


def _cum_matrices():
    t = np.arange(CHUNK)[:, None]
    r = np.arange(CHUNK)[None, :]

    def mat(kind):
        c, L = kind
        return ((r // L == t // L) & ((r <= t) if c == "c" else (r > t))).astype(np.float32)

    fwd = np.concatenate([mat(kd) for kd in _MXU_KINDS], axis=0)
    full = np.concatenate([mat(kd) for kd in _ALL_KINDS], axis=0)
    return jnp.asarray(fwd, BF16), jnp.asarray(full.T.copy(), BF16)


def _level_masks():
    t = np.arange(CHUNK)[:, None]
    s = np.arange(CHUNK)[None, :]
    ms = []
    for L in _LEVELS:
        if L == 0:
            ms.append(t == s)
        else:
            ms.append((t // (2 * L) == s // (2 * L)) & ((t // L) % 2 == 1) & ((s // L) % 2 == 0))
    return jnp.asarray(np.stack(ms).astype(np.float32))


def _split3(x):
    hi = x.astype(BF16)
    r1 = x - hi.astype(F32)
    mid = r1.astype(BF16)
    lo = (r1 - mid.astype(F32)).astype(BF16)
    return hi, mid, lo


def _cum3(ts, x, terms=3):
    d = lambda p: lax.dot_general(ts, p, (NN, ((), ())), preferred_element_type=F32)
    return sum(d(p) for p in _split3(x)[:terms])


def _lb_terms(lbp, layer):
    mx = jnp.max(lbp, axis=0, keepdims=True)
    e = jnp.exp(lbp - mx)
    p = e / jnp.sum(e, axis=0, keepdims=True)
    cum = p[0:1]
    for j in range(1, layer + 1):
        cum = cum + p[j:j + 1]
    lb = cum - p[0:1]
    lbf = jnp.maximum(lb, LB_FLOOR)
    return dict(lbf=lbf, one_m=1.0 - lb, kcorr=lb - lbf, ind=jnp.where(lb > LB_FLOOR, 1.0, 0.0))


def _gate(x, lt):
    sig, nsig = _sigmoids(x)
    f = lt["lbf"] + lt["one_m"] * sig
    return jnp.log(f), lt["one_m"] * nsig + lt["kcorr"], f, sig, nsig


def _ck(x, ci):
    return x[ci * CHUNK:(ci + 1) * CHUNK]


def _block_cums(ts, g, nc):
    cs = [_cum3(ts, _ck(g, ci), terms=2) for ci in range(nc)]
    out = {kind: jnp.concatenate([c[CHUNK * i:CHUNK * (i + 1)] for c in cs], axis=0)
           for i, kind in enumerate(_MXU_KINDS)}
    b = out[("c", CHUNK)]
    ng = CHUNK // 8
    last = b.reshape(nc, ng, 8, HG_DIM)[:, :, 7:8, :]
    zero = jnp.zeros((nc, 1, 1, HG_DIM), F32)

    def spread(groups):
        return jnp.broadcast_to(jnp.concatenate(groups, axis=1), (nc, ng, 8, HG_DIM)).reshape(nc * CHUNK, HG_DIM)

    def get(kind):
        if kind in out:
            return out[kind]
        c, L = kind
        nb = L // 8
        first = lambda r: (r // nb) * nb
        if c == "c":
            return b - spread([last[:, first(r) - 1:first(r)] if r >= nb else zero for r in range(ng)])
        return spread([last[:, first(r) + nb - 1:first(r) + nb] for r in range(ng)]) - b

    return get


def _level_factors(cums, g, L):
    if L == 0:
        return None, None
    if L == 1:
        return jnp.exp(g), None
    return jnp.exp(cums(("c", L))), jnp.exp(cums(("r", L)))


def _mul(a, e):
    return a if e is None else a * e


def _hg_block_fwd(qf, k, v, g, ts, m_ref, nc):
    cums = _block_cums(ts, g, nc)
    amat = [jnp.zeros((CHUNK, CHUNK), F32)] * nc
    for li, L in enumerate(_LEVELS):
        eq, ek = _level_factors(cums, g, L)
        ql, kl, m = _mul(qf, eq), _mul(k, ek), m_ref[li]
        amat = [amat[ci] + _dot(_ck(ql, ci), _ck(kl, ci), NT) * m for ci in range(nc)]
    b = cums(("c", CHUNK))
    kst = k * jnp.exp(cums(("r", CHUNK)))
    o = [_dot(amat[ci], _ck(v, ci), NN) for ci in range(nc)]
    kv = [_dot(_ck(v, ci), _ck(kst, ci), TN) for ci in range(nc)]
    dec = [jnp.exp(b[(ci + 1) * CHUNK - 1:(ci + 1) * CHUNK, :]) for ci in range(nc)]
    return o, dec, kv, qf * jnp.exp(b), amat


def _hg_block_bwd(qf, k, v, g, do, amat, ts, m_ref, nc):
    cums = _block_cums(ts, g, nc)
    dcs = {}
    da = [_dot(_ck(do, ci), _ck(v, ci), NT) for ci in range(nc)]
    dq = jnp.zeros_like(qf)
    dk = jnp.zeros_like(qf)
    dg = jnp.zeros_like(qf)
    for li, L in enumerate(_LEVELS):
        eq, ek = _level_factors(cums, g, L)
        ql, kl, m = _mul(qf, eq), _mul(k, ek), m_ref[li]
        qlb, klb = ql.astype(BF16), kl.astype(BF16)
        dal = [(da[ci] * m).astype(BF16) for ci in range(nc)]
        dql = jnp.concatenate([_dot(dal[ci], _ck(klb, ci), NN) for ci in range(nc)], axis=0)
        dkl = jnp.concatenate([_dot(dal[ci], _ck(qlb, ci), TN) for ci in range(nc)], axis=0)
        dq = dq + _mul(dql, eq)
        dk = dk + _mul(dkl, ek)
        if L == 1:
            dg = dg + dql * ql
        elif L > 1:
            dcs[("c", L)] = (dql * ql).astype(BF16)
            dcs[("r", L)] = (dkl * kl).astype(BF16)
    b = cums(("c", CHUNK))
    e64 = jnp.exp(b)
    er64 = jnp.exp(cums(("r", CHUNK)))
    qb = qf * e64
    return dict(dq=dq, dk=dk, dg=dg, dcs=dcs, e64=e64, er64=er64, qb=qb, kst=k * er64,
                dv=[_dot(amat[ci], _ck(do, ci), TN) for ci in range(nc)],
                dec=[jnp.exp(b[(ci + 1) * CHUNK - 1:(ci + 1) * CHUNK, :]) for ci in range(nc)],
                qd=[_dot(_ck(do, ci), _ck(qb, ci), TN) for ci in range(nc)])


def _hg_state_bwd(w, v, do, starts, ends, tst, nc):
    dqb = jnp.concatenate([_dot(_ck(do, ci), starts[ci], NN) for ci in range(nc)], axis=0)
    dkst = jnp.concatenate([_dot(_ck(v, ci), ends[ci], NN) for ci in range(nc)], axis=0)
    dq = w["dq"] + dqb * w["e64"]
    dk = w["dk"] + dkst * w["er64"]
    dv = jnp.concatenate([w["dv"][ci] + _dot(_ck(w["kst"], ci), ends[ci], NT) for ci in range(nc)], axis=0)
    trow = lax.broadcasted_iota(jnp.int32, (CHUNK, 1), 0)
    dtot = jnp.concatenate(
        [jnp.where(trow == CHUNK - 1, jnp.sum(ends[ci] * starts[ci], axis=0, keepdims=True) * w["dec"][ci], 0.0)
         for ci in range(nc)], axis=0)
    dcs = dict(w["dcs"])
    dcs[("c", CHUNK)] = (dqb * w["qb"] + dtot).astype(BF16)
    dcs[("r", CHUNK)] = (dkst * w["kst"]).astype(BF16)
    dgs = [_dot(tst, jnp.concatenate([_ck(dcs[kind], ci) for kind in _ALL_KINDS], axis=0), NN) for ci in range(nc)]
    return dq, dk, dv, w["dg"] + jnp.concatenate(dgs, axis=0)


def _hgrn_fwd(proj_h, u_rows, lb_param, g_head, layer, name, phase=None):
    B, S, _ = proj_h.shape
    sb = _pick(S, (1024, 512, 256, 128, 64))
    nc = sb // CHUNK
    ts, _ = _cum_matrices()

    def body(*refs):
        ins, outs, (st,), p_in, p_out, p_sems = _split_refs(refs, 8, 4, 1, phase)
        q_ref, f_ref, i_ref, z_ref, lbp_ref, gh_ref, ts_ref, m_ref = ins
        o_ref, u_ref, sts_ref, am_ref = outs
        h_id, b_id, s_id = pl.program_id(0), pl.program_id(1), pl.program_id(2)
        _hosted_start(phase, p_in, p_out, p_sems, (h_id == 0) & (b_id == 0) & (s_id == 0))

        @pl.when(s_id == 0)
        def _():
            st[...] = jnp.zeros_like(st)

        lt = _lb_terms(lbp_ref[...], layer)
        tsv = ts_ref[...]
        gh = gh_ref[...]
        logf, k = _gate(f_ref[...], lt)[:2]
        o_part, dec, kv, qb, amat = _hg_block_fwd(_silu(q_ref[...]), k, i_ref[...], logf, tsv, m_ref, nc)
        for ci in range(nc):
            am_ref[ci] = amat[ci].astype(BF16)
        cur = st[...]
        starts = []
        for ci in range(nc):
            sts_ref[ci] = cur
            starts.append(cur)
            cur = cur * dec[ci] + kv[ci]
        st[...] = cur
        o = jnp.concatenate([o_part[ci] + _dot(_ck(qb, ci), starts[ci], NT) for ci in range(nc)], axis=0)
        o_ref[...] = o
        r = lax.rsqrt(jnp.mean(o * o, axis=-1, keepdims=True) + NORM_EPS)
        u_ref[...] = (((o * r) * gh) * _silu(z_ref[...])).astype(BF16)
        _hosted_finish(phase, p_in, p_out, p_sems, (h_id == HG_HEADS - 1) & (b_id == B - 1) & (s_id == S // sb - 1))

    col = lambda base: pl.BlockSpec((None, sb, HG_DIM), lambda h, b, s: (b, s, base + h))
    p_ispecs, p_ospecs, p_oshapes, p_alias, p_scratch, p_args = _host_phase(phase, 8, 4)
    res = pl.pallas_call(
        body, name=name,
        grid=(HG_HEADS, B, S // sb),
        in_specs=[col(0), col(HG_HEADS), col(2 * HG_HEADS), col(3 * HG_HEADS),
                  pl.BlockSpec((DEPTH, HG_DIM), lambda h, b, s: (0, h)),
                  pl.BlockSpec((1, HG_DIM), lambda h, b, s: (0, 0)),
                  pl.BlockSpec((N_CUM_F, CHUNK), lambda h, b, s: (0, 0)),
                  pl.BlockSpec((len(_LEVELS), CHUNK, CHUNK), lambda h, b, s: (0, 0, 0))] + p_ispecs,
        out_specs=[col(0), col(0),
                   pl.BlockSpec((None, None, nc, HG_DIM, HG_DIM), lambda h, b, s: (b, h, s, 0, 0)),
                   pl.BlockSpec((None, None, nc, CHUNK, CHUNK), lambda h, b, s: (b, h, s, 0, 0))] + p_ospecs,
        out_shape=[jax.ShapeDtypeStruct((B, S, HG_WIDTH), F32),
                   jax.ShapeDtypeStruct((B, S, u_rows), BF16),
                   jax.ShapeDtypeStruct((B, HG_HEADS, S // CHUNK, HG_DIM, HG_DIM), F32),
                   jax.ShapeDtypeStruct((B, HG_HEADS, S // CHUNK, CHUNK, CHUNK), BF16)] + p_oshapes,
        input_output_aliases=p_alias,
        scratch_shapes=[pltpu.VMEM((HG_DIM, HG_DIM), F32)] + p_scratch,
        compiler_params=_params(("arbitrary", "arbitrary", "arbitrary")),
    )(proj_h, proj_h, proj_h, proj_h, lb_param, g_head, ts, _level_masks(), *p_args)
    return res[0], res[1], (res[2], res[3]), list(res[4:])


def _hgrn_bwd(proj_h, o_h, du, kept, lb_param, g_head, layer, name, phase=None):
    B, S, _ = proj_h.shape
    sb = _pick(S, (512, 256, 128, 64))
    nc = sb // CHUNK
    ns = S // sb
    ts, tst = _cum_matrices()

    def body(*refs):
        ins, outs, (dst,), p_in, p_out, p_sems = _split_refs(refs, 13, 6, 1, phase)
        q_ref, f_ref, i_ref, z_ref, o_ref, du_ref, sts_ref, am_ref, lbp_ref, gh_ref, ts_ref, tst_ref, m_ref = ins
        dq_ref, df_ref, di_ref, dz_ref, dlb_ref, dgh_ref = outs
        h_id, b_id, s_id = pl.program_id(0), pl.program_id(1), pl.program_id(2)
        _hosted_start(phase, p_in, p_out, p_sems, (h_id == 0) & (b_id == 0) & (s_id == 0))

        @pl.when(s_id == 0)
        def _():
            dst[...] = jnp.zeros_like(dst)

        @pl.when((b_id == 0) & (s_id == 0))
        def _():
            dlb_ref[...] = jnp.zeros_like(dlb_ref)

        @pl.when((h_id == 0) & (b_id == 0) & (s_id == 0))
        def _():
            dgh_ref[...] = jnp.zeros_like(dgh_ref)

        lt = _lb_terms(lbp_ref[...], layer)
        gh = gh_ref[...]
        tsv = ts_ref[...]
        tstv = tst_ref[...]
        logf, k, f, sig, nsig = _gate(f_ref[...], lt)
        o = o_ref[...]
        dub = du_ref[...]
        r = lax.rsqrt(jnp.mean(o * o, axis=-1, keepdims=True) + NORM_EPS)
        n = o * r
        sg, sg_grad = _silu_and_grad(z_ref[...])
        dz_ref[...] = (dub * (n * gh) * sg_grad).astype(BF16)
        dgh_ref[...] += jnp.sum(dub * sg * n, axis=0, keepdims=True)
        dn = dub * sg * gh
        do = r * (dn - n * jnp.mean(dn * n, axis=-1, keepdims=True))
        v = i_ref[...]
        qf, qf_grad = _silu_and_grad(q_ref[...])
        w = _hg_block_bwd(qf, k, v, logf, do, [am_ref[ci] for ci in range(nc)], tsv, m_ref, nc)
        cur = dst[...]
        ends = [None] * nc
        for ci in reversed(range(nc)):
            ends[ci] = cur
            cur = cur * w["dec"][ci] + w["qd"][ci]
        dst[...] = cur
        dq, dk, dv, dg = _hg_state_bwd(w, v, do, [sts_ref[ci] for ci in range(nc)], ends, tstv, nc)
        di_ref[...] = dv.astype(BF16)
        dq_ref[...] = (dq * qf_grad).astype(BF16)
        scaled = (dg - f * dk) / f
        df_ref[...] = (scaled * lt["one_m"] * sig * nsig).astype(BF16)
        dlb_ref[...] += jnp.sum(scaled * (lt["ind"] - sig), axis=0, keepdims=True)
        _hosted_finish(phase, p_in, p_out, p_sems, (h_id == HG_HEADS - 1) & (b_id == B - 1) & (s_id == ns - 1))

    col = lambda base: pl.BlockSpec((None, sb, HG_DIM), lambda h, b, s: (b, ns - 1 - s, base + h))
    out_col = pl.BlockSpec((None, sb, HG_DIM), lambda h, b, s: (b, ns - 1 - s, h))
    dt = jax.ShapeDtypeStruct((B, S, HG_WIDTH), BF16)
    p_ispecs, p_ospecs, p_oshapes, p_alias, p_scratch, p_args = _host_phase(phase, 13, 6)
    res = pl.pallas_call(
        body, name=name,
        grid=(HG_HEADS, B, ns),
        in_specs=[col(0), col(HG_HEADS), col(2 * HG_HEADS), col(3 * HG_HEADS), col(0), col(0),
                  pl.BlockSpec((None, None, nc, HG_DIM, HG_DIM), lambda h, b, s: (b, h, ns - 1 - s, 0, 0)),
                  pl.BlockSpec((None, None, nc, CHUNK, CHUNK), lambda h, b, s: (b, h, ns - 1 - s, 0, 0)),
                  pl.BlockSpec((DEPTH, HG_DIM), lambda h, b, s: (0, h)),
                  pl.BlockSpec((1, HG_DIM), lambda h, b, s: (0, 0)),
                  pl.BlockSpec((N_CUM_F, CHUNK), lambda h, b, s: (0, 0)),
                  pl.BlockSpec((CHUNK, N_CUM), lambda h, b, s: (0, 0)),
                  pl.BlockSpec((len(_LEVELS), CHUNK, CHUNK), lambda h, b, s: (0, 0, 0))] + p_ispecs,
        out_specs=[out_col, out_col, out_col, out_col,
                   pl.BlockSpec((1, HG_DIM), lambda h, b, s: (0, h)),
                   pl.BlockSpec((1, HG_DIM), lambda h, b, s: (0, 0))] + p_ospecs,
        out_shape=[dt, dt, dt, dt, jax.ShapeDtypeStruct((1, HG_WIDTH), F32),
                   jax.ShapeDtypeStruct((1, HG_DIM), F32)] + p_oshapes,
        input_output_aliases=p_alias,
        scratch_shapes=[pltpu.VMEM((HG_DIM, HG_DIM), F32)] + p_scratch,
        compiler_params=_params(("arbitrary", "arbitrary", "arbitrary")),
    )(proj_h, proj_h, proj_h, proj_h, o_h, du, kept[0], kept[1], lb_param, g_head, ts, tst, _level_masks(), *p_args)
    return tuple(res[:6]) + (list(res[6:]),)


def _rope_tables(S):
    half = ATT_DIM // 2
    inv_freq = ROPE_THETA ** (-jnp.arange(half, dtype=F32) / half)
    ang = jnp.arange(S).astype(F32)[:, None] * inv_freq[None, :]
    cos = jnp.cos(ang)
    sin = jnp.sin(ang)
    cos = jnp.concatenate([cos, cos, cos, cos], axis=1)
    sin = jnp.concatenate([-sin, sin, -sin, sin], axis=1)
    return cos, sin


def _attn_common():
    lane = lax.broadcasted_iota(jnp.int32, (1, 2 * ATT_DIM), 1)
    first_half = (lane % ATT_DIM) < (ATT_DIM // 2)
    left = lane < ATT_DIM

    def swap(x):
        return jnp.where(first_half, pltpu.roll(x, 128 - ATT_DIM // 2, 1), pltpu.roll(x, ATT_DIM // 2, 1))

    def rope(x, cos, sin):
        return x * cos + swap(x) * sin

    def rope_bwd(dy, cos, sin):
        return dy * cos + swap(dy * sin)

    def dup(x):
        xs = pltpu.roll(x, ATT_DIM, 1)
        return [jnp.where(left, x, xs), jnp.where(left, xs, x)]

    return left, rope, rope_bwd, dup


GROUP = ATT_HEADS // 2
GROUP_ROWS = GROUP * ATT_BLOCK


def _attn_bias(i):
    r = lax.broadcasted_iota(jnp.int32, (ATT_BLOCK, 2 * ATT_BLOCK), 0)
    c = lax.broadcasted_iota(jnp.int32, (ATT_BLOCK, 2 * ATT_BLOCK), 1)
    ok = (c > r) & (c <= r + ATT_BLOCK) & ((c >= ATT_BLOCK) | (i > 0))
    return jnp.where(ok, 0.0, NEG_INF)


def _stack_heads(pairs, left):
    rows = []
    for x in pairs:
        rows += [jnp.where(left, x, 0.0), jnp.where(left, 0.0, x)]
    return jnp.concatenate(rows, axis=0)


def _unstack_heads(y, left, pp):
    r0 = 2 * pp * ATT_BLOCK
    return jnp.where(left, y[r0:r0 + ATT_BLOCK], y[r0 + ATT_BLOCK:r0 + 2 * ATT_BLOCK])


def _row_sums(x):
    return _dot(x, jnp.ones((x.shape[1], 128), BF16), NN)


def _attn_probs(qs, kd, vd, sink, bias):
    n = range(len(qs))
    rows = qs[0].shape[0]
    s = [(_dot(qs[j], kd[j], NT).reshape(rows // ATT_BLOCK, ATT_BLOCK, 2 * ATT_BLOCK) * ATT_SCALE + bias[None])
         .reshape(rows, 2 * ATT_BLOCK) for j in n]
    m = [jnp.max(jnp.maximum(jnp.maximum(s[j][:, :128], s[j][:, 128:]), sink[j]), axis=-1, keepdims=True) for j in n]
    pu = [jnp.exp(s[j] - m[j]) for j in n]
    es = [jnp.exp(sink[j] - m[j]) for j in n]
    ones = jnp.ones((2 * ATT_BLOCK, 128), BF16)
    ov = [_dot(pu[j], jnp.concatenate([vd[j].astype(BF16), ones], axis=1), NN) for j in n]
    inv = [1.0 / (ov[j][:, 128:] + es[j]) for j in n]
    return ([pu[j] * jnp.concatenate([inv[j], inv[j]], axis=1) for j in n], [es[j] * inv[j] for j in n],
            [ov[j][:, :128] * inv[j] for j in n])


def _sink_rows(sinks_l):
    return jnp.broadcast_to(jnp.repeat(sinks_l, ATT_BLOCK)[:, None], (ATT_HEADS * ATT_BLOCK, 128))


_Z0 = (2 * ATT_WIDTH + 2 * KV_WIDTH - ATT_WIDTH) // 256


def _attn_fwd(proj_a, u, sinks_l, cos, sin, name, phase=None):
    B, S, _ = proj_a.shape
    nb = S // ATT_BLOCK

    def body(*refs):
        ins, (u_ref, p_ref, o_ref, ps_ref), _, p_in, p_out, p_sems = _split_refs(refs, 13, 4, 0, phase)
        q_ref, kvc_ref, kvp_ref, z0, z1, z2, z3, cos_ref, sin_ref, cosp_ref, sinp_ref, sinks_ref, _ = ins
        i = pl.program_id(1)
        _hosted_start(phase, p_in, p_out, p_sems, (pl.program_id(0) == 0) & (i == 0))
        left, rope, _, dup = _attn_common()
        cos_c, sin_c = cos_ref[...], sin_ref[...]
        kvc = kvc_ref[...]
        kvp = kvp_ref[...]
        kw = jnp.concatenate([rope(kvp[:, :KV_WIDTH], cosp_ref[...], sinp_ref[...]),
                              rope(kvc[:, :KV_WIDTH], cos_c, sin_c)], axis=0)
        vw = jnp.concatenate([kvp[:, KV_WIDTH:], kvc[:, KV_WIDTH:]], axis=0)
        kd, vd = dup(kw), dup(vw)
        bias = _attn_bias(i)
        zs = (z0, z1, z2, z3)
        pairs = [range(4 * kvh, 4 * kvh + 4) for kvh in range(2)]
        qs = [_stack_heads([rope(q_ref[:, 128 * pr:128 * (pr + 1)], cos_c, sin_c) for pr in pairs[kvh]], left)
              for kvh in range(2)]
        sink = [sinks_ref[kvh * GROUP_ROWS:(kvh + 1) * GROUP_ROWS, :] for kvh in range(2)]
        p, ps, o = _attn_probs(qs, kd, vd, sink, bias)
        eye = (lax.broadcasted_iota(jnp.int32, (ATT_BLOCK, 128), 0)
               == lax.broadcasted_iota(jnp.int32, (ATT_BLOCK, 128), 1))
        for kvh in range(2):
            p_ref[kvh] = p[kvh].astype(BF16)
            for g in range(GROUP):
                blk = ps[kvh][g * ATT_BLOCK:(g + 1) * ATT_BLOCK, :]
                ps_ref[kvh * GROUP + g:kvh * GROUP + g + 1, :] = jnp.sum(jnp.where(eye, blk, 0.0), axis=0, keepdims=True)
            for pp, pr in enumerate(pairs[kvh]):
                z = zs[pr // 2][:, 128 * (pr % 2):128 * (pr % 2 + 1)]
                o128 = _unstack_heads(o[kvh], left, pp)
                o_ref[:, 128 * pr:128 * (pr + 1)] = o128.astype(BF16)
                u_ref[:, 128 * pr:128 * (pr + 1)] = (o128 * _silu(z)).astype(BF16)
        _hosted_finish(phase, p_in, p_out, p_sems, (pl.program_id(0) == B - 1) & (i == nb - 1))

    rowblk = lambda w, cb: pl.BlockSpec((None, ATT_BLOCK, w), lambda b, i: (b, i, cb))
    tab = pl.BlockSpec((ATT_BLOCK, 128), lambda b, i: (i, 0))
    tabp = pl.BlockSpec((ATT_BLOCK, 128), lambda b, i: (jnp.maximum(i - 1, 0), 0))
    p_ispecs, p_ospecs, p_oshapes, p_alias, p_scratch, p_args = _host_phase(phase, 13, 4)
    res = pl.pallas_call(
        body, name=name,
        grid=(B, nb),
        in_specs=[rowblk(ATT_WIDTH, 0), rowblk(256, 4),
                  pl.BlockSpec((None, ATT_BLOCK, 256), lambda b, i: (b, jnp.maximum(i - 1, 0), 4)),
                  rowblk(256, _Z0), rowblk(256, _Z0 + 1), rowblk(256, _Z0 + 2), rowblk(256, _Z0 + 3),
                  tab, tab, tabp, tabp,
                  pl.BlockSpec((ATT_HEADS * ATT_BLOCK, 128), lambda b, i: (0, 0)),
                  pl.BlockSpec(memory_space=pl.ANY)] + p_ispecs,
        out_specs=[pl.BlockSpec((None, ATT_BLOCK, ATT_WIDTH), lambda b, i: (b, i, 1)),
                   pl.BlockSpec((None, None, 2, GROUP_ROWS, 2 * ATT_BLOCK), lambda b, i: (b, i, 0, 0, 0)),
                   pl.BlockSpec((None, ATT_BLOCK, ATT_WIDTH), lambda b, i: (b, i, 0)),
                   pl.BlockSpec((None, None, ATT_HEADS, 128), lambda b, i: (b, i, 0, 0))] + p_ospecs,
        out_shape=[jax.ShapeDtypeStruct(u.shape, BF16),
                   jax.ShapeDtypeStruct((B, nb, 2, GROUP_ROWS, 2 * ATT_BLOCK), BF16),
                   jax.ShapeDtypeStruct((B, S, ATT_WIDTH), BF16),
                   jax.ShapeDtypeStruct((B, nb, ATT_HEADS, 128), F32)] + p_oshapes,
        input_output_aliases={12: 0, **p_alias},
        scratch_shapes=p_scratch,
        compiler_params=_params(("arbitrary", "arbitrary")),
    )(proj_a, proj_a, proj_a, proj_a, proj_a, proj_a, proj_a, cos, sin, cos, sin, sinks_l, u, *p_args)
    return res[0], tuple(res[1:4]), list(res[4:])


def _attn_bwd(proj_a, du, kept, cos, sin, name, phase=None):
    B, S, _ = proj_a.shape
    nb = S // ATT_BLOCK
    p_kept, o_kept, ps_kept = kept

    def body(*refs):
        ins, outs, (carry, sk_acc), p_in, p_out, p_sems = _split_refs(refs, 15, 4, 2, phase)
        (q_ref, kvc_ref, kvp_ref, z0, z1, z2, z3, du_ref, cos_ref, sin_ref, cosp_ref, sinp_ref,
         p_ref, o_ref, ps_ref) = ins
        dq_ref, dkv_ref, dz_ref, dsk_ref = outs
        b_id, i = pl.program_id(0), pl.program_id(1)
        _hosted_start(phase, p_in, p_out, p_sems, (b_id == 0) & (i == 0))

        @pl.when((b_id == 0) & (i == 0))
        def _():
            sk_acc[...] = jnp.zeros_like(sk_acc)

        @pl.when(i == 0)
        def _():
            carry[...] = jnp.zeros_like(carry)

        @pl.when(i < nb)
        def _():
            left, rope, rope_bwd, dup = _attn_common()
            cos_c, sin_c = cos_ref[...], sin_ref[...]
            cos_p, sin_p = cosp_ref[...], sinp_ref[...]
            kvc = kvc_ref[...]
            kvp = kvp_ref[...]
            kw = jnp.concatenate([rope(kvp[:, :KV_WIDTH], cos_p, sin_p), rope(kvc[:, :KV_WIDTH], cos_c, sin_c)], axis=0)
            vw = jnp.concatenate([kvp[:, KV_WIDTH:], kvc[:, KV_WIDTH:]], axis=0)
            kd, vd = dup(kw), dup(vw)
            zs = (z0, z1, z2, z3)
            units = [(kvh, hf) for kvh in range(2) for hf in range(2)]
            half = GROUP_ROWS // 2
            pairs = [range(4 * kvh + 2 * hf, 4 * kvh + 2 * hf + 2) for kvh, hf in units]
            ku = [kd[kvh] for kvh, _ in units]
            vu = [vd[kvh] for kvh, _ in units]
            ps_all = ps_ref[...]
            head_row = lax.broadcasted_iota(jnp.int32, (ATT_HEADS, 128), 0)
            eye = (lax.broadcasted_iota(jnp.int32, (ATT_BLOCK, 128), 0)
                   == lax.broadcasted_iota(jnp.int32, (ATT_BLOCK, 128), 1))

            def first(j):
                kvh, hf = units[j]
                p = p_ref[kvh, hf * half:(hf + 1) * half, :]
                parts = []
                for pr in pairs[j]:
                    cols = slice(128 * pr, 128 * (pr + 1))
                    sg, sg_grad = _silu_and_grad(zs[pr // 2][:, 128 * (pr % 2):128 * (pr % 2 + 1)])
                    du128 = du_ref[:, cols]
                    dz_ref[:, cols] = (du128 * o_ref[:, cols].astype(F32) * sg_grad).astype(BF16)
                    parts.append(du128 * sg)
                dos = _stack_heads(parts, left)
                dp = _dot(dos, vu[j], NT)
                delta = _row_sums(p.astype(F32) * dp)
                ds = (p.astype(F32) * (dp - jnp.concatenate([delta, delta], axis=1)) * ATT_SCALE).astype(BF16)
                sk = jnp.zeros((ATT_HEADS, 128), F32)
                for hh in range(4):
                    hd = kvh * GROUP + 4 * hf + hh
                    drow = jnp.sum(jnp.where(eye, delta[hh * ATT_BLOCK:(hh + 1) * ATT_BLOCK, :], 0.0), axis=0,
                                   keepdims=True)
                    sk = sk - jnp.where(head_row == hd, ps_all * drow, 0.0)
                sk_acc[...] += sk
                qs = _stack_heads([rope(q_ref[:, 128 * pr:128 * (pr + 1)], cos_c, sin_c) for pr in pairs[j]], left)
                return ds, p, dos.astype(BF16), qs.astype(BF16)

            def second(j, ds, p, dos, qs):
                dqs = _dot(ds, ku[j], NN)
                for pp, pr in enumerate(pairs[j]):
                    dq_ref[:, 128 * pr:128 * (pr + 1)] = rope_bwd(_unstack_heads(dqs, left, pp),
                                                                  cos_c, sin_c).astype(BF16)
                return _dot(ds, qs, TN), _dot(p, dos, TN)

            got, dku, dvu = {}, [None] * len(units), [None] * len(units)
            for j in range(len(units) + 1):
                if j < len(units):
                    got[j] = first(j)
                if j >= 1:
                    dku[j - 1], dvu[j - 1] = second(j - 1, *got.pop(j - 1))
            dkd = [dku[0] + dku[1], dku[2] + dku[3]]
            dvd = [dvu[0] + dvu[1], dvu[2] + dvu[3]]
            fold = lambda pr: jnp.where(left, pr[0] + pltpu.roll(pr[0], ATT_DIM, 1), pr[1] + pltpu.roll(pr[1], ATT_DIM, 1))
            dkw = fold(dkd)
            dvw = fold(dvd)
            prev = jnp.concatenate([rope_bwd(dkw[:ATT_BLOCK], cos_p, sin_p), dvw[:ATT_BLOCK]], axis=1)
            cur = jnp.concatenate([rope_bwd(dkw[ATT_BLOCK:], cos_c, sin_c), dvw[ATT_BLOCK:]], axis=1)
            dkv_ref[...] = (carry[...] + prev).astype(BF16)
            carry[...] = cur

        @pl.when(i == nb)
        def _():
            dkv_ref[...] = carry[...].astype(BF16)

        @pl.when((b_id == B - 1) & (i == nb))
        def _():
            diag = (lax.broadcasted_iota(jnp.int32, (ATT_HEADS, 128), 0)
                    == lax.broadcasted_iota(jnp.int32, (ATT_HEADS, 128), 1))
            tot = jnp.sum(sk_acc[...], axis=1, keepdims=True)
            dsk_ref[...] = jnp.sum(jnp.where(diag, tot, 0.0), axis=0, keepdims=True)

        _hosted_finish(phase, p_in, p_out, p_sems, (b_id == B - 1) & (i == nb))

    cl = lambda i: jnp.minimum(i, nb - 1)
    pv = lambda i: jnp.maximum(jnp.minimum(i, nb - 1) - 1, 0)
    rowblk = lambda w, cb: pl.BlockSpec((None, ATT_BLOCK, w), lambda b, i: (b, cl(i), cb))
    tab = pl.BlockSpec((ATT_BLOCK, 128), lambda b, i: (cl(i), 0))
    tabp = pl.BlockSpec((ATT_BLOCK, 128), lambda b, i: (pv(i), 0))
    p_ispecs, p_ospecs, p_oshapes, p_alias, p_scratch, p_args = _host_phase(phase, 15, 4)
    res = pl.pallas_call(
        body, name=name,
        grid=(B, nb + 1),
        in_specs=[rowblk(ATT_WIDTH, 0), rowblk(256, 4),
                  pl.BlockSpec((None, ATT_BLOCK, 256), lambda b, i: (b, pv(i), 4)),
                  rowblk(256, _Z0), rowblk(256, _Z0 + 1), rowblk(256, _Z0 + 2), rowblk(256, _Z0 + 3),
                  rowblk(ATT_WIDTH, 1),
                  tab, tab, tabp, tabp,
                  pl.BlockSpec((None, None, 2, GROUP_ROWS, 2 * ATT_BLOCK), lambda b, i: (b, cl(i), 0, 0, 0)),
                  rowblk(ATT_WIDTH, 0),
                  pl.BlockSpec((None, None, ATT_HEADS, 128), lambda b, i: (b, cl(i), 0, 0))] + p_ispecs,
        out_specs=[rowblk(ATT_WIDTH, 0),
                   pl.BlockSpec((None, ATT_BLOCK, 256), lambda b, i: (b, jnp.maximum(i - 1, 0), 0)),
                   rowblk(ATT_WIDTH, 0),
                   pl.BlockSpec((1, 128), lambda b, i: (0, 0))] + p_ospecs,
        out_shape=[jax.ShapeDtypeStruct((B, S, ATT_WIDTH), BF16), jax.ShapeDtypeStruct((B, S, 256), BF16),
                   jax.ShapeDtypeStruct((B, S, ATT_WIDTH), BF16), jax.ShapeDtypeStruct((1, 128), F32)] + p_oshapes,
        input_output_aliases=p_alias,
        scratch_shapes=[pltpu.VMEM((ATT_BLOCK, 256), F32), pltpu.VMEM((ATT_HEADS, 128), F32)] + p_scratch,
        compiler_params=_params(("arbitrary", "arbitrary")),
    )(proj_a, proj_a, proj_a, proj_a, proj_a, proj_a, proj_a, du, cos, sin, cos, sin, p_kept, o_kept, ps_kept, *p_args)
    return tuple(res[:4]) + (list(res[4:]),)


def _outproj_fwd(u2, w_out, x2, g_post, target2, name):
    T, D = x2.shape
    tm = _pick(T, (512, 256, 128))
    last = target2 is not None

    def body(u_ref, w_ref, x_ref, g_ref, *rest):
        y = lax.dot_general(u_ref[...], w_ref[...], (NN, ((), ())), preferred_element_type=F32)
        r = lax.rsqrt(jnp.mean(y * y, axis=-1, keepdims=True) + NORM_EPS)
        xn = x_ref[...] + (y * r) * g_ref[...]
        if last:
            t_ref, y_ref, dx_ref, loss_ref = rest
            err = xn - t_ref[...]
            dx_ref[...] = err * (1.0 / D)
            sq = err * err
            acc = sq[:, 0:128]
            for kk in range(1, D // 128):
                acc = acc + sq[:, 128 * kk:128 * (kk + 1)]
            part = jnp.sum(acc.reshape(tm // 8, 8, 128), axis=0) * (0.5 / D)

            @pl.when(pl.program_id(0) == 0)
            def _():
                loss_ref[...] = jnp.zeros_like(loss_ref)

            loss_ref[...] += part
        else:
            y_ref, xn_ref = rest
            xn_ref[...] = xn
        y_ref[...] = y

    row = pl.BlockSpec((tm, D), lambda i: (i, 0))
    in_specs = [pl.BlockSpec((tm, MIX_WIDTH), lambda i: (i, 0)),
                pl.BlockSpec((MIX_WIDTH, D), lambda i: (0, 0)), row,
                pl.BlockSpec((1, D), lambda i: (0, 0))]
    args = [u2, w_out, x2, g_post]
    out_specs = [row, row]
    out_shape = [jax.ShapeDtypeStruct((T, D), F32), jax.ShapeDtypeStruct((T, D), F32)]
    if last:
        in_specs.append(row)
        args.append(target2)
        out_specs.append(pl.BlockSpec((8, 128), lambda i: (0, 0)))
        out_shape.append(jax.ShapeDtypeStruct((8, 128), F32))
    return pl.pallas_call(
        body, name=name, grid=(T // tm,), in_specs=in_specs, out_specs=out_specs, out_shape=out_shape,
        compiler_params=_params(("arbitrary",)),
    )(*args)


def _outproj_bwd(dxn2, y2, g_post, w_out, name):
    T, D = y2.shape
    N = w_out.shape[0]
    tm = _pick(T, (512, 256, 128))
    nt = T // tm

    def body(dx_ref, y_ref, g_ref, w_ref, dy_ref, dg_ref, du_ref, acc):
        i = pl.program_id(0)

        @pl.when(i == 0)
        def _():
            acc[...] = jnp.zeros_like(acc)

        y = y_ref[...]
        dxn = dx_ref[...]
        r = lax.rsqrt(jnp.mean(y * y, axis=-1, keepdims=True) + NORM_EPS)
        n = y * r
        dn = dxn * g_ref[...]
        dy = (r * (dn - n * jnp.mean(dn * n, axis=-1, keepdims=True))).astype(BF16)
        dy_ref[...] = dy
        du_ref[...] = lax.dot_general(dy, w_ref[...], (NT, ((), ())), preferred_element_type=F32)
        acc[...] += jnp.sum((dxn * n).reshape(tm // 8, 8, D), axis=0)

        @pl.when(i == nt - 1)
        def _():
            dg_ref[...] = jnp.sum(acc[...], axis=0, keepdims=True)

    row = pl.BlockSpec((tm, D), lambda i: (i, 0))
    vec = pl.BlockSpec((1, D), lambda i: (0, 0))
    return pl.pallas_call(
        body, name=name, grid=(nt,),
        in_specs=[row, row, vec, pl.BlockSpec((N, D), lambda i: (0, 0), pipeline_mode=pl.Buffered(1))],
        out_specs=[row, vec, pl.BlockSpec((tm, N), lambda i: (i, 0))],
        out_shape=[jax.ShapeDtypeStruct((T, D), BF16), jax.ShapeDtypeStruct((1, D), F32),
                   jax.ShapeDtypeStruct((T, N), F32)],
        scratch_shapes=[pltpu.VMEM((8, D), F32)],
        compiler_params=_params(("arbitrary",)),
    )(dxn2, y2, g_post, w_out)


def _inproj_bwd(pieces, w_t, x2, dxn2, g_pre, name, phase=None):
    T, D = x2.shape
    widths = [p.shape[1] for p in pieces]
    offs = [sum(widths[:i]) for i in range(len(pieces))]
    n_p = len(pieces)
    tm = _pick(T, (256, 128))
    nt = T // tm

    def body(*refs):
        ins, (dx_ref, dg_ref), (acc,), p_in, p_out, p_sems = _split_refs(refs, n_p + 4, 2, 1, phase)
        w_ref, x_ref, dxn_ref, g_ref = ins[n_p:]
        i = pl.program_id(0)
        _hosted_start(phase, p_in, p_out, p_sems, i == 0)

        @pl.when(i == 0)
        def _():
            acc[...] = jnp.zeros_like(acc)

        dh = jnp.zeros((tm, D), F32)
        for p in range(n_p):
            dh = dh + lax.dot_general(ins[p][...], w_ref[offs[p]:offs[p] + widths[p], :], (NN, ((), ())),
                                      preferred_element_type=F32)
        x = x_ref[...]
        r = lax.rsqrt(jnp.mean(x * x, axis=-1, keepdims=True) + NORM_EPS)
        n = x * r
        dn = dh * g_ref[...]
        dx_ref[...] = dxn_ref[...] + r * (dn - n * jnp.mean(dn * n, axis=-1, keepdims=True))
        acc[...] += jnp.sum((dh * n).reshape(tm // 8, 8, D), axis=0)

        @pl.when(i == nt - 1)
        def _():
            dg_ref[...] = jnp.sum(acc[...], axis=0, keepdims=True)

        _hosted_finish(phase, p_in, p_out, p_sems, i == nt - 1)

    row = pl.BlockSpec((tm, D), lambda i: (i, 0))
    vec = pl.BlockSpec((1, D), lambda i: (0, 0))
    p_ispecs, p_ospecs, p_oshapes, p_alias, p_scratch, p_args = _host_phase(phase, n_p + 4, 2)
    res = pl.pallas_call(
        body, name=name, grid=(nt,),
        in_specs=[pl.BlockSpec((tm, w), lambda i: (i, 0)) for w in widths]
        + [pl.BlockSpec((sum(widths), D), lambda i: (0, 0), pipeline_mode=pl.Buffered(1)), row, row, vec] + p_ispecs,
        out_specs=[row, vec] + p_ospecs,
        out_shape=[jax.ShapeDtypeStruct((T, D), F32), jax.ShapeDtypeStruct((1, D), F32)] + p_oshapes,
        input_output_aliases=p_alias,
        scratch_shapes=[pltpu.VMEM((8, D), F32)] + p_scratch,
        compiler_params=_params(("arbitrary",)),
    )(*pieces, w_t, x2, dxn2, g_pre, *p_args)
    return res[0], res[1], list(res[2:])


def _step(x, target, g_pre, g_post, lb_param, g_head, sinks, shards=None, full=None):
    B, S, D = x.shape
    T = B * S
    dist = shards is not None
    first, last = 0, DEPTH - 1
    if dist:
        a_loc, b_loc = shards
        ra, rb = a_loc.shape[1], b_loc.shape[1]
        side = _own_side_blocks()
        placed = lambda loc, nm: _place_own(loc, side, "place_" + nm)
        w_in0 = _run_phase(_gather_ici_phase([a_loc[0]], [placed(a_loc[0], "in0")]), "gather_in0_ici")
        w_in0 = _run_phase(_gather_d2d_phase(w_in0, [ra]), "gather_in0_d2d")[0]
        w_in, w_out = [w_in0, None], [None, None]
    else:
        w_in, w_out = list(full[0]), list(full[1])
    cos, sin = _rope_tables(S)
    saved = []
    xs = x
    loss_part = None
    dxn = None
    for l in range(DEPTH):
        x2 = xs.reshape(T, D)
        proj_h, proj_a, h = _inproj(x2, g_pre[l:l + 1], w_in[l], f"inproj{l}")
        proj_h = proj_h.reshape(B, S, N_H)
        proj_a = proj_a.reshape(B, S, N_A)
        phase = None
        if dist and l == first:
            phase = _gather_ici_phase([a_loc[1], b_loc[0]], [placed(a_loc[1], "in1"), placed(b_loc[0], "out0")])
        if dist and l == last:
            phase = _gather_d2d_phase([w_out1_part], [rb])
        o_h, u, states, got = _hgrn_fwd(proj_h, MIX_WIDTH, lb_param, g_head[l:l + 1], l, f"hgrn_fwd{l}", phase)
        phase = None
        if dist and l == first:
            phase = _merge_phases(_gather_d2d_phase(got, [ra, rb]),
                                  _gather_ici_phase([b_loc[1]], [placed(b_loc[1], "out1")]))
        if dist and l == last:
            w_out[1] = got[0]
        u, kept_a, got = _attn_fwd(proj_a, u, _sink_rows(sinks[l]), cos, sin, f"attn_fwd{l}", phase)
        if dist and l == first:
            w_in[1], w_out[0], w_out1_part = got
        u2 = u.reshape(T, MIX_WIDTH)
        if l < last:
            y, xn = _outproj_fwd(u2, w_out[l], x2, g_post[l:l + 1], None, f"outproj{l}")
            xn = xn.reshape(B, S, D)
        else:
            y, dxn, loss_part = _outproj_fwd(u2, w_out[l], x2, g_post[l:l + 1], target.reshape(T, D), f"outproj{l}")
            xn = None
        saved.append((x2, h, proj_h, proj_a, o_h, u2, states, kept_a, y))
        xs = xn

    dw_in, dw_out = [None] * DEPTH, [None] * DEPTH
    dg_pre, dg_post, dlb, dg_head, dsinks = [], [], [], [], []
    for l in reversed(range(DEPTH)):
        x2, h, proj_h, proj_a, o_h, u2, states, kept_a, y = saved[l]
        dy, dgp, du = _outproj_bwd(dxn, y, g_post[l:l + 1], w_out[l], f"outproj_bwd{l}")
        du = du.reshape(B, S, MIX_WIDTH)
        dw_out[l] = _mm_tn([u2], dy, f"wgrad_out{l}")
        phase = None
        if dist:
            phase = _reduce_d2d_phase([dw_out[l]], [rb])
            if l == first:
                phase = _merge_phases(_reduce_ici_phase([part_in1]), phase)
        dqh, dfh, dih, dzh, dlb_l, dgh, got = _hgrn_bwd(
            proj_h, o_h, du, states, lb_param, g_head[l:l + 1], l, f"hgrn_bwd{l}", phase)
        if dist:
            if l == first:
                sum_in = _chip_sum(part_in1, got[0], "chip_sum_in1", 1)
            part_out = _pair_sum(dw_out[l], got[-1], side, f"pair_sum_out{l}")
        dqa, dkv, dza, dsk, got = _attn_bwd(proj_a, du, kept_a, cos, sin, f"attn_bwd{l}",
                                            _reduce_ici_phase([part_out]) if dist else None)
        if dist:
            sum_out = _chip_sum(part_out, got[0], f"chip_sum_out{l}", l, None if l == last else sum_out)
        dproj = [p.reshape(T, p.shape[-1]) for p in (dqh, dfh, dih, dzh, dqa, dkv, dza)]
        dw_in[l] = _mm_tn(dproj, h, f"wgrad_in{l}")
        phase = None
        if dist and l == last:
            phase = _reduce_d2d_phase([dw_in[l]], [ra])
        if dist and l == first:
            got = _run_phase(_reduce_d2d_phase([dw_in[l]], [ra]), "reduce_in0_d2d")
            part_in0 = _pair_sum(dw_in[l], got[0], side, "pair_sum_in0")
            phase = _reduce_ici_phase([part_in0])
        dxn, dgpre, got = _inproj_bwd(dproj, w_in[l], x2, dxn, g_pre[l:l + 1], f"inproj_bwd{l}", phase)
        if dist and l == last:
            part_in1 = _pair_sum(dw_in[l], got[0], side, "pair_sum_in1")
        if dist and l == first:
            sum_in = _chip_sum(part_in0, got[0], "chip_sum_in0", 0, sum_in)
        dg_pre.append(dgpre)
        dg_post.append(dgp)
        dlb.append(dlb_l)
        dg_head.append(dgh)
        dsinks.append(dsk)
    rev = lambda lst: jnp.concatenate(lst[::-1], axis=0)
    if not dist:
        sum_in, sum_out = jnp.stack(dw_in), jnp.stack(dw_out)
    return (loss_part, dxn.reshape(B, S, D), sum_in, sum_out,
            rev(dg_pre), rev(dg_post), rev(dlb), rev(dg_head), rev(dsinks))


def _me_and_peers():
    x, y, c = lax.axis_index("x"), lax.axis_index("y"), lax.axis_index("c")
    me = 4 * x + 2 * y + c
    peers = []
    for k in range(1, N_DEV):
        px = 1 - x if k & 4 else x
        py = 1 - y if k & 2 else y
        pc = 1 - c if k & 1 else c
        peers.append(((px, py, pc), 4 * px + 2 * py + pc))
    return me, peers


class _Phase:
    def __init__(self, arrays, out_shapes, aliases, n_send, build):
        self.arrays, self.out_shapes, self.aliases = list(arrays), list(out_shapes), dict(aliases)
        self.n_send, self.build = n_send, build

    def scratch(self):
        return [pltpu.SemaphoreType.DMA((self.n_send,)), pltpu.SemaphoreType.DMA((self.n_send,))]

    def _copies(self, in_refs, out_refs, sems, arrivals):
        send_sems, recv_sems = sems
        sends, recvs = self.build(in_refs, out_refs)
        assert len(sends) == self.n_send == len(recvs)
        out = [pltpu.make_async_remote_copy(src_ref=s, dst_ref=d, send_sem=send_sems.at[i], recv_sem=recv_sems.at[i],
                                            device_id=dev, device_id_type=MESH) for i, (s, d, dev) in enumerate(sends)]
        inc = [pltpu.make_async_remote_copy(src_ref=s, dst_ref=r, send_sem=send_sems.at[i], recv_sem=recv_sems.at[i],
                                            device_id=dev, device_id_type=MESH)
               for i, ((s, _, dev), r) in enumerate(zip(sends, recvs))] if arrivals else []
        return out, inc

    def start(self, in_refs, out_refs, sems):
        out, _ = self._copies(in_refs, out_refs, sems, False)
        for cp in out:
            cp.start()

    def finish(self, in_refs, out_refs, sems):
        out, inc = self._copies(in_refs, out_refs, sems, True)
        for cp in inc:
            cp.wait_recv()
        for cp in out:
            cp.wait_send()


_ANY = pl.BlockSpec(memory_space=pl.ANY)


def _host_phase(phase, n_in, n_out):
    if phase is None:
        return [], [], [], {}, [], []
    aliases = {n_in + i: n_out + o for i, o in phase.aliases.items()}
    return ([_ANY] * len(phase.arrays), [_ANY] * len(phase.out_shapes), phase.out_shapes, aliases, phase.scratch(),
            phase.arrays)


def _split_refs(refs, n_in, n_out, n_scr, phase):
    pi = len(phase.arrays) if phase else 0
    po = len(phase.out_shapes) if phase else 0
    a = n_in + pi
    b = a + n_out + po
    return (refs[:n_in], refs[a:a + n_out], refs[b:b + n_scr], refs[n_in:a], refs[a + n_out:b], refs[b + n_scr:])


def _hosted_start(phase, p_in, p_out, p_sems, first):
    if phase is not None:
        @pl.when(first)
        def _():
            phase.start(p_in, p_out, p_sems)


def _hosted_finish(phase, p_in, p_out, p_sems, last):
    if phase is not None:
        @pl.when(last)
        def _():
            phase.finish(p_in, p_out, p_sems)


def _run_phase(phase, name):
    n_in, n_out = len(phase.arrays), len(phase.out_shapes)

    def body(*refs):
        phase.start(refs[:n_in], refs[n_in:n_in + n_out], refs[n_in + n_out:])
        phase.finish(refs[:n_in], refs[n_in:n_in + n_out], refs[n_in + n_out:])

    return pl.pallas_call(
        body, name=name, in_specs=[_ANY] * n_in, out_specs=[_ANY] * n_out,
        out_shape=phase.out_shapes, input_output_aliases=phase.aliases, scratch_shapes=phase.scratch(),
        compiler_params=pltpu.CompilerParams(has_side_effects=True),
    )(*phase.arrays)


def _merge_phases(a, b):
    n_in, n_out = len(a.arrays), len(a.out_shapes)
    aliases = dict(a.aliases)
    aliases.update({n_in + i: n_out + o for i, o in b.aliases.items()})

    def build(ins, outs):
        sa, ra = a.build(ins[:n_in], outs[:n_out])
        sb, rb = b.build(ins[n_in:], outs[n_out:])
        return sa + sb, ra + rb

    return _Phase(a.arrays + b.arrays, a.out_shapes + b.out_shapes, aliases, a.n_send + b.n_send, build)


def _mesh_place():
    x, y, c = lax.axis_index("x"), lax.axis_index("y"), lax.axis_index("c")
    chips = [(x, y), (1 - x, y), (x, 1 - y), (1 - x, 1 - y)]
    num = lambda chip, core: 4 * chip[0] + 2 * chip[1] + core
    return c, chips, num


def _own_side_blocks():
    c, chips, num = _mesh_place()
    return jnp.stack([num(ch, c) for ch in chips]).astype(jnp.int32)


def _rows(ref, r, dev):
    return ref.at[pl.ds(pl.multiple_of(dev * r, 16), r), :]


def _place_own(loc, blocks, name):
    r, D = loc.shape
    tr = _pick(r, (400, 256, 200, 128, 64, 16))

    def body(idx_ref, l_ref, o_ref):
        del idx_ref
        o_ref[...] = l_ref[...]

    return pl.pallas_call(
        body, name=name,
        grid_spec=pltpu.PrefetchScalarGridSpec(
            num_scalar_prefetch=1, grid=(r // tr,),
            in_specs=[pl.BlockSpec((tr, D), lambda i, idx: (i, 0))],
            out_specs=pl.BlockSpec((tr, D), lambda i, idx: (idx[0] * (r // tr) + i, 0))),
        out_shape=jax.ShapeDtypeStruct((N_DEV * r, D), loc.dtype),
        compiler_params=_params(("arbitrary",)),
    )(blocks, loc)


def _gather_ici_phase(locs, fulls):
    rs = [a.shape[0] for a in locs]
    n = len(locs)

    def build(ins, outs):
        c, chips, num = _mesh_place()
        me = num(chips[0], c)
        targets = [((*chips[0], 1 - c), num(chips[0], 1 - c))] + [((*ch, c), num(ch, c)) for ch in chips[1:]]
        sends, recvs = [], []
        for dev, dnum in targets:
            for i, r in enumerate(rs):
                sends.append((ins[i], _rows(outs[i], r, me), dev))
                recvs.append(_rows(outs[i], r, dnum))
        return sends, recvs

    shapes = [jax.ShapeDtypeStruct(a.shape, a.dtype) for a in fulls]
    return _Phase(list(locs) + list(fulls), shapes, {n + i: i for i in range(n)}, 4 * n, build)


def _gather_d2d_phase(fulls, rs):
    def build(ins, outs):
        c, chips, num = _mesh_place()
        sib = (*chips[0], 1 - c)
        sends, recvs = [], []
        for ch in chips[1:]:
            for i, r in enumerate(rs):
                blk = _rows(outs[i], r, num(ch, c))
                sends.append((blk, blk, sib))
                recvs.append(_rows(outs[i], r, num(ch, 1 - c)))
        return sends, recvs

    shapes = [jax.ShapeDtypeStruct(a.shape, a.dtype) for a in fulls]
    return _Phase(fulls, shapes, {i: i for i in range(len(fulls))}, 3 * len(fulls), build)


def _reduce_d2d_phase(grads, rs):
    def build(ins, outs):
        c, chips, num = _mesh_place()
        sib = (*chips[0], 1 - c)
        sends, recvs = [], []
        for j, ch in enumerate(chips):
            for i, r in enumerate(rs):
                sends.append((_rows(ins[i], r, num(ch, 1 - c)), outs[i].at[j], sib))
                recvs.append(outs[i].at[j])
        return sends, recvs

    shapes = [jax.ShapeDtypeStruct((4, r, g.shape[1]), g.dtype) for g, r in zip(grads, rs)]
    return _Phase(grads, shapes, {}, 4 * len(grads), build)


def _reduce_ici_phase(parts):
    def build(ins, outs):
        c, chips, _ = _mesh_place()
        sends, recvs = [], []
        for t in range(1, 4):
            for i in range(len(parts)):
                sends.append((ins[i].at[t], outs[i].at[t - 1], (*chips[t], c)))
                recvs.append(outs[i].at[t - 1])
        return sends, recvs

    shapes = [jax.ShapeDtypeStruct((3,) + p.shape[1:], p.dtype) for p in parts]
    return _Phase(parts, shapes, {}, 3 * len(parts), build)


def _pair_sum(g, got, blocks, name):
    n, r, D = got.shape
    tr = _pick(r, (400, 256, 200, 128, 64, 16))

    def body(idx_ref, g_ref, r_ref, o_ref):
        del idx_ref
        o_ref[...] = (g_ref[...].astype(F32) + r_ref[...].astype(F32)).astype(o_ref.dtype)

    blk = pl.BlockSpec((None, tr, D), lambda j, i, idx: (j, i, 0))
    return pl.pallas_call(
        body, name=name,
        grid_spec=pltpu.PrefetchScalarGridSpec(
            num_scalar_prefetch=1, grid=(n, r // tr),
            in_specs=[pl.BlockSpec((tr, D), lambda j, i, idx: (idx[j] * (r // tr) + i, 0)), blk],
            out_specs=blk),
        out_shape=jax.ShapeDtypeStruct(got.shape, got.dtype),
        compiler_params=_params(("arbitrary", "arbitrary")),
    )(blocks, g, got)


def _chip_sum(p, r, name, layer, into=None):
    _, R, D = p.shape
    tr = _pick(R, (400, 256, 200, 128, 64, 16))

    def body(p_ref, r_ref, *rest):
        acc = p_ref[...].astype(F32)
        for t in range(3):
            acc = acc + r_ref[t].astype(F32)
        rest[-1][...] = acc

    args = [p, r] + ([] if into is None else [into])
    return pl.pallas_call(
        body, name=name, grid=(R // tr,),
        in_specs=[pl.BlockSpec((None, tr, D), lambda i: (0, i, 0)), pl.BlockSpec((3, tr, D), lambda i: (0, i, 0))]
        + ([] if into is None else [_ANY]),
        out_specs=pl.BlockSpec((None, tr, D), lambda i: (layer, i, 0)),
        out_shape=jax.ShapeDtypeStruct((DEPTH, R, D), F32),
        input_output_aliases={} if into is None else {2: 0},
        compiler_params=_params(("parallel",)))(*args)


def _allreduce_small(vec):
    R, C = vec.shape

    def body(v_ref, o_ref, buf, send_sems, recv_sems):
        me, peers = _me_and_peers()
        buf[me] = v_ref[...]
        sends = []
        for k, (pid, _) in enumerate(peers):
            cp = pltpu.make_async_remote_copy(src_ref=v_ref, dst_ref=buf.at[me], send_sem=send_sems.at[k],
                                              recv_sem=recv_sems.at[k], device_id=pid, device_id_type=MESH)
            cp.start()
            sends.append(cp)
        for k, (pid, pnum) in enumerate(peers):
            pltpu.make_async_remote_copy(src_ref=v_ref, dst_ref=buf.at[pnum], send_sem=send_sems.at[k],
                                         recv_sem=recv_sems.at[k], device_id=pid, device_id_type=MESH).wait_recv()
        for cp in sends:
            cp.wait_send()
        acc = buf[0]
        for d in range(1, N_DEV):
            acc = acc + buf[d]
        o_ref[...] = acc

    vm = pl.BlockSpec(memory_space=pltpu.VMEM)
    return pl.pallas_call(
        body, name="allreduce_small",
        in_specs=[vm], out_specs=vm,
        out_shape=jax.ShapeDtypeStruct((R, C), F32),
        scratch_shapes=[pltpu.VMEM((N_DEV, R, C), F32), pltpu.SemaphoreType.DMA((N_DEV - 1,)),
                        pltpu.SemaphoreType.DMA((N_DEV - 1,))],
        compiler_params=pltpu.CompilerParams(has_side_effects=True),
    )(vec)


def _adamw(w, g, m, v, name):
    R, C = w.shape
    tr = _pick(R, (512, 400, 256, 128, 64, 32, 16, 8)) if R >= 8 else R
    c1 = 1.0 - ADAM_B1 ** ADAM_STEP
    c2 = 1.0 - ADAM_B2 ** ADAM_STEP

    def body(w_ref, g_ref, m_ref, v_ref, d_ref, mo_ref, vo_ref):
        gg = g_ref[...]
        mn = ADAM_B1 * m_ref[...] + (1.0 - ADAM_B1) * gg
        vn = ADAM_B2 * v_ref[...] + (1.0 - ADAM_B2) * (gg * gg)
        d_ref[...] = -ADAM_LR * ((mn / c1) / (jnp.sqrt(vn / c2) + ADAM_EPS) + ADAM_WD * w_ref[...])
        mo_ref[...] = mn
        vo_ref[...] = vn

    blk = pl.BlockSpec((tr, C), lambda i: (i, 0))
    sh = jax.ShapeDtypeStruct((R, C), F32)
    return pl.pallas_call(
        body, name=name, grid=(R // tr,), in_specs=[blk] * 4, out_specs=[blk] * 3, out_shape=[sh] * 3,
        compiler_params=_params(("parallel",)),
    )(w, g, m, v)


def _lb_param_grad(lb_param, dlb):
    L, C = lb_param.shape

    def body(p_ref, d_ref, o_ref):
        lbp = p_ref[...]
        d = d_ref[...]
        mx = jnp.max(lbp, axis=0, keepdims=True)
        e = jnp.exp(lbp - mx)
        p = e / jnp.sum(e, axis=0, keepdims=True)
        tot = jnp.sum(d, axis=0, keepdims=True)
        dps = []
        rest = tot
        for j in range(L):
            dps.append(rest - tot if j == 0 else rest)
            rest = rest - d[j:j + 1]
        dp = jnp.concatenate(dps, axis=0)
        o_ref[...] = p * (dp - jnp.sum(p * dp, axis=0, keepdims=True))

    vm = pl.BlockSpec(memory_space=pltpu.VMEM)
    return pl.pallas_call(body, name="lb_param_grad", in_specs=[vm, vm], out_specs=vm,
                          out_shape=jax.ShapeDtypeStruct((L, C), F32))(lb_param, dlb)


def _pack_small(loss_part, dg_pre, dg_post, dlb, dg_head, dsinks):
    pad8 = lambda a: jnp.pad(a.reshape(-1, 128), ((0, 8 - DEPTH), (0, 0)))
    rows = [dg_pre.reshape(-1, 128), dg_post.reshape(-1, 128), dlb.reshape(-1, 128), pad8(dg_head), pad8(dsinks),
            loss_part]
    return jnp.concatenate(rows, axis=0)


def _unpack_small(vec):
    n = DEPTH * D_MODEL // 128
    o = 0
    dg_pre = vec[o:o + n].reshape(DEPTH, D_MODEL); o += n
    dg_post = vec[o:o + n].reshape(DEPTH, D_MODEL); o += n
    dlb = vec[o:o + n].reshape(DEPTH, HG_WIDTH); o += n
    dg_head = vec[o:o + DEPTH]; o += 8
    dsinks = vec[o:o + DEPTH, :ATT_HEADS]; o += 8
    loss = jnp.sum(vec[o:o + 8])
    return loss, dg_pre, dg_post, dlb, dg_head, dsinks


def kernel(x, w_in, w_out, g_pre, g_post, lb_param, g_head, sinks, loss_target, m_w_in, m_w_out, m_g_pre, m_g_post, m_lb_param, m_g_head, m_sinks, v_w_in, v_w_out, v_g_pre, v_g_post, v_lb_param, v_g_head, v_sinks):
    tr = lambda a: jnp.swapaxes(a, 1, 2)
    w_in_t = tr(w_in)
    (loss_part, dx, gw_in_t, gw_out, dg_pre, dg_post, dlb, dg_head, dsinks) = _step(
        x, loss_target, g_pre, g_post, lb_param, g_head, sinks, shards=(w_in_t.astype(BF16), w_out.astype(BF16)))

    small = _allreduce_small(_pack_small(loss_part, dg_pre, dg_post, dlb, dg_head, dsinks))
    loss, gg_pre, gg_post, gdlb, gg_head, gsinks = _unpack_small(small)
    glb = _lb_param_grad(lb_param, gdlb)

    grads = [gw_in_t, gw_out, gg_pre, gg_post, glb, gg_head, gsinks]
    ws = [w_in_t, w_out, g_pre, g_post, lb_param, g_head, sinks]
    ms = [tr(m_w_in), m_w_out, m_g_pre, m_g_post, m_lb_param, m_g_head, m_sinks]
    vs = [tr(v_w_in), v_w_out, v_g_pre, v_g_post, v_lb_param, v_g_head, v_sinks]
    names = ["w_in", "w_out", "g_pre", "g_post", "lb_param", "g_head", "sinks"]
    deltas, new_m, new_v = [], [], []
    for w, g, m, v, nm in zip(ws, grads, ms, vs, names):
        sh = w.shape
        two = lambda a: a.reshape(-1, sh[-1])
        d, mn, vn = _adamw(two(w), two(g), two(m), two(v), "adamw_" + nm)
        deltas.append(d.reshape(sh))
        new_m.append(mn.reshape(sh))
        new_v.append(vn.reshape(sh))
    grads[0], deltas[0], new_m[0], new_v[0] = tr(grads[0]), tr(deltas[0]), tr(new_m[0]), tr(new_v[0])
    return (loss, dx, *grads, *deltas, *new_m, *new_v)
```

```python
import math

import numpy as np
import jax
import jax.numpy as jnp
from jax import lax
from jax.experimental import pallas as pl
from jax.experimental.pallas import tpu as pltpu

F32 = jnp.float32
BF16 = jnp.bfloat16

D_MODEL = 1024
DEPTH = 2
HG_HEADS = 8
HG_DIM = 128
HG_WIDTH = HG_HEADS * HG_DIM
CHUNK = 64
ATT_HEADS = 16
ATT_DIM = 64
ATT_WIDTH = ATT_HEADS * ATT_DIM
KV_WIDTH = 128
ATT_BLOCK = 128
ATT_SCALE = 1.0 / math.sqrt(ATT_DIM)
ROPE_THETA = 10000.0
NORM_EPS = 1e-6
NEG_INF = -1e30
LB_FLOOR = 1e-20
N_H = 4 * HG_WIDTH
N_A = 2 * ATT_WIDTH + 2 * KV_WIDTH
IN_WIDTH = N_H + N_A
MIX_WIDTH = HG_WIDTH + ATT_WIDTH

ADAM_LR = 0.001
ADAM_B1 = 0.9
ADAM_B2 = 0.999
ADAM_EPS = 1e-08
ADAM_WD = 0.01
ADAM_STEP = 10

N_DEV = 8
MESH = pl.DeviceIdType.MESH
VMEM_LIMIT = 56 * 1024 * 1024

NN = ((1,), (0,))
NT = ((1,), (1,))
TN = ((0,), (0,))


def _dot(a, b, dims):
    return lax.dot_general(a.astype(BF16), b.astype(BF16), (dims, ((), ())), preferred_element_type=F32)


def _params(sem=None, **kw):
    return pltpu.CompilerParams(dimension_semantics=sem, vmem_limit_bytes=VMEM_LIMIT, **kw)


def _sigmoids(x):
    e = jnp.exp(-jnp.abs(x))
    r = 1.0 / (1.0 + e)
    er = e * r
    pos = x >= 0.0
    return jnp.where(pos, r, er), jnp.where(pos, er, r)


def _silu(x):
    return x * _sigmoids(x)[0]


def _silu_and_grad(x):
    s, ns = _sigmoids(x)
    return x * s, s * (1.0 + x * ns)


def _pick(n, prefs):
    for p in prefs:
        if n % p == 0:
            return p
    return n


def _inproj(x2, g, w, name):
    T, D = x2.shape
    tm = _pick(T, (512, 256, 128))
    nchunk = 1024

    def body(x_ref, g_ref, w_ref, oh_ref, oa_ref, h_ref):
        x = x_ref[...]
        r = lax.rsqrt(jnp.mean(x * x, axis=-1, keepdims=True) + NORM_EPS)
        h = ((x * r) * g_ref[...]).astype(BF16)
        h_ref[...] = h
        for j in range(0, N_H, nchunk):
            oh_ref[:, j:j + nchunk] = lax.dot_general(h, w_ref[j:j + nchunk, :], (NT, ((), ())),
                                                      preferred_element_type=F32)
        for j in range(0, N_A, N_A // 2):
            oa_ref[:, j:j + N_A // 2] = lax.dot_general(h, w_ref[N_H + j:N_H + j + N_A // 2, :], (NT, ((), ())),
                                                        preferred_element_type=F32)

    row = lambda w_: pl.BlockSpec((tm, w_), lambda i: (i, 0))
    return pl.pallas_call(
        body, name=name,
        grid=(T // tm,),
        in_specs=[row(D), pl.BlockSpec((1, D), lambda i: (0, 0)),
                  pl.BlockSpec((IN_WIDTH, D), lambda i: (0, 0), pipeline_mode=pl.Buffered(1))],
        out_specs=[row(N_H), row(N_A), row(D)],
        out_shape=[jax.ShapeDtypeStruct((T, N_H), F32), jax.ShapeDtypeStruct((T, N_A), F32),
                   jax.ShapeDtypeStruct((T, D), BF16)],
        compiler_params=_params(("parallel",)),
    )(x2, g, w)


def _mm_tn(pieces, b, name, out_dtype=BF16):
    T, m = b.shape
    tn = 256
    counts = [p.shape[1] // tn for p in pieces]
    starts = [sum(counts[:i]) for i in range(len(pieces))]
    n_p = len(pieces)

    def body(*refs):
        b_ref, o_ref = refs[n_p], refs[n_p + 1]
        i = pl.program_id(0)
        for p in range(n_p):
            @pl.when((i >= starts[p]) & (i < starts[p] + counts[p]))
            def _(p=p):
                o_ref[...] = lax.dot_general(refs[p][...], b_ref[...], (TN, ((), ())),
                                             preferred_element_type=F32).astype(out_dtype)

    piece_spec = lambda s, c: pl.BlockSpec((T, tn), lambda i: (0, jnp.clip(i - s, 0, c - 1)))
    return pl.pallas_call(
        body, name=name,
        grid=(sum(counts),),
        in_specs=[piece_spec(s, c) for s, c in zip(starts, counts)]
        + [pl.BlockSpec((T, m), lambda i: (0, 0), pipeline_mode=pl.Buffered(1))],
        out_specs=pl.BlockSpec((tn, m), lambda i: (i, 0)),
        out_shape=jax.ShapeDtypeStruct((sum(counts) * tn, m), out_dtype),
        compiler_params=_params(("arbitrary",)),
    )(*pieces, b)


_LEVELS = (0, 1, 2, 4, 8, 16, 32)
_CUM_L = (2, 4, 8, 16, 32, 64)
_ALL_KINDS = tuple(("c", L) for L in _CUM_L) + tuple(("r", L) for L in _CUM_L)
_MXU_KINDS = (("c", 2), ("c", 4), ("c", CHUNK), ("r", 2), ("r", 4))
N_CUM = len(_ALL_KINDS) * CHUNK
N_CUM_F = len(_MXU_KINDS) * CHUNK


def _cum_matrices():
    t = np.arange(CHUNK)[:, None]
    r = np.arange(CHUNK)[None, :]

    def mat(kind):
        c, L = kind
        return ((r // L == t // L) & ((r <= t) if c == "c" else (r > t))).astype(np.float32)

    fwd = np.concatenate([mat(kd) for kd in _MXU_KINDS], axis=0)
    full = np.concatenate([mat(kd) for kd in _ALL_KINDS], axis=0)
    return jnp.asarray(fwd, BF16), jnp.asarray(full.T.copy(), BF16)


def _level_masks():
    t = np.arange(CHUNK)[:, None]
    s = np.arange(CHUNK)[None, :]
    ms = []
    for L in _LEVELS:
        if L == 0:
            ms.append(t == s)
        else:
            ms.append((t // (2 * L) == s // (2 * L)) & ((t // L) % 2 == 1) & ((s // L) % 2 == 0))
    return jnp.asarray(np.stack(ms).astype(np.float32))


def _split3(x):
    hi = x.astype(BF16)
    r1 = x - hi.astype(F32)
    mid = r1.astype(BF16)
    lo = (r1 - mid.astype(F32)).astype(BF16)
    return hi, mid, lo


def _cum3(ts, x, terms=3):
    d = lambda p: lax.dot_general(ts, p, (NN, ((), ())), preferred_element_type=F32)
    return sum(d(p) for p in _split3(x)[:terms])


def _lb_terms(lbp, layer):
    mx = jnp.max(lbp, axis=0, keepdims=True)
    e = jnp.exp(lbp - mx)
    p = e / jnp.sum(e, axis=0, keepdims=True)
    cum = p[0:1]
    for j in range(1, layer + 1):
        cum = cum + p[j:j + 1]
    lb = cum - p[0:1]
    lbf = jnp.maximum(lb, LB_FLOOR)
    return dict(lbf=lbf, one_m=1.0 - lb, kcorr=lb - lbf, ind=jnp.where(lb > LB_FLOOR, 1.0, 0.0))


def _gate(x, lt):
    sig, nsig = _sigmoids(x)
    f = lt["lbf"] + lt["one_m"] * sig
    return jnp.log(f), lt["one_m"] * nsig + lt["kcorr"], f, sig, nsig


def _ck(x, ci):
    return x[ci * CHUNK:(ci + 1) * CHUNK]


def _block_cums(ts, g, nc):
    cs = [_cum3(ts, _ck(g, ci), terms=2) for ci in range(nc)]
    out = {kind: jnp.concatenate([c[CHUNK * i:CHUNK * (i + 1)] for c in cs], axis=0)
           for i, kind in enumerate(_MXU_KINDS)}
    b = out[("c", CHUNK)]
    ng = CHUNK // 8
    last = b.reshape(nc, ng, 8, HG_DIM)[:, :, 7:8, :]
    zero = jnp.zeros((nc, 1, 1, HG_DIM), F32)

    def spread(groups):
        return jnp.broadcast_to(jnp.concatenate(groups, axis=1), (nc, ng, 8, HG_DIM)).reshape(nc * CHUNK, HG_DIM)

    def get(kind):
        if kind in out:
            return out[kind]
        c, L = kind
        nb = L // 8
        first = lambda r: (r // nb) * nb
        if c == "c":
            return b - spread([last[:, first(r) - 1:first(r)] if r >= nb else zero for r in range(ng)])
        return spread([last[:, first(r) + nb - 1:first(r) + nb] for r in range(ng)]) - b

    return get


def _level_factors(cums, g, L):
    if L == 0:
        return None, None
    if L == 1:
        return jnp.exp(g), None
    return jnp.exp(cums(("c", L))), jnp.exp(cums(("r", L)))


def _mul(a, e):
    return a if e is None else a * e


def _hg_block_fwd(qf, k, v, g, ts, m_ref, nc):
    cums = _block_cums(ts, g, nc)
    amat = [jnp.zeros((CHUNK, CHUNK), F32)] * nc
    for li, L in enumerate(_LEVELS):
        eq, ek = _level_factors(cums, g, L)
        ql, kl, m = _mul(qf, eq), _mul(k, ek), m_ref[li]
        amat = [amat[ci] + _dot(_ck(ql, ci), _ck(kl, ci), NT) * m for ci in range(nc)]
    b = cums(("c", CHUNK))
    kst = k * jnp.exp(cums(("r", CHUNK)))
    o = [_dot(amat[ci], _ck(v, ci), NN) for ci in range(nc)]
    kv = [_dot(_ck(v, ci), _ck(kst, ci), TN) for ci in range(nc)]
    dec = [jnp.exp(b[(ci + 1) * CHUNK - 1:(ci + 1) * CHUNK, :]) for ci in range(nc)]
    return o, dec, kv, qf * jnp.exp(b), amat


def _hg_block_bwd(qf, k, v, g, do, amat, ts, m_ref, nc):
    cums = _block_cums(ts, g, nc)
    dcs = {}
    da = [_dot(_ck(do, ci), _ck(v, ci), NT) for ci in range(nc)]
    dq = jnp.zeros_like(qf)
    dk = jnp.zeros_like(qf)
    dg = jnp.zeros_like(qf)
    for li, L in enumerate(_LEVELS):
        eq, ek = _level_factors(cums, g, L)
        ql, kl, m = _mul(qf, eq), _mul(k, ek), m_ref[li]
        qlb, klb = ql.astype(BF16), kl.astype(BF16)
        dal = [(da[ci] * m).astype(BF16) for ci in range(nc)]
        dql = jnp.concatenate([_dot(dal[ci], _ck(klb, ci), NN) for ci in range(nc)], axis=0)
        dkl = jnp.concatenate([_dot(dal[ci], _ck(qlb, ci), TN) for ci in range(nc)], axis=0)
        dq = dq + _mul(dql, eq)
        dk = dk + _mul(dkl, ek)
        if L == 1:
            dg = dg + dql * ql
        elif L > 1:
            dcs[("c", L)] = (dql * ql).astype(BF16)
            dcs[("r", L)] = (dkl * kl).astype(BF16)
    b = cums(("c", CHUNK))
    e64 = jnp.exp(b)
    er64 = jnp.exp(cums(("r", CHUNK)))
    qb = qf * e64
    return dict(dq=dq, dk=dk, dg=dg, dcs=dcs, e64=e64, er64=er64, qb=qb, kst=k * er64,
                dv=[_dot(amat[ci], _ck(do, ci), TN) for ci in range(nc)],
                dec=[jnp.exp(b[(ci + 1) * CHUNK - 1:(ci + 1) * CHUNK, :]) for ci in range(nc)],
                qd=[_dot(_ck(do, ci), _ck(qb, ci), TN) for ci in range(nc)])


def _hg_state_bwd(w, v, do, starts, ends, tst, nc):
    dqb = jnp.concatenate([_dot(_ck(do, ci), starts[ci], NN) for ci in range(nc)], axis=0)
    dkst = jnp.concatenate([_dot(_ck(v, ci), ends[ci], NN) for ci in range(nc)], axis=0)
    dq = w["dq"] + dqb * w["e64"]
    dk = w["dk"] + dkst * w["er64"]
    dv = jnp.concatenate([w["dv"][ci] + _dot(_ck(w["kst"], ci), ends[ci], NT) for ci in range(nc)], axis=0)
    trow = lax.broadcasted_iota(jnp.int32, (CHUNK, 1), 0)
    dtot = jnp.concatenate(
        [jnp.where(trow == CHUNK - 1, jnp.sum(ends[ci] * starts[ci], axis=0, keepdims=True) * w["dec"][ci], 0.0)
         for ci in range(nc)], axis=0)
    dcs = dict(w["dcs"])
    dcs[("c", CHUNK)] = (dqb * w["qb"] + dtot).astype(BF16)
    dcs[("r", CHUNK)] = (dkst * w["kst"]).astype(BF16)
    dgs = [_dot(tst, jnp.concatenate([_ck(dcs[kind], ci) for kind in _ALL_KINDS], axis=0), NN) for ci in range(nc)]
    return dq, dk, dv, w["dg"] + jnp.concatenate(dgs, axis=0)


def _hgrn_fwd(proj_h, u_rows, lb_param, g_head, layer, name, phase=None):
    B, S, _ = proj_h.shape
    sb = _pick(S, (1024, 512, 256, 128, 64))
    nc = sb // CHUNK
    ts, _ = _cum_matrices()

    def body(*refs):
        ins, outs, (st,), p_in, p_out, p_sems = _split_refs(refs, 8, 4, 1, phase)
        q_ref, f_ref, i_ref, z_ref, lbp_ref, gh_ref, ts_ref, m_ref = ins
        o_ref, u_ref, sts_ref, am_ref = outs
        h_id, b_id, s_id = pl.program_id(0), pl.program_id(1), pl.program_id(2)
        _hosted_start(phase, p_in, p_out, p_sems, (h_id == 0) & (b_id == 0) & (s_id == 0))

        @pl.when(s_id == 0)
        def _():
            st[...] = jnp.zeros_like(st)

        lt = _lb_terms(lbp_ref[...], layer)
        tsv = ts_ref[...]
        gh = gh_ref[...]
        logf, k = _gate(f_ref[...], lt)[:2]
        o_part, dec, kv, qb, amat = _hg_block_fwd(_silu(q_ref[...]), k, i_ref[...], logf, tsv, m_ref, nc)
        for ci in range(nc):
            am_ref[ci] = amat[ci].astype(BF16)
        cur = st[...]
        starts = []
        for ci in range(nc):
            sts_ref[ci] = cur
            starts.append(cur)
            cur = cur * dec[ci] + kv[ci]
        st[...] = cur
        o = jnp.concatenate([o_part[ci] + _dot(_ck(qb, ci), starts[ci], NT) for ci in range(nc)], axis=0)
        o_ref[...] = o
        r = lax.rsqrt(jnp.mean(o * o, axis=-1, keepdims=True) + NORM_EPS)
        u_ref[...] = (((o * r) * gh) * _silu(z_ref[...])).astype(BF16)
        _hosted_finish(phase, p_in, p_out, p_sems, (h_id == HG_HEADS - 1) & (b_id == B - 1) & (s_id == S // sb - 1))

    col = lambda base: pl.BlockSpec((None, sb, HG_DIM), lambda h, b, s: (b, s, base + h))
    p_ispecs, p_ospecs, p_oshapes, p_alias, p_scratch, p_args = _host_phase(phase, 8, 4)
    res = pl.pallas_call(
        body, name=name,
        grid=(HG_HEADS, B, S // sb),
        in_specs=[col(0), col(HG_HEADS), col(2 * HG_HEADS), col(3 * HG_HEADS),
                  pl.BlockSpec((DEPTH, HG_DIM), lambda h, b, s: (0, h)),
                  pl.BlockSpec((1, HG_DIM), lambda h, b, s: (0, 0)),
                  pl.BlockSpec((N_CUM_F, CHUNK), lambda h, b, s: (0, 0)),
                  pl.BlockSpec((len(_LEVELS), CHUNK, CHUNK), lambda h, b, s: (0, 0, 0))] + p_ispecs,
        out_specs=[col(0), col(0),
                   pl.BlockSpec((None, None, nc, HG_DIM, HG_DIM), lambda h, b, s: (b, h, s, 0, 0)),
                   pl.BlockSpec((None, None, nc, CHUNK, CHUNK), lambda h, b, s: (b, h, s, 0, 0))] + p_ospecs,
        out_shape=[jax.ShapeDtypeStruct((B, S, HG_WIDTH), F32),
                   jax.ShapeDtypeStruct((B, S, u_rows), BF16),
                   jax.ShapeDtypeStruct((B, HG_HEADS, S // CHUNK, HG_DIM, HG_DIM), F32),
                   jax.ShapeDtypeStruct((B, HG_HEADS, S // CHUNK, CHUNK, CHUNK), BF16)] + p_oshapes,
        input_output_aliases=p_alias,
        scratch_shapes=[pltpu.VMEM((HG_DIM, HG_DIM), F32)] + p_scratch,
        compiler_params=_params(("arbitrary", "arbitrary", "arbitrary")),
    )(proj_h, proj_h, proj_h, proj_h, lb_param, g_head, ts, _level_masks(), *p_args)
    return res[0], res[1], (res[2], res[3]), list(res[4:])


def _hgrn_bwd(proj_h, o_h, du, kept, lb_param, g_head, layer, name, phase=None):
    B, S, _ = proj_h.shape
    sb = _pick(S, (512, 256, 128, 64))
    nc = sb // CHUNK
    ns = S // sb
    ts, tst = _cum_matrices()

    def body(*refs):
        ins, outs, (dst,), p_in, p_out, p_sems = _split_refs(refs, 13, 6, 1, phase)
        q_ref, f_ref, i_ref, z_ref, o_ref, du_ref, sts_ref, am_ref, lbp_ref, gh_ref, ts_ref, tst_ref, m_ref = ins
        dq_ref, df_ref, di_ref, dz_ref, dlb_ref, dgh_ref = outs
        h_id, b_id, s_id = pl.program_id(0), pl.program_id(1), pl.program_id(2)
        _hosted_start(phase, p_in, p_out, p_sems, (h_id == 0) & (b_id == 0) & (s_id == 0))

        @pl.when(s_id == 0)
        def _():
            dst[...] = jnp.zeros_like(dst)

        @pl.when((b_id == 0) & (s_id == 0))
        def _():
            dlb_ref[...] = jnp.zeros_like(dlb_ref)

        @pl.when((h_id == 0) & (b_id == 0) & (s_id == 0))
        def _():
            dgh_ref[...] = jnp.zeros_like(dgh_ref)

        lt = _lb_terms(lbp_ref[...], layer)
        gh = gh_ref[...]
        tsv = ts_ref[...]
        tstv = tst_ref[...]
        logf, k, f, sig, nsig = _gate(f_ref[...], lt)
        o = o_ref[...]
        dub = du_ref[...]
        r = lax.rsqrt(jnp.mean(o * o, axis=-1, keepdims=True) + NORM_EPS)
        n = o * r
        sg, sg_grad = _silu_and_grad(z_ref[...])
        dz_ref[...] = (dub * (n * gh) * sg_grad).astype(BF16)
        dgh_ref[...] += jnp.sum(dub * sg * n, axis=0, keepdims=True)
        dn = dub * sg * gh
        do = r * (dn - n * jnp.mean(dn * n, axis=-1, keepdims=True))
        v = i_ref[...]
        qf, qf_grad = _silu_and_grad(q_ref[...])
        w = _hg_block_bwd(qf, k, v, logf, do, [am_ref[ci] for ci in range(nc)], tsv, m_ref, nc)
        cur = dst[...]
        ends = [None] * nc
        for ci in reversed(range(nc)):
            ends[ci] = cur
            cur = cur * w["dec"][ci] + w["qd"][ci]
        dst[...] = cur
        dq, dk, dv, dg = _hg_state_bwd(w, v, do, [sts_ref[ci] for ci in range(nc)], ends, tstv, nc)
        di_ref[...] = dv.astype(BF16)
        dq_ref[...] = (dq * qf_grad).astype(BF16)
        scaled = (dg - f * dk) / f
        df_ref[...] = (scaled * lt["one_m"] * sig * nsig).astype(BF16)
        dlb_ref[...] += jnp.sum(scaled * (lt["ind"] - sig), axis=0, keepdims=True)
        _hosted_finish(phase, p_in, p_out, p_sems, (h_id == HG_HEADS - 1) & (b_id == B - 1) & (s_id == ns - 1))

    col = lambda base: pl.BlockSpec((None, sb, HG_DIM), lambda h, b, s: (b, ns - 1 - s, base + h))
    out_col = pl.BlockSpec((None, sb, HG_DIM), lambda h, b, s: (b, ns - 1 - s, h))
    dt = jax.ShapeDtypeStruct((B, S, HG_WIDTH), BF16)
    p_ispecs, p_ospecs, p_oshapes, p_alias, p_scratch, p_args = _host_phase(phase, 13, 6)
    res = pl.pallas_call(
        body, name=name,
        grid=(HG_HEADS, B, ns),
        in_specs=[col(0), col(HG_HEADS), col(2 * HG_HEADS), col(3 * HG_HEADS), col(0), col(0),
                  pl.BlockSpec((None, None, nc, HG_DIM, HG_DIM), lambda h, b, s: (b, h, ns - 1 - s, 0, 0)),
                  pl.BlockSpec((None, None, nc, CHUNK, CHUNK), lambda h, b, s: (b, h, ns - 1 - s, 0, 0)),
                  pl.BlockSpec((DEPTH, HG_DIM), lambda h, b, s: (0, h)),
                  pl.BlockSpec((1, HG_DIM), lambda h, b, s: (0, 0)),
                  pl.BlockSpec((N_CUM_F, CHUNK), lambda h, b, s: (0, 0)),
                  pl.BlockSpec((CHUNK, N_CUM), lambda h, b, s: (0, 0)),
                  pl.BlockSpec((len(_LEVELS), CHUNK, CHUNK), lambda h, b, s: (0, 0, 0))] + p_ispecs,
        out_specs=[out_col, out_col, out_col, out_col,
                   pl.BlockSpec((1, HG_DIM), lambda h, b, s: (0, h)),
                   pl.BlockSpec((1, HG_DIM), lambda h, b, s: (0, 0))] + p_ospecs,
        out_shape=[dt, dt, dt, dt, jax.ShapeDtypeStruct((1, HG_WIDTH), F32),
                   jax.ShapeDtypeStruct((1, HG_DIM), F32)] + p_oshapes,
        input_output_aliases=p_alias,
        scratch_shapes=[pltpu.VMEM((HG_DIM, HG_DIM), F32)] + p_scratch,
        compiler_params=_params(("arbitrary", "arbitrary", "arbitrary")),
    )(proj_h, proj_h, proj_h, proj_h, o_h, du, kept[0], kept[1], lb_param, g_head, ts, tst, _level_masks(), *p_args)
    return tuple(res[:6]) + (list(res[6:]),)


def _rope_tables(S):
    half = ATT_DIM // 2
    inv_freq = ROPE_THETA ** (-jnp.arange(half, dtype=F32) / half)
    ang = jnp.arange(S).astype(F32)[:, None] * inv_freq[None, :]
    cos = jnp.cos(ang)
    sin = jnp.sin(ang)
    cos = jnp.concatenate([cos, cos, cos, cos], axis=1)
    sin = jnp.concatenate([-sin, sin, -sin, sin], axis=1)
    return cos, sin


def _attn_common():
    lane = lax.broadcasted_iota(jnp.int32, (1, 2 * ATT_DIM), 1)
    first_half = (lane % ATT_DIM) < (ATT_DIM // 2)
    left = lane < ATT_DIM

    def swap(x):
        return jnp.where(first_half, pltpu.roll(x, 128 - ATT_DIM // 2, 1), pltpu.roll(x, ATT_DIM // 2, 1))

    def rope(x, cos, sin):
        return x * cos + swap(x) * sin

    def rope_bwd(dy, cos, sin):
        return dy * cos + swap(dy * sin)

    def dup(x):
        xs = pltpu.roll(x, ATT_DIM, 1)
        return [jnp.where(left, x, xs), jnp.where(left, xs, x)]

    return left, rope, rope_bwd, dup


GROUP = ATT_HEADS // 2
GROUP_ROWS = GROUP * ATT_BLOCK


def _attn_bias(i):
    r = lax.broadcasted_iota(jnp.int32, (ATT_BLOCK, 2 * ATT_BLOCK), 0)
    c = lax.broadcasted_iota(jnp.int32, (ATT_BLOCK, 2 * ATT_BLOCK), 1)
    ok = (c > r) & (c <= r + ATT_BLOCK) & ((c >= ATT_BLOCK) | (i > 0))
    return jnp.where(ok, 0.0, NEG_INF)


def _stack_heads(pairs, left):
    rows = []
    for x in pairs:
        rows += [jnp.where(left, x, 0.0), jnp.where(left, 0.0, x)]
    return jnp.concatenate(rows, axis=0)


def _unstack_heads(y, left, pp):
    r0 = 2 * pp * ATT_BLOCK
    return jnp.where(left, y[r0:r0 + ATT_BLOCK], y[r0 + ATT_BLOCK:r0 + 2 * ATT_BLOCK])


def _row_sums(x):
    return _dot(x, jnp.ones((x.shape[1], 128), BF16), NN)


def _attn_probs(qs, kd, vd, sink, bias):
    n = range(len(qs))
    rows = qs[0].shape[0]
    s = [(_dot(qs[j], kd[j], NT).reshape(rows // ATT_BLOCK, ATT_BLOCK, 2 * ATT_BLOCK) * ATT_SCALE + bias[None])
         .reshape(rows, 2 * ATT_BLOCK) for j in n]
    m = [jnp.max(jnp.maximum(jnp.maximum(s[j][:, :128], s[j][:, 128:]), sink[j]), axis=-1, keepdims=True) for j in n]
    pu = [jnp.exp(s[j] - m[j]) for j in n]
    es = [jnp.exp(sink[j] - m[j]) for j in n]
    ones = jnp.ones((2 * ATT_BLOCK, 128), BF16)
    ov = [_dot(pu[j], jnp.concatenate([vd[j].astype(BF16), ones], axis=1), NN) for j in n]
    inv = [1.0 / (ov[j][:, 128:] + es[j]) for j in n]
    return ([pu[j] * jnp.concatenate([inv[j], inv[j]], axis=1) for j in n], [es[j] * inv[j] for j in n],
            [ov[j][:, :128] * inv[j] for j in n])


def _sink_rows(sinks_l):
    return jnp.broadcast_to(jnp.repeat(sinks_l, ATT_BLOCK)[:, None], (ATT_HEADS * ATT_BLOCK, 128))


_Z0 = (2 * ATT_WIDTH + 2 * KV_WIDTH - ATT_WIDTH) // 256


def _attn_fwd(proj_a, u, sinks_l, cos, sin, name, phase=None):
    B, S, _ = proj_a.shape
    nb = S // ATT_BLOCK

    def body(*refs):
        ins, (u_ref, p_ref, o_ref, ps_ref), _, p_in, p_out, p_sems = _split_refs(refs, 13, 4, 0, phase)
        q_ref, kvc_ref, kvp_ref, z0, z1, z2, z3, cos_ref, sin_ref, cosp_ref, sinp_ref, sinks_ref, _ = ins
        i = pl.program_id(1)
        _hosted_start(phase, p_in, p_out, p_sems, (pl.program_id(0) == 0) & (i == 0))
        left, rope, _, dup = _attn_common()
        cos_c, sin_c = cos_ref[...], sin_ref[...]
        kvc = kvc_ref[...]
        kvp = kvp_ref[...]
        kw = jnp.concatenate([rope(kvp[:, :KV_WIDTH], cosp_ref[...], sinp_ref[...]),
                              rope(kvc[:, :KV_WIDTH], cos_c, sin_c)], axis=0)
        vw = jnp.concatenate([kvp[:, KV_WIDTH:], kvc[:, KV_WIDTH:]], axis=0)
        kd, vd = dup(kw), dup(vw)
        bias = _attn_bias(i)
        zs = (z0, z1, z2, z3)
        pairs = [range(4 * kvh, 4 * kvh + 4) for kvh in range(2)]
        qs = [_stack_heads([rope(q_ref[:, 128 * pr:128 * (pr + 1)], cos_c, sin_c) for pr in pairs[kvh]], left)
              for kvh in range(2)]
        sink = [sinks_ref[kvh * GROUP_ROWS:(kvh + 1) * GROUP_ROWS, :] for kvh in range(2)]
        p, ps, o = _attn_probs(qs, kd, vd, sink, bias)
        eye = (lax.broadcasted_iota(jnp.int32, (ATT_BLOCK, 128), 0)
               == lax.broadcasted_iota(jnp.int32, (ATT_BLOCK, 128), 1))
        for kvh in range(2):
            p_ref[kvh] = p[kvh].astype(BF16)
            for g in range(GROUP):
                blk = ps[kvh][g * ATT_BLOCK:(g + 1) * ATT_BLOCK, :]
                ps_ref[kvh * GROUP + g:kvh * GROUP + g + 1, :] = jnp.sum(jnp.where(eye, blk, 0.0), axis=0, keepdims=True)
            for pp, pr in enumerate(pairs[kvh]):
                z = zs[pr // 2][:, 128 * (pr % 2):128 * (pr % 2 + 1)]
                o128 = _unstack_heads(o[kvh], left, pp)
                o_ref[:, 128 * pr:128 * (pr + 1)] = o128.astype(BF16)
                u_ref[:, 128 * pr:128 * (pr + 1)] = (o128 * _silu(z)).astype(BF16)
        _hosted_finish(phase, p_in, p_out, p_sems, (pl.program_id(0) == B - 1) & (i == nb - 1))

    rowblk = lambda w, cb: pl.BlockSpec((None, ATT_BLOCK, w), lambda b, i: (b, i, cb))
    tab = pl.BlockSpec((ATT_BLOCK, 128), lambda b, i: (i, 0))
    tabp = pl.BlockSpec((ATT_BLOCK, 128), lambda b, i: (jnp.maximum(i - 1, 0), 0))
    p_ispecs, p_ospecs, p_oshapes, p_alias, p_scratch, p_args = _host_phase(phase, 13, 4)
    res = pl.pallas_call(
        body, name=name,
        grid=(B, nb),
        in_specs=[rowblk(ATT_WIDTH, 0), rowblk(256, 4),
                  pl.BlockSpec((None, ATT_BLOCK, 256), lambda b, i: (b, jnp.maximum(i - 1, 0), 4)),
                  rowblk(256, _Z0), rowblk(256, _Z0 + 1), rowblk(256, _Z0 + 2), rowblk(256, _Z0 + 3),
                  tab, tab, tabp, tabp,
                  pl.BlockSpec((ATT_HEADS * ATT_BLOCK, 128), lambda b, i: (0, 0)),
                  pl.BlockSpec(memory_space=pl.ANY)] + p_ispecs,
        out_specs=[pl.BlockSpec((None, ATT_BLOCK, ATT_WIDTH), lambda b, i: (b, i, 1)),
                   pl.BlockSpec((None, None, 2, GROUP_ROWS, 2 * ATT_BLOCK), lambda b, i: (b, i, 0, 0, 0)),
                   pl.BlockSpec((None, ATT_BLOCK, ATT_WIDTH), lambda b, i: (b, i, 0)),
                   pl.BlockSpec((None, None, ATT_HEADS, 128), lambda b, i: (b, i, 0, 0))] + p_ospecs,
        out_shape=[jax.ShapeDtypeStruct(u.shape, BF16),
                   jax.ShapeDtypeStruct((B, nb, 2, GROUP_ROWS, 2 * ATT_BLOCK), BF16),
                   jax.ShapeDtypeStruct((B, S, ATT_WIDTH), BF16),
                   jax.ShapeDtypeStruct((B, nb, ATT_HEADS, 128), F32)] + p_oshapes,
        input_output_aliases={12: 0, **p_alias},
        scratch_shapes=p_scratch,
        compiler_params=_params(("arbitrary", "arbitrary")),
    )(proj_a, proj_a, proj_a, proj_a, proj_a, proj_a, proj_a, cos, sin, cos, sin, sinks_l, u, *p_args)
    return res[0], tuple(res[1:4]), list(res[4:])


def _attn_bwd(proj_a, du, kept, cos, sin, name, phase=None):
    B, S, _ = proj_a.shape
    nb = S // ATT_BLOCK
    p_kept, o_kept, ps_kept = kept

    def body(*refs):
        ins, outs, (carry, sk_acc), p_in, p_out, p_sems = _split_refs(refs, 15, 4, 2, phase)
        (q_ref, kvc_ref, kvp_ref, z0, z1, z2, z3, du_ref, cos_ref, sin_ref, cosp_ref, sinp_ref,
         p_ref, o_ref, ps_ref) = ins
        dq_ref, dkv_ref, dz_ref, dsk_ref = outs
        b_id, i = pl.program_id(0), pl.program_id(1)
        _hosted_start(phase, p_in, p_out, p_sems, (b_id == 0) & (i == 0))

        @pl.when((b_id == 0) & (i == 0))
        def _():
            sk_acc[...] = jnp.zeros_like(sk_acc)

        @pl.when(i == 0)
        def _():
            carry[...] = jnp.zeros_like(carry)

        @pl.when(i < nb)
        def _():
            left, rope, rope_bwd, dup = _attn_common()
            cos_c, sin_c = cos_ref[...], sin_ref[...]
            cos_p, sin_p = cosp_ref[...], sinp_ref[...]
            kvc = kvc_ref[...]
            kvp = kvp_ref[...]
            kw = jnp.concatenate([rope(kvp[:, :KV_WIDTH], cos_p, sin_p), rope(kvc[:, :KV_WIDTH], cos_c, sin_c)], axis=0)
            vw = jnp.concatenate([kvp[:, KV_WIDTH:], kvc[:, KV_WIDTH:]], axis=0)
            kd, vd = dup(kw), dup(vw)
            zs = (z0, z1, z2, z3)
            units = [(kvh, hf) for kvh in range(2) for hf in range(2)]
            half = GROUP_ROWS // 2
            pairs = [range(4 * kvh + 2 * hf, 4 * kvh + 2 * hf + 2) for kvh, hf in units]
            ku = [kd[kvh] for kvh, _ in units]
            vu = [vd[kvh] for kvh, _ in units]
            ps_all = ps_ref[...]
            head_row = lax.broadcasted_iota(jnp.int32, (ATT_HEADS, 128), 0)
            eye = (lax.broadcasted_iota(jnp.int32, (ATT_BLOCK, 128), 0)
                   == lax.broadcasted_iota(jnp.int32, (ATT_BLOCK, 128), 1))

            def first(j):
                kvh, hf = units[j]
                p = p_ref[kvh, hf * half:(hf + 1) * half, :]
                parts = []
                for pr in pairs[j]:
                    cols = slice(128 * pr, 128 * (pr + 1))
                    sg, sg_grad = _silu_and_grad(zs[pr // 2][:, 128 * (pr % 2):128 * (pr % 2 + 1)])
                    du128 = du_ref[:, cols]
                    dz_ref[:, cols] = (du128 * o_ref[:, cols].astype(F32) * sg_grad).astype(BF16)
                    parts.append(du128 * sg)
                dos = _stack_heads(parts, left)
                dp = _dot(dos, vu[j], NT)
                delta = _row_sums(p.astype(F32) * dp)
                ds = (p.astype(F32) * (dp - jnp.concatenate([delta, delta], axis=1)) * ATT_SCALE).astype(BF16)
                sk = jnp.zeros((ATT_HEADS, 128), F32)
                for hh in range(4):
                    hd = kvh * GROUP + 4 * hf + hh
                    drow = jnp.sum(jnp.where(eye, delta[hh * ATT_BLOCK:(hh + 1) * ATT_BLOCK, :], 0.0), axis=0,
                                   keepdims=True)
                    sk = sk - jnp.where(head_row == hd, ps_all * drow, 0.0)
                sk_acc[...] += sk
                qs = _stack_heads([rope(q_ref[:, 128 * pr:128 * (pr + 1)], cos_c, sin_c) for pr in pairs[j]], left)
                return ds, p, dos.astype(BF16), qs.astype(BF16)

            def second(j, ds, p, dos, qs):
                dqs = _dot(ds, ku[j], NN)
                for pp, pr in enumerate(pairs[j]):
                    dq_ref[:, 128 * pr:128 * (pr + 1)] = rope_bwd(_unstack_heads(dqs, left, pp),
                                                                  cos_c, sin_c).astype(BF16)
                return _dot(ds, qs, TN), _dot(p, dos, TN)

            got, dku, dvu = {}, [None] * len(units), [None] * len(units)
            for j in range(len(units) + 1):
                if j < len(units):
                    got[j] = first(j)
                if j >= 1:
                    dku[j - 1], dvu[j - 1] = second(j - 1, *got.pop(j - 1))
            dkd = [dku[0] + dku[1], dku[2] + dku[3]]
            dvd = [dvu[0] + dvu[1], dvu[2] + dvu[3]]
            fold = lambda pr: jnp.where(left, pr[0] + pltpu.roll(pr[0], ATT_DIM, 1), pr[1] + pltpu.roll(pr[1], ATT_DIM, 1))
            dkw = fold(dkd)
            dvw = fold(dvd)
            prev = jnp.concatenate([rope_bwd(dkw[:ATT_BLOCK], cos_p, sin_p), dvw[:ATT_BLOCK]], axis=1)
            cur = jnp.concatenate([rope_bwd(dkw[ATT_BLOCK:], cos_c, sin_c), dvw[ATT_BLOCK:]], axis=1)
            dkv_ref[...] = (carry[...] + prev).astype(BF16)
            carry[...] = cur

        @pl.when(i == nb)
        def _():
            dkv_ref[...] = carry[...].astype(BF16)

        @pl.when((b_id == B - 1) & (i == nb))
        def _():
            diag = (lax.broadcasted_iota(jnp.int32, (ATT_HEADS, 128), 0)
                    == lax.broadcasted_iota(jnp.int32, (ATT_HEADS, 128), 1))
            tot = jnp.sum(sk_acc[...], axis=1, keepdims=True)
            dsk_ref[...] = jnp.sum(jnp.where(diag, tot, 0.0), axis=0, keepdims=True)

        _hosted_finish(phase, p_in, p_out, p_sems, (b_id == B - 1) & (i == nb))

    cl = lambda i: jnp.minimum(i, nb - 1)
    pv = lambda i: jnp.maximum(jnp.minimum(i, nb - 1) - 1, 0)
    rowblk = lambda w, cb: pl.BlockSpec((None, ATT_BLOCK, w), lambda b, i: (b, cl(i), cb))
    tab = pl.BlockSpec((ATT_BLOCK, 128), lambda b, i: (cl(i), 0))
    tabp = pl.BlockSpec((ATT_BLOCK, 128), lambda b, i: (pv(i), 0))
    p_ispecs, p_ospecs, p_oshapes, p_alias, p_scratch, p_args = _host_phase(phase, 15, 4)
    res = pl.pallas_call(
        body, name=name,
        grid=(B, nb + 1),
        in_specs=[rowblk(ATT_WIDTH, 0), rowblk(256, 4),
                  pl.BlockSpec((None, ATT_BLOCK, 256), lambda b, i: (b, pv(i), 4)),
                  rowblk(256, _Z0), rowblk(256, _Z0 + 1), rowblk(256, _Z0 + 2), rowblk(256, _Z0 + 3),
                  rowblk(ATT_WIDTH, 1),
                  tab, tab, tabp, tabp,
                  pl.BlockSpec((None, None, 2, GROUP_ROWS, 2 * ATT_BLOCK), lambda b, i: (b, cl(i), 0, 0, 0)),
                  rowblk(ATT_WIDTH, 0),
                  pl.BlockSpec((None, None, ATT_HEADS, 128), lambda b, i: (b, cl(i), 0, 0))] + p_ispecs,
        out_specs=[rowblk(ATT_WIDTH, 0),
                   pl.BlockSpec((None, ATT_BLOCK, 256), lambda b, i: (b, jnp.maximum(i - 1, 0), 0)),
                   rowblk(ATT_WIDTH, 0),
                   pl.BlockSpec((1, 128), lambda b, i: (0, 0))] + p_ospecs,
        out_shape=[jax.ShapeDtypeStruct((B, S, ATT_WIDTH), BF16), jax.ShapeDtypeStruct((B, S, 256), BF16),
                   jax.ShapeDtypeStruct((B, S, ATT_WIDTH), BF16), jax.ShapeDtypeStruct((1, 128), F32)] + p_oshapes,
        input_output_aliases=p_alias,
        scratch_shapes=[pltpu.VMEM((ATT_BLOCK, 256), F32), pltpu.VMEM((ATT_HEADS, 128), F32)] + p_scratch,
        compiler_params=_params(("arbitrary", "arbitrary")),
    )(proj_a, proj_a, proj_a, proj_a, proj_a, proj_a, proj_a, du, cos, sin, cos, sin, p_kept, o_kept, ps_kept, *p_args)
    return tuple(res[:4]) + (list(res[4:]),)


def _outproj_fwd(u2, w_out, x2, g_post, target2, name):
    T, D = x2.shape
    tm = _pick(T, (512, 256, 128))
    last = target2 is not None

    def body(u_ref, w_ref, x_ref, g_ref, *rest):
        y = lax.dot_general(u_ref[...], w_ref[...], (NN, ((), ())), preferred_element_type=F32)
        r = lax.rsqrt(jnp.mean(y * y, axis=-1, keepdims=True) + NORM_EPS)
        xn = x_ref[...] + (y * r) * g_ref[...]
        if last:
            t_ref, y_ref, dx_ref, loss_ref = rest
            err = xn - t_ref[...]
            dx_ref[...] = err * (1.0 / D)
            sq = err * err
            acc = sq[:, 0:128]
            for kk in range(1, D // 128):
                acc = acc + sq[:, 128 * kk:128 * (kk + 1)]
            part = jnp.sum(acc.reshape(tm // 8, 8, 128), axis=0) * (0.5 / D)

            @pl.when(pl.program_id(0) == 0)
            def _():
                loss_ref[...] = jnp.zeros_like(loss_ref)

            loss_ref[...] += part
        else:
            y_ref, xn_ref = rest
            xn_ref[...] = xn
        y_ref[...] = y

    row = pl.BlockSpec((tm, D), lambda i: (i, 0))
    in_specs = [pl.BlockSpec((tm, MIX_WIDTH), lambda i: (i, 0)),
                pl.BlockSpec((MIX_WIDTH, D), lambda i: (0, 0)), row,
                pl.BlockSpec((1, D), lambda i: (0, 0))]
    args = [u2, w_out, x2, g_post]
    out_specs = [row, row]
    out_shape = [jax.ShapeDtypeStruct((T, D), F32), jax.ShapeDtypeStruct((T, D), F32)]
    if last:
        in_specs.append(row)
        args.append(target2)
        out_specs.append(pl.BlockSpec((8, 128), lambda i: (0, 0)))
        out_shape.append(jax.ShapeDtypeStruct((8, 128), F32))
    return pl.pallas_call(
        body, name=name, grid=(T // tm,), in_specs=in_specs, out_specs=out_specs, out_shape=out_shape,
        compiler_params=_params(("arbitrary",)),
    )(*args)


def _outproj_bwd(dxn2, y2, g_post, w_out, u2, name):
    T, D = y2.shape
    N = w_out.shape[0]
    tm = _pick(T, (512, 256, 128))
    nt = T // tm

    def body(dx_ref, y_ref, g_ref, w_ref, u_ref, dg_ref, du_ref, dw_ref, acc, wacc):
        i = pl.program_id(0)

        @pl.when(i == 0)
        def _():
            acc[...] = jnp.zeros_like(acc)
            wacc[...] = jnp.zeros_like(wacc)

        y = y_ref[...]
        dxn = dx_ref[...]
        r = lax.rsqrt(jnp.mean(y * y, axis=-1, keepdims=True) + NORM_EPS)
        n = y * r
        dn = dxn * g_ref[...]
        dy = (r * (dn - n * jnp.mean(dn * n, axis=-1, keepdims=True))).astype(BF16)
        du_ref[...] = lax.dot_general(dy, w_ref[...], (NT, ((), ())), preferred_element_type=F32)
        wacc[...] += lax.dot_general(u_ref[...], dy, (TN, ((), ())), preferred_element_type=F32)
        acc[...] += jnp.sum((dxn * n).reshape(tm // 8, 8, D), axis=0)

        @pl.when(i == nt - 1)
        def _():
            dg_ref[...] = jnp.sum(acc[...], axis=0, keepdims=True)
            dw_ref[...] = wacc[...].astype(BF16)

    row = pl.BlockSpec((tm, D), lambda i: (i, 0))
    wide = pl.BlockSpec((tm, N), lambda i: (i, 0))
    vec = pl.BlockSpec((1, D), lambda i: (0, 0))
    whole = pl.BlockSpec((N, D), lambda i: (0, 0))
    return pl.pallas_call(
        body, name=name, grid=(nt,),
        in_specs=[row, row, vec, pl.BlockSpec((N, D), lambda i: (0, 0), pipeline_mode=pl.Buffered(1)), wide],
        out_specs=[vec, wide, whole],
        out_shape=[jax.ShapeDtypeStruct((1, D), F32), jax.ShapeDtypeStruct((T, N), F32),
                   jax.ShapeDtypeStruct((N, D), BF16)],
        scratch_shapes=[pltpu.VMEM((8, D), F32), pltpu.VMEM((N, D), F32)],
        compiler_params=_params(("arbitrary",)),
    )(dxn2, y2, g_post, w_out, u2)


def _inproj_bwd(pieces, w_t, x2, dxn2, g_pre, name, phase=None):
    T, D = x2.shape
    widths = [p.shape[1] for p in pieces]
    offs = [sum(widths[:i]) for i in range(len(pieces))]
    n_p = len(pieces)
    tm = _pick(T, (256, 128))
    nt = T // tm

    def body(*refs):
        ins, (dx_ref, dg_ref), (acc,), p_in, p_out, p_sems = _split_refs(refs, n_p + 4, 2, 1, phase)
        w_ref, x_ref, dxn_ref, g_ref = ins[n_p:]
        i = pl.program_id(0)
        _hosted_start(phase, p_in, p_out, p_sems, i == 0)

        @pl.when(i == 0)
        def _():
            acc[...] = jnp.zeros_like(acc)

        dh = jnp.zeros((tm, D), F32)
        for p in range(n_p):
            dh = dh + lax.dot_general(ins[p][...], w_ref[offs[p]:offs[p] + widths[p], :], (NN, ((), ())),
                                      preferred_element_type=F32)
        x = x_ref[...]
        r = lax.rsqrt(jnp.mean(x * x, axis=-1, keepdims=True) + NORM_EPS)
        n = x * r
        dn = dh * g_ref[...]
        dx_ref[...] = dxn_ref[...] + r * (dn - n * jnp.mean(dn * n, axis=-1, keepdims=True))
        acc[...] += jnp.sum((dh * n).reshape(tm // 8, 8, D), axis=0)

        @pl.when(i == nt - 1)
        def _():
            dg_ref[...] = jnp.sum(acc[...], axis=0, keepdims=True)

        _hosted_finish(phase, p_in, p_out, p_sems, i == nt - 1)

    row = pl.BlockSpec((tm, D), lambda i: (i, 0))
    vec = pl.BlockSpec((1, D), lambda i: (0, 0))
    p_ispecs, p_ospecs, p_oshapes, p_alias, p_scratch, p_args = _host_phase(phase, n_p + 4, 2)
    res = pl.pallas_call(
        body, name=name, grid=(nt,),
        in_specs=[pl.BlockSpec((tm, w), lambda i: (i, 0)) for w in widths]
        + [pl.BlockSpec((sum(widths), D), lambda i: (0, 0), pipeline_mode=pl.Buffered(1)), row, row, vec] + p_ispecs,
        out_specs=[row, vec] + p_ospecs,
        out_shape=[jax.ShapeDtypeStruct((T, D), F32), jax.ShapeDtypeStruct((1, D), F32)] + p_oshapes,
        input_output_aliases=p_alias,
        scratch_shapes=[pltpu.VMEM((8, D), F32)] + p_scratch,
        compiler_params=_params(("arbitrary",)),
    )(*pieces, w_t, x2, dxn2, g_pre, *p_args)
    return res[0], res[1], list(res[2:])


def _step(x, target, g_pre, g_post, lb_param, g_head, sinks, shards=None, full=None):
    B, S, D = x.shape
    T = B * S
    dist = shards is not None
    first, last = 0, DEPTH - 1
    if dist:
        a_loc, b_loc = shards
        ra, rb = a_loc.shape[1], b_loc.shape[1]
        side = _own_side_blocks()
        placed = lambda loc, nm: _place_own(loc, side, "place_" + nm)
        w_in0 = _run_phase(_gather_ici_phase([a_loc[0]], [placed(a_loc[0], "in0")]), "gather_in0_ici")
        w_in0 = _run_phase(_gather_d2d_phase(w_in0, [ra]), "gather_in0_d2d")[0]
        w_in, w_out = [w_in0, None], [None, None]
    else:
        w_in, w_out = list(full[0]), list(full[1])
    cos, sin = _rope_tables(S)
    saved = []
    xs = x
    loss_part = None
    dxn = None
    for l in range(DEPTH):
        x2 = xs.reshape(T, D)
        proj_h, proj_a, h = _inproj(x2, g_pre[l:l + 1], w_in[l], f"inproj{l}")
        proj_h = proj_h.reshape(B, S, N_H)
        proj_a = proj_a.reshape(B, S, N_A)
        phase = None
        if dist and l == first:
            phase = _gather_ici_phase([a_loc[1], b_loc[0]], [placed(a_loc[1], "in1"), placed(b_loc[0], "out0")])
        if dist and l == last:
            phase = _gather_d2d_phase([w_out1_part], [rb])
        o_h, u, states, got = _hgrn_fwd(proj_h, MIX_WIDTH, lb_param, g_head[l:l + 1], l, f"hgrn_fwd{l}", phase)
        phase = None
        if dist and l == first:
            phase = _merge_phases(_gather_d2d_phase(got, [ra, rb]),
                                  _gather_ici_phase([b_loc[1]], [placed(b_loc[1], "out1")]))
        if dist and l == last:
            w_out[1] = got[0]
        u, kept_a, got = _attn_fwd(proj_a, u, _sink_rows(sinks[l]), cos, sin, f"attn_fwd{l}", phase)
        if dist and l == first:
            w_in[1], w_out[0], w_out1_part = got
        u2 = u.reshape(T, MIX_WIDTH)
        if l < last:
            y, xn = _outproj_fwd(u2, w_out[l], x2, g_post[l:l + 1], None, f"outproj{l}")
            xn = xn.reshape(B, S, D)
        else:
            y, dxn, loss_part = _outproj_fwd(u2, w_out[l], x2, g_post[l:l + 1], target.reshape(T, D), f"outproj{l}")
            xn = None
        saved.append((x2, h, proj_h, proj_a, o_h, u2, states, kept_a, y))
        xs = xn

    dw_in, dw_out = [None] * DEPTH, [None] * DEPTH
    dg_pre, dg_post, dlb, dg_head, dsinks = [], [], [], [], []
    for l in reversed(range(DEPTH)):
        x2, h, proj_h, proj_a, o_h, u2, states, kept_a, y = saved[l]
        dgp, du, dw_out[l] = _outproj_bwd(dxn, y, g_post[l:l + 1], w_out[l], u2, f"outproj_bwd{l}")
        du = du.reshape(B, S, MIX_WIDTH)
        phase = None
        if dist:
            phase = _reduce_d2d_phase([dw_out[l]], [rb])
            if l == first:
                phase = _merge_phases(_reduce_ici_phase([part_in1]), phase)
        dqh, dfh, dih, dzh, dlb_l, dgh, got = _hgrn_bwd(
            proj_h, o_h, du, states, lb_param, g_head[l:l + 1], l, f"hgrn_bwd{l}", phase)
        if dist:
            if l == first:
                sum_in = _chip_sum(part_in1, got[0], "chip_sum_in1", 1)
            part_out = _pair_sum(dw_out[l], got[-1], side, f"pair_sum_out{l}")
        dqa, dkv, dza, dsk, got = _attn_bwd(proj_a, du, kept_a, cos, sin, f"attn_bwd{l}",
                                            _reduce_ici_phase([part_out]) if dist else None)
        if dist:
            sum_out = _chip_sum(part_out, got[0], f"chip_sum_out{l}", l, None if l == last else sum_out)
        dproj = [p.reshape(T, p.shape[-1]) for p in (dqh, dfh, dih, dzh, dqa, dkv, dza)]
        dw_in[l] = _mm_tn(dproj, h, f"wgrad_in{l}")
        phase = None
        if dist and l == last:
            phase = _reduce_d2d_phase([dw_in[l]], [ra])
        if dist and l == first:
            got = _run_phase(_reduce_d2d_phase([dw_in[l]], [ra]), "reduce_in0_d2d")
            part_in0 = _pair_sum(dw_in[l], got[0], side, "pair_sum_in0")
            phase = _reduce_ici_phase([part_in0])
        dxn, dgpre, got = _inproj_bwd(dproj, w_in[l], x2, dxn, g_pre[l:l + 1], f"inproj_bwd{l}", phase)
        if dist and l == last:
            part_in1 = _pair_sum(dw_in[l], got[0], side, "pair_sum_in1")
        if dist and l == first:
            sum_in = _chip_sum(part_in0, got[0], "chip_sum_in0", 0, sum_in)
        dg_pre.append(dgpre)
        dg_post.append(dgp)
        dlb.append(dlb_l)
        dg_head.append(dgh)
        dsinks.append(dsk)
    rev = lambda lst: jnp.concatenate(lst[::-1], axis=0)
    if not dist:
        sum_in, sum_out = jnp.stack(dw_in), jnp.stack(dw_out)
    return (loss_part, dxn.reshape(B, S, D), sum_in, sum_out,
            rev(dg_pre), rev(dg_post), rev(dlb), rev(dg_head), rev(dsinks))


def _me_and_peers():
    x, y, c = lax.axis_index("x"), lax.axis_index("y"), lax.axis_index("c")
    me = 4 * x + 2 * y + c
    peers = []
    for k in range(1, N_DEV):
        px = 1 - x if k & 4 else x
        py = 1 - y if k & 2 else y
        pc = 1 - c if k & 1 else c
        peers.append(((px, py, pc), 4 * px + 2 * py + pc))
    return me, peers


class _Phase:
    def __init__(self, arrays, out_shapes, aliases, n_send, build):
        self.arrays, self.out_shapes, self.aliases = list(arrays), list(out_shapes), dict(aliases)
        self.n_send, self.build = n_send, build

    def scratch(self):
        return [pltpu.SemaphoreType.DMA((self.n_send,)), pltpu.SemaphoreType.DMA((self.n_send,))]

    def _copies(self, in_refs, out_refs, sems, arrivals):
        send_sems, recv_sems = sems
        sends, recvs = self.build(in_refs, out_refs)
        assert len(sends) == self.n_send == len(recvs)
        out = [pltpu.make_async_remote_copy(src_ref=s, dst_ref=d, send_sem=send_sems.at[i], recv_sem=recv_sems.at[i],
                                            device_id=dev, device_id_type=MESH) for i, (s, d, dev) in enumerate(sends)]
        inc = [pltpu.make_async_remote_copy(src_ref=s, dst_ref=r, send_sem=send_sems.at[i], recv_sem=recv_sems.at[i],
                                            device_id=dev, device_id_type=MESH)
               for i, ((s, _, dev), r) in enumerate(zip(sends, recvs))] if arrivals else []
        return out, inc

    def start(self, in_refs, out_refs, sems):
        out, _ = self._copies(in_refs, out_refs, sems, False)
        for cp in out:
            cp.start()

    def finish(self, in_refs, out_refs, sems):
        out, inc = self._copies(in_refs, out_refs, sems, True)
        for cp in inc:
            cp.wait_recv()
        for cp in out:
            cp.wait_send()


_ANY = pl.BlockSpec(memory_space=pl.ANY)


def _host_phase(phase, n_in, n_out):
    if phase is None:
        return [], [], [], {}, [], []
    aliases = {n_in + i: n_out + o for i, o in phase.aliases.items()}
    return ([_ANY] * len(phase.arrays), [_ANY] * len(phase.out_shapes), phase.out_shapes, aliases, phase.scratch(),
            phase.arrays)


def _split_refs(refs, n_in, n_out, n_scr, phase):
    pi = len(phase.arrays) if phase else 0
    po = len(phase.out_shapes) if phase else 0
    a = n_in + pi
    b = a + n_out + po
    return (refs[:n_in], refs[a:a + n_out], refs[b:b + n_scr], refs[n_in:a], refs[a + n_out:b], refs[b + n_scr:])


def _hosted_start(phase, p_in, p_out, p_sems, first):
    if phase is not None:
        @pl.when(first)
        def _():
            phase.start(p_in, p_out, p_sems)


def _hosted_finish(phase, p_in, p_out, p_sems, last):
    if phase is not None:
        @pl.when(last)
        def _():
            phase.finish(p_in, p_out, p_sems)


def _run_phase(phase, name):
    n_in, n_out = len(phase.arrays), len(phase.out_shapes)

    def body(*refs):
        phase.start(refs[:n_in], refs[n_in:n_in + n_out], refs[n_in + n_out:])
        phase.finish(refs[:n_in], refs[n_in:n_in + n_out], refs[n_in + n_out:])

    return pl.pallas_call(
        body, name=name, in_specs=[_ANY] * n_in, out_specs=[_ANY] * n_out,
        out_shape=phase.out_shapes, input_output_aliases=phase.aliases, scratch_shapes=phase.scratch(),
        compiler_params=pltpu.CompilerParams(has_side_effects=True),
    )(*phase.arrays)


def _merge_phases(a, b):
    n_in, n_out = len(a.arrays), len(a.out_shapes)
    aliases = dict(a.aliases)
    aliases.update({n_in + i: n_out + o for i, o in b.aliases.items()})

    def build(ins, outs):
        sa, ra = a.build(ins[:n_in], outs[:n_out])
        sb, rb = b.build(ins[n_in:], outs[n_out:])
        return sa + sb, ra + rb

    return _Phase(a.arrays + b.arrays, a.out_shapes + b.out_shapes, aliases, a.n_send + b.n_send, build)


def _mesh_place():
    x, y, c = lax.axis_index("x"), lax.axis_index("y"), lax.axis_index("c")
    chips = [(x, y), (1 - x, y), (x, 1 - y), (1 - x, 1 - y)]
    num = lambda chip, core: 4 * chip[0] + 2 * chip[1] + core
    return c, chips, num


def _own_side_blocks():
    c, chips, num = _mesh_place()
    return jnp.stack([num(ch, c) for ch in chips]).astype(jnp.int32)


def _rows(ref, r, dev):
    return ref.at[pl.ds(pl.multiple_of(dev * r, 16), r), :]


def _place_own(loc, blocks, name):
    r, D = loc.shape
    tr = _pick(r, (400, 256, 200, 128, 64, 16))

    def body(idx_ref, l_ref, o_ref):
        del idx_ref
        o_ref[...] = l_ref[...]

    return pl.pallas_call(
        body, name=name,
        grid_spec=pltpu.PrefetchScalarGridSpec(
            num_scalar_prefetch=1, grid=(r // tr,),
            in_specs=[pl.BlockSpec((tr, D), lambda i, idx: (i, 0))],
            out_specs=pl.BlockSpec((tr, D), lambda i, idx: (idx[0] * (r // tr) + i, 0))),
        out_shape=jax.ShapeDtypeStruct((N_DEV * r, D), loc.dtype),
        compiler_params=_params(("arbitrary",)),
    )(blocks, loc)


def _gather_ici_phase(locs, fulls):
    rs = [a.shape[0] for a in locs]
    n = len(locs)

    def build(ins, outs):
        c, chips, num = _mesh_place()
        me = num(chips[0], c)
        targets = [((*chips[0], 1 - c), num(chips[0], 1 - c))] + [((*ch, c), num(ch, c)) for ch in chips[1:]]
        sends, recvs = [], []
        for dev, dnum in targets:
            for i, r in enumerate(rs):
                sends.append((ins[i], _rows(outs[i], r, me), dev))
                recvs.append(_rows(outs[i], r, dnum))
        return sends, recvs

    shapes = [jax.ShapeDtypeStruct(a.shape, a.dtype) for a in fulls]
    return _Phase(list(locs) + list(fulls), shapes, {n + i: i for i in range(n)}, 4 * n, build)


def _gather_d2d_phase(fulls, rs):
    def build(ins, outs):
        c, chips, num = _mesh_place()
        sib = (*chips[0], 1 - c)
        sends, recvs = [], []
        for ch in chips[1:]:
            for i, r in enumerate(rs):
                blk = _rows(outs[i], r, num(ch, c))
                sends.append((blk, blk, sib))
                recvs.append(_rows(outs[i], r, num(ch, 1 - c)))
        return sends, recvs

    shapes = [jax.ShapeDtypeStruct(a.shape, a.dtype) for a in fulls]
    return _Phase(fulls, shapes, {i: i for i in range(len(fulls))}, 3 * len(fulls), build)


def _reduce_d2d_phase(grads, rs):
    def build(ins, outs):
        c, chips, num = _mesh_place()
        sib = (*chips[0], 1 - c)
        sends, recvs = [], []
        for j, ch in enumerate(chips):
            for i, r in enumerate(rs):
                sends.append((_rows(ins[i], r, num(ch, 1 - c)), outs[i].at[j], sib))
                recvs.append(outs[i].at[j])
        return sends, recvs

    shapes = [jax.ShapeDtypeStruct((4, r, g.shape[1]), g.dtype) for g, r in zip(grads, rs)]
    return _Phase(grads, shapes, {}, 4 * len(grads), build)


def _reduce_ici_phase(parts):
    def build(ins, outs):
        c, chips, _ = _mesh_place()
        sends, recvs = [], []
        for t in range(1, 4):
            for i in range(len(parts)):
                sends.append((ins[i].at[t], outs[i].at[t - 1], (*chips[t], c)))
                recvs.append(outs[i].at[t - 1])
        return sends, recvs

    shapes = [jax.ShapeDtypeStruct((3,) + p.shape[1:], p.dtype) for p in parts]
    return _Phase(parts, shapes, {}, 3 * len(parts), build)


def _pair_sum(g, got, blocks, name):
    n, r, D = got.shape
    tr = _pick(r, (400, 256, 200, 128, 64, 16))

    def body(idx_ref, g_ref, r_ref, o_ref):
        del idx_ref
        o_ref[...] = (g_ref[...].astype(F32) + r_ref[...].astype(F32)).astype(o_ref.dtype)

    blk = pl.BlockSpec((None, tr, D), lambda j, i, idx: (j, i, 0))
    return pl.pallas_call(
        body, name=name,
        grid_spec=pltpu.PrefetchScalarGridSpec(
            num_scalar_prefetch=1, grid=(n, r // tr),
            in_specs=[pl.BlockSpec((tr, D), lambda j, i, idx: (idx[j] * (r // tr) + i, 0)), blk],
            out_specs=blk),
        out_shape=jax.ShapeDtypeStruct(got.shape, got.dtype),
        compiler_params=_params(("arbitrary", "arbitrary")),
    )(blocks, g, got)


def _chip_sum(p, r, name, layer, into=None):
    _, R, D = p.shape
    tr = _pick(R, (400, 256, 200, 128, 64, 16))

    def body(p_ref, r_ref, *rest):
        acc = p_ref[...].astype(F32)
        for t in range(3):
            acc = acc + r_ref[t].astype(F32)
        rest[-1][...] = acc

    args = [p, r] + ([] if into is None else [into])
    return pl.pallas_call(
        body, name=name, grid=(R // tr,),
        in_specs=[pl.BlockSpec((None, tr, D), lambda i: (0, i, 0)), pl.BlockSpec((3, tr, D), lambda i: (0, i, 0))]
        + ([] if into is None else [_ANY]),
        out_specs=pl.BlockSpec((None, tr, D), lambda i: (layer, i, 0)),
        out_shape=jax.ShapeDtypeStruct((DEPTH, R, D), F32),
        input_output_aliases={} if into is None else {2: 0},
        compiler_params=_params(("parallel",)))(*args)


def _allreduce_small(vec):
    R, C = vec.shape

    def body(v_ref, o_ref, buf, send_sems, recv_sems):
        me, peers = _me_and_peers()
        buf[me] = v_ref[...]
        sends = []
        for k, (pid, _) in enumerate(peers):
            cp = pltpu.make_async_remote_copy(src_ref=v_ref, dst_ref=buf.at[me], send_sem=send_sems.at[k],
                                              recv_sem=recv_sems.at[k], device_id=pid, device_id_type=MESH)
            cp.start()
            sends.append(cp)
        for k, (pid, pnum) in enumerate(peers):
            pltpu.make_async_remote_copy(src_ref=v_ref, dst_ref=buf.at[pnum], send_sem=send_sems.at[k],
                                         recv_sem=recv_sems.at[k], device_id=pid, device_id_type=MESH).wait_recv()
        for cp in sends:
            cp.wait_send()
        acc = buf[0]
        for d in range(1, N_DEV):
            acc = acc + buf[d]
        o_ref[...] = acc

    vm = pl.BlockSpec(memory_space=pltpu.VMEM)
    return pl.pallas_call(
        body, name="allreduce_small",
        in_specs=[vm], out_specs=vm,
        out_shape=jax.ShapeDtypeStruct((R, C), F32),
        scratch_shapes=[pltpu.VMEM((N_DEV, R, C), F32), pltpu.SemaphoreType.DMA((N_DEV - 1,)),
                        pltpu.SemaphoreType.DMA((N_DEV - 1,))],
        compiler_params=pltpu.CompilerParams(has_side_effects=True),
    )(vec)


def _adamw(w, g, m, v, name):
    R, C = w.shape
    tr = _pick(R, (512, 400, 256, 128, 64, 32, 16, 8)) if R >= 8 else R
    c1 = 1.0 - ADAM_B1 ** ADAM_STEP
    c2 = 1.0 - ADAM_B2 ** ADAM_STEP

    def body(w_ref, g_ref, m_ref, v_ref, d_ref, mo_ref, vo_ref):
        gg = g_ref[...]
        mn = ADAM_B1 * m_ref[...] + (1.0 - ADAM_B1) * gg
        vn = ADAM_B2 * v_ref[...] + (1.0 - ADAM_B2) * (gg * gg)
        d_ref[...] = -ADAM_LR * ((mn / c1) / (jnp.sqrt(vn / c2) + ADAM_EPS) + ADAM_WD * w_ref[...])
        mo_ref[...] = mn
        vo_ref[...] = vn

    blk = pl.BlockSpec((tr, C), lambda i: (i, 0))
    sh = jax.ShapeDtypeStruct((R, C), F32)
    return pl.pallas_call(
        body, name=name, grid=(R // tr,), in_specs=[blk] * 4, out_specs=[blk] * 3, out_shape=[sh] * 3,
        compiler_params=_params(("parallel",)),
    )(w, g, m, v)


def _lb_param_grad(lb_param, dlb):
    L, C = lb_param.shape

    def body(p_ref, d_ref, o_ref):
        lbp = p_ref[...]
        d = d_ref[...]
        mx = jnp.max(lbp, axis=0, keepdims=True)
        e = jnp.exp(lbp - mx)
        p = e / jnp.sum(e, axis=0, keepdims=True)
        tot = jnp.sum(d, axis=0, keepdims=True)
        dps = []
        rest = tot
        for j in range(L):
            dps.append(rest - tot if j == 0 else rest)
            rest = rest - d[j:j + 1]
        dp = jnp.concatenate(dps, axis=0)
        o_ref[...] = p * (dp - jnp.sum(p * dp, axis=0, keepdims=True))

    vm = pl.BlockSpec(memory_space=pltpu.VMEM)
    return pl.pallas_call(body, name="lb_param_grad", in_specs=[vm, vm], out_specs=vm,
                          out_shape=jax.ShapeDtypeStruct((L, C), F32))(lb_param, dlb)


def _pack_small(loss_part, dg_pre, dg_post, dlb, dg_head, dsinks):
    pad8 = lambda a: jnp.pad(a.reshape(-1, 128), ((0, 8 - DEPTH), (0, 0)))
    rows = [dg_pre.reshape(-1, 128), dg_post.reshape(-1, 128), dlb.reshape(-1, 128), pad8(dg_head), pad8(dsinks),
            loss_part]
    return jnp.concatenate(rows, axis=0)


def _unpack_small(vec):
    n = DEPTH * D_MODEL // 128
    o = 0
    dg_pre = vec[o:o + n].reshape(DEPTH, D_MODEL); o += n
    dg_post = vec[o:o + n].reshape(DEPTH, D_MODEL); o += n
    dlb = vec[o:o + n].reshape(DEPTH, HG_WIDTH); o += n
    dg_head = vec[o:o + DEPTH]; o += 8
    dsinks = vec[o:o + DEPTH, :ATT_HEADS]; o += 8
    loss = jnp.sum(vec[o:o + 8])
    return loss, dg_pre, dg_post, dlb, dg_head, dsinks


def kernel(x, w_in, w_out, g_pre, g_post, lb_param, g_head, sinks, loss_target, m_w_in, m_w_out, m_g_pre, m_g_post, m_lb_param, m_g_head, m_sinks, v_w_in, v_w_out, v_g_pre, v_g_post, v_lb_param, v_g_head, v_sinks):
    tr = lambda a: jnp.swapaxes(a, 1, 2)
    w_in_t = tr(w_in)
    (loss_part, dx, gw_in_t, gw_out, dg_pre, dg_post, dlb, dg_head, dsinks) = _step(
        x, loss_target, g_pre, g_post, lb_param, g_head, sinks, shards=(w_in_t.astype(BF16), w_out.astype(BF16)))

    small = _allreduce_small(_pack_small(loss_part, dg_pre, dg_post, dlb, dg_head, dsinks))
    loss, gg_pre, gg_post, gdlb, gg_head, gsinks = _unpack_small(small)
    glb = _lb_param_grad(lb_param, gdlb)

    grads = [gw_in_t, gw_out, gg_pre, gg_post, glb, gg_head, gsinks]
    ws = [w_in_t, w_out, g_pre, g_post, lb_param, g_head, sinks]
    ms = [tr(m_w_in), m_w_out, m_g_pre, m_g_post, m_lb_param, m_g_head, m_sinks]
    vs = [tr(v_w_in), v_w_out, v_g_pre, v_g_post, v_lb_param, v_g_head, v_sinks]
    names = ["w_in", "w_out", "g_pre", "g_post", "lb_param", "g_head", "sinks"]
    deltas, new_m, new_v = [], [], []
    for w, g, m, v, nm in zip(ws, grads, ms, vs, names):
        sh = w.shape
        two = lambda a: a.reshape(-1, sh[-1])
        d, mn, vn = _adamw(two(w), two(g), two(m), two(v), "adamw_" + nm)
        deltas.append(d.reshape(sh))
        new_m.append(mn.reshape(sh))
        new_v.append(vn.reshape(sh))
    grads[0], deltas[0], new_m[0], new_v[0] = tr(grads[0]), tr(deltas[0]), tr(new_m[0]), tr(new_v[0])
    return (loss, dx, *grads, *deltas, *new_m, *new_v)
```

```python
import math

import numpy as np
import jax
import jax.numpy as jnp
from jax import lax
from jax.experimental import pallas as pl
from jax.experimental.pallas import tpu as pltpu

F32 = jnp.float32
BF16 = jnp.bfloat16

D_MODEL = 1024
DEPTH = 2
HG_HEADS = 8
HG_DIM = 128
HG_WIDTH = HG_HEADS * HG_DIM
CHUNK = 64
ATT_HEADS = 16
ATT_DIM = 64
ATT_WIDTH = ATT_HEADS * ATT_DIM
KV_WIDTH = 128
ATT_BLOCK = 128
ATT_SCALE = 1.0 / math.sqrt(ATT_DIM)
ROPE_THETA = 10000.0
NORM_EPS = 1e-6
NEG_INF = -1e30
LB_FLOOR = 1e-20
N_H = 4 * HG_WIDTH
N_A = 2 * ATT_WIDTH + 2 * KV_WIDTH
IN_WIDTH = N_H + N_A
MIX_WIDTH = HG_WIDTH + ATT_WIDTH

ADAM_LR = 0.001
ADAM_B1 = 0.9
ADAM_B2 = 0.999
ADAM_EPS = 1e-08
ADAM_WD = 0.01
ADAM_STEP = 10

N_DEV = 8
MESH = pl.DeviceIdType.MESH
VMEM_LIMIT = 56 * 1024 * 1024

NN = ((1,), (0,))
NT = ((1,), (1,))
TN = ((0,), (0,))


def _dot(a, b, dims):
    return lax.dot_general(a.astype(BF16), b.astype(BF16), (dims, ((), ())), preferred_element_type=F32)


def _params(sem=None, **kw):
    return pltpu.CompilerParams(dimension_semantics=sem, vmem_limit_bytes=VMEM_LIMIT, **kw)


def _sigmoids(x):
    e = jnp.exp(-jnp.abs(x))
    r = 1.0 / (1.0 + e)
    er = e * r
    pos = x >= 0.0
    return jnp.where(pos, r, er), jnp.where(pos, er, r)


def _silu(x):
    return x * _sigmoids(x)[0]


def _silu_and_grad(x):
    s, ns = _sigmoids(x)
    return x * s, s * (1.0 + x * ns)


def _pick(n, prefs):
    for p in prefs:
        if n % p == 0:
            return p
    return n


def _inproj(x2, g, w, name):
    T, D = x2.shape
    tm = _pick(T, (512, 256, 128))
    nchunk = 1024

    def body(x_ref, g_ref, w_ref, oh_ref, oa_ref, h_ref):
        x = x_ref[...]
        r = lax.rsqrt(jnp.mean(x * x, axis=-1, keepdims=True) + NORM_EPS)
        h = ((x * r) * g_ref[...]).astype(BF16)
        h_ref[...] = h
        for j in range(0, N_H, nchunk):
            oh_ref[:, j:j + nchunk] = lax.dot_general(h, w_ref[j:j + nchunk, :], (NT, ((), ())),
                                                      preferred_element_type=F32)
        for j in range(0, N_A, N_A // 2):
            oa_ref[:, j:j + N_A // 2] = lax.dot_general(h, w_ref[N_H + j:N_H + j + N_A // 2, :], (NT, ((), ())),
                                                        preferred_element_type=F32)

    row = lambda w_: pl.BlockSpec((tm, w_), lambda i: (i, 0))
    return pl.pallas_call(
        body, name=name,
        grid=(T // tm,),
        in_specs=[row(D), pl.BlockSpec((1, D), lambda i: (0, 0)),
                  pl.BlockSpec((IN_WIDTH, D), lambda i: (0, 0), pipeline_mode=pl.Buffered(1))],
        out_specs=[row(N_H), row(N_A), row(D)],
        out_shape=[jax.ShapeDtypeStruct((T, N_H), F32), jax.ShapeDtypeStruct((T, N_A), F32),
                   jax.ShapeDtypeStruct((T, D), BF16)],
        compiler_params=_params(("parallel",)),
    )(x2, g, w)


def _mm_tn(pieces, b, name, out_dtype=BF16):
    T, m = b.shape
    tn = 256
    counts = [p.shape[1] // tn for p in pieces]
    starts = [sum(counts[:i]) for i in range(len(pieces))]
    n_p = len(pieces)

    def body(*refs):
        b_ref, o_ref = refs[n_p], refs[n_p + 1]
        i = pl.program_id(0)
        for p in range(n_p):
            @pl.when((i >= starts[p]) & (i < starts[p] + counts[p]))
            def _(p=p):
                o_ref[...] = lax.dot_general(refs[p][...], b_ref[...], (TN, ((), ())),
                                             preferred_element_type=F32).astype(out_dtype)

    piece_spec = lambda s, c: pl.BlockSpec((T, tn), lambda i: (0, jnp.clip(i - s, 0, c - 1)))
    return pl.pallas_call(
        body, name=name,
        grid=(sum(counts),),
        in_specs=[piece_spec(s, c) for s, c in zip(starts, counts)]
        + [pl.BlockSpec((T, m), lambda i: (0, 0), pipeline_mode=pl.Buffered(1))],
        out_specs=pl.BlockSpec((tn, m), lambda i: (i, 0)),
        out_shape=jax.ShapeDtypeStruct((sum(counts) * tn, m), out_dtype),
        compiler_params=_params(("arbitrary",)),
    )(*pieces, b)


_LEVELS = (0, 1, 2, 4, 8, 16, 32)
_CUM_L = (2, 4, 8, 16, 32, 64)
_ALL_KINDS = tuple(("c", L) for L in _CUM_L) + tuple(("r", L) for L in _CUM_L)
_MXU_KINDS = (("c", 2), ("c", 4), ("c", CHUNK), ("r", 2), ("r", 4))
N_CUM = len(_ALL_KINDS) * CHUNK
N_CUM_F = len(_MXU_KINDS) * CHUNK


def _cum_matrices():
    t = np.arange(CHUNK)[:, None]
    r = np.arange(CHUNK)[None, :]

    def mat(kind):
        c, L = kind
        return ((r // L == t // L) & ((r <= t) if c == "c" else (r > t))).astype(np.float32)

    fwd = np.concatenate([mat(kd) for kd in _MXU_KINDS], axis=0)
    full = np.concatenate([mat(kd) for kd in _ALL_KINDS], axis=0)
    return jnp.asarray(fwd, BF16), jnp.asarray(full.T.copy(), BF16)


def _level_masks():
    t = np.arange(CHUNK)[:, None]
    s = np.arange(CHUNK)[None, :]
    ms = []
    for L in _LEVELS:
        if L == 0:
            ms.append(t == s)
        else:
            ms.append((t // (2 * L) == s // (2 * L)) & ((t // L) % 2 == 1) & ((s // L) % 2 == 0))
    return jnp.asarray(np.stack(ms).astype(np.float32))


def _split3(x):
    hi = x.astype(BF16)
    r1 = x - hi.astype(F32)
    mid = r1.astype(BF16)
    lo = (r1 - mid.astype(F32)).astype(BF16)
    return hi, mid, lo


def _cum3(ts, x, terms=3):
    d = lambda p: lax.dot_general(ts, p, (NN, ((), ())), preferred_element_type=F32)
    return sum(d(p) for p in _split3(x)[:terms])


def _lb_terms(lbp, layer):
    mx = jnp.max(lbp, axis=0, keepdims=True)
    e = jnp.exp(lbp - mx)
    p = e / jnp.sum(e, axis=0, keepdims=True)
    cum = p[0:1]
    for j in range(1, layer + 1):
        cum = cum + p[j:j + 1]
    lb = cum - p[0:1]
    lbf = jnp.maximum(lb, LB_FLOOR)
    return dict(lbf=lbf, one_m=1.0 - lb, kcorr=lb - lbf, ind=jnp.where(lb > LB_FLOOR, 1.0, 0.0))


def _gate(x, lt):
    sig, nsig = _sigmoids(x)
    f = lt["lbf"] + lt["one_m"] * sig
    return jnp.log(f), lt["one_m"] * nsig + lt["kcorr"], f, sig, nsig


def _ck(x, ci):
    return x[ci * CHUNK:(ci + 1) * CHUNK]


def _block_cums(ts, g, nc):
    cs = [_cum3(ts, _ck(g, ci), terms=2) for ci in range(nc)]
    out = {kind: jnp.concatenate([c[CHUNK * i:CHUNK * (i + 1)] for c in cs], axis=0)
           for i, kind in enumerate(_MXU_KINDS)}
    b = out[("c", CHUNK)]
    ng = CHUNK // 8
    last = b.reshape(nc, ng, 8, HG_DIM)[:, :, 7:8, :]
    zero = jnp.zeros((nc, 1, 1, HG_DIM), F32)

    def spread(groups):
        return jnp.broadcast_to(jnp.concatenate(groups, axis=1), (nc, ng, 8, HG_DIM)).reshape(nc * CHUNK, HG_DIM)

    def get(kind):
        if kind in out:
            return out[kind]
        c, L = kind
        nb = L // 8
        first = lambda r: (r // nb) * nb
        if c == "c":
            return b - spread([last[:, first(r) - 1:first(r)] if r >= nb else zero for r in range(ng)])
        return spread([last[:, first(r) + nb - 1:first(r) + nb] for r in range(ng)]) - b

    return get


def _level_factors(cums, g, L):
    if L == 0:
        return None, None
    if L == 1:
        return jnp.exp(g), None
    return jnp.exp(cums(("c", L))), jnp.exp(cums(("r", L)))


def _mul(a, e):
    return a if e is None else a * e


def _hg_block_fwd(qf, k, v, g, ts, m_ref, nc):
    cums = _block_cums(ts, g, nc)
    amat = [jnp.zeros((CHUNK, CHUNK), F32)] * nc
    for li, L in enumerate(_LEVELS):
        eq, ek = _level_factors(cums, g, L)
        ql, kl, m = _mul(qf, eq), _mul(k, ek), m_ref[li]
        amat = [amat[ci] + _dot(_ck(ql, ci), _ck(kl, ci), NT) * m for ci in range(nc)]
    b = cums(("c", CHUNK))
    kst = k * jnp.exp(cums(("r", CHUNK)))
    o = [_dot(amat[ci], _ck(v, ci), NN) for ci in range(nc)]
    kv = [_dot(_ck(v, ci), _ck(kst, ci), TN) for ci in range(nc)]
    dec = [jnp.exp(b[(ci + 1) * CHUNK - 1:(ci + 1) * CHUNK, :]) for ci in range(nc)]
    return o, dec, kv, qf * jnp.exp(b), amat


def _hg_block_bwd(qf, k, v, g, do, amat, ts, m_ref, nc):
    cums = _block_cums(ts, g, nc)
    dcs = {}
    da = [_dot(_ck(do, ci), _ck(v, ci), NT) for ci in range(nc)]
    dq = jnp.zeros_like(qf)
    dk = jnp.zeros_like(qf)
    dg = jnp.zeros_like(qf)
    for li, L in enumerate(_LEVELS):
        eq, ek = _level_factors(cums, g, L)
        ql, kl, m = _mul(qf, eq), _mul(k, ek), m_ref[li]
        qlb, klb = ql.astype(BF16), kl.astype(BF16)
        dal = [(da[ci] * m).astype(BF16) for ci in range(nc)]
        dql = jnp.concatenate([_dot(dal[ci], _ck(klb, ci), NN) for ci in range(nc)], axis=0)
        dkl = jnp.concatenate([_dot(dal[ci], _ck(qlb, ci), TN) for ci in range(nc)], axis=0)
        dq = dq + _mul(dql, eq)
        dk = dk + _mul(dkl, ek)
        if L == 1:
            dg = dg + dql * ql
        elif L > 1:
            dcs[("c", L)] = (dql * ql).astype(BF16)
            dcs[("r", L)] = (dkl * kl).astype(BF16)
    b = cums(("c", CHUNK))
    e64 = jnp.exp(b)
    er64 = jnp.exp(cums(("r", CHUNK)))
    qb = qf * e64
    return dict(dq=dq, dk=dk, dg=dg, dcs=dcs, e64=e64, er64=er64, qb=qb, kst=k * er64,
                dv=[_dot(amat[ci], _ck(do, ci), TN) for ci in range(nc)],
                dec=[jnp.exp(b[(ci + 1) * CHUNK - 1:(ci + 1) * CHUNK, :]) for ci in range(nc)],
                qd=[_dot(_ck(do, ci), _ck(qb, ci), TN) for ci in range(nc)])


def _hg_state_bwd(w, v, do, starts, ends, tst, nc):
    dqb = jnp.concatenate([_dot(_ck(do, ci), starts[ci], NN) for ci in range(nc)], axis=0)
    dkst = jnp.concatenate([_dot(_ck(v, ci), ends[ci], NN) for ci in range(nc)], axis=0)
    dq = w["dq"] + dqb * w["e64"]
    dk = w["dk"] + dkst * w["er64"]
    dv = jnp.concatenate([w["dv"][ci] + _dot(_ck(w["kst"], ci), ends[ci], NT) for ci in range(nc)], axis=0)
    trow = lax.broadcasted_iota(jnp.int32, (CHUNK, 1), 0)
    dtot = jnp.concatenate(
        [jnp.where(trow == CHUNK - 1, jnp.sum(ends[ci] * starts[ci], axis=0, keepdims=True) * w["dec"][ci], 0.0)
         for ci in range(nc)], axis=0)
    dcs = dict(w["dcs"])
    dcs[("c", CHUNK)] = (dqb * w["qb"] + dtot).astype(BF16)
    dcs[("r", CHUNK)] = (dkst * w["kst"]).astype(BF16)
    dgs = [_dot(tst, jnp.concatenate([_ck(dcs[kind], ci) for kind in _ALL_KINDS], axis=0), NN) for ci in range(nc)]
    return dq, dk, dv, w["dg"] + jnp.concatenate(dgs, axis=0)


def _hgrn_fwd(proj_h, u_rows, lb_param, g_head, layer, name, phase=None):
    B, S, _ = proj_h.shape
    sb = _pick(S, (2048, 1024, 512, 256, 128, 64))
    nc = sb // CHUNK
    ts, _ = _cum_matrices()

    def body(*refs):
        ins, outs, (st,), p_in, p_out, p_sems = _split_refs(refs, 8, 4, 1, phase)
        q_ref, f_ref, i_ref, z_ref, lbp_ref, gh_ref, ts_ref, m_ref = ins
        o_ref, u_ref, sts_ref, am_ref = outs
        h_id, b_id, s_id = pl.program_id(0), pl.program_id(1), pl.program_id(2)
        _hosted_start(phase, p_in, p_out, p_sems, (h_id == 0) & (b_id == 0) & (s_id == 0))

        @pl.when(s_id == 0)
        def _():
            st[...] = jnp.zeros_like(st)

        lt = _lb_terms(lbp_ref[...], layer)
        tsv = ts_ref[...]
        gh = gh_ref[...]
        logf, k = _gate(f_ref[...], lt)[:2]
        o_part, dec, kv, qb, amat = _hg_block_fwd(_silu(q_ref[...]), k, i_ref[...], logf, tsv, m_ref, nc)
        for ci in range(nc):
            am_ref[ci] = amat[ci].astype(BF16)
        cur = st[...]
        starts = []
        for ci in range(nc):
            sts_ref[ci] = cur
            starts.append(cur)
            cur = cur * dec[ci] + kv[ci]
        st[...] = cur
        o = jnp.concatenate([o_part[ci] + _dot(_ck(qb, ci), starts[ci], NT) for ci in range(nc)], axis=0)
        o_ref[...] = o
        r = lax.rsqrt(jnp.mean(o * o, axis=-1, keepdims=True) + NORM_EPS)
        u_ref[...] = (((o * r) * gh) * _silu(z_ref[...])).astype(BF16)
        _hosted_finish(phase, p_in, p_out, p_sems, (h_id == HG_HEADS - 1) & (b_id == B - 1) & (s_id == S // sb - 1))

    col = lambda base: pl.BlockSpec((None, sb, HG_DIM), lambda h, b, s: (b, s, base + h))
    p_ispecs, p_ospecs, p_oshapes, p_alias, p_scratch, p_args = _host_phase(phase, 8, 4)
    res = pl.pallas_call(
        body, name=name,
        grid=(HG_HEADS, B, S // sb),
        in_specs=[col(0), col(HG_HEADS), col(2 * HG_HEADS), col(3 * HG_HEADS),
                  pl.BlockSpec((DEPTH, HG_DIM), lambda h, b, s: (0, h)),
                  pl.BlockSpec((1, HG_DIM), lambda h, b, s: (0, 0)),
                  pl.BlockSpec((N_CUM_F, CHUNK), lambda h, b, s: (0, 0)),
                  pl.BlockSpec((len(_LEVELS), CHUNK, CHUNK), lambda h, b, s: (0, 0, 0))] + p_ispecs,
        out_specs=[col(0), col(0),
                   pl.BlockSpec((None, None, nc, HG_DIM, HG_DIM), lambda h, b, s: (b, h, s, 0, 0)),
                   pl.BlockSpec((None, None, nc, CHUNK, CHUNK), lambda h, b, s: (b, h, s, 0, 0))] + p_ospecs,
        out_shape=[jax.ShapeDtypeStruct((B, S, HG_WIDTH), F32),
                   jax.ShapeDtypeStruct((B, S, u_rows), BF16),
                   jax.ShapeDtypeStruct((B, HG_HEADS, S // CHUNK, HG_DIM, HG_DIM), F32),
                   jax.ShapeDtypeStruct((B, HG_HEADS, S // CHUNK, CHUNK, CHUNK), BF16)] + p_oshapes,
        input_output_aliases=p_alias,
        scratch_shapes=[pltpu.VMEM((HG_DIM, HG_DIM), F32)] + p_scratch,
        compiler_params=_params(("arbitrary", "arbitrary", "arbitrary")),
    )(proj_h, proj_h, proj_h, proj_h, lb_param, g_head, ts, _level_masks(), *p_args)
    return res[0], res[1], (res[2], res[3]), list(res[4:])


def _hgrn_bwd(proj_h, o_h, du, kept, lb_param, g_head, layer, name, phase=None):
    B, S, _ = proj_h.shape
    sb = _pick(S, (512, 256, 128, 64))
    nc = sb // CHUNK
    ns = S // sb
    ts, tst = _cum_matrices()

    def body(*refs):
        ins, outs, (dst,), p_in, p_out, p_sems = _split_refs(refs, 13, 6, 1, phase)
        q_ref, f_ref, i_ref, z_ref, o_ref, du_ref, sts_ref, am_ref, lbp_ref, gh_ref, ts_ref, tst_ref, m_ref = ins
        dq_ref, df_ref, di_ref, dz_ref, dlb_ref, dgh_ref = outs
        h_id, b_id, s_id = pl.program_id(0), pl.program_id(1), pl.program_id(2)
        _hosted_start(phase, p_in, p_out, p_sems, (h_id == 0) & (b_id == 0) & (s_id == 0))

        @pl.when(s_id == 0)
        def _():
            dst[...] = jnp.zeros_like(dst)

        @pl.when((b_id == 0) & (s_id == 0))
        def _():
            dlb_ref[...] = jnp.zeros_like(dlb_ref)

        @pl.when((h_id == 0) & (b_id == 0) & (s_id == 0))
        def _():
            dgh_ref[...] = jnp.zeros_like(dgh_ref)

        lt = _lb_terms(lbp_ref[...], layer)
        gh = gh_ref[...]
        tsv = ts_ref[...]
        tstv = tst_ref[...]
        logf, k, f, sig, nsig = _gate(f_ref[...], lt)
        o = o_ref[...]
        dub = du_ref[...]
        r = lax.rsqrt(jnp.mean(o * o, axis=-1, keepdims=True) + NORM_EPS)
        n = o * r
        sg, sg_grad = _silu_and_grad(z_ref[...])
        dz_ref[...] = (dub * (n * gh) * sg_grad).astype(BF16)
        dgh_ref[...] += jnp.sum(dub * sg * n, axis=0, keepdims=True)
        dn = dub * sg * gh
        do = r * (dn - n * jnp.mean(dn * n, axis=-1, keepdims=True))
        v = i_ref[...]
        qf, qf_grad = _silu_and_grad(q_ref[...])
        w = _hg_block_bwd(qf, k, v, logf, do, [am_ref[ci] for ci in range(nc)], tsv, m_ref, nc)
        cur = dst[...]
        ends = [None] * nc
        for ci in reversed(range(nc)):
            ends[ci] = cur
            cur = cur * w["dec"][ci] + w["qd"][ci]
        dst[...] = cur
        dq, dk, dv, dg = _hg_state_bwd(w, v, do, [sts_ref[ci] for ci in range(nc)], ends, tstv, nc)
        di_ref[...] = dv.astype(BF16)
        dq_ref[...] = (dq * qf_grad).astype(BF16)
        scaled = (dg - f * dk) / f
        df_ref[...] = (scaled * lt["one_m"] * sig * nsig).astype(BF16)
        dlb_ref[...] += jnp.sum(scaled * (lt["ind"] - sig), axis=0, keepdims=True)
        _hosted_finish(phase, p_in, p_out, p_sems, (h_id == HG_HEADS - 1) & (b_id == B - 1) & (s_id == ns - 1))

    col = lambda base: pl.BlockSpec((None, sb, HG_DIM), lambda h, b, s: (b, ns - 1 - s, base + h))
    out_col = pl.BlockSpec((None, sb, HG_DIM), lambda h, b, s: (b, ns - 1 - s, h))
    dt = jax.ShapeDtypeStruct((B, S, HG_WIDTH), BF16)
    p_ispecs, p_ospecs, p_oshapes, p_alias, p_scratch, p_args = _host_phase(phase, 13, 6)
    res = pl.pallas_call(
        body, name=name,
        grid=(HG_HEADS, B, ns),
        in_specs=[col(0), col(HG_HEADS), col(2 * HG_HEADS), col(3 * HG_HEADS), col(0), col(0),
                  pl.BlockSpec((None, None, nc, HG_DIM, HG_DIM), lambda h, b, s: (b, h, ns - 1 - s, 0, 0)),
                  pl.BlockSpec((None, None, nc, CHUNK, CHUNK), lambda h, b, s: (b, h, ns - 1 - s, 0, 0)),
                  pl.BlockSpec((DEPTH, HG_DIM), lambda h, b, s: (0, h)),
                  pl.BlockSpec((1, HG_DIM), lambda h, b, s: (0, 0)),
                  pl.BlockSpec((N_CUM_F, CHUNK), lambda h, b, s: (0, 0)),
                  pl.BlockSpec((CHUNK, N_CUM), lambda h, b, s: (0, 0)),
                  pl.BlockSpec((len(_LEVELS), CHUNK, CHUNK), lambda h, b, s: (0, 0, 0))] + p_ispecs,
        out_specs=[out_col, out_col, out_col, out_col,
                   pl.BlockSpec((1, HG_DIM), lambda h, b, s: (0, h)),
                   pl.BlockSpec((1, HG_DIM), lambda h, b, s: (0, 0))] + p_ospecs,
        out_shape=[dt, dt, dt, dt, jax.ShapeDtypeStruct((1, HG_WIDTH), F32),
                   jax.ShapeDtypeStruct((1, HG_DIM), F32)] + p_oshapes,
        input_output_aliases=p_alias,
        scratch_shapes=[pltpu.VMEM((HG_DIM, HG_DIM), F32)] + p_scratch,
        compiler_params=_params(("arbitrary", "arbitrary", "arbitrary")),
    )(proj_h, proj_h, proj_h, proj_h, o_h, du, kept[0], kept[1], lb_param, g_head, ts, tst, _level_masks(), *p_args)
    return tuple(res[:6]) + (list(res[6:]),)


def _rope_tables(S):
    half = ATT_DIM // 2
    inv_freq = ROPE_THETA ** (-jnp.arange(half, dtype=F32) / half)
    ang = jnp.arange(S).astype(F32)[:, None] * inv_freq[None, :]
    cos = jnp.cos(ang)
    sin = jnp.sin(ang)
    cos = jnp.concatenate([cos, cos, cos, cos], axis=1)
    sin = jnp.concatenate([-sin, sin, -sin, sin], axis=1)
    return cos, sin


def _attn_common():
    lane = lax.broadcasted_iota(jnp.int32, (1, 2 * ATT_DIM), 1)
    first_half = (lane % ATT_DIM) < (ATT_DIM // 2)
    left = lane < ATT_DIM

    def swap(x):
        return jnp.where(first_half, pltpu.roll(x, 128 - ATT_DIM // 2, 1), pltpu.roll(x, ATT_DIM // 2, 1))

    def rope(x, cos, sin):
        return x * cos + swap(x) * sin

    def rope_bwd(dy, cos, sin):
        return dy * cos + swap(dy * sin)

    def dup(x):
        xs = pltpu.roll(x, ATT_DIM, 1)
        return [jnp.where(left, x, xs), jnp.where(left, xs, x)]

    return left, rope, rope_bwd, dup


GROUP = ATT_HEADS // 2
GROUP_ROWS = GROUP * ATT_BLOCK


def _attn_bias(i):
    r = lax.broadcasted_iota(jnp.int32, (ATT_BLOCK, 2 * ATT_BLOCK), 0)
    c = lax.broadcasted_iota(jnp.int32, (ATT_BLOCK, 2 * ATT_BLOCK), 1)
    ok = (c > r) & (c <= r + ATT_BLOCK) & ((c >= ATT_BLOCK) | (i > 0))
    return jnp.where(ok, 0.0, NEG_INF)


def _stack_heads(pairs, left):
    rows = []
    for x in pairs:
        rows += [jnp.where(left, x, 0.0), jnp.where(left, 0.0, x)]
    return jnp.concatenate(rows, axis=0)


def _unstack_heads(y, left, pp):
    r0 = 2 * pp * ATT_BLOCK
    return jnp.where(left, y[r0:r0 + ATT_BLOCK], y[r0 + ATT_BLOCK:r0 + 2 * ATT_BLOCK])


def _row_sums(x):
    return _dot(x, jnp.ones((x.shape[1], 128), BF16), NN)


def _attn_probs(qs, kd, vd, sink, bias):
    n = range(len(qs))
    rows = qs[0].shape[0]
    s = [(_dot(qs[j], kd[j], NT).reshape(rows // ATT_BLOCK, ATT_BLOCK, 2 * ATT_BLOCK) * ATT_SCALE + bias[None])
         .reshape(rows, 2 * ATT_BLOCK) for j in n]
    m = [jnp.max(jnp.maximum(jnp.maximum(s[j][:, :128], s[j][:, 128:]), sink[j]), axis=-1, keepdims=True) for j in n]
    pu = [jnp.exp(s[j] - m[j]) for j in n]
    es = [jnp.exp(sink[j] - m[j]) for j in n]
    ones = jnp.ones((2 * ATT_BLOCK, 128), BF16)
    ov = [_dot(pu[j], jnp.concatenate([vd[j].astype(BF16), ones], axis=1), NN) for j in n]
    inv = [1.0 / (ov[j][:, 128:] + es[j]) for j in n]
    return ([pu[j] * jnp.concatenate([inv[j], inv[j]], axis=1) for j in n], [es[j] * inv[j] for j in n],
            [ov[j][:, :128] * inv[j] for j in n])


def _sink_rows(sinks_l):
    return jnp.broadcast_to(jnp.repeat(sinks_l, ATT_BLOCK)[:, None], (ATT_HEADS * ATT_BLOCK, 128))


_Z0 = (2 * ATT_WIDTH + 2 * KV_WIDTH - ATT_WIDTH) // 256


def _attn_fwd(proj_a, u, sinks_l, cos, sin, name, phase=None):
    B, S, _ = proj_a.shape
    nb = S // ATT_BLOCK

    def body(*refs):
        ins, (u_ref, p_ref, o_ref, ps_ref), _, p_in, p_out, p_sems = _split_refs(refs, 13, 4, 0, phase)
        q_ref, kvc_ref, kvp_ref, z0, z1, z2, z3, cos_ref, sin_ref, cosp_ref, sinp_ref, sinks_ref, _ = ins
        i = pl.program_id(1)
        _hosted_start(phase, p_in, p_out, p_sems, (pl.program_id(0) == 0) & (i == 0))
        left, rope, _, dup = _attn_common()
        cos_c, sin_c = cos_ref[...], sin_ref[...]
        kvc = kvc_ref[...]
        kvp = kvp_ref[...]
        kw = jnp.concatenate([rope(kvp[:, :KV_WIDTH], cosp_ref[...], sinp_ref[...]),
                              rope(kvc[:, :KV_WIDTH], cos_c, sin_c)], axis=0)
        vw = jnp.concatenate([kvp[:, KV_WIDTH:], kvc[:, KV_WIDTH:]], axis=0)
        kd, vd = dup(kw), dup(vw)
        bias = _attn_bias(i)
        zs = (z0, z1, z2, z3)
        pairs = [range(4 * kvh, 4 * kvh + 4) for kvh in range(2)]
        qs = [_stack_heads([rope(q_ref[:, 128 * pr:128 * (pr + 1)], cos_c, sin_c) for pr in pairs[kvh]], left)
              for kvh in range(2)]
        sink = [sinks_ref[kvh * GROUP_ROWS:(kvh + 1) * GROUP_ROWS, :] for kvh in range(2)]
        p, ps, o = _attn_probs(qs, kd, vd, sink, bias)
        eye = (lax.broadcasted_iota(jnp.int32, (ATT_BLOCK, 128), 0)
               == lax.broadcasted_iota(jnp.int32, (ATT_BLOCK, 128), 1))
        for kvh in range(2):
            p_ref[kvh] = p[kvh].astype(BF16)
            for g in range(GROUP):
                blk = ps[kvh][g * ATT_BLOCK:(g + 1) * ATT_BLOCK, :]
                ps_ref[kvh * GROUP + g:kvh * GROUP + g + 1, :] = jnp.sum(jnp.where(eye, blk, 0.0), axis=0, keepdims=True)
            for pp, pr in enumerate(pairs[kvh]):
                z = zs[pr // 2][:, 128 * (pr % 2):128 * (pr % 2 + 1)]
                o128 = _unstack_heads(o[kvh], left, pp)
                o_ref[:, 128 * pr:128 * (pr + 1)] = o128.astype(BF16)
                u_ref[:, 128 * pr:128 * (pr + 1)] = (o128 * _silu(z)).astype(BF16)
        _hosted_finish(phase, p_in, p_out, p_sems, (pl.program_id(0) == B - 1) & (i == nb - 1))

    rowblk = lambda w, cb: pl.BlockSpec((None, ATT_BLOCK, w), lambda b, i: (b, i, cb))
    tab = pl.BlockSpec((ATT_BLOCK, 128), lambda b, i: (i, 0))
    tabp = pl.BlockSpec((ATT_BLOCK, 128), lambda b, i: (jnp.maximum(i - 1, 0), 0))
    p_ispecs, p_ospecs, p_oshapes, p_alias, p_scratch, p_args = _host_phase(phase, 13, 4)
    res = pl.pallas_call(
        body, name=name,
        grid=(B, nb),
        in_specs=[rowblk(ATT_WIDTH, 0), rowblk(256, 4),
                  pl.BlockSpec((None, ATT_BLOCK, 256), lambda b, i: (b, jnp.maximum(i - 1, 0), 4)),
                  rowblk(256, _Z0), rowblk(256, _Z0 + 1), rowblk(256, _Z0 + 2), rowblk(256, _Z0 + 3),
                  tab, tab, tabp, tabp,
                  pl.BlockSpec((ATT_HEADS * ATT_BLOCK, 128), lambda b, i: (0, 0)),
                  pl.BlockSpec(memory_space=pl.ANY)] + p_ispecs,
        out_specs=[pl.BlockSpec((None, ATT_BLOCK, ATT_WIDTH), lambda b, i: (b, i, 1)),
                   pl.BlockSpec((None, None, 2, GROUP_ROWS, 2 * ATT_BLOCK), lambda b, i: (b, i, 0, 0, 0)),
                   pl.BlockSpec((None, ATT_BLOCK, ATT_WIDTH), lambda b, i: (b, i, 0)),
                   pl.BlockSpec((None, None, ATT_HEADS, 128), lambda b, i: (b, i, 0, 0))] + p_ospecs,
        out_shape=[jax.ShapeDtypeStruct(u.shape, BF16),
                   jax.ShapeDtypeStruct((B, nb, 2, GROUP_ROWS, 2 * ATT_BLOCK), BF16),
                   jax.ShapeDtypeStruct((B, S, ATT_WIDTH), BF16),
                   jax.ShapeDtypeStruct((B, nb, ATT_HEADS, 128), F32)] + p_oshapes,
        input_output_aliases={12: 0, **p_alias},
        scratch_shapes=p_scratch,
        compiler_params=_params(("arbitrary", "arbitrary")),
    )(proj_a, proj_a, proj_a, proj_a, proj_a, proj_a, proj_a, cos, sin, cos, sin, sinks_l, u, *p_args)
    return res[0], tuple(res[1:4]), list(res[4:])


def _attn_bwd(proj_a, du, kept, cos, sin, name, phase=None):
    B, S, _ = proj_a.shape
    nb = S // ATT_BLOCK
    p_kept, o_kept, ps_kept = kept

    def body(*refs):
        ins, outs, (carry, sk_acc), p_in, p_out, p_sems = _split_refs(refs, 15, 4, 2, phase)
        (q_ref, kvc_ref, kvp_ref, z0, z1, z2, z3, du_ref, cos_ref, sin_ref, cosp_ref, sinp_ref,
         p_ref, o_ref, ps_ref) = ins
        dq_ref, dkv_ref, dz_ref, dsk_ref = outs
        b_id, i = pl.program_id(0), pl.program_id(1)
        _hosted_start(phase, p_in, p_out, p_sems, (b_id == 0) & (i == 0))

        @pl.when((b_id == 0) & (i == 0))
        def _():
            sk_acc[...] = jnp.zeros_like(sk_acc)

        @pl.when(i == 0)
        def _():
            carry[...] = jnp.zeros_like(carry)

        @pl.when(i < nb)
        def _():
            left, rope, rope_bwd, dup = _attn_common()
            cos_c, sin_c = cos_ref[...], sin_ref[...]
            cos_p, sin_p = cosp_ref[...], sinp_ref[...]
            kvc = kvc_ref[...]
            kvp = kvp_ref[...]
            kw = jnp.concatenate([rope(kvp[:, :KV_WIDTH], cos_p, sin_p), rope(kvc[:, :KV_WIDTH], cos_c, sin_c)], axis=0)
            vw = jnp.concatenate([kvp[:, KV_WIDTH:], kvc[:, KV_WIDTH:]], axis=0)
            kd, vd = dup(kw), dup(vw)
            zs = (z0, z1, z2, z3)
            units = [(kvh, hf) for kvh in range(2) for hf in range(2)]
            half = GROUP_ROWS // 2
            pairs = [range(4 * kvh + 2 * hf, 4 * kvh + 2 * hf + 2) for kvh, hf in units]
            ku = [kd[kvh] for kvh, _ in units]
            vu = [vd[kvh] for kvh, _ in units]
            ps_all = ps_ref[...]
            head_row = lax.broadcasted_iota(jnp.int32, (ATT_HEADS, 128), 0)
            eye = (lax.broadcasted_iota(jnp.int32, (ATT_BLOCK, 128), 0)
                   == lax.broadcasted_iota(jnp.int32, (ATT_BLOCK, 128), 1))

            def first(j):
                kvh, hf = units[j]
                p = p_ref[kvh, hf * half:(hf + 1) * half, :]
                parts = []
                for pr in pairs[j]:
                    cols = slice(128 * pr, 128 * (pr + 1))
                    sg, sg_grad = _silu_and_grad(zs[pr // 2][:, 128 * (pr % 2):128 * (pr % 2 + 1)])
                    du128 = du_ref[:, cols]
                    dz_ref[:, cols] = (du128 * o_ref[:, cols].astype(F32) * sg_grad).astype(BF16)
                    parts.append(du128 * sg)
                dos = _stack_heads(parts, left)
                dp = _dot(dos, vu[j], NT)
                delta = _row_sums(p.astype(F32) * dp)
                ds = (p.astype(F32) * (dp - jnp.concatenate([delta, delta], axis=1)) * ATT_SCALE).astype(BF16)
                sk = jnp.zeros((ATT_HEADS, 128), F32)
                for hh in range(4):
                    hd = kvh * GROUP + 4 * hf + hh
                    drow = jnp.sum(jnp.where(eye, delta[hh * ATT_BLOCK:(hh + 1) * ATT_BLOCK, :], 0.0), axis=0,
                                   keepdims=True)
                    sk = sk - jnp.where(head_row == hd, ps_all * drow, 0.0)
                sk_acc[...] += sk
                qs = _stack_heads([rope(q_ref[:, 128 * pr:128 * (pr + 1)], cos_c, sin_c) for pr in pairs[j]], left)
                return ds, p, dos.astype(BF16), qs.astype(BF16)

            def second(j, ds, p, dos, qs):
                dqs = _dot(ds, ku[j], NN)
                for pp, pr in enumerate(pairs[j]):
                    dq_ref[:, 128 * pr:128 * (pr + 1)] = rope_bwd(_unstack_heads(dqs, left, pp),
                                                                  cos_c, sin_c).astype(BF16)
                return _dot(ds, qs, TN), _dot(p, dos, TN)

            got, dku, dvu = {}, [None] * len(units), [None] * len(units)
            for j in range(len(units) + 1):
                if j < len(units):
                    got[j] = first(j)
                if j >= 1:
                    dku[j - 1], dvu[j - 1] = second(j - 1, *got.pop(j - 1))
            dkd = [dku[0] + dku[1], dku[2] + dku[3]]
            dvd = [dvu[0] + dvu[1], dvu[2] + dvu[3]]
            fold = lambda pr: jnp.where(left, pr[0] + pltpu.roll(pr[0], ATT_DIM, 1), pr[1] + pltpu.roll(pr[1], ATT_DIM, 1))
            dkw = fold(dkd)
            dvw = fold(dvd)
            prev = jnp.concatenate([rope_bwd(dkw[:ATT_BLOCK], cos_p, sin_p), dvw[:ATT_BLOCK]], axis=1)
            cur = jnp.concatenate([rope_bwd(dkw[ATT_BLOCK:], cos_c, sin_c), dvw[ATT_BLOCK:]], axis=1)
            dkv_ref[...] = (carry[...] + prev).astype(BF16)
            carry[...] = cur

        @pl.when(i == nb)
        def _():
            dkv_ref[...] = carry[...].astype(BF16)

        @pl.when((b_id == B - 1) & (i == nb))
        def _():
            diag = (lax.broadcasted_iota(jnp.int32, (ATT_HEADS, 128), 0)
                    == lax.broadcasted_iota(jnp.int32, (ATT_HEADS, 128), 1))
            tot = jnp.sum(sk_acc[...], axis=1, keepdims=True)
            dsk_ref[...] = jnp.sum(jnp.where(diag, tot, 0.0), axis=0, keepdims=True)

        _hosted_finish(phase, p_in, p_out, p_sems, (b_id == B - 1) & (i == nb))

    cl = lambda i: jnp.minimum(i, nb - 1)
    pv = lambda i: jnp.maximum(jnp.minimum(i, nb - 1) - 1, 0)
    rowblk = lambda w, cb: pl.BlockSpec((None, ATT_BLOCK, w), lambda b, i: (b, cl(i), cb))
    tab = pl.BlockSpec((ATT_BLOCK, 128), lambda b, i: (cl(i), 0))
    tabp = pl.BlockSpec((ATT_BLOCK, 128), lambda b, i: (pv(i), 0))
    p_ispecs, p_ospecs, p_oshapes, p_alias, p_scratch, p_args = _host_phase(phase, 15, 4)
    res = pl.pallas_call(
        body, name=name,
        grid=(B, nb + 1),
        in_specs=[rowblk(ATT_WIDTH, 0), rowblk(256, 4),
                  pl.BlockSpec((None, ATT_BLOCK, 256), lambda b, i: (b, pv(i), 4)),
                  rowblk(256, _Z0), rowblk(256, _Z0 + 1), rowblk(256, _Z0 + 2), rowblk(256, _Z0 + 3),
                  rowblk(ATT_WIDTH, 1),
                  tab, tab, tabp, tabp,
                  pl.BlockSpec((None, None, 2, GROUP_ROWS, 2 * ATT_BLOCK), lambda b, i: (b, cl(i), 0, 0, 0)),
                  rowblk(ATT_WIDTH, 0),
                  pl.BlockSpec((None, None, ATT_HEADS, 128), lambda b, i: (b, cl(i), 0, 0))] + p_ispecs,
        out_specs=[rowblk(ATT_WIDTH, 0),
                   pl.BlockSpec((None, ATT_BLOCK, 256), lambda b, i: (b, jnp.maximum(i - 1, 0), 0)),
                   rowblk(ATT_WIDTH, 0),
                   pl.BlockSpec((1, 128), lambda b, i: (0, 0))] + p_ospecs,
        out_shape=[jax.ShapeDtypeStruct((B, S, ATT_WIDTH), BF16), jax.ShapeDtypeStruct((B, S, 256), BF16),
                   jax.ShapeDtypeStruct((B, S, ATT_WIDTH), BF16), jax.ShapeDtypeStruct((1, 128), F32)] + p_oshapes,
        input_output_aliases=p_alias,
        scratch_shapes=[pltpu.VMEM((ATT_BLOCK, 256), F32), pltpu.VMEM((ATT_HEADS, 128), F32)] + p_scratch,
        compiler_params=_params(("arbitrary", "arbitrary")),
    )(proj_a, proj_a, proj_a, proj_a, proj_a, proj_a, proj_a, du, cos, sin, cos, sin, p_kept, o_kept, ps_kept, *p_args)
    return tuple(res[:4]) + (list(res[4:]),)


def _outproj_fwd(u2, w_out, x2, g_post, target2, name):
    T, D = x2.shape
    tm = _pick(T, (512, 256, 128))
    last = target2 is not None

    def body(u_ref, w_ref, x_ref, g_ref, *rest):
        y = lax.dot_general(u_ref[...], w_ref[...], (NN, ((), ())), preferred_element_type=F32)
        r = lax.rsqrt(jnp.mean(y * y, axis=-1, keepdims=True) + NORM_EPS)
        xn = x_ref[...] + (y * r) * g_ref[...]
        if last:
            t_ref, y_ref, dx_ref, loss_ref = rest
            err = xn - t_ref[...]
            dx_ref[...] = err * (1.0 / D)
            sq = err * err
            acc = sq[:, 0:128]
            for kk in range(1, D // 128):
                acc = acc + sq[:, 128 * kk:128 * (kk + 1)]
            part = jnp.sum(acc.reshape(tm // 8, 8, 128), axis=0) * (0.5 / D)

            @pl.when(pl.program_id(0) == 0)
            def _():
                loss_ref[...] = jnp.zeros_like(loss_ref)

            loss_ref[...] += part
        else:
            y_ref, xn_ref = rest
            xn_ref[...] = xn
        y_ref[...] = y

    row = pl.BlockSpec((tm, D), lambda i: (i, 0))
    in_specs = [pl.BlockSpec((tm, MIX_WIDTH), lambda i: (i, 0)),
                pl.BlockSpec((MIX_WIDTH, D), lambda i: (0, 0)), row,
                pl.BlockSpec((1, D), lambda i: (0, 0))]
    args = [u2, w_out, x2, g_post]
    out_specs = [row, row]
    out_shape = [jax.ShapeDtypeStruct((T, D), F32), jax.ShapeDtypeStruct((T, D), F32)]
    if last:
        in_specs.append(row)
        args.append(target2)
        out_specs.append(pl.BlockSpec((8, 128), lambda i: (0, 0)))
        out_shape.append(jax.ShapeDtypeStruct((8, 128), F32))
    return pl.pallas_call(
        body, name=name, grid=(T // tm,), in_specs=in_specs, out_specs=out_specs, out_shape=out_shape,
        compiler_params=_params(("arbitrary",)),
    )(*args)


def _outproj_bwd(dxn2, y2, g_post, w_out, u2, name):
    T, D = y2.shape
    N = w_out.shape[0]
    tm = _pick(T, (512, 256, 128))
    nt = T // tm

    def body(dx_ref, y_ref, g_ref, w_ref, u_ref, dg_ref, du_ref, dw_ref, acc, wacc):
        i = pl.program_id(0)

        @pl.when(i == 0)
        def _():
            acc[...] = jnp.zeros_like(acc)
            wacc[...] = jnp.zeros_like(wacc)

        y = y_ref[...]
        dxn = dx_ref[...]
        r = lax.rsqrt(jnp.mean(y * y, axis=-1, keepdims=True) + NORM_EPS)
        n = y * r
        dn = dxn * g_ref[...]
        dy = (r * (dn - n * jnp.mean(dn * n, axis=-1, keepdims=True))).astype(BF16)
        du_ref[...] = lax.dot_general(dy, w_ref[...], (NT, ((), ())), preferred_element_type=F32)
        wacc[...] += lax.dot_general(u_ref[...], dy, (TN, ((), ())), preferred_element_type=F32)
        acc[...] += jnp.sum((dxn * n).reshape(tm // 8, 8, D), axis=0)

        @pl.when(i == nt - 1)
        def _():
            dg_ref[...] = jnp.sum(acc[...], axis=0, keepdims=True)
            dw_ref[...] = wacc[...].astype(BF16)

    row = pl.BlockSpec((tm, D), lambda i: (i, 0))
    wide = pl.BlockSpec((tm, N), lambda i: (i, 0))
    vec = pl.BlockSpec((1, D), lambda i: (0, 0))
    whole = pl.BlockSpec((N, D), lambda i: (0, 0))
    return pl.pallas_call(
        body, name=name, grid=(nt,),
        in_specs=[row, row, vec, pl.BlockSpec((N, D), lambda i: (0, 0), pipeline_mode=pl.Buffered(1)), wide],
        out_specs=[vec, wide, whole],
        out_shape=[jax.ShapeDtypeStruct((1, D), F32), jax.ShapeDtypeStruct((T, N), F32),
                   jax.ShapeDtypeStruct((N, D), BF16)],
        scratch_shapes=[pltpu.VMEM((8, D), F32), pltpu.VMEM((N, D), F32)],
        compiler_params=_params(("arbitrary",)),
    )(dxn2, y2, g_post, w_out, u2)


def _inproj_bwd(pieces, w_t, x2, dxn2, g_pre, name, phase=None):
    T, D = x2.shape
    widths = [p.shape[1] for p in pieces]
    offs = [sum(widths[:i]) for i in range(len(pieces))]
    n_p = len(pieces)
    tm = _pick(T, (512, 256, 128))
    nt = T // tm

    def body(*refs):
        ins, (dx_ref, dg_ref), (acc,), p_in, p_out, p_sems = _split_refs(refs, n_p + 4, 2, 1, phase)
        w_ref, x_ref, dxn_ref, g_ref = ins[n_p:]
        i = pl.program_id(0)
        _hosted_start(phase, p_in, p_out, p_sems, i == 0)

        @pl.when(i == 0)
        def _():
            acc[...] = jnp.zeros_like(acc)

        dh = jnp.zeros((tm, D), F32)
        for p in range(n_p):
            dh = dh + lax.dot_general(ins[p][...], w_ref[offs[p]:offs[p] + widths[p], :], (NN, ((), ())),
                                      preferred_element_type=F32)
        x = x_ref[...]
        r = lax.rsqrt(jnp.mean(x * x, axis=-1, keepdims=True) + NORM_EPS)
        n = x * r
        dn = dh * g_ref[...]
        dx_ref[...] = dxn_ref[...] + r * (dn - n * jnp.mean(dn * n, axis=-1, keepdims=True))
        acc[...] += jnp.sum((dh * n).reshape(tm // 8, 8, D), axis=0)

        @pl.when(i == nt - 1)
        def _():
            dg_ref[...] = jnp.sum(acc[...], axis=0, keepdims=True)

        _hosted_finish(phase, p_in, p_out, p_sems, i == nt - 1)

    row = pl.BlockSpec((tm, D), lambda i: (i, 0))
    vec = pl.BlockSpec((1, D), lambda i: (0, 0))
    p_ispecs, p_ospecs, p_oshapes, p_alias, p_scratch, p_args = _host_phase(phase, n_p + 4, 2)
    res = pl.pallas_call(
        body, name=name, grid=(nt,),
        in_specs=[pl.BlockSpec((tm, w), lambda i: (i, 0)) for w in widths]
        + [pl.BlockSpec((sum(widths), D), lambda i: (0, 0), pipeline_mode=pl.Buffered(1)), row, row, vec] + p_ispecs,
        out_specs=[row, vec] + p_ospecs,
        out_shape=[jax.ShapeDtypeStruct((T, D), F32), jax.ShapeDtypeStruct((1, D), F32)] + p_oshapes,
        input_output_aliases=p_alias,
        scratch_shapes=[pltpu.VMEM((8, D), F32)] + p_scratch,
        compiler_params=_params(("arbitrary",)),
    )(*pieces, w_t, x2, dxn2, g_pre, *p_args)
    return res[0], res[1], list(res[2:])


def _step(x, target, g_pre, g_post, lb_param, g_head, sinks, shards=None, full=None):
    B, S, D = x.shape
    T = B * S
    dist = shards is not None
    first, last = 0, DEPTH - 1
    if dist:
        a_loc, b_loc = shards
        ra, rb = a_loc.shape[1], b_loc.shape[1]
        side = _own_side_blocks()
        placed = lambda loc, nm: _place_own(loc, side, "place_" + nm)
        w_in0 = _run_phase(_gather_ici_phase([a_loc[0]], [placed(a_loc[0], "in0")]), "gather_in0_ici")
        w_in0 = _run_phase(_gather_d2d_phase(w_in0, [ra]), "gather_in0_d2d")[0]
        w_in, w_out = [w_in0, None], [None, None]
    else:
        w_in, w_out = list(full[0]), list(full[1])
    cos, sin = _rope_tables(S)
    saved = []
    xs = x
    loss_part = None
    dxn = None
    for l in range(DEPTH):
        x2 = xs.reshape(T, D)
        proj_h, proj_a, h = _inproj(x2, g_pre[l:l + 1], w_in[l], f"inproj{l}")
        proj_h = proj_h.reshape(B, S, N_H)
        proj_a = proj_a.reshape(B, S, N_A)
        phase = None
        if dist and l == first:
            phase = _gather_ici_phase([a_loc[1], b_loc[0]], [placed(a_loc[1], "in1"), placed(b_loc[0], "out0")])
        if dist and l == last:
            phase = _gather_d2d_phase([w_out1_part], [rb])
        o_h, u, states, got = _hgrn_fwd(proj_h, MIX_WIDTH, lb_param, g_head[l:l + 1], l, f"hgrn_fwd{l}", phase)
        phase = None
        if dist and l == first:
            phase = _merge_phases(_gather_d2d_phase(got, [ra, rb]),
                                  _gather_ici_phase([b_loc[1]], [placed(b_loc[1], "out1")]))
        if dist and l == last:
            w_out[1] = got[0]
        u, kept_a, got = _attn_fwd(proj_a, u, _sink_rows(sinks[l]), cos, sin, f"attn_fwd{l}", phase)
        if dist and l == first:
            w_in[1], w_out[0], w_out1_part = got
        u2 = u.reshape(T, MIX_WIDTH)
        if l < last:
            y, xn = _outproj_fwd(u2, w_out[l], x2, g_post[l:l + 1], None, f"outproj{l}")
            xn = xn.reshape(B, S, D)
        else:
            y, dxn, loss_part = _outproj_fwd(u2, w_out[l], x2, g_post[l:l + 1], target.reshape(T, D), f"outproj{l}")
            xn = None
        saved.append((x2, h, proj_h, proj_a, o_h, u2, states, kept_a, y))
        xs = xn

    dw_in, dw_out = [None] * DEPTH, [None] * DEPTH
    dg_pre, dg_post, dlb, dg_head, dsinks = [], [], [], [], []
    for l in reversed(range(DEPTH)):
        x2, h, proj_h, proj_a, o_h, u2, states, kept_a, y = saved[l]
        dgp, du, dw_out[l] = _outproj_bwd(dxn, y, g_post[l:l + 1], w_out[l], u2, f"outproj_bwd{l}")
        du = du.reshape(B, S, MIX_WIDTH)
        phase = None
        if dist:
            phase = _reduce_d2d_phase([dw_out[l]], [rb])
            if l == first:
                phase = _merge_phases(_reduce_ici_phase([part_in1]), phase)
        dqh, dfh, dih, dzh, dlb_l, dgh, got = _hgrn_bwd(
            proj_h, o_h, du, states, lb_param, g_head[l:l + 1], l, f"hgrn_bwd{l}", phase)
        if dist:
            if l == first:
                sum_in = _chip_sum(part_in1, got[0], "chip_sum_in1", 1)
            part_out = _pair_sum(dw_out[l], got[-1], side, f"pair_sum_out{l}")
        dqa, dkv, dza, dsk, got = _attn_bwd(proj_a, du, kept_a, cos, sin, f"attn_bwd{l}",
                                            _reduce_ici_phase([part_out]) if dist else None)
        if dist:
            sum_out = _chip_sum(part_out, got[0], f"chip_sum_out{l}", l, None if l == last else sum_out)
        dproj = [p.reshape(T, p.shape[-1]) for p in (dqh, dfh, dih, dzh, dqa, dkv, dza)]
        dw_in[l] = _mm_tn(dproj, h, f"wgrad_in{l}")
        phase = None
        if dist and l == last:
            phase = _reduce_d2d_phase([dw_in[l]], [ra])
        if dist and l == first:
            got = _run_phase(_reduce_d2d_phase([dw_in[l]], [ra]), "reduce_in0_d2d")
            part_in0 = _pair_sum(dw_in[l], got[0], side, "pair_sum_in0")
            phase = _reduce_ici_phase([part_in0])
        dxn, dgpre, got = _inproj_bwd(dproj, w_in[l], x2, dxn, g_pre[l:l + 1], f"inproj_bwd{l}", phase)
        if dist and l == last:
            part_in1 = _pair_sum(dw_in[l], got[0], side, "pair_sum_in1")
        if dist and l == first:
            sum_in = _chip_sum(part_in0, got[0], "chip_sum_in0", 0, sum_in)
        dg_pre.append(dgpre)
        dg_post.append(dgp)
        dlb.append(dlb_l)
        dg_head.append(dgh)
        dsinks.append(dsk)
    rev = lambda lst: jnp.concatenate(lst[::-1], axis=0)
    if not dist:
        sum_in, sum_out = jnp.stack(dw_in), jnp.stack(dw_out)
    return (loss_part, dxn.reshape(B, S, D), sum_in, sum_out,
            rev(dg_pre), rev(dg_post), rev(dlb), rev(dg_head), rev(dsinks))


def _me_and_peers():
    x, y, c = lax.axis_index("x"), lax.axis_index("y"), lax.axis_index("c")
    me = 4 * x + 2 * y + c
    peers = []
    for k in range(1, N_DEV):
        px = 1 - x if k & 4 else x
        py = 1 - y if k & 2 else y
        pc = 1 - c if k & 1 else c
        peers.append(((px, py, pc), 4 * px + 2 * py + pc))
    return me, peers


class _Phase:
    def __init__(self, arrays, out_shapes, aliases, n_send, build):
        self.arrays, self.out_shapes, self.aliases = list(arrays), list(out_shapes), dict(aliases)
        self.n_send, self.build = n_send, build

    def scratch(self):
        return [pltpu.SemaphoreType.DMA((self.n_send,)), pltpu.SemaphoreType.DMA((self.n_send,))]

    def _copies(self, in_refs, out_refs, sems, arrivals):
        send_sems, recv_sems = sems
        sends, recvs = self.build(in_refs, out_refs)
        assert len(sends) == self.n_send == len(recvs)
        out = [pltpu.make_async_remote_copy(src_ref=s, dst_ref=d, send_sem=send_sems.at[i], recv_sem=recv_sems.at[i],
                                            device_id=dev, device_id_type=MESH) for i, (s, d, dev) in enumerate(sends)]
        inc = [pltpu.make_async_remote_copy(src_ref=s, dst_ref=r, send_sem=send_sems.at[i], recv_sem=recv_sems.at[i],
                                            device_id=dev, device_id_type=MESH)
               for i, ((s, _, dev), r) in enumerate(zip(sends, recvs))] if arrivals else []
        return out, inc

    def start(self, in_refs, out_refs, sems):
        out, _ = self._copies(in_refs, out_refs, sems, False)
        for cp in out:
            cp.start()

    def finish(self, in_refs, out_refs, sems):
        out, inc = self._copies(in_refs, out_refs, sems, True)
        for cp in inc:
            cp.wait_recv()
        for cp in out:
            cp.wait_send()


_ANY = pl.BlockSpec(memory_space=pl.ANY)


def _host_phase(phase, n_in, n_out):
    if phase is None:
        return [], [], [], {}, [], []
    aliases = {n_in + i: n_out + o for i, o in phase.aliases.items()}
    return ([_ANY] * len(phase.arrays), [_ANY] * len(phase.out_shapes), phase.out_shapes, aliases, phase.scratch(),
            phase.arrays)


def _split_refs(refs, n_in, n_out, n_scr, phase):
    pi = len(phase.arrays) if phase else 0
    po = len(phase.out_shapes) if phase else 0
    a = n_in + pi
    b = a + n_out + po
    return (refs[:n_in], refs[a:a + n_out], refs[b:b + n_scr], refs[n_in:a], refs[a + n_out:b], refs[b + n_scr:])


def _hosted_start(phase, p_in, p_out, p_sems, first):
    if phase is not None:
        @pl.when(first)
        def _():
            phase.start(p_in, p_out, p_sems)


def _hosted_finish(phase, p_in, p_out, p_sems, last):
    if phase is not None:
        @pl.when(last)
        def _():
            phase.finish(p_in, p_out, p_sems)


def _run_phase(phase, name):
    n_in, n_out = len(phase.arrays), len(phase.out_shapes)

    def body(*refs):
        phase.start(refs[:n_in], refs[n_in:n_in + n_out], refs[n_in + n_out:])
        phase.finish(refs[:n_in], refs[n_in:n_in + n_out], refs[n_in + n_out:])

    return pl.pallas_call(
        body, name=name, in_specs=[_ANY] * n_in, out_specs=[_ANY] * n_out,
        out_shape=phase.out_shapes, input_output_aliases=phase.aliases, scratch_shapes=phase.scratch(),
        compiler_params=pltpu.CompilerParams(has_side_effects=True),
    )(*phase.arrays)


def _merge_phases(a, b):
    n_in, n_out = len(a.arrays), len(a.out_shapes)
    aliases = dict(a.aliases)
    aliases.update({n_in + i: n_out + o for i, o in b.aliases.items()})

    def build(ins, outs):
        sa, ra = a.build(ins[:n_in], outs[:n_out])
        sb, rb = b.build(ins[n_in:], outs[n_out:])
        return sa + sb, ra + rb

    return _Phase(a.arrays + b.arrays, a.out_shapes + b.out_shapes, aliases, a.n_send + b.n_send, build)


def _mesh_place():
    x, y, c = lax.axis_index("x"), lax.axis_index("y"), lax.axis_index("c")
    chips = [(x, y), (1 - x, y), (x, 1 - y), (1 - x, 1 - y)]
    num = lambda chip, core: 4 * chip[0] + 2 * chip[1] + core
    return c, chips, num


def _own_side_blocks():
    c, chips, num = _mesh_place()
    return jnp.stack([num(ch, c) for ch in chips]).astype(jnp.int32)


def _rows(ref, r, dev):
    return ref.at[pl.ds(pl.multiple_of(dev * r, 16), r), :]


def _place_own(loc, blocks, name):
    r, D = loc.shape
    tr = _pick(r, (400, 256, 200, 128, 64, 16))

    def body(idx_ref, l_ref, o_ref):
        del idx_ref
        o_ref[...] = l_ref[...]

    return pl.pallas_call(
        body, name=name,
        grid_spec=pltpu.PrefetchScalarGridSpec(
            num_scalar_prefetch=1, grid=(r // tr,),
            in_specs=[pl.BlockSpec((tr, D), lambda i, idx: (i, 0))],
            out_specs=pl.BlockSpec((tr, D), lambda i, idx: (idx[0] * (r // tr) + i, 0))),
        out_shape=jax.ShapeDtypeStruct((N_DEV * r, D), loc.dtype),
        compiler_params=_params(("arbitrary",)),
    )(blocks, loc)


def _gather_ici_phase(locs, fulls):
    rs = [a.shape[0] for a in locs]
    n = len(locs)

    def build(ins, outs):
        c, chips, num = _mesh_place()
        me = num(chips[0], c)
        targets = [((*chips[0], 1 - c), num(chips[0], 1 - c))] + [((*ch, c), num(ch, c)) for ch in chips[1:]]
        sends, recvs = [], []
        for dev, dnum in targets:
            for i, r in enumerate(rs):
                sends.append((ins[i], _rows(outs[i], r, me), dev))
                recvs.append(_rows(outs[i], r, dnum))
        return sends, recvs

    shapes = [jax.ShapeDtypeStruct(a.shape, a.dtype) for a in fulls]
    return _Phase(list(locs) + list(fulls), shapes, {n + i: i for i in range(n)}, 4 * n, build)


def _gather_d2d_phase(fulls, rs):
    def build(ins, outs):
        c, chips, num = _mesh_place()
        sib = (*chips[0], 1 - c)
        sends, recvs = [], []
        for ch in chips[1:]:
            for i, r in enumerate(rs):
                blk = _rows(outs[i], r, num(ch, c))
                sends.append((blk, blk, sib))
                recvs.append(_rows(outs[i], r, num(ch, 1 - c)))
        return sends, recvs

    shapes = [jax.ShapeDtypeStruct(a.shape, a.dtype) for a in fulls]
    return _Phase(fulls, shapes, {i: i for i in range(len(fulls))}, 3 * len(fulls), build)


def _reduce_d2d_phase(grads, rs):
    def build(ins, outs):
        c, chips, num = _mesh_place()
        sib = (*chips[0], 1 - c)
        sends, recvs = [], []
        for j, ch in enumerate(chips):
            for i, r in enumerate(rs):
                sends.append((_rows(ins[i], r, num(ch, 1 - c)), outs[i].at[j], sib))
                recvs.append(outs[i].at[j])
        return sends, recvs

    shapes = [jax.ShapeDtypeStruct((4, r, g.shape[1]), g.dtype) for g, r in zip(grads, rs)]
    return _Phase(grads, shapes, {}, 4 * len(grads), build)


def _reduce_ici_phase(parts):
    def build(ins, outs):
        c, chips, _ = _mesh_place()
        sends, recvs = [], []
        for t in range(1, 4):
            for i in range(len(parts)):
                sends.append((ins[i].at[t], outs[i].at[t - 1], (*chips[t], c)))
                recvs.append(outs[i].at[t - 1])
        return sends, recvs

    shapes = [jax.ShapeDtypeStruct((3,) + p.shape[1:], p.dtype) for p in parts]
    return _Phase(parts, shapes, {}, 3 * len(parts), build)


def _pair_sum(g, got, blocks, name):
    n, r, D = got.shape
    tr = _pick(r, (800, 400, 256, 200, 128, 64, 16))

    def body(idx_ref, g_ref, r_ref, o_ref):
        del idx_ref
        o_ref[...] = (g_ref[...].astype(F32) + r_ref[...].astype(F32)).astype(o_ref.dtype)

    blk = pl.BlockSpec((None, tr, D), lambda j, i, idx: (j, i, 0))
    return pl.pallas_call(
        body, name=name,
        grid_spec=pltpu.PrefetchScalarGridSpec(
            num_scalar_prefetch=1, grid=(n, r // tr),
            in_specs=[pl.BlockSpec((tr, D), lambda j, i, idx: (idx[j] * (r // tr) + i, 0)), blk],
            out_specs=blk),
        out_shape=jax.ShapeDtypeStruct(got.shape, got.dtype),
        compiler_params=_params(("arbitrary", "arbitrary")),
    )(blocks, g, got)


def _chip_sum(p, r, name, layer, into=None):
    _, R, D = p.shape
    tr = _pick(R, (800, 400, 256, 200, 128, 64, 16))

    def body(p_ref, r_ref, *rest):
        acc = p_ref[...].astype(F32)
        for t in range(3):
            acc = acc + r_ref[t].astype(F32)
        rest[-1][...] = acc

    args = [p, r] + ([] if into is None else [into])
    return pl.pallas_call(
        body, name=name, grid=(R // tr,),
        in_specs=[pl.BlockSpec((None, tr, D), lambda i: (0, i, 0)), pl.BlockSpec((3, tr, D), lambda i: (0, i, 0))]
        + ([] if into is None else [_ANY]),
        out_specs=pl.BlockSpec((None, tr, D), lambda i: (layer, i, 0)),
        out_shape=jax.ShapeDtypeStruct((DEPTH, R, D), F32),
        input_output_aliases={} if into is None else {2: 0},
        compiler_params=_params(("parallel",)))(*args)


def _allreduce_small(vec):
    R, C = vec.shape

    def body(v_ref, o_ref, buf, send_sems, recv_sems):
        me, peers = _me_and_peers()
        buf[me] = v_ref[...]
        sends = []
        for k, (pid, _) in enumerate(peers):
            cp = pltpu.make_async_remote_copy(src_ref=v_ref, dst_ref=buf.at[me], send_sem=send_sems.at[k],
                                              recv_sem=recv_sems.at[k], device_id=pid, device_id_type=MESH)
            cp.start()
            sends.append(cp)
        for k, (pid, pnum) in enumerate(peers):
            pltpu.make_async_remote_copy(src_ref=v_ref, dst_ref=buf.at[pnum], send_sem=send_sems.at[k],
                                         recv_sem=recv_sems.at[k], device_id=pid, device_id_type=MESH).wait_recv()
        for cp in sends:
            cp.wait_send()
        acc = buf[0]
        for d in range(1, N_DEV):
            acc = acc + buf[d]
        o_ref[...] = acc

    vm = pl.BlockSpec(memory_space=pltpu.VMEM)
    return pl.pallas_call(
        body, name="allreduce_small",
        in_specs=[vm], out_specs=vm,
        out_shape=jax.ShapeDtypeStruct((R, C), F32),
        scratch_shapes=[pltpu.VMEM((N_DEV, R, C), F32), pltpu.SemaphoreType.DMA((N_DEV - 1,)),
                        pltpu.SemaphoreType.DMA((N_DEV - 1,))],
        compiler_params=pltpu.CompilerParams(has_side_effects=True),
    )(vec)


def _adamw(w, g, m, v, name):
    R, C = w.shape
    tr = _pick(R, (512, 400, 256, 128, 64, 32, 16, 8)) if R >= 8 else R
    c1 = 1.0 - ADAM_B1 ** ADAM_STEP
    c2 = 1.0 - ADAM_B2 ** ADAM_STEP

    def body(w_ref, g_ref, m_ref, v_ref, d_ref, mo_ref, vo_ref):
        gg = g_ref[...]
        mn = ADAM_B1 * m_ref[...] + (1.0 - ADAM_B1) * gg
        vn = ADAM_B2 * v_ref[...] + (1.0 - ADAM_B2) * (gg * gg)
        d_ref[...] = -ADAM_LR * ((mn / c1) / (jnp.sqrt(vn / c2) + ADAM_EPS) + ADAM_WD * w_ref[...])
        mo_ref[...] = mn
        vo_ref[...] = vn

    blk = pl.BlockSpec((tr, C), lambda i: (i, 0))
    sh = jax.ShapeDtypeStruct((R, C), F32)
    return pl.pallas_call(
        body, name=name, grid=(R // tr,), in_specs=[blk] * 4, out_specs=[blk] * 3, out_shape=[sh] * 3,
        compiler_params=_params(("parallel",)),
    )(w, g, m, v)


def _lb_param_grad(lb_param, dlb):
    L, C = lb_param.shape

    def body(p_ref, d_ref, o_ref):
        lbp = p_ref[...]
        d = d_ref[...]
        mx = jnp.max(lbp, axis=0, keepdims=True)
        e = jnp.exp(lbp - mx)
        p = e / jnp.sum(e, axis=0, keepdims=True)
        tot = jnp.sum(d, axis=0, keepdims=True)
        dps = []
        rest = tot
        for j in range(L):
            dps.append(rest - tot if j == 0 else rest)
            rest = rest - d[j:j + 1]
        dp = jnp.concatenate(dps, axis=0)
        o_ref[...] = p * (dp - jnp.sum(p * dp, axis=0, keepdims=True))

    vm = pl.BlockSpec(memory_space=pltpu.VMEM)
    return pl.pallas_call(body, name="lb_param_grad", in_specs=[vm, vm], out_specs=vm,
                          out_shape=jax.ShapeDtypeStruct((L, C), F32))(lb_param, dlb)


def _pack_small(loss_part, dg_pre, dg_post, dlb, dg_head, dsinks):
    pad8 = lambda a: jnp.pad(a.reshape(-1, 128), ((0, 8 - DEPTH), (0, 0)))
    rows = [dg_pre.reshape(-1, 128), dg_post.reshape(-1, 128), dlb.reshape(-1, 128), pad8(dg_head), pad8(dsinks),
            loss_part]
    return jnp.concatenate(rows, axis=0)


def _unpack_small(vec):
    n = DEPTH * D_MODEL // 128
    o = 0
    dg_pre = vec[o:o + n].reshape(DEPTH, D_MODEL); o += n
    dg_post = vec[o:o + n].reshape(DEPTH, D_MODEL); o += n
    dlb = vec[o:o + n].reshape(DEPTH, HG_WIDTH); o += n
    dg_head = vec[o:o + DEPTH]; o += 8
    dsinks = vec[o:o + DEPTH, :ATT_HEADS]; o += 8
    loss = jnp.sum(vec[o:o + 8])
    return loss, dg_pre, dg_post, dlb, dg_head, dsinks


def kernel(x, w_in, w_out, g_pre, g_post, lb_param, g_head, sinks, loss_target, m_w_in, m_w_out, m_g_pre, m_g_post, m_lb_param, m_g_head, m_sinks, v_w_in, v_w_out, v_g_pre, v_g_post, v_lb_param, v_g_head, v_sinks):
    tr = lambda a: jnp.swapaxes(a, 1, 2)
    w_in_t = tr(w_in)
    (loss_part, dx, gw_in_t, gw_out, dg_pre, dg_post, dlb, dg_head, dsinks) = _step(
        x, loss_target, g_pre, g_post, lb_param, g_head, sinks, shards=(w_in_t.astype(BF16), w_out.astype(BF16)))

    small = _allreduce_small(_pack_small(loss_part, dg_pre, dg_post, dlb, dg_head, dsinks))
    loss, gg_pre, gg_post, gdlb, gg_head, gsinks = _unpack_small(small)
    glb = _lb_param_grad(lb_param, gdlb)

    grads = [gw_in_t, gw_out, gg_pre, gg_post, glb, gg_head, gsinks]
    ws = [w_in_t, w_out, g_pre, g_post, lb_param, g_head, sinks]
    ms = [tr(m_w_in), m_w_out, m_g_pre, m_g_post, m_lb_param, m_g_head, m_sinks]
    vs = [tr(v_w_in), v_w_out, v_g_pre, v_g_post, v_lb_param, v_g_head, v_sinks]
    names = ["w_in", "w_out", "g_pre", "g_post", "lb_param", "g_head", "sinks"]
    deltas, new_m, new_v = [], [], []
    for w, g, m, v, nm in zip(ws, grads, ms, vs, names):
        sh = w.shape
        two = lambda a: a.reshape(-1, sh[-1])
        d, mn, vn = _adamw(two(w), two(g), two(m), two(v), "adamw_" + nm)
        deltas.append(d.reshape(sh))
        new_m.append(mn.reshape(sh))
        new_v.append(vn.reshape(sh))
    grads[0], deltas[0], new_m[0], new_v[0] = tr(grads[0]), tr(deltas[0]), tr(new_m[0]), tr(new_v[0])
    return (loss, dx, *grads, *deltas, *new_m, *new_v)
```

```python
import math

import numpy as np
import jax
import jax.numpy as jnp
from jax import lax
from jax.experimental import pallas as pl
from jax.experimental.pallas import tpu as pltpu

F32 = jnp.float32
BF16 = jnp.bfloat16

D_MODEL = 1024
DEPTH = 2
HG_HEADS = 8
HG_DIM = 128
HG_WIDTH = HG_HEADS * HG_DIM
CHUNK = 64
ATT_HEADS = 16
ATT_DIM = 64
ATT_WIDTH = ATT_HEADS * ATT_DIM
KV_WIDTH = 128
ATT_BLOCK = 128
ATT_SCALE = 1.0 / math.sqrt(ATT_DIM)
ROPE_THETA = 10000.0
NORM_EPS = 1e-6
NEG_INF = -1e30
LB_FLOOR = 1e-20
N_H = 4 * HG_WIDTH
N_A = 2 * ATT_WIDTH + 2 * KV_WIDTH
IN_WIDTH = N_H + N_A
MIX_WIDTH = HG_WIDTH + ATT_WIDTH

ADAM_LR = 0.001
ADAM_B1 = 0.9
ADAM_B2 = 0.999
ADAM_EPS = 1e-08
ADAM_WD = 0.01
ADAM_STEP = 10

N_DEV = 8
MESH = pl.DeviceIdType.MESH
VMEM_LIMIT = 56 * 1024 * 1024

NN = ((1,), (0,))
NT = ((1,), (1,))
TN = ((0,), (0,))


def _dot(a, b, dims):
    return lax.dot_general(a.astype(BF16), b.astype(BF16), (dims, ((), ())), preferred_element_type=F32)


def _params(sem=None, **kw):
    return pltpu.CompilerParams(dimension_semantics=sem, vmem_limit_bytes=VMEM_LIMIT, **kw)


def _sigmoids(x):
    e = jnp.exp(-jnp.abs(x))
    r = 1.0 / (1.0 + e)
    er = e * r
    pos = x >= 0.0
    return jnp.where(pos, r, er), jnp.where(pos, er, r)


def _silu(x):
    return x * _sigmoids(x)[0]


def _silu_and_grad(x):
    s, ns = _sigmoids(x)
    return x * s, s * (1.0 + x * ns)


def _pick(n, prefs):
    for p in prefs:
        if n % p == 0:
            return p
    return n


def _inproj(x2, g, w, name):
    T, D = x2.shape
    tm = _pick(T, (512, 256, 128))
    nchunk = 1024

    def body(x_ref, g_ref, w_ref, oh_ref, oa_ref, h_ref):
        x = x_ref[...]
        r = lax.rsqrt(jnp.mean(x * x, axis=-1, keepdims=True) + NORM_EPS)
        h = ((x * r) * g_ref[...]).astype(BF16)
        h_ref[...] = h
        for j in range(0, N_H, nchunk):
            oh_ref[:, j:j + nchunk] = lax.dot_general(h, w_ref[j:j + nchunk, :], (NT, ((), ())),
                                                      preferred_element_type=F32)
        for j in range(0, N_A, N_A // 2):
            oa_ref[:, j:j + N_A // 2] = lax.dot_general(h, w_ref[N_H + j:N_H + j + N_A // 2, :], (NT, ((), ())),
                                                        preferred_element_type=F32)

    row = lambda w_: pl.BlockSpec((tm, w_), lambda i: (i, 0))
    return pl.pallas_call(
        body, name=name,
        grid=(T // tm,),
        in_specs=[row(D), pl.BlockSpec((1, D), lambda i: (0, 0)),
                  pl.BlockSpec((IN_WIDTH, D), lambda i: (0, 0), pipeline_mode=pl.Buffered(1))],
        out_specs=[row(N_H), row(N_A), row(D)],
        out_shape=[jax.ShapeDtypeStruct((T, N_H), F32), jax.ShapeDtypeStruct((T, N_A), F32),
                   jax.ShapeDtypeStruct((T, D), BF16)],
        compiler_params=_params(("parallel",)),
    )(x2, g, w)


def _mm_tn(pieces, b, name, out_dtype=BF16):
    T, m = b.shape
    tn = 256
    counts = [p.shape[1] // tn for p in pieces]
    starts = [sum(counts[:i]) for i in range(len(pieces))]
    n_p = len(pieces)

    def body(*refs):
        b_ref, o_ref = refs[n_p], refs[n_p + 1]
        i = pl.program_id(0)
        for p in range(n_p):
            @pl.when((i >= starts[p]) & (i < starts[p] + counts[p]))
            def _(p=p):
                o_ref[...] = lax.dot_general(refs[p][...], b_ref[...], (TN, ((), ())),
                                             preferred_element_type=F32).astype(out_dtype)

    piece_spec = lambda s, c: pl.BlockSpec((T, tn), lambda i: (0, jnp.clip(i - s, 0, c - 1)))
    return pl.pallas_call(
        body, name=name,
        grid=(sum(counts),),
        in_specs=[piece_spec(s, c) for s, c in zip(starts, counts)]
        + [pl.BlockSpec((T, m), lambda i: (0, 0), pipeline_mode=pl.Buffered(1))],
        out_specs=pl.BlockSpec((tn, m), lambda i: (i, 0)),
        out_shape=jax.ShapeDtypeStruct((sum(counts) * tn, m), out_dtype),
        compiler_params=_params(("arbitrary",)),
    )(*pieces, b)


_LEVELS = (0, 1, 2, 4, 8, 16, 32)
_CUM_L = (2, 4, 8, 16, 32, 64)
_ALL_KINDS = tuple(("c", L) for L in _CUM_L) + tuple(("r", L) for L in _CUM_L)
_MXU_KINDS = (("c", 2), ("c", 4), ("c", CHUNK), ("r", 2), ("r", 4))
N_CUM = len(_ALL_KINDS) * CHUNK
N_CUM_F = len(_MXU_KINDS) * CHUNK


def _cum_matrices():
    t = np.arange(CHUNK)[:, None]
    r = np.arange(CHUNK)[None, :]

    def mat(kind):
        c, L = kind
        return ((r // L == t // L) & ((r <= t) if c == "c" else (r > t))).astype(np.float32)

    fwd = np.concatenate([mat(kd) for kd in _MXU_KINDS], axis=0)
    full = np.concatenate([mat(kd) for kd in _ALL_KINDS], axis=0)
    return jnp.asarray(fwd, BF16), jnp.asarray(full.T.copy(), BF16)


def _level_masks():
    t = np.arange(CHUNK)[:, None]
    s = np.arange(CHUNK)[None, :]
    ms = []
    for L in _LEVELS:
        if L == 0:
            ms.append(t == s)
        else:
            ms.append((t // (2 * L) == s // (2 * L)) & ((t // L) % 2 == 1) & ((s // L) % 2 == 0))
    return jnp.asarray(np.stack(ms).astype(np.float32))


def _split3(x):
    hi = x.astype(BF16)
    r1 = x - hi.astype(F32)
    mid = r1.astype(BF16)
    lo = (r1 - mid.astype(F32)).astype(BF16)
    return hi, mid, lo


def _cum3(ts, x, terms=3):
    d = lambda p: lax.dot_general(ts, p, (NN, ((), ())), preferred_element_type=F32)
    return sum(d(p) for p in _split3(x)[:terms])


def _lb_terms(lbp, layer):
    mx = jnp.max(lbp, axis=0, keepdims=True)
    e = jnp.exp(lbp - mx)
    p = e / jnp.sum(e, axis=0, keepdims=True)
    cum = p[0:1]
    for j in range(1, layer + 1):
        cum = cum + p[j:j + 1]
    lb = cum - p[0:1]
    lbf = jnp.maximum(lb, LB_FLOOR)
    return dict(lbf=lbf, one_m=1.0 - lb, kcorr=lb - lbf, ind=jnp.where(lb > LB_FLOOR, 1.0, 0.0))


def _gate(x, lt):
    sig, nsig = _sigmoids(x)
    f = lt["lbf"] + lt["one_m"] * sig
    return jnp.log(f), lt["one_m"] * nsig + lt["kcorr"], f, sig, nsig


def _ck(x, ci):
    return x[ci * CHUNK:(ci + 1) * CHUNK]


def _block_cums(ts, g, nc):
    cs = [_cum3(ts, _ck(g, ci), terms=2) for ci in range(nc)]
    out = {kind: jnp.concatenate([c[CHUNK * i:CHUNK * (i + 1)] for c in cs], axis=0)
           for i, kind in enumerate(_MXU_KINDS)}
    b = out[("c", CHUNK)]
    ng = CHUNK // 8
    last = b.reshape(nc, ng, 8, HG_DIM)[:, :, 7:8, :]
    zero = jnp.zeros((nc, 1, 1, HG_DIM), F32)

    def spread(groups):
        return jnp.broadcast_to(jnp.concatenate(groups, axis=1), (nc, ng, 8, HG_DIM)).reshape(nc * CHUNK, HG_DIM)

    def get(kind):
        if kind in out:
            return out[kind]
        c, L = kind
        nb = L // 8
        first = lambda r: (r // nb) * nb
        if c == "c":
            return b - spread([last[:, first(r) - 1:first(r)] if r >= nb else zero for r in range(ng)])
        return spread([last[:, first(r) + nb - 1:first(r) + nb] for r in range(ng)]) - b

    return get


def _level_factors(cums, g, L):
    if L == 0:
        return None, None
    if L == 1:
        return jnp.exp(g), None
    return jnp.exp(cums(("c", L))), jnp.exp(cums(("r", L)))


def _mul(a, e):
    return a if e is None else a * e


def _hg_block_fwd(qf, k, v, g, ts, m_ref, nc):
    cums = _block_cums(ts, g, nc)
    amat = [jnp.zeros((CHUNK, CHUNK), F32)] * nc
    for li, L in enumerate(_LEVELS):
        eq, ek = _level_factors(cums, g, L)
        ql, kl, m = _mul(qf, eq), _mul(k, ek), m_ref[li]
        amat = [amat[ci] + _dot(_ck(ql, ci), _ck(kl, ci), NT) * m for ci in range(nc)]
    b = cums(("c", CHUNK))
    kst = k * jnp.exp(cums(("r", CHUNK)))
    o = [_dot(amat[ci], _ck(v, ci), NN) for ci in range(nc)]
    kv = [_dot(_ck(v, ci), _ck(kst, ci), TN) for ci in range(nc)]
    dec = [jnp.exp(b[(ci + 1) * CHUNK - 1:(ci + 1) * CHUNK, :]) for ci in range(nc)]
    return o, dec, kv, qf * jnp.exp(b), amat


def _hg_block_bwd(qf, k, v, g, do, amat, ts, m_ref, nc):
    cums = _block_cums(ts, g, nc)
    dcs = {}
    da = [_dot(_ck(do, ci), _ck(v, ci), NT) for ci in range(nc)]
    dq = jnp.zeros_like(qf)
    dk = jnp.zeros_like(qf)
    dg = jnp.zeros_like(qf)
    for li, L in enumerate(_LEVELS):
        eq, ek = _level_factors(cums, g, L)
        ql, kl, m = _mul(qf, eq), _mul(k, ek), m_ref[li]
        qlb, klb = ql.astype(BF16), kl.astype(BF16)
        dal = [(da[ci] * m).astype(BF16) for ci in range(nc)]
        dql = jnp.concatenate([_dot(dal[ci], _ck(klb, ci), NN) for ci in range(nc)], axis=0)
        dkl = jnp.concatenate([_dot(dal[ci], _ck(qlb, ci), TN) for ci in range(nc)], axis=0)
        dq = dq + _mul(dql, eq)
        dk = dk + _mul(dkl, ek)
        if L == 1:
            dg = dg + dql * ql
        elif L > 1:
            dcs[("c", L)] = (dql * ql).astype(BF16)
            dcs[("r", L)] = (dkl * kl).astype(BF16)
    b = cums(("c", CHUNK))
    e64 = jnp.exp(b)
    er64 = jnp.exp(cums(("r", CHUNK)))
    qb = qf * e64
    return dict(dq=dq, dk=dk, dg=dg, dcs=dcs, e64=e64, er64=er64, qb=qb, kst=k * er64,
                dv=[_dot(amat[ci], _ck(do, ci), TN) for ci in range(nc)],
                dec=[jnp.exp(b[(ci + 1) * CHUNK - 1:(ci + 1) * CHUNK, :]) for ci in range(nc)],
                qd=[_dot(_ck(do, ci), _ck(qb, ci), TN) for ci in range(nc)])


def _hg_state_bwd(w, v, do, starts, ends, tst, nc):
    dqb = jnp.concatenate([_dot(_ck(do, ci), starts[ci], NN) for ci in range(nc)], axis=0)
    dkst = jnp.concatenate([_dot(_ck(v, ci), ends[ci], NN) for ci in range(nc)], axis=0)
    dq = w["dq"] + dqb * w["e64"]
    dk = w["dk"] + dkst * w["er64"]
    dv = jnp.concatenate([w["dv"][ci] + _dot(_ck(w["kst"], ci), ends[ci], NT) for ci in range(nc)], axis=0)
    trow = lax.broadcasted_iota(jnp.int32, (CHUNK, 1), 0)
    dtot = jnp.concatenate(
        [jnp.where(trow == CHUNK - 1, jnp.sum(ends[ci] * starts[ci], axis=0, keepdims=True) * w["dec"][ci], 0.0)
         for ci in range(nc)], axis=0)
    dcs = dict(w["dcs"])
    dcs[("c", CHUNK)] = (dqb * w["qb"] + dtot).astype(BF16)
    dcs[("r", CHUNK)] = (dkst * w["kst"]).astype(BF16)
    dgs = [_dot(tst, jnp.concatenate([_ck(dcs[kind], ci) for kind in _ALL_KINDS], axis=0), NN) for ci in range(nc)]
    return dq, dk, dv, w["dg"] + jnp.concatenate(dgs, axis=0)


def _hgrn_fwd(proj_h, u_rows, lb_param, g_head, layer, name, phase=None):
    B, S, _ = proj_h.shape
    sb = _pick(S, (2048, 1024, 512, 256, 128, 64))
    nc = sb // CHUNK
    ts, _ = _cum_matrices()

    def body(*refs):
        ins, outs, (st,), p_in, p_out, p_sems = _split_refs(refs, 8, 4, 1, phase)
        q_ref, f_ref, i_ref, z_ref, lbp_ref, gh_ref, ts_ref, m_ref = ins
        o_ref, u_ref, sts_ref, am_ref = outs
        h_id, b_id, s_id = pl.program_id(0), pl.program_id(1), pl.program_id(2)
        _hosted_start(phase, p_in, p_out, p_sems, (h_id == 0) & (b_id == 0) & (s_id == 0))

        @pl.when(s_id == 0)
        def _():
            st[...] = jnp.zeros_like(st)

        lt = _lb_terms(lbp_ref[...], layer)
        tsv = ts_ref[...]
        gh = gh_ref[...]
        logf, k = _gate(f_ref[...], lt)[:2]
        o_part, dec, kv, qb, amat = _hg_block_fwd(_silu(q_ref[...]), k, i_ref[...], logf, tsv, m_ref, nc)
        for ci in range(nc):
            am_ref[ci] = amat[ci].astype(BF16)
        cur = st[...]
        starts = []
        for ci in range(nc):
            sts_ref[ci] = cur
            starts.append(cur)
            cur = cur * dec[ci] + kv[ci]
        st[...] = cur
        o = jnp.concatenate([o_part[ci] + _dot(_ck(qb, ci), starts[ci], NT) for ci in range(nc)], axis=0)
        o_ref[...] = o
        r = lax.rsqrt(jnp.mean(o * o, axis=-1, keepdims=True) + NORM_EPS)
        u_ref[...] = (((o * r) * gh) * _silu(z_ref[...])).astype(BF16)
        _hosted_finish(phase, p_in, p_out, p_sems, (h_id == HG_HEADS - 1) & (b_id == B - 1) & (s_id == S // sb - 1))

    col = lambda base: pl.BlockSpec((None, sb, HG_DIM), lambda h, b, s: (b, s, base + h))
    p_ispecs, p_ospecs, p_oshapes, p_alias, p_scratch, p_args = _host_phase(phase, 8, 4)
    res = pl.pallas_call(
        body, name=name,
        grid=(HG_HEADS, B, S // sb),
        in_specs=[col(0), col(HG_HEADS), col(2 * HG_HEADS), col(3 * HG_HEADS),
                  pl.BlockSpec((DEPTH, HG_DIM), lambda h, b, s: (0, h)),
                  pl.BlockSpec((1, HG_DIM), lambda h, b, s: (0, 0)),
                  pl.BlockSpec((N_CUM_F, CHUNK), lambda h, b, s: (0, 0)),
                  pl.BlockSpec((len(_LEVELS), CHUNK, CHUNK), lambda h, b, s: (0, 0, 0))] + p_ispecs,
        out_specs=[col(0), col(0),
                   pl.BlockSpec((None, None, nc, HG_DIM, HG_DIM), lambda h, b, s: (b, h, s, 0, 0)),
                   pl.BlockSpec((None, None, nc, CHUNK, CHUNK), lambda h, b, s: (b, h, s, 0, 0))] + p_ospecs,
        out_shape=[jax.ShapeDtypeStruct((B, S, HG_WIDTH), F32),
                   jax.ShapeDtypeStruct((B, S, u_rows), BF16),
                   jax.ShapeDtypeStruct((B, HG_HEADS, S // CHUNK, HG_DIM, HG_DIM), F32),
                   jax.ShapeDtypeStruct((B, HG_HEADS, S // CHUNK, CHUNK, CHUNK), BF16)] + p_oshapes,
        input_output_aliases=p_alias,
        scratch_shapes=[pltpu.VMEM((HG_DIM, HG_DIM), F32)] + p_scratch,
        compiler_params=_params(("arbitrary", "arbitrary", "arbitrary")),
    )(proj_h, proj_h, proj_h, proj_h, lb_param, g_head, ts, _level_masks(), *p_args)
    return res[0], res[1], (res[2], res[3]), list(res[4:])


def _hgrn_bwd(proj_h, o_h, du, kept, lb_param, g_head, layer, name, phase=None):
    B, S, _ = proj_h.shape
    sb = _pick(S, (512, 256, 128, 64))
    nc = sb // CHUNK
    ns = S // sb
    ts, tst = _cum_matrices()

    def body(*refs):
        ins, outs, (dst,), p_in, p_out, p_sems = _split_refs(refs, 13, 6, 1, phase)
        q_ref, f_ref, i_ref, z_ref, o_ref, du_ref, sts_ref, am_ref, lbp_ref, gh_ref, ts_ref, tst_ref, m_ref = ins
        dq_ref, df_ref, di_ref, dz_ref, dlb_ref, dgh_ref = outs
        h_id, b_id, s_id = pl.program_id(0), pl.program_id(1), pl.program_id(2)
        _hosted_start(phase, p_in, p_out, p_sems, (h_id == 0) & (b_id == 0) & (s_id == 0))

        @pl.when(s_id == 0)
        def _():
            dst[...] = jnp.zeros_like(dst)

        @pl.when((b_id == 0) & (s_id == 0))
        def _():
            dlb_ref[...] = jnp.zeros_like(dlb_ref)

        @pl.when((h_id == 0) & (b_id == 0) & (s_id == 0))
        def _():
            dgh_ref[...] = jnp.zeros_like(dgh_ref)

        lt = _lb_terms(lbp_ref[...], layer)
        gh = gh_ref[...]
        tsv = ts_ref[...]
        tstv = tst_ref[...]
        logf, k, f, sig, nsig = _gate(f_ref[...], lt)
        o = o_ref[...]
        dub = du_ref[...]
        r = lax.rsqrt(jnp.mean(o * o, axis=-1, keepdims=True) + NORM_EPS)
        n = o * r
        sg, sg_grad = _silu_and_grad(z_ref[...])
        dz_ref[...] = (dub * (n * gh) * sg_grad).astype(BF16)
        dgh_ref[...] += jnp.sum(dub * sg * n, axis=0, keepdims=True)
        dn = dub * sg * gh
        do = r * (dn - n * jnp.mean(dn * n, axis=-1, keepdims=True))
        v = i_ref[...]
        qf, qf_grad = _silu_and_grad(q_ref[...])
        w = _hg_block_bwd(qf, k, v, logf, do, [am_ref[ci] for ci in range(nc)], tsv, m_ref, nc)
        cur = dst[...]
        ends = [None] * nc
        for ci in reversed(range(nc)):
            ends[ci] = cur
            cur = cur * w["dec"][ci] + w["qd"][ci]
        dst[...] = cur
        dq, dk, dv, dg = _hg_state_bwd(w, v, do, [sts_ref[ci] for ci in range(nc)], ends, tstv, nc)
        di_ref[...] = dv.astype(BF16)
        dq_ref[...] = (dq * qf_grad).astype(BF16)
        scaled = (dg - f * dk) / f
        df_ref[...] = (scaled * lt["one_m"] * sig * nsig).astype(BF16)
        dlb_ref[...] += jnp.sum(scaled * (lt["ind"] - sig), axis=0, keepdims=True)
        _hosted_finish(phase, p_in, p_out, p_sems, (h_id == HG_HEADS - 1) & (b_id == B - 1) & (s_id == ns - 1))

    col = lambda base: pl.BlockSpec((None, sb, HG_DIM), lambda h, b, s: (b, ns - 1 - s, base + h))
    out_col = pl.BlockSpec((None, sb, HG_DIM), lambda h, b, s: (b, ns - 1 - s, h))
    dt = jax.ShapeDtypeStruct((B, S, HG_WIDTH), BF16)
    p_ispecs, p_ospecs, p_oshapes, p_alias, p_scratch, p_args = _host_phase(phase, 13, 6)
    res = pl.pallas_call(
        body, name=name,
        grid=(HG_HEADS, B, ns),
        in_specs=[col(0), col(HG_HEADS), col(2 * HG_HEADS), col(3 * HG_HEADS), col(0), col(0),
                  pl.BlockSpec((None, None, nc, HG_DIM, HG_DIM), lambda h, b, s: (b, h, ns - 1 - s, 0, 0)),
                  pl.BlockSpec((None, None, nc, CHUNK, CHUNK), lambda h, b, s: (b, h, ns - 1 - s, 0, 0)),
                  pl.BlockSpec((DEPTH, HG_DIM), lambda h, b, s: (0, h)),
                  pl.BlockSpec((1, HG_DIM), lambda h, b, s: (0, 0)),
                  pl.BlockSpec((N_CUM_F, CHUNK), lambda h, b, s: (0, 0)),
                  pl.BlockSpec((CHUNK, N_CUM), lambda h, b, s: (0, 0)),
                  pl.BlockSpec((len(_LEVELS), CHUNK, CHUNK), lambda h, b, s: (0, 0, 0))] + p_ispecs,
        out_specs=[out_col, out_col, out_col, out_col,
                   pl.BlockSpec((1, HG_DIM), lambda h, b, s: (0, h)),
                   pl.BlockSpec((1, HG_DIM), lambda h, b, s: (0, 0))] + p_ospecs,
        out_shape=[dt, dt, dt, dt, jax.ShapeDtypeStruct((1, HG_WIDTH), F32),
                   jax.ShapeDtypeStruct((1, HG_DIM), F32)] + p_oshapes,
        input_output_aliases=p_alias,
        scratch_shapes=[pltpu.VMEM((HG_DIM, HG_DIM), F32)] + p_scratch,
        compiler_params=_params(("arbitrary", "arbitrary", "arbitrary")),
    )(proj_h, proj_h, proj_h, proj_h, o_h, du, kept[0], kept[1], lb_param, g_head, ts, tst, _level_masks(), *p_args)
    return tuple(res[:6]) + (list(res[6:]),)


def _rope_tables(S):
    half = ATT_DIM // 2
    inv_freq = ROPE_THETA ** (-jnp.arange(half, dtype=F32) / half)
    ang = jnp.arange(S).astype(F32)[:, None] * inv_freq[None, :]
    cos = jnp.cos(ang)
    sin = jnp.sin(ang)
    cos = jnp.concatenate([cos, cos, cos, cos], axis=1)
    sin = jnp.concatenate([-sin, sin, -sin, sin], axis=1)
    return cos, sin


def _attn_common():
    lane = lax.broadcasted_iota(jnp.int32, (1, 2 * ATT_DIM), 1)
    first_half = (lane % ATT_DIM) < (ATT_DIM // 2)
    left = lane < ATT_DIM

    def swap(x):
        return jnp.where(first_half, pltpu.roll(x, 128 - ATT_DIM // 2, 1), pltpu.roll(x, ATT_DIM // 2, 1))

    def rope(x, cos, sin):
        return x * cos + swap(x) * sin

    def rope_bwd(dy, cos, sin):
        return dy * cos + swap(dy * sin)

    def dup(x):
        xs = pltpu.roll(x, ATT_DIM, 1)
        return [jnp.where(left, x, xs), jnp.where(left, xs, x)]

    return left, rope, rope_bwd, dup


GROUP = ATT_HEADS // 2
GROUP_ROWS = GROUP * ATT_BLOCK


def _attn_bias(i):
    r = lax.broadcasted_iota(jnp.int32, (ATT_BLOCK, 2 * ATT_BLOCK), 0)
    c = lax.broadcasted_iota(jnp.int32, (ATT_BLOCK, 2 * ATT_BLOCK), 1)
    ok = (c > r) & (c <= r + ATT_BLOCK) & ((c >= ATT_BLOCK) | (i > 0))
    return jnp.where(ok, 0.0, NEG_INF)


def _stack_heads(pairs, left):
    rows = []
    for x in pairs:
        rows += [jnp.where(left, x, 0.0), jnp.where(left, 0.0, x)]
    return jnp.concatenate(rows, axis=0)


def _unstack_heads(y, left, pp):
    r0 = 2 * pp * ATT_BLOCK
    return jnp.where(left, y[r0:r0 + ATT_BLOCK], y[r0 + ATT_BLOCK:r0 + 2 * ATT_BLOCK])


def _row_sums(x):
    return _dot(x, jnp.ones((x.shape[1], 128), BF16), NN)


def _attn_probs(qs, kd, vd, sink, bias):
    n = range(len(qs))
    rows = qs[0].shape[0]
    s = [(_dot(qs[j], kd[j], NT).reshape(rows // ATT_BLOCK, ATT_BLOCK, 2 * ATT_BLOCK) * ATT_SCALE + bias[None])
         .reshape(rows, 2 * ATT_BLOCK) for j in n]
    m = [jnp.max(jnp.maximum(jnp.maximum(s[j][:, :128], s[j][:, 128:]), sink[j]), axis=-1, keepdims=True) for j in n]
    pu = [jnp.exp(s[j] - m[j]) for j in n]
    es = [jnp.exp(sink[j] - m[j]) for j in n]
    ones = jnp.ones((2 * ATT_BLOCK, 128), BF16)
    ov = [_dot(pu[j], jnp.concatenate([vd[j].astype(BF16), ones], axis=1), NN) for j in n]
    inv = [1.0 / (ov[j][:, 128:] + es[j]) for j in n]
    return ([pu[j] * jnp.concatenate([inv[j], inv[j]], axis=1) for j in n], [es[j] * inv[j] for j in n],
            [ov[j][:, :128] * inv[j] for j in n])


def _sink_rows(sinks_l):
    return jnp.broadcast_to(jnp.repeat(sinks_l, ATT_BLOCK)[:, None], (ATT_HEADS * ATT_BLOCK, 128))


_Z0 = (2 * ATT_WIDTH + 2 * KV_WIDTH - ATT_WIDTH) // 256


def _attn_fwd(proj_a, u, sinks_l, cos, sin, name, phase=None):
    B, S, _ = proj_a.shape
    nb = S // ATT_BLOCK

    def body(*refs):
        ins, (u_ref, p_ref, o_ref, ps_ref, qs_ref), _, p_in, p_out, p_sems = _split_refs(refs, 13, 5, 0, phase)
        q_ref, kvc_ref, kvp_ref, z0, z1, z2, z3, cos_ref, sin_ref, cosp_ref, sinp_ref, sinks_ref, _ = ins
        i = pl.program_id(1)
        _hosted_start(phase, p_in, p_out, p_sems, (pl.program_id(0) == 0) & (i == 0))
        left, rope, _, dup = _attn_common()
        cos_c, sin_c = cos_ref[...], sin_ref[...]
        kvc = kvc_ref[...]
        kvp = kvp_ref[...]
        kw = jnp.concatenate([rope(kvp[:, :KV_WIDTH], cosp_ref[...], sinp_ref[...]),
                              rope(kvc[:, :KV_WIDTH], cos_c, sin_c)], axis=0)
        vw = jnp.concatenate([kvp[:, KV_WIDTH:], kvc[:, KV_WIDTH:]], axis=0)
        kd, vd = dup(kw), dup(vw)
        bias = _attn_bias(i)
        zs = (z0, z1, z2, z3)
        pairs = [range(4 * kvh, 4 * kvh + 4) for kvh in range(2)]
        qs = [_stack_heads([rope(q_ref[:, 128 * pr:128 * (pr + 1)], cos_c, sin_c) for pr in pairs[kvh]], left)
              for kvh in range(2)]
        sink = [sinks_ref[kvh * GROUP_ROWS:(kvh + 1) * GROUP_ROWS, :] for kvh in range(2)]
        p, ps, o = _attn_probs(qs, kd, vd, sink, bias)
        eye = (lax.broadcasted_iota(jnp.int32, (ATT_BLOCK, 128), 0)
               == lax.broadcasted_iota(jnp.int32, (ATT_BLOCK, 128), 1))
        for kvh in range(2):
            p_ref[kvh] = p[kvh].astype(BF16)
            qs_ref[kvh] = qs[kvh].astype(BF16)
            for g in range(GROUP):
                blk = ps[kvh][g * ATT_BLOCK:(g + 1) * ATT_BLOCK, :]
                ps_ref[kvh * GROUP + g:kvh * GROUP + g + 1, :] = jnp.sum(jnp.where(eye, blk, 0.0), axis=0, keepdims=True)
            for pp, pr in enumerate(pairs[kvh]):
                z = zs[pr // 2][:, 128 * (pr % 2):128 * (pr % 2 + 1)]
                o128 = _unstack_heads(o[kvh], left, pp)
                o_ref[:, 128 * pr:128 * (pr + 1)] = o128.astype(BF16)
                u_ref[:, 128 * pr:128 * (pr + 1)] = (o128 * _silu(z)).astype(BF16)
        _hosted_finish(phase, p_in, p_out, p_sems, (pl.program_id(0) == B - 1) & (i == nb - 1))

    rowblk = lambda w, cb: pl.BlockSpec((None, ATT_BLOCK, w), lambda b, i: (b, i, cb))
    tab = pl.BlockSpec((ATT_BLOCK, 128), lambda b, i: (i, 0))
    tabp = pl.BlockSpec((ATT_BLOCK, 128), lambda b, i: (jnp.maximum(i - 1, 0), 0))
    p_ispecs, p_ospecs, p_oshapes, p_alias, p_scratch, p_args = _host_phase(phase, 13, 5)
    res = pl.pallas_call(
        body, name=name,
        grid=(B, nb),
        in_specs=[rowblk(ATT_WIDTH, 0), rowblk(256, 4),
                  pl.BlockSpec((None, ATT_BLOCK, 256), lambda b, i: (b, jnp.maximum(i - 1, 0), 4)),
                  rowblk(256, _Z0), rowblk(256, _Z0 + 1), rowblk(256, _Z0 + 2), rowblk(256, _Z0 + 3),
                  tab, tab, tabp, tabp,
                  pl.BlockSpec((ATT_HEADS * ATT_BLOCK, 128), lambda b, i: (0, 0)),
                  pl.BlockSpec(memory_space=pl.ANY)] + p_ispecs,
        out_specs=[pl.BlockSpec((None, ATT_BLOCK, ATT_WIDTH), lambda b, i: (b, i, 1)),
                   pl.BlockSpec((None, None, 2, GROUP_ROWS, 2 * ATT_BLOCK), lambda b, i: (b, i, 0, 0, 0)),
                   pl.BlockSpec((None, ATT_BLOCK, ATT_WIDTH), lambda b, i: (b, i, 0)),
                   pl.BlockSpec((None, None, ATT_HEADS, 128), lambda b, i: (b, i, 0, 0)),
                   pl.BlockSpec((None, None, 2, GROUP_ROWS, 128), lambda b, i: (b, i, 0, 0, 0))] + p_ospecs,
        out_shape=[jax.ShapeDtypeStruct(u.shape, BF16),
                   jax.ShapeDtypeStruct((B, nb, 2, GROUP_ROWS, 2 * ATT_BLOCK), BF16),
                   jax.ShapeDtypeStruct((B, S, ATT_WIDTH), BF16),
                   jax.ShapeDtypeStruct((B, nb, ATT_HEADS, 128), F32),
                   jax.ShapeDtypeStruct((B, nb, 2, GROUP_ROWS, 128), BF16)] + p_oshapes,
        input_output_aliases={12: 0, **p_alias},
        scratch_shapes=p_scratch,
        compiler_params=_params(("arbitrary", "arbitrary")),
    )(proj_a, proj_a, proj_a, proj_a, proj_a, proj_a, proj_a, cos, sin, cos, sin, sinks_l, u, *p_args)
    return res[0], tuple(res[1:5]), list(res[5:])


def _attn_bwd(proj_a, du, kept, cos, sin, name, phase=None):
    B, S, _ = proj_a.shape
    nb = S // ATT_BLOCK
    p_kept, o_kept, ps_kept, qs_kept = kept

    def body(*refs):
        ins, outs, (carry, sk_acc), p_in, p_out, p_sems = _split_refs(refs, 15, 4, 2, phase)
        (qs_ref, kvc_ref, kvp_ref, z0, z1, z2, z3, du_ref, cos_ref, sin_ref, cosp_ref, sinp_ref,
         p_ref, o_ref, ps_ref) = ins
        dq_ref, dkv_ref, dz_ref, dsk_ref = outs
        b_id, i = pl.program_id(0), pl.program_id(1)
        _hosted_start(phase, p_in, p_out, p_sems, (b_id == 0) & (i == 0))

        @pl.when((b_id == 0) & (i == 0))
        def _():
            sk_acc[...] = jnp.zeros_like(sk_acc)

        @pl.when(i == 0)
        def _():
            carry[...] = jnp.zeros_like(carry)

        @pl.when(i < nb)
        def _():
            left, rope, rope_bwd, dup = _attn_common()
            cos_c, sin_c = cos_ref[...], sin_ref[...]
            cos_p, sin_p = cosp_ref[...], sinp_ref[...]
            kvc = kvc_ref[...]
            kvp = kvp_ref[...]
            kw = jnp.concatenate([rope(kvp[:, :KV_WIDTH], cos_p, sin_p), rope(kvc[:, :KV_WIDTH], cos_c, sin_c)], axis=0)
            vw = jnp.concatenate([kvp[:, KV_WIDTH:], kvc[:, KV_WIDTH:]], axis=0)
            kd, vd = dup(kw), dup(vw)
            zs = (z0, z1, z2, z3)
            units = [(kvh, hf) for kvh in range(2) for hf in range(2)]
            half = GROUP_ROWS // 2
            pairs = [range(4 * kvh + 2 * hf, 4 * kvh + 2 * hf + 2) for kvh, hf in units]
            ku = [kd[kvh] for kvh, _ in units]
            vu = [vd[kvh] for kvh, _ in units]
            ps_all = ps_ref[...]
            head_row = lax.broadcasted_iota(jnp.int32, (ATT_HEADS, 128), 0)
            eye = (lax.broadcasted_iota(jnp.int32, (ATT_BLOCK, 128), 0)
                   == lax.broadcasted_iota(jnp.int32, (ATT_BLOCK, 128), 1))

            def first(j):
                kvh, hf = units[j]
                p = p_ref[kvh, hf * half:(hf + 1) * half, :]
                parts = []
                for pr in pairs[j]:
                    cols = slice(128 * pr, 128 * (pr + 1))
                    sg, sg_grad = _silu_and_grad(zs[pr // 2][:, 128 * (pr % 2):128 * (pr % 2 + 1)])
                    du128 = du_ref[:, cols]
                    dz_ref[:, cols] = (du128 * o_ref[:, cols].astype(F32) * sg_grad).astype(BF16)
                    parts.append(du128 * sg)
                dos = _stack_heads(parts, left)
                dp = _dot(dos, vu[j], NT)
                delta = _row_sums(p.astype(F32) * dp)
                ds = (p.astype(F32) * (dp - jnp.concatenate([delta, delta], axis=1)) * ATT_SCALE).astype(BF16)
                sk = jnp.zeros((ATT_HEADS, 128), F32)
                for hh in range(4):
                    hd = kvh * GROUP + 4 * hf + hh
                    drow = jnp.sum(jnp.where(eye, delta[hh * ATT_BLOCK:(hh + 1) * ATT_BLOCK, :], 0.0), axis=0,
                                   keepdims=True)
                    sk = sk - jnp.where(head_row == hd, ps_all * drow, 0.0)
                sk_acc[...] += sk
                return ds, p, dos.astype(BF16), qs_ref[kvh, hf * half:(hf + 1) * half, :]

            def second(j, ds, p, dos, qs):
                dqs = _dot(ds, ku[j], NN)
                for pp, pr in enumerate(pairs[j]):
                    dq_ref[:, 128 * pr:128 * (pr + 1)] = rope_bwd(_unstack_heads(dqs, left, pp),
                                                                  cos_c, sin_c).astype(BF16)
                return _dot(ds, qs, TN), _dot(p, dos, TN)

            got, dku, dvu = {}, [None] * len(units), [None] * len(units)
            for j in range(len(units) + 1):
                if j < len(units):
                    got[j] = first(j)
                if j >= 1:
                    dku[j - 1], dvu[j - 1] = second(j - 1, *got.pop(j - 1))
            dkd = [dku[0] + dku[1], dku[2] + dku[3]]
            dvd = [dvu[0] + dvu[1], dvu[2] + dvu[3]]
            fold = lambda pr: jnp.where(left, pr[0] + pltpu.roll(pr[0], ATT_DIM, 1), pr[1] + pltpu.roll(pr[1], ATT_DIM, 1))
            dkw = fold(dkd)
            dvw = fold(dvd)
            prev = jnp.concatenate([rope_bwd(dkw[:ATT_BLOCK], cos_p, sin_p), dvw[:ATT_BLOCK]], axis=1)
            cur = jnp.concatenate([rope_bwd(dkw[ATT_BLOCK:], cos_c, sin_c), dvw[ATT_BLOCK:]], axis=1)
            dkv_ref[...] = (carry[...] + prev).astype(BF16)
            carry[...] = cur

        @pl.when(i == nb)
        def _():
            dkv_ref[...] = carry[...].astype(BF16)

        @pl.when((b_id == B - 1) & (i == nb))
        def _():
            diag = (lax.broadcasted_iota(jnp.int32, (ATT_HEADS, 128), 0)
                    == lax.broadcasted_iota(jnp.int32, (ATT_HEADS, 128), 1))
            tot = jnp.sum(sk_acc[...], axis=1, keepdims=True)
            dsk_ref[...] = jnp.sum(jnp.where(diag, tot, 0.0), axis=0, keepdims=True)

        _hosted_finish(phase, p_in, p_out, p_sems, (b_id == B - 1) & (i == nb))

    cl = lambda i: jnp.minimum(i, nb - 1)
    pv = lambda i: jnp.maximum(jnp.minimum(i, nb - 1) - 1, 0)
    rowblk = lambda w, cb: pl.BlockSpec((None, ATT_BLOCK, w), lambda b, i: (b, cl(i), cb))
    tab = pl.BlockSpec((ATT_BLOCK, 128), lambda b, i: (cl(i), 0))
    tabp = pl.BlockSpec((ATT_BLOCK, 128), lambda b, i: (pv(i), 0))
    p_ispecs, p_ospecs, p_oshapes, p_alias, p_scratch, p_args = _host_phase(phase, 15, 4)
    res = pl.pallas_call(
        body, name=name,
        grid=(B, nb + 1),
        in_specs=[pl.BlockSpec((None, None, 2, GROUP_ROWS, 128), lambda b, i: (b, cl(i), 0, 0, 0)), rowblk(256, 4),
                  pl.BlockSpec((None, ATT_BLOCK, 256), lambda b, i: (b, pv(i), 4)),
                  rowblk(256, _Z0), rowblk(256, _Z0 + 1), rowblk(256, _Z0 + 2), rowblk(256, _Z0 + 3),
                  rowblk(ATT_WIDTH, 1),
                  tab, tab, tabp, tabp,
                  pl.BlockSpec((None, None, 2, GROUP_ROWS, 2 * ATT_BLOCK), lambda b, i: (b, cl(i), 0, 0, 0)),
                  rowblk(ATT_WIDTH, 0),
                  pl.BlockSpec((None, None, ATT_HEADS, 128), lambda b, i: (b, cl(i), 0, 0))] + p_ispecs,
        out_specs=[rowblk(ATT_WIDTH, 0),
                   pl.BlockSpec((None, ATT_BLOCK, 256), lambda b, i: (b, jnp.maximum(i - 1, 0), 0)),
                   rowblk(ATT_WIDTH, 0),
                   pl.BlockSpec((1, 128), lambda b, i: (0, 0))] + p_ospecs,
        out_shape=[jax.ShapeDtypeStruct((B, S, ATT_WIDTH), BF16), jax.ShapeDtypeStruct((B, S, 256), BF16),
                   jax.ShapeDtypeStruct((B, S, ATT_WIDTH), BF16), jax.ShapeDtypeStruct((1, 128), F32)] + p_oshapes,
        input_output_aliases=p_alias,
        scratch_shapes=[pltpu.VMEM((ATT_BLOCK, 256), F32), pltpu.VMEM((ATT_HEADS, 128), F32)] + p_scratch,
        compiler_params=_params(("arbitrary", "arbitrary")),
    )(qs_kept, proj_a, proj_a, proj_a, proj_a, proj_a, proj_a, du, cos, sin, cos, sin, p_kept, o_kept, ps_kept, *p_args)
    return tuple(res[:4]) + (list(res[4:]),)


def _outproj_fwd(u2, w_out, x2, g_post, target2, name):
    T, D = x2.shape
    tm = _pick(T, (512, 256, 128))
    last = target2 is not None

    def body(u_ref, w_ref, x_ref, g_ref, *rest):
        y = lax.dot_general(u_ref[...], w_ref[...], (NN, ((), ())), preferred_element_type=F32)
        r = lax.rsqrt(jnp.mean(y * y, axis=-1, keepdims=True) + NORM_EPS)
        xn = x_ref[...] + (y * r) * g_ref[...]
        if last:
            t_ref, y_ref, dx_ref, loss_ref = rest
            err = xn - t_ref[...]
            dx_ref[...] = err * (1.0 / D)
            sq = err * err
            acc = sq[:, 0:128]
            for kk in range(1, D // 128):
                acc = acc + sq[:, 128 * kk:128 * (kk + 1)]
            part = jnp.sum(acc.reshape(tm // 8, 8, 128), axis=0) * (0.5 / D)

            @pl.when(pl.program_id(0) == 0)
            def _():
                loss_ref[...] = jnp.zeros_like(loss_ref)

            loss_ref[...] += part
        else:
            y_ref, xn_ref = rest
            xn_ref[...] = xn
        y_ref[...] = y

    row = pl.BlockSpec((tm, D), lambda i: (i, 0))
    in_specs = [pl.BlockSpec((tm, MIX_WIDTH), lambda i: (i, 0)),
                pl.BlockSpec((MIX_WIDTH, D), lambda i: (0, 0)), row,
                pl.BlockSpec((1, D), lambda i: (0, 0))]
    args = [u2, w_out, x2, g_post]
    out_specs = [row, row]
    out_shape = [jax.ShapeDtypeStruct((T, D), F32), jax.ShapeDtypeStruct((T, D), F32)]
    if last:
        in_specs.append(row)
        args.append(target2)
        out_specs.append(pl.BlockSpec((8, 128), lambda i: (0, 0)))
        out_shape.append(jax.ShapeDtypeStruct((8, 128), F32))
    return pl.pallas_call(
        body, name=name, grid=(T // tm,), in_specs=in_specs, out_specs=out_specs, out_shape=out_shape,
        compiler_params=_params(("arbitrary",)),
    )(*args)


def _outproj_bwd(dxn2, y2, g_post, w_out, u2, name):
    T, D = y2.shape
    N = w_out.shape[0]
    tm = _pick(T, (512, 256, 128))
    nt = T // tm

    def body(dx_ref, y_ref, g_ref, w_ref, u_ref, dg_ref, du_ref, dw_ref, acc, wacc):
        i = pl.program_id(0)

        @pl.when(i == 0)
        def _():
            acc[...] = jnp.zeros_like(acc)
            wacc[...] = jnp.zeros_like(wacc)

        y = y_ref[...]
        dxn = dx_ref[...]
        r = lax.rsqrt(jnp.mean(y * y, axis=-1, keepdims=True) + NORM_EPS)
        n = y * r
        dn = dxn * g_ref[...]
        dy = (r * (dn - n * jnp.mean(dn * n, axis=-1, keepdims=True))).astype(BF16)
        du_ref[...] = lax.dot_general(dy, w_ref[...], (NT, ((), ())), preferred_element_type=F32)
        wacc[...] += lax.dot_general(u_ref[...], dy, (TN, ((), ())), preferred_element_type=F32)
        acc[...] += jnp.sum((dxn * n).reshape(tm // 8, 8, D), axis=0)

        @pl.when(i == nt - 1)
        def _():
            dg_ref[...] = jnp.sum(acc[...], axis=0, keepdims=True)
            dw_ref[...] = wacc[...].astype(BF16)

    row = pl.BlockSpec((tm, D), lambda i: (i, 0))
    wide = pl.BlockSpec((tm, N), lambda i: (i, 0))
    vec = pl.BlockSpec((1, D), lambda i: (0, 0))
    whole = pl.BlockSpec((N, D), lambda i: (0, 0))
    return pl.pallas_call(
        body, name=name, grid=(nt,),
        in_specs=[row, row, vec, pl.BlockSpec((N, D), lambda i: (0, 0), pipeline_mode=pl.Buffered(1)), wide],
        out_specs=[vec, wide, whole],
        out_shape=[jax.ShapeDtypeStruct((1, D), F32), jax.ShapeDtypeStruct((T, N), F32),
                   jax.ShapeDtypeStruct((N, D), BF16)],
        scratch_shapes=[pltpu.VMEM((8, D), F32), pltpu.VMEM((N, D), F32)],
        compiler_params=_params(("arbitrary",)),
    )(dxn2, y2, g_post, w_out, u2)


def _inproj_bwd(pieces, w_t, x2, dxn2, g_pre, name, phase=None):
    T, D = x2.shape
    widths = [p.shape[1] for p in pieces]
    offs = [sum(widths[:i]) for i in range(len(pieces))]
    n_p = len(pieces)
    tm = _pick(T, (256, 128))
    nt = T // tm

    def body(*refs):
        ins, (dx_ref, dg_ref), (acc,), p_in, p_out, p_sems = _split_refs(refs, n_p + 4, 2, 1, phase)
        w_ref, x_ref, dxn_ref, g_ref = ins[n_p:]
        i = pl.program_id(0)
        _hosted_start(phase, p_in, p_out, p_sems, i == 0)

        @pl.when(i == 0)
        def _():
            acc[...] = jnp.zeros_like(acc)

        dh = jnp.zeros((tm, D), F32)
        for p in range(n_p):
            dh = dh + lax.dot_general(ins[p][...], w_ref[offs[p]:offs[p] + widths[p], :], (NN, ((), ())),
                                      preferred_element_type=F32)
        x = x_ref[...]
        r = lax.rsqrt(jnp.mean(x * x, axis=-1, keepdims=True) + NORM_EPS)
        n = x * r
        dn = dh * g_ref[...]
        dx_ref[...] = dxn_ref[...] + r * (dn - n * jnp.mean(dn * n, axis=-1, keepdims=True))
        acc[...] += jnp.sum((dh * n).reshape(tm // 8, 8, D), axis=0)

        @pl.when(i == nt - 1)
        def _():
            dg_ref[...] = jnp.sum(acc[...], axis=0, keepdims=True)

        _hosted_finish(phase, p_in, p_out, p_sems, i == nt - 1)

    row = pl.BlockSpec((tm, D), lambda i: (i, 0))
    vec = pl.BlockSpec((1, D), lambda i: (0, 0))
    p_ispecs, p_ospecs, p_oshapes, p_alias, p_scratch, p_args = _host_phase(phase, n_p + 4, 2)
    res = pl.pallas_call(
        body, name=name, grid=(nt,),
        in_specs=[pl.BlockSpec((tm, w), lambda i: (i, 0)) for w in widths]
        + [pl.BlockSpec((sum(widths), D), lambda i: (0, 0), pipeline_mode=pl.Buffered(1)), row, row, vec] + p_ispecs,
        out_specs=[row, vec] + p_ospecs,
        out_shape=[jax.ShapeDtypeStruct((T, D), F32), jax.ShapeDtypeStruct((1, D), F32)] + p_oshapes,
        input_output_aliases=p_alias,
        scratch_shapes=[pltpu.VMEM((8, D), F32)] + p_scratch,
        compiler_params=_params(("arbitrary",)),
    )(*pieces, w_t, x2, dxn2, g_pre, *p_args)
    return res[0], res[1], list(res[2:])


def _step(x, target, g_pre, g_post, lb_param, g_head, sinks, shards=None, full=None):
    B, S, D = x.shape
    T = B * S
    dist = shards is not None
    first, last = 0, DEPTH - 1
    if dist:
        a_loc, b_loc = shards
        ra, rb = a_loc.shape[1], b_loc.shape[1]
        side = _own_side_blocks()
        placed = lambda loc, nm: _place_own(loc, side, "place_" + nm)
        w_in0 = _gather_one_call(a_loc[0], placed(a_loc[0], "in0"), "gather_in0")
        w_in, w_out = [w_in0, None], [None, None]
    else:
        w_in, w_out = list(full[0]), list(full[1])
    cos, sin = _rope_tables(S)
    saved = []
    xs = x
    loss_part = None
    dxn = None
    for l in range(DEPTH):
        x2 = xs.reshape(T, D)
        proj_h, proj_a, h = _inproj(x2, g_pre[l:l + 1], w_in[l], f"inproj{l}")
        proj_h = proj_h.reshape(B, S, N_H)
        proj_a = proj_a.reshape(B, S, N_A)
        phase = None
        if dist and l == first:
            phase = _gather_ici_phase([a_loc[1], b_loc[0]], [placed(a_loc[1], "in1"), placed(b_loc[0], "out0")])
        if dist and l == last:
            phase = _gather_d2d_phase([w_out1_part], [rb])
        o_h, u, states, got = _hgrn_fwd(proj_h, MIX_WIDTH, lb_param, g_head[l:l + 1], l, f"hgrn_fwd{l}", phase)
        phase = None
        if dist and l == first:
            phase = _merge_phases(_gather_d2d_phase(got, [ra, rb]),
                                  _gather_ici_phase([b_loc[1]], [placed(b_loc[1], "out1")]))
        if dist and l == last:
            w_out[1] = got[0]
        u, kept_a, got = _attn_fwd(proj_a, u, _sink_rows(sinks[l]), cos, sin, f"attn_fwd{l}", phase)
        if dist and l == first:
            w_in[1], w_out[0], w_out1_part = got
        u2 = u.reshape(T, MIX_WIDTH)
        if l < last:
            y, xn = _outproj_fwd(u2, w_out[l], x2, g_post[l:l + 1], None, f"outproj{l}")
            xn = xn.reshape(B, S, D)
        else:
            y, dxn, loss_part = _outproj_fwd(u2, w_out[l], x2, g_post[l:l + 1], target.reshape(T, D), f"outproj{l}")
            xn = None
        saved.append((x2, h, proj_h, proj_a, o_h, u2, states, kept_a, y))
        xs = xn

    dw_in, dw_out = [None] * DEPTH, [None] * DEPTH
    dg_pre, dg_post, dlb, dg_head, dsinks = [], [], [], [], []
    for l in reversed(range(DEPTH)):
        x2, h, proj_h, proj_a, o_h, u2, states, kept_a, y = saved[l]
        dgp, du, dw_out[l] = _outproj_bwd(dxn, y, g_post[l:l + 1], w_out[l], u2, f"outproj_bwd{l}")
        du = du.reshape(B, S, MIX_WIDTH)
        phase = None
        if dist:
            phase = _reduce_d2d_phase([dw_out[l]], [rb])
            if l == first:
                phase = _merge_phases(_reduce_ici_phase([part_in1]), phase)
        dqh, dfh, dih, dzh, dlb_l, dgh, got = _hgrn_bwd(
            proj_h, o_h, du, states, lb_param, g_head[l:l + 1], l, f"hgrn_bwd{l}", phase)
        if dist:
            if l == first:
                sum_in = _chip_sum(part_in1, got[0], "chip_sum_in1", 1)
            part_out = _pair_sum(dw_out[l], got[-1], side, f"pair_sum_out{l}")
        dqa, dkv, dza, dsk, got = _attn_bwd(proj_a, du, kept_a, cos, sin, f"attn_bwd{l}",
                                            _reduce_ici_phase([part_out]) if dist else None)
        if dist:
            sum_out = _chip_sum(part_out, got[0], f"chip_sum_out{l}", l, None if l == last else sum_out)
        dproj = [p.reshape(T, p.shape[-1]) for p in (dqh, dfh, dih, dzh, dqa, dkv, dza)]
        dw_in[l] = _mm_tn(dproj, h, f"wgrad_in{l}")
        phase = None
        if dist and l == last:
            phase = _reduce_d2d_phase([dw_in[l]], [ra])
        if dist and l == first:
            got = _run_phase(_reduce_d2d_phase([dw_in[l]], [ra]), "reduce_in0_d2d")
            part_in0 = _pair_sum(dw_in[l], got[0], side, "pair_sum_in0")
            phase = _reduce_ici_phase([part_in0])
        dxn, dgpre, got = _inproj_bwd(dproj, w_in[l], x2, dxn, g_pre[l:l + 1], f"inproj_bwd{l}", phase)
        if dist and l == last:
            part_in1 = _pair_sum(dw_in[l], got[0], side, "pair_sum_in1")
        if dist and l == first:
            sum_in = _chip_sum(part_in0, got[0], "chip_sum_in0", 0, sum_in)
        dg_pre.append(dgpre)
        dg_post.append(dgp)
        dlb.append(dlb_l)
        dg_head.append(dgh)
        dsinks.append(dsk)
    rev = lambda lst: jnp.concatenate(lst[::-1], axis=0)
    if not dist:
        sum_in, sum_out = jnp.stack(dw_in), jnp.stack(dw_out)
    return (loss_part, dxn.reshape(B, S, D), sum_in, sum_out,
            rev(dg_pre), rev(dg_post), rev(dlb), rev(dg_head), rev(dsinks))


def _me_and_peers():
    x, y, c = lax.axis_index("x"), lax.axis_index("y"), lax.axis_index("c")
    me = 4 * x + 2 * y + c
    peers = []
    for k in range(1, N_DEV):
        px = 1 - x if k & 4 else x
        py = 1 - y if k & 2 else y
        pc = 1 - c if k & 1 else c
        peers.append(((px, py, pc), 4 * px + 2 * py + pc))
    return me, peers


class _Phase:
    def __init__(self, arrays, out_shapes, aliases, n_send, build):
        self.arrays, self.out_shapes, self.aliases = list(arrays), list(out_shapes), dict(aliases)
        self.n_send, self.build = n_send, build

    def scratch(self):
        return [pltpu.SemaphoreType.DMA((self.n_send,)), pltpu.SemaphoreType.DMA((self.n_send,))]

    def _copies(self, in_refs, out_refs, sems, arrivals):
        send_sems, recv_sems = sems
        sends, recvs = self.build(in_refs, out_refs)
        assert len(sends) == self.n_send == len(recvs)
        out = [pltpu.make_async_remote_copy(src_ref=s, dst_ref=d, send_sem=send_sems.at[i], recv_sem=recv_sems.at[i],
                                            device_id=dev, device_id_type=MESH) for i, (s, d, dev) in enumerate(sends)]
        inc = [pltpu.make_async_remote_copy(src_ref=s, dst_ref=r, send_sem=send_sems.at[i], recv_sem=recv_sems.at[i],
                                            device_id=dev, device_id_type=MESH)
               for i, ((s, _, dev), r) in enumerate(zip(sends, recvs))] if arrivals else []
        return out, inc

    def start(self, in_refs, out_refs, sems):
        out, _ = self._copies(in_refs, out_refs, sems, False)
        for cp in out:
            cp.start()

    def finish(self, in_refs, out_refs, sems):
        out, inc = self._copies(in_refs, out_refs, sems, True)
        for cp in inc:
            cp.wait_recv()
        for cp in out:
            cp.wait_send()


_ANY = pl.BlockSpec(memory_space=pl.ANY)


def _host_phase(phase, n_in, n_out):
    if phase is None:
        return [], [], [], {}, [], []
    aliases = {n_in + i: n_out + o for i, o in phase.aliases.items()}
    return ([_ANY] * len(phase.arrays), [_ANY] * len(phase.out_shapes), phase.out_shapes, aliases, phase.scratch(),
            phase.arrays)


def _split_refs(refs, n_in, n_out, n_scr, phase):
    pi = len(phase.arrays) if phase else 0
    po = len(phase.out_shapes) if phase else 0
    a = n_in + pi
    b = a + n_out + po
    return (refs[:n_in], refs[a:a + n_out], refs[b:b + n_scr], refs[n_in:a], refs[a + n_out:b], refs[b + n_scr:])


def _hosted_start(phase, p_in, p_out, p_sems, first):
    if phase is not None:
        @pl.when(first)
        def _():
            phase.start(p_in, p_out, p_sems)


def _hosted_finish(phase, p_in, p_out, p_sems, last):
    if phase is not None:
        @pl.when(last)
        def _():
            phase.finish(p_in, p_out, p_sems)


def _run_phase(phase, name):
    n_in, n_out = len(phase.arrays), len(phase.out_shapes)

    def body(*refs):
        phase.start(refs[:n_in], refs[n_in:n_in + n_out], refs[n_in + n_out:])
        phase.finish(refs[:n_in], refs[n_in:n_in + n_out], refs[n_in + n_out:])

    return pl.pallas_call(
        body, name=name, in_specs=[_ANY] * n_in, out_specs=[_ANY] * n_out,
        out_shape=phase.out_shapes, input_output_aliases=phase.aliases, scratch_shapes=phase.scratch(),
        compiler_params=pltpu.CompilerParams(has_side_effects=True),
    )(*phase.arrays)


def _gather_one_call(loc, full, name):
    r = loc.shape[0]

    def body(loc_ref, full_in, full_ref, send_sems, recv_sems):
        del full_in
        c, chips, num = _mesh_place()
        me = num(chips[0], c)
        sib = (*chips[0], 1 - c)

        def copy(k, src, dev_rows, to):
            return pltpu.make_async_remote_copy(src_ref=src, dst_ref=_rows(full_ref, r, dev_rows),
                                                send_sem=send_sems.at[k], recv_sem=recv_sems.at[k],
                                                device_id=to, device_id_type=MESH)

        sent = [copy(0, loc_ref, me, sib)] + [copy(1 + j, loc_ref, me, (*ch, c)) for j, ch in enumerate(chips[1:])]
        for cp in sent:
            cp.start()
        for j, ch in enumerate(chips[1:]):
            copy(1 + j, loc_ref, num(ch, c), sib).wait_recv()
            fw = copy(4 + j, _rows(full_ref, r, num(ch, c)), num(ch, c), sib)
            fw.start()
            sent.append(fw)
        copy(0, loc_ref, num(chips[0], 1 - c), sib).wait_recv()
        for j, ch in enumerate(chips[1:]):
            copy(4 + j, loc_ref, num(ch, 1 - c), sib).wait_recv()
        for cp in sent:
            cp.wait_send()

    return pl.pallas_call(
        body, name=name, in_specs=[_ANY, _ANY], out_specs=_ANY,
        out_shape=jax.ShapeDtypeStruct(full.shape, full.dtype), input_output_aliases={1: 0},
        scratch_shapes=[pltpu.SemaphoreType.DMA((7,)), pltpu.SemaphoreType.DMA((7,))],
        compiler_params=pltpu.CompilerParams(has_side_effects=True),
    )(loc, full)


def _merge_phases(a, b):
    n_in, n_out = len(a.arrays), len(a.out_shapes)
    aliases = dict(a.aliases)
    aliases.update({n_in + i: n_out + o for i, o in b.aliases.items()})

    def build(ins, outs):
        sa, ra = a.build(ins[:n_in], outs[:n_out])
        sb, rb = b.build(ins[n_in:], outs[n_out:])
        return sa + sb, ra + rb

    return _Phase(a.arrays + b.arrays, a.out_shapes + b.out_shapes, aliases, a.n_send + b.n_send, build)


def _mesh_place():
    x, y, c = lax.axis_index("x"), lax.axis_index("y"), lax.axis_index("c")
    chips = [(x, y), (1 - x, y), (x, 1 - y), (1 - x, 1 - y)]
    num = lambda chip, core: 4 * chip[0] + 2 * chip[1] + core
    return c, chips, num


def _own_side_blocks():
    c, chips, num = _mesh_place()
    return jnp.stack([num(ch, c) for ch in chips]).astype(jnp.int32)


def _rows(ref, r, dev):
    return ref.at[pl.ds(pl.multiple_of(dev * r, 16), r), :]


def _place_own(loc, blocks, name):
    r, D = loc.shape
    tr = _pick(r, (400, 256, 200, 128, 64, 16))

    def body(idx_ref, l_ref, o_ref):
        del idx_ref
        o_ref[...] = l_ref[...]

    return pl.pallas_call(
        body, name=name,
        grid_spec=pltpu.PrefetchScalarGridSpec(
            num_scalar_prefetch=1, grid=(r // tr,),
            in_specs=[pl.BlockSpec((tr, D), lambda i, idx: (i, 0))],
            out_specs=pl.BlockSpec((tr, D), lambda i, idx: (idx[0] * (r // tr) + i, 0))),
        out_shape=jax.ShapeDtypeStruct((N_DEV * r, D), loc.dtype),
        compiler_params=_params(("arbitrary",)),
    )(blocks, loc)


def _gather_ici_phase(locs, fulls):
    rs = [a.shape[0] for a in locs]
    n = len(locs)

    def build(ins, outs):
        c, chips, num = _mesh_place()
        me = num(chips[0], c)
        targets = [((*chips[0], 1 - c), num(chips[0], 1 - c))] + [((*ch, c), num(ch, c)) for ch in chips[1:]]
        sends, recvs = [], []
        for dev, dnum in targets:
            for i, r in enumerate(rs):
                sends.append((ins[i], _rows(outs[i], r, me), dev))
                recvs.append(_rows(outs[i], r, dnum))
        return sends, recvs

    shapes = [jax.ShapeDtypeStruct(a.shape, a.dtype) for a in fulls]
    return _Phase(list(locs) + list(fulls), shapes, {n + i: i for i in range(n)}, 4 * n, build)


def _gather_d2d_phase(fulls, rs):
    def build(ins, outs):
        c, chips, num = _mesh_place()
        sib = (*chips[0], 1 - c)
        sends, recvs = [], []
        for ch in chips[1:]:
            for i, r in enumerate(rs):
                blk = _rows(outs[i], r, num(ch, c))
                sends.append((blk, blk, sib))
                recvs.append(_rows(outs[i], r, num(ch, 1 - c)))
        return sends, recvs

    shapes = [jax.ShapeDtypeStruct(a.shape, a.dtype) for a in fulls]
    return _Phase(fulls, shapes, {i: i for i in range(len(fulls))}, 3 * len(fulls), build)


def _reduce_d2d_phase(grads, rs):
    def build(ins, outs):
        c, chips, num = _mesh_place()
        sib = (*chips[0], 1 - c)
        sends, recvs = [], []
        for j, ch in enumerate(chips):
            for i, r in enumerate(rs):
                sends.append((_rows(ins[i], r, num(ch, 1 - c)), outs[i].at[j], sib))
                recvs.append(outs[i].at[j])
        return sends, recvs

    shapes = [jax.ShapeDtypeStruct((4, r, g.shape[1]), g.dtype) for g, r in zip(grads, rs)]
    return _Phase(grads, shapes, {}, 4 * len(grads), build)


def _reduce_ici_phase(parts):
    def build(ins, outs):
        c, chips, _ = _mesh_place()
        sends, recvs = [], []
        for t in range(1, 4):
            for i in range(len(parts)):
                sends.append((ins[i].at[t], outs[i].at[t - 1], (*chips[t], c)))
                recvs.append(outs[i].at[t - 1])
        return sends, recvs

    shapes = [jax.ShapeDtypeStruct((3,) + p.shape[1:], p.dtype) for p in parts]
    return _Phase(parts, shapes, {}, 3 * len(parts), build)


def _pair_sum(g, got, blocks, name):
    n, r, D = got.shape
    tr = _pick(r, (800, 400, 256, 200, 128, 64, 16))

    def body(idx_ref, g_ref, r_ref, o_ref):
        del idx_ref
        o_ref[...] = (g_ref[...].astype(F32) + r_ref[...].astype(F32)).astype(o_ref.dtype)

    blk = pl.BlockSpec((None, tr, D), lambda j, i, idx: (j, i, 0))
    return pl.pallas_call(
        body, name=name,
        grid_spec=pltpu.PrefetchScalarGridSpec(
            num_scalar_prefetch=1, grid=(n, r // tr),
            in_specs=[pl.BlockSpec((tr, D), lambda j, i, idx: (idx[j] * (r // tr) + i, 0)), blk],
            out_specs=blk),
        out_shape=jax.ShapeDtypeStruct(got.shape, got.dtype),
        compiler_params=_params(("arbitrary", "arbitrary")),
    )(blocks, g, got)


def _chip_sum(p, r, name, layer, into=None):
    _, R, D = p.shape
    tr = _pick(R, (800, 400, 256, 200, 128, 64, 16))

    def body(p_ref, r_ref, *rest):
        acc = p_ref[...].astype(F32)
        for t in range(3):
            acc = acc + r_ref[t].astype(F32)
        rest[-1][...] = acc

    args = [p, r] + ([] if into is None else [into])
    return pl.pallas_call(
        body, name=name, grid=(R // tr,),
        in_specs=[pl.BlockSpec((None, tr, D), lambda i: (0, i, 0)), pl.BlockSpec((3, tr, D), lambda i: (0, i, 0))]
        + ([] if into is None else [_ANY]),
        out_specs=pl.BlockSpec((None, tr, D), lambda i: (layer, i, 0)),
        out_shape=jax.ShapeDtypeStruct((DEPTH, R, D), F32),
        input_output_aliases={} if into is None else {2: 0},
        compiler_params=_params(("parallel",)))(*args)


def _allreduce_small(vec):
    R, C = vec.shape

    def body(v_ref, o_ref, buf, send_sems, recv_sems):
        me, peers = _me_and_peers()
        buf[me] = v_ref[...]
        sends = []
        for k, (pid, _) in enumerate(peers):
            cp = pltpu.make_async_remote_copy(src_ref=v_ref, dst_ref=buf.at[me], send_sem=send_sems.at[k],
                                              recv_sem=recv_sems.at[k], device_id=pid, device_id_type=MESH)
            cp.start()
            sends.append(cp)
        for k, (pid, pnum) in enumerate(peers):
            pltpu.make_async_remote_copy(src_ref=v_ref, dst_ref=buf.at[pnum], send_sem=send_sems.at[k],
                                         recv_sem=recv_sems.at[k], device_id=pid, device_id_type=MESH).wait_recv()
        for cp in sends:
            cp.wait_send()
        acc = buf[0]
        for d in range(1, N_DEV):
            acc = acc + buf[d]
        o_ref[...] = acc

    vm = pl.BlockSpec(memory_space=pltpu.VMEM)
    return pl.pallas_call(
        body, name="allreduce_small",
        in_specs=[vm], out_specs=vm,
        out_shape=jax.ShapeDtypeStruct((R, C), F32),
        scratch_shapes=[pltpu.VMEM((N_DEV, R, C), F32), pltpu.SemaphoreType.DMA((N_DEV - 1,)),
                        pltpu.SemaphoreType.DMA((N_DEV - 1,))],
        compiler_params=pltpu.CompilerParams(has_side_effects=True),
    )(vec)


def _adamw(w, g, m, v, name):
    R, C = w.shape
    tr = _pick(R, (512, 400, 256, 128, 64, 32, 16, 8)) if R >= 8 else R
    c1 = 1.0 - ADAM_B1 ** ADAM_STEP
    c2 = 1.0 - ADAM_B2 ** ADAM_STEP

    def body(w_ref, g_ref, m_ref, v_ref, d_ref, mo_ref, vo_ref):
        gg = g_ref[...]
        mn = ADAM_B1 * m_ref[...] + (1.0 - ADAM_B1) * gg
        vn = ADAM_B2 * v_ref[...] + (1.0 - ADAM_B2) * (gg * gg)
        d_ref[...] = -ADAM_LR * ((mn / c1) / (jnp.sqrt(vn / c2) + ADAM_EPS) + ADAM_WD * w_ref[...])
        mo_ref[...] = mn
        vo_ref[...] = vn

    blk = pl.BlockSpec((tr, C), lambda i: (i, 0))
    sh = jax.ShapeDtypeStruct((R, C), F32)
    return pl.pallas_call(
        body, name=name, grid=(R // tr,), in_specs=[blk] * 4, out_specs=[blk] * 3, out_shape=[sh] * 3,
        compiler_params=_params(("parallel",)),
    )(w, g, m, v)


def _lb_param_grad(lb_param, dlb):
    L, C = lb_param.shape

    def body(p_ref, d_ref, o_ref):
        lbp = p_ref[...]
        d = d_ref[...]
        mx = jnp.max(lbp, axis=0, keepdims=True)
        e = jnp.exp(lbp - mx)
        p = e / jnp.sum(e, axis=0, keepdims=True)
        tot = jnp.sum(d, axis=0, keepdims=True)
        dps = []
        rest = tot
        for j in range(L):
            dps.append(rest - tot if j == 0 else rest)
            rest = rest - d[j:j + 1]
        dp = jnp.concatenate(dps, axis=0)
        o_ref[...] = p * (dp - jnp.sum(p * dp, axis=0, keepdims=True))

    vm = pl.BlockSpec(memory_space=pltpu.VMEM)
    return pl.pallas_call(body, name="lb_param_grad", in_specs=[vm, vm], out_specs=vm,
                          out_shape=jax.ShapeDtypeStruct((L, C), F32))(lb_param, dlb)


def _pack_small(loss_part, dg_pre, dg_post, dlb, dg_head, dsinks):
    pad8 = lambda a: jnp.pad(a.reshape(-1, 128), ((0, 8 - DEPTH), (0, 0)))
    rows = [dg_pre.reshape(-1, 128), dg_post.reshape(-1, 128), dlb.reshape(-1, 128), pad8(dg_head), pad8(dsinks),
            loss_part]
    return jnp.concatenate(rows, axis=0)


def _unpack_small(vec):
    n = DEPTH * D_MODEL // 128
    o = 0
    dg_pre = vec[o:o + n].reshape(DEPTH, D_MODEL); o += n
    dg_post = vec[o:o + n].reshape(DEPTH, D_MODEL); o += n
    dlb = vec[o:o + n].reshape(DEPTH, HG_WIDTH); o += n
    dg_head = vec[o:o + DEPTH]; o += 8
    dsinks = vec[o:o + DEPTH, :ATT_HEADS]; o += 8
    loss = jnp.sum(vec[o:o + 8])
    return loss, dg_pre, dg_post, dlb, dg_head, dsinks


def kernel(x, w_in, w_out, g_pre, g_post, lb_param, g_head, sinks, loss_target, m_w_in, m_w_out, m_g_pre, m_g_post, m_lb_param, m_g_head, m_sinks, v_w_in, v_w_out, v_g_pre, v_g_post, v_lb_param, v_g_head, v_sinks):
    tr = lambda a: jnp.swapaxes(a, 1, 2)
    w_in_t = tr(w_in)
    (loss_part, dx, gw_in_t, gw_out, dg_pre, dg_post, dlb, dg_head, dsinks) = _step(
        x, loss_target, g_pre, g_post, lb_param, g_head, sinks, shards=(w_in_t.astype(BF16), w_out.astype(BF16)))

    small = _allreduce_small(_pack_small(loss_part, dg_pre, dg_post, dlb, dg_head, dsinks))
    loss, gg_pre, gg_post, gdlb, gg_head, gsinks = _unpack_small(small)
    glb = _lb_param_grad(lb_param, gdlb)

    grads = [gw_in_t, gw_out, gg_pre, gg_post, glb, gg_head, gsinks]
    ws = [w_in_t, w_out, g_pre, g_post, lb_param, g_head, sinks]
    ms = [tr(m_w_in), m_w_out, m_g_pre, m_g_post, m_lb_param, m_g_head, m_sinks]
    vs = [tr(v_w_in), v_w_out, v_g_pre, v_g_post, v_lb_param, v_g_head, v_sinks]
    names = ["w_in", "w_out", "g_pre", "g_post", "lb_param", "g_head", "sinks"]
    deltas, new_m, new_v = [], [], []
    for w, g, m, v, nm in zip(ws, grads, ms, vs, names):
        sh = w.shape
        two = lambda a: a.reshape(-1, sh[-1])
        d, mn, vn = _adamw(two(w), two(g), two(m), two(v), "adamw_" + nm)
        deltas.append(d.reshape(sh))
        new_m.append(mn.reshape(sh))
        new_v.append(vn.reshape(sh))
    grads[0], deltas[0], new_m[0], new_v[0] = tr(grads[0]), tr(deltas[0]), tr(new_m[0]), tr(new_v[0])
    return (loss, dx, *grads, *deltas, *new_m, *new_v)
```

```python
import math

import numpy as np
import jax
import jax.numpy as jnp
from jax import lax
from jax.experimental import pallas as pl
from jax.experimental.pallas import tpu as pltpu

F32 = jnp.float32
BF16 = jnp.bfloat16

D_MODEL = 1024
DEPTH = 2
HG_HEADS = 8
HG_DIM = 128
HG_WIDTH = HG_HEADS * HG_DIM
CHUNK = 64
ATT_HEADS = 16
ATT_DIM = 64
ATT_WIDTH = ATT_HEADS * ATT_DIM
KV_WIDTH = 128
ATT_BLOCK = 128
ATT_SCALE = 1.0 / math.sqrt(ATT_DIM)
ROPE_THETA = 10000.0
NORM_EPS = 1e-6
NEG_INF = -1e30
LB_FLOOR = 1e-20
N_H = 4 * HG_WIDTH
N_A = 2 * ATT_WIDTH + 2 * KV_WIDTH
IN_WIDTH = N_H + N_A
MIX_WIDTH = HG_WIDTH + ATT_WIDTH

ADAM_LR = 0.001
ADAM_B1 = 0.9
ADAM_B2 = 0.999
ADAM_EPS = 1e-08
ADAM_WD = 0.01
ADAM_STEP = 10

N_DEV = 8
MESH = pl.DeviceIdType.MESH
VMEM_LIMIT = 56 * 1024 * 1024

NN = ((1,), (0,))
NT = ((1,), (1,))
TN = ((0,), (0,))


def _dot(a, b, dims):
    return lax.dot_general(a.astype(BF16), b.astype(BF16), (dims, ((), ())), preferred_element_type=F32)


def _params(sem=None, **kw):
    return pltpu.CompilerParams(dimension_semantics=sem, vmem_limit_bytes=VMEM_LIMIT, **kw)


def _sigmoids(x):
    e = jnp.exp(-jnp.abs(x))
    r = 1.0 / (1.0 + e)
    er = e * r
    pos = x >= 0.0
    return jnp.where(pos, r, er), jnp.where(pos, er, r)


def _silu(x):
    return x * _sigmoids(x)[0]


def _silu_and_grad(x):
    s, ns = _sigmoids(x)
    return x * s, s * (1.0 + x * ns)


def _pick(n, prefs):
    for p in prefs:
        if n % p == 0:
            return p
    return n


def _inproj(x2, g, w, name):
    T, D = x2.shape
    tm = _pick(T, (512, 256, 128))
    nchunk = 1024

    def body(x_ref, g_ref, w_ref, oh_ref, oa_ref, h_ref):
        x = x_ref[...]
        r = lax.rsqrt(jnp.mean(x * x, axis=-1, keepdims=True) + NORM_EPS)
        h = ((x * r) * g_ref[...]).astype(BF16)
        h_ref[...] = h
        for j in range(0, N_H, nchunk):
            oh_ref[:, j:j + nchunk] = lax.dot_general(h, w_ref[j:j + nchunk, :], (NT, ((), ())),
                                                      preferred_element_type=F32)
        for j in range(0, N_A, N_A // 2):
            oa_ref[:, j:j + N_A // 2] = lax.dot_general(h, w_ref[N_H + j:N_H + j + N_A // 2, :], (NT, ((), ())),
                                                        preferred_element_type=F32)

    row = lambda w_: pl.BlockSpec((tm, w_), lambda i: (i, 0))
    return pl.pallas_call(
        body, name=name,
        grid=(T // tm,),
        in_specs=[row(D), pl.BlockSpec((1, D), lambda i: (0, 0)),
                  pl.BlockSpec((IN_WIDTH, D), lambda i: (0, 0), pipeline_mode=pl.Buffered(1))],
        out_specs=[row(N_H), row(N_A), row(D)],
        out_shape=[jax.ShapeDtypeStruct((T, N_H), F32), jax.ShapeDtypeStruct((T, N_A), F32),
                   jax.ShapeDtypeStruct((T, D), BF16)],
        compiler_params=_params(("parallel",)),
    )(x2, g, w)


def _mm_tn(pieces, b, name, out_dtype=BF16):
    T, m = b.shape
    tn = 256
    counts = [p.shape[1] // tn for p in pieces]
    starts = [sum(counts[:i]) for i in range(len(pieces))]
    n_p = len(pieces)

    def body(*refs):
        b_ref, o_ref = refs[n_p], refs[n_p + 1]
        i = pl.program_id(0)
        for p in range(n_p):
            @pl.when((i >= starts[p]) & (i < starts[p] + counts[p]))
            def _(p=p):
                o_ref[...] = lax.dot_general(refs[p][...], b_ref[...], (TN, ((), ())),
                                             preferred_element_type=F32).astype(out_dtype)

    piece_spec = lambda s, c: pl.BlockSpec((T, tn), lambda i: (0, jnp.clip(i - s, 0, c - 1)))
    return pl.pallas_call(
        body, name=name,
        grid=(sum(counts),),
        in_specs=[piece_spec(s, c) for s, c in zip(starts, counts)]
        + [pl.BlockSpec((T, m), lambda i: (0, 0), pipeline_mode=pl.Buffered(1))],
        out_specs=pl.BlockSpec((tn, m), lambda i: (i, 0)),
        out_shape=jax.ShapeDtypeStruct((sum(counts) * tn, m), out_dtype),
        compiler_params=_params(("arbitrary",)),
    )(*pieces, b)


_LEVELS = (0, 1, 2, 4, 8, 16, 32)
_CUM_L = (2, 4, 8, 16, 32, 64)
_ALL_KINDS = tuple(("c", L) for L in _CUM_L) + tuple(("r", L) for L in _CUM_L)
_MXU_KINDS = (("c", 2), ("c", 4), ("c", CHUNK), ("r", 2), ("r", 4))
N_CUM = len(_ALL_KINDS) * CHUNK
N_CUM_F = len(_MXU_KINDS) * CHUNK


def _cum_matrices():
    t = np.arange(CHUNK)[:, None]
    r = np.arange(CHUNK)[None, :]

    def mat(kind):
        c, L = kind
        return ((r // L == t // L) & ((r <= t) if c == "c" else (r > t))).astype(np.float32)

    fwd = np.concatenate([mat(kd) for kd in _MXU_KINDS], axis=0)
    full = np.concatenate([mat(kd) for kd in _ALL_KINDS], axis=0)
    return jnp.asarray(fwd, BF16), jnp.asarray(full.T.copy(), BF16)


def _level_masks():
    t = np.arange(CHUNK)[:, None]
    s = np.arange(CHUNK)[None, :]
    ms = []
    for L in _LEVELS:
        if L == 0:
            ms.append(t == s)
        else:
            ms.append((t // (2 * L) == s // (2 * L)) & ((t // L) % 2 == 1) & ((s // L) % 2 == 0))
    return jnp.asarray(np.stack(ms).astype(np.float32))


def _split3(x):
    hi = x.astype(BF16)
    r1 = x - hi.astype(F32)
    mid = r1.astype(BF16)
    lo = (r1 - mid.astype(F32)).astype(BF16)
    return hi, mid, lo


def _cum3(ts, x, terms=3):
    d = lambda p: lax.dot_general(ts, p, (NN, ((), ())), preferred_element_type=F32)
    return sum(d(p) for p in _split3(x)[:terms])


def _lb_terms(lbp, layer):
    mx = jnp.max(lbp, axis=0, keepdims=True)
    e = jnp.exp(lbp - mx)
    p = e / jnp.sum(e, axis=0, keepdims=True)
    cum = p[0:1]
    for j in range(1, layer + 1):
        cum = cum + p[j:j + 1]
    lb = cum - p[0:1]
    lbf = jnp.maximum(lb, LB_FLOOR)
    return dict(lbf=lbf, one_m=1.0 - lb, kcorr=lb - lbf, ind=jnp.where(lb > LB_FLOOR, 1.0, 0.0))


def _gate(x, lt):
    sig, nsig = _sigmoids(x)
    f = lt["lbf"] + lt["one_m"] * sig
    return jnp.log(f), lt["one_m"] * nsig + lt["kcorr"], f, sig, nsig


def _ck(x, ci):
    return x[ci * CHUNK:(ci + 1) * CHUNK]


def _block_cums(ts, g, nc):
    cs = [_cum3(ts, _ck(g, ci), terms=2) for ci in range(nc)]
    out = {kind: jnp.concatenate([c[CHUNK * i:CHUNK * (i + 1)] for c in cs], axis=0)
           for i, kind in enumerate(_MXU_KINDS)}
    b = out[("c", CHUNK)]
    ng = CHUNK // 8
    last = b.reshape(nc, ng, 8, HG_DIM)[:, :, 7:8, :]
    zero = jnp.zeros((nc, 1, 1, HG_DIM), F32)

    def spread(groups):
        return jnp.broadcast_to(jnp.concatenate(groups, axis=1), (nc, ng, 8, HG_DIM)).reshape(nc * CHUNK, HG_DIM)

    def get(kind):
        if kind in out:
            return out[kind]
        c, L = kind
        nb = L // 8
        first = lambda r: (r // nb) * nb
        if c == "c":
            return b - spread([last[:, first(r) - 1:first(r)] if r >= nb else zero for r in range(ng)])
        return spread([last[:, first(r) + nb - 1:first(r) + nb] for r in range(ng)]) - b

    return get


def _level_factors(cums, g, L):
    if L == 0:
        return None, None
    if L == 1:
        return jnp.exp(g), None
    return jnp.exp(cums(("c", L))), jnp.exp(cums(("r", L)))


def _mul(a, e):
    return a if e is None else a * e


def _hg_block_fwd(qf, k, v, g, ts, m_ref, nc):
    cums = _block_cums(ts, g, nc)
    amat = [jnp.zeros((CHUNK, CHUNK), F32)] * nc
    for li, L in enumerate(_LEVELS):
        eq, ek = _level_factors(cums, g, L)
        ql, kl, m = _mul(qf, eq), _mul(k, ek), m_ref[li]
        amat = [amat[ci] + _dot(_ck(ql, ci), _ck(kl, ci), NT) * m for ci in range(nc)]
    b = cums(("c", CHUNK))
    kst = k * jnp.exp(cums(("r", CHUNK)))
    o = [_dot(amat[ci], _ck(v, ci), NN) for ci in range(nc)]
    kv = [_dot(_ck(v, ci), _ck(kst, ci), TN) for ci in range(nc)]
    dec = [jnp.exp(b[(ci + 1) * CHUNK - 1:(ci + 1) * CHUNK, :]) for ci in range(nc)]
    return o, dec, kv, qf * jnp.exp(b), amat


def _hg_block_bwd(qf, k, v, g, do, amat, ts, m_ref, nc):
    cums = _block_cums(ts, g, nc)
    dcs = {}
    da = [_dot(_ck(do, ci), _ck(v, ci), NT) for ci in range(nc)]
    dq = jnp.zeros_like(qf)
    dk = jnp.zeros_like(qf)
    dg = jnp.zeros_like(qf)
    for li, L in enumerate(_LEVELS):
        eq, ek = _level_factors(cums, g, L)
        ql, kl, m = _mul(qf, eq), _mul(k, ek), m_ref[li]
        qlb, klb = ql.astype(BF16), kl.astype(BF16)
        dal = [(da[ci] * m).astype(BF16) for ci in range(nc)]
        dql = jnp.concatenate([_dot(dal[ci], _ck(klb, ci), NN) for ci in range(nc)], axis=0)
        dkl = jnp.concatenate([_dot(dal[ci], _ck(qlb, ci), TN) for ci in range(nc)], axis=0)
        dq = dq + _mul(dql, eq)
        dk = dk + _mul(dkl, ek)
        if L == 1:
            dg = dg + dql * ql
        elif L > 1:
            dcs[("c", L)] = (dql * ql).astype(BF16)
            dcs[("r", L)] = (dkl * kl).astype(BF16)
    b = cums(("c", CHUNK))
    e64 = jnp.exp(b)
    er64 = jnp.exp(cums(("r", CHUNK)))
    qb = qf * e64
    return dict(dq=dq, dk=dk, dg=dg, dcs=dcs, e64=e64, er64=er64, qb=qb, kst=k * er64,
                dv=[_dot(amat[ci], _ck(do, ci), TN) for ci in range(nc)],
                dec=[jnp.exp(b[(ci + 1) * CHUNK - 1:(ci + 1) * CHUNK, :]) for ci in range(nc)],
                qd=[_dot(_ck(do, ci), _ck(qb, ci), TN) for ci in range(nc)])


def _hg_state_bwd(w, v, do, starts, ends, tst, nc):
    dqb = jnp.concatenate([_dot(_ck(do, ci), starts[ci], NN) for ci in range(nc)], axis=0)
    dkst = jnp.concatenate([_dot(_ck(v, ci), ends[ci], NN) for ci in range(nc)], axis=0)
    dq = w["dq"] + dqb * w["e64"]
    dk = w["dk"] + dkst * w["er64"]
    dv = jnp.concatenate([w["dv"][ci] + _dot(_ck(w["kst"], ci), ends[ci], NT) for ci in range(nc)], axis=0)
    trow = lax.broadcasted_iota(jnp.int32, (CHUNK, 1), 0)
    dtot = jnp.concatenate(
        [jnp.where(trow == CHUNK - 1, jnp.sum(ends[ci] * starts[ci], axis=0, keepdims=True) * w["dec"][ci], 0.0)
         for ci in range(nc)], axis=0)
    dcs = dict(w["dcs"])
    dcs[("c", CHUNK)] = (dqb * w["qb"] + dtot).astype(BF16)
    dcs[("r", CHUNK)] = (dkst * w["kst"]).astype(BF16)
    dgs = [_dot(tst, jnp.concatenate([_ck(dcs[kind], ci) for kind in _ALL_KINDS], axis=0), NN) for ci in range(nc)]
    return dq, dk, dv, w["dg"] + jnp.concatenate(dgs, axis=0)


def _hgrn_fwd(proj_h, u_rows, lb_param, g_head, layer, name, phase=None):
    B, S, _ = proj_h.shape
    sb = _pick(S, (2048, 1024, 512, 256, 128, 64))
    nc = sb // CHUNK
    ts, _ = _cum_matrices()

    def body(*refs):
        ins, outs, (st,), p_in, p_out, p_sems = _split_refs(refs, 8, 12, 1, phase)
        q_ref, f_ref, i_ref, z_ref, lbp_ref, gh_ref, ts_ref, m_ref = ins
        o_ref, u_ref, sts_ref, am_ref = outs[:4]
        logf_ref, k_ref, qf_ref, sg_ref, qg_ref, zg_ref, fg_ref, sig_ref = outs[4:]
        h_id, b_id, s_id = pl.program_id(0), pl.program_id(1), pl.program_id(2)
        _hosted_start(phase, p_in, p_out, p_sems, (h_id == 0) & (b_id == 0) & (s_id == 0))

        @pl.when(s_id == 0)
        def _():
            st[...] = jnp.zeros_like(st)

        lt = _lb_terms(lbp_ref[...], layer)
        tsv = ts_ref[...]
        gh = gh_ref[...]
        logf, k, _, sig, nsig = _gate(f_ref[...], lt)
        qf, qf_grad = _silu_and_grad(q_ref[...])
        sg, sg_grad = _silu_and_grad(z_ref[...])
        logf_ref[...], k_ref[...], qf_ref[...], sg_ref[...] = logf, k, qf, sg
        qg_ref[...] = qf_grad.astype(BF16)
        zg_ref[...] = sg_grad.astype(BF16)
        fg_ref[...] = (lt["one_m"] * sig * nsig).astype(BF16)
        sig_ref[...] = sig.astype(BF16)
        o_part, dec, kv, qb, amat = _hg_block_fwd(qf, k, i_ref[...], logf, tsv, m_ref, nc)
        for ci in range(nc):
            am_ref[ci] = amat[ci].astype(BF16)
        cur = st[...]
        starts = []
        for ci in range(nc):
            sts_ref[ci] = cur
            starts.append(cur)
            cur = cur * dec[ci] + kv[ci]
        st[...] = cur
        o = jnp.concatenate([o_part[ci] + _dot(_ck(qb, ci), starts[ci], NT) for ci in range(nc)], axis=0)
        o_ref[...] = o
        r = lax.rsqrt(jnp.mean(o * o, axis=-1, keepdims=True) + NORM_EPS)
        u_ref[...] = (((o * r) * gh) * sg).astype(BF16)
        _hosted_finish(phase, p_in, p_out, p_sems, (h_id == HG_HEADS - 1) & (b_id == B - 1) & (s_id == S // sb - 1))

    col = lambda base: pl.BlockSpec((None, sb, HG_DIM), lambda h, b, s: (b, s, base + h))
    p_ispecs, p_ospecs, p_oshapes, p_alias, p_scratch, p_args = _host_phase(phase, 8, 12)
    wide = lambda dt: jax.ShapeDtypeStruct((B, S, HG_WIDTH), dt)
    res = pl.pallas_call(
        body, name=name,
        grid=(HG_HEADS, B, S // sb),
        in_specs=[col(0), col(HG_HEADS), col(2 * HG_HEADS), col(3 * HG_HEADS),
                  pl.BlockSpec((DEPTH, HG_DIM), lambda h, b, s: (0, h)),
                  pl.BlockSpec((1, HG_DIM), lambda h, b, s: (0, 0)),
                  pl.BlockSpec((N_CUM_F, CHUNK), lambda h, b, s: (0, 0)),
                  pl.BlockSpec((len(_LEVELS), CHUNK, CHUNK), lambda h, b, s: (0, 0, 0))] + p_ispecs,
        out_specs=[col(0), col(0),
                   pl.BlockSpec((None, None, nc, HG_DIM, HG_DIM), lambda h, b, s: (b, h, s, 0, 0)),
                   pl.BlockSpec((None, None, nc, CHUNK, CHUNK), lambda h, b, s: (b, h, s, 0, 0))]
        + [col(0)] * 8 + p_ospecs,
        out_shape=[wide(F32),
                   jax.ShapeDtypeStruct((B, S, u_rows), BF16),
                   jax.ShapeDtypeStruct((B, HG_HEADS, S // CHUNK, HG_DIM, HG_DIM), F32),
                   jax.ShapeDtypeStruct((B, HG_HEADS, S // CHUNK, CHUNK, CHUNK), BF16)]
        + [wide(F32)] * 4 + [wide(BF16)] * 4 + p_oshapes,
        input_output_aliases=p_alias,
        scratch_shapes=[pltpu.VMEM((HG_DIM, HG_DIM), F32)] + p_scratch,
        compiler_params=_params(("arbitrary", "arbitrary", "arbitrary")),
    )(proj_h, proj_h, proj_h, proj_h, lb_param, g_head, ts, _level_masks(), *p_args)
    return res[0], res[1], tuple(res[2:12]), list(res[12:])


def _hgrn_bwd(proj_h, o_h, du, kept, lb_param, g_head, layer, name, phase=None):
    B, S, _ = proj_h.shape
    sb = _pick(S, (512, 256, 128, 64))
    nc = sb // CHUNK
    ns = S // sb
    ts, tst = _cum_matrices()

    def body(*refs):
        ins, outs, (dst,), p_in, p_out, p_sems = _split_refs(refs, 18, 6, 1, phase)
        (i_ref, o_ref, du_ref, sts_ref, am_ref, logf_ref, k_ref, qf_ref, sg_ref, qg_ref, zg_ref, fg_ref, sig_ref,
         lbp_ref, gh_ref, ts_ref, tst_ref, m_ref) = ins
        dq_ref, df_ref, di_ref, dz_ref, dlb_ref, dgh_ref = outs
        h_id, b_id, s_id = pl.program_id(0), pl.program_id(1), pl.program_id(2)
        _hosted_start(phase, p_in, p_out, p_sems, (h_id == 0) & (b_id == 0) & (s_id == 0))

        @pl.when(s_id == 0)
        def _():
            dst[...] = jnp.zeros_like(dst)

        @pl.when((b_id == 0) & (s_id == 0))
        def _():
            dlb_ref[...] = jnp.zeros_like(dlb_ref)

        @pl.when((h_id == 0) & (b_id == 0) & (s_id == 0))
        def _():
            dgh_ref[...] = jnp.zeros_like(dgh_ref)

        lt = _lb_terms(lbp_ref[...], layer)
        gh = gh_ref[...]
        tsv = ts_ref[...]
        tstv = tst_ref[...]
        logf, k, qf, sg = logf_ref[...], k_ref[...], qf_ref[...], sg_ref[...]
        o = o_ref[...]
        dub = du_ref[...]
        r = lax.rsqrt(jnp.mean(o * o, axis=-1, keepdims=True) + NORM_EPS)
        n = o * r
        dz_ref[...] = (dub * (n * gh) * zg_ref[...].astype(F32)).astype(BF16)
        dgh_ref[...] += jnp.sum(dub * sg * n, axis=0, keepdims=True)
        dn = dub * sg * gh
        do = r * (dn - n * jnp.mean(dn * n, axis=-1, keepdims=True))
        v = i_ref[...]
        w = _hg_block_bwd(qf, k, v, logf, do, [am_ref[ci] for ci in range(nc)], tsv, m_ref, nc)
        cur = dst[...]
        ends = [None] * nc
        for ci in reversed(range(nc)):
            ends[ci] = cur
            cur = cur * w["dec"][ci] + w["qd"][ci]
        dst[...] = cur
        dq, dk, dv, dg = _hg_state_bwd(w, v, do, [sts_ref[ci] for ci in range(nc)], ends, tstv, nc)
        di_ref[...] = dv.astype(BF16)
        dq_ref[...] = (dq * qg_ref[...].astype(F32)).astype(BF16)
        f = jnp.exp(logf)
        scaled = (dg - f * dk) / f
        df_ref[...] = (scaled * fg_ref[...].astype(F32)).astype(BF16)
        dlb_ref[...] += jnp.sum(scaled * (lt["ind"] - sig_ref[...].astype(F32)), axis=0, keepdims=True)
        _hosted_finish(phase, p_in, p_out, p_sems, (h_id == HG_HEADS - 1) & (b_id == B - 1) & (s_id == ns - 1))

    col = lambda base: pl.BlockSpec((None, sb, HG_DIM), lambda h, b, s: (b, ns - 1 - s, base + h))
    out_col = pl.BlockSpec((None, sb, HG_DIM), lambda h, b, s: (b, ns - 1 - s, h))
    dt = jax.ShapeDtypeStruct((B, S, HG_WIDTH), BF16)
    p_ispecs, p_ospecs, p_oshapes, p_alias, p_scratch, p_args = _host_phase(phase, 18, 6)
    res = pl.pallas_call(
        body, name=name,
        grid=(HG_HEADS, B, ns),
        in_specs=[col(2 * HG_HEADS), col(0), col(0),
                  pl.BlockSpec((None, None, nc, HG_DIM, HG_DIM), lambda h, b, s: (b, h, ns - 1 - s, 0, 0)),
                  pl.BlockSpec((None, None, nc, CHUNK, CHUNK), lambda h, b, s: (b, h, ns - 1 - s, 0, 0))]
        + [col(0)] * 8
        + [pl.BlockSpec((DEPTH, HG_DIM), lambda h, b, s: (0, h)),
           pl.BlockSpec((1, HG_DIM), lambda h, b, s: (0, 0)),
           pl.BlockSpec((N_CUM_F, CHUNK), lambda h, b, s: (0, 0)),
           pl.BlockSpec((CHUNK, N_CUM), lambda h, b, s: (0, 0)),
           pl.BlockSpec((len(_LEVELS), CHUNK, CHUNK), lambda h, b, s: (0, 0, 0))] + p_ispecs,
        out_specs=[out_col, out_col, out_col, out_col,
                   pl.BlockSpec((1, HG_DIM), lambda h, b, s: (0, h)),
                   pl.BlockSpec((1, HG_DIM), lambda h, b, s: (0, 0))] + p_ospecs,
        out_shape=[dt, dt, dt, dt, jax.ShapeDtypeStruct((1, HG_WIDTH), F32),
                   jax.ShapeDtypeStruct((1, HG_DIM), F32)] + p_oshapes,
        input_output_aliases=p_alias,
        scratch_shapes=[pltpu.VMEM((HG_DIM, HG_DIM), F32)] + p_scratch,
        compiler_params=_params(("arbitrary", "arbitrary", "arbitrary")),
    )(proj_h, o_h, du, *kept, lb_param, g_head, ts, tst, _level_masks(), *p_args)
    return tuple(res[:6]) + (list(res[6:]),)


def _rope_tables(S):
    half = ATT_DIM // 2
    inv_freq = ROPE_THETA ** (-jnp.arange(half, dtype=F32) / half)
    ang = jnp.arange(S).astype(F32)[:, None] * inv_freq[None, :]
    cos = jnp.cos(ang)
    sin = jnp.sin(ang)
    cos = jnp.concatenate([cos, cos, cos, cos], axis=1)
    sin = jnp.concatenate([-sin, sin, -sin, sin], axis=1)
    return cos, sin


def _attn_common():
    lane = lax.broadcasted_iota(jnp.int32, (1, 2 * ATT_DIM), 1)
    first_half = (lane % ATT_DIM) < (ATT_DIM // 2)
    left = lane < ATT_DIM

    def swap(x):
        return jnp.where(first_half, pltpu.roll(x, 128 - ATT_DIM // 2, 1), pltpu.roll(x, ATT_DIM // 2, 1))

    def rope(x, cos, sin):
        return x * cos + swap(x) * sin

    def rope_bwd(dy, cos, sin):
        return dy * cos + swap(dy * sin)

    def dup(x):
        xs = pltpu.roll(x, ATT_DIM, 1)
        return [jnp.where(left, x, xs), jnp.where(left, xs, x)]

    return left, rope, rope_bwd, dup


GROUP = ATT_HEADS // 2
GROUP_ROWS = GROUP * ATT_BLOCK


def _attn_bias(i):
    r = lax.broadcasted_iota(jnp.int32, (ATT_BLOCK, 2 * ATT_BLOCK), 0)
    c = lax.broadcasted_iota(jnp.int32, (ATT_BLOCK, 2 * ATT_BLOCK), 1)
    ok = (c > r) & (c <= r + ATT_BLOCK) & ((c >= ATT_BLOCK) | (i > 0))
    return jnp.where(ok, 0.0, NEG_INF)


def _stack_heads(pairs, left):
    rows = []
    for x in pairs:
        rows += [jnp.where(left, x, 0.0), jnp.where(left, 0.0, x)]
    return jnp.concatenate(rows, axis=0)


def _unstack_heads(y, left, pp):
    r0 = 2 * pp * ATT_BLOCK
    return jnp.where(left, y[r0:r0 + ATT_BLOCK], y[r0 + ATT_BLOCK:r0 + 2 * ATT_BLOCK])


def _row_sums(x):
    return _dot(x, jnp.ones((x.shape[1], 128), BF16), NN)


def _attn_probs(qs, kd, vd, sink, bias):
    n = range(len(qs))
    rows = qs[0].shape[0]
    s = [(_dot(qs[j], kd[j], NT).reshape(rows // ATT_BLOCK, ATT_BLOCK, 2 * ATT_BLOCK) * ATT_SCALE + bias[None])
         .reshape(rows, 2 * ATT_BLOCK) for j in n]
    m = [jnp.max(jnp.maximum(jnp.maximum(s[j][:, :128], s[j][:, 128:]), sink[j]), axis=-1, keepdims=True) for j in n]
    pu = [jnp.exp(s[j] - m[j]) for j in n]
    es = [jnp.exp(sink[j] - m[j]) for j in n]
    ones = jnp.ones((2 * ATT_BLOCK, 128), BF16)
    ov = [_dot(pu[j], jnp.concatenate([vd[j].astype(BF16), ones], axis=1), NN) for j in n]
    inv = [1.0 / (ov[j][:, 128:] + es[j]) for j in n]
    return ([pu[j] * jnp.concatenate([inv[j], inv[j]], axis=1) for j in n], [es[j] * inv[j] for j in n],
            [ov[j][:, :128] * inv[j] for j in n])


def _sink_rows(sinks_l):
    return jnp.broadcast_to(jnp.repeat(sinks_l, ATT_BLOCK)[:, None], (ATT_HEADS * ATT_BLOCK, 128))


_Z0 = (2 * ATT_WIDTH + 2 * KV_WIDTH - ATT_WIDTH) // 256


def _attn_fwd(proj_a, u, sinks_l, cos, sin, name, phase=None):
    B, S, _ = proj_a.shape
    nb = S // ATT_BLOCK

    def body(*refs):
        ins, (u_ref, p_ref, o_ref, ps_ref, qs_ref), _, p_in, p_out, p_sems = _split_refs(refs, 13, 5, 0, phase)
        q_ref, kvc_ref, kvp_ref, z0, z1, z2, z3, cos_ref, sin_ref, cosp_ref, sinp_ref, sinks_ref, _ = ins
        i = pl.program_id(1)
        _hosted_start(phase, p_in, p_out, p_sems, (pl.program_id(0) == 0) & (i == 0))
        left, rope, _, dup = _attn_common()
        cos_c, sin_c = cos_ref[...], sin_ref[...]
        kvc = kvc_ref[...]
        kvp = kvp_ref[...]
        kw = jnp.concatenate([rope(kvp[:, :KV_WIDTH], cosp_ref[...], sinp_ref[...]),
                              rope(kvc[:, :KV_WIDTH], cos_c, sin_c)], axis=0)
        vw = jnp.concatenate([kvp[:, KV_WIDTH:], kvc[:, KV_WIDTH:]], axis=0)
        kd, vd = dup(kw), dup(vw)
        bias = _attn_bias(i)
        zs = (z0, z1, z2, z3)
        pairs = [range(4 * kvh, 4 * kvh + 4) for kvh in range(2)]
        qs = [_stack_heads([rope(q_ref[:, 128 * pr:128 * (pr + 1)], cos_c, sin_c) for pr in pairs[kvh]], left)
              for kvh in range(2)]
        sink = [sinks_ref[kvh * GROUP_ROWS:(kvh + 1) * GROUP_ROWS, :] for kvh in range(2)]
        p, ps, o = _attn_probs(qs, kd, vd, sink, bias)
        eye = (lax.broadcasted_iota(jnp.int32, (ATT_BLOCK, 128), 0)
               == lax.broadcasted_iota(jnp.int32, (ATT_BLOCK, 128), 1))
        for kvh in range(2):
            p_ref[kvh] = p[kvh].astype(BF16)
            qs_ref[kvh] = qs[kvh].astype(BF16)
            for g in range(GROUP):
                blk = ps[kvh][g * ATT_BLOCK:(g + 1) * ATT_BLOCK, :]
                ps_ref[kvh * GROUP + g:kvh * GROUP + g + 1, :] = jnp.sum(jnp.where(eye, blk, 0.0), axis=0, keepdims=True)
            for pp, pr in enumerate(pairs[kvh]):
                z = zs[pr // 2][:, 128 * (pr % 2):128 * (pr % 2 + 1)]
                o128 = _unstack_heads(o[kvh], left, pp)
                o_ref[:, 128 * pr:128 * (pr + 1)] = o128.astype(BF16)
                u_ref[:, 128 * pr:128 * (pr + 1)] = (o128 * _silu(z)).astype(BF16)
        _hosted_finish(phase, p_in, p_out, p_sems, (pl.program_id(0) == B - 1) & (i == nb - 1))

    rowblk = lambda w, cb: pl.BlockSpec((None, ATT_BLOCK, w), lambda b, i: (b, i, cb))
    tab = pl.BlockSpec((ATT_BLOCK, 128), lambda b, i: (i, 0))
    tabp = pl.BlockSpec((ATT_BLOCK, 128), lambda b, i: (jnp.maximum(i - 1, 0), 0))
    p_ispecs, p_ospecs, p_oshapes, p_alias, p_scratch, p_args = _host_phase(phase, 13, 5)
    res = pl.pallas_call(
        body, name=name,
        grid=(B, nb),
        in_specs=[rowblk(ATT_WIDTH, 0), rowblk(256, 4),
                  pl.BlockSpec((None, ATT_BLOCK, 256), lambda b, i: (b, jnp.maximum(i - 1, 0), 4)),
                  rowblk(256, _Z0), rowblk(256, _Z0 + 1), rowblk(256, _Z0 + 2), rowblk(256, _Z0 + 3),
                  tab, tab, tabp, tabp,
                  pl.BlockSpec((ATT_HEADS * ATT_BLOCK, 128), lambda b, i: (0, 0)),
                  pl.BlockSpec(memory_space=pl.ANY)] + p_ispecs,
        out_specs=[pl.BlockSpec((None, ATT_BLOCK, ATT_WIDTH), lambda b, i: (b, i, 1)),
                   pl.BlockSpec((None, None, 2, GROUP_ROWS, 2 * ATT_BLOCK), lambda b, i: (b, i, 0, 0, 0)),
                   pl.BlockSpec((None, ATT_BLOCK, ATT_WIDTH), lambda b, i: (b, i, 0)),
                   pl.BlockSpec((None, None, ATT_HEADS, 128), lambda b, i: (b, i, 0, 0)),
                   pl.BlockSpec((None, None, 2, GROUP_ROWS, 128), lambda b, i: (b, i, 0, 0, 0))] + p_ospecs,
        out_shape=[jax.ShapeDtypeStruct(u.shape, BF16),
                   jax.ShapeDtypeStruct((B, nb, 2, GROUP_ROWS, 2 * ATT_BLOCK), BF16),
                   jax.ShapeDtypeStruct((B, S, ATT_WIDTH), BF16),
                   jax.ShapeDtypeStruct((B, nb, ATT_HEADS, 128), F32),
                   jax.ShapeDtypeStruct((B, nb, 2, GROUP_ROWS, 128), BF16)] + p_oshapes,
        input_output_aliases={12: 0, **p_alias},
        scratch_shapes=p_scratch,
        compiler_params=_params(("arbitrary", "arbitrary")),
    )(proj_a, proj_a, proj_a, proj_a, proj_a, proj_a, proj_a, cos, sin, cos, sin, sinks_l, u, *p_args)
    return res[0], tuple(res[1:5]), list(res[5:])


def _attn_bwd(proj_a, du, kept, cos, sin, name, phase=None):
    B, S, _ = proj_a.shape
    nb = S // ATT_BLOCK
    p_kept, o_kept, ps_kept, qs_kept = kept

    def body(*refs):
        ins, outs, (carry, sk_acc), p_in, p_out, p_sems = _split_refs(refs, 15, 4, 2, phase)
        (qs_ref, kvc_ref, kvp_ref, z0, z1, z2, z3, du_ref, cos_ref, sin_ref, cosp_ref, sinp_ref,
         p_ref, o_ref, ps_ref) = ins
        dq_ref, dkv_ref, dz_ref, dsk_ref = outs
        b_id, i = pl.program_id(0), pl.program_id(1)
        _hosted_start(phase, p_in, p_out, p_sems, (b_id == 0) & (i == 0))

        @pl.when((b_id == 0) & (i == 0))
        def _():
            sk_acc[...] = jnp.zeros_like(sk_acc)

        @pl.when(i == 0)
        def _():
            carry[...] = jnp.zeros_like(carry)

        @pl.when(i < nb)
        def _():
            left, rope, rope_bwd, dup = _attn_common()
            cos_c, sin_c = cos_ref[...], sin_ref[...]
            cos_p, sin_p = cosp_ref[...], sinp_ref[...]
            kvc = kvc_ref[...]
            kvp = kvp_ref[...]
            kw = jnp.concatenate([rope(kvp[:, :KV_WIDTH], cos_p, sin_p), rope(kvc[:, :KV_WIDTH], cos_c, sin_c)], axis=0)
            vw = jnp.concatenate([kvp[:, KV_WIDTH:], kvc[:, KV_WIDTH:]], axis=0)
            kd, vd = dup(kw), dup(vw)
            zs = (z0, z1, z2, z3)
            units = [(kvh, hf) for kvh in range(2) for hf in range(2)]
            half = GROUP_ROWS // 2
            pairs = [range(4 * kvh + 2 * hf, 4 * kvh + 2 * hf + 2) for kvh, hf in units]
            ku = [kd[kvh] for kvh, _ in units]
            vu = [vd[kvh] for kvh, _ in units]
            ps_all = ps_ref[...]
            head_row = lax.broadcasted_iota(jnp.int32, (ATT_HEADS, 128), 0)
            eye = (lax.broadcasted_iota(jnp.int32, (ATT_BLOCK, 128), 0)
                   == lax.broadcasted_iota(jnp.int32, (ATT_BLOCK, 128), 1))

            def first(j):
                kvh, hf = units[j]
                p = p_ref[kvh, hf * half:(hf + 1) * half, :]
                parts = []
                for pr in pairs[j]:
                    cols = slice(128 * pr, 128 * (pr + 1))
                    sg, sg_grad = _silu_and_grad(zs[pr // 2][:, 128 * (pr % 2):128 * (pr % 2 + 1)])
                    du128 = du_ref[:, cols]
                    dz_ref[:, cols] = (du128 * o_ref[:, cols].astype(F32) * sg_grad).astype(BF16)
                    parts.append(du128 * sg)
                dos = _stack_heads(parts, left)
                dp = _dot(dos, vu[j], NT)
                delta = _row_sums(p.astype(F32) * dp)
                ds = (p.astype(F32) * (dp - jnp.concatenate([delta, delta], axis=1)) * ATT_SCALE).astype(BF16)
                sk = jnp.zeros((ATT_HEADS, 128), F32)
                for hh in range(4):
                    hd = kvh * GROUP + 4 * hf + hh
                    drow = jnp.sum(jnp.where(eye, delta[hh * ATT_BLOCK:(hh + 1) * ATT_BLOCK, :], 0.0), axis=0,
                                   keepdims=True)
                    sk = sk - jnp.where(head_row == hd, ps_all * drow, 0.0)
                sk_acc[...] += sk
                return ds, p, dos.astype(BF16), qs_ref[kvh, hf * half:(hf + 1) * half, :]

            def second(j, ds, p, dos, qs):
                dqs = _dot(ds, ku[j], NN)
                for pp, pr in enumerate(pairs[j]):
                    dq_ref[:, 128 * pr:128 * (pr + 1)] = rope_bwd(_unstack_heads(dqs, left, pp),
                                                                  cos_c, sin_c).astype(BF16)
                return _dot(ds, qs, TN), _dot(p, dos, TN)

            got, dku, dvu = {}, [None] * len(units), [None] * len(units)
            for j in range(len(units) + 1):
                if j < len(units):
                    got[j] = first(j)
                if j >= 1:
                    dku[j - 1], dvu[j - 1] = second(j - 1, *got.pop(j - 1))
            dkd = [dku[0] + dku[1], dku[2] + dku[3]]
            dvd = [dvu[0] + dvu[1], dvu[2] + dvu[3]]
            fold = lambda pr: jnp.where(left, pr[0] + pltpu.roll(pr[0], ATT_DIM, 1), pr[1] + pltpu.roll(pr[1], ATT_DIM, 1))
            dkw = fold(dkd)
            dvw = fold(dvd)
            prev = jnp.concatenate([rope_bwd(dkw[:ATT_BLOCK], cos_p, sin_p), dvw[:ATT_BLOCK]], axis=1)
            cur = jnp.concatenate([rope_bwd(dkw[ATT_BLOCK:], cos_c, sin_c), dvw[ATT_BLOCK:]], axis=1)
            dkv_ref[...] = (carry[...] + prev).astype(BF16)
            carry[...] = cur

        @pl.when(i == nb)
        def _():
            dkv_ref[...] = carry[...].astype(BF16)

        @pl.when((b_id == B - 1) & (i == nb))
        def _():
            diag = (lax.broadcasted_iota(jnp.int32, (ATT_HEADS, 128), 0)
                    == lax.broadcasted_iota(jnp.int32, (ATT_HEADS, 128), 1))
            tot = jnp.sum(sk_acc[...], axis=1, keepdims=True)
            dsk_ref[...] = jnp.sum(jnp.where(diag, tot, 0.0), axis=0, keepdims=True)

        _hosted_finish(phase, p_in, p_out, p_sems, (b_id == B - 1) & (i == nb))

    cl = lambda i: jnp.minimum(i, nb - 1)
    pv = lambda i: jnp.maximum(jnp.minimum(i, nb - 1) - 1, 0)
    rowblk = lambda w, cb: pl.BlockSpec((None, ATT_BLOCK, w), lambda b, i: (b, cl(i), cb))
    tab = pl.BlockSpec((ATT_BLOCK, 128), lambda b, i: (cl(i), 0))
    tabp = pl.BlockSpec((ATT_BLOCK, 128), lambda b, i: (pv(i), 0))
    p_ispecs, p_ospecs, p_oshapes, p_alias, p_scratch, p_args = _host_phase(phase, 15, 4)
    res = pl.pallas_call(
        body, name=name,
        grid=(B, nb + 1),
        in_specs=[pl.BlockSpec((None, None, 2, GROUP_ROWS, 128), lambda b, i: (b, cl(i), 0, 0, 0)), rowblk(256, 4),
                  pl.BlockSpec((None, ATT_BLOCK, 256), lambda b, i: (b, pv(i), 4)),
                  rowblk(256, _Z0), rowblk(256, _Z0 + 1), rowblk(256, _Z0 + 2), rowblk(256, _Z0 + 3),
                  rowblk(ATT_WIDTH, 1),
                  tab, tab, tabp, tabp,
                  pl.BlockSpec((None, None, 2, GROUP_ROWS, 2 * ATT_BLOCK), lambda b, i: (b, cl(i), 0, 0, 0)),
                  rowblk(ATT_WIDTH, 0),
                  pl.BlockSpec((None, None, ATT_HEADS, 128), lambda b, i: (b, cl(i), 0, 0))] + p_ispecs,
        out_specs=[rowblk(ATT_WIDTH, 0),
                   pl.BlockSpec((None, ATT_BLOCK, 256), lambda b, i: (b, jnp.maximum(i - 1, 0), 0)),
                   rowblk(ATT_WIDTH, 0),
                   pl.BlockSpec((1, 128), lambda b, i: (0, 0))] + p_ospecs,
        out_shape=[jax.ShapeDtypeStruct((B, S, ATT_WIDTH), BF16), jax.ShapeDtypeStruct((B, S, 256), BF16),
                   jax.ShapeDtypeStruct((B, S, ATT_WIDTH), BF16), jax.ShapeDtypeStruct((1, 128), F32)] + p_oshapes,
        input_output_aliases=p_alias,
        scratch_shapes=[pltpu.VMEM((ATT_BLOCK, 256), F32), pltpu.VMEM((ATT_HEADS, 128), F32)] + p_scratch,
        compiler_params=_params(("arbitrary", "arbitrary")),
    )(qs_kept, proj_a, proj_a, proj_a, proj_a, proj_a, proj_a, du, cos, sin, cos, sin, p_kept, o_kept, ps_kept, *p_args)
    return tuple(res[:4]) + (list(res[4:]),)


def _outproj_fwd(u2, w_out, x2, g_post, target2, name):
    T, D = x2.shape
    tm = _pick(T, (512, 256, 128))
    last = target2 is not None

    def body(u_ref, w_ref, x_ref, g_ref, *rest):
        y = lax.dot_general(u_ref[...], w_ref[...], (NN, ((), ())), preferred_element_type=F32)
        r = lax.rsqrt(jnp.mean(y * y, axis=-1, keepdims=True) + NORM_EPS)
        xn = x_ref[...] + (y * r) * g_ref[...]
        if last:
            t_ref, y_ref, dx_ref, loss_ref = rest
            err = xn - t_ref[...]
            dx_ref[...] = err * (1.0 / D)
            sq = err * err
            acc = sq[:, 0:128]
            for kk in range(1, D // 128):
                acc = acc + sq[:, 128 * kk:128 * (kk + 1)]
            part = jnp.sum(acc.reshape(tm // 8, 8, 128), axis=0) * (0.5 / D)

            @pl.when(pl.program_id(0) == 0)
            def _():
                loss_ref[...] = jnp.zeros_like(loss_ref)

            loss_ref[...] += part
        else:
            y_ref, xn_ref = rest
            xn_ref[...] = xn
        y_ref[...] = y

    row = pl.BlockSpec((tm, D), lambda i: (i, 0))
    in_specs = [pl.BlockSpec((tm, MIX_WIDTH), lambda i: (i, 0)),
                pl.BlockSpec((MIX_WIDTH, D), lambda i: (0, 0)), row,
                pl.BlockSpec((1, D), lambda i: (0, 0))]
    args = [u2, w_out, x2, g_post]
    out_specs = [row, row]
    out_shape = [jax.ShapeDtypeStruct((T, D), F32), jax.ShapeDtypeStruct((T, D), F32)]
    if last:
        in_specs.append(row)
        args.append(target2)
        out_specs.append(pl.BlockSpec((8, 128), lambda i: (0, 0)))
        out_shape.append(jax.ShapeDtypeStruct((8, 128), F32))
    return pl.pallas_call(
        body, name=name, grid=(T // tm,), in_specs=in_specs, out_specs=out_specs, out_shape=out_shape,
        compiler_params=_params(("arbitrary",)),
    )(*args)


def _outproj_bwd(dxn2, y2, g_post, w_out, u2, name):
    T, D = y2.shape
    N = w_out.shape[0]
    tm = _pick(T, (512, 256, 128))
    nt = T // tm

    def body(dx_ref, y_ref, g_ref, w_ref, u_ref, dg_ref, du_ref, dw_ref, acc, wacc):
        i = pl.program_id(0)

        @pl.when(i == 0)
        def _():
            acc[...] = jnp.zeros_like(acc)
            wacc[...] = jnp.zeros_like(wacc)

        y = y_ref[...]
        dxn = dx_ref[...]
        r = lax.rsqrt(jnp.mean(y * y, axis=-1, keepdims=True) + NORM_EPS)
        n = y * r
        dn = dxn * g_ref[...]
        dy = (r * (dn - n * jnp.mean(dn * n, axis=-1, keepdims=True))).astype(BF16)
        du_ref[...] = lax.dot_general(dy, w_ref[...], (NT, ((), ())), preferred_element_type=F32)
        wacc[...] += lax.dot_general(u_ref[...], dy, (TN, ((), ())), preferred_element_type=F32)
        acc[...] += jnp.sum((dxn * n).reshape(tm // 8, 8, D), axis=0)

        @pl.when(i == nt - 1)
        def _():
            dg_ref[...] = jnp.sum(acc[...], axis=0, keepdims=True)
            dw_ref[...] = wacc[...].astype(BF16)

    row = pl.BlockSpec((tm, D), lambda i: (i, 0))
    wide = pl.BlockSpec((tm, N), lambda i: (i, 0))
    vec = pl.BlockSpec((1, D), lambda i: (0, 0))
    whole = pl.BlockSpec((N, D), lambda i: (0, 0))
    return pl.pallas_call(
        body, name=name, grid=(nt,),
        in_specs=[row, row, vec, pl.BlockSpec((N, D), lambda i: (0, 0), pipeline_mode=pl.Buffered(1)), wide],
        out_specs=[vec, wide, whole],
        out_shape=[jax.ShapeDtypeStruct((1, D), F32), jax.ShapeDtypeStruct((T, N), F32),
                   jax.ShapeDtypeStruct((N, D), BF16)],
        scratch_shapes=[pltpu.VMEM((8, D), F32), pltpu.VMEM((N, D), F32)],
        compiler_params=_params(("arbitrary",)),
    )(dxn2, y2, g_post, w_out, u2)


def _inproj_bwd(pieces, w_t, x2, dxn2, g_pre, name, phase=None):
    T, D = x2.shape
    widths = [p.shape[1] for p in pieces]
    offs = [sum(widths[:i]) for i in range(len(pieces))]
    n_p = len(pieces)
    tm = _pick(T, (256, 128))
    nt = T // tm

    def body(*refs):
        ins, (dx_ref, dg_ref), (acc,), p_in, p_out, p_sems = _split_refs(refs, n_p + 4, 2, 1, phase)
        w_ref, x_ref, dxn_ref, g_ref = ins[n_p:]
        i = pl.program_id(0)
        _hosted_start(phase, p_in, p_out, p_sems, i == 0)

        @pl.when(i == 0)
        def _():
            acc[...] = jnp.zeros_like(acc)

        dh = jnp.zeros((tm, D), F32)
        for p in range(n_p):
            dh = dh + lax.dot_general(ins[p][...], w_ref[offs[p]:offs[p] + widths[p], :], (NN, ((), ())),
                                      preferred_element_type=F32)
        x = x_ref[...]
        r = lax.rsqrt(jnp.mean(x * x, axis=-1, keepdims=True) + NORM_EPS)
        n = x * r
        dn = dh * g_ref[...]
        dx_ref[...] = dxn_ref[...] + r * (dn - n * jnp.mean(dn * n, axis=-1, keepdims=True))
        acc[...] += jnp.sum((dh * n).reshape(tm // 8, 8, D), axis=0)

        @pl.when(i == nt - 1)
        def _():
            dg_ref[...] = jnp.sum(acc[...], axis=0, keepdims=True)

        _hosted_finish(phase, p_in, p_out, p_sems, i == nt - 1)

    row = pl.BlockSpec((tm, D), lambda i: (i, 0))
    vec = pl.BlockSpec((1, D), lambda i: (0, 0))
    p_ispecs, p_ospecs, p_oshapes, p_alias, p_scratch, p_args = _host_phase(phase, n_p + 4, 2)
    res = pl.pallas_call(
        body, name=name, grid=(nt,),
        in_specs=[pl.BlockSpec((tm, w), lambda i: (i, 0)) for w in widths]
        + [pl.BlockSpec((sum(widths), D), lambda i: (0, 0), pipeline_mode=pl.Buffered(1)), row, row, vec] + p_ispecs,
        out_specs=[row, vec] + p_ospecs,
        out_shape=[jax.ShapeDtypeStruct((T, D), F32), jax.ShapeDtypeStruct((1, D), F32)] + p_oshapes,
        input_output_aliases=p_alias,
        scratch_shapes=[pltpu.VMEM((8, D), F32)] + p_scratch,
        compiler_params=_params(("arbitrary",)),
    )(*pieces, w_t, x2, dxn2, g_pre, *p_args)
    return res[0], res[1], list(res[2:])


def _step(x, target, g_pre, g_post, lb_param, g_head, sinks, shards=None, full=None):
    B, S, D = x.shape
    T = B * S
    dist = shards is not None
    first, last = 0, DEPTH - 1
    if dist:
        a_loc, b_loc = shards
        ra, rb = a_loc.shape[1], b_loc.shape[1]
        side = _own_side_blocks()
        placed = lambda loc, nm: _place_own(loc, side, "place_" + nm)
        w_in0 = _gather_one_call(a_loc[0], placed(a_loc[0], "in0"), "gather_in0")
        w_in, w_out = [w_in0, None], [None, None]
    else:
        w_in, w_out = list(full[0]), list(full[1])
    cos, sin = _rope_tables(S)
    saved = []
    xs = x
    loss_part = None
    dxn = None
    for l in range(DEPTH):
        x2 = xs.reshape(T, D)
        proj_h, proj_a, h = _inproj(x2, g_pre[l:l + 1], w_in[l], f"inproj{l}")
        proj_h = proj_h.reshape(B, S, N_H)
        proj_a = proj_a.reshape(B, S, N_A)
        phase = None
        if dist and l == first:
            phase = _gather_ici_phase([a_loc[1], b_loc[0]], [placed(a_loc[1], "in1"), placed(b_loc[0], "out0")])
        if dist and l == last:
            phase = _gather_d2d_phase([w_out1_part], [rb])
        o_h, u, states, got = _hgrn_fwd(proj_h, MIX_WIDTH, lb_param, g_head[l:l + 1], l, f"hgrn_fwd{l}", phase)
        phase = None
        if dist and l == first:
            phase = _merge_phases(_gather_d2d_phase(got, [ra, rb]),
                                  _gather_ici_phase([b_loc[1]], [placed(b_loc[1], "out1")]))
        if dist and l == last:
            w_out[1] = got[0]
        u, kept_a, got = _attn_fwd(proj_a, u, _sink_rows(sinks[l]), cos, sin, f"attn_fwd{l}", phase)
        if dist and l == first:
            w_in[1], w_out[0], w_out1_part = got
        u2 = u.reshape(T, MIX_WIDTH)
        if l < last:
            y, xn = _outproj_fwd(u2, w_out[l], x2, g_post[l:l + 1], None, f"outproj{l}")
            xn = xn.reshape(B, S, D)
        else:
            y, dxn, loss_part = _outproj_fwd(u2, w_out[l], x2, g_post[l:l + 1], target.reshape(T, D), f"outproj{l}")
            xn = None
        saved.append((x2, h, proj_h, proj_a, o_h, u2, states, kept_a, y))
        xs = xn

    dw_in, dw_out = [None] * DEPTH, [None] * DEPTH
    dg_pre, dg_post, dlb, dg_head, dsinks = [], [], [], [], []
    for l in reversed(range(DEPTH)):
        x2, h, proj_h, proj_a, o_h, u2, states, kept_a, y = saved[l]
        dgp, du, dw_out[l] = _outproj_bwd(dxn, y, g_post[l:l + 1], w_out[l], u2, f"outproj_bwd{l}")
        du = du.reshape(B, S, MIX_WIDTH)
        phase = None
        if dist:
            phase = _reduce_d2d_phase([dw_out[l]], [rb])
            if l == first:
                phase = _merge_phases(_reduce_ici_phase([part_in1]), phase)
        dqh, dfh, dih, dzh, dlb_l, dgh, got = _hgrn_bwd(
            proj_h, o_h, du, states, lb_param, g_head[l:l + 1], l, f"hgrn_bwd{l}", phase)
        if dist:
            if l == first:
                sum_in = _chip_sum(part_in1, got[0], "chip_sum_in1", 1)
            part_out = _pair_sum(dw_out[l], got[-1], side, f"pair_sum_out{l}")
        dqa, dkv, dza, dsk, got = _attn_bwd(proj_a, du, kept_a, cos, sin, f"attn_bwd{l}",
                                            _reduce_ici_phase([part_out]) if dist else None)
        if dist:
            sum_out = _chip_sum(part_out, got[0], f"chip_sum_out{l}", l, None if l == last else sum_out)
        dproj = [p.reshape(T, p.shape[-1]) for p in (dqh, dfh, dih, dzh, dqa, dkv, dza)]
        dw_in[l] = _mm_tn(dproj, h, f"wgrad_in{l}")
        phase = None
        if dist and l == last:
            phase = _reduce_d2d_phase([dw_in[l]], [ra])
        if dist and l == first:
            got = _run_phase(_reduce_d2d_phase([dw_in[l]], [ra]), "reduce_in0_d2d")
            part_in0 = _pair_sum(dw_in[l], got[0], side, "pair_sum_in0")
            phase = _reduce_ici_phase([part_in0])
        dxn, dgpre, got = _inproj_bwd(dproj, w_in[l], x2, dxn, g_pre[l:l + 1], f"inproj_bwd{l}", phase)
        if dist and l == last:
            part_in1 = _pair_sum(dw_in[l], got[0], side, "pair_sum_in1")
        if dist and l == first:
            sum_in = _chip_sum(part_in0, got[0], "chip_sum_in0", 0, sum_in)
        dg_pre.append(dgpre)
        dg_post.append(dgp)
        dlb.append(dlb_l)
        dg_head.append(dgh)
        dsinks.append(dsk)
    rev = lambda lst: jnp.concatenate(lst[::-1], axis=0)
    if not dist:
        sum_in, sum_out = jnp.stack(dw_in), jnp.stack(dw_out)
    return (loss_part, dxn.reshape(B, S, D), sum_in, sum_out,
            rev(dg_pre), rev(dg_post), rev(dlb), rev(dg_head), rev(dsinks))


def _me_and_peers():
    x, y, c = lax.axis_index("x"), lax.axis_index("y"), lax.axis_index("c")
    me = 4 * x + 2 * y + c
    peers = []
    for k in range(1, N_DEV):
        px = 1 - x if k & 4 else x
        py = 1 - y if k & 2 else y
        pc = 1 - c if k & 1 else c
        peers.append(((px, py, pc), 4 * px + 2 * py + pc))
    return me, peers


class _Phase:
    def __init__(self, arrays, out_shapes, aliases, n_send, build):
        self.arrays, self.out_shapes, self.aliases = list(arrays), list(out_shapes), dict(aliases)
        self.n_send, self.build = n_send, build

    def scratch(self):
        return [pltpu.SemaphoreType.DMA((self.n_send,)), pltpu.SemaphoreType.DMA((self.n_send,))]

    def _copies(self, in_refs, out_refs, sems, arrivals):
        send_sems, recv_sems = sems
        sends, recvs = self.build(in_refs, out_refs)
        assert len(sends) == self.n_send == len(recvs)
        out = [pltpu.make_async_remote_copy(src_ref=s, dst_ref=d, send_sem=send_sems.at[i], recv_sem=recv_sems.at[i],
                                            device_id=dev, device_id_type=MESH) for i, (s, d, dev) in enumerate(sends)]
        inc = [pltpu.make_async_remote_copy(src_ref=s, dst_ref=r, send_sem=send_sems.at[i], recv_sem=recv_sems.at[i],
                                            device_id=dev, device_id_type=MESH)
               for i, ((s, _, dev), r) in enumerate(zip(sends, recvs))] if arrivals else []
        return out, inc

    def start(self, in_refs, out_refs, sems):
        out, _ = self._copies(in_refs, out_refs, sems, False)
        for cp in out:
            cp.start()

    def finish(self, in_refs, out_refs, sems):
        out, inc = self._copies(in_refs, out_refs, sems, True)
        for cp in inc:
            cp.wait_recv()
        for cp in out:
            cp.wait_send()


_ANY = pl.BlockSpec(memory_space=pl.ANY)


def _host_phase(phase, n_in, n_out):
    if phase is None:
        return [], [], [], {}, [], []
    aliases = {n_in + i: n_out + o for i, o in phase.aliases.items()}
    return ([_ANY] * len(phase.arrays), [_ANY] * len(phase.out_shapes), phase.out_shapes, aliases, phase.scratch(),
            phase.arrays)


def _split_refs(refs, n_in, n_out, n_scr, phase):
    pi = len(phase.arrays) if phase else 0
    po = len(phase.out_shapes) if phase else 0
    a = n_in + pi
    b = a + n_out + po
    return (refs[:n_in], refs[a:a + n_out], refs[b:b + n_scr], refs[n_in:a], refs[a + n_out:b], refs[b + n_scr:])


def _hosted_start(phase, p_in, p_out, p_sems, first):
    if phase is not None:
        @pl.when(first)
        def _():
            phase.start(p_in, p_out, p_sems)


def _hosted_finish(phase, p_in, p_out, p_sems, last):
    if phase is not None:
        @pl.when(last)
        def _():
            phase.finish(p_in, p_out, p_sems)


def _run_phase(phase, name):
    n_in, n_out = len(phase.arrays), len(phase.out_shapes)

    def body(*refs):
        phase.start(refs[:n_in], refs[n_in:n_in + n_out], refs[n_in + n_out:])
        phase.finish(refs[:n_in], refs[n_in:n_in + n_out], refs[n_in + n_out:])

    return pl.pallas_call(
        body, name=name, in_specs=[_ANY] * n_in, out_specs=[_ANY] * n_out,
        out_shape=phase.out_shapes, input_output_aliases=phase.aliases, scratch_shapes=phase.scratch(),
        compiler_params=pltpu.CompilerParams(has_side_effects=True),
    )(*phase.arrays)


def _gather_one_call(loc, full, name):
    r = loc.shape[0]

    def body(loc_ref, full_in, full_ref, send_sems, recv_sems):
        del full_in
        c, chips, num = _mesh_place()
        me = num(chips[0], c)
        sib = (*chips[0], 1 - c)

        def copy(k, src, dev_rows, to):
            return pltpu.make_async_remote_copy(src_ref=src, dst_ref=_rows(full_ref, r, dev_rows),
                                                send_sem=send_sems.at[k], recv_sem=recv_sems.at[k],
                                                device_id=to, device_id_type=MESH)

        sent = [copy(0, loc_ref, me, sib)] + [copy(1 + j, loc_ref, me, (*ch, c)) for j, ch in enumerate(chips[1:])]
        for cp in sent:
            cp.start()
        for j, ch in enumerate(chips[1:]):
            copy(1 + j, loc_ref, num(ch, c), sib).wait_recv()
            fw = copy(4 + j, _rows(full_ref, r, num(ch, c)), num(ch, c), sib)
            fw.start()
            sent.append(fw)
        copy(0, loc_ref, num(chips[0], 1 - c), sib).wait_recv()
        for j, ch in enumerate(chips[1:]):
            copy(4 + j, loc_ref, num(ch, 1 - c), sib).wait_recv()
        for cp in sent:
            cp.wait_send()

    return pl.pallas_call(
        body, name=name, in_specs=[_ANY, _ANY], out_specs=_ANY,
        out_shape=jax.ShapeDtypeStruct(full.shape, full.dtype), input_output_aliases={1: 0},
        scratch_shapes=[pltpu.SemaphoreType.DMA((7,)), pltpu.SemaphoreType.DMA((7,))],
        compiler_params=pltpu.CompilerParams(has_side_effects=True),
    )(loc, full)


def _merge_phases(a, b):
    n_in, n_out = len(a.arrays), len(a.out_shapes)
    aliases = dict(a.aliases)
    aliases.update({n_in + i: n_out + o for i, o in b.aliases.items()})

    def build(ins, outs):
        sa, ra = a.build(ins[:n_in], outs[:n_out])
        sb, rb = b.build(ins[n_in:], outs[n_out:])
        return sa + sb, ra + rb

    return _Phase(a.arrays + b.arrays, a.out_shapes + b.out_shapes, aliases, a.n_send + b.n_send, build)


def _mesh_place():
    x, y, c = lax.axis_index("x"), lax.axis_index("y"), lax.axis_index("c")
    chips = [(x, y), (1 - x, y), (x, 1 - y), (1 - x, 1 - y)]
    num = lambda chip, core: 4 * chip[0] + 2 * chip[1] + core
    return c, chips, num


def _own_side_blocks():
    c, chips, num = _mesh_place()
    return jnp.stack([num(ch, c) for ch in chips]).astype(jnp.int32)


def _rows(ref, r, dev):
    return ref.at[pl.ds(pl.multiple_of(dev * r, 16), r), :]


def _place_own(loc, blocks, name):
    r, D = loc.shape
    tr = _pick(r, (400, 256, 200, 128, 64, 16))

    def body(idx_ref, l_ref, o_ref):
        del idx_ref
        o_ref[...] = l_ref[...]

    return pl.pallas_call(
        body, name=name,
        grid_spec=pltpu.PrefetchScalarGridSpec(
            num_scalar_prefetch=1, grid=(r // tr,),
            in_specs=[pl.BlockSpec((tr, D), lambda i, idx: (i, 0))],
            out_specs=pl.BlockSpec((tr, D), lambda i, idx: (idx[0] * (r // tr) + i, 0))),
        out_shape=jax.ShapeDtypeStruct((N_DEV * r, D), loc.dtype),
        compiler_params=_params(("arbitrary",)),
    )(blocks, loc)


def _gather_ici_phase(locs, fulls):
    rs = [a.shape[0] for a in locs]
    n = len(locs)

    def build(ins, outs):
        c, chips, num = _mesh_place()
        me = num(chips[0], c)
        targets = [((*chips[0], 1 - c), num(chips[0], 1 - c))] + [((*ch, c), num(ch, c)) for ch in chips[1:]]
        sends, recvs = [], []
        for dev, dnum in targets:
            for i, r in enumerate(rs):
                sends.append((ins[i], _rows(outs[i], r, me), dev))
                recvs.append(_rows(outs[i], r, dnum))
        return sends, recvs

    shapes = [jax.ShapeDtypeStruct(a.shape, a.dtype) for a in fulls]
    return _Phase(list(locs) + list(fulls), shapes, {n + i: i for i in range(n)}, 4 * n, build)


def _gather_d2d_phase(fulls, rs):
    def build(ins, outs):
        c, chips, num = _mesh_place()
        sib = (*chips[0], 1 - c)
        sends, recvs = [], []
        for ch in chips[1:]:
            for i, r in enumerate(rs):
                blk = _rows(outs[i], r, num(ch, c))
                sends.append((blk, blk, sib))
                recvs.append(_rows(outs[i], r, num(ch, 1 - c)))
        return sends, recvs

    shapes = [jax.ShapeDtypeStruct(a.shape, a.dtype) for a in fulls]
    return _Phase(fulls, shapes, {i: i for i in range(len(fulls))}, 3 * len(fulls), build)


def _reduce_d2d_phase(grads, rs):
    def build(ins, outs):
        c, chips, num = _mesh_place()
        sib = (*chips[0], 1 - c)
        sends, recvs = [], []
        for j, ch in enumerate(chips):
            for i, r in enumerate(rs):
                sends.append((_rows(ins[i], r, num(ch, 1 - c)), outs[i].at[j], sib))
                recvs.append(outs[i].at[j])
        return sends, recvs

    shapes = [jax.ShapeDtypeStruct((4, r, g.shape[1]), g.dtype) for g, r in zip(grads, rs)]
    return _Phase(grads, shapes, {}, 4 * len(grads), build)


def _reduce_ici_phase(parts):
    def build(ins, outs):
        c, chips, _ = _mesh_place()
        sends, recvs = [], []
        for t in range(1, 4):
            for i in range(len(parts)):
                sends.append((ins[i].at[t], outs[i].at[t - 1], (*chips[t], c)))
                recvs.append(outs[i].at[t - 1])
        return sends, recvs

    shapes = [jax.ShapeDtypeStruct((3,) + p.shape[1:], p.dtype) for p in parts]
    return _Phase(parts, shapes, {}, 3 * len(parts), build)


def _pair_sum(g, got, blocks, name):
    n, r, D = got.shape
    tr = _pick(r, (800, 400, 256, 200, 128, 64, 16))

    def body(idx_ref, g_ref, r_ref, o_ref):
        del idx_ref
        o_ref[...] = (g_ref[...].astype(F32) + r_ref[...].astype(F32)).astype(o_ref.dtype)

    blk = pl.BlockSpec((None, tr, D), lambda j, i, idx: (j, i, 0))
    return pl.pallas_call(
        body, name=name,
        grid_spec=pltpu.PrefetchScalarGridSpec(
            num_scalar_prefetch=1, grid=(n, r // tr),
            in_specs=[pl.BlockSpec((tr, D), lambda j, i, idx: (idx[j] * (r // tr) + i, 0)), blk],
            out_specs=blk),
        out_shape=jax.ShapeDtypeStruct(got.shape, got.dtype),
        compiler_params=_params(("arbitrary", "arbitrary")),
    )(blocks, g, got)


def _chip_sum(p, r, name, layer, into=None):
    _, R, D = p.shape
    tr = _pick(R, (800, 400, 256, 200, 128, 64, 16))

    def body(p_ref, r_ref, *rest):
        acc = p_ref[...].astype(F32)
        for t in range(3):
            acc = acc + r_ref[t].astype(F32)
        rest[-1][...] = acc

    args = [p, r] + ([] if into is None else [into])
    return pl.pallas_call(
        body, name=name, grid=(R // tr,),
        in_specs=[pl.BlockSpec((None, tr, D), lambda i: (0, i, 0)), pl.BlockSpec((3, tr, D), lambda i: (0, i, 0))]
        + ([] if into is None else [_ANY]),
        out_specs=pl.BlockSpec((None, tr, D), lambda i: (layer, i, 0)),
        out_shape=jax.ShapeDtypeStruct((DEPTH, R, D), F32),
        input_output_aliases={} if into is None else {2: 0},
        compiler_params=_params(("parallel",)))(*args)


def _allreduce_small(vec):
    R, C = vec.shape

    def body(v_ref, o_ref, buf, send_sems, recv_sems):
        me, peers = _me_and_peers()
        buf[me] = v_ref[...]
        sends = []
        for k, (pid, _) in enumerate(peers):
            cp = pltpu.make_async_remote_copy(src_ref=v_ref, dst_ref=buf.at[me], send_sem=send_sems.at[k],
                                              recv_sem=recv_sems.at[k], device_id=pid, device_id_type=MESH)
            cp.start()
            sends.append(cp)
        for k, (pid, pnum) in enumerate(peers):
            pltpu.make_async_remote_copy(src_ref=v_ref, dst_ref=buf.at[pnum], send_sem=send_sems.at[k],
                                         recv_sem=recv_sems.at[k], device_id=pid, device_id_type=MESH).wait_recv()
        for cp in sends:
            cp.wait_send()
        acc = buf[0]
        for d in range(1, N_DEV):
            acc = acc + buf[d]
        o_ref[...] = acc

    vm = pl.BlockSpec(memory_space=pltpu.VMEM)
    return pl.pallas_call(
        body, name="allreduce_small",
        in_specs=[vm], out_specs=vm,
        out_shape=jax.ShapeDtypeStruct((R, C), F32),
        scratch_shapes=[pltpu.VMEM((N_DEV, R, C), F32), pltpu.SemaphoreType.DMA((N_DEV - 1,)),
                        pltpu.SemaphoreType.DMA((N_DEV - 1,))],
        compiler_params=pltpu.CompilerParams(has_side_effects=True),
    )(vec)


def _adamw(w, g, m, v, name):
    R, C = w.shape
    tr = _pick(R, (512, 400, 256, 128, 64, 32, 16, 8)) if R >= 8 else R
    c1 = 1.0 - ADAM_B1 ** ADAM_STEP
    c2 = 1.0 - ADAM_B2 ** ADAM_STEP

    def body(w_ref, g_ref, m_ref, v_ref, d_ref, mo_ref, vo_ref):
        gg = g_ref[...]
        mn = ADAM_B1 * m_ref[...] + (1.0 - ADAM_B1) * gg
        vn = ADAM_B2 * v_ref[...] + (1.0 - ADAM_B2) * (gg * gg)
        d_ref[...] = -ADAM_LR * ((mn / c1) / (jnp.sqrt(vn / c2) + ADAM_EPS) + ADAM_WD * w_ref[...])
        mo_ref[...] = mn
        vo_ref[...] = vn

    blk = pl.BlockSpec((tr, C), lambda i: (i, 0))
    sh = jax.ShapeDtypeStruct((R, C), F32)
    return pl.pallas_call(
        body, name=name, grid=(R // tr,), in_specs=[blk] * 4, out_specs=[blk] * 3, out_shape=[sh] * 3,
        compiler_params=_params(("parallel",)),
    )(w, g, m, v)


def _lb_param_grad(lb_param, dlb):
    L, C = lb_param.shape

    def body(p_ref, d_ref, o_ref):
        lbp = p_ref[...]
        d = d_ref[...]
        mx = jnp.max(lbp, axis=0, keepdims=True)
        e = jnp.exp(lbp - mx)
        p = e / jnp.sum(e, axis=0, keepdims=True)
        tot = jnp.sum(d, axis=0, keepdims=True)
        dps = []
        rest = tot
        for j in range(L):
            dps.append(rest - tot if j == 0 else rest)
            rest = rest - d[j:j + 1]
        dp = jnp.concatenate(dps, axis=0)
        o_ref[...] = p * (dp - jnp.sum(p * dp, axis=0, keepdims=True))

    vm = pl.BlockSpec(memory_space=pltpu.VMEM)
    return pl.pallas_call(body, name="lb_param_grad", in_specs=[vm, vm], out_specs=vm,
                          out_shape=jax.ShapeDtypeStruct((L, C), F32))(lb_param, dlb)


def _pack_small(loss_part, dg_pre, dg_post, dlb, dg_head, dsinks):
    pad8 = lambda a: jnp.pad(a.reshape(-1, 128), ((0, 8 - DEPTH), (0, 0)))
    rows = [dg_pre.reshape(-1, 128), dg_post.reshape(-1, 128), dlb.reshape(-1, 128), pad8(dg_head), pad8(dsinks),
            loss_part]
    return jnp.concatenate(rows, axis=0)


def _unpack_small(vec):
    n = DEPTH * D_MODEL // 128
    o = 0
    dg_pre = vec[o:o + n].reshape(DEPTH, D_MODEL); o += n
    dg_post = vec[o:o + n].reshape(DEPTH, D_MODEL); o += n
    dlb = vec[o:o + n].reshape(DEPTH, HG_WIDTH); o += n
    dg_head = vec[o:o + DEPTH]; o += 8
    dsinks = vec[o:o + DEPTH, :ATT_HEADS]; o += 8
    loss = jnp.sum(vec[o:o + 8])
    return loss, dg_pre, dg_post, dlb, dg_head, dsinks


def kernel(x, w_in, w_out, g_pre, g_post, lb_param, g_head, sinks, loss_target, m_w_in, m_w_out, m_g_pre, m_g_post, m_lb_param, m_g_head, m_sinks, v_w_in, v_w_out, v_g_pre, v_g_post, v_lb_param, v_g_head, v_sinks):
    tr = lambda a: jnp.swapaxes(a, 1, 2)
    w_in_t = tr(w_in)
    (loss_part, dx, gw_in_t, gw_out, dg_pre, dg_post, dlb, dg_head, dsinks) = _step(
        x, loss_target, g_pre, g_post, lb_param, g_head, sinks, shards=(w_in_t.astype(BF16), w_out.astype(BF16)))

    small = _allreduce_small(_pack_small(loss_part, dg_pre, dg_post, dlb, dg_head, dsinks))
    loss, gg_pre, gg_post, gdlb, gg_head, gsinks = _unpack_small(small)
    glb = _lb_param_grad(lb_param, gdlb)

    grads = [gw_in_t, gw_out, gg_pre, gg_post, glb, gg_head, gsinks]
    ws = [w_in_t, w_out, g_pre, g_post, lb_param, g_head, sinks]
    ms = [tr(m_w_in), m_w_out, m_g_pre, m_g_post, m_lb_param, m_g_head, m_sinks]
    vs = [tr(v_w_in), v_w_out, v_g_pre, v_g_post, v_lb_param, v_g_head, v_sinks]
    names = ["w_in", "w_out", "g_pre", "g_post", "lb_param", "g_head", "sinks"]
    deltas, new_m, new_v = [], [], []
    for w, g, m, v, nm in zip(ws, grads, ms, vs, names):
        sh = w.shape
        two = lambda a: a.reshape(-1, sh[-1])
        d, mn, vn = _adamw(two(w), two(g), two(m), two(v), "adamw_" + nm)
        deltas.append(d.reshape(sh))
        new_m.append(mn.reshape(sh))
        new_v.append(vn.reshape(sh))
    grads[0], deltas[0], new_m[0], new_v[0] = tr(grads[0]), tr(deltas[0]), tr(new_m[0]), tr(new_v[0])
    return (loss, dx, *grads, *deltas, *new_m, *new_v)
```

```python
import math

import numpy as np
import jax
import jax.numpy as jnp
from jax import lax
from jax.experimental import pallas as pl
from jax.experimental.pallas import tpu as pltpu

F32 = jnp.float32
BF16 = jnp.bfloat16

D_MODEL = 1024
DEPTH = 2
HG_HEADS = 8
HG_DIM = 128
HG_WIDTH = HG_HEADS * HG_DIM
CHUNK = 64
ATT_HEADS = 16
ATT_DIM = 64
ATT_WIDTH = ATT_HEADS * ATT_DIM
KV_WIDTH = 128
ATT_BLOCK = 128
ATT_SCALE = 1.0 / math.sqrt(ATT_DIM)
ROPE_THETA = 10000.0
NORM_EPS = 1e-6
NEG_INF = -1e30
LB_FLOOR = 1e-20
N_H = 4 * HG_WIDTH
N_A = 2 * ATT_WIDTH + 2 * KV_WIDTH
IN_WIDTH = N_H + N_A
MIX_WIDTH = HG_WIDTH + ATT_WIDTH

ADAM_LR = 0.001
ADAM_B1 = 0.9
ADAM_B2 = 0.999
ADAM_EPS = 1e-08
ADAM_WD = 0.01
ADAM_STEP = 10

N_DEV = 8
MESH = pl.DeviceIdType.MESH
VMEM_LIMIT = 56 * 1024 * 1024

NN = ((1,), (0,))
NT = ((1,), (1,))
TN = ((0,), (0,))


def _dot(a, b, dims):
    return lax.dot_general(a.astype(BF16), b.astype(BF16), (dims, ((), ())), preferred_element_type=F32)


def _params(sem=None, **kw):
    return pltpu.CompilerParams(dimension_semantics=sem, vmem_limit_bytes=VMEM_LIMIT, **kw)


def _sigmoids(x):
    e = jnp.exp(-jnp.abs(x))
    r = 1.0 / (1.0 + e)
    er = e * r
    pos = x >= 0.0
    return jnp.where(pos, r, er), jnp.where(pos, er, r)


def _silu(x):
    return x * _sigmoids(x)[0]


def _silu_and_grad(x):
    s, ns = _sigmoids(x)
    return x * s, s * (1.0 + x * ns)


def _pick(n, prefs):
    for p in prefs:
        if n % p == 0:
            return p
    return n


def _inproj(x2, g, w, name):
    T, D = x2.shape
    tm = _pick(T, (512, 256, 128))
    nchunk = 1024

    def body(x_ref, g_ref, w_ref, oh_ref, oa_ref, h_ref):
        x = x_ref[...]
        r = lax.rsqrt(jnp.mean(x * x, axis=-1, keepdims=True) + NORM_EPS)
        h = ((x * r) * g_ref[...]).astype(BF16)
        h_ref[...] = h
        for j in range(0, N_H, nchunk):
            oh_ref[:, j:j + nchunk] = lax.dot_general(h, w_ref[j:j + nchunk, :], (NT, ((), ())),
                                                      preferred_element_type=F32)
        for j in range(0, N_A, N_A // 2):
            oa_ref[:, j:j + N_A // 2] = lax.dot_general(h, w_ref[N_H + j:N_H + j + N_A // 2, :], (NT, ((), ())),
                                                        preferred_element_type=F32)

    row = lambda w_: pl.BlockSpec((tm, w_), lambda i: (i, 0))
    return pl.pallas_call(
        body, name=name,
        grid=(T // tm,),
        in_specs=[row(D), pl.BlockSpec((1, D), lambda i: (0, 0)),
                  pl.BlockSpec((IN_WIDTH, D), lambda i: (0, 0), pipeline_mode=pl.Buffered(1))],
        out_specs=[row(N_H), row(N_A), row(D)],
        out_shape=[jax.ShapeDtypeStruct((T, N_H), F32), jax.ShapeDtypeStruct((T, N_A), F32),
                   jax.ShapeDtypeStruct((T, D), BF16)],
        compiler_params=_params(("parallel",)),
    )(x2, g, w)


def _mm_tn(pieces, b, name, out_dtype=BF16):
    T, m = b.shape
    tn = 256
    counts = [p.shape[1] // tn for p in pieces]
    starts = [sum(counts[:i]) for i in range(len(pieces))]
    n_p = len(pieces)

    def body(*refs):
        b_ref, o_ref = refs[n_p], refs[n_p + 1]
        i = pl.program_id(0)
        for p in range(n_p):
            @pl.when((i >= starts[p]) & (i < starts[p] + counts[p]))
            def _(p=p):
                o_ref[...] = lax.dot_general(refs[p][...], b_ref[...], (TN, ((), ())),
                                             preferred_element_type=F32).astype(out_dtype)

    piece_spec = lambda s, c: pl.BlockSpec((T, tn), lambda i: (0, jnp.clip(i - s, 0, c - 1)))
    return pl.pallas_call(
        body, name=name,
        grid=(sum(counts),),
        in_specs=[piece_spec(s, c) for s, c in zip(starts, counts)]
        + [pl.BlockSpec((T, m), lambda i: (0, 0), pipeline_mode=pl.Buffered(1))],
        out_specs=pl.BlockSpec((tn, m), lambda i: (i, 0)),
        out_shape=jax.ShapeDtypeStruct((sum(counts) * tn, m), out_dtype),
        compiler_params=_params(("arbitrary",)),
    )(*pieces, b)


_LEVELS = (0, 1, 2, 4, 8, 16, 32)
_CUM_L = (2, 4, 8, 16, 32, 64)
_ALL_KINDS = tuple(("c", L) for L in _CUM_L) + tuple(("r", L) for L in _CUM_L)
_MXU_KINDS = (("c", 2), ("c", 4), ("c", CHUNK), ("r", 2), ("r", 4))
N_CUM = len(_ALL_KINDS) * CHUNK
N_CUM_F = len(_MXU_KINDS) * CHUNK


def _cum_matrices():
    t = np.arange(CHUNK)[:, None]
    r = np.arange(CHUNK)[None, :]

    def mat(kind):
        c, L = kind
        return ((r // L == t // L) & ((r <= t) if c == "c" else (r > t))).astype(np.float32)

    fwd = np.concatenate([mat(kd) for kd in _MXU_KINDS], axis=0)
    full = np.concatenate([mat(kd) for kd in _ALL_KINDS], axis=0)
    return jnp.asarray(fwd, BF16), jnp.asarray(full.T.copy(), BF16)


def _level_masks():
    t = np.arange(CHUNK)[:, None]
    s = np.arange(CHUNK)[None, :]
    ms = []
    for L in _LEVELS:
        if L == 0:
            ms.append(t == s)
        else:
            ms.append((t // (2 * L) == s // (2 * L)) & ((t // L) % 2 == 1) & ((s // L) % 2 == 0))
    return jnp.asarray(np.stack(ms).astype(np.float32))


def _split3(x):
    hi = x.astype(BF16)
    r1 = x - hi.astype(F32)
    mid = r1.astype(BF16)
    lo = (r1 - mid.astype(F32)).astype(BF16)
    return hi, mid, lo


def _cum3(ts, x, terms=3):
    d = lambda p: lax.dot_general(ts, p, (NN, ((), ())), preferred_element_type=F32)
    return sum(d(p) for p in _split3(x)[:terms])


def _lb_terms(lbp, layer):
    mx = jnp.max(lbp, axis=0, keepdims=True)
    e = jnp.exp(lbp - mx)
    p = e / jnp.sum(e, axis=0, keepdims=True)
    cum = p[0:1]
    for j in range(1, layer + 1):
        cum = cum + p[j:j + 1]
    lb = cum - p[0:1]
    lbf = jnp.maximum(lb, LB_FLOOR)
    return dict(lbf=lbf, one_m=1.0 - lb, kcorr=lb - lbf, ind=jnp.where(lb > LB_FLOOR, 1.0, 0.0))


def _gate(x, lt):
    sig, nsig = _sigmoids(x)
    f = lt["lbf"] + lt["one_m"] * sig
    return jnp.log(f), lt["one_m"] * nsig + lt["kcorr"], f, sig, nsig


def _ck(x, ci):
    return x[ci * CHUNK:(ci + 1) * CHUNK]


def _block_cums(ts, g, nc):
    cs = [_cum3(ts, _ck(g, ci), terms=2) for ci in range(nc)]
    out = {kind: jnp.concatenate([c[CHUNK * i:CHUNK * (i + 1)] for c in cs], axis=0)
           for i, kind in enumerate(_MXU_KINDS)}
    b = out[("c", CHUNK)]
    ng = CHUNK // 8
    last = b.reshape(nc, ng, 8, HG_DIM)[:, :, 7:8, :]
    zero = jnp.zeros((nc, 1, 1, HG_DIM), F32)

    def spread(groups):
        return jnp.broadcast_to(jnp.concatenate(groups, axis=1), (nc, ng, 8, HG_DIM)).reshape(nc * CHUNK, HG_DIM)

    def get(kind):
        if kind in out:
            return out[kind]
        c, L = kind
        nb = L // 8
        first = lambda r: (r // nb) * nb
        if c == "c":
            return b - spread([last[:, first(r) - 1:first(r)] if r >= nb else zero for r in range(ng)])
        return spread([last[:, first(r) + nb - 1:first(r) + nb] for r in range(ng)]) - b

    return get


def _level_factors(cums, g, L):
    if L == 0:
        return None, None
    if L == 1:
        return jnp.exp(g), None
    return jnp.exp(cums(("c", L))), jnp.exp(cums(("r", L)))


def _mul(a, e):
    return a if e is None else a * e


def _hg_block_fwd(qf, k, v, g, ts, m_ref, nc):
    cums = _block_cums(ts, g, nc)
    amat = [jnp.zeros((CHUNK, CHUNK), F32)] * nc
    for li, L in enumerate(_LEVELS):
        eq, ek = _level_factors(cums, g, L)
        ql, kl, m = _mul(qf, eq), _mul(k, ek), m_ref[li]
        amat = [amat[ci] + _dot(_ck(ql, ci), _ck(kl, ci), NT) * m for ci in range(nc)]
    b = cums(("c", CHUNK))
    kst = k * jnp.exp(cums(("r", CHUNK)))
    o = [_dot(amat[ci], _ck(v, ci), NN) for ci in range(nc)]
    kv = [_dot(_ck(v, ci), _ck(kst, ci), TN) for ci in range(nc)]
    dec = [jnp.exp(b[(ci + 1) * CHUNK - 1:(ci + 1) * CHUNK, :]) for ci in range(nc)]
    return o, dec, kv, qf * jnp.exp(b), amat


def _hg_block_bwd(qf, k, v, g, do, amat, ts, m_ref, nc):
    cums = _block_cums(ts, g, nc)
    dcs = {}
    da = [_dot(_ck(do, ci), _ck(v, ci), NT) for ci in range(nc)]
    dq = jnp.zeros_like(qf)
    dk = jnp.zeros_like(qf)
    dg = jnp.zeros_like(qf)
    for li, L in enumerate(_LEVELS):
        eq, ek = _level_factors(cums, g, L)
        ql, kl, m = _mul(qf, eq), _mul(k, ek), m_ref[li]
        qlb, klb = ql.astype(BF16), kl.astype(BF16)
        dal = [(da[ci] * m).astype(BF16) for ci in range(nc)]
        dql = jnp.concatenate([_dot(dal[ci], _ck(klb, ci), NN) for ci in range(nc)], axis=0)
        dkl = jnp.concatenate([_dot(dal[ci], _ck(qlb, ci), TN) for ci in range(nc)], axis=0)
        dq = dq + _mul(dql, eq)
        dk = dk + _mul(dkl, ek)
        if L == 1:
            dg = dg + dql * ql
        elif L > 1:
            dcs[("c", L)] = (dql * ql).astype(BF16)
            dcs[("r", L)] = (dkl * kl).astype(BF16)
    b = cums(("c", CHUNK))
    e64 = jnp.exp(b)
    er64 = jnp.exp(cums(("r", CHUNK)))
    qb = qf * e64
    return dict(dq=dq, dk=dk, dg=dg, dcs=dcs, e64=e64, er64=er64, qb=qb, kst=k * er64,
                dv=[_dot(amat[ci], _ck(do, ci), TN) for ci in range(nc)],
                dec=[jnp.exp(b[(ci + 1) * CHUNK - 1:(ci + 1) * CHUNK, :]) for ci in range(nc)],
                qd=[_dot(_ck(do, ci), _ck(qb, ci), TN) for ci in range(nc)])


def _hg_state_bwd(w, v, do, starts, ends, tst, nc):
    dqb = jnp.concatenate([_dot(_ck(do, ci), starts[ci], NN) for ci in range(nc)], axis=0)
    dkst = jnp.concatenate([_dot(_ck(v, ci), ends[ci], NN) for ci in range(nc)], axis=0)
    dq = w["dq"] + dqb * w["e64"]
    dk = w["dk"] + dkst * w["er64"]
    dv = jnp.concatenate([w["dv"][ci] + _dot(_ck(w["kst"], ci), ends[ci], NT) for ci in range(nc)], axis=0)
    trow = lax.broadcasted_iota(jnp.int32, (CHUNK, 1), 0)
    dtot = jnp.concatenate(
        [jnp.where(trow == CHUNK - 1, jnp.sum(ends[ci] * starts[ci], axis=0, keepdims=True) * w["dec"][ci], 0.0)
         for ci in range(nc)], axis=0)
    dcs = dict(w["dcs"])
    dcs[("c", CHUNK)] = (dqb * w["qb"] + dtot).astype(BF16)
    dcs[("r", CHUNK)] = (dkst * w["kst"]).astype(BF16)
    dgs = [_dot(tst, jnp.concatenate([_ck(dcs[kind], ci) for kind in _ALL_KINDS], axis=0), NN) for ci in range(nc)]
    return dq, dk, dv, w["dg"] + jnp.concatenate(dgs, axis=0)


def _hgrn_fwd(proj_h, u_rows, lb_param, g_head, layer, name, phase=None):
    B, S, _ = proj_h.shape
    sb = _pick(S, (2048, 1024, 512, 256, 128, 64))
    nc = sb // CHUNK
    ts, _ = _cum_matrices()

    def body(*refs):
        ins, outs, (st,), p_in, p_out, p_sems = _split_refs(refs, 8, 12, 1, phase)
        q_ref, f_ref, i_ref, z_ref, lbp_ref, gh_ref, ts_ref, m_ref = ins
        o_ref, u_ref, sts_ref, am_ref = outs[:4]
        logf_ref, k_ref, qf_ref, sg_ref, qg_ref, zg_ref, fg_ref, sig_ref = outs[4:]
        h_id, b_id, s_id = pl.program_id(0), pl.program_id(1), pl.program_id(2)
        _hosted_start(phase, p_in, p_out, p_sems, (h_id == 0) & (b_id == 0) & (s_id == 0))

        @pl.when(s_id == 0)
        def _():
            st[...] = jnp.zeros_like(st)

        lt = _lb_terms(lbp_ref[...], layer)
        tsv = ts_ref[...]
        gh = gh_ref[...]
        logf, k, _, sig, nsig = _gate(f_ref[...], lt)
        qf, qf_grad = _silu_and_grad(q_ref[...])
        sg, sg_grad = _silu_and_grad(z_ref[...])
        logf_ref[...], k_ref[...], qf_ref[...], sg_ref[...] = logf, k, qf, sg
        qg_ref[...] = qf_grad.astype(BF16)
        zg_ref[...] = sg_grad.astype(BF16)
        fg_ref[...] = (lt["one_m"] * sig * nsig).astype(BF16)
        sig_ref[...] = sig.astype(BF16)
        o_part, dec, kv, qb, amat = _hg_block_fwd(qf, k, i_ref[...], logf, tsv, m_ref, nc)
        for ci in range(nc):
            am_ref[ci] = amat[ci].astype(BF16)
        cur = st[...]
        starts = []
        for ci in range(nc):
            sts_ref[ci] = cur
            starts.append(cur)
            cur = cur * dec[ci] + kv[ci]
        st[...] = cur
        o = jnp.concatenate([o_part[ci] + _dot(_ck(qb, ci), starts[ci], NT) for ci in range(nc)], axis=0)
        o_ref[...] = o
        r = lax.rsqrt(jnp.mean(o * o, axis=-1, keepdims=True) + NORM_EPS)
        u_ref[...] = (((o * r) * gh) * sg).astype(BF16)
        _hosted_finish(phase, p_in, p_out, p_sems, (h_id == HG_HEADS - 1) & (b_id == B - 1) & (s_id == S // sb - 1))

    col = lambda base: pl.BlockSpec((None, sb, HG_DIM), lambda h, b, s: (b, s, base + h))
    p_ispecs, p_ospecs, p_oshapes, p_alias, p_scratch, p_args = _host_phase(phase, 8, 12)
    wide = lambda dt: jax.ShapeDtypeStruct((B, S, HG_WIDTH), dt)
    res = pl.pallas_call(
        body, name=name,
        grid=(HG_HEADS, B, S // sb),
        in_specs=[col(0), col(HG_HEADS), col(2 * HG_HEADS), col(3 * HG_HEADS),
                  pl.BlockSpec((DEPTH, HG_DIM), lambda h, b, s: (0, h)),
                  pl.BlockSpec((1, HG_DIM), lambda h, b, s: (0, 0)),
                  pl.BlockSpec((N_CUM_F, CHUNK), lambda h, b, s: (0, 0)),
                  pl.BlockSpec((len(_LEVELS), CHUNK, CHUNK), lambda h, b, s: (0, 0, 0))] + p_ispecs,
        out_specs=[col(0), col(0),
                   pl.BlockSpec((None, None, nc, HG_DIM, HG_DIM), lambda h, b, s: (b, h, s, 0, 0)),
                   pl.BlockSpec((None, None, nc, CHUNK, CHUNK), lambda h, b, s: (b, h, s, 0, 0))]
        + [col(0)] * 8 + p_ospecs,
        out_shape=[wide(F32),
                   jax.ShapeDtypeStruct((B, S, u_rows), BF16),
                   jax.ShapeDtypeStruct((B, HG_HEADS, S // CHUNK, HG_DIM, HG_DIM), F32),
                   jax.ShapeDtypeStruct((B, HG_HEADS, S // CHUNK, CHUNK, CHUNK), BF16)]
        + [wide(F32)] * 4 + [wide(BF16)] * 4 + p_oshapes,
        input_output_aliases=p_alias,
        scratch_shapes=[pltpu.VMEM((HG_DIM, HG_DIM), F32)] + p_scratch,
        compiler_params=_params(("arbitrary", "arbitrary", "arbitrary")),
    )(proj_h, proj_h, proj_h, proj_h, lb_param, g_head, ts, _level_masks(), *p_args)
    return res[0], res[1], tuple(res[2:12]), list(res[12:])


def _hgrn_bwd(proj_h, o_h, du, kept, lb_param, g_head, layer, name, phase=None):
    B, S, _ = proj_h.shape
    sb = _pick(S, (512, 256, 128, 64))
    nc = sb // CHUNK
    ns = S // sb
    ts, tst = _cum_matrices()

    def body(*refs):
        ins, outs, (dst,), p_in, p_out, p_sems = _split_refs(refs, 18, 6, 1, phase)
        (i_ref, o_ref, du_ref, sts_ref, am_ref, logf_ref, k_ref, qf_ref, sg_ref, qg_ref, zg_ref, fg_ref, sig_ref,
         lbp_ref, gh_ref, ts_ref, tst_ref, m_ref) = ins
        dq_ref, df_ref, di_ref, dz_ref, dlb_ref, dgh_ref = outs
        h_id, b_id, s_id = pl.program_id(0), pl.program_id(1), pl.program_id(2)
        _hosted_start(phase, p_in, p_out, p_sems, (h_id == 0) & (b_id == 0) & (s_id == 0))

        @pl.when(s_id == 0)
        def _():
            dst[...] = jnp.zeros_like(dst)

        @pl.when((b_id == 0) & (s_id == 0))
        def _():
            dlb_ref[...] = jnp.zeros_like(dlb_ref)

        @pl.when((h_id == 0) & (b_id == 0) & (s_id == 0))
        def _():
            dgh_ref[...] = jnp.zeros_like(dgh_ref)

        lt = _lb_terms(lbp_ref[...], layer)
        gh = gh_ref[...]
        tsv = ts_ref[...]
        tstv = tst_ref[...]
        logf, k, qf, sg = logf_ref[...], k_ref[...], qf_ref[...], sg_ref[...]
        o = o_ref[...]
        dub = du_ref[...]
        r = lax.rsqrt(jnp.mean(o * o, axis=-1, keepdims=True) + NORM_EPS)
        n = o * r
        dz_ref[...] = (dub * (n * gh) * zg_ref[...].astype(F32)).astype(BF16)
        dgh_ref[...] += jnp.sum(dub * sg * n, axis=0, keepdims=True)
        dn = dub * sg * gh
        do = r * (dn - n * jnp.mean(dn * n, axis=-1, keepdims=True))
        v = i_ref[...]
        w = _hg_block_bwd(qf, k, v, logf, do, [am_ref[ci] for ci in range(nc)], tsv, m_ref, nc)
        cur = dst[...]
        ends = [None] * nc
        for ci in reversed(range(nc)):
            ends[ci] = cur
            cur = cur * w["dec"][ci] + w["qd"][ci]
        dst[...] = cur
        dq, dk, dv, dg = _hg_state_bwd(w, v, do, [sts_ref[ci] for ci in range(nc)], ends, tstv, nc)
        di_ref[...] = dv.astype(BF16)
        dq_ref[...] = (dq * qg_ref[...].astype(F32)).astype(BF16)
        f = jnp.exp(logf)
        scaled = (dg - f * dk) / f
        df_ref[...] = (scaled * fg_ref[...].astype(F32)).astype(BF16)
        dlb_ref[...] += jnp.sum(scaled * (lt["ind"] - sig_ref[...].astype(F32)), axis=0, keepdims=True)
        _hosted_finish(phase, p_in, p_out, p_sems, (h_id == HG_HEADS - 1) & (b_id == B - 1) & (s_id == ns - 1))

    col = lambda base: pl.BlockSpec((None, sb, HG_DIM), lambda h, b, s: (b, ns - 1 - s, base + h))
    out_col = pl.BlockSpec((None, sb, HG_DIM), lambda h, b, s: (b, ns - 1 - s, h))
    dt = jax.ShapeDtypeStruct((B, S, HG_WIDTH), BF16)
    p_ispecs, p_ospecs, p_oshapes, p_alias, p_scratch, p_args = _host_phase(phase, 18, 6)
    res = pl.pallas_call(
        body, name=name,
        grid=(HG_HEADS, B, ns),
        in_specs=[col(2 * HG_HEADS), col(0), col(0),
                  pl.BlockSpec((None, None, nc, HG_DIM, HG_DIM), lambda h, b, s: (b, h, ns - 1 - s, 0, 0)),
                  pl.BlockSpec((None, None, nc, CHUNK, CHUNK), lambda h, b, s: (b, h, ns - 1 - s, 0, 0))]
        + [col(0)] * 8
        + [pl.BlockSpec((DEPTH, HG_DIM), lambda h, b, s: (0, h)),
           pl.BlockSpec((1, HG_DIM), lambda h, b, s: (0, 0)),
           pl.BlockSpec((N_CUM_F, CHUNK), lambda h, b, s: (0, 0)),
           pl.BlockSpec((CHUNK, N_CUM), lambda h, b, s: (0, 0)),
           pl.BlockSpec((len(_LEVELS), CHUNK, CHUNK), lambda h, b, s: (0, 0, 0))] + p_ispecs,
        out_specs=[out_col, out_col, out_col, out_col,
                   pl.BlockSpec((1, HG_DIM), lambda h, b, s: (0, h)),
                   pl.BlockSpec((1, HG_DIM), lambda h, b, s: (0, 0))] + p_ospecs,
        out_shape=[dt, dt, dt, dt, jax.ShapeDtypeStruct((1, HG_WIDTH), F32),
                   jax.ShapeDtypeStruct((1, HG_DIM), F32)] + p_oshapes,
        input_output_aliases=p_alias,
        scratch_shapes=[pltpu.VMEM((HG_DIM, HG_DIM), F32)] + p_scratch,
        compiler_params=_params(("arbitrary", "arbitrary", "arbitrary")),
    )(proj_h, o_h, du, *kept, lb_param, g_head, ts, tst, _level_masks(), *p_args)
    return tuple(res[:6]) + (list(res[6:]),)


def _rope_tables(S):
    half = ATT_DIM // 2
    inv_freq = ROPE_THETA ** (-jnp.arange(half, dtype=F32) / half)
    ang = jnp.arange(S).astype(F32)[:, None] * inv_freq[None, :]
    cos = jnp.cos(ang)
    sin = jnp.sin(ang)
    cos = jnp.concatenate([cos, cos, cos, cos], axis=1)
    sin = jnp.concatenate([-sin, sin, -sin, sin], axis=1)
    return cos, sin


def _attn_common():
    lane = lax.broadcasted_iota(jnp.int32, (1, 2 * ATT_DIM), 1)
    first_half = (lane % ATT_DIM) < (ATT_DIM // 2)
    left = lane < ATT_DIM

    def swap(x):
        return jnp.where(first_half, pltpu.roll(x, 128 - ATT_DIM // 2, 1), pltpu.roll(x, ATT_DIM // 2, 1))

    def rope(x, cos, sin):
        return x * cos + swap(x) * sin

    def rope_bwd(dy, cos, sin):
        return dy * cos + swap(dy * sin)

    def dup(x):
        xs = pltpu.roll(x, ATT_DIM, 1)
        return [jnp.where(left, x, xs), jnp.where(left, xs, x)]

    return left, rope, rope_bwd, dup


GROUP = ATT_HEADS // 2
GROUP_ROWS = GROUP * ATT_BLOCK


def _attn_bias(i):
    r = lax.broadcasted_iota(jnp.int32, (ATT_BLOCK, 2 * ATT_BLOCK), 0)
    c = lax.broadcasted_iota(jnp.int32, (ATT_BLOCK, 2 * ATT_BLOCK), 1)
    ok = (c > r) & (c <= r + ATT_BLOCK) & ((c >= ATT_BLOCK) | (i > 0))
    return jnp.where(ok, 0.0, NEG_INF)


def _stack_heads(pairs, left):
    rows = []
    for x in pairs:
        rows += [jnp.where(left, x, 0.0), jnp.where(left, 0.0, x)]
    return jnp.concatenate(rows, axis=0)


def _unstack_heads(y, left, pp):
    r0 = 2 * pp * ATT_BLOCK
    return jnp.where(left, y[r0:r0 + ATT_BLOCK], y[r0 + ATT_BLOCK:r0 + 2 * ATT_BLOCK])


def _row_sums(x):
    return _dot(x, jnp.ones((x.shape[1], 128), BF16), NN)


def _attn_probs(qs, kd, vd, sink, bias):
    n = range(len(qs))
    rows = qs[0].shape[0]
    s = [(_dot(qs[j], kd[j], NT).reshape(rows // ATT_BLOCK, ATT_BLOCK, 2 * ATT_BLOCK) * ATT_SCALE + bias[None])
         .reshape(rows, 2 * ATT_BLOCK) for j in n]
    m = [jnp.max(jnp.maximum(jnp.maximum(s[j][:, :128], s[j][:, 128:]), sink[j]), axis=-1, keepdims=True) for j in n]
    pu = [jnp.exp(s[j] - m[j]) for j in n]
    es = [jnp.exp(sink[j] - m[j]) for j in n]
    ones = jnp.ones((2 * ATT_BLOCK, 128), BF16)
    ov = [_dot(pu[j], jnp.concatenate([vd[j].astype(BF16), ones], axis=1), NN) for j in n]
    inv = [1.0 / (ov[j][:, 128:] + es[j]) for j in n]
    return ([pu[j] * jnp.concatenate([inv[j], inv[j]], axis=1) for j in n], [es[j] * inv[j] for j in n],
            [ov[j][:, :128] * inv[j] for j in n])


def _sink_rows(sinks_l):
    return jnp.broadcast_to(jnp.repeat(sinks_l, ATT_BLOCK)[:, None], (ATT_HEADS * ATT_BLOCK, 128))


_Z0 = (2 * ATT_WIDTH + 2 * KV_WIDTH - ATT_WIDTH) // 256


def _attn_fwd(proj_a, u, sinks_l, cos, sin, name, phase=None):
    B, S, _ = proj_a.shape
    nb = S // ATT_BLOCK

    def body(*refs):
        ins, (u_ref, p_ref, o_ref, ps_ref, qs_ref), _, p_in, p_out, p_sems = _split_refs(refs, 13, 5, 0, phase)
        q_ref, kvc_ref, kvp_ref, z0, z1, z2, z3, cos_ref, sin_ref, cosp_ref, sinp_ref, sinks_ref, _ = ins
        i = pl.program_id(1)
        _hosted_start(phase, p_in, p_out, p_sems, (pl.program_id(0) == 0) & (i == 0))
        left, rope, _, dup = _attn_common()
        cos_c, sin_c = cos_ref[...], sin_ref[...]
        kvc = kvc_ref[...]
        kvp = kvp_ref[...]
        kw = jnp.concatenate([rope(kvp[:, :KV_WIDTH], cosp_ref[...], sinp_ref[...]),
                              rope(kvc[:, :KV_WIDTH], cos_c, sin_c)], axis=0)
        vw = jnp.concatenate([kvp[:, KV_WIDTH:], kvc[:, KV_WIDTH:]], axis=0)
        kd, vd = dup(kw), dup(vw)
        bias = _attn_bias(i)
        zs = (z0, z1, z2, z3)
        pairs = [range(4 * kvh, 4 * kvh + 4) for kvh in range(2)]
        qs = [_stack_heads([rope(q_ref[:, 128 * pr:128 * (pr + 1)], cos_c, sin_c) for pr in pairs[kvh]], left)
              for kvh in range(2)]
        sink = [sinks_ref[kvh * GROUP_ROWS:(kvh + 1) * GROUP_ROWS, :] for kvh in range(2)]
        p, ps, o = _attn_probs(qs, kd, vd, sink, bias)
        eye = (lax.broadcasted_iota(jnp.int32, (ATT_BLOCK, 128), 0)
               == lax.broadcasted_iota(jnp.int32, (ATT_BLOCK, 128), 1))
        for kvh in range(2):
            p_ref[kvh] = p[kvh].astype(BF16)
            qs_ref[kvh] = qs[kvh].astype(BF16)
            for g in range(GROUP):
                blk = ps[kvh][g * ATT_BLOCK:(g + 1) * ATT_BLOCK, :]
                ps_ref[kvh * GROUP + g:kvh * GROUP + g + 1, :] = jnp.sum(jnp.where(eye, blk, 0.0), axis=0, keepdims=True)
            for pp, pr in enumerate(pairs[kvh]):
                z = zs[pr // 2][:, 128 * (pr % 2):128 * (pr % 2 + 1)]
                o128 = _unstack_heads(o[kvh], left, pp)
                o_ref[:, 128 * pr:128 * (pr + 1)] = o128.astype(BF16)
                u_ref[:, 128 * pr:128 * (pr + 1)] = (o128 * _silu(z)).astype(BF16)
        _hosted_finish(phase, p_in, p_out, p_sems, (pl.program_id(0) == B - 1) & (i == nb - 1))

    rowblk = lambda w, cb: pl.BlockSpec((None, ATT_BLOCK, w), lambda b, i: (b, i, cb))
    tab = pl.BlockSpec((ATT_BLOCK, 128), lambda b, i: (i, 0))
    tabp = pl.BlockSpec((ATT_BLOCK, 128), lambda b, i: (jnp.maximum(i - 1, 0), 0))
    p_ispecs, p_ospecs, p_oshapes, p_alias, p_scratch, p_args = _host_phase(phase, 13, 5)
    res = pl.pallas_call(
        body, name=name,
        grid=(B, nb),
        in_specs=[rowblk(ATT_WIDTH, 0), rowblk(256, 4),
                  pl.BlockSpec((None, ATT_BLOCK, 256), lambda b, i: (b, jnp.maximum(i - 1, 0), 4)),
                  rowblk(256, _Z0), rowblk(256, _Z0 + 1), rowblk(256, _Z0 + 2), rowblk(256, _Z0 + 3),
                  tab, tab, tabp, tabp,
                  pl.BlockSpec((ATT_HEADS * ATT_BLOCK, 128), lambda b, i: (0, 0)),
                  pl.BlockSpec(memory_space=pl.ANY)] + p_ispecs,
        out_specs=[pl.BlockSpec((None, ATT_BLOCK, ATT_WIDTH), lambda b, i: (b, i, 1)),
                   pl.BlockSpec((None, None, 2, GROUP_ROWS, 2 * ATT_BLOCK), lambda b, i: (b, i, 0, 0, 0)),
                   pl.BlockSpec((None, ATT_BLOCK, ATT_WIDTH), lambda b, i: (b, i, 0)),
                   pl.BlockSpec((None, None, ATT_HEADS, 128), lambda b, i: (b, i, 0, 0)),
                   pl.BlockSpec((None, None, 2, GROUP_ROWS, 128), lambda b, i: (b, i, 0, 0, 0))] + p_ospecs,
        out_shape=[jax.ShapeDtypeStruct(u.shape, BF16),
                   jax.ShapeDtypeStruct((B, nb, 2, GROUP_ROWS, 2 * ATT_BLOCK), BF16),
                   jax.ShapeDtypeStruct((B, S, ATT_WIDTH), BF16),
                   jax.ShapeDtypeStruct((B, nb, ATT_HEADS, 128), F32),
                   jax.ShapeDtypeStruct((B, nb, 2, GROUP_ROWS, 128), BF16)] + p_oshapes,
        input_output_aliases={12: 0, **p_alias},
        scratch_shapes=p_scratch,
        compiler_params=_params(("arbitrary", "arbitrary")),
    )(proj_a, proj_a, proj_a, proj_a, proj_a, proj_a, proj_a, cos, sin, cos, sin, sinks_l, u, *p_args)
    return res[0], tuple(res[1:5]), list(res[5:])


def _attn_bwd(proj_a, du, kept, cos, sin, name, phase=None):
    B, S, _ = proj_a.shape
    nb = S // ATT_BLOCK
    p_kept, o_kept, ps_kept, qs_kept = kept

    def body(*refs):
        ins, outs, (carry, sk_acc), p_in, p_out, p_sems = _split_refs(refs, 15, 4, 2, phase)
        (qs_ref, kvc_ref, kvp_ref, z0, z1, z2, z3, du_ref, cos_ref, sin_ref, cosp_ref, sinp_ref,
         p_ref, o_ref, ps_ref) = ins
        dq_ref, dkv_ref, dz_ref, dsk_ref = outs
        b_id, i = pl.program_id(0), pl.program_id(1)
        _hosted_start(phase, p_in, p_out, p_sems, (b_id == 0) & (i == 0))

        @pl.when((b_id == 0) & (i == 0))
        def _():
            sk_acc[...] = jnp.zeros_like(sk_acc)

        @pl.when(i == 0)
        def _():
            carry[...] = jnp.zeros_like(carry)

        @pl.when(i < nb)
        def _():
            left, rope, rope_bwd, dup = _attn_common()
            cos_c, sin_c = cos_ref[...], sin_ref[...]
            cos_p, sin_p = cosp_ref[...], sinp_ref[...]
            kvc = kvc_ref[...]
            kvp = kvp_ref[...]
            kw = jnp.concatenate([rope(kvp[:, :KV_WIDTH], cos_p, sin_p), rope(kvc[:, :KV_WIDTH], cos_c, sin_c)], axis=0)
            vw = jnp.concatenate([kvp[:, KV_WIDTH:], kvc[:, KV_WIDTH:]], axis=0)
            kd, vd = dup(kw), dup(vw)
            zs = (z0, z1, z2, z3)
            units = [(kvh, hf) for kvh in range(2) for hf in range(2)]
            half = GROUP_ROWS // 2
            pairs = [range(4 * kvh + 2 * hf, 4 * kvh + 2 * hf + 2) for kvh, hf in units]
            ku = [kd[kvh] for kvh, _ in units]
            vu = [vd[kvh] for kvh, _ in units]
            ps_all = ps_ref[...]
            head_row = lax.broadcasted_iota(jnp.int32, (ATT_HEADS, 128), 0)
            eye = (lax.broadcasted_iota(jnp.int32, (ATT_BLOCK, 128), 0)
                   == lax.broadcasted_iota(jnp.int32, (ATT_BLOCK, 128), 1))

            def first(j):
                kvh, hf = units[j]
                p = p_ref[kvh, hf * half:(hf + 1) * half, :]
                parts = []
                for pr in pairs[j]:
                    cols = slice(128 * pr, 128 * (pr + 1))
                    sg, sg_grad = _silu_and_grad(zs[pr // 2][:, 128 * (pr % 2):128 * (pr % 2 + 1)])
                    du128 = du_ref[:, cols]
                    dz_ref[:, cols] = (du128 * o_ref[:, cols].astype(F32) * sg_grad).astype(BF16)
                    parts.append(du128 * sg)
                dos = _stack_heads(parts, left)
                dp = _dot(dos, vu[j], NT)
                delta = _row_sums(p.astype(F32) * dp)
                ds = (p.astype(F32) * (dp - jnp.concatenate([delta, delta], axis=1)) * ATT_SCALE).astype(BF16)
                sk = jnp.zeros((ATT_HEADS, 128), F32)
                for hh in range(4):
                    hd = kvh * GROUP + 4 * hf + hh
                    drow = jnp.sum(jnp.where(eye, delta[hh * ATT_BLOCK:(hh + 1) * ATT_BLOCK, :], 0.0), axis=0,
                                   keepdims=True)
                    sk = sk - jnp.where(head_row == hd, ps_all * drow, 0.0)
                sk_acc[...] += sk
                return ds, p, dos.astype(BF16), qs_ref[kvh, hf * half:(hf + 1) * half, :]

            def second(j, ds, p, dos, qs):
                dqs = _dot(ds, ku[j], NN)
                for pp, pr in enumerate(pairs[j]):
                    dq_ref[:, 128 * pr:128 * (pr + 1)] = rope_bwd(_unstack_heads(dqs, left, pp),
                                                                  cos_c, sin_c).astype(BF16)
                return _dot(ds, qs, TN), _dot(p, dos, TN)

            got, dku, dvu = {}, [None] * len(units), [None] * len(units)
            for j in range(len(units) + 1):
                if j < len(units):
                    got[j] = first(j)
                if j >= 1:
                    dku[j - 1], dvu[j - 1] = second(j - 1, *got.pop(j - 1))
            dkd = [dku[0] + dku[1], dku[2] + dku[3]]
            dvd = [dvu[0] + dvu[1], dvu[2] + dvu[3]]
            fold = lambda pr: jnp.where(left, pr[0] + pltpu.roll(pr[0], ATT_DIM, 1), pr[1] + pltpu.roll(pr[1], ATT_DIM, 1))
            dkw = fold(dkd)
            dvw = fold(dvd)
            prev = jnp.concatenate([rope_bwd(dkw[:ATT_BLOCK], cos_p, sin_p), dvw[:ATT_BLOCK]], axis=1)
            cur = jnp.concatenate([rope_bwd(dkw[ATT_BLOCK:], cos_c, sin_c), dvw[ATT_BLOCK:]], axis=1)
            dkv_ref[...] = (carry[...] + prev).astype(BF16)
            carry[...] = cur

        @pl.when(i == nb)
        def _():
            dkv_ref[...] = carry[...].astype(BF16)

        @pl.when((b_id == B - 1) & (i == nb))
        def _():
            diag = (lax.broadcasted_iota(jnp.int32, (ATT_HEADS, 128), 0)
                    == lax.broadcasted_iota(jnp.int32, (ATT_HEADS, 128), 1))
            tot = jnp.sum(sk_acc[...], axis=1, keepdims=True)
            dsk_ref[...] = jnp.sum(jnp.where(diag, tot, 0.0), axis=0, keepdims=True)

        _hosted_finish(phase, p_in, p_out, p_sems, (b_id == B - 1) & (i == nb))

    cl = lambda i: jnp.minimum(i, nb - 1)
    pv = lambda i: jnp.maximum(jnp.minimum(i, nb - 1) - 1, 0)
    rowblk = lambda w, cb: pl.BlockSpec((None, ATT_BLOCK, w), lambda b, i: (b, cl(i), cb))
    tab = pl.BlockSpec((ATT_BLOCK, 128), lambda b, i: (cl(i), 0))
    tabp = pl.BlockSpec((ATT_BLOCK, 128), lambda b, i: (pv(i), 0))
    p_ispecs, p_ospecs, p_oshapes, p_alias, p_scratch, p_args = _host_phase(phase, 15, 4)
    res = pl.pallas_call(
        body, name=name,
        grid=(B, nb + 1),
        in_specs=[pl.BlockSpec((None, None, 2, GROUP_ROWS, 128), lambda b, i: (b, cl(i), 0, 0, 0)), rowblk(256, 4),
                  pl.BlockSpec((None, ATT_BLOCK, 256), lambda b, i: (b, pv(i), 4)),
                  rowblk(256, _Z0), rowblk(256, _Z0 + 1), rowblk(256, _Z0 + 2), rowblk(256, _Z0 + 3),
                  rowblk(ATT_WIDTH, 1),
                  tab, tab, tabp, tabp,
                  pl.BlockSpec((None, None, 2, GROUP_ROWS, 2 * ATT_BLOCK), lambda b, i: (b, cl(i), 0, 0, 0)),
                  rowblk(ATT_WIDTH, 0),
                  pl.BlockSpec((None, None, ATT_HEADS, 128), lambda b, i: (b, cl(i), 0, 0))] + p_ispecs,
        out_specs=[rowblk(ATT_WIDTH, 0),
                   pl.BlockSpec((None, ATT_BLOCK, 256), lambda b, i: (b, jnp.maximum(i - 1, 0), 0)),
                   rowblk(ATT_WIDTH, 0),
                   pl.BlockSpec((1, 128), lambda b, i: (0, 0))] + p_ospecs,
        out_shape=[jax.ShapeDtypeStruct((B, S, ATT_WIDTH), BF16), jax.ShapeDtypeStruct((B, S, 256), BF16),
                   jax.ShapeDtypeStruct((B, S, ATT_WIDTH), BF16), jax.ShapeDtypeStruct((1, 128), F32)] + p_oshapes,
        input_output_aliases=p_alias,
        scratch_shapes=[pltpu.VMEM((ATT_BLOCK, 256), F32), pltpu.VMEM((ATT_HEADS, 128), F32)] + p_scratch,
        compiler_params=_params(("arbitrary", "arbitrary")),
    )(qs_kept, proj_a, proj_a, proj_a, proj_a, proj_a, proj_a, du, cos, sin, cos, sin, p_kept, o_kept, ps_kept, *p_args)
    return tuple(res[:4]) + (list(res[4:]),)


def _outproj_fwd(u2, w_out, x2, g_post, target2, name):
    T, D = x2.shape
    tm = _pick(T, (512, 256, 128))
    last = target2 is not None

    def body(u_ref, w_ref, x_ref, g_ref, *rest):
        y = lax.dot_general(u_ref[...], w_ref[...], (NN, ((), ())), preferred_element_type=F32)
        r = lax.rsqrt(jnp.mean(y * y, axis=-1, keepdims=True) + NORM_EPS)
        xn = x_ref[...] + (y * r) * g_ref[...]
        if last:
            t_ref, y_ref, dx_ref, loss_ref = rest
            err = xn - t_ref[...]
            dx_ref[...] = err * (1.0 / D)
            sq = err * err
            acc = sq[:, 0:128]
            for kk in range(1, D // 128):
                acc = acc + sq[:, 128 * kk:128 * (kk + 1)]
            part = jnp.sum(acc.reshape(tm // 8, 8, 128), axis=0) * (0.5 / D)

            @pl.when(pl.program_id(0) == 0)
            def _():
                loss_ref[...] = jnp.zeros_like(loss_ref)

            loss_ref[...] += part
        else:
            y_ref, xn_ref = rest
            xn_ref[...] = xn
        y_ref[...] = y

    row = pl.BlockSpec((tm, D), lambda i: (i, 0))
    in_specs = [pl.BlockSpec((tm, MIX_WIDTH), lambda i: (i, 0)),
                pl.BlockSpec((MIX_WIDTH, D), lambda i: (0, 0)), row,
                pl.BlockSpec((1, D), lambda i: (0, 0))]
    args = [u2, w_out, x2, g_post]
    out_specs = [row, row]
    out_shape = [jax.ShapeDtypeStruct((T, D), F32), jax.ShapeDtypeStruct((T, D), F32)]
    if last:
        in_specs.append(row)
        args.append(target2)
        out_specs.append(pl.BlockSpec((8, 128), lambda i: (0, 0)))
        out_shape.append(jax.ShapeDtypeStruct((8, 128), F32))
    return pl.pallas_call(
        body, name=name, grid=(T // tm,), in_specs=in_specs, out_specs=out_specs, out_shape=out_shape,
        compiler_params=_params(("arbitrary",)),
    )(*args)


def _outproj_bwd(dxn2, y2, g_post, w_out, u2, name):
    T, D = y2.shape
    N = w_out.shape[0]
    tm = _pick(T, (512, 256, 128))
    nt = T // tm

    def body(dx_ref, y_ref, g_ref, w_ref, u_ref, dg_ref, du_ref, dw_ref, acc, wacc):
        i = pl.program_id(0)

        @pl.when(i == 0)
        def _():
            acc[...] = jnp.zeros_like(acc)
            wacc[...] = jnp.zeros_like(wacc)

        y = y_ref[...]
        dxn = dx_ref[...]
        r = lax.rsqrt(jnp.mean(y * y, axis=-1, keepdims=True) + NORM_EPS)
        n = y * r
        dn = dxn * g_ref[...]
        dy = (r * (dn - n * jnp.mean(dn * n, axis=-1, keepdims=True))).astype(BF16)
        du_ref[...] = lax.dot_general(dy, w_ref[...], (NT, ((), ())), preferred_element_type=F32)
        wacc[...] += lax.dot_general(u_ref[...], dy, (TN, ((), ())), preferred_element_type=F32)
        acc[...] += jnp.sum((dxn * n).reshape(tm // 8, 8, D), axis=0)

        @pl.when(i == nt - 1)
        def _():
            dg_ref[...] = jnp.sum(acc[...], axis=0, keepdims=True)
            dw_ref[...] = wacc[...].astype(BF16)

    row = pl.BlockSpec((tm, D), lambda i: (i, 0))
    wide = pl.BlockSpec((tm, N), lambda i: (i, 0))
    vec = pl.BlockSpec((1, D), lambda i: (0, 0))
    whole = pl.BlockSpec((N, D), lambda i: (0, 0))
    return pl.pallas_call(
        body, name=name, grid=(nt,),
        in_specs=[row, row, vec, pl.BlockSpec((N, D), lambda i: (0, 0), pipeline_mode=pl.Buffered(1)), wide],
        out_specs=[vec, wide, whole],
        out_shape=[jax.ShapeDtypeStruct((1, D), F32), jax.ShapeDtypeStruct((T, N), F32),
                   jax.ShapeDtypeStruct((N, D), BF16)],
        scratch_shapes=[pltpu.VMEM((8, D), F32), pltpu.VMEM((N, D), F32)],
        compiler_params=_params(("arbitrary",)),
    )(dxn2, y2, g_post, w_out, u2)


def _inproj_bwd(pieces, w_t, x2, dxn2, g_pre, name, phase=None):
    T, D = x2.shape
    widths = [p.shape[1] for p in pieces]
    offs = [sum(widths[:i]) for i in range(len(pieces))]
    n_p = len(pieces)
    tm = _pick(T, (256, 128))
    nt = T // tm

    def body(*refs):
        ins, (dx_ref, dg_ref), (acc,), p_in, p_out, p_sems = _split_refs(refs, n_p + 4, 2, 1, phase)
        w_ref, x_ref, dxn_ref, g_ref = ins[n_p:]
        i = pl.program_id(0)
        _hosted_start(phase, p_in, p_out, p_sems, i == 0)

        @pl.when(i == 0)
        def _():
            acc[...] = jnp.zeros_like(acc)

        dh = jnp.zeros((tm, D), F32)
        for p in range(n_p):
            dh = dh + lax.dot_general(ins[p][...], w_ref[offs[p]:offs[p] + widths[p], :], (NN, ((), ())),
                                      preferred_element_type=F32)
        x = x_ref[...]
        r = lax.rsqrt(jnp.mean(x * x, axis=-1, keepdims=True) + NORM_EPS)
        n = x * r
        dn = dh * g_ref[...]
        dx_ref[...] = dxn_ref[...] + r * (dn - n * jnp.mean(dn * n, axis=-1, keepdims=True))
        acc[...] += jnp.sum((dh * n).reshape(tm // 8, 8, D), axis=0)

        @pl.when(i == nt - 1)
        def _():
            dg_ref[...] = jnp.sum(acc[...], axis=0, keepdims=True)

        _hosted_finish(phase, p_in, p_out, p_sems, i == nt - 1)

    row = pl.BlockSpec((tm, D), lambda i: (i, 0))
    vec = pl.BlockSpec((1, D), lambda i: (0, 0))
    p_ispecs, p_ospecs, p_oshapes, p_alias, p_scratch, p_args = _host_phase(phase, n_p + 4, 2)
    res = pl.pallas_call(
        body, name=name, grid=(nt,),
        in_specs=[pl.BlockSpec((tm, w), lambda i: (i, 0)) for w in widths]
        + [pl.BlockSpec((sum(widths), D), lambda i: (0, 0), pipeline_mode=pl.Buffered(1)), row, row, vec] + p_ispecs,
        out_specs=[row, vec] + p_ospecs,
        out_shape=[jax.ShapeDtypeStruct((T, D), F32), jax.ShapeDtypeStruct((1, D), F32)] + p_oshapes,
        input_output_aliases=p_alias,
        scratch_shapes=[pltpu.VMEM((8, D), F32)] + p_scratch,
        compiler_params=_params(("arbitrary",)),
    )(*pieces, w_t, x2, dxn2, g_pre, *p_args)
    return res[0], res[1], list(res[2:])


def _step(x, target, g_pre, g_post, lb_param, g_head, sinks, shards=None, full=None):
    B, S, D = x.shape
    T = B * S
    dist = shards is not None
    first, last = 0, DEPTH - 1
    if dist:
        a_loc, b_loc = shards
        ra, rb = a_loc.shape[1], b_loc.shape[1]
        side = _own_side_blocks()
        placed = lambda loc, nm: _place_own(loc, side, "place_" + nm)
        w_in0 = _gather_one_call(a_loc[0], placed(a_loc[0], "in0"), "gather_in0")
        w_in, w_out = [w_in0, None], [None, None]
    else:
        w_in, w_out = list(full[0]), list(full[1])
    cos, sin = _rope_tables(S)
    saved = []
    xs = x
    loss_part = None
    dxn = None
    for l in range(DEPTH):
        x2 = xs.reshape(T, D)
        proj_h, proj_a, h = _inproj(x2, g_pre[l:l + 1], w_in[l], f"inproj{l}")
        proj_h = proj_h.reshape(B, S, N_H)
        proj_a = proj_a.reshape(B, S, N_A)
        phase = None
        if dist and l == first:
            phase = _gather_ici_phase([a_loc[1], b_loc[0]], [placed(a_loc[1], "in1"), placed(b_loc[0], "out0")])
        if dist and l == last:
            phase = _gather_d2d_phase([w_out1_part], [rb])
        o_h, u, states, got = _hgrn_fwd(proj_h, MIX_WIDTH, lb_param, g_head[l:l + 1], l, f"hgrn_fwd{l}", phase)
        phase = None
        if dist and l == first:
            phase = _merge_phases(_gather_d2d_phase(got, [ra, rb]),
                                  _gather_ici_phase([b_loc[1]], [placed(b_loc[1], "out1")]))
        if dist and l == last:
            w_out[1] = got[0]
        u, kept_a, got = _attn_fwd(proj_a, u, _sink_rows(sinks[l]), cos, sin, f"attn_fwd{l}", phase)
        if dist and l == first:
            w_in[1], w_out[0], w_out1_part = got
        u2 = u.reshape(T, MIX_WIDTH)
        if l < last:
            y, xn = _outproj_fwd(u2, w_out[l], x2, g_post[l:l + 1], None, f"outproj{l}")
            xn = xn.reshape(B, S, D)
        else:
            y, dxn, loss_part = _outproj_fwd(u2, w_out[l], x2, g_post[l:l + 1], target.reshape(T, D), f"outproj{l}")
            xn = None
        saved.append((x2, h, proj_h, proj_a, o_h, u2, states, kept_a, y))
        xs = xn

    dw_in, dw_out = [None] * DEPTH, [None] * DEPTH
    dg_pre, dg_post, dlb, dg_head, dsinks = [], [], [], [], []
    for l in reversed(range(DEPTH)):
        x2, h, proj_h, proj_a, o_h, u2, states, kept_a, y = saved[l]
        dgp, du, dw_out[l] = _outproj_bwd(dxn, y, g_post[l:l + 1], w_out[l], u2, f"outproj_bwd{l}")
        du = du.reshape(B, S, MIX_WIDTH)
        phase = None
        if dist:
            phase = _reduce_d2d_phase([dw_out[l]], [rb])
            if l == first:
                phase = _merge_phases(_reduce_ici_phase([part_in1]), phase)
        dqh, dfh, dih, dzh, dlb_l, dgh, got = _hgrn_bwd(
            proj_h, o_h, du, states, lb_param, g_head[l:l + 1], l, f"hgrn_bwd{l}", phase)
        if dist:
            if l == first:
                sum_in = _chip_sum(part_in1, got[0], "chip_sum_in1", 1)
            part_out = _pair_sum(dw_out[l], got[-1], side, f"pair_sum_out{l}")
        dqa, dkv, dza, dsk, got = _attn_bwd(proj_a, du, kept_a, cos, sin, f"attn_bwd{l}",
                                            _reduce_ici_phase([part_out]) if dist else None)
        if dist:
            sum_out = _chip_sum(part_out, got[0], f"chip_sum_out{l}", l, None if l == last else sum_out)
        dproj = [p.reshape(T, p.shape[-1]) for p in (dqh, dfh, dih, dzh, dqa, dkv, dza)]
        dw_in[l] = _mm_tn(dproj, h, f"wgrad_in{l}")
        phase = None
        if dist and l == last:
            phase = _reduce_d2d_phase([dw_in[l]], [ra])
        if dist and l == first:
            got = _run_phase(_reduce_d2d_phase([dw_in[l]], [ra]), "reduce_in0_d2d")
            part_in0 = _pair_sum(dw_in[l], got[0], side, "pair_sum_in0")
            phase = _reduce_ici_phase([part_in0])
        dxn, dgpre, got = _inproj_bwd(dproj, w_in[l], x2, dxn, g_pre[l:l + 1], f"inproj_bwd{l}", phase)
        if dist and l == last:
            part_in1 = _pair_sum(dw_in[l], got[0], side, "pair_sum_in1")
        if dist and l == first:
            sum_in = _chip_sum(part_in0, got[0], "chip_sum_in0", 0, sum_in)
        dg_pre.append(dgpre)
        dg_post.append(dgp)
        dlb.append(dlb_l)
        dg_head.append(dgh)
        dsinks.append(dsk)
    rev = lambda lst: jnp.concatenate(lst[::-1], axis=0)
    if not dist:
        sum_in, sum_out = jnp.stack(dw_in), jnp.stack(dw_out)
    return (loss_part, dxn.reshape(B, S, D), sum_in, sum_out,
            rev(dg_pre), rev(dg_post), rev(dlb), rev(dg_head), rev(dsinks))


def _me_and_peers():
    x, y, c = lax.axis_index("x"), lax.axis_index("y"), lax.axis_index("c")
    me = 4 * x + 2 * y + c
    peers = []
    for k in range(1, N_DEV):
        px = 1 - x if k & 4 else x
        py = 1 - y if k & 2 else y
        pc = 1 - c if k & 1 else c
        peers.append(((px, py, pc), 4 * px + 2 * py + pc))
    return me, peers


class _Phase:
    def __init__(self, arrays, out_shapes, aliases, n_send, build):
        self.arrays, self.out_shapes, self.aliases = list(arrays), list(out_shapes), dict(aliases)
        self.n_send, self.build = n_send, build

    def scratch(self):
        return [pltpu.SemaphoreType.DMA((self.n_send,)), pltpu.SemaphoreType.DMA((self.n_send,))]

    def _copies(self, in_refs, out_refs, sems, arrivals):
        send_sems, recv_sems = sems
        sends, recvs = self.build(in_refs, out_refs)
        assert len(sends) == self.n_send == len(recvs)
        out = [pltpu.make_async_remote_copy(src_ref=s, dst_ref=d, send_sem=send_sems.at[i], recv_sem=recv_sems.at[i],
                                            device_id=dev, device_id_type=MESH) for i, (s, d, dev) in enumerate(sends)]
        inc = [pltpu.make_async_remote_copy(src_ref=s, dst_ref=r, send_sem=send_sems.at[i], recv_sem=recv_sems.at[i],
                                            device_id=dev, device_id_type=MESH)
               for i, ((s, _, dev), r) in enumerate(zip(sends, recvs))] if arrivals else []
        return out, inc

    def start(self, in_refs, out_refs, sems):
        out, _ = self._copies(in_refs, out_refs, sems, False)
        for cp in out:
            cp.start()

    def finish(self, in_refs, out_refs, sems):
        out, inc = self._copies(in_refs, out_refs, sems, True)
        for cp in inc:
            cp.wait_recv()
        for cp in out:
            cp.wait_send()


_ANY = pl.BlockSpec(memory_space=pl.ANY)


def _host_phase(phase, n_in, n_out):
    if phase is None:
        return [], [], [], {}, [], []
    aliases = {n_in + i: n_out + o for i, o in phase.aliases.items()}
    return ([_ANY] * len(phase.arrays), [_ANY] * len(phase.out_shapes), phase.out_shapes, aliases, phase.scratch(),
            phase.arrays)


def _split_refs(refs, n_in, n_out, n_scr, phase):
    pi = len(phase.arrays) if phase else 0
    po = len(phase.out_shapes) if phase else 0
    a = n_in + pi
    b = a + n_out + po
    return (refs[:n_in], refs[a:a + n_out], refs[b:b + n_scr], refs[n_in:a], refs[a + n_out:b], refs[b + n_scr:])


def _hosted_start(phase, p_in, p_out, p_sems, first):
    if phase is not None:
        @pl.when(first)
        def _():
            phase.start(p_in, p_out, p_sems)


def _hosted_finish(phase, p_in, p_out, p_sems, last):
    if phase is not None:
        @pl.when(last)
        def _():
            phase.finish(p_in, p_out, p_sems)


def _run_phase(phase, name):
    n_in, n_out = len(phase.arrays), len(phase.out_shapes)

    def body(*refs):
        phase.start(refs[:n_in], refs[n_in:n_in + n_out], refs[n_in + n_out:])
        phase.finish(refs[:n_in], refs[n_in:n_in + n_out], refs[n_in + n_out:])

    return pl.pallas_call(
        body, name=name, in_specs=[_ANY] * n_in, out_specs=[_ANY] * n_out,
        out_shape=phase.out_shapes, input_output_aliases=phase.aliases, scratch_shapes=phase.scratch(),
        compiler_params=pltpu.CompilerParams(has_side_effects=True),
    )(*phase.arrays)


def _gather_one_call(loc, full, name):
    r = loc.shape[0]
    half = r // 2

    def body(loc_ref, full_in, full_ref, send_sems, recv_sems):
        del full_in
        c, (own, xn, yn, dg), num = _mesh_place()
        me, sib = num(own, c), (*own, 1 - c)

        def blk(dev, part=None):
            start, n = (dev * r, r) if part is None else (dev * r + part * half, half)
            return full_ref.at[pl.ds(pl.multiple_of(start, 16), n), :]

        def copy(k, src, dev, to, part=None):
            return pltpu.make_async_remote_copy(src_ref=src, dst_ref=blk(dev, part),
                                                send_sem=send_sems.at[k], recv_sem=recv_sems.at[k],
                                                device_id=to, device_id_type=MESH)

        def landed(k, dev, part=None):
            copy(k, blk(dev, part), dev, sib, part).wait_recv()

        sent = []

        def start(*cps):
            for cp in cps:
                cp.start()
                sent.append(cp)

        xs, ys, ds = num(xn, c), num(yn, c), num(dg, c)
        start(copy(0, loc_ref, me, sib), copy(1, loc_ref, me, (*xn, c)), copy(2, loc_ref, me, (*yn, c)))
        landed(1, xs)
        start(copy(3, blk(xs, 0), xs, (*yn, c), 0), copy(5, blk(xs), xs, sib))
        landed(2, ys)
        start(copy(4, blk(ys, 1), ys, (*xn, c), 1), copy(6, blk(ys), ys, sib))
        landed(3, ds, 0)
        landed(4, ds, 1)
        start(copy(7, blk(ds), ds, sib))
        landed(0, num(own, 1 - c))
        for k, ch in ((5, xn), (6, yn), (7, dg)):
            landed(k, num(ch, 1 - c))
        for cp in sent:
            cp.wait_send()

    assert half % 16 == 0
    return pl.pallas_call(
        body, name=name, in_specs=[_ANY, _ANY], out_specs=_ANY,
        out_shape=jax.ShapeDtypeStruct(full.shape, full.dtype), input_output_aliases={1: 0},
        scratch_shapes=[pltpu.SemaphoreType.DMA((8,)), pltpu.SemaphoreType.DMA((8,))],
        compiler_params=pltpu.CompilerParams(has_side_effects=True),
    )(loc, full)


def _merge_phases(a, b):
    n_in, n_out = len(a.arrays), len(a.out_shapes)
    aliases = dict(a.aliases)
    aliases.update({n_in + i: n_out + o for i, o in b.aliases.items()})

    def build(ins, outs):
        sa, ra = a.build(ins[:n_in], outs[:n_out])
        sb, rb = b.build(ins[n_in:], outs[n_out:])
        return sa + sb, ra + rb

    return _Phase(a.arrays + b.arrays, a.out_shapes + b.out_shapes, aliases, a.n_send + b.n_send, build)


def _mesh_place():
    x, y, c = lax.axis_index("x"), lax.axis_index("y"), lax.axis_index("c")
    chips = [(x, y), (1 - x, y), (x, 1 - y), (1 - x, 1 - y)]
    num = lambda chip, core: 4 * chip[0] + 2 * chip[1] + core
    return c, chips, num


def _own_side_blocks():
    c, chips, num = _mesh_place()
    return jnp.stack([num(ch, c) for ch in chips]).astype(jnp.int32)


def _rows(ref, r, dev):
    return ref.at[pl.ds(pl.multiple_of(dev * r, 16), r), :]


def _place_own(loc, blocks, name):
    r, D = loc.shape
    tr = _pick(r, (400, 256, 200, 128, 64, 16))

    def body(idx_ref, l_ref, o_ref):
        del idx_ref
        o_ref[...] = l_ref[...]

    return pl.pallas_call(
        body, name=name,
        grid_spec=pltpu.PrefetchScalarGridSpec(
            num_scalar_prefetch=1, grid=(r // tr,),
            in_specs=[pl.BlockSpec((tr, D), lambda i, idx: (i, 0))],
            out_specs=pl.BlockSpec((tr, D), lambda i, idx: (idx[0] * (r // tr) + i, 0))),
        out_shape=jax.ShapeDtypeStruct((N_DEV * r, D), loc.dtype),
        compiler_params=_params(("arbitrary",)),
    )(blocks, loc)


def _gather_ici_phase(locs, fulls):
    rs = [a.shape[0] for a in locs]
    n = len(locs)

    def build(ins, outs):
        c, chips, num = _mesh_place()
        me = num(chips[0], c)
        targets = [((*chips[0], 1 - c), num(chips[0], 1 - c))] + [((*ch, c), num(ch, c)) for ch in chips[1:]]
        sends, recvs = [], []
        for dev, dnum in targets:
            for i, r in enumerate(rs):
                sends.append((ins[i], _rows(outs[i], r, me), dev))
                recvs.append(_rows(outs[i], r, dnum))
        return sends, recvs

    shapes = [jax.ShapeDtypeStruct(a.shape, a.dtype) for a in fulls]
    return _Phase(list(locs) + list(fulls), shapes, {n + i: i for i in range(n)}, 4 * n, build)


def _gather_d2d_phase(fulls, rs):
    def build(ins, outs):
        c, chips, num = _mesh_place()
        sib = (*chips[0], 1 - c)
        sends, recvs = [], []
        for ch in chips[1:]:
            for i, r in enumerate(rs):
                blk = _rows(outs[i], r, num(ch, c))
                sends.append((blk, blk, sib))
                recvs.append(_rows(outs[i], r, num(ch, 1 - c)))
        return sends, recvs

    shapes = [jax.ShapeDtypeStruct(a.shape, a.dtype) for a in fulls]
    return _Phase(fulls, shapes, {i: i for i in range(len(fulls))}, 3 * len(fulls), build)


def _reduce_d2d_phase(grads, rs):
    def build(ins, outs):
        c, chips, num = _mesh_place()
        sib = (*chips[0], 1 - c)
        sends, recvs = [], []
        for j, ch in enumerate(chips):
            for i, r in enumerate(rs):
                sends.append((_rows(ins[i], r, num(ch, 1 - c)), outs[i].at[j], sib))
                recvs.append(outs[i].at[j])
        return sends, recvs

    shapes = [jax.ShapeDtypeStruct((4, r, g.shape[1]), g.dtype) for g, r in zip(grads, rs)]
    return _Phase(grads, shapes, {}, 4 * len(grads), build)


def _reduce_ici_phase(parts):
    def build(ins, outs):
        c, chips, _ = _mesh_place()
        sends, recvs = [], []
        for t in range(1, 4):
            for i in range(len(parts)):
                sends.append((ins[i].at[t], outs[i].at[t - 1], (*chips[t], c)))
                recvs.append(outs[i].at[t - 1])
        return sends, recvs

    shapes = [jax.ShapeDtypeStruct((3,) + p.shape[1:], p.dtype) for p in parts]
    return _Phase(parts, shapes, {}, 3 * len(parts), build)


def _pair_sum(g, got, blocks, name):
    n, r, D = got.shape
    tr = _pick(r, (800, 400, 256, 200, 128, 64, 16))

    def body(idx_ref, g_ref, r_ref, o_ref):
        del idx_ref
        o_ref[...] = (g_ref[...].astype(F32) + r_ref[...].astype(F32)).astype(o_ref.dtype)

    blk = pl.BlockSpec((None, tr, D), lambda j, i, idx: (j, i, 0))
    return pl.pallas_call(
        body, name=name,
        grid_spec=pltpu.PrefetchScalarGridSpec(
            num_scalar_prefetch=1, grid=(n, r // tr),
            in_specs=[pl.BlockSpec((tr, D), lambda j, i, idx: (idx[j] * (r // tr) + i, 0)), blk],
            out_specs=blk),
        out_shape=jax.ShapeDtypeStruct(got.shape, got.dtype),
        compiler_params=_params(("arbitrary", "arbitrary")),
    )(blocks, g, got)


def _chip_sum(p, r, name, layer, into=None):
    _, R, D = p.shape
    tr = _pick(R, (800, 400, 256, 200, 128, 64, 16))

    def body(p_ref, r_ref, *rest):
        acc = p_ref[...].astype(F32)
        for t in range(3):
            acc = acc + r_ref[t].astype(F32)
        rest[-1][...] = acc

    args = [p, r] + ([] if into is None else [into])
    return pl.pallas_call(
        body, name=name, grid=(R // tr,),
        in_specs=[pl.BlockSpec((None, tr, D), lambda i: (0, i, 0)), pl.BlockSpec((3, tr, D), lambda i: (0, i, 0))]
        + ([] if into is None else [_ANY]),
        out_specs=pl.BlockSpec((None, tr, D), lambda i: (layer, i, 0)),
        out_shape=jax.ShapeDtypeStruct((DEPTH, R, D), F32),
        input_output_aliases={} if into is None else {2: 0},
        compiler_params=_params(("parallel",)))(*args)


def _allreduce_small(vec):
    R, C = vec.shape

    def body(v_ref, o_ref, buf, send_sems, recv_sems):
        me, peers = _me_and_peers()
        buf[me] = v_ref[...]
        sends = []
        for k, (pid, _) in enumerate(peers):
            cp = pltpu.make_async_remote_copy(src_ref=v_ref, dst_ref=buf.at[me], send_sem=send_sems.at[k],
                                              recv_sem=recv_sems.at[k], device_id=pid, device_id_type=MESH)
            cp.start()
            sends.append(cp)
        for k, (pid, pnum) in enumerate(peers):
            pltpu.make_async_remote_copy(src_ref=v_ref, dst_ref=buf.at[pnum], send_sem=send_sems.at[k],
                                         recv_sem=recv_sems.at[k], device_id=pid, device_id_type=MESH).wait_recv()
        for cp in sends:
            cp.wait_send()
        acc = buf[0]
        for d in range(1, N_DEV):
            acc = acc + buf[d]
        o_ref[...] = acc

    vm = pl.BlockSpec(memory_space=pltpu.VMEM)
    return pl.pallas_call(
        body, name="allreduce_small",
        in_specs=[vm], out_specs=vm,
        out_shape=jax.ShapeDtypeStruct((R, C), F32),
        scratch_shapes=[pltpu.VMEM((N_DEV, R, C), F32), pltpu.SemaphoreType.DMA((N_DEV - 1,)),
                        pltpu.SemaphoreType.DMA((N_DEV - 1,))],
        compiler_params=pltpu.CompilerParams(has_side_effects=True),
    )(vec)


def _adamw(w, g, m, v, name):
    R, C = w.shape
    tr = _pick(R, (512, 400, 256, 128, 64, 32, 16, 8)) if R >= 8 else R
    c1 = 1.0 - ADAM_B1 ** ADAM_STEP
    c2 = 1.0 - ADAM_B2 ** ADAM_STEP

    def body(w_ref, g_ref, m_ref, v_ref, d_ref, mo_ref, vo_ref):
        gg = g_ref[...]
        mn = ADAM_B1 * m_ref[...] + (1.0 - ADAM_B1) * gg
        vn = ADAM_B2 * v_ref[...] + (1.0 - ADAM_B2) * (gg * gg)
        d_ref[...] = -ADAM_LR * ((mn / c1) / (jnp.sqrt(vn / c2) + ADAM_EPS) + ADAM_WD * w_ref[...])
        mo_ref[...] = mn
        vo_ref[...] = vn

    blk = pl.BlockSpec((tr, C), lambda i: (i, 0))
    sh = jax.ShapeDtypeStruct((R, C), F32)
    return pl.pallas_call(
        body, name=name, grid=(R // tr,), in_specs=[blk] * 4, out_specs=[blk] * 3, out_shape=[sh] * 3,
        compiler_params=_params(("parallel",)),
    )(w, g, m, v)


def _lb_param_grad(lb_param, dlb):
    L, C = lb_param.shape

    def body(p_ref, d_ref, o_ref):
        lbp = p_ref[...]
        d = d_ref[...]
        mx = jnp.max(lbp, axis=0, keepdims=True)
        e = jnp.exp(lbp - mx)
        p = e / jnp.sum(e, axis=0, keepdims=True)
        tot = jnp.sum(d, axis=0, keepdims=True)
        dps = []
        rest = tot
        for j in range(L):
            dps.append(rest - tot if j == 0 else rest)
            rest = rest - d[j:j + 1]
        dp = jnp.concatenate(dps, axis=0)
        o_ref[...] = p * (dp - jnp.sum(p * dp, axis=0, keepdims=True))

    vm = pl.BlockSpec(memory_space=pltpu.VMEM)
    return pl.pallas_call(body, name="lb_param_grad", in_specs=[vm, vm], out_specs=vm,
                          out_shape=jax.ShapeDtypeStruct((L, C), F32))(lb_param, dlb)


def _pack_small(loss_part, dg_pre, dg_post, dlb, dg_head, dsinks):
    pad8 = lambda a: jnp.pad(a.reshape(-1, 128), ((0, 8 - DEPTH), (0, 0)))
    rows = [dg_pre.reshape(-1, 128), dg_post.reshape(-1, 128), dlb.reshape(-1, 128), pad8(dg_head), pad8(dsinks),
            loss_part]
    return jnp.concatenate(rows, axis=0)


def _unpack_small(vec):
    n = DEPTH * D_MODEL // 128
    o = 0
    dg_pre = vec[o:o + n].reshape(DEPTH, D_MODEL); o += n
    dg_post = vec[o:o + n].reshape(DEPTH, D_MODEL); o += n
    dlb = vec[o:o + n].reshape(DEPTH, HG_WIDTH); o += n
    dg_head = vec[o:o + DEPTH]; o += 8
    dsinks = vec[o:o + DEPTH, :ATT_HEADS]; o += 8
    loss = jnp.sum(vec[o:o + 8])
    return loss, dg_pre, dg_post, dlb, dg_head, dsinks


def kernel(x, w_in, w_out, g_pre, g_post, lb_param, g_head, sinks, loss_target, m_w_in, m_w_out, m_g_pre, m_g_post, m_lb_param, m_g_head, m_sinks, v_w_in, v_w_out, v_g_pre, v_g_post, v_lb_param, v_g_head, v_sinks):
    tr = lambda a: jnp.swapaxes(a, 1, 2)
    w_in_t = tr(w_in)
    (loss_part, dx, gw_in_t, gw_out, dg_pre, dg_post, dlb, dg_head, dsinks) = _step(
        x, loss_target, g_pre, g_post, lb_param, g_head, sinks, shards=(w_in_t.astype(BF16), w_out.astype(BF16)))

    small = _allreduce_small(_pack_small(loss_part, dg_pre, dg_post, dlb, dg_head, dsinks))
    loss, gg_pre, gg_post, gdlb, gg_head, gsinks = _unpack_small(small)
    glb = _lb_param_grad(lb_param, gdlb)

    grads = [gw_in_t, gw_out, gg_pre, gg_post, glb, gg_head, gsinks]
    ws = [w_in_t, w_out, g_pre, g_post, lb_param, g_head, sinks]
    ms = [tr(m_w_in), m_w_out, m_g_pre, m_g_post, m_lb_param, m_g_head, m_sinks]
    vs = [tr(v_w_in), v_w_out, v_g_pre, v_g_post, v_lb_param, v_g_head, v_sinks]
    names = ["w_in", "w_out", "g_pre", "g_post", "lb_param", "g_head", "sinks"]
    deltas, new_m, new_v = [], [], []
    for w, g, m, v, nm in zip(ws, grads, ms, vs, names):
        sh = w.shape
        two = lambda a: a.reshape(-1, sh[-1])
        d, mn, vn = _adamw(two(w), two(g), two(m), two(v), "adamw_" + nm)
        deltas.append(d.reshape(sh))
        new_m.append(mn.reshape(sh))
        new_v.append(vn.reshape(sh))
    grads[0], deltas[0], new_m[0], new_v[0] = tr(grads[0]), tr(deltas[0]), tr(new_m[0]), tr(new_v[0])
    return (loss, dx, *grads, *deltas, *new_m, *new_v)
```

```python
import math

import numpy as np
import jax
import jax.numpy as jnp
from jax import lax
from jax.experimental import pallas as pl
from jax.experimental.pallas import tpu as pltpu

F32 = jnp.float32
BF16 = jnp.bfloat16

D_MODEL = 1024
DEPTH = 2
HG_HEADS = 8
HG_DIM = 128
HG_WIDTH = HG_HEADS * HG_DIM
CHUNK = 64
ATT_HEADS = 16
ATT_DIM = 64
ATT_WIDTH = ATT_HEADS * ATT_DIM
KV_WIDTH = 128
ATT_BLOCK = 128
ATT_SCALE = 1.0 / math.sqrt(ATT_DIM)
ROPE_THETA = 10000.0
NORM_EPS = 1e-6
NEG_INF = -1e30
LB_FLOOR = 1e-20
N_H = 4 * HG_WIDTH
N_A = 2 * ATT_WIDTH + 2 * KV_WIDTH
IN_WIDTH = N_H + N_A
MIX_WIDTH = HG_WIDTH + ATT_WIDTH

ADAM_LR = 0.001
ADAM_B1 = 0.9
ADAM_B2 = 0.999
ADAM_EPS = 1e-08
ADAM_WD = 0.01
ADAM_STEP = 10

N_DEV = 8
MESH = pl.DeviceIdType.MESH
VMEM_LIMIT = 56 * 1024 * 1024

NN = ((1,), (0,))
NT = ((1,), (1,))
TN = ((0,), (0,))


def _dot(a, b, dims):
    return lax.dot_general(a.astype(BF16), b.astype(BF16), (dims, ((), ())), preferred_element_type=F32)


def _params(sem=None, **kw):
    return pltpu.CompilerParams(dimension_semantics=sem, vmem_limit_bytes=VMEM_LIMIT, **kw)


def _sigmoids(x):
    e = jnp.exp(-jnp.abs(x))
    r = 1.0 / (1.0 + e)
    er = e * r
    pos = x >= 0.0
    return jnp.where(pos, r, er), jnp.where(pos, er, r)


def _silu(x):
    return x * _sigmoids(x)[0]


def _silu_and_grad(x):
    s, ns = _sigmoids(x)
    return x * s, s * (1.0 + x * ns)


def _pick(n, prefs):
    for p in prefs:
        if n % p == 0:
            return p
    return n


def _inproj(x2, g, w, name):
    T, D = x2.shape
    tm = _pick(T, (512, 256, 128))
    nchunk = 1024

    def body(x_ref, g_ref, w_ref, oh_ref, oa_ref, h_ref):
        x = x_ref[...]
        r = lax.rsqrt(jnp.mean(x * x, axis=-1, keepdims=True) + NORM_EPS)
        h = ((x * r) * g_ref[...]).astype(BF16)
        h_ref[...] = h
        for j in range(0, N_H, nchunk):
            oh_ref[:, j:j + nchunk] = lax.dot_general(h, w_ref[j:j + nchunk, :], (NT, ((), ())),
                                                      preferred_element_type=F32)
        for j in range(0, N_A, N_A // 2):
            oa_ref[:, j:j + N_A // 2] = lax.dot_general(h, w_ref[N_H + j:N_H + j + N_A // 2, :], (NT, ((), ())),
                                                        preferred_element_type=F32)

    row = lambda w_: pl.BlockSpec((tm, w_), lambda i: (i, 0))
    return pl.pallas_call(
        body, name=name,
        grid=(T // tm,),
        in_specs=[row(D), pl.BlockSpec((1, D), lambda i: (0, 0)),
                  pl.BlockSpec((IN_WIDTH, D), lambda i: (0, 0), pipeline_mode=pl.Buffered(1))],
        out_specs=[row(N_H), row(N_A), row(D)],
        out_shape=[jax.ShapeDtypeStruct((T, N_H), F32), jax.ShapeDtypeStruct((T, N_A), F32),
                   jax.ShapeDtypeStruct((T, D), BF16)],
        compiler_params=_params(("parallel",)),
    )(x2, g, w)


def _mm_tn(pieces, b, name, out_dtype=BF16):
    T, m = b.shape
    tn = 256
    parts = 2
    tr = T // parts
    counts = [p.shape[1] // tn for p in pieces]
    starts = [sum(counts[:i]) for i in range(len(pieces))]
    n_p = len(pieces)

    def body(*refs):
        b_ref, o_ref = refs[n_p * parts], refs[n_p * parts + 1]
        i = pl.program_id(0)
        for p in range(n_p):
            @pl.when((i >= starts[p]) & (i < starts[p] + counts[p]))
            def _(p=p):
                acc = sum(lax.dot_general(refs[p * parts + j][...], b_ref[j * tr:(j + 1) * tr, :], (TN, ((), ())),
                                          preferred_element_type=F32) for j in range(parts))
                o_ref[...] = acc.astype(out_dtype)

    piece_spec = lambda s, c, j: pl.BlockSpec((tr, tn), lambda i: (j, jnp.clip(i - s, 0, c - 1)))
    return pl.pallas_call(
        body, name=name,
        grid=(sum(counts),),
        in_specs=[piece_spec(s, c, j) for s, c in zip(starts, counts) for j in range(parts)]
        + [pl.BlockSpec((T, m), lambda i: (0, 0), pipeline_mode=pl.Buffered(1))],
        out_specs=pl.BlockSpec((tn, m), lambda i: (i, 0)),
        out_shape=jax.ShapeDtypeStruct((sum(counts) * tn, m), out_dtype),
        compiler_params=_params(("arbitrary",)),
    )(*[p for p in pieces for _ in range(parts)], b)


_LEVELS = (0, 1, 2, 4, 8, 16, 32)
_CUM_L = (2, 4, 8, 16, 32, 64)
_ALL_KINDS = tuple(("c", L) for L in _CUM_L) + tuple(("r", L) for L in _CUM_L)
_MXU_KINDS = (("c", 2), ("c", 4), ("c", CHUNK), ("r", 2), ("r", 4))
N_CUM = len(_ALL_KINDS) * CHUNK
N_CUM_F = len(_MXU_KINDS) * CHUNK


def _cum_matrices():
    t = np.arange(CHUNK)[:, None]
    r = np.arange(CHUNK)[None, :]

    def mat(kind):
        c, L = kind
        return ((r // L == t // L) & ((r <= t) if c == "c" else (r > t))).astype(np.float32)

    fwd = np.concatenate([mat(kd) for kd in _MXU_KINDS], axis=0)
    full = np.concatenate([mat(kd) for kd in _ALL_KINDS], axis=0)
    return jnp.asarray(fwd, BF16), jnp.asarray(full.T.copy(), BF16)


def _level_masks():
    t = np.arange(CHUNK)[:, None]
    s = np.arange(CHUNK)[None, :]
    ms = []
    for L in _LEVELS:
        if L == 0:
            ms.append(t == s)
        else:
            ms.append((t // (2 * L) == s // (2 * L)) & ((t // L) % 2 == 1) & ((s // L) % 2 == 0))
    return jnp.asarray(np.stack(ms).astype(np.float32))


def _split3(x):
    hi = x.astype(BF16)
    r1 = x - hi.astype(F32)
    mid = r1.astype(BF16)
    lo = (r1 - mid.astype(F32)).astype(BF16)
    return hi, mid, lo


def _cum3(ts, x, terms=3):
    d = lambda p: lax.dot_general(ts, p, (NN, ((), ())), preferred_element_type=F32)
    return sum(d(p) for p in _split3(x)[:terms])


def _lb_terms(lbp, layer):
    mx = jnp.max(lbp, axis=0, keepdims=True)
    e = jnp.exp(lbp - mx)
    p = e / jnp.sum(e, axis=0, keepdims=True)
    cum = p[0:1]
    for j in range(1, layer + 1):
        cum = cum + p[j:j + 1]
    lb = cum - p[0:1]
    lbf = jnp.maximum(lb, LB_FLOOR)
    return dict(lbf=lbf, one_m=1.0 - lb, kcorr=lb - lbf, ind=jnp.where(lb > LB_FLOOR, 1.0, 0.0))


def _gate(x, lt):
    sig, nsig = _sigmoids(x)
    f = lt["lbf"] + lt["one_m"] * sig
    return jnp.log(f), lt["one_m"] * nsig + lt["kcorr"], f, sig, nsig


def _ck(x, ci):
    return x[ci * CHUNK:(ci + 1) * CHUNK]


def _block_cums(ts, g, nc):
    cs = [_cum3(ts, _ck(g, ci), terms=2) for ci in range(nc)]
    out = {kind: jnp.concatenate([c[CHUNK * i:CHUNK * (i + 1)] for c in cs], axis=0)
           for i, kind in enumerate(_MXU_KINDS)}
    b = out[("c", CHUNK)]
    ng = CHUNK // 8
    last = b.reshape(nc, ng, 8, HG_DIM)[:, :, 7:8, :]
    zero = jnp.zeros((nc, 1, 1, HG_DIM), F32)

    def spread(groups):
        return jnp.broadcast_to(jnp.concatenate(groups, axis=1), (nc, ng, 8, HG_DIM)).reshape(nc * CHUNK, HG_DIM)

    def get(kind):
        if kind in out:
            return out[kind]
        c, L = kind
        nb = L // 8
        first = lambda r: (r // nb) * nb
        if c == "c":
            return b - spread([last[:, first(r) - 1:first(r)] if r >= nb else zero for r in range(ng)])
        return spread([last[:, first(r) + nb - 1:first(r) + nb] for r in range(ng)]) - b

    return get


def _level_factors(cums, g, L):
    if L == 0:
        return None, None
    if L == 1:
        return jnp.exp(g), None
    return jnp.exp(cums(("c", L))), jnp.exp(cums(("r", L)))


def _mul(a, e):
    return a if e is None else a * e


def _hg_block_fwd(qf, k, v, g, ts, m_ref, nc):
    cums = _block_cums(ts, g, nc)
    amat = [jnp.zeros((CHUNK, CHUNK), F32)] * nc
    for li, L in enumerate(_LEVELS):
        eq, ek = _level_factors(cums, g, L)
        ql, kl, m = _mul(qf, eq), _mul(k, ek), m_ref[li]
        amat = [amat[ci] + _dot(_ck(ql, ci), _ck(kl, ci), NT) * m for ci in range(nc)]
    b = cums(("c", CHUNK))
    kst = k * jnp.exp(cums(("r", CHUNK)))
    o = [_dot(amat[ci], _ck(v, ci), NN) for ci in range(nc)]
    kv = [_dot(_ck(v, ci), _ck(kst, ci), TN) for ci in range(nc)]
    dec = [jnp.exp(b[(ci + 1) * CHUNK - 1:(ci + 1) * CHUNK, :]) for ci in range(nc)]
    return o, dec, kv, qf * jnp.exp(b), amat


def _hg_block_bwd(qf, k, v, g, do, amat, ts, m_ref, nc):
    cums = _block_cums(ts, g, nc)
    dcs = {}
    da = [_dot(_ck(do, ci), _ck(v, ci), NT) for ci in range(nc)]
    dq = jnp.zeros_like(qf)
    dk = jnp.zeros_like(qf)
    dg = jnp.zeros_like(qf)
    for li, L in enumerate(_LEVELS):
        eq, ek = _level_factors(cums, g, L)
        ql, kl, m = _mul(qf, eq), _mul(k, ek), m_ref[li]
        qlb, klb = ql.astype(BF16), kl.astype(BF16)
        dal = [(da[ci] * m).astype(BF16) for ci in range(nc)]
        dql = jnp.concatenate([_dot(dal[ci], _ck(klb, ci), NN) for ci in range(nc)], axis=0)
        dkl = jnp.concatenate([_dot(dal[ci], _ck(qlb, ci), TN) for ci in range(nc)], axis=0)
        dql, dkl = _mul(dql, eq), _mul(dkl, ek)
        dq = dq + dql
        dk = dk + dkl
        if L == 1:
            dg = dg + dql * qf
        elif L > 1:
            dcs[("c", L)] = (dql * qf).astype(BF16)
            dcs[("r", L)] = (dkl * k).astype(BF16)
    b = cums(("c", CHUNK))
    e64 = jnp.exp(b)
    er64 = jnp.exp(cums(("r", CHUNK)))
    qb = (qf * e64).astype(BF16)
    return dict(dq=dq, dk=dk, dg=dg, dcs=dcs, e64=e64, er64=er64, qf=qf, k=k, kst=(k * er64).astype(BF16),
                dv=[_dot(amat[ci], _ck(do, ci), TN) for ci in range(nc)],
                dec=[jnp.exp(b[(ci + 1) * CHUNK - 1:(ci + 1) * CHUNK, :]) for ci in range(nc)],
                qd=[_dot(_ck(do, ci), _ck(qb, ci), TN) for ci in range(nc)])


def _hg_state_bwd(w, v, do, starts, ends, tst, nc):
    dqb = jnp.concatenate([_dot(_ck(do, ci), starts[ci], NN) for ci in range(nc)], axis=0)
    dkst = jnp.concatenate([_dot(_ck(v, ci), ends[ci], NN) for ci in range(nc)], axis=0)
    dqb, dkst = dqb * w["e64"], dkst * w["er64"]
    dq = w["dq"] + dqb
    dk = w["dk"] + dkst
    dv = jnp.concatenate([w["dv"][ci] + _dot(_ck(w["kst"], ci), ends[ci], NT) for ci in range(nc)], axis=0)
    trow = lax.broadcasted_iota(jnp.int32, (CHUNK, 1), 0)
    dtot = jnp.concatenate(
        [jnp.where(trow == CHUNK - 1, jnp.sum(ends[ci] * starts[ci], axis=0, keepdims=True) * w["dec"][ci], 0.0)
         for ci in range(nc)], axis=0)
    dcs = dict(w["dcs"])
    dcs[("c", CHUNK)] = (dqb * w["qf"] + dtot).astype(BF16)
    dcs[("r", CHUNK)] = (dkst * w["k"]).astype(BF16)
    dgs = [_dot(tst, jnp.concatenate([_ck(dcs[kind], ci) for kind in _ALL_KINDS], axis=0), NN) for ci in range(nc)]
    return dq, dk, dv, w["dg"] + jnp.concatenate(dgs, axis=0)


def _hgrn_fwd(proj_h, u_rows, lb_param, g_head, layer, name, phase=None):
    B, S, _ = proj_h.shape
    sb = _pick(S, (2048, 1024, 512, 256, 128, 64))
    nc = sb // CHUNK
    ts, _ = _cum_matrices()

    def body(*refs):
        ins, outs, (st,), p_in, p_out, p_sems = _split_refs(refs, 8, 12, 1, phase)
        q_ref, f_ref, i_ref, z_ref, lbp_ref, gh_ref, ts_ref, m_ref = ins
        o_ref, u_ref, sts_ref, am_ref = outs[:4]
        logf_ref, k_ref, qf_ref, sg_ref, qg_ref, zg_ref, fg_ref, sig_ref = outs[4:]
        h_id, b_id, s_id = pl.program_id(0), pl.program_id(1), pl.program_id(2)
        _hosted_start(phase, p_in, p_out, p_sems, (h_id == 0) & (b_id == 0) & (s_id == 0))

        @pl.when(s_id == 0)
        def _():
            st[...] = jnp.zeros_like(st)

        lt = _lb_terms(lbp_ref[...], layer)
        tsv = ts_ref[...]
        gh = gh_ref[...]
        logf, k, _, sig, nsig = _gate(f_ref[...], lt)
        qf, qf_grad = _silu_and_grad(q_ref[...])
        sg, sg_grad = _silu_and_grad(z_ref[...])
        logf_ref[...], k_ref[...], qf_ref[...], sg_ref[...] = logf, k, qf, sg
        qg_ref[...] = qf_grad.astype(BF16)
        zg_ref[...] = sg_grad.astype(BF16)
        fg_ref[...] = (lt["one_m"] * sig * nsig).astype(BF16)
        sig_ref[...] = sig.astype(BF16)
        o_part, dec, kv, qb, amat = _hg_block_fwd(qf, k, i_ref[...], logf, tsv, m_ref, nc)
        for ci in range(nc):
            am_ref[ci] = amat[ci].astype(BF16)
        cur = st[...]
        starts = []
        for ci in range(nc):
            sts_ref[ci] = cur
            starts.append(cur)
            cur = cur * dec[ci] + kv[ci]
        st[...] = cur
        o = jnp.concatenate([o_part[ci] + _dot(_ck(qb, ci), starts[ci], NT) for ci in range(nc)], axis=0)
        o_ref[...] = o
        r = lax.rsqrt(jnp.mean(o * o, axis=-1, keepdims=True) + NORM_EPS)
        u_ref[...] = (((o * r) * gh) * sg).astype(BF16)
        _hosted_finish(phase, p_in, p_out, p_sems, (h_id == HG_HEADS - 1) & (b_id == B - 1) & (s_id == S // sb - 1))

    col = lambda base: pl.BlockSpec((None, sb, HG_DIM), lambda h, b, s: (b, s, base + h))
    p_ispecs, p_ospecs, p_oshapes, p_alias, p_scratch, p_args = _host_phase(phase, 8, 12)
    wide = lambda dt: jax.ShapeDtypeStruct((B, S, HG_WIDTH), dt)
    res = pl.pallas_call(
        body, name=name,
        grid=(HG_HEADS, B, S // sb),
        in_specs=[col(0), col(HG_HEADS), col(2 * HG_HEADS), col(3 * HG_HEADS),
                  pl.BlockSpec((DEPTH, HG_DIM), lambda h, b, s: (0, h)),
                  pl.BlockSpec((1, HG_DIM), lambda h, b, s: (0, 0)),
                  pl.BlockSpec((N_CUM_F, CHUNK), lambda h, b, s: (0, 0)),
                  pl.BlockSpec((len(_LEVELS), CHUNK, CHUNK), lambda h, b, s: (0, 0, 0))] + p_ispecs,
        out_specs=[col(0), col(0),
                   pl.BlockSpec((None, None, nc, HG_DIM, HG_DIM), lambda h, b, s: (b, h, s, 0, 0)),
                   pl.BlockSpec((None, None, nc, CHUNK, CHUNK), lambda h, b, s: (b, h, s, 0, 0))]
        + [col(0)] * 8 + p_ospecs,
        out_shape=[wide(F32),
                   jax.ShapeDtypeStruct((B, S, u_rows), BF16),
                   jax.ShapeDtypeStruct((B, HG_HEADS, S // CHUNK, HG_DIM, HG_DIM), F32),
                   jax.ShapeDtypeStruct((B, HG_HEADS, S // CHUNK, CHUNK, CHUNK), BF16)]
        + [wide(F32)] * 4 + [wide(BF16)] * 4 + p_oshapes,
        input_output_aliases=p_alias,
        scratch_shapes=[pltpu.VMEM((HG_DIM, HG_DIM), F32)] + p_scratch,
        compiler_params=_params(("arbitrary", "arbitrary", "arbitrary")),
    )(proj_h, proj_h, proj_h, proj_h, lb_param, g_head, ts, _level_masks(), *p_args)
    return res[0], res[1], tuple(res[2:12]), list(res[12:])


def _hgrn_bwd(proj_h, o_h, du, kept, lb_param, g_head, layer, name, phase=None):
    B, S, _ = proj_h.shape
    sb = _pick(S, (512, 256, 128, 64))
    nc = sb // CHUNK
    ns = S // sb
    ts, tst = _cum_matrices()

    def body(*refs):
        ins, outs, (dst,), p_in, p_out, p_sems = _split_refs(refs, 18, 6, 1, phase)
        (i_ref, o_ref, du_ref, sts_ref, am_ref, logf_ref, k_ref, qf_ref, sg_ref, qg_ref, zg_ref, fg_ref, sig_ref,
         lbp_ref, gh_ref, ts_ref, tst_ref, m_ref) = ins
        dq_ref, df_ref, di_ref, dz_ref, dlb_ref, dgh_ref = outs
        h_id, b_id, s_id = pl.program_id(0), pl.program_id(1), pl.program_id(2)
        _hosted_start(phase, p_in, p_out, p_sems, (h_id == 0) & (b_id == 0) & (s_id == 0))

        @pl.when(s_id == 0)
        def _():
            dst[...] = jnp.zeros_like(dst)

        @pl.when((b_id == 0) & (s_id == 0))
        def _():
            dlb_ref[...] = jnp.zeros_like(dlb_ref)

        @pl.when((h_id == 0) & (b_id == 0) & (s_id == 0))
        def _():
            dgh_ref[...] = jnp.zeros_like(dgh_ref)

        lt = _lb_terms(lbp_ref[...], layer)
        gh = gh_ref[...]
        tsv = ts_ref[...]
        tstv = tst_ref[...]
        logf, k, qf, sg = logf_ref[...], k_ref[...], qf_ref[...], sg_ref[...]
        o = o_ref[...]
        dub = du_ref[...]
        r = lax.rsqrt(jnp.mean(o * o, axis=-1, keepdims=True) + NORM_EPS)
        n = o * r
        dz_ref[...] = (dub * (n * gh) * zg_ref[...].astype(F32)).astype(BF16)
        dgh_ref[...] += jnp.sum(dub * sg * n, axis=0, keepdims=True)
        dn = dub * sg * gh
        do = (r * (dn - n * jnp.mean(dn * n, axis=-1, keepdims=True))).astype(BF16)
        v = i_ref[...].astype(BF16)
        w = _hg_block_bwd(qf, k, v, logf, do, [am_ref[ci] for ci in range(nc)], tsv, m_ref, nc)
        cur = dst[...]
        ends = [None] * nc
        for ci in reversed(range(nc)):
            ends[ci] = cur
            cur = cur * w["dec"][ci] + w["qd"][ci]
        dst[...] = cur
        dq, dk, dv, dg = _hg_state_bwd(w, v, do, [sts_ref[ci] for ci in range(nc)], ends, tstv, nc)
        di_ref[...] = dv.astype(BF16)
        dq_ref[...] = (dq * qg_ref[...].astype(F32)).astype(BF16)
        f = jnp.exp(logf)
        scaled = (dg - f * dk) / f
        df_ref[...] = (scaled * fg_ref[...].astype(F32)).astype(BF16)
        dlb_ref[...] += jnp.sum(scaled * (lt["ind"] - sig_ref[...].astype(F32)), axis=0, keepdims=True)
        _hosted_finish(phase, p_in, p_out, p_sems, (h_id == HG_HEADS - 1) & (b_id == B - 1) & (s_id == ns - 1))

    col = lambda base: pl.BlockSpec((None, sb, HG_DIM), lambda h, b, s: (b, ns - 1 - s, base + h))
    out_col = pl.BlockSpec((None, sb, HG_DIM), lambda h, b, s: (b, ns - 1 - s, h))
    dt = jax.ShapeDtypeStruct((B, S, HG_WIDTH), BF16)
    p_ispecs, p_ospecs, p_oshapes, p_alias, p_scratch, p_args = _host_phase(phase, 18, 6)
    res = pl.pallas_call(
        body, name=name,
        grid=(HG_HEADS, B, ns),
        in_specs=[col(2 * HG_HEADS), col(0), col(0),
                  pl.BlockSpec((None, None, nc, HG_DIM, HG_DIM), lambda h, b, s: (b, h, ns - 1 - s, 0, 0)),
                  pl.BlockSpec((None, None, nc, CHUNK, CHUNK), lambda h, b, s: (b, h, ns - 1 - s, 0, 0))]
        + [col(0)] * 8
        + [pl.BlockSpec((DEPTH, HG_DIM), lambda h, b, s: (0, h)),
           pl.BlockSpec((1, HG_DIM), lambda h, b, s: (0, 0)),
           pl.BlockSpec((N_CUM_F, CHUNK), lambda h, b, s: (0, 0)),
           pl.BlockSpec((CHUNK, N_CUM), lambda h, b, s: (0, 0)),
           pl.BlockSpec((len(_LEVELS), CHUNK, CHUNK), lambda h, b, s: (0, 0, 0))] + p_ispecs,
        out_specs=[out_col, out_col, out_col, out_col,
                   pl.BlockSpec((1, HG_DIM), lambda h, b, s: (0, h)),
                   pl.BlockSpec((1, HG_DIM), lambda h, b, s: (0, 0))] + p_ospecs,
        out_shape=[dt, dt, dt, dt, jax.ShapeDtypeStruct((1, HG_WIDTH), F32),
                   jax.ShapeDtypeStruct((1, HG_DIM), F32)] + p_oshapes,
        input_output_aliases=p_alias,
        scratch_shapes=[pltpu.VMEM((HG_DIM, HG_DIM), F32)] + p_scratch,
        compiler_params=_params(("arbitrary", "arbitrary", "arbitrary")),
    )(proj_h, o_h, du, *kept, lb_param, g_head, ts, tst, _level_masks(), *p_args)
    return tuple(res[:6]) + (list(res[6:]),)


def _rope_tables(S):
    half = ATT_DIM // 2
    inv_freq = ROPE_THETA ** (-jnp.arange(half, dtype=F32) / half)
    ang = jnp.arange(S).astype(F32)[:, None] * inv_freq[None, :]
    cos = jnp.cos(ang)
    sin = jnp.sin(ang)
    cos = jnp.concatenate([cos, cos, cos, cos], axis=1)
    sin = jnp.concatenate([-sin, sin, -sin, sin], axis=1)
    return cos, sin


def _attn_common():
    lane = lax.broadcasted_iota(jnp.int32, (1, 2 * ATT_DIM), 1)
    first_half = (lane % ATT_DIM) < (ATT_DIM // 2)
    left = lane < ATT_DIM

    def swap(x):
        return jnp.where(first_half, pltpu.roll(x, 128 - ATT_DIM // 2, 1), pltpu.roll(x, ATT_DIM // 2, 1))

    def rope(x, cos, sin):
        return x * cos + swap(x) * sin

    def rope_bwd(dy, cos, sin):
        return dy * cos + swap(dy * sin)

    def dup(x):
        xs = pltpu.roll(x, ATT_DIM, 1)
        return [jnp.where(left, x, xs), jnp.where(left, xs, x)]

    return left, rope, rope_bwd, dup


GROUP = ATT_HEADS // 2
GROUP_ROWS = GROUP * ATT_BLOCK


def _attn_bias(i):
    r = lax.broadcasted_iota(jnp.int32, (ATT_BLOCK, 2 * ATT_BLOCK), 0)
    c = lax.broadcasted_iota(jnp.int32, (ATT_BLOCK, 2 * ATT_BLOCK), 1)
    ok = (c > r) & (c <= r + ATT_BLOCK) & ((c >= ATT_BLOCK) | (i > 0))
    return jnp.where(ok, 0.0, NEG_INF)


def _stack_heads(pairs, left):
    rows = []
    for x in pairs:
        rows += [jnp.where(left, x, 0.0), jnp.where(left, 0.0, x)]
    return jnp.concatenate(rows, axis=0)


def _unstack_heads(y, left, pp):
    r0 = 2 * pp * ATT_BLOCK
    return jnp.where(left, y[r0:r0 + ATT_BLOCK], y[r0 + ATT_BLOCK:r0 + 2 * ATT_BLOCK])


def _row_sums(x):
    return _dot(x, jnp.ones((x.shape[1], 128), BF16), NN)


def _attn_probs(qs, kd, vd, sink, bias):
    n = range(len(qs))
    rows = qs[0].shape[0]
    s = [(_dot(qs[j], kd[j], NT).reshape(rows // ATT_BLOCK, ATT_BLOCK, 2 * ATT_BLOCK) * ATT_SCALE + bias[None])
         .reshape(rows, 2 * ATT_BLOCK) for j in n]
    m = [jnp.max(jnp.maximum(jnp.maximum(s[j][:, :128], s[j][:, 128:]), sink[j]), axis=-1, keepdims=True) for j in n]
    pu = [jnp.exp(s[j] - m[j]) for j in n]
    es = [jnp.exp(sink[j] - m[j]) for j in n]
    ones = jnp.ones((2 * ATT_BLOCK, 128), BF16)
    ov = [_dot(pu[j], jnp.concatenate([vd[j].astype(BF16), ones], axis=1), NN) for j in n]
    inv = [1.0 / (ov[j][:, 128:] + es[j]) for j in n]
    return ([pu[j] * jnp.concatenate([inv[j], inv[j]], axis=1) for j in n], [es[j] * inv[j] for j in n],
            [ov[j][:, :128] * inv[j] for j in n])


def _sink_rows(sinks_l):
    return jnp.broadcast_to(jnp.repeat(sinks_l, ATT_BLOCK)[:, None], (ATT_HEADS * ATT_BLOCK, 128))


_Z0 = (2 * ATT_WIDTH + 2 * KV_WIDTH - ATT_WIDTH) // 256


def _attn_fwd(proj_a, u, sinks_l, cos, sin, name, phase=None):
    B, S, _ = proj_a.shape
    nb = S // ATT_BLOCK

    def body(*refs):
        ins, (u_ref, p_ref, o_ref, ps_ref, qs_ref), _, p_in, p_out, p_sems = _split_refs(refs, 13, 5, 0, phase)
        q_ref, kvc_ref, kvp_ref, z0, z1, z2, z3, cos_ref, sin_ref, cosp_ref, sinp_ref, sinks_ref, _ = ins
        i = pl.program_id(1)
        _hosted_start(phase, p_in, p_out, p_sems, (pl.program_id(0) == 0) & (i == 0))
        left, rope, _, dup = _attn_common()
        cos_c, sin_c = cos_ref[...], sin_ref[...]
        kvc = kvc_ref[...]
        kvp = kvp_ref[...]
        kw = jnp.concatenate([rope(kvp[:, :KV_WIDTH], cosp_ref[...], sinp_ref[...]),
                              rope(kvc[:, :KV_WIDTH], cos_c, sin_c)], axis=0)
        vw = jnp.concatenate([kvp[:, KV_WIDTH:], kvc[:, KV_WIDTH:]], axis=0)
        kd, vd = dup(kw), dup(vw)
        bias = _attn_bias(i)
        zs = (z0, z1, z2, z3)
        pairs = [range(4 * kvh, 4 * kvh + 4) for kvh in range(2)]
        qs = [_stack_heads([rope(q_ref[:, 128 * pr:128 * (pr + 1)], cos_c, sin_c) for pr in pairs[kvh]], left)
              for kvh in range(2)]
        sink = [sinks_ref[kvh * GROUP_ROWS:(kvh + 1) * GROUP_ROWS, :] for kvh in range(2)]
        p, ps, o = _attn_probs(qs, kd, vd, sink, bias)
        eye = (lax.broadcasted_iota(jnp.int32, (ATT_BLOCK, 128), 0)
               == lax.broadcasted_iota(jnp.int32, (ATT_BLOCK, 128), 1))
        for kvh in range(2):
            p_ref[kvh] = p[kvh].astype(BF16)
            qs_ref[kvh] = qs[kvh].astype(BF16)
            for g in range(GROUP):
                blk = ps[kvh][g * ATT_BLOCK:(g + 1) * ATT_BLOCK, :]
                ps_ref[kvh * GROUP + g:kvh * GROUP + g + 1, :] = jnp.sum(jnp.where(eye, blk, 0.0), axis=0, keepdims=True)
            for pp, pr in enumerate(pairs[kvh]):
                z = zs[pr // 2][:, 128 * (pr % 2):128 * (pr % 2 + 1)]
                o128 = _unstack_heads(o[kvh], left, pp)
                o_ref[:, 128 * pr:128 * (pr + 1)] = o128.astype(BF16)
                u_ref[:, 128 * pr:128 * (pr + 1)] = (o128 * _silu(z)).astype(BF16)
        _hosted_finish(phase, p_in, p_out, p_sems, (pl.program_id(0) == B - 1) & (i == nb - 1))

    rowblk = lambda w, cb: pl.BlockSpec((None, ATT_BLOCK, w), lambda b, i: (b, i, cb))
    tab = pl.BlockSpec((ATT_BLOCK, 128), lambda b, i: (i, 0))
    tabp = pl.BlockSpec((ATT_BLOCK, 128), lambda b, i: (jnp.maximum(i - 1, 0), 0))
    p_ispecs, p_ospecs, p_oshapes, p_alias, p_scratch, p_args = _host_phase(phase, 13, 5)
    res = pl.pallas_call(
        body, name=name,
        grid=(B, nb),
        in_specs=[rowblk(ATT_WIDTH, 0), rowblk(256, 4),
                  pl.BlockSpec((None, ATT_BLOCK, 256), lambda b, i: (b, jnp.maximum(i - 1, 0), 4)),
                  rowblk(256, _Z0), rowblk(256, _Z0 + 1), rowblk(256, _Z0 + 2), rowblk(256, _Z0 + 3),
                  tab, tab, tabp, tabp,
                  pl.BlockSpec((ATT_HEADS * ATT_BLOCK, 128), lambda b, i: (0, 0)),
                  pl.BlockSpec(memory_space=pl.ANY)] + p_ispecs,
        out_specs=[pl.BlockSpec((None, ATT_BLOCK, ATT_WIDTH), lambda b, i: (b, i, 1)),
                   pl.BlockSpec((None, None, 2, GROUP_ROWS, 2 * ATT_BLOCK), lambda b, i: (b, i, 0, 0, 0)),
                   pl.BlockSpec((None, ATT_BLOCK, ATT_WIDTH), lambda b, i: (b, i, 0)),
                   pl.BlockSpec((None, None, ATT_HEADS, 128), lambda b, i: (b, i, 0, 0)),
                   pl.BlockSpec((None, None, 2, GROUP_ROWS, 128), lambda b, i: (b, i, 0, 0, 0))] + p_ospecs,
        out_shape=[jax.ShapeDtypeStruct(u.shape, BF16),
                   jax.ShapeDtypeStruct((B, nb, 2, GROUP_ROWS, 2 * ATT_BLOCK), BF16),
                   jax.ShapeDtypeStruct((B, S, ATT_WIDTH), BF16),
                   jax.ShapeDtypeStruct((B, nb, ATT_HEADS, 128), F32),
                   jax.ShapeDtypeStruct((B, nb, 2, GROUP_ROWS, 128), BF16)] + p_oshapes,
        input_output_aliases={12: 0, **p_alias},
        scratch_shapes=p_scratch,
        compiler_params=_params(("arbitrary", "arbitrary")),
    )(proj_a, proj_a, proj_a, proj_a, proj_a, proj_a, proj_a, cos, sin, cos, sin, sinks_l, u, *p_args)
    return res[0], tuple(res[1:5]), list(res[5:])


def _attn_bwd(proj_a, du, kept, cos, sin, name, phase=None):
    B, S, _ = proj_a.shape
    nb = S // ATT_BLOCK
    p_kept, o_kept, ps_kept, qs_kept = kept

    def body(*refs):
        ins, outs, (carry, sk_acc), p_in, p_out, p_sems = _split_refs(refs, 15, 4, 2, phase)
        (qs_ref, kvc_ref, kvp_ref, z0, z1, z2, z3, du_ref, cos_ref, sin_ref, cosp_ref, sinp_ref,
         p_ref, o_ref, ps_ref) = ins
        dq_ref, dkv_ref, dz_ref, dsk_ref = outs
        b_id, i = pl.program_id(0), pl.program_id(1)
        _hosted_start(phase, p_in, p_out, p_sems, (b_id == 0) & (i == 0))

        @pl.when((b_id == 0) & (i == 0))
        def _():
            sk_acc[...] = jnp.zeros_like(sk_acc)

        @pl.when(i == 0)
        def _():
            carry[...] = jnp.zeros_like(carry)

        @pl.when(i < nb)
        def _():
            left, rope, rope_bwd, dup = _attn_common()
            cos_c, sin_c = cos_ref[...], sin_ref[...]
            cos_p, sin_p = cosp_ref[...], sinp_ref[...]
            kvc = kvc_ref[...]
            kvp = kvp_ref[...]
            kw = jnp.concatenate([rope(kvp[:, :KV_WIDTH], cos_p, sin_p), rope(kvc[:, :KV_WIDTH], cos_c, sin_c)], axis=0)
            vw = jnp.concatenate([kvp[:, KV_WIDTH:], kvc[:, KV_WIDTH:]], axis=0)
            kd, vd = dup(kw), dup(vw)
            zs = (z0, z1, z2, z3)
            units = [(kvh, hf) for kvh in range(2) for hf in range(2)]
            half = GROUP_ROWS // 2
            pairs = [range(4 * kvh + 2 * hf, 4 * kvh + 2 * hf + 2) for kvh, hf in units]
            ku = [kd[kvh] for kvh, _ in units]
            vu = [vd[kvh] for kvh, _ in units]
            ps_all = ps_ref[...]
            head_row = lax.broadcasted_iota(jnp.int32, (ATT_HEADS, 128), 0)
            eye = (lax.broadcasted_iota(jnp.int32, (ATT_BLOCK, 128), 0)
                   == lax.broadcasted_iota(jnp.int32, (ATT_BLOCK, 128), 1))

            def first(j):
                kvh, hf = units[j]
                p = p_ref[kvh, hf * half:(hf + 1) * half, :]
                parts = []
                for pr in pairs[j]:
                    cols = slice(128 * pr, 128 * (pr + 1))
                    sg, sg_grad = _silu_and_grad(zs[pr // 2][:, 128 * (pr % 2):128 * (pr % 2 + 1)])
                    du128 = du_ref[:, cols]
                    dz_ref[:, cols] = (du128 * o_ref[:, cols].astype(F32) * sg_grad).astype(BF16)
                    parts.append(du128 * sg)
                dos = _stack_heads(parts, left)
                dp = _dot(dos, vu[j], NT)
                delta = _row_sums(p.astype(F32) * dp)
                ds = (p.astype(F32) * (dp - jnp.concatenate([delta, delta], axis=1)) * ATT_SCALE).astype(BF16)
                sk = jnp.zeros((ATT_HEADS, 128), F32)
                for hh in range(4):
                    hd = kvh * GROUP + 4 * hf + hh
                    drow = jnp.sum(jnp.where(eye, delta[hh * ATT_BLOCK:(hh + 1) * ATT_BLOCK, :], 0.0), axis=0,
                                   keepdims=True)
                    sk = sk - jnp.where(head_row == hd, ps_all * drow, 0.0)
                sk_acc[...] += sk
                return ds, p, dos.astype(BF16), qs_ref[kvh, hf * half:(hf + 1) * half, :]

            def second(j, ds, p, dos, qs):
                dqs = _dot(ds, ku[j], NN)
                for pp, pr in enumerate(pairs[j]):
                    dq_ref[:, 128 * pr:128 * (pr + 1)] = rope_bwd(_unstack_heads(dqs, left, pp),
                                                                  cos_c, sin_c).astype(BF16)
                return _dot(ds, qs, TN), _dot(p, dos, TN)

            got, dku, dvu = {}, [None] * len(units), [None] * len(units)
            for j in range(len(units) + 1):
                if j < len(units):
                    got[j] = first(j)
                if j >= 1:
                    dku[j - 1], dvu[j - 1] = second(j - 1, *got.pop(j - 1))
            dkd = [dku[0] + dku[1], dku[2] + dku[3]]
            dvd = [dvu[0] + dvu[1], dvu[2] + dvu[3]]
            fold = lambda pr: jnp.where(left, pr[0] + pltpu.roll(pr[0], ATT_DIM, 1), pr[1] + pltpu.roll(pr[1], ATT_DIM, 1))
            dkw = fold(dkd)
            dvw = fold(dvd)
            prev = jnp.concatenate([rope_bwd(dkw[:ATT_BLOCK], cos_p, sin_p), dvw[:ATT_BLOCK]], axis=1)
            cur = jnp.concatenate([rope_bwd(dkw[ATT_BLOCK:], cos_c, sin_c), dvw[ATT_BLOCK:]], axis=1)
            dkv_ref[...] = (carry[...] + prev).astype(BF16)
            carry[...] = cur

        @pl.when(i == nb)
        def _():
            dkv_ref[...] = carry[...].astype(BF16)

        @pl.when((b_id == B - 1) & (i == nb))
        def _():
            diag = (lax.broadcasted_iota(jnp.int32, (ATT_HEADS, 128), 0)
                    == lax.broadcasted_iota(jnp.int32, (ATT_HEADS, 128), 1))
            tot = jnp.sum(sk_acc[...], axis=1, keepdims=True)
            dsk_ref[...] = jnp.sum(jnp.where(diag, tot, 0.0), axis=0, keepdims=True)

        _hosted_finish(phase, p_in, p_out, p_sems, (b_id == B - 1) & (i == nb))

    cl = lambda i: jnp.minimum(i, nb - 1)
    pv = lambda i: jnp.maximum(jnp.minimum(i, nb - 1) - 1, 0)
    rowblk = lambda w, cb: pl.BlockSpec((None, ATT_BLOCK, w), lambda b, i: (b, cl(i), cb))
    tab = pl.BlockSpec((ATT_BLOCK, 128), lambda b, i: (cl(i), 0))
    tabp = pl.BlockSpec((ATT_BLOCK, 128), lambda b, i: (pv(i), 0))
    p_ispecs, p_ospecs, p_oshapes, p_alias, p_scratch, p_args = _host_phase(phase, 15, 4)
    res = pl.pallas_call(
        body, name=name,
        grid=(B, nb + 1),
        in_specs=[pl.BlockSpec((None, None, 2, GROUP_ROWS, 128), lambda b, i: (b, cl(i), 0, 0, 0)), rowblk(256, 4),
                  pl.BlockSpec((None, ATT_BLOCK, 256), lambda b, i: (b, pv(i), 4)),
                  rowblk(256, _Z0), rowblk(256, _Z0 + 1), rowblk(256, _Z0 + 2), rowblk(256, _Z0 + 3),
                  rowblk(ATT_WIDTH, 1),
                  tab, tab, tabp, tabp,
                  pl.BlockSpec((None, None, 2, GROUP_ROWS, 2 * ATT_BLOCK), lambda b, i: (b, cl(i), 0, 0, 0)),
                  rowblk(ATT_WIDTH, 0),
                  pl.BlockSpec((None, None, ATT_HEADS, 128), lambda b, i: (b, cl(i), 0, 0))] + p_ispecs,
        out_specs=[rowblk(ATT_WIDTH, 0),
                   pl.BlockSpec((None, ATT_BLOCK, 256), lambda b, i: (b, jnp.maximum(i - 1, 0), 0)),
                   rowblk(ATT_WIDTH, 0),
                   pl.BlockSpec((1, 128), lambda b, i: (0, 0))] + p_ospecs,
        out_shape=[jax.ShapeDtypeStruct((B, S, ATT_WIDTH), BF16), jax.ShapeDtypeStruct((B, S, 256), BF16),
                   jax.ShapeDtypeStruct((B, S, ATT_WIDTH), BF16), jax.ShapeDtypeStruct((1, 128), F32)] + p_oshapes,
        input_output_aliases=p_alias,
        scratch_shapes=[pltpu.VMEM((ATT_BLOCK, 256), F32), pltpu.VMEM((ATT_HEADS, 128), F32)] + p_scratch,
        compiler_params=_params(("arbitrary", "arbitrary")),
    )(qs_kept, proj_a, proj_a, proj_a, proj_a, proj_a, proj_a, du, cos, sin, cos, sin, p_kept, o_kept, ps_kept, *p_args)
    return tuple(res[:4]) + (list(res[4:]),)


def _outproj_fwd(u2, w_out, x2, g_post, target2, name):
    T, D = x2.shape
    tm = _pick(T, (512, 256, 128))
    last = target2 is not None

    def body(u_ref, w_ref, x_ref, g_ref, *rest):
        y = lax.dot_general(u_ref[...], w_ref[...], (NN, ((), ())), preferred_element_type=F32)
        r = lax.rsqrt(jnp.mean(y * y, axis=-1, keepdims=True) + NORM_EPS)
        xn = x_ref[...] + (y * r) * g_ref[...]
        if last:
            t_ref, y_ref, dx_ref, loss_ref = rest
            err = xn - t_ref[...]
            dx_ref[...] = err * (1.0 / D)
            sq = err * err
            acc = sq[:, 0:128]
            for kk in range(1, D // 128):
                acc = acc + sq[:, 128 * kk:128 * (kk + 1)]
            part = jnp.sum(acc.reshape(tm // 8, 8, 128), axis=0) * (0.5 / D)

            @pl.when(pl.program_id(0) == 0)
            def _():
                loss_ref[...] = jnp.zeros_like(loss_ref)

            loss_ref[...] += part
        else:
            y_ref, xn_ref = rest
            xn_ref[...] = xn
        y_ref[...] = y

    row = pl.BlockSpec((tm, D), lambda i: (i, 0))
    in_specs = [pl.BlockSpec((tm, MIX_WIDTH), lambda i: (i, 0)),
                pl.BlockSpec((MIX_WIDTH, D), lambda i: (0, 0)), row,
                pl.BlockSpec((1, D), lambda i: (0, 0))]
    args = [u2, w_out, x2, g_post]
    out_specs = [row, row]
    out_shape = [jax.ShapeDtypeStruct((T, D), F32), jax.ShapeDtypeStruct((T, D), F32)]
    if last:
        in_specs.append(row)
        args.append(target2)
        out_specs.append(pl.BlockSpec((8, 128), lambda i: (0, 0)))
        out_shape.append(jax.ShapeDtypeStruct((8, 128), F32))
    return pl.pallas_call(
        body, name=name, grid=(T // tm,), in_specs=in_specs, out_specs=out_specs, out_shape=out_shape,
        compiler_params=_params(("arbitrary",)),
    )(*args)


def _outproj_bwd(dxn2, y2, g_post, w_out, u2, name):
    T, D = y2.shape
    N = w_out.shape[0]
    tm = _pick(T, (512, 256, 128))
    nt = T // tm

    def body(dx_ref, y_ref, g_ref, w_ref, u_ref, dg_ref, du_ref, dw_ref, acc, wacc):
        i = pl.program_id(0)

        @pl.when(i == 0)
        def _():
            acc[...] = jnp.zeros_like(acc)
            wacc[...] = jnp.zeros_like(wacc)

        y = y_ref[...]
        dxn = dx_ref[...]
        r = lax.rsqrt(jnp.mean(y * y, axis=-1, keepdims=True) + NORM_EPS)
        n = y * r
        dn = dxn * g_ref[...]
        dy = (r * (dn - n * jnp.mean(dn * n, axis=-1, keepdims=True))).astype(BF16)
        du_ref[...] = lax.dot_general(dy, w_ref[...], (NT, ((), ())), preferred_element_type=F32)
        wacc[...] += lax.dot_general(u_ref[...], dy, (TN, ((), ())), preferred_element_type=F32)
        acc[...] += jnp.sum((dxn * n).reshape(tm // 8, 8, D), axis=0)

        @pl.when(i == nt - 1)
        def _():
            dg_ref[...] = jnp.sum(acc[...], axis=0, keepdims=True)
            dw_ref[...] = wacc[...].astype(BF16)

    row = pl.BlockSpec((tm, D), lambda i: (i, 0))
    wide = pl.BlockSpec((tm, N), lambda i: (i, 0))
    vec = pl.BlockSpec((1, D), lambda i: (0, 0))
    whole = pl.BlockSpec((N, D), lambda i: (0, 0))
    return pl.pallas_call(
        body, name=name, grid=(nt,),
        in_specs=[row, row, vec, pl.BlockSpec((N, D), lambda i: (0, 0), pipeline_mode=pl.Buffered(1)), wide],
        out_specs=[vec, wide, whole],
        out_shape=[jax.ShapeDtypeStruct((1, D), F32), jax.ShapeDtypeStruct((T, N), F32),
                   jax.ShapeDtypeStruct((N, D), BF16)],
        scratch_shapes=[pltpu.VMEM((8, D), F32), pltpu.VMEM((N, D), F32)],
        compiler_params=_params(("arbitrary",)),
    )(dxn2, y2, g_post, w_out, u2)


def _inproj_bwd(pieces, w_t, x2, dxn2, g_pre, name, phase=None):
    T, D = x2.shape
    widths = [p.shape[1] for p in pieces]
    offs = [sum(widths[:i]) for i in range(len(pieces))]
    n_p = len(pieces)
    tm = _pick(T, (256, 128))
    nt = T // tm

    def body(*refs):
        ins, (dx_ref, dg_ref), (acc,), p_in, p_out, p_sems = _split_refs(refs, n_p + 4, 2, 1, phase)
        w_ref, x_ref, dxn_ref, g_ref = ins[n_p:]
        i = pl.program_id(0)
        _hosted_start(phase, p_in, p_out, p_sems, i == 0)

        @pl.when(i == 0)
        def _():
            acc[...] = jnp.zeros_like(acc)

        dh = jnp.zeros((tm, D), F32)
        for p in range(n_p):
            dh = dh + lax.dot_general(ins[p][...], w_ref[offs[p]:offs[p] + widths[p], :], (NN, ((), ())),
                                      preferred_element_type=F32)
        x = x_ref[...]
        r = lax.rsqrt(jnp.mean(x * x, axis=-1, keepdims=True) + NORM_EPS)
        n = x * r
        dn = dh * g_ref[...]
        dx_ref[...] = dxn_ref[...] + r * (dn - n * jnp.mean(dn * n, axis=-1, keepdims=True))
        acc[...] += jnp.sum((dh * n).reshape(tm // 8, 8, D), axis=0)

        @pl.when(i == nt - 1)
        def _():
            dg_ref[...] = jnp.sum(acc[...], axis=0, keepdims=True)

        _hosted_finish(phase, p_in, p_out, p_sems, i == nt - 1)

    row = pl.BlockSpec((tm, D), lambda i: (i, 0))
    vec = pl.BlockSpec((1, D), lambda i: (0, 0))
    p_ispecs, p_ospecs, p_oshapes, p_alias, p_scratch, p_args = _host_phase(phase, n_p + 4, 2)
    res = pl.pallas_call(
        body, name=name, grid=(nt,),
        in_specs=[pl.BlockSpec((tm, w), lambda i: (i, 0)) for w in widths]
        + [pl.BlockSpec((sum(widths), D), lambda i: (0, 0), pipeline_mode=pl.Buffered(1)), row, row, vec] + p_ispecs,
        out_specs=[row, vec] + p_ospecs,
        out_shape=[jax.ShapeDtypeStruct((T, D), F32), jax.ShapeDtypeStruct((1, D), F32)] + p_oshapes,
        input_output_aliases=p_alias,
        scratch_shapes=[pltpu.VMEM((8, D), F32)] + p_scratch,
        compiler_params=_params(("arbitrary",)),
    )(*pieces, w_t, x2, dxn2, g_pre, *p_args)
    return res[0], res[1], list(res[2:])


def _step(x, target, g_pre, g_post, lb_param, g_head, sinks, shards=None, full=None):
    B, S, D = x.shape
    T = B * S
    dist = shards is not None
    first, last = 0, DEPTH - 1
    if dist:
        a_loc, b_loc = shards
        ra, rb = a_loc.shape[1], b_loc.shape[1]
        side = _own_side_blocks()
        placed = lambda loc, nm: _place_own(loc, side, "place_" + nm)
        w_in0 = _gather_one_call(a_loc[0], placed(a_loc[0], "in0"), "gather_in0")
        w_in, w_out = [w_in0, None], [None, None]
    else:
        w_in, w_out = list(full[0]), list(full[1])
    cos, sin = _rope_tables(S)
    saved = []
    xs = x
    loss_part = None
    dxn = None
    for l in range(DEPTH):
        x2 = xs.reshape(T, D)
        proj_h, proj_a, h = _inproj(x2, g_pre[l:l + 1], w_in[l], f"inproj{l}")
        proj_h = proj_h.reshape(B, S, N_H)
        proj_a = proj_a.reshape(B, S, N_A)
        phase = None
        if dist and l == first:
            phase = _gather_ici_phase([a_loc[1], b_loc[0]], [placed(a_loc[1], "in1"), placed(b_loc[0], "out0")])
        if dist and l == last:
            phase = _gather_d2d_phase([w_out1_part], [rb])
        o_h, u, states, got = _hgrn_fwd(proj_h, MIX_WIDTH, lb_param, g_head[l:l + 1], l, f"hgrn_fwd{l}", phase)
        phase = None
        if dist and l == first:
            phase = _merge_phases(_gather_d2d_phase(got, [ra, rb]),
                                  _gather_ici_phase([b_loc[1]], [placed(b_loc[1], "out1")]))
        if dist and l == last:
            w_out[1] = got[0]
        u, kept_a, got = _attn_fwd(proj_a, u, _sink_rows(sinks[l]), cos, sin, f"attn_fwd{l}", phase)
        if dist and l == first:
            w_in[1], w_out[0], w_out1_part = got
        u2 = u.reshape(T, MIX_WIDTH)
        if l < last:
            y, xn = _outproj_fwd(u2, w_out[l], x2, g_post[l:l + 1], None, f"outproj{l}")
            xn = xn.reshape(B, S, D)
        else:
            y, dxn, loss_part = _outproj_fwd(u2, w_out[l], x2, g_post[l:l + 1], target.reshape(T, D), f"outproj{l}")
            xn = None
        saved.append((x2, h, proj_h, proj_a, o_h, u2, states, kept_a, y))
        xs = xn

    dw_in, dw_out = [None] * DEPTH, [None] * DEPTH
    dg_pre, dg_post, dlb, dg_head, dsinks = [], [], [], [], []
    for l in reversed(range(DEPTH)):
        x2, h, proj_h, proj_a, o_h, u2, states, kept_a, y = saved[l]
        dgp, du, dw_out[l] = _outproj_bwd(dxn, y, g_post[l:l + 1], w_out[l], u2, f"outproj_bwd{l}")
        du = du.reshape(B, S, MIX_WIDTH)
        phase = None
        if dist:
            phase = _reduce_d2d_phase([dw_out[l]], [rb])
            if l == first:
                phase = _merge_phases(_reduce_ici_phase([part_in1]), phase)
        dqh, dfh, dih, dzh, dlb_l, dgh, got = _hgrn_bwd(
            proj_h, o_h, du, states, lb_param, g_head[l:l + 1], l, f"hgrn_bwd{l}", phase)
        if dist:
            if l == first:
                sum_in = _chip_sum(part_in1, got[0], "chip_sum_in1", 1)
            part_out = _pair_sum(dw_out[l], got[-1], side, f"pair_sum_out{l}")
        dqa, dkv, dza, dsk, got = _attn_bwd(proj_a, du, kept_a, cos, sin, f"attn_bwd{l}",
                                            _reduce_ici_phase([part_out]) if dist else None)
        if dist:
            sum_out = _chip_sum(part_out, got[0], f"chip_sum_out{l}", l, None if l == last else sum_out)
        dproj = [p.reshape(T, p.shape[-1]) for p in (dqh, dfh, dih, dzh, dqa, dkv, dza)]
        dw_in[l] = _mm_tn(dproj, h, f"wgrad_in{l}")
        phase = None
        if dist and l == last:
            phase = _reduce_d2d_phase([dw_in[l]], [ra])
        if dist and l == first:
            got = _run_phase(_reduce_d2d_phase([dw_in[l]], [ra]), "reduce_in0_d2d")
            part_in0 = _pair_sum(dw_in[l], got[0], side, "pair_sum_in0")
            phase = _reduce_ici_phase([part_in0])
        dxn, dgpre, got = _inproj_bwd(dproj, w_in[l], x2, dxn, g_pre[l:l + 1], f"inproj_bwd{l}", phase)
        if dist and l == last:
            part_in1 = _pair_sum(dw_in[l], got[0], side, "pair_sum_in1")
        if dist and l == first:
            sum_in = _chip_sum(part_in0, got[0], "chip_sum_in0", 0, sum_in)
        dg_pre.append(dgpre)
        dg_post.append(dgp)
        dlb.append(dlb_l)
        dg_head.append(dgh)
        dsinks.append(dsk)
    rev = lambda lst: jnp.concatenate(lst[::-1], axis=0)
    if not dist:
        sum_in, sum_out = jnp.stack(dw_in), jnp.stack(dw_out)
    return (loss_part, dxn.reshape(B, S, D), sum_in, sum_out,
            rev(dg_pre), rev(dg_post), rev(dlb), rev(dg_head), rev(dsinks))


def _me_and_peers():
    x, y, c = lax.axis_index("x"), lax.axis_index("y"), lax.axis_index("c")
    me = 4 * x + 2 * y + c
    peers = []
    for k in range(1, N_DEV):
        px = 1 - x if k & 4 else x
        py = 1 - y if k & 2 else y
        pc = 1 - c if k & 1 else c
        peers.append(((px, py, pc), 4 * px + 2 * py + pc))
    return me, peers


class _Phase:
    def __init__(self, arrays, out_shapes, aliases, n_send, build):
        self.arrays, self.out_shapes, self.aliases = list(arrays), list(out_shapes), dict(aliases)
        self.n_send, self.build = n_send, build

    def scratch(self):
        return [pltpu.SemaphoreType.DMA((self.n_send,)), pltpu.SemaphoreType.DMA((self.n_send,))]

    def _copies(self, in_refs, out_refs, sems, arrivals):
        send_sems, recv_sems = sems
        sends, recvs = self.build(in_refs, out_refs)
        assert len(sends) == self.n_send == len(recvs)
        out = [pltpu.make_async_remote_copy(src_ref=s, dst_ref=d, send_sem=send_sems.at[i], recv_sem=recv_sems.at[i],
                                            device_id=dev, device_id_type=MESH) for i, (s, d, dev) in enumerate(sends)]
        inc = [pltpu.make_async_remote_copy(src_ref=s, dst_ref=r, send_sem=send_sems.at[i], recv_sem=recv_sems.at[i],
                                            device_id=dev, device_id_type=MESH)
               for i, ((s, _, dev), r) in enumerate(zip(sends, recvs))] if arrivals else []
        return out, inc

    def start(self, in_refs, out_refs, sems):
        out, _ = self._copies(in_refs, out_refs, sems, False)
        for cp in out:
            cp.start()

    def finish(self, in_refs, out_refs, sems):
        out, inc = self._copies(in_refs, out_refs, sems, True)
        for cp in inc:
            cp.wait_recv()
        for cp in out:
            cp.wait_send()


_ANY = pl.BlockSpec(memory_space=pl.ANY)


def _host_phase(phase, n_in, n_out):
    if phase is None:
        return [], [], [], {}, [], []
    aliases = {n_in + i: n_out + o for i, o in phase.aliases.items()}
    return ([_ANY] * len(phase.arrays), [_ANY] * len(phase.out_shapes), phase.out_shapes, aliases, phase.scratch(),
            phase.arrays)


def _split_refs(refs, n_in, n_out, n_scr, phase):
    pi = len(phase.arrays) if phase else 0
    po = len(phase.out_shapes) if phase else 0
    a = n_in + pi
    b = a + n_out + po
    return (refs[:n_in], refs[a:a + n_out], refs[b:b + n_scr], refs[n_in:a], refs[a + n_out:b], refs[b + n_scr:])


def _hosted_start(phase, p_in, p_out, p_sems, first):
    if phase is not None:
        @pl.when(first)
        def _():
            phase.start(p_in, p_out, p_sems)


def _hosted_finish(phase, p_in, p_out, p_sems, last):
    if phase is not None:
        @pl.when(last)
        def _():
            phase.finish(p_in, p_out, p_sems)


def _run_phase(phase, name):
    n_in, n_out = len(phase.arrays), len(phase.out_shapes)

    def body(*refs):
        phase.start(refs[:n_in], refs[n_in:n_in + n_out], refs[n_in + n_out:])
        phase.finish(refs[:n_in], refs[n_in:n_in + n_out], refs[n_in + n_out:])

    return pl.pallas_call(
        body, name=name, in_specs=[_ANY] * n_in, out_specs=[_ANY] * n_out,
        out_shape=phase.out_shapes, input_output_aliases=phase.aliases, scratch_shapes=phase.scratch(),
        compiler_params=pltpu.CompilerParams(has_side_effects=True),
    )(*phase.arrays)


def _gather_one_call(loc, full, name):
    r = loc.shape[0]
    half = r // 2

    def body(loc_ref, full_in, full_ref, send_sems, recv_sems):
        del full_in
        c, (own, xn, yn, dg), num = _mesh_place()
        me, sib = num(own, c), (*own, 1 - c)

        def blk(dev, part=None):
            start, n = (dev * r, r) if part is None else (dev * r + part * half, half)
            return full_ref.at[pl.ds(pl.multiple_of(start, 16), n), :]

        def copy(k, src, dev, to, part=None):
            return pltpu.make_async_remote_copy(src_ref=src, dst_ref=blk(dev, part),
                                                send_sem=send_sems.at[k], recv_sem=recv_sems.at[k],
                                                device_id=to, device_id_type=MESH)

        def landed(k, dev, part=None):
            copy(k, blk(dev, part), dev, sib, part).wait_recv()

        sent = []

        def start(*cps):
            for cp in cps:
                cp.start()
                sent.append(cp)

        xs, ys, ds = num(xn, c), num(yn, c), num(dg, c)
        start(copy(0, loc_ref, me, sib), copy(1, loc_ref, me, (*xn, c)), copy(2, loc_ref, me, (*yn, c)))
        landed(1, xs)
        start(copy(3, blk(xs, 0), xs, (*yn, c), 0), copy(5, blk(xs), xs, sib))
        landed(2, ys)
        start(copy(4, blk(ys, 1), ys, (*xn, c), 1), copy(6, blk(ys), ys, sib))
        landed(3, ds, 0)
        landed(4, ds, 1)
        start(copy(7, blk(ds), ds, sib))
        landed(0, num(own, 1 - c))
        for k, ch in ((5, xn), (6, yn), (7, dg)):
            landed(k, num(ch, 1 - c))
        for cp in sent:
            cp.wait_send()

    assert half % 16 == 0
    return pl.pallas_call(
        body, name=name, in_specs=[_ANY, _ANY], out_specs=_ANY,
        out_shape=jax.ShapeDtypeStruct(full.shape, full.dtype), input_output_aliases={1: 0},
        scratch_shapes=[pltpu.SemaphoreType.DMA((8,)), pltpu.SemaphoreType.DMA((8,))],
        compiler_params=pltpu.CompilerParams(has_side_effects=True),
    )(loc, full)


def _merge_phases(a, b):
    n_in, n_out = len(a.arrays), len(a.out_shapes)
    aliases = dict(a.aliases)
    aliases.update({n_in + i: n_out + o for i, o in b.aliases.items()})

    def build(ins, outs):
        sa, ra = a.build(ins[:n_in], outs[:n_out])
        sb, rb = b.build(ins[n_in:], outs[n_out:])
        return sa + sb, ra + rb

    return _Phase(a.arrays + b.arrays, a.out_shapes + b.out_shapes, aliases, a.n_send + b.n_send, build)


def _mesh_place():
    x, y, c = lax.axis_index("x"), lax.axis_index("y"), lax.axis_index("c")
    chips = [(x, y), (1 - x, y), (x, 1 - y), (1 - x, 1 - y)]
    num = lambda chip, core: 4 * chip[0] + 2 * chip[1] + core
    return c, chips, num


def _own_side_blocks():
    c, chips, num = _mesh_place()
    return jnp.stack([num(ch, c) for ch in chips]).astype(jnp.int32)


def _rows(ref, r, dev):
    return ref.at[pl.ds(pl.multiple_of(dev * r, 16), r), :]


def _place_own(loc, blocks, name):
    r, D = loc.shape
    tr = _pick(r, (400, 256, 200, 128, 64, 16))

    def body(idx_ref, l_ref, o_ref):
        del idx_ref
        o_ref[...] = l_ref[...]

    return pl.pallas_call(
        body, name=name,
        grid_spec=pltpu.PrefetchScalarGridSpec(
            num_scalar_prefetch=1, grid=(r // tr,),
            in_specs=[pl.BlockSpec((tr, D), lambda i, idx: (i, 0))],
            out_specs=pl.BlockSpec((tr, D), lambda i, idx: (idx[0] * (r // tr) + i, 0))),
        out_shape=jax.ShapeDtypeStruct((N_DEV * r, D), loc.dtype),
        compiler_params=_params(("arbitrary",)),
    )(blocks, loc)


def _gather_ici_phase(locs, fulls):
    rs = [a.shape[0] for a in locs]
    n = len(locs)

    def build(ins, outs):
        c, chips, num = _mesh_place()
        me = num(chips[0], c)
        targets = [((*chips[0], 1 - c), num(chips[0], 1 - c))] + [((*ch, c), num(ch, c)) for ch in chips[1:]]
        sends, recvs = [], []
        for dev, dnum in targets:
            for i, r in enumerate(rs):
                sends.append((ins[i], _rows(outs[i], r, me), dev))
                recvs.append(_rows(outs[i], r, dnum))
        return sends, recvs

    shapes = [jax.ShapeDtypeStruct(a.shape, a.dtype) for a in fulls]
    return _Phase(list(locs) + list(fulls), shapes, {n + i: i for i in range(n)}, 4 * n, build)


def _gather_d2d_phase(fulls, rs):
    def build(ins, outs):
        c, chips, num = _mesh_place()
        sib = (*chips[0], 1 - c)
        sends, recvs = [], []
        for ch in chips[1:]:
            for i, r in enumerate(rs):
                blk = _rows(outs[i], r, num(ch, c))
                sends.append((blk, blk, sib))
                recvs.append(_rows(outs[i], r, num(ch, 1 - c)))
        return sends, recvs

    shapes = [jax.ShapeDtypeStruct(a.shape, a.dtype) for a in fulls]
    return _Phase(fulls, shapes, {i: i for i in range(len(fulls))}, 3 * len(fulls), build)


def _reduce_d2d_phase(grads, rs):
    def build(ins, outs):
        c, chips, num = _mesh_place()
        sib = (*chips[0], 1 - c)
        sends, recvs = [], []
        for j, ch in enumerate(chips):
            for i, r in enumerate(rs):
                sends.append((_rows(ins[i], r, num(ch, 1 - c)), outs[i].at[j], sib))
                recvs.append(outs[i].at[j])
        return sends, recvs

    shapes = [jax.ShapeDtypeStruct((4, r, g.shape[1]), g.dtype) for g, r in zip(grads, rs)]
    return _Phase(grads, shapes, {}, 4 * len(grads), build)


def _reduce_ici_phase(parts):
    def build(ins, outs):
        c, chips, _ = _mesh_place()
        sends, recvs = [], []
        for t in range(1, 4):
            for i in range(len(parts)):
                sends.append((ins[i].at[t], outs[i].at[t - 1], (*chips[t], c)))
                recvs.append(outs[i].at[t - 1])
        return sends, recvs

    shapes = [jax.ShapeDtypeStruct((3,) + p.shape[1:], p.dtype) for p in parts]
    return _Phase(parts, shapes, {}, 3 * len(parts), build)


def _pair_sum(g, got, blocks, name):
    n, r, D = got.shape
    tr = _pick(r, (800, 400, 256, 200, 128, 64, 16))

    def body(idx_ref, g_ref, r_ref, o_ref):
        del idx_ref
        o_ref[...] = (g_ref[...].astype(F32) + r_ref[...].astype(F32)).astype(o_ref.dtype)

    blk = pl.BlockSpec((None, tr, D), lambda j, i, idx: (j, i, 0))
    return pl.pallas_call(
        body, name=name,
        grid_spec=pltpu.PrefetchScalarGridSpec(
            num_scalar_prefetch=1, grid=(n, r // tr),
            in_specs=[pl.BlockSpec((tr, D), lambda j, i, idx: (idx[j] * (r // tr) + i, 0)), blk],
            out_specs=blk),
        out_shape=jax.ShapeDtypeStruct(got.shape, got.dtype),
        compiler_params=_params(("arbitrary", "arbitrary")),
    )(blocks, g, got)


def _chip_sum(p, r, name, layer, into=None):
    _, R, D = p.shape
    tr = _pick(R, (800, 400, 256, 200, 128, 64, 16))

    def body(p_ref, r_ref, *rest):
        acc = p_ref[...].astype(F32)
        for t in range(3):
            acc = acc + r_ref[t].astype(F32)
        rest[-1][...] = acc

    args = [p, r] + ([] if into is None else [into])
    return pl.pallas_call(
        body, name=name, grid=(R // tr,),
        in_specs=[pl.BlockSpec((None, tr, D), lambda i: (0, i, 0)), pl.BlockSpec((3, tr, D), lambda i: (0, i, 0))]
        + ([] if into is None else [_ANY]),
        out_specs=pl.BlockSpec((None, tr, D), lambda i: (layer, i, 0)),
        out_shape=jax.ShapeDtypeStruct((DEPTH, R, D), F32),
        input_output_aliases={} if into is None else {2: 0},
        compiler_params=_params(("parallel",)))(*args)


def _allreduce_small(vec):
    R, C = vec.shape

    def body(v_ref, o_ref, buf, send_sems, recv_sems):
        me, peers = _me_and_peers()
        buf[me] = v_ref[...]
        sends = []
        for k, (pid, _) in enumerate(peers):
            cp = pltpu.make_async_remote_copy(src_ref=v_ref, dst_ref=buf.at[me], send_sem=send_sems.at[k],
                                              recv_sem=recv_sems.at[k], device_id=pid, device_id_type=MESH)
            cp.start()
            sends.append(cp)
        for k, (pid, pnum) in enumerate(peers):
            pltpu.make_async_remote_copy(src_ref=v_ref, dst_ref=buf.at[pnum], send_sem=send_sems.at[k],
                                         recv_sem=recv_sems.at[k], device_id=pid, device_id_type=MESH).wait_recv()
        for cp in sends:
            cp.wait_send()
        acc = buf[0]
        for d in range(1, N_DEV):
            acc = acc + buf[d]
        o_ref[...] = acc

    vm = pl.BlockSpec(memory_space=pltpu.VMEM)
    return pl.pallas_call(
        body, name="allreduce_small",
        in_specs=[vm], out_specs=vm,
        out_shape=jax.ShapeDtypeStruct((R, C), F32),
        scratch_shapes=[pltpu.VMEM((N_DEV, R, C), F32), pltpu.SemaphoreType.DMA((N_DEV - 1,)),
                        pltpu.SemaphoreType.DMA((N_DEV - 1,))],
        compiler_params=pltpu.CompilerParams(has_side_effects=True),
    )(vec)


def _adamw(w, g, m, v, name):
    R, C = w.shape
    tr = _pick(R, (512, 400, 256, 128, 64, 32, 16, 8)) if R >= 8 else R
    c1 = 1.0 - ADAM_B1 ** ADAM_STEP
    c2 = 1.0 - ADAM_B2 ** ADAM_STEP

    def body(w_ref, g_ref, m_ref, v_ref, d_ref, mo_ref, vo_ref):
        gg = g_ref[...]
        mn = ADAM_B1 * m_ref[...] + (1.0 - ADAM_B1) * gg
        vn = ADAM_B2 * v_ref[...] + (1.0 - ADAM_B2) * (gg * gg)
        d_ref[...] = -ADAM_LR * ((mn / c1) / (jnp.sqrt(vn / c2) + ADAM_EPS) + ADAM_WD * w_ref[...])
        mo_ref[...] = mn
        vo_ref[...] = vn

    blk = pl.BlockSpec((tr, C), lambda i: (i, 0))
    sh = jax.ShapeDtypeStruct((R, C), F32)
    return pl.pallas_call(
        body, name=name, grid=(R // tr,), in_specs=[blk] * 4, out_specs=[blk] * 3, out_shape=[sh] * 3,
        compiler_params=_params(("parallel",)),
    )(w, g, m, v)


def _lb_param_grad(lb_param, dlb):
    L, C = lb_param.shape

    def body(p_ref, d_ref, o_ref):
        lbp = p_ref[...]
        d = d_ref[...]
        mx = jnp.max(lbp, axis=0, keepdims=True)
        e = jnp.exp(lbp - mx)
        p = e / jnp.sum(e, axis=0, keepdims=True)
        tot = jnp.sum(d, axis=0, keepdims=True)
        dps = []
        rest = tot
        for j in range(L):
            dps.append(rest - tot if j == 0 else rest)
            rest = rest - d[j:j + 1]
        dp = jnp.concatenate(dps, axis=0)
        o_ref[...] = p * (dp - jnp.sum(p * dp, axis=0, keepdims=True))

    vm = pl.BlockSpec(memory_space=pltpu.VMEM)
    return pl.pallas_call(body, name="lb_param_grad", in_specs=[vm, vm], out_specs=vm,
                          out_shape=jax.ShapeDtypeStruct((L, C), F32))(lb_param, dlb)


def _pack_small(loss_part, dg_pre, dg_post, dlb, dg_head, dsinks):
    pad8 = lambda a: jnp.pad(a.reshape(-1, 128), ((0, 8 - DEPTH), (0, 0)))
    rows = [dg_pre.reshape(-1, 128), dg_post.reshape(-1, 128), dlb.reshape(-1, 128), pad8(dg_head), pad8(dsinks),
            loss_part]
    return jnp.concatenate(rows, axis=0)


def _unpack_small(vec):
    n = DEPTH * D_MODEL // 128
    o = 0
    dg_pre = vec[o:o + n].reshape(DEPTH, D_MODEL); o += n
    dg_post = vec[o:o + n].reshape(DEPTH, D_MODEL); o += n
    dlb = vec[o:o + n].reshape(DEPTH, HG_WIDTH); o += n
    dg_head = vec[o:o + DEPTH]; o += 8
    dsinks = vec[o:o + DEPTH, :ATT_HEADS]; o += 8
    loss = jnp.sum(vec[o:o + 8])
    return loss, dg_pre, dg_post, dlb, dg_head, dsinks


def kernel(x, w_in, w_out, g_pre, g_post, lb_param, g_head, sinks, loss_target, m_w_in, m_w_out, m_g_pre, m_g_post, m_lb_param, m_g_head, m_sinks, v_w_in, v_w_out, v_g_pre, v_g_post, v_lb_param, v_g_head, v_sinks):
    tr = lambda a: jnp.swapaxes(a, 1, 2)
    w_in_t = tr(w_in)
    (loss_part, dx, gw_in_t, gw_out, dg_pre, dg_post, dlb, dg_head, dsinks) = _step(
        x, loss_target, g_pre, g_post, lb_param, g_head, sinks, shards=(w_in_t.astype(BF16), w_out.astype(BF16)))

    small = _allreduce_small(_pack_small(loss_part, dg_pre, dg_post, dlb, dg_head, dsinks))
    loss, gg_pre, gg_post, gdlb, gg_head, gsinks = _unpack_small(small)
    glb = _lb_param_grad(lb_param, gdlb)

    grads = [gw_in_t, gw_out, gg_pre, gg_post, glb, gg_head, gsinks]
    ws = [w_in_t, w_out, g_pre, g_post, lb_param, g_head, sinks]
    ms = [tr(m_w_in), m_w_out, m_g_pre, m_g_post, m_lb_param, m_g_head, m_sinks]
    vs = [tr(v_w_in), v_w_out, v_g_pre, v_g_post, v_lb_param, v_g_head, v_sinks]
    names = ["w_in", "w_out", "g_pre", "g_post", "lb_param", "g_head", "sinks"]
    deltas, new_m, new_v = [], [], []
    for w, g, m, v, nm in zip(ws, grads, ms, vs, names):
        sh = w.shape
        two = lambda a: a.reshape(-1, sh[-1])
        d, mn, vn = _adamw(two(w), two(g), two(m), two(v), "adamw_" + nm)
        deltas.append(d.reshape(sh))
        new_m.append(mn.reshape(sh))
        new_v.append(vn.reshape(sh))
    grads[0], deltas[0], new_m[0], new_v[0] = tr(grads[0]), tr(deltas[0]), tr(new_m[0]), tr(new_v[0])
    return (loss, dx, *grads, *deltas, *new_m, *new_v)
```

```python
import math

import numpy as np
import jax
import jax.numpy as jnp
from jax import lax
from jax.experimental import pallas as pl
from jax.experimental.pallas import tpu as pltpu

F32 = jnp.float32
BF16 = jnp.bfloat16

D_MODEL = 1024
DEPTH = 2
HG_HEADS = 8
HG_DIM = 128
HG_WIDTH = HG_HEADS * HG_DIM
CHUNK = 64
ATT_HEADS = 16
ATT_DIM = 64
ATT_WIDTH = ATT_HEADS * ATT_DIM
KV_WIDTH = 128
ATT_BLOCK = 128
ATT_SCALE = 1.0 / math.sqrt(ATT_DIM)
ROPE_THETA = 10000.0
NORM_EPS = 1e-6
NEG_INF = -1e30
LB_FLOOR = 1e-20
N_H = 4 * HG_WIDTH
N_A = 2 * ATT_WIDTH + 2 * KV_WIDTH
IN_WIDTH = N_H + N_A
MIX_WIDTH = HG_WIDTH + ATT_WIDTH

ADAM_LR = 0.001
ADAM_B1 = 0.9
ADAM_B2 = 0.999
ADAM_EPS = 1e-08
ADAM_WD = 0.01
ADAM_STEP = 10

N_DEV = 8
MESH = pl.DeviceIdType.MESH
VMEM_LIMIT = 56 * 1024 * 1024

NN = ((1,), (0,))
NT = ((1,), (1,))
TN = ((0,), (0,))


def _dot(a, b, dims):
    return lax.dot_general(a.astype(BF16), b.astype(BF16), (dims, ((), ())), preferred_element_type=F32)


def _params(sem=None, **kw):
    return pltpu.CompilerParams(dimension_semantics=sem, vmem_limit_bytes=VMEM_LIMIT, **kw)


def _sigmoids(x):
    e = jnp.exp(-jnp.abs(x))
    r = 1.0 / (1.0 + e)
    er = e * r
    pos = x >= 0.0
    return jnp.where(pos, r, er), jnp.where(pos, er, r)


def _silu(x):
    return x * _sigmoids(x)[0]


def _silu_and_grad(x):
    s, ns = _sigmoids(x)
    return x * s, s * (1.0 + x * ns)


def _pick(n, prefs):
    for p in prefs:
        if n % p == 0:
            return p
    return n


def _inproj(x2, g, w, name):
    T, D = x2.shape
    tm = _pick(T, (512, 256, 128))
    nchunk = 1024

    def body(x_ref, g_ref, w_ref, oh_ref, oa_ref, h_ref):
        x = x_ref[...]
        r = lax.rsqrt(jnp.mean(x * x, axis=-1, keepdims=True) + NORM_EPS)
        h = ((x * r) * g_ref[...]).astype(BF16)
        h_ref[...] = h
        for j in range(0, N_H, nchunk):
            oh_ref[:, j:j + nchunk] = lax.dot_general(h, w_ref[j:j + nchunk, :], (NT, ((), ())),
                                                      preferred_element_type=F32)
        for j in range(0, N_A, N_A // 2):
            oa_ref[:, j:j + N_A // 2] = lax.dot_general(h, w_ref[N_H + j:N_H + j + N_A // 2, :], (NT, ((), ())),
                                                        preferred_element_type=F32)

    row = lambda w_: pl.BlockSpec((tm, w_), lambda i: (i, 0))
    return pl.pallas_call(
        body, name=name,
        grid=(T // tm,),
        in_specs=[row(D), pl.BlockSpec((1, D), lambda i: (0, 0)),
                  pl.BlockSpec((IN_WIDTH, D), lambda i: (0, 0), pipeline_mode=pl.Buffered(1))],
        out_specs=[row(N_H), row(N_A), row(D)],
        out_shape=[jax.ShapeDtypeStruct((T, N_H), F32), jax.ShapeDtypeStruct((T, N_A), F32),
                   jax.ShapeDtypeStruct((T, D), BF16)],
        compiler_params=_params(("parallel",)),
    )(x2, g, w)


def _mm_tn(pieces, b, name, out_dtype=BF16):
    T, m = b.shape
    tn = 256
    counts = [p.shape[1] // tn for p in pieces]
    starts = [sum(counts[:i]) for i in range(len(pieces))]
    n_p = len(pieces)

    def body(*refs):
        b_ref, o_ref = refs[n_p], refs[n_p + 1]
        i = pl.program_id(0)
        for p in range(n_p):
            @pl.when((i >= starts[p]) & (i < starts[p] + counts[p]))
            def _(p=p):
                o_ref[...] = lax.dot_general(refs[p][...], b_ref[...], (TN, ((), ())),
                                             preferred_element_type=F32).astype(out_dtype)

    piece_spec = lambda s, c: pl.BlockSpec((T, tn), lambda i: (0, jnp.clip(i - s, 0, c - 1)))
    return pl.pallas_call(
        body, name=name,
        grid=(sum(counts),),
        in_specs=[piece_spec(s, c) for s, c in zip(starts, counts)]
        + [pl.BlockSpec((T, m), lambda i: (0, 0), pipeline_mode=pl.Buffered(1))],
        out_specs=pl.BlockSpec((tn, m), lambda i: (i, 0)),
        out_shape=jax.ShapeDtypeStruct((sum(counts) * tn, m), out_dtype),
        compiler_params=_params(("arbitrary",)),
    )(*pieces, b)


_LEVELS = (0, 1, 2, 4, 8, 16, 32)
_CUM_L = (2, 4, 8, 16, 32, 64)
_ALL_KINDS = tuple(("c", L) for L in _CUM_L) + tuple(("r", L) for L in _CUM_L)
_MXU_KINDS = (("c", 2), ("c", 4), ("c", CHUNK), ("r", 2), ("r", 4))
N_CUM = len(_ALL_KINDS) * CHUNK
N_CUM_F = len(_MXU_KINDS) * CHUNK


def _cum_matrices():
    t = np.arange(CHUNK)[:, None]
    r = np.arange(CHUNK)[None, :]

    def mat(kind):
        c, L = kind
        return ((r // L == t // L) & ((r <= t) if c == "c" else (r > t))).astype(np.float32)

    fwd = np.concatenate([mat(kd) for kd in _MXU_KINDS], axis=0)
    full = np.concatenate([mat(kd) for kd in _ALL_KINDS], axis=0)
    return jnp.asarray(fwd, BF16), jnp.asarray(full.T.copy(), BF16)


def _level_masks():
    t = np.arange(CHUNK)[:, None]
    s = np.arange(CHUNK)[None, :]
    ms = []
    for L in _LEVELS:
        if L == 0:
            ms.append(t == s)
        else:
            ms.append((t // (2 * L) == s // (2 * L)) & ((t // L) % 2 == 1) & ((s // L) % 2 == 0))
    return jnp.asarray(np.stack(ms).astype(np.float32))


def _split3(x):
    hi = x.astype(BF16)
    r1 = x - hi.astype(F32)
    mid = r1.astype(BF16)
    lo = (r1 - mid.astype(F32)).astype(BF16)
    return hi, mid, lo


def _cum3(ts, x, terms=3):
    d = lambda p: lax.dot_general(ts, p, (NN, ((), ())), preferred_element_type=F32)
    return sum(d(p) for p in _split3(x)[:terms])


def _lb_terms(lbp, layer):
    mx = jnp.max(lbp, axis=0, keepdims=True)
    e = jnp.exp(lbp - mx)
    p = e / jnp.sum(e, axis=0, keepdims=True)
    cum = p[0:1]
    for j in range(1, layer + 1):
        cum = cum + p[j:j + 1]
    lb = cum - p[0:1]
    lbf = jnp.maximum(lb, LB_FLOOR)
    return dict(lbf=lbf, one_m=1.0 - lb, kcorr=lb - lbf, ind=jnp.where(lb > LB_FLOOR, 1.0, 0.0))


def _gate(x, lt):
    sig, nsig = _sigmoids(x)
    f = lt["lbf"] + lt["one_m"] * sig
    return jnp.log(f), lt["one_m"] * nsig + lt["kcorr"], f, sig, nsig


def _ck(x, ci):
    return x[ci * CHUNK:(ci + 1) * CHUNK]


def _block_cums(ts, g, nc):
    cs = [_cum3(ts, _ck(g, ci), terms=2) for ci in range(nc)]
    out = {kind: jnp.concatenate([c[CHUNK * i:CHUNK * (i + 1)] for c in cs], axis=0)
           for i, kind in enumerate(_MXU_KINDS)}
    b = out[("c", CHUNK)]
    ng = CHUNK // 8
    last = b.reshape(nc, ng, 8, HG_DIM)[:, :, 7:8, :]
    zero = jnp.zeros((nc, 1, 1, HG_DIM), F32)

    def spread(groups):
        return jnp.broadcast_to(jnp.concatenate(groups, axis=1), (nc, ng, 8, HG_DIM)).reshape(nc * CHUNK, HG_DIM)

    def get(kind):
        if kind in out:
            return out[kind]
        c, L = kind
        nb = L // 8
        first = lambda r: (r // nb) * nb
        if c == "c":
            return b - spread([last[:, first(r) - 1:first(r)] if r >= nb else zero for r in range(ng)])
        return spread([last[:, first(r) + nb - 1:first(r) + nb] for r in range(ng)]) - b

    return get


def _level_factors(cums, g, L):
    if L == 0:
        return None, None
    if L == 1:
        return jnp.exp(g), None
    return jnp.exp(cums(("c", L))), jnp.exp(cums(("r", L)))


def _mul(a, e):
    return a if e is None else a * e


def _hg_block_fwd(qf, k, v, g, ts, m_ref, nc):
    cums = _block_cums(ts, g, nc)
    amat = [jnp.zeros((CHUNK, CHUNK), F32)] * nc
    for li, L in enumerate(_LEVELS):
        eq, ek = _level_factors(cums, g, L)
        ql, kl, m = _mul(qf, eq), _mul(k, ek), m_ref[li]
        amat = [amat[ci] + _dot(_ck(ql, ci), _ck(kl, ci), NT) * m for ci in range(nc)]
    b = cums(("c", CHUNK))
    kst = k * jnp.exp(cums(("r", CHUNK)))
    o = [_dot(amat[ci], _ck(v, ci), NN) for ci in range(nc)]
    kv = [_dot(_ck(v, ci), _ck(kst, ci), TN) for ci in range(nc)]
    dec = [jnp.exp(b[(ci + 1) * CHUNK - 1:(ci + 1) * CHUNK, :]) for ci in range(nc)]
    return o, dec, kv, qf * jnp.exp(b), amat


def _hg_block_bwd(qf, k, v, g, do, amat, ts, m_ref, nc):
    cums = _block_cums(ts, g, nc)
    dcs = {}
    da = [_dot(_ck(do, ci), _ck(v, ci), NT) for ci in range(nc)]
    dq = jnp.zeros_like(qf)
    dk = jnp.zeros_like(qf)
    dg = jnp.zeros_like(qf)
    for li, L in enumerate(_LEVELS):
        eq, ek = _level_factors(cums, g, L)
        ql, kl, m = _mul(qf, eq), _mul(k, ek), m_ref[li]
        qlb, klb = ql.astype(BF16), kl.astype(BF16)
        dal = [(da[ci] * m).astype(BF16) for ci in range(nc)]
        dql = jnp.concatenate([_dot(dal[ci], _ck(klb, ci), NN) for ci in range(nc)], axis=0)
        dkl = jnp.concatenate([_dot(dal[ci], _ck(qlb, ci), TN) for ci in range(nc)], axis=0)
        dql, dkl = _mul(dql, eq), _mul(dkl, ek)
        dq = dq + dql
        dk = dk + dkl
        if L == 1:
            dg = dg + dql * qf
        elif L > 1:
            dcs[("c", L)] = (dql * qf).astype(BF16)
            dcs[("r", L)] = (dkl * k).astype(BF16)
    b = cums(("c", CHUNK))
    e64 = jnp.exp(b)
    er64 = jnp.exp(cums(("r", CHUNK)))
    qb = (qf * e64).astype(BF16)
    return dict(dq=dq, dk=dk, dg=dg, dcs=dcs, e64=e64, er64=er64, qf=qf, k=k, kst=(k * er64).astype(BF16),
                dv=[_dot(amat[ci], _ck(do, ci), TN) for ci in range(nc)],
                dec=[jnp.exp(b[(ci + 1) * CHUNK - 1:(ci + 1) * CHUNK, :]) for ci in range(nc)],
                qd=[_dot(_ck(do, ci), _ck(qb, ci), TN) for ci in range(nc)])


def _hg_state_bwd(w, v, do, starts, ends, tst, nc):
    dqb = jnp.concatenate([_dot(_ck(do, ci), starts[ci], NN) for ci in range(nc)], axis=0)
    dkst = jnp.concatenate([_dot(_ck(v, ci), ends[ci], NN) for ci in range(nc)], axis=0)
    dqb, dkst = dqb * w["e64"], dkst * w["er64"]
    dq = w["dq"] + dqb
    dk = w["dk"] + dkst
    dv = jnp.concatenate([w["dv"][ci] + _dot(_ck(w["kst"], ci), ends[ci], NT) for ci in range(nc)], axis=0)
    trow = lax.broadcasted_iota(jnp.int32, (CHUNK, 1), 0)
    dtot = jnp.concatenate(
        [jnp.where(trow == CHUNK - 1, jnp.sum(ends[ci] * starts[ci], axis=0, keepdims=True) * w["dec"][ci], 0.0)
         for ci in range(nc)], axis=0)
    dcs = dict(w["dcs"])
    dcs[("c", CHUNK)] = (dqb * w["qf"] + dtot).astype(BF16)
    dcs[("r", CHUNK)] = (dkst * w["k"]).astype(BF16)
    dgs = [_dot(tst, jnp.concatenate([_ck(dcs[kind], ci) for kind in _ALL_KINDS], axis=0), NN) for ci in range(nc)]
    return dq, dk, dv, w["dg"] + jnp.concatenate(dgs, axis=0)


def _hgrn_fwd(proj_h, u_rows, lb_param, g_head, layer, name, phase=None):
    B, S, _ = proj_h.shape
    sb = _pick(S, (2048, 1024, 512, 256, 128, 64))
    nc = sb // CHUNK
    ts, _ = _cum_matrices()

    def body(*refs):
        ins, outs, (st,), p_in, p_out, p_sems = _split_refs(refs, 8, 12, 1, phase)
        q_ref, f_ref, i_ref, z_ref, lbp_ref, gh_ref, ts_ref, m_ref = ins
        o_ref, u_ref, sts_ref, am_ref = outs[:4]
        logf_ref, k_ref, qf_ref, sg_ref, qg_ref, zg_ref, fg_ref, sig_ref = outs[4:]
        h_id, b_id, s_id = pl.program_id(0), pl.program_id(1), pl.program_id(2)
        _hosted_start(phase, p_in, p_out, p_sems, (h_id == 0) & (b_id == 0) & (s_id == 0))

        @pl.when(s_id == 0)
        def _():
            st[...] = jnp.zeros_like(st)

        lt = _lb_terms(lbp_ref[...], layer)
        tsv = ts_ref[...]
        gh = gh_ref[...]
        logf, k, _, sig, nsig = _gate(f_ref[...], lt)
        qf, qf_grad = _silu_and_grad(q_ref[...])
        sg, sg_grad = _silu_and_grad(z_ref[...])
        logf_ref[...], k_ref[...], qf_ref[...], sg_ref[...] = logf, k, qf, sg
        qg_ref[...] = qf_grad.astype(BF16)
        zg_ref[...] = sg_grad.astype(BF16)
        fg_ref[...] = (lt["one_m"] * sig * nsig).astype(BF16)
        sig_ref[...] = sig.astype(BF16)
        o_part, dec, kv, qb, amat = _hg_block_fwd(qf, k, i_ref[...], logf, tsv, m_ref, nc)
        for ci in range(nc):
            am_ref[ci] = amat[ci].astype(BF16)
        cur = st[...]
        starts = []
        for ci in range(nc):
            sts_ref[ci] = cur
            starts.append(cur)
            cur = cur * dec[ci] + kv[ci]
        st[...] = cur
        o = jnp.concatenate([o_part[ci] + _dot(_ck(qb, ci), starts[ci], NT) for ci in range(nc)], axis=0)
        o_ref[...] = o
        r = lax.rsqrt(jnp.mean(o * o, axis=-1, keepdims=True) + NORM_EPS)
        u_ref[...] = (((o * r) * gh) * sg).astype(BF16)
        _hosted_finish(phase, p_in, p_out, p_sems, (h_id == HG_HEADS - 1) & (b_id == B - 1) & (s_id == S // sb - 1))

    col = lambda base: pl.BlockSpec((None, sb, HG_DIM), lambda h, b, s: (b, s, base + h))
    p_ispecs, p_ospecs, p_oshapes, p_alias, p_scratch, p_args = _host_phase(phase, 8, 12)
    wide = lambda dt: jax.ShapeDtypeStruct((B, S, HG_WIDTH), dt)
    res = pl.pallas_call(
        body, name=name,
        grid=(HG_HEADS, B, S // sb),
        in_specs=[col(0), col(HG_HEADS), col(2 * HG_HEADS), col(3 * HG_HEADS),
                  pl.BlockSpec((DEPTH, HG_DIM), lambda h, b, s: (0, h)),
                  pl.BlockSpec((1, HG_DIM), lambda h, b, s: (0, 0)),
                  pl.BlockSpec((N_CUM_F, CHUNK), lambda h, b, s: (0, 0)),
                  pl.BlockSpec((len(_LEVELS), CHUNK, CHUNK), lambda h, b, s: (0, 0, 0))] + p_ispecs,
        out_specs=[col(0), col(0),
                   pl.BlockSpec((None, None, nc, HG_DIM, HG_DIM), lambda h, b, s: (b, h, s, 0, 0)),
                   pl.BlockSpec((None, None, nc, CHUNK, CHUNK), lambda h, b, s: (b, h, s, 0, 0))]
        + [col(0)] * 8 + p_ospecs,
        out_shape=[wide(F32),
                   jax.ShapeDtypeStruct((B, S, u_rows), BF16),
                   jax.ShapeDtypeStruct((B, HG_HEADS, S // CHUNK, HG_DIM, HG_DIM), F32),
                   jax.ShapeDtypeStruct((B, HG_HEADS, S // CHUNK, CHUNK, CHUNK), BF16)]
        + [wide(F32)] * 4 + [wide(BF16)] * 4 + p_oshapes,
        input_output_aliases=p_alias,
        scratch_shapes=[pltpu.VMEM((HG_DIM, HG_DIM), F32)] + p_scratch,
        compiler_params=_params(("arbitrary", "arbitrary", "arbitrary")),
    )(proj_h, proj_h, proj_h, proj_h, lb_param, g_head, ts, _level_masks(), *p_args)
    return res[0], res[1], tuple(res[2:12]), list(res[12:])


def _hgrn_bwd(proj_h, o_h, du, kept, lb_param, g_head, layer, name, phase=None):
    B, S, _ = proj_h.shape
    sb = _pick(S, (512, 256, 128, 64))
    nc = sb // CHUNK
    ns = S // sb
    ts, tst = _cum_matrices()

    def body(*refs):
        ins, outs, (dst,), p_in, p_out, p_sems = _split_refs(refs, 18, 6, 1, phase)
        (i_ref, o_ref, du_ref, sts_ref, am_ref, logf_ref, k_ref, qf_ref, sg_ref, qg_ref, zg_ref, fg_ref, sig_ref,
         lbp_ref, gh_ref, ts_ref, tst_ref, m_ref) = ins
        dq_ref, df_ref, di_ref, dz_ref, dlb_ref, dgh_ref = outs
        h_id, b_id, s_id = pl.program_id(0), pl.program_id(1), pl.program_id(2)
        _hosted_start(phase, p_in, p_out, p_sems, (h_id == 0) & (b_id == 0) & (s_id == 0))

        @pl.when(s_id == 0)
        def _():
            dst[...] = jnp.zeros_like(dst)

        @pl.when((b_id == 0) & (s_id == 0))
        def _():
            dlb_ref[...] = jnp.zeros_like(dlb_ref)

        @pl.when((h_id == 0) & (b_id == 0) & (s_id == 0))
        def _():
            dgh_ref[...] = jnp.zeros_like(dgh_ref)

        lt = _lb_terms(lbp_ref[...], layer)
        gh = gh_ref[...]
        tsv = ts_ref[...]
        tstv = tst_ref[...]
        logf, k, qf, sg = logf_ref[...], k_ref[...], qf_ref[...], sg_ref[...]
        o = o_ref[...]
        dub = du_ref[...]
        r = lax.rsqrt(jnp.mean(o * o, axis=-1, keepdims=True) + NORM_EPS)
        n = o * r
        dz_ref[...] = (dub * (n * gh) * zg_ref[...].astype(F32)).astype(BF16)
        dgh_ref[...] += jnp.sum(dub * sg * n, axis=0, keepdims=True)
        dn = dub * sg * gh
        do = (r * (dn - n * jnp.mean(dn * n, axis=-1, keepdims=True))).astype(BF16)
        v = i_ref[...].astype(BF16)
        w = _hg_block_bwd(qf, k, v, logf, do, [am_ref[ci] for ci in range(nc)], tsv, m_ref, nc)
        cur = dst[...]
        ends = [None] * nc
        for ci in reversed(range(nc)):
            ends[ci] = cur
            cur = cur * w["dec"][ci] + w["qd"][ci]
        dst[...] = cur
        dq, dk, dv, dg = _hg_state_bwd(w, v, do, [sts_ref[ci] for ci in range(nc)], ends, tstv, nc)
        di_ref[...] = dv.astype(BF16)
        dq_ref[...] = (dq * qg_ref[...].astype(F32)).astype(BF16)
        f = jnp.exp(logf)
        scaled = (dg - f * dk) / f
        df_ref[...] = (scaled * fg_ref[...].astype(F32)).astype(BF16)
        dlb_ref[...] += jnp.sum(scaled * (lt["ind"] - sig_ref[...].astype(F32)), axis=0, keepdims=True)
        _hosted_finish(phase, p_in, p_out, p_sems, (h_id == HG_HEADS - 1) & (b_id == B - 1) & (s_id == ns - 1))

    col = lambda base: pl.BlockSpec((None, sb, HG_DIM), lambda h, b, s: (b, ns - 1 - s, base + h))
    out_col = pl.BlockSpec((None, sb, HG_DIM), lambda h, b, s: (b, ns - 1 - s, h))
    dt = jax.ShapeDtypeStruct((B, S, HG_WIDTH), BF16)
    p_ispecs, p_ospecs, p_oshapes, p_alias, p_scratch, p_args = _host_phase(phase, 18, 6)
    res = pl.pallas_call(
        body, name=name,
        grid=(HG_HEADS, B, ns),
        in_specs=[col(2 * HG_HEADS), col(0), col(0),
                  pl.BlockSpec((None, None, nc, HG_DIM, HG_DIM), lambda h, b, s: (b, h, ns - 1 - s, 0, 0)),
                  pl.BlockSpec((None, None, nc, CHUNK, CHUNK), lambda h, b, s: (b, h, ns - 1 - s, 0, 0))]
        + [col(0)] * 8
        + [pl.BlockSpec((DEPTH, HG_DIM), lambda h, b, s: (0, h)),
           pl.BlockSpec((1, HG_DIM), lambda h, b, s: (0, 0)),
           pl.BlockSpec((N_CUM_F, CHUNK), lambda h, b, s: (0, 0)),
           pl.BlockSpec((CHUNK, N_CUM), lambda h, b, s: (0, 0)),
           pl.BlockSpec((len(_LEVELS), CHUNK, CHUNK), lambda h, b, s: (0, 0, 0))] + p_ispecs,
        out_specs=[out_col, out_col, out_col, out_col,
                   pl.BlockSpec((1, HG_DIM), lambda h, b, s: (0, h)),
                   pl.BlockSpec((1, HG_DIM), lambda h, b, s: (0, 0))] + p_ospecs,
        out_shape=[dt, dt, dt, dt, jax.ShapeDtypeStruct((1, HG_WIDTH), F32),
                   jax.ShapeDtypeStruct((1, HG_DIM), F32)] + p_oshapes,
        input_output_aliases=p_alias,
        scratch_shapes=[pltpu.VMEM((HG_DIM, HG_DIM), F32)] + p_scratch,
        compiler_params=_params(("arbitrary", "arbitrary", "arbitrary")),
    )(proj_h, o_h, du, *kept, lb_param, g_head, ts, tst, _level_masks(), *p_args)
    return tuple(res[:6]) + (list(res[6:]),)


def _rope_tables(S):
    half = ATT_DIM // 2
    inv_freq = np.float32(ROPE_THETA) ** (-np.arange(half, dtype=np.float32) / half)
    ang = np.arange(S, dtype=np.float32)[:, None] * inv_freq[None, :]
    cos = np.cos(ang)
    sin = np.sin(ang)
    cos = np.concatenate([cos, cos, cos, cos], axis=1)
    sin = np.concatenate([-sin, sin, -sin, sin], axis=1)
    return jnp.asarray(cos, F32), jnp.asarray(sin, F32)


def _attn_common():
    lane = lax.broadcasted_iota(jnp.int32, (1, 2 * ATT_DIM), 1)
    first_half = (lane % ATT_DIM) < (ATT_DIM // 2)
    left = lane < ATT_DIM

    def swap(x):
        return jnp.where(first_half, pltpu.roll(x, 128 - ATT_DIM // 2, 1), pltpu.roll(x, ATT_DIM // 2, 1))

    def rope(x, cos, sin):
        return x * cos + swap(x) * sin

    def rope_bwd(dy, cos, sin):
        return dy * cos + swap(dy * sin)

    def dup(x):
        xs = pltpu.roll(x, ATT_DIM, 1)
        return [jnp.where(left, x, xs), jnp.where(left, xs, x)]

    return left, rope, rope_bwd, dup


GROUP = ATT_HEADS // 2
GROUP_ROWS = GROUP * ATT_BLOCK


def _attn_bias(i):
    r = lax.broadcasted_iota(jnp.int32, (ATT_BLOCK, 2 * ATT_BLOCK), 0)
    c = lax.broadcasted_iota(jnp.int32, (ATT_BLOCK, 2 * ATT_BLOCK), 1)
    ok = (c > r) & (c <= r + ATT_BLOCK) & ((c >= ATT_BLOCK) | (i > 0))
    return jnp.where(ok, 0.0, NEG_INF)


def _stack_heads(pairs, left):
    rows = []
    for x in pairs:
        rows += [jnp.where(left, x, 0.0), jnp.where(left, 0.0, x)]
    return jnp.concatenate(rows, axis=0)


def _unstack_heads(y, left, pp):
    r0 = 2 * pp * ATT_BLOCK
    return jnp.where(left, y[r0:r0 + ATT_BLOCK], y[r0 + ATT_BLOCK:r0 + 2 * ATT_BLOCK])


def _row_sums(x):
    return _dot(x, jnp.ones((x.shape[1], 128), BF16), NN)


def _attn_probs(qs, kd, vd, sink, bias):
    n = range(len(qs))
    rows = qs[0].shape[0]
    s = [(_dot(qs[j], kd[j], NT).reshape(rows // ATT_BLOCK, ATT_BLOCK, 2 * ATT_BLOCK) * ATT_SCALE + bias[None])
         .reshape(rows, 2 * ATT_BLOCK) for j in n]
    m = [jnp.max(jnp.maximum(jnp.maximum(s[j][:, :128], s[j][:, 128:]), sink[j]), axis=-1, keepdims=True) for j in n]
    pu = [jnp.exp(s[j] - m[j]) for j in n]
    es = [jnp.exp(sink[j] - m[j]) for j in n]
    ones = jnp.ones((2 * ATT_BLOCK, 128), BF16)
    ov = [_dot(pu[j], jnp.concatenate([vd[j].astype(BF16), ones], axis=1), NN) for j in n]
    inv = [1.0 / (ov[j][:, 128:] + es[j]) for j in n]
    return ([pu[j] * jnp.concatenate([inv[j], inv[j]], axis=1) for j in n], [es[j] * inv[j] for j in n],
            [ov[j][:, :128] * inv[j] for j in n])


def _sink_rows(sinks_l):
    return jnp.broadcast_to(jnp.repeat(sinks_l, ATT_BLOCK)[:, None], (ATT_HEADS * ATT_BLOCK, 128))


_Z0 = (2 * ATT_WIDTH + 2 * KV_WIDTH - ATT_WIDTH) // 256


def _attn_fwd(proj_a, u, sinks_l, cos, sin, name, phase=None):
    B, S, _ = proj_a.shape
    nb = S // ATT_BLOCK

    def body(*refs):
        ins, (u_ref, p_ref, o_ref, ps_ref, qs_ref), _, p_in, p_out, p_sems = _split_refs(refs, 13, 5, 0, phase)
        q_ref, kvc_ref, kvp_ref, z0, z1, z2, z3, cos_ref, sin_ref, cosp_ref, sinp_ref, sinks_ref, _ = ins
        i = pl.program_id(1)
        _hosted_start(phase, p_in, p_out, p_sems, (pl.program_id(0) == 0) & (i == 0))
        left, rope, _, dup = _attn_common()
        cos_c, sin_c = cos_ref[...], sin_ref[...]
        kvc = kvc_ref[...]
        kvp = kvp_ref[...]
        kw = jnp.concatenate([rope(kvp[:, :KV_WIDTH], cosp_ref[...], sinp_ref[...]),
                              rope(kvc[:, :KV_WIDTH], cos_c, sin_c)], axis=0)
        vw = jnp.concatenate([kvp[:, KV_WIDTH:], kvc[:, KV_WIDTH:]], axis=0)
        kd, vd = dup(kw), dup(vw)
        bias = _attn_bias(i)
        zs = (z0, z1, z2, z3)
        pairs = [range(4 * kvh, 4 * kvh + 4) for kvh in range(2)]
        qs = [_stack_heads([rope(q_ref[:, 128 * pr:128 * (pr + 1)], cos_c, sin_c) for pr in pairs[kvh]], left)
              for kvh in range(2)]
        sink = [sinks_ref[kvh * GROUP_ROWS:(kvh + 1) * GROUP_ROWS, :] for kvh in range(2)]
        p, ps, o = _attn_probs(qs, kd, vd, sink, bias)
        eye = (lax.broadcasted_iota(jnp.int32, (ATT_BLOCK, 128), 0)
               == lax.broadcasted_iota(jnp.int32, (ATT_BLOCK, 128), 1))
        for kvh in range(2):
            p_ref[kvh] = p[kvh].astype(BF16)
            qs_ref[kvh] = qs[kvh].astype(BF16)
            for g in range(GROUP):
                blk = ps[kvh][g * ATT_BLOCK:(g + 1) * ATT_BLOCK, :]
                ps_ref[kvh * GROUP + g:kvh * GROUP + g + 1, :] = jnp.sum(jnp.where(eye, blk, 0.0), axis=0, keepdims=True)
            for pp, pr in enumerate(pairs[kvh]):
                z = zs[pr // 2][:, 128 * (pr % 2):128 * (pr % 2 + 1)]
                o128 = _unstack_heads(o[kvh], left, pp)
                o_ref[:, 128 * pr:128 * (pr + 1)] = o128.astype(BF16)
                u_ref[:, 128 * pr:128 * (pr + 1)] = (o128 * _silu(z)).astype(BF16)
        _hosted_finish(phase, p_in, p_out, p_sems, (pl.program_id(0) == B - 1) & (i == nb - 1))

    rowblk = lambda w, cb: pl.BlockSpec((None, ATT_BLOCK, w), lambda b, i: (b, i, cb))
    tab = pl.BlockSpec((ATT_BLOCK, 128), lambda b, i: (i, 0))
    tabp = pl.BlockSpec((ATT_BLOCK, 128), lambda b, i: (jnp.maximum(i - 1, 0), 0))
    p_ispecs, p_ospecs, p_oshapes, p_alias, p_scratch, p_args = _host_phase(phase, 13, 5)
    res = pl.pallas_call(
        body, name=name,
        grid=(B, nb),
        in_specs=[rowblk(ATT_WIDTH, 0), rowblk(256, 4),
                  pl.BlockSpec((None, ATT_BLOCK, 256), lambda b, i: (b, jnp.maximum(i - 1, 0), 4)),
                  rowblk(256, _Z0), rowblk(256, _Z0 + 1), rowblk(256, _Z0 + 2), rowblk(256, _Z0 + 3),
                  tab, tab, tabp, tabp,
                  pl.BlockSpec((ATT_HEADS * ATT_BLOCK, 128), lambda b, i: (0, 0)),
                  pl.BlockSpec(memory_space=pl.ANY)] + p_ispecs,
        out_specs=[pl.BlockSpec((None, ATT_BLOCK, ATT_WIDTH), lambda b, i: (b, i, 1)),
                   pl.BlockSpec((None, None, 2, GROUP_ROWS, 2 * ATT_BLOCK), lambda b, i: (b, i, 0, 0, 0)),
                   pl.BlockSpec((None, ATT_BLOCK, ATT_WIDTH), lambda b, i: (b, i, 0)),
                   pl.BlockSpec((None, None, ATT_HEADS, 128), lambda b, i: (b, i, 0, 0)),
                   pl.BlockSpec((None, None, 2, GROUP_ROWS, 128), lambda b, i: (b, i, 0, 0, 0))] + p_ospecs,
        out_shape=[jax.ShapeDtypeStruct(u.shape, BF16),
                   jax.ShapeDtypeStruct((B, nb, 2, GROUP_ROWS, 2 * ATT_BLOCK), BF16),
                   jax.ShapeDtypeStruct((B, S, ATT_WIDTH), BF16),
                   jax.ShapeDtypeStruct((B, nb, ATT_HEADS, 128), F32),
                   jax.ShapeDtypeStruct((B, nb, 2, GROUP_ROWS, 128), BF16)] + p_oshapes,
        input_output_aliases={12: 0, **p_alias},
        scratch_shapes=p_scratch,
        compiler_params=_params(("arbitrary", "arbitrary")),
    )(proj_a, proj_a, proj_a, proj_a, proj_a, proj_a, proj_a, cos, sin, cos, sin, sinks_l, u, *p_args)
    return res[0], tuple(res[1:5]), list(res[5:])


def _attn_bwd(proj_a, du, kept, cos, sin, name, phase=None):
    B, S, _ = proj_a.shape
    nb = S // ATT_BLOCK
    p_kept, o_kept, ps_kept, qs_kept = kept

    def body(*refs):
        ins, outs, (carry, sk_acc), p_in, p_out, p_sems = _split_refs(refs, 15, 4, 2, phase)
        (qs_ref, kvc_ref, kvp_ref, z0, z1, z2, z3, du_ref, cos_ref, sin_ref, cosp_ref, sinp_ref,
         p_ref, o_ref, ps_ref) = ins
        dq_ref, dkv_ref, dz_ref, dsk_ref = outs
        b_id, i = pl.program_id(0), pl.program_id(1)
        _hosted_start(phase, p_in, p_out, p_sems, (b_id == 0) & (i == 0))

        @pl.when((b_id == 0) & (i == 0))
        def _():
            sk_acc[...] = jnp.zeros_like(sk_acc)

        @pl.when(i == 0)
        def _():
            carry[...] = jnp.zeros_like(carry)

        @pl.when(i < nb)
        def _():
            left, rope, rope_bwd, dup = _attn_common()
            cos_c, sin_c = cos_ref[...], sin_ref[...]
            cos_p, sin_p = cosp_ref[...], sinp_ref[...]
            kvc = kvc_ref[...]
            kvp = kvp_ref[...]
            kw = jnp.concatenate([rope(kvp[:, :KV_WIDTH], cos_p, sin_p), rope(kvc[:, :KV_WIDTH], cos_c, sin_c)], axis=0)
            vw = jnp.concatenate([kvp[:, KV_WIDTH:], kvc[:, KV_WIDTH:]], axis=0)
            kd, vd = dup(kw), dup(vw)
            zs = (z0, z1, z2, z3)
            units = [(kvh, hf) for kvh in range(2) for hf in range(2)]
            half = GROUP_ROWS // 2
            pairs = [range(4 * kvh + 2 * hf, 4 * kvh + 2 * hf + 2) for kvh, hf in units]
            ku = [kd[kvh] for kvh, _ in units]
            vu = [vd[kvh] for kvh, _ in units]
            ps_all = ps_ref[...]
            head_row = lax.broadcasted_iota(jnp.int32, (ATT_HEADS, 128), 0)
            eye = (lax.broadcasted_iota(jnp.int32, (ATT_BLOCK, 128), 0)
                   == lax.broadcasted_iota(jnp.int32, (ATT_BLOCK, 128), 1))

            def first(j):
                kvh, hf = units[j]
                p = p_ref[kvh, hf * half:(hf + 1) * half, :]
                parts = []
                for pr in pairs[j]:
                    cols = slice(128 * pr, 128 * (pr + 1))
                    sg, sg_grad = _silu_and_grad(zs[pr // 2][:, 128 * (pr % 2):128 * (pr % 2 + 1)])
                    du128 = du_ref[:, cols]
                    dz_ref[:, cols] = (du128 * o_ref[:, cols].astype(F32) * sg_grad).astype(BF16)
                    parts.append(du128 * sg)
                dos = _stack_heads(parts, left)
                dp = _dot(dos, vu[j], NT)
                delta = _row_sums(p.astype(F32) * dp)
                ds = (p.astype(F32) * (dp - jnp.concatenate([delta, delta], axis=1)) * ATT_SCALE).astype(BF16)
                sk = jnp.zeros((ATT_HEADS, 128), F32)
                for hh in range(4):
                    hd = kvh * GROUP + 4 * hf + hh
                    drow = jnp.sum(jnp.where(eye, delta[hh * ATT_BLOCK:(hh + 1) * ATT_BLOCK, :], 0.0), axis=0,
                                   keepdims=True)
                    sk = sk - jnp.where(head_row == hd, ps_all * drow, 0.0)
                sk_acc[...] += sk
                return ds, p, dos.astype(BF16), qs_ref[kvh, hf * half:(hf + 1) * half, :]

            def second(j, ds, p, dos, qs):
                dqs = _dot(ds, ku[j], NN)
                for pp, pr in enumerate(pairs[j]):
                    dq_ref[:, 128 * pr:128 * (pr + 1)] = rope_bwd(_unstack_heads(dqs, left, pp),
                                                                  cos_c, sin_c).astype(BF16)
                return _dot(ds, qs, TN), _dot(p, dos, TN)

            got, dku, dvu = {}, [None] * len(units), [None] * len(units)
            for j in range(len(units) + 1):
                if j < len(units):
                    got[j] = first(j)
                if j >= 1:
                    dku[j - 1], dvu[j - 1] = second(j - 1, *got.pop(j - 1))
            dkd = [dku[0] + dku[1], dku[2] + dku[3]]
            dvd = [dvu[0] + dvu[1], dvu[2] + dvu[3]]
            fold = lambda pr: jnp.where(left, pr[0] + pltpu.roll(pr[0], ATT_DIM, 1), pr[1] + pltpu.roll(pr[1], ATT_DIM, 1))
            dkw = fold(dkd)
            dvw = fold(dvd)
            prev = jnp.concatenate([rope_bwd(dkw[:ATT_BLOCK], cos_p, sin_p), dvw[:ATT_BLOCK]], axis=1)
            cur = jnp.concatenate([rope_bwd(dkw[ATT_BLOCK:], cos_c, sin_c), dvw[ATT_BLOCK:]], axis=1)
            dkv_ref[...] = (carry[...] + prev).astype(BF16)
            carry[...] = cur

        @pl.when(i == nb)
        def _():
            dkv_ref[...] = carry[...].astype(BF16)

        @pl.when((b_id == B - 1) & (i == nb))
        def _():
            diag = (lax.broadcasted_iota(jnp.int32, (ATT_HEADS, 128), 0)
                    == lax.broadcasted_iota(jnp.int32, (ATT_HEADS, 128), 1))
            tot = jnp.sum(sk_acc[...], axis=1, keepdims=True)
            dsk_ref[...] = jnp.sum(jnp.where(diag, tot, 0.0), axis=0, keepdims=True)

        _hosted_finish(phase, p_in, p_out, p_sems, (b_id == B - 1) & (i == nb))

    cl = lambda i: jnp.minimum(i, nb - 1)
    pv = lambda i: jnp.maximum(jnp.minimum(i, nb - 1) - 1, 0)
    rowblk = lambda w, cb: pl.BlockSpec((None, ATT_BLOCK, w), lambda b, i: (b, cl(i), cb))
    tab = pl.BlockSpec((ATT_BLOCK, 128), lambda b, i: (cl(i), 0))
    tabp = pl.BlockSpec((ATT_BLOCK, 128), lambda b, i: (pv(i), 0))
    p_ispecs, p_ospecs, p_oshapes, p_alias, p_scratch, p_args = _host_phase(phase, 15, 4)
    res = pl.pallas_call(
        body, name=name,
        grid=(B, nb + 1),
        in_specs=[pl.BlockSpec((None, None, 2, GROUP_ROWS, 128), lambda b, i: (b, cl(i), 0, 0, 0)), rowblk(256, 4),
                  pl.BlockSpec((None, ATT_BLOCK, 256), lambda b, i: (b, pv(i), 4)),
                  rowblk(256, _Z0), rowblk(256, _Z0 + 1), rowblk(256, _Z0 + 2), rowblk(256, _Z0 + 3),
                  rowblk(ATT_WIDTH, 1),
                  tab, tab, tabp, tabp,
                  pl.BlockSpec((None, None, 2, GROUP_ROWS, 2 * ATT_BLOCK), lambda b, i: (b, cl(i), 0, 0, 0)),
                  rowblk(ATT_WIDTH, 0),
                  pl.BlockSpec((None, None, ATT_HEADS, 128), lambda b, i: (b, cl(i), 0, 0))] + p_ispecs,
        out_specs=[rowblk(ATT_WIDTH, 0),
                   pl.BlockSpec((None, ATT_BLOCK, 256), lambda b, i: (b, jnp.maximum(i - 1, 0), 0)),
                   rowblk(ATT_WIDTH, 0),
                   pl.BlockSpec((1, 128), lambda b, i: (0, 0))] + p_ospecs,
        out_shape=[jax.ShapeDtypeStruct((B, S, ATT_WIDTH), BF16), jax.ShapeDtypeStruct((B, S, 256), BF16),
                   jax.ShapeDtypeStruct((B, S, ATT_WIDTH), BF16), jax.ShapeDtypeStruct((1, 128), F32)] + p_oshapes,
        input_output_aliases=p_alias,
        scratch_shapes=[pltpu.VMEM((ATT_BLOCK, 256), F32), pltpu.VMEM((ATT_HEADS, 128), F32)] + p_scratch,
        compiler_params=_params(("arbitrary", "arbitrary")),
    )(qs_kept, proj_a, proj_a, proj_a, proj_a, proj_a, proj_a, du, cos, sin, cos, sin, p_kept, o_kept, ps_kept, *p_args)
    return tuple(res[:4]) + (list(res[4:]),)


def _outproj_fwd(u2, w_out, x2, g_post, target2, name):
    T, D = x2.shape
    tm = _pick(T, (512, 256, 128))
    last = target2 is not None

    def body(u_ref, w_ref, x_ref, g_ref, *rest):
        y = lax.dot_general(u_ref[...], w_ref[...], (NN, ((), ())), preferred_element_type=F32)
        r = lax.rsqrt(jnp.mean(y * y, axis=-1, keepdims=True) + NORM_EPS)
        xn = x_ref[...] + (y * r) * g_ref[...]
        if last:
            t_ref, y_ref, dx_ref, loss_ref = rest
            err = xn - t_ref[...]
            dx_ref[...] = err * (1.0 / D)
            sq = err * err
            acc = sq[:, 0:128]
            for kk in range(1, D // 128):
                acc = acc + sq[:, 128 * kk:128 * (kk + 1)]
            part = jnp.sum(acc.reshape(tm // 8, 8, 128), axis=0) * (0.5 / D)

            @pl.when(pl.program_id(0) == 0)
            def _():
                loss_ref[...] = jnp.zeros_like(loss_ref)

            loss_ref[...] += part
        else:
            y_ref, xn_ref = rest
            xn_ref[...] = xn
        y_ref[...] = y

    row = pl.BlockSpec((tm, D), lambda i: (i, 0))
    in_specs = [pl.BlockSpec((tm, MIX_WIDTH), lambda i: (i, 0)),
                pl.BlockSpec((MIX_WIDTH, D), lambda i: (0, 0)), row,
                pl.BlockSpec((1, D), lambda i: (0, 0))]
    args = [u2, w_out, x2, g_post]
    out_specs = [row, row]
    out_shape = [jax.ShapeDtypeStruct((T, D), F32), jax.ShapeDtypeStruct((T, D), F32)]
    if last:
        in_specs.append(row)
        args.append(target2)
        out_specs.append(pl.BlockSpec((8, 128), lambda i: (0, 0)))
        out_shape.append(jax.ShapeDtypeStruct((8, 128), F32))
    return pl.pallas_call(
        body, name=name, grid=(T // tm,), in_specs=in_specs, out_specs=out_specs, out_shape=out_shape,
        compiler_params=_params(("arbitrary",)),
    )(*args)


def _outproj_bwd(dxn2, y2, g_post, w_out, u2, name):
    T, D = y2.shape
    N = w_out.shape[0]
    tm = _pick(T, (512, 256, 128))
    nt = T // tm

    def body(dx_ref, y_ref, g_ref, w_ref, u_ref, dg_ref, du_ref, dw_ref, acc, wacc):
        i = pl.program_id(0)

        @pl.when(i == 0)
        def _():
            acc[...] = jnp.zeros_like(acc)
            wacc[...] = jnp.zeros_like(wacc)

        y = y_ref[...]
        dxn = dx_ref[...]
        r = lax.rsqrt(jnp.mean(y * y, axis=-1, keepdims=True) + NORM_EPS)
        n = y * r
        dn = dxn * g_ref[...]
        dy = (r * (dn - n * jnp.mean(dn * n, axis=-1, keepdims=True))).astype(BF16)
        du_ref[...] = lax.dot_general(dy, w_ref[...], (NT, ((), ())), preferred_element_type=F32)
        wacc[...] += lax.dot_general(u_ref[...], dy, (TN, ((), ())), preferred_element_type=F32)
        acc[...] += jnp.sum((dxn * n).reshape(tm // 8, 8, D), axis=0)

        @pl.when(i == nt - 1)
        def _():
            dg_ref[...] = jnp.sum(acc[...], axis=0, keepdims=True)
            dw_ref[...] = wacc[...].astype(BF16)

    row = pl.BlockSpec((tm, D), lambda i: (i, 0))
    wide = pl.BlockSpec((tm, N), lambda i: (i, 0))
    vec = pl.BlockSpec((1, D), lambda i: (0, 0))
    whole = pl.BlockSpec((N, D), lambda i: (0, 0))
    return pl.pallas_call(
        body, name=name, grid=(nt,),
        in_specs=[row, row, vec, pl.BlockSpec((N, D), lambda i: (0, 0), pipeline_mode=pl.Buffered(1)), wide],
        out_specs=[vec, wide, whole],
        out_shape=[jax.ShapeDtypeStruct((1, D), F32), jax.ShapeDtypeStruct((T, N), F32),
                   jax.ShapeDtypeStruct((N, D), BF16)],
        scratch_shapes=[pltpu.VMEM((8, D), F32), pltpu.VMEM((N, D), F32)],
        compiler_params=_params(("arbitrary",)),
    )(dxn2, y2, g_post, w_out, u2)


def _inproj_bwd(pieces, w_t, x2, dxn2, g_pre, name, phase=None):
    T, D = x2.shape
    widths = [p.shape[1] for p in pieces]
    offs = [sum(widths[:i]) for i in range(len(pieces))]
    n_p = len(pieces)
    tm = _pick(T, (256, 128))
    nt = T // tm

    def body(*refs):
        ins, (dx_ref, dg_ref), (acc,), p_in, p_out, p_sems = _split_refs(refs, n_p + 4, 2, 1, phase)
        w_ref, x_ref, dxn_ref, g_ref = ins[n_p:]
        i = pl.program_id(0)
        _hosted_start(phase, p_in, p_out, p_sems, i == 0)

        @pl.when(i == 0)
        def _():
            acc[...] = jnp.zeros_like(acc)

        dh = jnp.zeros((tm, D), F32)
        for p in range(n_p):
            dh = dh + lax.dot_general(ins[p][...], w_ref[offs[p]:offs[p] + widths[p], :], (NN, ((), ())),
                                      preferred_element_type=F32)
        x = x_ref[...]
        r = lax.rsqrt(jnp.mean(x * x, axis=-1, keepdims=True) + NORM_EPS)
        n = x * r
        dn = dh * g_ref[...]
        dx_ref[...] = dxn_ref[...] + r * (dn - n * jnp.mean(dn * n, axis=-1, keepdims=True))
        acc[...] += jnp.sum((dh * n).reshape(tm // 8, 8, D), axis=0)

        @pl.when(i == nt - 1)
        def _():
            dg_ref[...] = jnp.sum(acc[...], axis=0, keepdims=True)

        _hosted_finish(phase, p_in, p_out, p_sems, i == nt - 1)

    row = pl.BlockSpec((tm, D), lambda i: (i, 0))
    vec = pl.BlockSpec((1, D), lambda i: (0, 0))
    p_ispecs, p_ospecs, p_oshapes, p_alias, p_scratch, p_args = _host_phase(phase, n_p + 4, 2)
    res = pl.pallas_call(
        body, name=name, grid=(nt,),
        in_specs=[pl.BlockSpec((tm, w), lambda i: (i, 0)) for w in widths]
        + [pl.BlockSpec((sum(widths), D), lambda i: (0, 0), pipeline_mode=pl.Buffered(1)), row, row, vec] + p_ispecs,
        out_specs=[row, vec] + p_ospecs,
        out_shape=[jax.ShapeDtypeStruct((T, D), F32), jax.ShapeDtypeStruct((1, D), F32)] + p_oshapes,
        input_output_aliases=p_alias,
        scratch_shapes=[pltpu.VMEM((8, D), F32)] + p_scratch,
        compiler_params=_params(("arbitrary",)),
    )(*pieces, w_t, x2, dxn2, g_pre, *p_args)
    return res[0], res[1], list(res[2:])


def _step(x, target, g_pre, g_post, lb_param, g_head, sinks, shards=None, full=None):
    B, S, D = x.shape
    T = B * S
    dist = shards is not None
    first, last = 0, DEPTH - 1
    if dist:
        a_loc, b_loc = shards
        ra, rb = a_loc.shape[1], b_loc.shape[1]
        side = _own_side_blocks()
        placed = lambda loc, l, nm: _place_own(loc, l, side, "place_" + nm)
        w_in0 = _gather_one_call(placed(a_loc, 0, "in0"), "gather_in0")
        w_in, w_out = [w_in0, None], [None, None]
    else:
        w_in, w_out = list(full[0]), list(full[1])
    cos, sin = _rope_tables(S)
    saved = []
    xs = x
    loss_part = None
    dxn = None
    for l in range(DEPTH):
        x2 = xs.reshape(T, D)
        proj_h, proj_a, h = _inproj(x2, g_pre[l:l + 1], w_in[l], f"inproj{l}")
        proj_h = proj_h.reshape(B, S, N_H)
        proj_a = proj_a.reshape(B, S, N_A)
        phase = None
        if dist and l == first:
            phase = _gather_ici_phase([placed(a_loc, 1, "in1"), placed(b_loc, 0, "out0")])
        if dist and l == last:
            phase = _gather_d2d_phase([w_out1_part], [rb])
        o_h, u, states, got = _hgrn_fwd(proj_h, MIX_WIDTH, lb_param, g_head[l:l + 1], l, f"hgrn_fwd{l}", phase)
        phase = None
        if dist and l == first:
            phase = _merge_phases(_gather_d2d_phase(got, [ra, rb]),
                                  _gather_ici_phase([placed(b_loc, 1, "out1")]))
        if dist and l == last:
            w_out[1] = got[0]
        u, kept_a, got = _attn_fwd(proj_a, u, _sink_rows(sinks[l]), cos, sin, f"attn_fwd{l}", phase)
        if dist and l == first:
            w_in[1], w_out[0], w_out1_part = got
        u2 = u.reshape(T, MIX_WIDTH)
        if l < last:
            y, xn = _outproj_fwd(u2, w_out[l], x2, g_post[l:l + 1], None, f"outproj{l}")
            xn = xn.reshape(B, S, D)
        else:
            y, dxn, loss_part = _outproj_fwd(u2, w_out[l], x2, g_post[l:l + 1], target.reshape(T, D), f"outproj{l}")
            xn = None
        saved.append((x2, h, proj_h, proj_a, o_h, u2, states, kept_a, y))
        xs = xn

    dw_in, dw_out = [None] * DEPTH, [None] * DEPTH
    dg_pre, dg_post, dlb, dg_head, dsinks = [], [], [], [], []
    for l in reversed(range(DEPTH)):
        x2, h, proj_h, proj_a, o_h, u2, states, kept_a, y = saved[l]
        dgp, du, dw_out[l] = _outproj_bwd(dxn, y, g_post[l:l + 1], w_out[l], u2, f"outproj_bwd{l}")
        du = du.reshape(B, S, MIX_WIDTH)
        phase = None
        if dist:
            phase = _reduce_d2d_phase([dw_out[l]], [rb])
            if l == first:
                phase = _merge_phases(_reduce_ici_phase([part_in1]), phase)
        dqh, dfh, dih, dzh, dlb_l, dgh, got = _hgrn_bwd(
            proj_h, o_h, du, states, lb_param, g_head[l:l + 1], l, f"hgrn_bwd{l}", phase)
        if dist:
            if l == first:
                sum_in = _chip_sum(part_in1, got[0], "chip_sum_in1", 1)
            part_out = _pair_sum(dw_out[l], got[-1], side, f"pair_sum_out{l}")
        dqa, dkv, dza, dsk, got = _attn_bwd(proj_a, du, kept_a, cos, sin, f"attn_bwd{l}",
                                            _reduce_ici_phase([part_out]) if dist else None)
        if dist:
            sum_out = _chip_sum(part_out, got[0], f"chip_sum_out{l}", l, None if l == last else sum_out)
        dproj = [p.reshape(T, p.shape[-1]) for p in (dqh, dfh, dih, dzh, dqa, dkv, dza)]
        dw_in[l] = _mm_tn(dproj, h, f"wgrad_in{l}")
        phase = None
        if dist and l == last:
            phase = _reduce_d2d_phase([dw_in[l]], [ra])
        if dist and l == first:
            got = _run_phase(_reduce_d2d_phase([dw_in[l]], [ra]), "reduce_in0_d2d")
            part_in0 = _pair_sum(dw_in[l], got[0], side, "pair_sum_in0")
            phase = _reduce_ici_phase([part_in0])
        dxn, dgpre, got = _inproj_bwd(dproj, w_in[l], x2, dxn, g_pre[l:l + 1], f"inproj_bwd{l}", phase)
        if dist and l == last:
            part_in1 = _pair_sum(dw_in[l], got[0], side, "pair_sum_in1")
        if dist and l == first:
            sum_in = _chip_sum(part_in0, got[0], "chip_sum_in0", 0, sum_in)
        dg_pre.append(dgpre)
        dg_post.append(dgp)
        dlb.append(dlb_l)
        dg_head.append(dgh)
        dsinks.append(dsk)
    rev = lambda lst: jnp.concatenate(lst[::-1], axis=0)
    if not dist:
        sum_in, sum_out = jnp.stack(dw_in), jnp.stack(dw_out)
    return (loss_part, dxn.reshape(B, S, D), sum_in, sum_out,
            rev(dg_pre), rev(dg_post), rev(dlb), rev(dg_head), rev(dsinks))


def _me_and_peers():
    x, y, c = lax.axis_index("x"), lax.axis_index("y"), lax.axis_index("c")
    me = 4 * x + 2 * y + c
    peers = []
    for k in range(1, N_DEV):
        px = 1 - x if k & 4 else x
        py = 1 - y if k & 2 else y
        pc = 1 - c if k & 1 else c
        peers.append(((px, py, pc), 4 * px + 2 * py + pc))
    return me, peers


class _Phase:
    def __init__(self, arrays, out_shapes, aliases, n_send, build):
        self.arrays, self.out_shapes, self.aliases = list(arrays), list(out_shapes), dict(aliases)
        self.n_send, self.build = n_send, build

    def scratch(self):
        return [pltpu.SemaphoreType.DMA((self.n_send,)), pltpu.SemaphoreType.DMA((self.n_send,))]

    def _copies(self, in_refs, out_refs, sems, arrivals):
        send_sems, recv_sems = sems
        sends, recvs = self.build(in_refs, out_refs)
        assert len(sends) == self.n_send == len(recvs)
        out = [pltpu.make_async_remote_copy(src_ref=s, dst_ref=d, send_sem=send_sems.at[i], recv_sem=recv_sems.at[i],
                                            device_id=dev, device_id_type=MESH) for i, (s, d, dev) in enumerate(sends)]
        inc = [pltpu.make_async_remote_copy(src_ref=s, dst_ref=r, send_sem=send_sems.at[i], recv_sem=recv_sems.at[i],
                                            device_id=dev, device_id_type=MESH)
               for i, ((s, _, dev), r) in enumerate(zip(sends, recvs))] if arrivals else []
        return out, inc

    def start(self, in_refs, out_refs, sems):
        out, _ = self._copies(in_refs, out_refs, sems, False)
        for cp in out:
            cp.start()

    def finish(self, in_refs, out_refs, sems):
        out, inc = self._copies(in_refs, out_refs, sems, True)
        for cp in inc:
            cp.wait_recv()
        for cp in out:
            cp.wait_send()


_ANY = pl.BlockSpec(memory_space=pl.ANY)


def _host_phase(phase, n_in, n_out):
    if phase is None:
        return [], [], [], {}, [], []
    aliases = {n_in + i: n_out + o for i, o in phase.aliases.items()}
    return ([_ANY] * len(phase.arrays), [_ANY] * len(phase.out_shapes), phase.out_shapes, aliases, phase.scratch(),
            phase.arrays)


def _split_refs(refs, n_in, n_out, n_scr, phase):
    pi = len(phase.arrays) if phase else 0
    po = len(phase.out_shapes) if phase else 0
    a = n_in + pi
    b = a + n_out + po
    return (refs[:n_in], refs[a:a + n_out], refs[b:b + n_scr], refs[n_in:a], refs[a + n_out:b], refs[b + n_scr:])


def _hosted_start(phase, p_in, p_out, p_sems, first):
    if phase is not None:
        @pl.when(first)
        def _():
            phase.start(p_in, p_out, p_sems)


def _hosted_finish(phase, p_in, p_out, p_sems, last):
    if phase is not None:
        @pl.when(last)
        def _():
            phase.finish(p_in, p_out, p_sems)


def _run_phase(phase, name):
    n_in, n_out = len(phase.arrays), len(phase.out_shapes)

    def body(*refs):
        phase.start(refs[:n_in], refs[n_in:n_in + n_out], refs[n_in + n_out:])
        phase.finish(refs[:n_in], refs[n_in:n_in + n_out], refs[n_in + n_out:])

    return pl.pallas_call(
        body, name=name, in_specs=[_ANY] * n_in, out_specs=[_ANY] * n_out,
        out_shape=phase.out_shapes, input_output_aliases=phase.aliases, scratch_shapes=phase.scratch(),
        compiler_params=pltpu.CompilerParams(has_side_effects=True),
    )(*phase.arrays)


def _gather_one_call(full, name):
    r = full.shape[0] // N_DEV
    half = r // 2

    def body(full_in, full_ref, send_sems, recv_sems):
        del full_in
        c, (own, xn, yn, dg), num = _mesh_place()
        me, sib = num(own, c), (*own, 1 - c)

        def blk(dev, part=None):
            start, n = (dev * r, r) if part is None else (dev * r + part * half, half)
            return full_ref.at[pl.ds(pl.multiple_of(start, 16), n), :]

        def copy(k, src, dev, to, part=None):
            return pltpu.make_async_remote_copy(src_ref=src, dst_ref=blk(dev, part),
                                                send_sem=send_sems.at[k], recv_sem=recv_sems.at[k],
                                                device_id=to, device_id_type=MESH)

        def landed(k, dev, part=None):
            copy(k, blk(dev, part), dev, sib, part).wait_recv()

        sent = []

        def start(*cps):
            for cp in cps:
                cp.start()
                sent.append(cp)

        xs, ys, ds = num(xn, c), num(yn, c), num(dg, c)
        start(copy(0, blk(me), me, sib), copy(1, blk(me), me, (*xn, c)), copy(2, blk(me), me, (*yn, c)))
        landed(1, xs)
        start(copy(3, blk(xs, 0), xs, (*yn, c), 0), copy(5, blk(xs), xs, sib))
        landed(2, ys)
        start(copy(4, blk(ys, 1), ys, (*xn, c), 1), copy(6, blk(ys), ys, sib))
        landed(3, ds, 0)
        landed(4, ds, 1)
        start(copy(7, blk(ds), ds, sib))
        landed(0, num(own, 1 - c))
        for k, ch in ((5, xn), (6, yn), (7, dg)):
            landed(k, num(ch, 1 - c))
        for cp in sent:
            cp.wait_send()

    assert half % 16 == 0
    return pl.pallas_call(
        body, name=name, in_specs=[_ANY], out_specs=_ANY,
        out_shape=jax.ShapeDtypeStruct(full.shape, full.dtype), input_output_aliases={0: 0},
        scratch_shapes=[pltpu.SemaphoreType.DMA((8,)), pltpu.SemaphoreType.DMA((8,))],
        compiler_params=pltpu.CompilerParams(has_side_effects=True),
    )(full)


def _merge_phases(a, b):
    n_in, n_out = len(a.arrays), len(a.out_shapes)
    aliases = dict(a.aliases)
    aliases.update({n_in + i: n_out + o for i, o in b.aliases.items()})

    def build(ins, outs):
        sa, ra = a.build(ins[:n_in], outs[:n_out])
        sb, rb = b.build(ins[n_in:], outs[n_out:])
        return sa + sb, ra + rb

    return _Phase(a.arrays + b.arrays, a.out_shapes + b.out_shapes, aliases, a.n_send + b.n_send, build)


def _mesh_place():
    x, y, c = lax.axis_index("x"), lax.axis_index("y"), lax.axis_index("c")
    chips = [(x, y), (1 - x, y), (x, 1 - y), (1 - x, 1 - y)]
    num = lambda chip, core: 4 * chip[0] + 2 * chip[1] + core
    return c, chips, num


def _own_side_blocks():
    c, chips, num = _mesh_place()
    return jnp.stack([num(ch, c) for ch in chips]).astype(jnp.int32)


def _rows(ref, r, dev):
    return ref.at[pl.ds(pl.multiple_of(dev * r, 16), r), :]


def _place_own(shards, layer, blocks, name):
    _, r, D = shards.shape
    tr = _pick(r, (400, 256, 200, 128, 64, 16))

    def body(idx_ref, l_ref, o_ref):
        del idx_ref
        o_ref[...] = l_ref[...].astype(BF16)

    return pl.pallas_call(
        body, name=name,
        grid_spec=pltpu.PrefetchScalarGridSpec(
            num_scalar_prefetch=1, grid=(r // tr,),
            in_specs=[pl.BlockSpec((None, tr, D), lambda i, idx: (layer, i, 0))],
            out_specs=pl.BlockSpec((tr, D), lambda i, idx: (idx[0] * (r // tr) + i, 0))),
        out_shape=jax.ShapeDtypeStruct((N_DEV * r, D), BF16),
        compiler_params=_params(("arbitrary",)),
    )(blocks, shards)


def _gather_ici_phase(fulls):
    rs = [a.shape[0] // N_DEV for a in fulls]
    n = len(fulls)

    def build(ins, outs):
        del ins
        c, chips, num = _mesh_place()
        me = num(chips[0], c)
        targets = [((*chips[0], 1 - c), num(chips[0], 1 - c))] + [((*ch, c), num(ch, c)) for ch in chips[1:]]
        sends, recvs = [], []
        for dev, dnum in targets:
            for i, r in enumerate(rs):
                sends.append((_rows(outs[i], r, me), _rows(outs[i], r, me), dev))
                recvs.append(_rows(outs[i], r, dnum))
        return sends, recvs

    shapes = [jax.ShapeDtypeStruct(a.shape, a.dtype) for a in fulls]
    return _Phase(list(fulls), shapes, {i: i for i in range(n)}, 4 * n, build)


def _gather_d2d_phase(fulls, rs):
    def build(ins, outs):
        c, chips, num = _mesh_place()
        sib = (*chips[0], 1 - c)
        sends, recvs = [], []
        for ch in chips[1:]:
            for i, r in enumerate(rs):
                blk = _rows(outs[i], r, num(ch, c))
                sends.append((blk, blk, sib))
                recvs.append(_rows(outs[i], r, num(ch, 1 - c)))
        return sends, recvs

    shapes = [jax.ShapeDtypeStruct(a.shape, a.dtype) for a in fulls]
    return _Phase(fulls, shapes, {i: i for i in range(len(fulls))}, 3 * len(fulls), build)


def _reduce_d2d_phase(grads, rs):
    def build(ins, outs):
        c, chips, num = _mesh_place()
        sib = (*chips[0], 1 - c)
        sends, recvs = [], []
        for j, ch in enumerate(chips):
            for i, r in enumerate(rs):
                sends.append((_rows(ins[i], r, num(ch, 1 - c)), outs[i].at[j], sib))
                recvs.append(outs[i].at[j])
        return sends, recvs

    shapes = [jax.ShapeDtypeStruct((4, r, g.shape[1]), g.dtype) for g, r in zip(grads, rs)]
    return _Phase(grads, shapes, {}, 4 * len(grads), build)


def _reduce_ici_phase(parts):
    def build(ins, outs):
        c, chips, _ = _mesh_place()
        sends, recvs = [], []
        for t in range(1, 4):
            for i in range(len(parts)):
                sends.append((ins[i].at[t], outs[i].at[t - 1], (*chips[t], c)))
                recvs.append(outs[i].at[t - 1])
        return sends, recvs

    shapes = [jax.ShapeDtypeStruct((3,) + p.shape[1:], p.dtype) for p in parts]
    return _Phase(parts, shapes, {}, 3 * len(parts), build)


def _pair_sum(g, got, blocks, name):
    n, r, D = got.shape
    tr = _pick(r, (800, 400, 256, 200, 128, 64, 16))

    def body(idx_ref, g_ref, r_ref, o_ref):
        del idx_ref
        o_ref[...] = (g_ref[...].astype(F32) + r_ref[...].astype(F32)).astype(o_ref.dtype)

    blk = pl.BlockSpec((None, tr, D), lambda j, i, idx: (j, i, 0))
    return pl.pallas_call(
        body, name=name,
        grid_spec=pltpu.PrefetchScalarGridSpec(
            num_scalar_prefetch=1, grid=(n, r // tr),
            in_specs=[pl.BlockSpec((tr, D), lambda j, i, idx: (idx[j] * (r // tr) + i, 0)), blk],
            out_specs=blk),
        out_shape=jax.ShapeDtypeStruct(got.shape, got.dtype),
        compiler_params=_params(("arbitrary", "arbitrary")),
    )(blocks, g, got)


def _chip_sum(p, r, name, layer, into=None):
    _, R, D = p.shape
    tr = _pick(R, (800, 400, 256, 200, 128, 64, 16))

    def body(p_ref, r_ref, *rest):
        acc = p_ref[...].astype(F32)
        for t in range(3):
            acc = acc + r_ref[t].astype(F32)
        rest[-1][...] = acc

    args = [p, r] + ([] if into is None else [into])
    return pl.pallas_call(
        body, name=name, grid=(R // tr,),
        in_specs=[pl.BlockSpec((None, tr, D), lambda i: (0, i, 0)), pl.BlockSpec((3, tr, D), lambda i: (0, i, 0))]
        + ([] if into is None else [_ANY]),
        out_specs=pl.BlockSpec((None, tr, D), lambda i: (layer, i, 0)),
        out_shape=jax.ShapeDtypeStruct((DEPTH, R, D), F32),
        input_output_aliases={} if into is None else {2: 0},
        compiler_params=_params(("parallel",)))(*args)


def _allreduce_small(vec):
    R, C = vec.shape

    def body(v_ref, o_ref, buf, send_sems, recv_sems):
        me, peers = _me_and_peers()
        buf[me] = v_ref[...]
        sends = []
        for k, (pid, _) in enumerate(peers):
            cp = pltpu.make_async_remote_copy(src_ref=v_ref, dst_ref=buf.at[me], send_sem=send_sems.at[k],
                                              recv_sem=recv_sems.at[k], device_id=pid, device_id_type=MESH)
            cp.start()
            sends.append(cp)
        for k, (pid, pnum) in enumerate(peers):
            pltpu.make_async_remote_copy(src_ref=v_ref, dst_ref=buf.at[pnum], send_sem=send_sems.at[k],
                                         recv_sem=recv_sems.at[k], device_id=pid, device_id_type=MESH).wait_recv()
        for cp in sends:
            cp.wait_send()
        acc = buf[0]
        for d in range(1, N_DEV):
            acc = acc + buf[d]
        o_ref[...] = acc

    vm = pl.BlockSpec(memory_space=pltpu.VMEM)
    return pl.pallas_call(
        body, name="allreduce_small",
        in_specs=[vm], out_specs=vm,
        out_shape=jax.ShapeDtypeStruct((R, C), F32),
        scratch_shapes=[pltpu.VMEM((N_DEV, R, C), F32), pltpu.SemaphoreType.DMA((N_DEV - 1,)),
                        pltpu.SemaphoreType.DMA((N_DEV - 1,))],
        compiler_params=pltpu.CompilerParams(has_side_effects=True),
    )(vec)


def _adamw(w, g, m, v, name):
    R, C = w.shape
    tr = _pick(R, (512, 400, 256, 128, 64, 32, 16, 8)) if R >= 8 else R
    c1 = 1.0 - ADAM_B1 ** ADAM_STEP
    c2 = 1.0 - ADAM_B2 ** ADAM_STEP

    def body(w_ref, g_ref, m_ref, v_ref, d_ref, mo_ref, vo_ref):
        gg = g_ref[...]
        mn = ADAM_B1 * m_ref[...] + (1.0 - ADAM_B1) * gg
        vn = ADAM_B2 * v_ref[...] + (1.0 - ADAM_B2) * (gg * gg)
        d_ref[...] = -ADAM_LR * ((mn / c1) / (jnp.sqrt(vn / c2) + ADAM_EPS) + ADAM_WD * w_ref[...])
        mo_ref[...] = mn
        vo_ref[...] = vn

    blk = pl.BlockSpec((tr, C), lambda i: (i, 0))
    sh = jax.ShapeDtypeStruct((R, C), F32)
    return pl.pallas_call(
        body, name=name, grid=(R // tr,), in_specs=[blk] * 4, out_specs=[blk] * 3, out_shape=[sh] * 3,
        compiler_params=_params(("parallel",)),
    )(w, g, m, v)


def _lb_param_grad(lb_param, dlb):
    L, C = lb_param.shape

    def body(p_ref, d_ref, o_ref):
        lbp = p_ref[...]
        d = d_ref[...]
        mx = jnp.max(lbp, axis=0, keepdims=True)
        e = jnp.exp(lbp - mx)
        p = e / jnp.sum(e, axis=0, keepdims=True)
        tot = jnp.sum(d, axis=0, keepdims=True)
        dps = []
        rest = tot
        for j in range(L):
            dps.append(rest - tot if j == 0 else rest)
            rest = rest - d[j:j + 1]
        dp = jnp.concatenate(dps, axis=0)
        o_ref[...] = p * (dp - jnp.sum(p * dp, axis=0, keepdims=True))

    vm = pl.BlockSpec(memory_space=pltpu.VMEM)
    return pl.pallas_call(body, name="lb_param_grad", in_specs=[vm, vm], out_specs=vm,
                          out_shape=jax.ShapeDtypeStruct((L, C), F32))(lb_param, dlb)


def _pack_small(loss_part, dg_pre, dg_post, dlb, dg_head, dsinks):
    pad8 = lambda a: jnp.pad(a.reshape(-1, 128), ((0, 8 - DEPTH), (0, 0)))
    rows = [dg_pre.reshape(-1, 128), dg_post.reshape(-1, 128), dlb.reshape(-1, 128), pad8(dg_head), pad8(dsinks),
            loss_part]
    return jnp.concatenate(rows, axis=0)


def _unpack_small(vec):
    n = DEPTH * D_MODEL // 128
    o = 0
    dg_pre = vec[o:o + n].reshape(DEPTH, D_MODEL); o += n
    dg_post = vec[o:o + n].reshape(DEPTH, D_MODEL); o += n
    dlb = vec[o:o + n].reshape(DEPTH, HG_WIDTH); o += n
    dg_head = vec[o:o + DEPTH]; o += 8
    dsinks = vec[o:o + DEPTH, :ATT_HEADS]; o += 8
    loss = jnp.sum(vec[o:o + 8])
    return loss, dg_pre, dg_post, dlb, dg_head, dsinks


def kernel(x, w_in, w_out, g_pre, g_post, lb_param, g_head, sinks, loss_target, m_w_in, m_w_out, m_g_pre, m_g_post, m_lb_param, m_g_head, m_sinks, v_w_in, v_w_out, v_g_pre, v_g_post, v_lb_param, v_g_head, v_sinks):
    tr = lambda a: jnp.swapaxes(a, 1, 2)
    w_in_t = tr(w_in)
    (loss_part, dx, gw_in_t, gw_out, dg_pre, dg_post, dlb, dg_head, dsinks) = _step(
        x, loss_target, g_pre, g_post, lb_param, g_head, sinks, shards=(w_in_t, w_out))

    small = _allreduce_small(_pack_small(loss_part, dg_pre, dg_post, dlb, dg_head, dsinks))
    loss, gg_pre, gg_post, gdlb, gg_head, gsinks = _unpack_small(small)
    glb = _lb_param_grad(lb_param, gdlb)

    grads = [gw_in_t, gw_out, gg_pre, gg_post, glb, gg_head, gsinks]
    ws = [w_in_t, w_out, g_pre, g_post, lb_param, g_head, sinks]
    ms = [tr(m_w_in), m_w_out, m_g_pre, m_g_post, m_lb_param, m_g_head, m_sinks]
    vs = [tr(v_w_in), v_w_out, v_g_pre, v_g_post, v_lb_param, v_g_head, v_sinks]
    names = ["w_in", "w_out", "g_pre", "g_post", "lb_param", "g_head", "sinks"]
    deltas, new_m, new_v = [], [], []
    for w, g, m, v, nm in zip(ws, grads, ms, vs, names):
        sh = w.shape
        two = lambda a: a.reshape(-1, sh[-1])
        d, mn, vn = _adamw(two(w), two(g), two(m), two(v), "adamw_" + nm)
        deltas.append(d.reshape(sh))
        new_m.append(mn.reshape(sh))
        new_v.append(vn.reshape(sh))
    grads[0], deltas[0], new_m[0], new_v[0] = tr(grads[0]), tr(deltas[0]), tr(new_m[0]), tr(new_v[0])
    return (loss, dx, *grads, *deltas, *new_m, *new_v)
```

```python
import math

import numpy as np
import jax
import jax.numpy as jnp
from jax import lax
from jax.experimental import pallas as pl
from jax.experimental.pallas import tpu as pltpu

F32 = jnp.float32
BF16 = jnp.bfloat16

D_MODEL = 1024
DEPTH = 2
HG_HEADS = 8
HG_DIM = 128
HG_WIDTH = HG_HEADS * HG_DIM
CHUNK = 64
ATT_HEADS = 16
ATT_DIM = 64
ATT_WIDTH = ATT_HEADS * ATT_DIM
KV_WIDTH = 128
ATT_BLOCK = 128
ATT_SCALE = 1.0 / math.sqrt(ATT_DIM)
ROPE_THETA = 10000.0
NORM_EPS = 1e-6
NEG_INF = -1e30
LB_FLOOR = 1e-20
N_H = 4 * HG_WIDTH
N_A = 2 * ATT_WIDTH + 2 * KV_WIDTH
IN_WIDTH = N_H + N_A
MIX_WIDTH = HG_WIDTH + ATT_WIDTH

ADAM_LR = 0.001
ADAM_B1 = 0.9
ADAM_B2 = 0.999
ADAM_EPS = 1e-08
ADAM_WD = 0.01
ADAM_STEP = 10

N_DEV = 8
MESH = pl.DeviceIdType.MESH
VMEM_LIMIT = 56 * 1024 * 1024

NN = ((1,), (0,))
NT = ((1,), (1,))
TN = ((0,), (0,))


def _dot(a, b, dims):
    return lax.dot_general(a.astype(BF16), b.astype(BF16), (dims, ((), ())), preferred_element_type=F32)


def _params(sem=None, **kw):
    return pltpu.CompilerParams(dimension_semantics=sem, vmem_limit_bytes=VMEM_LIMIT, **kw)


def _sigmoids(x):
    e = jnp.exp(-jnp.abs(x))
    r = 1.0 / (1.0 + e)
    er = e * r
    pos = x >= 0.0
    return jnp.where(pos, r, er), jnp.where(pos, er, r)


def _silu(x):
    return x * _sigmoids(x)[0]


def _silu_and_grad(x):
    s, ns = _sigmoids(x)
    return x * s, s * (1.0 + x * ns)


def _pick(n, prefs):
    for p in prefs:
        if n % p == 0:
            return p
    return n


def _inproj(x2, g, w, name):
    T, D = x2.shape
    tm = _pick(T, (512, 256, 128))
    nchunk = 1024

    def body(x_ref, g_ref, w_ref, oh_ref, oa_ref, h_ref):
        x = x_ref[...]
        r = lax.rsqrt(jnp.mean(x * x, axis=-1, keepdims=True) + NORM_EPS)
        h = ((x * r) * g_ref[...]).astype(BF16)
        h_ref[...] = h
        for j in range(0, N_H, nchunk):
            oh_ref[:, j:j + nchunk] = lax.dot_general(h, w_ref[j:j + nchunk, :], (NT, ((), ())),
                                                      preferred_element_type=F32)
        for j in range(0, N_A, N_A // 2):
            oa_ref[:, j:j + N_A // 2] = lax.dot_general(h, w_ref[N_H + j:N_H + j + N_A // 2, :], (NT, ((), ())),
                                                        preferred_element_type=F32)

    row = lambda w_: pl.BlockSpec((tm, w_), lambda i: (i, 0))
    return pl.pallas_call(
        body, name=name,
        grid=(T // tm,),
        in_specs=[row(D), pl.BlockSpec((1, D), lambda i: (0, 0)),
                  pl.BlockSpec((IN_WIDTH, D), lambda i: (0, 0), pipeline_mode=pl.Buffered(1))],
        out_specs=[row(N_H), row(N_A), row(D)],
        out_shape=[jax.ShapeDtypeStruct((T, N_H), F32), jax.ShapeDtypeStruct((T, N_A), F32),
                   jax.ShapeDtypeStruct((T, D), BF16)],
        compiler_params=_params(("parallel",)),
    )(x2, g, w)


def _mm_tn(pieces, b, name, out_dtype=BF16):
    T, m = b.shape
    tn = 256
    counts = [p.shape[1] // tn for p in pieces]
    starts = [sum(counts[:i]) for i in range(len(pieces))]
    n_p = len(pieces)

    def body(*refs):
        b_ref, o_ref = refs[n_p], refs[n_p + 1]
        i = pl.program_id(0)
        for p in range(n_p):
            @pl.when((i >= starts[p]) & (i < starts[p] + counts[p]))
            def _(p=p):
                o_ref[...] = lax.dot_general(refs[p][...], b_ref[...], (TN, ((), ())),
                                             preferred_element_type=F32).astype(out_dtype)

    piece_spec = lambda s, c: pl.BlockSpec((T, tn), lambda i: (0, jnp.clip(i - s, 0, c - 1)))
    return pl.pallas_call(
        body, name=name,
        grid=(sum(counts),),
        in_specs=[piece_spec(s, c) for s, c in zip(starts, counts)]
        + [pl.BlockSpec((T, m), lambda i: (0, 0), pipeline_mode=pl.Buffered(1))],
        out_specs=pl.BlockSpec((tn, m), lambda i: (i, 0)),
        out_shape=jax.ShapeDtypeStruct((sum(counts) * tn, m), out_dtype),
        compiler_params=_params(("arbitrary",)),
    )(*pieces, b)


_LEVELS = (0, 1, 2, 4, 8, 16, 32)
_CUM_L = (2, 4, 8, 16, 32, 64)
_ALL_KINDS = tuple(("c", L) for L in _CUM_L) + tuple(("r", L) for L in _CUM_L)
_MXU_KINDS = (("c", 2), ("c", 4), ("c", CHUNK), ("r", 2), ("r", 4))
N_CUM = len(_ALL_KINDS) * CHUNK
N_CUM_F = len(_MXU_KINDS) * CHUNK


def _cum_matrices():
    t = np.arange(CHUNK)[:, None]
    r = np.arange(CHUNK)[None, :]

    def mat(kind):
        c, L = kind
        return ((r // L == t // L) & ((r <= t) if c == "c" else (r > t))).astype(np.float32)

    fwd = np.concatenate([mat(kd) for kd in _MXU_KINDS], axis=0)
    full = np.concatenate([mat(kd) for kd in _ALL_KINDS], axis=0)
    return jnp.asarray(fwd, BF16), jnp.asarray(full.T.copy(), BF16)


def _level_masks():
    t = np.arange(CHUNK)[:, None]
    s = np.arange(CHUNK)[None, :]
    ms = []
    for L in _LEVELS:
        if L == 0:
            ms.append(t == s)
        else:
            ms.append((t // (2 * L) == s // (2 * L)) & ((t // L) % 2 == 1) & ((s // L) % 2 == 0))
    return jnp.asarray(np.stack(ms).astype(np.float32))


def _split3(x):
    hi = x.astype(BF16)
    r1 = x - hi.astype(F32)
    mid = r1.astype(BF16)
    lo = (r1 - mid.astype(F32)).astype(BF16)
    return hi, mid, lo


def _cum3(ts, x, terms=3):
    d = lambda p: lax.dot_general(ts, p, (NN, ((), ())), preferred_element_type=F32)
    return sum(d(p) for p in _split3(x)[:terms])


def _lb_terms(lbp, layer):
    mx = jnp.max(lbp, axis=0, keepdims=True)
    e = jnp.exp(lbp - mx)
    p = e / jnp.sum(e, axis=0, keepdims=True)
    cum = p[0:1]
    for j in range(1, layer + 1):
        cum = cum + p[j:j + 1]
    lb = cum - p[0:1]
    lbf = jnp.maximum(lb, LB_FLOOR)
    return dict(lbf=lbf, one_m=1.0 - lb, kcorr=lb - lbf, ind=jnp.where(lb > LB_FLOOR, 1.0, 0.0))


def _gate(x, lt):
    sig, nsig = _sigmoids(x)
    f = lt["lbf"] + lt["one_m"] * sig
    return jnp.log(f), lt["one_m"] * nsig + lt["kcorr"], f, sig, nsig


def _ck(x, ci):
    return x[ci * CHUNK:(ci + 1) * CHUNK]


def _block_cums(ts, g, nc):
    cs = [_cum3(ts, _ck(g, ci), terms=2) for ci in range(nc)]
    out = {kind: jnp.concatenate([c[CHUNK * i:CHUNK * (i + 1)] for c in cs], axis=0)
           for i, kind in enumerate(_MXU_KINDS)}
    b = out[("c", CHUNK)]
    ng = CHUNK // 8
    last = b.reshape(nc, ng, 8, HG_DIM)[:, :, 7:8, :]
    zero = jnp.zeros((nc, 1, 1, HG_DIM), F32)

    def spread(groups):
        return jnp.broadcast_to(jnp.concatenate(groups, axis=1), (nc, ng, 8, HG_DIM)).reshape(nc * CHUNK, HG_DIM)

    def get(kind):
        if kind in out:
            return out[kind]
        c, L = kind
        nb = L // 8
        first = lambda r: (r // nb) * nb
        if c == "c":
            return b - spread([last[:, first(r) - 1:first(r)] if r >= nb else zero for r in range(ng)])
        return spread([last[:, first(r) + nb - 1:first(r) + nb] for r in range(ng)]) - b

    return get


def _level_factors(cums, g, L):
    if L == 0:
        return None, None
    if L == 1:
        return jnp.exp(g[...]), None
    return jnp.exp(cums(("c", L))), jnp.exp(cums(("r", L)))


def _mul(a, e):
    return a if e is None else a * e


def _hg_block_fwd(qf, k, v, g, ts, m_ref, nc):
    cums = _block_cums(ts, g, nc)
    amat = [jnp.zeros((CHUNK, CHUNK), F32)] * nc
    for li, L in enumerate(_LEVELS):
        eq, ek = _level_factors(cums, g, L)
        ql, kl, m = _mul(qf, eq), _mul(k, ek), m_ref[li]
        amat = [amat[ci] + _dot(_ck(ql, ci), _ck(kl, ci), NT) * m for ci in range(nc)]
    b = cums(("c", CHUNK))
    kst = k * jnp.exp(cums(("r", CHUNK)))
    o = [_dot(amat[ci], _ck(v, ci), NN) for ci in range(nc)]
    kv = [_dot(_ck(v, ci), _ck(kst, ci), TN) for ci in range(nc)]
    dec = [jnp.exp(b[(ci + 1) * CHUNK - 1:(ci + 1) * CHUNK, :]) for ci in range(nc)]
    return o, dec, kv, qf * jnp.exp(b), amat


def _hg_block_bwd(qf, k, v, g, do, amat, ts, m_ref, nc):
    cums = _block_cums(ts, g, nc)
    dcs = {}
    da = [_dot(_ck(do, ci), _ck(v, ci), NT) for ci in range(nc)]
    dq = jnp.zeros(qf.shape, F32)
    dk = jnp.zeros(qf.shape, F32)
    dg = jnp.zeros(qf.shape, F32)
    for li, L in enumerate(_LEVELS):
        eq, ek = _level_factors(cums, g, L)
        qlb, klb, m = _mul(qf[...], eq).astype(BF16), _mul(k[...], ek).astype(BF16), m_ref[li]
        dal = [(da[ci] * m).astype(BF16) for ci in range(nc)]
        both = [(_dot(dal[ci], _ck(klb, ci), NN), _dot(dal[ci], _ck(qlb, ci), TN)) for ci in range(nc)]
        dql = _mul(jnp.concatenate([p[0] for p in both], axis=0), eq)
        dkl = _mul(jnp.concatenate([p[1] for p in both], axis=0), ek)
        dq = dq + dql
        dk = dk + dkl
        if L == 1:
            dg = dg + dql * qf[...]
        elif L > 1:
            dcs[("c", L)] = (dql * qf[...]).astype(BF16)
            dcs[("r", L)] = (dkl * k[...]).astype(BF16)
    b = cums(("c", CHUNK))
    e64 = jnp.exp(b)
    er64 = jnp.exp(cums(("r", CHUNK)))
    qb = (qf[...] * e64).astype(BF16)
    return dict(dq=dq, dk=dk, dg=dg, dcs=dcs, e64=e64, er64=er64, qf=qf, k=k, kst=(k[...] * er64).astype(BF16),
                dv=[_dot(amat[ci], _ck(do, ci), TN) for ci in range(nc)],
                dec=[jnp.exp(b[(ci + 1) * CHUNK - 1:(ci + 1) * CHUNK, :]) for ci in range(nc)],
                qd=[_dot(_ck(do, ci), _ck(qb, ci), TN) for ci in range(nc)])


def _hg_state_bwd(w, v, do, starts, ends, tst, nc):
    dqb = jnp.concatenate([_dot(_ck(do, ci), starts[ci], NN) for ci in range(nc)], axis=0)
    dkst = jnp.concatenate([_dot(_ck(v, ci), ends[ci], NN) for ci in range(nc)], axis=0)
    dqb, dkst = dqb * w["e64"], dkst * w["er64"]
    dq = w["dq"] + dqb
    dk = w["dk"] + dkst
    dv = jnp.concatenate([w["dv"][ci] + _dot(_ck(w["kst"], ci), ends[ci], NT) for ci in range(nc)], axis=0)
    trow = lax.broadcasted_iota(jnp.int32, (CHUNK, 1), 0)
    dtot = jnp.concatenate(
        [jnp.where(trow == CHUNK - 1, jnp.sum(ends[ci] * starts[ci], axis=0, keepdims=True) * w["dec"][ci], 0.0)
         for ci in range(nc)], axis=0)
    dcs = dict(w["dcs"])
    dcs[("c", CHUNK)] = (dqb * w["qf"][...] + dtot).astype(BF16)
    dcs[("r", CHUNK)] = (dkst * w["k"][...]).astype(BF16)
    dgs = [_dot(tst, jnp.concatenate([_ck(dcs[kind], ci) for kind in _ALL_KINDS], axis=0), NN) for ci in range(nc)]
    return dq, dk, dv, w["dg"] + jnp.concatenate(dgs, axis=0)


def _hgrn_fwd(proj_h, u_rows, lb_param, g_head, layer, name, phase=None):
    B, S, _ = proj_h.shape
    sb = _pick(S, (2048, 1024, 512, 256, 128, 64))
    nc = sb // CHUNK
    ts, _ = _cum_matrices()

    def body(*refs):
        ins, outs, (st,), p_in, p_out, p_sems = _split_refs(refs, 8, 12, 1, phase)
        q_ref, f_ref, i_ref, z_ref, lbp_ref, gh_ref, ts_ref, m_ref = ins
        o_ref, u_ref, sts_ref, am_ref = outs[:4]
        logf_ref, k_ref, qf_ref, sg_ref, qg_ref, zg_ref, fg_ref, sig_ref = outs[4:]
        h_id, b_id, s_id = pl.program_id(0), pl.program_id(1), pl.program_id(2)
        _hosted_start(phase, p_in, p_out, p_sems, (h_id == 0) & (b_id == 0) & (s_id == 0))

        @pl.when(s_id == 0)
        def _():
            st[...] = jnp.zeros_like(st)

        lt = _lb_terms(lbp_ref[...], layer)
        tsv = ts_ref[...]
        gh = gh_ref[...]
        logf, k, _, sig, nsig = _gate(f_ref[...], lt)
        qf, qf_grad = _silu_and_grad(q_ref[...])
        sg, sg_grad = _silu_and_grad(z_ref[...])
        logf_ref[...], k_ref[...], qf_ref[...], sg_ref[...] = logf, k, qf, sg
        qg_ref[...] = qf_grad.astype(BF16)
        zg_ref[...] = sg_grad.astype(BF16)
        fg_ref[...] = (lt["one_m"] * sig * nsig).astype(BF16)
        sig_ref[...] = sig.astype(BF16)
        o_part, dec, kv, qb, amat = _hg_block_fwd(qf, k, i_ref[...], logf, tsv, m_ref, nc)
        for ci in range(nc):
            am_ref[ci] = amat[ci].astype(BF16)
        cur = st[...]
        starts = []
        for ci in range(nc):
            sts_ref[ci] = cur
            starts.append(cur)
            cur = cur * dec[ci] + kv[ci]
        st[...] = cur
        o = jnp.concatenate([o_part[ci] + _dot(_ck(qb, ci), starts[ci], NT) for ci in range(nc)], axis=0)
        o_ref[...] = o
        r = lax.rsqrt(jnp.mean(o * o, axis=-1, keepdims=True) + NORM_EPS)
        u_ref[...] = (((o * r) * gh) * sg).astype(BF16)
        _hosted_finish(phase, p_in, p_out, p_sems, (h_id == HG_HEADS - 1) & (b_id == B - 1) & (s_id == S // sb - 1))

    col = lambda base: pl.BlockSpec((None, sb, HG_DIM), lambda h, b, s: (b, s, base + h))
    p_ispecs, p_ospecs, p_oshapes, p_alias, p_scratch, p_args = _host_phase(phase, 8, 12)
    wide = lambda dt: jax.ShapeDtypeStruct((B, S, HG_WIDTH), dt)
    res = pl.pallas_call(
        body, name=name,
        grid=(HG_HEADS, B, S // sb),
        in_specs=[col(0), col(HG_HEADS), col(2 * HG_HEADS), col(3 * HG_HEADS),
                  pl.BlockSpec((DEPTH, HG_DIM), lambda h, b, s: (0, h)),
                  pl.BlockSpec((1, HG_DIM), lambda h, b, s: (0, 0)),
                  pl.BlockSpec((N_CUM_F, CHUNK), lambda h, b, s: (0, 0)),
                  pl.BlockSpec((len(_LEVELS), CHUNK, CHUNK), lambda h, b, s: (0, 0, 0))] + p_ispecs,
        out_specs=[col(0), col(0),
                   pl.BlockSpec((None, None, nc, HG_DIM, HG_DIM), lambda h, b, s: (b, h, s, 0, 0)),
                   pl.BlockSpec((None, None, nc, CHUNK, CHUNK), lambda h, b, s: (b, h, s, 0, 0))]
        + [col(0)] * 8 + p_ospecs,
        out_shape=[wide(F32),
                   jax.ShapeDtypeStruct((B, S, u_rows), BF16),
                   jax.ShapeDtypeStruct((B, HG_HEADS, S // CHUNK, HG_DIM, HG_DIM), F32),
                   jax.ShapeDtypeStruct((B, HG_HEADS, S // CHUNK, CHUNK, CHUNK), BF16)]
        + [wide(F32)] * 4 + [wide(BF16)] * 4 + p_oshapes,
        input_output_aliases=p_alias,
        scratch_shapes=[pltpu.VMEM((HG_DIM, HG_DIM), F32)] + p_scratch,
        compiler_params=_params(("arbitrary", "arbitrary", "arbitrary")),
    )(proj_h, proj_h, proj_h, proj_h, lb_param, g_head, ts, _level_masks(), *p_args)
    return res[0], res[1], tuple(res[2:12]), list(res[12:])


def _hgrn_bwd(proj_h, o_h, du, kept, lb_param, g_head, layer, name, phase=None):
    B, S, _ = proj_h.shape
    sb = _pick(S, (512, 256, 128, 64))
    nc = sb // CHUNK
    ns = S // sb
    ts, tst = _cum_matrices()

    def body(*refs):
        ins, outs, (dst,), p_in, p_out, p_sems = _split_refs(refs, 18, 6, 1, phase)
        (i_ref, o_ref, du_ref, sts_ref, am_ref, logf_ref, k_ref, qf_ref, sg_ref, qg_ref, zg_ref, fg_ref, sig_ref,
         lbp_ref, gh_ref, ts_ref, tst_ref, m_ref) = ins
        dq_ref, df_ref, di_ref, dz_ref, dlb_ref, dgh_ref = outs
        h_id, b_id, s_id = pl.program_id(0), pl.program_id(1), pl.program_id(2)
        _hosted_start(phase, p_in, p_out, p_sems, (h_id == 0) & (b_id == 0) & (s_id == 0))

        @pl.when(s_id == 0)
        def _():
            dst[...] = jnp.zeros_like(dst)

        @pl.when((b_id == 0) & (s_id == 0))
        def _():
            dlb_ref[...] = jnp.zeros_like(dlb_ref)

        @pl.when((h_id == 0) & (b_id == 0) & (s_id == 0))
        def _():
            dgh_ref[...] = jnp.zeros_like(dgh_ref)

        lt = _lb_terms(lbp_ref[...], layer)
        gh = gh_ref[...]
        tsv = ts_ref[...]
        tstv = tst_ref[...]
        sg = sg_ref[...]
        o = o_ref[...]
        dub = du_ref[...]
        r = lax.rsqrt(jnp.mean(o * o, axis=-1, keepdims=True) + NORM_EPS)
        n = o * r
        dz_ref[...] = (dub * (n * gh) * zg_ref[...].astype(F32)).astype(BF16)
        dgh_ref[...] += jnp.sum(dub * sg * n, axis=0, keepdims=True)
        dn = dub * sg * gh
        do = (r * (dn - n * jnp.mean(dn * n, axis=-1, keepdims=True))).astype(BF16)
        v = i_ref[...].astype(BF16)
        w = _hg_block_bwd(qf_ref, k_ref, v, logf_ref, do, [am_ref[ci] for ci in range(nc)], tsv, m_ref, nc)
        cur = dst[...]
        ends = [None] * nc
        for ci in reversed(range(nc)):
            ends[ci] = cur
            cur = cur * w["dec"][ci] + w["qd"][ci]
        dst[...] = cur
        dq, dk, dv, dg = _hg_state_bwd(w, v, do, [sts_ref[ci] for ci in range(nc)], ends, tstv, nc)
        di_ref[...] = dv.astype(BF16)
        dq_ref[...] = (dq * qg_ref[...].astype(F32)).astype(BF16)
        f = jnp.exp(logf_ref[...])
        scaled = (dg - f * dk) / f
        df_ref[...] = (scaled * fg_ref[...].astype(F32)).astype(BF16)
        dlb_ref[...] += jnp.sum(scaled * (lt["ind"] - sig_ref[...].astype(F32)), axis=0, keepdims=True)
        _hosted_finish(phase, p_in, p_out, p_sems, (h_id == HG_HEADS - 1) & (b_id == B - 1) & (s_id == ns - 1))

    col = lambda base: pl.BlockSpec((None, sb, HG_DIM), lambda h, b, s: (b, ns - 1 - s, base + h))
    out_col = pl.BlockSpec((None, sb, HG_DIM), lambda h, b, s: (b, ns - 1 - s, h))
    dt = jax.ShapeDtypeStruct((B, S, HG_WIDTH), BF16)
    p_ispecs, p_ospecs, p_oshapes, p_alias, p_scratch, p_args = _host_phase(phase, 18, 6)
    res = pl.pallas_call(
        body, name=name,
        grid=(HG_HEADS, B, ns),
        in_specs=[col(2 * HG_HEADS), col(0), col(0),
                  pl.BlockSpec((None, None, nc, HG_DIM, HG_DIM), lambda h, b, s: (b, h, ns - 1 - s, 0, 0)),
                  pl.BlockSpec((None, None, nc, CHUNK, CHUNK), lambda h, b, s: (b, h, ns - 1 - s, 0, 0))]
        + [col(0)] * 8
        + [pl.BlockSpec((DEPTH, HG_DIM), lambda h, b, s: (0, h)),
           pl.BlockSpec((1, HG_DIM), lambda h, b, s: (0, 0)),
           pl.BlockSpec((N_CUM_F, CHUNK), lambda h, b, s: (0, 0)),
           pl.BlockSpec((CHUNK, N_CUM), lambda h, b, s: (0, 0)),
           pl.BlockSpec((len(_LEVELS), CHUNK, CHUNK), lambda h, b, s: (0, 0, 0))] + p_ispecs,
        out_specs=[out_col, out_col, out_col, out_col,
                   pl.BlockSpec((1, HG_DIM), lambda h, b, s: (0, h)),
                   pl.BlockSpec((1, HG_DIM), lambda h, b, s: (0, 0))] + p_ospecs,
        out_shape=[dt, dt, dt, dt, jax.ShapeDtypeStruct((1, HG_WIDTH), F32),
                   jax.ShapeDtypeStruct((1, HG_DIM), F32)] + p_oshapes,
        input_output_aliases=p_alias,
        scratch_shapes=[pltpu.VMEM((HG_DIM, HG_DIM), F32)] + p_scratch,
        compiler_params=_params(("arbitrary", "arbitrary", "arbitrary")),
    )(proj_h, o_h, du, *kept, lb_param, g_head, ts, tst, _level_masks(), *p_args)
    return tuple(res[:6]) + (list(res[6:]),)


def _rope_tables(S):
    half = ATT_DIM // 2
    inv_freq = np.float32(ROPE_THETA) ** (-np.arange(half, dtype=np.float32) / half)
    ang = np.arange(S, dtype=np.float32)[:, None] * inv_freq[None, :]
    cos = np.cos(ang)
    sin = np.sin(ang)
    cos = np.concatenate([cos, cos, cos, cos], axis=1)
    sin = np.concatenate([-sin, sin, -sin, sin], axis=1)
    return jnp.asarray(cos, F32), jnp.asarray(sin, F32)


def _attn_common():
    lane = lax.broadcasted_iota(jnp.int32, (1, 2 * ATT_DIM), 1)
    first_half = (lane % ATT_DIM) < (ATT_DIM // 2)
    left = lane < ATT_DIM

    def swap(x):
        return jnp.where(first_half, pltpu.roll(x, 128 - ATT_DIM // 2, 1), pltpu.roll(x, ATT_DIM // 2, 1))

    def rope(x, cos, sin):
        return x * cos + swap(x) * sin

    def rope_bwd(dy, cos, sin):
        return dy * cos + swap(dy * sin)

    def dup(x):
        xs = pltpu.roll(x, ATT_DIM, 1)
        return [jnp.where(left, x, xs), jnp.where(left, xs, x)]

    return left, rope, rope_bwd, dup


GROUP = ATT_HEADS // 2
GROUP_ROWS = GROUP * ATT_BLOCK


def _attn_bias(i):
    r = lax.broadcasted_iota(jnp.int32, (ATT_BLOCK, 2 * ATT_BLOCK), 0)
    c = lax.broadcasted_iota(jnp.int32, (ATT_BLOCK, 2 * ATT_BLOCK), 1)
    ok = (c > r) & (c <= r + ATT_BLOCK) & ((c >= ATT_BLOCK) | (i > 0))
    return jnp.where(ok, 0.0, NEG_INF)


def _stack_heads(pairs, left):
    rows = []
    for x in pairs:
        rows += [jnp.where(left, x, 0.0), jnp.where(left, 0.0, x)]
    return jnp.concatenate(rows, axis=0)


def _unstack_heads(y, left, pp):
    r0 = 2 * pp * ATT_BLOCK
    return jnp.where(left, y[r0:r0 + ATT_BLOCK], y[r0 + ATT_BLOCK:r0 + 2 * ATT_BLOCK])


def _row_sums(x):
    return _dot(x, jnp.ones((x.shape[1], 128), BF16), NN)


def _attn_probs(qs, kd, vd, sink, bias):
    n = range(len(qs))
    rows = qs[0].shape[0]
    s = [(_dot(qs[j], kd[j], NT).reshape(rows // ATT_BLOCK, ATT_BLOCK, 2 * ATT_BLOCK) * ATT_SCALE + bias[None])
         .reshape(rows, 2 * ATT_BLOCK) for j in n]
    m = [jnp.max(jnp.maximum(jnp.maximum(s[j][:, :128], s[j][:, 128:]), sink[j]), axis=-1, keepdims=True) for j in n]
    pu = [jnp.exp(s[j] - m[j]) for j in n]
    es = [jnp.exp(sink[j] - m[j]) for j in n]
    ones = jnp.ones((2 * ATT_BLOCK, 128), BF16)
    ov = [_dot(pu[j], jnp.concatenate([vd[j].astype(BF16), ones], axis=1), NN) for j in n]
    inv = [1.0 / (ov[j][:, 128:] + es[j]) for j in n]
    return ([pu[j] * jnp.concatenate([inv[j], inv[j]], axis=1) for j in n], [es[j] * inv[j] for j in n],
            [ov[j][:, :128] * inv[j] for j in n])


def _sink_rows(sinks_l):
    return jnp.broadcast_to(jnp.repeat(sinks_l, ATT_BLOCK)[:, None], (ATT_HEADS * ATT_BLOCK, 128))


_Z0 = (2 * ATT_WIDTH + 2 * KV_WIDTH - ATT_WIDTH) // 256


def _attn_fwd(proj_a, u, sinks_l, cos, sin, name, phase=None):
    B, S, _ = proj_a.shape
    nb = S // ATT_BLOCK

    def body(*refs):
        ins, (u_ref, p_ref, o_ref, ps_ref, qs_ref), _, p_in, p_out, p_sems = _split_refs(refs, 13, 5, 0, phase)
        q_ref, kvc_ref, kvp_ref, z0, z1, z2, z3, cos_ref, sin_ref, cosp_ref, sinp_ref, sinks_ref, _ = ins
        i = pl.program_id(1)
        _hosted_start(phase, p_in, p_out, p_sems, (pl.program_id(0) == 0) & (i == 0))
        left, rope, _, dup = _attn_common()
        cos_c, sin_c = cos_ref[...], sin_ref[...]
        kvc = kvc_ref[...]
        kvp = kvp_ref[...]
        kw = jnp.concatenate([rope(kvp[:, :KV_WIDTH], cosp_ref[...], sinp_ref[...]),
                              rope(kvc[:, :KV_WIDTH], cos_c, sin_c)], axis=0)
        vw = jnp.concatenate([kvp[:, KV_WIDTH:], kvc[:, KV_WIDTH:]], axis=0)
        kd, vd = dup(kw), dup(vw)
        bias = _attn_bias(i)
        zs = (z0, z1, z2, z3)
        pairs = [range(4 * kvh, 4 * kvh + 4) for kvh in range(2)]
        qs = [_stack_heads([rope(q_ref[:, 128 * pr:128 * (pr + 1)], cos_c, sin_c) for pr in pairs[kvh]], left)
              for kvh in range(2)]
        sink = [sinks_ref[kvh * GROUP_ROWS:(kvh + 1) * GROUP_ROWS, :] for kvh in range(2)]
        p, ps, o = _attn_probs(qs, kd, vd, sink, bias)
        eye = (lax.broadcasted_iota(jnp.int32, (ATT_BLOCK, 128), 0)
               == lax.broadcasted_iota(jnp.int32, (ATT_BLOCK, 128), 1))
        for kvh in range(2):
            p_ref[kvh] = p[kvh].astype(BF16)
            qs_ref[kvh] = qs[kvh].astype(BF16)
            for g in range(GROUP):
                blk = ps[kvh][g * ATT_BLOCK:(g + 1) * ATT_BLOCK, :]
                ps_ref[kvh * GROUP + g:kvh * GROUP + g + 1, :] = jnp.sum(jnp.where(eye, blk, 0.0), axis=0, keepdims=True)
            for pp, pr in enumerate(pairs[kvh]):
                z = zs[pr // 2][:, 128 * (pr % 2):128 * (pr % 2 + 1)]
                o128 = _unstack_heads(o[kvh], left, pp)
                o_ref[:, 128 * pr:128 * (pr + 1)] = o128.astype(BF16)
                u_ref[:, 128 * pr:128 * (pr + 1)] = (o128 * _silu(z)).astype(BF16)
        _hosted_finish(phase, p_in, p_out, p_sems, (pl.program_id(0) == B - 1) & (i == nb - 1))

    rowblk = lambda w, cb: pl.BlockSpec((None, ATT_BLOCK, w), lambda b, i: (b, i, cb))
    tab = pl.BlockSpec((ATT_BLOCK, 128), lambda b, i: (i, 0))
    tabp = pl.BlockSpec((ATT_BLOCK, 128), lambda b, i: (jnp.maximum(i - 1, 0), 0))
    p_ispecs, p_ospecs, p_oshapes, p_alias, p_scratch, p_args = _host_phase(phase, 13, 5)
    res = pl.pallas_call(
        body, name=name,
        grid=(B, nb),
        in_specs=[rowblk(ATT_WIDTH, 0), rowblk(256, 4),
                  pl.BlockSpec((None, ATT_BLOCK, 256), lambda b, i: (b, jnp.maximum(i - 1, 0), 4)),
                  rowblk(256, _Z0), rowblk(256, _Z0 + 1), rowblk(256, _Z0 + 2), rowblk(256, _Z0 + 3),
                  tab, tab, tabp, tabp,
                  pl.BlockSpec((ATT_HEADS * ATT_BLOCK, 128), lambda b, i: (0, 0)),
                  pl.BlockSpec(memory_space=pl.ANY)] + p_ispecs,
        out_specs=[pl.BlockSpec((None, ATT_BLOCK, ATT_WIDTH), lambda b, i: (b, i, 1)),
                   pl.BlockSpec((None, None, 2, GROUP_ROWS, 2 * ATT_BLOCK), lambda b, i: (b, i, 0, 0, 0)),
                   pl.BlockSpec((None, ATT_BLOCK, ATT_WIDTH), lambda b, i: (b, i, 0)),
                   pl.BlockSpec((None, None, ATT_HEADS, 128), lambda b, i: (b, i, 0, 0)),
                   pl.BlockSpec((None, None, 2, GROUP_ROWS, 128), lambda b, i: (b, i, 0, 0, 0))] + p_ospecs,
        out_shape=[jax.ShapeDtypeStruct(u.shape, BF16),
                   jax.ShapeDtypeStruct((B, nb, 2, GROUP_ROWS, 2 * ATT_BLOCK), BF16),
                   jax.ShapeDtypeStruct((B, S, ATT_WIDTH), BF16),
                   jax.ShapeDtypeStruct((B, nb, ATT_HEADS, 128), F32),
                   jax.ShapeDtypeStruct((B, nb, 2, GROUP_ROWS, 128), BF16)] + p_oshapes,
        input_output_aliases={12: 0, **p_alias},
        scratch_shapes=p_scratch,
        compiler_params=_params(("arbitrary", "arbitrary")),
    )(proj_a, proj_a, proj_a, proj_a, proj_a, proj_a, proj_a, cos, sin, cos, sin, sinks_l, u, *p_args)
    return res[0], tuple(res[1:5]), list(res[5:])


def _attn_bwd(proj_a, du, kept, cos, sin, name, phase=None):
    B, S, _ = proj_a.shape
    nb = S // ATT_BLOCK
    p_kept, o_kept, ps_kept, qs_kept = kept

    def body(*refs):
        ins, outs, (carry, sk_acc), p_in, p_out, p_sems = _split_refs(refs, 15, 4, 2, phase)
        (qs_ref, kvc_ref, kvp_ref, z0, z1, z2, z3, du_ref, cos_ref, sin_ref, cosp_ref, sinp_ref,
         p_ref, o_ref, ps_ref) = ins
        dq_ref, dkv_ref, dz_ref, dsk_ref = outs
        b_id, i = pl.program_id(0), pl.program_id(1)
        _hosted_start(phase, p_in, p_out, p_sems, (b_id == 0) & (i == 0))

        @pl.when((b_id == 0) & (i == 0))
        def _():
            sk_acc[...] = jnp.zeros_like(sk_acc)

        @pl.when(i == 0)
        def _():
            carry[...] = jnp.zeros_like(carry)

        @pl.when(i < nb)
        def _():
            left, rope, rope_bwd, dup = _attn_common()
            cos_c, sin_c = cos_ref[...], sin_ref[...]
            cos_p, sin_p = cosp_ref[...], sinp_ref[...]
            kvc = kvc_ref[...]
            kvp = kvp_ref[...]
            kw = jnp.concatenate([rope(kvp[:, :KV_WIDTH], cos_p, sin_p), rope(kvc[:, :KV_WIDTH], cos_c, sin_c)], axis=0)
            vw = jnp.concatenate([kvp[:, KV_WIDTH:], kvc[:, KV_WIDTH:]], axis=0)
            kd, vd = dup(kw), dup(vw)
            zs = (z0, z1, z2, z3)
            units = [(kvh, hf) for kvh in range(2) for hf in range(2)]
            half = GROUP_ROWS // 2
            pairs = [range(4 * kvh + 2 * hf, 4 * kvh + 2 * hf + 2) for kvh, hf in units]
            ku = [kd[kvh] for kvh, _ in units]
            vu = [vd[kvh] for kvh, _ in units]
            ps_all = ps_ref[...]
            head_row = lax.broadcasted_iota(jnp.int32, (ATT_HEADS, 128), 0)
            eye = (lax.broadcasted_iota(jnp.int32, (ATT_BLOCK, 128), 0)
                   == lax.broadcasted_iota(jnp.int32, (ATT_BLOCK, 128), 1))

            def first(j):
                kvh, hf = units[j]
                p = p_ref[kvh, hf * half:(hf + 1) * half, :]
                parts = []
                for pr in pairs[j]:
                    cols = slice(128 * pr, 128 * (pr + 1))
                    sg, sg_grad = _silu_and_grad(zs[pr // 2][:, 128 * (pr % 2):128 * (pr % 2 + 1)])
                    du128 = du_ref[:, cols]
                    dz_ref[:, cols] = (du128 * o_ref[:, cols].astype(F32) * sg_grad).astype(BF16)
                    parts.append(du128 * sg)
                dos = _stack_heads(parts, left)
                dp = _dot(dos, vu[j], NT)
                delta = _row_sums(p.astype(F32) * dp)
                ds = (p.astype(F32) * (dp - jnp.concatenate([delta, delta], axis=1)) * ATT_SCALE).astype(BF16)
                sk = jnp.zeros((ATT_HEADS, 128), F32)
                for hh in range(4):
                    hd = kvh * GROUP + 4 * hf + hh
                    drow = jnp.sum(jnp.where(eye, delta[hh * ATT_BLOCK:(hh + 1) * ATT_BLOCK, :], 0.0), axis=0,
                                   keepdims=True)
                    sk = sk - jnp.where(head_row == hd, ps_all * drow, 0.0)
                sk_acc[...] += sk
                return ds, p, dos.astype(BF16), qs_ref[kvh, hf * half:(hf + 1) * half, :]

            def second(j, ds, p, dos, qs):
                dqs = _dot(ds, ku[j], NN)
                for pp, pr in enumerate(pairs[j]):
                    dq_ref[:, 128 * pr:128 * (pr + 1)] = rope_bwd(_unstack_heads(dqs, left, pp),
                                                                  cos_c, sin_c).astype(BF16)
                return _dot(ds, qs, TN), _dot(p, dos, TN)

            got, dku, dvu = {}, [None] * len(units), [None] * len(units)
            for j in range(len(units) + 1):
                if j < len(units):
                    got[j] = first(j)
                if j >= 1:
                    dku[j - 1], dvu[j - 1] = second(j - 1, *got.pop(j - 1))
            dkd = [dku[0] + dku[1], dku[2] + dku[3]]
            dvd = [dvu[0] + dvu[1], dvu[2] + dvu[3]]
            fold = lambda pr: jnp.where(left, pr[0] + pltpu.roll(pr[0], ATT_DIM, 1), pr[1] + pltpu.roll(pr[1], ATT_DIM, 1))
            dkw = fold(dkd)
            dvw = fold(dvd)
            prev = jnp.concatenate([rope_bwd(dkw[:ATT_BLOCK], cos_p, sin_p), dvw[:ATT_BLOCK]], axis=1)
            cur = jnp.concatenate([rope_bwd(dkw[ATT_BLOCK:], cos_c, sin_c), dvw[ATT_BLOCK:]], axis=1)
            dkv_ref[...] = (carry[...] + prev).astype(BF16)
            carry[...] = cur

        @pl.when(i == nb)
        def _():
            dkv_ref[...] = carry[...].astype(BF16)

        @pl.when((b_id == B - 1) & (i == nb))
        def _():
            diag = (lax.broadcasted_iota(jnp.int32, (ATT_HEADS, 128), 0)
                    == lax.broadcasted_iota(jnp.int32, (ATT_HEADS, 128), 1))
            tot = jnp.sum(sk_acc[...], axis=1, keepdims=True)
            dsk_ref[...] = jnp.sum(jnp.where(diag, tot, 0.0), axis=0, keepdims=True)

        _hosted_finish(phase, p_in, p_out, p_sems, (b_id == B - 1) & (i == nb))

    cl = lambda i: jnp.minimum(i, nb - 1)
    pv = lambda i: jnp.maximum(jnp.minimum(i, nb - 1) - 1, 0)
    rowblk = lambda w, cb: pl.BlockSpec((None, ATT_BLOCK, w), lambda b, i: (b, cl(i), cb))
    tab = pl.BlockSpec((ATT_BLOCK, 128), lambda b, i: (cl(i), 0))
    tabp = pl.BlockSpec((ATT_BLOCK, 128), lambda b, i: (pv(i), 0))
    p_ispecs, p_ospecs, p_oshapes, p_alias, p_scratch, p_args = _host_phase(phase, 15, 4)
    res = pl.pallas_call(
        body, name=name,
        grid=(B, nb + 1),
        in_specs=[pl.BlockSpec((None, None, 2, GROUP_ROWS, 128), lambda b, i: (b, cl(i), 0, 0, 0)), rowblk(256, 4),
                  pl.BlockSpec((None, ATT_BLOCK, 256), lambda b, i: (b, pv(i), 4)),
                  rowblk(256, _Z0), rowblk(256, _Z0 + 1), rowblk(256, _Z0 + 2), rowblk(256, _Z0 + 3),
                  rowblk(ATT_WIDTH, 1),
                  tab, tab, tabp, tabp,
                  pl.BlockSpec((None, None, 2, GROUP_ROWS, 2 * ATT_BLOCK), lambda b, i: (b, cl(i), 0, 0, 0)),
                  rowblk(ATT_WIDTH, 0),
                  pl.BlockSpec((None, None, ATT_HEADS, 128), lambda b, i: (b, cl(i), 0, 0))] + p_ispecs,
        out_specs=[rowblk(ATT_WIDTH, 0),
                   pl.BlockSpec((None, ATT_BLOCK, 256), lambda b, i: (b, jnp.maximum(i - 1, 0), 0)),
                   rowblk(ATT_WIDTH, 0),
                   pl.BlockSpec((1, 128), lambda b, i: (0, 0))] + p_ospecs,
        out_shape=[jax.ShapeDtypeStruct((B, S, ATT_WIDTH), BF16), jax.ShapeDtypeStruct((B, S, 256), BF16),
                   jax.ShapeDtypeStruct((B, S, ATT_WIDTH), BF16), jax.ShapeDtypeStruct((1, 128), F32)] + p_oshapes,
        input_output_aliases=p_alias,
        scratch_shapes=[pltpu.VMEM((ATT_BLOCK, 256), F32), pltpu.VMEM((ATT_HEADS, 128), F32)] + p_scratch,
        compiler_params=_params(("arbitrary", "arbitrary")),
    )(qs_kept, proj_a, proj_a, proj_a, proj_a, proj_a, proj_a, du, cos, sin, cos, sin, p_kept, o_kept, ps_kept, *p_args)
    return tuple(res[:4]) + (list(res[4:]),)


def _outproj_fwd(u2, w_out, x2, g_post, target2, name):
    T, D = x2.shape
    tm = _pick(T, (512, 256, 128))
    last = target2 is not None

    def body(u_ref, w_ref, x_ref, g_ref, *rest):
        y = lax.dot_general(u_ref[...], w_ref[...], (NN, ((), ())), preferred_element_type=F32)
        r = lax.rsqrt(jnp.mean(y * y, axis=-1, keepdims=True) + NORM_EPS)
        xn = x_ref[...] + (y * r) * g_ref[...]
        if last:
            t_ref, y_ref, dx_ref, loss_ref = rest
            err = xn - t_ref[...]
            dx_ref[...] = err * (1.0 / D)
            sq = err * err
            acc = sq[:, 0:128]
            for kk in range(1, D // 128):
                acc = acc + sq[:, 128 * kk:128 * (kk + 1)]
            part = jnp.sum(acc.reshape(tm // 8, 8, 128), axis=0) * (0.5 / D)

            @pl.when(pl.program_id(0) == 0)
            def _():
                loss_ref[...] = jnp.zeros_like(loss_ref)

            loss_ref[...] += part
        else:
            y_ref, xn_ref = rest
            xn_ref[...] = xn
        y_ref[...] = y

    row = pl.BlockSpec((tm, D), lambda i: (i, 0))
    in_specs = [pl.BlockSpec((tm, MIX_WIDTH), lambda i: (i, 0)),
                pl.BlockSpec((MIX_WIDTH, D), lambda i: (0, 0)), row,
                pl.BlockSpec((1, D), lambda i: (0, 0))]
    args = [u2, w_out, x2, g_post]
    out_specs = [row, row]
    out_shape = [jax.ShapeDtypeStruct((T, D), F32), jax.ShapeDtypeStruct((T, D), F32)]
    if last:
        in_specs.append(row)
        args.append(target2)
        out_specs.append(pl.BlockSpec((8, 128), lambda i: (0, 0)))
        out_shape.append(jax.ShapeDtypeStruct((8, 128), F32))
    return pl.pallas_call(
        body, name=name, grid=(T // tm,), in_specs=in_specs, out_specs=out_specs, out_shape=out_shape,
        compiler_params=_params(("arbitrary",)),
    )(*args)


def _outproj_bwd(dxn2, y2, g_post, w_out, u2, name):
    T, D = y2.shape
    N = w_out.shape[0]
    tm = _pick(T, (512, 256, 128))
    nt = T // tm

    def body(dx_ref, y_ref, g_ref, w_ref, u_ref, dg_ref, du_ref, dw_ref, acc, wacc):
        i = pl.program_id(0)

        @pl.when(i == 0)
        def _():
            acc[...] = jnp.zeros_like(acc)
            wacc[...] = jnp.zeros_like(wacc)

        y = y_ref[...]
        dxn = dx_ref[...]
        r = lax.rsqrt(jnp.mean(y * y, axis=-1, keepdims=True) + NORM_EPS)
        n = y * r
        dn = dxn * g_ref[...]
        dy = (r * (dn - n * jnp.mean(dn * n, axis=-1, keepdims=True))).astype(BF16)
        du_ref[...] = lax.dot_general(dy, w_ref[...], (NT, ((), ())), preferred_element_type=F32)
        wacc[...] += lax.dot_general(u_ref[...], dy, (TN, ((), ())), preferred_element_type=F32)
        acc[...] += jnp.sum((dxn * n).reshape(tm // 8, 8, D), axis=0)

        @pl.when(i == nt - 1)
        def _():
            dg_ref[...] = jnp.sum(acc[...], axis=0, keepdims=True)
            dw_ref[...] = wacc[...].astype(BF16)

    row = pl.BlockSpec((tm, D), lambda i: (i, 0))
    wide = pl.BlockSpec((tm, N), lambda i: (i, 0))
    vec = pl.BlockSpec((1, D), lambda i: (0, 0))
    whole = pl.BlockSpec((N, D), lambda i: (0, 0))
    return pl.pallas_call(
        body, name=name, grid=(nt,),
        in_specs=[row, row, vec, pl.BlockSpec((N, D), lambda i: (0, 0), pipeline_mode=pl.Buffered(1)), wide],
        out_specs=[vec, wide, whole],
        out_shape=[jax.ShapeDtypeStruct((1, D), F32), jax.ShapeDtypeStruct((T, N), F32),
                   jax.ShapeDtypeStruct((N, D), BF16)],
        scratch_shapes=[pltpu.VMEM((8, D), F32), pltpu.VMEM((N, D), F32)],
        compiler_params=_params(("arbitrary",)),
    )(dxn2, y2, g_post, w_out, u2)


def _inproj_bwd(pieces, w_t, x2, dxn2, g_pre, name, phase=None):
    T, D = x2.shape
    widths = [p.shape[1] for p in pieces]
    offs = [sum(widths[:i]) for i in range(len(pieces))]
    n_p = len(pieces)
    tm = _pick(T, (256, 128))
    nt = T // tm

    def body(*refs):
        ins, (dx_ref, dg_ref), (acc,), p_in, p_out, p_sems = _split_refs(refs, n_p + 4, 2, 1, phase)
        w_ref, x_ref, dxn_ref, g_ref = ins[n_p:]
        i = pl.program_id(0)
        _hosted_start(phase, p_in, p_out, p_sems, i == 0)

        @pl.when(i == 0)
        def _():
            acc[...] = jnp.zeros_like(acc)

        dh = jnp.zeros((tm, D), F32)
        for p in range(n_p):
            dh = dh + lax.dot_general(ins[p][...], w_ref[offs[p]:offs[p] + widths[p], :], (NN, ((), ())),
                                      preferred_element_type=F32)
        x = x_ref[...]
        r = lax.rsqrt(jnp.mean(x * x, axis=-1, keepdims=True) + NORM_EPS)
        n = x * r
        dn = dh * g_ref[...]
        dx_ref[...] = dxn_ref[...] + r * (dn - n * jnp.mean(dn * n, axis=-1, keepdims=True))
        acc[...] += jnp.sum((dh * n).reshape(tm // 8, 8, D), axis=0)

        @pl.when(i == nt - 1)
        def _():
            dg_ref[...] = jnp.sum(acc[...], axis=0, keepdims=True)

        _hosted_finish(phase, p_in, p_out, p_sems, i == nt - 1)

    row = pl.BlockSpec((tm, D), lambda i: (i, 0))
    vec = pl.BlockSpec((1, D), lambda i: (0, 0))
    p_ispecs, p_ospecs, p_oshapes, p_alias, p_scratch, p_args = _host_phase(phase, n_p + 4, 2)
    res = pl.pallas_call(
        body, name=name, grid=(nt,),
        in_specs=[pl.BlockSpec((tm, w), lambda i: (i, 0)) for w in widths]
        + [pl.BlockSpec((sum(widths), D), lambda i: (0, 0), pipeline_mode=pl.Buffered(1)), row, row, vec] + p_ispecs,
        out_specs=[row, vec] + p_ospecs,
        out_shape=[jax.ShapeDtypeStruct((T, D), F32), jax.ShapeDtypeStruct((1, D), F32)] + p_oshapes,
        input_output_aliases=p_alias,
        scratch_shapes=[pltpu.VMEM((8, D), F32)] + p_scratch,
        compiler_params=_params(("arbitrary",)),
    )(*pieces, w_t, x2, dxn2, g_pre, *p_args)
    return res[0], res[1], list(res[2:])


def _step(x, target, g_pre, g_post, lb_param, g_head, sinks, shards=None, full=None):
    B, S, D = x.shape
    T = B * S
    dist = shards is not None
    first, last = 0, DEPTH - 1
    if dist:
        a_loc, b_loc = shards
        ra, rb = a_loc.shape[1], b_loc.shape[1]
        side = _own_side_blocks()
        a_full, b_full = _place_own([a_loc, b_loc], side, "place_own")
        w_in0 = _gather_one_call(a_full[0], "gather_in0")
        w_in, w_out = [w_in0, None], [None, None]
    else:
        w_in, w_out = list(full[0]), list(full[1])
    cos, sin = _rope_tables(S)
    saved = []
    xs = x
    loss_part = None
    dxn = None
    for l in range(DEPTH):
        x2 = xs.reshape(T, D)
        proj_h, proj_a, h = _inproj(x2, g_pre[l:l + 1], w_in[l], f"inproj{l}")
        proj_h = proj_h.reshape(B, S, N_H)
        proj_a = proj_a.reshape(B, S, N_A)
        phase = None
        if dist and l == first:
            phase = _gather_ici_phase([a_full[1], b_full[0]])
        if dist and l == last:
            phase = _gather_d2d_phase([w_out1_part], [rb])
        o_h, u, states, got = _hgrn_fwd(proj_h, MIX_WIDTH, lb_param, g_head[l:l + 1], l, f"hgrn_fwd{l}", phase)
        phase = None
        if dist and l == first:
            phase = _merge_phases(_gather_d2d_phase(got, [ra, rb]),
                                  _gather_ici_phase([b_full[1]]))
        if dist and l == last:
            w_out[1] = got[0]
        u, kept_a, got = _attn_fwd(proj_a, u, _sink_rows(sinks[l]), cos, sin, f"attn_fwd{l}", phase)
        if dist and l == first:
            w_in[1], w_out[0], w_out1_part = got
        u2 = u.reshape(T, MIX_WIDTH)
        if l < last:
            y, xn = _outproj_fwd(u2, w_out[l], x2, g_post[l:l + 1], None, f"outproj{l}")
            xn = xn.reshape(B, S, D)
        else:
            y, dxn, loss_part = _outproj_fwd(u2, w_out[l], x2, g_post[l:l + 1], target.reshape(T, D), f"outproj{l}")
            xn = None
        saved.append((x2, h, proj_h, proj_a, o_h, u2, states, kept_a, y))
        xs = xn

    dw_in, dw_out = [None] * DEPTH, [None] * DEPTH
    dg_pre, dg_post, dlb, dg_head, dsinks = [], [], [], [], []
    for l in reversed(range(DEPTH)):
        x2, h, proj_h, proj_a, o_h, u2, states, kept_a, y = saved[l]
        dgp, du, dw_out[l] = _outproj_bwd(dxn, y, g_post[l:l + 1], w_out[l], u2, f"outproj_bwd{l}")
        du = du.reshape(B, S, MIX_WIDTH)
        phase = None
        if dist:
            phase = _reduce_d2d_phase([dw_out[l]], [rb])
            if l == first:
                phase = _merge_phases(_reduce_ici_phase([part_in1]), phase)
        dqh, dfh, dih, dzh, dlb_l, dgh, got = _hgrn_bwd(
            proj_h, o_h, du, states, lb_param, g_head[l:l + 1], l, f"hgrn_bwd{l}", phase)
        if dist:
            if l == first:
                sum_in = _chip_sum(part_in1, got[0], "chip_sum_in1", 1)
            part_out = _pair_sum(dw_out[l], got[-1], side, f"pair_sum_out{l}")
        dqa, dkv, dza, dsk, got = _attn_bwd(proj_a, du, kept_a, cos, sin, f"attn_bwd{l}",
                                            _reduce_ici_phase([part_out]) if dist else None)
        if dist:
            sum_out = _chip_sum(part_out, got[0], f"chip_sum_out{l}", l, None if l == last else sum_out)
        dproj = [p.reshape(T, p.shape[-1]) for p in (dqh, dfh, dih, dzh, dqa, dkv, dza)]
        dw_in[l] = _mm_tn(dproj, h, f"wgrad_in{l}")
        phase = None
        if dist and l == last:
            phase = _reduce_d2d_phase([dw_in[l]], [ra])
        if dist and l == first:
            got = _run_phase(_reduce_d2d_phase([dw_in[l]], [ra]), "reduce_in0_d2d")
            part_in0 = _pair_sum(dw_in[l], got[0], side, "pair_sum_in0")
            phase = _reduce_ici_phase([part_in0])
        dxn, dgpre, got = _inproj_bwd(dproj, w_in[l], x2, dxn, g_pre[l:l + 1], f"inproj_bwd{l}", phase)
        if dist and l == last:
            part_in1 = _pair_sum(dw_in[l], got[0], side, "pair_sum_in1")
        if dist and l == first:
            sum_in = _chip_sum(part_in0, got[0], "chip_sum_in0", 0, sum_in)
        dg_pre.append(dgpre)
        dg_post.append(dgp)
        dlb.append(dlb_l)
        dg_head.append(dgh)
        dsinks.append(dsk)
    rev = lambda lst: jnp.concatenate(lst[::-1], axis=0)
    if not dist:
        sum_in, sum_out = jnp.stack(dw_in), jnp.stack(dw_out)
    return (loss_part, dxn.reshape(B, S, D), sum_in, sum_out,
            rev(dg_pre), rev(dg_post), rev(dlb), rev(dg_head), rev(dsinks))


def _me_and_peers():
    x, y, c = lax.axis_index("x"), lax.axis_index("y"), lax.axis_index("c")
    me = 4 * x + 2 * y + c
    peers = []
    for k in range(1, N_DEV):
        px = 1 - x if k & 4 else x
        py = 1 - y if k & 2 else y
        pc = 1 - c if k & 1 else c
        peers.append(((px, py, pc), 4 * px + 2 * py + pc))
    return me, peers


class _Phase:
    def __init__(self, arrays, out_shapes, aliases, n_send, build):
        self.arrays, self.out_shapes, self.aliases = list(arrays), list(out_shapes), dict(aliases)
        self.n_send, self.build = n_send, build

    def scratch(self):
        return [pltpu.SemaphoreType.DMA((self.n_send,)), pltpu.SemaphoreType.DMA((self.n_send,))]

    def _copies(self, in_refs, out_refs, sems, arrivals):
        send_sems, recv_sems = sems
        sends, recvs = self.build(in_refs, out_refs)
        assert len(sends) == self.n_send == len(recvs)
        out = [pltpu.make_async_remote_copy(src_ref=s, dst_ref=d, send_sem=send_sems.at[i], recv_sem=recv_sems.at[i],
                                            device_id=dev, device_id_type=MESH) for i, (s, d, dev) in enumerate(sends)]
        inc = [pltpu.make_async_remote_copy(src_ref=s, dst_ref=r, send_sem=send_sems.at[i], recv_sem=recv_sems.at[i],
                                            device_id=dev, device_id_type=MESH)
               for i, ((s, _, dev), r) in enumerate(zip(sends, recvs))] if arrivals else []
        return out, inc

    def start(self, in_refs, out_refs, sems):
        out, _ = self._copies(in_refs, out_refs, sems, False)
        for cp in out:
            cp.start()

    def finish(self, in_refs, out_refs, sems):
        out, inc = self._copies(in_refs, out_refs, sems, True)
        for cp in inc:
            cp.wait_recv()
        for cp in out:
            cp.wait_send()


_ANY = pl.BlockSpec(memory_space=pl.ANY)


def _host_phase(phase, n_in, n_out):
    if phase is None:
        return [], [], [], {}, [], []
    aliases = {n_in + i: n_out + o for i, o in phase.aliases.items()}
    return ([_ANY] * len(phase.arrays), [_ANY] * len(phase.out_shapes), phase.out_shapes, aliases, phase.scratch(),
            phase.arrays)


def _split_refs(refs, n_in, n_out, n_scr, phase):
    pi = len(phase.arrays) if phase else 0
    po = len(phase.out_shapes) if phase else 0
    a = n_in + pi
    b = a + n_out + po
    return (refs[:n_in], refs[a:a + n_out], refs[b:b + n_scr], refs[n_in:a], refs[a + n_out:b], refs[b + n_scr:])


def _hosted_start(phase, p_in, p_out, p_sems, first):
    if phase is not None:
        @pl.when(first)
        def _():
            phase.start(p_in, p_out, p_sems)


def _hosted_finish(phase, p_in, p_out, p_sems, last):
    if phase is not None:
        @pl.when(last)
        def _():
            phase.finish(p_in, p_out, p_sems)


def _run_phase(phase, name):
    n_in, n_out = len(phase.arrays), len(phase.out_shapes)

    def body(*refs):
        phase.start(refs[:n_in], refs[n_in:n_in + n_out], refs[n_in + n_out:])
        phase.finish(refs[:n_in], refs[n_in:n_in + n_out], refs[n_in + n_out:])

    return pl.pallas_call(
        body, name=name, in_specs=[_ANY] * n_in, out_specs=[_ANY] * n_out,
        out_shape=phase.out_shapes, input_output_aliases=phase.aliases, scratch_shapes=phase.scratch(),
        compiler_params=pltpu.CompilerParams(has_side_effects=True),
    )(*phase.arrays)


def _gather_one_call(full, name):
    r = full.shape[0] // N_DEV
    half = r // 2

    def body(full_in, full_ref, send_sems, recv_sems):
        del full_in
        c, (own, xn, yn, dg), num = _mesh_place()
        me, sib = num(own, c), (*own, 1 - c)

        def blk(dev, part=None):
            start, n = (dev * r, r) if part is None else (dev * r + part * half, half)
            return full_ref.at[pl.ds(pl.multiple_of(start, 16), n), :]

        def copy(k, src, dev, to, part=None):
            return pltpu.make_async_remote_copy(src_ref=src, dst_ref=blk(dev, part),
                                                send_sem=send_sems.at[k], recv_sem=recv_sems.at[k],
                                                device_id=to, device_id_type=MESH)

        def landed(k, dev, part=None):
            copy(k, blk(dev, part), dev, sib, part).wait_recv()

        sent = []

        def start(*cps):
            for cp in cps:
                cp.start()
                sent.append(cp)

        xs, ys, ds = num(xn, c), num(yn, c), num(dg, c)
        start(copy(0, blk(me), me, sib), copy(1, blk(me), me, (*xn, c)), copy(2, blk(me), me, (*yn, c)))
        landed(1, xs)
        start(copy(3, blk(xs, 0), xs, (*yn, c), 0), copy(5, blk(xs), xs, sib))
        landed(2, ys)
        start(copy(4, blk(ys, 1), ys, (*xn, c), 1), copy(6, blk(ys), ys, sib))
        landed(3, ds, 0)
        landed(4, ds, 1)
        start(copy(7, blk(ds), ds, sib))
        landed(0, num(own, 1 - c))
        for k, ch in ((5, xn), (6, yn), (7, dg)):
            landed(k, num(ch, 1 - c))
        for cp in sent:
            cp.wait_send()

    assert half % 16 == 0
    return pl.pallas_call(
        body, name=name, in_specs=[_ANY], out_specs=_ANY,
        out_shape=jax.ShapeDtypeStruct(full.shape, full.dtype), input_output_aliases={0: 0},
        scratch_shapes=[pltpu.SemaphoreType.DMA((8,)), pltpu.SemaphoreType.DMA((8,))],
        compiler_params=pltpu.CompilerParams(has_side_effects=True),
    )(full)


def _merge_phases(a, b):
    n_in, n_out = len(a.arrays), len(a.out_shapes)
    aliases = dict(a.aliases)
    aliases.update({n_in + i: n_out + o for i, o in b.aliases.items()})

    def build(ins, outs):
        sa, ra = a.build(ins[:n_in], outs[:n_out])
        sb, rb = b.build(ins[n_in:], outs[n_out:])
        return sa + sb, ra + rb

    return _Phase(a.arrays + b.arrays, a.out_shapes + b.out_shapes, aliases, a.n_send + b.n_send, build)


def _mesh_place():
    x, y, c = lax.axis_index("x"), lax.axis_index("y"), lax.axis_index("c")
    chips = [(x, y), (1 - x, y), (x, 1 - y), (1 - x, 1 - y)]
    num = lambda chip, core: 4 * chip[0] + 2 * chip[1] + core
    return c, chips, num


def _own_side_blocks():
    c, chips, num = _mesh_place()
    return jnp.stack([num(ch, c) for ch in chips]).astype(jnp.int32)


def _rows(ref, r, dev):
    return ref.at[pl.ds(pl.multiple_of(dev * r, 16), r), :]


def _place_own(shards, blocks, name):
    n = len(shards)

    def body(idx_ref, *refs):
        del idx_ref
        outs = iter(refs[n:])
        for s_ref in refs[:n]:
            for l in range(DEPTH):
                next(outs)[...] = s_ref[l].astype(BF16)

    whole = lambda s: pl.BlockSpec(s.shape, lambda i, idx: (0, 0, 0))
    own = lambda s: pl.BlockSpec(s.shape[1:], lambda i, idx: (idx[0], 0))
    res = pl.pallas_call(
        body, name=name,
        grid_spec=pltpu.PrefetchScalarGridSpec(
            num_scalar_prefetch=1, grid=(1,),
            in_specs=[whole(s) for s in shards],
            out_specs=[own(s) for s in shards for _ in range(DEPTH)]),
        out_shape=[jax.ShapeDtypeStruct((N_DEV * s.shape[1], s.shape[2]), BF16) for s in shards for _ in range(DEPTH)],
        compiler_params=_params(("arbitrary",)),
    )(blocks, *shards)
    return [list(res[i * DEPTH:(i + 1) * DEPTH]) for i in range(n)]


def _gather_ici_phase(fulls):
    rs = [a.shape[0] // N_DEV for a in fulls]
    n = len(fulls)

    def build(ins, outs):
        del ins
        c, chips, num = _mesh_place()
        me = num(chips[0], c)
        targets = [((*chips[0], 1 - c), num(chips[0], 1 - c))] + [((*ch, c), num(ch, c)) for ch in chips[1:]]
        sends, recvs = [], []
        for dev, dnum in targets:
            for i, r in enumerate(rs):
                sends.append((_rows(outs[i], r, me), _rows(outs[i], r, me), dev))
                recvs.append(_rows(outs[i], r, dnum))
        return sends, recvs

    shapes = [jax.ShapeDtypeStruct(a.shape, a.dtype) for a in fulls]
    return _Phase(list(fulls), shapes, {i: i for i in range(n)}, 4 * n, build)


def _gather_d2d_phase(fulls, rs):
    def build(ins, outs):
        c, chips, num = _mesh_place()
        sib = (*chips[0], 1 - c)
        sends, recvs = [], []
        for ch in chips[1:]:
            for i, r in enumerate(rs):
                blk = _rows(outs[i], r, num(ch, c))
                sends.append((blk, blk, sib))
                recvs.append(_rows(outs[i], r, num(ch, 1 - c)))
        return sends, recvs

    shapes = [jax.ShapeDtypeStruct(a.shape, a.dtype) for a in fulls]
    return _Phase(fulls, shapes, {i: i for i in range(len(fulls))}, 3 * len(fulls), build)


def _reduce_d2d_phase(grads, rs):
    def build(ins, outs):
        c, chips, num = _mesh_place()
        sib = (*chips[0], 1 - c)
        sends, recvs = [], []
        for j, ch in enumerate(chips):
            for i, r in enumerate(rs):
                sends.append((_rows(ins[i], r, num(ch, 1 - c)), outs[i].at[j], sib))
                recvs.append(outs[i].at[j])
        return sends, recvs

    shapes = [jax.ShapeDtypeStruct((4, r, g.shape[1]), g.dtype) for g, r in zip(grads, rs)]
    return _Phase(grads, shapes, {}, 4 * len(grads), build)


def _reduce_ici_phase(parts):
    def build(ins, outs):
        c, chips, _ = _mesh_place()
        sends, recvs = [], []
        for t in range(1, 4):
            for i in range(len(parts)):
                sends.append((ins[i].at[t], outs[i].at[t - 1], (*chips[t], c)))
                recvs.append(outs[i].at[t - 1])
        return sends, recvs

    shapes = [jax.ShapeDtypeStruct((3,) + p.shape[1:], p.dtype) for p in parts]
    return _Phase(parts, shapes, {}, 3 * len(parts), build)


def _pair_sum(g, got, blocks, name):
    n, r, D = got.shape
    tr = _pick(r, (800, 400, 256, 200, 128, 64, 16))

    def body(idx_ref, g_ref, r_ref, o_ref):
        del idx_ref
        o_ref[...] = (g_ref[...].astype(F32) + r_ref[...].astype(F32)).astype(o_ref.dtype)

    blk = pl.BlockSpec((None, tr, D), lambda j, i, idx: (j, i, 0))
    return pl.pallas_call(
        body, name=name,
        grid_spec=pltpu.PrefetchScalarGridSpec(
            num_scalar_prefetch=1, grid=(n, r // tr),
            in_specs=[pl.BlockSpec((tr, D), lambda j, i, idx: (idx[j] * (r // tr) + i, 0)), blk],
            out_specs=blk),
        out_shape=jax.ShapeDtypeStruct(got.shape, got.dtype),
        compiler_params=_params(("arbitrary", "arbitrary")),
    )(blocks, g, got)


def _chip_sum(p, r, name, layer, into=None):
    _, R, D = p.shape
    tr = _pick(R, (800, 400, 256, 200, 128, 64, 16))

    def body(p_ref, r_ref, *rest):
        acc = p_ref[...].astype(F32)
        for t in range(3):
            acc = acc + r_ref[t].astype(F32)
        rest[-1][...] = acc

    args = [p, r] + ([] if into is None else [into])
    return pl.pallas_call(
        body, name=name, grid=(R // tr,),
        in_specs=[pl.BlockSpec((None, tr, D), lambda i: (0, i, 0)), pl.BlockSpec((3, tr, D), lambda i: (0, i, 0))]
        + ([] if into is None else [_ANY]),
        out_specs=pl.BlockSpec((None, tr, D), lambda i: (layer, i, 0)),
        out_shape=jax.ShapeDtypeStruct((DEPTH, R, D), F32),
        input_output_aliases={} if into is None else {2: 0},
        compiler_params=_params(("parallel",)))(*args)


def _allreduce_small(vec):
    R, C = vec.shape

    def body(v_ref, o_ref, buf, send_sems, recv_sems):
        me, peers = _me_and_peers()
        buf[me] = v_ref[...]
        sends = []
        for k, (pid, _) in enumerate(peers):
            cp = pltpu.make_async_remote_copy(src_ref=v_ref, dst_ref=buf.at[me], send_sem=send_sems.at[k],
                                              recv_sem=recv_sems.at[k], device_id=pid, device_id_type=MESH)
            cp.start()
            sends.append(cp)
        for k, (pid, pnum) in enumerate(peers):
            pltpu.make_async_remote_copy(src_ref=v_ref, dst_ref=buf.at[pnum], send_sem=send_sems.at[k],
                                         recv_sem=recv_sems.at[k], device_id=pid, device_id_type=MESH).wait_recv()
        for cp in sends:
            cp.wait_send()
        acc = buf[0]
        for d in range(1, N_DEV):
            acc = acc + buf[d]
        o_ref[...] = acc

    vm = pl.BlockSpec(memory_space=pltpu.VMEM)
    return pl.pallas_call(
        body, name="allreduce_small",
        in_specs=[vm], out_specs=vm,
        out_shape=jax.ShapeDtypeStruct((R, C), F32),
        scratch_shapes=[pltpu.VMEM((N_DEV, R, C), F32), pltpu.SemaphoreType.DMA((N_DEV - 1,)),
                        pltpu.SemaphoreType.DMA((N_DEV - 1,))],
        compiler_params=pltpu.CompilerParams(has_side_effects=True),
    )(vec)


def _adamw(w, g, m, v, name):
    R, C = w.shape
    tr = _pick(R, (512, 400, 256, 128, 64, 32, 16, 8)) if R >= 8 else R
    c1 = 1.0 - ADAM_B1 ** ADAM_STEP
    c2 = 1.0 - ADAM_B2 ** ADAM_STEP

    def body(w_ref, g_ref, m_ref, v_ref, d_ref, mo_ref, vo_ref):
        gg = g_ref[...]
        mn = ADAM_B1 * m_ref[...] + (1.0 - ADAM_B1) * gg
        vn = ADAM_B2 * v_ref[...] + (1.0 - ADAM_B2) * (gg * gg)
        d_ref[...] = -ADAM_LR * ((mn / c1) / (jnp.sqrt(vn / c2) + ADAM_EPS) + ADAM_WD * w_ref[...])
        mo_ref[...] = mn
        vo_ref[...] = vn

    blk = pl.BlockSpec((tr, C), lambda i: (i, 0))
    sh = jax.ShapeDtypeStruct((R, C), F32)
    return pl.pallas_call(
        body, name=name, grid=(R // tr,), in_specs=[blk] * 4, out_specs=[blk] * 3, out_shape=[sh] * 3,
        compiler_params=_params(("parallel",)),
    )(w, g, m, v)


def _lb_param_grad(lb_param, dlb):
    L, C = lb_param.shape

    def body(p_ref, d_ref, o_ref):
        lbp = p_ref[...]
        d = d_ref[...]
        mx = jnp.max(lbp, axis=0, keepdims=True)
        e = jnp.exp(lbp - mx)
        p = e / jnp.sum(e, axis=0, keepdims=True)
        tot = jnp.sum(d, axis=0, keepdims=True)
        dps = []
        rest = tot
        for j in range(L):
            dps.append(rest - tot if j == 0 else rest)
            rest = rest - d[j:j + 1]
        dp = jnp.concatenate(dps, axis=0)
        o_ref[...] = p * (dp - jnp.sum(p * dp, axis=0, keepdims=True))

    vm = pl.BlockSpec(memory_space=pltpu.VMEM)
    return pl.pallas_call(body, name="lb_param_grad", in_specs=[vm, vm], out_specs=vm,
                          out_shape=jax.ShapeDtypeStruct((L, C), F32))(lb_param, dlb)


def _pack_small(loss_part, dg_pre, dg_post, dlb, dg_head, dsinks):
    pad8 = lambda a: jnp.pad(a.reshape(-1, 128), ((0, 8 - DEPTH), (0, 0)))
    rows = [dg_pre.reshape(-1, 128), dg_post.reshape(-1, 128), dlb.reshape(-1, 128), pad8(dg_head), pad8(dsinks),
            loss_part]
    return jnp.concatenate(rows, axis=0)


def _unpack_small(vec):
    n = DEPTH * D_MODEL // 128
    o = 0
    dg_pre = vec[o:o + n].reshape(DEPTH, D_MODEL); o += n
    dg_post = vec[o:o + n].reshape(DEPTH, D_MODEL); o += n
    dlb = vec[o:o + n].reshape(DEPTH, HG_WIDTH); o += n
    dg_head = vec[o:o + DEPTH]; o += 8
    dsinks = vec[o:o + DEPTH, :ATT_HEADS]; o += 8
    loss = jnp.sum(vec[o:o + 8])
    return loss, dg_pre, dg_post, dlb, dg_head, dsinks


def kernel(x, w_in, w_out, g_pre, g_post, lb_param, g_head, sinks, loss_target, m_w_in, m_w_out, m_g_pre, m_g_post, m_lb_param, m_g_head, m_sinks, v_w_in, v_w_out, v_g_pre, v_g_post, v_lb_param, v_g_head, v_sinks):
    tr = lambda a: jnp.swapaxes(a, 1, 2)
    w_in_t = tr(w_in)
    (loss_part, dx, gw_in_t, gw_out, dg_pre, dg_post, dlb, dg_head, dsinks) = _step(
        x, loss_target, g_pre, g_post, lb_param, g_head, sinks, shards=(w_in_t, w_out))

    small = _allreduce_small(_pack_small(loss_part, dg_pre, dg_post, dlb, dg_head, dsinks))
    loss, gg_pre, gg_post, gdlb, gg_head, gsinks = _unpack_small(small)
    glb = _lb_param_grad(lb_param, gdlb)

    grads = [gw_in_t, gw_out, gg_pre, gg_post, glb, gg_head, gsinks]
    ws = [w_in_t, w_out, g_pre, g_post, lb_param, g_head, sinks]
    ms = [tr(m_w_in), m_w_out, m_g_pre, m_g_post, m_lb_param, m_g_head, m_sinks]
    vs = [tr(v_w_in), v_w_out, v_g_pre, v_g_post, v_lb_param, v_g_head, v_sinks]
    names = ["w_in", "w_out", "g_pre", "g_post", "lb_param", "g_head", "sinks"]
    deltas, new_m, new_v = [], [], []
    for w, g, m, v, nm in zip(ws, grads, ms, vs, names):
        sh = w.shape
        two = lambda a: a.reshape(-1, sh[-1])
        d, mn, vn = _adamw(two(w), two(g), two(m), two(v), "adamw_" + nm)
        deltas.append(d.reshape(sh))
        new_m.append(mn.reshape(sh))
        new_v.append(vn.reshape(sh))
    grads[0], deltas[0], new_m[0], new_v[0] = tr(grads[0]), tr(deltas[0]), tr(new_m[0]), tr(new_v[0])
    return (loss, dx, *grads, *deltas, *new_m, *new_v)
```

```python
import math

import numpy as np
import jax
import jax.numpy as jnp
from jax import lax
from jax.experimental import pallas as pl
from jax.experimental.pallas import tpu as pltpu

F32 = jnp.float32
BF16 = jnp.bfloat16

D_MODEL = 1024
DEPTH = 2
HG_HEADS = 8
HG_DIM = 128
HG_WIDTH = HG_HEADS * HG_DIM
CHUNK = 64
ATT_HEADS = 16
ATT_DIM = 64
ATT_WIDTH = ATT_HEADS * ATT_DIM
KV_WIDTH = 128
ATT_BLOCK = 128
ATT_SCALE = 1.0 / math.sqrt(ATT_DIM)
ROPE_THETA = 10000.0
NORM_EPS = 1e-6
NEG_INF = -1e30
LB_FLOOR = 1e-20
N_H = 4 * HG_WIDTH
N_A = 2 * ATT_WIDTH + 2 * KV_WIDTH
IN_WIDTH = N_H + N_A
MIX_WIDTH = HG_WIDTH + ATT_WIDTH

ADAM_LR = 0.001
ADAM_B1 = 0.9
ADAM_B2 = 0.999
ADAM_EPS = 1e-08
ADAM_WD = 0.01
ADAM_STEP = 10

N_DEV = 8
MESH = pl.DeviceIdType.MESH
VMEM_LIMIT = 56 * 1024 * 1024

NN = ((1,), (0,))
NT = ((1,), (1,))
TN = ((0,), (0,))


def _dot(a, b, dims):
    return lax.dot_general(a.astype(BF16), b.astype(BF16), (dims, ((), ())), preferred_element_type=F32)


def _params(sem=None, **kw):
    return pltpu.CompilerParams(dimension_semantics=sem, vmem_limit_bytes=VMEM_LIMIT, **kw)


def _sigmoids(x):
    e = jnp.exp(-jnp.abs(x))
    r = 1.0 / (1.0 + e)
    er = e * r
    pos = x >= 0.0
    return jnp.where(pos, r, er), jnp.where(pos, er, r)


def _silu(x):
    return x * _sigmoids(x)[0]


def _silu_and_grad(x):
    s, ns = _sigmoids(x)
    return x * s, s * (1.0 + x * ns)


def _pick(n, prefs):
    for p in prefs:
        if n % p == 0:
            return p
    return n


def _inproj(x2, g, w, name):
    T, D = x2.shape
    tm = _pick(T, (512, 256, 128))
    nchunk = 1024

    def body(x_ref, g_ref, w_ref, oh_ref, oa_ref, h_ref):
        x = x_ref[...]
        r = lax.rsqrt(jnp.mean(x * x, axis=-1, keepdims=True) + NORM_EPS)
        h = ((x * r) * g_ref[...]).astype(BF16)
        h_ref[...] = h
        for j in range(0, N_H, nchunk):
            oh_ref[:, j:j + nchunk] = lax.dot_general(h, w_ref[j:j + nchunk, :], (NT, ((), ())),
                                                      preferred_element_type=F32)
        for j in range(0, N_A, N_A // 2):
            oa_ref[:, j:j + N_A // 2] = lax.dot_general(h, w_ref[N_H + j:N_H + j + N_A // 2, :], (NT, ((), ())),
                                                        preferred_element_type=F32)

    row = lambda w_: pl.BlockSpec((tm, w_), lambda i: (i, 0))
    return pl.pallas_call(
        body, name=name,
        grid=(T // tm,),
        in_specs=[row(D), pl.BlockSpec((1, D), lambda i: (0, 0)),
                  pl.BlockSpec((IN_WIDTH, D), lambda i: (0, 0), pipeline_mode=pl.Buffered(1))],
        out_specs=[row(N_H), row(N_A), row(D)],
        out_shape=[jax.ShapeDtypeStruct((T, N_H), F32), jax.ShapeDtypeStruct((T, N_A), F32),
                   jax.ShapeDtypeStruct((T, D), BF16)],
        compiler_params=_params(("parallel",)),
    )(x2, g, w)


def _mm_tn(pieces, b, name, out_dtype=BF16):
    T, m = b.shape
    tn = 256
    counts = [p.shape[1] // tn for p in pieces]
    starts = [sum(counts[:i]) for i in range(len(pieces))]
    n_p = len(pieces)

    def body(*refs):
        b_ref, o_ref = refs[n_p], refs[n_p + 1]
        i = pl.program_id(0)
        for p in range(n_p):
            @pl.when((i >= starts[p]) & (i < starts[p] + counts[p]))
            def _(p=p):
                o_ref[...] = lax.dot_general(refs[p][...], b_ref[...], (TN, ((), ())),
                                             preferred_element_type=F32).astype(out_dtype)

    piece_spec = lambda s, c: pl.BlockSpec((T, tn), lambda i: (0, jnp.clip(i - s, 0, c - 1)))
    return pl.pallas_call(
        body, name=name,
        grid=(sum(counts),),
        in_specs=[piece_spec(s, c) for s, c in zip(starts, counts)]
        + [pl.BlockSpec((T, m), lambda i: (0, 0), pipeline_mode=pl.Buffered(1))],
        out_specs=pl.BlockSpec((tn, m), lambda i: (i, 0)),
        out_shape=jax.ShapeDtypeStruct((sum(counts) * tn, m), out_dtype),
        compiler_params=_params(("arbitrary",)),
    )(*pieces, b)


_LEVELS = (0, 1, 2, 4, 8, 16, 32)
_CUM_L = (2, 4, 8, 16, 32, 64)
_ALL_KINDS = tuple(("c", L) for L in _CUM_L) + tuple(("r", L) for L in _CUM_L)
_MXU_KINDS = (("c", 2), ("c", 4), ("c", CHUNK), ("r", 2), ("r", 4))
N_CUM = len(_ALL_KINDS) * CHUNK
N_CUM_F = len(_MXU_KINDS) * CHUNK


def _cum_matrices():
    t = np.arange(CHUNK)[:, None]
    r = np.arange(CHUNK)[None, :]

    def mat(kind):
        c, L = kind
        return ((r // L == t // L) & ((r <= t) if c == "c" else (r > t))).astype(np.float32)

    fwd = np.concatenate([mat(kd) for kd in _MXU_KINDS], axis=0)
    full = np.concatenate([mat(kd) for kd in _ALL_KINDS], axis=0)
    return jnp.asarray(fwd, BF16), jnp.asarray(full.T.copy(), BF16)


def _level_masks():
    t = np.arange(CHUNK)[:, None]
    s = np.arange(CHUNK)[None, :]
    ms = []
    for L in _LEVELS:
        if L == 0:
            ms.append(t == s)
        else:
            ms.append((t // (2 * L) == s // (2 * L)) & ((t // L) % 2 == 1) & ((s // L) % 2 == 0))
    return jnp.asarray(np.stack(ms).astype(np.float32))


def _split3(x):
    hi = x.astype(BF16)
    r1 = x - hi.astype(F32)
    mid = r1.astype(BF16)
    lo = (r1 - mid.astype(F32)).astype(BF16)
    return hi, mid, lo


def _cum3(ts, x, terms=3):
    d = lambda p: lax.dot_general(ts, p, (NN, ((), ())), preferred_element_type=F32)
    return sum(d(p) for p in _split3(x)[:terms])


def _lb_terms(lbp, layer):
    mx = jnp.max(lbp, axis=0, keepdims=True)
    e = jnp.exp(lbp - mx)
    p = e / jnp.sum(e, axis=0, keepdims=True)
    cum = p[0:1]
    for j in range(1, layer + 1):
        cum = cum + p[j:j + 1]
    lb = cum - p[0:1]
    lbf = jnp.maximum(lb, LB_FLOOR)
    return dict(lbf=lbf, one_m=1.0 - lb, kcorr=lb - lbf, ind=jnp.where(lb > LB_FLOOR, 1.0, 0.0))


def _gate(x, lt):
    sig, nsig = _sigmoids(x)
    f = lt["lbf"] + lt["one_m"] * sig
    return jnp.log(f), lt["one_m"] * nsig + lt["kcorr"], f, sig, nsig


def _ck(x, ci):
    return x[ci * CHUNK:(ci + 1) * CHUNK]


def _block_cums(ts, g, nc):
    cs = [_cum3(ts, _ck(g, ci), terms=2) for ci in range(nc)]
    out = {kind: jnp.concatenate([c[CHUNK * i:CHUNK * (i + 1)] for c in cs], axis=0)
           for i, kind in enumerate(_MXU_KINDS)}
    b = out[("c", CHUNK)]
    ng = CHUNK // 8
    last = b.reshape(nc, ng, 8, HG_DIM)[:, :, 7:8, :]
    zero = jnp.zeros((nc, 1, 1, HG_DIM), F32)

    def spread(groups):
        return jnp.broadcast_to(jnp.concatenate(groups, axis=1), (nc, ng, 8, HG_DIM)).reshape(nc * CHUNK, HG_DIM)

    def get(kind):
        if kind in out:
            return out[kind]
        c, L = kind
        nb = L // 8
        first = lambda r: (r // nb) * nb
        if c == "c":
            return b - spread([last[:, first(r) - 1:first(r)] if r >= nb else zero for r in range(ng)])
        return spread([last[:, first(r) + nb - 1:first(r) + nb] for r in range(ng)]) - b

    return get


def _level_factors(cums, g, L):
    if L == 0:
        return None, None
    if L == 1:
        return jnp.exp(g[...]), None
    return jnp.exp(cums(("c", L))), jnp.exp(cums(("r", L)))


def _mul(a, e):
    return a if e is None else a * e


def _hg_block_fwd(qf, k, v, g, ts, m_ref, nc):
    cums = _block_cums(ts, g, nc)
    amat = [jnp.zeros((CHUNK, CHUNK), F32)] * nc
    for li, L in enumerate(_LEVELS):
        eq, ek = _level_factors(cums, g, L)
        ql, kl, m = _mul(qf, eq), _mul(k, ek), m_ref[li]
        amat = [amat[ci] + _dot(_ck(ql, ci), _ck(kl, ci), NT) * m for ci in range(nc)]
    b = cums(("c", CHUNK))
    kst = k * jnp.exp(cums(("r", CHUNK)))
    o = [_dot(amat[ci], _ck(v, ci), NN) for ci in range(nc)]
    kv = [_dot(_ck(v, ci), _ck(kst, ci), TN) for ci in range(nc)]
    dec = [jnp.exp(b[(ci + 1) * CHUNK - 1:(ci + 1) * CHUNK, :]) for ci in range(nc)]
    return o, dec, kv, qf * jnp.exp(b), amat


def _hg_block_bwd(qf, k, v, g, do, amat, ts, m_ref, nc):
    cums = _block_cums(ts, g, nc)
    dcs = {}
    da = [_dot(_ck(do, ci), _ck(v, ci), NT) for ci in range(nc)]
    dq = jnp.zeros(qf.shape, F32)
    dk = jnp.zeros(qf.shape, F32)
    dg = jnp.zeros(qf.shape, F32)
    for li, L in enumerate(_LEVELS):
        eq, ek = _level_factors(cums, g, L)
        qlb, klb, m = _mul(qf[...], eq).astype(BF16), _mul(k[...], ek).astype(BF16), m_ref[li]
        dal = [(da[ci] * m).astype(BF16) for ci in range(nc)]
        both = [(_dot(dal[ci], _ck(klb, ci), NN), _dot(dal[ci], _ck(qlb, ci), TN)) for ci in range(nc)]
        dql = _mul(jnp.concatenate([p[0] for p in both], axis=0), eq)
        dkl = _mul(jnp.concatenate([p[1] for p in both], axis=0), ek)
        dq = dq + dql
        dk = dk + dkl
        if L == 1:
            dg = dg + dql * qf[...]
        elif L > 1:
            dcs[("c", L)] = (dql * qf[...]).astype(BF16)
            dcs[("r", L)] = (dkl * k[...]).astype(BF16)
    b = cums(("c", CHUNK))
    e64 = jnp.exp(b)
    er64 = jnp.exp(cums(("r", CHUNK)))
    qb = (qf[...] * e64).astype(BF16)
    return dict(dq=dq, dk=dk, dg=dg, dcs=dcs, e64=e64, er64=er64, qf=qf, k=k, kst=(k[...] * er64).astype(BF16),
                dv=[_dot(amat[ci], _ck(do, ci), TN) for ci in range(nc)],
                dec=[jnp.exp(b[(ci + 1) * CHUNK - 1:(ci + 1) * CHUNK, :]) for ci in range(nc)],
                qd=[_dot(_ck(do, ci), _ck(qb, ci), TN) for ci in range(nc)])


def _hg_state_bwd(w, v, do, starts, ends, tst, nc):
    dqb = jnp.concatenate([_dot(_ck(do, ci), starts[ci], NN) for ci in range(nc)], axis=0)
    dkst = jnp.concatenate([_dot(_ck(v, ci), ends[ci], NN) for ci in range(nc)], axis=0)
    dqb, dkst = dqb * w["e64"], dkst * w["er64"]
    dq = w["dq"] + dqb
    dk = w["dk"] + dkst
    dv = jnp.concatenate([w["dv"][ci] + _dot(_ck(w["kst"], ci), ends[ci], NT) for ci in range(nc)], axis=0)
    trow = lax.broadcasted_iota(jnp.int32, (CHUNK, 1), 0)
    dtot = jnp.concatenate(
        [jnp.where(trow == CHUNK - 1, jnp.sum(ends[ci] * starts[ci], axis=0, keepdims=True) * w["dec"][ci], 0.0)
         for ci in range(nc)], axis=0)
    dcs = dict(w["dcs"])
    dcs[("c", CHUNK)] = (dqb * w["qf"][...] + dtot).astype(BF16)
    dcs[("r", CHUNK)] = (dkst * w["k"][...]).astype(BF16)
    dgs = [_dot(tst, jnp.concatenate([_ck(dcs[kind], ci) for kind in _ALL_KINDS], axis=0), NN) for ci in range(nc)]
    return dq, dk, dv, w["dg"] + jnp.concatenate(dgs, axis=0)


def _hgrn_fwd(proj_h, u_rows, lb_param, g_head, layer, name, phase=None):
    B, S, _ = proj_h.shape
    sb = _pick(S, (2048, 1024, 512, 256, 128, 64))
    nc = sb // CHUNK
    ts, _ = _cum_matrices()

    def body(*refs):
        ins, outs, (st,), p_in, p_out, p_sems = _split_refs(refs, 8, 12, 1, phase)
        q_ref, f_ref, i_ref, z_ref, lbp_ref, gh_ref, ts_ref, m_ref = ins
        o_ref, u_ref, sts_ref, am_ref = outs[:4]
        logf_ref, k_ref, qf_ref, sg_ref, qg_ref, zg_ref, fg_ref, sig_ref = outs[4:]
        h_id, b_id, s_id = pl.program_id(0), pl.program_id(1), pl.program_id(2)
        _hosted_start(phase, p_in, p_out, p_sems, (h_id == 0) & (b_id == 0) & (s_id == 0))

        @pl.when(s_id == 0)
        def _():
            st[...] = jnp.zeros_like(st)

        lt = _lb_terms(lbp_ref[...], layer)
        tsv = ts_ref[...]
        gh = gh_ref[...]
        logf, k, _, sig, nsig = _gate(f_ref[...], lt)
        qf, qf_grad = _silu_and_grad(q_ref[...])
        sg, sg_grad = _silu_and_grad(z_ref[...])
        logf_ref[...], k_ref[...], qf_ref[...], sg_ref[...] = logf, k, qf, sg
        qg_ref[...] = qf_grad.astype(BF16)
        zg_ref[...] = sg_grad.astype(BF16)
        fg_ref[...] = (lt["one_m"] * sig * nsig).astype(BF16)
        sig_ref[...] = sig.astype(BF16)
        o_part, dec, kv, qb, amat = _hg_block_fwd(qf, k, i_ref[...], logf, tsv, m_ref, nc)
        for ci in range(nc):
            am_ref[ci] = amat[ci].astype(BF16)
        cur = st[...]
        starts = []
        for ci in range(nc):
            sts_ref[ci] = cur
            starts.append(cur)
            cur = cur * dec[ci] + kv[ci]
        st[...] = cur
        o = jnp.concatenate([o_part[ci] + _dot(_ck(qb, ci), starts[ci], NT) for ci in range(nc)], axis=0)
        o_ref[...] = o
        r = lax.rsqrt(jnp.mean(o * o, axis=-1, keepdims=True) + NORM_EPS)
        u_ref[...] = (((o * r) * gh) * sg).astype(BF16)
        _hosted_finish(phase, p_in, p_out, p_sems, (h_id == HG_HEADS - 1) & (b_id == B - 1) & (s_id == S // sb - 1))

    col = lambda base: pl.BlockSpec((None, sb, HG_DIM), lambda h, b, s: (b, s, base + h))
    p_ispecs, p_ospecs, p_oshapes, p_alias, p_scratch, p_args = _host_phase(phase, 8, 12)
    wide = lambda dt: jax.ShapeDtypeStruct((B, S, HG_WIDTH), dt)
    res = pl.pallas_call(
        body, name=name,
        grid=(HG_HEADS, B, S // sb),
        in_specs=[col(0), col(HG_HEADS), col(2 * HG_HEADS), col(3 * HG_HEADS),
                  pl.BlockSpec((DEPTH, HG_DIM), lambda h, b, s: (0, h)),
                  pl.BlockSpec((1, HG_DIM), lambda h, b, s: (0, 0)),
                  pl.BlockSpec((N_CUM_F, CHUNK), lambda h, b, s: (0, 0)),
                  pl.BlockSpec((len(_LEVELS), CHUNK, CHUNK), lambda h, b, s: (0, 0, 0))] + p_ispecs,
        out_specs=[col(0), col(0),
                   pl.BlockSpec((None, None, nc, HG_DIM, HG_DIM), lambda h, b, s: (b, h, s, 0, 0)),
                   pl.BlockSpec((None, None, nc, CHUNK, CHUNK), lambda h, b, s: (b, h, s, 0, 0))]
        + [col(0)] * 8 + p_ospecs,
        out_shape=[wide(F32),
                   jax.ShapeDtypeStruct((B, S, u_rows), BF16),
                   jax.ShapeDtypeStruct((B, HG_HEADS, S // CHUNK, HG_DIM, HG_DIM), F32),
                   jax.ShapeDtypeStruct((B, HG_HEADS, S // CHUNK, CHUNK, CHUNK), BF16)]
        + [wide(F32)] * 4 + [wide(BF16)] * 4 + p_oshapes,
        input_output_aliases=p_alias,
        scratch_shapes=[pltpu.VMEM((HG_DIM, HG_DIM), F32)] + p_scratch,
        compiler_params=_params(("arbitrary", "arbitrary", "arbitrary")),
    )(proj_h, proj_h, proj_h, proj_h, lb_param, g_head, ts, _level_masks(), *p_args)
    return res[0], res[1], tuple(res[2:12]), list(res[12:])


def _hgrn_bwd(proj_h, o_h, du, kept, lb_param, g_head, layer, name, phase=None):
    B, S, _ = proj_h.shape
    sb = _pick(S, (512, 256, 128, 64))
    nc = sb // CHUNK
    ns = S // sb
    ts, tst = _cum_matrices()

    def body(*refs):
        ins, outs, (dst,), p_in, p_out, p_sems = _split_refs(refs, 18, 6, 1, phase)
        (i_ref, o_ref, du_ref, sts_ref, am_ref, logf_ref, k_ref, qf_ref, sg_ref, qg_ref, zg_ref, fg_ref, sig_ref,
         lbp_ref, gh_ref, ts_ref, tst_ref, m_ref) = ins
        dq_ref, df_ref, di_ref, dz_ref, dlb_ref, dgh_ref = outs
        h_id, b_id, s_id = pl.program_id(0), pl.program_id(1), pl.program_id(2)
        _hosted_start(phase, p_in, p_out, p_sems, (h_id == 0) & (b_id == 0) & (s_id == 0))

        @pl.when(s_id == 0)
        def _():
            dst[...] = jnp.zeros_like(dst)

        @pl.when((b_id == 0) & (s_id == 0))
        def _():
            dlb_ref[...] = jnp.zeros_like(dlb_ref)

        @pl.when((h_id == 0) & (b_id == 0) & (s_id == 0))
        def _():
            dgh_ref[...] = jnp.zeros_like(dgh_ref)

        lt = _lb_terms(lbp_ref[...], layer)
        gh = gh_ref[...]
        tsv = ts_ref[...]
        tstv = tst_ref[...]
        sg = sg_ref[...]
        o = o_ref[...]
        dub = du_ref[...]
        r = lax.rsqrt(jnp.mean(o * o, axis=-1, keepdims=True) + NORM_EPS)
        n = o * r
        dz_ref[...] = (dub * (n * gh) * zg_ref[...].astype(F32)).astype(BF16)
        dgh_ref[...] += jnp.sum(dub * sg * n, axis=0, keepdims=True)
        dn = dub * sg * gh
        do = (r * (dn - n * jnp.mean(dn * n, axis=-1, keepdims=True))).astype(BF16)
        v = i_ref[...].astype(BF16)
        w = _hg_block_bwd(qf_ref, k_ref, v, logf_ref, do, [am_ref[ci] for ci in range(nc)], tsv, m_ref, nc)
        cur = dst[...]
        ends = [None] * nc
        for ci in reversed(range(nc)):
            ends[ci] = cur
            cur = cur * w["dec"][ci] + w["qd"][ci]
        dst[...] = cur
        dq, dk, dv, dg = _hg_state_bwd(w, v, do, [sts_ref[ci] for ci in range(nc)], ends, tstv, nc)
        di_ref[...] = dv.astype(BF16)
        dq_ref[...] = (dq * qg_ref[...].astype(F32)).astype(BF16)
        f = jnp.exp(logf_ref[...])
        scaled = (dg - f * dk) / f
        df_ref[...] = (scaled * fg_ref[...].astype(F32)).astype(BF16)
        dlb_ref[...] += jnp.sum(scaled * (lt["ind"] - sig_ref[...].astype(F32)), axis=0, keepdims=True)
        _hosted_finish(phase, p_in, p_out, p_sems, (h_id == HG_HEADS - 1) & (b_id == B - 1) & (s_id == ns - 1))

    col = lambda base: pl.BlockSpec((None, sb, HG_DIM), lambda h, b, s: (b, ns - 1 - s, base + h))
    out_col = pl.BlockSpec((None, sb, HG_DIM), lambda h, b, s: (b, ns - 1 - s, h))
    dt = jax.ShapeDtypeStruct((B, S, HG_WIDTH), BF16)
    p_ispecs, p_ospecs, p_oshapes, p_alias, p_scratch, p_args = _host_phase(phase, 18, 6)
    res = pl.pallas_call(
        body, name=name,
        grid=(HG_HEADS, B, ns),
        in_specs=[col(2 * HG_HEADS), col(0), col(0),
                  pl.BlockSpec((None, None, nc, HG_DIM, HG_DIM), lambda h, b, s: (b, h, ns - 1 - s, 0, 0)),
                  pl.BlockSpec((None, None, nc, CHUNK, CHUNK), lambda h, b, s: (b, h, ns - 1 - s, 0, 0))]
        + [col(0)] * 8
        + [pl.BlockSpec((DEPTH, HG_DIM), lambda h, b, s: (0, h)),
           pl.BlockSpec((1, HG_DIM), lambda h, b, s: (0, 0)),
           pl.BlockSpec((N_CUM_F, CHUNK), lambda h, b, s: (0, 0)),
           pl.BlockSpec((CHUNK, N_CUM), lambda h, b, s: (0, 0)),
           pl.BlockSpec((len(_LEVELS), CHUNK, CHUNK), lambda h, b, s: (0, 0, 0))] + p_ispecs,
        out_specs=[out_col, out_col, out_col, out_col,
                   pl.BlockSpec((1, HG_DIM), lambda h, b, s: (0, h)),
                   pl.BlockSpec((1, HG_DIM), lambda h, b, s: (0, 0))] + p_ospecs,
        out_shape=[dt, dt, dt, dt, jax.ShapeDtypeStruct((1, HG_WIDTH), F32),
                   jax.ShapeDtypeStruct((1, HG_DIM), F32)] + p_oshapes,
        input_output_aliases=p_alias,
        scratch_shapes=[pltpu.VMEM((HG_DIM, HG_DIM), F32)] + p_scratch,
        compiler_params=_params(("arbitrary", "arbitrary", "arbitrary")),
    )(proj_h, o_h, du, *kept, lb_param, g_head, ts, tst, _level_masks(), *p_args)
    return tuple(res[:6]) + (list(res[6:]),)


def _rope_tables(S):
    half = ATT_DIM // 2
    inv_freq = np.float32(ROPE_THETA) ** (-np.arange(half, dtype=np.float32) / half)
    ang = np.arange(S, dtype=np.float32)[:, None] * inv_freq[None, :]
    cos = np.cos(ang)
    sin = np.sin(ang)
    cos = np.concatenate([cos, cos, cos, cos], axis=1)
    sin = np.concatenate([-sin, sin, -sin, sin], axis=1)
    return jnp.asarray(cos, F32), jnp.asarray(sin, F32)


def _attn_common():
    lane = lax.broadcasted_iota(jnp.int32, (1, 2 * ATT_DIM), 1)
    first_half = (lane % ATT_DIM) < (ATT_DIM // 2)
    left = lane < ATT_DIM

    def swap(x):
        return jnp.where(first_half, pltpu.roll(x, 128 - ATT_DIM // 2, 1), pltpu.roll(x, ATT_DIM // 2, 1))

    def rope(x, cos, sin):
        return x * cos + swap(x) * sin

    def rope_bwd(dy, cos, sin):
        return dy * cos + swap(dy * sin)

    def dup(x):
        xs = pltpu.roll(x, ATT_DIM, 1)
        return [jnp.where(left, x, xs), jnp.where(left, xs, x)]

    return left, rope, rope_bwd, dup


GROUP = ATT_HEADS // 2
GROUP_ROWS = GROUP * ATT_BLOCK


def _attn_bias(i):
    r = lax.broadcasted_iota(jnp.int32, (ATT_BLOCK, 2 * ATT_BLOCK), 0)
    c = lax.broadcasted_iota(jnp.int32, (ATT_BLOCK, 2 * ATT_BLOCK), 1)
    ok = (c > r) & (c <= r + ATT_BLOCK) & ((c >= ATT_BLOCK) | (i > 0))
    return jnp.where(ok, 0.0, NEG_INF)


def _stack_heads(pairs, left):
    rows = []
    for x in pairs:
        rows += [jnp.where(left, x, 0.0), jnp.where(left, 0.0, x)]
    return jnp.concatenate(rows, axis=0)


def _unstack_heads(y, left, pp):
    r0 = 2 * pp * ATT_BLOCK
    return jnp.where(left, y[r0:r0 + ATT_BLOCK], y[r0 + ATT_BLOCK:r0 + 2 * ATT_BLOCK])


def _row_sums(x):
    return _dot(x, jnp.ones((x.shape[1], 128), BF16), NN)


def _attn_probs(qs, kd, vd, sink, bias):
    n = range(len(qs))
    rows = qs[0].shape[0]
    s = [(_dot(qs[j], kd[j], NT).reshape(rows // ATT_BLOCK, ATT_BLOCK, 2 * ATT_BLOCK) * ATT_SCALE + bias[None])
         .reshape(rows, 2 * ATT_BLOCK) for j in n]
    m = [jnp.max(jnp.maximum(jnp.maximum(s[j][:, :128], s[j][:, 128:]), sink[j]), axis=-1, keepdims=True) for j in n]
    pu = [jnp.exp(s[j] - m[j]) for j in n]
    es = [jnp.exp(sink[j] - m[j]) for j in n]
    ones = jnp.ones((2 * ATT_BLOCK, 128), BF16)
    ov = [_dot(pu[j], jnp.concatenate([vd[j].astype(BF16), ones], axis=1), NN) for j in n]
    inv = [1.0 / (ov[j][:, 128:] + es[j]) for j in n]
    return ([pu[j] * jnp.concatenate([inv[j], inv[j]], axis=1) for j in n], [es[j] * inv[j] for j in n],
            [ov[j][:, :128] * inv[j] for j in n])


def _sink_rows(sinks_l):
    return jnp.broadcast_to(jnp.repeat(sinks_l, ATT_BLOCK)[:, None], (ATT_HEADS * ATT_BLOCK, 128))


_Z0 = (2 * ATT_WIDTH + 2 * KV_WIDTH - ATT_WIDTH) // 256


def _attn_fwd(proj_a, u, sinks_l, cos, sin, name, phase=None):
    B, S, _ = proj_a.shape
    nb = S // ATT_BLOCK

    def body(*refs):
        ins, (u_ref, p_ref, o_ref, ps_ref, qs_ref), _, p_in, p_out, p_sems = _split_refs(refs, 13, 5, 0, phase)
        q_ref, kvc_ref, kvp_ref, z0, z1, z2, z3, cos_ref, sin_ref, cosp_ref, sinp_ref, sinks_ref, _ = ins
        i = pl.program_id(1)
        _hosted_start(phase, p_in, p_out, p_sems, (pl.program_id(0) == 0) & (i == 0))
        left, rope, _, dup = _attn_common()
        cos_c, sin_c = cos_ref[...], sin_ref[...]
        kvc = kvc_ref[...]
        kvp = kvp_ref[...]
        kw = jnp.concatenate([rope(kvp[:, :KV_WIDTH], cosp_ref[...], sinp_ref[...]),
                              rope(kvc[:, :KV_WIDTH], cos_c, sin_c)], axis=0)
        vw = jnp.concatenate([kvp[:, KV_WIDTH:], kvc[:, KV_WIDTH:]], axis=0)
        kd, vd = dup(kw), dup(vw)
        bias = _attn_bias(i)
        zs = (z0, z1, z2, z3)
        pairs = [range(4 * kvh, 4 * kvh + 4) for kvh in range(2)]
        qs = [_stack_heads([rope(q_ref[:, 128 * pr:128 * (pr + 1)], cos_c, sin_c) for pr in pairs[kvh]], left)
              for kvh in range(2)]
        sink = [sinks_ref[kvh * GROUP_ROWS:(kvh + 1) * GROUP_ROWS, :] for kvh in range(2)]
        p, ps, o = _attn_probs(qs, kd, vd, sink, bias)
        eye = (lax.broadcasted_iota(jnp.int32, (ATT_BLOCK, 128), 0)
               == lax.broadcasted_iota(jnp.int32, (ATT_BLOCK, 128), 1))
        for kvh in range(2):
            p_ref[kvh] = p[kvh].astype(BF16)
            qs_ref[kvh] = qs[kvh].astype(BF16)
            for g in range(GROUP):
                blk = ps[kvh][g * ATT_BLOCK:(g + 1) * ATT_BLOCK, :]
                ps_ref[kvh * GROUP + g:kvh * GROUP + g + 1, :] = jnp.sum(jnp.where(eye, blk, 0.0), axis=0, keepdims=True)
            for pp, pr in enumerate(pairs[kvh]):
                z = zs[pr // 2][:, 128 * (pr % 2):128 * (pr % 2 + 1)]
                o128 = _unstack_heads(o[kvh], left, pp)
                o_ref[:, 128 * pr:128 * (pr + 1)] = o128.astype(BF16)
                u_ref[:, 128 * pr:128 * (pr + 1)] = (o128 * _silu(z)).astype(BF16)
        _hosted_finish(phase, p_in, p_out, p_sems, (pl.program_id(0) == B - 1) & (i == nb - 1))

    rowblk = lambda w, cb: pl.BlockSpec((None, ATT_BLOCK, w), lambda b, i: (b, i, cb))
    tab = pl.BlockSpec((ATT_BLOCK, 128), lambda b, i: (i, 0))
    tabp = pl.BlockSpec((ATT_BLOCK, 128), lambda b, i: (jnp.maximum(i - 1, 0), 0))
    p_ispecs, p_ospecs, p_oshapes, p_alias, p_scratch, p_args = _host_phase(phase, 13, 5)
    res = pl.pallas_call(
        body, name=name,
        grid=(B, nb),
        in_specs=[rowblk(ATT_WIDTH, 0), rowblk(256, 4),
                  pl.BlockSpec((None, ATT_BLOCK, 256), lambda b, i: (b, jnp.maximum(i - 1, 0), 4)),
                  rowblk(256, _Z0), rowblk(256, _Z0 + 1), rowblk(256, _Z0 + 2), rowblk(256, _Z0 + 3),
                  tab, tab, tabp, tabp,
                  pl.BlockSpec((ATT_HEADS * ATT_BLOCK, 128), lambda b, i: (0, 0)),
                  pl.BlockSpec(memory_space=pl.ANY)] + p_ispecs,
        out_specs=[pl.BlockSpec((None, ATT_BLOCK, ATT_WIDTH), lambda b, i: (b, i, 1)),
                   pl.BlockSpec((None, None, 2, GROUP_ROWS, 2 * ATT_BLOCK), lambda b, i: (b, i, 0, 0, 0)),
                   pl.BlockSpec((None, ATT_BLOCK, ATT_WIDTH), lambda b, i: (b, i, 0)),
                   pl.BlockSpec((None, None, ATT_HEADS, 128), lambda b, i: (b, i, 0, 0)),
                   pl.BlockSpec((None, None, 2, GROUP_ROWS, 128), lambda b, i: (b, i, 0, 0, 0))] + p_ospecs,
        out_shape=[jax.ShapeDtypeStruct(u.shape, BF16),
                   jax.ShapeDtypeStruct((B, nb, 2, GROUP_ROWS, 2 * ATT_BLOCK), BF16),
                   jax.ShapeDtypeStruct((B, S, ATT_WIDTH), BF16),
                   jax.ShapeDtypeStruct((B, nb, ATT_HEADS, 128), F32),
                   jax.ShapeDtypeStruct((B, nb, 2, GROUP_ROWS, 128), BF16)] + p_oshapes,
        input_output_aliases={12: 0, **p_alias},
        scratch_shapes=p_scratch,
        compiler_params=_params(("arbitrary", "arbitrary")),
    )(proj_a, proj_a, proj_a, proj_a, proj_a, proj_a, proj_a, cos, sin, cos, sin, sinks_l, u, *p_args)
    return res[0], tuple(res[1:5]), list(res[5:])


def _attn_bwd(proj_a, du, kept, cos, sin, name, phase=None):
    B, S, _ = proj_a.shape
    nb = S // ATT_BLOCK
    p_kept, o_kept, ps_kept, qs_kept = kept

    def body(*refs):
        ins, outs, (carry, sk_acc), p_in, p_out, p_sems = _split_refs(refs, 15, 4, 2, phase)
        (qs_ref, kvc_ref, kvp_ref, z0, z1, z2, z3, du_ref, cos_ref, sin_ref, cosp_ref, sinp_ref,
         p_ref, o_ref, ps_ref) = ins
        dq_ref, dkv_ref, dz_ref, dsk_ref = outs
        b_id, i = pl.program_id(0), pl.program_id(1)
        _hosted_start(phase, p_in, p_out, p_sems, (b_id == 0) & (i == 0))

        @pl.when((b_id == 0) & (i == 0))
        def _():
            sk_acc[...] = jnp.zeros_like(sk_acc)

        @pl.when(i == 0)
        def _():
            carry[...] = jnp.zeros_like(carry)

        @pl.when(i < nb)
        def _():
            left, rope, rope_bwd, dup = _attn_common()
            cos_c, sin_c = cos_ref[...], sin_ref[...]
            cos_p, sin_p = cosp_ref[...], sinp_ref[...]
            kvc = kvc_ref[...]
            kvp = kvp_ref[...]
            kw = jnp.concatenate([rope(kvp[:, :KV_WIDTH], cos_p, sin_p), rope(kvc[:, :KV_WIDTH], cos_c, sin_c)], axis=0)
            vw = jnp.concatenate([kvp[:, KV_WIDTH:], kvc[:, KV_WIDTH:]], axis=0)
            kd, vd = dup(kw), dup(vw)
            zs = (z0, z1, z2, z3)
            units = [(kvh, hf) for kvh in range(2) for hf in range(2)]
            half = GROUP_ROWS // 2
            pairs = [range(4 * kvh + 2 * hf, 4 * kvh + 2 * hf + 2) for kvh, hf in units]
            ku = [kd[kvh] for kvh, _ in units]
            vu = [vd[kvh] for kvh, _ in units]
            ps_all = ps_ref[...]
            head_row = lax.broadcasted_iota(jnp.int32, (ATT_HEADS, 128), 0)
            eye = (lax.broadcasted_iota(jnp.int32, (ATT_BLOCK, 128), 0)
                   == lax.broadcasted_iota(jnp.int32, (ATT_BLOCK, 128), 1))

            def first(j):
                kvh, hf = units[j]
                p = p_ref[kvh, hf * half:(hf + 1) * half, :]
                parts = []
                for pr in pairs[j]:
                    cols = slice(128 * pr, 128 * (pr + 1))
                    sg, sg_grad = _silu_and_grad(zs[pr // 2][:, 128 * (pr % 2):128 * (pr % 2 + 1)])
                    du128 = du_ref[:, cols]
                    dz_ref[:, cols] = (du128 * o_ref[:, cols].astype(F32) * sg_grad).astype(BF16)
                    parts.append(du128 * sg)
                dos = _stack_heads(parts, left)
                dp = _dot(dos, vu[j], NT)
                delta = _row_sums(p.astype(F32) * dp)
                ds = (p.astype(F32) * (dp - jnp.concatenate([delta, delta], axis=1)) * ATT_SCALE).astype(BF16)
                sk = jnp.zeros((ATT_HEADS, 128), F32)
                for hh in range(4):
                    hd = kvh * GROUP + 4 * hf + hh
                    drow = jnp.sum(jnp.where(eye, delta[hh * ATT_BLOCK:(hh + 1) * ATT_BLOCK, :], 0.0), axis=0,
                                   keepdims=True)
                    sk = sk - jnp.where(head_row == hd, ps_all * drow, 0.0)
                sk_acc[...] += sk
                return ds, p, dos.astype(BF16), qs_ref[kvh, hf * half:(hf + 1) * half, :]

            def second(j, ds, p, dos, qs):
                dqs = _dot(ds, ku[j], NN)
                for pp, pr in enumerate(pairs[j]):
                    dq_ref[:, 128 * pr:128 * (pr + 1)] = rope_bwd(_unstack_heads(dqs, left, pp),
                                                                  cos_c, sin_c).astype(BF16)
                return _dot(ds, qs, TN), _dot(p, dos, TN)

            got, dku, dvu = {}, [None] * len(units), [None] * len(units)
            for j in range(len(units) + 1):
                if j < len(units):
                    got[j] = first(j)
                if j >= 1:
                    dku[j - 1], dvu[j - 1] = second(j - 1, *got.pop(j - 1))
            dkd = [dku[0] + dku[1], dku[2] + dku[3]]
            dvd = [dvu[0] + dvu[1], dvu[2] + dvu[3]]
            fold = lambda pr: jnp.where(left, pr[0] + pltpu.roll(pr[0], ATT_DIM, 1), pr[1] + pltpu.roll(pr[1], ATT_DIM, 1))
            dkw = fold(dkd)
            dvw = fold(dvd)
            prev = jnp.concatenate([rope_bwd(dkw[:ATT_BLOCK], cos_p, sin_p), dvw[:ATT_BLOCK]], axis=1)
            cur = jnp.concatenate([rope_bwd(dkw[ATT_BLOCK:], cos_c, sin_c), dvw[ATT_BLOCK:]], axis=1)
            dkv_ref[...] = (carry[...] + prev).astype(BF16)
            carry[...] = cur

        @pl.when(i == nb)
        def _():
            dkv_ref[...] = carry[...].astype(BF16)

        @pl.when((b_id == B - 1) & (i == nb))
        def _():
            diag = (lax.broadcasted_iota(jnp.int32, (ATT_HEADS, 128), 0)
                    == lax.broadcasted_iota(jnp.int32, (ATT_HEADS, 128), 1))
            tot = jnp.sum(sk_acc[...], axis=1, keepdims=True)
            dsk_ref[...] = jnp.sum(jnp.where(diag, tot, 0.0), axis=0, keepdims=True)

        _hosted_finish(phase, p_in, p_out, p_sems, (b_id == B - 1) & (i == nb))

    cl = lambda i: jnp.minimum(i, nb - 1)
    pv = lambda i: jnp.maximum(jnp.minimum(i, nb - 1) - 1, 0)
    rowblk = lambda w, cb: pl.BlockSpec((None, ATT_BLOCK, w), lambda b, i: (b, cl(i), cb))
    tab = pl.BlockSpec((ATT_BLOCK, 128), lambda b, i: (cl(i), 0))
    tabp = pl.BlockSpec((ATT_BLOCK, 128), lambda b, i: (pv(i), 0))
    p_ispecs, p_ospecs, p_oshapes, p_alias, p_scratch, p_args = _host_phase(phase, 15, 4)
    res = pl.pallas_call(
        body, name=name,
        grid=(B, nb + 1),
        in_specs=[pl.BlockSpec((None, None, 2, GROUP_ROWS, 128), lambda b, i: (b, cl(i), 0, 0, 0)), rowblk(256, 4),
                  pl.BlockSpec((None, ATT_BLOCK, 256), lambda b, i: (b, pv(i), 4)),
                  rowblk(256, _Z0), rowblk(256, _Z0 + 1), rowblk(256, _Z0 + 2), rowblk(256, _Z0 + 3),
                  rowblk(ATT_WIDTH, 1),
                  tab, tab, tabp, tabp,
                  pl.BlockSpec((None, None, 2, GROUP_ROWS, 2 * ATT_BLOCK), lambda b, i: (b, cl(i), 0, 0, 0)),
                  rowblk(ATT_WIDTH, 0),
                  pl.BlockSpec((None, None, ATT_HEADS, 128), lambda b, i: (b, cl(i), 0, 0))] + p_ispecs,
        out_specs=[rowblk(ATT_WIDTH, 0),
                   pl.BlockSpec((None, ATT_BLOCK, 256), lambda b, i: (b, jnp.maximum(i - 1, 0), 0)),
                   rowblk(ATT_WIDTH, 0),
                   pl.BlockSpec((1, 128), lambda b, i: (0, 0))] + p_ospecs,
        out_shape=[jax.ShapeDtypeStruct((B, S, ATT_WIDTH), BF16), jax.ShapeDtypeStruct((B, S, 256), BF16),
                   jax.ShapeDtypeStruct((B, S, ATT_WIDTH), BF16), jax.ShapeDtypeStruct((1, 128), F32)] + p_oshapes,
        input_output_aliases=p_alias,
        scratch_shapes=[pltpu.VMEM((ATT_BLOCK, 256), F32), pltpu.VMEM((ATT_HEADS, 128), F32)] + p_scratch,
        compiler_params=_params(("arbitrary", "arbitrary")),
    )(qs_kept, proj_a, proj_a, proj_a, proj_a, proj_a, proj_a, du, cos, sin, cos, sin, p_kept, o_kept, ps_kept, *p_args)
    return tuple(res[:4]) + (list(res[4:]),)


def _outproj_fwd(u2, w_out, x2, g_post, target2, name):
    T, D = x2.shape
    tm = _pick(T, (512, 256, 128))
    last = target2 is not None

    def body(u_ref, w_ref, x_ref, g_ref, *rest):
        y = lax.dot_general(u_ref[...], w_ref[...], (NN, ((), ())), preferred_element_type=F32)
        r = lax.rsqrt(jnp.mean(y * y, axis=-1, keepdims=True) + NORM_EPS)
        xn = x_ref[...] + (y * r) * g_ref[...]
        if last:
            t_ref, y_ref, dx_ref, loss_ref = rest
            err = xn - t_ref[...]
            dx_ref[...] = err * (1.0 / D)
            sq = err * err
            acc = sq[:, 0:128]
            for kk in range(1, D // 128):
                acc = acc + sq[:, 128 * kk:128 * (kk + 1)]
            part = jnp.sum(acc.reshape(tm // 8, 8, 128), axis=0) * (0.5 / D)

            @pl.when(pl.program_id(0) == 0)
            def _():
                loss_ref[...] = jnp.zeros_like(loss_ref)

            loss_ref[...] += part
        else:
            y_ref, xn_ref = rest
            xn_ref[...] = xn
        y_ref[...] = y

    row = pl.BlockSpec((tm, D), lambda i: (i, 0))
    in_specs = [pl.BlockSpec((tm, MIX_WIDTH), lambda i: (i, 0)),
                pl.BlockSpec((MIX_WIDTH, D), lambda i: (0, 0)), row,
                pl.BlockSpec((1, D), lambda i: (0, 0))]
    args = [u2, w_out, x2, g_post]
    out_specs = [row, row]
    out_shape = [jax.ShapeDtypeStruct((T, D), F32), jax.ShapeDtypeStruct((T, D), F32)]
    if last:
        in_specs.append(row)
        args.append(target2)
        out_specs.append(pl.BlockSpec((8, 128), lambda i: (0, 0)))
        out_shape.append(jax.ShapeDtypeStruct((8, 128), F32))
    return pl.pallas_call(
        body, name=name, grid=(T // tm,), in_specs=in_specs, out_specs=out_specs, out_shape=out_shape,
        compiler_params=_params(("arbitrary",)),
    )(*args)


def _outproj_bwd(dxn2, y2, g_post, w_out, u2, name):
    T, D = y2.shape
    N = w_out.shape[0]
    tm = _pick(T, (512, 256, 128))
    nt = T // tm

    def body(dx_ref, y_ref, g_ref, w_ref, u_ref, dg_ref, du_ref, dw_ref, acc, wacc):
        i = pl.program_id(0)

        @pl.when(i == 0)
        def _():
            acc[...] = jnp.zeros_like(acc)
            wacc[...] = jnp.zeros_like(wacc)

        y = y_ref[...]
        dxn = dx_ref[...]
        r = lax.rsqrt(jnp.mean(y * y, axis=-1, keepdims=True) + NORM_EPS)
        n = y * r
        dn = dxn * g_ref[...]
        dy = (r * (dn - n * jnp.mean(dn * n, axis=-1, keepdims=True))).astype(BF16)
        du_ref[...] = lax.dot_general(dy, w_ref[...], (NT, ((), ())), preferred_element_type=F32)
        wacc[...] += lax.dot_general(u_ref[...], dy, (TN, ((), ())), preferred_element_type=F32)
        acc[...] += jnp.sum((dxn * n).reshape(tm // 8, 8, D), axis=0)

        @pl.when(i == nt - 1)
        def _():
            dg_ref[...] = jnp.sum(acc[...], axis=0, keepdims=True)
            dw_ref[...] = wacc[...].astype(BF16)

    row = pl.BlockSpec((tm, D), lambda i: (i, 0))
    wide = pl.BlockSpec((tm, N), lambda i: (i, 0))
    vec = pl.BlockSpec((1, D), lambda i: (0, 0))
    whole = pl.BlockSpec((N, D), lambda i: (0, 0))
    return pl.pallas_call(
        body, name=name, grid=(nt,),
        in_specs=[row, row, vec, pl.BlockSpec((N, D), lambda i: (0, 0), pipeline_mode=pl.Buffered(1)), wide],
        out_specs=[vec, wide, whole],
        out_shape=[jax.ShapeDtypeStruct((1, D), F32), jax.ShapeDtypeStruct((T, N), F32),
                   jax.ShapeDtypeStruct((N, D), BF16)],
        scratch_shapes=[pltpu.VMEM((8, D), F32), pltpu.VMEM((N, D), F32)],
        compiler_params=_params(("arbitrary",)),
    )(dxn2, y2, g_post, w_out, u2)


def _inproj_bwd(pieces, w_t, x2, dxn2, g_pre, name, phase=None):
    T, D = x2.shape
    widths = [p.shape[1] for p in pieces]
    offs = [sum(widths[:i]) for i in range(len(pieces))]
    n_p = len(pieces)
    tm = _pick(T, (256, 128))
    nt = T // tm

    def body(*refs):
        ins, (dx_ref, dg_ref), (acc,), p_in, p_out, p_sems = _split_refs(refs, n_p + 4, 2, 1, phase)
        w_ref, x_ref, dxn_ref, g_ref = ins[n_p:]
        i = pl.program_id(0)
        _hosted_start(phase, p_in, p_out, p_sems, i == 0)

        @pl.when(i == 0)
        def _():
            acc[...] = jnp.zeros_like(acc)

        dh = jnp.zeros((tm, D), F32)
        for p in range(n_p):
            dh = dh + lax.dot_general(ins[p][...], w_ref[offs[p]:offs[p] + widths[p], :], (NN, ((), ())),
                                      preferred_element_type=F32)
        x = x_ref[...]
        r = lax.rsqrt(jnp.mean(x * x, axis=-1, keepdims=True) + NORM_EPS)
        n = x * r
        dn = dh * g_ref[...]
        dx_ref[...] = dxn_ref[...] + r * (dn - n * jnp.mean(dn * n, axis=-1, keepdims=True))
        acc[...] += jnp.sum((dh * n).reshape(tm // 8, 8, D), axis=0)

        @pl.when(i == nt - 1)
        def _():
            dg_ref[...] = jnp.sum(acc[...], axis=0, keepdims=True)

        _hosted_finish(phase, p_in, p_out, p_sems, i == nt - 1)

    row = pl.BlockSpec((tm, D), lambda i: (i, 0))
    vec = pl.BlockSpec((1, D), lambda i: (0, 0))
    p_ispecs, p_ospecs, p_oshapes, p_alias, p_scratch, p_args = _host_phase(phase, n_p + 4, 2)
    res = pl.pallas_call(
        body, name=name, grid=(nt,),
        in_specs=[pl.BlockSpec((tm, w), lambda i: (i, 0)) for w in widths]
        + [pl.BlockSpec((sum(widths), D), lambda i: (0, 0), pipeline_mode=pl.Buffered(1)), row, row, vec] + p_ispecs,
        out_specs=[row, vec] + p_ospecs,
        out_shape=[jax.ShapeDtypeStruct((T, D), F32), jax.ShapeDtypeStruct((1, D), F32)] + p_oshapes,
        input_output_aliases=p_alias,
        scratch_shapes=[pltpu.VMEM((8, D), F32)] + p_scratch,
        compiler_params=_params(("arbitrary",)),
    )(*pieces, w_t, x2, dxn2, g_pre, *p_args)
    return res[0], res[1], list(res[2:])


def _step(x, target, g_pre, g_post, lb_param, g_head, sinks, shards=None, full=None):
    B, S, D = x.shape
    T = B * S
    dist = shards is not None
    first, last = 0, DEPTH - 1
    if dist:
        a_loc, b_loc = shards
        ra, rb = a_loc.shape[1], b_loc.shape[1]
        side = _own_side_blocks()
        a_full, b_full = _place_own([a_loc, b_loc], side, "place_own")
        w_in0 = _gather_one_call(a_full[0], "gather_in0")
        w_in, w_out = [w_in0, None], [None, None]
    else:
        w_in, w_out = list(full[0]), list(full[1])
    cos, sin = _rope_tables(S)
    saved = []
    xs = x
    loss_part = None
    dxn = None
    for l in range(DEPTH):
        x2 = xs.reshape(T, D)
        proj_h, proj_a, h = _inproj(x2, g_pre[l:l + 1], w_in[l], f"inproj{l}")
        proj_h = proj_h.reshape(B, S, N_H)
        proj_a = proj_a.reshape(B, S, N_A)
        phase = None
        if dist and l == first:
            phase = _gather_ici_phase([a_full[1], b_full[0]])
        if dist and l == last:
            phase = _gather_d2d_phase([w_out1_part], [rb])
        o_h, u, states, got = _hgrn_fwd(proj_h, MIX_WIDTH, lb_param, g_head[l:l + 1], l, f"hgrn_fwd{l}", phase)
        phase = None
        if dist and l == first:
            phase = _merge_phases(_gather_d2d_phase(got, [ra, rb]),
                                  _gather_ici_phase([b_full[1]]))
        if dist and l == last:
            w_out[1] = got[0]
        u, kept_a, got = _attn_fwd(proj_a, u, _sink_rows(sinks[l]), cos, sin, f"attn_fwd{l}", phase)
        if dist and l == first:
            w_in[1], w_out[0], w_out1_part = got
        u2 = u.reshape(T, MIX_WIDTH)
        if l < last:
            y, xn = _outproj_fwd(u2, w_out[l], x2, g_post[l:l + 1], None, f"outproj{l}")
            xn = xn.reshape(B, S, D)
        else:
            y, dxn, loss_part = _outproj_fwd(u2, w_out[l], x2, g_post[l:l + 1], target.reshape(T, D), f"outproj{l}")
            xn = None
        saved.append((x2, h, proj_h, proj_a, o_h, u2, states, kept_a, y))
        xs = xn

    dw_in, dw_out = [None] * DEPTH, [None] * DEPTH
    dg_pre, dg_post, dlb, dg_head, dsinks = [], [], [], [], []
    for l in reversed(range(DEPTH)):
        x2, h, proj_h, proj_a, o_h, u2, states, kept_a, y = saved[l]
        dgp, du, dw_out[l] = _outproj_bwd(dxn, y, g_post[l:l + 1], w_out[l], u2, f"outproj_bwd{l}")
        du = du.reshape(B, S, MIX_WIDTH)
        phase = None
        if dist:
            phase = _reduce_d2d_phase([dw_out[l]], [rb])
            if l == first:
                phase = _merge_phases(_reduce_ici_phase([part_in1]), phase)
        dqh, dfh, dih, dzh, dlb_l, dgh, got = _hgrn_bwd(
            proj_h, o_h, du, states, lb_param, g_head[l:l + 1], l, f"hgrn_bwd{l}", phase)
        if dist:
            if l == first:
                sum_in = _chip_sum(part_in1, got[0], "chip_sum_in1", 1)
            part_out = _pair_sum(dw_out[l], got[-1], side, f"pair_sum_out{l}")
        dqa, dkv, dza, dsk, got = _attn_bwd(proj_a, du, kept_a, cos, sin, f"attn_bwd{l}",
                                            _reduce_ici_phase([part_out]) if dist else None)
        if dist:
            sum_out = _chip_sum(part_out, got[0], f"chip_sum_out{l}", l, None if l == last else sum_out)
        dproj = [p.reshape(T, p.shape[-1]) for p in (dqh, dfh, dih, dzh, dqa, dkv, dza)]
        dw_in[l] = _mm_tn(dproj, h, f"wgrad_in{l}")
        phase = None
        if dist and l == last:
            phase = _reduce_d2d_phase([dw_in[l]], [ra])
        if dist and l == first:
            got = _run_phase(_reduce_d2d_phase([dw_in[l]], [ra]), "reduce_in0_d2d")
            part_in0 = _pair_sum(dw_in[l], got[0], side, "pair_sum_in0")
            phase = _reduce_ici_phase([part_in0])
        dxn, dgpre, got = _inproj_bwd(dproj, w_in[l], x2, dxn, g_pre[l:l + 1], f"inproj_bwd{l}", phase)
        if dist and l == last:
            part_in1 = _pair_sum(dw_in[l], got[0], side, "pair_sum_in1")
        if dist and l == first:
            sum_in = _chip_sum(part_in0, got[0], "chip_sum_in0", 0, sum_in)
        dg_pre.append(dgpre)
        dg_post.append(dgp)
        dlb.append(dlb_l)
        dg_head.append(dgh)
        dsinks.append(dsk)
    rev = lambda lst: jnp.concatenate(lst[::-1], axis=0)
    if not dist:
        sum_in, sum_out = jnp.stack(dw_in), jnp.stack(dw_out)
    return (loss_part, dxn.reshape(B, S, D), sum_in, sum_out,
            rev(dg_pre), rev(dg_post), rev(dlb), rev(dg_head), rev(dsinks))


def _me_and_peers():
    x, y, c = lax.axis_index("x"), lax.axis_index("y"), lax.axis_index("c")
    me = 4 * x + 2 * y + c
    peers = []
    for k in range(1, N_DEV):
        px = 1 - x if k & 4 else x
        py = 1 - y if k & 2 else y
        pc = 1 - c if k & 1 else c
        peers.append(((px, py, pc), 4 * px + 2 * py + pc))
    return me, peers


class _Phase:
    def __init__(self, arrays, out_shapes, aliases, n_send, build):
        self.arrays, self.out_shapes, self.aliases = list(arrays), list(out_shapes), dict(aliases)
        self.n_send, self.build = n_send, build

    def scratch(self):
        return [pltpu.SemaphoreType.DMA((self.n_send,)), pltpu.SemaphoreType.DMA((self.n_send,))]

    def _copies(self, in_refs, out_refs, sems, arrivals):
        send_sems, recv_sems = sems
        sends, recvs = self.build(in_refs, out_refs)
        assert len(sends) == self.n_send == len(recvs)
        out = [pltpu.make_async_remote_copy(src_ref=s, dst_ref=d, send_sem=send_sems.at[i], recv_sem=recv_sems.at[i],
                                            device_id=dev, device_id_type=MESH) for i, (s, d, dev) in enumerate(sends)]
        inc = [pltpu.make_async_remote_copy(src_ref=s, dst_ref=r, send_sem=send_sems.at[i], recv_sem=recv_sems.at[i],
                                            device_id=dev, device_id_type=MESH)
               for i, ((s, _, dev), r) in enumerate(zip(sends, recvs))] if arrivals else []
        return out, inc

    def start(self, in_refs, out_refs, sems):
        out, _ = self._copies(in_refs, out_refs, sems, False)
        for cp in out:
            cp.start()

    def finish(self, in_refs, out_refs, sems):
        out, inc = self._copies(in_refs, out_refs, sems, True)
        for cp in inc:
            cp.wait_recv()
        for cp in out:
            cp.wait_send()


_ANY = pl.BlockSpec(memory_space=pl.ANY)


def _host_phase(phase, n_in, n_out):
    if phase is None:
        return [], [], [], {}, [], []
    aliases = {n_in + i: n_out + o for i, o in phase.aliases.items()}
    return ([_ANY] * len(phase.arrays), [_ANY] * len(phase.out_shapes), phase.out_shapes, aliases, phase.scratch(),
            phase.arrays)


def _split_refs(refs, n_in, n_out, n_scr, phase):
    pi = len(phase.arrays) if phase else 0
    po = len(phase.out_shapes) if phase else 0
    a = n_in + pi
    b = a + n_out + po
    return (refs[:n_in], refs[a:a + n_out], refs[b:b + n_scr], refs[n_in:a], refs[a + n_out:b], refs[b + n_scr:])


def _hosted_start(phase, p_in, p_out, p_sems, first):
    if phase is not None:
        @pl.when(first)
        def _():
            phase.start(p_in, p_out, p_sems)


def _hosted_finish(phase, p_in, p_out, p_sems, last):
    if phase is not None:
        @pl.when(last)
        def _():
            phase.finish(p_in, p_out, p_sems)


def _run_phase(phase, name):
    n_in, n_out = len(phase.arrays), len(phase.out_shapes)

    def body(*refs):
        phase.start(refs[:n_in], refs[n_in:n_in + n_out], refs[n_in + n_out:])
        phase.finish(refs[:n_in], refs[n_in:n_in + n_out], refs[n_in + n_out:])

    return pl.pallas_call(
        body, name=name, in_specs=[_ANY] * n_in, out_specs=[_ANY] * n_out,
        out_shape=phase.out_shapes, input_output_aliases=phase.aliases, scratch_shapes=phase.scratch(),
        compiler_params=pltpu.CompilerParams(has_side_effects=True),
    )(*phase.arrays)


def _gather_one_call(full, name):
    r = full.shape[0] // N_DEV
    half = r // 2

    def body(full_in, full_ref, send_sems, recv_sems):
        del full_in
        c, (own, xn, yn, dg), num = _mesh_place()
        me, sib = num(own, c), (*own, 1 - c)

        def blk(dev, part=None):
            start, n = (dev * r, r) if part is None else (dev * r + part * half, half)
            return full_ref.at[pl.ds(pl.multiple_of(start, 16), n), :]

        def copy(k, src, dev, to, part=None):
            return pltpu.make_async_remote_copy(src_ref=src, dst_ref=blk(dev, part),
                                                send_sem=send_sems.at[k], recv_sem=recv_sems.at[k],
                                                device_id=to, device_id_type=MESH)

        def landed(k, dev, part=None):
            copy(k, blk(dev, part), dev, sib, part).wait_recv()

        sent = []

        def start(*cps):
            for cp in cps:
                cp.start()
                sent.append(cp)

        xs, ys, ds = num(xn, c), num(yn, c), num(dg, c)
        start(copy(0, blk(me), me, sib), copy(1, blk(me), me, (*xn, c)), copy(2, blk(me), me, (*yn, c)))
        landed(1, xs)
        start(copy(3, blk(xs, 0), xs, (*yn, c), 0), copy(5, blk(xs), xs, sib))
        landed(2, ys)
        start(copy(4, blk(ys, 1), ys, (*xn, c), 1), copy(6, blk(ys), ys, sib))
        landed(3, ds, 0)
        landed(4, ds, 1)
        start(copy(7, blk(ds), ds, sib))
        landed(0, num(own, 1 - c))
        for k, ch in ((5, xn), (6, yn), (7, dg)):
            landed(k, num(ch, 1 - c))
        for cp in sent:
            cp.wait_send()

    assert half % 16 == 0
    return pl.pallas_call(
        body, name=name, in_specs=[_ANY], out_specs=_ANY,
        out_shape=jax.ShapeDtypeStruct(full.shape, full.dtype), input_output_aliases={0: 0},
        scratch_shapes=[pltpu.SemaphoreType.DMA((8,)), pltpu.SemaphoreType.DMA((8,))],
        compiler_params=pltpu.CompilerParams(has_side_effects=True),
    )(full)


def _merge_phases(a, b):
    n_in, n_out = len(a.arrays), len(a.out_shapes)
    aliases = dict(a.aliases)
    aliases.update({n_in + i: n_out + o for i, o in b.aliases.items()})

    def build(ins, outs):
        sa, ra = a.build(ins[:n_in], outs[:n_out])
        sb, rb = b.build(ins[n_in:], outs[n_out:])
        return sa + sb, ra + rb

    return _Phase(a.arrays + b.arrays, a.out_shapes + b.out_shapes, aliases, a.n_send + b.n_send, build)


def _mesh_place():
    x, y, c = lax.axis_index("x"), lax.axis_index("y"), lax.axis_index("c")
    chips = [(x, y), (1 - x, y), (x, 1 - y), (1 - x, 1 - y)]
    num = lambda chip, core: 4 * chip[0] + 2 * chip[1] + core
    return c, chips, num


def _own_side_blocks():
    c, chips, num = _mesh_place()
    return jnp.stack([num(ch, c) for ch in chips]).astype(jnp.int32)


def _rows(ref, r, dev):
    return ref.at[pl.ds(pl.multiple_of(dev * r, 16), r), :]


def _place_own(shards, blocks, name):
    n = len(shards)

    def body(idx_ref, *refs):
        del idx_ref
        outs = iter(refs[n:])
        for s_ref in refs[:n]:
            for l in range(DEPTH):
                next(outs)[...] = s_ref[l].astype(BF16)

    whole = lambda s: pl.BlockSpec(s.shape, lambda i, idx: (0, 0, 0))
    own = lambda s: pl.BlockSpec(s.shape[1:], lambda i, idx: (idx[0], 0))
    res = pl.pallas_call(
        body, name=name,
        grid_spec=pltpu.PrefetchScalarGridSpec(
            num_scalar_prefetch=1, grid=(1,),
            in_specs=[whole(s) for s in shards],
            out_specs=[own(s) for s in shards for _ in range(DEPTH)]),
        out_shape=[jax.ShapeDtypeStruct((N_DEV * s.shape[1], s.shape[2]), BF16) for s in shards for _ in range(DEPTH)],
        compiler_params=_params(("arbitrary",)),
    )(blocks, *shards)
    return [list(res[i * DEPTH:(i + 1) * DEPTH]) for i in range(n)]


def _gather_ici_phase(fulls):
    rs = [a.shape[0] // N_DEV for a in fulls]
    n = len(fulls)

    def build(ins, outs):
        del ins
        c, chips, num = _mesh_place()
        me = num(chips[0], c)
        targets = [((*chips[0], 1 - c), num(chips[0], 1 - c))] + [((*ch, c), num(ch, c)) for ch in chips[1:]]
        sends, recvs = [], []
        for dev, dnum in targets:
            for i, r in enumerate(rs):
                sends.append((_rows(outs[i], r, me), _rows(outs[i], r, me), dev))
                recvs.append(_rows(outs[i], r, dnum))
        return sends, recvs

    shapes = [jax.ShapeDtypeStruct(a.shape, a.dtype) for a in fulls]
    return _Phase(list(fulls), shapes, {i: i for i in range(n)}, 4 * n, build)


def _gather_d2d_phase(fulls, rs):
    def build(ins, outs):
        c, chips, num = _mesh_place()
        sib = (*chips[0], 1 - c)
        sends, recvs = [], []
        for ch in chips[1:]:
            for i, r in enumerate(rs):
                blk = _rows(outs[i], r, num(ch, c))
                sends.append((blk, blk, sib))
                recvs.append(_rows(outs[i], r, num(ch, 1 - c)))
        return sends, recvs

    shapes = [jax.ShapeDtypeStruct(a.shape, a.dtype) for a in fulls]
    return _Phase(fulls, shapes, {i: i for i in range(len(fulls))}, 3 * len(fulls), build)


def _reduce_d2d_phase(grads, rs):
    def build(ins, outs):
        c, chips, num = _mesh_place()
        sib = (*chips[0], 1 - c)
        sends, recvs = [], []
        for j, ch in enumerate(chips):
            for i, r in enumerate(rs):
                sends.append((_rows(ins[i], r, num(ch, 1 - c)), outs[i].at[j], sib))
                recvs.append(outs[i].at[j])
        return sends, recvs

    shapes = [jax.ShapeDtypeStruct((4, r, g.shape[1]), g.dtype) for g, r in zip(grads, rs)]
    return _Phase(grads, shapes, {}, 4 * len(grads), build)


def _reduce_ici_phase(parts):
    def build(ins, outs):
        c, chips, _ = _mesh_place()
        sends, recvs = [], []
        for t in range(1, 4):
            for i in range(len(parts)):
                sends.append((ins[i].at[t], outs[i].at[t - 1], (*chips[t], c)))
                recvs.append(outs[i].at[t - 1])
        return sends, recvs

    shapes = [jax.ShapeDtypeStruct((3,) + p.shape[1:], p.dtype) for p in parts]
    return _Phase(parts, shapes, {}, 3 * len(parts), build)


def _pair_sum(g, got, blocks, name):
    n, r, D = got.shape
    tr = _pick(r, (800, 400, 256, 200, 128, 64, 16))

    def body(idx_ref, g_ref, r_ref, o_ref):
        del idx_ref
        o_ref[...] = (g_ref[...].astype(F32) + r_ref[...].astype(F32)).astype(o_ref.dtype)

    blk = pl.BlockSpec((None, tr, D), lambda j, i, idx: (j, i, 0))
    return pl.pallas_call(
        body, name=name,
        grid_spec=pltpu.PrefetchScalarGridSpec(
            num_scalar_prefetch=1, grid=(n, r // tr),
            in_specs=[pl.BlockSpec((tr, D), lambda j, i, idx: (idx[j] * (r // tr) + i, 0)), blk],
            out_specs=blk),
        out_shape=jax.ShapeDtypeStruct(got.shape, got.dtype),
        compiler_params=_params(("arbitrary", "arbitrary")),
    )(blocks, g, got)


def _chip_sum(p, r, name, layer, into=None):
    _, R, D = p.shape
    tr = _pick(R, (800, 400, 256, 200, 128, 64, 16))

    def body(p_ref, r_ref, *rest):
        acc = p_ref[...].astype(F32)
        for t in range(3):
            acc = acc + r_ref[t].astype(F32)
        rest[-1][...] = acc

    args = [p, r] + ([] if into is None else [into])
    return pl.pallas_call(
        body, name=name, grid=(R // tr,),
        in_specs=[pl.BlockSpec((None, tr, D), lambda i: (0, i, 0)), pl.BlockSpec((3, tr, D), lambda i: (0, i, 0))]
        + ([] if into is None else [_ANY]),
        out_specs=pl.BlockSpec((None, tr, D), lambda i: (layer, i, 0)),
        out_shape=jax.ShapeDtypeStruct((DEPTH, R, D), F32),
        input_output_aliases={} if into is None else {2: 0},
        compiler_params=_params(("parallel",)))(*args)


def _allreduce_small(vec):
    R, C = vec.shape

    def body(v_ref, o_ref, buf, send_sems, recv_sems):
        me, peers = _me_and_peers()
        buf[me] = v_ref[...]
        sends = []
        for k, (pid, _) in enumerate(peers):
            cp = pltpu.make_async_remote_copy(src_ref=v_ref, dst_ref=buf.at[me], send_sem=send_sems.at[k],
                                              recv_sem=recv_sems.at[k], device_id=pid, device_id_type=MESH)
            cp.start()
            sends.append(cp)
        for k, (pid, pnum) in enumerate(peers):
            pltpu.make_async_remote_copy(src_ref=v_ref, dst_ref=buf.at[pnum], send_sem=send_sems.at[k],
                                         recv_sem=recv_sems.at[k], device_id=pid, device_id_type=MESH).wait_recv()
        for cp in sends:
            cp.wait_send()
        acc = buf[0]
        for d in range(1, N_DEV):
            acc = acc + buf[d]
        o_ref[...] = acc

    vm = pl.BlockSpec(memory_space=pltpu.VMEM)
    return pl.pallas_call(
        body, name="allreduce_small",
        in_specs=[vm], out_specs=vm,
        out_shape=jax.ShapeDtypeStruct((R, C), F32),
        scratch_shapes=[pltpu.VMEM((N_DEV, R, C), F32), pltpu.SemaphoreType.DMA((N_DEV - 1,)),
                        pltpu.SemaphoreType.DMA((N_DEV - 1,))],
        compiler_params=pltpu.CompilerParams(has_side_effects=True),
    )(vec)


def _adamw_update(w, g, m, v):
    c1 = 1.0 - ADAM_B1 ** ADAM_STEP
    c2 = 1.0 - ADAM_B2 ** ADAM_STEP
    mn = ADAM_B1 * m + (1.0 - ADAM_B1) * g
    vn = ADAM_B2 * v + (1.0 - ADAM_B2) * (g * g)
    return -ADAM_LR * ((mn / c1) / (jnp.sqrt(vn / c2) + ADAM_EPS) + ADAM_WD * w), mn, vn


def _adamw(w, g, m, v, name):
    R, C = w.shape
    tr = _pick(R, (512, 400, 256, 128, 64, 32, 16, 8))

    def body(w_ref, g_ref, m_ref, v_ref, d_ref, mo_ref, vo_ref):
        d_ref[...], mo_ref[...], vo_ref[...] = _adamw_update(w_ref[...], g_ref[...], m_ref[...], v_ref[...])

    blk = pl.BlockSpec((tr, C), lambda i: (i, 0))
    sh = jax.ShapeDtypeStruct((R, C), F32)
    return pl.pallas_call(
        body, name=name, grid=(R // tr,), in_specs=[blk] * 4, out_specs=[blk] * 3, out_shape=[sh] * 3,
        compiler_params=_params(("parallel",)),
    )(w, g, m, v)


def _adamw_whole(ws, gs, ms, vs, name):
    n = len(ws)

    def body(*refs):
        for j in range(n):
            outs = refs[4 * n + 3 * j:4 * n + 3 * j + 3]
            outs[0][...], outs[1][...], outs[2][...] = _adamw_update(*(r[...] for r in refs[4 * j:4 * j + 4]))

    vm = pl.BlockSpec(memory_space=pltpu.VMEM)
    res = pl.pallas_call(
        body, name=name, in_specs=[vm] * (4 * n), out_specs=[vm] * (3 * n),
        out_shape=[jax.ShapeDtypeStruct(w.shape, F32) for w in ws for _ in range(3)],
    )(*[a for four in zip(ws, gs, ms, vs) for a in four])
    return [tuple(res[3 * j:3 * j + 3]) for j in range(n)]


def _lb_param_grad(lb_param, dlb):
    L, C = lb_param.shape

    def body(p_ref, d_ref, o_ref):
        lbp = p_ref[...]
        d = d_ref[...]
        mx = jnp.max(lbp, axis=0, keepdims=True)
        e = jnp.exp(lbp - mx)
        p = e / jnp.sum(e, axis=0, keepdims=True)
        tot = jnp.sum(d, axis=0, keepdims=True)
        dps = []
        rest = tot
        for j in range(L):
            dps.append(rest - tot if j == 0 else rest)
            rest = rest - d[j:j + 1]
        dp = jnp.concatenate(dps, axis=0)
        o_ref[...] = p * (dp - jnp.sum(p * dp, axis=0, keepdims=True))

    vm = pl.BlockSpec(memory_space=pltpu.VMEM)
    return pl.pallas_call(body, name="lb_param_grad", in_specs=[vm, vm], out_specs=vm,
                          out_shape=jax.ShapeDtypeStruct((L, C), F32))(lb_param, dlb)


def _pack_small(loss_part, dg_pre, dg_post, dlb, dg_head, dsinks):
    pad8 = lambda a: jnp.pad(a.reshape(-1, 128), ((0, 8 - DEPTH), (0, 0)))
    rows = [dg_pre.reshape(-1, 128), dg_post.reshape(-1, 128), dlb.reshape(-1, 128), pad8(dg_head), pad8(dsinks),
            loss_part]
    return jnp.concatenate(rows, axis=0)


def _unpack_small(vec):
    n = DEPTH * D_MODEL // 128
    o = 0
    dg_pre = vec[o:o + n].reshape(DEPTH, D_MODEL); o += n
    dg_post = vec[o:o + n].reshape(DEPTH, D_MODEL); o += n
    dlb = vec[o:o + n].reshape(DEPTH, HG_WIDTH); o += n
    dg_head = vec[o:o + DEPTH]; o += 8
    dsinks = vec[o:o + DEPTH, :ATT_HEADS]; o += 8
    loss = jnp.sum(vec[o:o + 8])
    return loss, dg_pre, dg_post, dlb, dg_head, dsinks


def kernel(x, w_in, w_out, g_pre, g_post, lb_param, g_head, sinks, loss_target, m_w_in, m_w_out, m_g_pre, m_g_post, m_lb_param, m_g_head, m_sinks, v_w_in, v_w_out, v_g_pre, v_g_post, v_lb_param, v_g_head, v_sinks):
    tr = lambda a: jnp.swapaxes(a, 1, 2)
    w_in_t = tr(w_in)
    (loss_part, dx, gw_in_t, gw_out, dg_pre, dg_post, dlb, dg_head, dsinks) = _step(
        x, loss_target, g_pre, g_post, lb_param, g_head, sinks, shards=(w_in_t, w_out))

    small = _allreduce_small(_pack_small(loss_part, dg_pre, dg_post, dlb, dg_head, dsinks))
    loss, gg_pre, gg_post, gdlb, gg_head, gsinks = _unpack_small(small)
    glb = _lb_param_grad(lb_param, gdlb)

    grads = [gw_in_t, gw_out, gg_pre, gg_post, glb, gg_head, gsinks]
    ws = [w_in_t, w_out, g_pre, g_post, lb_param, g_head, sinks]
    ms = [tr(m_w_in), m_w_out, m_g_pre, m_g_post, m_lb_param, m_g_head, m_sinks]
    vs = [tr(v_w_in), v_w_out, v_g_pre, v_g_post, v_lb_param, v_g_head, v_sinks]
    deltas, new_m, new_v = [], [], []
    big = 2
    for w, g, m, v, nm in zip(ws[:big], grads, ms, vs, ("w_in", "w_out")):
        sh = w.shape
        two = lambda a: a.reshape(-1, sh[-1])
        d, mn, vn = _adamw(two(w), two(g), two(m), two(v), "adamw_" + nm)
        deltas.append(d.reshape(sh))
        new_m.append(mn.reshape(sh))
        new_v.append(vn.reshape(sh))
    for d, mn, vn in _adamw_whole(ws[big:], grads[big:], ms[big:], vs[big:], "adamw_vectors"):
        deltas.append(d)
        new_m.append(mn)
        new_v.append(vn)
    grads[0], deltas[0], new_m[0], new_v[0] = tr(grads[0]), tr(deltas[0]), tr(new_m[0]), tr(new_v[0])
    return (loss, dx, *grads, *deltas, *new_m, *new_v)
```

```python
import math

import numpy as np
import jax
import jax.numpy as jnp
from jax import lax
from jax.experimental import pallas as pl
from jax.experimental.pallas import tpu as pltpu

F32 = jnp.float32
BF16 = jnp.bfloat16

D_MODEL = 1024
DEPTH = 2
HG_HEADS = 8
HG_DIM = 128
HG_WIDTH = HG_HEADS * HG_DIM
CHUNK = 64
ATT_HEADS = 16
ATT_DIM = 64
ATT_WIDTH = ATT_HEADS * ATT_DIM
KV_WIDTH = 128
ATT_BLOCK = 128
ATT_SCALE = 1.0 / math.sqrt(ATT_DIM)
ROPE_THETA = 10000.0
NORM_EPS = 1e-6
NEG_INF = -1e30
LB_FLOOR = 1e-20
N_H = 4 * HG_WIDTH
N_A = 2 * ATT_WIDTH + 2 * KV_WIDTH
IN_WIDTH = N_H + N_A
MIX_WIDTH = HG_WIDTH + ATT_WIDTH

ADAM_LR = 0.001
ADAM_B1 = 0.9
ADAM_B2 = 0.999
ADAM_EPS = 1e-08
ADAM_WD = 0.01
ADAM_STEP = 10

N_DEV = 8
MESH = pl.DeviceIdType.MESH
VMEM_LIMIT = 56 * 1024 * 1024

NN = ((1,), (0,))
NT = ((1,), (1,))
TN = ((0,), (0,))


def _dot(a, b, dims):
    return lax.dot_general(a.astype(BF16), b.astype(BF16), (dims, ((), ())), preferred_element_type=F32)


def _params(sem=None, **kw):
    return pltpu.CompilerParams(dimension_semantics=sem, vmem_limit_bytes=VMEM_LIMIT, **kw)


def _sigmoids(x):
    e = jnp.exp(-jnp.abs(x))
    r = 1.0 / (1.0 + e)
    er = e * r
    pos = x >= 0.0
    return jnp.where(pos, r, er), jnp.where(pos, er, r)


def _silu(x):
    return x * _sigmoids(x)[0]


def _silu_and_grad(x):
    s, ns = _sigmoids(x)
    return x * s, s * (1.0 + x * ns)


def _pick(n, prefs):
    for p in prefs:
        if n % p == 0:
            return p
    return n


def _inproj(x2, g, layer, w, name):
    T, D = x2.shape
    tm = _pick(T, (512, 256, 128))
    nchunk = 1024

    def body(x_ref, g_ref, w_ref, oh_ref, oa_ref, h_ref):
        x = x_ref[...]
        r = lax.rsqrt(jnp.mean(x * x, axis=-1, keepdims=True) + NORM_EPS)
        h = ((x * r) * g_ref[layer:layer + 1, :]).astype(BF16)
        h_ref[...] = h
        for j in range(0, N_H, nchunk):
            oh_ref[:, j:j + nchunk] = lax.dot_general(h, w_ref[j:j + nchunk, :], (NT, ((), ())),
                                                      preferred_element_type=F32)
        for j in range(0, N_A, N_A // 2):
            oa_ref[:, j:j + N_A // 2] = lax.dot_general(h, w_ref[N_H + j:N_H + j + N_A // 2, :], (NT, ((), ())),
                                                        preferred_element_type=F32)

    row = lambda w_: pl.BlockSpec((tm, w_), lambda i: (i, 0))
    return pl.pallas_call(
        body, name=name,
        grid=(T // tm,),
        in_specs=[row(D), pl.BlockSpec((DEPTH, D), lambda i: (0, 0)),
                  pl.BlockSpec((IN_WIDTH, D), lambda i: (0, 0), pipeline_mode=pl.Buffered(1))],
        out_specs=[row(N_H), row(N_A), row(D)],
        out_shape=[jax.ShapeDtypeStruct((T, N_H), F32), jax.ShapeDtypeStruct((T, N_A), F32),
                   jax.ShapeDtypeStruct((T, D), BF16)],
        compiler_params=_params(("parallel",)),
    )(x2, g, w)


def _mm_tn(pieces, b, name, out_dtype=BF16):
    T, m = b.shape
    tn = 256
    counts = [p.shape[1] // tn for p in pieces]
    starts = [sum(counts[:i]) for i in range(len(pieces))]
    n_p = len(pieces)

    def body(*refs):
        b_ref, o_ref = refs[n_p], refs[n_p + 1]
        i = pl.program_id(0)
        for p in range(n_p):
            @pl.when((i >= starts[p]) & (i < starts[p] + counts[p]))
            def _(p=p):
                o_ref[...] = lax.dot_general(refs[p][...], b_ref[...], (TN, ((), ())),
                                             preferred_element_type=F32).astype(out_dtype)

    piece_spec = lambda s, c: pl.BlockSpec((T, tn), lambda i: (0, jnp.clip(i - s, 0, c - 1)))
    return pl.pallas_call(
        body, name=name,
        grid=(sum(counts),),
        in_specs=[piece_spec(s, c) for s, c in zip(starts, counts)]
        + [pl.BlockSpec((T, m), lambda i: (0, 0), pipeline_mode=pl.Buffered(1))],
        out_specs=pl.BlockSpec((tn, m), lambda i: (i, 0)),
        out_shape=jax.ShapeDtypeStruct((sum(counts) * tn, m), out_dtype),
        compiler_params=_params(("arbitrary",)),
    )(*pieces, b)


_LEVELS = (0, 1, 2, 4, 8, 16, 32)
_CUM_L = (2, 4, 8, 16, 32, 64)
_ALL_KINDS = tuple(("c", L) for L in _CUM_L) + tuple(("r", L) for L in _CUM_L)
_MXU_KINDS = (("c", 2), ("c", 4), ("c", CHUNK), ("r", 2), ("r", 4))
N_CUM = len(_ALL_KINDS) * CHUNK
N_CUM_F = len(_MXU_KINDS) * CHUNK


def _cum_matrices():
    t = np.arange(CHUNK)[:, None]
    r = np.arange(CHUNK)[None, :]

    def mat(kind):
        c, L = kind
        return ((r // L == t // L) & ((r <= t) if c == "c" else (r > t))).astype(np.float32)

    fwd = np.concatenate([mat(kd) for kd in _MXU_KINDS], axis=0)
    full = np.concatenate([mat(kd) for kd in _ALL_KINDS], axis=0)
    return jnp.asarray(fwd, BF16), jnp.asarray(full.T.copy(), BF16)


def _level_masks():
    t = np.arange(CHUNK)[:, None]
    s = np.arange(CHUNK)[None, :]
    ms = []
    for L in _LEVELS:
        if L == 0:
            ms.append(t == s)
        else:
            ms.append((t // (2 * L) == s // (2 * L)) & ((t // L) % 2 == 1) & ((s // L) % 2 == 0))
    return jnp.asarray(np.stack(ms).astype(np.float32))


def _split3(x):
    hi = x.astype(BF16)
    r1 = x - hi.astype(F32)
    mid = r1.astype(BF16)
    lo = (r1 - mid.astype(F32)).astype(BF16)
    return hi, mid, lo


def _cum3(ts, x, terms=3):
    d = lambda p: lax.dot_general(ts, p, (NN, ((), ())), preferred_element_type=F32)
    return sum(d(p) for p in _split3(x)[:terms])


def _lb_terms(lbp, layer):
    mx = jnp.max(lbp, axis=0, keepdims=True)
    e = jnp.exp(lbp - mx)
    p = e / jnp.sum(e, axis=0, keepdims=True)
    cum = p[0:1]
    for j in range(1, layer + 1):
        cum = cum + p[j:j + 1]
    lb = cum - p[0:1]
    lbf = jnp.maximum(lb, LB_FLOOR)
    return dict(lbf=lbf, one_m=1.0 - lb, kcorr=lb - lbf, ind=jnp.where(lb > LB_FLOOR, 1.0, 0.0))


def _gate(x, lt):
    sig, nsig = _sigmoids(x)
    f = lt["lbf"] + lt["one_m"] * sig
    return jnp.log(f), lt["one_m"] * nsig + lt["kcorr"], f, sig, nsig


def _ck(x, ci):
    return x[ci * CHUNK:(ci + 1) * CHUNK]


def _block_cums(ts, g, nc):
    cs = [_cum3(ts, _ck(g, ci), terms=2) for ci in range(nc)]
    out = {kind: jnp.concatenate([c[CHUNK * i:CHUNK * (i + 1)] for c in cs], axis=0)
           for i, kind in enumerate(_MXU_KINDS)}
    b = out[("c", CHUNK)]
    ng = CHUNK // 8
    last = b.reshape(nc, ng, 8, HG_DIM)[:, :, 7:8, :]
    zero = jnp.zeros((nc, 1, 1, HG_DIM), F32)

    def spread(groups):
        return jnp.broadcast_to(jnp.concatenate(groups, axis=1), (nc, ng, 8, HG_DIM)).reshape(nc * CHUNK, HG_DIM)

    def get(kind):
        if kind in out:
            return out[kind]
        c, L = kind
        nb = L // 8
        first = lambda r: (r // nb) * nb
        if c == "c":
            return b - spread([last[:, first(r) - 1:first(r)] if r >= nb else zero for r in range(ng)])
        return spread([last[:, first(r) + nb - 1:first(r) + nb] for r in range(ng)]) - b

    return get


def _level_factors(cums, g, L):
    if L == 0:
        return None, None
    if L == 1:
        return jnp.exp(g[...]), None
    return jnp.exp(cums(("c", L))), jnp.exp(cums(("r", L)))


def _mul(a, e):
    return a if e is None else a * e


def _hg_block_fwd(qf, k, v, g, ts, m_ref, nc):
    cums = _block_cums(ts, g, nc)
    amat = [jnp.zeros((CHUNK, CHUNK), F32)] * nc
    for li, L in enumerate(_LEVELS):
        eq, ek = _level_factors(cums, g, L)
        ql, kl, m = _mul(qf, eq), _mul(k, ek), m_ref[li]
        amat = [amat[ci] + _dot(_ck(ql, ci), _ck(kl, ci), NT) * m for ci in range(nc)]
    b = cums(("c", CHUNK))
    kst = k * jnp.exp(cums(("r", CHUNK)))
    o = [_dot(amat[ci], _ck(v, ci), NN) for ci in range(nc)]
    kv = [_dot(_ck(v, ci), _ck(kst, ci), TN) for ci in range(nc)]
    dec = [jnp.exp(b[(ci + 1) * CHUNK - 1:(ci + 1) * CHUNK, :]) for ci in range(nc)]
    return o, dec, kv, qf * jnp.exp(b), amat


def _hg_block_bwd(qf, k, v, g, do, amat, ts, m_ref, nc):
    cums = _block_cums(ts, g, nc)
    dcs = {}
    da = [_dot(_ck(do, ci), _ck(v, ci), NT) for ci in range(nc)]
    dq = jnp.zeros(qf.shape, F32)
    dk = jnp.zeros(qf.shape, F32)
    dg = jnp.zeros(qf.shape, F32)
    for li, L in enumerate(_LEVELS):
        eq, ek = _level_factors(cums, g, L)
        qlb, klb, m = _mul(qf[...], eq).astype(BF16), _mul(k[...], ek).astype(BF16), m_ref[li]
        dal = [(da[ci] * m).astype(BF16) for ci in range(nc)]
        both = [(_dot(dal[ci], _ck(klb, ci), NN), _dot(dal[ci], _ck(qlb, ci), TN)) for ci in range(nc)]
        dql = _mul(jnp.concatenate([p[0] for p in both], axis=0), eq)
        dkl = _mul(jnp.concatenate([p[1] for p in both], axis=0), ek)
        dq = dq + dql
        dk = dk + dkl
        if L == 1:
            dg = dg + dql * qf[...]
        elif L > 1:
            dcs[("c", L)] = (dql * qf[...]).astype(BF16)
            dcs[("r", L)] = (dkl * k[...]).astype(BF16)
    b = cums(("c", CHUNK))
    e64 = jnp.exp(b)
    er64 = jnp.exp(cums(("r", CHUNK)))
    qb = (qf[...] * e64).astype(BF16)
    return dict(dq=dq, dk=dk, dg=dg, dcs=dcs, e64=e64, er64=er64, qf=qf, k=k, kst=(k[...] * er64).astype(BF16),
                dv=[_dot(amat[ci], _ck(do, ci), TN) for ci in range(nc)],
                dec=[jnp.exp(b[(ci + 1) * CHUNK - 1:(ci + 1) * CHUNK, :]) for ci in range(nc)],
                qd=[_dot(_ck(do, ci), _ck(qb, ci), TN) for ci in range(nc)])


def _hg_state_bwd(w, v, do, starts, ends, tst, nc):
    dqb = jnp.concatenate([_dot(_ck(do, ci), starts[ci], NN) for ci in range(nc)], axis=0)
    dkst = jnp.concatenate([_dot(_ck(v, ci), ends[ci], NN) for ci in range(nc)], axis=0)
    dqb, dkst = dqb * w["e64"], dkst * w["er64"]
    dq = w["dq"] + dqb
    dk = w["dk"] + dkst
    dv = jnp.concatenate([w["dv"][ci] + _dot(_ck(w["kst"], ci), ends[ci], NT) for ci in range(nc)], axis=0)
    trow = lax.broadcasted_iota(jnp.int32, (CHUNK, 1), 0)
    dtot = jnp.concatenate(
        [jnp.where(trow == CHUNK - 1, jnp.sum(ends[ci] * starts[ci], axis=0, keepdims=True) * w["dec"][ci], 0.0)
         for ci in range(nc)], axis=0)
    dcs = dict(w["dcs"])
    dcs[("c", CHUNK)] = (dqb * w["qf"][...] + dtot).astype(BF16)
    dcs[("r", CHUNK)] = (dkst * w["k"][...]).astype(BF16)
    dgs = [_dot(tst, jnp.concatenate([_ck(dcs[kind], ci) for kind in _ALL_KINDS], axis=0), NN) for ci in range(nc)]
    return dq, dk, dv, w["dg"] + jnp.concatenate(dgs, axis=0)


def _hgrn_fwd(proj_h, u_rows, lb_param, g_head, layer, name, phase=None):
    B, S, _ = proj_h.shape
    sb = _pick(S, (2048, 1024, 512, 256, 128, 64))
    nc = sb // CHUNK
    ts, _ = _cum_matrices()

    def body(*refs):
        ins, outs, (st,), p_in, p_out, p_sems = _split_refs(refs, 8, 12, 1, phase)
        q_ref, f_ref, i_ref, z_ref, lbp_ref, gh_ref, ts_ref, m_ref = ins
        o_ref, u_ref, sts_ref, am_ref = outs[:4]
        logf_ref, k_ref, qf_ref, sg_ref, qg_ref, zg_ref, fg_ref, sig_ref = outs[4:]
        h_id, b_id, s_id = pl.program_id(0), pl.program_id(1), pl.program_id(2)
        _hosted_start(phase, p_in, p_out, p_sems, (h_id == 0) & (b_id == 0) & (s_id == 0))

        @pl.when(s_id == 0)
        def _():
            st[...] = jnp.zeros_like(st)

        lt = _lb_terms(lbp_ref[...], layer)
        tsv = ts_ref[...]
        gh = gh_ref[layer:layer + 1, :]
        logf, k, _, sig, nsig = _gate(f_ref[...], lt)
        qf, qf_grad = _silu_and_grad(q_ref[...])
        sg, sg_grad = _silu_and_grad(z_ref[...])
        logf_ref[...], k_ref[...], qf_ref[...], sg_ref[...] = logf, k, qf, sg
        qg_ref[...] = qf_grad.astype(BF16)
        zg_ref[...] = sg_grad.astype(BF16)
        fg_ref[...] = (lt["one_m"] * sig * nsig).astype(BF16)
        sig_ref[...] = sig.astype(BF16)
        o_part, dec, kv, qb, amat = _hg_block_fwd(qf, k, i_ref[...], logf, tsv, m_ref, nc)
        for ci in range(nc):
            am_ref[ci] = amat[ci].astype(BF16)
        cur = st[...]
        starts = []
        for ci in range(nc):
            sts_ref[ci] = cur
            starts.append(cur)
            cur = cur * dec[ci] + kv[ci]
        st[...] = cur
        o = jnp.concatenate([o_part[ci] + _dot(_ck(qb, ci), starts[ci], NT) for ci in range(nc)], axis=0)
        o_ref[...] = o
        r = lax.rsqrt(jnp.mean(o * o, axis=-1, keepdims=True) + NORM_EPS)
        u_ref[...] = (((o * r) * gh) * sg).astype(BF16)
        _hosted_finish(phase, p_in, p_out, p_sems, (h_id == HG_HEADS - 1) & (b_id == B - 1) & (s_id == S // sb - 1))

    col = lambda base: pl.BlockSpec((None, sb, HG_DIM), lambda h, b, s: (b, s, base + h))
    p_ispecs, p_ospecs, p_oshapes, p_alias, p_scratch, p_args = _host_phase(phase, 8, 12)
    wide = lambda dt: jax.ShapeDtypeStruct((B, S, HG_WIDTH), dt)
    res = pl.pallas_call(
        body, name=name,
        grid=(HG_HEADS, B, S // sb),
        in_specs=[col(0), col(HG_HEADS), col(2 * HG_HEADS), col(3 * HG_HEADS),
                  pl.BlockSpec((DEPTH, HG_DIM), lambda h, b, s: (0, h)),
                  pl.BlockSpec((DEPTH, HG_DIM), lambda h, b, s: (0, 0)),
                  pl.BlockSpec((N_CUM_F, CHUNK), lambda h, b, s: (0, 0)),
                  pl.BlockSpec((len(_LEVELS), CHUNK, CHUNK), lambda h, b, s: (0, 0, 0))] + p_ispecs,
        out_specs=[col(0), col(0),
                   pl.BlockSpec((None, None, nc, HG_DIM, HG_DIM), lambda h, b, s: (b, h, s, 0, 0)),
                   pl.BlockSpec((None, None, nc, CHUNK, CHUNK), lambda h, b, s: (b, h, s, 0, 0))]
        + [col(0)] * 8 + p_ospecs,
        out_shape=[wide(F32),
                   jax.ShapeDtypeStruct((B, S, u_rows), BF16),
                   jax.ShapeDtypeStruct((B, HG_HEADS, S // CHUNK, HG_DIM, HG_DIM), F32),
                   jax.ShapeDtypeStruct((B, HG_HEADS, S // CHUNK, CHUNK, CHUNK), BF16)]
        + [wide(F32)] * 4 + [wide(BF16)] * 4 + p_oshapes,
        input_output_aliases=p_alias,
        scratch_shapes=[pltpu.VMEM((HG_DIM, HG_DIM), F32)] + p_scratch,
        compiler_params=_params(("arbitrary", "arbitrary", "arbitrary")),
    )(proj_h, proj_h, proj_h, proj_h, lb_param, g_head, ts, _level_masks(), *p_args)
    return res[0], res[1], tuple(res[2:12]), list(res[12:])


def _hgrn_bwd(proj_h, o_h, du, kept, lb_param, g_head, layer, name, phase=None):
    B, S, _ = proj_h.shape
    sb = _pick(S, (512, 256, 128, 64))
    nc = sb // CHUNK
    ns = S // sb
    ts, tst = _cum_matrices()

    def body(*refs):
        ins, outs, (dst,), p_in, p_out, p_sems = _split_refs(refs, 18, 6, 1, phase)
        (i_ref, o_ref, du_ref, sts_ref, am_ref, logf_ref, k_ref, qf_ref, sg_ref, qg_ref, zg_ref, fg_ref, sig_ref,
         lbp_ref, gh_ref, ts_ref, tst_ref, m_ref) = ins
        dq_ref, df_ref, di_ref, dz_ref, dlb_ref, dgh_ref = outs
        h_id, b_id, s_id = pl.program_id(0), pl.program_id(1), pl.program_id(2)
        _hosted_start(phase, p_in, p_out, p_sems, (h_id == 0) & (b_id == 0) & (s_id == 0))

        @pl.when(s_id == 0)
        def _():
            dst[...] = jnp.zeros_like(dst)

        @pl.when((b_id == 0) & (s_id == 0))
        def _():
            dlb_ref[...] = jnp.zeros_like(dlb_ref)

        @pl.when((h_id == 0) & (b_id == 0) & (s_id == 0))
        def _():
            dgh_ref[...] = jnp.zeros_like(dgh_ref)

        lt = _lb_terms(lbp_ref[...], layer)
        gh = gh_ref[layer:layer + 1, :]
        tsv = ts_ref[...]
        tstv = tst_ref[...]
        sg = sg_ref[...]
        o = o_ref[...]
        dub = du_ref[...]
        r = lax.rsqrt(jnp.mean(o * o, axis=-1, keepdims=True) + NORM_EPS)
        n = o * r
        dz_ref[...] = (dub * (n * gh) * zg_ref[...].astype(F32)).astype(BF16)
        dgh_ref[...] += jnp.sum(dub * sg * n, axis=0, keepdims=True)
        dn = dub * sg * gh
        do = (r * (dn - n * jnp.mean(dn * n, axis=-1, keepdims=True))).astype(BF16)
        v = i_ref[...].astype(BF16)
        w = _hg_block_bwd(qf_ref, k_ref, v, logf_ref, do, [am_ref[ci] for ci in range(nc)], tsv, m_ref, nc)
        cur = dst[...]
        ends = [None] * nc
        for ci in reversed(range(nc)):
            ends[ci] = cur
            cur = cur * w["dec"][ci] + w["qd"][ci]
        dst[...] = cur
        dq, dk, dv, dg = _hg_state_bwd(w, v, do, [sts_ref[ci] for ci in range(nc)], ends, tstv, nc)
        di_ref[...] = dv.astype(BF16)
        dq_ref[...] = (dq * qg_ref[...].astype(F32)).astype(BF16)
        f = jnp.exp(logf_ref[...])
        scaled = (dg - f * dk) / f
        df_ref[...] = (scaled * fg_ref[...].astype(F32)).astype(BF16)
        dlb_ref[...] += jnp.sum(scaled * (lt["ind"] - sig_ref[...].astype(F32)), axis=0, keepdims=True)
        _hosted_finish(phase, p_in, p_out, p_sems, (h_id == HG_HEADS - 1) & (b_id == B - 1) & (s_id == ns - 1))

    col = lambda base: pl.BlockSpec((None, sb, HG_DIM), lambda h, b, s: (b, ns - 1 - s, base + h))
    out_col = pl.BlockSpec((None, sb, HG_DIM), lambda h, b, s: (b, ns - 1 - s, h))
    dt = jax.ShapeDtypeStruct((B, S, HG_WIDTH), BF16)
    p_ispecs, p_ospecs, p_oshapes, p_alias, p_scratch, p_args = _host_phase(phase, 18, 6)
    res = pl.pallas_call(
        body, name=name,
        grid=(HG_HEADS, B, ns),
        in_specs=[col(2 * HG_HEADS), col(0), col(0),
                  pl.BlockSpec((None, None, nc, HG_DIM, HG_DIM), lambda h, b, s: (b, h, ns - 1 - s, 0, 0)),
                  pl.BlockSpec((None, None, nc, CHUNK, CHUNK), lambda h, b, s: (b, h, ns - 1 - s, 0, 0))]
        + [col(0)] * 8
        + [pl.BlockSpec((DEPTH, HG_DIM), lambda h, b, s: (0, h)),
           pl.BlockSpec((DEPTH, HG_DIM), lambda h, b, s: (0, 0)),
           pl.BlockSpec((N_CUM_F, CHUNK), lambda h, b, s: (0, 0)),
           pl.BlockSpec((CHUNK, N_CUM), lambda h, b, s: (0, 0)),
           pl.BlockSpec((len(_LEVELS), CHUNK, CHUNK), lambda h, b, s: (0, 0, 0))] + p_ispecs,
        out_specs=[out_col, out_col, out_col, out_col,
                   pl.BlockSpec((1, HG_DIM), lambda h, b, s: (0, h)),
                   pl.BlockSpec((1, HG_DIM), lambda h, b, s: (0, 0))] + p_ospecs,
        out_shape=[dt, dt, dt, dt, jax.ShapeDtypeStruct((1, HG_WIDTH), F32),
                   jax.ShapeDtypeStruct((1, HG_DIM), F32)] + p_oshapes,
        input_output_aliases=p_alias,
        scratch_shapes=[pltpu.VMEM((HG_DIM, HG_DIM), F32)] + p_scratch,
        compiler_params=_params(("arbitrary", "arbitrary", "arbitrary")),
    )(proj_h, o_h, du, *kept, lb_param, g_head, ts, tst, _level_masks(), *p_args)
    return tuple(res[:6]) + (list(res[6:]),)


def _rope_tables(S):
    half = ATT_DIM // 2
    inv_freq = np.float32(ROPE_THETA) ** (-np.arange(half, dtype=np.float32) / half)
    ang = np.arange(S, dtype=np.float32)[:, None] * inv_freq[None, :]
    cos = np.cos(ang)
    sin = np.sin(ang)
    cos = np.concatenate([cos, cos, cos, cos], axis=1)
    sin = np.concatenate([-sin, sin, -sin, sin], axis=1)
    return jnp.asarray(cos, F32), jnp.asarray(sin, F32)


def _attn_common():
    lane = lax.broadcasted_iota(jnp.int32, (1, 2 * ATT_DIM), 1)
    first_half = (lane % ATT_DIM) < (ATT_DIM // 2)
    left = lane < ATT_DIM

    def swap(x):
        return jnp.where(first_half, pltpu.roll(x, 128 - ATT_DIM // 2, 1), pltpu.roll(x, ATT_DIM // 2, 1))

    def rope(x, cos, sin):
        return x * cos + swap(x) * sin

    def rope_bwd(dy, cos, sin):
        return dy * cos + swap(dy * sin)

    def dup(x):
        xs = pltpu.roll(x, ATT_DIM, 1)
        return [jnp.where(left, x, xs), jnp.where(left, xs, x)]

    return left, rope, rope_bwd, dup


GROUP = ATT_HEADS // 2
GROUP_ROWS = GROUP * ATT_BLOCK


def _attn_bias(i):
    r = lax.broadcasted_iota(jnp.int32, (ATT_BLOCK, 2 * ATT_BLOCK), 0)
    c = lax.broadcasted_iota(jnp.int32, (ATT_BLOCK, 2 * ATT_BLOCK), 1)
    ok = (c > r) & (c <= r + ATT_BLOCK) & ((c >= ATT_BLOCK) | (i > 0))
    return jnp.where(ok, 0.0, NEG_INF)


def _stack_heads(pairs, left):
    rows = []
    for x in pairs:
        rows += [jnp.where(left, x, 0.0), jnp.where(left, 0.0, x)]
    return jnp.concatenate(rows, axis=0)


def _unstack_heads(y, left, pp):
    r0 = 2 * pp * ATT_BLOCK
    return jnp.where(left, y[r0:r0 + ATT_BLOCK], y[r0 + ATT_BLOCK:r0 + 2 * ATT_BLOCK])


def _row_sums(x):
    return _dot(x, jnp.ones((x.shape[1], 128), BF16), NN)


def _attn_probs(qs, kd, vd, sink, bias):
    n = range(len(qs))
    rows = qs[0].shape[0]
    s = [(_dot(qs[j], kd[j], NT).reshape(rows // ATT_BLOCK, ATT_BLOCK, 2 * ATT_BLOCK) * ATT_SCALE + bias[None])
         .reshape(rows, 2 * ATT_BLOCK) for j in n]
    m = [jnp.max(jnp.maximum(jnp.maximum(s[j][:, :128], s[j][:, 128:]), sink[j]), axis=-1, keepdims=True) for j in n]
    pu = [jnp.exp(s[j] - m[j]) for j in n]
    es = [jnp.exp(sink[j] - m[j]) for j in n]
    ones = jnp.ones((2 * ATT_BLOCK, 128), BF16)
    ov = [_dot(pu[j], jnp.concatenate([vd[j].astype(BF16), ones], axis=1), NN) for j in n]
    inv = [1.0 / (ov[j][:, 128:] + es[j]) for j in n]
    return ([pu[j] * jnp.concatenate([inv[j], inv[j]], axis=1) for j in n], [es[j] * inv[j] for j in n],
            [ov[j][:, :128] * inv[j] for j in n])


def _sink_rows(sinks_l):
    return jnp.broadcast_to(jnp.repeat(sinks_l, ATT_BLOCK)[:, None], (ATT_HEADS * ATT_BLOCK, 128))


_Z0 = (2 * ATT_WIDTH + 2 * KV_WIDTH - ATT_WIDTH) // 256


def _attn_fwd(proj_a, u, sinks_l, cos, sin, name, phase=None):
    B, S, _ = proj_a.shape
    nb = S // ATT_BLOCK

    def body(*refs):
        ins, (u_ref, p_ref, o_ref, ps_ref, qs_ref), _, p_in, p_out, p_sems = _split_refs(refs, 13, 5, 0, phase)
        q_ref, kvc_ref, kvp_ref, z0, z1, z2, z3, cos_ref, sin_ref, cosp_ref, sinp_ref, sinks_ref, _ = ins
        i = pl.program_id(1)
        _hosted_start(phase, p_in, p_out, p_sems, (pl.program_id(0) == 0) & (i == 0))
        left, rope, _, dup = _attn_common()
        cos_c, sin_c = cos_ref[...], sin_ref[...]
        kvc = kvc_ref[...]
        kvp = kvp_ref[...]
        kw = jnp.concatenate([rope(kvp[:, :KV_WIDTH], cosp_ref[...], sinp_ref[...]),
                              rope(kvc[:, :KV_WIDTH], cos_c, sin_c)], axis=0)
        vw = jnp.concatenate([kvp[:, KV_WIDTH:], kvc[:, KV_WIDTH:]], axis=0)
        kd, vd = dup(kw), dup(vw)
        bias = _attn_bias(i)
        zs = (z0, z1, z2, z3)
        pairs = [range(4 * kvh, 4 * kvh + 4) for kvh in range(2)]
        qs = [_stack_heads([rope(q_ref[:, 128 * pr:128 * (pr + 1)], cos_c, sin_c) for pr in pairs[kvh]], left)
              for kvh in range(2)]
        sink = [sinks_ref[kvh * GROUP_ROWS:(kvh + 1) * GROUP_ROWS, :] for kvh in range(2)]
        p, ps, o = _attn_probs(qs, kd, vd, sink, bias)
        eye = (lax.broadcasted_iota(jnp.int32, (ATT_BLOCK, 128), 0)
               == lax.broadcasted_iota(jnp.int32, (ATT_BLOCK, 128), 1))
        for kvh in range(2):
            p_ref[kvh] = p[kvh].astype(BF16)
            qs_ref[kvh] = qs[kvh].astype(BF16)
            for g in range(GROUP):
                blk = ps[kvh][g * ATT_BLOCK:(g + 1) * ATT_BLOCK, :]
                ps_ref[kvh * GROUP + g:kvh * GROUP + g + 1, :] = jnp.sum(jnp.where(eye, blk, 0.0), axis=0, keepdims=True)
            for pp, pr in enumerate(pairs[kvh]):
                z = zs[pr // 2][:, 128 * (pr % 2):128 * (pr % 2 + 1)]
                o128 = _unstack_heads(o[kvh], left, pp)
                o_ref[:, 128 * pr:128 * (pr + 1)] = o128.astype(BF16)
                u_ref[:, 128 * pr:128 * (pr + 1)] = (o128 * _silu(z)).astype(BF16)
        _hosted_finish(phase, p_in, p_out, p_sems, (pl.program_id(0) == B - 1) & (i == nb - 1))

    rowblk = lambda w, cb: pl.BlockSpec((None, ATT_BLOCK, w), lambda b, i: (b, i, cb))
    tab = pl.BlockSpec((ATT_BLOCK, 128), lambda b, i: (i, 0))
    tabp = pl.BlockSpec((ATT_BLOCK, 128), lambda b, i: (jnp.maximum(i - 1, 0), 0))
    p_ispecs, p_ospecs, p_oshapes, p_alias, p_scratch, p_args = _host_phase(phase, 13, 5)
    res = pl.pallas_call(
        body, name=name,
        grid=(B, nb),
        in_specs=[rowblk(ATT_WIDTH, 0), rowblk(256, 4),
                  pl.BlockSpec((None, ATT_BLOCK, 256), lambda b, i: (b, jnp.maximum(i - 1, 0), 4)),
                  rowblk(256, _Z0), rowblk(256, _Z0 + 1), rowblk(256, _Z0 + 2), rowblk(256, _Z0 + 3),
                  tab, tab, tabp, tabp,
                  pl.BlockSpec((ATT_HEADS * ATT_BLOCK, 128), lambda b, i: (0, 0)),
                  pl.BlockSpec(memory_space=pl.ANY)] + p_ispecs,
        out_specs=[pl.BlockSpec((None, ATT_BLOCK, ATT_WIDTH), lambda b, i: (b, i, 1)),
                   pl.BlockSpec((None, None, 2, GROUP_ROWS, 2 * ATT_BLOCK), lambda b, i: (b, i, 0, 0, 0)),
                   pl.BlockSpec((None, ATT_BLOCK, ATT_WIDTH), lambda b, i: (b, i, 0)),
                   pl.BlockSpec((None, None, ATT_HEADS, 128), lambda b, i: (b, i, 0, 0)),
                   pl.BlockSpec((None, None, 2, GROUP_ROWS, 128), lambda b, i: (b, i, 0, 0, 0))] + p_ospecs,
        out_shape=[jax.ShapeDtypeStruct(u.shape, BF16),
                   jax.ShapeDtypeStruct((B, nb, 2, GROUP_ROWS, 2 * ATT_BLOCK), BF16),
                   jax.ShapeDtypeStruct((B, S, ATT_WIDTH), BF16),
                   jax.ShapeDtypeStruct((B, nb, ATT_HEADS, 128), F32),
                   jax.ShapeDtypeStruct((B, nb, 2, GROUP_ROWS, 128), BF16)] + p_oshapes,
        input_output_aliases={12: 0, **p_alias},
        scratch_shapes=p_scratch,
        compiler_params=_params(("arbitrary", "arbitrary")),
    )(proj_a, proj_a, proj_a, proj_a, proj_a, proj_a, proj_a, cos, sin, cos, sin, sinks_l, u, *p_args)
    return res[0], tuple(res[1:5]), list(res[5:])


def _attn_bwd(proj_a, du, kept, cos, sin, name, phase=None):
    B, S, _ = proj_a.shape
    nb = S // ATT_BLOCK
    p_kept, o_kept, ps_kept, qs_kept = kept

    def body(*refs):
        ins, outs, (carry, sk_acc), p_in, p_out, p_sems = _split_refs(refs, 15, 4, 2, phase)
        (qs_ref, kvc_ref, kvp_ref, z0, z1, z2, z3, du_ref, cos_ref, sin_ref, cosp_ref, sinp_ref,
         p_ref, o_ref, ps_ref) = ins
        dq_ref, dkv_ref, dz_ref, dsk_ref = outs
        b_id, i = pl.program_id(0), pl.program_id(1)
        _hosted_start(phase, p_in, p_out, p_sems, (b_id == 0) & (i == 0))

        @pl.when((b_id == 0) & (i == 0))
        def _():
            sk_acc[...] = jnp.zeros_like(sk_acc)

        @pl.when(i == 0)
        def _():
            carry[...] = jnp.zeros_like(carry)

        @pl.when(i < nb)
        def _():
            left, rope, rope_bwd, dup = _attn_common()
            cos_c, sin_c = cos_ref[...], sin_ref[...]
            cos_p, sin_p = cosp_ref[...], sinp_ref[...]
            kvc = kvc_ref[...]
            kvp = kvp_ref[...]
            kw = jnp.concatenate([rope(kvp[:, :KV_WIDTH], cos_p, sin_p), rope(kvc[:, :KV_WIDTH], cos_c, sin_c)], axis=0)
            vw = jnp.concatenate([kvp[:, KV_WIDTH:], kvc[:, KV_WIDTH:]], axis=0)
            kd, vd = dup(kw), dup(vw)
            zs = (z0, z1, z2, z3)
            units = [(kvh, hf) for kvh in range(2) for hf in range(2)]
            half = GROUP_ROWS // 2
            pairs = [range(4 * kvh + 2 * hf, 4 * kvh + 2 * hf + 2) for kvh, hf in units]
            ku = [kd[kvh] for kvh, _ in units]
            vu = [vd[kvh] for kvh, _ in units]
            ps_all = ps_ref[...]
            head_row = lax.broadcasted_iota(jnp.int32, (ATT_HEADS, 128), 0)
            eye = (lax.broadcasted_iota(jnp.int32, (ATT_BLOCK, 128), 0)
                   == lax.broadcasted_iota(jnp.int32, (ATT_BLOCK, 128), 1))

            def first(j):
                kvh, hf = units[j]
                p = p_ref[kvh, hf * half:(hf + 1) * half, :]
                parts = []
                for pr in pairs[j]:
                    cols = slice(128 * pr, 128 * (pr + 1))
                    sg, sg_grad = _silu_and_grad(zs[pr // 2][:, 128 * (pr % 2):128 * (pr % 2 + 1)])
                    du128 = du_ref[:, cols]
                    dz_ref[:, cols] = (du128 * o_ref[:, cols].astype(F32) * sg_grad).astype(BF16)
                    parts.append(du128 * sg)
                dos = _stack_heads(parts, left)
                dp = _dot(dos, vu[j], NT)
                delta = _row_sums(p.astype(F32) * dp)
                ds = (p.astype(F32) * (dp - jnp.concatenate([delta, delta], axis=1)) * ATT_SCALE).astype(BF16)
                sk = jnp.zeros((ATT_HEADS, 128), F32)
                for hh in range(4):
                    hd = kvh * GROUP + 4 * hf + hh
                    drow = jnp.sum(jnp.where(eye, delta[hh * ATT_BLOCK:(hh + 1) * ATT_BLOCK, :], 0.0), axis=0,
                                   keepdims=True)
                    sk = sk - jnp.where(head_row == hd, ps_all * drow, 0.0)
                sk_acc[...] += sk
                return ds, p, dos.astype(BF16), qs_ref[kvh, hf * half:(hf + 1) * half, :]

            def second(j, ds, p, dos, qs):
                dqs = _dot(ds, ku[j], NN)
                for pp, pr in enumerate(pairs[j]):
                    dq_ref[:, 128 * pr:128 * (pr + 1)] = rope_bwd(_unstack_heads(dqs, left, pp),
                                                                  cos_c, sin_c).astype(BF16)
                return _dot(ds, qs, TN), _dot(p, dos, TN)

            got, dku, dvu = {}, [None] * len(units), [None] * len(units)
            for j in range(len(units) + 1):
                if j < len(units):
                    got[j] = first(j)
                if j >= 1:
                    dku[j - 1], dvu[j - 1] = second(j - 1, *got.pop(j - 1))
            dkd = [dku[0] + dku[1], dku[2] + dku[3]]
            dvd = [dvu[0] + dvu[1], dvu[2] + dvu[3]]
            fold = lambda pr: jnp.where(left, pr[0] + pltpu.roll(pr[0], ATT_DIM, 1), pr[1] + pltpu.roll(pr[1], ATT_DIM, 1))
            dkw = fold(dkd)
            dvw = fold(dvd)
            prev = jnp.concatenate([rope_bwd(dkw[:ATT_BLOCK], cos_p, sin_p), dvw[:ATT_BLOCK]], axis=1)
            cur = jnp.concatenate([rope_bwd(dkw[ATT_BLOCK:], cos_c, sin_c), dvw[ATT_BLOCK:]], axis=1)
            dkv_ref[...] = (carry[...] + prev).astype(BF16)
            carry[...] = cur

        @pl.when(i == nb)
        def _():
            dkv_ref[...] = carry[...].astype(BF16)

        @pl.when((b_id == B - 1) & (i == nb))
        def _():
            diag = (lax.broadcasted_iota(jnp.int32, (ATT_HEADS, 128), 0)
                    == lax.broadcasted_iota(jnp.int32, (ATT_HEADS, 128), 1))
            tot = jnp.sum(sk_acc[...], axis=1, keepdims=True)
            dsk_ref[...] = jnp.sum(jnp.where(diag, tot, 0.0), axis=0, keepdims=True)

        _hosted_finish(phase, p_in, p_out, p_sems, (b_id == B - 1) & (i == nb))

    cl = lambda i: jnp.minimum(i, nb - 1)
    pv = lambda i: jnp.maximum(jnp.minimum(i, nb - 1) - 1, 0)
    rowblk = lambda w, cb: pl.BlockSpec((None, ATT_BLOCK, w), lambda b, i: (b, cl(i), cb))
    tab = pl.BlockSpec((ATT_BLOCK, 128), lambda b, i: (cl(i), 0))
    tabp = pl.BlockSpec((ATT_BLOCK, 128), lambda b, i: (pv(i), 0))
    p_ispecs, p_ospecs, p_oshapes, p_alias, p_scratch, p_args = _host_phase(phase, 15, 4)
    res = pl.pallas_call(
        body, name=name,
        grid=(B, nb + 1),
        in_specs=[pl.BlockSpec((None, None, 2, GROUP_ROWS, 128), lambda b, i: (b, cl(i), 0, 0, 0)), rowblk(256, 4),
                  pl.BlockSpec((None, ATT_BLOCK, 256), lambda b, i: (b, pv(i), 4)),
                  rowblk(256, _Z0), rowblk(256, _Z0 + 1), rowblk(256, _Z0 + 2), rowblk(256, _Z0 + 3),
                  rowblk(ATT_WIDTH, 1),
                  tab, tab, tabp, tabp,
                  pl.BlockSpec((None, None, 2, GROUP_ROWS, 2 * ATT_BLOCK), lambda b, i: (b, cl(i), 0, 0, 0)),
                  rowblk(ATT_WIDTH, 0),
                  pl.BlockSpec((None, None, ATT_HEADS, 128), lambda b, i: (b, cl(i), 0, 0))] + p_ispecs,
        out_specs=[rowblk(ATT_WIDTH, 0),
                   pl.BlockSpec((None, ATT_BLOCK, 256), lambda b, i: (b, jnp.maximum(i - 1, 0), 0)),
                   rowblk(ATT_WIDTH, 0),
                   pl.BlockSpec((1, 128), lambda b, i: (0, 0))] + p_ospecs,
        out_shape=[jax.ShapeDtypeStruct((B, S, ATT_WIDTH), BF16), jax.ShapeDtypeStruct((B, S, 256), BF16),
                   jax.ShapeDtypeStruct((B, S, ATT_WIDTH), BF16), jax.ShapeDtypeStruct((1, 128), F32)] + p_oshapes,
        input_output_aliases=p_alias,
        scratch_shapes=[pltpu.VMEM((ATT_BLOCK, 256), F32), pltpu.VMEM((ATT_HEADS, 128), F32)] + p_scratch,
        compiler_params=_params(("arbitrary", "arbitrary")),
    )(qs_kept, proj_a, proj_a, proj_a, proj_a, proj_a, proj_a, du, cos, sin, cos, sin, p_kept, o_kept, ps_kept, *p_args)
    return tuple(res[:4]) + (list(res[4:]),)


def _outproj_fwd(u2, w_out, x2, g_post, layer, target2, name):
    T, D = x2.shape
    tm = _pick(T, (512, 256, 128))
    last = target2 is not None

    def body(u_ref, w_ref, x_ref, g_ref, *rest):
        y = lax.dot_general(u_ref[...], w_ref[...], (NN, ((), ())), preferred_element_type=F32)
        r = lax.rsqrt(jnp.mean(y * y, axis=-1, keepdims=True) + NORM_EPS)
        xn = x_ref[...] + (y * r) * g_ref[layer:layer + 1, :]
        if last:
            t_ref, y_ref, dx_ref, loss_ref = rest
            err = xn - t_ref[...]
            dx_ref[...] = err * (1.0 / D)
            sq = err * err
            acc = sq[:, 0:128]
            for kk in range(1, D // 128):
                acc = acc + sq[:, 128 * kk:128 * (kk + 1)]
            part = jnp.sum(acc.reshape(tm // 8, 8, 128), axis=0) * (0.5 / D)

            @pl.when(pl.program_id(0) == 0)
            def _():
                loss_ref[...] = jnp.zeros_like(loss_ref)

            loss_ref[...] += part
        else:
            y_ref, xn_ref = rest
            xn_ref[...] = xn
        y_ref[...] = y

    row = pl.BlockSpec((tm, D), lambda i: (i, 0))
    in_specs = [pl.BlockSpec((tm, MIX_WIDTH), lambda i: (i, 0)),
                pl.BlockSpec((MIX_WIDTH, D), lambda i: (0, 0)), row,
                pl.BlockSpec((DEPTH, D), lambda i: (0, 0))]
    args = [u2, w_out, x2, g_post]
    out_specs = [row, row]
    out_shape = [jax.ShapeDtypeStruct((T, D), F32), jax.ShapeDtypeStruct((T, D), F32)]
    if last:
        in_specs.append(row)
        args.append(target2)
        out_specs.append(pl.BlockSpec((8, 128), lambda i: (0, 0)))
        out_shape.append(jax.ShapeDtypeStruct((8, 128), F32))
    return pl.pallas_call(
        body, name=name, grid=(T // tm,), in_specs=in_specs, out_specs=out_specs, out_shape=out_shape,
        compiler_params=_params(("arbitrary",)),
    )(*args)


def _outproj_bwd(dxn2, y2, g_post, layer, w_out, u2, name):
    T, D = y2.shape
    N = w_out.shape[0]
    tm = _pick(T, (512, 256, 128))
    nt = T // tm

    def body(dx_ref, y_ref, g_ref, w_ref, u_ref, dg_ref, du_ref, dw_ref, acc, wacc):
        i = pl.program_id(0)

        @pl.when(i == 0)
        def _():
            acc[...] = jnp.zeros_like(acc)
            wacc[...] = jnp.zeros_like(wacc)

        y = y_ref[...]
        dxn = dx_ref[...]
        r = lax.rsqrt(jnp.mean(y * y, axis=-1, keepdims=True) + NORM_EPS)
        n = y * r
        dn = dxn * g_ref[layer:layer + 1, :]
        dy = (r * (dn - n * jnp.mean(dn * n, axis=-1, keepdims=True))).astype(BF16)
        du_ref[...] = lax.dot_general(dy, w_ref[...], (NT, ((), ())), preferred_element_type=F32)
        wacc[...] += lax.dot_general(u_ref[...], dy, (TN, ((), ())), preferred_element_type=F32)
        acc[...] += jnp.sum((dxn * n).reshape(tm // 8, 8, D), axis=0)

        @pl.when(i == nt - 1)
        def _():
            dg_ref[...] = jnp.sum(acc[...], axis=0, keepdims=True)
            dw_ref[...] = wacc[...].astype(BF16)

    row = pl.BlockSpec((tm, D), lambda i: (i, 0))
    wide = pl.BlockSpec((tm, N), lambda i: (i, 0))
    vec = pl.BlockSpec((1, D), lambda i: (0, 0))
    whole = pl.BlockSpec((N, D), lambda i: (0, 0))
    return pl.pallas_call(
        body, name=name, grid=(nt,),
        in_specs=[row, row, pl.BlockSpec((DEPTH, D), lambda i: (0, 0)),
                  pl.BlockSpec((N, D), lambda i: (0, 0), pipeline_mode=pl.Buffered(1)), wide],
        out_specs=[vec, wide, whole],
        out_shape=[jax.ShapeDtypeStruct((1, D), F32), jax.ShapeDtypeStruct((T, N), F32),
                   jax.ShapeDtypeStruct((N, D), BF16)],
        scratch_shapes=[pltpu.VMEM((8, D), F32), pltpu.VMEM((N, D), F32)],
        compiler_params=_params(("arbitrary",)),
    )(dxn2, y2, g_post, w_out, u2)


def _inproj_bwd(pieces, w_t, x2, dxn2, g_pre, layer, name, phase=None):
    T, D = x2.shape
    widths = [p.shape[1] for p in pieces]
    offs = [sum(widths[:i]) for i in range(len(pieces))]
    n_p = len(pieces)
    tm = _pick(T, (256, 128))
    nt = T // tm

    def body(*refs):
        ins, (dx_ref, dg_ref), (acc,), p_in, p_out, p_sems = _split_refs(refs, n_p + 4, 2, 1, phase)
        w_ref, x_ref, dxn_ref, g_ref = ins[n_p:]
        i = pl.program_id(0)
        _hosted_start(phase, p_in, p_out, p_sems, i == 0)

        @pl.when(i == 0)
        def _():
            acc[...] = jnp.zeros_like(acc)

        dh = jnp.zeros((tm, D), F32)
        for p in range(n_p):
            dh = dh + lax.dot_general(ins[p][...], w_ref[offs[p]:offs[p] + widths[p], :], (NN, ((), ())),
                                      preferred_element_type=F32)
        x = x_ref[...]
        r = lax.rsqrt(jnp.mean(x * x, axis=-1, keepdims=True) + NORM_EPS)
        n = x * r
        dn = dh * g_ref[layer:layer + 1, :]
        dx_ref[...] = dxn_ref[...] + r * (dn - n * jnp.mean(dn * n, axis=-1, keepdims=True))
        acc[...] += jnp.sum((dh * n).reshape(tm // 8, 8, D), axis=0)

        @pl.when(i == nt - 1)
        def _():
            dg_ref[...] = jnp.sum(acc[...], axis=0, keepdims=True)

        _hosted_finish(phase, p_in, p_out, p_sems, i == nt - 1)

    row = pl.BlockSpec((tm, D), lambda i: (i, 0))
    vec = pl.BlockSpec((1, D), lambda i: (0, 0))
    p_ispecs, p_ospecs, p_oshapes, p_alias, p_scratch, p_args = _host_phase(phase, n_p + 4, 2)
    res = pl.pallas_call(
        body, name=name, grid=(nt,),
        in_specs=[pl.BlockSpec((tm, w), lambda i: (i, 0)) for w in widths]
        + [pl.BlockSpec((sum(widths), D), lambda i: (0, 0), pipeline_mode=pl.Buffered(1)), row, row,
           pl.BlockSpec((DEPTH, D), lambda i: (0, 0))] + p_ispecs,
        out_specs=[row, vec] + p_ospecs,
        out_shape=[jax.ShapeDtypeStruct((T, D), F32), jax.ShapeDtypeStruct((1, D), F32)] + p_oshapes,
        input_output_aliases=p_alias,
        scratch_shapes=[pltpu.VMEM((8, D), F32)] + p_scratch,
        compiler_params=_params(("arbitrary",)),
    )(*pieces, w_t, x2, dxn2, g_pre, *p_args)
    return res[0], res[1], list(res[2:])


def _step(x, target, g_pre, g_post, lb_param, g_head, sinks, shards=None, full=None):
    B, S, D = x.shape
    T = B * S
    dist = shards is not None
    first, last = 0, DEPTH - 1
    if dist:
        a_loc, b_loc = shards
        ra, rb = a_loc.shape[1], b_loc.shape[1]
        side = _own_side_blocks()
        a_full, b_full = _place_own([a_loc, b_loc], side, "place_own")
        w_in0 = _gather_one_call(a_full[0], "gather_in0")
        w_in, w_out = [w_in0, None], [None, None]
    else:
        w_in, w_out = list(full[0]), list(full[1])
    cos, sin = _rope_tables(S)
    saved = []
    xs = x
    loss_part = None
    dxn = None
    for l in range(DEPTH):
        x2 = xs.reshape(T, D)
        proj_h, proj_a, h = _inproj(x2, g_pre, l, w_in[l], f"inproj{l}")
        proj_h = proj_h.reshape(B, S, N_H)
        proj_a = proj_a.reshape(B, S, N_A)
        phase = None
        if dist and l == first:
            phase = _gather_ici_phase([a_full[1], b_full[0]])
        if dist and l == last:
            phase = _gather_d2d_phase([w_out1_part], [rb])
        o_h, u, states, got = _hgrn_fwd(proj_h, MIX_WIDTH, lb_param, g_head, l, f"hgrn_fwd{l}", phase)
        phase = None
        if dist and l == first:
            phase = _merge_phases(_gather_d2d_phase(got, [ra, rb]),
                                  _gather_ici_phase([b_full[1]]))
        if dist and l == last:
            w_out[1] = got[0]
        u, kept_a, got = _attn_fwd(proj_a, u, _sink_rows(sinks[l]), cos, sin, f"attn_fwd{l}", phase)
        if dist and l == first:
            w_in[1], w_out[0], w_out1_part = got
        u2 = u.reshape(T, MIX_WIDTH)
        if l < last:
            y, xn = _outproj_fwd(u2, w_out[l], x2, g_post, l, None, f"outproj{l}")
            xn = xn.reshape(B, S, D)
        else:
            y, dxn, loss_part = _outproj_fwd(u2, w_out[l], x2, g_post, l, target.reshape(T, D), f"outproj{l}")
            xn = None
        saved.append((x2, h, proj_h, proj_a, o_h, u2, states, kept_a, y))
        xs = xn

    dw_in, dw_out = [None] * DEPTH, [None] * DEPTH
    dg_pre, dg_post, dlb, dg_head, dsinks = [], [], [], [], []
    for l in reversed(range(DEPTH)):
        x2, h, proj_h, proj_a, o_h, u2, states, kept_a, y = saved[l]
        dgp, du, dw_out[l] = _outproj_bwd(dxn, y, g_post, l, w_out[l], u2, f"outproj_bwd{l}")
        du = du.reshape(B, S, MIX_WIDTH)
        phase = None
        if dist:
            phase = _reduce_d2d_phase([dw_out[l]], [rb])
            if l == first:
                phase = _merge_phases(_reduce_ici_phase([part_in1]), phase)
        dqh, dfh, dih, dzh, dlb_l, dgh, got = _hgrn_bwd(
            proj_h, o_h, du, states, lb_param, g_head, l, f"hgrn_bwd{l}", phase)
        if dist:
            if l == first:
                sum_in = _chip_sum(part_in1, got[0], "chip_sum_in1", 1)
            part_out = _pair_sum(dw_out[l], got[-1], side, f"pair_sum_out{l}")
        dqa, dkv, dza, dsk, got = _attn_bwd(proj_a, du, kept_a, cos, sin, f"attn_bwd{l}",
                                            _reduce_ici_phase([part_out]) if dist else None)
        if dist:
            sum_out = _chip_sum(part_out, got[0], f"chip_sum_out{l}", l, None if l == last else sum_out)
        dproj = [p.reshape(T, p.shape[-1]) for p in (dqh, dfh, dih, dzh, dqa, dkv, dza)]
        dw_in[l] = _mm_tn(dproj, h, f"wgrad_in{l}")
        phase = None
        if dist and l == last:
            phase = _reduce_d2d_phase([dw_in[l]], [ra])
        if dist and l == first:
            got = _run_phase(_reduce_d2d_phase([dw_in[l]], [ra]), "reduce_in0_d2d")
            part_in0 = _pair_sum(dw_in[l], got[0], side, "pair_sum_in0")
            phase = _reduce_ici_phase([part_in0])
        dxn, dgpre, got = _inproj_bwd(dproj, w_in[l], x2, dxn, g_pre, l, f"inproj_bwd{l}", phase)
        if dist and l == last:
            part_in1 = _pair_sum(dw_in[l], got[0], side, "pair_sum_in1")
        if dist and l == first:
            sum_in = _chip_sum(part_in0, got[0], "chip_sum_in0", 0, sum_in)
        dg_pre.append(dgpre)
        dg_post.append(dgp)
        dlb.append(dlb_l)
        dg_head.append(dgh)
        dsinks.append(dsk)
    rev = lambda lst: jnp.concatenate(lst[::-1], axis=0)
    if not dist:
        sum_in, sum_out = jnp.stack(dw_in), jnp.stack(dw_out)
    return (loss_part, dxn.reshape(B, S, D), sum_in, sum_out,
            rev(dg_pre), rev(dg_post), rev(dlb), rev(dg_head), rev(dsinks))


def _me_and_peers():
    x, y, c = lax.axis_index("x"), lax.axis_index("y"), lax.axis_index("c")
    me = 4 * x + 2 * y + c
    peers = []
    for k in range(1, N_DEV):
        px = 1 - x if k & 4 else x
        py = 1 - y if k & 2 else y
        pc = 1 - c if k & 1 else c
        peers.append(((px, py, pc), 4 * px + 2 * py + pc))
    return me, peers


class _Phase:
    def __init__(self, arrays, out_shapes, aliases, n_send, build):
        self.arrays, self.out_shapes, self.aliases = list(arrays), list(out_shapes), dict(aliases)
        self.n_send, self.build = n_send, build

    def scratch(self):
        return [pltpu.SemaphoreType.DMA((self.n_send,)), pltpu.SemaphoreType.DMA((self.n_send,))]

    def _copies(self, in_refs, out_refs, sems, arrivals):
        send_sems, recv_sems = sems
        sends, recvs = self.build(in_refs, out_refs)
        assert len(sends) == self.n_send == len(recvs)
        out = [pltpu.make_async_remote_copy(src_ref=s, dst_ref=d, send_sem=send_sems.at[i], recv_sem=recv_sems.at[i],
                                            device_id=dev, device_id_type=MESH) for i, (s, d, dev) in enumerate(sends)]
        inc = [pltpu.make_async_remote_copy(src_ref=s, dst_ref=r, send_sem=send_sems.at[i], recv_sem=recv_sems.at[i],
                                            device_id=dev, device_id_type=MESH)
               for i, ((s, _, dev), r) in enumerate(zip(sends, recvs))] if arrivals else []
        return out, inc

    def start(self, in_refs, out_refs, sems):
        out, _ = self._copies(in_refs, out_refs, sems, False)
        for cp in out:
            cp.start()

    def finish(self, in_refs, out_refs, sems):
        out, inc = self._copies(in_refs, out_refs, sems, True)
        for cp in inc:
            cp.wait_recv()
        for cp in out:
            cp.wait_send()


_ANY = pl.BlockSpec(memory_space=pl.ANY)


def _host_phase(phase, n_in, n_out):
    if phase is None:
        return [], [], [], {}, [], []
    aliases = {n_in + i: n_out + o for i, o in phase.aliases.items()}
    return ([_ANY] * len(phase.arrays), [_ANY] * len(phase.out_shapes), phase.out_shapes, aliases, phase.scratch(),
            phase.arrays)


def _split_refs(refs, n_in, n_out, n_scr, phase):
    pi = len(phase.arrays) if phase else 0
    po = len(phase.out_shapes) if phase else 0
    a = n_in + pi
    b = a + n_out + po
    return (refs[:n_in], refs[a:a + n_out], refs[b:b + n_scr], refs[n_in:a], refs[a + n_out:b], refs[b + n_scr:])


def _hosted_start(phase, p_in, p_out, p_sems, first):
    if phase is not None:
        @pl.when(first)
        def _():
            phase.start(p_in, p_out, p_sems)


def _hosted_finish(phase, p_in, p_out, p_sems, last):
    if phase is not None:
        @pl.when(last)
        def _():
            phase.finish(p_in, p_out, p_sems)


def _run_phase(phase, name):
    n_in, n_out = len(phase.arrays), len(phase.out_shapes)

    def body(*refs):
        phase.start(refs[:n_in], refs[n_in:n_in + n_out], refs[n_in + n_out:])
        phase.finish(refs[:n_in], refs[n_in:n_in + n_out], refs[n_in + n_out:])

    return pl.pallas_call(
        body, name=name, in_specs=[_ANY] * n_in, out_specs=[_ANY] * n_out,
        out_shape=phase.out_shapes, input_output_aliases=phase.aliases, scratch_shapes=phase.scratch(),
        compiler_params=pltpu.CompilerParams(has_side_effects=True),
    )(*phase.arrays)


def _gather_one_call(full, name):
    r = full.shape[0] // N_DEV
    half = r // 2

    def body(full_in, full_ref, send_sems, recv_sems):
        del full_in
        c, (own, xn, yn, dg), num = _mesh_place()
        me, sib = num(own, c), (*own, 1 - c)

        def blk(dev, part=None):
            start, n = (dev * r, r) if part is None else (dev * r + part * half, half)
            return full_ref.at[pl.ds(pl.multiple_of(start, 16), n), :]

        def copy(k, src, dev, to, part=None):
            return pltpu.make_async_remote_copy(src_ref=src, dst_ref=blk(dev, part),
                                                send_sem=send_sems.at[k], recv_sem=recv_sems.at[k],
                                                device_id=to, device_id_type=MESH)

        def landed(k, dev, part=None):
            copy(k, blk(dev, part), dev, sib, part).wait_recv()

        sent = []

        def start(*cps):
            for cp in cps:
                cp.start()
                sent.append(cp)

        xs, ys, ds = num(xn, c), num(yn, c), num(dg, c)
        start(copy(0, blk(me), me, sib), copy(1, blk(me), me, (*xn, c)), copy(2, blk(me), me, (*yn, c)))
        landed(1, xs)
        start(copy(3, blk(xs, 0), xs, (*yn, c), 0), copy(5, blk(xs), xs, sib))
        landed(2, ys)
        start(copy(4, blk(ys, 1), ys, (*xn, c), 1), copy(6, blk(ys), ys, sib))
        landed(3, ds, 0)
        landed(4, ds, 1)
        start(copy(7, blk(ds), ds, sib))
        landed(0, num(own, 1 - c))
        for k, ch in ((5, xn), (6, yn), (7, dg)):
            landed(k, num(ch, 1 - c))
        for cp in sent:
            cp.wait_send()

    assert half % 16 == 0
    return pl.pallas_call(
        body, name=name, in_specs=[_ANY], out_specs=_ANY,
        out_shape=jax.ShapeDtypeStruct(full.shape, full.dtype), input_output_aliases={0: 0},
        scratch_shapes=[pltpu.SemaphoreType.DMA((8,)), pltpu.SemaphoreType.DMA((8,))],
        compiler_params=pltpu.CompilerParams(has_side_effects=True),
    )(full)


def _merge_phases(a, b):
    n_in, n_out = len(a.arrays), len(a.out_shapes)
    aliases = dict(a.aliases)
    aliases.update({n_in + i: n_out + o for i, o in b.aliases.items()})

    def build(ins, outs):
        sa, ra = a.build(ins[:n_in], outs[:n_out])
        sb, rb = b.build(ins[n_in:], outs[n_out:])
        return sa + sb, ra + rb

    return _Phase(a.arrays + b.arrays, a.out_shapes + b.out_shapes, aliases, a.n_send + b.n_send, build)


def _mesh_place():
    x, y, c = lax.axis_index("x"), lax.axis_index("y"), lax.axis_index("c")
    chips = [(x, y), (1 - x, y), (x, 1 - y), (1 - x, 1 - y)]
    num = lambda chip, core: 4 * chip[0] + 2 * chip[1] + core
    return c, chips, num


def _own_side_blocks():
    c, chips, num = _mesh_place()
    return jnp.stack([num(ch, c) for ch in chips]).astype(jnp.int32)


def _rows(ref, r, dev):
    return ref.at[pl.ds(pl.multiple_of(dev * r, 16), r), :]


def _place_own(shards, blocks, name):
    n = len(shards)

    def body(idx_ref, *refs):
        del idx_ref
        outs = iter(refs[n:])
        for s_ref in refs[:n]:
            for l in range(DEPTH):
                next(outs)[...] = s_ref[l].astype(BF16)

    whole = lambda s: pl.BlockSpec(s.shape, lambda i, idx: (0, 0, 0))
    own = lambda s: pl.BlockSpec(s.shape[1:], lambda i, idx: (idx[0], 0))
    res = pl.pallas_call(
        body, name=name,
        grid_spec=pltpu.PrefetchScalarGridSpec(
            num_scalar_prefetch=1, grid=(1,),
            in_specs=[whole(s) for s in shards],
            out_specs=[own(s) for s in shards for _ in range(DEPTH)]),
        out_shape=[jax.ShapeDtypeStruct((N_DEV * s.shape[1], s.shape[2]), BF16) for s in shards for _ in range(DEPTH)],
        compiler_params=_params(("arbitrary",)),
    )(blocks, *shards)
    return [list(res[i * DEPTH:(i + 1) * DEPTH]) for i in range(n)]


def _gather_ici_phase(fulls):
    rs = [a.shape[0] // N_DEV for a in fulls]
    n = len(fulls)

    def build(ins, outs):
        del ins
        c, chips, num = _mesh_place()
        me = num(chips[0], c)
        targets = [((*chips[0], 1 - c), num(chips[0], 1 - c))] + [((*ch, c), num(ch, c)) for ch in chips[1:]]
        sends, recvs = [], []
        for dev, dnum in targets:
            for i, r in enumerate(rs):
                sends.append((_rows(outs[i], r, me), _rows(outs[i], r, me), dev))
                recvs.append(_rows(outs[i], r, dnum))
        return sends, recvs

    shapes = [jax.ShapeDtypeStruct(a.shape, a.dtype) for a in fulls]
    return _Phase(list(fulls), shapes, {i: i for i in range(n)}, 4 * n, build)


def _gather_d2d_phase(fulls, rs):
    def build(ins, outs):
        c, chips, num = _mesh_place()
        sib = (*chips[0], 1 - c)
        sends, recvs = [], []
        for ch in chips[1:]:
            for i, r in enumerate(rs):
                blk = _rows(outs[i], r, num(ch, c))
                sends.append((blk, blk, sib))
                recvs.append(_rows(outs[i], r, num(ch, 1 - c)))
        return sends, recvs

    shapes = [jax.ShapeDtypeStruct(a.shape, a.dtype) for a in fulls]
    return _Phase(fulls, shapes, {i: i for i in range(len(fulls))}, 3 * len(fulls), build)


def _reduce_d2d_phase(grads, rs):
    def build(ins, outs):
        c, chips, num = _mesh_place()
        sib = (*chips[0], 1 - c)
        sends, recvs = [], []
        for j, ch in enumerate(chips):
            for i, r in enumerate(rs):
                sends.append((_rows(ins[i], r, num(ch, 1 - c)), outs[i].at[j], sib))
                recvs.append(outs[i].at[j])
        return sends, recvs

    shapes = [jax.ShapeDtypeStruct((4, r, g.shape[1]), g.dtype) for g, r in zip(grads, rs)]
    return _Phase(grads, shapes, {}, 4 * len(grads), build)


def _reduce_ici_phase(parts):
    def build(ins, outs):
        c, chips, _ = _mesh_place()
        sends, recvs = [], []
        for t in range(1, 4):
            for i in range(len(parts)):
                sends.append((ins[i].at[t], outs[i].at[t - 1], (*chips[t], c)))
                recvs.append(outs[i].at[t - 1])
        return sends, recvs

    shapes = [jax.ShapeDtypeStruct((3,) + p.shape[1:], p.dtype) for p in parts]
    return _Phase(parts, shapes, {}, 3 * len(parts), build)


def _pair_sum(g, got, blocks, name):
    n, r, D = got.shape
    tr = _pick(r, (800, 400, 256, 200, 128, 64, 16))

    def body(idx_ref, g_ref, r_ref, o_ref):
        del idx_ref
        o_ref[...] = (g_ref[...].astype(F32) + r_ref[...].astype(F32)).astype(o_ref.dtype)

    blk = pl.BlockSpec((None, tr, D), lambda j, i, idx: (j, i, 0))
    return pl.pallas_call(
        body, name=name,
        grid_spec=pltpu.PrefetchScalarGridSpec(
            num_scalar_prefetch=1, grid=(n, r // tr),
            in_specs=[pl.BlockSpec((tr, D), lambda j, i, idx: (idx[j] * (r // tr) + i, 0)), blk],
            out_specs=blk),
        out_shape=jax.ShapeDtypeStruct(got.shape, got.dtype),
        compiler_params=_params(("arbitrary", "arbitrary")),
    )(blocks, g, got)


def _chip_sum(p, r, name, layer, into=None):
    _, R, D = p.shape
    tr = _pick(R, (800, 400, 256, 200, 128, 64, 16))

    def body(p_ref, r_ref, *rest):
        acc = p_ref[...].astype(F32)
        for t in range(3):
            acc = acc + r_ref[t].astype(F32)
        rest[-1][...] = acc

    args = [p, r] + ([] if into is None else [into])
    return pl.pallas_call(
        body, name=name, grid=(R // tr,),
        in_specs=[pl.BlockSpec((None, tr, D), lambda i: (0, i, 0)), pl.BlockSpec((3, tr, D), lambda i: (0, i, 0))]
        + ([] if into is None else [_ANY]),
        out_specs=pl.BlockSpec((None, tr, D), lambda i: (layer, i, 0)),
        out_shape=jax.ShapeDtypeStruct((DEPTH, R, D), F32),
        input_output_aliases={} if into is None else {2: 0},
        compiler_params=_params(("parallel",)))(*args)


def _allreduce_small(vec):
    R, C = vec.shape

    def body(v_ref, o_ref, buf, send_sems, recv_sems):
        me, peers = _me_and_peers()
        buf[me] = v_ref[...]
        sends = []
        for k, (pid, _) in enumerate(peers):
            cp = pltpu.make_async_remote_copy(src_ref=v_ref, dst_ref=buf.at[me], send_sem=send_sems.at[k],
                                              recv_sem=recv_sems.at[k], device_id=pid, device_id_type=MESH)
            cp.start()
            sends.append(cp)
        for k, (pid, pnum) in enumerate(peers):
            pltpu.make_async_remote_copy(src_ref=v_ref, dst_ref=buf.at[pnum], send_sem=send_sems.at[k],
                                         recv_sem=recv_sems.at[k], device_id=pid, device_id_type=MESH).wait_recv()
        for cp in sends:
            cp.wait_send()
        acc = buf[0]
        for d in range(1, N_DEV):
            acc = acc + buf[d]
        o_ref[...] = acc

    vm = pl.BlockSpec(memory_space=pltpu.VMEM)
    return pl.pallas_call(
        body, name="allreduce_small",
        in_specs=[vm], out_specs=vm,
        out_shape=jax.ShapeDtypeStruct((R, C), F32),
        scratch_shapes=[pltpu.VMEM((N_DEV, R, C), F32), pltpu.SemaphoreType.DMA((N_DEV - 1,)),
                        pltpu.SemaphoreType.DMA((N_DEV - 1,))],
        compiler_params=pltpu.CompilerParams(has_side_effects=True),
    )(vec)


def _adamw_update(w, g, m, v):
    c1 = 1.0 - ADAM_B1 ** ADAM_STEP
    c2 = 1.0 - ADAM_B2 ** ADAM_STEP
    mn = ADAM_B1 * m + (1.0 - ADAM_B1) * g
    vn = ADAM_B2 * v + (1.0 - ADAM_B2) * (g * g)
    return -ADAM_LR * ((mn / c1) / (jnp.sqrt(vn / c2) + ADAM_EPS) + ADAM_WD * w), mn, vn


def _adamw(w, g, m, v, name):
    R, C = w.shape
    tr = _pick(R, (512, 400, 256, 128, 64, 32, 16, 8))

    def body(w_ref, g_ref, m_ref, v_ref, d_ref, mo_ref, vo_ref):
        d_ref[...], mo_ref[...], vo_ref[...] = _adamw_update(w_ref[...], g_ref[...], m_ref[...], v_ref[...])

    blk = pl.BlockSpec((tr, C), lambda i: (i, 0))
    sh = jax.ShapeDtypeStruct((R, C), F32)
    return pl.pallas_call(
        body, name=name, grid=(R // tr,), in_specs=[blk] * 4, out_specs=[blk] * 3, out_shape=[sh] * 3,
        compiler_params=_params(("parallel",)),
    )(w, g, m, v)


def _adamw_whole(ws, gs, ms, vs, name):
    n = len(ws)

    def body(*refs):
        for j in range(n):
            outs = refs[4 * n + 3 * j:4 * n + 3 * j + 3]
            outs[0][...], outs[1][...], outs[2][...] = _adamw_update(*(r[...] for r in refs[4 * j:4 * j + 4]))

    vm = pl.BlockSpec(memory_space=pltpu.VMEM)
    res = pl.pallas_call(
        body, name=name, in_specs=[vm] * (4 * n), out_specs=[vm] * (3 * n),
        out_shape=[jax.ShapeDtypeStruct(w.shape, F32) for w in ws for _ in range(3)],
    )(*[a for four in zip(ws, gs, ms, vs) for a in four])
    return [tuple(res[3 * j:3 * j + 3]) for j in range(n)]


def _lb_param_grad(lb_param, dlb):
    L, C = lb_param.shape

    def body(p_ref, d_ref, o_ref):
        lbp = p_ref[...]
        d = d_ref[...]
        mx = jnp.max(lbp, axis=0, keepdims=True)
        e = jnp.exp(lbp - mx)
        p = e / jnp.sum(e, axis=0, keepdims=True)
        tot = jnp.sum(d, axis=0, keepdims=True)
        dps = []
        rest = tot
        for j in range(L):
            dps.append(rest - tot if j == 0 else rest)
            rest = rest - d[j:j + 1]
        dp = jnp.concatenate(dps, axis=0)
        o_ref[...] = p * (dp - jnp.sum(p * dp, axis=0, keepdims=True))

    vm = pl.BlockSpec(memory_space=pltpu.VMEM)
    return pl.pallas_call(body, name="lb_param_grad", in_specs=[vm, vm], out_specs=vm,
                          out_shape=jax.ShapeDtypeStruct((L, C), F32))(lb_param, dlb)


def _pack_small(loss_part, dg_pre, dg_post, dlb, dg_head, dsinks):
    pad8 = lambda a: jnp.pad(a.reshape(-1, 128), ((0, 8 - DEPTH), (0, 0)))
    rows = [dg_pre.reshape(-1, 128), dg_post.reshape(-1, 128), dlb.reshape(-1, 128), pad8(dg_head), pad8(dsinks),
            loss_part]
    return jnp.concatenate(rows, axis=0)


def _unpack_small(vec):
    n = DEPTH * D_MODEL // 128
    o = 0
    dg_pre = vec[o:o + n].reshape(DEPTH, D_MODEL); o += n
    dg_post = vec[o:o + n].reshape(DEPTH, D_MODEL); o += n
    dlb = vec[o:o + n].reshape(DEPTH, HG_WIDTH); o += n
    dg_head = vec[o:o + DEPTH]; o += 8
    dsinks = vec[o:o + DEPTH, :ATT_HEADS]; o += 8
    loss = jnp.sum(vec[o:o + 8])
    return loss, dg_pre, dg_post, dlb, dg_head, dsinks


def kernel(x, w_in, w_out, g_pre, g_post, lb_param, g_head, sinks, loss_target, m_w_in, m_w_out, m_g_pre, m_g_post, m_lb_param, m_g_head, m_sinks, v_w_in, v_w_out, v_g_pre, v_g_post, v_lb_param, v_g_head, v_sinks):
    tr = lambda a: jnp.swapaxes(a, 1, 2)
    w_in_t = tr(w_in)
    (loss_part, dx, gw_in_t, gw_out, dg_pre, dg_post, dlb, dg_head, dsinks) = _step(
        x, loss_target, g_pre, g_post, lb_param, g_head, sinks, shards=(w_in_t, w_out))

    small = _allreduce_small(_pack_small(loss_part, dg_pre, dg_post, dlb, dg_head, dsinks))
    loss, gg_pre, gg_post, gdlb, gg_head, gsinks = _unpack_small(small)
    glb = _lb_param_grad(lb_param, gdlb)

    grads = [gw_in_t, gw_out, gg_pre, gg_post, glb, gg_head, gsinks]
    ws = [w_in_t, w_out, g_pre, g_post, lb_param, g_head, sinks]
    ms = [tr(m_w_in), m_w_out, m_g_pre, m_g_post, m_lb_param, m_g_head, m_sinks]
    vs = [tr(v_w_in), v_w_out, v_g_pre, v_g_post, v_lb_param, v_g_head, v_sinks]
    deltas, new_m, new_v = [], [], []
    big = 2
    for w, g, m, v, nm in zip(ws[:big], grads, ms, vs, ("w_in", "w_out")):
        sh = w.shape
        two = lambda a: a.reshape(-1, sh[-1])
        d, mn, vn = _adamw(two(w), two(g), two(m), two(v), "adamw_" + nm)
        deltas.append(d.reshape(sh))
        new_m.append(mn.reshape(sh))
        new_v.append(vn.reshape(sh))
    for d, mn, vn in _adamw_whole(ws[big:], grads[big:], ms[big:], vs[big:], "adamw_vectors"):
        deltas.append(d)
        new_m.append(mn)
        new_v.append(vn)
    grads[0], deltas[0], new_m[0], new_v[0] = tr(grads[0]), tr(deltas[0]), tr(new_m[0]), tr(new_v[0])
    return (loss, dx, *grads, *deltas, *new_m, *new_v)
```

```python
import math

import numpy as np
import jax
import jax.numpy as jnp
from jax import lax
from jax.experimental import pallas as pl
from jax.experimental.pallas import tpu as pltpu

F32 = jnp.float32
BF16 = jnp.bfloat16

D_MODEL = 1024
DEPTH = 2
HG_HEADS = 8
HG_DIM = 128
HG_WIDTH = HG_HEADS * HG_DIM
CHUNK = 64
ATT_HEADS = 16
ATT_DIM = 64
ATT_WIDTH = ATT_HEADS * ATT_DIM
KV_WIDTH = 128
ATT_BLOCK = 128
ATT_SCALE = 1.0 / math.sqrt(ATT_DIM)
ROPE_THETA = 10000.0
NORM_EPS = 1e-6
NEG_INF = -1e30
LB_FLOOR = 1e-20
N_H = 4 * HG_WIDTH
N_A = 2 * ATT_WIDTH + 2 * KV_WIDTH
IN_WIDTH = N_H + N_A
MIX_WIDTH = HG_WIDTH + ATT_WIDTH

ADAM_LR = 0.001
ADAM_B1 = 0.9
ADAM_B2 = 0.999
ADAM_EPS = 1e-08
ADAM_WD = 0.01
ADAM_STEP = 10

N_DEV = 8
MESH = pl.DeviceIdType.MESH
VMEM_LIMIT = 56 * 1024 * 1024

NN = ((1,), (0,))
NT = ((1,), (1,))
TN = ((0,), (0,))


def _dot(a, b, dims):
    return lax.dot_general(a.astype(BF16), b.astype(BF16), (dims, ((), ())), preferred_element_type=F32)


def _params(sem=None, **kw):
    return pltpu.CompilerParams(dimension_semantics=sem, vmem_limit_bytes=VMEM_LIMIT, **kw)


def _sigmoids(x):
    e = jnp.exp(-jnp.abs(x))
    r = 1.0 / (1.0 + e)
    er = e * r
    pos = x >= 0.0
    return jnp.where(pos, r, er), jnp.where(pos, er, r)


def _silu(x):
    return x * _sigmoids(x)[0]


def _silu_and_grad(x):
    s, ns = _sigmoids(x)
    return x * s, s * (1.0 + x * ns)


def _pick(n, prefs):
    for p in prefs:
        if n % p == 0:
            return p
    return n


def _inproj(x2, g, layer, w, name):
    T, D = x2.shape
    tm = _pick(T, (512, 256, 128))
    nchunk = 1024

    def body(x_ref, g_ref, w_ref, oh_ref, oa_ref, h_ref):
        x = x_ref[...]
        r = lax.rsqrt(jnp.mean(x * x, axis=-1, keepdims=True) + NORM_EPS)
        h = ((x * r) * g_ref[layer:layer + 1, :]).astype(BF16)
        h_ref[...] = h
        for j in range(0, N_H, nchunk):
            oh_ref[:, j:j + nchunk] = lax.dot_general(h, w_ref[j:j + nchunk, :], (NT, ((), ())),
                                                      preferred_element_type=F32)
        for j in range(0, N_A, N_A // 2):
            oa_ref[:, j:j + N_A // 2] = lax.dot_general(h, w_ref[N_H + j:N_H + j + N_A // 2, :], (NT, ((), ())),
                                                        preferred_element_type=F32)

    row = lambda w_: pl.BlockSpec((tm, w_), lambda i: (i, 0))
    return pl.pallas_call(
        body, name=name,
        grid=(T // tm,),
        in_specs=[row(D), pl.BlockSpec((DEPTH, D), lambda i: (0, 0)),
                  pl.BlockSpec((IN_WIDTH, D), lambda i: (0, 0), pipeline_mode=pl.Buffered(1))],
        out_specs=[row(N_H), row(N_A), row(D)],
        out_shape=[jax.ShapeDtypeStruct((T, N_H), F32), jax.ShapeDtypeStruct((T, N_A), F32),
                   jax.ShapeDtypeStruct((T, D), BF16)],
        compiler_params=_params(("parallel",)),
    )(x2, g, w)


def _mm_tn(pieces, b, name, out_dtype=BF16):
    T, m = b.shape
    tn = 256
    counts = [p.shape[1] // tn for p in pieces]
    starts = [sum(counts[:i]) for i in range(len(pieces))]
    n_p = len(pieces)

    def body(*refs):
        b_ref, o_ref = refs[n_p], refs[n_p + 1]
        i = pl.program_id(0)
        for p in range(n_p):
            @pl.when((i >= starts[p]) & (i < starts[p] + counts[p]))
            def _(p=p):
                o_ref[...] = lax.dot_general(refs[p][...], b_ref[...], (TN, ((), ())),
                                             preferred_element_type=F32).astype(out_dtype)

    piece_spec = lambda s, c: pl.BlockSpec((T, tn), lambda i: (0, jnp.clip(i - s, 0, c - 1)))
    return pl.pallas_call(
        body, name=name,
        grid=(sum(counts),),
        in_specs=[piece_spec(s, c) for s, c in zip(starts, counts)]
        + [pl.BlockSpec((T, m), lambda i: (0, 0), pipeline_mode=pl.Buffered(1))],
        out_specs=pl.BlockSpec((tn, m), lambda i: (i, 0)),
        out_shape=jax.ShapeDtypeStruct((sum(counts) * tn, m), out_dtype),
        compiler_params=_params(("arbitrary",)),
    )(*pieces, b)


_LEVELS = (0, 1, 2, 4, 8, 16, 32)
_CUM_L = (2, 4, 8, 16, 32, 64)
_ALL_KINDS = tuple(("c", L) for L in _CUM_L) + tuple(("r", L) for L in _CUM_L)
_MXU_KINDS = (("c", 2), ("c", 4), ("c", CHUNK), ("r", 2), ("r", 4))
N_CUM = len(_ALL_KINDS) * CHUNK
N_CUM_F = len(_MXU_KINDS) * CHUNK


def _cum_matrices():
    t = np.arange(CHUNK)[:, None]
    r = np.arange(CHUNK)[None, :]

    def mat(kind):
        c, L = kind
        return ((r // L == t // L) & ((r <= t) if c == "c" else (r > t))).astype(np.float32)

    fwd = np.concatenate([mat(kd) for kd in _MXU_KINDS], axis=0)
    full = np.concatenate([mat(kd) for kd in _ALL_KINDS], axis=0)
    return jnp.asarray(fwd, BF16), jnp.asarray(full.T.copy(), BF16)


def _level_masks():
    t = np.arange(CHUNK)[:, None]
    s = np.arange(CHUNK)[None, :]
    ms = []
    for L in _LEVELS:
        if L == 0:
            ms.append(t == s)
        else:
            ms.append((t // (2 * L) == s // (2 * L)) & ((t // L) % 2 == 1) & ((s // L) % 2 == 0))
    return jnp.asarray(np.stack(ms).astype(np.float32))


def _split3(x):
    hi = x.astype(BF16)
    r1 = x - hi.astype(F32)
    mid = r1.astype(BF16)
    lo = (r1 - mid.astype(F32)).astype(BF16)
    return hi, mid, lo


def _cum3(ts, x, terms=3):
    d = lambda p: lax.dot_general(ts, p, (NN, ((), ())), preferred_element_type=F32)
    return sum(d(p) for p in _split3(x)[:terms])


def _lb_terms(lbp, layer):
    mx = jnp.max(lbp, axis=0, keepdims=True)
    e = jnp.exp(lbp - mx)
    p = e / jnp.sum(e, axis=0, keepdims=True)
    cum = p[0:1]
    for j in range(1, layer + 1):
        cum = cum + p[j:j + 1]
    lb = cum - p[0:1]
    lbf = jnp.maximum(lb, LB_FLOOR)
    return dict(lbf=lbf, one_m=1.0 - lb, kcorr=lb - lbf, ind=jnp.where(lb > LB_FLOOR, 1.0, 0.0))


def _gate(x, lt):
    sig, nsig = _sigmoids(x)
    f = lt["lbf"] + lt["one_m"] * sig
    return jnp.log(f), lt["one_m"] * nsig + lt["kcorr"], f, sig, nsig


def _ck(x, ci):
    return x[ci * CHUNK:(ci + 1) * CHUNK]


def _block_cums(ts, g, nc):
    cs = [_cum3(ts, _ck(g, ci), terms=2) for ci in range(nc)]
    out = {kind: jnp.concatenate([c[CHUNK * i:CHUNK * (i + 1)] for c in cs], axis=0)
           for i, kind in enumerate(_MXU_KINDS)}
    b = out[("c", CHUNK)]
    ng = CHUNK // 8
    last = b.reshape(nc, ng, 8, HG_DIM)[:, :, 7:8, :]
    zero = jnp.zeros((nc, 1, 1, HG_DIM), F32)

    def spread(groups):
        return jnp.broadcast_to(jnp.concatenate(groups, axis=1), (nc, ng, 8, HG_DIM)).reshape(nc * CHUNK, HG_DIM)

    def get(kind):
        if kind in out:
            return out[kind]
        c, L = kind
        nb = L // 8
        first = lambda r: (r // nb) * nb
        if c == "c":
            return b - spread([last[:, first(r) - 1:first(r)] if r >= nb else zero for r in range(ng)])
        return spread([last[:, first(r) + nb - 1:first(r) + nb] for r in range(ng)]) - b

    return get


def _level_factors(cums, g, L):
    if L == 0:
        return None, None
    if L == 1:
        return jnp.exp(g[...]), None
    return jnp.exp(cums(("c", L))), jnp.exp(cums(("r", L)))


def _mul(a, e):
    return a if e is None else a * e


def _hg_block_fwd(qf, k, v, g, ts, m_ref, nc):
    cums = _block_cums(ts, g, nc)
    amat = [jnp.zeros((CHUNK, CHUNK), F32)] * nc
    for li, L in enumerate(_LEVELS):
        eq, ek = _level_factors(cums, g, L)
        ql, kl, m = _mul(qf, eq), _mul(k, ek), m_ref[li]
        amat = [amat[ci] + _dot(_ck(ql, ci), _ck(kl, ci), NT) * m for ci in range(nc)]
    b = cums(("c", CHUNK))
    kst = k * jnp.exp(cums(("r", CHUNK)))
    o = [_dot(amat[ci], _ck(v, ci), NN) for ci in range(nc)]
    kv = [_dot(_ck(v, ci), _ck(kst, ci), TN) for ci in range(nc)]
    dec = [jnp.exp(b[(ci + 1) * CHUNK - 1:(ci + 1) * CHUNK, :]) for ci in range(nc)]
    return o, dec, kv, qf * jnp.exp(b), amat


def _hg_block_bwd(qf, k, v, g, do, amat, ts, m_ref, nc):
    cums = _block_cums(ts, g, nc)
    dcs = {}
    da = [_dot(_ck(do, ci), _ck(v, ci), NT) for ci in range(nc)]
    dq = jnp.zeros(qf.shape, F32)
    dk = jnp.zeros(qf.shape, F32)
    dg = jnp.zeros(qf.shape, F32)
    for li, L in enumerate(_LEVELS):
        eq, ek = _level_factors(cums, g, L)
        qlb, klb, m = _mul(qf[...], eq).astype(BF16), _mul(k[...], ek).astype(BF16), m_ref[li]
        dal = [(da[ci] * m).astype(BF16) for ci in range(nc)]
        both = [(_dot(dal[ci], _ck(klb, ci), NN), _dot(dal[ci], _ck(qlb, ci), TN)) for ci in range(nc)]
        dql = _mul(jnp.concatenate([p[0] for p in both], axis=0), eq)
        dkl = _mul(jnp.concatenate([p[1] for p in both], axis=0), ek)
        dq = dq + dql
        dk = dk + dkl
        if L == 1:
            dg = dg + dql * qf[...]
        elif L > 1:
            dcs[("c", L)] = (dql * qf[...]).astype(BF16)
            dcs[("r", L)] = (dkl * k[...]).astype(BF16)
    b = cums(("c", CHUNK))
    e64 = jnp.exp(b)
    er64 = jnp.exp(cums(("r", CHUNK)))
    qb = (qf[...] * e64).astype(BF16)
    return dict(dq=dq, dk=dk, dg=dg, dcs=dcs, e64=e64, er64=er64, qf=qf, k=k, kst=(k[...] * er64).astype(BF16),
                dv=[_dot(amat[ci], _ck(do, ci), TN) for ci in range(nc)],
                dec=[jnp.exp(b[(ci + 1) * CHUNK - 1:(ci + 1) * CHUNK, :]) for ci in range(nc)],
                qd=[_dot(_ck(do, ci), _ck(qb, ci), TN) for ci in range(nc)])


def _hg_state_bwd(w, v, do, starts, ends, tst, nc):
    dqb = jnp.concatenate([_dot(_ck(do, ci), starts[ci], NN) for ci in range(nc)], axis=0)
    dkst = jnp.concatenate([_dot(_ck(v, ci), ends[ci], NN) for ci in range(nc)], axis=0)
    dqb, dkst = dqb * w["e64"], dkst * w["er64"]
    dq = w["dq"] + dqb
    dk = w["dk"] + dkst
    dv = jnp.concatenate([w["dv"][ci] + _dot(_ck(w["kst"], ci), ends[ci], NT) for ci in range(nc)], axis=0)
    trow = lax.broadcasted_iota(jnp.int32, (CHUNK, 1), 0)
    dtot = jnp.concatenate(
        [jnp.where(trow == CHUNK - 1, jnp.sum(ends[ci] * starts[ci], axis=0, keepdims=True) * w["dec"][ci], 0.0)
         for ci in range(nc)], axis=0)
    dcs = dict(w["dcs"])
    dcs[("c", CHUNK)] = (dqb * w["qf"][...] + dtot).astype(BF16)
    dcs[("r", CHUNK)] = (dkst * w["k"][...]).astype(BF16)
    dgs = [_dot(tst, jnp.concatenate([_ck(dcs[kind], ci) for kind in _ALL_KINDS], axis=0), NN) for ci in range(nc)]
    return dq, dk, dv, w["dg"] + jnp.concatenate(dgs, axis=0)


def _hgrn_fwd(proj_h, u_rows, lb_param, g_head, layer, name, phase=None):
    B, S, _ = proj_h.shape
    sb = _pick(S, (2048, 1024, 512, 256, 128, 64))
    nc = sb // CHUNK
    ts, _ = _cum_matrices()

    def body(*refs):
        ins, outs, (st,), p_in, p_out, p_sems = _split_refs(refs, 8, 12, 1, phase)
        q_ref, f_ref, i_ref, z_ref, lbp_ref, gh_ref, ts_ref, m_ref = ins
        o_ref, u_ref, sts_ref, am_ref = outs[:4]
        logf_ref, k_ref, qf_ref, sg_ref, qg_ref, zg_ref, fg_ref, sig_ref = outs[4:]
        h_id, b_id, s_id = pl.program_id(0), pl.program_id(1), pl.program_id(2)
        _hosted_start(phase, p_in, p_out, p_sems, (h_id == 0) & (b_id == 0) & (s_id == 0))

        @pl.when(s_id == 0)
        def _():
            st[...] = jnp.zeros_like(st)

        lt = _lb_terms(lbp_ref[...], layer)
        tsv = ts_ref[...]
        gh = gh_ref[layer:layer + 1, :]
        logf, k, _, sig, nsig = _gate(f_ref[...], lt)
        qf, qf_grad = _silu_and_grad(q_ref[...])
        sg, sg_grad = _silu_and_grad(z_ref[...])
        logf_ref[...], k_ref[...], qf_ref[...], sg_ref[...] = logf, k, qf, sg
        qg_ref[...] = qf_grad.astype(BF16)
        zg_ref[...] = sg_grad.astype(BF16)
        fg_ref[...] = (lt["one_m"] * sig * nsig).astype(BF16)
        sig_ref[...] = sig.astype(BF16)
        o_part, dec, kv, qb, amat = _hg_block_fwd(qf, k, i_ref[...], logf, tsv, m_ref, nc)
        for ci in range(nc):
            am_ref[ci] = amat[ci].astype(BF16)
        cur = st[...]
        starts = []
        for ci in range(nc):
            sts_ref[ci] = cur
            starts.append(cur)
            cur = cur * dec[ci] + kv[ci]
        st[...] = cur
        o = jnp.concatenate([o_part[ci] + _dot(_ck(qb, ci), starts[ci], NT) for ci in range(nc)], axis=0)
        o_ref[...] = o
        r = lax.rsqrt(jnp.mean(o * o, axis=-1, keepdims=True) + NORM_EPS)
        u_ref[...] = (((o * r) * gh) * sg).astype(BF16)
        _hosted_finish(phase, p_in, p_out, p_sems, (h_id == HG_HEADS - 1) & (b_id == B - 1) & (s_id == S // sb - 1))

    col = lambda base: pl.BlockSpec((None, sb, HG_DIM), lambda h, b, s: (b, s, base + h))
    p_ispecs, p_ospecs, p_oshapes, p_alias, p_scratch, p_args = _host_phase(phase, 8, 12)
    wide = lambda dt: jax.ShapeDtypeStruct((B, S, HG_WIDTH), dt)
    res = pl.pallas_call(
        body, name=name,
        grid=(HG_HEADS, B, S // sb),
        in_specs=[col(0), col(HG_HEADS), col(2 * HG_HEADS), col(3 * HG_HEADS),
                  pl.BlockSpec((DEPTH, HG_DIM), lambda h, b, s: (0, h)),
                  pl.BlockSpec((DEPTH, HG_DIM), lambda h, b, s: (0, 0)),
                  pl.BlockSpec((N_CUM_F, CHUNK), lambda h, b, s: (0, 0)),
                  pl.BlockSpec((len(_LEVELS), CHUNK, CHUNK), lambda h, b, s: (0, 0, 0))] + p_ispecs,
        out_specs=[col(0), col(0),
                   pl.BlockSpec((None, None, nc, HG_DIM, HG_DIM), lambda h, b, s: (b, h, s, 0, 0)),
                   pl.BlockSpec((None, None, nc, CHUNK, CHUNK), lambda h, b, s: (b, h, s, 0, 0))]
        + [col(0)] * 8 + p_ospecs,
        out_shape=[wide(F32),
                   jax.ShapeDtypeStruct((B, S, u_rows), BF16),
                   jax.ShapeDtypeStruct((B, HG_HEADS, S // CHUNK, HG_DIM, HG_DIM), F32),
                   jax.ShapeDtypeStruct((B, HG_HEADS, S // CHUNK, CHUNK, CHUNK), BF16)]
        + [wide(F32)] * 4 + [wide(BF16)] * 4 + p_oshapes,
        input_output_aliases=p_alias,
        scratch_shapes=[pltpu.VMEM((HG_DIM, HG_DIM), F32)] + p_scratch,
        compiler_params=_params(("arbitrary", "arbitrary", "arbitrary")),
    )(proj_h, proj_h, proj_h, proj_h, lb_param, g_head, ts, _level_masks(), *p_args)
    return res[0], res[1], tuple(res[2:12]), list(res[12:])


def _hgrn_bwd(proj_h, o_h, du, kept, lb_param, g_head, layer, name, phase=None):
    B, S, _ = proj_h.shape
    sb = _pick(S, (512, 256, 128, 64))
    nc = sb // CHUNK
    ns = S // sb
    ts, tst = _cum_matrices()

    def body(*refs):
        ins, outs, (dst,), p_in, p_out, p_sems = _split_refs(refs, 18, 6, 1, phase)
        (i_ref, o_ref, du_ref, sts_ref, am_ref, logf_ref, k_ref, qf_ref, sg_ref, qg_ref, zg_ref, fg_ref, sig_ref,
         lbp_ref, gh_ref, ts_ref, tst_ref, m_ref) = ins
        dq_ref, df_ref, di_ref, dz_ref, dlb_ref, dgh_ref = outs
        h_id, b_id, s_id = pl.program_id(0), pl.program_id(1), pl.program_id(2)
        _hosted_start(phase, p_in, p_out, p_sems, (h_id == 0) & (b_id == 0) & (s_id == 0))

        @pl.when(s_id == 0)
        def _():
            dst[...] = jnp.zeros_like(dst)

        @pl.when((b_id == 0) & (s_id == 0))
        def _():
            dlb_ref[...] = jnp.zeros_like(dlb_ref)

        @pl.when((h_id == 0) & (b_id == 0) & (s_id == 0))
        def _():
            dgh_ref[...] = jnp.zeros_like(dgh_ref)

        lt = _lb_terms(lbp_ref[...], layer)
        gh = gh_ref[layer:layer + 1, :]
        tsv = ts_ref[...]
        tstv = tst_ref[...]
        sg = sg_ref[...]
        o = o_ref[...]
        dub = du_ref[...]
        r = lax.rsqrt(jnp.mean(o * o, axis=-1, keepdims=True) + NORM_EPS)
        n = o * r
        dz_ref[...] = (dub * (n * gh) * zg_ref[...].astype(F32)).astype(BF16)
        dgh_ref[...] += jnp.sum(dub * sg * n, axis=0, keepdims=True)
        dn = dub * sg * gh
        do = (r * (dn - n * jnp.mean(dn * n, axis=-1, keepdims=True))).astype(BF16)
        v = i_ref[...].astype(BF16)
        w = _hg_block_bwd(qf_ref, k_ref, v, logf_ref, do, [am_ref[ci] for ci in range(nc)], tsv, m_ref, nc)
        cur = dst[...]
        ends = [None] * nc
        for ci in reversed(range(nc)):
            ends[ci] = cur
            cur = cur * w["dec"][ci] + w["qd"][ci]
        dst[...] = cur
        dq, dk, dv, dg = _hg_state_bwd(w, v, do, [sts_ref[ci] for ci in range(nc)], ends, tstv, nc)
        di_ref[...] = dv.astype(BF16)
        dq_ref[...] = (dq * qg_ref[...].astype(F32)).astype(BF16)
        f = jnp.exp(logf_ref[...])
        scaled = (dg - f * dk) / f
        df_ref[...] = (scaled * fg_ref[...].astype(F32)).astype(BF16)
        dlb_ref[...] += jnp.sum(scaled * (lt["ind"] - sig_ref[...].astype(F32)), axis=0, keepdims=True)
        _hosted_finish(phase, p_in, p_out, p_sems, (h_id == HG_HEADS - 1) & (b_id == B - 1) & (s_id == ns - 1))

    col = lambda base: pl.BlockSpec((None, sb, HG_DIM), lambda h, b, s: (b, ns - 1 - s, base + h))
    out_col = pl.BlockSpec((None, sb, HG_DIM), lambda h, b, s: (b, ns - 1 - s, h))
    dt = jax.ShapeDtypeStruct((B, S, HG_WIDTH), BF16)
    p_ispecs, p_ospecs, p_oshapes, p_alias, p_scratch, p_args = _host_phase(phase, 18, 6)
    res = pl.pallas_call(
        body, name=name,
        grid=(HG_HEADS, B, ns),
        in_specs=[col(2 * HG_HEADS), col(0), col(0),
                  pl.BlockSpec((None, None, nc, HG_DIM, HG_DIM), lambda h, b, s: (b, h, ns - 1 - s, 0, 0)),
                  pl.BlockSpec((None, None, nc, CHUNK, CHUNK), lambda h, b, s: (b, h, ns - 1 - s, 0, 0))]
        + [col(0)] * 8
        + [pl.BlockSpec((DEPTH, HG_DIM), lambda h, b, s: (0, h)),
           pl.BlockSpec((DEPTH, HG_DIM), lambda h, b, s: (0, 0)),
           pl.BlockSpec((N_CUM_F, CHUNK), lambda h, b, s: (0, 0)),
           pl.BlockSpec((CHUNK, N_CUM), lambda h, b, s: (0, 0)),
           pl.BlockSpec((len(_LEVELS), CHUNK, CHUNK), lambda h, b, s: (0, 0, 0))] + p_ispecs,
        out_specs=[out_col, out_col, out_col, out_col,
                   pl.BlockSpec((1, HG_DIM), lambda h, b, s: (0, h)),
                   pl.BlockSpec((1, HG_DIM), lambda h, b, s: (0, 0))] + p_ospecs,
        out_shape=[dt, dt, dt, dt, jax.ShapeDtypeStruct((1, HG_WIDTH), F32),
                   jax.ShapeDtypeStruct((1, HG_DIM), F32)] + p_oshapes,
        input_output_aliases=p_alias,
        scratch_shapes=[pltpu.VMEM((HG_DIM, HG_DIM), F32)] + p_scratch,
        compiler_params=_params(("arbitrary", "arbitrary", "arbitrary")),
    )(proj_h, o_h, du, *kept, lb_param, g_head, ts, tst, _level_masks(), *p_args)
    return tuple(res[:6]) + (list(res[6:]),)


def _rope_tables(S):
    half = ATT_DIM // 2
    inv_freq = np.float32(ROPE_THETA) ** (-np.arange(half, dtype=np.float32) / half)
    ang = np.arange(S, dtype=np.float32)[:, None] * inv_freq[None, :]
    cos = np.cos(ang)
    sin = np.sin(ang)
    cos = np.concatenate([cos, cos, cos, cos], axis=1)
    sin = np.concatenate([-sin, sin, -sin, sin], axis=1)
    return jnp.asarray(cos, F32), jnp.asarray(sin, F32)


def _attn_common():
    lane = lax.broadcasted_iota(jnp.int32, (1, 2 * ATT_DIM), 1)
    first_half = (lane % ATT_DIM) < (ATT_DIM // 2)
    left = lane < ATT_DIM

    def swap(x):
        return jnp.where(first_half, pltpu.roll(x, 128 - ATT_DIM // 2, 1), pltpu.roll(x, ATT_DIM // 2, 1))

    def rope(x, cos, sin):
        return x * cos + swap(x) * sin

    def rope_bwd(dy, cos, sin):
        return dy * cos + swap(dy * sin)

    def dup(x):
        xs = pltpu.roll(x, ATT_DIM, 1)
        return [jnp.where(left, x, xs), jnp.where(left, xs, x)]

    return left, rope, rope_bwd, dup


GROUP = ATT_HEADS // 2
GROUP_ROWS = GROUP * ATT_BLOCK


def _attn_bias(i):
    r = lax.broadcasted_iota(jnp.int32, (ATT_BLOCK, 2 * ATT_BLOCK), 0)
    c = lax.broadcasted_iota(jnp.int32, (ATT_BLOCK, 2 * ATT_BLOCK), 1)
    ok = (c > r) & (c <= r + ATT_BLOCK) & ((c >= ATT_BLOCK) | (i > 0))
    return jnp.where(ok, 0.0, NEG_INF)


def _stack_heads(pairs, left):
    rows = []
    for x in pairs:
        rows += [jnp.where(left, x, 0.0), jnp.where(left, 0.0, x)]
    return jnp.concatenate(rows, axis=0)


def _unstack_heads(y, left, pp):
    r0 = 2 * pp * ATT_BLOCK
    return jnp.where(left, y[r0:r0 + ATT_BLOCK], y[r0 + ATT_BLOCK:r0 + 2 * ATT_BLOCK])


def _row_sums(x):
    return _dot(x, jnp.ones((x.shape[1], 128), BF16), NN)


def _attn_probs(qs, kd, vd, sink, bias):
    n = range(len(qs))
    rows = qs[0].shape[0]
    s = [(_dot(qs[j], kd[j], NT).reshape(rows // ATT_BLOCK, ATT_BLOCK, 2 * ATT_BLOCK) * ATT_SCALE + bias[None])
         .reshape(rows, 2 * ATT_BLOCK) for j in n]
    m = [jnp.max(jnp.maximum(jnp.maximum(s[j][:, :128], s[j][:, 128:]), sink[j]), axis=-1, keepdims=True) for j in n]
    pu = [jnp.exp(s[j] - m[j]) for j in n]
    es = [jnp.exp(sink[j] - m[j]) for j in n]
    ones = jnp.ones((2 * ATT_BLOCK, 128), BF16)
    ov = [_dot(pu[j], jnp.concatenate([vd[j].astype(BF16), ones], axis=1), NN) for j in n]
    inv = [1.0 / (ov[j][:, 128:] + es[j]) for j in n]
    return ([pu[j] * jnp.concatenate([inv[j], inv[j]], axis=1) for j in n], [es[j] * inv[j] for j in n],
            [ov[j][:, :128] * inv[j] for j in n])


def _sink_rows(sinks):
    return jnp.broadcast_to(jnp.repeat(sinks, ATT_BLOCK, axis=1)[:, :, None], (DEPTH, ATT_HEADS * ATT_BLOCK, 128))


_Z0 = (2 * ATT_WIDTH + 2 * KV_WIDTH - ATT_WIDTH) // 256


def _attn_fwd(proj_a, u, sink_rows, layer, cos, sin, name, phase=None):
    B, S, _ = proj_a.shape
    nb = S // ATT_BLOCK

    def body(*refs):
        ins, (u_ref, p_ref, o_ref, ps_ref, qs_ref), _, p_in, p_out, p_sems = _split_refs(refs, 13, 5, 0, phase)
        q_ref, kvc_ref, kvp_ref, z0, z1, z2, z3, cos_ref, sin_ref, cosp_ref, sinp_ref, sinks_ref, _ = ins
        i = pl.program_id(1)
        _hosted_start(phase, p_in, p_out, p_sems, (pl.program_id(0) == 0) & (i == 0))
        left, rope, _, dup = _attn_common()
        cos_c, sin_c = cos_ref[...], sin_ref[...]
        kvc = kvc_ref[...]
        kvp = kvp_ref[...]
        kw = jnp.concatenate([rope(kvp[:, :KV_WIDTH], cosp_ref[...], sinp_ref[...]),
                              rope(kvc[:, :KV_WIDTH], cos_c, sin_c)], axis=0)
        vw = jnp.concatenate([kvp[:, KV_WIDTH:], kvc[:, KV_WIDTH:]], axis=0)
        kd, vd = dup(kw), dup(vw)
        bias = _attn_bias(i)
        zs = (z0, z1, z2, z3)
        pairs = [range(4 * kvh, 4 * kvh + 4) for kvh in range(2)]
        qs = [_stack_heads([rope(q_ref[:, 128 * pr:128 * (pr + 1)], cos_c, sin_c) for pr in pairs[kvh]], left)
              for kvh in range(2)]
        sink = [sinks_ref[kvh * GROUP_ROWS:(kvh + 1) * GROUP_ROWS, :] for kvh in range(2)]
        p, ps, o = _attn_probs(qs, kd, vd, sink, bias)
        eye = (lax.broadcasted_iota(jnp.int32, (ATT_BLOCK, 128), 0)
               == lax.broadcasted_iota(jnp.int32, (ATT_BLOCK, 128), 1))
        for kvh in range(2):
            p_ref[kvh] = p[kvh].astype(BF16)
            qs_ref[kvh] = qs[kvh].astype(BF16)
            for g in range(GROUP):
                blk = ps[kvh][g * ATT_BLOCK:(g + 1) * ATT_BLOCK, :]
                ps_ref[kvh * GROUP + g:kvh * GROUP + g + 1, :] = jnp.sum(jnp.where(eye, blk, 0.0), axis=0, keepdims=True)
            for pp, pr in enumerate(pairs[kvh]):
                z = zs[pr // 2][:, 128 * (pr % 2):128 * (pr % 2 + 1)]
                o128 = _unstack_heads(o[kvh], left, pp)
                o_ref[:, 128 * pr:128 * (pr + 1)] = o128.astype(BF16)
                u_ref[:, 128 * pr:128 * (pr + 1)] = (o128 * _silu(z)).astype(BF16)
        _hosted_finish(phase, p_in, p_out, p_sems, (pl.program_id(0) == B - 1) & (i == nb - 1))

    rowblk = lambda w, cb: pl.BlockSpec((None, ATT_BLOCK, w), lambda b, i: (b, i, cb))
    tab = pl.BlockSpec((ATT_BLOCK, 128), lambda b, i: (i, 0))
    tabp = pl.BlockSpec((ATT_BLOCK, 128), lambda b, i: (jnp.maximum(i - 1, 0), 0))
    p_ispecs, p_ospecs, p_oshapes, p_alias, p_scratch, p_args = _host_phase(phase, 13, 5)
    res = pl.pallas_call(
        body, name=name,
        grid=(B, nb),
        in_specs=[rowblk(ATT_WIDTH, 0), rowblk(256, 4),
                  pl.BlockSpec((None, ATT_BLOCK, 256), lambda b, i: (b, jnp.maximum(i - 1, 0), 4)),
                  rowblk(256, _Z0), rowblk(256, _Z0 + 1), rowblk(256, _Z0 + 2), rowblk(256, _Z0 + 3),
                  tab, tab, tabp, tabp,
                  pl.BlockSpec((None, ATT_HEADS * ATT_BLOCK, 128), lambda b, i: (layer, 0, 0)),
                  pl.BlockSpec(memory_space=pl.ANY)] + p_ispecs,
        out_specs=[pl.BlockSpec((None, ATT_BLOCK, ATT_WIDTH), lambda b, i: (b, i, 1)),
                   pl.BlockSpec((None, None, 2, GROUP_ROWS, 2 * ATT_BLOCK), lambda b, i: (b, i, 0, 0, 0)),
                   pl.BlockSpec((None, ATT_BLOCK, ATT_WIDTH), lambda b, i: (b, i, 0)),
                   pl.BlockSpec((None, None, ATT_HEADS, 128), lambda b, i: (b, i, 0, 0)),
                   pl.BlockSpec((None, None, 2, GROUP_ROWS, 128), lambda b, i: (b, i, 0, 0, 0))] + p_ospecs,
        out_shape=[jax.ShapeDtypeStruct(u.shape, BF16),
                   jax.ShapeDtypeStruct((B, nb, 2, GROUP_ROWS, 2 * ATT_BLOCK), BF16),
                   jax.ShapeDtypeStruct((B, S, ATT_WIDTH), BF16),
                   jax.ShapeDtypeStruct((B, nb, ATT_HEADS, 128), F32),
                   jax.ShapeDtypeStruct((B, nb, 2, GROUP_ROWS, 128), BF16)] + p_oshapes,
        input_output_aliases={12: 0, **p_alias},
        scratch_shapes=p_scratch,
        compiler_params=_params(("arbitrary", "arbitrary")),
    )(proj_a, proj_a, proj_a, proj_a, proj_a, proj_a, proj_a, cos, sin, cos, sin, sink_rows, u, *p_args)
    return res[0], tuple(res[1:5]), list(res[5:])


def _attn_bwd(proj_a, du, kept, cos, sin, name, phase=None):
    B, S, _ = proj_a.shape
    nb = S // ATT_BLOCK
    p_kept, o_kept, ps_kept, qs_kept = kept

    def body(*refs):
        ins, outs, (carry, sk_acc), p_in, p_out, p_sems = _split_refs(refs, 15, 4, 2, phase)
        (qs_ref, kvc_ref, kvp_ref, z0, z1, z2, z3, du_ref, cos_ref, sin_ref, cosp_ref, sinp_ref,
         p_ref, o_ref, ps_ref) = ins
        dq_ref, dkv_ref, dz_ref, dsk_ref = outs
        b_id, i = pl.program_id(0), pl.program_id(1)
        _hosted_start(phase, p_in, p_out, p_sems, (b_id == 0) & (i == 0))

        @pl.when((b_id == 0) & (i == 0))
        def _():
            sk_acc[...] = jnp.zeros_like(sk_acc)

        @pl.when(i == 0)
        def _():
            carry[...] = jnp.zeros_like(carry)

        @pl.when(i < nb)
        def _():
            left, rope, rope_bwd, dup = _attn_common()
            cos_c, sin_c = cos_ref[...], sin_ref[...]
            cos_p, sin_p = cosp_ref[...], sinp_ref[...]
            kvc = kvc_ref[...]
            kvp = kvp_ref[...]
            kw = jnp.concatenate([rope(kvp[:, :KV_WIDTH], cos_p, sin_p), rope(kvc[:, :KV_WIDTH], cos_c, sin_c)], axis=0)
            vw = jnp.concatenate([kvp[:, KV_WIDTH:], kvc[:, KV_WIDTH:]], axis=0)
            kd, vd = dup(kw), dup(vw)
            zs = (z0, z1, z2, z3)
            units = [(kvh, hf) for kvh in range(2) for hf in range(2)]
            half = GROUP_ROWS // 2
            pairs = [range(4 * kvh + 2 * hf, 4 * kvh + 2 * hf + 2) for kvh, hf in units]
            ku = [kd[kvh] for kvh, _ in units]
            vu = [vd[kvh] for kvh, _ in units]
            ps_all = ps_ref[...]
            head_row = lax.broadcasted_iota(jnp.int32, (ATT_HEADS, 128), 0)
            eye = (lax.broadcasted_iota(jnp.int32, (ATT_BLOCK, 128), 0)
                   == lax.broadcasted_iota(jnp.int32, (ATT_BLOCK, 128), 1))

            def first(j):
                kvh, hf = units[j]
                p = p_ref[kvh, hf * half:(hf + 1) * half, :]
                parts = []
                for pr in pairs[j]:
                    cols = slice(128 * pr, 128 * (pr + 1))
                    sg, sg_grad = _silu_and_grad(zs[pr // 2][:, 128 * (pr % 2):128 * (pr % 2 + 1)])
                    du128 = du_ref[:, cols]
                    dz_ref[:, cols] = (du128 * o_ref[:, cols].astype(F32) * sg_grad).astype(BF16)
                    parts.append(du128 * sg)
                dos = _stack_heads(parts, left)
                dp = _dot(dos, vu[j], NT)
                delta = _row_sums(p.astype(F32) * dp)
                ds = (p.astype(F32) * (dp - jnp.concatenate([delta, delta], axis=1)) * ATT_SCALE).astype(BF16)
                sk = jnp.zeros((ATT_HEADS, 128), F32)
                for hh in range(4):
                    hd = kvh * GROUP + 4 * hf + hh
                    drow = jnp.sum(jnp.where(eye, delta[hh * ATT_BLOCK:(hh + 1) * ATT_BLOCK, :], 0.0), axis=0,
                                   keepdims=True)
                    sk = sk - jnp.where(head_row == hd, ps_all * drow, 0.0)
                sk_acc[...] += sk
                return ds, p, dos.astype(BF16), qs_ref[kvh, hf * half:(hf + 1) * half, :]

            def second(j, ds, p, dos, qs):
                dqs = _dot(ds, ku[j], NN)
                for pp, pr in enumerate(pairs[j]):
                    dq_ref[:, 128 * pr:128 * (pr + 1)] = rope_bwd(_unstack_heads(dqs, left, pp),
                                                                  cos_c, sin_c).astype(BF16)
                return _dot(ds, qs, TN), _dot(p, dos, TN)

            got, dku, dvu = {}, [None] * len(units), [None] * len(units)
            for j in range(len(units) + 1):
                if j < len(units):
                    got[j] = first(j)
                if j >= 1:
                    dku[j - 1], dvu[j - 1] = second(j - 1, *got.pop(j - 1))
            dkd = [dku[0] + dku[1], dku[2] + dku[3]]
            dvd = [dvu[0] + dvu[1], dvu[2] + dvu[3]]
            fold = lambda pr: jnp.where(left, pr[0] + pltpu.roll(pr[0], ATT_DIM, 1), pr[1] + pltpu.roll(pr[1], ATT_DIM, 1))
            dkw = fold(dkd)
            dvw = fold(dvd)
            prev = jnp.concatenate([rope_bwd(dkw[:ATT_BLOCK], cos_p, sin_p), dvw[:ATT_BLOCK]], axis=1)
            cur = jnp.concatenate([rope_bwd(dkw[ATT_BLOCK:], cos_c, sin_c), dvw[ATT_BLOCK:]], axis=1)
            dkv_ref[...] = (carry[...] + prev).astype(BF16)
            carry[...] = cur

        @pl.when(i == nb)
        def _():
            dkv_ref[...] = carry[...].astype(BF16)

        @pl.when((b_id == B - 1) & (i == nb))
        def _():
            diag = (lax.broadcasted_iota(jnp.int32, (ATT_HEADS, 128), 0)
                    == lax.broadcasted_iota(jnp.int32, (ATT_HEADS, 128), 1))
            tot = jnp.sum(sk_acc[...], axis=1, keepdims=True)
            dsk_ref[...] = jnp.sum(jnp.where(diag, tot, 0.0), axis=0, keepdims=True)

        _hosted_finish(phase, p_in, p_out, p_sems, (b_id == B - 1) & (i == nb))

    cl = lambda i: jnp.minimum(i, nb - 1)
    pv = lambda i: jnp.maximum(jnp.minimum(i, nb - 1) - 1, 0)
    rowblk = lambda w, cb: pl.BlockSpec((None, ATT_BLOCK, w), lambda b, i: (b, cl(i), cb))
    tab = pl.BlockSpec((ATT_BLOCK, 128), lambda b, i: (cl(i), 0))
    tabp = pl.BlockSpec((ATT_BLOCK, 128), lambda b, i: (pv(i), 0))
    p_ispecs, p_ospecs, p_oshapes, p_alias, p_scratch, p_args = _host_phase(phase, 15, 4)
    res = pl.pallas_call(
        body, name=name,
        grid=(B, nb + 1),
        in_specs=[pl.BlockSpec((None, None, 2, GROUP_ROWS, 128), lambda b, i: (b, cl(i), 0, 0, 0)), rowblk(256, 4),
                  pl.BlockSpec((None, ATT_BLOCK, 256), lambda b, i: (b, pv(i), 4)),
                  rowblk(256, _Z0), rowblk(256, _Z0 + 1), rowblk(256, _Z0 + 2), rowblk(256, _Z0 + 3),
                  rowblk(ATT_WIDTH, 1),
                  tab, tab, tabp, tabp,
                  pl.BlockSpec((None, None, 2, GROUP_ROWS, 2 * ATT_BLOCK), lambda b, i: (b, cl(i), 0, 0, 0)),
                  rowblk(ATT_WIDTH, 0),
                  pl.BlockSpec((None, None, ATT_HEADS, 128), lambda b, i: (b, cl(i), 0, 0))] + p_ispecs,
        out_specs=[rowblk(ATT_WIDTH, 0),
                   pl.BlockSpec((None, ATT_BLOCK, 256), lambda b, i: (b, jnp.maximum(i - 1, 0), 0)),
                   rowblk(ATT_WIDTH, 0),
                   pl.BlockSpec((1, 128), lambda b, i: (0, 0))] + p_ospecs,
        out_shape=[jax.ShapeDtypeStruct((B, S, ATT_WIDTH), BF16), jax.ShapeDtypeStruct((B, S, 256), BF16),
                   jax.ShapeDtypeStruct((B, S, ATT_WIDTH), BF16), jax.ShapeDtypeStruct((1, 128), F32)] + p_oshapes,
        input_output_aliases=p_alias,
        scratch_shapes=[pltpu.VMEM((ATT_BLOCK, 256), F32), pltpu.VMEM((ATT_HEADS, 128), F32)] + p_scratch,
        compiler_params=_params(("arbitrary", "arbitrary")),
    )(qs_kept, proj_a, proj_a, proj_a, proj_a, proj_a, proj_a, du, cos, sin, cos, sin, p_kept, o_kept, ps_kept, *p_args)
    return tuple(res[:4]) + (list(res[4:]),)


def _outproj_fwd(u2, w_out, x2, g_post, layer, target2, name):
    T, D = x2.shape
    tm = _pick(T, (512, 256, 128))
    last = target2 is not None

    def body(u_ref, w_ref, x_ref, g_ref, *rest):
        y = lax.dot_general(u_ref[...], w_ref[...], (NN, ((), ())), preferred_element_type=F32)
        r = lax.rsqrt(jnp.mean(y * y, axis=-1, keepdims=True) + NORM_EPS)
        xn = x_ref[...] + (y * r) * g_ref[layer:layer + 1, :]
        if last:
            t_ref, y_ref, dx_ref, loss_ref = rest
            err = xn - t_ref[...]
            dx_ref[...] = err * (1.0 / D)
            sq = err * err
            acc = sq[:, 0:128]
            for kk in range(1, D // 128):
                acc = acc + sq[:, 128 * kk:128 * (kk + 1)]
            part = jnp.sum(acc.reshape(tm // 8, 8, 128), axis=0) * (0.5 / D)

            @pl.when(pl.program_id(0) == 0)
            def _():
                loss_ref[...] = jnp.zeros_like(loss_ref)

            loss_ref[...] += part
        else:
            y_ref, xn_ref = rest
            xn_ref[...] = xn
        y_ref[...] = y

    row = pl.BlockSpec((tm, D), lambda i: (i, 0))
    in_specs = [pl.BlockSpec((tm, MIX_WIDTH), lambda i: (i, 0)),
                pl.BlockSpec((MIX_WIDTH, D), lambda i: (0, 0)), row,
                pl.BlockSpec((DEPTH, D), lambda i: (0, 0))]
    args = [u2, w_out, x2, g_post]
    out_specs = [row, row]
    out_shape = [jax.ShapeDtypeStruct((T, D), F32), jax.ShapeDtypeStruct((T, D), F32)]
    if last:
        in_specs.append(row)
        args.append(target2)
        out_specs.append(pl.BlockSpec((8, 128), lambda i: (0, 0)))
        out_shape.append(jax.ShapeDtypeStruct((8, 128), F32))
    return pl.pallas_call(
        body, name=name, grid=(T // tm,), in_specs=in_specs, out_specs=out_specs, out_shape=out_shape,
        compiler_params=_params(("arbitrary",)),
    )(*args)


def _outproj_bwd(dxn2, y2, g_post, layer, w_out, u2, name):
    T, D = y2.shape
    N = w_out.shape[0]
    tm = _pick(T, (512, 256, 128))
    nt = T // tm

    def body(dx_ref, y_ref, g_ref, w_ref, u_ref, dg_ref, du_ref, dw_ref, acc, wacc):
        i = pl.program_id(0)

        @pl.when(i == 0)
        def _():
            acc[...] = jnp.zeros_like(acc)
            wacc[...] = jnp.zeros_like(wacc)

        y = y_ref[...]
        dxn = dx_ref[...]
        r = lax.rsqrt(jnp.mean(y * y, axis=-1, keepdims=True) + NORM_EPS)
        n = y * r
        dn = dxn * g_ref[layer:layer + 1, :]
        dy = (r * (dn - n * jnp.mean(dn * n, axis=-1, keepdims=True))).astype(BF16)
        du_ref[...] = lax.dot_general(dy, w_ref[...], (NT, ((), ())), preferred_element_type=F32)
        wacc[...] += lax.dot_general(u_ref[...], dy, (TN, ((), ())), preferred_element_type=F32)
        acc[...] += jnp.sum((dxn * n).reshape(tm // 8, 8, D), axis=0)

        @pl.when(i == nt - 1)
        def _():
            dg_ref[...] = jnp.sum(acc[...], axis=0, keepdims=True)
            dw_ref[...] = wacc[...].astype(BF16)

    row = pl.BlockSpec((tm, D), lambda i: (i, 0))
    wide = pl.BlockSpec((tm, N), lambda i: (i, 0))
    vec = pl.BlockSpec((1, D), lambda i: (0, 0))
    whole = pl.BlockSpec((N, D), lambda i: (0, 0))
    return pl.pallas_call(
        body, name=name, grid=(nt,),
        in_specs=[row, row, pl.BlockSpec((DEPTH, D), lambda i: (0, 0)),
                  pl.BlockSpec((N, D), lambda i: (0, 0), pipeline_mode=pl.Buffered(1)), wide],
        out_specs=[vec, wide, whole],
        out_shape=[jax.ShapeDtypeStruct((1, D), F32), jax.ShapeDtypeStruct((T, N), F32),
                   jax.ShapeDtypeStruct((N, D), BF16)],
        scratch_shapes=[pltpu.VMEM((8, D), F32), pltpu.VMEM((N, D), F32)],
        compiler_params=_params(("arbitrary",)),
    )(dxn2, y2, g_post, w_out, u2)


def _inproj_bwd(pieces, w_t, x2, dxn2, g_pre, layer, name, phase=None):
    T, D = x2.shape
    widths = [p.shape[1] for p in pieces]
    offs = [sum(widths[:i]) for i in range(len(pieces))]
    n_p = len(pieces)
    tm = _pick(T, (256, 128))
    nt = T // tm

    def body(*refs):
        ins, (dx_ref, dg_ref), (acc,), p_in, p_out, p_sems = _split_refs(refs, n_p + 4, 2, 1, phase)
        w_ref, x_ref, dxn_ref, g_ref = ins[n_p:]
        i = pl.program_id(0)
        _hosted_start(phase, p_in, p_out, p_sems, i == 0)

        @pl.when(i == 0)
        def _():
            acc[...] = jnp.zeros_like(acc)

        dh = jnp.zeros((tm, D), F32)
        for p in range(n_p):
            dh = dh + lax.dot_general(ins[p][...], w_ref[offs[p]:offs[p] + widths[p], :], (NN, ((), ())),
                                      preferred_element_type=F32)
        x = x_ref[...]
        r = lax.rsqrt(jnp.mean(x * x, axis=-1, keepdims=True) + NORM_EPS)
        n = x * r
        dn = dh * g_ref[layer:layer + 1, :]
        dx_ref[...] = dxn_ref[...] + r * (dn - n * jnp.mean(dn * n, axis=-1, keepdims=True))
        acc[...] += jnp.sum((dh * n).reshape(tm // 8, 8, D), axis=0)

        @pl.when(i == nt - 1)
        def _():
            dg_ref[...] = jnp.sum(acc[...], axis=0, keepdims=True)

        _hosted_finish(phase, p_in, p_out, p_sems, i == nt - 1)

    row = pl.BlockSpec((tm, D), lambda i: (i, 0))
    vec = pl.BlockSpec((1, D), lambda i: (0, 0))
    p_ispecs, p_ospecs, p_oshapes, p_alias, p_scratch, p_args = _host_phase(phase, n_p + 4, 2)
    res = pl.pallas_call(
        body, name=name, grid=(nt,),
        in_specs=[pl.BlockSpec((tm, w), lambda i: (i, 0)) for w in widths]
        + [pl.BlockSpec((sum(widths), D), lambda i: (0, 0), pipeline_mode=pl.Buffered(1)), row, row,
           pl.BlockSpec((DEPTH, D), lambda i: (0, 0))] + p_ispecs,
        out_specs=[row, vec] + p_ospecs,
        out_shape=[jax.ShapeDtypeStruct((T, D), F32), jax.ShapeDtypeStruct((1, D), F32)] + p_oshapes,
        input_output_aliases=p_alias,
        scratch_shapes=[pltpu.VMEM((8, D), F32)] + p_scratch,
        compiler_params=_params(("arbitrary",)),
    )(*pieces, w_t, x2, dxn2, g_pre, *p_args)
    return res[0], res[1], list(res[2:])


def _step(x, target, g_pre, g_post, lb_param, g_head, sinks, shards=None, full=None):
    B, S, D = x.shape
    T = B * S
    dist = shards is not None
    first, last = 0, DEPTH - 1
    if dist:
        a_loc, b_loc = shards
        ra, rb = a_loc.shape[1], b_loc.shape[1]
        side = _own_side_blocks()
        a_full, b_full = _place_own([a_loc, b_loc], side, "place_own")
        w_in0 = _gather_one_call(a_full[0], "gather_in0")
        w_in, w_out = [w_in0, None], [None, None]
    else:
        w_in, w_out = list(full[0]), list(full[1])
    cos, sin = _rope_tables(S)
    sink_rows = _sink_rows(sinks)
    saved = []
    xs = x
    loss_part = None
    dxn = None
    for l in range(DEPTH):
        x2 = xs.reshape(T, D)
        proj_h, proj_a, h = _inproj(x2, g_pre, l, w_in[l], f"inproj{l}")
        proj_h = proj_h.reshape(B, S, N_H)
        proj_a = proj_a.reshape(B, S, N_A)
        phase = None
        if dist and l == first:
            phase = _gather_ici_phase([a_full[1], b_full[0]])
        if dist and l == last:
            phase = _gather_d2d_phase([w_out1_part], [rb])
        o_h, u, states, got = _hgrn_fwd(proj_h, MIX_WIDTH, lb_param, g_head, l, f"hgrn_fwd{l}", phase)
        phase = None
        if dist and l == first:
            phase = _merge_phases(_gather_d2d_phase(got, [ra, rb]),
                                  _gather_ici_phase([b_full[1]]))
        if dist and l == last:
            w_out[1] = got[0]
        u, kept_a, got = _attn_fwd(proj_a, u, sink_rows, l, cos, sin, f"attn_fwd{l}", phase)
        if dist and l == first:
            w_in[1], w_out[0], w_out1_part = got
        u2 = u.reshape(T, MIX_WIDTH)
        if l < last:
            y, xn = _outproj_fwd(u2, w_out[l], x2, g_post, l, None, f"outproj{l}")
            xn = xn.reshape(B, S, D)
        else:
            y, dxn, loss_part = _outproj_fwd(u2, w_out[l], x2, g_post, l, target.reshape(T, D), f"outproj{l}")
            xn = None
        saved.append((x2, h, proj_h, proj_a, o_h, u2, states, kept_a, y))
        xs = xn

    dw_in, dw_out = [None] * DEPTH, [None] * DEPTH
    dg_pre, dg_post, dlb, dg_head, dsinks = [], [], [], [], []
    for l in reversed(range(DEPTH)):
        x2, h, proj_h, proj_a, o_h, u2, states, kept_a, y = saved[l]
        dgp, du, dw_out[l] = _outproj_bwd(dxn, y, g_post, l, w_out[l], u2, f"outproj_bwd{l}")
        du = du.reshape(B, S, MIX_WIDTH)
        phase = None
        if dist:
            phase = _reduce_d2d_phase([dw_out[l]], [rb])
            if l == first:
                phase = _merge_phases(_reduce_ici_phase([part_in1]), phase)
        dqh, dfh, dih, dzh, dlb_l, dgh, got = _hgrn_bwd(
            proj_h, o_h, du, states, lb_param, g_head, l, f"hgrn_bwd{l}", phase)
        if dist:
            if l == first:
                sum_in = _chip_sum(part_in1, got[0], "chip_sum_in1", 1)
            part_out = _pair_sum(dw_out[l], got[-1], side, f"pair_sum_out{l}")
        dqa, dkv, dza, dsk, got = _attn_bwd(proj_a, du, kept_a, cos, sin, f"attn_bwd{l}",
                                            _reduce_ici_phase([part_out]) if dist else None)
        if dist:
            sum_out = _chip_sum(part_out, got[0], f"chip_sum_out{l}", l, None if l == last else sum_out)
        dproj = [p.reshape(T, p.shape[-1]) for p in (dqh, dfh, dih, dzh, dqa, dkv, dza)]
        dw_in[l] = _mm_tn(dproj, h, f"wgrad_in{l}")
        phase = None
        if dist and l == last:
            phase = _reduce_d2d_phase([dw_in[l]], [ra])
        if dist and l == first:
            got = _run_phase(_reduce_d2d_phase([dw_in[l]], [ra]), "reduce_in0_d2d")
            part_in0 = _pair_sum(dw_in[l], got[0], side, "pair_sum_in0")
            phase = _reduce_ici_phase([part_in0])
        dxn, dgpre, got = _inproj_bwd(dproj, w_in[l], x2, dxn, g_pre, l, f"inproj_bwd{l}", phase)
        if dist and l == last:
            part_in1 = _pair_sum(dw_in[l], got[0], side, "pair_sum_in1")
        if dist and l == first:
            sum_in = _chip_sum(part_in0, got[0], "chip_sum_in0", 0, sum_in)
        dg_pre.append(dgpre)
        dg_post.append(dgp)
        dlb.append(dlb_l)
        dg_head.append(dgh)
        dsinks.append(dsk)
    rev = lambda lst: jnp.concatenate(lst[::-1], axis=0)
    if not dist:
        sum_in, sum_out = jnp.stack(dw_in), jnp.stack(dw_out)
    return (loss_part, dxn.reshape(B, S, D), sum_in, sum_out,
            rev(dg_pre), rev(dg_post), rev(dlb), rev(dg_head), rev(dsinks))


def _me_and_peers():
    x, y, c = lax.axis_index("x"), lax.axis_index("y"), lax.axis_index("c")
    me = 4 * x + 2 * y + c
    peers = []
    for k in range(1, N_DEV):
        px = 1 - x if k & 4 else x
        py = 1 - y if k & 2 else y
        pc = 1 - c if k & 1 else c
        peers.append(((px, py, pc), 4 * px + 2 * py + pc))
    return me, peers


class _Phase:
    def __init__(self, arrays, out_shapes, aliases, n_send, build):
        self.arrays, self.out_shapes, self.aliases = list(arrays), list(out_shapes), dict(aliases)
        self.n_send, self.build = n_send, build

    def scratch(self):
        return [pltpu.SemaphoreType.DMA((self.n_send,)), pltpu.SemaphoreType.DMA((self.n_send,))]

    def _copies(self, in_refs, out_refs, sems, arrivals):
        send_sems, recv_sems = sems
        sends, recvs = self.build(in_refs, out_refs)
        assert len(sends) == self.n_send == len(recvs)
        out = [pltpu.make_async_remote_copy(src_ref=s, dst_ref=d, send_sem=send_sems.at[i], recv_sem=recv_sems.at[i],
                                            device_id=dev, device_id_type=MESH) for i, (s, d, dev) in enumerate(sends)]
        inc = [pltpu.make_async_remote_copy(src_ref=s, dst_ref=r, send_sem=send_sems.at[i], recv_sem=recv_sems.at[i],
                                            device_id=dev, device_id_type=MESH)
               for i, ((s, _, dev), r) in enumerate(zip(sends, recvs))] if arrivals else []
        return out, inc

    def start(self, in_refs, out_refs, sems):
        out, _ = self._copies(in_refs, out_refs, sems, False)
        for cp in out:
            cp.start()

    def finish(self, in_refs, out_refs, sems):
        out, inc = self._copies(in_refs, out_refs, sems, True)
        for cp in inc:
            cp.wait_recv()
        for cp in out:
            cp.wait_send()


_ANY = pl.BlockSpec(memory_space=pl.ANY)


def _host_phase(phase, n_in, n_out):
    if phase is None:
        return [], [], [], {}, [], []
    aliases = {n_in + i: n_out + o for i, o in phase.aliases.items()}
    return ([_ANY] * len(phase.arrays), [_ANY] * len(phase.out_shapes), phase.out_shapes, aliases, phase.scratch(),
            phase.arrays)


def _split_refs(refs, n_in, n_out, n_scr, phase):
    pi = len(phase.arrays) if phase else 0
    po = len(phase.out_shapes) if phase else 0
    a = n_in + pi
    b = a + n_out + po
    return (refs[:n_in], refs[a:a + n_out], refs[b:b + n_scr], refs[n_in:a], refs[a + n_out:b], refs[b + n_scr:])


def _hosted_start(phase, p_in, p_out, p_sems, first):
    if phase is not None:
        @pl.when(first)
        def _():
            phase.start(p_in, p_out, p_sems)


def _hosted_finish(phase, p_in, p_out, p_sems, last):
    if phase is not None:
        @pl.when(last)
        def _():
            phase.finish(p_in, p_out, p_sems)


def _run_phase(phase, name):
    n_in, n_out = len(phase.arrays), len(phase.out_shapes)

    def body(*refs):
        phase.start(refs[:n_in], refs[n_in:n_in + n_out], refs[n_in + n_out:])
        phase.finish(refs[:n_in], refs[n_in:n_in + n_out], refs[n_in + n_out:])

    return pl.pallas_call(
        body, name=name, in_specs=[_ANY] * n_in, out_specs=[_ANY] * n_out,
        out_shape=phase.out_shapes, input_output_aliases=phase.aliases, scratch_shapes=phase.scratch(),
        compiler_params=pltpu.CompilerParams(has_side_effects=True),
    )(*phase.arrays)


def _gather_one_call(full, name):
    r = full.shape[0] // N_DEV
    half = r // 2

    def body(full_in, full_ref, send_sems, recv_sems):
        del full_in
        c, (own, xn, yn, dg), num = _mesh_place()
        me, sib = num(own, c), (*own, 1 - c)

        def blk(dev, part=None):
            start, n = (dev * r, r) if part is None else (dev * r + part * half, half)
            return full_ref.at[pl.ds(pl.multiple_of(start, 16), n), :]

        def copy(k, src, dev, to, part=None):
            return pltpu.make_async_remote_copy(src_ref=src, dst_ref=blk(dev, part),
                                                send_sem=send_sems.at[k], recv_sem=recv_sems.at[k],
                                                device_id=to, device_id_type=MESH)

        def landed(k, dev, part=None):
            copy(k, blk(dev, part), dev, sib, part).wait_recv()

        sent = []

        def start(*cps):
            for cp in cps:
                cp.start()
                sent.append(cp)

        xs, ys, ds = num(xn, c), num(yn, c), num(dg, c)
        start(copy(0, blk(me), me, sib), copy(1, blk(me), me, (*xn, c)), copy(2, blk(me), me, (*yn, c)))
        landed(1, xs)
        start(copy(3, blk(xs, 0), xs, (*yn, c), 0), copy(5, blk(xs), xs, sib))
        landed(2, ys)
        start(copy(4, blk(ys, 1), ys, (*xn, c), 1), copy(6, blk(ys), ys, sib))
        landed(3, ds, 0)
        landed(4, ds, 1)
        start(copy(7, blk(ds), ds, sib))
        landed(0, num(own, 1 - c))
        for k, ch in ((5, xn), (6, yn), (7, dg)):
            landed(k, num(ch, 1 - c))
        for cp in sent:
            cp.wait_send()

    assert half % 16 == 0
    return pl.pallas_call(
        body, name=name, in_specs=[_ANY], out_specs=_ANY,
        out_shape=jax.ShapeDtypeStruct(full.shape, full.dtype), input_output_aliases={0: 0},
        scratch_shapes=[pltpu.SemaphoreType.DMA((8,)), pltpu.SemaphoreType.DMA((8,))],
        compiler_params=pltpu.CompilerParams(has_side_effects=True),
    )(full)


def _merge_phases(a, b):
    n_in, n_out = len(a.arrays), len(a.out_shapes)
    aliases = dict(a.aliases)
    aliases.update({n_in + i: n_out + o for i, o in b.aliases.items()})

    def build(ins, outs):
        sa, ra = a.build(ins[:n_in], outs[:n_out])
        sb, rb = b.build(ins[n_in:], outs[n_out:])
        return sa + sb, ra + rb

    return _Phase(a.arrays + b.arrays, a.out_shapes + b.out_shapes, aliases, a.n_send + b.n_send, build)


def _mesh_place():
    x, y, c = lax.axis_index("x"), lax.axis_index("y"), lax.axis_index("c")
    chips = [(x, y), (1 - x, y), (x, 1 - y), (1 - x, 1 - y)]
    num = lambda chip, core: 4 * chip[0] + 2 * chip[1] + core
    return c, chips, num


def _own_side_blocks():
    c, chips, num = _mesh_place()
    return jnp.stack([num(ch, c) for ch in chips]).astype(jnp.int32)


def _rows(ref, r, dev):
    return ref.at[pl.ds(pl.multiple_of(dev * r, 16), r), :]


def _place_own(shards, blocks, name):
    n = len(shards)

    def body(idx_ref, *refs):
        del idx_ref
        outs = iter(refs[n:])
        for s_ref in refs[:n]:
            for l in range(DEPTH):
                next(outs)[...] = s_ref[l].astype(BF16)

    whole = lambda s: pl.BlockSpec(s.shape, lambda i, idx: (0, 0, 0))
    own = lambda s: pl.BlockSpec(s.shape[1:], lambda i, idx: (idx[0], 0))
    res = pl.pallas_call(
        body, name=name,
        grid_spec=pltpu.PrefetchScalarGridSpec(
            num_scalar_prefetch=1, grid=(1,),
            in_specs=[whole(s) for s in shards],
            out_specs=[own(s) for s in shards for _ in range(DEPTH)]),
        out_shape=[jax.ShapeDtypeStruct((N_DEV * s.shape[1], s.shape[2]), BF16) for s in shards for _ in range(DEPTH)],
        compiler_params=_params(("arbitrary",)),
    )(blocks, *shards)
    return [list(res[i * DEPTH:(i + 1) * DEPTH]) for i in range(n)]


def _gather_ici_phase(fulls):
    rs = [a.shape[0] // N_DEV for a in fulls]
    n = len(fulls)

    def build(ins, outs):
        del ins
        c, chips, num = _mesh_place()
        me = num(chips[0], c)
        targets = [((*chips[0], 1 - c), num(chips[0], 1 - c))] + [((*ch, c), num(ch, c)) for ch in chips[1:]]
        sends, recvs = [], []
        for dev, dnum in targets:
            for i, r in enumerate(rs):
                sends.append((_rows(outs[i], r, me), _rows(outs[i], r, me), dev))
                recvs.append(_rows(outs[i], r, dnum))
        return sends, recvs

    shapes = [jax.ShapeDtypeStruct(a.shape, a.dtype) for a in fulls]
    return _Phase(list(fulls), shapes, {i: i for i in range(n)}, 4 * n, build)


def _gather_d2d_phase(fulls, rs):
    def build(ins, outs):
        c, chips, num = _mesh_place()
        sib = (*chips[0], 1 - c)
        sends, recvs = [], []
        for ch in chips[1:]:
            for i, r in enumerate(rs):
                blk = _rows(outs[i], r, num(ch, c))
                sends.append((blk, blk, sib))
                recvs.append(_rows(outs[i], r, num(ch, 1 - c)))
        return sends, recvs

    shapes = [jax.ShapeDtypeStruct(a.shape, a.dtype) for a in fulls]
    return _Phase(fulls, shapes, {i: i for i in range(len(fulls))}, 3 * len(fulls), build)


def _reduce_d2d_phase(grads, rs):
    def build(ins, outs):
        c, chips, num = _mesh_place()
        sib = (*chips[0], 1 - c)
        sends, recvs = [], []
        for j, ch in enumerate(chips):
            for i, r in enumerate(rs):
                sends.append((_rows(ins[i], r, num(ch, 1 - c)), outs[i].at[j], sib))
                recvs.append(outs[i].at[j])
        return sends, recvs

    shapes = [jax.ShapeDtypeStruct((4, r, g.shape[1]), g.dtype) for g, r in zip(grads, rs)]
    return _Phase(grads, shapes, {}, 4 * len(grads), build)


def _reduce_ici_phase(parts):
    def build(ins, outs):
        c, chips, _ = _mesh_place()
        sends, recvs = [], []
        for t in range(1, 4):
            for i in range(len(parts)):
                sends.append((ins[i].at[t], outs[i].at[t - 1], (*chips[t], c)))
                recvs.append(outs[i].at[t - 1])
        return sends, recvs

    shapes = [jax.ShapeDtypeStruct((3,) + p.shape[1:], p.dtype) for p in parts]
    return _Phase(parts, shapes, {}, 3 * len(parts), build)


def _pair_sum(g, got, blocks, name):
    n, r, D = got.shape
    tr = _pick(r, (800, 400, 256, 200, 128, 64, 16))

    def body(idx_ref, g_ref, r_ref, o_ref):
        del idx_ref
        o_ref[...] = (g_ref[...].astype(F32) + r_ref[...].astype(F32)).astype(o_ref.dtype)

    blk = pl.BlockSpec((None, tr, D), lambda j, i, idx: (j, i, 0))
    return pl.pallas_call(
        body, name=name,
        grid_spec=pltpu.PrefetchScalarGridSpec(
            num_scalar_prefetch=1, grid=(n, r // tr),
            in_specs=[pl.BlockSpec((tr, D), lambda j, i, idx: (idx[j] * (r // tr) + i, 0)), blk],
            out_specs=blk),
        out_shape=jax.ShapeDtypeStruct(got.shape, got.dtype),
        compiler_params=_params(("arbitrary", "arbitrary")),
    )(blocks, g, got)


def _chip_sum(p, r, name, layer, into=None):
    _, R, D = p.shape
    tr = _pick(R, (800, 400, 256, 200, 128, 64, 16))

    def body(p_ref, r_ref, *rest):
        acc = p_ref[...].astype(F32)
        for t in range(3):
            acc = acc + r_ref[t].astype(F32)
        rest[-1][...] = acc

    args = [p, r] + ([] if into is None else [into])
    return pl.pallas_call(
        body, name=name, grid=(R // tr,),
        in_specs=[pl.BlockSpec((None, tr, D), lambda i: (0, i, 0)), pl.BlockSpec((3, tr, D), lambda i: (0, i, 0))]
        + ([] if into is None else [_ANY]),
        out_specs=pl.BlockSpec((None, tr, D), lambda i: (layer, i, 0)),
        out_shape=jax.ShapeDtypeStruct((DEPTH, R, D), F32),
        input_output_aliases={} if into is None else {2: 0},
        compiler_params=_params(("parallel",)))(*args)


def _allreduce_small(vec):
    R, C = vec.shape

    def body(v_ref, o_ref, buf, send_sems, recv_sems):
        me, peers = _me_and_peers()
        buf[me] = v_ref[...]
        sends = []
        for k, (pid, _) in enumerate(peers):
            cp = pltpu.make_async_remote_copy(src_ref=v_ref, dst_ref=buf.at[me], send_sem=send_sems.at[k],
                                              recv_sem=recv_sems.at[k], device_id=pid, device_id_type=MESH)
            cp.start()
            sends.append(cp)
        for k, (pid, pnum) in enumerate(peers):
            pltpu.make_async_remote_copy(src_ref=v_ref, dst_ref=buf.at[pnum], send_sem=send_sems.at[k],
                                         recv_sem=recv_sems.at[k], device_id=pid, device_id_type=MESH).wait_recv()
        for cp in sends:
            cp.wait_send()
        acc = buf[0]
        for d in range(1, N_DEV):
            acc = acc + buf[d]
        o_ref[...] = acc

    vm = pl.BlockSpec(memory_space=pltpu.VMEM)
    return pl.pallas_call(
        body, name="allreduce_small",
        in_specs=[vm], out_specs=vm,
        out_shape=jax.ShapeDtypeStruct((R, C), F32),
        scratch_shapes=[pltpu.VMEM((N_DEV, R, C), F32), pltpu.SemaphoreType.DMA((N_DEV - 1,)),
                        pltpu.SemaphoreType.DMA((N_DEV - 1,))],
        compiler_params=pltpu.CompilerParams(has_side_effects=True),
    )(vec)


def _adamw_update(w, g, m, v):
    c1 = 1.0 - ADAM_B1 ** ADAM_STEP
    c2 = 1.0 - ADAM_B2 ** ADAM_STEP
    mn = ADAM_B1 * m + (1.0 - ADAM_B1) * g
    vn = ADAM_B2 * v + (1.0 - ADAM_B2) * (g * g)
    return -ADAM_LR * ((mn / c1) / (jnp.sqrt(vn / c2) + ADAM_EPS) + ADAM_WD * w), mn, vn


def _adamw(w, g, m, v, name):
    R, C = w.shape
    tr = _pick(R, (512, 400, 256, 128, 64, 32, 16, 8))

    def body(w_ref, g_ref, m_ref, v_ref, d_ref, mo_ref, vo_ref):
        d_ref[...], mo_ref[...], vo_ref[...] = _adamw_update(w_ref[...], g_ref[...], m_ref[...], v_ref[...])

    blk = pl.BlockSpec((tr, C), lambda i: (i, 0))
    sh = jax.ShapeDtypeStruct((R, C), F32)
    return pl.pallas_call(
        body, name=name, grid=(R // tr,), in_specs=[blk] * 4, out_specs=[blk] * 3, out_shape=[sh] * 3,
        compiler_params=_params(("parallel",)),
    )(w, g, m, v)


def _adamw_whole(ws, gs, ms, vs, name):
    n = len(ws)

    def body(*refs):
        for j in range(n):
            outs = refs[4 * n + 3 * j:4 * n + 3 * j + 3]
            outs[0][...], outs[1][...], outs[2][...] = _adamw_update(*(r[...] for r in refs[4 * j:4 * j + 4]))

    vm = pl.BlockSpec(memory_space=pltpu.VMEM)
    res = pl.pallas_call(
        body, name=name, in_specs=[vm] * (4 * n), out_specs=[vm] * (3 * n),
        out_shape=[jax.ShapeDtypeStruct(w.shape, F32) for w in ws for _ in range(3)],
    )(*[a for four in zip(ws, gs, ms, vs) for a in four])
    return [tuple(res[3 * j:3 * j + 3]) for j in range(n)]


def _lb_param_grad(lb_param, dlb):
    L, C = lb_param.shape

    def body(p_ref, d_ref, o_ref):
        lbp = p_ref[...]
        d = d_ref[...]
        mx = jnp.max(lbp, axis=0, keepdims=True)
        e = jnp.exp(lbp - mx)
        p = e / jnp.sum(e, axis=0, keepdims=True)
        tot = jnp.sum(d, axis=0, keepdims=True)
        dps = []
        rest = tot
        for j in range(L):
            dps.append(rest - tot if j == 0 else rest)
            rest = rest - d[j:j + 1]
        dp = jnp.concatenate(dps, axis=0)
        o_ref[...] = p * (dp - jnp.sum(p * dp, axis=0, keepdims=True))

    vm = pl.BlockSpec(memory_space=pltpu.VMEM)
    return pl.pallas_call(body, name="lb_param_grad", in_specs=[vm, vm], out_specs=vm,
                          out_shape=jax.ShapeDtypeStruct((L, C), F32))(lb_param, dlb)


def _pack_small(loss_part, dg_pre, dg_post, dlb, dg_head, dsinks):
    pad8 = lambda a: jnp.pad(a.reshape(-1, 128), ((0, 8 - DEPTH), (0, 0)))
    rows = [dg_pre.reshape(-1, 128), dg_post.reshape(-1, 128), dlb.reshape(-1, 128), pad8(dg_head), pad8(dsinks),
            loss_part]
    return jnp.concatenate(rows, axis=0)


def _unpack_small(vec):
    n = DEPTH * D_MODEL // 128
    o = 0
    dg_pre = vec[o:o + n].reshape(DEPTH, D_MODEL); o += n
    dg_post = vec[o:o + n].reshape(DEPTH, D_MODEL); o += n
    dlb = vec[o:o + n].reshape(DEPTH, HG_WIDTH); o += n
    dg_head = vec[o:o + DEPTH]; o += 8
    dsinks = vec[o:o + DEPTH, :ATT_HEADS]; o += 8
    loss = jnp.sum(vec[o:o + 8])
    return loss, dg_pre, dg_post, dlb, dg_head, dsinks


def kernel(x, w_in, w_out, g_pre, g_post, lb_param, g_head, sinks, loss_target, m_w_in, m_w_out, m_g_pre, m_g_post, m_lb_param, m_g_head, m_sinks, v_w_in, v_w_out, v_g_pre, v_g_post, v_lb_param, v_g_head, v_sinks):
    tr = lambda a: jnp.swapaxes(a, 1, 2)
    w_in_t = tr(w_in)
    (loss_part, dx, gw_in_t, gw_out, dg_pre, dg_post, dlb, dg_head, dsinks) = _step(
        x, loss_target, g_pre, g_post, lb_param, g_head, sinks, shards=(w_in_t, w_out))

    small = _allreduce_small(_pack_small(loss_part, dg_pre, dg_post, dlb, dg_head, dsinks))
    loss, gg_pre, gg_post, gdlb, gg_head, gsinks = _unpack_small(small)
    glb = _lb_param_grad(lb_param, gdlb)

    grads = [gw_in_t, gw_out, gg_pre, gg_post, glb, gg_head, gsinks]
    ws = [w_in_t, w_out, g_pre, g_post, lb_param, g_head, sinks]
    ms = [tr(m_w_in), m_w_out, m_g_pre, m_g_post, m_lb_param, m_g_head, m_sinks]
    vs = [tr(v_w_in), v_w_out, v_g_pre, v_g_post, v_lb_param, v_g_head, v_sinks]
    deltas, new_m, new_v = [], [], []
    big = 2
    for w, g, m, v, nm in zip(ws[:big], grads, ms, vs, ("w_in", "w_out")):
        sh = w.shape
        two = lambda a: a.reshape(-1, sh[-1])
        d, mn, vn = _adamw(two(w), two(g), two(m), two(v), "adamw_" + nm)
        deltas.append(d.reshape(sh))
        new_m.append(mn.reshape(sh))
        new_v.append(vn.reshape(sh))
    for d, mn, vn in _adamw_whole(ws[big:], grads[big:], ms[big:], vs[big:], "adamw_vectors"):
        deltas.append(d)
        new_m.append(mn)
        new_v.append(vn)
    grads[0], deltas[0], new_m[0], new_v[0] = tr(grads[0]), tr(deltas[0]), tr(new_m[0]), tr(new_v[0])
    return (loss, dx, *grads, *deltas, *new_m, *new_v)
```

```python
import math

import numpy as np
import jax
import jax.numpy as jnp
from jax import lax
from jax.experimental import pallas as pl
from jax.experimental.pallas import tpu as pltpu

F32 = jnp.float32
BF16 = jnp.bfloat16

D_MODEL = 1024
DEPTH = 2
HG_HEADS = 8
HG_DIM = 128
HG_WIDTH = HG_HEADS * HG_DIM
CHUNK = 64
ATT_HEADS = 16
ATT_DIM = 64
ATT_WIDTH = ATT_HEADS * ATT_DIM
KV_WIDTH = 128
ATT_BLOCK = 128
ATT_SCALE = 1.0 / math.sqrt(ATT_DIM)
ROPE_THETA = 10000.0
NORM_EPS = 1e-6
NEG_INF = -1e30
LB_FLOOR = 1e-20
N_H = 4 * HG_WIDTH
N_A = 2 * ATT_WIDTH + 2 * KV_WIDTH
IN_WIDTH = N_H + N_A
MIX_WIDTH = HG_WIDTH + ATT_WIDTH

ADAM_LR = 0.001
ADAM_B1 = 0.9
ADAM_B2 = 0.999
ADAM_EPS = 1e-08
ADAM_WD = 0.01
ADAM_STEP = 10

N_DEV = 8
MESH = pl.DeviceIdType.MESH
VMEM_LIMIT = 56 * 1024 * 1024

NN = ((1,), (0,))
NT = ((1,), (1,))
TN = ((0,), (0,))


def _dot(a, b, dims):
    return lax.dot_general(a.astype(BF16), b.astype(BF16), (dims, ((), ())), preferred_element_type=F32)


def _params(sem=None, **kw):
    return pltpu.CompilerParams(dimension_semantics=sem, vmem_limit_bytes=VMEM_LIMIT, **kw)


def _sigmoids(x):
    e = jnp.exp(-jnp.abs(x))
    r = 1.0 / (1.0 + e)
    er = e * r
    pos = x >= 0.0
    return jnp.where(pos, r, er), jnp.where(pos, er, r)


def _silu(x):
    return x * _sigmoids(x)[0]


def _silu_and_grad(x):
    s, ns = _sigmoids(x)
    return x * s, s * (1.0 + x * ns)


def _pick(n, prefs):
    for p in prefs:
        if n % p == 0:
            return p
    return n


def _inproj(x2, g, layer, w, name):
    T, D = x2.shape
    tm = _pick(T, (512, 256, 128))
    nchunk = 1024

    def body(x_ref, g_ref, w_ref, oh_ref, oa_ref, h_ref):
        x = x_ref[...]
        r = lax.rsqrt(jnp.mean(x * x, axis=-1, keepdims=True) + NORM_EPS)
        h = ((x * r) * g_ref[layer:layer + 1, :]).astype(BF16)
        h_ref[...] = h
        for j in range(0, N_H, nchunk):
            oh_ref[:, j:j + nchunk] = lax.dot_general(h, w_ref[j:j + nchunk, :], (NT, ((), ())),
                                                      preferred_element_type=F32)
        for j in range(0, N_A, N_A // 2):
            oa_ref[:, j:j + N_A // 2] = lax.dot_general(h, w_ref[N_H + j:N_H + j + N_A // 2, :], (NT, ((), ())),
                                                        preferred_element_type=F32)

    row = lambda w_: pl.BlockSpec((tm, w_), lambda i: (i, 0))
    return pl.pallas_call(
        body, name=name,
        grid=(T // tm,),
        in_specs=[row(D), pl.BlockSpec((DEPTH, D), lambda i: (0, 0)),
                  pl.BlockSpec((IN_WIDTH, D), lambda i: (0, 0), pipeline_mode=pl.Buffered(1))],
        out_specs=[row(N_H), row(N_A), row(D)],
        out_shape=[jax.ShapeDtypeStruct((T, N_H), F32), jax.ShapeDtypeStruct((T, N_A), F32),
                   jax.ShapeDtypeStruct((T, D), BF16)],
        compiler_params=_params(("parallel",)),
    )(x2, g, w)


def _mm_tn(pieces, b, name, out_dtype=BF16):
    T, m = b.shape
    tn = 256
    slots = 3
    counts = [p.shape[1] // tn for p in pieces]
    starts = [sum(counts[:i]) for i in range(len(pieces))]
    n_p = len(pieces)
    n = sum(counts)
    assert n >= slots

    def body(*refs):
        b_ref, o_ref, ring, sems = refs[n_p:]
        i = pl.program_id(0)

        def tile_copy(p, col, slot):
            return pltpu.make_async_copy(refs[p].at[:, pl.ds(col, tn)], ring.at[slot], sems.at[slot])

        def fetch(t):
            for p in range(n_p):
                @pl.when((t >= starts[p]) & (t < starts[p] + counts[p]))
                def _(p=p):
                    tile_copy(p, pl.multiple_of((t - starts[p]) * tn, tn), t % slots).start()

        @pl.when(i == 0)
        def _():
            for t in range(slots - 1):
                fetch(jnp.int32(t))

        @pl.when(i + (slots - 1) < n)
        def _():
            fetch(i + (slots - 1))

        slot = i % slots
        tile_copy(0, 0, slot).wait()
        o_ref[...] = lax.dot_general(ring[slot], b_ref[...], (TN, ((), ())),
                                     preferred_element_type=F32).astype(out_dtype)

    return pl.pallas_call(
        body, name=name,
        grid=(n,),
        in_specs=[_ANY] * n_p + [pl.BlockSpec((T, m), lambda i: (0, 0), pipeline_mode=pl.Buffered(1))],
        out_specs=pl.BlockSpec((tn, m), lambda i: (i, 0)),
        out_shape=jax.ShapeDtypeStruct((n * tn, m), out_dtype),
        scratch_shapes=[pltpu.VMEM((slots, T, tn), BF16), pltpu.SemaphoreType.DMA((slots,))],
        compiler_params=_params(("arbitrary",)),
    )(*pieces, b)


_LEVELS = (0, 1, 2, 4, 8, 16, 32)
_CUM_L = (2, 4, 8, 16, 32, 64)
_ALL_KINDS = tuple(("c", L) for L in _CUM_L) + tuple(("r", L) for L in _CUM_L)
_MXU_KINDS = (("c", 2), ("c", 4), ("c", CHUNK), ("r", 2), ("r", 4))
N_CUM = len(_ALL_KINDS) * CHUNK
N_CUM_F = len(_MXU_KINDS) * CHUNK


def _cum_matrices():
    t = np.arange(CHUNK)[:, None]
    r = np.arange(CHUNK)[None, :]

    def mat(kind):
        c, L = kind
        return ((r // L == t // L) & ((r <= t) if c == "c" else (r > t))).astype(np.float32)

    fwd = np.concatenate([mat(kd) for kd in _MXU_KINDS], axis=0)
    full = np.concatenate([mat(kd) for kd in _ALL_KINDS], axis=0)
    return jnp.asarray(fwd, BF16), jnp.asarray(full.T.copy(), BF16)


def _level_masks():
    t = np.arange(CHUNK)[:, None]
    s = np.arange(CHUNK)[None, :]
    ms = []
    for L in _LEVELS:
        if L == 0:
            ms.append(t == s)
        else:
            ms.append((t // (2 * L) == s // (2 * L)) & ((t // L) % 2 == 1) & ((s // L) % 2 == 0))
    return jnp.asarray(np.stack(ms).astype(np.float32))


def _split3(x):
    hi = x.astype(BF16)
    r1 = x - hi.astype(F32)
    mid = r1.astype(BF16)
    lo = (r1 - mid.astype(F32)).astype(BF16)
    return hi, mid, lo


def _cum3(ts, x, terms=3):
    d = lambda p: lax.dot_general(ts, p, (NN, ((), ())), preferred_element_type=F32)
    return sum(d(p) for p in _split3(x)[:terms])


def _lb_terms(lbp, layer):
    mx = jnp.max(lbp, axis=0, keepdims=True)
    e = jnp.exp(lbp - mx)
    p = e / jnp.sum(e, axis=0, keepdims=True)
    cum = p[0:1]
    for j in range(1, layer + 1):
        cum = cum + p[j:j + 1]
    lb = cum - p[0:1]
    lbf = jnp.maximum(lb, LB_FLOOR)
    return dict(lbf=lbf, one_m=1.0 - lb, kcorr=lb - lbf, ind=jnp.where(lb > LB_FLOOR, 1.0, 0.0))


def _gate(x, lt):
    sig, nsig = _sigmoids(x)
    f = lt["lbf"] + lt["one_m"] * sig
    return jnp.log(f), lt["one_m"] * nsig + lt["kcorr"], f, sig, nsig


def _ck(x, ci):
    return x[ci * CHUNK:(ci + 1) * CHUNK]


def _block_cums(ts, g, nc):
    cs = [_cum3(ts, _ck(g, ci), terms=2) for ci in range(nc)]
    out = {kind: jnp.concatenate([c[CHUNK * i:CHUNK * (i + 1)] for c in cs], axis=0)
           for i, kind in enumerate(_MXU_KINDS)}
    b = out[("c", CHUNK)]
    ng = CHUNK // 8
    last = b.reshape(nc, ng, 8, HG_DIM)[:, :, 7:8, :]
    zero = jnp.zeros((nc, 1, 1, HG_DIM), F32)

    def spread(groups):
        return jnp.broadcast_to(jnp.concatenate(groups, axis=1), (nc, ng, 8, HG_DIM)).reshape(nc * CHUNK, HG_DIM)

    def get(kind):
        if kind in out:
            return out[kind]
        c, L = kind
        nb = L // 8
        first = lambda r: (r // nb) * nb
        if c == "c":
            return b - spread([last[:, first(r) - 1:first(r)] if r >= nb else zero for r in range(ng)])
        return spread([last[:, first(r) + nb - 1:first(r) + nb] for r in range(ng)]) - b

    return get


def _level_factors(cums, g, L):
    if L == 0:
        return None, None
    if L == 1:
        return jnp.exp(g[...]), None
    return jnp.exp(cums(("c", L))), jnp.exp(cums(("r", L)))


def _mul(a, e):
    return a if e is None else a * e


def _hg_block_fwd(qf, k, v, g, ts, m_ref, nc):
    cums = _block_cums(ts, g, nc)
    amat = [jnp.zeros((CHUNK, CHUNK), F32)] * nc
    for li, L in enumerate(_LEVELS):
        eq, ek = _level_factors(cums, g, L)
        ql, kl, m = _mul(qf, eq), _mul(k, ek), m_ref[li]
        amat = [amat[ci] + _dot(_ck(ql, ci), _ck(kl, ci), NT) * m for ci in range(nc)]
    b = cums(("c", CHUNK))
    kst = k * jnp.exp(cums(("r", CHUNK)))
    o = [_dot(amat[ci], _ck(v, ci), NN) for ci in range(nc)]
    kv = [_dot(_ck(v, ci), _ck(kst, ci), TN) for ci in range(nc)]
    dec = [jnp.exp(b[(ci + 1) * CHUNK - 1:(ci + 1) * CHUNK, :]) for ci in range(nc)]
    return o, dec, kv, qf * jnp.exp(b), amat


def _hg_block_bwd(qf, k, v, g, do, amat, ts, m_ref, nc):
    cums = _block_cums(ts, g, nc)
    dcs = {}
    da = [_dot(_ck(do, ci), _ck(v, ci), NT) for ci in range(nc)]
    dq = jnp.zeros(qf.shape, F32)
    dk = jnp.zeros(qf.shape, F32)
    dg = jnp.zeros(qf.shape, F32)
    for li, L in enumerate(_LEVELS):
        eq, ek = _level_factors(cums, g, L)
        qlb, klb, m = _mul(qf[...], eq).astype(BF16), _mul(k[...], ek).astype(BF16), m_ref[li]
        dal = [(da[ci] * m).astype(BF16) for ci in range(nc)]
        both = [(_dot(dal[ci], _ck(klb, ci), NN), _dot(dal[ci], _ck(qlb, ci), TN)) for ci in range(nc)]
        dql = _mul(jnp.concatenate([p[0] for p in both], axis=0), eq)
        dkl = _mul(jnp.concatenate([p[1] for p in both], axis=0), ek)
        dq = dq + dql
        dk = dk + dkl
        if L == 1:
            dg = dg + dql * qf[...]
        elif L > 1:
            dcs[("c", L)] = (dql * qf[...]).astype(BF16)
            dcs[("r", L)] = (dkl * k[...]).astype(BF16)
    b = cums(("c", CHUNK))
    e64 = jnp.exp(b)
    er64 = jnp.exp(cums(("r", CHUNK)))
    qb = (qf[...] * e64).astype(BF16)
    return dict(dq=dq, dk=dk, dg=dg, dcs=dcs, e64=e64, er64=er64, qf=qf, k=k, kst=(k[...] * er64).astype(BF16),
                dv=[_dot(amat[ci], _ck(do, ci), TN) for ci in range(nc)],
                dec=[jnp.exp(b[(ci + 1) * CHUNK - 1:(ci + 1) * CHUNK, :]) for ci in range(nc)],
                qd=[_dot(_ck(do, ci), _ck(qb, ci), TN) for ci in range(nc)])


def _hg_state_bwd(w, v, do, starts, ends, tst, nc):
    dqb = jnp.concatenate([_dot(_ck(do, ci), starts[ci], NN) for ci in range(nc)], axis=0)
    dkst = jnp.concatenate([_dot(_ck(v, ci), ends[ci], NN) for ci in range(nc)], axis=0)
    dqb, dkst = dqb * w["e64"], dkst * w["er64"]
    dq = w["dq"] + dqb
    dk = w["dk"] + dkst
    dv = jnp.concatenate([w["dv"][ci] + _dot(_ck(w["kst"], ci), ends[ci], NT) for ci in range(nc)], axis=0)
    trow = lax.broadcasted_iota(jnp.int32, (CHUNK, 1), 0)
    dtot = jnp.concatenate(
        [jnp.where(trow == CHUNK - 1, jnp.sum(ends[ci] * starts[ci], axis=0, keepdims=True) * w["dec"][ci], 0.0)
         for ci in range(nc)], axis=0)
    dcs = dict(w["dcs"])
    dcs[("c", CHUNK)] = (dqb * w["qf"][...] + dtot).astype(BF16)
    dcs[("r", CHUNK)] = (dkst * w["k"][...]).astype(BF16)
    dgs = [_dot(tst, jnp.concatenate([_ck(dcs[kind], ci) for kind in _ALL_KINDS], axis=0), NN) for ci in range(nc)]
    return dq, dk, dv, w["dg"] + jnp.concatenate(dgs, axis=0)


def _hgrn_fwd(proj_h, u_rows, lb_param, g_head, layer, name, phase=None):
    B, S, _ = proj_h.shape
    sb = _pick(S, (2048, 1024, 512, 256, 128, 64))
    nc = sb // CHUNK
    ts, _ = _cum_matrices()

    def body(*refs):
        ins, outs, (st,), p_in, p_out, p_sems = _split_refs(refs, 8, 12, 1, phase)
        q_ref, f_ref, i_ref, z_ref, lbp_ref, gh_ref, ts_ref, m_ref = ins
        o_ref, u_ref, sts_ref, am_ref = outs[:4]
        logf_ref, k_ref, qf_ref, sg_ref, qg_ref, zg_ref, fg_ref, sig_ref = outs[4:]
        h_id, b_id, s_id = pl.program_id(0), pl.program_id(1), pl.program_id(2)
        _hosted_start(phase, p_in, p_out, p_sems, (h_id == 0) & (b_id == 0) & (s_id == 0))

        @pl.when(s_id == 0)
        def _():
            st[...] = jnp.zeros_like(st)

        lt = _lb_terms(lbp_ref[...], layer)
        tsv = ts_ref[...]
        gh = gh_ref[layer:layer + 1, :]
        logf, k, _, sig, nsig = _gate(f_ref[...], lt)
        qf, qf_grad = _silu_and_grad(q_ref[...])
        sg, sg_grad = _silu_and_grad(z_ref[...])
        logf_ref[...], k_ref[...], qf_ref[...], sg_ref[...] = logf, k, qf, sg
        qg_ref[...] = qf_grad.astype(BF16)
        zg_ref[...] = sg_grad.astype(BF16)
        fg_ref[...] = (lt["one_m"] * sig * nsig).astype(BF16)
        sig_ref[...] = sig.astype(BF16)
        o_part, dec, kv, qb, amat = _hg_block_fwd(qf, k, i_ref[...], logf, tsv, m_ref, nc)
        for ci in range(nc):
            am_ref[ci] = amat[ci].astype(BF16)
        cur = st[...]
        starts = []
        for ci in range(nc):
            sts_ref[ci] = cur
            starts.append(cur)
            cur = cur * dec[ci] + kv[ci]
        st[...] = cur
        o = jnp.concatenate([o_part[ci] + _dot(_ck(qb, ci), starts[ci], NT) for ci in range(nc)], axis=0)
        o_ref[...] = o
        r = lax.rsqrt(jnp.mean(o * o, axis=-1, keepdims=True) + NORM_EPS)
        u_ref[...] = (((o * r) * gh) * sg).astype(BF16)
        _hosted_finish(phase, p_in, p_out, p_sems, (h_id == HG_HEADS - 1) & (b_id == B - 1) & (s_id == S // sb - 1))

    col = lambda base: pl.BlockSpec((None, sb, HG_DIM), lambda h, b, s: (b, s, base + h))
    p_ispecs, p_ospecs, p_oshapes, p_alias, p_scratch, p_args = _host_phase(phase, 8, 12)
    wide = lambda dt: jax.ShapeDtypeStruct((B, S, HG_WIDTH), dt)
    res = pl.pallas_call(
        body, name=name,
        grid=(HG_HEADS, B, S // sb),
        in_specs=[col(0), col(HG_HEADS), col(2 * HG_HEADS), col(3 * HG_HEADS),
                  pl.BlockSpec((DEPTH, HG_DIM), lambda h, b, s: (0, h)),
                  pl.BlockSpec((DEPTH, HG_DIM), lambda h, b, s: (0, 0)),
                  pl.BlockSpec((N_CUM_F, CHUNK), lambda h, b, s: (0, 0)),
                  pl.BlockSpec((len(_LEVELS), CHUNK, CHUNK), lambda h, b, s: (0, 0, 0))] + p_ispecs,
        out_specs=[col(0), col(0),
                   pl.BlockSpec((None, None, nc, HG_DIM, HG_DIM), lambda h, b, s: (b, h, s, 0, 0)),
                   pl.BlockSpec((None, None, nc, CHUNK, CHUNK), lambda h, b, s: (b, h, s, 0, 0))]
        + [col(0)] * 8 + p_ospecs,
        out_shape=[wide(F32),
                   jax.ShapeDtypeStruct((B, S, u_rows), BF16),
                   jax.ShapeDtypeStruct((B, HG_HEADS, S // CHUNK, HG_DIM, HG_DIM), F32),
                   jax.ShapeDtypeStruct((B, HG_HEADS, S // CHUNK, CHUNK, CHUNK), BF16)]
        + [wide(F32)] * 4 + [wide(BF16)] * 4 + p_oshapes,
        input_output_aliases=p_alias,
        scratch_shapes=[pltpu.VMEM((HG_DIM, HG_DIM), F32)] + p_scratch,
        compiler_params=_params(("arbitrary", "arbitrary", "arbitrary")),
    )(proj_h, proj_h, proj_h, proj_h, lb_param, g_head, ts, _level_masks(), *p_args)
    return res[0], res[1], tuple(res[2:12]), list(res[12:])


def _hgrn_bwd(proj_h, o_h, du, kept, lb_param, g_head, layer, name, phase=None):
    B, S, _ = proj_h.shape
    sb = _pick(S, (512, 256, 128, 64))
    nc = sb // CHUNK
    ns = S // sb
    ts, tst = _cum_matrices()

    def body(*refs):
        ins, outs, (dst,), p_in, p_out, p_sems = _split_refs(refs, 18, 6, 1, phase)
        (i_ref, o_ref, du_ref, sts_ref, am_ref, logf_ref, k_ref, qf_ref, sg_ref, qg_ref, zg_ref, fg_ref, sig_ref,
         lbp_ref, gh_ref, ts_ref, tst_ref, m_ref) = ins
        dq_ref, df_ref, di_ref, dz_ref, dlb_ref, dgh_ref = outs
        h_id, b_id, s_id = pl.program_id(0), pl.program_id(1), pl.program_id(2)
        _hosted_start(phase, p_in, p_out, p_sems, (h_id == 0) & (b_id == 0) & (s_id == 0))

        @pl.when(s_id == 0)
        def _():
            dst[...] = jnp.zeros_like(dst)

        @pl.when((b_id == 0) & (s_id == 0))
        def _():
            dlb_ref[...] = jnp.zeros_like(dlb_ref)

        @pl.when((h_id == 0) & (b_id == 0) & (s_id == 0))
        def _():
            dgh_ref[...] = jnp.zeros_like(dgh_ref)

        lt = _lb_terms(lbp_ref[...], layer)
        gh = gh_ref[layer:layer + 1, :]
        tsv = ts_ref[...]
        tstv = tst_ref[...]
        sg = sg_ref[...]
        o = o_ref[...]
        dub = du_ref[...]
        r = lax.rsqrt(jnp.mean(o * o, axis=-1, keepdims=True) + NORM_EPS)
        n = o * r
        dz_ref[...] = (dub * (n * gh) * zg_ref[...].astype(F32)).astype(BF16)
        dgh_ref[...] += jnp.sum(dub * sg * n, axis=0, keepdims=True)
        dn = dub * sg * gh
        do = (r * (dn - n * jnp.mean(dn * n, axis=-1, keepdims=True))).astype(BF16)
        v = i_ref[...].astype(BF16)
        w = _hg_block_bwd(qf_ref, k_ref, v, logf_ref, do, [am_ref[ci] for ci in range(nc)], tsv, m_ref, nc)
        cur = dst[...]
        ends = [None] * nc
        for ci in reversed(range(nc)):
            ends[ci] = cur
            cur = cur * w["dec"][ci] + w["qd"][ci]
        dst[...] = cur
        dq, dk, dv, dg = _hg_state_bwd(w, v, do, [sts_ref[ci] for ci in range(nc)], ends, tstv, nc)
        di_ref[...] = dv.astype(BF16)
        dq_ref[...] = (dq * qg_ref[...].astype(F32)).astype(BF16)
        f = jnp.exp(logf_ref[...])
        scaled = (dg - f * dk) / f
        df_ref[...] = (scaled * fg_ref[...].astype(F32)).astype(BF16)
        dlb_ref[...] += jnp.sum(scaled * (lt["ind"] - sig_ref[...].astype(F32)), axis=0, keepdims=True)
        _hosted_finish(phase, p_in, p_out, p_sems, (h_id == HG_HEADS - 1) & (b_id == B - 1) & (s_id == ns - 1))

    col = lambda base: pl.BlockSpec((None, sb, HG_DIM), lambda h, b, s: (b, ns - 1 - s, base + h))
    out_col = pl.BlockSpec((None, sb, HG_DIM), lambda h, b, s: (b, ns - 1 - s, h))
    dt = jax.ShapeDtypeStruct((B, S, HG_WIDTH), BF16)
    p_ispecs, p_ospecs, p_oshapes, p_alias, p_scratch, p_args = _host_phase(phase, 18, 6)
    res = pl.pallas_call(
        body, name=name,
        grid=(HG_HEADS, B, ns),
        in_specs=[col(2 * HG_HEADS), col(0), col(0),
                  pl.BlockSpec((None, None, nc, HG_DIM, HG_DIM), lambda h, b, s: (b, h, ns - 1 - s, 0, 0)),
                  pl.BlockSpec((None, None, nc, CHUNK, CHUNK), lambda h, b, s: (b, h, ns - 1 - s, 0, 0))]
        + [col(0)] * 8
        + [pl.BlockSpec((DEPTH, HG_DIM), lambda h, b, s: (0, h)),
           pl.BlockSpec((DEPTH, HG_DIM), lambda h, b, s: (0, 0)),
           pl.BlockSpec((N_CUM_F, CHUNK), lambda h, b, s: (0, 0)),
           pl.BlockSpec((CHUNK, N_CUM), lambda h, b, s: (0, 0)),
           pl.BlockSpec((len(_LEVELS), CHUNK, CHUNK), lambda h, b, s: (0, 0, 0))] + p_ispecs,
        out_specs=[out_col, out_col, out_col, out_col,
                   pl.BlockSpec((1, HG_DIM), lambda h, b, s: (0, h)),
                   pl.BlockSpec((1, HG_DIM), lambda h, b, s: (0, 0))] + p_ospecs,
        out_shape=[dt, dt, dt, dt, jax.ShapeDtypeStruct((1, HG_WIDTH), F32),
                   jax.ShapeDtypeStruct((1, HG_DIM), F32)] + p_oshapes,
        input_output_aliases=p_alias,
        scratch_shapes=[pltpu.VMEM((HG_DIM, HG_DIM), F32)] + p_scratch,
        compiler_params=_params(("arbitrary", "arbitrary", "arbitrary")),
    )(proj_h, o_h, du, *kept, lb_param, g_head, ts, tst, _level_masks(), *p_args)
    return tuple(res[:6]) + (list(res[6:]),)


def _rope_tables(S):
    half = ATT_DIM // 2
    inv_freq = np.float32(ROPE_THETA) ** (-np.arange(half, dtype=np.float32) / half)
    ang = np.arange(S, dtype=np.float32)[:, None] * inv_freq[None, :]
    cos = np.cos(ang)
    sin = np.sin(ang)
    cos = np.concatenate([cos, cos, cos, cos], axis=1)
    sin = np.concatenate([-sin, sin, -sin, sin], axis=1)
    return jnp.asarray(cos, F32), jnp.asarray(sin, F32)


def _attn_common():
    lane = lax.broadcasted_iota(jnp.int32, (1, 2 * ATT_DIM), 1)
    first_half = (lane % ATT_DIM) < (ATT_DIM // 2)
    left = lane < ATT_DIM

    def swap(x):
        return jnp.where(first_half, pltpu.roll(x, 128 - ATT_DIM // 2, 1), pltpu.roll(x, ATT_DIM // 2, 1))

    def rope(x, cos, sin):
        return x * cos + swap(x) * sin

    def rope_bwd(dy, cos, sin):
        return dy * cos + swap(dy * sin)

    def dup(x):
        xs = pltpu.roll(x, ATT_DIM, 1)
        return [jnp.where(left, x, xs), jnp.where(left, xs, x)]

    return left, rope, rope_bwd, dup


GROUP = ATT_HEADS // 2
GROUP_ROWS = GROUP * ATT_BLOCK


def _attn_bias(i):
    r = lax.broadcasted_iota(jnp.int32, (ATT_BLOCK, 2 * ATT_BLOCK), 0)
    c = lax.broadcasted_iota(jnp.int32, (ATT_BLOCK, 2 * ATT_BLOCK), 1)
    ok = (c > r) & (c <= r + ATT_BLOCK) & ((c >= ATT_BLOCK) | (i > 0))
    return jnp.where(ok, 0.0, NEG_INF)


def _stack_heads(pairs, left):
    rows = []
    for x in pairs:
        rows += [jnp.where(left, x, 0.0), jnp.where(left, 0.0, x)]
    return jnp.concatenate(rows, axis=0)


def _unstack_heads(y, left, pp):
    r0 = 2 * pp * ATT_BLOCK
    return jnp.where(left, y[r0:r0 + ATT_BLOCK], y[r0 + ATT_BLOCK:r0 + 2 * ATT_BLOCK])


def _row_sums(x):
    return _dot(x, jnp.ones((x.shape[1], 128), BF16), NN)


def _attn_probs(qs, kd, vd, sink, bias):
    n = range(len(qs))
    rows = qs[0].shape[0]
    s = [(_dot(qs[j], kd[j], NT).reshape(rows // ATT_BLOCK, ATT_BLOCK, 2 * ATT_BLOCK) * ATT_SCALE + bias[None])
         .reshape(rows, 2 * ATT_BLOCK) for j in n]
    m = [jnp.max(jnp.maximum(jnp.maximum(s[j][:, :128], s[j][:, 128:]), sink[j]), axis=-1, keepdims=True) for j in n]
    pu = [jnp.exp(s[j] - m[j]) for j in n]
    es = [jnp.exp(sink[j] - m[j]) for j in n]
    ones = jnp.ones((2 * ATT_BLOCK, 128), BF16)
    ov = [_dot(pu[j], jnp.concatenate([vd[j].astype(BF16), ones], axis=1), NN) for j in n]
    inv = [1.0 / (ov[j][:, 128:] + es[j]) for j in n]
    return ([pu[j] * jnp.concatenate([inv[j], inv[j]], axis=1) for j in n], [es[j] * inv[j] for j in n],
            [ov[j][:, :128] * inv[j] for j in n])


def _sink_rows(sinks):
    return jnp.broadcast_to(jnp.repeat(sinks, ATT_BLOCK, axis=1)[:, :, None], (DEPTH, ATT_HEADS * ATT_BLOCK, 128))


_Z0 = (2 * ATT_WIDTH + 2 * KV_WIDTH - ATT_WIDTH) // 256


def _attn_fwd(proj_a, u, sink_rows, layer, cos, sin, name, phase=None):
    B, S, _ = proj_a.shape
    nb = S // ATT_BLOCK

    def body(*refs):
        ins, (u_ref, p_ref, o_ref, ps_ref, qs_ref), _, p_in, p_out, p_sems = _split_refs(refs, 13, 5, 0, phase)
        q_ref, kvc_ref, kvp_ref, z0, z1, z2, z3, cos_ref, sin_ref, cosp_ref, sinp_ref, sinks_ref, _ = ins
        i = pl.program_id(1)
        _hosted_start(phase, p_in, p_out, p_sems, (pl.program_id(0) == 0) & (i == 0))
        left, rope, _, dup = _attn_common()
        cos_c, sin_c = cos_ref[...], sin_ref[...]
        kvc = kvc_ref[...]
        kvp = kvp_ref[...]
        kw = jnp.concatenate([rope(kvp[:, :KV_WIDTH], cosp_ref[...], sinp_ref[...]),
                              rope(kvc[:, :KV_WIDTH], cos_c, sin_c)], axis=0)
        vw = jnp.concatenate([kvp[:, KV_WIDTH:], kvc[:, KV_WIDTH:]], axis=0)
        kd, vd = dup(kw), dup(vw)
        bias = _attn_bias(i)
        zs = (z0, z1, z2, z3)
        pairs = [range(4 * kvh, 4 * kvh + 4) for kvh in range(2)]
        qs = [_stack_heads([rope(q_ref[:, 128 * pr:128 * (pr + 1)], cos_c, sin_c) for pr in pairs[kvh]], left)
              for kvh in range(2)]
        sink = [sinks_ref[kvh * GROUP_ROWS:(kvh + 1) * GROUP_ROWS, :] for kvh in range(2)]
        p, ps, o = _attn_probs(qs, kd, vd, sink, bias)
        eye = (lax.broadcasted_iota(jnp.int32, (ATT_BLOCK, 128), 0)
               == lax.broadcasted_iota(jnp.int32, (ATT_BLOCK, 128), 1))
        for kvh in range(2):
            p_ref[kvh] = p[kvh].astype(BF16)
            qs_ref[kvh] = qs[kvh].astype(BF16)
            for g in range(GROUP):
                blk = ps[kvh][g * ATT_BLOCK:(g + 1) * ATT_BLOCK, :]
                ps_ref[kvh * GROUP + g:kvh * GROUP + g + 1, :] = jnp.sum(jnp.where(eye, blk, 0.0), axis=0, keepdims=True)
            for pp, pr in enumerate(pairs[kvh]):
                z = zs[pr // 2][:, 128 * (pr % 2):128 * (pr % 2 + 1)]
                o128 = _unstack_heads(o[kvh], left, pp)
                o_ref[:, 128 * pr:128 * (pr + 1)] = o128.astype(BF16)
                u_ref[:, 128 * pr:128 * (pr + 1)] = (o128 * _silu(z)).astype(BF16)
        _hosted_finish(phase, p_in, p_out, p_sems, (pl.program_id(0) == B - 1) & (i == nb - 1))

    rowblk = lambda w, cb: pl.BlockSpec((None, ATT_BLOCK, w), lambda b, i: (b, i, cb))
    tab = pl.BlockSpec((ATT_BLOCK, 128), lambda b, i: (i, 0))
    tabp = pl.BlockSpec((ATT_BLOCK, 128), lambda b, i: (jnp.maximum(i - 1, 0), 0))
    p_ispecs, p_ospecs, p_oshapes, p_alias, p_scratch, p_args = _host_phase(phase, 13, 5)
    res = pl.pallas_call(
        body, name=name,
        grid=(B, nb),
        in_specs=[rowblk(ATT_WIDTH, 0), rowblk(256, 4),
                  pl.BlockSpec((None, ATT_BLOCK, 256), lambda b, i: (b, jnp.maximum(i - 1, 0), 4)),
                  rowblk(256, _Z0), rowblk(256, _Z0 + 1), rowblk(256, _Z0 + 2), rowblk(256, _Z0 + 3),
                  tab, tab, tabp, tabp,
                  pl.BlockSpec((None, ATT_HEADS * ATT_BLOCK, 128), lambda b, i: (layer, 0, 0)),
                  pl.BlockSpec(memory_space=pl.ANY)] + p_ispecs,
        out_specs=[pl.BlockSpec((None, ATT_BLOCK, ATT_WIDTH), lambda b, i: (b, i, 1)),
                   pl.BlockSpec((None, None, 2, GROUP_ROWS, 2 * ATT_BLOCK), lambda b, i: (b, i, 0, 0, 0)),
                   pl.BlockSpec((None, ATT_BLOCK, ATT_WIDTH), lambda b, i: (b, i, 0)),
                   pl.BlockSpec((None, None, ATT_HEADS, 128), lambda b, i: (b, i, 0, 0)),
                   pl.BlockSpec((None, None, 2, GROUP_ROWS, 128), lambda b, i: (b, i, 0, 0, 0))] + p_ospecs,
        out_shape=[jax.ShapeDtypeStruct(u.shape, BF16),
                   jax.ShapeDtypeStruct((B, nb, 2, GROUP_ROWS, 2 * ATT_BLOCK), BF16),
                   jax.ShapeDtypeStruct((B, S, ATT_WIDTH), BF16),
                   jax.ShapeDtypeStruct((B, nb, ATT_HEADS, 128), F32),
                   jax.ShapeDtypeStruct((B, nb, 2, GROUP_ROWS, 128), BF16)] + p_oshapes,
        input_output_aliases={12: 0, **p_alias},
        scratch_shapes=p_scratch,
        compiler_params=_params(("arbitrary", "arbitrary")),
    )(proj_a, proj_a, proj_a, proj_a, proj_a, proj_a, proj_a, cos, sin, cos, sin, sink_rows, u, *p_args)
    return res[0], tuple(res[1:5]), list(res[5:])


def _attn_bwd(proj_a, du, kept, cos, sin, name, phase=None):
    B, S, _ = proj_a.shape
    nb = S // ATT_BLOCK
    p_kept, o_kept, ps_kept, qs_kept = kept

    def body(*refs):
        ins, outs, (carry, sk_acc), p_in, p_out, p_sems = _split_refs(refs, 15, 4, 2, phase)
        (qs_ref, kvc_ref, kvp_ref, z0, z1, z2, z3, du_ref, cos_ref, sin_ref, cosp_ref, sinp_ref,
         p_ref, o_ref, ps_ref) = ins
        dq_ref, dkv_ref, dz_ref, dsk_ref = outs
        b_id, i = pl.program_id(0), pl.program_id(1)
        _hosted_start(phase, p_in, p_out, p_sems, (b_id == 0) & (i == 0))

        @pl.when((b_id == 0) & (i == 0))
        def _():
            sk_acc[...] = jnp.zeros_like(sk_acc)

        @pl.when(i == 0)
        def _():
            carry[...] = jnp.zeros_like(carry)

        @pl.when(i < nb)
        def _():
            left, rope, rope_bwd, dup = _attn_common()
            cos_c, sin_c = cos_ref[...], sin_ref[...]
            cos_p, sin_p = cosp_ref[...], sinp_ref[...]
            kvc = kvc_ref[...]
            kvp = kvp_ref[...]
            kw = jnp.concatenate([rope(kvp[:, :KV_WIDTH], cos_p, sin_p), rope(kvc[:, :KV_WIDTH], cos_c, sin_c)], axis=0)
            vw = jnp.concatenate([kvp[:, KV_WIDTH:], kvc[:, KV_WIDTH:]], axis=0)
            kd, vd = dup(kw), dup(vw)
            zs = (z0, z1, z2, z3)
            units = [(kvh, hf) for kvh in range(2) for hf in range(2)]
            half = GROUP_ROWS // 2
            pairs = [range(4 * kvh + 2 * hf, 4 * kvh + 2 * hf + 2) for kvh, hf in units]
            ku = [kd[kvh] for kvh, _ in units]
            vu = [vd[kvh] for kvh, _ in units]
            ps_all = ps_ref[...]
            head_row = lax.broadcasted_iota(jnp.int32, (ATT_HEADS, 128), 0)
            eye = (lax.broadcasted_iota(jnp.int32, (ATT_BLOCK, 128), 0)
                   == lax.broadcasted_iota(jnp.int32, (ATT_BLOCK, 128), 1))

            def first(j):
                kvh, hf = units[j]
                p = p_ref[kvh, hf * half:(hf + 1) * half, :]
                parts = []
                for pr in pairs[j]:
                    cols = slice(128 * pr, 128 * (pr + 1))
                    sg, sg_grad = _silu_and_grad(zs[pr // 2][:, 128 * (pr % 2):128 * (pr % 2 + 1)])
                    du128 = du_ref[:, cols]
                    dz_ref[:, cols] = (du128 * o_ref[:, cols].astype(F32) * sg_grad).astype(BF16)
                    parts.append(du128 * sg)
                dos = _stack_heads(parts, left)
                dp = _dot(dos, vu[j], NT)
                delta = _row_sums(p.astype(F32) * dp)
                ds = (p.astype(F32) * (dp - jnp.concatenate([delta, delta], axis=1)) * ATT_SCALE).astype(BF16)
                sk = jnp.zeros((ATT_HEADS, 128), F32)
                for hh in range(4):
                    hd = kvh * GROUP + 4 * hf + hh
                    drow = jnp.sum(jnp.where(eye, delta[hh * ATT_BLOCK:(hh + 1) * ATT_BLOCK, :], 0.0), axis=0,
                                   keepdims=True)
                    sk = sk - jnp.where(head_row == hd, ps_all * drow, 0.0)
                sk_acc[...] += sk
                return ds, p, dos.astype(BF16), qs_ref[kvh, hf * half:(hf + 1) * half, :]

            def second(j, ds, p, dos, qs):
                dqs = _dot(ds, ku[j], NN)
                for pp, pr in enumerate(pairs[j]):
                    dq_ref[:, 128 * pr:128 * (pr + 1)] = rope_bwd(_unstack_heads(dqs, left, pp),
                                                                  cos_c, sin_c).astype(BF16)
                return _dot(ds, qs, TN), _dot(p, dos, TN)

            got, dku, dvu = {}, [None] * len(units), [None] * len(units)
            for j in range(len(units) + 1):
                if j < len(units):
                    got[j] = first(j)
                if j >= 1:
                    dku[j - 1], dvu[j - 1] = second(j - 1, *got.pop(j - 1))
            dkd = [dku[0] + dku[1], dku[2] + dku[3]]
            dvd = [dvu[0] + dvu[1], dvu[2] + dvu[3]]
            fold = lambda pr: jnp.where(left, pr[0] + pltpu.roll(pr[0], ATT_DIM, 1), pr[1] + pltpu.roll(pr[1], ATT_DIM, 1))
            dkw = fold(dkd)
            dvw = fold(dvd)
            prev = jnp.concatenate([rope_bwd(dkw[:ATT_BLOCK], cos_p, sin_p), dvw[:ATT_BLOCK]], axis=1)
            cur = jnp.concatenate([rope_bwd(dkw[ATT_BLOCK:], cos_c, sin_c), dvw[ATT_BLOCK:]], axis=1)
            dkv_ref[...] = (carry[...] + prev).astype(BF16)
            carry[...] = cur

        @pl.when(i == nb)
        def _():
            dkv_ref[...] = carry[...].astype(BF16)

        @pl.when((b_id == B - 1) & (i == nb))
        def _():
            diag = (lax.broadcasted_iota(jnp.int32, (ATT_HEADS, 128), 0)
                    == lax.broadcasted_iota(jnp.int32, (ATT_HEADS, 128), 1))
            tot = jnp.sum(sk_acc[...], axis=1, keepdims=True)
            dsk_ref[...] = jnp.sum(jnp.where(diag, tot, 0.0), axis=0, keepdims=True)

        _hosted_finish(phase, p_in, p_out, p_sems, (b_id == B - 1) & (i == nb))

    cl = lambda i: jnp.minimum(i, nb - 1)
    pv = lambda i: jnp.maximum(jnp.minimum(i, nb - 1) - 1, 0)
    rowblk = lambda w, cb: pl.BlockSpec((None, ATT_BLOCK, w), lambda b, i: (b, cl(i), cb))
    tab = pl.BlockSpec((ATT_BLOCK, 128), lambda b, i: (cl(i), 0))
    tabp = pl.BlockSpec((ATT_BLOCK, 128), lambda b, i: (pv(i), 0))
    p_ispecs, p_ospecs, p_oshapes, p_alias, p_scratch, p_args = _host_phase(phase, 15, 4)
    res = pl.pallas_call(
        body, name=name,
        grid=(B, nb + 1),
        in_specs=[pl.BlockSpec((None, None, 2, GROUP_ROWS, 128), lambda b, i: (b, cl(i), 0, 0, 0)), rowblk(256, 4),
                  pl.BlockSpec((None, ATT_BLOCK, 256), lambda b, i: (b, pv(i), 4)),
                  rowblk(256, _Z0), rowblk(256, _Z0 + 1), rowblk(256, _Z0 + 2), rowblk(256, _Z0 + 3),
                  rowblk(ATT_WIDTH, 1),
                  tab, tab, tabp, tabp,
                  pl.BlockSpec((None, None, 2, GROUP_ROWS, 2 * ATT_BLOCK), lambda b, i: (b, cl(i), 0, 0, 0)),
                  rowblk(ATT_WIDTH, 0),
                  pl.BlockSpec((None, None, ATT_HEADS, 128), lambda b, i: (b, cl(i), 0, 0))] + p_ispecs,
        out_specs=[rowblk(ATT_WIDTH, 0),
                   pl.BlockSpec((None, ATT_BLOCK, 256), lambda b, i: (b, jnp.maximum(i - 1, 0), 0)),
                   rowblk(ATT_WIDTH, 0),
                   pl.BlockSpec((1, 128), lambda b, i: (0, 0))] + p_ospecs,
        out_shape=[jax.ShapeDtypeStruct((B, S, ATT_WIDTH), BF16), jax.ShapeDtypeStruct((B, S, 256), BF16),
                   jax.ShapeDtypeStruct((B, S, ATT_WIDTH), BF16), jax.ShapeDtypeStruct((1, 128), F32)] + p_oshapes,
        input_output_aliases=p_alias,
        scratch_shapes=[pltpu.VMEM((ATT_BLOCK, 256), F32), pltpu.VMEM((ATT_HEADS, 128), F32)] + p_scratch,
        compiler_params=_params(("arbitrary", "arbitrary")),
    )(qs_kept, proj_a, proj_a, proj_a, proj_a, proj_a, proj_a, du, cos, sin, cos, sin, p_kept, o_kept, ps_kept, *p_args)
    return tuple(res[:4]) + (list(res[4:]),)


def _outproj_fwd(u2, w_out, x2, g_post, layer, target2, name):
    T, D = x2.shape
    tm = _pick(T, (512, 256, 128))
    last = target2 is not None

    def body(u_ref, w_ref, x_ref, g_ref, *rest):
        y = lax.dot_general(u_ref[...], w_ref[...], (NN, ((), ())), preferred_element_type=F32)
        r = lax.rsqrt(jnp.mean(y * y, axis=-1, keepdims=True) + NORM_EPS)
        xn = x_ref[...] + (y * r) * g_ref[layer:layer + 1, :]
        if last:
            t_ref, y_ref, dx_ref, loss_ref = rest
            err = xn - t_ref[...]
            dx_ref[...] = err * (1.0 / D)
            sq = err * err
            acc = sq[:, 0:128]
            for kk in range(1, D // 128):
                acc = acc + sq[:, 128 * kk:128 * (kk + 1)]
            part = jnp.sum(acc.reshape(tm // 8, 8, 128), axis=0) * (0.5 / D)

            @pl.when(pl.program_id(0) == 0)
            def _():
                loss_ref[...] = jnp.zeros_like(loss_ref)

            loss_ref[...] += part
        else:
            y_ref, xn_ref = rest
            xn_ref[...] = xn
        y_ref[...] = y

    row = pl.BlockSpec((tm, D), lambda i: (i, 0))
    in_specs = [pl.BlockSpec((tm, MIX_WIDTH), lambda i: (i, 0)),
                pl.BlockSpec((MIX_WIDTH, D), lambda i: (0, 0)), row,
                pl.BlockSpec((DEPTH, D), lambda i: (0, 0))]
    args = [u2, w_out, x2, g_post]
    out_specs = [row, row]
    out_shape = [jax.ShapeDtypeStruct((T, D), F32), jax.ShapeDtypeStruct((T, D), F32)]
    if last:
        in_specs.append(row)
        args.append(target2)
        out_specs.append(pl.BlockSpec((8, 128), lambda i: (0, 0)))
        out_shape.append(jax.ShapeDtypeStruct((8, 128), F32))
    return pl.pallas_call(
        body, name=name, grid=(T // tm,), in_specs=in_specs, out_specs=out_specs, out_shape=out_shape,
        compiler_params=_params(("arbitrary",)),
    )(*args)


def _outproj_bwd(dxn2, y2, g_post, layer, w_out, u2, name):
    T, D = y2.shape
    N = w_out.shape[0]
    tm = _pick(T, (512, 256, 128))
    nt = T // tm

    def body(dx_ref, y_ref, g_ref, w_ref, u_ref, dg_ref, du_ref, dw_ref, acc, wacc):
        i = pl.program_id(0)

        @pl.when(i == 0)
        def _():
            acc[...] = jnp.zeros_like(acc)
            wacc[...] = jnp.zeros_like(wacc)

        y = y_ref[...]
        dxn = dx_ref[...]
        r = lax.rsqrt(jnp.mean(y * y, axis=-1, keepdims=True) + NORM_EPS)
        n = y * r
        dn = dxn * g_ref[layer:layer + 1, :]
        dy = (r * (dn - n * jnp.mean(dn * n, axis=-1, keepdims=True))).astype(BF16)
        du_ref[...] = lax.dot_general(dy, w_ref[...], (NT, ((), ())), preferred_element_type=F32)
        wacc[...] += lax.dot_general(u_ref[...], dy, (TN, ((), ())), preferred_element_type=F32)
        acc[...] += jnp.sum((dxn * n).reshape(tm // 8, 8, D), axis=0)

        @pl.when(i == nt - 1)
        def _():
            dg_ref[...] = jnp.sum(acc[...], axis=0, keepdims=True)
            dw_ref[...] = wacc[...].astype(BF16)

    row = pl.BlockSpec((tm, D), lambda i: (i, 0))
    wide = pl.BlockSpec((tm, N), lambda i: (i, 0))
    vec = pl.BlockSpec((1, D), lambda i: (0, 0))
    whole = pl.BlockSpec((N, D), lambda i: (0, 0))
    return pl.pallas_call(
        body, name=name, grid=(nt,),
        in_specs=[row, row, pl.BlockSpec((DEPTH, D), lambda i: (0, 0)),
                  pl.BlockSpec((N, D), lambda i: (0, 0), pipeline_mode=pl.Buffered(1)), wide],
        out_specs=[vec, wide, whole],
        out_shape=[jax.ShapeDtypeStruct((1, D), F32), jax.ShapeDtypeStruct((T, N), F32),
                   jax.ShapeDtypeStruct((N, D), BF16)],
        scratch_shapes=[pltpu.VMEM((8, D), F32), pltpu.VMEM((N, D), F32)],
        compiler_params=_params(("arbitrary",)),
    )(dxn2, y2, g_post, w_out, u2)


def _inproj_bwd(pieces, w_t, x2, dxn2, g_pre, layer, name, phase=None):
    T, D = x2.shape
    widths = [p.shape[1] for p in pieces]
    offs = [sum(widths[:i]) for i in range(len(pieces))]
    n_p = len(pieces)
    tm = _pick(T, (256, 128))
    nt = T // tm

    def body(*refs):
        ins, (dx_ref, dg_ref), (acc,), p_in, p_out, p_sems = _split_refs(refs, n_p + 4, 2, 1, phase)
        w_ref, x_ref, dxn_ref, g_ref = ins[n_p:]
        i = pl.program_id(0)
        _hosted_start(phase, p_in, p_out, p_sems, i == 0)

        @pl.when(i == 0)
        def _():
            acc[...] = jnp.zeros_like(acc)

        dh = jnp.zeros((tm, D), F32)
        for p in range(n_p):
            dh = dh + lax.dot_general(ins[p][...], w_ref[offs[p]:offs[p] + widths[p], :], (NN, ((), ())),
                                      preferred_element_type=F32)
        x = x_ref[...]
        r = lax.rsqrt(jnp.mean(x * x, axis=-1, keepdims=True) + NORM_EPS)
        n = x * r
        dn = dh * g_ref[layer:layer + 1, :]
        dx_ref[...] = dxn_ref[...] + r * (dn - n * jnp.mean(dn * n, axis=-1, keepdims=True))
        acc[...] += jnp.sum((dh * n).reshape(tm // 8, 8, D), axis=0)

        @pl.when(i == nt - 1)
        def _():
            dg_ref[...] = jnp.sum(acc[...], axis=0, keepdims=True)

        _hosted_finish(phase, p_in, p_out, p_sems, i == nt - 1)

    row = pl.BlockSpec((tm, D), lambda i: (i, 0))
    vec = pl.BlockSpec((1, D), lambda i: (0, 0))
    p_ispecs, p_ospecs, p_oshapes, p_alias, p_scratch, p_args = _host_phase(phase, n_p + 4, 2)
    res = pl.pallas_call(
        body, name=name, grid=(nt,),
        in_specs=[pl.BlockSpec((tm, w), lambda i: (i, 0)) for w in widths]
        + [pl.BlockSpec((sum(widths), D), lambda i: (0, 0), pipeline_mode=pl.Buffered(1)), row, row,
           pl.BlockSpec((DEPTH, D), lambda i: (0, 0))] + p_ispecs,
        out_specs=[row, vec] + p_ospecs,
        out_shape=[jax.ShapeDtypeStruct((T, D), F32), jax.ShapeDtypeStruct((1, D), F32)] + p_oshapes,
        input_output_aliases=p_alias,
        scratch_shapes=[pltpu.VMEM((8, D), F32)] + p_scratch,
        compiler_params=_params(("arbitrary",)),
    )(*pieces, w_t, x2, dxn2, g_pre, *p_args)
    return res[0], res[1], list(res[2:])


def _step(x, target, g_pre, g_post, lb_param, g_head, sinks, shards=None, full=None):
    B, S, D = x.shape
    T = B * S
    dist = shards is not None
    first, last = 0, DEPTH - 1
    if dist:
        a_loc, b_loc = shards
        ra, rb = a_loc.shape[1], b_loc.shape[1]
        side = _own_side_blocks()
        a_full, b_full = _place_own([a_loc, b_loc], side, "place_own")
        w_in0 = _gather_one_call(a_full[0], "gather_in0")
        w_in, w_out = [w_in0, None], [None, None]
    else:
        w_in, w_out = list(full[0]), list(full[1])
    cos, sin = _rope_tables(S)
    sink_rows = _sink_rows(sinks)
    saved = []
    xs = x
    loss_part = None
    dxn = None
    for l in range(DEPTH):
        x2 = xs.reshape(T, D)
        proj_h, proj_a, h = _inproj(x2, g_pre, l, w_in[l], f"inproj{l}")
        proj_h = proj_h.reshape(B, S, N_H)
        proj_a = proj_a.reshape(B, S, N_A)
        phase = None
        if dist and l == first:
            phase = _gather_ici_phase([a_full[1], b_full[0]])
        if dist and l == last:
            phase = _gather_d2d_phase([w_out1_part], [rb])
        o_h, u, states, got = _hgrn_fwd(proj_h, MIX_WIDTH, lb_param, g_head, l, f"hgrn_fwd{l}", phase)
        phase = None
        if dist and l == first:
            phase = _merge_phases(_gather_d2d_phase(got, [ra, rb]),
                                  _gather_ici_phase([b_full[1]]))
        if dist and l == last:
            w_out[1] = got[0]
        u, kept_a, got = _attn_fwd(proj_a, u, sink_rows, l, cos, sin, f"attn_fwd{l}", phase)
        if dist and l == first:
            w_in[1], w_out[0], w_out1_part = got
        u2 = u.reshape(T, MIX_WIDTH)
        if l < last:
            y, xn = _outproj_fwd(u2, w_out[l], x2, g_post, l, None, f"outproj{l}")
            xn = xn.reshape(B, S, D)
        else:
            y, dxn, loss_part = _outproj_fwd(u2, w_out[l], x2, g_post, l, target.reshape(T, D), f"outproj{l}")
            xn = None
        saved.append((x2, h, proj_h, proj_a, o_h, u2, states, kept_a, y))
        xs = xn

    dw_in, dw_out = [None] * DEPTH, [None] * DEPTH
    dg_pre, dg_post, dlb, dg_head, dsinks = [], [], [], [], []
    for l in reversed(range(DEPTH)):
        x2, h, proj_h, proj_a, o_h, u2, states, kept_a, y = saved[l]
        dgp, du, dw_out[l] = _outproj_bwd(dxn, y, g_post, l, w_out[l], u2, f"outproj_bwd{l}")
        du = du.reshape(B, S, MIX_WIDTH)
        phase = None
        if dist:
            phase = _reduce_d2d_phase([dw_out[l]], [rb])
            if l == first:
                phase = _merge_phases(_reduce_ici_phase([part_in1]), phase)
        dqh, dfh, dih, dzh, dlb_l, dgh, got = _hgrn_bwd(
            proj_h, o_h, du, states, lb_param, g_head, l, f"hgrn_bwd{l}", phase)
        if dist:
            if l == first:
                sum_in = _chip_sum(part_in1, got[0], "chip_sum_in1", 1)
            part_out = _pair_sum(dw_out[l], got[-1], side, f"pair_sum_out{l}")
        dqa, dkv, dza, dsk, got = _attn_bwd(proj_a, du, kept_a, cos, sin, f"attn_bwd{l}",
                                            _reduce_ici_phase([part_out]) if dist else None)
        if dist:
            sum_out = _chip_sum(part_out, got[0], f"chip_sum_out{l}", l, None if l == last else sum_out)
        dproj = [p.reshape(T, p.shape[-1]) for p in (dqh, dfh, dih, dzh, dqa, dkv, dza)]
        dw_in[l] = _mm_tn(dproj, h, f"wgrad_in{l}")
        phase = None
        if dist and l == last:
            phase = _reduce_d2d_phase([dw_in[l]], [ra])
        if dist and l == first:
            got = _run_phase(_reduce_d2d_phase([dw_in[l]], [ra]), "reduce_in0_d2d")
            part_in0 = _pair_sum(dw_in[l], got[0], side, "pair_sum_in0")
            phase = _reduce_ici_phase([part_in0])
        dxn, dgpre, got = _inproj_bwd(dproj, w_in[l], x2, dxn, g_pre, l, f"inproj_bwd{l}", phase)
        if dist and l == last:
            part_in1 = _pair_sum(dw_in[l], got[0], side, "pair_sum_in1")
        if dist and l == first:
            sum_in = _chip_sum(part_in0, got[0], "chip_sum_in0", 0, sum_in)
        dg_pre.append(dgpre)
        dg_post.append(dgp)
        dlb.append(dlb_l)
        dg_head.append(dgh)
        dsinks.append(dsk)
    rev = lambda lst: jnp.concatenate(lst[::-1], axis=0)
    if not dist:
        sum_in, sum_out = jnp.stack(dw_in), jnp.stack(dw_out)
    return (loss_part, dxn.reshape(B, S, D), sum_in, sum_out,
            rev(dg_pre), rev(dg_post), rev(dlb), rev(dg_head), rev(dsinks))


def _me_and_peers():
    x, y, c = lax.axis_index("x"), lax.axis_index("y"), lax.axis_index("c")
    me = 4 * x + 2 * y + c
    peers = []
    for k in range(1, N_DEV):
        px = 1 - x if k & 4 else x
        py = 1 - y if k & 2 else y
        pc = 1 - c if k & 1 else c
        peers.append(((px, py, pc), 4 * px + 2 * py + pc))
    return me, peers


class _Phase:
    def __init__(self, arrays, out_shapes, aliases, n_send, build):
        self.arrays, self.out_shapes, self.aliases = list(arrays), list(out_shapes), dict(aliases)
        self.n_send, self.build = n_send, build

    def scratch(self):
        return [pltpu.SemaphoreType.DMA((self.n_send,)), pltpu.SemaphoreType.DMA((self.n_send,))]

    def _copies(self, in_refs, out_refs, sems, arrivals):
        send_sems, recv_sems = sems
        sends, recvs = self.build(in_refs, out_refs)
        assert len(sends) == self.n_send == len(recvs)
        out = [pltpu.make_async_remote_copy(src_ref=s, dst_ref=d, send_sem=send_sems.at[i], recv_sem=recv_sems.at[i],
                                            device_id=dev, device_id_type=MESH) for i, (s, d, dev) in enumerate(sends)]
        inc = [pltpu.make_async_remote_copy(src_ref=s, dst_ref=r, send_sem=send_sems.at[i], recv_sem=recv_sems.at[i],
                                            device_id=dev, device_id_type=MESH)
               for i, ((s, _, dev), r) in enumerate(zip(sends, recvs))] if arrivals else []
        return out, inc

    def start(self, in_refs, out_refs, sems):
        out, _ = self._copies(in_refs, out_refs, sems, False)
        for cp in out:
            cp.start()

    def finish(self, in_refs, out_refs, sems):
        out, inc = self._copies(in_refs, out_refs, sems, True)
        for cp in inc:
            cp.wait_recv()
        for cp in out:
            cp.wait_send()


_ANY = pl.BlockSpec(memory_space=pl.ANY)


def _host_phase(phase, n_in, n_out):
    if phase is None:
        return [], [], [], {}, [], []
    aliases = {n_in + i: n_out + o for i, o in phase.aliases.items()}
    return ([_ANY] * len(phase.arrays), [_ANY] * len(phase.out_shapes), phase.out_shapes, aliases, phase.scratch(),
            phase.arrays)


def _split_refs(refs, n_in, n_out, n_scr, phase):
    pi = len(phase.arrays) if phase else 0
    po = len(phase.out_shapes) if phase else 0
    a = n_in + pi
    b = a + n_out + po
    return (refs[:n_in], refs[a:a + n_out], refs[b:b + n_scr], refs[n_in:a], refs[a + n_out:b], refs[b + n_scr:])


def _hosted_start(phase, p_in, p_out, p_sems, first):
    if phase is not None:
        @pl.when(first)
        def _():
            phase.start(p_in, p_out, p_sems)


def _hosted_finish(phase, p_in, p_out, p_sems, last):
    if phase is not None:
        @pl.when(last)
        def _():
            phase.finish(p_in, p_out, p_sems)


def _run_phase(phase, name):
    n_in, n_out = len(phase.arrays), len(phase.out_shapes)

    def body(*refs):
        phase.start(refs[:n_in], refs[n_in:n_in + n_out], refs[n_in + n_out:])
        phase.finish(refs[:n_in], refs[n_in:n_in + n_out], refs[n_in + n_out:])

    return pl.pallas_call(
        body, name=name, in_specs=[_ANY] * n_in, out_specs=[_ANY] * n_out,
        out_shape=phase.out_shapes, input_output_aliases=phase.aliases, scratch_shapes=phase.scratch(),
        compiler_params=pltpu.CompilerParams(has_side_effects=True),
    )(*phase.arrays)


def _gather_one_call(full, name):
    r = full.shape[0] // N_DEV
    half = r // 2

    def body(full_in, full_ref, send_sems, recv_sems):
        del full_in
        c, (own, xn, yn, dg), num = _mesh_place()
        me, sib = num(own, c), (*own, 1 - c)

        def blk(dev, part=None):
            start, n = (dev * r, r) if part is None else (dev * r + part * half, half)
            return full_ref.at[pl.ds(pl.multiple_of(start, 16), n), :]

        def copy(k, src, dev, to, part=None):
            return pltpu.make_async_remote_copy(src_ref=src, dst_ref=blk(dev, part),
                                                send_sem=send_sems.at[k], recv_sem=recv_sems.at[k],
                                                device_id=to, device_id_type=MESH)

        def landed(k, dev, part=None):
            copy(k, blk(dev, part), dev, sib, part).wait_recv()

        sent = []

        def start(*cps):
            for cp in cps:
                cp.start()
                sent.append(cp)

        xs, ys, ds = num(xn, c), num(yn, c), num(dg, c)
        start(copy(0, blk(me), me, sib), copy(1, blk(me), me, (*xn, c)), copy(2, blk(me), me, (*yn, c)))
        landed(1, xs)
        start(copy(3, blk(xs, 0), xs, (*yn, c), 0), copy(5, blk(xs), xs, sib))
        landed(2, ys)
        start(copy(4, blk(ys, 1), ys, (*xn, c), 1), copy(6, blk(ys), ys, sib))
        landed(3, ds, 0)
        landed(4, ds, 1)
        start(copy(7, blk(ds), ds, sib))
        landed(0, num(own, 1 - c))
        for k, ch in ((5, xn), (6, yn), (7, dg)):
            landed(k, num(ch, 1 - c))
        for cp in sent:
            cp.wait_send()

    assert half % 16 == 0
    return pl.pallas_call(
        body, name=name, in_specs=[_ANY], out_specs=_ANY,
        out_shape=jax.ShapeDtypeStruct(full.shape, full.dtype), input_output_aliases={0: 0},
        scratch_shapes=[pltpu.SemaphoreType.DMA((8,)), pltpu.SemaphoreType.DMA((8,))],
        compiler_params=pltpu.CompilerParams(has_side_effects=True),
    )(full)


def _merge_phases(a, b):
    n_in, n_out = len(a.arrays), len(a.out_shapes)
    aliases = dict(a.aliases)
    aliases.update({n_in + i: n_out + o for i, o in b.aliases.items()})

    def build(ins, outs):
        sa, ra = a.build(ins[:n_in], outs[:n_out])
        sb, rb = b.build(ins[n_in:], outs[n_out:])
        return sa + sb, ra + rb

    return _Phase(a.arrays + b.arrays, a.out_shapes + b.out_shapes, aliases, a.n_send + b.n_send, build)


def _mesh_place():
    x, y, c = lax.axis_index("x"), lax.axis_index("y"), lax.axis_index("c")
    chips = [(x, y), (1 - x, y), (x, 1 - y), (1 - x, 1 - y)]
    num = lambda chip, core: 4 * chip[0] + 2 * chip[1] + core
    return c, chips, num


def _own_side_blocks():
    c, chips, num = _mesh_place()
    return jnp.stack([num(ch, c) for ch in chips]).astype(jnp.int32)


def _rows(ref, r, dev):
    return ref.at[pl.ds(pl.multiple_of(dev * r, 16), r), :]


def _place_own(shards, blocks, name):
    n = len(shards)

    def body(idx_ref, *refs):
        del idx_ref
        outs = iter(refs[n:])
        for s_ref in refs[:n]:
            for l in range(DEPTH):
                next(outs)[...] = s_ref[l].astype(BF16)

    whole = lambda s: pl.BlockSpec(s.shape, lambda i, idx: (0, 0, 0))
    own = lambda s: pl.BlockSpec(s.shape[1:], lambda i, idx: (idx[0], 0))
    res = pl.pallas_call(
        body, name=name,
        grid_spec=pltpu.PrefetchScalarGridSpec(
            num_scalar_prefetch=1, grid=(1,),
            in_specs=[whole(s) for s in shards],
            out_specs=[own(s) for s in shards for _ in range(DEPTH)]),
        out_shape=[jax.ShapeDtypeStruct((N_DEV * s.shape[1], s.shape[2]), BF16) for s in shards for _ in range(DEPTH)],
        compiler_params=_params(("arbitrary",)),
    )(blocks, *shards)
    return [list(res[i * DEPTH:(i + 1) * DEPTH]) for i in range(n)]


def _gather_ici_phase(fulls):
    rs = [a.shape[0] // N_DEV for a in fulls]
    n = len(fulls)

    def build(ins, outs):
        del ins
        c, chips, num = _mesh_place()
        me = num(chips[0], c)
        targets = [((*chips[0], 1 - c), num(chips[0], 1 - c))] + [((*ch, c), num(ch, c)) for ch in chips[1:]]
        sends, recvs = [], []
        for dev, dnum in targets:
            for i, r in enumerate(rs):
                sends.append((_rows(outs[i], r, me), _rows(outs[i], r, me), dev))
                recvs.append(_rows(outs[i], r, dnum))
        return sends, recvs

    shapes = [jax.ShapeDtypeStruct(a.shape, a.dtype) for a in fulls]
    return _Phase(list(fulls), shapes, {i: i for i in range(n)}, 4 * n, build)


def _gather_d2d_phase(fulls, rs):
    def build(ins, outs):
        c, chips, num = _mesh_place()
        sib = (*chips[0], 1 - c)
        sends, recvs = [], []
        for ch in chips[1:]:
            for i, r in enumerate(rs):
                blk = _rows(outs[i], r, num(ch, c))
                sends.append((blk, blk, sib))
                recvs.append(_rows(outs[i], r, num(ch, 1 - c)))
        return sends, recvs

    shapes = [jax.ShapeDtypeStruct(a.shape, a.dtype) for a in fulls]
    return _Phase(fulls, shapes, {i: i for i in range(len(fulls))}, 3 * len(fulls), build)


def _reduce_d2d_phase(grads, rs):
    def build(ins, outs):
        c, chips, num = _mesh_place()
        sib = (*chips[0], 1 - c)
        sends, recvs = [], []
        for j, ch in enumerate(chips):
            for i, r in enumerate(rs):
                sends.append((_rows(ins[i], r, num(ch, 1 - c)), outs[i].at[j], sib))
                recvs.append(outs[i].at[j])
        return sends, recvs

    shapes = [jax.ShapeDtypeStruct((4, r, g.shape[1]), g.dtype) for g, r in zip(grads, rs)]
    return _Phase(grads, shapes, {}, 4 * len(grads), build)


def _reduce_ici_phase(parts):
    def build(ins, outs):
        c, chips, _ = _mesh_place()
        sends, recvs = [], []
        for t in range(1, 4):
            for i in range(len(parts)):
                sends.append((ins[i].at[t], outs[i].at[t - 1], (*chips[t], c)))
                recvs.append(outs[i].at[t - 1])
        return sends, recvs

    shapes = [jax.ShapeDtypeStruct((3,) + p.shape[1:], p.dtype) for p in parts]
    return _Phase(parts, shapes, {}, 3 * len(parts), build)


def _pair_sum(g, got, blocks, name):
    n, r, D = got.shape
    tr = _pick(r, (800, 400, 256, 200, 128, 64, 16))

    def body(idx_ref, g_ref, r_ref, o_ref):
        del idx_ref
        o_ref[...] = (g_ref[...].astype(F32) + r_ref[...].astype(F32)).astype(o_ref.dtype)

    blk = pl.BlockSpec((None, tr, D), lambda j, i, idx: (j, i, 0))
    return pl.pallas_call(
        body, name=name,
        grid_spec=pltpu.PrefetchScalarGridSpec(
            num_scalar_prefetch=1, grid=(n, r // tr),
            in_specs=[pl.BlockSpec((tr, D), lambda j, i, idx: (idx[j] * (r // tr) + i, 0)), blk],
            out_specs=blk),
        out_shape=jax.ShapeDtypeStruct(got.shape, got.dtype),
        compiler_params=_params(("arbitrary", "arbitrary")),
    )(blocks, g, got)


def _chip_sum(p, r, name, layer, into=None):
    _, R, D = p.shape
    tr = _pick(R, (800, 400, 256, 200, 128, 64, 16))

    def body(p_ref, r_ref, *rest):
        acc = p_ref[...].astype(F32)
        for t in range(3):
            acc = acc + r_ref[t].astype(F32)
        rest[-1][...] = acc

    args = [p, r] + ([] if into is None else [into])
    return pl.pallas_call(
        body, name=name, grid=(R // tr,),
        in_specs=[pl.BlockSpec((None, tr, D), lambda i: (0, i, 0)), pl.BlockSpec((3, tr, D), lambda i: (0, i, 0))]
        + ([] if into is None else [_ANY]),
        out_specs=pl.BlockSpec((None, tr, D), lambda i: (layer, i, 0)),
        out_shape=jax.ShapeDtypeStruct((DEPTH, R, D), F32),
        input_output_aliases={} if into is None else {2: 0},
        compiler_params=_params(("parallel",)))(*args)


def _allreduce_small(vec):
    R, C = vec.shape

    def body(v_ref, o_ref, buf, send_sems, recv_sems):
        me, peers = _me_and_peers()
        buf[me] = v_ref[...]
        sends = []
        for k, (pid, _) in enumerate(peers):
            cp = pltpu.make_async_remote_copy(src_ref=v_ref, dst_ref=buf.at[me], send_sem=send_sems.at[k],
                                              recv_sem=recv_sems.at[k], device_id=pid, device_id_type=MESH)
            cp.start()
            sends.append(cp)
        for k, (pid, pnum) in enumerate(peers):
            pltpu.make_async_remote_copy(src_ref=v_ref, dst_ref=buf.at[pnum], send_sem=send_sems.at[k],
                                         recv_sem=recv_sems.at[k], device_id=pid, device_id_type=MESH).wait_recv()
        for cp in sends:
            cp.wait_send()
        acc = buf[0]
        for d in range(1, N_DEV):
            acc = acc + buf[d]
        o_ref[...] = acc

    vm = pl.BlockSpec(memory_space=pltpu.VMEM)
    return pl.pallas_call(
        body, name="allreduce_small",
        in_specs=[vm], out_specs=vm,
        out_shape=jax.ShapeDtypeStruct((R, C), F32),
        scratch_shapes=[pltpu.VMEM((N_DEV, R, C), F32), pltpu.SemaphoreType.DMA((N_DEV - 1,)),
                        pltpu.SemaphoreType.DMA((N_DEV - 1,))],
        compiler_params=pltpu.CompilerParams(has_side_effects=True),
    )(vec)


def _adamw_update(w, g, m, v):
    c1 = 1.0 - ADAM_B1 ** ADAM_STEP
    c2 = 1.0 - ADAM_B2 ** ADAM_STEP
    mn = ADAM_B1 * m + (1.0 - ADAM_B1) * g
    vn = ADAM_B2 * v + (1.0 - ADAM_B2) * (g * g)
    return -ADAM_LR * ((mn / c1) / (jnp.sqrt(vn / c2) + ADAM_EPS) + ADAM_WD * w), mn, vn


def _adamw(w, g, m, v, name):
    R, C = w.shape
    tr = _pick(R, (512, 400, 256, 128, 64, 32, 16, 8))

    def body(w_ref, g_ref, m_ref, v_ref, d_ref, mo_ref, vo_ref):
        d_ref[...], mo_ref[...], vo_ref[...] = _adamw_update(w_ref[...], g_ref[...], m_ref[...], v_ref[...])

    blk = pl.BlockSpec((tr, C), lambda i: (i, 0))
    sh = jax.ShapeDtypeStruct((R, C), F32)
    return pl.pallas_call(
        body, name=name, grid=(R // tr,), in_specs=[blk] * 4, out_specs=[blk] * 3, out_shape=[sh] * 3,
        compiler_params=_params(("parallel",)),
    )(w, g, m, v)


def _adamw_whole(ws, gs, ms, vs, name):
    n = len(ws)

    def body(*refs):
        for j in range(n):
            outs = refs[4 * n + 3 * j:4 * n + 3 * j + 3]
            outs[0][...], outs[1][...], outs[2][...] = _adamw_update(*(r[...] for r in refs[4 * j:4 * j + 4]))

    vm = pl.BlockSpec(memory_space=pltpu.VMEM)
    res = pl.pallas_call(
        body, name=name, in_specs=[vm] * (4 * n), out_specs=[vm] * (3 * n),
        out_shape=[jax.ShapeDtypeStruct(w.shape, F32) for w in ws for _ in range(3)],
    )(*[a for four in zip(ws, gs, ms, vs) for a in four])
    return [tuple(res[3 * j:3 * j + 3]) for j in range(n)]


def _lb_param_grad(lb_param, dlb):
    L, C = lb_param.shape

    def body(p_ref, d_ref, o_ref):
        lbp = p_ref[...]
        d = d_ref[...]
        mx = jnp.max(lbp, axis=0, keepdims=True)
        e = jnp.exp(lbp - mx)
        p = e / jnp.sum(e, axis=0, keepdims=True)
        tot = jnp.sum(d, axis=0, keepdims=True)
        dps = []
        rest = tot
        for j in range(L):
            dps.append(rest - tot if j == 0 else rest)
            rest = rest - d[j:j + 1]
        dp = jnp.concatenate(dps, axis=0)
        o_ref[...] = p * (dp - jnp.sum(p * dp, axis=0, keepdims=True))

    vm = pl.BlockSpec(memory_space=pltpu.VMEM)
    return pl.pallas_call(body, name="lb_param_grad", in_specs=[vm, vm], out_specs=vm,
                          out_shape=jax.ShapeDtypeStruct((L, C), F32))(lb_param, dlb)


def _pack_small(loss_part, dg_pre, dg_post, dlb, dg_head, dsinks):
    pad8 = lambda a: jnp.pad(a.reshape(-1, 128), ((0, 8 - DEPTH), (0, 0)))
    rows = [dg_pre.reshape(-1, 128), dg_post.reshape(-1, 128), dlb.reshape(-1, 128), pad8(dg_head), pad8(dsinks),
            loss_part]
    return jnp.concatenate(rows, axis=0)


def _unpack_small(vec):
    n = DEPTH * D_MODEL // 128
    o = 0
    dg_pre = vec[o:o + n].reshape(DEPTH, D_MODEL); o += n
    dg_post = vec[o:o + n].reshape(DEPTH, D_MODEL); o += n
    dlb = vec[o:o + n].reshape(DEPTH, HG_WIDTH); o += n
    dg_head = vec[o:o + DEPTH]; o += 8
    dsinks = vec[o:o + DEPTH, :ATT_HEADS]; o += 8
    loss = jnp.sum(vec[o:o + 8])
    return loss, dg_pre, dg_post, dlb, dg_head, dsinks


def kernel(x, w_in, w_out, g_pre, g_post, lb_param, g_head, sinks, loss_target, m_w_in, m_w_out, m_g_pre, m_g_post, m_lb_param, m_g_head, m_sinks, v_w_in, v_w_out, v_g_pre, v_g_post, v_lb_param, v_g_head, v_sinks):
    tr = lambda a: jnp.swapaxes(a, 1, 2)
    w_in_t = tr(w_in)
    (loss_part, dx, gw_in_t, gw_out, dg_pre, dg_post, dlb, dg_head, dsinks) = _step(
        x, loss_target, g_pre, g_post, lb_param, g_head, sinks, shards=(w_in_t, w_out))

    small = _allreduce_small(_pack_small(loss_part, dg_pre, dg_post, dlb, dg_head, dsinks))
    loss, gg_pre, gg_post, gdlb, gg_head, gsinks = _unpack_small(small)
    glb = _lb_param_grad(lb_param, gdlb)

    grads = [gw_in_t, gw_out, gg_pre, gg_post, glb, gg_head, gsinks]
    ws = [w_in_t, w_out, g_pre, g_post, lb_param, g_head, sinks]
    ms = [tr(m_w_in), m_w_out, m_g_pre, m_g_post, m_lb_param, m_g_head, m_sinks]
    vs = [tr(v_w_in), v_w_out, v_g_pre, v_g_post, v_lb_param, v_g_head, v_sinks]
    deltas, new_m, new_v = [], [], []
    big = 2
    for w, g, m, v, nm in zip(ws[:big], grads, ms, vs, ("w_in", "w_out")):
        sh = w.shape
        two = lambda a: a.reshape(-1, sh[-1])
        d, mn, vn = _adamw(two(w), two(g), two(m), two(v), "adamw_" + nm)
        deltas.append(d.reshape(sh))
        new_m.append(mn.reshape(sh))
        new_v.append(vn.reshape(sh))
    for d, mn, vn in _adamw_whole(ws[big:], grads[big:], ms[big:], vs[big:], "adamw_vectors"):
        deltas.append(d)
        new_m.append(mn)
        new_v.append(vn)
    grads[0], deltas[0], new_m[0], new_v[0] = tr(grads[0]), tr(deltas[0]), tr(new_m[0]), tr(new_v[0])
    return (loss, dx, *grads, *deltas, *new_m, *new_v)
```

```python
import math

import numpy as np
import jax
import jax.numpy as jnp
from jax import lax
from jax.experimental import pallas as pl
from jax.experimental.pallas import tpu as pltpu

F32 = jnp.float32
BF16 = jnp.bfloat16

D_MODEL = 1024
DEPTH = 2
HG_HEADS = 8
HG_DIM = 128
HG_WIDTH = HG_HEADS * HG_DIM
CHUNK = 64
ATT_HEADS = 16
ATT_DIM = 64
ATT_WIDTH = ATT_HEADS * ATT_DIM
KV_WIDTH = 128
ATT_BLOCK = 128
ATT_SCALE = 1.0 / math.sqrt(ATT_DIM)
ROPE_THETA = 10000.0
NORM_EPS = 1e-6
NEG_INF = -1e30
LB_FLOOR = 1e-20
N_H = 4 * HG_WIDTH
N_A = 2 * ATT_WIDTH + 2 * KV_WIDTH
IN_WIDTH = N_H + N_A
MIX_WIDTH = HG_WIDTH + ATT_WIDTH

ADAM_LR = 0.001
ADAM_B1 = 0.9
ADAM_B2 = 0.999
ADAM_EPS = 1e-08
ADAM_WD = 0.01
ADAM_STEP = 10

N_DEV = 8
MESH = pl.DeviceIdType.MESH
VMEM_LIMIT = 56 * 1024 * 1024

NN = ((1,), (0,))
NT = ((1,), (1,))
TN = ((0,), (0,))


def _dot(a, b, dims):
    return lax.dot_general(a.astype(BF16), b.astype(BF16), (dims, ((), ())), preferred_element_type=F32)


def _params(sem=None, **kw):
    return pltpu.CompilerParams(dimension_semantics=sem, vmem_limit_bytes=VMEM_LIMIT, **kw)


def _sigmoids(x):
    e = jnp.exp(-jnp.abs(x))
    r = 1.0 / (1.0 + e)
    er = e * r
    pos = x >= 0.0
    return jnp.where(pos, r, er), jnp.where(pos, er, r)


def _silu(x):
    return x * _sigmoids(x)[0]


def _silu_and_grad(x):
    s, ns = _sigmoids(x)
    return x * s, s * (1.0 + x * ns)


def _pick(n, prefs):
    for p in prefs:
        if n % p == 0:
            return p
    return n


def _inproj(x2, g, layer, w, name):
    T, D = x2.shape
    tm = _pick(T, (512, 256, 128))
    nchunk = 1024

    def body(x_ref, g_ref, w_ref, oh_ref, oa_ref, h_ref):
        x = x_ref[...]
        r = lax.rsqrt(jnp.mean(x * x, axis=-1, keepdims=True) + NORM_EPS)
        h = ((x * r) * g_ref[layer:layer + 1, :]).astype(BF16)
        h_ref[...] = h
        for j in range(0, N_H, nchunk):
            oh_ref[:, j:j + nchunk] = lax.dot_general(h, w_ref[j:j + nchunk, :], (NT, ((), ())),
                                                      preferred_element_type=F32)
        for j in range(0, N_A, N_A // 2):
            oa_ref[:, j:j + N_A // 2] = lax.dot_general(h, w_ref[N_H + j:N_H + j + N_A // 2, :], (NT, ((), ())),
                                                        preferred_element_type=F32)

    row = lambda w_: pl.BlockSpec((tm, w_), lambda i: (i, 0))
    return pl.pallas_call(
        body, name=name,
        grid=(T // tm,),
        in_specs=[row(D), pl.BlockSpec((DEPTH, D), lambda i: (0, 0)),
                  pl.BlockSpec((IN_WIDTH, D), lambda i: (0, 0), pipeline_mode=pl.Buffered(1))],
        out_specs=[row(N_H), row(N_A), row(D)],
        out_shape=[jax.ShapeDtypeStruct((T, N_H), F32), jax.ShapeDtypeStruct((T, N_A), F32),
                   jax.ShapeDtypeStruct((T, D), BF16)],
        compiler_params=_params(("parallel",)),
    )(x2, g, w)


def _mm_tn(pieces, b, name, out_dtype=BF16):
    T, m = b.shape
    tn = 256
    slots = 3
    counts = [p.shape[1] // tn for p in pieces]
    starts = [sum(counts[:i]) for i in range(len(pieces))]
    n_p = len(pieces)
    n = sum(counts)
    assert n >= slots

    def body(*refs):
        b_ref, o_ref, ring, sems = refs[n_p:]
        i = pl.program_id(0)

        def tile_copy(p, col, slot):
            return pltpu.make_async_copy(refs[p].at[:, pl.ds(col, tn)], ring.at[slot], sems.at[slot])

        def fetch(t):
            for p in range(n_p):
                @pl.when((t >= starts[p]) & (t < starts[p] + counts[p]))
                def _(p=p):
                    tile_copy(p, pl.multiple_of((t - starts[p]) * tn, tn), t % slots).start()

        @pl.when(i == 0)
        def _():
            for t in range(slots - 1):
                fetch(jnp.int32(t))

        @pl.when(i + (slots - 1) < n)
        def _():
            fetch(i + (slots - 1))

        slot = i % slots
        tile_copy(0, 0, slot).wait()
        o_ref[...] = lax.dot_general(ring[slot], b_ref[...], (TN, ((), ())),
                                     preferred_element_type=F32).astype(out_dtype)

    return pl.pallas_call(
        body, name=name,
        grid=(n,),
        in_specs=[_ANY] * n_p + [pl.BlockSpec((T, m), lambda i: (0, 0), pipeline_mode=pl.Buffered(1))],
        out_specs=pl.BlockSpec((tn, m), lambda i: (i, 0)),
        out_shape=jax.ShapeDtypeStruct((n * tn, m), out_dtype),
        scratch_shapes=[pltpu.VMEM((slots, T, tn), BF16), pltpu.SemaphoreType.DMA((slots,))],
        compiler_params=_params(("arbitrary",)),
    )(*pieces, b)


_LEVELS = (0, 1, 2, 4, 8, 16, 32)
_CUM_L = (2, 4, 8, 16, 32, 64)
_ALL_KINDS = tuple(("c", L) for L in _CUM_L) + tuple(("r", L) for L in _CUM_L)
_MXU_KINDS = (("c", 2), ("c", 4), ("c", CHUNK), ("r", 2), ("r", 4))
N_CUM = len(_ALL_KINDS) * CHUNK
N_CUM_F = len(_MXU_KINDS) * CHUNK


def _cum_matrices():
    t = np.arange(CHUNK)[:, None]
    r = np.arange(CHUNK)[None, :]

    def mat(kind):
        c, L = kind
        return ((r // L == t // L) & ((r <= t) if c == "c" else (r > t))).astype(np.float32)

    fwd = np.concatenate([mat(kd) for kd in _MXU_KINDS], axis=0)
    full = np.concatenate([mat(kd) for kd in _ALL_KINDS], axis=0)
    return jnp.asarray(fwd, BF16), jnp.asarray(full.T.copy(), BF16)


def _level_masks():
    t = np.arange(CHUNK)[:, None]
    s = np.arange(CHUNK)[None, :]
    ms = []
    for L in _LEVELS:
        if L == 0:
            ms.append(t == s)
        else:
            ms.append((t // (2 * L) == s // (2 * L)) & ((t // L) % 2 == 1) & ((s // L) % 2 == 0))
    return jnp.asarray(np.stack(ms).astype(np.float32))


def _split3(x):
    hi = x.astype(BF16)
    r1 = x - hi.astype(F32)
    mid = r1.astype(BF16)
    lo = (r1 - mid.astype(F32)).astype(BF16)
    return hi, mid, lo


def _cum3(ts, x, terms=3):
    d = lambda p: lax.dot_general(ts, p, (NN, ((), ())), preferred_element_type=F32)
    return sum(d(p) for p in _split3(x)[:terms])


def _lb_terms(lbp, layer):
    mx = jnp.max(lbp, axis=0, keepdims=True)
    e = jnp.exp(lbp - mx)
    p = e / jnp.sum(e, axis=0, keepdims=True)
    cum = p[0:1]
    for j in range(1, layer + 1):
        cum = cum + p[j:j + 1]
    lb = cum - p[0:1]
    lbf = jnp.maximum(lb, LB_FLOOR)
    return dict(lbf=lbf, one_m=1.0 - lb, kcorr=lb - lbf, ind=jnp.where(lb > LB_FLOOR, 1.0, 0.0))


def _gate(x, lt):
    sig, nsig = _sigmoids(x)
    f = lt["lbf"] + lt["one_m"] * sig
    return jnp.log(f), lt["one_m"] * nsig + lt["kcorr"], f, sig, nsig


def _ck(x, ci):
    return x[ci * CHUNK:(ci + 1) * CHUNK]


def _block_cums(ts, g, nc):
    cs = [_cum3(ts, _ck(g, ci), terms=2) for ci in range(nc)]
    out = {kind: jnp.concatenate([c[CHUNK * i:CHUNK * (i + 1)] for c in cs], axis=0)
           for i, kind in enumerate(_MXU_KINDS)}
    b = out[("c", CHUNK)]
    ng = CHUNK // 8
    last = b.reshape(nc, ng, 8, HG_DIM)[:, :, 7:8, :]
    zero = jnp.zeros((nc, 1, 1, HG_DIM), F32)

    def spread(groups):
        return jnp.broadcast_to(jnp.concatenate(groups, axis=1), (nc, ng, 8, HG_DIM)).reshape(nc * CHUNK, HG_DIM)

    def get(kind):
        if kind in out:
            return out[kind]
        c, L = kind
        nb = L // 8
        first = lambda r: (r // nb) * nb
        if c == "c":
            return b - spread([last[:, first(r) - 1:first(r)] if r >= nb else zero for r in range(ng)])
        return spread([last[:, first(r) + nb - 1:first(r) + nb] for r in range(ng)]) - b

    return get


def _level_factors(cums, g, L):
    if L == 0:
        return None, None
    if L == 1:
        return jnp.exp(g[...]), None
    return jnp.exp(cums(("c", L))), jnp.exp(cums(("r", L)))


def _mul(a, e):
    return a if e is None else a * e


def _hg_block_fwd(qf, k, v, g, ts, m_ref, nc):
    cums = _block_cums(ts, g, nc)
    amat = [jnp.zeros((CHUNK, CHUNK), F32)] * nc
    for li, L in enumerate(_LEVELS):
        eq, ek = _level_factors(cums, g, L)
        ql, kl, m = _mul(qf, eq), _mul(k, ek), m_ref[li]
        amat = [amat[ci] + _dot(_ck(ql, ci), _ck(kl, ci), NT) * m for ci in range(nc)]
    b = cums(("c", CHUNK))
    kst = k * jnp.exp(cums(("r", CHUNK)))
    o = [_dot(amat[ci], _ck(v, ci), NN) for ci in range(nc)]
    kv = [_dot(_ck(v, ci), _ck(kst, ci), TN) for ci in range(nc)]
    dec = [jnp.exp(b[(ci + 1) * CHUNK - 1:(ci + 1) * CHUNK, :]) for ci in range(nc)]
    return o, dec, kv, qf * jnp.exp(b), amat


def _hg_block_bwd(qf, k, v, g, do, amat, ts, m_ref, nc):
    cums = _block_cums(ts, g, nc)
    dcs = {}
    da = [_dot(_ck(do, ci), _ck(v, ci), NT) for ci in range(nc)]
    dq = jnp.zeros(qf.shape, F32)
    dk = jnp.zeros(qf.shape, F32)
    dg = jnp.zeros(qf.shape, F32)
    for li, L in enumerate(_LEVELS):
        eq, ek = _level_factors(cums, g, L)
        qlb, klb, m = _mul(qf[...], eq).astype(BF16), _mul(k[...], ek).astype(BF16), m_ref[li]
        dal = [(da[ci] * m).astype(BF16) for ci in range(nc)]
        both = [(_dot(dal[ci], _ck(klb, ci), NN), _dot(dal[ci], _ck(qlb, ci), TN)) for ci in range(nc)]
        dql = _mul(jnp.concatenate([p[0] for p in both], axis=0), eq)
        dkl = _mul(jnp.concatenate([p[1] for p in both], axis=0), ek)
        dq = dq + dql
        dk = dk + dkl
        if L == 1:
            dg = dg + dql * qf[...]
        elif L > 1:
            dcs[("c", L)] = (dql * qf[...]).astype(BF16)
            dcs[("r", L)] = (dkl * k[...]).astype(BF16)
    b = cums(("c", CHUNK))
    e64 = jnp.exp(b)
    er64 = jnp.exp(cums(("r", CHUNK)))
    qb = (qf[...] * e64).astype(BF16)
    return dict(dq=dq, dk=dk, dg=dg, dcs=dcs, e64=e64, er64=er64, qf=qf, k=k, kst=(k[...] * er64).astype(BF16),
                dv=[_dot(amat[ci], _ck(do, ci), TN) for ci in range(nc)],
                dec=[jnp.exp(b[(ci + 1) * CHUNK - 1:(ci + 1) * CHUNK, :]) for ci in range(nc)],
                qd=[_dot(_ck(do, ci), _ck(qb, ci), TN) for ci in range(nc)])


def _hg_state_bwd(w, v, do, starts, ends, tst, nc):
    dqb = jnp.concatenate([_dot(_ck(do, ci), starts[ci], NN) for ci in range(nc)], axis=0)
    dkst = jnp.concatenate([_dot(_ck(v, ci), ends[ci], NN) for ci in range(nc)], axis=0)
    dqb, dkst = dqb * w["e64"], dkst * w["er64"]
    dq = w["dq"] + dqb
    dk = w["dk"] + dkst
    dv = jnp.concatenate([w["dv"][ci] + _dot(_ck(w["kst"], ci), ends[ci], NT) for ci in range(nc)], axis=0)
    trow = lax.broadcasted_iota(jnp.int32, (CHUNK, 1), 0)
    dtot = jnp.concatenate(
        [jnp.where(trow == CHUNK - 1, jnp.sum(ends[ci] * starts[ci], axis=0, keepdims=True) * w["dec"][ci], 0.0)
         for ci in range(nc)], axis=0)
    dcs = dict(w["dcs"])
    dcs[("c", CHUNK)] = (dqb * w["qf"][...] + dtot).astype(BF16)
    dcs[("r", CHUNK)] = (dkst * w["k"][...]).astype(BF16)
    dgs = [_dot(tst, jnp.concatenate([_ck(dcs[kind], ci) for kind in _ALL_KINDS], axis=0), NN) for ci in range(nc)]
    return dq, dk, dv, w["dg"] + jnp.concatenate(dgs, axis=0)


def _hgrn_fwd(proj_h, u_rows, lb_param, g_head, layer, name, phase=None):
    B, S, _ = proj_h.shape
    sb = _pick(S, (2048, 1024, 512, 256, 128, 64))
    nc = sb // CHUNK
    ts, _ = _cum_matrices()

    def body(*refs):
        ins, outs, (st,), p_in, p_out, p_sems = _split_refs(refs, 8, 12, 1, phase)
        q_ref, f_ref, i_ref, z_ref, lbp_ref, gh_ref, ts_ref, m_ref = ins
        o_ref, u_ref, sts_ref, am_ref = outs[:4]
        logf_ref, k_ref, qf_ref, sg_ref, qg_ref, zg_ref, fg_ref, sig_ref = outs[4:]
        h_id, b_id, s_id = pl.program_id(0), pl.program_id(1), pl.program_id(2)
        _hosted_start(phase, p_in, p_out, p_sems, (h_id == 0) & (b_id == 0) & (s_id == 0))

        @pl.when(s_id == 0)
        def _():
            st[...] = jnp.zeros_like(st)

        lt = _lb_terms(lbp_ref[...], layer)
        tsv = ts_ref[...]
        gh = gh_ref[layer:layer + 1, :]
        logf, k, _, sig, nsig = _gate(f_ref[...], lt)
        qf, qf_grad = _silu_and_grad(q_ref[...])
        sg, sg_grad = _silu_and_grad(z_ref[...])
        logf_ref[...], k_ref[...], qf_ref[...], sg_ref[...] = logf, k, qf, sg
        qg_ref[...] = qf_grad.astype(BF16)
        zg_ref[...] = sg_grad.astype(BF16)
        fg_ref[...] = (lt["one_m"] * sig * nsig).astype(BF16)
        sig_ref[...] = sig.astype(BF16)
        o_part, dec, kv, qb, amat = _hg_block_fwd(qf, k, i_ref[...], logf, tsv, m_ref, nc)
        for ci in range(nc):
            am_ref[ci] = amat[ci].astype(BF16)
        cur = st[...]
        starts = []
        for ci in range(nc):
            sts_ref[ci] = cur
            starts.append(cur)
            cur = cur * dec[ci] + kv[ci]
        st[...] = cur
        o = jnp.concatenate([o_part[ci] + _dot(_ck(qb, ci), starts[ci], NT) for ci in range(nc)], axis=0)
        o_ref[...] = o
        r = lax.rsqrt(jnp.mean(o * o, axis=-1, keepdims=True) + NORM_EPS)
        u_ref[...] = (((o * r) * gh) * sg).astype(BF16)
        _hosted_finish(phase, p_in, p_out, p_sems, (h_id == HG_HEADS - 1) & (b_id == B - 1) & (s_id == S // sb - 1))

    col = lambda base: pl.BlockSpec((None, sb, HG_DIM), lambda h, b, s: (b, s, base + h))
    p_ispecs, p_ospecs, p_oshapes, p_alias, p_scratch, p_args = _host_phase(phase, 8, 12)
    wide = lambda dt: jax.ShapeDtypeStruct((B, S, HG_WIDTH), dt)
    res = pl.pallas_call(
        body, name=name,
        grid=(HG_HEADS, B, S // sb),
        in_specs=[col(0), col(HG_HEADS), col(2 * HG_HEADS), col(3 * HG_HEADS),
                  pl.BlockSpec((DEPTH, HG_DIM), lambda h, b, s: (0, h)),
                  pl.BlockSpec((DEPTH, HG_DIM), lambda h, b, s: (0, 0)),
                  pl.BlockSpec((N_CUM_F, CHUNK), lambda h, b, s: (0, 0)),
                  pl.BlockSpec((len(_LEVELS), CHUNK, CHUNK), lambda h, b, s: (0, 0, 0))] + p_ispecs,
        out_specs=[col(0), col(0),
                   pl.BlockSpec((None, None, nc, HG_DIM, HG_DIM), lambda h, b, s: (b, h, s, 0, 0)),
                   pl.BlockSpec((None, None, nc, CHUNK, CHUNK), lambda h, b, s: (b, h, s, 0, 0))]
        + [col(0)] * 8 + p_ospecs,
        out_shape=[wide(F32),
                   jax.ShapeDtypeStruct((B, S, u_rows), BF16),
                   jax.ShapeDtypeStruct((B, HG_HEADS, S // CHUNK, HG_DIM, HG_DIM), F32),
                   jax.ShapeDtypeStruct((B, HG_HEADS, S // CHUNK, CHUNK, CHUNK), BF16)]
        + [wide(F32)] * 4 + [wide(BF16)] * 4 + p_oshapes,
        input_output_aliases=p_alias,
        scratch_shapes=[pltpu.VMEM((HG_DIM, HG_DIM), F32)] + p_scratch,
        compiler_params=_params(("arbitrary", "arbitrary", "arbitrary")),
    )(proj_h, proj_h, proj_h, proj_h, lb_param, g_head, ts, _level_masks(), *p_args)
    return res[0], res[1], tuple(res[2:12]), list(res[12:])


def _hgrn_bwd(proj_h, o_h, du, kept, lb_param, g_head, layer, name, phase=None):
    B, S, _ = proj_h.shape
    sb = _pick(S, (512, 256, 128, 64))
    nc = sb // CHUNK
    ns = S // sb
    ts, tst = _cum_matrices()

    def body(*refs):
        ins, outs, (dst,), p_in, p_out, p_sems = _split_refs(refs, 18, 6, 1, phase)
        (i_ref, o_ref, du_ref, sts_ref, am_ref, logf_ref, k_ref, qf_ref, sg_ref, qg_ref, zg_ref, fg_ref, sig_ref,
         lbp_ref, gh_ref, ts_ref, tst_ref, m_ref) = ins
        dq_ref, df_ref, di_ref, dz_ref, dlb_ref, dgh_ref = outs
        h_id, b_id, s_id = pl.program_id(0), pl.program_id(1), pl.program_id(2)
        _hosted_start(phase, p_in, p_out, p_sems, (h_id == 0) & (b_id == 0) & (s_id == 0))

        @pl.when(s_id == 0)
        def _():
            dst[...] = jnp.zeros_like(dst)

        @pl.when((b_id == 0) & (s_id == 0))
        def _():
            dlb_ref[...] = jnp.zeros_like(dlb_ref)

        @pl.when((h_id == 0) & (b_id == 0) & (s_id == 0))
        def _():
            dgh_ref[...] = jnp.zeros_like(dgh_ref)

        lt = _lb_terms(lbp_ref[...], layer)
        gh = gh_ref[layer:layer + 1, :]
        tsv = ts_ref[...]
        tstv = tst_ref[...]
        sg = sg_ref[...]
        o = o_ref[...]
        dub = du_ref[...]
        r = lax.rsqrt(jnp.mean(o * o, axis=-1, keepdims=True) + NORM_EPS)
        n = o * r
        dz_ref[...] = (dub * (n * gh) * zg_ref[...].astype(F32)).astype(BF16)
        dgh_ref[...] += jnp.sum(dub * sg * n, axis=0, keepdims=True)
        dn = dub * sg * gh
        do = (r * (dn - n * jnp.mean(dn * n, axis=-1, keepdims=True))).astype(BF16)
        v = i_ref[...].astype(BF16)
        w = _hg_block_bwd(qf_ref, k_ref, v, logf_ref, do, [am_ref[ci] for ci in range(nc)], tsv, m_ref, nc)
        cur = dst[...]
        ends = [None] * nc
        for ci in reversed(range(nc)):
            ends[ci] = cur
            cur = cur * w["dec"][ci] + w["qd"][ci]
        dst[...] = cur
        dq, dk, dv, dg = _hg_state_bwd(w, v, do, [sts_ref[ci] for ci in range(nc)], ends, tstv, nc)
        di_ref[...] = dv.astype(BF16)
        dq_ref[...] = (dq * qg_ref[...].astype(F32)).astype(BF16)
        f = jnp.exp(logf_ref[...])
        scaled = (dg - f * dk) / f
        df_ref[...] = (scaled * fg_ref[...].astype(F32)).astype(BF16)
        dlb_ref[...] += jnp.sum(scaled * (lt["ind"] - sig_ref[...].astype(F32)), axis=0, keepdims=True)
        _hosted_finish(phase, p_in, p_out, p_sems, (h_id == HG_HEADS - 1) & (b_id == B - 1) & (s_id == ns - 1))

    col = lambda base: pl.BlockSpec((None, sb, HG_DIM), lambda h, b, s: (b, ns - 1 - s, base + h))
    out_col = pl.BlockSpec((None, sb, HG_DIM), lambda h, b, s: (b, ns - 1 - s, h))
    dt = jax.ShapeDtypeStruct((B, S, HG_WIDTH), BF16)
    p_ispecs, p_ospecs, p_oshapes, p_alias, p_scratch, p_args = _host_phase(phase, 18, 6)
    res = pl.pallas_call(
        body, name=name,
        grid=(HG_HEADS, B, ns),
        in_specs=[col(2 * HG_HEADS), col(0), col(0),
                  pl.BlockSpec((None, None, nc, HG_DIM, HG_DIM), lambda h, b, s: (b, h, ns - 1 - s, 0, 0)),
                  pl.BlockSpec((None, None, nc, CHUNK, CHUNK), lambda h, b, s: (b, h, ns - 1 - s, 0, 0))]
        + [col(0)] * 8
        + [pl.BlockSpec((DEPTH, HG_DIM), lambda h, b, s: (0, h)),
           pl.BlockSpec((DEPTH, HG_DIM), lambda h, b, s: (0, 0)),
           pl.BlockSpec((N_CUM_F, CHUNK), lambda h, b, s: (0, 0)),
           pl.BlockSpec((CHUNK, N_CUM), lambda h, b, s: (0, 0)),
           pl.BlockSpec((len(_LEVELS), CHUNK, CHUNK), lambda h, b, s: (0, 0, 0))] + p_ispecs,
        out_specs=[out_col, out_col, out_col, out_col,
                   pl.BlockSpec((1, HG_DIM), lambda h, b, s: (0, h)),
                   pl.BlockSpec((1, HG_DIM), lambda h, b, s: (0, 0))] + p_ospecs,
        out_shape=[dt, dt, dt, dt, jax.ShapeDtypeStruct((1, HG_WIDTH), F32),
                   jax.ShapeDtypeStruct((1, HG_DIM), F32)] + p_oshapes,
        input_output_aliases=p_alias,
        scratch_shapes=[pltpu.VMEM((HG_DIM, HG_DIM), F32)] + p_scratch,
        compiler_params=_params(("arbitrary", "arbitrary", "arbitrary")),
    )(proj_h, o_h, du, *kept, lb_param, g_head, ts, tst, _level_masks(), *p_args)
    return tuple(res[:6]) + (list(res[6:]),)


def _rope_tables(S):
    half = ATT_DIM // 2
    inv_freq = np.float32(ROPE_THETA) ** (-np.arange(half, dtype=np.float32) / half)
    ang = np.arange(S, dtype=np.float32)[:, None] * inv_freq[None, :]
    cos = np.cos(ang)
    sin = np.sin(ang)
    cos = np.concatenate([cos, cos, cos, cos], axis=1)
    sin = np.concatenate([-sin, sin, -sin, sin], axis=1)
    return jnp.asarray(cos, F32), jnp.asarray(sin, F32)


def _attn_common():
    lane = lax.broadcasted_iota(jnp.int32, (1, 2 * ATT_DIM), 1)
    first_half = (lane % ATT_DIM) < (ATT_DIM // 2)
    left = lane < ATT_DIM

    def swap(x):
        return jnp.where(first_half, pltpu.roll(x, 128 - ATT_DIM // 2, 1), pltpu.roll(x, ATT_DIM // 2, 1))

    def rope(x, cos, sin):
        return x * cos + swap(x) * sin

    def rope_bwd(dy, cos, sin):
        return dy * cos + swap(dy * sin)

    def dup(x):
        xs = pltpu.roll(x, ATT_DIM, 1)
        return [jnp.where(left, x, xs), jnp.where(left, xs, x)]

    return left, rope, rope_bwd, dup


GROUP = ATT_HEADS // 2
GROUP_ROWS = GROUP * ATT_BLOCK


def _attn_bias(i):
    r = lax.broadcasted_iota(jnp.int32, (ATT_BLOCK, 2 * ATT_BLOCK), 0)
    c = lax.broadcasted_iota(jnp.int32, (ATT_BLOCK, 2 * ATT_BLOCK), 1)
    ok = (c > r) & (c <= r + ATT_BLOCK) & ((c >= ATT_BLOCK) | (i > 0))
    return jnp.where(ok, 0.0, NEG_INF)


def _stack_heads(pairs, left):
    rows = []
    for x in pairs:
        rows += [jnp.where(left, x, 0.0), jnp.where(left, 0.0, x)]
    return jnp.concatenate(rows, axis=0)


def _unstack_heads(y, left, pp):
    r0 = 2 * pp * ATT_BLOCK
    return jnp.where(left, y[r0:r0 + ATT_BLOCK], y[r0 + ATT_BLOCK:r0 + 2 * ATT_BLOCK])


def _row_sums(x):
    return _dot(x, jnp.ones((x.shape[1], 128), BF16), NN)


def _attn_probs(qs, kd, vd, sink, bias):
    n = range(len(qs))
    rows = qs[0].shape[0]
    s = [(_dot(qs[j], kd[j], NT).reshape(rows // ATT_BLOCK, ATT_BLOCK, 2 * ATT_BLOCK) * ATT_SCALE + bias[None])
         .reshape(rows, 2 * ATT_BLOCK) for j in n]
    m = [jnp.max(jnp.maximum(jnp.maximum(s[j][:, :128], s[j][:, 128:]), sink[j]), axis=-1, keepdims=True) for j in n]
    pu = [jnp.exp(s[j] - m[j]) for j in n]
    es = [jnp.exp(sink[j] - m[j]) for j in n]
    ones = jnp.ones((2 * ATT_BLOCK, 128), BF16)
    ov = [_dot(pu[j], jnp.concatenate([vd[j].astype(BF16), ones], axis=1), NN) for j in n]
    inv = [1.0 / (ov[j][:, 128:] + es[j]) for j in n]
    return ([pu[j] * jnp.concatenate([inv[j], inv[j]], axis=1) for j in n], [es[j] * inv[j] for j in n],
            [ov[j][:, :128] * inv[j] for j in n])


def _sink_rows(sinks):
    return jnp.broadcast_to(jnp.repeat(sinks, ATT_BLOCK, axis=1)[:, :, None], (DEPTH, ATT_HEADS * ATT_BLOCK, 128))


_Z0 = (2 * ATT_WIDTH + 2 * KV_WIDTH - ATT_WIDTH) // 256


def _attn_fwd(proj_a, u, sink_rows, layer, cos, sin, name, phase=None):
    B, S, _ = proj_a.shape
    nb = S // ATT_BLOCK

    def body(*refs):
        ins, (u_ref, p_ref, o_ref, ps_ref, qs_ref), _, p_in, p_out, p_sems = _split_refs(refs, 13, 5, 0, phase)
        q_ref, kvc_ref, kvp_ref, z0, z1, z2, z3, cos_ref, sin_ref, cosp_ref, sinp_ref, sinks_ref, _ = ins
        i = pl.program_id(1)
        _hosted_start(phase, p_in, p_out, p_sems, (pl.program_id(0) == 0) & (i == 0))
        left, rope, _, dup = _attn_common()
        cos_c, sin_c = cos_ref[...], sin_ref[...]
        kvc = kvc_ref[...]
        kvp = kvp_ref[...]
        kw = jnp.concatenate([rope(kvp[:, :KV_WIDTH], cosp_ref[...], sinp_ref[...]),
                              rope(kvc[:, :KV_WIDTH], cos_c, sin_c)], axis=0)
        vw = jnp.concatenate([kvp[:, KV_WIDTH:], kvc[:, KV_WIDTH:]], axis=0)
        kd, vd = dup(kw), dup(vw)
        bias = _attn_bias(i)
        zs = (z0, z1, z2, z3)
        pairs = [range(4 * kvh, 4 * kvh + 4) for kvh in range(2)]
        qs = [_stack_heads([rope(q_ref[:, 128 * pr:128 * (pr + 1)], cos_c, sin_c) for pr in pairs[kvh]], left)
              for kvh in range(2)]
        sink = [sinks_ref[kvh * GROUP_ROWS:(kvh + 1) * GROUP_ROWS, :] for kvh in range(2)]
        p, ps, o = _attn_probs(qs, kd, vd, sink, bias)
        eye = (lax.broadcasted_iota(jnp.int32, (ATT_BLOCK, 128), 0)
               == lax.broadcasted_iota(jnp.int32, (ATT_BLOCK, 128), 1))
        for kvh in range(2):
            p_ref[kvh] = p[kvh].astype(BF16)
            qs_ref[kvh] = qs[kvh].astype(BF16)
            for g in range(GROUP):
                blk = ps[kvh][g * ATT_BLOCK:(g + 1) * ATT_BLOCK, :]
                ps_ref[kvh * GROUP + g:kvh * GROUP + g + 1, :] = jnp.sum(jnp.where(eye, blk, 0.0), axis=0, keepdims=True)
            for pp, pr in enumerate(pairs[kvh]):
                z = zs[pr // 2][:, 128 * (pr % 2):128 * (pr % 2 + 1)]
                o128 = _unstack_heads(o[kvh], left, pp)
                o_ref[:, 128 * pr:128 * (pr + 1)] = o128.astype(BF16)
                u_ref[:, 128 * pr:128 * (pr + 1)] = (o128 * _silu(z)).astype(BF16)
        _hosted_finish(phase, p_in, p_out, p_sems, (pl.program_id(0) == B - 1) & (i == nb - 1))

    rowblk = lambda w, cb: pl.BlockSpec((None, ATT_BLOCK, w), lambda b, i: (b, i, cb))
    tab = pl.BlockSpec((ATT_BLOCK, 128), lambda b, i: (i, 0))
    tabp = pl.BlockSpec((ATT_BLOCK, 128), lambda b, i: (jnp.maximum(i - 1, 0), 0))
    p_ispecs, p_ospecs, p_oshapes, p_alias, p_scratch, p_args = _host_phase(phase, 13, 5)
    res = pl.pallas_call(
        body, name=name,
        grid=(B, nb),
        in_specs=[rowblk(ATT_WIDTH, 0), rowblk(256, 4),
                  pl.BlockSpec((None, ATT_BLOCK, 256), lambda b, i: (b, jnp.maximum(i - 1, 0), 4)),
                  rowblk(256, _Z0), rowblk(256, _Z0 + 1), rowblk(256, _Z0 + 2), rowblk(256, _Z0 + 3),
                  tab, tab, tabp, tabp,
                  pl.BlockSpec((None, ATT_HEADS * ATT_BLOCK, 128), lambda b, i: (layer, 0, 0)),
                  pl.BlockSpec(memory_space=pl.ANY)] + p_ispecs,
        out_specs=[pl.BlockSpec((None, ATT_BLOCK, ATT_WIDTH), lambda b, i: (b, i, 1)),
                   pl.BlockSpec((None, None, 2, GROUP_ROWS, 2 * ATT_BLOCK), lambda b, i: (b, i, 0, 0, 0)),
                   pl.BlockSpec((None, ATT_BLOCK, ATT_WIDTH), lambda b, i: (b, i, 0)),
                   pl.BlockSpec((None, None, ATT_HEADS, 128), lambda b, i: (b, i, 0, 0)),
                   pl.BlockSpec((None, None, 2, GROUP_ROWS, 128), lambda b, i: (b, i, 0, 0, 0))] + p_ospecs,
        out_shape=[jax.ShapeDtypeStruct(u.shape, BF16),
                   jax.ShapeDtypeStruct((B, nb, 2, GROUP_ROWS, 2 * ATT_BLOCK), BF16),
                   jax.ShapeDtypeStruct((B, S, ATT_WIDTH), BF16),
                   jax.ShapeDtypeStruct((B, nb, ATT_HEADS, 128), F32),
                   jax.ShapeDtypeStruct((B, nb, 2, GROUP_ROWS, 128), BF16)] + p_oshapes,
        input_output_aliases={12: 0, **p_alias},
        scratch_shapes=p_scratch,
        compiler_params=_params(("arbitrary", "arbitrary")),
    )(proj_a, proj_a, proj_a, proj_a, proj_a, proj_a, proj_a, cos, sin, cos, sin, sink_rows, u, *p_args)
    return res[0], tuple(res[1:5]), list(res[5:])


def _attn_bwd(proj_a, du, kept, cos, sin, name, phase=None):
    B, S, _ = proj_a.shape
    nb = S // ATT_BLOCK
    p_kept, o_kept, ps_kept, qs_kept = kept

    def body(*refs):
        ins, outs, (carry, sk_acc), p_in, p_out, p_sems = _split_refs(refs, 15, 4, 2, phase)
        (qs_ref, kvc_ref, kvp_ref, z0, z1, z2, z3, du_ref, cos_ref, sin_ref, cosp_ref, sinp_ref,
         p_ref, o_ref, ps_ref) = ins
        dq_ref, dkv_ref, dz_ref, dsk_ref = outs
        b_id, i = pl.program_id(0), pl.program_id(1)
        _hosted_start(phase, p_in, p_out, p_sems, (b_id == 0) & (i == 0))

        @pl.when((b_id == 0) & (i == 0))
        def _():
            sk_acc[...] = jnp.zeros_like(sk_acc)

        @pl.when(i == 0)
        def _():
            carry[...] = jnp.zeros_like(carry)

        @pl.when(i < nb)
        def _():
            left, rope, rope_bwd, dup = _attn_common()
            cos_c, sin_c = cos_ref[...], sin_ref[...]
            cos_p, sin_p = cosp_ref[...], sinp_ref[...]
            kvc = kvc_ref[...]
            kvp = kvp_ref[...]
            kw = jnp.concatenate([rope(kvp[:, :KV_WIDTH], cos_p, sin_p), rope(kvc[:, :KV_WIDTH], cos_c, sin_c)], axis=0)
            vw = jnp.concatenate([kvp[:, KV_WIDTH:], kvc[:, KV_WIDTH:]], axis=0)
            kd, vd = dup(kw), dup(vw)
            zs = (z0, z1, z2, z3)
            units = [(kvh, hf) for kvh in range(2) for hf in range(2)]
            half = GROUP_ROWS // 2
            pairs = [range(4 * kvh + 2 * hf, 4 * kvh + 2 * hf + 2) for kvh, hf in units]
            ku = [kd[kvh] for kvh, _ in units]
            vu = [vd[kvh] for kvh, _ in units]
            ps_all = ps_ref[...]
            head_row = lax.broadcasted_iota(jnp.int32, (ATT_HEADS, 128), 0)
            eye = (lax.broadcasted_iota(jnp.int32, (ATT_BLOCK, 128), 0)
                   == lax.broadcasted_iota(jnp.int32, (ATT_BLOCK, 128), 1))

            def first(j):
                kvh, hf = units[j]
                p = p_ref[kvh, hf * half:(hf + 1) * half, :]
                parts = []
                for pr in pairs[j]:
                    cols = slice(128 * pr, 128 * (pr + 1))
                    sg, sg_grad = _silu_and_grad(zs[pr // 2][:, 128 * (pr % 2):128 * (pr % 2 + 1)])
                    du128 = du_ref[:, cols]
                    dz_ref[:, cols] = (du128 * o_ref[:, cols].astype(F32) * sg_grad).astype(BF16)
                    parts.append(du128 * sg)
                dos = _stack_heads(parts, left)
                dp = _dot(dos, vu[j], NT)
                delta = _row_sums(p.astype(F32) * dp)
                ds = (p.astype(F32) * (dp - jnp.concatenate([delta, delta], axis=1)) * ATT_SCALE).astype(BF16)
                sk = jnp.zeros((ATT_HEADS, 128), F32)
                for hh in range(4):
                    hd = kvh * GROUP + 4 * hf + hh
                    drow = jnp.sum(jnp.where(eye, delta[hh * ATT_BLOCK:(hh + 1) * ATT_BLOCK, :], 0.0), axis=0,
                                   keepdims=True)
                    sk = sk - jnp.where(head_row == hd, ps_all * drow, 0.0)
                sk_acc[...] += sk
                return ds, p, dos.astype(BF16), qs_ref[kvh, hf * half:(hf + 1) * half, :]

            def second(j, ds, p, dos, qs):
                dqs = _dot(ds, ku[j], NN)
                for pp, pr in enumerate(pairs[j]):
                    dq_ref[:, 128 * pr:128 * (pr + 1)] = rope_bwd(_unstack_heads(dqs, left, pp),
                                                                  cos_c, sin_c).astype(BF16)
                return _dot(ds, qs, TN), _dot(p, dos, TN)

            got, dku, dvu = {}, [None] * len(units), [None] * len(units)
            for j in range(len(units) + 1):
                if j < len(units):
                    got[j] = first(j)
                if j >= 1:
                    dku[j - 1], dvu[j - 1] = second(j - 1, *got.pop(j - 1))
            dkd = [dku[0] + dku[1], dku[2] + dku[3]]
            dvd = [dvu[0] + dvu[1], dvu[2] + dvu[3]]
            fold = lambda pr: jnp.where(left, pr[0] + pltpu.roll(pr[0], ATT_DIM, 1), pr[1] + pltpu.roll(pr[1], ATT_DIM, 1))
            dkw = fold(dkd)
            dvw = fold(dvd)
            prev = jnp.concatenate([rope_bwd(dkw[:ATT_BLOCK], cos_p, sin_p), dvw[:ATT_BLOCK]], axis=1)
            cur = jnp.concatenate([rope_bwd(dkw[ATT_BLOCK:], cos_c, sin_c), dvw[ATT_BLOCK:]], axis=1)
            dkv_ref[...] = (carry[...] + prev).astype(BF16)
            carry[...] = cur

        @pl.when(i == nb)
        def _():
            dkv_ref[...] = carry[...].astype(BF16)

        @pl.when((b_id == B - 1) & (i == nb))
        def _():
            diag = (lax.broadcasted_iota(jnp.int32, (ATT_HEADS, 128), 0)
                    == lax.broadcasted_iota(jnp.int32, (ATT_HEADS, 128), 1))
            tot = jnp.sum(sk_acc[...], axis=1, keepdims=True)
            dsk_ref[...] = jnp.sum(jnp.where(diag, tot, 0.0), axis=0, keepdims=True)

        _hosted_finish(phase, p_in, p_out, p_sems, (b_id == B - 1) & (i == nb))

    cl = lambda i: jnp.minimum(i, nb - 1)
    pv = lambda i: jnp.maximum(jnp.minimum(i, nb - 1) - 1, 0)
    rowblk = lambda w, cb: pl.BlockSpec((None, ATT_BLOCK, w), lambda b, i: (b, cl(i), cb))
    tab = pl.BlockSpec((ATT_BLOCK, 128), lambda b, i: (cl(i), 0))
    tabp = pl.BlockSpec((ATT_BLOCK, 128), lambda b, i: (pv(i), 0))
    p_ispecs, p_ospecs, p_oshapes, p_alias, p_scratch, p_args = _host_phase(phase, 15, 4)
    res = pl.pallas_call(
        body, name=name,
        grid=(B, nb + 1),
        in_specs=[pl.BlockSpec((None, None, 2, GROUP_ROWS, 128), lambda b, i: (b, cl(i), 0, 0, 0)), rowblk(256, 4),
                  pl.BlockSpec((None, ATT_BLOCK, 256), lambda b, i: (b, pv(i), 4)),
                  rowblk(256, _Z0), rowblk(256, _Z0 + 1), rowblk(256, _Z0 + 2), rowblk(256, _Z0 + 3),
                  rowblk(ATT_WIDTH, 1),
                  tab, tab, tabp, tabp,
                  pl.BlockSpec((None, None, 2, GROUP_ROWS, 2 * ATT_BLOCK), lambda b, i: (b, cl(i), 0, 0, 0)),
                  rowblk(ATT_WIDTH, 0),
                  pl.BlockSpec((None, None, ATT_HEADS, 128), lambda b, i: (b, cl(i), 0, 0))] + p_ispecs,
        out_specs=[rowblk(ATT_WIDTH, 0),
                   pl.BlockSpec((None, ATT_BLOCK, 256), lambda b, i: (b, jnp.maximum(i - 1, 0), 0)),
                   rowblk(ATT_WIDTH, 0),
                   pl.BlockSpec((1, 128), lambda b, i: (0, 0))] + p_ospecs,
        out_shape=[jax.ShapeDtypeStruct((B, S, ATT_WIDTH), BF16), jax.ShapeDtypeStruct((B, S, 256), BF16),
                   jax.ShapeDtypeStruct((B, S, ATT_WIDTH), BF16), jax.ShapeDtypeStruct((1, 128), F32)] + p_oshapes,
        input_output_aliases=p_alias,
        scratch_shapes=[pltpu.VMEM((ATT_BLOCK, 256), F32), pltpu.VMEM((ATT_HEADS, 128), F32)] + p_scratch,
        compiler_params=_params(("arbitrary", "arbitrary")),
    )(qs_kept, proj_a, proj_a, proj_a, proj_a, proj_a, proj_a, du, cos, sin, cos, sin, p_kept, o_kept, ps_kept, *p_args)
    return tuple(res[:4]) + (list(res[4:]),)


def _outproj_fwd(u2, w_out, x2, g_post, layer, target2, name):
    T, D = x2.shape
    tm = _pick(T, (512, 256, 128))
    last = target2 is not None
    streamed = [u2, x2] + ([target2] if last else [])
    ns, nt, slots = len(streamed), T // tm, 3

    def body(w_ref, g_ref, *rest):
        srcs, rings, sems = rest[:ns], rest[-ns - 1:-1], rest[-1]
        rest = rest[ns:-ns - 1]
        i = pl.program_id(0)

        def tile_copy(k, t):
            return pltpu.make_async_copy(srcs[k].at[pl.ds(pl.multiple_of(t * tm, tm), tm), :],
                                         rings[k].at[t % slots], sems.at[k, t % slots])

        @pl.when(i == 0)
        def _():
            for t in range(min(slots - 1, nt)):
                for k in range(ns):
                    tile_copy(k, jnp.int32(t)).start()

        @pl.when(i + (slots - 1) < nt)
        def _():
            for k in range(ns):
                tile_copy(k, i + (slots - 1)).start()

        for k in range(ns):
            tile_copy(k, i).wait()
        tiles = [rings[k][i % slots] for k in range(ns)]
        y = lax.dot_general(tiles[0], w_ref[...], (NN, ((), ())), preferred_element_type=F32)
        r = lax.rsqrt(jnp.mean(y * y, axis=-1, keepdims=True) + NORM_EPS)
        xn = tiles[1] + (y * r) * g_ref[layer:layer + 1, :]
        if last:
            y_ref, dx_ref, loss_ref = rest
            err = xn - tiles[2]
            dx_ref[...] = err * (1.0 / D)
            sq = err * err
            acc = sq[:, 0:128]
            for kk in range(1, D // 128):
                acc = acc + sq[:, 128 * kk:128 * (kk + 1)]
            part = jnp.sum(acc.reshape(tm // 8, 8, 128), axis=0) * (0.5 / D)

            @pl.when(pl.program_id(0) == 0)
            def _():
                loss_ref[...] = jnp.zeros_like(loss_ref)

            loss_ref[...] += part
        else:
            y_ref, xn_ref = rest
            xn_ref[...] = xn
        y_ref[...] = y

    row = pl.BlockSpec((tm, D), lambda i: (i, 0))
    in_specs = [pl.BlockSpec((MIX_WIDTH, D), lambda i: (0, 0)), pl.BlockSpec((DEPTH, D), lambda i: (0, 0))] + [_ANY] * ns
    out_specs = [row, row]
    out_shape = [jax.ShapeDtypeStruct((T, D), F32), jax.ShapeDtypeStruct((T, D), F32)]
    if last:
        out_specs.append(pl.BlockSpec((8, 128), lambda i: (0, 0)))
        out_shape.append(jax.ShapeDtypeStruct((8, 128), F32))
    return pl.pallas_call(
        body, name=name, grid=(nt,), in_specs=in_specs, out_specs=out_specs, out_shape=out_shape,
        scratch_shapes=[pltpu.VMEM((slots, tm, a.shape[1]), a.dtype) for a in streamed]
        + [pltpu.SemaphoreType.DMA((ns, slots))],
        compiler_params=_params(("arbitrary",)),
    )(w_out, g_post, *streamed)


def _outproj_bwd(dxn2, y2, g_post, layer, w_out, u2, name):
    T, D = y2.shape
    N = w_out.shape[0]
    tm = _pick(T, (512, 256, 128))
    nt = T // tm

    def body(dx_ref, y_ref, g_ref, w_ref, u_ref, dg_ref, du_ref, dw_ref, acc, wacc):
        i = pl.program_id(0)

        @pl.when(i == 0)
        def _():
            acc[...] = jnp.zeros_like(acc)
            wacc[...] = jnp.zeros_like(wacc)

        y = y_ref[...]
        dxn = dx_ref[...]
        r = lax.rsqrt(jnp.mean(y * y, axis=-1, keepdims=True) + NORM_EPS)
        n = y * r
        dn = dxn * g_ref[layer:layer + 1, :]
        dy = (r * (dn - n * jnp.mean(dn * n, axis=-1, keepdims=True))).astype(BF16)
        du_ref[...] = lax.dot_general(dy, w_ref[...], (NT, ((), ())), preferred_element_type=F32)
        wacc[...] += lax.dot_general(u_ref[...], dy, (TN, ((), ())), preferred_element_type=F32)
        acc[...] += jnp.sum((dxn * n).reshape(tm // 8, 8, D), axis=0)

        @pl.when(i == nt - 1)
        def _():
            dg_ref[...] = jnp.sum(acc[...], axis=0, keepdims=True)
            dw_ref[...] = wacc[...].astype(BF16)

    row = pl.BlockSpec((tm, D), lambda i: (i, 0))
    wide = pl.BlockSpec((tm, N), lambda i: (i, 0))
    vec = pl.BlockSpec((1, D), lambda i: (0, 0))
    whole = pl.BlockSpec((N, D), lambda i: (0, 0))
    return pl.pallas_call(
        body, name=name, grid=(nt,),
        in_specs=[row, row, pl.BlockSpec((DEPTH, D), lambda i: (0, 0)),
                  pl.BlockSpec((N, D), lambda i: (0, 0), pipeline_mode=pl.Buffered(1)), wide],
        out_specs=[vec, wide, whole],
        out_shape=[jax.ShapeDtypeStruct((1, D), F32), jax.ShapeDtypeStruct((T, N), F32),
                   jax.ShapeDtypeStruct((N, D), BF16)],
        scratch_shapes=[pltpu.VMEM((8, D), F32), pltpu.VMEM((N, D), F32)],
        compiler_params=_params(("arbitrary",)),
    )(dxn2, y2, g_post, w_out, u2)


def _inproj_bwd(pieces, w_t, x2, dxn2, g_pre, layer, name, phase=None):
    T, D = x2.shape
    widths = [p.shape[1] for p in pieces]
    offs = [sum(widths[:i]) for i in range(len(pieces))]
    n_p = len(pieces)
    tm = _pick(T, (256, 128))
    nt = T // tm

    def body(*refs):
        ins, (dx_ref, dg_ref), (acc,), p_in, p_out, p_sems = _split_refs(refs, n_p + 4, 2, 1, phase)
        w_ref, x_ref, dxn_ref, g_ref = ins[n_p:]
        i = pl.program_id(0)
        _hosted_start(phase, p_in, p_out, p_sems, i == 0)

        @pl.when(i == 0)
        def _():
            acc[...] = jnp.zeros_like(acc)

        dh = jnp.zeros((tm, D), F32)
        for p in range(n_p):
            dh = dh + lax.dot_general(ins[p][...], w_ref[offs[p]:offs[p] + widths[p], :], (NN, ((), ())),
                                      preferred_element_type=F32)
        x = x_ref[...]
        r = lax.rsqrt(jnp.mean(x * x, axis=-1, keepdims=True) + NORM_EPS)
        n = x * r
        dn = dh * g_ref[layer:layer + 1, :]
        dx_ref[...] = dxn_ref[...] + r * (dn - n * jnp.mean(dn * n, axis=-1, keepdims=True))
        acc[...] += jnp.sum((dh * n).reshape(tm // 8, 8, D), axis=0)

        @pl.when(i == nt - 1)
        def _():
            dg_ref[...] = jnp.sum(acc[...], axis=0, keepdims=True)

        _hosted_finish(phase, p_in, p_out, p_sems, i == nt - 1)

    row = pl.BlockSpec((tm, D), lambda i: (i, 0))
    vec = pl.BlockSpec((1, D), lambda i: (0, 0))
    p_ispecs, p_ospecs, p_oshapes, p_alias, p_scratch, p_args = _host_phase(phase, n_p + 4, 2)
    res = pl.pallas_call(
        body, name=name, grid=(nt,),
        in_specs=[pl.BlockSpec((tm, w), lambda i: (i, 0)) for w in widths]
        + [pl.BlockSpec((sum(widths), D), lambda i: (0, 0), pipeline_mode=pl.Buffered(1)), row, row,
           pl.BlockSpec((DEPTH, D), lambda i: (0, 0))] + p_ispecs,
        out_specs=[row, vec] + p_ospecs,
        out_shape=[jax.ShapeDtypeStruct((T, D), F32), jax.ShapeDtypeStruct((1, D), F32)] + p_oshapes,
        input_output_aliases=p_alias,
        scratch_shapes=[pltpu.VMEM((8, D), F32)] + p_scratch,
        compiler_params=_params(("arbitrary",)),
    )(*pieces, w_t, x2, dxn2, g_pre, *p_args)
    return res[0], res[1], list(res[2:])


def _step(x, target, g_pre, g_post, lb_param, g_head, sinks, shards=None, full=None):
    B, S, D = x.shape
    T = B * S
    dist = shards is not None
    first, last = 0, DEPTH - 1
    if dist:
        a_loc, b_loc = shards
        ra, rb = a_loc.shape[1], b_loc.shape[1]
        side = _own_side_blocks()
        a_full, b_full = _place_own([a_loc, b_loc], side, "place_own")
        w_in0 = _gather_one_call(a_full[0], "gather_in0")
        w_in, w_out = [w_in0, None], [None, None]
    else:
        w_in, w_out = list(full[0]), list(full[1])
    cos, sin = _rope_tables(S)
    sink_rows = _sink_rows(sinks)
    saved = []
    xs = x
    loss_part = None
    dxn = None
    for l in range(DEPTH):
        x2 = xs.reshape(T, D)
        proj_h, proj_a, h = _inproj(x2, g_pre, l, w_in[l], f"inproj{l}")
        proj_h = proj_h.reshape(B, S, N_H)
        proj_a = proj_a.reshape(B, S, N_A)
        phase = None
        if dist and l == first:
            phase = _gather_ici_phase([a_full[1], b_full[0]])
        if dist and l == last:
            phase = _gather_d2d_phase([w_out1_part], [rb])
        o_h, u, states, got = _hgrn_fwd(proj_h, MIX_WIDTH, lb_param, g_head, l, f"hgrn_fwd{l}", phase)
        phase = None
        if dist and l == first:
            phase = _merge_phases(_gather_d2d_phase(got, [ra, rb]),
                                  _gather_ici_phase([b_full[1]]))
        if dist and l == last:
            w_out[1] = got[0]
        u, kept_a, got = _attn_fwd(proj_a, u, sink_rows, l, cos, sin, f"attn_fwd{l}", phase)
        if dist and l == first:
            w_in[1], w_out[0], w_out1_part = got
        u2 = u.reshape(T, MIX_WIDTH)
        if l < last:
            y, xn = _outproj_fwd(u2, w_out[l], x2, g_post, l, None, f"outproj{l}")
            xn = xn.reshape(B, S, D)
        else:
            y, dxn, loss_part = _outproj_fwd(u2, w_out[l], x2, g_post, l, target.reshape(T, D), f"outproj{l}")
            xn = None
        saved.append((x2, h, proj_h, proj_a, o_h, u2, states, kept_a, y))
        xs = xn

    dw_in, dw_out = [None] * DEPTH, [None] * DEPTH
    dg_pre, dg_post, dlb, dg_head, dsinks = [], [], [], [], []
    for l in reversed(range(DEPTH)):
        x2, h, proj_h, proj_a, o_h, u2, states, kept_a, y = saved[l]
        dgp, du, dw_out[l] = _outproj_bwd(dxn, y, g_post, l, w_out[l], u2, f"outproj_bwd{l}")
        du = du.reshape(B, S, MIX_WIDTH)
        phase = None
        if dist:
            phase = _reduce_d2d_phase([dw_out[l]], [rb])
            if l == first:
                phase = _merge_phases(_reduce_ici_phase([part_in1]), phase)
        dqh, dfh, dih, dzh, dlb_l, dgh, got = _hgrn_bwd(
            proj_h, o_h, du, states, lb_param, g_head, l, f"hgrn_bwd{l}", phase)
        if dist:
            if l == first:
                sum_in = _chip_sum(part_in1, got[0], "chip_sum_in1", 1)
            part_out = _pair_sum(dw_out[l], got[-1], side, f"pair_sum_out{l}")
        dqa, dkv, dza, dsk, got = _attn_bwd(proj_a, du, kept_a, cos, sin, f"attn_bwd{l}",
                                            _reduce_ici_phase([part_out]) if dist else None)
        if dist:
            sum_out = _chip_sum(part_out, got[0], f"chip_sum_out{l}", l, None if l == last else sum_out)
        dproj = [p.reshape(T, p.shape[-1]) for p in (dqh, dfh, dih, dzh, dqa, dkv, dza)]
        dw_in[l] = _mm_tn(dproj, h, f"wgrad_in{l}")
        phase = None
        if dist and l == last:
            phase = _reduce_d2d_phase([dw_in[l]], [ra])
        if dist and l == first:
            got = _run_phase(_reduce_d2d_phase([dw_in[l]], [ra]), "reduce_in0_d2d")
            part_in0 = _pair_sum(dw_in[l], got[0], side, "pair_sum_in0")
            phase = _reduce_ici_phase([part_in0])
        dxn, dgpre, got = _inproj_bwd(dproj, w_in[l], x2, dxn, g_pre, l, f"inproj_bwd{l}", phase)
        if dist and l == last:
            part_in1 = _pair_sum(dw_in[l], got[0], side, "pair_sum_in1")
        if dist and l == first:
            sum_in = _chip_sum(part_in0, got[0], "chip_sum_in0", 0, sum_in)
        dg_pre.append(dgpre)
        dg_post.append(dgp)
        dlb.append(dlb_l)
        dg_head.append(dgh)
        dsinks.append(dsk)
    rev = lambda lst: jnp.concatenate(lst[::-1], axis=0)
    if not dist:
        sum_in, sum_out = jnp.stack(dw_in), jnp.stack(dw_out)
    return (loss_part, dxn.reshape(B, S, D), sum_in, sum_out,
            rev(dg_pre), rev(dg_post), rev(dlb), rev(dg_head), rev(dsinks))


def _me_and_peers():
    x, y, c = lax.axis_index("x"), lax.axis_index("y"), lax.axis_index("c")
    me = 4 * x + 2 * y + c
    peers = []
    for k in range(1, N_DEV):
        px = 1 - x if k & 4 else x
        py = 1 - y if k & 2 else y
        pc = 1 - c if k & 1 else c
        peers.append(((px, py, pc), 4 * px + 2 * py + pc))
    return me, peers


class _Phase:
    def __init__(self, arrays, out_shapes, aliases, n_send, build):
        self.arrays, self.out_shapes, self.aliases = list(arrays), list(out_shapes), dict(aliases)
        self.n_send, self.build = n_send, build

    def scratch(self):
        return [pltpu.SemaphoreType.DMA((self.n_send,)), pltpu.SemaphoreType.DMA((self.n_send,))]

    def _copies(self, in_refs, out_refs, sems, arrivals):
        send_sems, recv_sems = sems
        sends, recvs = self.build(in_refs, out_refs)
        assert len(sends) == self.n_send == len(recvs)
        out = [pltpu.make_async_remote_copy(src_ref=s, dst_ref=d, send_sem=send_sems.at[i], recv_sem=recv_sems.at[i],
                                            device_id=dev, device_id_type=MESH) for i, (s, d, dev) in enumerate(sends)]
        inc = [pltpu.make_async_remote_copy(src_ref=s, dst_ref=r, send_sem=send_sems.at[i], recv_sem=recv_sems.at[i],
                                            device_id=dev, device_id_type=MESH)
               for i, ((s, _, dev), r) in enumerate(zip(sends, recvs))] if arrivals else []
        return out, inc

    def start(self, in_refs, out_refs, sems):
        out, _ = self._copies(in_refs, out_refs, sems, False)
        for cp in out:
            cp.start()

    def finish(self, in_refs, out_refs, sems):
        out, inc = self._copies(in_refs, out_refs, sems, True)
        for cp in inc:
            cp.wait_recv()
        for cp in out:
            cp.wait_send()


_ANY = pl.BlockSpec(memory_space=pl.ANY)


def _host_phase(phase, n_in, n_out):
    if phase is None:
        return [], [], [], {}, [], []
    aliases = {n_in + i: n_out + o for i, o in phase.aliases.items()}
    return ([_ANY] * len(phase.arrays), [_ANY] * len(phase.out_shapes), phase.out_shapes, aliases, phase.scratch(),
            phase.arrays)


def _split_refs(refs, n_in, n_out, n_scr, phase):
    pi = len(phase.arrays) if phase else 0
    po = len(phase.out_shapes) if phase else 0
    a = n_in + pi
    b = a + n_out + po
    return (refs[:n_in], refs[a:a + n_out], refs[b:b + n_scr], refs[n_in:a], refs[a + n_out:b], refs[b + n_scr:])


def _hosted_start(phase, p_in, p_out, p_sems, first):
    if phase is not None:
        @pl.when(first)
        def _():
            phase.start(p_in, p_out, p_sems)


def _hosted_finish(phase, p_in, p_out, p_sems, last):
    if phase is not None:
        @pl.when(last)
        def _():
            phase.finish(p_in, p_out, p_sems)


def _run_phase(phase, name):
    n_in, n_out = len(phase.arrays), len(phase.out_shapes)

    def body(*refs):
        phase.start(refs[:n_in], refs[n_in:n_in + n_out], refs[n_in + n_out:])
        phase.finish(refs[:n_in], refs[n_in:n_in + n_out], refs[n_in + n_out:])

    return pl.pallas_call(
        body, name=name, in_specs=[_ANY] * n_in, out_specs=[_ANY] * n_out,
        out_shape=phase.out_shapes, input_output_aliases=phase.aliases, scratch_shapes=phase.scratch(),
        compiler_params=pltpu.CompilerParams(has_side_effects=True),
    )(*phase.arrays)


def _gather_one_call(full, name):
    r = full.shape[0] // N_DEV
    half = r // 2

    def body(full_in, full_ref, send_sems, recv_sems):
        del full_in
        c, (own, xn, yn, dg), num = _mesh_place()
        me, sib = num(own, c), (*own, 1 - c)

        def blk(dev, part=None):
            start, n = (dev * r, r) if part is None else (dev * r + part * half, half)
            return full_ref.at[pl.ds(pl.multiple_of(start, 16), n), :]

        def copy(k, src, dev, to, part=None):
            return pltpu.make_async_remote_copy(src_ref=src, dst_ref=blk(dev, part),
                                                send_sem=send_sems.at[k], recv_sem=recv_sems.at[k],
                                                device_id=to, device_id_type=MESH)

        def landed(k, dev, part=None):
            copy(k, blk(dev, part), dev, sib, part).wait_recv()

        sent = []

        def start(*cps):
            for cp in cps:
                cp.start()
                sent.append(cp)

        xs, ys, ds = num(xn, c), num(yn, c), num(dg, c)
        start(copy(0, blk(me), me, sib), copy(1, blk(me), me, (*xn, c)), copy(2, blk(me), me, (*yn, c)))
        landed(1, xs)
        start(copy(3, blk(xs, 0), xs, (*yn, c), 0), copy(5, blk(xs), xs, sib))
        landed(2, ys)
        start(copy(4, blk(ys, 1), ys, (*xn, c), 1), copy(6, blk(ys), ys, sib))
        landed(3, ds, 0)
        landed(4, ds, 1)
        start(copy(7, blk(ds), ds, sib))
        landed(0, num(own, 1 - c))
        for k, ch in ((5, xn), (6, yn), (7, dg)):
            landed(k, num(ch, 1 - c))
        for cp in sent:
            cp.wait_send()

    assert half % 16 == 0
    return pl.pallas_call(
        body, name=name, in_specs=[_ANY], out_specs=_ANY,
        out_shape=jax.ShapeDtypeStruct(full.shape, full.dtype), input_output_aliases={0: 0},
        scratch_shapes=[pltpu.SemaphoreType.DMA((8,)), pltpu.SemaphoreType.DMA((8,))],
        compiler_params=pltpu.CompilerParams(has_side_effects=True),
    )(full)


def _merge_phases(a, b):
    n_in, n_out = len(a.arrays), len(a.out_shapes)
    aliases = dict(a.aliases)
    aliases.update({n_in + i: n_out + o for i, o in b.aliases.items()})

    def build(ins, outs):
        sa, ra = a.build(ins[:n_in], outs[:n_out])
        sb, rb = b.build(ins[n_in:], outs[n_out:])
        return sa + sb, ra + rb

    return _Phase(a.arrays + b.arrays, a.out_shapes + b.out_shapes, aliases, a.n_send + b.n_send, build)


def _mesh_place():
    x, y, c = lax.axis_index("x"), lax.axis_index("y"), lax.axis_index("c")
    chips = [(x, y), (1 - x, y), (x, 1 - y), (1 - x, 1 - y)]
    num = lambda chip, core: 4 * chip[0] + 2 * chip[1] + core
    return c, chips, num


def _own_side_blocks():
    c, chips, num = _mesh_place()
    return jnp.stack([num(ch, c) for ch in chips]).astype(jnp.int32)


def _rows(ref, r, dev):
    return ref.at[pl.ds(pl.multiple_of(dev * r, 16), r), :]


def _place_own(shards, blocks, name):
    n = len(shards)

    def body(idx_ref, *refs):
        del idx_ref
        outs = iter(refs[n:])
        for s_ref in refs[:n]:
            for l in range(DEPTH):
                next(outs)[...] = s_ref[l].astype(BF16)

    whole = lambda s: pl.BlockSpec(s.shape, lambda i, idx: (0, 0, 0))
    own = lambda s: pl.BlockSpec(s.shape[1:], lambda i, idx: (idx[0], 0))
    res = pl.pallas_call(
        body, name=name,
        grid_spec=pltpu.PrefetchScalarGridSpec(
            num_scalar_prefetch=1, grid=(1,),
            in_specs=[whole(s) for s in shards],
            out_specs=[own(s) for s in shards for _ in range(DEPTH)]),
        out_shape=[jax.ShapeDtypeStruct((N_DEV * s.shape[1], s.shape[2]), BF16) for s in shards for _ in range(DEPTH)],
        compiler_params=_params(("arbitrary",)),
    )(blocks, *shards)
    return [list(res[i * DEPTH:(i + 1) * DEPTH]) for i in range(n)]


def _gather_ici_phase(fulls):
    rs = [a.shape[0] // N_DEV for a in fulls]
    n = len(fulls)

    def build(ins, outs):
        del ins
        c, chips, num = _mesh_place()
        me = num(chips[0], c)
        targets = [((*chips[0], 1 - c), num(chips[0], 1 - c))] + [((*ch, c), num(ch, c)) for ch in chips[1:]]
        sends, recvs = [], []
        for dev, dnum in targets:
            for i, r in enumerate(rs):
                sends.append((_rows(outs[i], r, me), _rows(outs[i], r, me), dev))
                recvs.append(_rows(outs[i], r, dnum))
        return sends, recvs

    shapes = [jax.ShapeDtypeStruct(a.shape, a.dtype) for a in fulls]
    return _Phase(list(fulls), shapes, {i: i for i in range(n)}, 4 * n, build)


def _gather_d2d_phase(fulls, rs):
    def build(ins, outs):
        c, chips, num = _mesh_place()
        sib = (*chips[0], 1 - c)
        sends, recvs = [], []
        for ch in chips[1:]:
            for i, r in enumerate(rs):
                blk = _rows(outs[i], r, num(ch, c))
                sends.append((blk, blk, sib))
                recvs.append(_rows(outs[i], r, num(ch, 1 - c)))
        return sends, recvs

    shapes = [jax.ShapeDtypeStruct(a.shape, a.dtype) for a in fulls]
    return _Phase(fulls, shapes, {i: i for i in range(len(fulls))}, 3 * len(fulls), build)


def _reduce_d2d_phase(grads, rs):
    def build(ins, outs):
        c, chips, num = _mesh_place()
        sib = (*chips[0], 1 - c)
        sends, recvs = [], []
        for j, ch in enumerate(chips):
            for i, r in enumerate(rs):
                sends.append((_rows(ins[i], r, num(ch, 1 - c)), outs[i].at[j], sib))
                recvs.append(outs[i].at[j])
        return sends, recvs

    shapes = [jax.ShapeDtypeStruct((4, r, g.shape[1]), g.dtype) for g, r in zip(grads, rs)]
    return _Phase(grads, shapes, {}, 4 * len(grads), build)


def _reduce_ici_phase(parts):
    def build(ins, outs):
        c, chips, _ = _mesh_place()
        sends, recvs = [], []
        for t in range(1, 4):
            for i in range(len(parts)):
                sends.append((ins[i].at[t], outs[i].at[t - 1], (*chips[t], c)))
                recvs.append(outs[i].at[t - 1])
        return sends, recvs

    shapes = [jax.ShapeDtypeStruct((3,) + p.shape[1:], p.dtype) for p in parts]
    return _Phase(parts, shapes, {}, 3 * len(parts), build)


def _pair_sum(g, got, blocks, name):
    n, r, D = got.shape
    tr = _pick(r, (800, 400, 256, 200, 128, 64, 16))

    def body(idx_ref, g_ref, r_ref, o_ref):
        del idx_ref
        o_ref[...] = (g_ref[...].astype(F32) + r_ref[...].astype(F32)).astype(o_ref.dtype)

    blk = pl.BlockSpec((None, tr, D), lambda j, i, idx: (j, i, 0))
    return pl.pallas_call(
        body, name=name,
        grid_spec=pltpu.PrefetchScalarGridSpec(
            num_scalar_prefetch=1, grid=(n, r // tr),
            in_specs=[pl.BlockSpec((tr, D), lambda j, i, idx: (idx[j] * (r // tr) + i, 0)), blk],
            out_specs=blk),
        out_shape=jax.ShapeDtypeStruct(got.shape, got.dtype),
        compiler_params=_params(("arbitrary", "arbitrary")),
    )(blocks, g, got)


def _chip_sum(p, r, name, layer, into=None):
    _, R, D = p.shape
    tr = _pick(R, (800, 400, 256, 200, 128, 64, 16))

    def body(p_ref, r_ref, *rest):
        acc = p_ref[...].astype(F32)
        for t in range(3):
            acc = acc + r_ref[t].astype(F32)
        rest[-1][...] = acc

    args = [p, r] + ([] if into is None else [into])
    return pl.pallas_call(
        body, name=name, grid=(R // tr,),
        in_specs=[pl.BlockSpec((None, tr, D), lambda i: (0, i, 0)), pl.BlockSpec((3, tr, D), lambda i: (0, i, 0))]
        + ([] if into is None else [_ANY]),
        out_specs=pl.BlockSpec((None, tr, D), lambda i: (layer, i, 0)),
        out_shape=jax.ShapeDtypeStruct((DEPTH, R, D), F32),
        input_output_aliases={} if into is None else {2: 0},
        compiler_params=_params(("parallel",)))(*args)


def _allreduce_small(vec):
    R, C = vec.shape

    def body(v_ref, o_ref, buf, send_sems, recv_sems):
        me, peers = _me_and_peers()
        buf[me] = v_ref[...]
        sends = []
        for k, (pid, _) in enumerate(peers):
            cp = pltpu.make_async_remote_copy(src_ref=v_ref, dst_ref=buf.at[me], send_sem=send_sems.at[k],
                                              recv_sem=recv_sems.at[k], device_id=pid, device_id_type=MESH)
            cp.start()
            sends.append(cp)
        for k, (pid, pnum) in enumerate(peers):
            pltpu.make_async_remote_copy(src_ref=v_ref, dst_ref=buf.at[pnum], send_sem=send_sems.at[k],
                                         recv_sem=recv_sems.at[k], device_id=pid, device_id_type=MESH).wait_recv()
        for cp in sends:
            cp.wait_send()
        acc = buf[0]
        for d in range(1, N_DEV):
            acc = acc + buf[d]
        o_ref[...] = acc

    vm = pl.BlockSpec(memory_space=pltpu.VMEM)
    return pl.pallas_call(
        body, name="allreduce_small",
        in_specs=[vm], out_specs=vm,
        out_shape=jax.ShapeDtypeStruct((R, C), F32),
        scratch_shapes=[pltpu.VMEM((N_DEV, R, C), F32), pltpu.SemaphoreType.DMA((N_DEV - 1,)),
                        pltpu.SemaphoreType.DMA((N_DEV - 1,))],
        compiler_params=pltpu.CompilerParams(has_side_effects=True),
    )(vec)


def _adamw_update(w, g, m, v):
    c1 = 1.0 - ADAM_B1 ** ADAM_STEP
    c2 = 1.0 - ADAM_B2 ** ADAM_STEP
    mn = ADAM_B1 * m + (1.0 - ADAM_B1) * g
    vn = ADAM_B2 * v + (1.0 - ADAM_B2) * (g * g)
    return -ADAM_LR * ((mn / c1) / (jnp.sqrt(vn / c2) + ADAM_EPS) + ADAM_WD * w), mn, vn


def _adamw(w, g, m, v, name):
    R, C = w.shape
    tr = _pick(R, (512, 400, 256, 128, 64, 32, 16, 8))

    def body(w_ref, g_ref, m_ref, v_ref, d_ref, mo_ref, vo_ref):
        d_ref[...], mo_ref[...], vo_ref[...] = _adamw_update(w_ref[...], g_ref[...], m_ref[...], v_ref[...])

    blk = pl.BlockSpec((tr, C), lambda i: (i, 0))
    sh = jax.ShapeDtypeStruct((R, C), F32)
    return pl.pallas_call(
        body, name=name, grid=(R // tr,), in_specs=[blk] * 4, out_specs=[blk] * 3, out_shape=[sh] * 3,
        compiler_params=_params(("parallel",)),
    )(w, g, m, v)


def _adamw_whole(ws, gs, ms, vs, name):
    n = len(ws)

    def body(*refs):
        for j in range(n):
            outs = refs[4 * n + 3 * j:4 * n + 3 * j + 3]
            outs[0][...], outs[1][...], outs[2][...] = _adamw_update(*(r[...] for r in refs[4 * j:4 * j + 4]))

    vm = pl.BlockSpec(memory_space=pltpu.VMEM)
    res = pl.pallas_call(
        body, name=name, in_specs=[vm] * (4 * n), out_specs=[vm] * (3 * n),
        out_shape=[jax.ShapeDtypeStruct(w.shape, F32) for w in ws for _ in range(3)],
    )(*[a for four in zip(ws, gs, ms, vs) for a in four])
    return [tuple(res[3 * j:3 * j + 3]) for j in range(n)]


def _lb_param_grad(lb_param, dlb):
    L, C = lb_param.shape

    def body(p_ref, d_ref, o_ref):
        lbp = p_ref[...]
        d = d_ref[...]
        mx = jnp.max(lbp, axis=0, keepdims=True)
        e = jnp.exp(lbp - mx)
        p = e / jnp.sum(e, axis=0, keepdims=True)
        tot = jnp.sum(d, axis=0, keepdims=True)
        dps = []
        rest = tot
        for j in range(L):
            dps.append(rest - tot if j == 0 else rest)
            rest = rest - d[j:j + 1]
        dp = jnp.concatenate(dps, axis=0)
        o_ref[...] = p * (dp - jnp.sum(p * dp, axis=0, keepdims=True))

    vm = pl.BlockSpec(memory_space=pltpu.VMEM)
    return pl.pallas_call(body, name="lb_param_grad", in_specs=[vm, vm], out_specs=vm,
                          out_shape=jax.ShapeDtypeStruct((L, C), F32))(lb_param, dlb)


def _pack_small(loss_part, dg_pre, dg_post, dlb, dg_head, dsinks):
    pad8 = lambda a: jnp.pad(a.reshape(-1, 128), ((0, 8 - DEPTH), (0, 0)))
    rows = [dg_pre.reshape(-1, 128), dg_post.reshape(-1, 128), dlb.reshape(-1, 128), pad8(dg_head), pad8(dsinks),
            loss_part]
    return jnp.concatenate(rows, axis=0)


def _unpack_small(vec):
    n = DEPTH * D_MODEL // 128
    o = 0
    dg_pre = vec[o:o + n].reshape(DEPTH, D_MODEL); o += n
    dg_post = vec[o:o + n].reshape(DEPTH, D_MODEL); o += n
    dlb = vec[o:o + n].reshape(DEPTH, HG_WIDTH); o += n
    dg_head = vec[o:o + DEPTH]; o += 8
    dsinks = vec[o:o + DEPTH, :ATT_HEADS]; o += 8
    loss = jnp.sum(vec[o:o + 8])
    return loss, dg_pre, dg_post, dlb, dg_head, dsinks


def kernel(x, w_in, w_out, g_pre, g_post, lb_param, g_head, sinks, loss_target, m_w_in, m_w_out, m_g_pre, m_g_post, m_lb_param, m_g_head, m_sinks, v_w_in, v_w_out, v_g_pre, v_g_post, v_lb_param, v_g_head, v_sinks):
    tr = lambda a: jnp.swapaxes(a, 1, 2)
    w_in_t = tr(w_in)
    (loss_part, dx, gw_in_t, gw_out, dg_pre, dg_post, dlb, dg_head, dsinks) = _step(
        x, loss_target, g_pre, g_post, lb_param, g_head, sinks, shards=(w_in_t, w_out))

    small = _allreduce_small(_pack_small(loss_part, dg_pre, dg_post, dlb, dg_head, dsinks))
    loss, gg_pre, gg_post, gdlb, gg_head, gsinks = _unpack_small(small)
    glb = _lb_param_grad(lb_param, gdlb)

    grads = [gw_in_t, gw_out, gg_pre, gg_post, glb, gg_head, gsinks]
    ws = [w_in_t, w_out, g_pre, g_post, lb_param, g_head, sinks]
    ms = [tr(m_w_in), m_w_out, m_g_pre, m_g_post, m_lb_param, m_g_head, m_sinks]
    vs = [tr(v_w_in), v_w_out, v_g_pre, v_g_post, v_lb_param, v_g_head, v_sinks]
    deltas, new_m, new_v = [], [], []
    big = 2
    for w, g, m, v, nm in zip(ws[:big], grads, ms, vs, ("w_in", "w_out")):
        sh = w.shape
        two = lambda a: a.reshape(-1, sh[-1])
        d, mn, vn = _adamw(two(w), two(g), two(m), two(v), "adamw_" + nm)
        deltas.append(d.reshape(sh))
        new_m.append(mn.reshape(sh))
        new_v.append(vn.reshape(sh))
    for d, mn, vn in _adamw_whole(ws[big:], grads[big:], ms[big:], vs[big:], "adamw_vectors"):
        deltas.append(d)
        new_m.append(mn)
        new_v.append(vn)
    grads[0], deltas[0], new_m[0], new_v[0] = tr(grads[0]), tr(deltas[0]), tr(new_m[0]), tr(new_v[0])
    return (loss, dx, *grads, *deltas, *new_m, *new_v)
```

```python
import math

import numpy as np
import jax
import jax.numpy as jnp
from jax import lax
from jax.experimental import pallas as pl
from jax.experimental.pallas import tpu as pltpu

F32 = jnp.float32
BF16 = jnp.bfloat16

D_MODEL = 1024
DEPTH = 2
HG_HEADS = 8
HG_DIM = 128
HG_WIDTH = HG_HEADS * HG_DIM
CHUNK = 64
ATT_HEADS = 16
ATT_DIM = 64
ATT_WIDTH = ATT_HEADS * ATT_DIM
KV_WIDTH = 128
ATT_BLOCK = 128
ATT_SCALE = 1.0 / math.sqrt(ATT_DIM)
ROPE_THETA = 10000.0
NORM_EPS = 1e-6
NEG_INF = -1e30
LB_FLOOR = 1e-20
N_H = 4 * HG_WIDTH
N_A = 2 * ATT_WIDTH + 2 * KV_WIDTH
IN_WIDTH = N_H + N_A
MIX_WIDTH = HG_WIDTH + ATT_WIDTH

ADAM_LR = 0.001
ADAM_B1 = 0.9
ADAM_B2 = 0.999
ADAM_EPS = 1e-08
ADAM_WD = 0.01
ADAM_STEP = 10

N_DEV = 8
MESH = pl.DeviceIdType.MESH
VMEM_LIMIT = 56 * 1024 * 1024

NN = ((1,), (0,))
NT = ((1,), (1,))
TN = ((0,), (0,))


def _dot(a, b, dims):
    return lax.dot_general(a.astype(BF16), b.astype(BF16), (dims, ((), ())), preferred_element_type=F32)


def _params(sem=None, **kw):
    return pltpu.CompilerParams(dimension_semantics=sem, vmem_limit_bytes=VMEM_LIMIT, **kw)


def _sigmoids(x):
    e = jnp.exp(-jnp.abs(x))
    r = 1.0 / (1.0 + e)
    er = e * r
    pos = x >= 0.0
    return jnp.where(pos, r, er), jnp.where(pos, er, r)


def _silu(x):
    return x * _sigmoids(x)[0]


def _silu_and_grad(x):
    s, ns = _sigmoids(x)
    return x * s, s * (1.0 + x * ns)


def _pick(n, prefs):
    for p in prefs:
        if n % p == 0:
            return p
    return n


def _inproj(x2, g, layer, w, name):
    T, D = x2.shape
    tm = _pick(T, (512, 256, 128))
    nchunk = 1024

    def body(x_ref, g_ref, w_ref, oh_ref, oa_ref, h_ref):
        x = x_ref[...]
        r = lax.rsqrt(jnp.mean(x * x, axis=-1, keepdims=True) + NORM_EPS)
        h = ((x * r) * g_ref[layer:layer + 1, :]).astype(BF16)
        h_ref[...] = h
        for j in range(0, N_H, nchunk):
            oh_ref[:, j:j + nchunk] = lax.dot_general(h, w_ref[j:j + nchunk, :], (NT, ((), ())),
                                                      preferred_element_type=F32)
        for j in range(0, N_A, N_A // 2):
            oa_ref[:, j:j + N_A // 2] = lax.dot_general(h, w_ref[N_H + j:N_H + j + N_A // 2, :], (NT, ((), ())),
                                                        preferred_element_type=F32)

    row = lambda w_: pl.BlockSpec((tm, w_), lambda i: (i, 0))
    return pl.pallas_call(
        body, name=name,
        grid=(T // tm,),
        in_specs=[row(D), pl.BlockSpec((DEPTH, D), lambda i: (0, 0)),
                  pl.BlockSpec((IN_WIDTH, D), lambda i: (0, 0), pipeline_mode=pl.Buffered(1))],
        out_specs=[row(N_H), row(N_A), row(D)],
        out_shape=[jax.ShapeDtypeStruct((T, N_H), F32), jax.ShapeDtypeStruct((T, N_A), F32),
                   jax.ShapeDtypeStruct((T, D), BF16)],
        compiler_params=_params(("parallel",)),
    )(x2, g, w)


def _mm_tn(pieces, b, name, out_dtype=BF16):
    T, m = b.shape
    tn = 256
    slots = 3
    counts = [p.shape[1] // tn for p in pieces]
    starts = [sum(counts[:i]) for i in range(len(pieces))]
    n_p = len(pieces)
    n = sum(counts)
    assert n >= slots

    def body(*refs):
        b_ref, o_ref, ring, sems = refs[n_p:]
        i = pl.program_id(0)

        def tile_copy(p, col, slot):
            return pltpu.make_async_copy(refs[p].at[:, pl.ds(col, tn)], ring.at[slot], sems.at[slot])

        def fetch(t):
            for p in range(n_p):
                @pl.when((t >= starts[p]) & (t < starts[p] + counts[p]))
                def _(p=p):
                    tile_copy(p, pl.multiple_of((t - starts[p]) * tn, tn), t % slots).start()

        @pl.when(i == 0)
        def _():
            for t in range(slots - 1):
                fetch(jnp.int32(t))

        @pl.when(i + (slots - 1) < n)
        def _():
            fetch(i + (slots - 1))

        slot = i % slots
        tile_copy(0, 0, slot).wait()
        o_ref[...] = lax.dot_general(ring[slot], b_ref[...], (TN, ((), ())),
                                     preferred_element_type=F32).astype(out_dtype)

    return pl.pallas_call(
        body, name=name,
        grid=(n,),
        in_specs=[_ANY] * n_p + [pl.BlockSpec((T, m), lambda i: (0, 0), pipeline_mode=pl.Buffered(1))],
        out_specs=pl.BlockSpec((tn, m), lambda i: (i, 0)),
        out_shape=jax.ShapeDtypeStruct((n * tn, m), out_dtype),
        scratch_shapes=[pltpu.VMEM((slots, T, tn), BF16), pltpu.SemaphoreType.DMA((slots,))],
        compiler_params=_params(("arbitrary",)),
    )(*pieces, b)


_LEVELS = (0, 1, 2, 4, 8, 16, 32)
_CUM_L = (2, 4, 8, 16, 32, 64)
_ALL_KINDS = tuple(("c", L) for L in _CUM_L) + tuple(("r", L) for L in _CUM_L)
_MXU_KINDS = (("c", 2), ("c", 4), ("c", CHUNK), ("r", 2), ("r", 4))
N_CUM = len(_ALL_KINDS) * CHUNK
N_CUM_F = len(_MXU_KINDS) * CHUNK


def _cum_matrices():
    t = np.arange(CHUNK)[:, None]
    r = np.arange(CHUNK)[None, :]

    def mat(kind):
        c, L = kind
        return ((r // L == t // L) & ((r <= t) if c == "c" else (r > t))).astype(np.float32)

    fwd = np.concatenate([mat(kd) for kd in _MXU_KINDS], axis=0)
    full = np.concatenate([mat(kd) for kd in _ALL_KINDS], axis=0)
    return jnp.asarray(fwd, BF16), jnp.asarray(full.T.copy(), BF16)


def _level_masks():
    t = np.arange(CHUNK)[:, None]
    s = np.arange(CHUNK)[None, :]
    ms = []
    for L in _LEVELS:
        if L == 0:
            ms.append(t == s)
        else:
            ms.append((t // (2 * L) == s // (2 * L)) & ((t // L) % 2 == 1) & ((s // L) % 2 == 0))
    return jnp.asarray(np.stack(ms).astype(np.float32))


def _split3(x):
    hi = x.astype(BF16)
    r1 = x - hi.astype(F32)
    mid = r1.astype(BF16)
    lo = (r1 - mid.astype(F32)).astype(BF16)
    return hi, mid, lo


def _cum3(ts, x, terms=3):
    d = lambda p: lax.dot_general(ts, p, (NN, ((), ())), preferred_element_type=F32)
    return sum(d(p) for p in _split3(x)[:terms])


def _lb_terms(lbp, layer):
    mx = jnp.max(lbp, axis=0, keepdims=True)
    e = jnp.exp(lbp - mx)
    p = e / jnp.sum(e, axis=0, keepdims=True)
    cum = p[0:1]
    for j in range(1, layer + 1):
        cum = cum + p[j:j + 1]
    lb = cum - p[0:1]
    lbf = jnp.maximum(lb, LB_FLOOR)
    return dict(lbf=lbf, one_m=1.0 - lb, kcorr=lb - lbf, ind=jnp.where(lb > LB_FLOOR, 1.0, 0.0))


def _gate(x, lt):
    sig, nsig = _sigmoids(x)
    f = lt["lbf"] + lt["one_m"] * sig
    return jnp.log(f), lt["one_m"] * nsig + lt["kcorr"], f, sig, nsig


def _ck(x, ci):
    return x[ci * CHUNK:(ci + 1) * CHUNK]


def _block_cums(ts, g, nc):
    cs = [_cum3(ts, _ck(g, ci), terms=2) for ci in range(nc)]
    out = {kind: jnp.concatenate([c[CHUNK * i:CHUNK * (i + 1)] for c in cs], axis=0)
           for i, kind in enumerate(_MXU_KINDS)}
    b = out[("c", CHUNK)]
    ng = CHUNK // 8
    last = b.reshape(nc, ng, 8, HG_DIM)[:, :, 7:8, :]
    zero = jnp.zeros((nc, 1, 1, HG_DIM), F32)

    def spread(groups):
        return jnp.broadcast_to(jnp.concatenate(groups, axis=1), (nc, ng, 8, HG_DIM)).reshape(nc * CHUNK, HG_DIM)

    def get(kind):
        if kind in out:
            return out[kind]
        c, L = kind
        nb = L // 8
        first = lambda r: (r // nb) * nb
        if c == "c":
            return b - spread([last[:, first(r) - 1:first(r)] if r >= nb else zero for r in range(ng)])
        return spread([last[:, first(r) + nb - 1:first(r) + nb] for r in range(ng)]) - b

    return get


def _level_factors(cums, g, L):
    if L == 0:
        return None, None
    if L == 1:
        return jnp.exp(g[...]), None
    return jnp.exp(cums(("c", L))), jnp.exp(cums(("r", L)))


def _mul(a, e):
    return a if e is None else a * e


def _hg_block_fwd(qf, k, v, g, ts, m_ref, nc):
    cums = _block_cums(ts, g, nc)
    amat = [jnp.zeros((CHUNK, CHUNK), F32)] * nc
    for li, L in enumerate(_LEVELS):
        eq, ek = _level_factors(cums, g, L)
        ql, kl, m = _mul(qf, eq), _mul(k, ek), m_ref[li]
        amat = [amat[ci] + _dot(_ck(ql, ci), _ck(kl, ci), NT) * m for ci in range(nc)]
    b = cums(("c", CHUNK))
    kst = k * jnp.exp(cums(("r", CHUNK)))
    o = [_dot(amat[ci], _ck(v, ci), NN) for ci in range(nc)]
    kv = [_dot(_ck(v, ci), _ck(kst, ci), TN) for ci in range(nc)]
    dec = [jnp.exp(b[(ci + 1) * CHUNK - 1:(ci + 1) * CHUNK, :]) for ci in range(nc)]
    return o, dec, kv, qf * jnp.exp(b), amat


def _hg_block_bwd(qf, k, v, g, do, amat, ts, m_ref, nc):
    cums = _block_cums(ts, g, nc)
    dcs = {}
    da = [_dot(_ck(do, ci), _ck(v, ci), NT) for ci in range(nc)]
    dq = jnp.zeros(qf.shape, F32)
    dk = jnp.zeros(qf.shape, F32)
    dg = jnp.zeros(qf.shape, F32)
    for li, L in enumerate(_LEVELS):
        eq, ek = _level_factors(cums, g, L)
        qlb, klb, m = _mul(qf[...], eq).astype(BF16), _mul(k[...], ek).astype(BF16), m_ref[li]
        dal = [(da[ci] * m).astype(BF16) for ci in range(nc)]
        both = [(_dot(dal[ci], _ck(klb, ci), NN), _dot(dal[ci], _ck(qlb, ci), TN)) for ci in range(nc)]
        dql = _mul(jnp.concatenate([p[0] for p in both], axis=0), eq)
        dkl = _mul(jnp.concatenate([p[1] for p in both], axis=0), ek)
        dq = dq + dql
        dk = dk + dkl
        if L == 1:
            dg = dg + dql * qf[...]
        elif L > 1:
            dcs[("c", L)] = (dql * qf[...]).astype(BF16)
            dcs[("r", L)] = (dkl * k[...]).astype(BF16)
    b = cums(("c", CHUNK))
    e64 = jnp.exp(b)
    er64 = jnp.exp(cums(("r", CHUNK)))
    qb = (qf[...] * e64).astype(BF16)
    return dict(dq=dq, dk=dk, dg=dg, dcs=dcs, e64=e64, er64=er64, qf=qf, k=k, kst=(k[...] * er64).astype(BF16),
                dv=[_dot(amat[ci], _ck(do, ci), TN) for ci in range(nc)],
                dec=[jnp.exp(b[(ci + 1) * CHUNK - 1:(ci + 1) * CHUNK, :]) for ci in range(nc)],
                qd=[_dot(_ck(do, ci), _ck(qb, ci), TN) for ci in range(nc)])


def _hg_state_bwd(w, v, do, starts, ends, tst, nc):
    dqb = jnp.concatenate([_dot(_ck(do, ci), starts[ci], NN) for ci in range(nc)], axis=0)
    dkst = jnp.concatenate([_dot(_ck(v, ci), ends[ci], NN) for ci in range(nc)], axis=0)
    dqb, dkst = dqb * w["e64"], dkst * w["er64"]
    dq = w["dq"] + dqb
    dk = w["dk"] + dkst
    dv = jnp.concatenate([w["dv"][ci] + _dot(_ck(w["kst"], ci), ends[ci], NT) for ci in range(nc)], axis=0)
    trow = lax.broadcasted_iota(jnp.int32, (CHUNK, 1), 0)
    dtot = jnp.concatenate(
        [jnp.where(trow == CHUNK - 1, jnp.sum(ends[ci] * starts[ci], axis=0, keepdims=True) * w["dec"][ci], 0.0)
         for ci in range(nc)], axis=0)
    dcs = dict(w["dcs"])
    dcs[("c", CHUNK)] = (dqb * w["qf"][...] + dtot).astype(BF16)
    dcs[("r", CHUNK)] = (dkst * w["k"][...]).astype(BF16)
    dgs = [_dot(tst, jnp.concatenate([_ck(dcs[kind], ci) for kind in _ALL_KINDS], axis=0), NN) for ci in range(nc)]
    return dq, dk, dv, w["dg"] + jnp.concatenate(dgs, axis=0)


def _hgrn_fwd(proj_h, u_rows, lb_param, g_head, layer, name, phase=None):
    B, S, _ = proj_h.shape
    sb = _pick(S, (2048, 1024, 512, 256, 128, 64))
    nc = sb // CHUNK
    ts, _ = _cum_matrices()

    def body(*refs):
        ins, outs, (st,), p_in, p_out, p_sems = _split_refs(refs, 8, 12, 1, phase)
        q_ref, f_ref, i_ref, z_ref, lbp_ref, gh_ref, ts_ref, m_ref = ins
        o_ref, u_ref, sts_ref, am_ref = outs[:4]
        logf_ref, k_ref, qf_ref, sg_ref, qg_ref, zg_ref, fg_ref, sig_ref = outs[4:]
        h_id, b_id, s_id = pl.program_id(0), pl.program_id(1), pl.program_id(2)
        _hosted_start(phase, p_in, p_out, p_sems, (h_id == 0) & (b_id == 0) & (s_id == 0))

        @pl.when(s_id == 0)
        def _():
            st[...] = jnp.zeros_like(st)

        lt = _lb_terms(lbp_ref[...], layer)
        tsv = ts_ref[...]
        gh = gh_ref[layer:layer + 1, :]
        logf, k, _, sig, nsig = _gate(f_ref[...], lt)
        qf, qf_grad = _silu_and_grad(q_ref[...])
        sg, sg_grad = _silu_and_grad(z_ref[...])
        logf_ref[...], k_ref[...], qf_ref[...], sg_ref[...] = logf, k, qf, sg
        qg_ref[...] = qf_grad.astype(BF16)
        zg_ref[...] = sg_grad.astype(BF16)
        fg_ref[...] = (lt["one_m"] * sig * nsig).astype(BF16)
        sig_ref[...] = sig.astype(BF16)
        o_part, dec, kv, qb, amat = _hg_block_fwd(qf, k, i_ref[...], logf, tsv, m_ref, nc)
        for ci in range(nc):
            am_ref[ci] = amat[ci].astype(BF16)
        cur = st[...]
        starts = []
        for ci in range(nc):
            sts_ref[ci] = cur
            starts.append(cur)
            cur = cur * dec[ci] + kv[ci]
        st[...] = cur
        o = jnp.concatenate([o_part[ci] + _dot(_ck(qb, ci), starts[ci], NT) for ci in range(nc)], axis=0)
        o_ref[...] = o
        r = lax.rsqrt(jnp.mean(o * o, axis=-1, keepdims=True) + NORM_EPS)
        u_ref[...] = (((o * r) * gh) * sg).astype(BF16)
        _hosted_finish(phase, p_in, p_out, p_sems, (h_id == HG_HEADS - 1) & (b_id == B - 1) & (s_id == S // sb - 1))

    col = lambda base: pl.BlockSpec((None, sb, HG_DIM), lambda h, b, s: (b, s, base + h))
    p_ispecs, p_ospecs, p_oshapes, p_alias, p_scratch, p_args = _host_phase(phase, 8, 12)
    wide = lambda dt: jax.ShapeDtypeStruct((B, S, HG_WIDTH), dt)
    res = pl.pallas_call(
        body, name=name,
        grid=(HG_HEADS, B, S // sb),
        in_specs=[col(0), col(HG_HEADS), col(2 * HG_HEADS), col(3 * HG_HEADS),
                  pl.BlockSpec((DEPTH, HG_DIM), lambda h, b, s: (0, h)),
                  pl.BlockSpec((DEPTH, HG_DIM), lambda h, b, s: (0, 0)),
                  pl.BlockSpec((N_CUM_F, CHUNK), lambda h, b, s: (0, 0)),
                  pl.BlockSpec((len(_LEVELS), CHUNK, CHUNK), lambda h, b, s: (0, 0, 0))] + p_ispecs,
        out_specs=[col(0), col(0),
                   pl.BlockSpec((None, None, nc, HG_DIM, HG_DIM), lambda h, b, s: (b, h, s, 0, 0)),
                   pl.BlockSpec((None, None, nc, CHUNK, CHUNK), lambda h, b, s: (b, h, s, 0, 0))]
        + [col(0)] * 8 + p_ospecs,
        out_shape=[wide(F32),
                   jax.ShapeDtypeStruct((B, S, u_rows), BF16),
                   jax.ShapeDtypeStruct((B, HG_HEADS, S // CHUNK, HG_DIM, HG_DIM), F32),
                   jax.ShapeDtypeStruct((B, HG_HEADS, S // CHUNK, CHUNK, CHUNK), BF16)]
        + [wide(F32)] * 4 + [wide(BF16)] * 4 + p_oshapes,
        input_output_aliases=p_alias,
        scratch_shapes=[pltpu.VMEM((HG_DIM, HG_DIM), F32)] + p_scratch,
        compiler_params=_params(("arbitrary", "arbitrary", "arbitrary")),
    )(proj_h, proj_h, proj_h, proj_h, lb_param, g_head, ts, _level_masks(), *p_args)
    return res[0], res[1], tuple(res[2:12]), list(res[12:])


def _hgrn_bwd(proj_h, o_h, du, kept, lb_param, g_head, layer, name, phase=None):
    B, S, _ = proj_h.shape
    sb = _pick(S, (512, 256, 128, 64))
    nc = sb // CHUNK
    ns = S // sb
    ts, tst = _cum_matrices()

    def body(*refs):
        ins, outs, (dst,), p_in, p_out, p_sems = _split_refs(refs, 18, 6, 1, phase)
        (i_ref, o_ref, du_ref, sts_ref, am_ref, logf_ref, k_ref, qf_ref, sg_ref, qg_ref, zg_ref, fg_ref, sig_ref,
         lbp_ref, gh_ref, ts_ref, tst_ref, m_ref) = ins
        dq_ref, df_ref, di_ref, dz_ref, dlb_ref, dgh_ref = outs
        h_id, b_id, s_id = pl.program_id(0), pl.program_id(1), pl.program_id(2)
        _hosted_start(phase, p_in, p_out, p_sems, (h_id == 0) & (b_id == 0) & (s_id == 0))

        @pl.when(s_id == 0)
        def _():
            dst[...] = jnp.zeros_like(dst)

        @pl.when((b_id == 0) & (s_id == 0))
        def _():
            dlb_ref[...] = jnp.zeros_like(dlb_ref)

        @pl.when((h_id == 0) & (b_id == 0) & (s_id == 0))
        def _():
            dgh_ref[...] = jnp.zeros_like(dgh_ref)

        lt = _lb_terms(lbp_ref[...], layer)
        gh = gh_ref[layer:layer + 1, :]
        tsv = ts_ref[...]
        tstv = tst_ref[...]
        sg = sg_ref[...]
        o = o_ref[...]
        dub = du_ref[...]
        r = lax.rsqrt(jnp.mean(o * o, axis=-1, keepdims=True) + NORM_EPS)
        n = o * r
        dz_ref[...] = (dub * (n * gh) * zg_ref[...].astype(F32)).astype(BF16)
        dgh_ref[...] += jnp.sum(dub * sg * n, axis=0, keepdims=True)
        dn = dub * sg * gh
        do = (r * (dn - n * jnp.mean(dn * n, axis=-1, keepdims=True))).astype(BF16)
        v = i_ref[...].astype(BF16)
        w = _hg_block_bwd(qf_ref, k_ref, v, logf_ref, do, [am_ref[ci] for ci in range(nc)], tsv, m_ref, nc)
        cur = dst[...]
        ends = [None] * nc
        for ci in reversed(range(nc)):
            ends[ci] = cur
            cur = cur * w["dec"][ci] + w["qd"][ci]
        dst[...] = cur
        dq, dk, dv, dg = _hg_state_bwd(w, v, do, [sts_ref[ci] for ci in range(nc)], ends, tstv, nc)
        di_ref[...] = dv.astype(BF16)
        dq_ref[...] = (dq * qg_ref[...].astype(F32)).astype(BF16)
        f = jnp.exp(logf_ref[...])
        scaled = (dg - f * dk) / f
        df_ref[...] = (scaled * fg_ref[...].astype(F32)).astype(BF16)
        dlb_ref[...] += jnp.sum(scaled * (lt["ind"] - sig_ref[...].astype(F32)), axis=0, keepdims=True)
        _hosted_finish(phase, p_in, p_out, p_sems, (h_id == HG_HEADS - 1) & (b_id == B - 1) & (s_id == ns - 1))

    col = lambda base: pl.BlockSpec((None, sb, HG_DIM), lambda h, b, s: (b, ns - 1 - s, base + h))
    out_col = pl.BlockSpec((None, sb, HG_DIM), lambda h, b, s: (b, ns - 1 - s, h))
    dt = jax.ShapeDtypeStruct((B, S, HG_WIDTH), BF16)
    p_ispecs, p_ospecs, p_oshapes, p_alias, p_scratch, p_args = _host_phase(phase, 18, 6)
    res = pl.pallas_call(
        body, name=name,
        grid=(HG_HEADS, B, ns),
        in_specs=[col(2 * HG_HEADS), col(0), col(0),
                  pl.BlockSpec((None, None, nc, HG_DIM, HG_DIM), lambda h, b, s: (b, h, ns - 1 - s, 0, 0)),
                  pl.BlockSpec((None, None, nc, CHUNK, CHUNK), lambda h, b, s: (b, h, ns - 1 - s, 0, 0))]
        + [col(0)] * 8
        + [pl.BlockSpec((DEPTH, HG_DIM), lambda h, b, s: (0, h)),
           pl.BlockSpec((DEPTH, HG_DIM), lambda h, b, s: (0, 0)),
           pl.BlockSpec((N_CUM_F, CHUNK), lambda h, b, s: (0, 0)),
           pl.BlockSpec((CHUNK, N_CUM), lambda h, b, s: (0, 0)),
           pl.BlockSpec((len(_LEVELS), CHUNK, CHUNK), lambda h, b, s: (0, 0, 0))] + p_ispecs,
        out_specs=[out_col, out_col, out_col, out_col,
                   pl.BlockSpec((1, HG_DIM), lambda h, b, s: (0, h)),
                   pl.BlockSpec((1, HG_DIM), lambda h, b, s: (0, 0))] + p_ospecs,
        out_shape=[dt, dt, dt, dt, jax.ShapeDtypeStruct((1, HG_WIDTH), F32),
                   jax.ShapeDtypeStruct((1, HG_DIM), F32)] + p_oshapes,
        input_output_aliases=p_alias,
        scratch_shapes=[pltpu.VMEM((HG_DIM, HG_DIM), F32)] + p_scratch,
        compiler_params=_params(("arbitrary", "arbitrary", "arbitrary")),
    )(proj_h, o_h, du, *kept, lb_param, g_head, ts, tst, _level_masks(), *p_args)
    return tuple(res[:6]) + (list(res[6:]),)


def _rope_tables(S):
    half = ATT_DIM // 2
    inv_freq = np.float32(ROPE_THETA) ** (-np.arange(half, dtype=np.float32) / half)
    ang = np.arange(S, dtype=np.float32)[:, None] * inv_freq[None, :]
    cos = np.cos(ang)
    sin = np.sin(ang)
    cos = np.concatenate([cos, cos, cos, cos], axis=1)
    sin = np.concatenate([-sin, sin, -sin, sin], axis=1)
    return jnp.asarray(cos, F32), jnp.asarray(sin, F32)


def _attn_common():
    lane = lax.broadcasted_iota(jnp.int32, (1, 2 * ATT_DIM), 1)
    first_half = (lane % ATT_DIM) < (ATT_DIM // 2)
    left = lane < ATT_DIM

    def swap(x):
        return jnp.where(first_half, pltpu.roll(x, 128 - ATT_DIM // 2, 1), pltpu.roll(x, ATT_DIM // 2, 1))

    def rope(x, cos, sin):
        return x * cos + swap(x) * sin

    def rope_bwd(dy, cos, sin):
        return dy * cos + swap(dy * sin)

    def dup(x):
        xs = pltpu.roll(x, ATT_DIM, 1)
        return [jnp.where(left, x, xs), jnp.where(left, xs, x)]

    return left, rope, rope_bwd, dup


GROUP = ATT_HEADS // 2
GROUP_ROWS = GROUP * ATT_BLOCK


def _attn_bias(i):
    r = lax.broadcasted_iota(jnp.int32, (ATT_BLOCK, 2 * ATT_BLOCK), 0)
    c = lax.broadcasted_iota(jnp.int32, (ATT_BLOCK, 2 * ATT_BLOCK), 1)
    ok = (c > r) & (c <= r + ATT_BLOCK) & ((c >= ATT_BLOCK) | (i > 0))
    return jnp.where(ok, 0.0, NEG_INF)


def _stack_heads(pairs, left):
    rows = []
    for x in pairs:
        rows += [jnp.where(left, x, 0.0), jnp.where(left, 0.0, x)]
    return jnp.concatenate(rows, axis=0)


def _unstack_heads(y, left, pp):
    r0 = 2 * pp * ATT_BLOCK
    return jnp.where(left, y[r0:r0 + ATT_BLOCK], y[r0 + ATT_BLOCK:r0 + 2 * ATT_BLOCK])


def _row_sums(x):
    return _dot(x, jnp.ones((x.shape[1], 128), BF16), NN)


def _attn_probs(qs, kd, vd, sink, bias):
    n = range(len(qs))
    rows = qs[0].shape[0]
    s = [(_dot(qs[j], kd[j], NT).reshape(rows // ATT_BLOCK, ATT_BLOCK, 2 * ATT_BLOCK) * ATT_SCALE + bias[None])
         .reshape(rows, 2 * ATT_BLOCK) for j in n]
    m = [jnp.max(jnp.maximum(jnp.maximum(s[j][:, :128], s[j][:, 128:]), sink[j]), axis=-1, keepdims=True) for j in n]
    pu = [jnp.exp(s[j] - m[j]) for j in n]
    es = [jnp.exp(sink[j] - m[j]) for j in n]
    ones = jnp.ones((2 * ATT_BLOCK, 128), BF16)
    ov = [_dot(pu[j], jnp.concatenate([vd[j].astype(BF16), ones], axis=1), NN) for j in n]
    inv = [1.0 / (ov[j][:, 128:] + es[j]) for j in n]
    return ([pu[j] * jnp.concatenate([inv[j], inv[j]], axis=1) for j in n], [es[j] * inv[j] for j in n],
            [ov[j][:, :128] * inv[j] for j in n])


def _sink_rows(sinks):
    return jnp.broadcast_to(jnp.repeat(sinks, ATT_BLOCK, axis=1)[:, :, None], (DEPTH, ATT_HEADS * ATT_BLOCK, 128))


_Z0 = (2 * ATT_WIDTH + 2 * KV_WIDTH - ATT_WIDTH) // 256


def _attn_fwd(proj_a, u, sink_rows, layer, cos, sin, name, phase=None):
    B, S, _ = proj_a.shape
    nb = S // ATT_BLOCK

    def body(*refs):
        ins, (u_ref, p_ref, o_ref, ps_ref, qs_ref), _, p_in, p_out, p_sems = _split_refs(refs, 13, 5, 0, phase)
        q_ref, kvc_ref, kvp_ref, z0, z1, z2, z3, cos_ref, sin_ref, cosp_ref, sinp_ref, sinks_ref, _ = ins
        i = pl.program_id(1)
        _hosted_start(phase, p_in, p_out, p_sems, (pl.program_id(0) == 0) & (i == 0))
        left, rope, _, dup = _attn_common()
        cos_c, sin_c = cos_ref[...], sin_ref[...]
        kvc = kvc_ref[...]
        kvp = kvp_ref[...]
        kw = jnp.concatenate([rope(kvp[:, :KV_WIDTH], cosp_ref[...], sinp_ref[...]),
                              rope(kvc[:, :KV_WIDTH], cos_c, sin_c)], axis=0)
        vw = jnp.concatenate([kvp[:, KV_WIDTH:], kvc[:, KV_WIDTH:]], axis=0)
        kd, vd = dup(kw), dup(vw)
        bias = _attn_bias(i)
        zs = (z0, z1, z2, z3)
        pairs = [range(4 * kvh, 4 * kvh + 4) for kvh in range(2)]
        qs = [_stack_heads([rope(q_ref[:, 128 * pr:128 * (pr + 1)], cos_c, sin_c) for pr in pairs[kvh]], left)
              for kvh in range(2)]
        sink = [sinks_ref[kvh * GROUP_ROWS:(kvh + 1) * GROUP_ROWS, :] for kvh in range(2)]
        p, ps, o = _attn_probs(qs, kd, vd, sink, bias)
        eye = (lax.broadcasted_iota(jnp.int32, (ATT_BLOCK, 128), 0)
               == lax.broadcasted_iota(jnp.int32, (ATT_BLOCK, 128), 1))
        for kvh in range(2):
            p_ref[kvh] = p[kvh].astype(BF16)
            qs_ref[kvh] = qs[kvh].astype(BF16)
            for g in range(GROUP):
                blk = ps[kvh][g * ATT_BLOCK:(g + 1) * ATT_BLOCK, :]
                ps_ref[kvh * GROUP + g:kvh * GROUP + g + 1, :] = jnp.sum(jnp.where(eye, blk, 0.0), axis=0, keepdims=True)
            for pp, pr in enumerate(pairs[kvh]):
                z = zs[pr // 2][:, 128 * (pr % 2):128 * (pr % 2 + 1)]
                o128 = _unstack_heads(o[kvh], left, pp)
                o_ref[:, 128 * pr:128 * (pr + 1)] = o128.astype(BF16)
                u_ref[:, 128 * pr:128 * (pr + 1)] = (o128 * _silu(z)).astype(BF16)
        _hosted_finish(phase, p_in, p_out, p_sems, (pl.program_id(0) == B - 1) & (i == nb - 1))

    rowblk = lambda w, cb: pl.BlockSpec((None, ATT_BLOCK, w), lambda b, i: (b, i, cb))
    tab = pl.BlockSpec((ATT_BLOCK, 128), lambda b, i: (i, 0))
    tabp = pl.BlockSpec((ATT_BLOCK, 128), lambda b, i: (jnp.maximum(i - 1, 0), 0))
    p_ispecs, p_ospecs, p_oshapes, p_alias, p_scratch, p_args = _host_phase(phase, 13, 5)
    res = pl.pallas_call(
        body, name=name,
        grid=(B, nb),
        in_specs=[rowblk(ATT_WIDTH, 0), rowblk(256, 4),
                  pl.BlockSpec((None, ATT_BLOCK, 256), lambda b, i: (b, jnp.maximum(i - 1, 0), 4)),
                  rowblk(256, _Z0), rowblk(256, _Z0 + 1), rowblk(256, _Z0 + 2), rowblk(256, _Z0 + 3),
                  tab, tab, tabp, tabp,
                  pl.BlockSpec((None, ATT_HEADS * ATT_BLOCK, 128), lambda b, i: (layer, 0, 0)),
                  pl.BlockSpec(memory_space=pl.ANY)] + p_ispecs,
        out_specs=[pl.BlockSpec((None, ATT_BLOCK, ATT_WIDTH), lambda b, i: (b, i, 1)),
                   pl.BlockSpec((None, None, 2, GROUP_ROWS, 2 * ATT_BLOCK), lambda b, i: (b, i, 0, 0, 0)),
                   pl.BlockSpec((None, ATT_BLOCK, ATT_WIDTH), lambda b, i: (b, i, 0)),
                   pl.BlockSpec((None, None, ATT_HEADS, 128), lambda b, i: (b, i, 0, 0)),
                   pl.BlockSpec((None, None, 2, GROUP_ROWS, 128), lambda b, i: (b, i, 0, 0, 0))] + p_ospecs,
        out_shape=[jax.ShapeDtypeStruct(u.shape, BF16),
                   jax.ShapeDtypeStruct((B, nb, 2, GROUP_ROWS, 2 * ATT_BLOCK), BF16),
                   jax.ShapeDtypeStruct((B, S, ATT_WIDTH), BF16),
                   jax.ShapeDtypeStruct((B, nb, ATT_HEADS, 128), F32),
                   jax.ShapeDtypeStruct((B, nb, 2, GROUP_ROWS, 128), BF16)] + p_oshapes,
        input_output_aliases={12: 0, **p_alias},
        scratch_shapes=p_scratch,
        compiler_params=_params(("arbitrary", "arbitrary")),
    )(proj_a, proj_a, proj_a, proj_a, proj_a, proj_a, proj_a, cos, sin, cos, sin, sink_rows, u, *p_args)
    return res[0], tuple(res[1:5]), list(res[5:])


def _attn_bwd(proj_a, du, kept, cos, sin, name, phase=None):
    B, S, _ = proj_a.shape
    nb = S // ATT_BLOCK
    p_kept, o_kept, ps_kept, qs_kept = kept

    def body(*refs):
        ins, outs, (carry, sk_acc), p_in, p_out, p_sems = _split_refs(refs, 15, 4, 2, phase)
        (qs_ref, kvc_ref, kvp_ref, z0, z1, z2, z3, du_ref, cos_ref, sin_ref, cosp_ref, sinp_ref,
         p_ref, o_ref, ps_ref) = ins
        dq_ref, dkv_ref, dz_ref, dsk_ref = outs
        b_id, i = pl.program_id(0), pl.program_id(1)
        _hosted_start(phase, p_in, p_out, p_sems, (b_id == 0) & (i == 0))

        @pl.when((b_id == 0) & (i == 0))
        def _():
            sk_acc[...] = jnp.zeros_like(sk_acc)

        @pl.when(i == 0)
        def _():
            carry[...] = jnp.zeros_like(carry)

        @pl.when(i < nb)
        def _():
            left, rope, rope_bwd, dup = _attn_common()
            cos_c, sin_c = cos_ref[...], sin_ref[...]
            cos_p, sin_p = cosp_ref[...], sinp_ref[...]
            kvc = kvc_ref[...]
            kvp = kvp_ref[...]
            kw = jnp.concatenate([rope(kvp[:, :KV_WIDTH], cos_p, sin_p), rope(kvc[:, :KV_WIDTH], cos_c, sin_c)], axis=0)
            vw = jnp.concatenate([kvp[:, KV_WIDTH:], kvc[:, KV_WIDTH:]], axis=0)
            kd, vd = dup(kw), dup(vw)
            zs = (z0, z1, z2, z3)
            units = [(kvh, hf) for kvh in range(2) for hf in range(2)]
            half = GROUP_ROWS // 2
            pairs = [range(4 * kvh + 2 * hf, 4 * kvh + 2 * hf + 2) for kvh, hf in units]
            ku = [kd[kvh] for kvh, _ in units]
            vu = [vd[kvh] for kvh, _ in units]
            ps_all = ps_ref[...]
            head_row = lax.broadcasted_iota(jnp.int32, (ATT_HEADS, 128), 0)
            eye = (lax.broadcasted_iota(jnp.int32, (ATT_BLOCK, 128), 0)
                   == lax.broadcasted_iota(jnp.int32, (ATT_BLOCK, 128), 1))

            def first(j):
                kvh, hf = units[j]
                p = p_ref[kvh, hf * half:(hf + 1) * half, :]
                parts = []
                for pr in pairs[j]:
                    cols = slice(128 * pr, 128 * (pr + 1))
                    sg, sg_grad = _silu_and_grad(zs[pr // 2][:, 128 * (pr % 2):128 * (pr % 2 + 1)])
                    du128 = du_ref[:, cols]
                    dz_ref[:, cols] = (du128 * o_ref[:, cols].astype(F32) * sg_grad).astype(BF16)
                    parts.append(du128 * sg)
                dos = _stack_heads(parts, left)
                dp = _dot(dos, vu[j], NT)
                delta = _row_sums(p.astype(F32) * dp)
                ds = (p.astype(F32) * (dp - jnp.concatenate([delta, delta], axis=1)) * ATT_SCALE).astype(BF16)
                sk = jnp.zeros((ATT_HEADS, 128), F32)
                for hh in range(4):
                    hd = kvh * GROUP + 4 * hf + hh
                    drow = jnp.sum(jnp.where(eye, delta[hh * ATT_BLOCK:(hh + 1) * ATT_BLOCK, :], 0.0), axis=0,
                                   keepdims=True)
                    sk = sk - jnp.where(head_row == hd, ps_all * drow, 0.0)
                sk_acc[...] += sk
                return ds, p, dos.astype(BF16), qs_ref[kvh, hf * half:(hf + 1) * half, :]

            def second(j, ds, p, dos, qs):
                dqs = _dot(ds, ku[j], NN)
                for pp, pr in enumerate(pairs[j]):
                    dq_ref[:, 128 * pr:128 * (pr + 1)] = rope_bwd(_unstack_heads(dqs, left, pp),
                                                                  cos_c, sin_c).astype(BF16)
                return _dot(ds, qs, TN), _dot(p, dos, TN)

            got, dku, dvu = {}, [None] * len(units), [None] * len(units)
            for j in range(len(units) + 1):
                if j < len(units):
                    got[j] = first(j)
                if j >= 1:
                    dku[j - 1], dvu[j - 1] = second(j - 1, *got.pop(j - 1))
            dkd = [dku[0] + dku[1], dku[2] + dku[3]]
            dvd = [dvu[0] + dvu[1], dvu[2] + dvu[3]]
            fold = lambda pr: jnp.where(left, pr[0] + pltpu.roll(pr[0], ATT_DIM, 1), pr[1] + pltpu.roll(pr[1], ATT_DIM, 1))
            dkw = fold(dkd)
            dvw = fold(dvd)
            prev = jnp.concatenate([rope_bwd(dkw[:ATT_BLOCK], cos_p, sin_p), dvw[:ATT_BLOCK]], axis=1)
            cur = jnp.concatenate([rope_bwd(dkw[ATT_BLOCK:], cos_c, sin_c), dvw[ATT_BLOCK:]], axis=1)
            dkv_ref[...] = (carry[...] + prev).astype(BF16)
            carry[...] = cur

        @pl.when(i == nb)
        def _():
            dkv_ref[...] = carry[...].astype(BF16)

        @pl.when((b_id == B - 1) & (i == nb))
        def _():
            diag = (lax.broadcasted_iota(jnp.int32, (ATT_HEADS, 128), 0)
                    == lax.broadcasted_iota(jnp.int32, (ATT_HEADS, 128), 1))
            tot = jnp.sum(sk_acc[...], axis=1, keepdims=True)
            dsk_ref[...] = jnp.sum(jnp.where(diag, tot, 0.0), axis=0, keepdims=True)

        _hosted_finish(phase, p_in, p_out, p_sems, (b_id == B - 1) & (i == nb))

    cl = lambda i: jnp.minimum(i, nb - 1)
    pv = lambda i: jnp.maximum(jnp.minimum(i, nb - 1) - 1, 0)
    rowblk = lambda w, cb: pl.BlockSpec((None, ATT_BLOCK, w), lambda b, i: (b, cl(i), cb))
    tab = pl.BlockSpec((ATT_BLOCK, 128), lambda b, i: (cl(i), 0))
    tabp = pl.BlockSpec((ATT_BLOCK, 128), lambda b, i: (pv(i), 0))
    p_ispecs, p_ospecs, p_oshapes, p_alias, p_scratch, p_args = _host_phase(phase, 15, 4)
    res = pl.pallas_call(
        body, name=name,
        grid=(B, nb + 1),
        in_specs=[pl.BlockSpec((None, None, 2, GROUP_ROWS, 128), lambda b, i: (b, cl(i), 0, 0, 0)), rowblk(256, 4),
                  pl.BlockSpec((None, ATT_BLOCK, 256), lambda b, i: (b, pv(i), 4)),
                  rowblk(256, _Z0), rowblk(256, _Z0 + 1), rowblk(256, _Z0 + 2), rowblk(256, _Z0 + 3),
                  rowblk(ATT_WIDTH, 1),
                  tab, tab, tabp, tabp,
                  pl.BlockSpec((None, None, 2, GROUP_ROWS, 2 * ATT_BLOCK), lambda b, i: (b, cl(i), 0, 0, 0)),
                  rowblk(ATT_WIDTH, 0),
                  pl.BlockSpec((None, None, ATT_HEADS, 128), lambda b, i: (b, cl(i), 0, 0))] + p_ispecs,
        out_specs=[rowblk(ATT_WIDTH, 0),
                   pl.BlockSpec((None, ATT_BLOCK, 256), lambda b, i: (b, jnp.maximum(i - 1, 0), 0)),
                   rowblk(ATT_WIDTH, 0),
                   pl.BlockSpec((1, 128), lambda b, i: (0, 0))] + p_ospecs,
        out_shape=[jax.ShapeDtypeStruct((B, S, ATT_WIDTH), BF16), jax.ShapeDtypeStruct((B, S, 256), BF16),
                   jax.ShapeDtypeStruct((B, S, ATT_WIDTH), BF16), jax.ShapeDtypeStruct((1, 128), F32)] + p_oshapes,
        input_output_aliases=p_alias,
        scratch_shapes=[pltpu.VMEM((ATT_BLOCK, 256), F32), pltpu.VMEM((ATT_HEADS, 128), F32)] + p_scratch,
        compiler_params=_params(("arbitrary", "arbitrary")),
    )(qs_kept, proj_a, proj_a, proj_a, proj_a, proj_a, proj_a, du, cos, sin, cos, sin, p_kept, o_kept, ps_kept, *p_args)
    return tuple(res[:4]) + (list(res[4:]),)


def _outproj_fwd(u2, w_out, x2, g_post, layer, target2, name):
    T, D = x2.shape
    tm = _pick(T, (512, 256, 128))
    last = target2 is not None
    streamed = [u2, x2] + ([target2] if last else [])
    ns, nt, slots = len(streamed), T // tm, 3

    def body(w_ref, g_ref, *rest):
        srcs, rings, sems = rest[:ns], rest[-ns - 1:-1], rest[-1]
        rest = rest[ns:-ns - 1]
        i = pl.program_id(0)

        def tile_copy(k, t):
            return pltpu.make_async_copy(srcs[k].at[pl.ds(pl.multiple_of(t * tm, tm), tm), :],
                                         rings[k].at[t % slots], sems.at[k, t % slots])

        @pl.when(i == 0)
        def _():
            for t in range(min(slots - 1, nt)):
                for k in range(ns):
                    tile_copy(k, jnp.int32(t)).start()

        @pl.when(i + (slots - 1) < nt)
        def _():
            for k in range(ns):
                tile_copy(k, i + (slots - 1)).start()

        for k in range(ns):
            tile_copy(k, i).wait()
        tiles = [rings[k][i % slots] for k in range(ns)]
        y = lax.dot_general(tiles[0], w_ref[...], (NN, ((), ())), preferred_element_type=F32)
        r = lax.rsqrt(jnp.mean(y * y, axis=-1, keepdims=True) + NORM_EPS)
        xn = tiles[1] + (y * r) * g_ref[layer:layer + 1, :]
        if last:
            y_ref, dx_ref, loss_ref = rest
            err = xn - tiles[2]
            dx_ref[...] = err * (1.0 / D)
            sq = err * err
            acc = sq[:, 0:128]
            for kk in range(1, D // 128):
                acc = acc + sq[:, 128 * kk:128 * (kk + 1)]
            part = jnp.sum(acc.reshape(tm // 8, 8, 128), axis=0) * (0.5 / D)

            @pl.when(pl.program_id(0) == 0)
            def _():
                loss_ref[...] = jnp.zeros_like(loss_ref)

            loss_ref[...] += part
        else:
            y_ref, xn_ref = rest
            xn_ref[...] = xn
        y_ref[...] = y

    row = pl.BlockSpec((tm, D), lambda i: (i, 0))
    in_specs = [pl.BlockSpec((MIX_WIDTH, D), lambda i: (0, 0)), pl.BlockSpec((DEPTH, D), lambda i: (0, 0))] + [_ANY] * ns
    out_specs = [row, row]
    out_shape = [jax.ShapeDtypeStruct((T, D), F32), jax.ShapeDtypeStruct((T, D), F32)]
    if last:
        out_specs.append(pl.BlockSpec((8, 128), lambda i: (0, 0)))
        out_shape.append(jax.ShapeDtypeStruct((8, 128), F32))
    return pl.pallas_call(
        body, name=name, grid=(nt,), in_specs=in_specs, out_specs=out_specs, out_shape=out_shape,
        scratch_shapes=[pltpu.VMEM((slots, tm, a.shape[1]), a.dtype) for a in streamed]
        + [pltpu.SemaphoreType.DMA((ns, slots))],
        compiler_params=_params(("arbitrary",)),
    )(w_out, g_post, *streamed)


def _outproj_bwd(dxn2, y2, g_post, layer, w_out, u2, name):
    T, D = y2.shape
    N = w_out.shape[0]
    tm = _pick(T, (512, 256, 128))
    nt = T // tm

    def body(dx_ref, y_ref, g_ref, w_ref, u_ref, dg_ref, du_ref, dw_ref, acc, wacc):
        i = pl.program_id(0)

        @pl.when(i == 0)
        def _():
            acc[...] = jnp.zeros_like(acc)
            wacc[...] = jnp.zeros_like(wacc)

        y = y_ref[...]
        dxn = dx_ref[...]
        r = lax.rsqrt(jnp.mean(y * y, axis=-1, keepdims=True) + NORM_EPS)
        n = y * r
        dn = dxn * g_ref[layer:layer + 1, :]
        dy = (r * (dn - n * jnp.mean(dn * n, axis=-1, keepdims=True))).astype(BF16)
        du_ref[...] = lax.dot_general(dy, w_ref[...], (NT, ((), ())), preferred_element_type=F32)
        wacc[...] += lax.dot_general(u_ref[...], dy, (TN, ((), ())), preferred_element_type=F32)
        acc[...] += jnp.sum((dxn * n).reshape(tm // 8, 8, D), axis=0)

        @pl.when(i == nt - 1)
        def _():
            dg_ref[...] = jnp.sum(acc[...], axis=0, keepdims=True)
            dw_ref[...] = wacc[...].astype(BF16)

    row = pl.BlockSpec((tm, D), lambda i: (i, 0))
    wide = pl.BlockSpec((tm, N), lambda i: (i, 0))
    vec = pl.BlockSpec((1, D), lambda i: (0, 0))
    whole = pl.BlockSpec((N, D), lambda i: (0, 0))
    return pl.pallas_call(
        body, name=name, grid=(nt,),
        in_specs=[row, row, pl.BlockSpec((DEPTH, D), lambda i: (0, 0)),
                  pl.BlockSpec((N, D), lambda i: (0, 0), pipeline_mode=pl.Buffered(1)), wide],
        out_specs=[vec, wide, whole],
        out_shape=[jax.ShapeDtypeStruct((1, D), F32), jax.ShapeDtypeStruct((T, N), F32),
                   jax.ShapeDtypeStruct((N, D), BF16)],
        scratch_shapes=[pltpu.VMEM((8, D), F32), pltpu.VMEM((N, D), F32)],
        compiler_params=_params(("arbitrary",)),
    )(dxn2, y2, g_post, w_out, u2)


def _inproj_bwd(pieces, w_t, x2, dxn2, g_pre, layer, name, phase=None):
    T, D = x2.shape
    widths = [p.shape[1] for p in pieces]
    offs = [sum(widths[:i]) for i in range(len(pieces))]
    n_p = len(pieces)
    tm = _pick(T, (256, 128))
    nt = T // tm
    ns, slots = n_p + 2, 3

    def body(*refs):
        ins, (dx_ref, dg_ref), scr, p_in, p_out, p_sems = _split_refs(refs, n_p + 4, 2, ns + 2, phase)
        acc, rings, sems = scr[0], scr[1:-1], scr[-1]
        w_ref, x_any, dxn_any, g_ref = ins[n_p:]
        srcs = list(ins[:n_p]) + [x_any, dxn_any]
        i = pl.program_id(0)
        _hosted_start(phase, p_in, p_out, p_sems, i == 0)

        def tile_copy(k, t):
            return pltpu.make_async_copy(srcs[k].at[pl.ds(pl.multiple_of(t * tm, tm), tm), :],
                                         rings[k].at[t % slots], sems.at[k, t % slots])

        @pl.when(i == 0)
        def _():
            acc[...] = jnp.zeros_like(acc)
            for t in range(min(slots - 1, nt)):
                for k in range(ns):
                    tile_copy(k, jnp.int32(t)).start()

        @pl.when(i + (slots - 1) < nt)
        def _():
            for k in range(ns):
                tile_copy(k, i + (slots - 1)).start()

        for k in range(ns):
            tile_copy(k, i).wait()
        tiles = [rings[k][i % slots] for k in range(ns)]
        dh = jnp.zeros((tm, D), F32)
        for p in range(n_p):
            dh = dh + lax.dot_general(tiles[p], w_ref[offs[p]:offs[p] + widths[p], :], (NN, ((), ())),
                                      preferred_element_type=F32)
        x = tiles[n_p]
        r = lax.rsqrt(jnp.mean(x * x, axis=-1, keepdims=True) + NORM_EPS)
        n = x * r
        dn = dh * g_ref[layer:layer + 1, :]
        dx_ref[...] = tiles[n_p + 1] + r * (dn - n * jnp.mean(dn * n, axis=-1, keepdims=True))
        acc[...] += jnp.sum((dh * n).reshape(tm // 8, 8, D), axis=0)

        @pl.when(i == nt - 1)
        def _():
            dg_ref[...] = jnp.sum(acc[...], axis=0, keepdims=True)

        _hosted_finish(phase, p_in, p_out, p_sems, i == nt - 1)

    row = pl.BlockSpec((tm, D), lambda i: (i, 0))
    vec = pl.BlockSpec((1, D), lambda i: (0, 0))
    p_ispecs, p_ospecs, p_oshapes, p_alias, p_scratch, p_args = _host_phase(phase, n_p + 4, 2)
    res = pl.pallas_call(
        body, name=name, grid=(nt,),
        in_specs=[_ANY] * n_p
        + [pl.BlockSpec((sum(widths), D), lambda i: (0, 0), pipeline_mode=pl.Buffered(1)), _ANY, _ANY,
           pl.BlockSpec((DEPTH, D), lambda i: (0, 0))] + p_ispecs,
        out_specs=[row, vec] + p_ospecs,
        out_shape=[jax.ShapeDtypeStruct((T, D), F32), jax.ShapeDtypeStruct((1, D), F32)] + p_oshapes,
        input_output_aliases=p_alias,
        scratch_shapes=[pltpu.VMEM((8, D), F32)]
        + [pltpu.VMEM((slots, tm, a.shape[1]), a.dtype) for a in list(pieces) + [x2, dxn2]]
        + [pltpu.SemaphoreType.DMA((ns, slots))] + p_scratch,
        compiler_params=_params(("arbitrary",)),
    )(*pieces, w_t, x2, dxn2, g_pre, *p_args)
    return res[0], res[1], list(res[2:])


def _step(x, target, g_pre, g_post, lb_param, g_head, sinks, shards=None, full=None):
    B, S, D = x.shape
    T = B * S
    dist = shards is not None
    first, last = 0, DEPTH - 1
    if dist:
        a_loc, b_loc = shards
        ra, rb = a_loc.shape[1], b_loc.shape[1]
        side = _own_side_blocks()
        a_full, b_full = _place_own([a_loc, b_loc], side, "place_own")
        w_in0 = _gather_one_call(a_full[0], "gather_in0")
        w_in, w_out = [w_in0, None], [None, None]
    else:
        w_in, w_out = list(full[0]), list(full[1])
    cos, sin = _rope_tables(S)
    sink_rows = _sink_rows(sinks)
    saved = []
    xs = x
    loss_part = None
    dxn = None
    for l in range(DEPTH):
        x2 = xs.reshape(T, D)
        proj_h, proj_a, h = _inproj(x2, g_pre, l, w_in[l], f"inproj{l}")
        proj_h = proj_h.reshape(B, S, N_H)
        proj_a = proj_a.reshape(B, S, N_A)
        phase = None
        if dist and l == first:
            phase = _gather_ici_phase([a_full[1], b_full[0]])
        if dist and l == last:
            phase = _gather_d2d_phase([w_out1_part], [rb])
        o_h, u, states, got = _hgrn_fwd(proj_h, MIX_WIDTH, lb_param, g_head, l, f"hgrn_fwd{l}", phase)
        phase = None
        if dist and l == first:
            phase = _merge_phases(_gather_d2d_phase(got, [ra, rb]),
                                  _gather_ici_phase([b_full[1]]))
        if dist and l == last:
            w_out[1] = got[0]
        u, kept_a, got = _attn_fwd(proj_a, u, sink_rows, l, cos, sin, f"attn_fwd{l}", phase)
        if dist and l == first:
            w_in[1], w_out[0], w_out1_part = got
        u2 = u.reshape(T, MIX_WIDTH)
        if l < last:
            y, xn = _outproj_fwd(u2, w_out[l], x2, g_post, l, None, f"outproj{l}")
            xn = xn.reshape(B, S, D)
        else:
            y, dxn, loss_part = _outproj_fwd(u2, w_out[l], x2, g_post, l, target.reshape(T, D), f"outproj{l}")
            xn = None
        saved.append((x2, h, proj_h, proj_a, o_h, u2, states, kept_a, y))
        xs = xn

    dw_in, dw_out = [None] * DEPTH, [None] * DEPTH
    dg_pre, dg_post, dlb, dg_head, dsinks = [], [], [], [], []
    for l in reversed(range(DEPTH)):
        x2, h, proj_h, proj_a, o_h, u2, states, kept_a, y = saved[l]
        dgp, du, dw_out[l] = _outproj_bwd(dxn, y, g_post, l, w_out[l], u2, f"outproj_bwd{l}")
        du = du.reshape(B, S, MIX_WIDTH)
        phase = None
        if dist:
            phase = _reduce_d2d_phase([dw_out[l]], [rb])
            if l == first:
                phase = _merge_phases(_reduce_ici_phase([part_in1]), phase)
        dqh, dfh, dih, dzh, dlb_l, dgh, got = _hgrn_bwd(
            proj_h, o_h, du, states, lb_param, g_head, l, f"hgrn_bwd{l}", phase)
        if dist:
            if l == first:
                sum_in = _chip_sum(part_in1, got[0], "chip_sum_in1", 1)
            part_out = _pair_sum(dw_out[l], got[-1], side, f"pair_sum_out{l}")
        dqa, dkv, dza, dsk, got = _attn_bwd(proj_a, du, kept_a, cos, sin, f"attn_bwd{l}",
                                            _reduce_ici_phase([part_out]) if dist else None)
        if dist:
            sum_out = _chip_sum(part_out, got[0], f"chip_sum_out{l}", l, None if l == last else sum_out)
        dproj = [p.reshape(T, p.shape[-1]) for p in (dqh, dfh, dih, dzh, dqa, dkv, dza)]
        dw_in[l] = _mm_tn(dproj, h, f"wgrad_in{l}")
        phase = None
        if dist and l == last:
            phase = _reduce_d2d_phase([dw_in[l]], [ra])
        if dist and l == first:
            got = _run_phase(_reduce_d2d_phase([dw_in[l]], [ra]), "reduce_in0_d2d")
            part_in0 = _pair_sum(dw_in[l], got[0], side, "pair_sum_in0")
            phase = _reduce_ici_phase([part_in0])
        dxn, dgpre, got = _inproj_bwd(dproj, w_in[l], x2, dxn, g_pre, l, f"inproj_bwd{l}", phase)
        if dist and l == last:
            part_in1 = _pair_sum(dw_in[l], got[0], side, "pair_sum_in1")
        if dist and l == first:
            sum_in = _chip_sum(part_in0, got[0], "chip_sum_in0", 0, sum_in)
        dg_pre.append(dgpre)
        dg_post.append(dgp)
        dlb.append(dlb_l)
        dg_head.append(dgh)
        dsinks.append(dsk)
    rev = lambda lst: jnp.concatenate(lst[::-1], axis=0)
    if not dist:
        sum_in, sum_out = jnp.stack(dw_in), jnp.stack(dw_out)
    return (loss_part, dxn.reshape(B, S, D), sum_in, sum_out,
            rev(dg_pre), rev(dg_post), rev(dlb), rev(dg_head), rev(dsinks))


def _me_and_peers():
    x, y, c = lax.axis_index("x"), lax.axis_index("y"), lax.axis_index("c")
    me = 4 * x + 2 * y + c
    peers = []
    for k in range(1, N_DEV):
        px = 1 - x if k & 4 else x
        py = 1 - y if k & 2 else y
        pc = 1 - c if k & 1 else c
        peers.append(((px, py, pc), 4 * px + 2 * py + pc))
    return me, peers


class _Phase:
    def __init__(self, arrays, out_shapes, aliases, n_send, build):
        self.arrays, self.out_shapes, self.aliases = list(arrays), list(out_shapes), dict(aliases)
        self.n_send, self.build = n_send, build

    def scratch(self):
        return [pltpu.SemaphoreType.DMA((self.n_send,)), pltpu.SemaphoreType.DMA((self.n_send,))]

    def _copies(self, in_refs, out_refs, sems, arrivals):
        send_sems, recv_sems = sems
        sends, recvs = self.build(in_refs, out_refs)
        assert len(sends) == self.n_send == len(recvs)
        out = [pltpu.make_async_remote_copy(src_ref=s, dst_ref=d, send_sem=send_sems.at[i], recv_sem=recv_sems.at[i],
                                            device_id=dev, device_id_type=MESH) for i, (s, d, dev) in enumerate(sends)]
        inc = [pltpu.make_async_remote_copy(src_ref=s, dst_ref=r, send_sem=send_sems.at[i], recv_sem=recv_sems.at[i],
                                            device_id=dev, device_id_type=MESH)
               for i, ((s, _, dev), r) in enumerate(zip(sends, recvs))] if arrivals else []
        return out, inc

    def start(self, in_refs, out_refs, sems):
        out, _ = self._copies(in_refs, out_refs, sems, False)
        for cp in out:
            cp.start()

    def finish(self, in_refs, out_refs, sems):
        out, inc = self._copies(in_refs, out_refs, sems, True)
        for cp in inc:
            cp.wait_recv()
        for cp in out:
            cp.wait_send()


_ANY = pl.BlockSpec(memory_space=pl.ANY)


def _host_phase(phase, n_in, n_out):
    if phase is None:
        return [], [], [], {}, [], []
    aliases = {n_in + i: n_out + o for i, o in phase.aliases.items()}
    return ([_ANY] * len(phase.arrays), [_ANY] * len(phase.out_shapes), phase.out_shapes, aliases, phase.scratch(),
            phase.arrays)


def _split_refs(refs, n_in, n_out, n_scr, phase):
    pi = len(phase.arrays) if phase else 0
    po = len(phase.out_shapes) if phase else 0
    a = n_in + pi
    b = a + n_out + po
    return (refs[:n_in], refs[a:a + n_out], refs[b:b + n_scr], refs[n_in:a], refs[a + n_out:b], refs[b + n_scr:])


def _hosted_start(phase, p_in, p_out, p_sems, first):
    if phase is not None:
        @pl.when(first)
        def _():
            phase.start(p_in, p_out, p_sems)


def _hosted_finish(phase, p_in, p_out, p_sems, last):
    if phase is not None:
        @pl.when(last)
        def _():
            phase.finish(p_in, p_out, p_sems)


def _run_phase(phase, name):
    n_in, n_out = len(phase.arrays), len(phase.out_shapes)

    def body(*refs):
        phase.start(refs[:n_in], refs[n_in:n_in + n_out], refs[n_in + n_out:])
        phase.finish(refs[:n_in], refs[n_in:n_in + n_out], refs[n_in + n_out:])

    return pl.pallas_call(
        body, name=name, in_specs=[_ANY] * n_in, out_specs=[_ANY] * n_out,
        out_shape=phase.out_shapes, input_output_aliases=phase.aliases, scratch_shapes=phase.scratch(),
        compiler_params=pltpu.CompilerParams(has_side_effects=True),
    )(*phase.arrays)


def _gather_one_call(full, name):
    r = full.shape[0] // N_DEV
    half = r // 2

    def body(full_in, full_ref, send_sems, recv_sems):
        del full_in
        c, (own, xn, yn, dg), num = _mesh_place()
        me, sib = num(own, c), (*own, 1 - c)

        def blk(dev, part=None):
            start, n = (dev * r, r) if part is None else (dev * r + part * half, half)
            return full_ref.at[pl.ds(pl.multiple_of(start, 16), n), :]

        def copy(k, src, dev, to, part=None):
            return pltpu.make_async_remote_copy(src_ref=src, dst_ref=blk(dev, part),
                                                send_sem=send_sems.at[k], recv_sem=recv_sems.at[k],
                                                device_id=to, device_id_type=MESH)

        def landed(k, dev, part=None):
            copy(k, blk(dev, part), dev, sib, part).wait_recv()

        sent = []

        def start(*cps):
            for cp in cps:
                cp.start()
                sent.append(cp)

        xs, ys, ds = num(xn, c), num(yn, c), num(dg, c)
        start(copy(0, blk(me), me, sib), copy(1, blk(me), me, (*xn, c)), copy(2, blk(me), me, (*yn, c)))
        landed(1, xs)
        start(copy(3, blk(xs, 0), xs, (*yn, c), 0), copy(5, blk(xs), xs, sib))
        landed(2, ys)
        start(copy(4, blk(ys, 1), ys, (*xn, c), 1), copy(6, blk(ys), ys, sib))
        landed(3, ds, 0)
        landed(4, ds, 1)
        start(copy(7, blk(ds), ds, sib))
        landed(0, num(own, 1 - c))
        for k, ch in ((5, xn), (6, yn), (7, dg)):
            landed(k, num(ch, 1 - c))
        for cp in sent:
            cp.wait_send()

    assert half % 16 == 0
    return pl.pallas_call(
        body, name=name, in_specs=[_ANY], out_specs=_ANY,
        out_shape=jax.ShapeDtypeStruct(full.shape, full.dtype), input_output_aliases={0: 0},
        scratch_shapes=[pltpu.SemaphoreType.DMA((8,)), pltpu.SemaphoreType.DMA((8,))],
        compiler_params=pltpu.CompilerParams(has_side_effects=True),
    )(full)


def _merge_phases(a, b):
    n_in, n_out = len(a.arrays), len(a.out_shapes)
    aliases = dict(a.aliases)
    aliases.update({n_in + i: n_out + o for i, o in b.aliases.items()})

    def build(ins, outs):
        sa, ra = a.build(ins[:n_in], outs[:n_out])
        sb, rb = b.build(ins[n_in:], outs[n_out:])
        return sa + sb, ra + rb

    return _Phase(a.arrays + b.arrays, a.out_shapes + b.out_shapes, aliases, a.n_send + b.n_send, build)


def _mesh_place():
    x, y, c = lax.axis_index("x"), lax.axis_index("y"), lax.axis_index("c")
    chips = [(x, y), (1 - x, y), (x, 1 - y), (1 - x, 1 - y)]
    num = lambda chip, core: 4 * chip[0] + 2 * chip[1] + core
    return c, chips, num


def _own_side_blocks():
    c, chips, num = _mesh_place()
    return jnp.stack([num(ch, c) for ch in chips]).astype(jnp.int32)


def _rows(ref, r, dev):
    return ref.at[pl.ds(pl.multiple_of(dev * r, 16), r), :]


def _place_own(shards, blocks, name):
    n = len(shards)

    def body(idx_ref, *refs):
        del idx_ref
        outs = iter(refs[n:])
        for s_ref in refs[:n]:
            for l in range(DEPTH):
                next(outs)[...] = s_ref[l].astype(BF16)

    whole = lambda s: pl.BlockSpec(s.shape, lambda i, idx: (0, 0, 0))
    own = lambda s: pl.BlockSpec(s.shape[1:], lambda i, idx: (idx[0], 0))
    res = pl.pallas_call(
        body, name=name,
        grid_spec=pltpu.PrefetchScalarGridSpec(
            num_scalar_prefetch=1, grid=(1,),
            in_specs=[whole(s) for s in shards],
            out_specs=[own(s) for s in shards for _ in range(DEPTH)]),
        out_shape=[jax.ShapeDtypeStruct((N_DEV * s.shape[1], s.shape[2]), BF16) for s in shards for _ in range(DEPTH)],
        compiler_params=_params(("arbitrary",)),
    )(blocks, *shards)
    return [list(res[i * DEPTH:(i + 1) * DEPTH]) for i in range(n)]


def _gather_ici_phase(fulls):
    rs = [a.shape[0] // N_DEV for a in fulls]
    n = len(fulls)

    def build(ins, outs):
        del ins
        c, chips, num = _mesh_place()
        me = num(chips[0], c)
        targets = [((*chips[0], 1 - c), num(chips[0], 1 - c))] + [((*ch, c), num(ch, c)) for ch in chips[1:]]
        sends, recvs = [], []
        for dev, dnum in targets:
            for i, r in enumerate(rs):
                sends.append((_rows(outs[i], r, me), _rows(outs[i], r, me), dev))
                recvs.append(_rows(outs[i], r, dnum))
        return sends, recvs

    shapes = [jax.ShapeDtypeStruct(a.shape, a.dtype) for a in fulls]
    return _Phase(list(fulls), shapes, {i: i for i in range(n)}, 4 * n, build)


def _gather_d2d_phase(fulls, rs):
    def build(ins, outs):
        c, chips, num = _mesh_place()
        sib = (*chips[0], 1 - c)
        sends, recvs = [], []
        for ch in chips[1:]:
            for i, r in enumerate(rs):
                blk = _rows(outs[i], r, num(ch, c))
                sends.append((blk, blk, sib))
                recvs.append(_rows(outs[i], r, num(ch, 1 - c)))
        return sends, recvs

    shapes = [jax.ShapeDtypeStruct(a.shape, a.dtype) for a in fulls]
    return _Phase(fulls, shapes, {i: i for i in range(len(fulls))}, 3 * len(fulls), build)


def _reduce_d2d_phase(grads, rs):
    def build(ins, outs):
        c, chips, num = _mesh_place()
        sib = (*chips[0], 1 - c)
        sends, recvs = [], []
        for j, ch in enumerate(chips):
            for i, r in enumerate(rs):
                sends.append((_rows(ins[i], r, num(ch, 1 - c)), outs[i].at[j], sib))
                recvs.append(outs[i].at[j])
        return sends, recvs

    shapes = [jax.ShapeDtypeStruct((4, r, g.shape[1]), g.dtype) for g, r in zip(grads, rs)]
    return _Phase(grads, shapes, {}, 4 * len(grads), build)


def _reduce_ici_phase(parts):
    def build(ins, outs):
        c, chips, _ = _mesh_place()
        sends, recvs = [], []
        for t in range(1, 4):
            for i in range(len(parts)):
                sends.append((ins[i].at[t], outs[i].at[t - 1], (*chips[t], c)))
                recvs.append(outs[i].at[t - 1])
        return sends, recvs

    shapes = [jax.ShapeDtypeStruct((3,) + p.shape[1:], p.dtype) for p in parts]
    return _Phase(parts, shapes, {}, 3 * len(parts), build)


def _pair_sum(g, got, blocks, name):
    n, r, D = got.shape
    tr = _pick(r, (800, 400, 256, 200, 128, 64, 16))

    def body(idx_ref, g_ref, r_ref, o_ref):
        del idx_ref
        o_ref[...] = (g_ref[...].astype(F32) + r_ref[...].astype(F32)).astype(o_ref.dtype)

    blk = pl.BlockSpec((None, tr, D), lambda j, i, idx: (j, i, 0))
    return pl.pallas_call(
        body, name=name,
        grid_spec=pltpu.PrefetchScalarGridSpec(
            num_scalar_prefetch=1, grid=(n, r // tr),
            in_specs=[pl.BlockSpec((tr, D), lambda j, i, idx: (idx[j] * (r // tr) + i, 0)), blk],
            out_specs=blk),
        out_shape=jax.ShapeDtypeStruct(got.shape, got.dtype),
        compiler_params=_params(("arbitrary", "arbitrary")),
    )(blocks, g, got)


def _chip_sum(p, r, name, layer, into=None):
    _, R, D = p.shape
    tr = _pick(R, (800, 400, 256, 200, 128, 64, 16))

    def body(p_ref, r_ref, *rest):
        acc = p_ref[...].astype(F32)
        for t in range(3):
            acc = acc + r_ref[t].astype(F32)
        rest[-1][...] = acc

    args = [p, r] + ([] if into is None else [into])
    return pl.pallas_call(
        body, name=name, grid=(R // tr,),
        in_specs=[pl.BlockSpec((None, tr, D), lambda i: (0, i, 0)), pl.BlockSpec((3, tr, D), lambda i: (0, i, 0))]
        + ([] if into is None else [_ANY]),
        out_specs=pl.BlockSpec((None, tr, D), lambda i: (layer, i, 0)),
        out_shape=jax.ShapeDtypeStruct((DEPTH, R, D), F32),
        input_output_aliases={} if into is None else {2: 0},
        compiler_params=_params(("parallel",)))(*args)


def _allreduce_small(vec):
    R, C = vec.shape

    def body(v_ref, o_ref, buf, send_sems, recv_sems):
        me, peers = _me_and_peers()
        buf[me] = v_ref[...]
        sends = []
        for k, (pid, _) in enumerate(peers):
            cp = pltpu.make_async_remote_copy(src_ref=v_ref, dst_ref=buf.at[me], send_sem=send_sems.at[k],
                                              recv_sem=recv_sems.at[k], device_id=pid, device_id_type=MESH)
            cp.start()
            sends.append(cp)
        for k, (pid, pnum) in enumerate(peers):
            pltpu.make_async_remote_copy(src_ref=v_ref, dst_ref=buf.at[pnum], send_sem=send_sems.at[k],
                                         recv_sem=recv_sems.at[k], device_id=pid, device_id_type=MESH).wait_recv()
        for cp in sends:
            cp.wait_send()
        acc = buf[0]
        for d in range(1, N_DEV):
            acc = acc + buf[d]
        o_ref[...] = acc

    vm = pl.BlockSpec(memory_space=pltpu.VMEM)
    return pl.pallas_call(
        body, name="allreduce_small",
        in_specs=[vm], out_specs=vm,
        out_shape=jax.ShapeDtypeStruct((R, C), F32),
        scratch_shapes=[pltpu.VMEM((N_DEV, R, C), F32), pltpu.SemaphoreType.DMA((N_DEV - 1,)),
                        pltpu.SemaphoreType.DMA((N_DEV - 1,))],
        compiler_params=pltpu.CompilerParams(has_side_effects=True),
    )(vec)


def _adamw_update(w, g, m, v):
    c1 = 1.0 - ADAM_B1 ** ADAM_STEP
    c2 = 1.0 - ADAM_B2 ** ADAM_STEP
    mn = ADAM_B1 * m + (1.0 - ADAM_B1) * g
    vn = ADAM_B2 * v + (1.0 - ADAM_B2) * (g * g)
    return -ADAM_LR * ((mn / c1) / (jnp.sqrt(vn / c2) + ADAM_EPS) + ADAM_WD * w), mn, vn


def _adamw(w, g, m, v, name):
    R, C = w.shape
    tr = _pick(R, (512, 400, 256, 128, 64, 32, 16, 8))

    def body(w_ref, g_ref, m_ref, v_ref, d_ref, mo_ref, vo_ref):
        d_ref[...], mo_ref[...], vo_ref[...] = _adamw_update(w_ref[...], g_ref[...], m_ref[...], v_ref[...])

    blk = pl.BlockSpec((tr, C), lambda i: (i, 0))
    sh = jax.ShapeDtypeStruct((R, C), F32)
    return pl.pallas_call(
        body, name=name, grid=(R // tr,), in_specs=[blk] * 4, out_specs=[blk] * 3, out_shape=[sh] * 3,
        compiler_params=_params(("parallel",)),
    )(w, g, m, v)


def _adamw_whole(ws, gs, ms, vs, name):
    n = len(ws)

    def body(*refs):
        for j in range(n):
            outs = refs[4 * n + 3 * j:4 * n + 3 * j + 3]
            outs[0][...], outs[1][...], outs[2][...] = _adamw_update(*(r[...] for r in refs[4 * j:4 * j + 4]))

    vm = pl.BlockSpec(memory_space=pltpu.VMEM)
    res = pl.pallas_call(
        body, name=name, in_specs=[vm] * (4 * n), out_specs=[vm] * (3 * n),
        out_shape=[jax.ShapeDtypeStruct(w.shape, F32) for w in ws for _ in range(3)],
    )(*[a for four in zip(ws, gs, ms, vs) for a in four])
    return [tuple(res[3 * j:3 * j + 3]) for j in range(n)]


def _lb_param_grad(lb_param, dlb):
    L, C = lb_param.shape

    def body(p_ref, d_ref, o_ref):
        lbp = p_ref[...]
        d = d_ref[...]
        mx = jnp.max(lbp, axis=0, keepdims=True)
        e = jnp.exp(lbp - mx)
        p = e / jnp.sum(e, axis=0, keepdims=True)
        tot = jnp.sum(d, axis=0, keepdims=True)
        dps = []
        rest = tot
        for j in range(L):
            dps.append(rest - tot if j == 0 else rest)
            rest = rest - d[j:j + 1]
        dp = jnp.concatenate(dps, axis=0)
        o_ref[...] = p * (dp - jnp.sum(p * dp, axis=0, keepdims=True))

    vm = pl.BlockSpec(memory_space=pltpu.VMEM)
    return pl.pallas_call(body, name="lb_param_grad", in_specs=[vm, vm], out_specs=vm,
                          out_shape=jax.ShapeDtypeStruct((L, C), F32))(lb_param, dlb)


def _pack_small(loss_part, dg_pre, dg_post, dlb, dg_head, dsinks):
    pad8 = lambda a: jnp.pad(a.reshape(-1, 128), ((0, 8 - DEPTH), (0, 0)))
    rows = [dg_pre.reshape(-1, 128), dg_post.reshape(-1, 128), dlb.reshape(-1, 128), pad8(dg_head), pad8(dsinks),
            loss_part]
    return jnp.concatenate(rows, axis=0)


def _unpack_small(vec):
    n = DEPTH * D_MODEL // 128
    o = 0
    dg_pre = vec[o:o + n].reshape(DEPTH, D_MODEL); o += n
    dg_post = vec[o:o + n].reshape(DEPTH, D_MODEL); o += n
    dlb = vec[o:o + n].reshape(DEPTH, HG_WIDTH); o += n
    dg_head = vec[o:o + DEPTH]; o += 8
    dsinks = vec[o:o + DEPTH, :ATT_HEADS]; o += 8
    loss = jnp.sum(vec[o:o + 8])
    return loss, dg_pre, dg_post, dlb, dg_head, dsinks


def kernel(x, w_in, w_out, g_pre, g_post, lb_param, g_head, sinks, loss_target, m_w_in, m_w_out, m_g_pre, m_g_post, m_lb_param, m_g_head, m_sinks, v_w_in, v_w_out, v_g_pre, v_g_post, v_lb_param, v_g_head, v_sinks):
    tr = lambda a: jnp.swapaxes(a, 1, 2)
    w_in_t = tr(w_in)
    (loss_part, dx, gw_in_t, gw_out, dg_pre, dg_post, dlb, dg_head, dsinks) = _step(
        x, loss_target, g_pre, g_post, lb_param, g_head, sinks, shards=(w_in_t, w_out))

    small = _allreduce_small(_pack_small(loss_part, dg_pre, dg_post, dlb, dg_head, dsinks))
    loss, gg_pre, gg_post, gdlb, gg_head, gsinks = _unpack_small(small)
    glb = _lb_param_grad(lb_param, gdlb)

    grads = [gw_in_t, gw_out, gg_pre, gg_post, glb, gg_head, gsinks]
    ws = [w_in_t, w_out, g_pre, g_post, lb_param, g_head, sinks]
    ms = [tr(m_w_in), m_w_out, m_g_pre, m_g_post, m_lb_param, m_g_head, m_sinks]
    vs = [tr(v_w_in), v_w_out, v_g_pre, v_g_post, v_lb_param, v_g_head, v_sinks]
    deltas, new_m, new_v = [], [], []
    big = 2
    for w, g, m, v, nm in zip(ws[:big], grads, ms, vs, ("w_in", "w_out")):
        sh = w.shape
        two = lambda a: a.reshape(-1, sh[-1])
        d, mn, vn = _adamw(two(w), two(g), two(m), two(v), "adamw_" + nm)
        deltas.append(d.reshape(sh))
        new_m.append(mn.reshape(sh))
        new_v.append(vn.reshape(sh))
    for d, mn, vn in _adamw_whole(ws[big:], grads[big:], ms[big:], vs[big:], "adamw_vectors"):
        deltas.append(d)
        new_m.append(mn)
        new_v.append(vn)
    grads[0], deltas[0], new_m[0], new_v[0] = tr(grads[0]), tr(deltas[0]), tr(new_m[0]), tr(new_v[0])
    return (loss, dx, *grads, *deltas, *new_m, *new_v)
```
